```python
import jax, jax.numpy as jnp
from jax import lax
import numpy as np

D_MODEL = 1024
BATCH = 8
SEQ = 4096
DEPTH = 1

CONV_CH = D_MODEL
CONV_WIDTH = 31
HEAD_DIM = 128
HEADS_PER_GROUP = 4
ATTN_GROUPS = ((128, 1), (512, 4), (2048, 16))
N_GROUPS = len(ATTN_GROUPS)
N_ATTN_HEADS = HEADS_PER_GROUP * N_GROUPS
ATTN_WIDTH = N_ATTN_HEADS * HEAD_DIM
ATTN_OUT_WIDTH = HEADS_PER_GROUP * HEAD_DIM
BLOCK = 128
D_FF = 2816
EPS = 1e-6
IN_WIDTH = 2 * CONV_CH + 3 * ATTN_WIDTH + 2 * D_MODEL

kernel_name = "hybrid_conformer_dilated_attn_block"


def _alibi_slopes():
    h = np.arange(1, N_ATTN_HEADS + 1, dtype=np.float32)
    return np.power(np.float32(2.0), -8.0 * h / np.float32(N_ATTN_HEADS)).astype(np.float32)


def rmsnorm(x, g):
    xf = x.astype(jnp.float32)
    y = xf * lax.rsqrt(jnp.mean(xf * xf, axis=-1, keepdims=True) + EPS)
    return (y * g.astype(jnp.float32)).astype(x.dtype)


def swiglu_ffn(h, w_gate, w_up, w_down):
    return (jax.nn.silu(h @ w_gate) * (h @ w_up)) @ w_down


def conformer_conv(u, dw_kernel, dw_bias, ln_gain, ln_bias, w_out):
    a, b = jnp.split(u, 2, axis=-1)
    z = a * jax.nn.sigmoid(b)
    z = lax.conv_general_dilated(
        z, dw_kernel[:, None, :], window_strides=(1,),
        padding=[(CONV_WIDTH - 1, 0)],
        dimension_numbers=('NWC', 'WIO', 'NWC'),
        feature_group_count=CONV_CH) + dw_bias
    zf = z.astype(jnp.float32)
    mu = jnp.mean(zf, axis=-1, keepdims=True)
    var = jnp.mean(jnp.square(zf - mu), axis=-1, keepdims=True)
    zf = (zf - mu) * lax.rsqrt(var + EPS) * ln_gain.astype(jnp.float32) + ln_bias.astype(jnp.float32)
    z = jax.nn.silu(zf).astype(u.dtype)
    return z @ w_out


def dilated_group_attention(q, k, v, slopes, window, dilation):
    B, S, H, E = q.shape
    span = window // dilation
    chunk = dilation * BLOCK
    s_pad = -(-S // chunk) * chunk
    L = s_pad // dilation
    nb = L // BLOCK

    def to_blocks(t):
        t = jnp.pad(t, ((0, 0), (0, s_pad - S), (0, 0), (0, 0)))
        t = t.reshape(B, L, dilation, H, E)
        t = t.transpose(0, 3, 2, 1, 4)
        return t.reshape(B, H, dilation, nb, BLOCK, E)

    def with_prev(t):
        prev = jnp.pad(t[:, :, :, :-1], ((0, 0), (0, 0), (0, 0), (1, 0), (0, 0), (0, 0)))
        return jnp.concatenate([prev, t], axis=4)

    qb = to_blocks(q)
    kc = with_prev(to_blocks(k))
    vc = with_prev(to_blocks(v))

    scores = jnp.einsum('bhrnqe,bhrnke->bhrnqk', qb, kc).astype(jnp.float32) * (E ** -0.5)
    qi = jnp.arange(BLOCK)[:, None]
    ki = jnp.arange(2 * BLOCK)[None, :]
    steps = BLOCK + qi - ki
    in_band = (steps >= 0) & (steps <= span)
    first_block = (jnp.arange(nb) == 0)[:, None, None]
    valid = in_band[None] & ~(first_block & (ki < BLOCK)[None])
    bias = -slopes[:, None, None] * (dilation * steps).astype(jnp.float32)
    scores = scores + bias[None, :, None, None]
    scores = jnp.where(valid[None, None, None], scores, -jnp.inf)
    m = jnp.max(scores, axis=-1, keepdims=True)
    p = jnp.exp(scores - m)
    denom = jnp.sum(p, axis=-1, keepdims=True)
    out = jnp.einsum('bhrnqk,bhrnke->bhrnqe', p.astype(vc.dtype), vc).astype(jnp.float32) / denom
    lse = (m + jnp.log(denom))[..., 0]

    out = out.astype(q.dtype).reshape(B, H, dilation, L, E).transpose(0, 3, 2, 1, 4)
    out = out.reshape(B, s_pad, H, E)[:, :S]
    lse = lse.reshape(B, H, dilation, L).transpose(0, 3, 2, 1).reshape(B, s_pad, H)[:, :S]
    return out, lse


def _fwd_setup_inputs(seed: int = 0) -> dict:
    key = jax.random.key(seed)
    ks = jax.random.split(key, 24)

    def w(k, shape, fan_in):
        return jax.random.normal(k, shape, jnp.float32) * (fan_in ** -0.5)

    def gain(k, shape):
        return 1.0 + 0.02 * jax.random.normal(k, shape, jnp.float32)

    def small(k, shape):
        return 0.02 * jax.random.normal(k, shape, jnp.float32)

    Lr = DEPTH
    return {
        "x": jax.random.normal(ks[0], (BATCH, SEQ, D_MODEL), jnp.float32),
        "ffn1_norm": gain(ks[1], (Lr, D_MODEL)),
        "ffn1_w_gate": w(ks[2], (Lr, D_MODEL, D_FF), D_MODEL),
        "ffn1_w_up": w(ks[3], (Lr, D_MODEL, D_FF), D_MODEL),
        "ffn1_w_down": w(ks[4], (Lr, D_FF, D_MODEL), D_FF),
        "mix_norm": gain(ks[5], (Lr, D_MODEL)),
        "w_in": w(ks[6], (Lr, D_MODEL, IN_WIDTH), D_MODEL),
        "conv_dw_kernel": w(ks[7], (Lr, CONV_WIDTH, CONV_CH), CONV_WIDTH),
        "conv_dw_bias": small(ks[8], (Lr, CONV_CH)),
        "conv_ln_gain": gain(ks[9], (Lr, CONV_CH)),
        "conv_ln_bias": small(ks[10], (Lr, CONV_CH)),
        "conv_w_out": w(ks[11], (Lr, CONV_CH, D_MODEL), CONV_CH),
        "attn_w_out": w(ks[12], (Lr, ATTN_OUT_WIDTH, D_MODEL), ATTN_OUT_WIDTH),
        "w_o": w(ks[13], (Lr, D_MODEL, D_MODEL), D_MODEL),
        "ffn2_norm": gain(ks[14], (Lr, D_MODEL)),
        "ffn2_w_gate": w(ks[15], (Lr, D_MODEL, D_FF), D_MODEL),
        "ffn2_w_up": w(ks[16], (Lr, D_MODEL, D_FF), D_MODEL),
        "ffn2_w_down": w(ks[17], (Lr, D_FF, D_MODEL), D_FF),
        "final_norm": gain(ks[18], (D_MODEL,)),
    }


def _fwd_reference(x, ffn1_norm, ffn1_w_gate, ffn1_w_up, ffn1_w_down, mix_norm, w_in,
              conv_dw_kernel, conv_dw_bias, conv_ln_gain, conv_ln_bias, conv_w_out,
              attn_w_out, w_o, ffn2_norm, ffn2_w_gate, ffn2_w_up, ffn2_w_down, final_norm):
    B, S, _ = x.shape
    slopes = jnp.asarray(_alibi_slopes())
    split_at = [2 * CONV_CH,
                2 * CONV_CH + ATTN_WIDTH,
                2 * CONV_CH + 2 * ATTN_WIDTH,
                2 * CONV_CH + 3 * ATTN_WIDTH,
                2 * CONV_CH + 3 * ATTN_WIDTH + D_MODEL]
    for l in range(DEPTH):
        x = x + 0.5 * swiglu_ffn(rmsnorm(x, ffn1_norm[l]), ffn1_w_gate[l], ffn1_w_up[l], ffn1_w_down[l])

        h = rmsnorm(x, mix_norm[l])
        proj = h @ w_in[l]
        u_conv, q, k, v, g_conv, g_attn = jnp.split(proj, split_at, axis=-1)

        y_conv = conformer_conv(u_conv, conv_dw_kernel[l], conv_dw_bias[l],
                                conv_ln_gain[l], conv_ln_bias[l], conv_w_out[l])

        q = q.reshape(B, S, N_GROUPS, HEADS_PER_GROUP, HEAD_DIM)
        k = k.reshape(B, S, N_GROUPS, HEADS_PER_GROUP, HEAD_DIM)
        v = v.reshape(B, S, N_GROUPS, HEADS_PER_GROUP, HEAD_DIM)
        outs, lses = [], []
        for gi, (window, dilation) in enumerate(ATTN_GROUPS):
            o_g, lse_g = dilated_group_attention(
                q[:, :, gi], k[:, :, gi], v[:, :, gi],
                slopes[gi * HEADS_PER_GROUP:(gi + 1) * HEADS_PER_GROUP], window, dilation)
            outs.append(o_g)
            lses.append(lse_g)
        outs = jnp.stack(outs, axis=0)
        lam = jax.nn.softmax(jnp.stack(lses, axis=0), axis=0)
        attn = jnp.sum(lam[..., None].astype(outs.dtype) * outs, axis=0)
        y_attn = attn.reshape(B, S, ATTN_OUT_WIDTH) @ attn_w_out[l]

        mixed = jax.nn.sigmoid(g_conv) * y_conv + jax.nn.sigmoid(g_attn) * y_attn
        x = x + mixed @ w_o[l]

        x = x + 0.5 * swiglu_ffn(rmsnorm(x, ffn2_norm[l]), ffn2_w_gate[l], ffn2_w_up[l], ffn2_w_down[l])
    return rmsnorm(x, final_norm)


import jax as _jax
import jax.numpy as _jnp

TWIN_FORMAT = 'train_step'
FWD_PARAMS = ['x', 'ffn1_norm', 'ffn1_w_gate', 'ffn1_w_up', 'ffn1_w_down', 'mix_norm', 'w_in', 'conv_dw_kernel', 'conv_dw_bias', 'conv_ln_gain', 'conv_ln_bias', 'conv_w_out', 'attn_w_out', 'w_o', 'ffn2_norm', 'ffn2_w_gate', 'ffn2_w_up', 'ffn2_w_down', 'final_norm']
TWIN_WEIGHTS = ['ffn1_norm', 'ffn1_w_gate', 'ffn1_w_up', 'ffn1_w_down', 'mix_norm', 'w_in', 'conv_dw_kernel', 'conv_dw_bias', 'conv_ln_gain', 'conv_ln_bias', 'conv_w_out', 'attn_w_out', 'w_o', 'ffn2_norm', 'ffn2_w_gate', 'ffn2_w_up', 'ffn2_w_down', 'final_norm']
TWIN_DIFF_INPUT = 'x'
TWIN_INPUTS = ['x', 'ffn1_norm', 'ffn1_w_gate', 'ffn1_w_up', 'ffn1_w_down', 'mix_norm', 'w_in', 'conv_dw_kernel', 'conv_dw_bias', 'conv_ln_gain', 'conv_ln_bias', 'conv_w_out', 'attn_w_out', 'w_o', 'ffn2_norm', 'ffn2_w_gate', 'ffn2_w_up', 'ffn2_w_down', 'final_norm', 'loss_target', 'm_ffn1_norm', 'm_ffn1_w_gate', 'm_ffn1_w_up', 'm_ffn1_w_down', 'm_mix_norm', 'm_w_in', 'm_conv_dw_kernel', 'm_conv_dw_bias', 'm_conv_ln_gain', 'm_conv_ln_bias', 'm_conv_w_out', 'm_attn_w_out', 'm_w_o', 'm_ffn2_norm', 'm_ffn2_w_gate', 'm_ffn2_w_up', 'm_ffn2_w_down', 'm_final_norm', 'v_ffn1_norm', 'v_ffn1_w_gate', 'v_ffn1_w_up', 'v_ffn1_w_down', 'v_mix_norm', 'v_w_in', 'v_conv_dw_kernel', 'v_conv_dw_bias', 'v_conv_ln_gain', 'v_conv_ln_bias', 'v_conv_w_out', 'v_attn_w_out', 'v_w_o', 'v_ffn2_norm', 'v_ffn2_w_gate', 'v_ffn2_w_up', 'v_ffn2_w_down', 'v_final_norm']
TWIN_OUTPUTS = ['loss', 'grad_x', 'grad_ffn1_norm', 'grad_ffn1_w_gate', 'grad_ffn1_w_up', 'grad_ffn1_w_down', 'grad_mix_norm', 'grad_w_in', 'grad_conv_dw_kernel', 'grad_conv_dw_bias', 'grad_conv_ln_gain', 'grad_conv_ln_bias', 'grad_conv_w_out', 'grad_attn_w_out', 'grad_w_o', 'grad_ffn2_norm', 'grad_ffn2_w_gate', 'grad_ffn2_w_up', 'grad_ffn2_w_down', 'grad_final_norm', 'delta_ffn1_norm', 'delta_ffn1_w_gate', 'delta_ffn1_w_up', 'delta_ffn1_w_down', 'delta_mix_norm', 'delta_w_in', 'delta_conv_dw_kernel', 'delta_conv_dw_bias', 'delta_conv_ln_gain', 'delta_conv_ln_bias', 'delta_conv_w_out', 'delta_attn_w_out', 'delta_w_o', 'delta_ffn2_norm', 'delta_ffn2_w_gate', 'delta_ffn2_w_up', 'delta_ffn2_w_down', 'delta_final_norm', 'new_m_ffn1_norm', 'new_m_ffn1_w_gate', 'new_m_ffn1_w_up', 'new_m_ffn1_w_down', 'new_m_mix_norm', 'new_m_w_in', 'new_m_conv_dw_kernel', 'new_m_conv_dw_bias', 'new_m_conv_ln_gain', 'new_m_conv_ln_bias', 'new_m_conv_w_out', 'new_m_attn_w_out', 'new_m_w_o', 'new_m_ffn2_norm', 'new_m_ffn2_w_gate', 'new_m_ffn2_w_up', 'new_m_ffn2_w_down', 'new_m_final_norm', 'new_v_ffn1_norm', 'new_v_ffn1_w_gate', 'new_v_ffn1_w_up', 'new_v_ffn1_w_down', 'new_v_mix_norm', 'new_v_w_in', 'new_v_conv_dw_kernel', 'new_v_conv_dw_bias', 'new_v_conv_ln_gain', 'new_v_conv_ln_bias', 'new_v_conv_w_out', 'new_v_attn_w_out', 'new_v_w_o', 'new_v_ffn2_norm', 'new_v_ffn2_w_gate', 'new_v_ffn2_w_up', 'new_v_ffn2_w_down', 'new_v_final_norm']
TWIN_LEAF_KINDS = {'loss': 'loss', 'grad_x': 'grad_x', 'grad_ffn1_norm': 'grad_w', 'grad_ffn1_w_gate': 'grad_w', 'grad_ffn1_w_up': 'grad_w', 'grad_ffn1_w_down': 'grad_w', 'grad_mix_norm': 'grad_w', 'grad_w_in': 'grad_w', 'grad_conv_dw_kernel': 'grad_w', 'grad_conv_dw_bias': 'grad_w', 'grad_conv_ln_gain': 'grad_w', 'grad_conv_ln_bias': 'grad_w', 'grad_conv_w_out': 'grad_w', 'grad_attn_w_out': 'grad_w', 'grad_w_o': 'grad_w', 'grad_ffn2_norm': 'grad_w', 'grad_ffn2_w_gate': 'grad_w', 'grad_ffn2_w_up': 'grad_w', 'grad_ffn2_w_down': 'grad_w', 'grad_final_norm': 'grad_w', 'delta_ffn1_norm': 'delta_w', 'delta_ffn1_w_gate': 'delta_w', 'delta_ffn1_w_up': 'delta_w', 'delta_ffn1_w_down': 'delta_w', 'delta_mix_norm': 'delta_w', 'delta_w_in': 'delta_w', 'delta_conv_dw_kernel': 'delta_w', 'delta_conv_dw_bias': 'delta_w', 'delta_conv_ln_gain': 'delta_w', 'delta_conv_ln_bias': 'delta_w', 'delta_conv_w_out': 'delta_w', 'delta_attn_w_out': 'delta_w', 'delta_w_o': 'delta_w', 'delta_ffn2_norm': 'delta_w', 'delta_ffn2_w_gate': 'delta_w', 'delta_ffn2_w_up': 'delta_w', 'delta_ffn2_w_down': 'delta_w', 'delta_final_norm': 'delta_w', 'new_m_ffn1_norm': 'new_m', 'new_m_ffn1_w_gate': 'new_m', 'new_m_ffn1_w_up': 'new_m', 'new_m_ffn1_w_down': 'new_m', 'new_m_mix_norm': 'new_m', 'new_m_w_in': 'new_m', 'new_m_conv_dw_kernel': 'new_m', 'new_m_conv_dw_bias': 'new_m', 'new_m_conv_ln_gain': 'new_m', 'new_m_conv_ln_bias': 'new_m', 'new_m_conv_w_out': 'new_m', 'new_m_attn_w_out': 'new_m', 'new_m_w_o': 'new_m', 'new_m_ffn2_norm': 'new_m', 'new_m_ffn2_w_gate': 'new_m', 'new_m_ffn2_w_up': 'new_m', 'new_m_ffn2_w_down': 'new_m', 'new_m_final_norm': 'new_m', 'new_v_ffn1_norm': 'new_v', 'new_v_ffn1_w_gate': 'new_v', 'new_v_ffn1_w_up': 'new_v', 'new_v_ffn1_w_down': 'new_v', 'new_v_mix_norm': 'new_v', 'new_v_w_in': 'new_v', 'new_v_conv_dw_kernel': 'new_v', 'new_v_conv_dw_bias': 'new_v', 'new_v_conv_ln_gain': 'new_v', 'new_v_conv_ln_bias': 'new_v', 'new_v_conv_w_out': 'new_v', 'new_v_attn_w_out': 'new_v', 'new_v_w_o': 'new_v', 'new_v_ffn2_norm': 'new_v', 'new_v_ffn2_w_gate': 'new_v', 'new_v_ffn2_w_up': 'new_v', 'new_v_ffn2_w_down': 'new_v', 'new_v_final_norm': 'new_v'}


def _forward(args):
    return _fwd_reference(*[args[k] for k in FWD_PARAMS])


def _output_shape():
    out = _jax.eval_shape(lambda: _forward(_fwd_setup_inputs(0)))
    return out.shape, out.dtype

N_MICROBATCH = 1
ADAM_LR = 0.001
ADAM_B1 = 0.9
ADAM_B2 = 0.999
ADAM_EPS = 1e-08
ADAM_WD = 0.01
ADAM_STEP = 10
PER_EXAMPLE_BATCH_AXIS = {'x': 0, 'loss_target': 0}
SHARED_INPUTS = []
_WEIGHT_DTYPES = {'ffn1_norm': _jnp.float32, 'ffn1_w_gate': _jnp.float32, 'ffn1_w_up': _jnp.float32, 'ffn1_w_down': _jnp.float32, 'mix_norm': _jnp.float32, 'w_in': _jnp.float32, 'conv_dw_kernel': _jnp.float32, 'conv_dw_bias': _jnp.float32, 'conv_ln_gain': _jnp.float32, 'conv_ln_bias': _jnp.float32, 'conv_w_out': _jnp.float32, 'attn_w_out': _jnp.float32, 'w_o': _jnp.float32, 'ffn2_norm': _jnp.float32, 'ffn2_w_gate': _jnp.float32, 'ffn2_w_up': _jnp.float32, 'ffn2_w_down': _jnp.float32, 'final_norm': _jnp.float32}
MOMENT_SCALE = {'ffn1_norm': 8.978186e-02, 'ffn1_w_gate': 3.381041e-02, 'ffn1_w_up': 3.269358e-02, 'ffn1_w_down': 5.421712e-02, 'mix_norm': 8.104680e-02, 'w_in': 2.710066e-02, 'conv_dw_kernel': 5.626604e-02, 'conv_dw_bias': 1.200598e-01, 'conv_ln_gain': 6.494874e-02, 'conv_ln_bias': 5.755402e-02, 'conv_w_out': 5.446302e-02, 'attn_w_out': 2.953984e-02, 'w_o': 6.192583e-02, 'ffn2_norm': 6.560246e-02, 'ffn2_w_gate': 2.885337e-02, 'ffn2_w_up': 2.789749e-02, 'ffn2_w_down': 4.628235e-02, 'final_norm': 3.199923e+01}


def _to_microbatches(a, axis):
    t = _jnp.moveaxis(a, axis, 0)
    t = t.reshape((N_MICROBATCH, t.shape[0] // N_MICROBATCH) + t.shape[1:])
    return _jnp.moveaxis(t, 1, axis + 1)


def setup_inputs(seed: int = 0) -> dict:
    inp = _fwd_setup_inputs(seed)
    key = _jax.random.fold_in(_jax.random.key(seed), 7919)
    shape, _ = _output_shape()
    out = dict(inp)
    out["loss_target"] = _jax.random.normal(_jax.random.fold_in(key, 0), shape, _jnp.float32)
    for i, name in enumerate(TWIN_WEIGHTS):
        w = inp[name].astype(_jnp.float32)
        if MOMENT_SCALE is None:
            s = _jnp.sqrt(_jnp.mean(_jnp.square(w)) + 1e-30)
        else:
            s = MOMENT_SCALE[name]
        km, kv = _jax.random.split(_jax.random.fold_in(key, i + 1))
        out[name] = w
        out["m_" + name] = s * _jax.random.normal(km, w.shape, _jnp.float32)
        out["v_" + name] = (s * s) * _jax.random.uniform(kv, w.shape, _jnp.float32, 0.5, 1.5)
    if N_MICROBATCH > 1:
        for name, axis in PER_EXAMPLE_BATCH_AXIS.items():
            out[name] = _to_microbatches(out[name], axis)
    return {'x': out['x'], 'ffn1_norm': out['ffn1_norm'], 'ffn1_w_gate': out['ffn1_w_gate'], 'ffn1_w_up': out['ffn1_w_up'], 'ffn1_w_down': out['ffn1_w_down'], 'mix_norm': out['mix_norm'], 'w_in': out['w_in'], 'conv_dw_kernel': out['conv_dw_kernel'], 'conv_dw_bias': out['conv_dw_bias'], 'conv_ln_gain': out['conv_ln_gain'], 'conv_ln_bias': out['conv_ln_bias'], 'conv_w_out': out['conv_w_out'], 'attn_w_out': out['attn_w_out'], 'w_o': out['w_o'], 'ffn2_norm': out['ffn2_norm'], 'ffn2_w_gate': out['ffn2_w_gate'], 'ffn2_w_up': out['ffn2_w_up'], 'ffn2_w_down': out['ffn2_w_down'], 'final_norm': out['final_norm'], 'loss_target': out['loss_target'], 'm_ffn1_norm': out['m_ffn1_norm'], 'm_ffn1_w_gate': out['m_ffn1_w_gate'], 'm_ffn1_w_up': out['m_ffn1_w_up'], 'm_ffn1_w_down': out['m_ffn1_w_down'], 'm_mix_norm': out['m_mix_norm'], 'm_w_in': out['m_w_in'], 'm_conv_dw_kernel': out['m_conv_dw_kernel'], 'm_conv_dw_bias': out['m_conv_dw_bias'], 'm_conv_ln_gain': out['m_conv_ln_gain'], 'm_conv_ln_bias': out['m_conv_ln_bias'], 'm_conv_w_out': out['m_conv_w_out'], 'm_attn_w_out': out['m_attn_w_out'], 'm_w_o': out['m_w_o'], 'm_ffn2_norm': out['m_ffn2_norm'], 'm_ffn2_w_gate': out['m_ffn2_w_gate'], 'm_ffn2_w_up': out['m_ffn2_w_up'], 'm_ffn2_w_down': out['m_ffn2_w_down'], 'm_final_norm': out['m_final_norm'], 'v_ffn1_norm': out['v_ffn1_norm'], 'v_ffn1_w_gate': out['v_ffn1_w_gate'], 'v_ffn1_w_up': out['v_ffn1_w_up'], 'v_ffn1_w_down': out['v_ffn1_w_down'], 'v_mix_norm': out['v_mix_norm'], 'v_w_in': out['v_w_in'], 'v_conv_dw_kernel': out['v_conv_dw_kernel'], 'v_conv_dw_bias': out['v_conv_dw_bias'], 'v_conv_ln_gain': out['v_conv_ln_gain'], 'v_conv_ln_bias': out['v_conv_ln_bias'], 'v_conv_w_out': out['v_conv_w_out'], 'v_attn_w_out': out['v_attn_w_out'], 'v_w_o': out['v_w_o'], 'v_ffn2_norm': out['v_ffn2_norm'], 'v_ffn2_w_gate': out['v_ffn2_w_gate'], 'v_ffn2_w_up': out['v_ffn2_w_up'], 'v_ffn2_w_down': out['v_ffn2_w_down'], 'v_final_norm': out['v_final_norm']}


def _loss(weights, diff, rest, loss_target):
    with _jax.named_scope("forward"):
        args = {**rest, TWIN_DIFF_INPUT: diff, **{k: w.astype(_WEIGHT_DTYPES[k]) for k, w in weights.items()}}
        y = _forward(args)
    with _jax.named_scope("loss_head"):
        err = _jnp.square(y.astype(_jnp.float32) - loss_target)
        return 0.5 * _jnp.sum(_jnp.mean(err, axis=-1)) if err.ndim else 0.5 * err


def _adamw(w, g, m, v):
    m = ADAM_B1 * m + (1.0 - ADAM_B1) * g
    v = ADAM_B2 * v + (1.0 - ADAM_B2) * _jnp.square(g)
    m_hat = m / (1.0 - ADAM_B1 ** ADAM_STEP)
    v_hat = v / (1.0 - ADAM_B2 ** ADAM_STEP)
    delta = -ADAM_LR * (m_hat / (_jnp.sqrt(v_hat) + ADAM_EPS) + ADAM_WD * w)
    return delta, m, v


def reference(x, ffn1_norm, ffn1_w_gate, ffn1_w_up, ffn1_w_down, mix_norm, w_in, conv_dw_kernel, conv_dw_bias, conv_ln_gain, conv_ln_bias, conv_w_out, attn_w_out, w_o, ffn2_norm, ffn2_w_gate, ffn2_w_up, ffn2_w_down, final_norm, loss_target, m_ffn1_norm, m_ffn1_w_gate, m_ffn1_w_up, m_ffn1_w_down, m_mix_norm, m_w_in, m_conv_dw_kernel, m_conv_dw_bias, m_conv_ln_gain, m_conv_ln_bias, m_conv_w_out, m_attn_w_out, m_w_o, m_ffn2_norm, m_ffn2_w_gate, m_ffn2_w_up, m_ffn2_w_down, m_final_norm, v_ffn1_norm, v_ffn1_w_gate, v_ffn1_w_up, v_ffn1_w_down, v_mix_norm, v_w_in, v_conv_dw_kernel, v_conv_dw_bias, v_conv_ln_gain, v_conv_ln_bias, v_conv_w_out, v_attn_w_out, v_w_o, v_ffn2_norm, v_ffn2_w_gate, v_ffn2_w_up, v_ffn2_w_down, v_final_norm):
    given = dict(x=x, ffn1_norm=ffn1_norm, ffn1_w_gate=ffn1_w_gate, ffn1_w_up=ffn1_w_up, ffn1_w_down=ffn1_w_down, mix_norm=mix_norm, w_in=w_in, conv_dw_kernel=conv_dw_kernel, conv_dw_bias=conv_dw_bias, conv_ln_gain=conv_ln_gain, conv_ln_bias=conv_ln_bias, conv_w_out=conv_w_out, attn_w_out=attn_w_out, w_o=w_o, ffn2_norm=ffn2_norm, ffn2_w_gate=ffn2_w_gate, ffn2_w_up=ffn2_w_up, ffn2_w_down=ffn2_w_down, final_norm=final_norm, loss_target=loss_target, m_ffn1_norm=m_ffn1_norm, m_ffn1_w_gate=m_ffn1_w_gate, m_ffn1_w_up=m_ffn1_w_up, m_ffn1_w_down=m_ffn1_w_down, m_mix_norm=m_mix_norm, m_w_in=m_w_in, m_conv_dw_kernel=m_conv_dw_kernel, m_conv_dw_bias=m_conv_dw_bias, m_conv_ln_gain=m_conv_ln_gain, m_conv_ln_bias=m_conv_ln_bias, m_conv_w_out=m_conv_w_out, m_attn_w_out=m_attn_w_out, m_w_o=m_w_o, m_ffn2_norm=m_ffn2_norm, m_ffn2_w_gate=m_ffn2_w_gate, m_ffn2_w_up=m_ffn2_w_up, m_ffn2_w_down=m_ffn2_w_down, m_final_norm=m_final_norm, v_ffn1_norm=v_ffn1_norm, v_ffn1_w_gate=v_ffn1_w_gate, v_ffn1_w_up=v_ffn1_w_up, v_ffn1_w_down=v_ffn1_w_down, v_mix_norm=v_mix_norm, v_w_in=v_w_in, v_conv_dw_kernel=v_conv_dw_kernel, v_conv_dw_bias=v_conv_dw_bias, v_conv_ln_gain=v_conv_ln_gain, v_conv_ln_bias=v_conv_ln_bias, v_conv_w_out=v_conv_w_out, v_attn_w_out=v_attn_w_out, v_w_o=v_w_o, v_ffn2_norm=v_ffn2_norm, v_ffn2_w_gate=v_ffn2_w_gate, v_ffn2_w_up=v_ffn2_w_up, v_ffn2_w_down=v_ffn2_w_down, v_final_norm=v_final_norm)
    weights = {n: given[n] for n in TWIN_WEIGHTS}
    shared = {n: given[n] for n in SHARED_INPUTS}
    per_example = {n: given[n] for n in ['x']}
    grad_fn = _jax.value_and_grad(_loss, argnums=(0, 1))

    def one_microbatch(ex, loss_target):
        ex = dict(ex)
        diff = ex.pop(TWIN_DIFF_INPUT)
        return grad_fn(weights, diff, {**shared, **ex}, loss_target)

    if N_MICROBATCH == 1:
        loss, (grad_w, grad_x) = one_microbatch(per_example, given["loss_target"])
    else:
        def body(carry, xs):
            loss_sum, grad_sum = carry
            l_k, (gw_k, gx_k) = one_microbatch(xs[0], xs[1])
            with _jax.named_scope("update"):
                return (loss_sum + l_k, _jax.tree.map(_jnp.add, grad_sum, gw_k)), gx_k

        init = (_jnp.zeros((), _jnp.float32), _jax.tree.map(_jnp.zeros_like, weights))
        (loss, grad_w), grad_x = _jax.lax.scan(body, init, (per_example, given["loss_target"]))
    with _jax.named_scope("update"):
        delta_w, new_m, new_v = {}, {}, {}
        for n in TWIN_WEIGHTS:
            delta_w[n], new_m[n], new_v[n] = _adamw(weights[n], grad_w[n], given["m_" + n], given["v_" + n])
    return (loss, grad_x, *[grad_w[n] for n in TWIN_WEIGHTS], *[delta_w[n] for n in TWIN_WEIGHTS],
            *[new_m[n] for n in TWIN_WEIGHTS], *[new_v[n] for n in TWIN_WEIGHTS])
```

```python
import numpy as np
import jax
import jax.numpy as jnp
from jax import lax
from jax.experimental import pallas as pl
from jax.experimental.pallas import tpu as pltpu

F32 = jnp.float32
BF16 = jnp.bfloat16

T = 4096
D = 1024
FF = 2816
NDEV = 8
CONV_W = 31
HEAD = 128
BLK = 128
GROUPS = ((128, 1), (512, 4), (2048, 16))
NHG = 4
AW = NHG * HEAD
IN_W = 2 * D + 3 * 3 * AW + 2 * D
EPS = 1e-6
B1, B2, LR, AEPS, WD, STEP = 0.9, 0.999, 0.001, 1e-08, 0.01, 10
NEG = -1e30
VMEM_LIMIT = 56 * 1024 * 1024
MESH_ID = pl.DeviceIdType.MESH

NT = (((1,), (1,)), ((), ()))
NN = (((1,), (0,)), ((), ()))
TN = (((0,), (0,)), ((), ()))
_DIMS = {"nn": NN, "nt": NT, "tn": TN}


def _cp(sem=None):
    return pltpu.CompilerParams(dimension_semantics=sem, vmem_limit_bytes=VMEM_LIMIT)


def _sig(v):
    return 1.0 / (1.0 + jnp.exp(-v))


def _dot(a, b, dims):
    return lax.dot_general(a, b, dims, preferred_element_type=F32)


def _const_spec(shape):
    nd = len(shape)
    return pl.BlockSpec(shape, lambda *_: (0,) * nd)


def _mm(a, b, *, mode, m, n, k, tm, tn, tk, out_dtype, name, a_map=None, b_map=None,
        o_map=None, out_rows=None, init=None, passthru=None):
    gi, gj, gk = m // tm, n // tn, k // tk
    assert gi * tm == m and gj * tn == n and gk * tk == k, (name, m, n, k, tm, tn, tk)
    if mode == "nn":
        a_blk, b_blk = (tm, tk), (tk, tn)
        da, db = (lambda i, j, kk: (i, kk)), (lambda i, j, kk: (kk, j))
    elif mode == "nt":
        a_blk, b_blk = (tm, tk), (tn, tk)
        da, db = (lambda i, j, kk: (i, kk)), (lambda i, j, kk: (j, kk))
    else:
        a_blk, b_blk = (tk, tm), (tk, tn)
        da, db = (lambda i, j, kk: (kk, i)), (lambda i, j, kk: (kk, j))
    a_map = a_map or da
    b_map = b_map or db
    o_map = o_map or (lambda i, j, kk: (i, j))
    dims = _DIMS[mode]
    extra = init if init is not None else passthru
    out_rows = out_rows or m

    def body(*refs):
        if init is not None:
            a_ref, b_ref, i_ref, o_ref, acc = refs
        elif passthru is not None:
            a_ref, b_ref, _, o_ref, acc = refs
        else:
            a_ref, b_ref, o_ref, acc = refs
        kk = pl.program_id(2)

        @pl.when(kk == 0)
        def _():
            if init is not None:
                acc[...] = i_ref[...].astype(F32)
            else:
                acc[...] = jnp.zeros_like(acc)

        acc[...] += _dot(a_ref[...], b_ref[...], dims)

        @pl.when(kk == gk - 1)
        def _():
            o_ref[...] = acc[...].astype(out_dtype)

    in_specs = [pl.BlockSpec(a_blk, a_map), pl.BlockSpec(b_blk, b_map)]
    args = [a, b]
    aliases = {}
    if init is not None:
        in_specs.append(pl.BlockSpec((tm, tn), o_map))
        args.append(init)
        aliases = {2: 0}
    elif passthru is not None:
        in_specs.append(pl.BlockSpec(memory_space=pl.ANY))
        args.append(passthru)
        aliases = {2: 0}
    out_dt = extra.dtype if extra is not None else out_dtype
    assert out_dt == out_dtype
    return pl.pallas_call(
        body, name=name, grid=(gi, gj, gk),
        in_specs=in_specs, out_specs=pl.BlockSpec((tm, tn), o_map),
        out_shape=jax.ShapeDtypeStruct((out_rows, n), out_dtype),
        scratch_shapes=[pltpu.VMEM((tm, tn), F32)],
        input_output_aliases=aliases,
        compiler_params=_cp(("parallel", "parallel", "arbitrary")),
    )(*args)


def _ffn_fwd(x, g, wg_t, wu_t, wd, name):
    tm, fc = 512, 256
    nc = FF // fc

    def body(x_ref, g_ref, wg_ref, wu_ref, wd_ref, xo_ref, gg_ref, uu_ref, acc):
        xv = x_ref[...]
        r = lax.rsqrt(jnp.mean(xv * xv, axis=-1, keepdims=True) + EPS)
        h = (xv * r * g_ref[...]).astype(BF16)
        acc[...] = jnp.zeros_like(acc)
        for c in range(nc):
            sl = pl.ds(c * fc, fc)
            gg = _dot(h, wg_ref[sl, :], NT)
            uu = _dot(h, wu_ref[sl, :], NT)
            gg_ref[:, sl] = gg.astype(BF16)
            uu_ref[:, sl] = uu.astype(BF16)
            act = (gg * _sig(gg) * uu).astype(BF16)
            acc[...] += _dot(act, wd_ref[sl, :], NN)
        xo_ref[...] = xv + 0.5 * acc[...]

    wspec = pl.BlockSpec((FF, D), lambda i: (0, 0), pipeline_mode=pl.Buffered(1))
    return pl.pallas_call(
        body, name=name, grid=(T // tm,),
        in_specs=[pl.BlockSpec((tm, D), lambda i: (i, 0)), _const_spec((1, D)), wspec, wspec, wspec],
        out_specs=[pl.BlockSpec((tm, D), lambda i: (i, 0)), pl.BlockSpec((tm, FF), lambda i: (i, 0)),
                   pl.BlockSpec((tm, FF), lambda i: (i, 0))],
        out_shape=[jax.ShapeDtypeStruct((T, D), F32), jax.ShapeDtypeStruct((T, FF), BF16),
                   jax.ShapeDtypeStruct((T, FF), BF16)],
        scratch_shapes=[pltpu.VMEM((tm, D), F32)],
        compiler_params=_cp(("parallel",)),
    )(x, g, wg_t, wu_t, wd)


def _ffn_bwd(x, g, gg_all, uu_all, dout, wg_t, wu_t, wd, name):
    tm, fc = 256, 256
    nc = FF // fc

    def body(x_ref, g_ref, gg_ref, uu_ref, do_ref, wg_ref, wu_ref, wd_ref,
             dx_ref, dgam_ref, dg_ref, du_ref, act_ref, h_ref, db_ref, acc):
        i = pl.program_id(0)
        xv = x_ref[...]
        r = lax.rsqrt(jnp.mean(xv * xv, axis=-1, keepdims=True) + EPS)
        xhat = xv * r
        gam = g_ref[...]
        h_ref[...] = (xhat * gam).astype(BF16)
        dov = do_ref[...]
        dbv = (0.5 * dov).astype(BF16)
        db_ref[...] = dbv
        acc[...] = jnp.zeros_like(acc)
        for c in range(nc):
            sl = pl.ds(c * fc, fc)
            da = _dot(dbv, wd_ref[sl, :], NT)
            gg = gg_ref[:, sl].astype(F32)
            uu = uu_ref[:, sl].astype(F32)
            s = _sig(gg)
            si = gg * s
            dgv = (da * uu * (s * (1.0 + gg * (1.0 - s)))).astype(BF16)
            duv = (da * si).astype(BF16)
            dg_ref[:, sl] = dgv
            du_ref[:, sl] = duv
            act_ref[:, sl] = (si * uu).astype(BF16)
            acc[...] += _dot(dgv, wg_ref[sl, :], NN) + _dot(duv, wu_ref[sl, :], NN)
        dh = acc[...]

        @pl.when(i == 0)
        def _():
            dgam_ref[...] = jnp.zeros_like(dgam_ref)

        dgam_ref[...] += jnp.sum(dh * xhat, axis=0, keepdims=True)
        dxh = dh * gam
        dx_ref[...] = dov + r * (dxh - xhat * jnp.mean(dxh * xhat, axis=-1, keepdims=True))

    wspec = pl.BlockSpec((FF, D), lambda i: (0, 0), pipeline_mode=pl.Buffered(1))
    row_d = pl.BlockSpec((tm, D), lambda i: (i, 0))
    row_f = pl.BlockSpec((tm, FF), lambda i: (i, 0))
    return pl.pallas_call(
        body, name=name, grid=(T // tm,),
        in_specs=[row_d, _const_spec((1, D)), row_f, row_f, row_d, wspec, wspec, wspec],
        out_specs=[row_d, _const_spec((1, D)), row_f, row_f, row_f, row_d, row_d],
        out_shape=[jax.ShapeDtypeStruct((T, D), F32), jax.ShapeDtypeStruct((1, D), F32),
                   jax.ShapeDtypeStruct((T, FF), BF16), jax.ShapeDtypeStruct((T, FF), BF16),
                   jax.ShapeDtypeStruct((T, FF), BF16), jax.ShapeDtypeStruct((T, D), BF16),
                   jax.ShapeDtypeStruct((T, D), BF16)],
        scratch_shapes=[pltpu.VMEM((tm, D), F32)],
        compiler_params=_cp(("arbitrary",)),
    )(x, g, gg_all, uu_all, dout, wg_t, wu_t, wd)


def _wgrad(a, b, m, n, name, tk=512):
    return _mm(a, b, mode="tn", m=m, n=n, k=T, tm=m, tn=n, tk=tk, out_dtype=BF16, name=name)


def _norm_cast(x, g, name):
    tm = 512

    def body(x_ref, g_ref, h_ref):
        xv = x_ref[...]
        r = lax.rsqrt(jnp.mean(xv * xv, axis=-1, keepdims=True) + EPS)
        h_ref[...] = (xv * r * g_ref[...]).astype(BF16)

    return pl.pallas_call(
        body, name=name, grid=(T // tm,),
        in_specs=[pl.BlockSpec((tm, D), lambda i: (i, 0)), _const_spec((1, D))],
        out_specs=pl.BlockSpec((tm, D), lambda i: (i, 0)),
        out_shape=jax.ShapeDtypeStruct((T, D), BF16),
        compiler_params=_cp(("parallel",)),
    )(x, g)


def _final(x3, gf, tgt, name):
    tm = 512

    def body(x_ref, g_ref, t_ref, dx_ref, dgam_ref, loss_ref):
        i = pl.program_id(0)
        xv = x_ref[...]
        r = lax.rsqrt(jnp.mean(xv * xv, axis=-1, keepdims=True) + EPS)
        xhat = xv * r
        gam = g_ref[...]
        err = xhat * gam - t_ref[...]
        part = 0.5 * jnp.sum(jnp.mean(err * err, axis=-1, keepdims=True), axis=0, keepdims=True)
        dy = err * (1.0 / D)

        @pl.when(i == 0)
        def _():
            dgam_ref[...] = jnp.zeros_like(dgam_ref)
            loss_ref[...] = jnp.zeros_like(loss_ref)

        dgam_ref[...] += jnp.sum(dy * xhat, axis=0, keepdims=True)
        loss_ref[...] += jnp.broadcast_to(part, loss_ref.shape)
        dxh = dy * gam
        dx_ref[...] = r * (dxh - xhat * jnp.mean(dxh * xhat, axis=-1, keepdims=True))

    row_d = pl.BlockSpec((tm, D), lambda i: (i, 0))
    return pl.pallas_call(
        body, name=name, grid=(T // tm,),
        in_specs=[row_d, _const_spec((1, D)), row_d],
        out_specs=[row_d, _const_spec((1, D)), _const_spec((1, 128))],
        out_shape=[jax.ShapeDtypeStruct((T, D), F32), jax.ShapeDtypeStruct((1, D), F32),
                   jax.ShapeDtypeStruct((1, 128), F32)],
        compiler_params=_cp(("arbitrary",)),
    )(x3, gf, tgt)


def _rms_bwd(x, g, dhs, dres, name):
    tm = 512
    nh = len(dhs)

    def body(*refs):
        x_ref, g_ref = refs[:2]
        dh_refs = refs[2:2 + nh]
        dr_ref, dx_ref, dgam_ref = refs[2 + nh:]
        i = pl.program_id(0)
        xv = x_ref[...]
        r = lax.rsqrt(jnp.mean(xv * xv, axis=-1, keepdims=True) + EPS)
        xhat = xv * r
        gam = g_ref[...]
        dh = dh_refs[0][...]
        for ref in dh_refs[1:]:
            dh = dh + ref[...]

        @pl.when(i == 0)
        def _():
            dgam_ref[...] = jnp.zeros_like(dgam_ref)

        dgam_ref[...] += jnp.sum(dh * xhat, axis=0, keepdims=True)
        dxh = dh * gam
        dx_ref[...] = dr_ref[...] + r * (dxh - xhat * jnp.mean(dxh * xhat, axis=-1, keepdims=True))

    row_d = pl.BlockSpec((tm, D), lambda i: (i, 0))
    return pl.pallas_call(
        body, name=name, grid=(T // tm,),
        in_specs=[row_d, _const_spec((1, D))] + [row_d] * nh + [row_d],
        out_specs=[row_d, _const_spec((1, D))],
        out_shape=[jax.ShapeDtypeStruct((T, D), F32), jax.ShapeDtypeStruct((1, D), F32)],
        compiler_params=_cp(("arbitrary",)),
    )(x, g, *dhs, dres)


CONV_TM = 256
CONV_HALO = 32
CONV_RB = 32


def _glu(ab):
    return ab[:, :D] * _sig(ab[:, D:])


def _ln_stats(z1):
    mu = jnp.mean(z1, axis=-1, keepdims=True)
    zc = z1 - mu
    rstd = lax.rsqrt(jnp.mean(zc * zc, axis=-1, keepdims=True) + EPS)
    return zc * rstd, rstd


def _conv_fwd(ab, kern, dwb, lng, lnb, name):
    tm, hl, rb = CONV_TM, CONV_HALO, CONV_RB
    off = hl - (CONV_W - 1)

    def body(ab_ref, abh_ref, k_ref, dwb_ref, lng_ref, lnb_ref, z1_ref, z3_ref, zbuf):
        i = pl.program_id(0)
        zbuf[pl.ds(0, hl), :] = jnp.where(i > 0, _glu(abh_ref[...]), 0.0)
        zbuf[pl.ds(hl, tm), :] = _glu(ab_ref[...])
        for b in range(tm // rb):
            acc = jnp.zeros((rb, D), F32)
            for j in range(CONV_W):
                acc = acc + zbuf[pl.ds(b * rb + off + j, rb), :] * k_ref[pl.ds(j, 1), :]
            z1 = acc + dwb_ref[...]
            z1_ref[pl.ds(b * rb, rb), :] = z1
            zn, _ = _ln_stats(z1)
            z2 = zn * lng_ref[...] + lnb_ref[...]
            z3_ref[pl.ds(b * rb, rb), :] = (z2 * _sig(z2)).astype(BF16)

    row = pl.BlockSpec((tm, D), lambda i: (i, 0))
    return pl.pallas_call(
        body, name=name, grid=(T // tm,),
        in_specs=[pl.BlockSpec((tm, 2 * D), lambda i: (i, 0)),
                  pl.BlockSpec((hl, 2 * D), lambda i: (jnp.maximum(i * (tm // hl) - 1, 0), 0)),
                  _const_spec((32, D)), _const_spec((1, D)), _const_spec((1, D)), _const_spec((1, D))],
        out_specs=[row, row],
        out_shape=[jax.ShapeDtypeStruct((T, D), F32), jax.ShapeDtypeStruct((T, D), BF16)],
        scratch_shapes=[pltpu.VMEM((hl + tm, D), F32)],
        compiler_params=_cp(("parallel",)),
    )(ab, ab, kern, dwb, lng, lnb)


def _conv_bwd(dz3, z1, ab, kern, lng, lnb, name):
    tm, hl, rb = CONV_TM, CONV_HALO, CONV_RB
    off = hl - (CONV_W - 1)
    nsteps = T // tm

    def ln_bwd(dz3v, z1v, lngv, lnbv):
        zn, rstd = _ln_stats(z1v)
        z2 = zn * lngv + lnbv
        s = _sig(z2)
        dz2 = dz3v * (s * (1.0 + z2 * (1.0 - s)))
        dzn = dz2 * lngv
        dz1 = rstd * (dzn - jnp.mean(dzn, axis=-1, keepdims=True)
                      - zn * jnp.mean(dzn * zn, axis=-1, keepdims=True))
        return dz1, dz2, zn

    def body(dz3_ref, dz3h_ref, z1_ref, z1h_ref, ab_ref, abh_ref, k_ref, lng_ref, lnb_ref,
             dab_ref, dk_ref, dvec_ref, zbuf, dzbuf):
        i = pl.program_id(0)
        lngv, lnbv = lng_ref[...], lnb_ref[...]

        @pl.when(i == 0)
        def _():
            dk_ref[...] = jnp.zeros_like(dk_ref)
            dvec_ref[...] = jnp.zeros_like(dvec_ref)

        dz1, dz2, zn = ln_bwd(dz3_ref[...], z1_ref[...], lngv, lnbv)
        dvec_ref[pl.ds(0, 1), :] += jnp.sum(dz1, axis=0, keepdims=True)
        dvec_ref[pl.ds(1, 1), :] += jnp.sum(dz2 * zn, axis=0, keepdims=True)
        dvec_ref[pl.ds(2, 1), :] += jnp.sum(dz2, axis=0, keepdims=True)
        dzbuf[pl.ds(0, tm), :] = dz1
        dz1h, _, _ = ln_bwd(dz3h_ref[...], z1h_ref[...], lngv, lnbv)
        dzbuf[pl.ds(tm, hl), :] = jnp.where(i < nsteps - 1, dz1h, 0.0)
        zbuf[pl.ds(0, hl), :] = jnp.where(i > 0, _glu(abh_ref[...]), 0.0)
        zbuf[pl.ds(hl, tm), :] = _glu(ab_ref[...])

        for j in range(CONV_W):
            tot = jnp.zeros((rb, D), F32)
            for b in range(tm // rb):
                tot = tot + dzbuf[pl.ds(b * rb, rb), :] * zbuf[pl.ds(b * rb + off + j, rb), :]
            dk_ref[pl.ds(j, 1), :] += jnp.sum(tot, axis=0, keepdims=True)

        for b in range(tm // rb):
            acc = jnp.zeros((rb, D), F32)
            for j in range(CONV_W):
                acc = acc + dzbuf[pl.ds(b * rb + (CONV_W - 1) - j, rb), :] * k_ref[pl.ds(j, 1), :]
            av = ab_ref[pl.ds(b * rb, rb), pl.ds(0, D)]
            sb = _sig(ab_ref[pl.ds(b * rb, rb), pl.ds(D, D)])
            dab_ref[pl.ds(b * rb, rb), pl.ds(0, D)] = (acc * sb).astype(BF16)
            dab_ref[pl.ds(b * rb, rb), pl.ds(D, D)] = (acc * av * sb * (1.0 - sb)).astype(BF16)

    row = pl.BlockSpec((tm, D), lambda i: (i, 0))
    nxt = pl.BlockSpec((hl, D), lambda i: (jnp.minimum((i + 1) * (tm // hl), T // hl - 1), 0))
    return pl.pallas_call(
        body, name=name, grid=(nsteps,),
        in_specs=[row, nxt, row, nxt,
                  pl.BlockSpec((tm, 2 * D), lambda i: (i, 0)),
                  pl.BlockSpec((hl, 2 * D), lambda i: (jnp.maximum(i * (tm // hl) - 1, 0), 0)),
                  _const_spec((32, D)), _const_spec((1, D)), _const_spec((1, D))],
        out_specs=[pl.BlockSpec((tm, 2 * D), lambda i: (i, 0)), _const_spec((32, D)), _const_spec((8, D))],
        out_shape=[jax.ShapeDtypeStruct((T, 2 * D), BF16), jax.ShapeDtypeStruct((32, D), F32),
                   jax.ShapeDtypeStruct((8, D), F32)],
        scratch_shapes=[pltpu.VMEM((hl + tm, D), F32), pltpu.VMEM((tm + hl, D), F32)],
        compiler_params=_cp(("arbitrary",)),
    )(dz3, dz3, z1, z1, ab, ab, kern, lng, lnb)


def _alibi_slopes():
    h = np.arange(1, 3 * NHG + 1, dtype=np.float32)
    return np.power(np.float32(2.0), -8.0 * h / np.float32(3 * NHG)).astype(np.float32)


def _band_bias(gi):
    _, dil = GROUPS[gi]
    slopes = _alibi_slopes()[gi * NHG:(gi + 1) * NHG]
    qi = np.arange(BLK)[:, None]
    ki = np.arange(2 * BLK)[None, :]
    steps = BLK + qi - ki
    band = (steps >= 0) & (steps <= BLK)
    bias = -slopes[:, None, None] * (dil * steps).astype(np.float32)[None]
    return jnp.asarray(np.where(band[None], bias, np.float32(NEG)).astype(np.float32))


def _attn_specs(seg):
    prev = lambda n: jnp.maximum(n - 1, 0)
    return [pl.BlockSpec((BLK, HEAD), lambda h, n: (n, h)),
            pl.BlockSpec((BLK, HEAD), lambda h, n: (prev(n), NHG + h)),
            pl.BlockSpec((BLK, HEAD), lambda h, n: (n, NHG + h)),
            pl.BlockSpec((BLK, HEAD), lambda h, n: (prev(n), 2 * NHG + h)),
            pl.BlockSpec((BLK, HEAD), lambda h, n: (n, 2 * NHG + h)),
            pl.BlockSpec((None, BLK, 2 * BLK), lambda h, n: (h, 0, 0))]


def _scores(q_ref, kp_ref, kc_ref, bias_ref, n, seg):
    kcat = jnp.concatenate([kp_ref[...], kc_ref[...]], axis=0)
    s = _dot(q_ref[...], kcat, NT) * (HEAD ** -0.5) + bias_ref[...]
    col = lax.broadcasted_iota(jnp.int32, s.shape, 1)
    first = (n % seg) == 0
    return jnp.where(jnp.logical_and(first, col < BLK), NEG, s), kcat


def _attn_fwd(qkv, gi, name):
    seg = (T // GROUPS[gi][1]) // BLK

    def body(q_ref, kp_ref, kc_ref, vp_ref, vc_ref, bias_ref, o_ref, l_ref):
        n = pl.program_id(1)
        s, _ = _scores(q_ref, kp_ref, kc_ref, bias_ref, n, seg)
        mx = jnp.max(s, axis=-1, keepdims=True)
        p = jnp.exp(s - mx)
        den = jnp.sum(p, axis=-1, keepdims=True)
        vcat = jnp.concatenate([vp_ref[...], vc_ref[...]], axis=0)
        o_ref[...] = _dot(p.astype(BF16), vcat, NN) / den
        l_ref[...] = jnp.broadcast_to(mx + jnp.log(den), l_ref.shape)

    oblk = pl.BlockSpec((BLK, HEAD), lambda h, n: (n, h))
    return pl.pallas_call(
        body, name=name, grid=(NHG, T // BLK),
        in_specs=_attn_specs(seg), out_specs=[oblk, oblk],
        out_shape=[jax.ShapeDtypeStruct((T, AW), F32), jax.ShapeDtypeStruct((T, AW), F32)],
        compiler_params=_cp(("parallel", "parallel")),
    )(qkv, qkv, qkv, qkv, qkv, _band_bias(gi))


def _attn_bwd(qkv, dob, lse, delta, gi, name):
    seg = (T // GROUPS[gi][1]) // BLK
    nb = T // BLK
    scale = HEAD ** -0.5

    def body(q_ref, kp_ref, kc_ref, vp_ref, vc_ref, bias_ref, do_ref, l_ref, dl_ref, out_ref, dk_acc, dv_acc):
        n = pl.program_id(1)
        s, kcat = _scores(q_ref, kp_ref, kc_ref, bias_ref, n, seg)
        p = jnp.exp(s - l_ref[:, pl.ds(0, 1)])
        dov = do_ref[...]
        vcat = jnp.concatenate([vp_ref[...], vc_ref[...]], axis=0)
        dv2 = _dot(p.astype(BF16), dov, TN)
        dp = _dot(dov, vcat, NT)
        dsb = (p * (dp - dl_ref[:, pl.ds(0, 1)]) * scale).astype(BF16)
        row = pl.ds(pl.multiple_of(n * BLK, BLK), BLK)
        out_ref[0, row, :] = _dot(dsb, kcat, NN).astype(BF16)
        dk2 = _dot(dsb, q_ref[...], TN)
        dk_acc[row, :] = dk2[BLK:]
        dv_acc[row, :] = dv2[BLK:]

        @pl.when(n > 0)
        def _():
            prow = pl.ds(pl.multiple_of((n - 1) * BLK, BLK), BLK)
            dk_acc[prow, :] += dk2[:BLK]
            dv_acc[prow, :] += dv2[:BLK]

        @pl.when(n == nb - 1)
        def _():
            out_ref[1] = dk_acc[...].astype(BF16)
            out_ref[2] = dv_acc[...].astype(BF16)

    oblk = pl.BlockSpec((BLK, HEAD), lambda h, n: (n, h))
    return pl.pallas_call(
        body, name=name, grid=(NHG, nb),
        in_specs=_attn_specs(seg) + [oblk, oblk, oblk],
        out_specs=pl.BlockSpec((3, T, HEAD), lambda h, n: (0, 0, h)),
        out_shape=jax.ShapeDtypeStruct((3, T, AW), BF16),
        scratch_shapes=[pltpu.VMEM((T, HEAD), F32), pltpu.VMEM((T, HEAD), F32)],
        compiler_params=_cp(("parallel", "arbitrary")),
    )(qkv, qkv, qkv, qkv, qkv, _band_bias(gi), dob, lse, delta)


def _merge(outs, lses, name):
    tm = 512

    def body(o0, l0, o1, l1, o2, l2, a_ref, ab_ref, lse_ref):
        ls = [l0[...], l1[...], l2[...]]
        mx = jnp.maximum(jnp.maximum(ls[0], ls[1]), ls[2])
        es = [jnp.exp(v - mx) for v in ls]
        tot = es[0] + es[1] + es[2]
        att = (es[0] / tot) * o0[...] + (es[1] / tot) * o1[...] + (es[2] / tot) * o2[...]
        a_ref[...] = att
        ab_ref[...] = att.astype(BF16)
        lse_ref[...] = mx + jnp.log(tot)

    row = pl.BlockSpec((tm, AW), lambda i: (i, 0))
    return pl.pallas_call(
        body, name=name, grid=(T // tm,), in_specs=[row] * 6, out_specs=[row] * 3,
        out_shape=[jax.ShapeDtypeStruct((T, AW), F32), jax.ShapeDtypeStruct((T, AW), BF16),
                   jax.ShapeDtypeStruct((T, AW), F32)],
        compiler_params=_cp(("parallel",)),
    )(outs[0], lses[0], outs[1], lses[1], outs[2], lses[2])


def _mix_out(z3b, attnb, gates, wc, wa_t, wo, x1, name):
    tm = 512

    def body(z_ref, a_ref, g_ref, wc_ref, wa_ref, wo_ref, x_ref, xo_ref, yc_ref, ya_ref, mx_ref):
        yc = _dot(z_ref[...], wc_ref[...], NN)
        ya = _dot(a_ref[...], wa_ref[...], NT)
        yc_ref[...] = yc
        ya_ref[...] = ya
        mixed = (_sig(g_ref[:, pl.ds(0, D)]) * yc + _sig(g_ref[:, pl.ds(D, D)]) * ya).astype(BF16)
        mx_ref[...] = mixed
        xo_ref[...] = x_ref[...] + _dot(mixed, wo_ref[...], NN)

    row = pl.BlockSpec((tm, D), lambda i: (i, 0))
    return pl.pallas_call(
        body, name=name, grid=(T // tm,),
        in_specs=[row, pl.BlockSpec((tm, AW), lambda i: (i, 0)), pl.BlockSpec((tm, 2 * D), lambda i: (i, 0)),
                  _const_spec((D, D)), _const_spec((D, AW)), _const_spec((D, D)), row],
        out_specs=[row, row, row, row],
        out_shape=[jax.ShapeDtypeStruct((T, D), F32), jax.ShapeDtypeStruct((T, D), F32),
                   jax.ShapeDtypeStruct((T, D), F32), jax.ShapeDtypeStruct((T, D), BF16)],
        compiler_params=_cp(("parallel",)),
    )(z3b, attnb, gates, wc, wa_t, wo, x1)


def _mix_out_bwd(dx2, gates, yc, ya, attn, wc, wa_t, wo, name):
    tm = 512

    def body(dx_ref, g_ref, yc_ref, ya_ref, at_ref, wc_ref, wa_ref, wo_ref,
             dg_ref, dyc_ref, dya_ref, dxb_ref, dz3_ref, dat_ref, dl_ref):
        dxb = dx_ref[...].astype(BF16)
        dxb_ref[...] = dxb
        dmix = _dot(dxb, wo_ref[...], NT)
        sc = _sig(g_ref[:, pl.ds(0, D)])
        sa = _sig(g_ref[:, pl.ds(D, D)])
        ycv, yav = yc_ref[...], ya_ref[...]
        dg_ref[:, pl.ds(0, D)] = (dmix * ycv * sc * (1.0 - sc)).astype(BF16)
        dg_ref[:, pl.ds(D, D)] = (dmix * yav * sa * (1.0 - sa)).astype(BF16)
        dyc = (dmix * sc).astype(BF16)
        dya = (dmix * sa).astype(BF16)
        dyc_ref[...] = dyc
        dya_ref[...] = dya
        dz3_ref[...] = _dot(dyc, wc_ref[...], NT)
        dat = _dot(dya, wa_ref[...], NN)
        dat_ref[...] = dat.astype(BF16)
        prod = dat * at_ref[...]
        for h in range(NHG):
            sl = pl.ds(h * HEAD, HEAD)
            dl_ref[:, sl] = jnp.broadcast_to(jnp.sum(prod[:, h * HEAD:(h + 1) * HEAD], axis=-1, keepdims=True),
                                             (tm, HEAD))

    row = pl.BlockSpec((tm, D), lambda i: (i, 0))
    row2 = pl.BlockSpec((tm, 2 * D), lambda i: (i, 0))
    rowa = pl.BlockSpec((tm, AW), lambda i: (i, 0))
    return pl.pallas_call(
        body, name=name, grid=(T // tm,),
        in_specs=[row, row2, row, row, rowa, _const_spec((D, D)), _const_spec((D, AW)), _const_spec((D, D))],
        out_specs=[row2, row, row, row, row, rowa, rowa],
        out_shape=[jax.ShapeDtypeStruct((T, 2 * D), BF16), jax.ShapeDtypeStruct((T, D), BF16),
                   jax.ShapeDtypeStruct((T, D), BF16), jax.ShapeDtypeStruct((T, D), BF16),
                   jax.ShapeDtypeStruct((T, D), F32), jax.ShapeDtypeStruct((T, AW), BF16),
                   jax.ShapeDtypeStruct((T, AW), F32)],
        compiler_params=_cp(("parallel",)),
    )(dx2, gates, yc, ya, attn, wc, wa_t, wo)


def _peer(k):
    x, y, c = lax.axis_index("x"), lax.axis_index("y"), lax.axis_index("c")
    px = 1 - x if k & 4 else x
    py = 1 - y if k & 2 else y
    pc = 1 - c if k & 1 else c
    return (px, py, pc), 4 * px + 2 * py + pc


def _all_gather(shards, name):
    nw = len(shards)

    def body(*refs):
        ins, outs = refs[:nw], refs[nw:2 * nw]
        send, recv, lsem = refs[2 * nw:]
        _, me = _peer(0)
        copies = []
        for w in range(nw):
            rows = shards[w].shape[0]
            mine = outs[w].at[pl.ds(me * rows, rows)]
            cp = pltpu.make_async_copy(ins[w], mine, lsem.at[w])
            cp.start()
            copies.append(cp)
            for k in range(1, NDEV):
                peer, _ = _peer(k)
                cp = pltpu.make_async_remote_copy(src_ref=ins[w], dst_ref=mine, send_sem=send.at[w, k - 1],
                                                  recv_sem=recv.at[w, k - 1], device_id=peer, device_id_type=MESH_ID)
                cp.start()
                copies.append(cp)
        for cp in copies:
            cp.wait()

    anyspec = pl.BlockSpec(memory_space=pl.ANY)
    return pl.pallas_call(
        body, name=name, in_specs=[anyspec] * nw, out_specs=[anyspec] * nw,
        out_shape=[jax.ShapeDtypeStruct((NDEV * s.shape[0],) + s.shape[1:], s.dtype) for s in shards],
        scratch_shapes=[pltpu.SemaphoreType.DMA((nw, NDEV - 1)), pltpu.SemaphoreType.DMA((nw, NDEV - 1)),
                        pltpu.SemaphoreType.DMA((nw,))],
        compiler_params=pltpu.CompilerParams(has_side_effects=True),
    )(*shards)


def _exchange(grads, vec, name):
    nw = len(grads)

    def body(*refs):
        ins, vin = refs[:nw], refs[nw]
        outs, vout = refs[nw + 1:2 * nw + 1], refs[2 * nw + 1]
        send, recv, lsem = refs[2 * nw + 2:]
        _, me = _peer(0)
        copies = []
        for w in range(nw + 1):
            if w < nw:
                rows = grads[w].shape[0] // NDEV
                cp = pltpu.make_async_copy(ins[w].at[pl.ds(me * rows, rows)], outs[w].at[0], lsem.at[w])
            else:
                cp = pltpu.make_async_copy(vin, vout.at[me], lsem.at[w])
            cp.start()
            copies.append(cp)
            for k in range(1, NDEV):
                peer, plin = _peer(k)
                if w < nw:
                    src, dst = ins[w].at[pl.ds(plin * rows, rows)], outs[w].at[k]
                else:
                    src, dst = vin, vout.at[me]
                cp = pltpu.make_async_remote_copy(src_ref=src, dst_ref=dst, send_sem=send.at[w, k - 1],
                                                  recv_sem=recv.at[w, k - 1], device_id=peer, device_id_type=MESH_ID)
                cp.start()
                copies.append(cp)
        for cp in copies:
            cp.wait()

    anyspec = pl.BlockSpec(memory_space=pl.ANY)
    out_shape = [jax.ShapeDtypeStruct((NDEV, g.shape[0] // NDEV) + g.shape[1:], g.dtype) for g in grads]
    out_shape.append(jax.ShapeDtypeStruct((NDEV,) + vec.shape, vec.dtype))
    return pl.pallas_call(
        body, name=name, in_specs=[anyspec] * (nw + 1), out_specs=[anyspec] * (nw + 1), out_shape=out_shape,
        scratch_shapes=[pltpu.SemaphoreType.DMA((nw + 1, NDEV - 1)), pltpu.SemaphoreType.DMA((nw + 1, NDEV - 1)),
                        pltpu.SemaphoreType.DMA((nw + 1,))],
        compiler_params=pltpu.CompilerParams(has_side_effects=True),
    )(*grads, vec)


def _gsum(land, name):
    _, rows, cols = land.shape
    tr = rows // 2 if rows * cols > 512 * 1024 and rows % 32 == 0 else rows

    def body(l_ref, o_ref):
        tot = l_ref[0].astype(F32)
        for s in range(1, NDEV):
            tot = tot + l_ref[s].astype(F32)
        o_ref[...] = tot

    return pl.pallas_call(
        body, name=name, grid=(rows // tr,),
        in_specs=[pl.BlockSpec((NDEV, tr, cols), lambda i: (0, i, 0))],
        out_specs=pl.BlockSpec((tr, cols), lambda i: (i, 0)),
        out_shape=jax.ShapeDtypeStruct((rows, cols), F32),
        compiler_params=_cp(("parallel",)),
    )(land)


def _adamw_math(w, g, m, v):
    m2 = B1 * m + (1.0 - B1) * g
    v2 = B2 * v + (1.0 - B2) * (g * g)
    m_hat = m2 / (1.0 - B1 ** STEP)
    v_hat = v2 / (1.0 - B2 ** STEP)
    delta = -LR * (m_hat / (jnp.sqrt(v_hat) + AEPS) + WD * w)
    return delta, m2, v2


def _adamw(w, g, m, v, name):
    rows, cols = w.shape
    tr = 256 if rows % 256 == 0 and rows > 256 else rows

    def body(w_ref, g_ref, m_ref, v_ref, d_ref, mo_ref, vo_ref):
        d, m2, v2 = _adamw_math(w_ref[...], g_ref[...], m_ref[...], v_ref[...])
        d_ref[...] = d
        mo_ref[...] = m2
        vo_ref[...] = v2

    blk = pl.BlockSpec((tr, cols), lambda i: (i, 0))
    return pl.pallas_call(
        body, name=name, grid=(rows // tr,), in_specs=[blk] * 4, out_specs=[blk] * 3,
        out_shape=[jax.ShapeDtypeStruct((rows, cols), F32)] * 3,
        compiler_params=_cp(("parallel",)),
    )(w, g, m, v)


def _small_update(vland, w8, m8, v8, name):
    def body(l_ref, w_ref, m_ref, v_ref, g_ref, d_ref, mo_ref, vo_ref):
        g = l_ref[0]
        for s in range(1, NDEV):
            g = g + l_ref[s]
        g_ref[...] = g
        d, m2, v2 = _adamw_math(w_ref[...], g, m_ref[...], v_ref[...])
        d_ref[...] = d
        mo_ref[...] = m2
        vo_ref[...] = v2

    return pl.pallas_call(
        body, name=name, out_shape=[jax.ShapeDtypeStruct((8, D), F32)] * 4,
        compiler_params=_cp(None),
    )(vland, w8, m8, v8)


def _perm(a, dil):
    if dil == 1:
        return a
    t, n = a.shape
    return a.reshape(t // dil, dil, n).transpose(1, 0, 2).reshape(t, n)


def _unperm(a, dil):
    if dil == 1:
        return a
    t, n = a.shape
    return a.reshape(dil, t // dil, n).transpose(1, 0, 2).reshape(t, n)


def kernel(x, ffn1_norm, ffn1_w_gate, ffn1_w_up, ffn1_w_down, mix_norm, w_in, conv_dw_kernel, conv_dw_bias, conv_ln_gain, conv_ln_bias, conv_w_out, attn_w_out, w_o, ffn2_norm, ffn2_w_gate, ffn2_w_up, ffn2_w_down, final_norm, loss_target, m_ffn1_norm, m_ffn1_w_gate, m_ffn1_w_up, m_ffn1_w_down, m_mix_norm, m_w_in, m_conv_dw_kernel, m_conv_dw_bias, m_conv_ln_gain, m_conv_ln_bias, m_conv_w_out, m_attn_w_out, m_w_o, m_ffn2_norm, m_ffn2_w_gate, m_ffn2_w_up, m_ffn2_w_down, m_final_norm, v_ffn1_norm, v_ffn1_w_gate, v_ffn1_w_up, v_ffn1_w_down, v_mix_norm, v_w_in, v_conv_dw_kernel, v_conv_dw_bias, v_conv_ln_gain, v_conv_ln_bias, v_conv_w_out, v_attn_w_out, v_w_o, v_ffn2_norm, v_ffn2_w_gate, v_ffn2_w_up, v_ffn2_w_down, v_final_norm):
    names = ["ffn1_norm", "ffn1_w_gate", "ffn1_w_up", "ffn1_w_down", "mix_norm", "w_in", "conv_dw_kernel",
             "conv_dw_bias", "conv_ln_gain", "conv_ln_bias", "conv_w_out", "attn_w_out", "w_o", "ffn2_norm",
             "ffn2_w_gate", "ffn2_w_up", "ffn2_w_down", "final_norm"]
    w = dict(ffn1_norm=ffn1_norm, ffn1_w_gate=ffn1_w_gate, ffn1_w_up=ffn1_w_up, ffn1_w_down=ffn1_w_down, mix_norm=mix_norm, w_in=w_in, conv_dw_kernel=conv_dw_kernel, conv_dw_bias=conv_dw_bias, conv_ln_gain=conv_ln_gain, conv_ln_bias=conv_ln_bias, conv_w_out=conv_w_out, attn_w_out=attn_w_out, w_o=w_o, ffn2_norm=ffn2_norm, ffn2_w_gate=ffn2_w_gate, ffn2_w_up=ffn2_w_up, ffn2_w_down=ffn2_w_down, final_norm=final_norm)
    mo = dict(ffn1_norm=m_ffn1_norm, ffn1_w_gate=m_ffn1_w_gate, ffn1_w_up=m_ffn1_w_up, ffn1_w_down=m_ffn1_w_down, mix_norm=m_mix_norm, w_in=m_w_in, conv_dw_kernel=m_conv_dw_kernel, conv_dw_bias=m_conv_dw_bias, conv_ln_gain=m_conv_ln_gain, conv_ln_bias=m_conv_ln_bias, conv_w_out=m_conv_w_out, attn_w_out=m_attn_w_out, w_o=m_w_o, ffn2_norm=m_ffn2_norm, ffn2_w_gate=m_ffn2_w_gate, ffn2_w_up=m_ffn2_w_up, ffn2_w_down=m_ffn2_w_down, final_norm=m_final_norm)
    vo = dict(ffn1_norm=v_ffn1_norm, ffn1_w_gate=v_ffn1_w_gate, ffn1_w_up=v_ffn1_w_up, ffn1_w_down=v_ffn1_w_down, mix_norm=v_mix_norm, w_in=v_w_in, conv_dw_kernel=v_conv_dw_kernel, conv_dw_bias=v_conv_dw_bias, conv_ln_gain=v_conv_ln_gain, conv_ln_bias=v_conv_ln_bias, conv_w_out=v_conv_w_out, attn_w_out=v_attn_w_out, w_o=v_w_o, ffn2_norm=v_ffn2_norm, ffn2_w_gate=v_ffn2_w_gate, ffn2_w_up=v_ffn2_w_up, ffn2_w_down=v_ffn2_w_down, final_norm=v_final_norm)
    col_sharded = ("ffn1_w_gate", "ffn1_w_up", "w_in", "attn_w_out", "ffn2_w_gate", "ffn2_w_up")
    row_sharded = ("ffn1_w_down", "conv_w_out", "w_o", "ffn2_w_down")
    small = ("ffn1_norm", "mix_norm", "ffn2_norm", "final_norm", "conv_dw_bias", "conv_ln_gain", "conv_ln_bias")

    shards = [jnp.transpose(w[n][0]).astype(BF16) for n in col_sharded]
    shards += [w[n][0].astype(BF16) for n in row_sharded]
    kshard = jnp.pad(conv_dw_kernel[0], ((0, 1), (0, 0)))
    gathered = _all_gather(shards + [kshard], "all_gather_weights")
    full = dict(zip(col_sharded + row_sharded, gathered[:-1]))
    kern = gathered[-1].reshape(NDEV, 32, D // NDEV).transpose(1, 0, 2).reshape(32, D)
    wg1, wu1, win_t, wa_t, wg2, wu2 = (full[n] for n in col_sharded)
    wd1, wc, wo, wd2 = (full[n] for n in row_sharded)

    x0 = x[0]
    tgt = loss_target[0]
    gf = final_norm.reshape(1, D)

    x1, gg1, uu1 = _ffn_fwd(x0, ffn1_norm, wg1, wu1, wd1, "ffn1_fwd")
    h2 = _norm_cast(x1, mix_norm, "mix_norm_fwd")
    ab = _mm(h2, win_t, mode="nt", m=T, n=2 * D, k=D, tm=512, tn=512, tk=D, out_dtype=F32, name="proj_conv")
    gates = _mm(h2, win_t, mode="nt", m=T, n=2 * D, k=D, tm=512, tn=512, tk=D, out_dtype=F32,
                b_map=lambda i, j, kk: (13 + j, 0), name="proj_gates")
    h2p, qkv = [], []
    for gi, (_, dil) in enumerate(GROUPS):
        hp = _perm(h2, dil)
        h2p.append(hp)
        qkv.append(_mm(hp, win_t, mode="nt", m=T, n=3 * AW, k=D, tm=512, tn=AW, tk=D, out_dtype=BF16,
                       b_map=lambda i, j, kk, gi=gi: (4 + gi + 3 * j, 0), name=f"proj_qkv{gi}"))
    z1, z3b = _conv_fwd(ab, kern, conv_dw_bias, conv_ln_gain, conv_ln_bias, "conv_fwd")
    outs, lses = [], []
    for gi, (_, dil) in enumerate(GROUPS):
        o, l = _attn_fwd(qkv[gi], gi, f"attn_fwd{gi}")
        outs.append(_unperm(o, dil))
        lses.append(_unperm(l, dil))
    attn, attnb, lse = _merge(outs, lses, "attn_merge")
    x2, yc, ya, mixedb = _mix_out(z3b, attnb, gates, wc, wa_t, wo, x1, "mix_out_fwd")
    x3, gg2, uu2 = _ffn_fwd(x2, ffn2_norm, wg2, wu2, wd2, "ffn2_fwd")

    dx3, dgf, loss_part = _final(x3, gf, tgt, "final_norm_loss")
    dx2, dg3, dgb, dub, actb, hb, dob = _ffn_bwd(x2, ffn2_norm, gg2, uu2, dx3, wg2, wu2, wd2, "ffn2_bwd")
    grads = {}
    grads["ffn2_w_gate"] = _wgrad(dgb, hb, FF, D, "ffn2_dwg")
    grads["ffn2_w_up"] = _wgrad(dub, hb, FF, D, "ffn2_dwu")
    grads["ffn2_w_down"] = _wgrad(actb, dob, FF, D, "ffn2_dwd")

    dgates, dycb, dyab, dx2b, dz3, dattnb, delta = _mix_out_bwd(dx2, gates, yc, ya, attn, wc, wa_t, wo, "mix_out_bwd")
    grads["w_o"] = _wgrad(mixedb, dx2b, D, D, "dw_o")
    grads["conv_w_out"] = _wgrad(z3b, dycb, D, D, "dw_conv_out")
    grads["attn_w_out"] = _wgrad(dyab, attnb, D, AW, "dw_attn_out")
    dab, dkern, dvec = _conv_bwd(dz3, z1, ab, kern, conv_ln_gain, conv_ln_bias, "conv_bwd")

    dqkv = []
    for gi, (_, dil) in enumerate(GROUPS):
        dq3 = _attn_bwd(qkv[gi], _perm(dattnb, dil), _perm(lse, dil), _perm(delta, dil), gi, f"attn_bwd{gi}")
        dqkv.append(dq3.reshape(3 * T, AW))

    nrow = T // 512
    dh = _mm(dab, win_t, mode="nn", m=T, n=D, k=2 * D, tm=512, tn=D, tk=512, out_dtype=F32, name="dproj_conv")
    dh = _mm(dgates, win_t, mode="nn", m=T, n=D, k=2 * D, tm=512, tn=D, tk=512, out_dtype=F32,
             b_map=lambda i, j, kk: (13 + kk, 0), init=dh, name="dproj_gates")
    dhs = []
    for gi, (_, dil) in enumerate(GROUPS):
        part = _mm(dqkv[gi], win_t, mode="nn", m=T, n=D, k=3 * AW, tm=512, tn=D, tk=AW, out_dtype=F32,
                   a_map=lambda i, j, kk: (kk * nrow + i, 0), b_map=lambda i, j, kk, gi=gi: (4 + gi + 3 * kk, 0),
                   init=dh if gi == 0 else None, name=f"dproj_qkv{gi}")
        dhs.append(_unperm(part, dil))
    dx1, dg2 = _rms_bwd(x1, mix_norm, dhs, dx2, "mix_norm_bwd")

    dwin = _mm(dab, h2, mode="tn", m=2 * D, n=D, k=T, tm=2 * D, tn=D, tk=512, out_dtype=BF16, out_rows=IN_W,
               name="dw_in_conv")
    dwin = _mm(dgates, h2, mode="tn", m=2 * D, n=D, k=T, tm=512, tn=D, tk=512, out_dtype=BF16, out_rows=IN_W,
               o_map=lambda i, j, kk: (13 + i, 0), passthru=dwin, name="dw_in_gates")
    for gi in range(3):
        dwin = _mm(dqkv[gi], h2p[gi], mode="tn", m=3 * AW, n=D, k=T, tm=AW, tn=D, tk=512, out_dtype=BF16,
                   out_rows=IN_W, a_map=lambda i, j, kk: (i * (T // 512) + kk, 0),
                   o_map=lambda i, j, kk, gi=gi: (4 + gi + 3 * i, 0), passthru=dwin, name=f"dw_in_qkv{gi}")
    grads["w_in"] = dwin

    dx0, dg1, dgb, dub, actb, hb, dob = _ffn_bwd(x0, ffn1_norm, gg1, uu1, dx1, wg1, wu1, wd1, "ffn1_bwd")
    grads["ffn1_w_gate"] = _wgrad(dgb, hb, FF, D, "ffn1_dwg")
    grads["ffn1_w_up"] = _wgrad(dub, hb, FF, D, "ffn1_dwu")
    grads["ffn1_w_down"] = _wgrad(actb, dob, FF, D, "ffn1_dwd")

    dkern_blocks = dkern.reshape(32, NDEV, D // NDEV).transpose(1, 0, 2).reshape(NDEV * 32, D // NDEV)
    vec = jnp.concatenate([dg1, dg2, dg3, dgf, dvec[0:3], jnp.broadcast_to(loss_part[:, :1], (1, D))], axis=0)
    big = col_sharded + row_sharded
    landed = _exchange([grads[n] for n in big] + [dkern_blocks], vec, "exchange_grads")
    vland = landed[-1]

    g_out, d_out, m_out, v_out = {}, {}, {}, {}
    for n, land in zip(big + ("conv_dw_kernel",), landed[:-1]):
        g = _gsum(land, f"gsum_{n}")
        if n in col_sharded:
            g = jnp.transpose(g)
        elif n == "conv_dw_kernel":
            g = g[:CONV_W]
        d, m2, v2 = _adamw(w[n][0], g, mo[n][0], vo[n][0], f"adamw_{n}")
        g_out[n], d_out[n], m_out[n], v_out[n] = g[None], d[None], m2[None], v2[None]

    def rows8(src):
        return jnp.concatenate([src[n].reshape(1, D) for n in small] + [jnp.ones((1, D), F32)], axis=0)

    g8, d8, m8, v8 = _small_update(vland, rows8(w), rows8(mo), rows8(vo), "small_update")
    for r, n in enumerate(small):
        shp = w[n].shape
        g_out[n], d_out[n], m_out[n], v_out[n] = (a[r].reshape(shp) for a in (g8, d8, m8, v8))
    loss = g8[7, 0]

    return (loss, dx0[None], *[g_out[n] for n in names], *[d_out[n] for n in names],
            *[m_out[n] for n in names], *[v_out[n] for n in names])
```

```python
import numpy as np
import jax
import jax.numpy as jnp
from jax import lax
from jax.experimental import pallas as pl
from jax.experimental.pallas import tpu as pltpu

F32 = jnp.float32
BF16 = jnp.bfloat16

T = 4096
D = 1024
FF = 2816
NDEV = 8
CONV_W = 31
HEAD = 128
BLK = 128
GROUPS = ((128, 1), (512, 4), (2048, 16))
NHG = 4
AW = NHG * HEAD
IN_W = 2 * D + 3 * 3 * AW + 2 * D
EPS = 1e-6
B1, B2, LR, AEPS, WD, STEP = 0.9, 0.999, 0.001, 1e-08, 0.01, 10
NEG = -1e30
VMEM_LIMIT = 56 * 1024 * 1024
MESH_ID = pl.DeviceIdType.MESH

NT = (((1,), (1,)), ((), ()))
NN = (((1,), (0,)), ((), ()))
TN = (((0,), (0,)), ((), ()))
_DIMS = {"nn": NN, "nt": NT, "tn": TN}


def _cp(sem=None):
    return pltpu.CompilerParams(dimension_semantics=sem, vmem_limit_bytes=VMEM_LIMIT)


def _sig(v):
    return 1.0 / (1.0 + jnp.exp(-v))


def _dot(a, b, dims):
    return lax.dot_general(a, b, dims, preferred_element_type=F32)


def _const_spec(shape):
    nd = len(shape)
    return pl.BlockSpec(shape, lambda *_: (0,) * nd)


def _mm(a, b, *, mode, m, n, k, tm, tn, tk, out_dtype, name, a_map=None, b_map=None,
        o_map=None, out_rows=None, init=None, passthru=None):
    gi, gj, gk = m // tm, n // tn, k // tk
    assert gi * tm == m and gj * tn == n and gk * tk == k, (name, m, n, k, tm, tn, tk)
    if mode == "nn":
        a_blk, b_blk = (tm, tk), (tk, tn)
        da, db = (lambda i, j, kk: (i, kk)), (lambda i, j, kk: (kk, j))
    elif mode == "nt":
        a_blk, b_blk = (tm, tk), (tn, tk)
        da, db = (lambda i, j, kk: (i, kk)), (lambda i, j, kk: (j, kk))
    else:
        a_blk, b_blk = (tk, tm), (tk, tn)
        da, db = (lambda i, j, kk: (kk, i)), (lambda i, j, kk: (kk, j))
    a_map = a_map or da
    b_map = b_map or db
    o_map = o_map or (lambda i, j, kk: (i, j))
    dims = _DIMS[mode]
    extra = init if init is not None else passthru
    out_rows = out_rows or m

    def body(*refs):
        if init is not None:
            a_ref, b_ref, i_ref, o_ref, acc = refs
        elif passthru is not None:
            a_ref, b_ref, _, o_ref, acc = refs
        else:
            a_ref, b_ref, o_ref, acc = refs
        kk = pl.program_id(2)

        @pl.when(kk == 0)
        def _():
            if init is not None:
                acc[...] = i_ref[...].astype(F32)
            else:
                acc[...] = jnp.zeros_like(acc)

        acc[...] += _dot(a_ref[...], b_ref[...], dims)

        @pl.when(kk == gk - 1)
        def _():
            o_ref[...] = acc[...].astype(out_dtype)

    in_specs = [pl.BlockSpec(a_blk, a_map), pl.BlockSpec(b_blk, b_map)]
    args = [a, b]
    aliases = {}
    if init is not None:
        in_specs.append(pl.BlockSpec((tm, tn), o_map))
        args.append(init)
        aliases = {2: 0}
    elif passthru is not None:
        in_specs.append(pl.BlockSpec(memory_space=pl.ANY))
        args.append(passthru)
        aliases = {2: 0}
    out_dt = extra.dtype if extra is not None else out_dtype
    assert out_dt == out_dtype
    return pl.pallas_call(
        body, name=name, grid=(gi, gj, gk),
        in_specs=in_specs, out_specs=pl.BlockSpec((tm, tn), o_map),
        out_shape=jax.ShapeDtypeStruct((out_rows, n), out_dtype),
        scratch_shapes=[pltpu.VMEM((tm, tn), F32)],
        input_output_aliases=aliases,
        compiler_params=_cp(("parallel", "parallel", "arbitrary")),
    )(*args)


def _ffn_fwd(x, g, wg_t, wu_t, wd, name):
    tm, fc = 512, 256
    nc = FF // fc

    def body(x_ref, g_ref, wg_ref, wu_ref, wd_ref, xo_ref, gg_ref, uu_ref, acc):
        xv = x_ref[...]
        r = lax.rsqrt(jnp.mean(xv * xv, axis=-1, keepdims=True) + EPS)
        h = (xv * r * g_ref[...]).astype(BF16)
        acc[...] = jnp.zeros_like(acc)
        for c in range(nc):
            sl = pl.ds(c * fc, fc)
            gg = _dot(h, wg_ref[sl, :], NT)
            uu = _dot(h, wu_ref[sl, :], NT)
            gg_ref[:, sl] = gg.astype(BF16)
            uu_ref[:, sl] = uu.astype(BF16)
            act = (gg * _sig(gg) * uu).astype(BF16)
            acc[...] += _dot(act, wd_ref[sl, :], NN)
        xo_ref[...] = xv + 0.5 * acc[...]

    wspec = pl.BlockSpec((FF, D), lambda i: (0, 0), pipeline_mode=pl.Buffered(1))
    return pl.pallas_call(
        body, name=name, grid=(T // tm,),
        in_specs=[pl.BlockSpec((tm, D), lambda i: (i, 0)), _const_spec((1, D)), wspec, wspec, wspec],
        out_specs=[pl.BlockSpec((tm, D), lambda i: (i, 0)), pl.BlockSpec((tm, FF), lambda i: (i, 0)),
                   pl.BlockSpec((tm, FF), lambda i: (i, 0))],
        out_shape=[jax.ShapeDtypeStruct((T, D), F32), jax.ShapeDtypeStruct((T, FF), BF16),
                   jax.ShapeDtypeStruct((T, FF), BF16)],
        scratch_shapes=[pltpu.VMEM((tm, D), F32)],
        compiler_params=_cp(("parallel",)),
    )(x, g, wg_t, wu_t, wd)


def _ffn_bwd(x, g, gg_all, uu_all, dout, wg_t, wu_t, wd, name):
    tm, fc = 256, 256
    nc = FF // fc

    def body(x_ref, g_ref, gg_ref, uu_ref, do_ref, wg_ref, wu_ref, wd_ref,
             dx_ref, dgam_ref, dg_ref, du_ref, act_ref, h_ref, db_ref, acc):
        i = pl.program_id(0)
        xv = x_ref[...]
        r = lax.rsqrt(jnp.mean(xv * xv, axis=-1, keepdims=True) + EPS)
        xhat = xv * r
        gam = g_ref[...]
        h_ref[...] = (xhat * gam).astype(BF16)
        dov = do_ref[...]
        dbv = (0.5 * dov).astype(BF16)
        db_ref[...] = dbv
        acc[...] = jnp.zeros_like(acc)
        for c in range(nc):
            sl = pl.ds(c * fc, fc)
            da = _dot(dbv, wd_ref[sl, :], NT)
            gg = gg_ref[:, sl].astype(F32)
            uu = uu_ref[:, sl].astype(F32)
            s = _sig(gg)
            si = gg * s
            dgv = (da * uu * (s * (1.0 + gg * (1.0 - s)))).astype(BF16)
            duv = (da * si).astype(BF16)
            dg_ref[:, sl] = dgv
            du_ref[:, sl] = duv
            act_ref[:, sl] = (si * uu).astype(BF16)
            acc[...] += _dot(dgv, wg_ref[sl, :], NN) + _dot(duv, wu_ref[sl, :], NN)
        dh = acc[...]

        @pl.when(i == 0)
        def _():
            dgam_ref[...] = jnp.zeros_like(dgam_ref)

        dgam_ref[...] += jnp.sum(dh * xhat, axis=0, keepdims=True)
        dxh = dh * gam
        dx_ref[...] = dov + r * (dxh - xhat * jnp.mean(dxh * xhat, axis=-1, keepdims=True))

    wspec = pl.BlockSpec((FF, D), lambda i: (0, 0), pipeline_mode=pl.Buffered(1))
    row_d = pl.BlockSpec((tm, D), lambda i: (i, 0))
    row_f = pl.BlockSpec((tm, FF), lambda i: (i, 0))
    return pl.pallas_call(
        body, name=name, grid=(T // tm,),
        in_specs=[row_d, _const_spec((1, D)), row_f, row_f, row_d, wspec, wspec, wspec],
        out_specs=[row_d, _const_spec((1, D)), row_f, row_f, row_f, row_d, row_d],
        out_shape=[jax.ShapeDtypeStruct((T, D), F32), jax.ShapeDtypeStruct((1, D), F32),
                   jax.ShapeDtypeStruct((T, FF), BF16), jax.ShapeDtypeStruct((T, FF), BF16),
                   jax.ShapeDtypeStruct((T, FF), BF16), jax.ShapeDtypeStruct((T, D), BF16),
                   jax.ShapeDtypeStruct((T, D), BF16)],
        scratch_shapes=[pltpu.VMEM((tm, D), F32)],
        compiler_params=_cp(("arbitrary",)),
    )(x, g, gg_all, uu_all, dout, wg_t, wu_t, wd)


def _wgrad(a, b, m, n, name, tk=512):
    return _mm(a, b, mode="tn", m=m, n=n, k=T, tm=m, tn=n, tk=tk, out_dtype=BF16, name=name)


def _norm_cast(x, g, name):
    tm = 512

    def body(x_ref, g_ref, h_ref):
        xv = x_ref[...]
        r = lax.rsqrt(jnp.mean(xv * xv, axis=-1, keepdims=True) + EPS)
        h_ref[...] = (xv * r * g_ref[...]).astype(BF16)

    return pl.pallas_call(
        body, name=name, grid=(T // tm,),
        in_specs=[pl.BlockSpec((tm, D), lambda i: (i, 0)), _const_spec((1, D))],
        out_specs=pl.BlockSpec((tm, D), lambda i: (i, 0)),
        out_shape=jax.ShapeDtypeStruct((T, D), BF16),
        compiler_params=_cp(("parallel",)),
    )(x, g)


def _final(x3, gf, tgt, name):
    tm = 512

    def body(x_ref, g_ref, t_ref, dx_ref, dgam_ref, loss_ref):
        i = pl.program_id(0)
        xv = x_ref[...]
        r = lax.rsqrt(jnp.mean(xv * xv, axis=-1, keepdims=True) + EPS)
        xhat = xv * r
        gam = g_ref[...]
        err = xhat * gam - t_ref[...]
        part = 0.5 * jnp.sum(jnp.mean(err * err, axis=-1, keepdims=True), axis=0, keepdims=True)
        dy = err * (1.0 / D)

        @pl.when(i == 0)
        def _():
            dgam_ref[...] = jnp.zeros_like(dgam_ref)
            loss_ref[...] = jnp.zeros_like(loss_ref)

        dgam_ref[...] += jnp.sum(dy * xhat, axis=0, keepdims=True)
        loss_ref[...] += jnp.broadcast_to(part, loss_ref.shape)
        dxh = dy * gam
        dx_ref[...] = r * (dxh - xhat * jnp.mean(dxh * xhat, axis=-1, keepdims=True))

    row_d = pl.BlockSpec((tm, D), lambda i: (i, 0))
    return pl.pallas_call(
        body, name=name, grid=(T // tm,),
        in_specs=[row_d, _const_spec((1, D)), row_d],
        out_specs=[row_d, _const_spec((1, D)), _const_spec((1, 128))],
        out_shape=[jax.ShapeDtypeStruct((T, D), F32), jax.ShapeDtypeStruct((1, D), F32),
                   jax.ShapeDtypeStruct((1, 128), F32)],
        compiler_params=_cp(("arbitrary",)),
    )(x3, gf, tgt)


def _rms_bwd(x, g, dhs, dres, name):
    tm = 512
    nh = len(dhs)

    def body(*refs):
        x_ref, g_ref = refs[:2]
        dh_refs = refs[2:2 + nh]
        dr_ref, dx_ref, dgam_ref = refs[2 + nh:]
        i = pl.program_id(0)
        xv = x_ref[...]
        r = lax.rsqrt(jnp.mean(xv * xv, axis=-1, keepdims=True) + EPS)
        xhat = xv * r
        gam = g_ref[...]
        dh = dh_refs[0][...]
        for ref in dh_refs[1:]:
            dh = dh + ref[...]

        @pl.when(i == 0)
        def _():
            dgam_ref[...] = jnp.zeros_like(dgam_ref)

        dgam_ref[...] += jnp.sum(dh * xhat, axis=0, keepdims=True)
        dxh = dh * gam
        dx_ref[...] = dr_ref[...] + r * (dxh - xhat * jnp.mean(dxh * xhat, axis=-1, keepdims=True))

    row_d = pl.BlockSpec((tm, D), lambda i: (i, 0))
    return pl.pallas_call(
        body, name=name, grid=(T // tm,),
        in_specs=[row_d, _const_spec((1, D))] + [row_d] * nh + [row_d],
        out_specs=[row_d, _const_spec((1, D))],
        out_shape=[jax.ShapeDtypeStruct((T, D), F32), jax.ShapeDtypeStruct((1, D), F32)],
        compiler_params=_cp(("arbitrary",)),
    )(x, g, *dhs, dres)


CONV_TM = 256
CONV_HALO = 32
CONV_RB = 32


def _glu(ab):
    return ab[:, :D] * _sig(ab[:, D:])


def _ln_stats(z1):
    mu = jnp.mean(z1, axis=-1, keepdims=True)
    zc = z1 - mu
    rstd = lax.rsqrt(jnp.mean(zc * zc, axis=-1, keepdims=True) + EPS)
    return zc * rstd, rstd


def _conv_fwd(ab, kern, dwb, lng, lnb, name):
    tm, hl, rb = CONV_TM, CONV_HALO, CONV_RB
    off = hl - (CONV_W - 1)

    def body(ab_ref, abh_ref, k_ref, dwb_ref, lng_ref, lnb_ref, z1_ref, z3_ref, zbuf):
        i = pl.program_id(0)
        zbuf[pl.ds(0, hl), :] = jnp.where(i > 0, _glu(abh_ref[...]), 0.0)
        zbuf[pl.ds(hl, tm), :] = _glu(ab_ref[...])
        for b in range(tm // rb):
            acc = jnp.zeros((rb, D), F32)
            for j in range(CONV_W):
                acc = acc + zbuf[pl.ds(b * rb + off + j, rb), :] * k_ref[pl.ds(j, 1), :]
            z1 = acc + dwb_ref[...]
            z1_ref[pl.ds(b * rb, rb), :] = z1
            zn, _ = _ln_stats(z1)
            z2 = zn * lng_ref[...] + lnb_ref[...]
            z3_ref[pl.ds(b * rb, rb), :] = (z2 * _sig(z2)).astype(BF16)

    row = pl.BlockSpec((tm, D), lambda i: (i, 0))
    return pl.pallas_call(
        body, name=name, grid=(T // tm,),
        in_specs=[pl.BlockSpec((tm, 2 * D), lambda i: (i, 0)),
                  pl.BlockSpec((hl, 2 * D), lambda i: (jnp.maximum(i * (tm // hl) - 1, 0), 0)),
                  _const_spec((32, D)), _const_spec((1, D)), _const_spec((1, D)), _const_spec((1, D))],
        out_specs=[row, row],
        out_shape=[jax.ShapeDtypeStruct((T, D), F32), jax.ShapeDtypeStruct((T, D), BF16)],
        scratch_shapes=[pltpu.VMEM((hl + tm, D), F32)],
        compiler_params=_cp(("parallel",)),
    )(ab, ab, kern, dwb, lng, lnb)


def _conv_bwd(dz3, z1, ab, kern, lng, lnb, name):
    tm, hl, rb = CONV_TM, CONV_HALO, CONV_RB
    off = hl - (CONV_W - 1)
    nsteps = T // tm

    def ln_bwd(dz3v, z1v, lngv, lnbv):
        zn, rstd = _ln_stats(z1v)
        z2 = zn * lngv + lnbv
        s = _sig(z2)
        dz2 = dz3v * (s * (1.0 + z2 * (1.0 - s)))
        dzn = dz2 * lngv
        dz1 = rstd * (dzn - jnp.mean(dzn, axis=-1, keepdims=True)
                      - zn * jnp.mean(dzn * zn, axis=-1, keepdims=True))
        return dz1, dz2, zn

    def body(dz3_ref, dz3h_ref, z1_ref, z1h_ref, ab_ref, abh_ref, k_ref, lng_ref, lnb_ref,
             dab_ref, dk_ref, dvec_ref, zbuf, dzbuf):
        i = pl.program_id(0)
        lngv, lnbv = lng_ref[...], lnb_ref[...]

        @pl.when(i == 0)
        def _():
            dk_ref[...] = jnp.zeros_like(dk_ref)
            dvec_ref[...] = jnp.zeros_like(dvec_ref)

        dz1, dz2, zn = ln_bwd(dz3_ref[...], z1_ref[...], lngv, lnbv)
        dvec_ref[pl.ds(0, 1), :] += jnp.sum(dz1, axis=0, keepdims=True)
        dvec_ref[pl.ds(1, 1), :] += jnp.sum(dz2 * zn, axis=0, keepdims=True)
        dvec_ref[pl.ds(2, 1), :] += jnp.sum(dz2, axis=0, keepdims=True)
        dzbuf[pl.ds(0, tm), :] = dz1
        dz1h, _, _ = ln_bwd(dz3h_ref[...], z1h_ref[...], lngv, lnbv)
        dzbuf[pl.ds(tm, hl), :] = jnp.where(i < nsteps - 1, dz1h, 0.0)
        zbuf[pl.ds(0, hl), :] = jnp.where(i > 0, _glu(abh_ref[...]), 0.0)
        zbuf[pl.ds(hl, tm), :] = _glu(ab_ref[...])

        for j in range(CONV_W):
            tot = jnp.zeros((rb, D), F32)
            for b in range(tm // rb):
                tot = tot + dzbuf[pl.ds(b * rb, rb), :] * zbuf[pl.ds(b * rb + off + j, rb), :]
            dk_ref[pl.ds(j, 1), :] += jnp.sum(tot, axis=0, keepdims=True)

        for b in range(tm // rb):
            acc = jnp.zeros((rb, D), F32)
            for j in range(CONV_W):
                acc = acc + dzbuf[pl.ds(b * rb + (CONV_W - 1) - j, rb), :] * k_ref[pl.ds(j, 1), :]
            av = ab_ref[pl.ds(b * rb, rb), pl.ds(0, D)]
            sb = _sig(ab_ref[pl.ds(b * rb, rb), pl.ds(D, D)])
            dab_ref[pl.ds(b * rb, rb), pl.ds(0, D)] = (acc * sb).astype(BF16)
            dab_ref[pl.ds(b * rb, rb), pl.ds(D, D)] = (acc * av * sb * (1.0 - sb)).astype(BF16)

    row = pl.BlockSpec((tm, D), lambda i: (i, 0))
    nxt = pl.BlockSpec((hl, D), lambda i: (jnp.minimum((i + 1) * (tm // hl), T // hl - 1), 0))
    return pl.pallas_call(
        body, name=name, grid=(nsteps,),
        in_specs=[row, nxt, row, nxt,
                  pl.BlockSpec((tm, 2 * D), lambda i: (i, 0)),
                  pl.BlockSpec((hl, 2 * D), lambda i: (jnp.maximum(i * (tm // hl) - 1, 0), 0)),
                  _const_spec((32, D)), _const_spec((1, D)), _const_spec((1, D))],
        out_specs=[pl.BlockSpec((tm, 2 * D), lambda i: (i, 0)), _const_spec((32, D)), _const_spec((8, D))],
        out_shape=[jax.ShapeDtypeStruct((T, 2 * D), BF16), jax.ShapeDtypeStruct((32, D), F32),
                   jax.ShapeDtypeStruct((8, D), F32)],
        scratch_shapes=[pltpu.VMEM((hl + tm, D), F32), pltpu.VMEM((tm + hl, D), F32)],
        compiler_params=_cp(("arbitrary",)),
    )(dz3, dz3, z1, z1, ab, ab, kern, lng, lnb)


def _alibi_slopes():
    h = np.arange(1, 3 * NHG + 1, dtype=np.float32)
    return np.power(np.float32(2.0), -8.0 * h / np.float32(3 * NHG)).astype(np.float32)


def _band_bias(gi):
    _, dil = GROUPS[gi]
    slopes = _alibi_slopes()[gi * NHG:(gi + 1) * NHG]
    qi = np.arange(BLK)[:, None]
    ki = np.arange(2 * BLK)[None, :]
    steps = BLK + qi - ki
    band = (steps >= 0) & (steps <= BLK)
    bias = -slopes[:, None, None] * (dil * steps).astype(np.float32)[None]
    return jnp.asarray(np.where(band[None], bias, np.float32(NEG)).astype(np.float32))


QB = 4


def _attn_specs():
    prev = lambda n: jnp.maximum(n * QB - 1, 0)
    return [pl.BlockSpec((QB * BLK, HEAD), lambda h, n: (n, h)),
            pl.BlockSpec((BLK, HEAD), lambda h, n: (prev(n), NHG + h)),
            pl.BlockSpec((QB * BLK, HEAD), lambda h, n: (n, NHG + h)),
            pl.BlockSpec((BLK, HEAD), lambda h, n: (prev(n), 2 * NHG + h)),
            pl.BlockSpec((QB * BLK, HEAD), lambda h, n: (n, 2 * NHG + h)),
            pl.BlockSpec((None, BLK, 2 * BLK), lambda h, n: (h, 0, 0))]


def _scores(q, kcat, bias, blk, seg):
    s = _dot(q, kcat, NT) * (HEAD ** -0.5) + bias
    col = lax.broadcasted_iota(jnp.int32, s.shape, 1)
    first = (blk % seg) == 0
    return jnp.where(jnp.logical_and(first, col < BLK), NEG, s)


def _attn_fwd(qkv, gi, name):
    seg = (T // GROUPS[gi][1]) // BLK

    def body(q_ref, kp_ref, kc_ref, vp_ref, vc_ref, bias_ref, o_ref, l_ref):
        n = pl.program_id(1)
        kwin = jnp.concatenate([kp_ref[...], kc_ref[...]], axis=0)
        vwin = jnp.concatenate([vp_ref[...], vc_ref[...]], axis=0)
        bias = bias_ref[...]
        for b in range(QB):
            rows = pl.ds(b * BLK, BLK)
            s = _scores(q_ref[rows, :], kwin[b * BLK:(b + 2) * BLK], bias, n * QB + b, seg)
            mx = jnp.max(s, axis=-1, keepdims=True)
            p = jnp.exp(s - mx)
            den = jnp.sum(p, axis=-1, keepdims=True)
            o_ref[rows, :] = _dot(p.astype(BF16), vwin[b * BLK:(b + 2) * BLK], NN) / den
            l_ref[rows, :] = jnp.broadcast_to(mx + jnp.log(den), (BLK, HEAD))

    oblk = pl.BlockSpec((QB * BLK, HEAD), lambda h, n: (n, h))
    return pl.pallas_call(
        body, name=name, grid=(NHG, T // (QB * BLK)),
        in_specs=_attn_specs(), out_specs=[oblk, oblk],
        out_shape=[jax.ShapeDtypeStruct((T, AW), F32), jax.ShapeDtypeStruct((T, AW), F32)],
        compiler_params=_cp(("parallel", "parallel")),
    )(qkv, qkv, qkv, qkv, qkv, _band_bias(gi))


def _attn_bwd(qkv, dob, lse, delta, gi, name):
    seg = (T // GROUPS[gi][1]) // BLK
    nb = T // (QB * BLK)
    scale = HEAD ** -0.5

    def body(q_ref, kp_ref, kc_ref, vp_ref, vc_ref, bias_ref, do_ref, l_ref, dl_ref, out_ref, dk_acc, dv_acc):
        n = pl.program_id(1)
        kwin = jnp.concatenate([kp_ref[...], kc_ref[...]], axis=0)
        vwin = jnp.concatenate([vp_ref[...], vc_ref[...]], axis=0)
        bias = bias_ref[...]
        for b in range(QB):
            rows = pl.ds(b * BLK, BLK)
            q = q_ref[rows, :]
            kcat = kwin[b * BLK:(b + 2) * BLK]
            s = _scores(q, kcat, bias, n * QB + b, seg)
            p = jnp.exp(s - l_ref[rows, pl.ds(0, 1)])
            dov = do_ref[rows, :]
            dv2 = _dot(p.astype(BF16), dov, TN)
            dp = _dot(dov, vwin[b * BLK:(b + 2) * BLK], NT)
            dsb = (p * (dp - dl_ref[rows, pl.ds(0, 1)]) * scale).astype(BF16)
            row = pl.ds(pl.multiple_of((n * QB + b) * BLK, BLK), BLK)
            out_ref[0, row, :] = _dot(dsb, kcat, NN).astype(BF16)
            dk2 = _dot(dsb, q, TN)
            dk_acc[row, :] = dk2[BLK:]
            dv_acc[row, :] = dv2[BLK:]

            def add_prev(dk2=dk2, dv2=dv2, b=b):
                prow = pl.ds(pl.multiple_of((n * QB + b - 1) * BLK, BLK), BLK)
                dk_acc[prow, :] += dk2[:BLK]
                dv_acc[prow, :] += dv2[:BLK]

            if b == 0:
                pl.when(n > 0)(add_prev)
            else:
                add_prev()

        @pl.when(n == nb - 1)
        def _():
            out_ref[1] = dk_acc[...].astype(BF16)
            out_ref[2] = dv_acc[...].astype(BF16)

    oblk = pl.BlockSpec((QB * BLK, HEAD), lambda h, n: (n, h))
    return pl.pallas_call(
        body, name=name, grid=(NHG, nb),
        in_specs=_attn_specs() + [oblk, oblk, oblk],
        out_specs=pl.BlockSpec((3, T, HEAD), lambda h, n: (0, 0, h)),
        out_shape=jax.ShapeDtypeStruct((3, T, AW), BF16),
        scratch_shapes=[pltpu.VMEM((T, HEAD), F32), pltpu.VMEM((T, HEAD), F32)],
        compiler_params=_cp(("parallel", "arbitrary")),
    )(qkv, qkv, qkv, qkv, qkv, _band_bias(gi), dob, lse, delta)


def _merge(outs, lses, name):
    tm = 512

    def body(o0, l0, o1, l1, o2, l2, a_ref, ab_ref, lse_ref):
        ls = [l0[...], l1[...], l2[...]]
        mx = jnp.maximum(jnp.maximum(ls[0], ls[1]), ls[2])
        es = [jnp.exp(v - mx) for v in ls]
        tot = es[0] + es[1] + es[2]
        att = (es[0] / tot) * o0[...] + (es[1] / tot) * o1[...] + (es[2] / tot) * o2[...]
        a_ref[...] = att
        ab_ref[...] = att.astype(BF16)
        lse_ref[...] = mx + jnp.log(tot)

    row = pl.BlockSpec((tm, AW), lambda i: (i, 0))
    return pl.pallas_call(
        body, name=name, grid=(T // tm,), in_specs=[row] * 6, out_specs=[row] * 3,
        out_shape=[jax.ShapeDtypeStruct((T, AW), F32), jax.ShapeDtypeStruct((T, AW), BF16),
                   jax.ShapeDtypeStruct((T, AW), F32)],
        compiler_params=_cp(("parallel",)),
    )(outs[0], lses[0], outs[1], lses[1], outs[2], lses[2])


def _mix_out(z3b, attnb, gates, wc, wa_t, wo, x1, name):
    tm = 512

    def body(z_ref, a_ref, g_ref, wc_ref, wa_ref, wo_ref, x_ref, xo_ref, yc_ref, ya_ref, mx_ref):
        yc = _dot(z_ref[...], wc_ref[...], NN)
        ya = _dot(a_ref[...], wa_ref[...], NT)
        yc_ref[...] = yc
        ya_ref[...] = ya
        mixed = (_sig(g_ref[:, pl.ds(0, D)]) * yc + _sig(g_ref[:, pl.ds(D, D)]) * ya).astype(BF16)
        mx_ref[...] = mixed
        xo_ref[...] = x_ref[...] + _dot(mixed, wo_ref[...], NN)

    row = pl.BlockSpec((tm, D), lambda i: (i, 0))
    return pl.pallas_call(
        body, name=name, grid=(T // tm,),
        in_specs=[row, pl.BlockSpec((tm, AW), lambda i: (i, 0)), pl.BlockSpec((tm, 2 * D), lambda i: (i, 0)),
                  _const_spec((D, D)), _const_spec((D, AW)), _const_spec((D, D)), row],
        out_specs=[row, row, row, row],
        out_shape=[jax.ShapeDtypeStruct((T, D), F32), jax.ShapeDtypeStruct((T, D), F32),
                   jax.ShapeDtypeStruct((T, D), F32), jax.ShapeDtypeStruct((T, D), BF16)],
        compiler_params=_cp(("parallel",)),
    )(z3b, attnb, gates, wc, wa_t, wo, x1)


def _mix_out_bwd(dx2, gates, yc, ya, attn, wc, wa_t, wo, name):
    tm = 512

    def body(dx_ref, g_ref, yc_ref, ya_ref, at_ref, wc_ref, wa_ref, wo_ref,
             dg_ref, dyc_ref, dya_ref, dxb_ref, dz3_ref, dat_ref, dl_ref):
        dxb = dx_ref[...].astype(BF16)
        dxb_ref[...] = dxb
        dmix = _dot(dxb, wo_ref[...], NT)
        sc = _sig(g_ref[:, pl.ds(0, D)])
        sa = _sig(g_ref[:, pl.ds(D, D)])
        ycv, yav = yc_ref[...], ya_ref[...]
        dg_ref[:, pl.ds(0, D)] = (dmix * ycv * sc * (1.0 - sc)).astype(BF16)
        dg_ref[:, pl.ds(D, D)] = (dmix * yav * sa * (1.0 - sa)).astype(BF16)
        dyc = (dmix * sc).astype(BF16)
        dya = (dmix * sa).astype(BF16)
        dyc_ref[...] = dyc
        dya_ref[...] = dya
        dz3_ref[...] = _dot(dyc, wc_ref[...], NT)
        dat = _dot(dya, wa_ref[...], NN)
        dat_ref[...] = dat.astype(BF16)
        prod = dat * at_ref[...]
        for h in range(NHG):
            sl = pl.ds(h * HEAD, HEAD)
            dl_ref[:, sl] = jnp.broadcast_to(jnp.sum(prod[:, h * HEAD:(h + 1) * HEAD], axis=-1, keepdims=True),
                                             (tm, HEAD))

    row = pl.BlockSpec((tm, D), lambda i: (i, 0))
    row2 = pl.BlockSpec((tm, 2 * D), lambda i: (i, 0))
    rowa = pl.BlockSpec((tm, AW), lambda i: (i, 0))
    return pl.pallas_call(
        body, name=name, grid=(T // tm,),
        in_specs=[row, row2, row, row, rowa, _const_spec((D, D)), _const_spec((D, AW)), _const_spec((D, D))],
        out_specs=[row2, row, row, row, row, rowa, rowa],
        out_shape=[jax.ShapeDtypeStruct((T, 2 * D), BF16), jax.ShapeDtypeStruct((T, D), BF16),
                   jax.ShapeDtypeStruct((T, D), BF16), jax.ShapeDtypeStruct((T, D), BF16),
                   jax.ShapeDtypeStruct((T, D), F32), jax.ShapeDtypeStruct((T, AW), BF16),
                   jax.ShapeDtypeStruct((T, AW), F32)],
        compiler_params=_cp(("parallel",)),
    )(dx2, gates, yc, ya, attn, wc, wa_t, wo)


def _peer(k):
    x, y, c = lax.axis_index("x"), lax.axis_index("y"), lax.axis_index("c")
    px = 1 - x if k & 4 else x
    py = 1 - y if k & 2 else y
    pc = 1 - c if k & 1 else c
    return (px, py, pc), 4 * px + 2 * py + pc


HBM_SPEC = pl.BlockSpec(memory_space=pltpu.HBM)
SEM_SPEC = pl.BlockSpec(memory_space=pltpu.SEMAPHORE)
EFFECT = pltpu.SideEffectType.DATAFLOW_SIDE_EFFECTING


def _my_place():
    return 4 * lax.axis_index("x") + 2 * lax.axis_index("y") + lax.axis_index("c")


def _copy_ends(kind, src, land, me, plin, k):
    if kind == "gather":
        rows = src.shape[0]
        return src, land.at[pl.ds(me * rows, rows)]
    if kind == "scatter":
        rows = src.shape[0] // NDEV
        return src.at[pl.ds(plin * rows, rows)], land.at[k - 1]
    return src, land.at[me]


def _landing(kind, src):
    me = _my_place()
    if kind == "gather":
        rows = src.shape[0]
        land = lax.empty((NDEV * rows,) + src.shape[1:], src.dtype)
        return lax.dynamic_update_slice(land, src, (me * rows,) + (0,) * (src.ndim - 1))
    if kind == "scatter":
        return lax.empty((NDEV - 1, src.shape[0] // NDEV) + src.shape[1:], src.dtype)
    land = lax.empty((NDEV,) + src.shape, src.dtype)
    return lax.dynamic_update_slice(land, src[None], (me,) + (0,) * src.ndim)


def _send_start(kinds, srcs, order_after, name):
    n = len(srcs)
    lands = [_landing(kd, s) for kd, s in zip(kinds, srcs)]
    na = len(order_after)

    def body(*refs):
        src_refs, land_refs = refs[:n], refs[n:2 * n]
        send, recv = refs[2 * n + na], refs[2 * n + na + 1]
        token = refs[-1]
        _, me = _peer(0)
        for w in range(n):
            for k in range(1, NDEV):
                peer, plin = _peer(k)
                s, d = _copy_ends(kinds[w], src_refs[w], land_refs[w], me, plin, k)
                i = w * (NDEV - 1) + k - 1
                pltpu.make_async_remote_copy(src_ref=s, dst_ref=d, send_sem=send.at[i], recv_sem=recv.at[i],
                                             device_id=peer, device_id_type=MESH_ID).start()
        token[...] = jnp.zeros_like(token)

    nsem = n * (NDEV - 1)
    bufs = [pltpu.with_memory_space_constraint(a, pltpu.HBM) for a in list(srcs) + lands]
    out = pl.pallas_call(
        body, name=name,
        in_specs=[HBM_SPEC] * (2 * n) + [pl.BlockSpec(memory_space=pl.ANY)] * na,
        out_specs=[SEM_SPEC, SEM_SPEC] + [HBM_SPEC] * (2 * n) + [pl.BlockSpec(memory_space=pltpu.VMEM)],
        out_shape=[pltpu.SemaphoreType.DMA((nsem,)), pltpu.SemaphoreType.DMA((nsem,))]
        + [pltpu.HBM(a.shape, a.dtype) for a in bufs] + [jax.ShapeDtypeStruct((8, 128), F32)],
        input_output_aliases={i: 2 + i for i in range(2 * n)},
        compiler_params=pltpu.CompilerParams(has_side_effects=EFFECT),
    )(*bufs, *order_after)
    return out[0], out[1], out[2:2 + n], out[2 + n:2 + 2 * n], out[-1]


def _send_wait(kinds, started, order_after, name):
    send, recv, srcs, lands, _ = started
    n = len(srcs)
    na = len(order_after)

    def body(*refs):
        src_refs, land_refs = refs[:n], refs[n:2 * n]
        send_ref, recv_ref = refs[2 * n], refs[2 * n + 1]
        _, me = _peer(0)
        for w in range(n):
            for k in range(1, NDEV):
                peer, plin = _peer(k)
                s, d = _copy_ends(kinds[w], src_refs[w], land_refs[w], me, plin, k)
                i = w * (NDEV - 1) + k - 1
                cp = pltpu.make_async_remote_copy(src_ref=s, dst_ref=d, send_sem=send_ref.at[i],
                                                  recv_sem=recv_ref.at[i], device_id=peer, device_id_type=MESH_ID)
                cp.wait_send()
                cp.wait_recv()

    bufs = list(srcs) + list(lands)
    out = pl.pallas_call(
        body, name=name,
        in_specs=[HBM_SPEC] * (2 * n) + [SEM_SPEC, SEM_SPEC] + [pl.BlockSpec(memory_space=pl.ANY)] * na,
        out_specs=[HBM_SPEC] * (2 * n),
        out_shape=[pltpu.HBM(a.shape, a.dtype) for a in bufs],
        input_output_aliases={i: i for i in range(2 * n)},
        compiler_params=pltpu.CompilerParams(has_side_effects=EFFECT),
    )(*bufs, send, recv, *order_after)
    return out[:n], out[n:]


def _gsum(own, land, name):
    rows, cols = own.shape
    tr = rows // 2 if rows * cols > 512 * 1024 and rows % 32 == 0 else rows

    def body(own_ref, l_ref, o_ref):
        tot = own_ref[...].astype(F32)
        for s in range(NDEV - 1):
            tot = tot + l_ref[s].astype(F32)
        o_ref[...] = tot

    return pl.pallas_call(
        body, name=name, grid=(rows // tr,),
        in_specs=[pl.BlockSpec((tr, cols), lambda i: (i, 0)),
                  pl.BlockSpec((NDEV - 1, tr, cols), lambda i: (0, i, 0))],
        out_specs=pl.BlockSpec((tr, cols), lambda i: (i, 0)),
        out_shape=jax.ShapeDtypeStruct((rows, cols), F32),
        compiler_params=_cp(("parallel",)),
    )(own, land)


def _adamw_math(w, g, m, v):
    m2 = B1 * m + (1.0 - B1) * g
    v2 = B2 * v + (1.0 - B2) * (g * g)
    m_hat = m2 / (1.0 - B1 ** STEP)
    v_hat = v2 / (1.0 - B2 ** STEP)
    delta = -LR * (m_hat / (jnp.sqrt(v_hat) + AEPS) + WD * w)
    return delta, m2, v2


def _adamw(w, g, m, v, name):
    rows, cols = w.shape
    tr = 256 if rows % 256 == 0 and rows > 256 else rows

    def body(w_ref, g_ref, m_ref, v_ref, d_ref, mo_ref, vo_ref):
        d, m2, v2 = _adamw_math(w_ref[...], g_ref[...], m_ref[...], v_ref[...])
        d_ref[...] = d
        mo_ref[...] = m2
        vo_ref[...] = v2

    blk = pl.BlockSpec((tr, cols), lambda i: (i, 0))
    return pl.pallas_call(
        body, name=name, grid=(rows // tr,), in_specs=[blk] * 4, out_specs=[blk] * 3,
        out_shape=[jax.ShapeDtypeStruct((rows, cols), F32)] * 3,
        compiler_params=_cp(("parallel",)),
    )(w, g, m, v)


def _small_update(vland, w8, m8, v8, name):
    def body(l_ref, w_ref, m_ref, v_ref, g_ref, d_ref, mo_ref, vo_ref):
        g = l_ref[0]
        for s in range(1, NDEV):
            g = g + l_ref[s]
        g_ref[...] = g
        d, m2, v2 = _adamw_math(w_ref[...], g, m_ref[...], v_ref[...])
        d_ref[...] = d
        mo_ref[...] = m2
        vo_ref[...] = v2

    return pl.pallas_call(
        body, name=name, out_shape=[jax.ShapeDtypeStruct((8, D), F32)] * 4,
        compiler_params=_cp(None),
    )(vland, w8, m8, v8)


def _perm(a, dil):
    if dil == 1:
        return a
    t, n = a.shape
    return a.reshape(t // dil, dil, n).transpose(1, 0, 2).reshape(t, n)


def _unperm(a, dil):
    if dil == 1:
        return a
    t, n = a.shape
    return a.reshape(dil, t // dil, n).transpose(1, 0, 2).reshape(t, n)


def kernel(x, ffn1_norm, ffn1_w_gate, ffn1_w_up, ffn1_w_down, mix_norm, w_in, conv_dw_kernel, conv_dw_bias, conv_ln_gain, conv_ln_bias, conv_w_out, attn_w_out, w_o, ffn2_norm, ffn2_w_gate, ffn2_w_up, ffn2_w_down, final_norm, loss_target, m_ffn1_norm, m_ffn1_w_gate, m_ffn1_w_up, m_ffn1_w_down, m_mix_norm, m_w_in, m_conv_dw_kernel, m_conv_dw_bias, m_conv_ln_gain, m_conv_ln_bias, m_conv_w_out, m_attn_w_out, m_w_o, m_ffn2_norm, m_ffn2_w_gate, m_ffn2_w_up, m_ffn2_w_down, m_final_norm, v_ffn1_norm, v_ffn1_w_gate, v_ffn1_w_up, v_ffn1_w_down, v_mix_norm, v_w_in, v_conv_dw_kernel, v_conv_dw_bias, v_conv_ln_gain, v_conv_ln_bias, v_conv_w_out, v_attn_w_out, v_w_o, v_ffn2_norm, v_ffn2_w_gate, v_ffn2_w_up, v_ffn2_w_down, v_final_norm):
    names = ["ffn1_norm", "ffn1_w_gate", "ffn1_w_up", "ffn1_w_down", "mix_norm", "w_in", "conv_dw_kernel",
             "conv_dw_bias", "conv_ln_gain", "conv_ln_bias", "conv_w_out", "attn_w_out", "w_o", "ffn2_norm",
             "ffn2_w_gate", "ffn2_w_up", "ffn2_w_down", "final_norm"]
    w = dict(ffn1_norm=ffn1_norm, ffn1_w_gate=ffn1_w_gate, ffn1_w_up=ffn1_w_up, ffn1_w_down=ffn1_w_down, mix_norm=mix_norm, w_in=w_in, conv_dw_kernel=conv_dw_kernel, conv_dw_bias=conv_dw_bias, conv_ln_gain=conv_ln_gain, conv_ln_bias=conv_ln_bias, conv_w_out=conv_w_out, attn_w_out=attn_w_out, w_o=w_o, ffn2_norm=ffn2_norm, ffn2_w_gate=ffn2_w_gate, ffn2_w_up=ffn2_w_up, ffn2_w_down=ffn2_w_down, final_norm=final_norm)
    mo = dict(ffn1_norm=m_ffn1_norm, ffn1_w_gate=m_ffn1_w_gate, ffn1_w_up=m_ffn1_w_up, ffn1_w_down=m_ffn1_w_down, mix_norm=m_mix_norm, w_in=m_w_in, conv_dw_kernel=m_conv_dw_kernel, conv_dw_bias=m_conv_dw_bias, conv_ln_gain=m_conv_ln_gain, conv_ln_bias=m_conv_ln_bias, conv_w_out=m_conv_w_out, attn_w_out=m_attn_w_out, w_o=m_w_o, ffn2_norm=m_ffn2_norm, ffn2_w_gate=m_ffn2_w_gate, ffn2_w_up=m_ffn2_w_up, ffn2_w_down=m_ffn2_w_down, final_norm=m_final_norm)
    vo = dict(ffn1_norm=v_ffn1_norm, ffn1_w_gate=v_ffn1_w_gate, ffn1_w_up=v_ffn1_w_up, ffn1_w_down=v_ffn1_w_down, mix_norm=v_mix_norm, w_in=v_w_in, conv_dw_kernel=v_conv_dw_kernel, conv_dw_bias=v_conv_dw_bias, conv_ln_gain=v_conv_ln_gain, conv_ln_bias=v_conv_ln_bias, conv_w_out=v_conv_w_out, attn_w_out=v_attn_w_out, w_o=v_w_o, ffn2_norm=v_ffn2_norm, ffn2_w_gate=v_ffn2_w_gate, ffn2_w_up=v_ffn2_w_up, ffn2_w_down=v_ffn2_w_down, final_norm=v_final_norm)
    col_sharded = ("ffn1_w_gate", "ffn1_w_up", "w_in", "attn_w_out", "ffn2_w_gate", "ffn2_w_up")
    row_sharded = ("ffn1_w_down", "conv_w_out", "w_o", "ffn2_w_down")
    small = ("ffn1_norm", "mix_norm", "ffn2_norm", "final_norm", "conv_dw_bias", "conv_ln_gain", "conv_ln_bias")

    def shard(n):
        return (jnp.transpose(w[n][0]) if n in col_sharded else w[n][0]).astype(BF16)

    kshard = jnp.pad(conv_dw_kernel[0], ((0, 1), (0, 0)))
    ag_groups = (("ffn1_w_gate", "ffn1_w_up", "ffn1_w_down"),
                 ("w_in", "attn_w_out", "conv_w_out", "w_o"),
                 ("ffn2_w_gate", "ffn2_w_up", "ffn2_w_down"))
    ag, order = [], []
    for gi, grp in enumerate(ag_groups):
        srcs = [shard(n) for n in grp] + ([kshard] if gi == 1 else [])
        st = _send_start(["gather"] * len(srcs), srcs, order, f"gather_start{gi}")
        ag.append(st)
        order = [st[4]]

    def gathered(gi, after):
        grp = ag_groups[gi]
        nb = len(grp) + (1 if gi == 1 else 0)
        return _send_wait(["gather"] * nb, ag[gi], after, f"gather_wait{gi}")[1]

    x0 = x[0]
    tgt = loss_target[0]
    gf = final_norm.reshape(1, D)

    wg1, wu1, wd1 = gathered(0, [ag[2][4]])
    x1, gg1, uu1 = _ffn_fwd(x0, ffn1_norm, wg1, wu1, wd1, "ffn1_fwd")
    win_t, wa_t, wc, wo, kern_blocks = gathered(1, [x1])
    kern = kern_blocks.reshape(NDEV, 32, D // NDEV).transpose(1, 0, 2).reshape(32, D)
    h2 = _norm_cast(x1, mix_norm, "mix_norm_fwd")
    ab = _mm(h2, win_t, mode="nt", m=T, n=2 * D, k=D, tm=512, tn=512, tk=D, out_dtype=F32, name="proj_conv")
    gates = _mm(h2, win_t, mode="nt", m=T, n=2 * D, k=D, tm=512, tn=512, tk=D, out_dtype=F32,
                b_map=lambda i, j, kk: (13 + j, 0), name="proj_gates")
    h2p, qkv = [], []
    for gi, (_, dil) in enumerate(GROUPS):
        hp = _perm(h2, dil)
        h2p.append(hp)
        qkv.append(_mm(hp, win_t, mode="nt", m=T, n=3 * AW, k=D, tm=512, tn=AW, tk=D, out_dtype=BF16,
                       b_map=lambda i, j, kk, gi=gi: (4 + gi + 3 * j, 0), name=f"proj_qkv{gi}"))
    z1, z3b = _conv_fwd(ab, kern, conv_dw_bias, conv_ln_gain, conv_ln_bias, "conv_fwd")
    outs, lses = [], []
    for gi, (_, dil) in enumerate(GROUPS):
        o, l = _attn_fwd(qkv[gi], gi, f"attn_fwd{gi}")
        outs.append(_unperm(o, dil))
        lses.append(_unperm(l, dil))
    attn, attnb, lse = _merge(outs, lses, "attn_merge")
    x2, yc, ya, mixedb = _mix_out(z3b, attnb, gates, wc, wa_t, wo, x1, "mix_out_fwd")
    wg2, wu2, wd2 = gathered(2, [x2])
    x3, gg2, uu2 = _ffn_fwd(x2, ffn2_norm, wg2, wu2, wd2, "ffn2_fwd")

    dx3, dgf, loss_part = _final(x3, gf, tgt, "final_norm_loss")
    dx2, dg3, dgb, dub, actb, hb, dob = _ffn_bwd(x2, ffn2_norm, gg2, uu2, dx3, wg2, wu2, wd2, "ffn2_bwd")
    grads = {}
    grads["ffn2_w_gate"] = _wgrad(dgb, hb, FF, D, "ffn2_dwg")
    grads["ffn2_w_up"] = _wgrad(dub, hb, FF, D, "ffn2_dwu")
    grads["ffn2_w_down"] = _wgrad(actb, dob, FF, D, "ffn2_dwd")
    rs_groups = [("ffn2_w_gate", "ffn2_w_up", "ffn2_w_down"),
                 ("w_in", "attn_w_out", "conv_w_out", "w_o", "conv_dw_kernel"),
                 ("ffn1_w_gate", "ffn1_w_up", "ffn1_w_down")]
    rs = [_send_start(["scatter"] * 3, [grads[n] for n in rs_groups[0]], [], "scatter_start0")]

    dgates, dycb, dyab, dx2b, dz3, dattnb, delta = _mix_out_bwd(dx2, gates, yc, ya, attn, wc, wa_t, wo, "mix_out_bwd")
    grads["w_o"] = _wgrad(mixedb, dx2b, D, D, "dw_o")
    grads["conv_w_out"] = _wgrad(z3b, dycb, D, D, "dw_conv_out")
    grads["attn_w_out"] = _wgrad(dyab, attnb, D, AW, "dw_attn_out")
    dab, dkern, dvec = _conv_bwd(dz3, z1, ab, kern, conv_ln_gain, conv_ln_bias, "conv_bwd")

    dqkv = []
    for gi, (_, dil) in enumerate(GROUPS):
        dq3 = _attn_bwd(qkv[gi], _perm(dattnb, dil), _perm(lse, dil), _perm(delta, dil), gi, f"attn_bwd{gi}")
        dqkv.append(dq3.reshape(3 * T, AW))

    nrow = T // 512
    dh = _mm(dab, win_t, mode="nn", m=T, n=D, k=2 * D, tm=512, tn=D, tk=512, out_dtype=F32, name="dproj_conv")
    dh = _mm(dgates, win_t, mode="nn", m=T, n=D, k=2 * D, tm=512, tn=D, tk=512, out_dtype=F32,
             b_map=lambda i, j, kk: (13 + kk, 0), init=dh, name="dproj_gates")
    dhs = []
    for gi, (_, dil) in enumerate(GROUPS):
        part = _mm(dqkv[gi], win_t, mode="nn", m=T, n=D, k=3 * AW, tm=512, tn=D, tk=AW, out_dtype=F32,
                   a_map=lambda i, j, kk: (kk * nrow + i, 0), b_map=lambda i, j, kk, gi=gi: (4 + gi + 3 * kk, 0),
                   init=dh if gi == 0 else None, name=f"dproj_qkv{gi}")
        dhs.append(_unperm(part, dil))
    dx1, dg2 = _rms_bwd(x1, mix_norm, dhs, dx2, "mix_norm_bwd")

    dwin = _mm(dab, h2, mode="tn", m=2 * D, n=D, k=T, tm=2 * D, tn=D, tk=512, out_dtype=BF16, out_rows=IN_W,
               name="dw_in_conv")
    dwin = _mm(dgates, h2, mode="tn", m=2 * D, n=D, k=T, tm=512, tn=D, tk=512, out_dtype=BF16, out_rows=IN_W,
               o_map=lambda i, j, kk: (13 + i, 0), passthru=dwin, name="dw_in_gates")
    for gi in range(3):
        dwin = _mm(dqkv[gi], h2p[gi], mode="tn", m=3 * AW, n=D, k=T, tm=AW, tn=D, tk=512, out_dtype=BF16,
                   out_rows=IN_W, a_map=lambda i, j, kk: (i * (T // 512) + kk, 0),
                   o_map=lambda i, j, kk, gi=gi: (4 + gi + 3 * i, 0), passthru=dwin, name=f"dw_in_qkv{gi}")
    grads["w_in"] = dwin
    grads["conv_dw_kernel"] = dkern.reshape(32, NDEV, D // NDEV).transpose(1, 0, 2).reshape(NDEV * 32, D // NDEV)
    rs.append(_send_start(["scatter"] * 5, [grads[n] for n in rs_groups[1]], [rs[0][4]], "scatter_start1"))

    dx0, dg1, dgb, dub, actb, hb, dob = _ffn_bwd(x0, ffn1_norm, gg1, uu1, dx1, wg1, wu1, wd1, "ffn1_bwd")
    grads["ffn1_w_gate"] = _wgrad(dgb, hb, FF, D, "ffn1_dwg")
    grads["ffn1_w_up"] = _wgrad(dub, hb, FF, D, "ffn1_dwu")
    grads["ffn1_w_down"] = _wgrad(actb, dob, FF, D, "ffn1_dwd")
    vec = jnp.concatenate([dg1, dg2, dg3, dgf, dvec[0:3], jnp.broadcast_to(loss_part[:, :1], (1, D))], axis=0)
    rs.append(_send_start(["scatter"] * 3 + ["bcast"], [grads[n] for n in rs_groups[2]] + [vec], [rs[1][4]],
                          "scatter_start2"))

    g_out, d_out, m_out, v_out = {}, {}, {}, {}
    me = _my_place()
    after = [rs[2][4]]
    for gi, grp in enumerate(rs_groups):
        kinds = ["scatter"] * len(grp) + (["bcast"] if gi == 2 else [])
        srcs, lands = _send_wait(kinds, rs[gi], after, f"scatter_wait{gi}")
        for n, src, land in zip(grp, srcs, lands):
            rows = src.shape[0] // NDEV
            own = lax.dynamic_slice(src, (me * rows, 0), (rows, src.shape[1]))
            g = _gsum(own, land, f"gsum_{n}")
            if n in col_sharded:
                g = jnp.transpose(g)
            elif n == "conv_dw_kernel":
                g = g[:CONV_W]
            d, m2, v2 = _adamw(w[n][0], g, mo[n][0], vo[n][0], f"adamw_{n}")
            g_out[n], d_out[n], m_out[n], v_out[n] = g[None], d[None], m2[None], v2[None]
            after = [d]
    vland = lands[-1]

    def rows8(src):
        return jnp.concatenate([src[n].reshape(1, D) for n in small] + [jnp.ones((1, D), F32)], axis=0)

    g8, d8, m8, v8 = _small_update(vland, rows8(w), rows8(mo), rows8(vo), "small_update")
    for r, n in enumerate(small):
        shp = w[n].shape
        g_out[n], d_out[n], m_out[n], v_out[n] = (a[r].reshape(shp) for a in (g8, d8, m8, v8))
    loss = g8[7, 0]

    return (loss, dx0[None], *[g_out[n] for n in names], *[d_out[n] for n in names],
            *[m_out[n] for n in names], *[v_out[n] for n in names])
```

```python
import numpy as np
import jax
import jax.numpy as jnp
from jax import lax
from jax.experimental import pallas as pl
from jax.experimental.pallas import tpu as pltpu

F32 = jnp.float32
BF16 = jnp.bfloat16

T = 4096
D = 1024
FF = 2816
NDEV = 8
CONV_W = 31
HEAD = 128
BLK = 128
GROUPS = ((128, 1), (512, 4), (2048, 16))
NHG = 4
AW = NHG * HEAD
IN_W = 2 * D + 3 * 3 * AW + 2 * D
EPS = 1e-6
B1, B2, LR, AEPS, WD, STEP = 0.9, 0.999, 0.001, 1e-08, 0.01, 10
NEG = -1e30
VMEM_LIMIT = 56 * 1024 * 1024
MESH_ID = pl.DeviceIdType.MESH

NT = (((1,), (1,)), ((), ()))
NN = (((1,), (0,)), ((), ()))
TN = (((0,), (0,)), ((), ()))
_DIMS = {"nn": NN, "nt": NT, "tn": TN}


def _cp(sem=None):
    return pltpu.CompilerParams(dimension_semantics=sem, vmem_limit_bytes=VMEM_LIMIT)


def _sig(v):
    return 1.0 / (1.0 + jnp.exp(-v))


def _dot(a, b, dims):
    return lax.dot_general(a, b, dims, preferred_element_type=F32)


def _const_spec(shape):
    nd = len(shape)
    return pl.BlockSpec(shape, lambda *_: (0,) * nd)


def _mm(a, b, *, mode, m, n, k, tm, tn, tk, out_dtype, name, a_map=None, b_map=None,
        o_map=None, out_rows=None, init=None, passthru=None):
    gi, gj, gk = m // tm, n // tn, k // tk
    assert gi * tm == m and gj * tn == n and gk * tk == k, (name, m, n, k, tm, tn, tk)
    if mode == "nn":
        a_blk, b_blk = (tm, tk), (tk, tn)
        da, db = (lambda i, j, kk: (i, kk)), (lambda i, j, kk: (kk, j))
    elif mode == "nt":
        a_blk, b_blk = (tm, tk), (tn, tk)
        da, db = (lambda i, j, kk: (i, kk)), (lambda i, j, kk: (j, kk))
    else:
        a_blk, b_blk = (tk, tm), (tk, tn)
        da, db = (lambda i, j, kk: (kk, i)), (lambda i, j, kk: (kk, j))
    a_map = a_map or da
    b_map = b_map or db
    o_map = o_map or (lambda i, j, kk: (i, j))
    dims = _DIMS[mode]
    extra = init if init is not None else passthru
    out_rows = out_rows or m

    def body(*refs):
        if init is not None:
            a_ref, b_ref, i_ref, o_ref, acc = refs
        elif passthru is not None:
            a_ref, b_ref, _, o_ref, acc = refs
        else:
            a_ref, b_ref, o_ref, acc = refs
        kk = pl.program_id(2)

        @pl.when(kk == 0)
        def _():
            if init is not None:
                acc[...] = i_ref[...].astype(F32)
            else:
                acc[...] = jnp.zeros_like(acc)

        acc[...] += _dot(a_ref[...], b_ref[...], dims)

        @pl.when(kk == gk - 1)
        def _():
            o_ref[...] = acc[...].astype(out_dtype)

    in_specs = [pl.BlockSpec(a_blk, a_map), pl.BlockSpec(b_blk, b_map)]
    args = [a, b]
    aliases = {}
    if init is not None:
        in_specs.append(pl.BlockSpec((tm, tn), o_map))
        args.append(init)
        aliases = {2: 0}
    elif passthru is not None:
        in_specs.append(pl.BlockSpec(memory_space=pl.ANY))
        args.append(passthru)
        aliases = {2: 0}
    out_dt = extra.dtype if extra is not None else out_dtype
    assert out_dt == out_dtype
    return pl.pallas_call(
        body, name=name, grid=(gi, gj, gk),
        in_specs=in_specs, out_specs=pl.BlockSpec((tm, tn), o_map),
        out_shape=jax.ShapeDtypeStruct((out_rows, n), out_dtype),
        scratch_shapes=[pltpu.VMEM((tm, tn), F32)],
        input_output_aliases=aliases,
        compiler_params=_cp(("parallel", "parallel", "arbitrary")),
    )(*args)


def _ffn_fwd(x, g, wg_t, wu_t, wd, name):
    tm, fc = 512, 256
    nc = FF // fc

    def body(x_ref, g_ref, wg_ref, wu_ref, wd_ref, xo_ref, gg_ref, uu_ref, acc):
        xv = x_ref[...]
        r = lax.rsqrt(jnp.mean(xv * xv, axis=-1, keepdims=True) + EPS)
        h = (xv * r * g_ref[...]).astype(BF16)
        acc[...] = jnp.zeros_like(acc)
        for c in range(nc):
            sl = pl.ds(c * fc, fc)
            gg = _dot(h, wg_ref[sl, :], NT)
            uu = _dot(h, wu_ref[sl, :], NT)
            gg_ref[:, sl] = gg.astype(BF16)
            uu_ref[:, sl] = uu.astype(BF16)
            act = (gg * _sig(gg) * uu).astype(BF16)
            acc[...] += _dot(act, wd_ref[sl, :], NN)
        xo_ref[...] = xv + 0.5 * acc[...]

    wspec = pl.BlockSpec((FF, D), lambda i: (0, 0), pipeline_mode=pl.Buffered(1))
    return pl.pallas_call(
        body, name=name, grid=(T // tm,),
        in_specs=[pl.BlockSpec((tm, D), lambda i: (i, 0)), _const_spec((1, D)), wspec, wspec, wspec],
        out_specs=[pl.BlockSpec((tm, D), lambda i: (i, 0)), pl.BlockSpec((tm, FF), lambda i: (i, 0)),
                   pl.BlockSpec((tm, FF), lambda i: (i, 0))],
        out_shape=[jax.ShapeDtypeStruct((T, D), F32), jax.ShapeDtypeStruct((T, FF), BF16),
                   jax.ShapeDtypeStruct((T, FF), BF16)],
        scratch_shapes=[pltpu.VMEM((tm, D), F32)],
        compiler_params=_cp(("parallel",)),
    )(x, g, wg_t, wu_t, wd)


def _ffn_bwd(x, g, gg_all, uu_all, dout, wg_t, wu_t, wd, name):
    tm, fc = 256, 256
    nc = FF // fc

    def body(x_ref, g_ref, gg_ref, uu_ref, do_ref, wg_ref, wu_ref, wd_ref,
             dx_ref, dgam_ref, dg_ref, du_ref, act_ref, h_ref, db_ref, acc):
        i = pl.program_id(0)
        xv = x_ref[...]
        r = lax.rsqrt(jnp.mean(xv * xv, axis=-1, keepdims=True) + EPS)
        xhat = xv * r
        gam = g_ref[...]
        h_ref[...] = (xhat * gam).astype(BF16)
        dov = do_ref[...]
        dbv = (0.5 * dov).astype(BF16)
        db_ref[...] = dbv
        acc[...] = jnp.zeros_like(acc)
        for c in range(nc):
            sl = pl.ds(c * fc, fc)
            da = _dot(dbv, wd_ref[sl, :], NT)
            gg = gg_ref[:, sl].astype(F32)
            uu = uu_ref[:, sl].astype(F32)
            s = _sig(gg)
            si = gg * s
            dgv = (da * uu * (s * (1.0 + gg * (1.0 - s)))).astype(BF16)
            duv = (da * si).astype(BF16)
            dg_ref[:, sl] = dgv
            du_ref[:, sl] = duv
            act_ref[:, sl] = (si * uu).astype(BF16)
            acc[...] += _dot(dgv, wg_ref[sl, :], NN) + _dot(duv, wu_ref[sl, :], NN)
        dh = acc[...]

        @pl.when(i == 0)
        def _():
            dgam_ref[...] = jnp.zeros_like(dgam_ref)

        dgam_ref[...] += jnp.sum(dh * xhat, axis=0, keepdims=True)
        dxh = dh * gam
        dx_ref[...] = dov + r * (dxh - xhat * jnp.mean(dxh * xhat, axis=-1, keepdims=True))

    wspec = pl.BlockSpec((FF, D), lambda i: (0, 0), pipeline_mode=pl.Buffered(1))
    row_d = pl.BlockSpec((tm, D), lambda i: (i, 0))
    row_f = pl.BlockSpec((tm, FF), lambda i: (i, 0))
    return pl.pallas_call(
        body, name=name, grid=(T // tm,),
        in_specs=[row_d, _const_spec((1, D)), row_f, row_f, row_d, wspec, wspec, wspec],
        out_specs=[row_d, _const_spec((1, D)), row_f, row_f, row_f, row_d, row_d],
        out_shape=[jax.ShapeDtypeStruct((T, D), F32), jax.ShapeDtypeStruct((1, D), F32),
                   jax.ShapeDtypeStruct((T, FF), BF16), jax.ShapeDtypeStruct((T, FF), BF16),
                   jax.ShapeDtypeStruct((T, FF), BF16), jax.ShapeDtypeStruct((T, D), BF16),
                   jax.ShapeDtypeStruct((T, D), BF16)],
        scratch_shapes=[pltpu.VMEM((tm, D), F32)],
        compiler_params=_cp(("arbitrary",)),
    )(x, g, gg_all, uu_all, dout, wg_t, wu_t, wd)


def _wgrad(a, b, m, n, name, tk=512):
    return _mm(a, b, mode="tn", m=m, n=n, k=T, tm=m, tn=n, tk=tk, out_dtype=BF16, name=name)


def _norm_cast(x, g, name):
    tm = 512

    def body(x_ref, g_ref, h_ref):
        xv = x_ref[...]
        r = lax.rsqrt(jnp.mean(xv * xv, axis=-1, keepdims=True) + EPS)
        h_ref[...] = (xv * r * g_ref[...]).astype(BF16)

    return pl.pallas_call(
        body, name=name, grid=(T // tm,),
        in_specs=[pl.BlockSpec((tm, D), lambda i: (i, 0)), _const_spec((1, D))],
        out_specs=pl.BlockSpec((tm, D), lambda i: (i, 0)),
        out_shape=jax.ShapeDtypeStruct((T, D), BF16),
        compiler_params=_cp(("parallel",)),
    )(x, g)


def _final(x3, gf, tgt, name):
    tm = 512

    def body(x_ref, g_ref, t_ref, dx_ref, dgam_ref, loss_ref):
        i = pl.program_id(0)
        xv = x_ref[...]
        r = lax.rsqrt(jnp.mean(xv * xv, axis=-1, keepdims=True) + EPS)
        xhat = xv * r
        gam = g_ref[...]
        err = xhat * gam - t_ref[...]
        part = 0.5 * jnp.sum(jnp.mean(err * err, axis=-1, keepdims=True), axis=0, keepdims=True)
        dy = err * (1.0 / D)

        @pl.when(i == 0)
        def _():
            dgam_ref[...] = jnp.zeros_like(dgam_ref)
            loss_ref[...] = jnp.zeros_like(loss_ref)

        dgam_ref[...] += jnp.sum(dy * xhat, axis=0, keepdims=True)
        loss_ref[...] += jnp.broadcast_to(part, loss_ref.shape)
        dxh = dy * gam
        dx_ref[...] = r * (dxh - xhat * jnp.mean(dxh * xhat, axis=-1, keepdims=True))

    row_d = pl.BlockSpec((tm, D), lambda i: (i, 0))
    return pl.pallas_call(
        body, name=name, grid=(T // tm,),
        in_specs=[row_d, _const_spec((1, D)), row_d],
        out_specs=[row_d, _const_spec((1, D)), _const_spec((1, 128))],
        out_shape=[jax.ShapeDtypeStruct((T, D), F32), jax.ShapeDtypeStruct((1, D), F32),
                   jax.ShapeDtypeStruct((1, 128), F32)],
        compiler_params=_cp(("arbitrary",)),
    )(x3, gf, tgt)


def _rms_bwd(x, g, dhs, dres, name):
    tm = 512
    nh = len(dhs)

    def body(*refs):
        x_ref, g_ref = refs[:2]
        dh_refs = refs[2:2 + nh]
        dr_ref, dx_ref, dgam_ref = refs[2 + nh:]
        i = pl.program_id(0)
        xv = x_ref[...]
        r = lax.rsqrt(jnp.mean(xv * xv, axis=-1, keepdims=True) + EPS)
        xhat = xv * r
        gam = g_ref[...]
        dh = dh_refs[0][...]
        for ref in dh_refs[1:]:
            dh = dh + ref[...]

        @pl.when(i == 0)
        def _():
            dgam_ref[...] = jnp.zeros_like(dgam_ref)

        dgam_ref[...] += jnp.sum(dh * xhat, axis=0, keepdims=True)
        dxh = dh * gam
        dx_ref[...] = dr_ref[...] + r * (dxh - xhat * jnp.mean(dxh * xhat, axis=-1, keepdims=True))

    row_d = pl.BlockSpec((tm, D), lambda i: (i, 0))
    return pl.pallas_call(
        body, name=name, grid=(T // tm,),
        in_specs=[row_d, _const_spec((1, D))] + [row_d] * nh + [row_d],
        out_specs=[row_d, _const_spec((1, D))],
        out_shape=[jax.ShapeDtypeStruct((T, D), F32), jax.ShapeDtypeStruct((1, D), F32)],
        compiler_params=_cp(("arbitrary",)),
    )(x, g, *dhs, dres)


CONV_TM = 256
CONV_HALO = 32
CONV_RB = 32


def _glu(ab):
    return ab[:, :D] * _sig(ab[:, D:])


def _ln_stats(z1):
    mu = jnp.mean(z1, axis=-1, keepdims=True)
    zc = z1 - mu
    rstd = lax.rsqrt(jnp.mean(zc * zc, axis=-1, keepdims=True) + EPS)
    return zc * rstd, rstd


def _fill_shifts(zs):
    n = zs.shape[1] - 8
    for s in range(1, 8):
        zs[s, pl.ds(0, n), :] = zs[0, pl.ds(s, n), :]


def _shifted(zs, start, rows):
    q, s = divmod(start, 8)
    return zs[s, pl.ds(8 * q, rows), :]


def _conv_fwd(ab, kern, dwb, lng, lnb, name):
    tm, hl, rb = CONV_TM, CONV_HALO, CONV_RB
    off = hl - (CONV_W - 1)

    def body(ab_ref, abh_ref, k_ref, dwb_ref, lng_ref, lnb_ref, z1_ref, z3_ref, zs):
        i = pl.program_id(0)
        zs[0, pl.ds(0, hl), :] = jnp.where(i > 0, _glu(abh_ref[...]), 0.0)
        zs[0, pl.ds(hl, tm), :] = _glu(ab_ref[...])
        _fill_shifts(zs)
        for b in range(tm // rb):
            acc = jnp.zeros((rb, D), F32)
            for j in range(CONV_W):
                acc = acc + _shifted(zs, b * rb + off + j, rb) * k_ref[pl.ds(j, 1), :]
            z1 = acc + dwb_ref[...]
            z1_ref[pl.ds(b * rb, rb), :] = z1
            zn, _ = _ln_stats(z1)
            z2 = zn * lng_ref[...] + lnb_ref[...]
            z3_ref[pl.ds(b * rb, rb), :] = (z2 * _sig(z2)).astype(BF16)

    row = pl.BlockSpec((tm, D), lambda i: (i, 0))
    return pl.pallas_call(
        body, name=name, grid=(T // tm,),
        in_specs=[pl.BlockSpec((tm, 2 * D), lambda i: (i, 0)),
                  pl.BlockSpec((hl, 2 * D), lambda i: (jnp.maximum(i * (tm // hl) - 1, 0), 0)),
                  _const_spec((32, D)), _const_spec((1, D)), _const_spec((1, D)), _const_spec((1, D))],
        out_specs=[row, row],
        out_shape=[jax.ShapeDtypeStruct((T, D), F32), jax.ShapeDtypeStruct((T, D), BF16)],
        scratch_shapes=[pltpu.VMEM((8, hl + tm, D), F32)],
        compiler_params=_cp(("parallel",)),
    )(ab, ab, kern, dwb, lng, lnb)


def _conv_bwd(dz3, z1, ab, kern, lng, lnb, name):
    tm, hl, rb = CONV_TM, CONV_HALO, CONV_RB
    off = hl - (CONV_W - 1)
    nsteps = T // tm

    def ln_bwd(dz3v, z1v, lngv, lnbv):
        zn, rstd = _ln_stats(z1v)
        z2 = zn * lngv + lnbv
        s = _sig(z2)
        dz2 = dz3v * (s * (1.0 + z2 * (1.0 - s)))
        dzn = dz2 * lngv
        dz1 = rstd * (dzn - jnp.mean(dzn, axis=-1, keepdims=True)
                      - zn * jnp.mean(dzn * zn, axis=-1, keepdims=True))
        return dz1, dz2, zn

    def body(dz3_ref, dz3h_ref, z1_ref, z1h_ref, ab_ref, abh_ref, k_ref, lng_ref, lnb_ref,
             dab_ref, dk_ref, dvec_ref, zs, dzs):
        i = pl.program_id(0)
        lngv, lnbv = lng_ref[...], lnb_ref[...]

        @pl.when(i == 0)
        def _():
            dk_ref[...] = jnp.zeros_like(dk_ref)
            dvec_ref[...] = jnp.zeros_like(dvec_ref)

        dz1, dz2, zn = ln_bwd(dz3_ref[...], z1_ref[...], lngv, lnbv)
        dvec_ref[pl.ds(0, 1), :] += jnp.sum(dz1, axis=0, keepdims=True)
        dvec_ref[pl.ds(1, 1), :] += jnp.sum(dz2 * zn, axis=0, keepdims=True)
        dvec_ref[pl.ds(2, 1), :] += jnp.sum(dz2, axis=0, keepdims=True)
        dzs[0, pl.ds(0, tm), :] = dz1
        dz1h, _, _ = ln_bwd(dz3h_ref[...], z1h_ref[...], lngv, lnbv)
        dzs[0, pl.ds(tm, hl), :] = jnp.where(i < nsteps - 1, dz1h, 0.0)
        _fill_shifts(dzs)
        zs[0, pl.ds(0, hl), :] = jnp.where(i > 0, _glu(abh_ref[...]), 0.0)
        zs[0, pl.ds(hl, tm), :] = _glu(ab_ref[...])
        _fill_shifts(zs)

        for j in range(CONV_W):
            tot = jnp.zeros((rb, D), F32)
            for b in range(tm // rb):
                tot = tot + dzs[0, pl.ds(b * rb, rb), :] * _shifted(zs, b * rb + off + j, rb)
            dk_ref[pl.ds(j, 1), :] += jnp.sum(tot, axis=0, keepdims=True)

        for b in range(tm // rb):
            acc = jnp.zeros((rb, D), F32)
            for j in range(CONV_W):
                acc = acc + _shifted(dzs, b * rb + (CONV_W - 1) - j, rb) * k_ref[pl.ds(j, 1), :]
            av = ab_ref[pl.ds(b * rb, rb), pl.ds(0, D)]
            sb = _sig(ab_ref[pl.ds(b * rb, rb), pl.ds(D, D)])
            dab_ref[pl.ds(b * rb, rb), pl.ds(0, D)] = (acc * sb).astype(BF16)
            dab_ref[pl.ds(b * rb, rb), pl.ds(D, D)] = (acc * av * sb * (1.0 - sb)).astype(BF16)

    row = pl.BlockSpec((tm, D), lambda i: (i, 0))
    nxt = pl.BlockSpec((hl, D), lambda i: (jnp.minimum((i + 1) * (tm // hl), T // hl - 1), 0))
    return pl.pallas_call(
        body, name=name, grid=(nsteps,),
        in_specs=[row, nxt, row, nxt,
                  pl.BlockSpec((tm, 2 * D), lambda i: (i, 0)),
                  pl.BlockSpec((hl, 2 * D), lambda i: (jnp.maximum(i * (tm // hl) - 1, 0), 0)),
                  _const_spec((32, D)), _const_spec((1, D)), _const_spec((1, D))],
        out_specs=[pl.BlockSpec((tm, 2 * D), lambda i: (i, 0)), _const_spec((32, D)), _const_spec((8, D))],
        out_shape=[jax.ShapeDtypeStruct((T, 2 * D), BF16), jax.ShapeDtypeStruct((32, D), F32),
                   jax.ShapeDtypeStruct((8, D), F32)],
        scratch_shapes=[pltpu.VMEM((8, hl + tm, D), F32), pltpu.VMEM((8, tm + hl, D), F32)],
        compiler_params=_cp(("arbitrary",)),
    )(dz3, dz3, z1, z1, ab, ab, kern, lng, lnb)


def _alibi_slopes():
    h = np.arange(1, 3 * NHG + 1, dtype=np.float32)
    return np.power(np.float32(2.0), -8.0 * h / np.float32(3 * NHG)).astype(np.float32)


def _band_bias(gi):
    _, dil = GROUPS[gi]
    slopes = _alibi_slopes()[gi * NHG:(gi + 1) * NHG]
    qi = np.arange(BLK)[:, None]
    ki = np.arange(2 * BLK)[None, :]
    steps = BLK + qi - ki
    band = (steps >= 0) & (steps <= BLK)
    bias = -slopes[:, None, None] * (dil * steps).astype(np.float32)[None]
    return jnp.asarray(np.where(band[None], bias, np.float32(NEG)).astype(np.float32))


QB = 4


def _attn_specs():
    prev = lambda n: jnp.maximum(n * QB - 1, 0)
    return [pl.BlockSpec((QB * BLK, HEAD), lambda h, n: (n, h)),
            pl.BlockSpec((BLK, HEAD), lambda h, n: (prev(n), NHG + h)),
            pl.BlockSpec((QB * BLK, HEAD), lambda h, n: (n, NHG + h)),
            pl.BlockSpec((BLK, HEAD), lambda h, n: (prev(n), 2 * NHG + h)),
            pl.BlockSpec((QB * BLK, HEAD), lambda h, n: (n, 2 * NHG + h)),
            pl.BlockSpec((None, BLK, 2 * BLK), lambda h, n: (h, 0, 0))]


def _scores(q, kcat, bias, blk, seg):
    s = _dot(q, kcat, NT) * (HEAD ** -0.5) + bias
    col = lax.broadcasted_iota(jnp.int32, s.shape, 1)
    first = (blk % seg) == 0
    return jnp.where(jnp.logical_and(first, col < BLK), NEG, s)


def _attn_fwd(qkv, gi, name):
    seg = (T // GROUPS[gi][1]) // BLK

    def body(q_ref, kp_ref, kc_ref, vp_ref, vc_ref, bias_ref, o_ref, l_ref):
        n = pl.program_id(1)
        kwin = jnp.concatenate([kp_ref[...], kc_ref[...]], axis=0)
        vwin = jnp.concatenate([vp_ref[...], vc_ref[...]], axis=0)
        bias = bias_ref[...]
        for b in range(QB):
            rows = pl.ds(b * BLK, BLK)
            s = _scores(q_ref[rows, :], kwin[b * BLK:(b + 2) * BLK], bias, n * QB + b, seg)
            mx = jnp.max(s, axis=-1, keepdims=True)
            p = jnp.exp(s - mx)
            den = jnp.sum(p, axis=-1, keepdims=True)
            o_ref[rows, :] = _dot(p.astype(BF16), vwin[b * BLK:(b + 2) * BLK], NN) / den
            l_ref[rows, :] = jnp.broadcast_to(mx + jnp.log(den), (BLK, HEAD))

    oblk = pl.BlockSpec((QB * BLK, HEAD), lambda h, n: (n, h))
    return pl.pallas_call(
        body, name=name, grid=(NHG, T // (QB * BLK)),
        in_specs=_attn_specs(), out_specs=[oblk, oblk],
        out_shape=[jax.ShapeDtypeStruct((T, AW), F32), jax.ShapeDtypeStruct((T, AW), F32)],
        compiler_params=_cp(("parallel", "parallel")),
    )(qkv, qkv, qkv, qkv, qkv, _band_bias(gi))


def _attn_bwd(qkv, dob, lse, delta, gi, name):
    seg = (T // GROUPS[gi][1]) // BLK
    nb = T // (QB * BLK)
    scale = HEAD ** -0.5

    def body(q_ref, kp_ref, kc_ref, vp_ref, vc_ref, bias_ref, do_ref, l_ref, dl_ref, out_ref, dk_acc, dv_acc):
        n = pl.program_id(1)
        kwin = jnp.concatenate([kp_ref[...], kc_ref[...]], axis=0)
        vwin = jnp.concatenate([vp_ref[...], vc_ref[...]], axis=0)
        bias = bias_ref[...]
        for b in range(QB):
            rows = pl.ds(b * BLK, BLK)
            q = q_ref[rows, :]
            kcat = kwin[b * BLK:(b + 2) * BLK]
            s = _scores(q, kcat, bias, n * QB + b, seg)
            p = jnp.exp(s - l_ref[rows, pl.ds(0, 1)])
            dov = do_ref[rows, :]
            dv2 = _dot(p.astype(BF16), dov, TN)
            dp = _dot(dov, vwin[b * BLK:(b + 2) * BLK], NT)
            dsb = (p * (dp - dl_ref[rows, pl.ds(0, 1)]) * scale).astype(BF16)
            row = pl.ds(pl.multiple_of((n * QB + b) * BLK, BLK), BLK)
            out_ref[0, row, :] = _dot(dsb, kcat, NN).astype(BF16)
            dk2 = _dot(dsb, q, TN)
            dk_acc[row, :] = dk2[BLK:]
            dv_acc[row, :] = dv2[BLK:]

            def add_prev(dk2=dk2, dv2=dv2, b=b):
                prow = pl.ds(pl.multiple_of((n * QB + b - 1) * BLK, BLK), BLK)
                dk_acc[prow, :] += dk2[:BLK]
                dv_acc[prow, :] += dv2[:BLK]

            if b == 0:
                pl.when(n > 0)(add_prev)
            else:
                add_prev()

        @pl.when(n == nb - 1)
        def _():
            out_ref[1] = dk_acc[...].astype(BF16)
            out_ref[2] = dv_acc[...].astype(BF16)

    oblk = pl.BlockSpec((QB * BLK, HEAD), lambda h, n: (n, h))
    return pl.pallas_call(
        body, name=name, grid=(NHG, nb),
        in_specs=_attn_specs() + [oblk, oblk, oblk],
        out_specs=pl.BlockSpec((3, T, HEAD), lambda h, n: (0, 0, h)),
        out_shape=jax.ShapeDtypeStruct((3, T, AW), BF16),
        scratch_shapes=[pltpu.VMEM((T, HEAD), F32), pltpu.VMEM((T, HEAD), F32)],
        compiler_params=_cp(("parallel", "arbitrary")),
    )(qkv, qkv, qkv, qkv, qkv, _band_bias(gi), dob, lse, delta)


def _merge(outs, lses, name):
    tm = 512

    def body(o0, l0, o1, l1, o2, l2, a_ref, ab_ref, lse_ref):
        ls = [l0[...], l1[...], l2[...]]
        mx = jnp.maximum(jnp.maximum(ls[0], ls[1]), ls[2])
        es = [jnp.exp(v - mx) for v in ls]
        tot = es[0] + es[1] + es[2]
        att = (es[0] / tot) * o0[...] + (es[1] / tot) * o1[...] + (es[2] / tot) * o2[...]
        a_ref[...] = att
        ab_ref[...] = att.astype(BF16)
        lse_ref[...] = mx + jnp.log(tot)

    row = pl.BlockSpec((tm, AW), lambda i: (i, 0))
    return pl.pallas_call(
        body, name=name, grid=(T // tm,), in_specs=[row] * 6, out_specs=[row] * 3,
        out_shape=[jax.ShapeDtypeStruct((T, AW), F32), jax.ShapeDtypeStruct((T, AW), BF16),
                   jax.ShapeDtypeStruct((T, AW), F32)],
        compiler_params=_cp(("parallel",)),
    )(outs[0], lses[0], outs[1], lses[1], outs[2], lses[2])


def _mix_out(z3b, attnb, gates, wc, wa_t, wo, x1, name):
    tm = 512

    def body(z_ref, a_ref, g_ref, wc_ref, wa_ref, wo_ref, x_ref, xo_ref, yc_ref, ya_ref, mx_ref):
        yc = _dot(z_ref[...], wc_ref[...], NN)
        ya = _dot(a_ref[...], wa_ref[...], NT)
        yc_ref[...] = yc
        ya_ref[...] = ya
        mixed = (_sig(g_ref[:, pl.ds(0, D)]) * yc + _sig(g_ref[:, pl.ds(D, D)]) * ya).astype(BF16)
        mx_ref[...] = mixed
        xo_ref[...] = x_ref[...] + _dot(mixed, wo_ref[...], NN)

    row = pl.BlockSpec((tm, D), lambda i: (i, 0))
    return pl.pallas_call(
        body, name=name, grid=(T // tm,),
        in_specs=[row, pl.BlockSpec((tm, AW), lambda i: (i, 0)), pl.BlockSpec((tm, 2 * D), lambda i: (i, 0)),
                  _const_spec((D, D)), _const_spec((D, AW)), _const_spec((D, D)), row],
        out_specs=[row, row, row, row],
        out_shape=[jax.ShapeDtypeStruct((T, D), F32), jax.ShapeDtypeStruct((T, D), F32),
                   jax.ShapeDtypeStruct((T, D), F32), jax.ShapeDtypeStruct((T, D), BF16)],
        compiler_params=_cp(("parallel",)),
    )(z3b, attnb, gates, wc, wa_t, wo, x1)


def _mix_out_bwd(dx2, gates, yc, ya, attn, wc, wa_t, wo, name):
    tm = 512

    def body(dx_ref, g_ref, yc_ref, ya_ref, at_ref, wc_ref, wa_ref, wo_ref,
             dg_ref, dyc_ref, dya_ref, dxb_ref, dz3_ref, dat_ref, dl_ref):
        dxb = dx_ref[...].astype(BF16)
        dxb_ref[...] = dxb
        dmix = _dot(dxb, wo_ref[...], NT)
        sc = _sig(g_ref[:, pl.ds(0, D)])
        sa = _sig(g_ref[:, pl.ds(D, D)])
        ycv, yav = yc_ref[...], ya_ref[...]
        dg_ref[:, pl.ds(0, D)] = (dmix * ycv * sc * (1.0 - sc)).astype(BF16)
        dg_ref[:, pl.ds(D, D)] = (dmix * yav * sa * (1.0 - sa)).astype(BF16)
        dyc = (dmix * sc).astype(BF16)
        dya = (dmix * sa).astype(BF16)
        dyc_ref[...] = dyc
        dya_ref[...] = dya
        dz3_ref[...] = _dot(dyc, wc_ref[...], NT)
        dat = _dot(dya, wa_ref[...], NN)
        dat_ref[...] = dat.astype(BF16)
        prod = dat * at_ref[...]
        for h in range(NHG):
            sl = pl.ds(h * HEAD, HEAD)
            dl_ref[:, sl] = jnp.broadcast_to(jnp.sum(prod[:, h * HEAD:(h + 1) * HEAD], axis=-1, keepdims=True),
                                             (tm, HEAD))

    row = pl.BlockSpec((tm, D), lambda i: (i, 0))
    row2 = pl.BlockSpec((tm, 2 * D), lambda i: (i, 0))
    rowa = pl.BlockSpec((tm, AW), lambda i: (i, 0))
    return pl.pallas_call(
        body, name=name, grid=(T // tm,),
        in_specs=[row, row2, row, row, rowa, _const_spec((D, D)), _const_spec((D, AW)), _const_spec((D, D))],
        out_specs=[row2, row, row, row, row, rowa, rowa],
        out_shape=[jax.ShapeDtypeStruct((T, 2 * D), BF16), jax.ShapeDtypeStruct((T, D), BF16),
                   jax.ShapeDtypeStruct((T, D), BF16), jax.ShapeDtypeStruct((T, D), BF16),
                   jax.ShapeDtypeStruct((T, D), F32), jax.ShapeDtypeStruct((T, AW), BF16),
                   jax.ShapeDtypeStruct((T, AW), F32)],
        compiler_params=_cp(("parallel",)),
    )(dx2, gates, yc, ya, attn, wc, wa_t, wo)


def _peer(k):
    x, y, c = lax.axis_index("x"), lax.axis_index("y"), lax.axis_index("c")
    px = 1 - x if k & 4 else x
    py = 1 - y if k & 2 else y
    pc = 1 - c if k & 1 else c
    return (px, py, pc), 4 * px + 2 * py + pc


HBM_SPEC = pl.BlockSpec(memory_space=pltpu.HBM)
SEM_SPEC = pl.BlockSpec(memory_space=pltpu.SEMAPHORE)
EFFECT = pltpu.SideEffectType.DATAFLOW_SIDE_EFFECTING


def _my_place():
    return 4 * lax.axis_index("x") + 2 * lax.axis_index("y") + lax.axis_index("c")


def _copy_ends(kind, src, land, me, plin, k):
    if kind == "gather":
        rows = src.shape[0]
        return src, land.at[pl.ds(me * rows, rows)]
    if kind == "scatter":
        rows = src.shape[0] // NDEV
        return src.at[pl.ds(plin * rows, rows)], land.at[k - 1]
    return src, land.at[me]


def _landing(kind, src):
    me = _my_place()
    if kind == "gather":
        rows = src.shape[0]
        land = lax.empty((NDEV * rows,) + src.shape[1:], src.dtype)
        return lax.dynamic_update_slice(land, src, (me * rows,) + (0,) * (src.ndim - 1))
    if kind == "scatter":
        return lax.empty((NDEV - 1, src.shape[0] // NDEV) + src.shape[1:], src.dtype)
    land = lax.empty((NDEV,) + src.shape, src.dtype)
    return lax.dynamic_update_slice(land, src[None], (me,) + (0,) * src.ndim)


def _send_start(kinds, srcs, order_after, name):
    n = len(srcs)
    lands = [_landing(kd, s) for kd, s in zip(kinds, srcs)]
    na = len(order_after)

    def body(*refs):
        src_refs, land_refs = refs[:n], refs[n:2 * n]
        send, recv = refs[2 * n + na], refs[2 * n + na + 1]
        token = refs[-1]
        _, me = _peer(0)
        for w in range(n):
            for k in range(1, NDEV):
                peer, plin = _peer(k)
                s, d = _copy_ends(kinds[w], src_refs[w], land_refs[w], me, plin, k)
                i = w * (NDEV - 1) + k - 1
                pltpu.make_async_remote_copy(src_ref=s, dst_ref=d, send_sem=send.at[i], recv_sem=recv.at[i],
                                             device_id=peer, device_id_type=MESH_ID).start()
        token[...] = jnp.zeros_like(token)

    nsem = n * (NDEV - 1)
    bufs = [pltpu.with_memory_space_constraint(a, pltpu.HBM) for a in list(srcs) + lands]
    out = pl.pallas_call(
        body, name=name,
        in_specs=[HBM_SPEC] * (2 * n) + [pl.BlockSpec(memory_space=pl.ANY)] * na,
        out_specs=[SEM_SPEC, SEM_SPEC] + [HBM_SPEC] * (2 * n) + [pl.BlockSpec(memory_space=pltpu.VMEM)],
        out_shape=[pltpu.SemaphoreType.DMA((nsem,)), pltpu.SemaphoreType.DMA((nsem,))]
        + [pltpu.HBM(a.shape, a.dtype) for a in bufs] + [jax.ShapeDtypeStruct((8, 128), F32)],
        input_output_aliases={i: 2 + i for i in range(2 * n)},
        compiler_params=pltpu.CompilerParams(has_side_effects=EFFECT),
    )(*bufs, *order_after)
    return out[0], out[1], out[2:2 + n], out[2 + n:2 + 2 * n], out[-1]


def _send_wait(kinds, started, order_after, name):
    send, recv, srcs, lands, _ = started
    n = len(srcs)
    na = len(order_after)

    def body(*refs):
        src_refs, land_refs = refs[:n], refs[n:2 * n]
        send_ref, recv_ref = refs[2 * n], refs[2 * n + 1]
        _, me = _peer(0)
        for w in range(n):
            for k in range(1, NDEV):
                peer, plin = _peer(k)
                s, d = _copy_ends(kinds[w], src_refs[w], land_refs[w], me, plin, k)
                i = w * (NDEV - 1) + k - 1
                cp = pltpu.make_async_remote_copy(src_ref=s, dst_ref=d, send_sem=send_ref.at[i],
                                                  recv_sem=recv_ref.at[i], device_id=peer, device_id_type=MESH_ID)
                cp.wait_send()
                cp.wait_recv()

    bufs = list(srcs) + list(lands)
    out = pl.pallas_call(
        body, name=name,
        in_specs=[HBM_SPEC] * (2 * n) + [SEM_SPEC, SEM_SPEC] + [pl.BlockSpec(memory_space=pl.ANY)] * na,
        out_specs=[HBM_SPEC] * (2 * n),
        out_shape=[pltpu.HBM(a.shape, a.dtype) for a in bufs],
        input_output_aliases={i: i for i in range(2 * n)},
        compiler_params=pltpu.CompilerParams(has_side_effects=EFFECT),
    )(*bufs, send, recv, *order_after)
    return out[:n], out[n:]


def _gsum(own, land, name):
    rows, cols = own.shape
    tr = rows // 2 if rows * cols > 512 * 1024 and rows % 32 == 0 else rows

    def body(own_ref, l_ref, o_ref):
        tot = own_ref[...].astype(F32)
        for s in range(NDEV - 1):
            tot = tot + l_ref[s].astype(F32)
        o_ref[...] = tot

    return pl.pallas_call(
        body, name=name, grid=(rows // tr,),
        in_specs=[pl.BlockSpec((tr, cols), lambda i: (i, 0)),
                  pl.BlockSpec((NDEV - 1, tr, cols), lambda i: (0, i, 0))],
        out_specs=pl.BlockSpec((tr, cols), lambda i: (i, 0)),
        out_shape=jax.ShapeDtypeStruct((rows, cols), F32),
        compiler_params=_cp(("parallel",)),
    )(own, land)


def _adamw_math(w, g, m, v):
    m2 = B1 * m + (1.0 - B1) * g
    v2 = B2 * v + (1.0 - B2) * (g * g)
    m_hat = m2 / (1.0 - B1 ** STEP)
    v_hat = v2 / (1.0 - B2 ** STEP)
    delta = -LR * (m_hat / (jnp.sqrt(v_hat) + AEPS) + WD * w)
    return delta, m2, v2


def _adamw(w, g, m, v, name):
    rows, cols = w.shape
    tr = 256 if rows % 256 == 0 and rows > 256 else rows

    def body(w_ref, g_ref, m_ref, v_ref, d_ref, mo_ref, vo_ref):
        d, m2, v2 = _adamw_math(w_ref[...], g_ref[...], m_ref[...], v_ref[...])
        d_ref[...] = d
        mo_ref[...] = m2
        vo_ref[...] = v2

    blk = pl.BlockSpec((tr, cols), lambda i: (i, 0))
    return pl.pallas_call(
        body, name=name, grid=(rows // tr,), in_specs=[blk] * 4, out_specs=[blk] * 3,
        out_shape=[jax.ShapeDtypeStruct((rows, cols), F32)] * 3,
        compiler_params=_cp(("parallel",)),
    )(w, g, m, v)


def _small_update(vland, w8, m8, v8, name):
    def body(l_ref, w_ref, m_ref, v_ref, g_ref, d_ref, mo_ref, vo_ref):
        g = l_ref[0]
        for s in range(1, NDEV):
            g = g + l_ref[s]
        g_ref[...] = g
        d, m2, v2 = _adamw_math(w_ref[...], g, m_ref[...], v_ref[...])
        d_ref[...] = d
        mo_ref[...] = m2
        vo_ref[...] = v2

    return pl.pallas_call(
        body, name=name, out_shape=[jax.ShapeDtypeStruct((8, D), F32)] * 4,
        compiler_params=_cp(None),
    )(vland, w8, m8, v8)


def _perm(a, dil):
    if dil == 1:
        return a
    t, n = a.shape
    return a.reshape(t // dil, dil, n).transpose(1, 0, 2).reshape(t, n)


def _unperm(a, dil):
    if dil == 1:
        return a
    t, n = a.shape
    return a.reshape(dil, t // dil, n).transpose(1, 0, 2).reshape(t, n)


def kernel(x, ffn1_norm, ffn1_w_gate, ffn1_w_up, ffn1_w_down, mix_norm, w_in, conv_dw_kernel, conv_dw_bias, conv_ln_gain, conv_ln_bias, conv_w_out, attn_w_out, w_o, ffn2_norm, ffn2_w_gate, ffn2_w_up, ffn2_w_down, final_norm, loss_target, m_ffn1_norm, m_ffn1_w_gate, m_ffn1_w_up, m_ffn1_w_down, m_mix_norm, m_w_in, m_conv_dw_kernel, m_conv_dw_bias, m_conv_ln_gain, m_conv_ln_bias, m_conv_w_out, m_attn_w_out, m_w_o, m_ffn2_norm, m_ffn2_w_gate, m_ffn2_w_up, m_ffn2_w_down, m_final_norm, v_ffn1_norm, v_ffn1_w_gate, v_ffn1_w_up, v_ffn1_w_down, v_mix_norm, v_w_in, v_conv_dw_kernel, v_conv_dw_bias, v_conv_ln_gain, v_conv_ln_bias, v_conv_w_out, v_attn_w_out, v_w_o, v_ffn2_norm, v_ffn2_w_gate, v_ffn2_w_up, v_ffn2_w_down, v_final_norm):
    names = ["ffn1_norm", "ffn1_w_gate", "ffn1_w_up", "ffn1_w_down", "mix_norm", "w_in", "conv_dw_kernel",
             "conv_dw_bias", "conv_ln_gain", "conv_ln_bias", "conv_w_out", "attn_w_out", "w_o", "ffn2_norm",
             "ffn2_w_gate", "ffn2_w_up", "ffn2_w_down", "final_norm"]
    w = dict(ffn1_norm=ffn1_norm, ffn1_w_gate=ffn1_w_gate, ffn1_w_up=ffn1_w_up, ffn1_w_down=ffn1_w_down, mix_norm=mix_norm, w_in=w_in, conv_dw_kernel=conv_dw_kernel, conv_dw_bias=conv_dw_bias, conv_ln_gain=conv_ln_gain, conv_ln_bias=conv_ln_bias, conv_w_out=conv_w_out, attn_w_out=attn_w_out, w_o=w_o, ffn2_norm=ffn2_norm, ffn2_w_gate=ffn2_w_gate, ffn2_w_up=ffn2_w_up, ffn2_w_down=ffn2_w_down, final_norm=final_norm)
    mo = dict(ffn1_norm=m_ffn1_norm, ffn1_w_gate=m_ffn1_w_gate, ffn1_w_up=m_ffn1_w_up, ffn1_w_down=m_ffn1_w_down, mix_norm=m_mix_norm, w_in=m_w_in, conv_dw_kernel=m_conv_dw_kernel, conv_dw_bias=m_conv_dw_bias, conv_ln_gain=m_conv_ln_gain, conv_ln_bias=m_conv_ln_bias, conv_w_out=m_conv_w_out, attn_w_out=m_attn_w_out, w_o=m_w_o, ffn2_norm=m_ffn2_norm, ffn2_w_gate=m_ffn2_w_gate, ffn2_w_up=m_ffn2_w_up, ffn2_w_down=m_ffn2_w_down, final_norm=m_final_norm)
    vo = dict(ffn1_norm=v_ffn1_norm, ffn1_w_gate=v_ffn1_w_gate, ffn1_w_up=v_ffn1_w_up, ffn1_w_down=v_ffn1_w_down, mix_norm=v_mix_norm, w_in=v_w_in, conv_dw_kernel=v_conv_dw_kernel, conv_dw_bias=v_conv_dw_bias, conv_ln_gain=v_conv_ln_gain, conv_ln_bias=v_conv_ln_bias, conv_w_out=v_conv_w_out, attn_w_out=v_attn_w_out, w_o=v_w_o, ffn2_norm=v_ffn2_norm, ffn2_w_gate=v_ffn2_w_gate, ffn2_w_up=v_ffn2_w_up, ffn2_w_down=v_ffn2_w_down, final_norm=v_final_norm)
    col_sharded = ("ffn1_w_gate", "ffn1_w_up", "w_in", "attn_w_out", "ffn2_w_gate", "ffn2_w_up")
    row_sharded = ("ffn1_w_down", "conv_w_out", "w_o", "ffn2_w_down")
    small = ("ffn1_norm", "mix_norm", "ffn2_norm", "final_norm", "conv_dw_bias", "conv_ln_gain", "conv_ln_bias")

    def shard(n):
        return (jnp.transpose(w[n][0]) if n in col_sharded else w[n][0]).astype(BF16)

    kshard = jnp.pad(conv_dw_kernel[0], ((0, 1), (0, 0)))
    ag_groups = (("ffn1_w_gate", "ffn1_w_up", "ffn1_w_down"),
                 ("w_in", "attn_w_out", "conv_w_out", "w_o"),
                 ("ffn2_w_gate", "ffn2_w_up", "ffn2_w_down"))
    ag, order = [], []
    for gi, grp in enumerate(ag_groups):
        srcs = [shard(n) for n in grp] + ([kshard] if gi == 1 else [])
        st = _send_start(["gather"] * len(srcs), srcs, order, f"gather_start{gi}")
        ag.append(st)
        order = [st[4]]

    def gathered(gi, after):
        grp = ag_groups[gi]
        nb = len(grp) + (1 if gi == 1 else 0)
        return _send_wait(["gather"] * nb, ag[gi], after, f"gather_wait{gi}")[1]

    x0 = x[0]
    tgt = loss_target[0]
    gf = final_norm.reshape(1, D)

    wg1, wu1, wd1 = gathered(0, [ag[2][4]])
    x1, gg1, uu1 = _ffn_fwd(x0, ffn1_norm, wg1, wu1, wd1, "ffn1_fwd")
    win_t, wa_t, wc, wo, kern_blocks = gathered(1, [x1])
    kern = kern_blocks.reshape(NDEV, 32, D // NDEV).transpose(1, 0, 2).reshape(32, D)
    h2 = _norm_cast(x1, mix_norm, "mix_norm_fwd")
    ab = _mm(h2, win_t, mode="nt", m=T, n=2 * D, k=D, tm=1024, tn=512, tk=D, out_dtype=F32, name="proj_conv")
    gates = _mm(h2, win_t, mode="nt", m=T, n=2 * D, k=D, tm=1024, tn=512, tk=D, out_dtype=F32,
                b_map=lambda i, j, kk: (13 + j, 0), name="proj_gates")
    h2p, qkv = [], []
    for gi, (_, dil) in enumerate(GROUPS):
        hp = _perm(h2, dil)
        h2p.append(hp)
        qkv.append(_mm(hp, win_t, mode="nt", m=T, n=3 * AW, k=D, tm=1024, tn=AW, tk=D, out_dtype=BF16,
                       b_map=lambda i, j, kk, gi=gi: (4 + gi + 3 * j, 0), name=f"proj_qkv{gi}"))
    z1, z3b = _conv_fwd(ab, kern, conv_dw_bias, conv_ln_gain, conv_ln_bias, "conv_fwd")
    outs, lses = [], []
    for gi, (_, dil) in enumerate(GROUPS):
        o, l = _attn_fwd(qkv[gi], gi, f"attn_fwd{gi}")
        outs.append(_unperm(o, dil))
        lses.append(_unperm(l, dil))
    attn, attnb, lse = _merge(outs, lses, "attn_merge")
    x2, yc, ya, mixedb = _mix_out(z3b, attnb, gates, wc, wa_t, wo, x1, "mix_out_fwd")
    wg2, wu2, wd2 = gathered(2, [x2])
    x3, gg2, uu2 = _ffn_fwd(x2, ffn2_norm, wg2, wu2, wd2, "ffn2_fwd")

    dx3, dgf, loss_part = _final(x3, gf, tgt, "final_norm_loss")
    dx2, dg3, dgb, dub, actb, hb, dob = _ffn_bwd(x2, ffn2_norm, gg2, uu2, dx3, wg2, wu2, wd2, "ffn2_bwd")
    grads = {}
    grads["ffn2_w_gate"] = _wgrad(dgb, hb, FF, D, "ffn2_dwg")
    grads["ffn2_w_up"] = _wgrad(dub, hb, FF, D, "ffn2_dwu")
    grads["ffn2_w_down"] = _wgrad(actb, dob, FF, D, "ffn2_dwd")
    rs_groups = [("ffn2_w_gate", "ffn2_w_up", "ffn2_w_down"),
                 ("attn_w_out", "conv_w_out", "w_o", "conv_dw_kernel"),
                 ("w_in",),
                 ("ffn1_w_gate", "ffn1_w_up", "ffn1_w_down")]
    last = len(rs_groups) - 1
    rs = [_send_start(["scatter"] * 3, [grads[n] for n in rs_groups[0]], [], "scatter_start0")]

    dgates, dycb, dyab, dx2b, dz3, dattnb, delta = _mix_out_bwd(dx2, gates, yc, ya, attn, wc, wa_t, wo, "mix_out_bwd")
    grads["w_o"] = _wgrad(mixedb, dx2b, D, D, "dw_o")
    grads["conv_w_out"] = _wgrad(z3b, dycb, D, D, "dw_conv_out")
    grads["attn_w_out"] = _wgrad(dyab, attnb, D, AW, "dw_attn_out")
    dab, dkern, dvec = _conv_bwd(dz3, z1, ab, kern, conv_ln_gain, conv_ln_bias, "conv_bwd")
    grads["conv_dw_kernel"] = dkern.reshape(32, NDEV, D // NDEV).transpose(1, 0, 2).reshape(NDEV * 32, D // NDEV)
    rs.append(_send_start(["scatter"] * 4, [grads[n] for n in rs_groups[1]], [rs[0][4]], "scatter_start1"))

    dqkv = []
    for gi, (_, dil) in enumerate(GROUPS):
        dq3 = _attn_bwd(qkv[gi], _perm(dattnb, dil), _perm(lse, dil), _perm(delta, dil), gi, f"attn_bwd{gi}")
        dqkv.append(dq3.reshape(3 * T, AW))

    dwin = _mm(dab, h2, mode="tn", m=2 * D, n=D, k=T, tm=2 * D, tn=D, tk=512, out_dtype=BF16, out_rows=IN_W,
               name="dw_in_conv")
    dwin = _mm(dgates, h2, mode="tn", m=2 * D, n=D, k=T, tm=512, tn=D, tk=1024, out_dtype=BF16, out_rows=IN_W,
               o_map=lambda i, j, kk: (13 + i, 0), passthru=dwin, name="dw_in_gates")
    for gi in range(3):
        dwin = _mm(dqkv[gi], h2p[gi], mode="tn", m=3 * AW, n=D, k=T, tm=AW, tn=D, tk=1024, out_dtype=BF16,
                   out_rows=IN_W, a_map=lambda i, j, kk: (i * (T // 1024) + kk, 0),
                   o_map=lambda i, j, kk, gi=gi: (4 + gi + 3 * i, 0), passthru=dwin, name=f"dw_in_qkv{gi}")
    grads["w_in"] = dwin
    rs.append(_send_start(["scatter"], [dwin], [rs[1][4]], "scatter_start2"))

    nrow = T // 1024
    dh = _mm(dab, win_t, mode="nn", m=T, n=D, k=2 * D, tm=1024, tn=D, tk=512, out_dtype=F32, name="dproj_conv")
    dh = _mm(dgates, win_t, mode="nn", m=T, n=D, k=2 * D, tm=1024, tn=D, tk=512, out_dtype=F32,
             b_map=lambda i, j, kk: (13 + kk, 0), init=dh, name="dproj_gates")
    dhs = []
    for gi, (_, dil) in enumerate(GROUPS):
        part = _mm(dqkv[gi], win_t, mode="nn", m=T, n=D, k=3 * AW, tm=1024, tn=D, tk=AW, out_dtype=F32,
                   a_map=lambda i, j, kk: (kk * nrow + i, 0), b_map=lambda i, j, kk, gi=gi: (4 + gi + 3 * kk, 0),
                   init=dh if gi == 0 else None, name=f"dproj_qkv{gi}")
        dhs.append(_unperm(part, dil))
    dx1, dg2 = _rms_bwd(x1, mix_norm, dhs, dx2, "mix_norm_bwd")

    dx0, dg1, dgb, dub, actb, hb, dob = _ffn_bwd(x0, ffn1_norm, gg1, uu1, dx1, wg1, wu1, wd1, "ffn1_bwd")
    grads["ffn1_w_gate"] = _wgrad(dgb, hb, FF, D, "ffn1_dwg")
    grads["ffn1_w_up"] = _wgrad(dub, hb, FF, D, "ffn1_dwu")
    grads["ffn1_w_down"] = _wgrad(actb, dob, FF, D, "ffn1_dwd")
    vec = jnp.concatenate([dg1, dg2, dg3, dgf, dvec[0:3], jnp.broadcast_to(loss_part[:, :1], (1, D))], axis=0)
    rs.append(_send_start(["scatter"] * 3 + ["bcast"], [grads[n] for n in rs_groups[last]] + [vec], [rs[2][4]],
                          "scatter_start3"))

    g_out, d_out, m_out, v_out = {}, {}, {}, {}
    me = _my_place()
    after = [rs[last][4]]
    for gi, grp in enumerate(rs_groups):
        kinds = ["scatter"] * len(grp) + (["bcast"] if gi == last else [])
        srcs, lands = _send_wait(kinds, rs[gi], after, f"scatter_wait{gi}")
        for n, src, land in zip(grp, srcs, lands):
            rows = src.shape[0] // NDEV
            own = lax.dynamic_slice(src, (me * rows, 0), (rows, src.shape[1]))
            g = _gsum(own, land, f"gsum_{n}")
            if n in col_sharded:
                g = jnp.transpose(g)
            elif n == "conv_dw_kernel":
                g = g[:CONV_W]
            d, m2, v2 = _adamw(w[n][0], g, mo[n][0], vo[n][0], f"adamw_{n}")
            g_out[n], d_out[n], m_out[n], v_out[n] = g[None], d[None], m2[None], v2[None]
            after = [d]
    vland = lands[-1]

    def rows8(src):
        return jnp.concatenate([src[n].reshape(1, D) for n in small] + [jnp.ones((1, D), F32)], axis=0)

    g8, d8, m8, v8 = _small_update(vland, rows8(w), rows8(mo), rows8(vo), "small_update")
    for r, n in enumerate(small):
        shp = w[n].shape
        g_out[n], d_out[n], m_out[n], v_out[n] = (a[r].reshape(shp) for a in (g8, d8, m8, v8))
    loss = g8[7, 0]

    return (loss, dx0[None], *[g_out[n] for n in names], *[d_out[n] for n in names],
            *[m_out[n] for n in names], *[v_out[n] for n in names])
```

```python
import numpy as np
import jax
import jax.numpy as jnp
from jax import lax
from jax.experimental import pallas as pl
from jax.experimental.pallas import tpu as pltpu

F32 = jnp.float32
BF16 = jnp.bfloat16

T = 4096
D = 1024
FF = 2816
NDEV = 8
CONV_W = 31
HEAD = 128
BLK = 128
GROUPS = ((128, 1), (512, 4), (2048, 16))
NHG = 4
AW = NHG * HEAD
IN_W = 2 * D + 3 * 3 * AW + 2 * D
EPS = 1e-6
B1, B2, LR, AEPS, WD, STEP = 0.9, 0.999, 0.001, 1e-08, 0.01, 10
NEG = -1e30
VMEM_LIMIT = 56 * 1024 * 1024
MESH_ID = pl.DeviceIdType.MESH

NT = (((1,), (1,)), ((), ()))
NN = (((1,), (0,)), ((), ()))
TN = (((0,), (0,)), ((), ()))
_DIMS = {"nn": NN, "nt": NT, "tn": TN}


def _cp(sem=None):
    return pltpu.CompilerParams(dimension_semantics=sem, vmem_limit_bytes=VMEM_LIMIT)


def _sig(v):
    return 1.0 / (1.0 + jnp.exp(-v))


def _dot(a, b, dims):
    return lax.dot_general(a, b, dims, preferred_element_type=F32)


def _const_spec(shape):
    nd = len(shape)
    return pl.BlockSpec(shape, lambda *_: (0,) * nd)


def _mm(a, b, *, mode, m, n, k, tm, tn, tk, out_dtype, name, a_map=None, b_map=None,
        o_map=None, out_rows=None, init=None, passthru=None):
    gi, gj, gk = m // tm, n // tn, k // tk
    assert gi * tm == m and gj * tn == n and gk * tk == k, (name, m, n, k, tm, tn, tk)
    if mode == "nn":
        a_blk, b_blk = (tm, tk), (tk, tn)
        da, db = (lambda i, j, kk: (i, kk)), (lambda i, j, kk: (kk, j))
    elif mode == "nt":
        a_blk, b_blk = (tm, tk), (tn, tk)
        da, db = (lambda i, j, kk: (i, kk)), (lambda i, j, kk: (j, kk))
    else:
        a_blk, b_blk = (tk, tm), (tk, tn)
        da, db = (lambda i, j, kk: (kk, i)), (lambda i, j, kk: (kk, j))
    a_map = a_map or da
    b_map = b_map or db
    o_map = o_map or (lambda i, j, kk: (i, j))
    dims = _DIMS[mode]
    extra = init if init is not None else passthru
    out_rows = out_rows or m

    def body(*refs):
        if init is not None:
            a_ref, b_ref, i_ref, o_ref, acc = refs
        elif passthru is not None:
            a_ref, b_ref, _, o_ref, acc = refs
        else:
            a_ref, b_ref, o_ref, acc = refs
        kk = pl.program_id(2)

        @pl.when(kk == 0)
        def _():
            if init is not None:
                acc[...] = i_ref[...].astype(F32)
            else:
                acc[...] = jnp.zeros_like(acc)

        acc[...] += _dot(a_ref[...], b_ref[...], dims)

        @pl.when(kk == gk - 1)
        def _():
            o_ref[...] = acc[...].astype(out_dtype)

    in_specs = [pl.BlockSpec(a_blk, a_map), pl.BlockSpec(b_blk, b_map)]
    args = [a, b]
    aliases = {}
    if init is not None:
        in_specs.append(pl.BlockSpec((tm, tn), o_map))
        args.append(init)
        aliases = {2: 0}
    elif passthru is not None:
        in_specs.append(pl.BlockSpec(memory_space=pl.ANY))
        args.append(passthru)
        aliases = {2: 0}
    out_dt = extra.dtype if extra is not None else out_dtype
    assert out_dt == out_dtype
    return pl.pallas_call(
        body, name=name, grid=(gi, gj, gk),
        in_specs=in_specs, out_specs=pl.BlockSpec((tm, tn), o_map),
        out_shape=jax.ShapeDtypeStruct((out_rows, n), out_dtype),
        scratch_shapes=[pltpu.VMEM((tm, tn), F32)],
        input_output_aliases=aliases,
        compiler_params=_cp(("parallel", "parallel", "arbitrary")),
    )(*args)


def _ffn_fwd(x, g, wg_t, wu_t, wd, name):
    tm, fc = 512, 256
    nc = FF // fc

    def body(x_ref, g_ref, wg_ref, wu_ref, wd_ref, xo_ref, gg_ref, uu_ref, acc):
        xv = x_ref[...]
        r = lax.rsqrt(jnp.mean(xv * xv, axis=-1, keepdims=True) + EPS)
        h = (xv * r * g_ref[...]).astype(BF16)
        acc[...] = jnp.zeros_like(acc)
        for c in range(nc):
            sl = pl.ds(c * fc, fc)
            gg = _dot(h, wg_ref[sl, :], NT)
            uu = _dot(h, wu_ref[sl, :], NT)
            gg_ref[:, sl] = gg.astype(BF16)
            uu_ref[:, sl] = uu.astype(BF16)
            act = (gg * _sig(gg) * uu).astype(BF16)
            acc[...] += _dot(act, wd_ref[sl, :], NN)
        xo_ref[...] = xv + 0.5 * acc[...]

    wspec = pl.BlockSpec((FF, D), lambda i: (0, 0), pipeline_mode=pl.Buffered(1))
    return pl.pallas_call(
        body, name=name, grid=(T // tm,),
        in_specs=[pl.BlockSpec((tm, D), lambda i: (i, 0)), _const_spec((1, D)), wspec, wspec, wspec],
        out_specs=[pl.BlockSpec((tm, D), lambda i: (i, 0)), pl.BlockSpec((tm, FF), lambda i: (i, 0)),
                   pl.BlockSpec((tm, FF), lambda i: (i, 0))],
        out_shape=[jax.ShapeDtypeStruct((T, D), F32), jax.ShapeDtypeStruct((T, FF), BF16),
                   jax.ShapeDtypeStruct((T, FF), BF16)],
        scratch_shapes=[pltpu.VMEM((tm, D), F32)],
        compiler_params=_cp(("parallel",)),
    )(x, g, wg_t, wu_t, wd)


def _ffn_bwd(x, g, gg_all, uu_all, dout, wg_t, wu_t, wd, name):
    tm, fc = 256, 256
    nc = FF // fc

    def body(x_ref, g_ref, gg_ref, uu_ref, do_ref, wg_ref, wu_ref, wd_ref,
             dx_ref, dgam_ref, dg_ref, du_ref, act_ref, h_ref, db_ref, acc):
        i = pl.program_id(0)
        xv = x_ref[...]
        r = lax.rsqrt(jnp.mean(xv * xv, axis=-1, keepdims=True) + EPS)
        xhat = xv * r
        gam = g_ref[...]
        h_ref[...] = (xhat * gam).astype(BF16)
        dov = do_ref[...]
        dbv = (0.5 * dov).astype(BF16)
        db_ref[...] = dbv
        acc[...] = jnp.zeros_like(acc)
        for c in range(nc):
            sl = pl.ds(c * fc, fc)
            da = _dot(dbv, wd_ref[sl, :], NT)
            gg = gg_ref[:, sl].astype(F32)
            uu = uu_ref[:, sl].astype(F32)
            s = _sig(gg)
            si = gg * s
            dgv = (da * uu * (s * (1.0 + gg * (1.0 - s)))).astype(BF16)
            duv = (da * si).astype(BF16)
            dg_ref[:, sl] = dgv
            du_ref[:, sl] = duv
            act_ref[:, sl] = (si * uu).astype(BF16)
            acc[...] += _dot(dgv, wg_ref[sl, :], NN) + _dot(duv, wu_ref[sl, :], NN)
        dh = acc[...]

        @pl.when(i == 0)
        def _():
            dgam_ref[...] = jnp.zeros_like(dgam_ref)

        dgam_ref[...] += jnp.sum(dh * xhat, axis=0, keepdims=True)
        dxh = dh * gam
        dx_ref[...] = dov + r * (dxh - xhat * jnp.mean(dxh * xhat, axis=-1, keepdims=True))

    wspec = pl.BlockSpec((FF, D), lambda i: (0, 0), pipeline_mode=pl.Buffered(1))
    row_d = pl.BlockSpec((tm, D), lambda i: (i, 0))
    row_f = pl.BlockSpec((tm, FF), lambda i: (i, 0))
    return pl.pallas_call(
        body, name=name, grid=(T // tm,),
        in_specs=[row_d, _const_spec((1, D)), row_f, row_f, row_d, wspec, wspec, wspec],
        out_specs=[row_d, _const_spec((1, D)), row_f, row_f, row_f, row_d, row_d],
        out_shape=[jax.ShapeDtypeStruct((T, D), F32), jax.ShapeDtypeStruct((1, D), F32),
                   jax.ShapeDtypeStruct((T, FF), BF16), jax.ShapeDtypeStruct((T, FF), BF16),
                   jax.ShapeDtypeStruct((T, FF), BF16), jax.ShapeDtypeStruct((T, D), BF16),
                   jax.ShapeDtypeStruct((T, D), BF16)],
        scratch_shapes=[pltpu.VMEM((tm, D), F32)],
        compiler_params=_cp(("arbitrary",)),
    )(x, g, gg_all, uu_all, dout, wg_t, wu_t, wd)


def _wgrad(a, b, m, n, name, tk=512):
    return _mm(a, b, mode="tn", m=m, n=n, k=T, tm=m, tn=n, tk=tk, out_dtype=BF16, name=name)


def _norm_cast(x, g, name):
    tm = 512

    def body(x_ref, g_ref, h_ref):
        xv = x_ref[...]
        r = lax.rsqrt(jnp.mean(xv * xv, axis=-1, keepdims=True) + EPS)
        h_ref[...] = (xv * r * g_ref[...]).astype(BF16)

    return pl.pallas_call(
        body, name=name, grid=(T // tm,),
        in_specs=[pl.BlockSpec((tm, D), lambda i: (i, 0)), _const_spec((1, D))],
        out_specs=pl.BlockSpec((tm, D), lambda i: (i, 0)),
        out_shape=jax.ShapeDtypeStruct((T, D), BF16),
        compiler_params=_cp(("parallel",)),
    )(x, g)


def _final(x3, gf, tgt, name):
    tm = 512

    def body(x_ref, g_ref, t_ref, dx_ref, dgam_ref, loss_ref):
        i = pl.program_id(0)
        xv = x_ref[...]
        r = lax.rsqrt(jnp.mean(xv * xv, axis=-1, keepdims=True) + EPS)
        xhat = xv * r
        gam = g_ref[...]
        err = xhat * gam - t_ref[...]
        part = 0.5 * jnp.sum(jnp.mean(err * err, axis=-1, keepdims=True), axis=0, keepdims=True)
        dy = err * (1.0 / D)

        @pl.when(i == 0)
        def _():
            dgam_ref[...] = jnp.zeros_like(dgam_ref)
            loss_ref[...] = jnp.zeros_like(loss_ref)

        dgam_ref[...] += jnp.sum(dy * xhat, axis=0, keepdims=True)
        loss_ref[...] += jnp.broadcast_to(part, loss_ref.shape)
        dxh = dy * gam
        dx_ref[...] = r * (dxh - xhat * jnp.mean(dxh * xhat, axis=-1, keepdims=True))

    row_d = pl.BlockSpec((tm, D), lambda i: (i, 0))
    return pl.pallas_call(
        body, name=name, grid=(T // tm,),
        in_specs=[row_d, _const_spec((1, D)), row_d],
        out_specs=[row_d, _const_spec((1, D)), _const_spec((1, 128))],
        out_shape=[jax.ShapeDtypeStruct((T, D), F32), jax.ShapeDtypeStruct((1, D), F32),
                   jax.ShapeDtypeStruct((1, 128), F32)],
        compiler_params=_cp(("arbitrary",)),
    )(x3, gf, tgt)


def _rms_bwd(x, g, dhs, dres, name):
    tm = 512
    nh = len(dhs)

    def body(*refs):
        x_ref, g_ref = refs[:2]
        dh_refs = refs[2:2 + nh]
        dr_ref, dx_ref, dgam_ref = refs[2 + nh:]
        i = pl.program_id(0)
        xv = x_ref[...]
        r = lax.rsqrt(jnp.mean(xv * xv, axis=-1, keepdims=True) + EPS)
        xhat = xv * r
        gam = g_ref[...]
        dh = dh_refs[0][...]
        for ref in dh_refs[1:]:
            dh = dh + ref[...]

        @pl.when(i == 0)
        def _():
            dgam_ref[...] = jnp.zeros_like(dgam_ref)

        dgam_ref[...] += jnp.sum(dh * xhat, axis=0, keepdims=True)
        dxh = dh * gam
        dx_ref[...] = dr_ref[...] + r * (dxh - xhat * jnp.mean(dxh * xhat, axis=-1, keepdims=True))

    row_d = pl.BlockSpec((tm, D), lambda i: (i, 0))
    return pl.pallas_call(
        body, name=name, grid=(T // tm,),
        in_specs=[row_d, _const_spec((1, D))] + [row_d] * nh + [row_d],
        out_specs=[row_d, _const_spec((1, D))],
        out_shape=[jax.ShapeDtypeStruct((T, D), F32), jax.ShapeDtypeStruct((1, D), F32)],
        compiler_params=_cp(("arbitrary",)),
    )(x, g, *dhs, dres)


CONV_TM = 256
CONV_HALO = 32
CONV_RB = 32


def _glu(ab):
    return ab[:, :D] * _sig(ab[:, D:])


def _ln_stats(z1):
    mu = jnp.mean(z1, axis=-1, keepdims=True)
    zc = z1 - mu
    rstd = lax.rsqrt(jnp.mean(zc * zc, axis=-1, keepdims=True) + EPS)
    return zc * rstd, rstd


def _fill_shifts(zs):
    n = zs.shape[1] - 8
    for s in range(1, 8):
        zs[s, pl.ds(0, n), :] = zs[0, pl.ds(s, n), :]


def _shifted(zs, start, rows):
    q, s = divmod(start, 8)
    return zs[s, pl.ds(8 * q, rows), :]


def _conv_fwd(ab, kern, dwb, lng, lnb, name):
    tm, hl, rb = CONV_TM, CONV_HALO, CONV_RB
    off = hl - (CONV_W - 1)

    def body(ab_ref, abh_ref, k_ref, dwb_ref, lng_ref, lnb_ref, z1_ref, z3_ref, zs):
        i = pl.program_id(0)
        zs[0, pl.ds(0, hl), :] = jnp.where(i > 0, _glu(abh_ref[...]), 0.0)
        zs[0, pl.ds(hl, tm), :] = _glu(ab_ref[...])
        _fill_shifts(zs)
        for b in range(tm // rb):
            acc = jnp.zeros((rb, D), F32)
            for j in range(CONV_W):
                acc = acc + _shifted(zs, b * rb + off + j, rb) * k_ref[pl.ds(j, 1), :]
            z1 = acc + dwb_ref[...]
            z1_ref[pl.ds(b * rb, rb), :] = z1
            zn, _ = _ln_stats(z1)
            z2 = zn * lng_ref[...] + lnb_ref[...]
            z3_ref[pl.ds(b * rb, rb), :] = (z2 * _sig(z2)).astype(BF16)

    row = pl.BlockSpec((tm, D), lambda i: (i, 0))
    return pl.pallas_call(
        body, name=name, grid=(T // tm,),
        in_specs=[pl.BlockSpec((tm, 2 * D), lambda i: (i, 0)),
                  pl.BlockSpec((hl, 2 * D), lambda i: (jnp.maximum(i * (tm // hl) - 1, 0), 0)),
                  _const_spec((32, D)), _const_spec((1, D)), _const_spec((1, D)), _const_spec((1, D))],
        out_specs=[row, row],
        out_shape=[jax.ShapeDtypeStruct((T, D), F32), jax.ShapeDtypeStruct((T, D), BF16)],
        scratch_shapes=[pltpu.VMEM((8, hl + tm, D), F32)],
        compiler_params=_cp(("parallel",)),
    )(ab, ab, kern, dwb, lng, lnb)


def _conv_bwd(dz3, z1, ab, kern, lng, lnb, name):
    tm, hl, rb = CONV_TM, CONV_HALO, CONV_RB
    off = hl - (CONV_W - 1)
    nsteps = T // tm

    def ln_bwd(dz3v, z1v, lngv, lnbv):
        zn, rstd = _ln_stats(z1v)
        z2 = zn * lngv + lnbv
        s = _sig(z2)
        dz2 = dz3v * (s * (1.0 + z2 * (1.0 - s)))
        dzn = dz2 * lngv
        dz1 = rstd * (dzn - jnp.mean(dzn, axis=-1, keepdims=True)
                      - zn * jnp.mean(dzn * zn, axis=-1, keepdims=True))
        return dz1, dz2, zn

    def body(dz3_ref, dz3h_ref, z1_ref, z1h_ref, ab_ref, abh_ref, k_ref, lng_ref, lnb_ref,
             dab_ref, dk_ref, dvec_ref, zs, dzs):
        i = pl.program_id(0)
        lngv, lnbv = lng_ref[...], lnb_ref[...]

        @pl.when(i == 0)
        def _():
            dk_ref[...] = jnp.zeros_like(dk_ref)
            dvec_ref[...] = jnp.zeros_like(dvec_ref)

        dz1, dz2, zn = ln_bwd(dz3_ref[...], z1_ref[...], lngv, lnbv)
        dvec_ref[pl.ds(0, 1), :] += jnp.sum(dz1, axis=0, keepdims=True)
        dvec_ref[pl.ds(1, 1), :] += jnp.sum(dz2 * zn, axis=0, keepdims=True)
        dvec_ref[pl.ds(2, 1), :] += jnp.sum(dz2, axis=0, keepdims=True)
        dzs[0, pl.ds(0, tm), :] = dz1
        dz1h, _, _ = ln_bwd(dz3h_ref[...], z1h_ref[...], lngv, lnbv)
        dzs[0, pl.ds(tm, hl), :] = jnp.where(i < nsteps - 1, dz1h, 0.0)
        _fill_shifts(dzs)
        zs[0, pl.ds(0, hl), :] = jnp.where(i > 0, _glu(abh_ref[...]), 0.0)
        zs[0, pl.ds(hl, tm), :] = _glu(ab_ref[...])
        _fill_shifts(zs)

        for j in range(CONV_W):
            tot = jnp.zeros((rb, D), F32)
            for b in range(tm // rb):
                tot = tot + dzs[0, pl.ds(b * rb, rb), :] * _shifted(zs, b * rb + off + j, rb)
            dk_ref[pl.ds(j, 1), :] += jnp.sum(tot, axis=0, keepdims=True)

        for b in range(tm // rb):
            acc = jnp.zeros((rb, D), F32)
            for j in range(CONV_W):
                acc = acc + _shifted(dzs, b * rb + (CONV_W - 1) - j, rb) * k_ref[pl.ds(j, 1), :]
            av = ab_ref[pl.ds(b * rb, rb), pl.ds(0, D)]
            sb = _sig(ab_ref[pl.ds(b * rb, rb), pl.ds(D, D)])
            dab_ref[pl.ds(b * rb, rb), pl.ds(0, D)] = (acc * sb).astype(BF16)
            dab_ref[pl.ds(b * rb, rb), pl.ds(D, D)] = (acc * av * sb * (1.0 - sb)).astype(BF16)

    row = pl.BlockSpec((tm, D), lambda i: (i, 0))
    nxt = pl.BlockSpec((hl, D), lambda i: (jnp.minimum((i + 1) * (tm // hl), T // hl - 1), 0))
    return pl.pallas_call(
        body, name=name, grid=(nsteps,),
        in_specs=[row, nxt, row, nxt,
                  pl.BlockSpec((tm, 2 * D), lambda i: (i, 0)),
                  pl.BlockSpec((hl, 2 * D), lambda i: (jnp.maximum(i * (tm // hl) - 1, 0), 0)),
                  _const_spec((32, D)), _const_spec((1, D)), _const_spec((1, D))],
        out_specs=[pl.BlockSpec((tm, 2 * D), lambda i: (i, 0)), _const_spec((32, D)), _const_spec((8, D))],
        out_shape=[jax.ShapeDtypeStruct((T, 2 * D), BF16), jax.ShapeDtypeStruct((32, D), F32),
                   jax.ShapeDtypeStruct((8, D), F32)],
        scratch_shapes=[pltpu.VMEM((8, hl + tm, D), F32), pltpu.VMEM((8, tm + hl, D), F32)],
        compiler_params=_cp(("arbitrary",)),
    )(dz3, dz3, z1, z1, ab, ab, kern, lng, lnb)


def _alibi_slopes():
    h = np.arange(1, 3 * NHG + 1, dtype=np.float32)
    return np.power(np.float32(2.0), -8.0 * h / np.float32(3 * NHG)).astype(np.float32)


def _band_bias(gi):
    _, dil = GROUPS[gi]
    slopes = _alibi_slopes()[gi * NHG:(gi + 1) * NHG]
    qi = np.arange(BLK)[:, None]
    ki = np.arange(2 * BLK)[None, :]
    steps = BLK + qi - ki
    band = (steps >= 0) & (steps <= BLK)
    bias = -slopes[:, None, None] * (dil * steps).astype(np.float32)[None]
    return jnp.asarray(np.where(band[None], bias, np.float32(NEG)).astype(np.float32))


QB = 4


def _attn_specs():
    prev = lambda n: jnp.maximum(n * QB - 1, 0)
    return [pl.BlockSpec((QB * BLK, HEAD), lambda h, n: (n, h)),
            pl.BlockSpec((BLK, HEAD), lambda h, n: (prev(n), NHG + h)),
            pl.BlockSpec((QB * BLK, HEAD), lambda h, n: (n, NHG + h)),
            pl.BlockSpec((BLK, HEAD), lambda h, n: (prev(n), 2 * NHG + h)),
            pl.BlockSpec((QB * BLK, HEAD), lambda h, n: (n, 2 * NHG + h)),
            pl.BlockSpec((None, BLK, 2 * BLK), lambda h, n: (h, 0, 0))]


def _scores(q, kcat, bias, blk, seg):
    s = _dot(q, kcat, NT) * (HEAD ** -0.5) + bias
    col = lax.broadcasted_iota(jnp.int32, s.shape, 1)
    first = (blk % seg) == 0
    return jnp.where(jnp.logical_and(first, col < BLK), NEG, s)


def _attn_fwd(qkv, gi, name):
    seg = (T // GROUPS[gi][1]) // BLK

    def body(q_ref, kp_ref, kc_ref, vp_ref, vc_ref, bias_ref, o_ref, l_ref):
        n = pl.program_id(1)
        kwin = jnp.concatenate([kp_ref[...], kc_ref[...]], axis=0)
        vwin = jnp.concatenate([vp_ref[...], vc_ref[...]], axis=0)
        bias = bias_ref[...]
        for b in range(QB):
            rows = pl.ds(b * BLK, BLK)
            s = _scores(q_ref[rows, :], kwin[b * BLK:(b + 2) * BLK], bias, n * QB + b, seg)
            mx = jnp.max(s, axis=-1, keepdims=True)
            p = jnp.exp(s - mx)
            den = jnp.sum(p, axis=-1, keepdims=True)
            o_ref[rows, :] = _dot(p.astype(BF16), vwin[b * BLK:(b + 2) * BLK], NN) / den
            l_ref[rows, :] = jnp.broadcast_to(mx + jnp.log(den), (BLK, HEAD))

    oblk = pl.BlockSpec((QB * BLK, HEAD), lambda h, n: (n, h))
    return pl.pallas_call(
        body, name=name, grid=(NHG, T // (QB * BLK)),
        in_specs=_attn_specs(), out_specs=[oblk, oblk],
        out_shape=[jax.ShapeDtypeStruct((T, AW), F32), jax.ShapeDtypeStruct((T, AW), F32)],
        compiler_params=_cp(("parallel", "parallel")),
    )(qkv, qkv, qkv, qkv, qkv, _band_bias(gi))


def _attn_bwd(qkv, dob, lse, delta, gi, name):
    seg = (T // GROUPS[gi][1]) // BLK
    nb = T // (QB * BLK)
    scale = HEAD ** -0.5

    def body(q_ref, kp_ref, kc_ref, vp_ref, vc_ref, bias_ref, do_ref, l_ref, dl_ref, out_ref, dk_acc, dv_acc):
        n = pl.program_id(1)
        kwin = jnp.concatenate([kp_ref[...], kc_ref[...]], axis=0)
        vwin = jnp.concatenate([vp_ref[...], vc_ref[...]], axis=0)
        bias = bias_ref[...]
        for b in range(QB):
            rows = pl.ds(b * BLK, BLK)
            q = q_ref[rows, :]
            kcat = kwin[b * BLK:(b + 2) * BLK]
            s = _scores(q, kcat, bias, n * QB + b, seg)
            p = jnp.exp(s - l_ref[rows, pl.ds(0, 1)])
            dov = do_ref[rows, :]
            dv2 = _dot(p.astype(BF16), dov, TN)
            dp = _dot(dov, vwin[b * BLK:(b + 2) * BLK], NT)
            dsb = (p * (dp - dl_ref[rows, pl.ds(0, 1)]) * scale).astype(BF16)
            row = pl.ds(pl.multiple_of((n * QB + b) * BLK, BLK), BLK)
            out_ref[0, row, :] = _dot(dsb, kcat, NN).astype(BF16)
            dk2 = _dot(dsb, q, TN)
            dk_acc[row, :] = dk2[BLK:]
            dv_acc[row, :] = dv2[BLK:]

            def add_prev(dk2=dk2, dv2=dv2, b=b):
                prow = pl.ds(pl.multiple_of((n * QB + b - 1) * BLK, BLK), BLK)
                dk_acc[prow, :] += dk2[:BLK]
                dv_acc[prow, :] += dv2[:BLK]

            if b == 0:
                pl.when(n > 0)(add_prev)
            else:
                add_prev()

        @pl.when(n == nb - 1)
        def _():
            out_ref[1] = dk_acc[...].astype(BF16)
            out_ref[2] = dv_acc[...].astype(BF16)

    oblk = pl.BlockSpec((QB * BLK, HEAD), lambda h, n: (n, h))
    return pl.pallas_call(
        body, name=name, grid=(NHG, nb),
        in_specs=_attn_specs() + [oblk, oblk, oblk],
        out_specs=pl.BlockSpec((3, T, HEAD), lambda h, n: (0, 0, h)),
        out_shape=jax.ShapeDtypeStruct((3, T, AW), BF16),
        scratch_shapes=[pltpu.VMEM((T, HEAD), F32), pltpu.VMEM((T, HEAD), F32)],
        compiler_params=_cp(("parallel", "arbitrary")),
    )(qkv, qkv, qkv, qkv, qkv, _band_bias(gi), dob, lse, delta)


def _merge(outs, lses, name):
    tm = 512

    def body(o0, l0, o1, l1, o2, l2, a_ref, ab_ref, lse_ref):
        ls = [l0[...], l1[...], l2[...]]
        mx = jnp.maximum(jnp.maximum(ls[0], ls[1]), ls[2])
        es = [jnp.exp(v - mx) for v in ls]
        tot = es[0] + es[1] + es[2]
        att = (es[0] / tot) * o0[...] + (es[1] / tot) * o1[...] + (es[2] / tot) * o2[...]
        a_ref[...] = att
        ab_ref[...] = att.astype(BF16)
        lse_ref[...] = mx + jnp.log(tot)

    row = pl.BlockSpec((tm, AW), lambda i: (i, 0))
    return pl.pallas_call(
        body, name=name, grid=(T // tm,), in_specs=[row] * 6, out_specs=[row] * 3,
        out_shape=[jax.ShapeDtypeStruct((T, AW), F32), jax.ShapeDtypeStruct((T, AW), BF16),
                   jax.ShapeDtypeStruct((T, AW), F32)],
        compiler_params=_cp(("parallel",)),
    )(outs[0], lses[0], outs[1], lses[1], outs[2], lses[2])


def _mix_out(z3b, attnb, gates, wc, wa_t, wo, x1, name):
    tm = 512

    def body(z_ref, a_ref, g_ref, wc_ref, wa_ref, wo_ref, x_ref, xo_ref, yc_ref, ya_ref, mx_ref):
        yc = _dot(z_ref[...], wc_ref[...], NN)
        ya = _dot(a_ref[...], wa_ref[...], NT)
        yc_ref[...] = yc
        ya_ref[...] = ya
        mixed = (_sig(g_ref[:, pl.ds(0, D)]) * yc + _sig(g_ref[:, pl.ds(D, D)]) * ya).astype(BF16)
        mx_ref[...] = mixed
        xo_ref[...] = x_ref[...] + _dot(mixed, wo_ref[...], NN)

    row = pl.BlockSpec((tm, D), lambda i: (i, 0))
    return pl.pallas_call(
        body, name=name, grid=(T // tm,),
        in_specs=[row, pl.BlockSpec((tm, AW), lambda i: (i, 0)), pl.BlockSpec((tm, 2 * D), lambda i: (i, 0)),
                  _const_spec((D, D)), _const_spec((D, AW)), _const_spec((D, D)), row],
        out_specs=[row, row, row, row],
        out_shape=[jax.ShapeDtypeStruct((T, D), F32), jax.ShapeDtypeStruct((T, D), F32),
                   jax.ShapeDtypeStruct((T, D), F32), jax.ShapeDtypeStruct((T, D), BF16)],
        compiler_params=_cp(("parallel",)),
    )(z3b, attnb, gates, wc, wa_t, wo, x1)


def _mix_out_bwd(dx2, gates, yc, ya, attn, wc, wa_t, wo, name):
    tm = 512

    def body(dx_ref, g_ref, yc_ref, ya_ref, at_ref, wc_ref, wa_ref, wo_ref,
             dg_ref, dyc_ref, dya_ref, dxb_ref, dz3_ref, dat_ref, dl_ref):
        dxb = dx_ref[...].astype(BF16)
        dxb_ref[...] = dxb
        dmix = _dot(dxb, wo_ref[...], NT)
        sc = _sig(g_ref[:, pl.ds(0, D)])
        sa = _sig(g_ref[:, pl.ds(D, D)])
        ycv, yav = yc_ref[...], ya_ref[...]
        dg_ref[:, pl.ds(0, D)] = (dmix * ycv * sc * (1.0 - sc)).astype(BF16)
        dg_ref[:, pl.ds(D, D)] = (dmix * yav * sa * (1.0 - sa)).astype(BF16)
        dyc = (dmix * sc).astype(BF16)
        dya = (dmix * sa).astype(BF16)
        dyc_ref[...] = dyc
        dya_ref[...] = dya
        dz3_ref[...] = _dot(dyc, wc_ref[...], NT)
        dat = _dot(dya, wa_ref[...], NN)
        dat_ref[...] = dat.astype(BF16)
        prod = dat * at_ref[...]
        for h in range(NHG):
            sl = pl.ds(h * HEAD, HEAD)
            dl_ref[:, sl] = jnp.broadcast_to(jnp.sum(prod[:, h * HEAD:(h + 1) * HEAD], axis=-1, keepdims=True),
                                             (tm, HEAD))

    row = pl.BlockSpec((tm, D), lambda i: (i, 0))
    row2 = pl.BlockSpec((tm, 2 * D), lambda i: (i, 0))
    rowa = pl.BlockSpec((tm, AW), lambda i: (i, 0))
    return pl.pallas_call(
        body, name=name, grid=(T // tm,),
        in_specs=[row, row2, row, row, rowa, _const_spec((D, D)), _const_spec((D, AW)), _const_spec((D, D))],
        out_specs=[row2, row, row, row, row, rowa, rowa],
        out_shape=[jax.ShapeDtypeStruct((T, 2 * D), BF16), jax.ShapeDtypeStruct((T, D), BF16),
                   jax.ShapeDtypeStruct((T, D), BF16), jax.ShapeDtypeStruct((T, D), BF16),
                   jax.ShapeDtypeStruct((T, D), F32), jax.ShapeDtypeStruct((T, AW), BF16),
                   jax.ShapeDtypeStruct((T, AW), F32)],
        compiler_params=_cp(("parallel",)),
    )(dx2, gates, yc, ya, attn, wc, wa_t, wo)


def _peer(k):
    x, y, c = lax.axis_index("x"), lax.axis_index("y"), lax.axis_index("c")
    px = 1 - x if k & 4 else x
    py = 1 - y if k & 2 else y
    pc = 1 - c if k & 1 else c
    return (px, py, pc), 4 * px + 2 * py + pc


HBM_SPEC = pl.BlockSpec(memory_space=pltpu.HBM)
SEM_SPEC = pl.BlockSpec(memory_space=pltpu.SEMAPHORE)
EFFECT = pltpu.SideEffectType.DATAFLOW_SIDE_EFFECTING


def _my_place():
    return 4 * lax.axis_index("x") + 2 * lax.axis_index("y") + lax.axis_index("c")


def _tie(a, order_after, name):
    na = len(order_after)

    def body(*refs):
        del refs

    return pl.pallas_call(
        body, name=name, in_specs=[pl.BlockSpec(memory_space=pl.ANY)] * (1 + na),
        out_specs=pl.BlockSpec(memory_space=pl.ANY), out_shape=jax.ShapeDtypeStruct(a.shape, a.dtype),
        input_output_aliases={0: 0},
    )(a, *order_after)


def _prep_gather(ws, transposed, order_after, name):
    me = jnp.reshape(_my_place(), (1,)).astype(jnp.int32)
    n = len(ws)
    na = len(order_after)
    shapes = []
    for wv, tr in zip(ws, transposed):
        r, c = (wv.shape[1], wv.shape[0]) if tr else wv.shape
        shapes.append(((32, c), F32) if r == CONV_W else ((r, c), BF16))

    def body(me_ref, *refs):
        del me_ref
        ins, outs = refs[:n], refs[n + na:]
        for wv, tr, i_ref, o_ref in zip(ws, transposed, ins, outs):
            if wv.shape[0] == CONV_W:
                o_ref[pl.ds(0, CONV_W), :] = i_ref[...]
                o_ref[pl.ds(CONV_W, 1), :] = jnp.zeros((1, wv.shape[1]), F32)
            elif tr:
                o_ref[...] = jnp.transpose(i_ref[...]).astype(BF16)
            else:
                o_ref[...] = i_ref[...].astype(BF16)

    grid_spec = pltpu.PrefetchScalarGridSpec(
        num_scalar_prefetch=1, grid=(1,),
        in_specs=[pl.BlockSpec(wv.shape, lambda i, m: (0, 0)) for wv in ws]
        + [pl.BlockSpec(memory_space=pl.ANY)] * na,
        out_specs=[pl.BlockSpec(shp, lambda i, m: (m[0], 0)) for shp, _ in shapes])
    return pl.pallas_call(
        body, name=name, grid_spec=grid_spec,
        out_shape=[jax.ShapeDtypeStruct((NDEV * shp[0], shp[1]), dt) for shp, dt in shapes],
        compiler_params=_cp(("arbitrary",)),
    )(me, *ws, *order_after)


GATHER_A = ((1, 0), (2, 0), (4, 0), (6, 0))
GATHER_B = ((1, 2), (1, 4), (1, 6))


def _gather_start(lands, plan, order_after, name):
    n = len(lands)
    na = len(order_after)
    npl = len(plan)

    def body(*refs):
        land_refs = refs[:n]
        send, recv = refs[n + na], refs[n + na + 1]
        token = refs[-1]
        for w in range(n):
            rows = lands[w].shape[0] // NDEV
            for p, (k, j) in enumerate(plan):
                peer, _ = _peer(k)
                _, blk = _peer(j)
                part = land_refs[w].at[pl.ds(blk * rows, rows)]
                i = w * npl + p
                pltpu.make_async_remote_copy(src_ref=part, dst_ref=part, send_sem=send.at[i], recv_sem=recv.at[i],
                                             device_id=peer, device_id_type=MESH_ID).start()
        token[...] = jnp.zeros_like(token)

    nsem = n * npl
    bufs = [pltpu.with_memory_space_constraint(a, pltpu.HBM) for a in lands]
    out = pl.pallas_call(
        body, name=name,
        in_specs=[HBM_SPEC] * n + [pl.BlockSpec(memory_space=pl.ANY)] * na,
        out_specs=[SEM_SPEC, SEM_SPEC] + [HBM_SPEC] * n + [pl.BlockSpec(memory_space=pltpu.VMEM)],
        out_shape=[pltpu.SemaphoreType.DMA((nsem,)), pltpu.SemaphoreType.DMA((nsem,))]
        + [pltpu.HBM(a.shape, a.dtype) for a in bufs] + [jax.ShapeDtypeStruct((8, 128), F32)],
        input_output_aliases={i: 2 + i for i in range(n)},
        compiler_params=pltpu.CompilerParams(has_side_effects=EFFECT),
    )(*bufs, *order_after)
    return out[0], out[1], out[2:2 + n], out[-1]


def _gather_wait(started, plan, order_after, name):
    send, recv, lands, _ = started
    n = len(lands)
    na = len(order_after)
    npl = len(plan)

    def body(*refs):
        land_refs = refs[:n]
        send_ref, recv_ref = refs[n], refs[n + 1]
        for w in range(n):
            rows = lands[w].shape[0] // NDEV
            for p, (k, j) in enumerate(plan):
                peer, _ = _peer(k)
                _, blk = _peer(j)
                part = land_refs[w].at[pl.ds(blk * rows, rows)]
                i = w * npl + p
                cp = pltpu.make_async_remote_copy(src_ref=part, dst_ref=part, send_sem=send_ref.at[i],
                                                  recv_sem=recv_ref.at[i], device_id=peer, device_id_type=MESH_ID)
                cp.wait_send()
                cp.wait_recv()

    out = pl.pallas_call(
        body, name=name,
        in_specs=[HBM_SPEC] * n + [SEM_SPEC, SEM_SPEC] + [pl.BlockSpec(memory_space=pl.ANY)] * na,
        out_specs=[HBM_SPEC] * n,
        out_shape=[pltpu.HBM(a.shape, a.dtype) for a in lands],
        input_output_aliases={i: i for i in range(n)},
        compiler_params=pltpu.CompilerParams(has_side_effects=EFFECT),
    )(*lands, send, recv, *order_after)
    return list(out)


def _copy_ends(kind, src, land, me, plin, k):
    if kind == "scatter":
        rows = src.shape[0] // NDEV
        return src.at[pl.ds(plin * rows, rows)], land.at[k - 1]
    return src, land.at[me]


def _landing(kind, src):
    me = _my_place()
    if kind == "scatter":
        return lax.empty((NDEV - 1, src.shape[0] // NDEV) + src.shape[1:], src.dtype)
    land = lax.empty((NDEV,) + src.shape, src.dtype)
    return lax.dynamic_update_slice(land, src[None], (me,) + (0,) * src.ndim)


def _send_start(kinds, srcs, order_after, name):
    n = len(srcs)
    lands = [_landing(kd, s) for kd, s in zip(kinds, srcs)]
    na = len(order_after)

    def body(*refs):
        src_refs, land_refs = refs[:n], refs[n:2 * n]
        send, recv = refs[2 * n + na], refs[2 * n + na + 1]
        token = refs[-1]
        _, me = _peer(0)
        for w in range(n):
            for k in range(1, NDEV):
                peer, plin = _peer(k)
                s, d = _copy_ends(kinds[w], src_refs[w], land_refs[w], me, plin, k)
                i = w * (NDEV - 1) + k - 1
                pltpu.make_async_remote_copy(src_ref=s, dst_ref=d, send_sem=send.at[i], recv_sem=recv.at[i],
                                             device_id=peer, device_id_type=MESH_ID).start()
        token[...] = jnp.zeros_like(token)

    nsem = n * (NDEV - 1)
    bufs = [pltpu.with_memory_space_constraint(a, pltpu.HBM) for a in list(srcs) + lands]
    out = pl.pallas_call(
        body, name=name,
        in_specs=[HBM_SPEC] * (2 * n) + [pl.BlockSpec(memory_space=pl.ANY)] * na,
        out_specs=[SEM_SPEC, SEM_SPEC] + [HBM_SPEC] * (2 * n) + [pl.BlockSpec(memory_space=pltpu.VMEM)],
        out_shape=[pltpu.SemaphoreType.DMA((nsem,)), pltpu.SemaphoreType.DMA((nsem,))]
        + [pltpu.HBM(a.shape, a.dtype) for a in bufs] + [jax.ShapeDtypeStruct((8, 128), F32)],
        input_output_aliases={i: 2 + i for i in range(2 * n)},
        compiler_params=pltpu.CompilerParams(has_side_effects=EFFECT),
    )(*bufs, *order_after)
    return out[0], out[1], out[2:2 + n], out[2 + n:2 + 2 * n], out[-1]


def _send_wait(kinds, started, order_after, name):
    send, recv, srcs, lands, _ = started
    n = len(srcs)
    na = len(order_after)

    def body(*refs):
        src_refs, land_refs = refs[:n], refs[n:2 * n]
        send_ref, recv_ref = refs[2 * n], refs[2 * n + 1]
        _, me = _peer(0)
        for w in range(n):
            for k in range(1, NDEV):
                peer, plin = _peer(k)
                s, d = _copy_ends(kinds[w], src_refs[w], land_refs[w], me, plin, k)
                i = w * (NDEV - 1) + k - 1
                cp = pltpu.make_async_remote_copy(src_ref=s, dst_ref=d, send_sem=send_ref.at[i],
                                                  recv_sem=recv_ref.at[i], device_id=peer, device_id_type=MESH_ID)
                cp.wait_send()
                cp.wait_recv()

    bufs = list(srcs) + list(lands)
    out = pl.pallas_call(
        body, name=name,
        in_specs=[HBM_SPEC] * (2 * n) + [SEM_SPEC, SEM_SPEC] + [pl.BlockSpec(memory_space=pl.ANY)] * na,
        out_specs=[HBM_SPEC] * (2 * n),
        out_shape=[pltpu.HBM(a.shape, a.dtype) for a in bufs],
        input_output_aliases={i: i for i in range(2 * n)},
        compiler_params=pltpu.CompilerParams(has_side_effects=EFFECT),
    )(*bufs, send, recv, *order_after)
    return out[:n], out[n:]


def _gsum(own, land, name):
    rows, cols = own.shape
    tr = rows // 2 if rows * cols > 512 * 1024 and rows % 32 == 0 else rows

    def body(own_ref, l_ref, o_ref):
        tot = own_ref[...].astype(F32)
        for s in range(NDEV - 1):
            tot = tot + l_ref[s].astype(F32)
        o_ref[...] = tot

    return pl.pallas_call(
        body, name=name, grid=(rows // tr,),
        in_specs=[pl.BlockSpec((tr, cols), lambda i: (i, 0)),
                  pl.BlockSpec((NDEV - 1, tr, cols), lambda i: (0, i, 0))],
        out_specs=pl.BlockSpec((tr, cols), lambda i: (i, 0)),
        out_shape=jax.ShapeDtypeStruct((rows, cols), F32),
        compiler_params=_cp(("parallel",)),
    )(own, land)


def _adamw_math(w, g, m, v):
    m2 = B1 * m + (1.0 - B1) * g
    v2 = B2 * v + (1.0 - B2) * (g * g)
    m_hat = m2 / (1.0 - B1 ** STEP)
    v_hat = v2 / (1.0 - B2 ** STEP)
    delta = -LR * (m_hat / (jnp.sqrt(v_hat) + AEPS) + WD * w)
    return delta, m2, v2


def _adamw(w, g, m, v, name):
    rows, cols = w.shape
    tr = 256 if rows % 256 == 0 and rows > 256 else rows

    def body(w_ref, g_ref, m_ref, v_ref, d_ref, mo_ref, vo_ref):
        d, m2, v2 = _adamw_math(w_ref[...], g_ref[...], m_ref[...], v_ref[...])
        d_ref[...] = d
        mo_ref[...] = m2
        vo_ref[...] = v2

    blk = pl.BlockSpec((tr, cols), lambda i: (i, 0))
    return pl.pallas_call(
        body, name=name, grid=(rows // tr,), in_specs=[blk] * 4, out_specs=[blk] * 3,
        out_shape=[jax.ShapeDtypeStruct((rows, cols), F32)] * 3,
        compiler_params=_cp(("parallel",)),
    )(w, g, m, v)


def _small_update(vland, w8, m8, v8, name):
    def body(l_ref, w_ref, m_ref, v_ref, g_ref, d_ref, mo_ref, vo_ref):
        g = l_ref[0]
        for s in range(1, NDEV):
            g = g + l_ref[s]
        g_ref[...] = g
        d, m2, v2 = _adamw_math(w_ref[...], g, m_ref[...], v_ref[...])
        d_ref[...] = d
        mo_ref[...] = m2
        vo_ref[...] = v2

    return pl.pallas_call(
        body, name=name, out_shape=[jax.ShapeDtypeStruct((8, D), F32)] * 4,
        compiler_params=_cp(None),
    )(vland, w8, m8, v8)


def _perm(a, dil):
    if dil == 1:
        return a
    t, n = a.shape
    return a.reshape(t // dil, dil, n).transpose(1, 0, 2).reshape(t, n)


def _unperm(a, dil):
    if dil == 1:
        return a
    t, n = a.shape
    return a.reshape(dil, t // dil, n).transpose(1, 0, 2).reshape(t, n)


def kernel(x, ffn1_norm, ffn1_w_gate, ffn1_w_up, ffn1_w_down, mix_norm, w_in, conv_dw_kernel, conv_dw_bias, conv_ln_gain, conv_ln_bias, conv_w_out, attn_w_out, w_o, ffn2_norm, ffn2_w_gate, ffn2_w_up, ffn2_w_down, final_norm, loss_target, m_ffn1_norm, m_ffn1_w_gate, m_ffn1_w_up, m_ffn1_w_down, m_mix_norm, m_w_in, m_conv_dw_kernel, m_conv_dw_bias, m_conv_ln_gain, m_conv_ln_bias, m_conv_w_out, m_attn_w_out, m_w_o, m_ffn2_norm, m_ffn2_w_gate, m_ffn2_w_up, m_ffn2_w_down, m_final_norm, v_ffn1_norm, v_ffn1_w_gate, v_ffn1_w_up, v_ffn1_w_down, v_mix_norm, v_w_in, v_conv_dw_kernel, v_conv_dw_bias, v_conv_ln_gain, v_conv_ln_bias, v_conv_w_out, v_attn_w_out, v_w_o, v_ffn2_norm, v_ffn2_w_gate, v_ffn2_w_up, v_ffn2_w_down, v_final_norm):
    names = ["ffn1_norm", "ffn1_w_gate", "ffn1_w_up", "ffn1_w_down", "mix_norm", "w_in", "conv_dw_kernel",
             "conv_dw_bias", "conv_ln_gain", "conv_ln_bias", "conv_w_out", "attn_w_out", "w_o", "ffn2_norm",
             "ffn2_w_gate", "ffn2_w_up", "ffn2_w_down", "final_norm"]
    w = dict(ffn1_norm=ffn1_norm, ffn1_w_gate=ffn1_w_gate, ffn1_w_up=ffn1_w_up, ffn1_w_down=ffn1_w_down, mix_norm=mix_norm, w_in=w_in, conv_dw_kernel=conv_dw_kernel, conv_dw_bias=conv_dw_bias, conv_ln_gain=conv_ln_gain, conv_ln_bias=conv_ln_bias, conv_w_out=conv_w_out, attn_w_out=attn_w_out, w_o=w_o, ffn2_norm=ffn2_norm, ffn2_w_gate=ffn2_w_gate, ffn2_w_up=ffn2_w_up, ffn2_w_down=ffn2_w_down, final_norm=final_norm)
    mo = dict(ffn1_norm=m_ffn1_norm, ffn1_w_gate=m_ffn1_w_gate, ffn1_w_up=m_ffn1_w_up, ffn1_w_down=m_ffn1_w_down, mix_norm=m_mix_norm, w_in=m_w_in, conv_dw_kernel=m_conv_dw_kernel, conv_dw_bias=m_conv_dw_bias, conv_ln_gain=m_conv_ln_gain, conv_ln_bias=m_conv_ln_bias, conv_w_out=m_conv_w_out, attn_w_out=m_attn_w_out, w_o=m_w_o, ffn2_norm=m_ffn2_norm, ffn2_w_gate=m_ffn2_w_gate, ffn2_w_up=m_ffn2_w_up, ffn2_w_down=m_ffn2_w_down, final_norm=m_final_norm)
    vo = dict(ffn1_norm=v_ffn1_norm, ffn1_w_gate=v_ffn1_w_gate, ffn1_w_up=v_ffn1_w_up, ffn1_w_down=v_ffn1_w_down, mix_norm=v_mix_norm, w_in=v_w_in, conv_dw_kernel=v_conv_dw_kernel, conv_dw_bias=v_conv_dw_bias, conv_ln_gain=v_conv_ln_gain, conv_ln_bias=v_conv_ln_bias, conv_w_out=v_conv_w_out, attn_w_out=v_attn_w_out, w_o=v_w_o, ffn2_norm=v_ffn2_norm, ffn2_w_gate=v_ffn2_w_gate, ffn2_w_up=v_ffn2_w_up, ffn2_w_down=v_ffn2_w_down, final_norm=v_final_norm)
    col_sharded = ("ffn1_w_gate", "ffn1_w_up", "w_in", "attn_w_out", "ffn2_w_gate", "ffn2_w_up")
    row_sharded = ("ffn1_w_down", "conv_w_out", "w_o", "ffn2_w_down")
    small = ("ffn1_norm", "mix_norm", "ffn2_norm", "final_norm", "conv_dw_bias", "conv_ln_gain", "conv_ln_bias")

    ag_groups = (("ffn1_w_gate", "ffn1_w_up", "ffn1_w_down"),
                 ("w_in", "attn_w_out", "conv_w_out", "w_o", "conv_dw_kernel"),
                 ("ffn2_w_gate", "ffn2_w_up", "ffn2_w_down"))
    ag, order = [], []
    for gi, grp in enumerate(ag_groups):
        lands = _prep_gather([w[n][0] for n in grp], [n in col_sharded for n in grp], order, f"gather_prep{gi}")
        st = _gather_start(lands, GATHER_A, [], f"gather_a_start{gi}")
        ag.append(st)
        order = [st[3]]

    def chips_in(gi, after):
        lands = _gather_wait(ag[gi], GATHER_A, after, f"gather_a_wait{gi}")
        return _gather_start(lands, GATHER_B, [], f"gather_b_start{gi}")

    def all_in(gi, st, after):
        return _gather_wait(st, GATHER_B, after, f"gather_b_wait{gi}")

    x0 = x[0]
    tgt = loss_target[0]
    gf = final_norm.reshape(1, D)

    wg1, wu1, wd1 = all_in(0, chips_in(0, [ag[2][3]]), [])
    x1, gg1, uu1 = _ffn_fwd(x0, ffn1_norm, wg1, wu1, wd1, "ffn1_fwd")
    win_t, wa_t, wc, wo, kern_blocks = all_in(1, chips_in(1, [x1]), [])
    kern = kern_blocks.reshape(NDEV, 32, D // NDEV).transpose(1, 0, 2).reshape(32, D)
    h2 = _norm_cast(x1, mix_norm, "mix_norm_fwd")
    ab = _mm(h2, win_t, mode="nt", m=T, n=2 * D, k=D, tm=1024, tn=512, tk=D, out_dtype=F32, name="proj_conv")
    gates = _mm(h2, win_t, mode="nt", m=T, n=2 * D, k=D, tm=1024, tn=512, tk=D, out_dtype=F32,
                b_map=lambda i, j, kk: (13 + j, 0), name="proj_gates")
    h2p, qkv = [], []
    for gi, (_, dil) in enumerate(GROUPS):
        hp = _perm(h2, dil)
        h2p.append(hp)
        qkv.append(_mm(hp, win_t, mode="nt", m=T, n=3 * AW, k=D, tm=1024, tn=AW, tk=D, out_dtype=BF16,
                       b_map=lambda i, j, kk, gi=gi: (4 + gi + 3 * j, 0), name=f"proj_qkv{gi}"))
    z1, z3b = _conv_fwd(ab, kern, conv_dw_bias, conv_ln_gain, conv_ln_bias, "conv_fwd")
    ffn2_b = chips_in(2, [z3b])
    outs, lses = [], []
    for gi, (_, dil) in enumerate(GROUPS):
        o, l = _attn_fwd(qkv[gi], gi, f"attn_fwd{gi}")
        outs.append(_unperm(o, dil))
        lses.append(_unperm(l, dil))
    attn, attnb, lse = _merge(outs, lses, "attn_merge")
    x2, yc, ya, mixedb = _mix_out(z3b, attnb, gates, wc, wa_t, wo, x1, "mix_out_fwd")
    wg2, wu2, wd2 = all_in(2, ffn2_b, [x2])
    x3, gg2, uu2 = _ffn_fwd(x2, ffn2_norm, wg2, wu2, wd2, "ffn2_fwd")

    dx3, dgf, loss_part = _final(x3, gf, tgt, "final_norm_loss")
    dx2, dg3, dgb, dub, actb, hb, dob = _ffn_bwd(x2, ffn2_norm, gg2, uu2, dx3, wg2, wu2, wd2, "ffn2_bwd")
    grads = {}
    grads["ffn2_w_gate"] = _wgrad(dgb, hb, FF, D, "ffn2_dwg")
    grads["ffn2_w_up"] = _wgrad(dub, hb, FF, D, "ffn2_dwu")
    grads["ffn2_w_down"] = _wgrad(actb, dob, FF, D, "ffn2_dwd")
    rs_groups = [("ffn2_w_gate", "ffn2_w_up", "ffn2_w_down"),
                 ("attn_w_out", "conv_w_out", "w_o", "conv_dw_kernel"),
                 ("w_in",),
                 ("ffn1_w_gate",), ("ffn1_w_up",), ("ffn1_w_down",)]
    last = len(rs_groups) - 1
    rs = [_send_start(["scatter"] * 3, [grads[n] for n in rs_groups[0]], [], "scatter_start0")]
    dx2 = _tie(dx2, [rs[0][4]], "tie_after_scatter0")

    dgates, dycb, dyab, dx2b, dz3, dattnb, delta = _mix_out_bwd(dx2, gates, yc, ya, attn, wc, wa_t, wo, "mix_out_bwd")
    grads["w_o"] = _wgrad(mixedb, dx2b, D, D, "dw_o")
    grads["conv_w_out"] = _wgrad(z3b, dycb, D, D, "dw_conv_out")
    grads["attn_w_out"] = _wgrad(dyab, attnb, D, AW, "dw_attn_out")
    dab, dkern, dvec = _conv_bwd(dz3, z1, ab, kern, conv_ln_gain, conv_ln_bias, "conv_bwd")
    grads["conv_dw_kernel"] = dkern.reshape(32, NDEV, D // NDEV).transpose(1, 0, 2).reshape(NDEV * 32, D // NDEV)
    rs.append(_send_start(["scatter"] * 4, [grads[n] for n in rs_groups[1]], [rs[0][4]], "scatter_start1"))
    dattnb = _tie(dattnb, [rs[1][4]], "tie_after_scatter1")

    dqkv = []
    for gi, (_, dil) in enumerate(GROUPS):
        dq3 = _attn_bwd(qkv[gi], _perm(dattnb, dil), _perm(lse, dil), _perm(delta, dil), gi, f"attn_bwd{gi}")
        dqkv.append(dq3.reshape(3 * T, AW))

    dwin = _mm(dab, h2, mode="tn", m=2 * D, n=D, k=T, tm=2 * D, tn=D, tk=512, out_dtype=BF16, out_rows=IN_W,
               name="dw_in_conv")
    dwin = _mm(dgates, h2, mode="tn", m=2 * D, n=D, k=T, tm=512, tn=D, tk=1024, out_dtype=BF16, out_rows=IN_W,
               o_map=lambda i, j, kk: (13 + i, 0), passthru=dwin, name="dw_in_gates")
    for gi in range(3):
        dwin = _mm(dqkv[gi], h2p[gi], mode="tn", m=3 * AW, n=D, k=T, tm=AW, tn=D, tk=1024, out_dtype=BF16,
                   out_rows=IN_W, a_map=lambda i, j, kk: (i * (T // 1024) + kk, 0),
                   o_map=lambda i, j, kk, gi=gi: (4 + gi + 3 * i, 0), passthru=dwin, name=f"dw_in_qkv{gi}")
    grads["w_in"] = dwin
    rs.append(_send_start(["scatter"], [dwin], [rs[1][4]], "scatter_start2"))
    dab = _tie(dab, [rs[2][4]], "tie_after_scatter2")

    nrow = T // 1024
    dh = _mm(dab, win_t, mode="nn", m=T, n=D, k=2 * D, tm=1024, tn=D, tk=512, out_dtype=F32, name="dproj_conv")
    dh = _mm(dgates, win_t, mode="nn", m=T, n=D, k=2 * D, tm=1024, tn=D, tk=512, out_dtype=F32,
             b_map=lambda i, j, kk: (13 + kk, 0), init=dh, name="dproj_gates")
    dhs = []
    for gi, (_, dil) in enumerate(GROUPS):
        part = _mm(dqkv[gi], win_t, mode="nn", m=T, n=D, k=3 * AW, tm=1024, tn=D, tk=AW, out_dtype=F32,
                   a_map=lambda i, j, kk: (kk * nrow + i, 0), b_map=lambda i, j, kk, gi=gi: (4 + gi + 3 * kk, 0),
                   init=dh if gi == 0 else None, name=f"dproj_qkv{gi}")
        dhs.append(_unperm(part, dil))
    dx1, dg2 = _rms_bwd(x1, mix_norm, dhs, dx2, "mix_norm_bwd")

    dx0, dg1, dgb, dub, actb, hb, dob = _ffn_bwd(x0, ffn1_norm, gg1, uu1, dx1, wg1, wu1, wd1, "ffn1_bwd")
    grads["ffn1_w_gate"] = _wgrad(dgb, hb, FF, D, "ffn1_dwg")
    rs.append(_send_start(["scatter"], [grads["ffn1_w_gate"]], [rs[2][4]], "scatter_start3"))
    hb = _tie(hb, [rs[3][4]], "tie_after_scatter3")
    grads["ffn1_w_up"] = _wgrad(dub, hb, FF, D, "ffn1_dwu")
    rs.append(_send_start(["scatter"], [grads["ffn1_w_up"]], [rs[3][4]], "scatter_start4"))
    dob = _tie(dob, [rs[4][4]], "tie_after_scatter4")
    grads["ffn1_w_down"] = _wgrad(actb, dob, FF, D, "ffn1_dwd")
    vec = jnp.concatenate([dg1, dg2, dg3, dgf, dvec[0:3], jnp.broadcast_to(loss_part[:, :1], (1, D))], axis=0)
    rs.append(_send_start(["scatter", "bcast"], [grads["ffn1_w_down"], vec], [rs[4][4]], "scatter_start5"))

    g_out, d_out, m_out, v_out = {}, {}, {}, {}
    me = _my_place()
    after = [rs[last][4]]
    for gi, grp in enumerate(rs_groups):
        kinds = ["scatter"] * len(grp) + (["bcast"] if gi == last else [])
        srcs, lands = _send_wait(kinds, rs[gi], after, f"scatter_wait{gi}")
        for n, src, land in zip(grp, srcs, lands):
            rows = src.shape[0] // NDEV
            own = lax.dynamic_slice(src, (me * rows, 0), (rows, src.shape[1]))
            g = _gsum(own, land, f"gsum_{n}")
            if n in col_sharded:
                g = jnp.transpose(g)
            elif n == "conv_dw_kernel":
                g = g[:CONV_W]
            d, m2, v2 = _adamw(w[n][0], g, mo[n][0], vo[n][0], f"adamw_{n}")
            g_out[n], d_out[n], m_out[n], v_out[n] = g[None], d[None], m2[None], v2[None]
            after = [d]
    vland = lands[-1]

    def rows8(src):
        return jnp.concatenate([src[n].reshape(1, D) for n in small] + [jnp.ones((1, D), F32)], axis=0)

    g8, d8, m8, v8 = _small_update(vland, rows8(w), rows8(mo), rows8(vo), "small_update")
    for r, n in enumerate(small):
        shp = w[n].shape
        g_out[n], d_out[n], m_out[n], v_out[n] = (a[r].reshape(shp) for a in (g8, d8, m8, v8))
    loss = g8[7, 0]

    return (loss, dx0[None], *[g_out[n] for n in names], *[d_out[n] for n in names],
            *[m_out[n] for n in names], *[v_out[n] for n in names])
```

```python
import numpy as np
import jax
import jax.numpy as jnp
from jax import lax
from jax.experimental import pallas as pl
from jax.experimental.pallas import tpu as pltpu

F32 = jnp.float32
BF16 = jnp.bfloat16

T = 4096
D = 1024
FF = 2816
NDEV = 8
CONV_W = 31
HEAD = 128
BLK = 128
GROUPS = ((128, 1), (512, 4), (2048, 16))
NHG = 4
AW = NHG * HEAD
IN_W = 2 * D + 3 * 3 * AW + 2 * D
EPS = 1e-6
B1, B2, LR, AEPS, WD, STEP = 0.9, 0.999, 0.001, 1e-08, 0.01, 10
NEG = -1e30
VMEM_LIMIT = 56 * 1024 * 1024
MESH_ID = pl.DeviceIdType.MESH

NT = (((1,), (1,)), ((), ()))
NN = (((1,), (0,)), ((), ()))
TN = (((0,), (0,)), ((), ()))
_DIMS = {"nn": NN, "nt": NT, "tn": TN}


def _cp(sem=None):
    return pltpu.CompilerParams(dimension_semantics=sem, vmem_limit_bytes=VMEM_LIMIT)


def _sig(v):
    return 1.0 / (1.0 + jnp.exp(-v))


def _dot(a, b, dims):
    return lax.dot_general(a, b, dims, preferred_element_type=F32)


def _const_spec(shape):
    nd = len(shape)
    return pl.BlockSpec(shape, lambda *_: (0,) * nd)


def _mm(a, b, *, mode, m, n, k, tm, tn, tk, out_dtype, name, a_map=None, b_map=None,
        o_map=None, out_rows=None, init=None, passthru=None):
    gi, gj, gk = m // tm, n // tn, k // tk
    assert gi * tm == m and gj * tn == n and gk * tk == k, (name, m, n, k, tm, tn, tk)
    if mode == "nn":
        a_blk, b_blk = (tm, tk), (tk, tn)
        da, db = (lambda i, j, kk: (i, kk)), (lambda i, j, kk: (kk, j))
    elif mode == "nt":
        a_blk, b_blk = (tm, tk), (tn, tk)
        da, db = (lambda i, j, kk: (i, kk)), (lambda i, j, kk: (j, kk))
    else:
        a_blk, b_blk = (tk, tm), (tk, tn)
        da, db = (lambda i, j, kk: (kk, i)), (lambda i, j, kk: (kk, j))
    a_map = a_map or da
    b_map = b_map or db
    o_map = o_map or (lambda i, j, kk: (i, j))
    dims = _DIMS[mode]
    extra = init if init is not None else passthru
    out_rows = out_rows or m

    def body(*refs):
        if init is not None:
            a_ref, b_ref, i_ref, o_ref = refs[:4]
        elif passthru is not None:
            a_ref, b_ref, _, o_ref = refs[:4]
        else:
            a_ref, b_ref, o_ref = refs[:3]
        if gk == 1:
            prod = _dot(a_ref[...], b_ref[...], dims)
            if init is not None:
                prod = prod + i_ref[...].astype(F32)
            o_ref[...] = prod.astype(out_dtype)
            return
        acc = refs[-1]
        kk = pl.program_id(2)

        @pl.when(kk == 0)
        def _():
            if init is not None:
                acc[...] = i_ref[...].astype(F32)
            else:
                acc[...] = jnp.zeros_like(acc)

        acc[...] += _dot(a_ref[...], b_ref[...], dims)

        @pl.when(kk == gk - 1)
        def _():
            o_ref[...] = acc[...].astype(out_dtype)

    in_specs = [pl.BlockSpec(a_blk, a_map), pl.BlockSpec(b_blk, b_map)]
    args = [a, b]
    aliases = {}
    if init is not None:
        in_specs.append(pl.BlockSpec((tm, tn), o_map))
        args.append(init)
        aliases = {2: 0}
    elif passthru is not None:
        in_specs.append(pl.BlockSpec(memory_space=pl.ANY))
        args.append(passthru)
        aliases = {2: 0}
    out_dt = extra.dtype if extra is not None else out_dtype
    assert out_dt == out_dtype
    return pl.pallas_call(
        body, name=name, grid=(gi, gj, gk),
        in_specs=in_specs, out_specs=pl.BlockSpec((tm, tn), o_map),
        out_shape=jax.ShapeDtypeStruct((out_rows, n), out_dtype),
        scratch_shapes=[pltpu.VMEM((tm, tn), F32)] if gk > 1 else [],
        input_output_aliases=aliases,
        compiler_params=_cp(("parallel", "parallel", "arbitrary")),
    )(*args)


def _ffn_fwd(x, g, wg_t, wu_t, wd, name):
    tm, fc = 512, 256
    nc = FF // fc

    def body(x_ref, g_ref, wg_ref, wu_ref, wd_ref, xo_ref, gg_ref, uu_ref, act_ref):
        xv = x_ref[...]
        r = lax.rsqrt(jnp.mean(xv * xv, axis=-1, keepdims=True) + EPS)
        h = (xv * r * g_ref[...]).astype(BF16)
        for c in range(nc):
            sl = pl.ds(c * fc, fc)
            gg = _dot(h, wg_ref[sl, :], NT)
            uu = _dot(h, wu_ref[sl, :], NT)
            gg_ref[:, sl] = gg.astype(BF16)
            uu_ref[:, sl] = uu.astype(BF16)
            act_ref[:, sl] = (gg * _sig(gg) * uu).astype(BF16)
        xo_ref[...] = xv + 0.5 * _dot(act_ref[...], wd_ref[...], NN)

    wspec = pl.BlockSpec((FF, D), lambda i: (0, 0), pipeline_mode=pl.Buffered(1))
    return pl.pallas_call(
        body, name=name, grid=(T // tm,),
        in_specs=[pl.BlockSpec((tm, D), lambda i: (i, 0)), _const_spec((1, D)), wspec, wspec, wspec],
        out_specs=[pl.BlockSpec((tm, D), lambda i: (i, 0)), pl.BlockSpec((tm, FF), lambda i: (i, 0)),
                   pl.BlockSpec((tm, FF), lambda i: (i, 0))],
        out_shape=[jax.ShapeDtypeStruct((T, D), F32), jax.ShapeDtypeStruct((T, FF), BF16),
                   jax.ShapeDtypeStruct((T, FF), BF16)],
        scratch_shapes=[pltpu.VMEM((tm, FF), BF16)],
        compiler_params=_cp(("parallel",)),
    )(x, g, wg_t, wu_t, wd)


def _ffn_bwd(x, g, gg_all, uu_all, dout, wg_t, wu_t, wd, name):
    tm, fc = 256, 256
    nc = FF // fc

    def body(x_ref, g_ref, gg_ref, uu_ref, do_ref, wg_ref, wu_ref, wd_ref,
             dx_ref, dgam_ref, dg_ref, du_ref, act_ref, h_ref, db_ref):
        i = pl.program_id(0)
        xv = x_ref[...]
        r = lax.rsqrt(jnp.mean(xv * xv, axis=-1, keepdims=True) + EPS)
        xhat = xv * r
        gam = g_ref[...]
        h_ref[...] = (xhat * gam).astype(BF16)
        dov = do_ref[...]
        dbv = (0.5 * dov).astype(BF16)
        db_ref[...] = dbv
        for c in range(nc):
            sl = pl.ds(c * fc, fc)
            da = _dot(dbv, wd_ref[sl, :], NT)
            gg = gg_ref[:, sl].astype(F32)
            uu = uu_ref[:, sl].astype(F32)
            s = _sig(gg)
            si = gg * s
            dgv = (da * uu * (s * (1.0 + gg * (1.0 - s)))).astype(BF16)
            duv = (da * si).astype(BF16)
            dg_ref[:, sl] = dgv
            du_ref[:, sl] = duv
            act_ref[:, sl] = (si * uu).astype(BF16)
        dh = _dot(dg_ref[...], wg_ref[...], NN) + _dot(du_ref[...], wu_ref[...], NN)

        @pl.when(i == 0)
        def _():
            dgam_ref[...] = jnp.zeros_like(dgam_ref)

        dgam_ref[...] += jnp.sum(dh * xhat, axis=0, keepdims=True)
        dxh = dh * gam
        dx_ref[...] = dov + r * (dxh - xhat * jnp.mean(dxh * xhat, axis=-1, keepdims=True))

    wspec = pl.BlockSpec((FF, D), lambda i: (0, 0), pipeline_mode=pl.Buffered(1))
    row_d = pl.BlockSpec((tm, D), lambda i: (i, 0))
    row_f = pl.BlockSpec((tm, FF), lambda i: (i, 0))
    return pl.pallas_call(
        body, name=name, grid=(T // tm,),
        in_specs=[row_d, _const_spec((1, D)), row_f, row_f, row_d, wspec, wspec, wspec],
        out_specs=[row_d, _const_spec((1, D)), row_f, row_f, row_f, row_d, row_d],
        out_shape=[jax.ShapeDtypeStruct((T, D), F32), jax.ShapeDtypeStruct((1, D), F32),
                   jax.ShapeDtypeStruct((T, FF), BF16), jax.ShapeDtypeStruct((T, FF), BF16),
                   jax.ShapeDtypeStruct((T, FF), BF16), jax.ShapeDtypeStruct((T, D), BF16),
                   jax.ShapeDtypeStruct((T, D), BF16)],
        compiler_params=_cp(("arbitrary",)),
    )(x, g, gg_all, uu_all, dout, wg_t, wu_t, wd)


def _wgrad(a, b, m, n, name):
    tm = m // 2 if m == FF else m
    return _mm(a, b, mode="tn", m=m, n=n, k=T, tm=tm, tn=n, tk=min(T, 2048), out_dtype=BF16, name=name)


def _norm_cast(x, g, name):
    tm = 512

    def body(x_ref, g_ref, h_ref):
        xv = x_ref[...]
        r = lax.rsqrt(jnp.mean(xv * xv, axis=-1, keepdims=True) + EPS)
        h_ref[...] = (xv * r * g_ref[...]).astype(BF16)

    return pl.pallas_call(
        body, name=name, grid=(T // tm,),
        in_specs=[pl.BlockSpec((tm, D), lambda i: (i, 0)), _const_spec((1, D))],
        out_specs=pl.BlockSpec((tm, D), lambda i: (i, 0)),
        out_shape=jax.ShapeDtypeStruct((T, D), BF16),
        compiler_params=_cp(("parallel",)),
    )(x, g)


def _final(x3, gf, tgt, name):
    tm = 512

    def body(x_ref, g_ref, t_ref, dx_ref, dgam_ref, loss_ref):
        i = pl.program_id(0)
        xv = x_ref[...]
        r = lax.rsqrt(jnp.mean(xv * xv, axis=-1, keepdims=True) + EPS)
        xhat = xv * r
        gam = g_ref[...]
        err = xhat * gam - t_ref[...]
        part = 0.5 * jnp.sum(jnp.mean(err * err, axis=-1, keepdims=True), axis=0, keepdims=True)
        dy = err * (1.0 / D)

        @pl.when(i == 0)
        def _():
            dgam_ref[...] = jnp.zeros_like(dgam_ref)
            loss_ref[...] = jnp.zeros_like(loss_ref)

        dgam_ref[...] += jnp.sum(dy * xhat, axis=0, keepdims=True)
        loss_ref[...] += jnp.broadcast_to(part, loss_ref.shape)
        dxh = dy * gam
        dx_ref[...] = r * (dxh - xhat * jnp.mean(dxh * xhat, axis=-1, keepdims=True))

    row_d = pl.BlockSpec((tm, D), lambda i: (i, 0))
    return pl.pallas_call(
        body, name=name, grid=(T // tm,),
        in_specs=[row_d, _const_spec((1, D)), row_d],
        out_specs=[row_d, _const_spec((1, D)), _const_spec((1, 128))],
        out_shape=[jax.ShapeDtypeStruct((T, D), F32), jax.ShapeDtypeStruct((1, D), F32),
                   jax.ShapeDtypeStruct((1, 128), F32)],
        compiler_params=_cp(("arbitrary",)),
    )(x3, gf, tgt)


def _rms_bwd(x, g, dhs, dres, name):
    tm = 512
    nh = len(dhs)

    def body(*refs):
        x_ref, g_ref = refs[:2]
        dh_refs = refs[2:2 + nh]
        dr_ref, dx_ref, dgam_ref = refs[2 + nh:]
        i = pl.program_id(0)
        xv = x_ref[...]
        r = lax.rsqrt(jnp.mean(xv * xv, axis=-1, keepdims=True) + EPS)
        xhat = xv * r
        gam = g_ref[...]
        dh = dh_refs[0][...]
        for ref in dh_refs[1:]:
            dh = dh + ref[...]

        @pl.when(i == 0)
        def _():
            dgam_ref[...] = jnp.zeros_like(dgam_ref)

        dgam_ref[...] += jnp.sum(dh * xhat, axis=0, keepdims=True)
        dxh = dh * gam
        dx_ref[...] = dr_ref[...] + r * (dxh - xhat * jnp.mean(dxh * xhat, axis=-1, keepdims=True))

    row_d = pl.BlockSpec((tm, D), lambda i: (i, 0))
    return pl.pallas_call(
        body, name=name, grid=(T // tm,),
        in_specs=[row_d, _const_spec((1, D))] + [row_d] * nh + [row_d],
        out_specs=[row_d, _const_spec((1, D))],
        out_shape=[jax.ShapeDtypeStruct((T, D), F32), jax.ShapeDtypeStruct((1, D), F32)],
        compiler_params=_cp(("arbitrary",)),
    )(x, g, *dhs, dres)


CONV_TM = 256
CONV_HALO = 32
CONV_RB = 32


def _glu(ab):
    return ab[:, :D] * _sig(ab[:, D:])


def _ln_stats(z1):
    mu = jnp.mean(z1, axis=-1, keepdims=True)
    zc = z1 - mu
    rstd = lax.rsqrt(jnp.mean(zc * zc, axis=-1, keepdims=True) + EPS)
    return zc * rstd, rstd


def _fill_shifts(zs):
    n = zs.shape[1] - 8
    for s in range(1, 8):
        zs[s, pl.ds(0, n), :] = zs[0, pl.ds(s, n), :]


def _shifted(zs, start, rows):
    q, s = divmod(start, 8)
    return zs[s, pl.ds(8 * q, rows), :]


def _conv_fwd(ab, kern, dwb, lng, lnb, name):
    tm, hl, rb = CONV_TM, CONV_HALO, CONV_RB
    off = hl - (CONV_W - 1)

    def body(ab_ref, abh_ref, k_ref, dwb_ref, lng_ref, lnb_ref, z1_ref, z3_ref, zs):
        i = pl.program_id(0)
        zs[0, pl.ds(0, hl), :] = jnp.where(i > 0, _glu(abh_ref[...]), 0.0)
        zs[0, pl.ds(hl, tm), :] = _glu(ab_ref[...])
        _fill_shifts(zs)
        for b in range(tm // rb):
            acc = jnp.zeros((rb, D), F32)
            for j in range(CONV_W):
                acc = acc + _shifted(zs, b * rb + off + j, rb) * k_ref[pl.ds(j, 1), :]
            z1 = acc + dwb_ref[...]
            z1_ref[pl.ds(b * rb, rb), :] = z1
            zn, _ = _ln_stats(z1)
            z2 = zn * lng_ref[...] + lnb_ref[...]
            z3_ref[pl.ds(b * rb, rb), :] = (z2 * _sig(z2)).astype(BF16)

    row = pl.BlockSpec((tm, D), lambda i: (i, 0))
    return pl.pallas_call(
        body, name=name, grid=(T // tm,),
        in_specs=[pl.BlockSpec((tm, 2 * D), lambda i: (i, 0)),
                  pl.BlockSpec((hl, 2 * D), lambda i: (jnp.maximum(i * (tm // hl) - 1, 0), 0)),
                  _const_spec((32, D)), _const_spec((1, D)), _const_spec((1, D)), _const_spec((1, D))],
        out_specs=[row, row],
        out_shape=[jax.ShapeDtypeStruct((T, D), F32), jax.ShapeDtypeStruct((T, D), BF16)],
        scratch_shapes=[pltpu.VMEM((8, hl + tm, D), F32)],
        compiler_params=_cp(("parallel",)),
    )(ab, ab, kern, dwb, lng, lnb)


def _conv_bwd(dz3, z1, ab, kern, lng, lnb, name):
    tm, hl, rb = CONV_TM, CONV_HALO, CONV_RB
    off = hl - (CONV_W - 1)
    nsteps = T // tm

    def ln_bwd(dz3v, z1v, lngv, lnbv):
        zn, rstd = _ln_stats(z1v)
        z2 = zn * lngv + lnbv
        s = _sig(z2)
        dz2 = dz3v * (s * (1.0 + z2 * (1.0 - s)))
        dzn = dz2 * lngv
        dz1 = rstd * (dzn - jnp.mean(dzn, axis=-1, keepdims=True)
                      - zn * jnp.mean(dzn * zn, axis=-1, keepdims=True))
        return dz1, dz2, zn

    def body(dz3_ref, dz3h_ref, z1_ref, z1h_ref, ab_ref, abh_ref, k_ref, lng_ref, lnb_ref,
             dab_ref, dk_ref, dvec_ref, zs, dzs):
        i = pl.program_id(0)
        lngv, lnbv = lng_ref[...], lnb_ref[...]

        @pl.when(i == 0)
        def _():
            dk_ref[...] = jnp.zeros_like(dk_ref)
            dvec_ref[...] = jnp.zeros_like(dvec_ref)

        dz1, dz2, zn = ln_bwd(dz3_ref[...], z1_ref[...], lngv, lnbv)
        dvec_ref[pl.ds(0, 1), :] += jnp.sum(dz1, axis=0, keepdims=True)
        dvec_ref[pl.ds(1, 1), :] += jnp.sum(dz2 * zn, axis=0, keepdims=True)
        dvec_ref[pl.ds(2, 1), :] += jnp.sum(dz2, axis=0, keepdims=True)
        dzs[0, pl.ds(0, tm), :] = dz1
        dz1h, _, _ = ln_bwd(dz3h_ref[...], z1h_ref[...], lngv, lnbv)
        dzs[0, pl.ds(tm, hl), :] = jnp.where(i < nsteps - 1, dz1h, 0.0)
        _fill_shifts(dzs)
        zs[0, pl.ds(0, hl), :] = jnp.where(i > 0, _glu(abh_ref[...]), 0.0)
        zs[0, pl.ds(hl, tm), :] = _glu(ab_ref[...])
        _fill_shifts(zs)

        for j in range(CONV_W):
            tot = jnp.zeros((rb, D), F32)
            for b in range(tm // rb):
                tot = tot + dzs[0, pl.ds(b * rb, rb), :] * _shifted(zs, b * rb + off + j, rb)
            dk_ref[pl.ds(j, 1), :] += jnp.sum(tot, axis=0, keepdims=True)

        for b in range(tm // rb):
            acc = jnp.zeros((rb, D), F32)
            for j in range(CONV_W):
                acc = acc + _shifted(dzs, b * rb + (CONV_W - 1) - j, rb) * k_ref[pl.ds(j, 1), :]
            av = ab_ref[pl.ds(b * rb, rb), pl.ds(0, D)]
            sb = _sig(ab_ref[pl.ds(b * rb, rb), pl.ds(D, D)])
            dab_ref[pl.ds(b * rb, rb), pl.ds(0, D)] = (acc * sb).astype(BF16)
            dab_ref[pl.ds(b * rb, rb), pl.ds(D, D)] = (acc * av * sb * (1.0 - sb)).astype(BF16)

    row = pl.BlockSpec((tm, D), lambda i: (i, 0))
    nxt = pl.BlockSpec((hl, D), lambda i: (jnp.minimum((i + 1) * (tm // hl), T // hl - 1), 0))
    return pl.pallas_call(
        body, name=name, grid=(nsteps,),
        in_specs=[row, nxt, row, nxt,
                  pl.BlockSpec((tm, 2 * D), lambda i: (i, 0)),
                  pl.BlockSpec((hl, 2 * D), lambda i: (jnp.maximum(i * (tm // hl) - 1, 0), 0)),
                  _const_spec((32, D)), _const_spec((1, D)), _const_spec((1, D))],
        out_specs=[pl.BlockSpec((tm, 2 * D), lambda i: (i, 0)), _const_spec((32, D)), _const_spec((8, D))],
        out_shape=[jax.ShapeDtypeStruct((T, 2 * D), BF16), jax.ShapeDtypeStruct((32, D), F32),
                   jax.ShapeDtypeStruct((8, D), F32)],
        scratch_shapes=[pltpu.VMEM((8, hl + tm, D), F32), pltpu.VMEM((8, tm + hl, D), F32)],
        compiler_params=_cp(("arbitrary",)),
    )(dz3, dz3, z1, z1, ab, ab, kern, lng, lnb)


def _alibi_slopes():
    h = np.arange(1, 3 * NHG + 1, dtype=np.float32)
    return np.power(np.float32(2.0), -8.0 * h / np.float32(3 * NHG)).astype(np.float32)


def _band_bias(gi):
    _, dil = GROUPS[gi]
    slopes = _alibi_slopes()[gi * NHG:(gi + 1) * NHG]
    qi = np.arange(BLK)[:, None]
    ki = np.arange(2 * BLK)[None, :]
    steps = BLK + qi - ki
    band = (steps >= 0) & (steps <= BLK)
    bias = -slopes[:, None, None] * (dil * steps).astype(np.float32)[None]
    return jnp.asarray(np.where(band[None], bias, np.float32(NEG)).astype(np.float32))


QB_FWD = 4
QB_BWD = 8


def _attn_specs(qb):
    prev = lambda n: jnp.maximum(n * qb - 1, 0)
    return [pl.BlockSpec((qb * BLK, HEAD), lambda h, n: (n, h)),
            pl.BlockSpec((BLK, HEAD), lambda h, n: (prev(n), NHG + h)),
            pl.BlockSpec((qb * BLK, HEAD), lambda h, n: (n, NHG + h)),
            pl.BlockSpec((BLK, HEAD), lambda h, n: (prev(n), 2 * NHG + h)),
            pl.BlockSpec((qb * BLK, HEAD), lambda h, n: (n, 2 * NHG + h)),
            pl.BlockSpec((None, BLK, 2 * BLK), lambda h, n: (h, 0, 0))]


def _scores(q, kcat, bias, blk, seg):
    s = _dot(q, kcat, NT) * (HEAD ** -0.5) + bias
    col = lax.broadcasted_iota(jnp.int32, s.shape, 1)
    first = (blk % seg) == 0
    return jnp.where(jnp.logical_and(first, col < BLK), NEG, s)


def _attn_fwd(qkv, gi, name):
    seg = (T // GROUPS[gi][1]) // BLK

    qb = QB_FWD

    def body(q_ref, kp_ref, kc_ref, vp_ref, vc_ref, bias_ref, o_ref, l_ref):
        n = pl.program_id(0)
        for h in range(NHG):
            cols = pl.ds(h * HEAD, HEAD)
            kwin = jnp.concatenate([kp_ref[:, cols], kc_ref[:, cols]], axis=0)
            vwin = jnp.concatenate([vp_ref[:, cols], vc_ref[:, cols]], axis=0)
            bias = bias_ref[h]
            for b in range(qb):
                rows = pl.ds(b * BLK, BLK)
                s = _scores(q_ref[rows, cols], kwin[b * BLK:(b + 2) * BLK], bias, n * qb + b, seg)
                mx = jnp.max(s, axis=-1, keepdims=True)
                p = jnp.exp(s - mx)
                den = jnp.sum(p, axis=-1, keepdims=True)
                o_ref[rows, cols] = _dot(p.astype(BF16), vwin[b * BLK:(b + 2) * BLK], NN) / den
                l_ref[rows, cols] = jnp.broadcast_to(mx + jnp.log(den), (BLK, HEAD))

    prev = lambda n: jnp.maximum(n * qb - 1, 0)
    cur = lambda part: pl.BlockSpec((qb * BLK, AW), lambda n: (n, part))
    halo = lambda part: pl.BlockSpec((BLK, AW), lambda n: (prev(n), part))
    return pl.pallas_call(
        body, name=name, grid=(T // (qb * BLK),),
        in_specs=[cur(0), halo(1), cur(1), halo(2), cur(2), _const_spec((NHG, BLK, 2 * BLK))],
        out_specs=[cur(0), cur(0)],
        out_shape=[jax.ShapeDtypeStruct((T, AW), F32), jax.ShapeDtypeStruct((T, AW), F32)],
        compiler_params=_cp(("parallel",)),
    )(qkv, qkv, qkv, qkv, qkv, _band_bias(gi))


def _attn_bwd(qkv, dob, lse, delta, gi, name):
    seg = (T // GROUPS[gi][1]) // BLK
    qb = QB_BWD
    nb = T // (qb * BLK)
    scale = HEAD ** -0.5

    def body(q_ref, kp_ref, kc_ref, vp_ref, vc_ref, bias_ref, do_ref, l_ref, dl_ref, out_ref, dk_acc, dv_acc):
        n = pl.program_id(1)
        kwin = jnp.concatenate([kp_ref[...], kc_ref[...]], axis=0)
        vwin = jnp.concatenate([vp_ref[...], vc_ref[...]], axis=0)
        bias = bias_ref[...]
        for b in range(qb):
            rows = pl.ds(b * BLK, BLK)
            q = q_ref[rows, :]
            kcat = kwin[b * BLK:(b + 2) * BLK]
            s = _scores(q, kcat, bias, n * qb + b, seg)
            p = jnp.exp(s - l_ref[rows, pl.ds(0, 1)])
            dov = do_ref[rows, :]
            dv2 = _dot(p.astype(BF16), dov, TN)
            dp = _dot(dov, vwin[b * BLK:(b + 2) * BLK], NT)
            dsb = (p * (dp - dl_ref[rows, pl.ds(0, 1)]) * scale).astype(BF16)
            row = pl.ds(pl.multiple_of((n * qb + b) * BLK, BLK), BLK)
            out_ref[0, row, :] = _dot(dsb, kcat, NN).astype(BF16)
            dk2 = _dot(dsb, q, TN)
            dk_acc[row, :] = dk2[BLK:]
            dv_acc[row, :] = dv2[BLK:]

            def add_prev(dk2=dk2, dv2=dv2, b=b):
                prow = pl.ds(pl.multiple_of((n * qb + b - 1) * BLK, BLK), BLK)
                dk_acc[prow, :] += dk2[:BLK]
                dv_acc[prow, :] += dv2[:BLK]

            if b == 0:
                pl.when(n > 0)(add_prev)
            else:
                add_prev()

        @pl.when(n == nb - 1)
        def _():
            out_ref[1] = dk_acc[...].astype(BF16)
            out_ref[2] = dv_acc[...].astype(BF16)

    oblk = pl.BlockSpec((qb * BLK, HEAD), lambda h, n: (n, h))
    return pl.pallas_call(
        body, name=name, grid=(NHG, nb),
        in_specs=_attn_specs(qb) + [oblk, oblk, oblk],
        out_specs=pl.BlockSpec((3, T, HEAD), lambda h, n: (0, 0, h)),
        out_shape=jax.ShapeDtypeStruct((3, T, AW), BF16),
        scratch_shapes=[pltpu.VMEM((T, HEAD), F32), pltpu.VMEM((T, HEAD), F32)],
        compiler_params=_cp(("parallel", "arbitrary")),
    )(qkv, qkv, qkv, qkv, qkv, _band_bias(gi), dob, lse, delta)


def _merge(outs, lses, name):
    tm = 512

    def body(o0, l0, o1, l1, o2, l2, a_ref, ab_ref, lse_ref):
        ls = [l0[...], l1[...], l2[...]]
        mx = jnp.maximum(jnp.maximum(ls[0], ls[1]), ls[2])
        es = [jnp.exp(v - mx) for v in ls]
        tot = es[0] + es[1] + es[2]
        att = (es[0] / tot) * o0[...] + (es[1] / tot) * o1[...] + (es[2] / tot) * o2[...]
        a_ref[...] = att
        ab_ref[...] = att.astype(BF16)
        lse_ref[...] = mx + jnp.log(tot)

    row = pl.BlockSpec((tm, AW), lambda i: (i, 0))
    return pl.pallas_call(
        body, name=name, grid=(T // tm,), in_specs=[row] * 6, out_specs=[row] * 3,
        out_shape=[jax.ShapeDtypeStruct((T, AW), F32), jax.ShapeDtypeStruct((T, AW), BF16),
                   jax.ShapeDtypeStruct((T, AW), F32)],
        compiler_params=_cp(("parallel",)),
    )(outs[0], lses[0], outs[1], lses[1], outs[2], lses[2])


def _mix_out(z3b, attnb, gates, wc, wa_t, wo, x1, name):
    tm = 512

    def body(z_ref, a_ref, g_ref, wc_ref, wa_ref, wo_ref, x_ref, xo_ref, yc_ref, ya_ref, mx_ref):
        yc = _dot(z_ref[...], wc_ref[...], NN)
        ya = _dot(a_ref[...], wa_ref[...], NT)
        yc_ref[...] = yc
        ya_ref[...] = ya
        mixed = (_sig(g_ref[:, pl.ds(0, D)]) * yc + _sig(g_ref[:, pl.ds(D, D)]) * ya).astype(BF16)
        mx_ref[...] = mixed
        xo_ref[...] = x_ref[...] + _dot(mixed, wo_ref[...], NN)

    row = pl.BlockSpec((tm, D), lambda i: (i, 0))
    return pl.pallas_call(
        body, name=name, grid=(T // tm,),
        in_specs=[row, pl.BlockSpec((tm, AW), lambda i: (i, 0)), pl.BlockSpec((tm, 2 * D), lambda i: (i, 0)),
                  _const_spec((D, D)), _const_spec((D, AW)), _const_spec((D, D)), row],
        out_specs=[row, row, row, row],
        out_shape=[jax.ShapeDtypeStruct((T, D), F32), jax.ShapeDtypeStruct((T, D), F32),
                   jax.ShapeDtypeStruct((T, D), F32), jax.ShapeDtypeStruct((T, D), BF16)],
        compiler_params=_cp(("parallel",)),
    )(z3b, attnb, gates, wc, wa_t, wo, x1)


def _mix_out_bwd(dx2, gates, yc, ya, attn, wc, wa_t, wo, name):
    tm = 512

    def body(dx_ref, g_ref, yc_ref, ya_ref, at_ref, wc_ref, wa_ref, wo_ref,
             dg_ref, dyc_ref, dya_ref, dxb_ref, dz3_ref, dat_ref, dl_ref):
        dxb = dx_ref[...].astype(BF16)
        dxb_ref[...] = dxb
        dmix = _dot(dxb, wo_ref[...], NT)
        sc = _sig(g_ref[:, pl.ds(0, D)])
        sa = _sig(g_ref[:, pl.ds(D, D)])
        ycv, yav = yc_ref[...], ya_ref[...]
        dg_ref[:, pl.ds(0, D)] = (dmix * ycv * sc * (1.0 - sc)).astype(BF16)
        dg_ref[:, pl.ds(D, D)] = (dmix * yav * sa * (1.0 - sa)).astype(BF16)
        dyc = (dmix * sc).astype(BF16)
        dya = (dmix * sa).astype(BF16)
        dyc_ref[...] = dyc
        dya_ref[...] = dya
        dz3_ref[...] = _dot(dyc, wc_ref[...], NT)
        dat = _dot(dya, wa_ref[...], NN)
        dat_ref[...] = dat.astype(BF16)
        prod = dat * at_ref[...]
        for h in range(NHG):
            sl = pl.ds(h * HEAD, HEAD)
            dl_ref[:, sl] = jnp.broadcast_to(jnp.sum(prod[:, h * HEAD:(h + 1) * HEAD], axis=-1, keepdims=True),
                                             (tm, HEAD))

    row = pl.BlockSpec((tm, D), lambda i: (i, 0))
    row2 = pl.BlockSpec((tm, 2 * D), lambda i: (i, 0))
    rowa = pl.BlockSpec((tm, AW), lambda i: (i, 0))
    return pl.pallas_call(
        body, name=name, grid=(T // tm,),
        in_specs=[row, row2, row, row, rowa, _const_spec((D, D)), _const_spec((D, AW)), _const_spec((D, D))],
        out_specs=[row2, row, row, row, row, rowa, rowa],
        out_shape=[jax.ShapeDtypeStruct((T, 2 * D), BF16), jax.ShapeDtypeStruct((T, D), BF16),
                   jax.ShapeDtypeStruct((T, D), BF16), jax.ShapeDtypeStruct((T, D), BF16),
                   jax.ShapeDtypeStruct((T, D), F32), jax.ShapeDtypeStruct((T, AW), BF16),
                   jax.ShapeDtypeStruct((T, AW), F32)],
        compiler_params=_cp(("parallel",)),
    )(dx2, gates, yc, ya, attn, wc, wa_t, wo)


def _peer(k):
    x, y, c = lax.axis_index("x"), lax.axis_index("y"), lax.axis_index("c")
    px = 1 - x if k & 4 else x
    py = 1 - y if k & 2 else y
    pc = 1 - c if k & 1 else c
    return (px, py, pc), 4 * px + 2 * py + pc


HBM_SPEC = pl.BlockSpec(memory_space=pltpu.HBM)
SEM_SPEC = pl.BlockSpec(memory_space=pltpu.SEMAPHORE)
EFFECT = pltpu.SideEffectType.DATAFLOW_SIDE_EFFECTING


def _my_place():
    return 4 * lax.axis_index("x") + 2 * lax.axis_index("y") + lax.axis_index("c")


def _tie(a, order_after, name):
    na = len(order_after)

    def body(*refs):
        del refs

    return pl.pallas_call(
        body, name=name, in_specs=[pl.BlockSpec(memory_space=pl.ANY)] * (1 + na),
        out_specs=pl.BlockSpec(memory_space=pl.ANY), out_shape=jax.ShapeDtypeStruct(a.shape, a.dtype),
        input_output_aliases={0: 0},
    )(a, *order_after)


def _prep_gather(ws, transposed, order_after, name):
    me = jnp.reshape(_my_place(), (1,)).astype(jnp.int32)
    n = len(ws)
    na = len(order_after)
    shapes = []
    for wv, tr in zip(ws, transposed):
        r, c = (wv.shape[1], wv.shape[0]) if tr else wv.shape
        shapes.append(((32, c), F32) if r == CONV_W else ((r, c), BF16))

    def body(me_ref, *refs):
        del me_ref
        ins, outs = refs[:n], refs[n + na:]
        for wv, tr, i_ref, o_ref in zip(ws, transposed, ins, outs):
            if wv.shape[0] == CONV_W:
                o_ref[pl.ds(0, CONV_W), :] = i_ref[...]
                o_ref[pl.ds(CONV_W, 1), :] = jnp.zeros((1, wv.shape[1]), F32)
            elif tr:
                o_ref[...] = jnp.transpose(i_ref[...]).astype(BF16)
            else:
                o_ref[...] = i_ref[...].astype(BF16)

    grid_spec = pltpu.PrefetchScalarGridSpec(
        num_scalar_prefetch=1, grid=(1,),
        in_specs=[pl.BlockSpec(wv.shape, lambda i, m: (0, 0)) for wv in ws]
        + [pl.BlockSpec(memory_space=pl.ANY)] * na,
        out_specs=[pl.BlockSpec(shp, lambda i, m: (m[0], 0)) for shp, _ in shapes])
    return pl.pallas_call(
        body, name=name, grid_spec=grid_spec,
        out_shape=[jax.ShapeDtypeStruct((NDEV * shp[0], shp[1]), dt) for shp, dt in shapes],
        compiler_params=_cp(("arbitrary",)),
    )(me, *ws, *order_after)


GATHER_A = ((1, 0), (2, 0), (4, 0), (6, 0))
GATHER_B = ((1, 2), (1, 4), (1, 6))


def _gather_start(lands, plan, order_after, name):
    n = len(lands)
    na = len(order_after)
    npl = len(plan)

    def body(*refs):
        land_refs = refs[:n]
        send, recv = refs[n + na], refs[n + na + 1]
        token = refs[-1]
        for w in range(n):
            rows = lands[w].shape[0] // NDEV
            for p, (k, j) in enumerate(plan):
                peer, _ = _peer(k)
                _, blk = _peer(j)
                part = land_refs[w].at[pl.ds(blk * rows, rows)]
                i = w * npl + p
                pltpu.make_async_remote_copy(src_ref=part, dst_ref=part, send_sem=send.at[i], recv_sem=recv.at[i],
                                             device_id=peer, device_id_type=MESH_ID).start()
        token[...] = jnp.zeros_like(token)

    nsem = n * npl
    bufs = [pltpu.with_memory_space_constraint(a, pltpu.HBM) for a in lands]
    out = pl.pallas_call(
        body, name=name,
        in_specs=[HBM_SPEC] * n + [pl.BlockSpec(memory_space=pl.ANY)] * na,
        out_specs=[SEM_SPEC, SEM_SPEC] + [HBM_SPEC] * n + [pl.BlockSpec(memory_space=pltpu.VMEM)],
        out_shape=[pltpu.SemaphoreType.DMA((nsem,)), pltpu.SemaphoreType.DMA((nsem,))]
        + [pltpu.HBM(a.shape, a.dtype) for a in bufs] + [jax.ShapeDtypeStruct((8, 128), F32)],
        input_output_aliases={i: 2 + i for i in range(n)},
        compiler_params=pltpu.CompilerParams(has_side_effects=EFFECT),
    )(*bufs, *order_after)
    return out[0], out[1], out[2:2 + n], out[-1]


def _gather_wait(started, plan, order_after, name):
    send, recv, lands, _ = started
    n = len(lands)
    na = len(order_after)
    npl = len(plan)

    def body(*refs):
        land_refs = refs[:n]
        send_ref, recv_ref = refs[n], refs[n + 1]
        for w in range(n):
            rows = lands[w].shape[0] // NDEV
            for p, (k, j) in enumerate(plan):
                peer, _ = _peer(k)
                _, blk = _peer(j)
                part = land_refs[w].at[pl.ds(blk * rows, rows)]
                i = w * npl + p
                cp = pltpu.make_async_remote_copy(src_ref=part, dst_ref=part, send_sem=send_ref.at[i],
                                                  recv_sem=recv_ref.at[i], device_id=peer, device_id_type=MESH_ID)
                cp.wait_send()
                cp.wait_recv()

    out = pl.pallas_call(
        body, name=name,
        in_specs=[HBM_SPEC] * n + [SEM_SPEC, SEM_SPEC] + [pl.BlockSpec(memory_space=pl.ANY)] * na,
        out_specs=[HBM_SPEC] * n,
        out_shape=[pltpu.HBM(a.shape, a.dtype) for a in lands],
        input_output_aliases={i: i for i in range(n)},
        compiler_params=pltpu.CompilerParams(has_side_effects=EFFECT),
    )(*lands, send, recv, *order_after)
    return list(out)


def _copy_ends(kind, src, land, me, plin, k):
    if kind == "scatter":
        rows = src.shape[0] // NDEV
        return src.at[pl.ds(plin * rows, rows)], land.at[k - 1]
    return src, land.at[me]


def _landing(kind, src):
    me = _my_place()
    if kind == "scatter":
        return lax.empty((NDEV - 1, src.shape[0] // NDEV) + src.shape[1:], src.dtype)
    land = lax.empty((NDEV,) + src.shape, src.dtype)
    return lax.dynamic_update_slice(land, src[None], (me,) + (0,) * src.ndim)


def _send_start(kinds, srcs, order_after, name):
    n = len(srcs)
    lands = [_landing(kd, s) for kd, s in zip(kinds, srcs)]
    na = len(order_after)

    def body(*refs):
        src_refs, land_refs = refs[:n], refs[n:2 * n]
        send, recv = refs[2 * n + na], refs[2 * n + na + 1]
        token = refs[-1]
        _, me = _peer(0)
        for w in range(n):
            for k in range(1, NDEV):
                peer, plin = _peer(k)
                s, d = _copy_ends(kinds[w], src_refs[w], land_refs[w], me, plin, k)
                i = w * (NDEV - 1) + k - 1
                pltpu.make_async_remote_copy(src_ref=s, dst_ref=d, send_sem=send.at[i], recv_sem=recv.at[i],
                                             device_id=peer, device_id_type=MESH_ID).start()
        token[...] = jnp.zeros_like(token)

    nsem = n * (NDEV - 1)
    bufs = [pltpu.with_memory_space_constraint(a, pltpu.HBM) for a in list(srcs) + lands]
    out = pl.pallas_call(
        body, name=name,
        in_specs=[HBM_SPEC] * (2 * n) + [pl.BlockSpec(memory_space=pl.ANY)] * na,
        out_specs=[SEM_SPEC, SEM_SPEC] + [HBM_SPEC] * (2 * n) + [pl.BlockSpec(memory_space=pltpu.VMEM)],
        out_shape=[pltpu.SemaphoreType.DMA((nsem,)), pltpu.SemaphoreType.DMA((nsem,))]
        + [pltpu.HBM(a.shape, a.dtype) for a in bufs] + [jax.ShapeDtypeStruct((8, 128), F32)],
        input_output_aliases={i: 2 + i for i in range(2 * n)},
        compiler_params=pltpu.CompilerParams(has_side_effects=EFFECT),
    )(*bufs, *order_after)
    return out[0], out[1], out[2:2 + n], out[2 + n:2 + 2 * n], out[-1]


def _send_wait(kinds, started, order_after, name):
    send, recv, srcs, lands, _ = started
    n = len(srcs)
    na = len(order_after)

    def body(*refs):
        src_refs, land_refs = refs[:n], refs[n:2 * n]
        send_ref, recv_ref = refs[2 * n], refs[2 * n + 1]
        _, me = _peer(0)
        for w in range(n):
            for k in range(1, NDEV):
                peer, plin = _peer(k)
                s, d = _copy_ends(kinds[w], src_refs[w], land_refs[w], me, plin, k)
                i = w * (NDEV - 1) + k - 1
                cp = pltpu.make_async_remote_copy(src_ref=s, dst_ref=d, send_sem=send_ref.at[i],
                                                  recv_sem=recv_ref.at[i], device_id=peer, device_id_type=MESH_ID)
                cp.wait_send()
                cp.wait_recv()

    bufs = list(srcs) + list(lands)
    out = pl.pallas_call(
        body, name=name,
        in_specs=[HBM_SPEC] * (2 * n) + [SEM_SPEC, SEM_SPEC] + [pl.BlockSpec(memory_space=pl.ANY)] * na,
        out_specs=[HBM_SPEC] * (2 * n),
        out_shape=[pltpu.HBM(a.shape, a.dtype) for a in bufs],
        input_output_aliases={i: i for i in range(2 * n)},
        compiler_params=pltpu.CompilerParams(has_side_effects=EFFECT),
    )(*bufs, send, recv, *order_after)
    return out[:n], out[n:]


def _gsum(own, land, name):
    rows, cols = own.shape
    tr = rows // 2 if rows * cols > 512 * 1024 and rows % 32 == 0 else rows

    def body(own_ref, l_ref, o_ref):
        tot = own_ref[...].astype(F32)
        for s in range(NDEV - 1):
            tot = tot + l_ref[s].astype(F32)
        o_ref[...] = tot

    return pl.pallas_call(
        body, name=name, grid=(rows // tr,),
        in_specs=[pl.BlockSpec((tr, cols), lambda i: (i, 0)),
                  pl.BlockSpec((NDEV - 1, tr, cols), lambda i: (0, i, 0))],
        out_specs=pl.BlockSpec((tr, cols), lambda i: (i, 0)),
        out_shape=jax.ShapeDtypeStruct((rows, cols), F32),
        compiler_params=_cp(("parallel",)),
    )(own, land)


def _adamw_math(w, g, m, v):
    m2 = B1 * m + (1.0 - B1) * g
    v2 = B2 * v + (1.0 - B2) * (g * g)
    m_hat = m2 / (1.0 - B1 ** STEP)
    v_hat = v2 / (1.0 - B2 ** STEP)
    delta = -LR * (m_hat / (jnp.sqrt(v_hat) + AEPS) + WD * w)
    return delta, m2, v2


def _adamw(w, g, m, v, name):
    rows, cols = w.shape
    tr = 256 if rows % 256 == 0 and rows > 256 else rows

    def body(w_ref, g_ref, m_ref, v_ref, d_ref, mo_ref, vo_ref):
        d, m2, v2 = _adamw_math(w_ref[...], g_ref[...], m_ref[...], v_ref[...])
        d_ref[...] = d
        mo_ref[...] = m2
        vo_ref[...] = v2

    blk = pl.BlockSpec((tr, cols), lambda i: (i, 0))
    return pl.pallas_call(
        body, name=name, grid=(rows // tr,), in_specs=[blk] * 4, out_specs=[blk] * 3,
        out_shape=[jax.ShapeDtypeStruct((rows, cols), F32)] * 3,
        compiler_params=_cp(("parallel",)),
    )(w, g, m, v)


def _small_update(vland, w8, m8, v8, name):
    def body(l_ref, w_ref, m_ref, v_ref, g_ref, d_ref, mo_ref, vo_ref):
        g = l_ref[0]
        for s in range(1, NDEV):
            g = g + l_ref[s]
        g_ref[...] = g
        d, m2, v2 = _adamw_math(w_ref[...], g, m_ref[...], v_ref[...])
        d_ref[...] = d
        mo_ref[...] = m2
        vo_ref[...] = v2

    return pl.pallas_call(
        body, name=name, out_shape=[jax.ShapeDtypeStruct((8, D), F32)] * 4,
        compiler_params=_cp(None),
    )(vland, w8, m8, v8)


def _perm(a, dil):
    if dil == 1:
        return a
    t, n = a.shape
    return a.reshape(t // dil, dil, n).transpose(1, 0, 2).reshape(t, n)


def _unperm(a, dil):
    if dil == 1:
        return a
    t, n = a.shape
    return a.reshape(dil, t // dil, n).transpose(1, 0, 2).reshape(t, n)


def kernel(x, ffn1_norm, ffn1_w_gate, ffn1_w_up, ffn1_w_down, mix_norm, w_in, conv_dw_kernel, conv_dw_bias, conv_ln_gain, conv_ln_bias, conv_w_out, attn_w_out, w_o, ffn2_norm, ffn2_w_gate, ffn2_w_up, ffn2_w_down, final_norm, loss_target, m_ffn1_norm, m_ffn1_w_gate, m_ffn1_w_up, m_ffn1_w_down, m_mix_norm, m_w_in, m_conv_dw_kernel, m_conv_dw_bias, m_conv_ln_gain, m_conv_ln_bias, m_conv_w_out, m_attn_w_out, m_w_o, m_ffn2_norm, m_ffn2_w_gate, m_ffn2_w_up, m_ffn2_w_down, m_final_norm, v_ffn1_norm, v_ffn1_w_gate, v_ffn1_w_up, v_ffn1_w_down, v_mix_norm, v_w_in, v_conv_dw_kernel, v_conv_dw_bias, v_conv_ln_gain, v_conv_ln_bias, v_conv_w_out, v_attn_w_out, v_w_o, v_ffn2_norm, v_ffn2_w_gate, v_ffn2_w_up, v_ffn2_w_down, v_final_norm):
    names = ["ffn1_norm", "ffn1_w_gate", "ffn1_w_up", "ffn1_w_down", "mix_norm", "w_in", "conv_dw_kernel",
             "conv_dw_bias", "conv_ln_gain", "conv_ln_bias", "conv_w_out", "attn_w_out", "w_o", "ffn2_norm",
             "ffn2_w_gate", "ffn2_w_up", "ffn2_w_down", "final_norm"]
    w = dict(ffn1_norm=ffn1_norm, ffn1_w_gate=ffn1_w_gate, ffn1_w_up=ffn1_w_up, ffn1_w_down=ffn1_w_down, mix_norm=mix_norm, w_in=w_in, conv_dw_kernel=conv_dw_kernel, conv_dw_bias=conv_dw_bias, conv_ln_gain=conv_ln_gain, conv_ln_bias=conv_ln_bias, conv_w_out=conv_w_out, attn_w_out=attn_w_out, w_o=w_o, ffn2_norm=ffn2_norm, ffn2_w_gate=ffn2_w_gate, ffn2_w_up=ffn2_w_up, ffn2_w_down=ffn2_w_down, final_norm=final_norm)
    mo = dict(ffn1_norm=m_ffn1_norm, ffn1_w_gate=m_ffn1_w_gate, ffn1_w_up=m_ffn1_w_up, ffn1_w_down=m_ffn1_w_down, mix_norm=m_mix_norm, w_in=m_w_in, conv_dw_kernel=m_conv_dw_kernel, conv_dw_bias=m_conv_dw_bias, conv_ln_gain=m_conv_ln_gain, conv_ln_bias=m_conv_ln_bias, conv_w_out=m_conv_w_out, attn_w_out=m_attn_w_out, w_o=m_w_o, ffn2_norm=m_ffn2_norm, ffn2_w_gate=m_ffn2_w_gate, ffn2_w_up=m_ffn2_w_up, ffn2_w_down=m_ffn2_w_down, final_norm=m_final_norm)
    vo = dict(ffn1_norm=v_ffn1_norm, ffn1_w_gate=v_ffn1_w_gate, ffn1_w_up=v_ffn1_w_up, ffn1_w_down=v_ffn1_w_down, mix_norm=v_mix_norm, w_in=v_w_in, conv_dw_kernel=v_conv_dw_kernel, conv_dw_bias=v_conv_dw_bias, conv_ln_gain=v_conv_ln_gain, conv_ln_bias=v_conv_ln_bias, conv_w_out=v_conv_w_out, attn_w_out=v_attn_w_out, w_o=v_w_o, ffn2_norm=v_ffn2_norm, ffn2_w_gate=v_ffn2_w_gate, ffn2_w_up=v_ffn2_w_up, ffn2_w_down=v_ffn2_w_down, final_norm=v_final_norm)
    col_sharded = ("ffn1_w_gate", "ffn1_w_up", "w_in", "attn_w_out", "ffn2_w_gate", "ffn2_w_up")
    row_sharded = ("ffn1_w_down", "conv_w_out", "w_o", "ffn2_w_down")
    small = ("ffn1_norm", "mix_norm", "ffn2_norm", "final_norm", "conv_dw_bias", "conv_ln_gain", "conv_ln_bias")

    ag_groups = (("ffn1_w_gate", "ffn1_w_up", "ffn1_w_down"),
                 ("w_in", "attn_w_out", "conv_w_out", "w_o", "conv_dw_kernel"),
                 ("ffn2_w_gate", "ffn2_w_up", "ffn2_w_down"))
    ag, order = [], []
    for gi, grp in enumerate(ag_groups):
        lands = _prep_gather([w[n][0] for n in grp], [n in col_sharded for n in grp], order, f"gather_prep{gi}")
        st = _gather_start(lands, GATHER_A, [], f"gather_a_start{gi}")
        ag.append(st)
        order = [st[3]]

    def chips_in(gi, after):
        lands = _gather_wait(ag[gi], GATHER_A, after, f"gather_a_wait{gi}")
        return _gather_start(lands, GATHER_B, [], f"gather_b_start{gi}")

    def all_in(gi, st, after):
        return _gather_wait(st, GATHER_B, after, f"gather_b_wait{gi}")

    x0 = x[0]
    tgt = loss_target[0]
    gf = final_norm.reshape(1, D)

    wg1, wu1, wd1 = all_in(0, chips_in(0, [ag[2][3]]), [])
    x1, gg1, uu1 = _ffn_fwd(x0, ffn1_norm, wg1, wu1, wd1, "ffn1_fwd")
    win_t, wa_t, wc, wo, kern_blocks = all_in(1, chips_in(1, [x1]), [])
    kern = kern_blocks.reshape(NDEV, 32, D // NDEV).transpose(1, 0, 2).reshape(32, D)
    h2 = _norm_cast(x1, mix_norm, "mix_norm_fwd")
    ab = _mm(h2, win_t, mode="nt", m=T, n=2 * D, k=D, tm=1024, tn=512, tk=D, out_dtype=F32, name="proj_conv")
    gates = _mm(h2, win_t, mode="nt", m=T, n=2 * D, k=D, tm=1024, tn=512, tk=D, out_dtype=F32,
                b_map=lambda i, j, kk: (13 + j, 0), name="proj_gates")
    h2p, qkv = [], []
    for gi, (_, dil) in enumerate(GROUPS):
        hp = _perm(h2, dil)
        h2p.append(hp)
        qkv.append(_mm(hp, win_t, mode="nt", m=T, n=3 * AW, k=D, tm=1024, tn=AW, tk=D, out_dtype=BF16,
                       b_map=lambda i, j, kk, gi=gi: (4 + gi + 3 * j, 0), name=f"proj_qkv{gi}"))
    z1, z3b = _conv_fwd(ab, kern, conv_dw_bias, conv_ln_gain, conv_ln_bias, "conv_fwd")
    ffn2_b = chips_in(2, [z3b])
    outs, lses = [], []
    for gi, (_, dil) in enumerate(GROUPS):
        o, l = _attn_fwd(qkv[gi], gi, f"attn_fwd{gi}")
        outs.append(_unperm(o, dil))
        lses.append(_unperm(l, dil))
    attn, attnb, lse = _merge(outs, lses, "attn_merge")
    x2, yc, ya, mixedb = _mix_out(z3b, attnb, gates, wc, wa_t, wo, x1, "mix_out_fwd")
    wg2, wu2, wd2 = all_in(2, ffn2_b, [x2])
    x3, gg2, uu2 = _ffn_fwd(x2, ffn2_norm, wg2, wu2, wd2, "ffn2_fwd")

    dx3, dgf, loss_part = _final(x3, gf, tgt, "final_norm_loss")
    dx2, dg3, dgb, dub, actb, hb, dob = _ffn_bwd(x2, ffn2_norm, gg2, uu2, dx3, wg2, wu2, wd2, "ffn2_bwd")
    grads = {}
    grads["ffn2_w_gate"] = _wgrad(dgb, hb, FF, D, "ffn2_dwg")
    grads["ffn2_w_up"] = _wgrad(dub, hb, FF, D, "ffn2_dwu")
    grads["ffn2_w_down"] = _wgrad(actb, dob, FF, D, "ffn2_dwd")
    rs_groups = [("ffn2_w_gate", "ffn2_w_up", "ffn2_w_down"),
                 ("attn_w_out", "conv_w_out", "w_o", "conv_dw_kernel"),
                 ("w_in",),
                 ("ffn1_w_gate",), ("ffn1_w_up",), ("ffn1_w_down",)]
    last = len(rs_groups) - 1
    rs = [_send_start(["scatter"] * 3, [grads[n] for n in rs_groups[0]], [], "scatter_start0")]
    dx2 = _tie(dx2, [rs[0][4]], "tie_after_scatter0")

    dgates, dycb, dyab, dx2b, dz3, dattnb, delta = _mix_out_bwd(dx2, gates, yc, ya, attn, wc, wa_t, wo, "mix_out_bwd")
    grads["w_o"] = _wgrad(mixedb, dx2b, D, D, "dw_o")
    grads["conv_w_out"] = _wgrad(z3b, dycb, D, D, "dw_conv_out")
    grads["attn_w_out"] = _wgrad(dyab, attnb, D, AW, "dw_attn_out")
    dab, dkern, dvec = _conv_bwd(dz3, z1, ab, kern, conv_ln_gain, conv_ln_bias, "conv_bwd")
    grads["conv_dw_kernel"] = dkern.reshape(32, NDEV, D // NDEV).transpose(1, 0, 2).reshape(NDEV * 32, D // NDEV)
    rs.append(_send_start(["scatter"] * 4, [grads[n] for n in rs_groups[1]], [rs[0][4]], "scatter_start1"))
    dattnb = _tie(dattnb, [rs[1][4]], "tie_after_scatter1")

    dqkv = []
    for gi, (_, dil) in enumerate(GROUPS):
        dq3 = _attn_bwd(qkv[gi], _perm(dattnb, dil), _perm(lse, dil), _perm(delta, dil), gi, f"attn_bwd{gi}")
        dqkv.append(dq3.reshape(3 * T, AW))

    dwin = _mm(dab, h2, mode="tn", m=2 * D, n=D, k=T, tm=2 * D, tn=D, tk=512, out_dtype=BF16, out_rows=IN_W,
               name="dw_in_conv")
    dwin = _mm(dgates, h2, mode="tn", m=2 * D, n=D, k=T, tm=512, tn=D, tk=1024, out_dtype=BF16, out_rows=IN_W,
               o_map=lambda i, j, kk: (13 + i, 0), passthru=dwin, name="dw_in_gates")
    for gi in range(3):
        dwin = _mm(dqkv[gi], h2p[gi], mode="tn", m=3 * AW, n=D, k=T, tm=AW, tn=D, tk=1024, out_dtype=BF16,
                   out_rows=IN_W, a_map=lambda i, j, kk: (i * (T // 1024) + kk, 0),
                   o_map=lambda i, j, kk, gi=gi: (4 + gi + 3 * i, 0), passthru=dwin, name=f"dw_in_qkv{gi}")
    grads["w_in"] = dwin
    rs.append(_send_start(["scatter"], [dwin], [rs[1][4]], "scatter_start2"))
    dab = _tie(dab, [rs[2][4]], "tie_after_scatter2")

    nrow = T // 1024
    dh = _mm(dab, win_t, mode="nn", m=T, n=D, k=2 * D, tm=1024, tn=D, tk=512, out_dtype=F32, name="dproj_conv")
    dh = _mm(dgates, win_t, mode="nn", m=T, n=D, k=2 * D, tm=1024, tn=D, tk=512, out_dtype=F32,
             b_map=lambda i, j, kk: (13 + kk, 0), init=dh, name="dproj_gates")
    dhs = []
    for gi, (_, dil) in enumerate(GROUPS):
        part = _mm(dqkv[gi], win_t, mode="nn", m=T, n=D, k=3 * AW, tm=1024, tn=D, tk=AW, out_dtype=F32,
                   a_map=lambda i, j, kk: (kk * nrow + i, 0), b_map=lambda i, j, kk, gi=gi: (4 + gi + 3 * kk, 0),
                   init=dh if gi == 0 else None, name=f"dproj_qkv{gi}")
        dhs.append(_unperm(part, dil))
    dx1, dg2 = _rms_bwd(x1, mix_norm, dhs, dx2, "mix_norm_bwd")

    dx0, dg1, dgb, dub, actb, hb, dob = _ffn_bwd(x0, ffn1_norm, gg1, uu1, dx1, wg1, wu1, wd1, "ffn1_bwd")
    grads["ffn1_w_gate"] = _wgrad(dgb, hb, FF, D, "ffn1_dwg")
    rs.append(_send_start(["scatter"], [grads["ffn1_w_gate"]], [rs[2][4]], "scatter_start3"))
    hb = _tie(hb, [rs[3][4]], "tie_after_scatter3")
    grads["ffn1_w_up"] = _wgrad(dub, hb, FF, D, "ffn1_dwu")
    rs.append(_send_start(["scatter"], [grads["ffn1_w_up"]], [rs[3][4]], "scatter_start4"))
    dob = _tie(dob, [rs[4][4]], "tie_after_scatter4")
    grads["ffn1_w_down"] = _wgrad(actb, dob, FF, D, "ffn1_dwd")
    vec = jnp.concatenate([dg1, dg2, dg3, dgf, dvec[0:3], jnp.broadcast_to(loss_part[:, :1], (1, D))], axis=0)
    rs.append(_send_start(["scatter", "bcast"], [grads["ffn1_w_down"], vec], [rs[4][4]], "scatter_start5"))

    g_out, d_out, m_out, v_out = {}, {}, {}, {}
    me = _my_place()
    after = [rs[last][4]]
    for gi, grp in enumerate(rs_groups):
        kinds = ["scatter"] * len(grp) + (["bcast"] if gi == last else [])
        srcs, lands = _send_wait(kinds, rs[gi], after, f"scatter_wait{gi}")
        for n, src, land in zip(grp, srcs, lands):
            rows = src.shape[0] // NDEV
            own = lax.dynamic_slice(src, (me * rows, 0), (rows, src.shape[1]))
            g = _gsum(own, land, f"gsum_{n}")
            if n in col_sharded:
                g = jnp.transpose(g)
            elif n == "conv_dw_kernel":
                g = g[:CONV_W]
            d, m2, v2 = _adamw(w[n][0], g, mo[n][0], vo[n][0], f"adamw_{n}")
            g_out[n], d_out[n], m_out[n], v_out[n] = g[None], d[None], m2[None], v2[None]
            after = [d]
    vland = lands[-1]

    def rows8(src):
        return jnp.concatenate([src[n].reshape(1, D) for n in small] + [jnp.ones((1, D), F32)], axis=0)

    g8, d8, m8, v8 = _small_update(vland, rows8(w), rows8(mo), rows8(vo), "small_update")
    for r, n in enumerate(small):
        shp = w[n].shape
        g_out[n], d_out[n], m_out[n], v_out[n] = (a[r].reshape(shp) for a in (g8, d8, m8, v8))
    loss = g8[7, 0]

    return (loss, dx0[None], *[g_out[n] for n in names], *[d_out[n] for n in names],
            *[m_out[n] for n in names], *[v_out[n] for n in names])
```

```python
import numpy as np
import jax
import jax.numpy as jnp
from jax import lax
from jax.experimental import pallas as pl
from jax.experimental.pallas import tpu as pltpu

F32 = jnp.float32
BF16 = jnp.bfloat16

T = 4096
D = 1024
FF = 2816
NDEV = 8
CONV_W = 31
HEAD = 128
BLK = 128
GROUPS = ((128, 1), (512, 4), (2048, 16))
NHG = 4
AW = NHG * HEAD
IN_W = 2 * D + 3 * 3 * AW + 2 * D
EPS = 1e-6
B1, B2, LR, AEPS, WD, STEP = 0.9, 0.999, 0.001, 1e-08, 0.01, 10
NEG = -1e30
VMEM_LIMIT = 56 * 1024 * 1024
MESH_ID = pl.DeviceIdType.MESH

NT = (((1,), (1,)), ((), ()))
NN = (((1,), (0,)), ((), ()))
TN = (((0,), (0,)), ((), ()))
_DIMS = {"nn": NN, "nt": NT, "tn": TN}


def _cp(sem=None):
    return pltpu.CompilerParams(dimension_semantics=sem, vmem_limit_bytes=VMEM_LIMIT)


def _sig(v):
    return 1.0 / (1.0 + jnp.exp(-v))


def _dot(a, b, dims):
    return lax.dot_general(a, b, dims, preferred_element_type=F32)


def _const_spec(shape):
    nd = len(shape)
    return pl.BlockSpec(shape, lambda *_: (0,) * nd)


def _mm(a, b, *, mode, m, n, k, tm, tn, tk, out_dtype, name, a_map=None, b_map=None,
        o_map=None, out_rows=None, init=None, passthru=None):
    gi, gj, gk = m // tm, n // tn, k // tk
    assert gi * tm == m and gj * tn == n and gk * tk == k, (name, m, n, k, tm, tn, tk)
    if mode == "nn":
        a_blk, b_blk = (tm, tk), (tk, tn)
        da, db = (lambda i, j, kk: (i, kk)), (lambda i, j, kk: (kk, j))
    elif mode == "nt":
        a_blk, b_blk = (tm, tk), (tn, tk)
        da, db = (lambda i, j, kk: (i, kk)), (lambda i, j, kk: (j, kk))
    else:
        a_blk, b_blk = (tk, tm), (tk, tn)
        da, db = (lambda i, j, kk: (kk, i)), (lambda i, j, kk: (kk, j))
    a_map = a_map or da
    b_map = b_map or db
    o_map = o_map or (lambda i, j, kk: (i, j))
    dims = _DIMS[mode]
    extra = init if init is not None else passthru
    out_rows = out_rows or m

    def body(*refs):
        if init is not None:
            a_ref, b_ref, i_ref, o_ref = refs[:4]
        elif passthru is not None:
            a_ref, b_ref, _, o_ref = refs[:4]
        else:
            a_ref, b_ref, o_ref = refs[:3]
        if gk == 1:
            prod = _dot(a_ref[...], b_ref[...], dims)
            if init is not None:
                prod = prod + i_ref[...].astype(F32)
            o_ref[...] = prod.astype(out_dtype)
            return
        acc = refs[-1]
        kk = pl.program_id(2)

        @pl.when(kk == 0)
        def _():
            if init is not None:
                acc[...] = i_ref[...].astype(F32)
            else:
                acc[...] = jnp.zeros_like(acc)

        acc[...] += _dot(a_ref[...], b_ref[...], dims)

        @pl.when(kk == gk - 1)
        def _():
            o_ref[...] = acc[...].astype(out_dtype)

    in_specs = [pl.BlockSpec(a_blk, a_map), pl.BlockSpec(b_blk, b_map)]
    args = [a, b]
    aliases = {}
    if init is not None:
        in_specs.append(pl.BlockSpec((tm, tn), o_map))
        args.append(init)
        aliases = {2: 0}
    elif passthru is not None:
        in_specs.append(pl.BlockSpec(memory_space=pl.ANY))
        args.append(passthru)
        aliases = {2: 0}
    out_dt = extra.dtype if extra is not None else out_dtype
    assert out_dt == out_dtype
    return pl.pallas_call(
        body, name=name, grid=(gi, gj, gk),
        in_specs=in_specs, out_specs=pl.BlockSpec((tm, tn), o_map),
        out_shape=jax.ShapeDtypeStruct((out_rows, n), out_dtype),
        scratch_shapes=[pltpu.VMEM((tm, tn), F32)] if gk > 1 else [],
        input_output_aliases=aliases,
        compiler_params=_cp(("parallel", "parallel", "arbitrary")),
    )(*args)


def _ffn_fwd(x, g, wg_t, wu_t, wd, name):
    tm, fc = 512, 256
    nc = FF // fc

    def body(x_ref, g_ref, wg_ref, wu_ref, wd_ref, xo_ref, gg_ref, uu_ref, act_ref):
        xv = x_ref[...]
        r = lax.rsqrt(jnp.mean(xv * xv, axis=-1, keepdims=True) + EPS)
        h = (xv * r * g_ref[...]).astype(BF16)
        for c in range(nc):
            sl = pl.ds(c * fc, fc)
            gg = _dot(h, wg_ref[sl, :], NT)
            uu = _dot(h, wu_ref[sl, :], NT)
            gg_ref[:, sl] = gg.astype(BF16)
            uu_ref[:, sl] = uu.astype(BF16)
            act_ref[:, sl] = (gg * _sig(gg) * uu).astype(BF16)
        xo_ref[...] = xv + 0.5 * _dot(act_ref[...], wd_ref[...], NN)

    wspec = pl.BlockSpec((FF, D), lambda i: (0, 0), pipeline_mode=pl.Buffered(1))
    return pl.pallas_call(
        body, name=name, grid=(T // tm,),
        in_specs=[pl.BlockSpec((tm, D), lambda i: (i, 0)), _const_spec((1, D)), wspec, wspec, wspec],
        out_specs=[pl.BlockSpec((tm, D), lambda i: (i, 0)), pl.BlockSpec((tm, FF), lambda i: (i, 0)),
                   pl.BlockSpec((tm, FF), lambda i: (i, 0))],
        out_shape=[jax.ShapeDtypeStruct((T, D), F32), jax.ShapeDtypeStruct((T, FF), BF16),
                   jax.ShapeDtypeStruct((T, FF), BF16)],
        scratch_shapes=[pltpu.VMEM((tm, FF), BF16)],
        compiler_params=_cp(("parallel",)),
    )(x, g, wg_t, wu_t, wd)


def _ffn_bwd(x, g, gg_all, uu_all, dout, wg_t, wu_t, wd, name):
    tm, fc = 256, 256
    nc = FF // fc

    def body(x_ref, g_ref, gg_ref, uu_ref, do_ref, wg_ref, wu_ref, wd_ref,
             dx_ref, dgam_ref, dg_ref, du_ref, act_ref, h_ref, db_ref):
        i = pl.program_id(0)
        xv = x_ref[...]
        r = lax.rsqrt(jnp.mean(xv * xv, axis=-1, keepdims=True) + EPS)
        xhat = xv * r
        gam = g_ref[...]
        h_ref[...] = (xhat * gam).astype(BF16)
        dov = do_ref[...]
        dbv = (0.5 * dov).astype(BF16)
        db_ref[...] = dbv
        for c in range(nc):
            sl = pl.ds(c * fc, fc)
            da = _dot(dbv, wd_ref[sl, :], NT)
            gg = gg_ref[:, sl].astype(F32)
            uu = uu_ref[:, sl].astype(F32)
            s = _sig(gg)
            si = gg * s
            dgv = (da * uu * (s * (1.0 + gg * (1.0 - s)))).astype(BF16)
            duv = (da * si).astype(BF16)
            dg_ref[:, sl] = dgv
            du_ref[:, sl] = duv
            act_ref[:, sl] = (si * uu).astype(BF16)
        dh = _dot(dg_ref[...], wg_ref[...], NN) + _dot(du_ref[...], wu_ref[...], NN)

        @pl.when(i == 0)
        def _():
            dgam_ref[...] = jnp.zeros_like(dgam_ref)

        dgam_ref[...] += jnp.sum(dh * xhat, axis=0, keepdims=True)
        dxh = dh * gam
        dx_ref[...] = dov + r * (dxh - xhat * jnp.mean(dxh * xhat, axis=-1, keepdims=True))

    wspec = pl.BlockSpec((FF, D), lambda i: (0, 0), pipeline_mode=pl.Buffered(1))
    row_d = pl.BlockSpec((tm, D), lambda i: (i, 0))
    row_f = pl.BlockSpec((tm, FF), lambda i: (i, 0))
    return pl.pallas_call(
        body, name=name, grid=(T // tm,),
        in_specs=[row_d, _const_spec((1, D)), row_f, row_f, row_d, wspec, wspec, wspec],
        out_specs=[row_d, _const_spec((1, D)), row_f, row_f, row_f, row_d, row_d],
        out_shape=[jax.ShapeDtypeStruct((T, D), F32), jax.ShapeDtypeStruct((1, D), F32),
                   jax.ShapeDtypeStruct((T, FF), BF16), jax.ShapeDtypeStruct((T, FF), BF16),
                   jax.ShapeDtypeStruct((T, FF), BF16), jax.ShapeDtypeStruct((T, D), BF16),
                   jax.ShapeDtypeStruct((T, D), BF16)],
        compiler_params=_cp(("arbitrary",)),
    )(x, g, gg_all, uu_all, dout, wg_t, wu_t, wd)


def _wgrad(a, b, m, n, name):
    tm = m // 2 if m == FF else m
    return _mm(a, b, mode="tn", m=m, n=n, k=T, tm=tm, tn=n, tk=min(T, 2048), out_dtype=BF16, name=name)


PERM_TM = 512
DILS = tuple(d for _, d in GROUPS if d > 1)


def _perm_spec(dil, cols):
    return pl.BlockSpec((dil, PERM_TM // dil, cols), lambda i: (0, i, 0))


def _perm_shape(dil, cols, dtype):
    return jax.ShapeDtypeStruct((dil, T // dil, cols), dtype)


LANES = 128


def _tile_scratch(cols):
    return pltpu.VMEM((cols // LANES, PERM_TM, LANES), F32)


def _put_tile(tile, value):
    for c in range(tile.shape[0]):
        tile[c] = value[:, c * LANES:(c + 1) * LANES]


def _get_tile(tile):
    return jnp.concatenate([tile[c] for c in range(tile.shape[0])], axis=1)


def _store_perm(out_ref, tile, dil):
    for r in range(dil):
        for c in range(tile.shape[0]):
            out_ref[r, :, pl.ds(c * LANES, LANES)] = tile[c, pl.ds(r, PERM_TM // dil, stride=dil), :].astype(
                out_ref.dtype)


def _load_unperm(in_ref, tile, dil):
    for r in range(dil):
        for c in range(tile.shape[0]):
            tile[c, pl.ds(r, PERM_TM // dil, stride=dil), :] = in_ref[r, :, pl.ds(c * LANES, LANES)].astype(F32)


def _norm_cast(x, g, name):
    tm = PERM_TM

    def body(x_ref, g_ref, h_ref, *rest):
        p_refs, tile = rest[:-1], rest[-1]
        xv = x_ref[...]
        r = lax.rsqrt(jnp.mean(xv * xv, axis=-1, keepdims=True) + EPS)
        hv = xv * r * g_ref[...]
        h_ref[...] = hv.astype(BF16)
        _put_tile(tile, hv)
        for dil, p_ref in zip(DILS, p_refs):
            _store_perm(p_ref, tile, dil)

    out = pl.pallas_call(
        body, name=name, grid=(T // tm,),
        in_specs=[pl.BlockSpec((tm, D), lambda i: (i, 0)), _const_spec((1, D))],
        out_specs=[pl.BlockSpec((tm, D), lambda i: (i, 0))] + [_perm_spec(d, D) for d in DILS],
        out_shape=[jax.ShapeDtypeStruct((T, D), BF16)] + [_perm_shape(d, D, BF16) for d in DILS],
        scratch_shapes=[_tile_scratch(D)],
        compiler_params=_cp(("parallel",)),
    )(x, g)
    return [out[0]] + [o.reshape(T, D) for o in out[1:]]


def _final(x3, gf, tgt, name):
    tm = 512

    def body(x_ref, g_ref, t_ref, dx_ref, dgam_ref, loss_ref):
        i = pl.program_id(0)
        xv = x_ref[...]
        r = lax.rsqrt(jnp.mean(xv * xv, axis=-1, keepdims=True) + EPS)
        xhat = xv * r
        gam = g_ref[...]
        err = xhat * gam - t_ref[...]
        part = 0.5 * jnp.sum(jnp.mean(err * err, axis=-1, keepdims=True), axis=0, keepdims=True)
        dy = err * (1.0 / D)

        @pl.when(i == 0)
        def _():
            dgam_ref[...] = jnp.zeros_like(dgam_ref)
            loss_ref[...] = jnp.zeros_like(loss_ref)

        dgam_ref[...] += jnp.sum(dy * xhat, axis=0, keepdims=True)
        loss_ref[...] += jnp.broadcast_to(part, loss_ref.shape)
        dxh = dy * gam
        dx_ref[...] = r * (dxh - xhat * jnp.mean(dxh * xhat, axis=-1, keepdims=True))

    row_d = pl.BlockSpec((tm, D), lambda i: (i, 0))
    return pl.pallas_call(
        body, name=name, grid=(T // tm,),
        in_specs=[row_d, _const_spec((1, D)), row_d],
        out_specs=[row_d, _const_spec((1, D)), _const_spec((1, 128))],
        out_shape=[jax.ShapeDtypeStruct((T, D), F32), jax.ShapeDtypeStruct((1, D), F32),
                   jax.ShapeDtypeStruct((1, 128), F32)],
        compiler_params=_cp(("arbitrary",)),
    )(x3, gf, tgt)


def _rms_bwd(x, g, dhs, dres, name):
    tm = PERM_TM
    dils = [d for _, d in GROUPS]
    nh = len(dhs)
    assert nh == len(dils)

    def body(*refs):
        x_ref, g_ref = refs[:2]
        dh_refs = refs[2:2 + nh]
        dr_ref, dx_ref, dgam_ref, tile = refs[2 + nh:]
        i = pl.program_id(0)
        xv = x_ref[...]
        r = lax.rsqrt(jnp.mean(xv * xv, axis=-1, keepdims=True) + EPS)
        xhat = xv * r
        gam = g_ref[...]
        dh = None
        for dil, ref in zip(dils, dh_refs):
            if dil == 1:
                part = ref[...]
            else:
                _load_unperm(ref, tile, dil)
                part = _get_tile(tile)
            dh = part if dh is None else dh + part

        @pl.when(i == 0)
        def _():
            dgam_ref[...] = jnp.zeros_like(dgam_ref)

        dgam_ref[...] += jnp.sum(dh * xhat, axis=0, keepdims=True)
        dxh = dh * gam
        dx_ref[...] = dr_ref[...] + r * (dxh - xhat * jnp.mean(dxh * xhat, axis=-1, keepdims=True))

    row_d = pl.BlockSpec((tm, D), lambda i: (i, 0))
    dh_specs = [row_d if d == 1 else _perm_spec(d, D) for d in dils]
    dh_args = [a if d == 1 else a.reshape(d, T // d, D) for d, a in zip(dils, dhs)]
    return pl.pallas_call(
        body, name=name, grid=(T // tm,),
        in_specs=[row_d, _const_spec((1, D))] + dh_specs + [row_d],
        out_specs=[row_d, _const_spec((1, D))],
        out_shape=[jax.ShapeDtypeStruct((T, D), F32), jax.ShapeDtypeStruct((1, D), F32)],
        scratch_shapes=[_tile_scratch(D)],
        compiler_params=_cp(("arbitrary",)),
    )(x, g, *dh_args, dres)


CONV_TM = 256
CONV_HALO = 32
CONV_RB = 16


def _glu(ab):
    ab = ab.astype(F32)
    return ab[:, :D] * _sig(ab[:, D:])


def _ln_stats(z1):
    mu = jnp.mean(z1, axis=-1, keepdims=True)
    zc = z1 - mu
    rstd = lax.rsqrt(jnp.mean(zc * zc, axis=-1, keepdims=True) + EPS)
    return zc * rstd, rstd


def _fill_shifts(zs):
    n = zs.shape[1] - 8
    for s in range(1, 8):
        zs[s, pl.ds(0, n), :] = zs[0, pl.ds(s, n), :]


def _shifted(zs, start, rows):
    q, s = divmod(start, 8)
    return zs[s, pl.ds(8 * q, rows), :]


def _conv_fwd(ab, kern, dwb, lng, lnb, name):
    tm, hl, rb = CONV_TM, CONV_HALO, CONV_RB
    off = hl - (CONV_W - 1)

    def body(ab_ref, abh_ref, k_ref, dwb_ref, lng_ref, lnb_ref, z1_ref, z3_ref, zs):
        i = pl.program_id(0)
        zs[0, pl.ds(0, hl), :] = jnp.where(i > 0, _glu(abh_ref[...]), 0.0)
        zs[0, pl.ds(hl, tm), :] = _glu(ab_ref[...])
        _fill_shifts(zs)
        for b in range(tm // rb):
            acc = jnp.zeros((rb, D), F32)
            for j in range(CONV_W):
                acc = acc + _shifted(zs, b * rb + off + j, rb) * k_ref[pl.ds(j, 1), :]
            z1 = acc + dwb_ref[...]
            z1_ref[pl.ds(b * rb, rb), :] = z1
            zn, _ = _ln_stats(z1)
            z2 = zn * lng_ref[...] + lnb_ref[...]
            z3_ref[pl.ds(b * rb, rb), :] = (z2 * _sig(z2)).astype(BF16)

    row = pl.BlockSpec((tm, D), lambda i: (i, 0))
    return pl.pallas_call(
        body, name=name, grid=(T // tm,),
        in_specs=[pl.BlockSpec((tm, 2 * D), lambda i: (i, 0)),
                  pl.BlockSpec((hl, 2 * D), lambda i: (jnp.maximum(i * (tm // hl) - 1, 0), 0)),
                  _const_spec((32, D)), _const_spec((1, D)), _const_spec((1, D)), _const_spec((1, D))],
        out_specs=[row, row],
        out_shape=[jax.ShapeDtypeStruct((T, D), F32), jax.ShapeDtypeStruct((T, D), BF16)],
        scratch_shapes=[pltpu.VMEM((8, hl + tm, D), F32)],
        compiler_params=_cp(("parallel",)),
    )(ab, ab, kern, dwb, lng, lnb)


def _conv_bwd(dz3, z1, ab, kern, lng, lnb, name):
    tm, hl, rb = CONV_TM, CONV_HALO, CONV_RB
    off = hl - (CONV_W - 1)
    nsteps = T // tm

    def ln_bwd(dz3v, z1v, lngv, lnbv):
        zn, rstd = _ln_stats(z1v)
        z2 = zn * lngv + lnbv
        s = _sig(z2)
        dz2 = dz3v * (s * (1.0 + z2 * (1.0 - s)))
        dzn = dz2 * lngv
        dz1 = rstd * (dzn - jnp.mean(dzn, axis=-1, keepdims=True)
                      - zn * jnp.mean(dzn * zn, axis=-1, keepdims=True))
        return dz1, dz2, zn

    def body(dz3_ref, dz3h_ref, z1_ref, z1h_ref, ab_ref, abh_ref, k_ref, lng_ref, lnb_ref,
             dab_ref, dk_ref, dvec_ref, zs, dzs):
        i = pl.program_id(0)
        lngv, lnbv = lng_ref[...], lnb_ref[...]

        @pl.when(i == 0)
        def _():
            dk_ref[...] = jnp.zeros_like(dk_ref)
            dvec_ref[...] = jnp.zeros_like(dvec_ref)

        dz1, dz2, zn = ln_bwd(dz3_ref[...], z1_ref[...], lngv, lnbv)
        dvec_ref[pl.ds(0, 1), :] += jnp.sum(dz1, axis=0, keepdims=True)
        dvec_ref[pl.ds(1, 1), :] += jnp.sum(dz2 * zn, axis=0, keepdims=True)
        dvec_ref[pl.ds(2, 1), :] += jnp.sum(dz2, axis=0, keepdims=True)
        dzs[0, pl.ds(0, tm), :] = dz1
        dz1h, _, _ = ln_bwd(dz3h_ref[...], z1h_ref[...], lngv, lnbv)
        dzs[0, pl.ds(tm, hl), :] = jnp.where(i < nsteps - 1, dz1h, 0.0)
        _fill_shifts(dzs)
        zs[0, pl.ds(0, hl), :] = jnp.where(i > 0, _glu(abh_ref[...]), 0.0)
        zs[0, pl.ds(hl, tm), :] = _glu(ab_ref[...])
        _fill_shifts(zs)

        for j in range(CONV_W):
            tot = jnp.zeros((rb, D), F32)
            for b in range(tm // rb):
                tot = tot + dzs[0, pl.ds(b * rb, rb), :] * _shifted(zs, b * rb + off + j, rb)
            dk_ref[pl.ds(j, 1), :] += jnp.sum(tot, axis=0, keepdims=True)

        for b in range(tm // rb):
            acc = jnp.zeros((rb, D), F32)
            for j in range(CONV_W):
                acc = acc + _shifted(dzs, b * rb + (CONV_W - 1) - j, rb) * k_ref[pl.ds(j, 1), :]
            av = ab_ref[pl.ds(b * rb, rb), pl.ds(0, D)].astype(F32)
            sb = _sig(ab_ref[pl.ds(b * rb, rb), pl.ds(D, D)].astype(F32))
            dab_ref[pl.ds(b * rb, rb), pl.ds(0, D)] = (acc * sb).astype(BF16)
            dab_ref[pl.ds(b * rb, rb), pl.ds(D, D)] = (acc * av * sb * (1.0 - sb)).astype(BF16)

    row = pl.BlockSpec((tm, D), lambda i: (i, 0))
    nxt = pl.BlockSpec((hl, D), lambda i: (jnp.minimum((i + 1) * (tm // hl), T // hl - 1), 0))
    return pl.pallas_call(
        body, name=name, grid=(nsteps,),
        in_specs=[row, nxt, row, nxt,
                  pl.BlockSpec((tm, 2 * D), lambda i: (i, 0)),
                  pl.BlockSpec((hl, 2 * D), lambda i: (jnp.maximum(i * (tm // hl) - 1, 0), 0)),
                  _const_spec((32, D)), _const_spec((1, D)), _const_spec((1, D))],
        out_specs=[pl.BlockSpec((tm, 2 * D), lambda i: (i, 0)), _const_spec((32, D)), _const_spec((8, D))],
        out_shape=[jax.ShapeDtypeStruct((T, 2 * D), BF16), jax.ShapeDtypeStruct((32, D), F32),
                   jax.ShapeDtypeStruct((8, D), F32)],
        scratch_shapes=[pltpu.VMEM((8, hl + tm, D), F32), pltpu.VMEM((8, tm + hl, D), F32)],
        compiler_params=_cp(("arbitrary",)),
    )(dz3, dz3, z1, z1, ab, ab, kern, lng, lnb)


def _alibi_slopes():
    h = np.arange(1, 3 * NHG + 1, dtype=np.float32)
    return np.power(np.float32(2.0), -8.0 * h / np.float32(3 * NHG)).astype(np.float32)


def _band_bias(gi):
    _, dil = GROUPS[gi]
    slopes = _alibi_slopes()[gi * NHG:(gi + 1) * NHG]
    qi = np.arange(BLK)[:, None]
    ki = np.arange(2 * BLK)[None, :]
    steps = BLK + qi - ki
    band = (steps >= 0) & (steps <= BLK)
    bias = -slopes[:, None, None] * (dil * steps).astype(np.float32)[None]
    return jnp.asarray(np.where(band[None], bias, np.float32(NEG)).astype(np.float32))


QB_FWD = 4
QB_BWD = 8


def _attn_specs(qb):
    prev = lambda n: jnp.maximum(n * qb - 1, 0)
    return [pl.BlockSpec((qb * BLK, HEAD), lambda h, n: (n, h)),
            pl.BlockSpec((BLK, HEAD), lambda h, n: (prev(n), NHG + h)),
            pl.BlockSpec((qb * BLK, HEAD), lambda h, n: (n, NHG + h)),
            pl.BlockSpec((BLK, HEAD), lambda h, n: (prev(n), 2 * NHG + h)),
            pl.BlockSpec((qb * BLK, HEAD), lambda h, n: (n, 2 * NHG + h)),
            pl.BlockSpec((None, BLK, 2 * BLK), lambda h, n: (h, 0, 0))]


def _scores(q, kcat, bias, blk, seg):
    s = _dot(q, kcat, NT) * (HEAD ** -0.5) + bias
    col = lax.broadcasted_iota(jnp.int32, s.shape, 1)
    first = (blk % seg) == 0
    return jnp.where(jnp.logical_and(first, col < BLK), NEG, s)


def _attn_fwd(qkv, gi, name):
    seg = (T // GROUPS[gi][1]) // BLK

    qb = QB_FWD

    def body(q_ref, kp_ref, kc_ref, vp_ref, vc_ref, bias_ref, o_ref, l_ref):
        n = pl.program_id(0)
        for h in range(NHG):
            cols = pl.ds(h * HEAD, HEAD)
            kwin = jnp.concatenate([kp_ref[:, cols], kc_ref[:, cols]], axis=0)
            vwin = jnp.concatenate([vp_ref[:, cols], vc_ref[:, cols]], axis=0)
            bias = bias_ref[h]
            for b in range(qb):
                rows = pl.ds(b * BLK, BLK)
                s = _scores(q_ref[rows, cols], kwin[b * BLK:(b + 2) * BLK], bias, n * qb + b, seg)
                mx = jnp.max(s, axis=-1, keepdims=True)
                p = jnp.exp(s - mx)
                den = jnp.sum(p, axis=-1, keepdims=True)
                o_ref[rows, cols] = _dot(p.astype(BF16), vwin[b * BLK:(b + 2) * BLK], NN) / den
                l_ref[rows, cols] = jnp.broadcast_to(mx + jnp.log(den), (BLK, HEAD))

    prev = lambda n: jnp.maximum(n * qb - 1, 0)
    cur = lambda part: pl.BlockSpec((qb * BLK, AW), lambda n: (n, part))
    halo = lambda part: pl.BlockSpec((BLK, AW), lambda n: (prev(n), part))
    return pl.pallas_call(
        body, name=name, grid=(T // (qb * BLK),),
        in_specs=[cur(0), halo(1), cur(1), halo(2), cur(2), _const_spec((NHG, BLK, 2 * BLK))],
        out_specs=[cur(0), cur(0)],
        out_shape=[jax.ShapeDtypeStruct((T, AW), F32), jax.ShapeDtypeStruct((T, AW), F32)],
        compiler_params=_cp(("parallel",)),
    )(qkv, qkv, qkv, qkv, qkv, _band_bias(gi))


def _attn_bwd(qkv, dob, lse, delta, gi, name):
    seg = (T // GROUPS[gi][1]) // BLK
    qb = QB_BWD
    nb = T // (qb * BLK)
    scale = HEAD ** -0.5

    def body(q_ref, kp_ref, kc_ref, vp_ref, vc_ref, bias_ref, do_ref, l_ref, dl_ref, out_ref, dk_acc, dv_acc):
        n = pl.program_id(1)
        kwin = jnp.concatenate([kp_ref[...], kc_ref[...]], axis=0)
        vwin = jnp.concatenate([vp_ref[...], vc_ref[...]], axis=0)
        bias = bias_ref[...]
        for b in range(qb):
            rows = pl.ds(b * BLK, BLK)
            q = q_ref[rows, :]
            kcat = kwin[b * BLK:(b + 2) * BLK]
            s = _scores(q, kcat, bias, n * qb + b, seg)
            p = jnp.exp(s - l_ref[rows, pl.ds(0, 1)])
            dov = do_ref[rows, :]
            dv2 = _dot(p.astype(BF16), dov, TN)
            dp = _dot(dov, vwin[b * BLK:(b + 2) * BLK], NT)
            dsb = (p * (dp - dl_ref[rows, pl.ds(0, 1)]) * scale).astype(BF16)
            row = pl.ds(pl.multiple_of((n * qb + b) * BLK, BLK), BLK)
            out_ref[0, row, :] = _dot(dsb, kcat, NN).astype(BF16)
            dk2 = _dot(dsb, q, TN)
            dk_acc[row, :] = dk2[BLK:]
            dv_acc[row, :] = dv2[BLK:]

            def add_prev(dk2=dk2, dv2=dv2, b=b):
                prow = pl.ds(pl.multiple_of((n * qb + b - 1) * BLK, BLK), BLK)
                dk_acc[prow, :] += dk2[:BLK]
                dv_acc[prow, :] += dv2[:BLK]

            if b == 0:
                pl.when(n > 0)(add_prev)
            else:
                add_prev()

        @pl.when(n == nb - 1)
        def _():
            out_ref[1] = dk_acc[...].astype(BF16)
            out_ref[2] = dv_acc[...].astype(BF16)

    oblk = pl.BlockSpec((qb * BLK, HEAD), lambda h, n: (n, h))
    return pl.pallas_call(
        body, name=name, grid=(NHG, nb),
        in_specs=_attn_specs(qb) + [oblk, oblk, oblk],
        out_specs=pl.BlockSpec((3, T, HEAD), lambda h, n: (0, 0, h)),
        out_shape=jax.ShapeDtypeStruct((3, T, AW), BF16),
        scratch_shapes=[pltpu.VMEM((T, HEAD), F32), pltpu.VMEM((T, HEAD), F32)],
        compiler_params=_cp(("parallel", "arbitrary")),
    )(qkv, qkv, qkv, qkv, qkv, _band_bias(gi), dob, lse, delta)


def _merge(outs, lses, name):
    tm = PERM_TM
    dils = [d for _, d in GROUPS]
    ng = len(dils)

    def body(*refs):
        in_refs = refs[:2 * ng]
        a_ref, ab_ref = refs[2 * ng:2 * ng + 2]
        lse_refs = refs[2 * ng + 2:3 * ng + 2]
        tile = refs[-1]

        def token_order(ref, dil):
            if dil == 1:
                return ref[...]
            _load_unperm(ref, tile, dil)
            return _get_tile(tile)

        os = [token_order(in_refs[2 * i], d) for i, d in enumerate(dils)]
        ls = [token_order(in_refs[2 * i + 1], d) for i, d in enumerate(dils)]
        mx = jnp.maximum(jnp.maximum(ls[0], ls[1]), ls[2])
        es = [jnp.exp(v - mx) for v in ls]
        tot = es[0] + es[1] + es[2]
        att = (es[0] / tot) * os[0] + (es[1] / tot) * os[1] + (es[2] / tot) * os[2]
        a_ref[...] = att
        ab_ref[...] = att.astype(BF16)
        lse = mx + jnp.log(tot)
        _put_tile(tile, lse)
        for dil, ref in zip(dils, lse_refs):
            if dil == 1:
                ref[...] = lse
            else:
                _store_perm(ref, tile, dil)

    row = pl.BlockSpec((tm, AW), lambda i: (i, 0))
    specs = [row if d == 1 else _perm_spec(d, AW) for d in dils]
    args = []
    for d, o, l in zip(dils, outs, lses):
        args += [o, l] if d == 1 else [o.reshape(d, T // d, AW), l.reshape(d, T // d, AW)]
    out = pl.pallas_call(
        body, name=name, grid=(T // tm,),
        in_specs=[sp for sp in specs for _ in range(2)], out_specs=[row, row] + specs,
        out_shape=[jax.ShapeDtypeStruct((T, AW), F32), jax.ShapeDtypeStruct((T, AW), BF16)]
        + [jax.ShapeDtypeStruct((T, AW), F32) if d == 1 else _perm_shape(d, AW, F32) for d in dils],
        scratch_shapes=[_tile_scratch(AW)],
        compiler_params=_cp(("parallel",)),
    )(*args)
    return out[0], out[1], [o.reshape(T, AW) for o in out[2:]]


def _mix_out(z3b, attnb, gates, wc, wa_t, wo, x1, name):
    tm = 512

    def body(z_ref, a_ref, g_ref, wc_ref, wa_ref, wo_ref, x_ref, xo_ref, yc_ref, ya_ref, mx_ref):
        yc = _dot(z_ref[...], wc_ref[...], NN)
        ya = _dot(a_ref[...], wa_ref[...], NT)
        yc_ref[...] = yc
        ya_ref[...] = ya
        gv = g_ref[...].astype(F32)
        mixed = (_sig(gv[:, :D]) * yc + _sig(gv[:, D:]) * ya).astype(BF16)
        mx_ref[...] = mixed
        xo_ref[...] = x_ref[...] + _dot(mixed, wo_ref[...], NN)

    row = pl.BlockSpec((tm, D), lambda i: (i, 0))
    return pl.pallas_call(
        body, name=name, grid=(T // tm,),
        in_specs=[row, pl.BlockSpec((tm, AW), lambda i: (i, 0)), pl.BlockSpec((tm, 2 * D), lambda i: (i, 0)),
                  _const_spec((D, D)), _const_spec((D, AW)), _const_spec((D, D)), row],
        out_specs=[row, row, row, row],
        out_shape=[jax.ShapeDtypeStruct((T, D), F32), jax.ShapeDtypeStruct((T, D), F32),
                   jax.ShapeDtypeStruct((T, D), F32), jax.ShapeDtypeStruct((T, D), BF16)],
        compiler_params=_cp(("parallel",)),
    )(z3b, attnb, gates, wc, wa_t, wo, x1)


def _mix_out_bwd(dx2, gates, yc, ya, attn, wc, wa_t, wo, name):
    tm = PERM_TM
    dils = [d for _, d in GROUPS]
    ng = len(dils)

    def body(dx_ref, g_ref, yc_ref, ya_ref, at_ref, wc_ref, wa_ref, wo_ref,
             dg_ref, dyc_ref, dya_ref, dxb_ref, dz3_ref, *rest):
        dat_refs, dl_refs, tile = rest[:ng], rest[ng:2 * ng], rest[-1]
        dxb = dx_ref[...].astype(BF16)
        dxb_ref[...] = dxb
        dmix = _dot(dxb, wo_ref[...], NT)
        gv = g_ref[...].astype(F32)
        sc = _sig(gv[:, :D])
        sa = _sig(gv[:, D:])
        ycv, yav = yc_ref[...], ya_ref[...]
        dg_ref[:, pl.ds(0, D)] = (dmix * ycv * sc * (1.0 - sc)).astype(BF16)
        dg_ref[:, pl.ds(D, D)] = (dmix * yav * sa * (1.0 - sa)).astype(BF16)
        dyc = (dmix * sc).astype(BF16)
        dya = (dmix * sa).astype(BF16)
        dyc_ref[...] = dyc
        dya_ref[...] = dya
        dz3_ref[...] = _dot(dyc, wc_ref[...], NT)
        dat = _dot(dya, wa_ref[...], NN)
        prod = dat * at_ref[...]
        delta = jnp.concatenate(
            [jnp.broadcast_to(jnp.sum(prod[:, h * HEAD:(h + 1) * HEAD], axis=-1, keepdims=True), (tm, HEAD))
             for h in range(NHG)], axis=1)
        for value, out_refs in ((dat, dat_refs), (delta, dl_refs)):
            _put_tile(tile, value)
            for dil, ref in zip(dils, out_refs):
                if dil == 1:
                    ref[...] = value.astype(ref.dtype)
                else:
                    _store_perm(ref, tile, dil)

    row = pl.BlockSpec((tm, D), lambda i: (i, 0))
    row2 = pl.BlockSpec((tm, 2 * D), lambda i: (i, 0))
    rowa = pl.BlockSpec((tm, AW), lambda i: (i, 0))
    aspecs = [rowa if d == 1 else _perm_spec(d, AW) for d in dils]

    def ashapes(dtype):
        return [jax.ShapeDtypeStruct((T, AW), dtype) if d == 1 else _perm_shape(d, AW, dtype) for d in dils]

    out = pl.pallas_call(
        body, name=name, grid=(T // tm,),
        in_specs=[row, row2, row, row, rowa, _const_spec((D, D)), _const_spec((D, AW)), _const_spec((D, D))],
        out_specs=[row2, row, row, row, row] + aspecs + aspecs,
        out_shape=[jax.ShapeDtypeStruct((T, 2 * D), BF16), jax.ShapeDtypeStruct((T, D), BF16),
                   jax.ShapeDtypeStruct((T, D), BF16), jax.ShapeDtypeStruct((T, D), BF16),
                   jax.ShapeDtypeStruct((T, D), F32)] + ashapes(BF16) + ashapes(F32),
        scratch_shapes=[_tile_scratch(AW)],
        compiler_params=_cp(("parallel",)),
    )(dx2, gates, yc, ya, attn, wc, wa_t, wo)
    dats = [o.reshape(T, AW) for o in out[5:5 + ng]]
    deltas = [o.reshape(T, AW) for o in out[5 + ng:5 + 2 * ng]]
    return out[0], out[1], out[2], out[3], out[4], dats, deltas


def _peer(k):
    x, y, c = lax.axis_index("x"), lax.axis_index("y"), lax.axis_index("c")
    px = 1 - x if k & 4 else x
    py = 1 - y if k & 2 else y
    pc = 1 - c if k & 1 else c
    return (px, py, pc), 4 * px + 2 * py + pc


HBM_SPEC = pl.BlockSpec(memory_space=pltpu.HBM)
SEM_SPEC = pl.BlockSpec(memory_space=pltpu.SEMAPHORE)
EFFECT = pltpu.SideEffectType.DATAFLOW_SIDE_EFFECTING


def _my_place():
    return 4 * lax.axis_index("x") + 2 * lax.axis_index("y") + lax.axis_index("c")


def _tie(a, order_after, name):
    na = len(order_after)

    def body(*refs):
        del refs

    return pl.pallas_call(
        body, name=name, in_specs=[pl.BlockSpec(memory_space=pl.ANY)] * (1 + na),
        out_specs=pl.BlockSpec(memory_space=pl.ANY), out_shape=jax.ShapeDtypeStruct(a.shape, a.dtype),
        input_output_aliases={0: 0},
    )(a, *order_after)


def _prep_gather(ws, transposed, order_after, name):
    me = jnp.reshape(_my_place(), (1,)).astype(jnp.int32)
    n = len(ws)
    na = len(order_after)
    shapes = []
    for wv, tr in zip(ws, transposed):
        r, c = (wv.shape[1], wv.shape[0]) if tr else wv.shape
        shapes.append(((32, c), F32) if r == CONV_W else ((r, c), BF16))

    def body(me_ref, *refs):
        del me_ref
        ins, outs = refs[:n], refs[n + na:]
        for wv, tr, i_ref, o_ref in zip(ws, transposed, ins, outs):
            if wv.shape[0] == CONV_W:
                o_ref[pl.ds(0, CONV_W), :] = i_ref[...]
                o_ref[pl.ds(CONV_W, 1), :] = jnp.zeros((1, wv.shape[1]), F32)
            elif tr:
                o_ref[...] = jnp.transpose(i_ref[...]).astype(BF16)
            else:
                o_ref[...] = i_ref[...].astype(BF16)

    grid_spec = pltpu.PrefetchScalarGridSpec(
        num_scalar_prefetch=1, grid=(1,),
        in_specs=[pl.BlockSpec(wv.shape, lambda i, m: (0, 0)) for wv in ws]
        + [pl.BlockSpec(memory_space=pl.ANY)] * na,
        out_specs=[pl.BlockSpec(shp, lambda i, m: (m[0], 0)) for shp, _ in shapes])
    return pl.pallas_call(
        body, name=name, grid_spec=grid_spec,
        out_shape=[jax.ShapeDtypeStruct((NDEV * shp[0], shp[1]), dt) for shp, dt in shapes],
        compiler_params=_cp(("arbitrary",)),
    )(me, *ws, *order_after)


GATHER_A = ((1, 0), (2, 0), (4, 0), (6, 0))
GATHER_B = ((1, 2), (1, 4), (1, 6))


def _gather_start(lands, plan, order_after, name):
    n = len(lands)
    na = len(order_after)
    npl = len(plan)

    def body(*refs):
        land_refs = refs[:n]
        send, recv = refs[n + na], refs[n + na + 1]
        token = refs[-1]
        for w in range(n):
            rows = lands[w].shape[0] // NDEV
            for p, (k, j) in enumerate(plan):
                peer, _ = _peer(k)
                _, blk = _peer(j)
                part = land_refs[w].at[pl.ds(blk * rows, rows)]
                i = w * npl + p
                pltpu.make_async_remote_copy(src_ref=part, dst_ref=part, send_sem=send.at[i], recv_sem=recv.at[i],
                                             device_id=peer, device_id_type=MESH_ID).start()
        token[...] = jnp.zeros_like(token)

    nsem = n * npl
    bufs = [pltpu.with_memory_space_constraint(a, pltpu.HBM) for a in lands]
    out = pl.pallas_call(
        body, name=name,
        in_specs=[HBM_SPEC] * n + [pl.BlockSpec(memory_space=pl.ANY)] * na,
        out_specs=[SEM_SPEC, SEM_SPEC] + [HBM_SPEC] * n + [pl.BlockSpec(memory_space=pltpu.VMEM)],
        out_shape=[pltpu.SemaphoreType.DMA((nsem,)), pltpu.SemaphoreType.DMA((nsem,))]
        + [pltpu.HBM(a.shape, a.dtype) for a in bufs] + [jax.ShapeDtypeStruct((8, 128), F32)],
        input_output_aliases={i: 2 + i for i in range(n)},
        compiler_params=pltpu.CompilerParams(has_side_effects=EFFECT),
    )(*bufs, *order_after)
    return out[0], out[1], out[2:2 + n], out[-1]


def _gather_wait(started, plan, order_after, name):
    send, recv, lands, _ = started
    n = len(lands)
    na = len(order_after)
    npl = len(plan)

    def body(*refs):
        land_refs = refs[:n]
        send_ref, recv_ref = refs[n], refs[n + 1]
        for w in range(n):
            rows = lands[w].shape[0] // NDEV
            for p, (k, j) in enumerate(plan):
                peer, _ = _peer(k)
                _, blk = _peer(j)
                part = land_refs[w].at[pl.ds(blk * rows, rows)]
                i = w * npl + p
                cp = pltpu.make_async_remote_copy(src_ref=part, dst_ref=part, send_sem=send_ref.at[i],
                                                  recv_sem=recv_ref.at[i], device_id=peer, device_id_type=MESH_ID)
                cp.wait_send()
                cp.wait_recv()

    out = pl.pallas_call(
        body, name=name,
        in_specs=[HBM_SPEC] * n + [SEM_SPEC, SEM_SPEC] + [pl.BlockSpec(memory_space=pl.ANY)] * na,
        out_specs=[HBM_SPEC] * n,
        out_shape=[pltpu.HBM(a.shape, a.dtype) for a in lands],
        input_output_aliases={i: i for i in range(n)},
        compiler_params=pltpu.CompilerParams(has_side_effects=EFFECT),
    )(*lands, send, recv, *order_after)
    return list(out)


def _copy_ends(kind, src, land, me, plin, k):
    if kind == "scatter":
        rows = src.shape[0] // NDEV
        return src.at[pl.ds(plin * rows, rows)], land.at[k - 1]
    return src, land.at[me]


def _landing(kind, src):
    me = _my_place()
    if kind == "scatter":
        return lax.empty((NDEV - 1, src.shape[0] // NDEV) + src.shape[1:], src.dtype)
    land = lax.empty((NDEV,) + src.shape, src.dtype)
    return lax.dynamic_update_slice(land, src[None], (me,) + (0,) * src.ndim)


def _send_start(kinds, srcs, order_after, name):
    n = len(srcs)
    lands = [_landing(kd, s) for kd, s in zip(kinds, srcs)]
    na = len(order_after)

    def body(*refs):
        src_refs, land_refs = refs[:n], refs[n:2 * n]
        send, recv = refs[2 * n + na], refs[2 * n + na + 1]
        token = refs[-1]
        _, me = _peer(0)
        for w in range(n):
            for k in range(1, NDEV):
                peer, plin = _peer(k)
                s, d = _copy_ends(kinds[w], src_refs[w], land_refs[w], me, plin, k)
                i = w * (NDEV - 1) + k - 1
                pltpu.make_async_remote_copy(src_ref=s, dst_ref=d, send_sem=send.at[i], recv_sem=recv.at[i],
                                             device_id=peer, device_id_type=MESH_ID).start()
        token[...] = jnp.zeros_like(token)

    nsem = n * (NDEV - 1)
    bufs = [pltpu.with_memory_space_constraint(a, pltpu.HBM) for a in list(srcs) + lands]
    out = pl.pallas_call(
        body, name=name,
        in_specs=[HBM_SPEC] * (2 * n) + [pl.BlockSpec(memory_space=pl.ANY)] * na,
        out_specs=[SEM_SPEC, SEM_SPEC] + [HBM_SPEC] * (2 * n) + [pl.BlockSpec(memory_space=pltpu.VMEM)],
        out_shape=[pltpu.SemaphoreType.DMA((nsem,)), pltpu.SemaphoreType.DMA((nsem,))]
        + [pltpu.HBM(a.shape, a.dtype) for a in bufs] + [jax.ShapeDtypeStruct((8, 128), F32)],
        input_output_aliases={i: 2 + i for i in range(2 * n)},
        compiler_params=pltpu.CompilerParams(has_side_effects=EFFECT),
    )(*bufs, *order_after)
    return out[0], out[1], out[2:2 + n], out[2 + n:2 + 2 * n], out[-1]


def _send_wait(kinds, started, order_after, name):
    send, recv, srcs, lands, _ = started
    n = len(srcs)
    na = len(order_after)

    def body(*refs):
        src_refs, land_refs = refs[:n], refs[n:2 * n]
        send_ref, recv_ref = refs[2 * n], refs[2 * n + 1]
        _, me = _peer(0)
        for w in range(n):
            for k in range(1, NDEV):
                peer, plin = _peer(k)
                s, d = _copy_ends(kinds[w], src_refs[w], land_refs[w], me, plin, k)
                i = w * (NDEV - 1) + k - 1
                cp = pltpu.make_async_remote_copy(src_ref=s, dst_ref=d, send_sem=send_ref.at[i],
                                                  recv_sem=recv_ref.at[i], device_id=peer, device_id_type=MESH_ID)
                cp.wait_send()
                cp.wait_recv()

    bufs = list(srcs) + list(lands)
    out = pl.pallas_call(
        body, name=name,
        in_specs=[HBM_SPEC] * (2 * n) + [SEM_SPEC, SEM_SPEC] + [pl.BlockSpec(memory_space=pl.ANY)] * na,
        out_specs=[HBM_SPEC] * (2 * n),
        out_shape=[pltpu.HBM(a.shape, a.dtype) for a in bufs],
        input_output_aliases={i: i for i in range(2 * n)},
        compiler_params=pltpu.CompilerParams(has_side_effects=EFFECT),
    )(*bufs, send, recv, *order_after)
    return out[:n], out[n:]


def _gsum(own, land, name):
    rows, cols = own.shape
    tr = rows // 2 if rows * cols > 512 * 1024 and rows % 32 == 0 else rows

    def body(own_ref, l_ref, o_ref):
        tot = own_ref[...].astype(F32)
        for s in range(NDEV - 1):
            tot = tot + l_ref[s].astype(F32)
        o_ref[...] = tot

    return pl.pallas_call(
        body, name=name, grid=(rows // tr,),
        in_specs=[pl.BlockSpec((tr, cols), lambda i: (i, 0)),
                  pl.BlockSpec((NDEV - 1, tr, cols), lambda i: (0, i, 0))],
        out_specs=pl.BlockSpec((tr, cols), lambda i: (i, 0)),
        out_shape=jax.ShapeDtypeStruct((rows, cols), F32),
        compiler_params=_cp(("parallel",)),
    )(own, land)


def _adamw_math(w, g, m, v):
    m2 = B1 * m + (1.0 - B1) * g
    v2 = B2 * v + (1.0 - B2) * (g * g)
    m_hat = m2 / (1.0 - B1 ** STEP)
    v_hat = v2 / (1.0 - B2 ** STEP)
    delta = -LR * (m_hat / (jnp.sqrt(v_hat) + AEPS) + WD * w)
    return delta, m2, v2


def _adamw(w, g, m, v, name):
    rows, cols = w.shape
    tr = 256 if rows % 256 == 0 and rows > 256 else rows

    def body(w_ref, g_ref, m_ref, v_ref, d_ref, mo_ref, vo_ref):
        d, m2, v2 = _adamw_math(w_ref[...], g_ref[...], m_ref[...], v_ref[...])
        d_ref[...] = d
        mo_ref[...] = m2
        vo_ref[...] = v2

    blk = pl.BlockSpec((tr, cols), lambda i: (i, 0))
    return pl.pallas_call(
        body, name=name, grid=(rows // tr,), in_specs=[blk] * 4, out_specs=[blk] * 3,
        out_shape=[jax.ShapeDtypeStruct((rows, cols), F32)] * 3,
        compiler_params=_cp(("parallel",)),
    )(w, g, m, v)


def _small_update(vland, w8, m8, v8, name):
    def body(l_ref, w_ref, m_ref, v_ref, g_ref, d_ref, mo_ref, vo_ref):
        g = l_ref[0]
        for s in range(1, NDEV):
            g = g + l_ref[s]
        g_ref[...] = g
        d, m2, v2 = _adamw_math(w_ref[...], g, m_ref[...], v_ref[...])
        d_ref[...] = d
        mo_ref[...] = m2
        vo_ref[...] = v2

    return pl.pallas_call(
        body, name=name, out_shape=[jax.ShapeDtypeStruct((8, D), F32)] * 4,
        compiler_params=_cp(None),
    )(vland, w8, m8, v8)


def kernel(x, ffn1_norm, ffn1_w_gate, ffn1_w_up, ffn1_w_down, mix_norm, w_in, conv_dw_kernel, conv_dw_bias, conv_ln_gain, conv_ln_bias, conv_w_out, attn_w_out, w_o, ffn2_norm, ffn2_w_gate, ffn2_w_up, ffn2_w_down, final_norm, loss_target, m_ffn1_norm, m_ffn1_w_gate, m_ffn1_w_up, m_ffn1_w_down, m_mix_norm, m_w_in, m_conv_dw_kernel, m_conv_dw_bias, m_conv_ln_gain, m_conv_ln_bias, m_conv_w_out, m_attn_w_out, m_w_o, m_ffn2_norm, m_ffn2_w_gate, m_ffn2_w_up, m_ffn2_w_down, m_final_norm, v_ffn1_norm, v_ffn1_w_gate, v_ffn1_w_up, v_ffn1_w_down, v_mix_norm, v_w_in, v_conv_dw_kernel, v_conv_dw_bias, v_conv_ln_gain, v_conv_ln_bias, v_conv_w_out, v_attn_w_out, v_w_o, v_ffn2_norm, v_ffn2_w_gate, v_ffn2_w_up, v_ffn2_w_down, v_final_norm):
    names = ["ffn1_norm", "ffn1_w_gate", "ffn1_w_up", "ffn1_w_down", "mix_norm", "w_in", "conv_dw_kernel",
             "conv_dw_bias", "conv_ln_gain", "conv_ln_bias", "conv_w_out", "attn_w_out", "w_o", "ffn2_norm",
             "ffn2_w_gate", "ffn2_w_up", "ffn2_w_down", "final_norm"]
    w = dict(ffn1_norm=ffn1_norm, ffn1_w_gate=ffn1_w_gate, ffn1_w_up=ffn1_w_up, ffn1_w_down=ffn1_w_down, mix_norm=mix_norm, w_in=w_in, conv_dw_kernel=conv_dw_kernel, conv_dw_bias=conv_dw_bias, conv_ln_gain=conv_ln_gain, conv_ln_bias=conv_ln_bias, conv_w_out=conv_w_out, attn_w_out=attn_w_out, w_o=w_o, ffn2_norm=ffn2_norm, ffn2_w_gate=ffn2_w_gate, ffn2_w_up=ffn2_w_up, ffn2_w_down=ffn2_w_down, final_norm=final_norm)
    mo = dict(ffn1_norm=m_ffn1_norm, ffn1_w_gate=m_ffn1_w_gate, ffn1_w_up=m_ffn1_w_up, ffn1_w_down=m_ffn1_w_down, mix_norm=m_mix_norm, w_in=m_w_in, conv_dw_kernel=m_conv_dw_kernel, conv_dw_bias=m_conv_dw_bias, conv_ln_gain=m_conv_ln_gain, conv_ln_bias=m_conv_ln_bias, conv_w_out=m_conv_w_out, attn_w_out=m_attn_w_out, w_o=m_w_o, ffn2_norm=m_ffn2_norm, ffn2_w_gate=m_ffn2_w_gate, ffn2_w_up=m_ffn2_w_up, ffn2_w_down=m_ffn2_w_down, final_norm=m_final_norm)
    vo = dict(ffn1_norm=v_ffn1_norm, ffn1_w_gate=v_ffn1_w_gate, ffn1_w_up=v_ffn1_w_up, ffn1_w_down=v_ffn1_w_down, mix_norm=v_mix_norm, w_in=v_w_in, conv_dw_kernel=v_conv_dw_kernel, conv_dw_bias=v_conv_dw_bias, conv_ln_gain=v_conv_ln_gain, conv_ln_bias=v_conv_ln_bias, conv_w_out=v_conv_w_out, attn_w_out=v_attn_w_out, w_o=v_w_o, ffn2_norm=v_ffn2_norm, ffn2_w_gate=v_ffn2_w_gate, ffn2_w_up=v_ffn2_w_up, ffn2_w_down=v_ffn2_w_down, final_norm=v_final_norm)
    col_sharded = ("ffn1_w_gate", "ffn1_w_up", "w_in", "attn_w_out", "ffn2_w_gate", "ffn2_w_up")
    row_sharded = ("ffn1_w_down", "conv_w_out", "w_o", "ffn2_w_down")
    small = ("ffn1_norm", "mix_norm", "ffn2_norm", "final_norm", "conv_dw_bias", "conv_ln_gain", "conv_ln_bias")

    ag_groups = (("ffn1_w_gate", "ffn1_w_up", "ffn1_w_down"),
                 ("w_in", "attn_w_out", "conv_w_out", "w_o", "conv_dw_kernel"),
                 ("ffn2_w_gate", "ffn2_w_up", "ffn2_w_down"))
    ag, order = [], []
    for gi, grp in enumerate(ag_groups):
        lands = _prep_gather([w[n][0] for n in grp], [n in col_sharded for n in grp], order, f"gather_prep{gi}")
        st = _gather_start(lands, GATHER_A, [], f"gather_a_start{gi}")
        ag.append(st)
        order = [st[3]]

    def chips_in(gi, after):
        lands = _gather_wait(ag[gi], GATHER_A, after, f"gather_a_wait{gi}")
        return _gather_start(lands, GATHER_B, [], f"gather_b_start{gi}")

    def all_in(gi, st, after):
        return _gather_wait(st, GATHER_B, after, f"gather_b_wait{gi}")

    x0 = x[0]
    tgt = loss_target[0]
    gf = final_norm.reshape(1, D)

    wg1, wu1, wd1 = all_in(0, chips_in(0, [ag[2][3]]), [])
    x1, gg1, uu1 = _ffn_fwd(x0, ffn1_norm, wg1, wu1, wd1, "ffn1_fwd")
    win_t, wa_t, wc, wo, kern_blocks = all_in(1, chips_in(1, [x1]), [])
    kern = kern_blocks.reshape(NDEV, 32, D // NDEV).transpose(1, 0, 2).reshape(32, D)
    h2p = _norm_cast(x1, mix_norm, "mix_norm_fwd")
    h2 = h2p[0]
    ab = _mm(h2, win_t, mode="nt", m=T, n=2 * D, k=D, tm=1024, tn=512, tk=D, out_dtype=BF16, name="proj_conv")
    gates = _mm(h2, win_t, mode="nt", m=T, n=2 * D, k=D, tm=1024, tn=512, tk=D, out_dtype=BF16,
                b_map=lambda i, j, kk: (13 + j, 0), name="proj_gates")
    qkv = []
    for gi in range(len(GROUPS)):
        qkv.append(_mm(h2p[gi], win_t, mode="nt", m=T, n=3 * AW, k=D, tm=1024, tn=AW, tk=D, out_dtype=BF16,
                       b_map=lambda i, j, kk, gi=gi: (4 + gi + 3 * j, 0), name=f"proj_qkv{gi}"))
    z1, z3b = _conv_fwd(ab, kern, conv_dw_bias, conv_ln_gain, conv_ln_bias, "conv_fwd")
    ffn2_b = chips_in(2, [z3b])
    outs, lses = [], []
    for gi, (_, dil) in enumerate(GROUPS):
        o, l = _attn_fwd(qkv[gi], gi, f"attn_fwd{gi}")
        outs.append(o)
        lses.append(l)
    attn, attnb, lse = _merge(outs, lses, "attn_merge")
    x2, yc, ya, mixedb = _mix_out(z3b, attnb, gates, wc, wa_t, wo, x1, "mix_out_fwd")
    wg2, wu2, wd2 = all_in(2, ffn2_b, [x2])
    x3, gg2, uu2 = _ffn_fwd(x2, ffn2_norm, wg2, wu2, wd2, "ffn2_fwd")

    dx3, dgf, loss_part = _final(x3, gf, tgt, "final_norm_loss")
    dx2, dg3, dgb, dub, actb, hb, dob = _ffn_bwd(x2, ffn2_norm, gg2, uu2, dx3, wg2, wu2, wd2, "ffn2_bwd")
    grads = {}
    grads["ffn2_w_gate"] = _wgrad(dgb, hb, FF, D, "ffn2_dwg")
    grads["ffn2_w_up"] = _wgrad(dub, hb, FF, D, "ffn2_dwu")
    grads["ffn2_w_down"] = _wgrad(actb, dob, FF, D, "ffn2_dwd")
    rs_groups = [("ffn2_w_gate", "ffn2_w_up", "ffn2_w_down"),
                 ("attn_w_out", "conv_w_out", "w_o", "conv_dw_kernel"),
                 ("w_in",),
                 ("ffn1_w_gate",), ("ffn1_w_up",), ("ffn1_w_down",)]
    last = len(rs_groups) - 1
    rs = [_send_start(["scatter"] * 3, [grads[n] for n in rs_groups[0]], [], "scatter_start0")]
    dx2 = _tie(dx2, [rs[0][4]], "tie_after_scatter0")

    dgates, dycb, dyab, dx2b, dz3, dattnb, delta = _mix_out_bwd(dx2, gates, yc, ya, attn, wc, wa_t, wo, "mix_out_bwd")
    grads["w_o"] = _wgrad(mixedb, dx2b, D, D, "dw_o")
    grads["conv_w_out"] = _wgrad(z3b, dycb, D, D, "dw_conv_out")
    grads["attn_w_out"] = _wgrad(dyab, attnb, D, AW, "dw_attn_out")
    dab, dkern, dvec = _conv_bwd(dz3, z1, ab, kern, conv_ln_gain, conv_ln_bias, "conv_bwd")
    grads["conv_dw_kernel"] = dkern.reshape(32, NDEV, D // NDEV).transpose(1, 0, 2).reshape(NDEV * 32, D // NDEV)
    rs.append(_send_start(["scatter"] * 4, [grads[n] for n in rs_groups[1]], [rs[0][4]], "scatter_start1"))
    dattnb = [_tie(a, [rs[1][4]], f"tie_after_scatter1_{i}") for i, a in enumerate(dattnb)]

    dqkv = []
    for gi, (_, dil) in enumerate(GROUPS):
        dq3 = _attn_bwd(qkv[gi], dattnb[gi], lse[gi], delta[gi], gi, f"attn_bwd{gi}")
        dqkv.append(dq3.reshape(3 * T, AW))

    dwin = _mm(dab, h2, mode="tn", m=2 * D, n=D, k=T, tm=2 * D, tn=D, tk=512, out_dtype=BF16, out_rows=IN_W,
               name="dw_in_conv")
    dwin = _mm(dgates, h2, mode="tn", m=2 * D, n=D, k=T, tm=512, tn=D, tk=1024, out_dtype=BF16, out_rows=IN_W,
               o_map=lambda i, j, kk: (13 + i, 0), passthru=dwin, name="dw_in_gates")
    for gi in range(3):
        dwin = _mm(dqkv[gi], h2p[gi], mode="tn", m=3 * AW, n=D, k=T, tm=AW, tn=D, tk=1024, out_dtype=BF16,
                   out_rows=IN_W, a_map=lambda i, j, kk: (i * (T // 1024) + kk, 0),
                   o_map=lambda i, j, kk, gi=gi: (4 + gi + 3 * i, 0), passthru=dwin, name=f"dw_in_qkv{gi}")
    grads["w_in"] = dwin
    rs.append(_send_start(["scatter"], [dwin], [rs[1][4]], "scatter_start2"))
    dab = _tie(dab, [rs[2][4]], "tie_after_scatter2")

    nrow = T // 1024
    dh = _mm(dab, win_t, mode="nn", m=T, n=D, k=2 * D, tm=1024, tn=D, tk=512, out_dtype=F32, name="dproj_conv")
    dh = _mm(dgates, win_t, mode="nn", m=T, n=D, k=2 * D, tm=1024, tn=D, tk=512, out_dtype=F32,
             b_map=lambda i, j, kk: (13 + kk, 0), init=dh, name="dproj_gates")
    dhs = []
    for gi, (_, dil) in enumerate(GROUPS):
        part = _mm(dqkv[gi], win_t, mode="nn", m=T, n=D, k=3 * AW, tm=1024, tn=D, tk=AW, out_dtype=F32,
                   a_map=lambda i, j, kk: (kk * nrow + i, 0), b_map=lambda i, j, kk, gi=gi: (4 + gi + 3 * kk, 0),
                   init=dh if gi == 0 else None, name=f"dproj_qkv{gi}")
        dhs.append(part)
    dx1, dg2 = _rms_bwd(x1, mix_norm, dhs, dx2, "mix_norm_bwd")

    dx0, dg1, dgb, dub, actb, hb, dob = _ffn_bwd(x0, ffn1_norm, gg1, uu1, dx1, wg1, wu1, wd1, "ffn1_bwd")
    grads["ffn1_w_gate"] = _wgrad(dgb, hb, FF, D, "ffn1_dwg")
    rs.append(_send_start(["scatter"], [grads["ffn1_w_gate"]], [rs[2][4]], "scatter_start3"))
    hb = _tie(hb, [rs[3][4]], "tie_after_scatter3")
    grads["ffn1_w_up"] = _wgrad(dub, hb, FF, D, "ffn1_dwu")
    rs.append(_send_start(["scatter"], [grads["ffn1_w_up"]], [rs[3][4]], "scatter_start4"))
    dob = _tie(dob, [rs[4][4]], "tie_after_scatter4")
    grads["ffn1_w_down"] = _wgrad(actb, dob, FF, D, "ffn1_dwd")
    vec = jnp.concatenate([dg1, dg2, dg3, dgf, dvec[0:3], jnp.broadcast_to(loss_part[:, :1], (1, D))], axis=0)
    rs.append(_send_start(["scatter", "bcast"], [grads["ffn1_w_down"], vec], [rs[4][4]], "scatter_start5"))

    g_out, d_out, m_out, v_out = {}, {}, {}, {}
    me = _my_place()
    after = [rs[last][4]]
    for gi, grp in enumerate(rs_groups):
        kinds = ["scatter"] * len(grp) + (["bcast"] if gi == last else [])
        srcs, lands = _send_wait(kinds, rs[gi], after, f"scatter_wait{gi}")
        for n, src, land in zip(grp, srcs, lands):
            rows = src.shape[0] // NDEV
            own = lax.dynamic_slice(src, (me * rows, 0), (rows, src.shape[1]))
            g = _gsum(own, land, f"gsum_{n}")
            if n in col_sharded:
                g = jnp.transpose(g)
            elif n == "conv_dw_kernel":
                g = g[:CONV_W]
            d, m2, v2 = _adamw(w[n][0], g, mo[n][0], vo[n][0], f"adamw_{n}")
            g_out[n], d_out[n], m_out[n], v_out[n] = g[None], d[None], m2[None], v2[None]
            after = [d]
    vland = lands[-1]

    def rows8(src):
        return jnp.concatenate([src[n].reshape(1, D) for n in small] + [jnp.ones((1, D), F32)], axis=0)

    g8, d8, m8, v8 = _small_update(vland, rows8(w), rows8(mo), rows8(vo), "small_update")
    for r, n in enumerate(small):
        shp = w[n].shape
        g_out[n], d_out[n], m_out[n], v_out[n] = (a[r].reshape(shp) for a in (g8, d8, m8, v8))
    loss = g8[7, 0]

    return (loss, dx0[None], *[g_out[n] for n in names], *[d_out[n] for n in names],
            *[m_out[n] for n in names], *[v_out[n] for n in names])
```

```python
import numpy as np
import jax
import jax.numpy as jnp
from jax import lax
from jax.experimental import pallas as pl
from jax.experimental.pallas import tpu as pltpu

F32 = jnp.float32
BF16 = jnp.bfloat16

T = 4096
D = 1024
FF = 2816
NDEV = 8
CONV_W = 31
HEAD = 128
BLK = 128
GROUPS = ((128, 1), (512, 4), (2048, 16))
NHG = 4
AW = NHG * HEAD
IN_W = 2 * D + 3 * 3 * AW + 2 * D
EPS = 1e-6
B1, B2, LR, AEPS, WD, STEP = 0.9, 0.999, 0.001, 1e-08, 0.01, 10
NEG = -1e30
VMEM_LIMIT = 56 * 1024 * 1024
MESH_ID = pl.DeviceIdType.MESH

NT = (((1,), (1,)), ((), ()))
NN = (((1,), (0,)), ((), ()))
TN = (((0,), (0,)), ((), ()))
_DIMS = {"nn": NN, "nt": NT, "tn": TN}


def _cp(sem=None):
    return pltpu.CompilerParams(dimension_semantics=sem, vmem_limit_bytes=VMEM_LIMIT)


def _sig(v):
    return 1.0 / (1.0 + jnp.exp(-v))


def _dot(a, b, dims):
    return lax.dot_general(a, b, dims, preferred_element_type=F32)


def _const_spec(shape):
    nd = len(shape)
    return pl.BlockSpec(shape, lambda *_: (0,) * nd)


def _mm(a, b, *, mode, m, n, k, tm, tn, tk, out_dtype, name, a_map=None, b_map=None,
        o_map=None, out_rows=None, init=None, passthru=None):
    gi, gj, gk = m // tm, n // tn, k // tk
    assert gi * tm == m and gj * tn == n and gk * tk == k, (name, m, n, k, tm, tn, tk)
    if mode == "nn":
        a_blk, b_blk = (tm, tk), (tk, tn)
        da, db = (lambda i, j, kk: (i, kk)), (lambda i, j, kk: (kk, j))
    elif mode == "nt":
        a_blk, b_blk = (tm, tk), (tn, tk)
        da, db = (lambda i, j, kk: (i, kk)), (lambda i, j, kk: (j, kk))
    else:
        a_blk, b_blk = (tk, tm), (tk, tn)
        da, db = (lambda i, j, kk: (kk, i)), (lambda i, j, kk: (kk, j))
    a_map = a_map or da
    b_map = b_map or db
    o_map = o_map or (lambda i, j, kk: (i, j))
    dims = _DIMS[mode]
    extra = init if init is not None else passthru
    out_rows = out_rows or m

    def body(*refs):
        if init is not None:
            a_ref, b_ref, i_ref, o_ref = refs[:4]
        elif passthru is not None:
            a_ref, b_ref, _, o_ref = refs[:4]
        else:
            a_ref, b_ref, o_ref = refs[:3]
        if gk == 1:
            prod = _dot(a_ref[...], b_ref[...], dims)
            if init is not None:
                prod = prod + i_ref[...].astype(F32)
            o_ref[...] = prod.astype(out_dtype)
            return
        acc = refs[-1]
        kk = pl.program_id(2)

        @pl.when(kk == 0)
        def _():
            if init is not None:
                acc[...] = i_ref[...].astype(F32)
            else:
                acc[...] = jnp.zeros_like(acc)

        acc[...] += _dot(a_ref[...], b_ref[...], dims)

        @pl.when(kk == gk - 1)
        def _():
            o_ref[...] = acc[...].astype(out_dtype)

    in_specs = [pl.BlockSpec(a_blk, a_map), pl.BlockSpec(b_blk, b_map)]
    args = [a, b]
    aliases = {}
    if init is not None:
        in_specs.append(pl.BlockSpec((tm, tn), o_map))
        args.append(init)
        aliases = {2: 0}
    elif passthru is not None:
        in_specs.append(pl.BlockSpec(memory_space=pl.ANY))
        args.append(passthru)
        aliases = {2: 0}
    out_dt = extra.dtype if extra is not None else out_dtype
    assert out_dt == out_dtype
    return pl.pallas_call(
        body, name=name, grid=(gi, gj, gk),
        in_specs=in_specs, out_specs=pl.BlockSpec((tm, tn), o_map),
        out_shape=jax.ShapeDtypeStruct((out_rows, n), out_dtype),
        scratch_shapes=[pltpu.VMEM((tm, tn), F32)] if gk > 1 else [],
        input_output_aliases=aliases,
        compiler_params=_cp(("parallel", "parallel", "arbitrary")),
    )(*args)


def _ffn_fwd(x, g, wg_t, wu_t, wd, name):
    tm, fc = 512, 256
    nc = FF // fc

    def body(x_ref, g_ref, wg_ref, wu_ref, wd_ref, xo_ref, gg_ref, uu_ref, act_ref):
        xv = x_ref[...]
        r = lax.rsqrt(jnp.mean(xv * xv, axis=-1, keepdims=True) + EPS)
        h = (xv * r * g_ref[...]).astype(BF16)
        for c in range(nc):
            sl = pl.ds(c * fc, fc)
            gg = _dot(h, wg_ref[sl, :], NT)
            uu = _dot(h, wu_ref[sl, :], NT)
            gg_ref[:, sl] = gg.astype(BF16)
            uu_ref[:, sl] = uu.astype(BF16)
            act_ref[:, sl] = (gg * _sig(gg) * uu).astype(BF16)
        xo_ref[...] = xv + 0.5 * _dot(act_ref[...], wd_ref[...], NN)

    wspec = pl.BlockSpec((FF, D), lambda i: (0, 0), pipeline_mode=pl.Buffered(1))
    return pl.pallas_call(
        body, name=name, grid=(T // tm,),
        in_specs=[pl.BlockSpec((tm, D), lambda i: (i, 0)), _const_spec((1, D)), wspec, wspec, wspec],
        out_specs=[pl.BlockSpec((tm, D), lambda i: (i, 0)), pl.BlockSpec((tm, FF), lambda i: (i, 0)),
                   pl.BlockSpec((tm, FF), lambda i: (i, 0))],
        out_shape=[jax.ShapeDtypeStruct((T, D), F32), jax.ShapeDtypeStruct((T, FF), BF16),
                   jax.ShapeDtypeStruct((T, FF), BF16)],
        scratch_shapes=[pltpu.VMEM((tm, FF), BF16)],
        compiler_params=_cp(("parallel",)),
    )(x, g, wg_t, wu_t, wd)


def _ffn_bwd(x, g, gg_all, uu_all, dout, wg_t, wu_t, wd, name):
    tm, fc = 256, 256
    nc = FF // fc

    def body(x_ref, g_ref, gg_ref, uu_ref, do_ref, wg_ref, wu_ref, wd_ref,
             dx_ref, dgam_ref, dg_ref, du_ref, act_ref, h_ref, db_ref):
        i = pl.program_id(0)
        xv = x_ref[...]
        r = lax.rsqrt(jnp.mean(xv * xv, axis=-1, keepdims=True) + EPS)
        xhat = xv * r
        gam = g_ref[...]
        h_ref[...] = (xhat * gam).astype(BF16)
        dov = do_ref[...]
        dbv = (0.5 * dov).astype(BF16)
        db_ref[...] = dbv
        for c in range(nc):
            sl = pl.ds(c * fc, fc)
            da = _dot(dbv, wd_ref[sl, :], NT)
            gg = gg_ref[:, sl].astype(F32)
            uu = uu_ref[:, sl].astype(F32)
            s = _sig(gg)
            si = gg * s
            dgv = (da * uu * (s * (1.0 + gg * (1.0 - s)))).astype(BF16)
            duv = (da * si).astype(BF16)
            dg_ref[:, sl] = dgv
            du_ref[:, sl] = duv
            act_ref[:, sl] = (si * uu).astype(BF16)
        dh = _dot(dg_ref[...], wg_ref[...], NN) + _dot(du_ref[...], wu_ref[...], NN)

        @pl.when(i == 0)
        def _():
            dgam_ref[...] = jnp.zeros_like(dgam_ref)

        dgam_ref[...] += jnp.sum(dh * xhat, axis=0, keepdims=True)
        dxh = dh * gam
        dx_ref[...] = dov + r * (dxh - xhat * jnp.mean(dxh * xhat, axis=-1, keepdims=True))

    wspec = pl.BlockSpec((FF, D), lambda i: (0, 0), pipeline_mode=pl.Buffered(1))
    row_d = pl.BlockSpec((tm, D), lambda i: (i, 0))
    row_f = pl.BlockSpec((tm, FF), lambda i: (i, 0))
    return pl.pallas_call(
        body, name=name, grid=(T // tm,),
        in_specs=[row_d, _const_spec((1, D)), row_f, row_f, row_d, wspec, wspec, wspec],
        out_specs=[row_d, _const_spec((1, D)), row_f, row_f, row_f, row_d, row_d],
        out_shape=[jax.ShapeDtypeStruct((T, D), F32), jax.ShapeDtypeStruct((1, D), F32),
                   jax.ShapeDtypeStruct((T, FF), BF16), jax.ShapeDtypeStruct((T, FF), BF16),
                   jax.ShapeDtypeStruct((T, FF), BF16), jax.ShapeDtypeStruct((T, D), BF16),
                   jax.ShapeDtypeStruct((T, D), BF16)],
        compiler_params=_cp(("arbitrary",)),
    )(x, g, gg_all, uu_all, dout, wg_t, wu_t, wd)


def _wgrad(a, b, m, n, name):
    tm = m // 2 if m == FF else m
    return _mm(a, b, mode="tn", m=m, n=n, k=T, tm=tm, tn=n, tk=min(T, 2048), out_dtype=BF16, name=name)


PERM_TM = 512
DILS = tuple(d for _, d in GROUPS if d > 1)


def _perm_spec(dil, cols):
    return pl.BlockSpec((dil, PERM_TM // dil, cols), lambda i: (0, i, 0))


def _perm_shape(dil, cols, dtype):
    return jax.ShapeDtypeStruct((dil, T // dil, cols), dtype)


LANES = 128


def _tile_scratch(cols):
    return pltpu.VMEM((cols // LANES, PERM_TM, LANES), F32)


def _put_tile(tile, value):
    for c in range(tile.shape[0]):
        tile[c] = value[:, c * LANES:(c + 1) * LANES]


def _get_tile(tile):
    return jnp.concatenate([tile[c] for c in range(tile.shape[0])], axis=1)


def _store_perm(out_ref, tile, dil):
    for r in range(dil):
        for c in range(tile.shape[0]):
            out_ref[r, :, pl.ds(c * LANES, LANES)] = tile[c, pl.ds(r, PERM_TM // dil, stride=dil), :].astype(
                out_ref.dtype)


def _load_unperm(in_ref, tile, dil):
    for r in range(dil):
        for c in range(tile.shape[0]):
            tile[c, pl.ds(r, PERM_TM // dil, stride=dil), :] = in_ref[r, :, pl.ds(c * LANES, LANES)].astype(F32)


def _norm_cast(x, g, name):
    tm = PERM_TM

    def body(x_ref, g_ref, h_ref, *rest):
        p_refs, tile = rest[:-1], rest[-1]
        xv = x_ref[...]
        r = lax.rsqrt(jnp.mean(xv * xv, axis=-1, keepdims=True) + EPS)
        hv = xv * r * g_ref[...]
        h_ref[...] = hv.astype(BF16)
        _put_tile(tile, hv)
        for dil, p_ref in zip(DILS, p_refs):
            _store_perm(p_ref, tile, dil)

    out = pl.pallas_call(
        body, name=name, grid=(T // tm,),
        in_specs=[pl.BlockSpec((tm, D), lambda i: (i, 0)), _const_spec((1, D))],
        out_specs=[pl.BlockSpec((tm, D), lambda i: (i, 0))] + [_perm_spec(d, D) for d in DILS],
        out_shape=[jax.ShapeDtypeStruct((T, D), BF16)] + [_perm_shape(d, D, BF16) for d in DILS],
        scratch_shapes=[_tile_scratch(D)],
        compiler_params=_cp(("parallel",)),
    )(x, g)
    return [out[0]] + [o.reshape(T, D) for o in out[1:]]


def _final(x3, gf, tgt, name):
    tm = 512

    def body(x_ref, g_ref, t_ref, dx_ref, dgam_ref, loss_ref):
        i = pl.program_id(0)
        xv = x_ref[...]
        r = lax.rsqrt(jnp.mean(xv * xv, axis=-1, keepdims=True) + EPS)
        xhat = xv * r
        gam = g_ref[...]
        err = xhat * gam - t_ref[...]
        part = 0.5 * jnp.sum(jnp.mean(err * err, axis=-1, keepdims=True), axis=0, keepdims=True)
        dy = err * (1.0 / D)

        @pl.when(i == 0)
        def _():
            dgam_ref[...] = jnp.zeros_like(dgam_ref)
            loss_ref[...] = jnp.zeros_like(loss_ref)

        dgam_ref[...] += jnp.sum(dy * xhat, axis=0, keepdims=True)
        loss_ref[...] += jnp.broadcast_to(part, loss_ref.shape)
        dxh = dy * gam
        dx_ref[...] = r * (dxh - xhat * jnp.mean(dxh * xhat, axis=-1, keepdims=True))

    row_d = pl.BlockSpec((tm, D), lambda i: (i, 0))
    return pl.pallas_call(
        body, name=name, grid=(T // tm,),
        in_specs=[row_d, _const_spec((1, D)), row_d],
        out_specs=[row_d, _const_spec((1, D)), _const_spec((1, 128))],
        out_shape=[jax.ShapeDtypeStruct((T, D), F32), jax.ShapeDtypeStruct((1, D), F32),
                   jax.ShapeDtypeStruct((1, 128), F32)],
        compiler_params=_cp(("arbitrary",)),
    )(x3, gf, tgt)


def _rms_bwd(x, g, dhs, dres, name):
    tm = PERM_TM
    dils = [d for _, d in GROUPS]
    nh = len(dhs)
    assert nh == len(dils)

    def body(*refs):
        x_ref, g_ref = refs[:2]
        dh_refs = refs[2:2 + nh]
        dr_ref, dx_ref, dgam_ref, tile = refs[2 + nh:]
        i = pl.program_id(0)
        xv = x_ref[...]
        r = lax.rsqrt(jnp.mean(xv * xv, axis=-1, keepdims=True) + EPS)
        xhat = xv * r
        gam = g_ref[...]
        dh = None
        for dil, ref in zip(dils, dh_refs):
            if dil == 1:
                part = ref[...]
            else:
                _load_unperm(ref, tile, dil)
                part = _get_tile(tile)
            dh = part if dh is None else dh + part

        @pl.when(i == 0)
        def _():
            dgam_ref[...] = jnp.zeros_like(dgam_ref)

        dgam_ref[...] += jnp.sum(dh * xhat, axis=0, keepdims=True)
        dxh = dh * gam
        dx_ref[...] = dr_ref[...] + r * (dxh - xhat * jnp.mean(dxh * xhat, axis=-1, keepdims=True))

    row_d = pl.BlockSpec((tm, D), lambda i: (i, 0))
    dh_specs = [row_d if d == 1 else _perm_spec(d, D) for d in dils]
    dh_args = [a if d == 1 else a.reshape(d, T // d, D) for d, a in zip(dils, dhs)]
    return pl.pallas_call(
        body, name=name, grid=(T // tm,),
        in_specs=[row_d, _const_spec((1, D))] + dh_specs + [row_d],
        out_specs=[row_d, _const_spec((1, D))],
        out_shape=[jax.ShapeDtypeStruct((T, D), F32), jax.ShapeDtypeStruct((1, D), F32)],
        scratch_shapes=[_tile_scratch(D)],
        compiler_params=_cp(("arbitrary",)),
    )(x, g, *dh_args, dres)


CONV_TM = 256
CONV_HALO = 32
CONV_RB = 16


def _glu(ab):
    ab = ab.astype(F32)
    return ab[:, :D] * _sig(ab[:, D:])


def _ln_stats(z1):
    mu = jnp.mean(z1, axis=-1, keepdims=True)
    zc = z1 - mu
    rstd = lax.rsqrt(jnp.mean(zc * zc, axis=-1, keepdims=True) + EPS)
    return zc * rstd, rstd


def _fill_shifts(zs):
    n = zs.shape[1] - 8
    for s in range(1, 8):
        zs[s, pl.ds(0, n), :] = zs[0, pl.ds(s, n), :]


def _shifted(zs, start, rows):
    q, s = divmod(start, 8)
    return zs[s, pl.ds(8 * q, rows), :]


def _conv_fwd(ab, kern, dwb, lng, lnb, name):
    tm, hl, rb = CONV_TM, CONV_HALO, CONV_RB
    off = hl - (CONV_W - 1)

    def body(ab_ref, abh_ref, k_ref, dwb_ref, lng_ref, lnb_ref, z1_ref, z3_ref, zs):
        i = pl.program_id(0)
        zs[0, pl.ds(0, hl), :] = jnp.where(i > 0, _glu(abh_ref[...]), 0.0)
        zs[0, pl.ds(hl, tm), :] = _glu(ab_ref[...])
        _fill_shifts(zs)
        for b in range(tm // rb):
            acc = jnp.zeros((rb, D), F32)
            for j in range(CONV_W):
                acc = acc + _shifted(zs, b * rb + off + j, rb) * k_ref[pl.ds(j, 1), :]
            z1 = acc + dwb_ref[...]
            z1_ref[pl.ds(b * rb, rb), :] = z1
            zn, _ = _ln_stats(z1)
            z2 = zn * lng_ref[...] + lnb_ref[...]
            z3_ref[pl.ds(b * rb, rb), :] = (z2 * _sig(z2)).astype(BF16)

    row = pl.BlockSpec((tm, D), lambda i: (i, 0))
    return pl.pallas_call(
        body, name=name, grid=(T // tm,),
        in_specs=[pl.BlockSpec((tm, 2 * D), lambda i: (i, 0)),
                  pl.BlockSpec((hl, 2 * D), lambda i: (jnp.maximum(i * (tm // hl) - 1, 0), 0)),
                  _const_spec((32, D)), _const_spec((1, D)), _const_spec((1, D)), _const_spec((1, D))],
        out_specs=[row, row],
        out_shape=[jax.ShapeDtypeStruct((T, D), F32), jax.ShapeDtypeStruct((T, D), BF16)],
        scratch_shapes=[pltpu.VMEM((8, hl + tm, D), F32)],
        compiler_params=_cp(("parallel",)),
    )(ab, ab, kern, dwb, lng, lnb)


def _conv_bwd(dz3, z1, ab, kern, lng, lnb, name):
    tm, hl, rb = CONV_TM, CONV_HALO, CONV_RB
    off = hl - (CONV_W - 1)
    nsteps = T // tm

    def ln_bwd(dz3v, z1v, lngv, lnbv):
        zn, rstd = _ln_stats(z1v)
        z2 = zn * lngv + lnbv
        s = _sig(z2)
        dz2 = dz3v * (s * (1.0 + z2 * (1.0 - s)))
        dzn = dz2 * lngv
        dz1 = rstd * (dzn - jnp.mean(dzn, axis=-1, keepdims=True)
                      - zn * jnp.mean(dzn * zn, axis=-1, keepdims=True))
        return dz1, dz2, zn

    def body(dz3_ref, dz3h_ref, z1_ref, z1h_ref, ab_ref, abh_ref, k_ref, lng_ref, lnb_ref,
             dab_ref, dk_ref, dvec_ref, zs, dzs):
        i = pl.program_id(0)
        lngv, lnbv = lng_ref[...], lnb_ref[...]

        @pl.when(i == 0)
        def _():
            dk_ref[...] = jnp.zeros_like(dk_ref)
            dvec_ref[...] = jnp.zeros_like(dvec_ref)

        dz1, dz2, zn = ln_bwd(dz3_ref[...], z1_ref[...], lngv, lnbv)
        dvec_ref[pl.ds(0, 1), :] += jnp.sum(dz1, axis=0, keepdims=True)
        dvec_ref[pl.ds(1, 1), :] += jnp.sum(dz2 * zn, axis=0, keepdims=True)
        dvec_ref[pl.ds(2, 1), :] += jnp.sum(dz2, axis=0, keepdims=True)
        dzs[0, pl.ds(0, tm), :] = dz1
        dz1h, _, _ = ln_bwd(dz3h_ref[...], z1h_ref[...], lngv, lnbv)
        dzs[0, pl.ds(tm, hl), :] = jnp.where(i < nsteps - 1, dz1h, 0.0)
        _fill_shifts(dzs)
        zs[0, pl.ds(0, hl), :] = jnp.where(i > 0, _glu(abh_ref[...]), 0.0)
        zs[0, pl.ds(hl, tm), :] = _glu(ab_ref[...])
        _fill_shifts(zs)

        for j in range(CONV_W):
            tot = jnp.zeros((rb, D), F32)
            for b in range(tm // rb):
                tot = tot + dzs[0, pl.ds(b * rb, rb), :] * _shifted(zs, b * rb + off + j, rb)
            dk_ref[pl.ds(j, 1), :] += jnp.sum(tot, axis=0, keepdims=True)

        for b in range(tm // rb):
            acc = jnp.zeros((rb, D), F32)
            for j in range(CONV_W):
                acc = acc + _shifted(dzs, b * rb + (CONV_W - 1) - j, rb) * k_ref[pl.ds(j, 1), :]
            av = ab_ref[pl.ds(b * rb, rb), pl.ds(0, D)].astype(F32)
            sb = _sig(ab_ref[pl.ds(b * rb, rb), pl.ds(D, D)].astype(F32))
            dab_ref[pl.ds(b * rb, rb), pl.ds(0, D)] = (acc * sb).astype(BF16)
            dab_ref[pl.ds(b * rb, rb), pl.ds(D, D)] = (acc * av * sb * (1.0 - sb)).astype(BF16)

    row = pl.BlockSpec((tm, D), lambda i: (i, 0))
    nxt = pl.BlockSpec((hl, D), lambda i: (jnp.minimum((i + 1) * (tm // hl), T // hl - 1), 0))
    return pl.pallas_call(
        body, name=name, grid=(nsteps,),
        in_specs=[row, nxt, row, nxt,
                  pl.BlockSpec((tm, 2 * D), lambda i: (i, 0)),
                  pl.BlockSpec((hl, 2 * D), lambda i: (jnp.maximum(i * (tm // hl) - 1, 0), 0)),
                  _const_spec((32, D)), _const_spec((1, D)), _const_spec((1, D))],
        out_specs=[pl.BlockSpec((tm, 2 * D), lambda i: (i, 0)), _const_spec((32, D)), _const_spec((8, D))],
        out_shape=[jax.ShapeDtypeStruct((T, 2 * D), BF16), jax.ShapeDtypeStruct((32, D), F32),
                   jax.ShapeDtypeStruct((8, D), F32)],
        scratch_shapes=[pltpu.VMEM((8, hl + tm, D), F32), pltpu.VMEM((8, tm + hl, D), F32)],
        compiler_params=_cp(("arbitrary",)),
    )(dz3, dz3, z1, z1, ab, ab, kern, lng, lnb)


def _alibi_slopes():
    h = np.arange(1, 3 * NHG + 1, dtype=np.float32)
    return np.power(np.float32(2.0), -8.0 * h / np.float32(3 * NHG)).astype(np.float32)


def _band_bias(gi):
    _, dil = GROUPS[gi]
    slopes = _alibi_slopes()[gi * NHG:(gi + 1) * NHG]
    qi = np.arange(BLK)[:, None]
    ki = np.arange(2 * BLK)[None, :]
    steps = BLK + qi - ki
    band = (steps >= 0) & (steps <= BLK)
    bias = -slopes[:, None, None] * (dil * steps).astype(np.float32)[None]
    return jnp.asarray(np.where(band[None], bias, np.float32(NEG)).astype(np.float32))


QB_FWD = 4
QB_BWD = 8


def _attn_specs(qb):
    prev = lambda n: jnp.maximum(n * qb - 1, 0)
    return [pl.BlockSpec((qb * BLK, HEAD), lambda h, n: (n, h)),
            pl.BlockSpec((BLK, HEAD), lambda h, n: (prev(n), NHG + h)),
            pl.BlockSpec((qb * BLK, HEAD), lambda h, n: (n, NHG + h)),
            pl.BlockSpec((BLK, HEAD), lambda h, n: (prev(n), 2 * NHG + h)),
            pl.BlockSpec((qb * BLK, HEAD), lambda h, n: (n, 2 * NHG + h)),
            pl.BlockSpec((None, BLK, 2 * BLK), lambda h, n: (h, 0, 0))]


def _scores(q, kcat, bias, blk, seg):
    s = _dot(q, kcat, NT) * (HEAD ** -0.5) + bias
    col = lax.broadcasted_iota(jnp.int32, s.shape, 1)
    first = (blk % seg) == 0
    return jnp.where(jnp.logical_and(first, col < BLK), NEG, s)


def _attn_fwd(qkv, gi, name):
    seg = (T // GROUPS[gi][1]) // BLK

    qb = QB_FWD

    def body(q_ref, kp_ref, kc_ref, vp_ref, vc_ref, bias_ref, o_ref, l_ref):
        n = pl.program_id(0)
        for h in range(NHG):
            cols = pl.ds(h * HEAD, HEAD)
            kwin = jnp.concatenate([kp_ref[:, cols], kc_ref[:, cols]], axis=0)
            vwin = jnp.concatenate([vp_ref[:, cols], vc_ref[:, cols]], axis=0)
            bias = bias_ref[h]
            for b in range(qb):
                rows = pl.ds(b * BLK, BLK)
                s = _scores(q_ref[rows, cols], kwin[b * BLK:(b + 2) * BLK], bias, n * qb + b, seg)
                mx = jnp.max(s, axis=-1, keepdims=True)
                p = jnp.exp(s - mx)
                den = jnp.sum(p, axis=-1, keepdims=True)
                o_ref[rows, cols] = _dot(p.astype(BF16), vwin[b * BLK:(b + 2) * BLK], NN) / den
                l_ref[rows, cols] = jnp.broadcast_to(mx + jnp.log(den), (BLK, HEAD))

    prev = lambda n: jnp.maximum(n * qb - 1, 0)
    cur = lambda part: pl.BlockSpec((qb * BLK, AW), lambda n: (n, part))
    halo = lambda part: pl.BlockSpec((BLK, AW), lambda n: (prev(n), part))
    return pl.pallas_call(
        body, name=name, grid=(T // (qb * BLK),),
        in_specs=[cur(0), halo(1), cur(1), halo(2), cur(2), _const_spec((NHG, BLK, 2 * BLK))],
        out_specs=[cur(0), cur(0)],
        out_shape=[jax.ShapeDtypeStruct((T, AW), F32), jax.ShapeDtypeStruct((T, AW), F32)],
        compiler_params=_cp(("parallel",)),
    )(qkv, qkv, qkv, qkv, qkv, _band_bias(gi))


def _attn_bwd(qkv, dob, lse, delta, gi, name):
    seg = (T // GROUPS[gi][1]) // BLK
    qb = QB_BWD
    nb = T // (qb * BLK)
    scale = HEAD ** -0.5

    def body(q_ref, kp_ref, kc_ref, vp_ref, vc_ref, bias_ref, do_ref, l_ref, dl_ref, out_ref, dk_acc, dv_acc):
        n = pl.program_id(1)
        kwin = jnp.concatenate([kp_ref[...], kc_ref[...]], axis=0)
        vwin = jnp.concatenate([vp_ref[...], vc_ref[...]], axis=0)
        bias = bias_ref[...]
        for b in range(qb):
            rows = pl.ds(b * BLK, BLK)
            q = q_ref[rows, :]
            kcat = kwin[b * BLK:(b + 2) * BLK]
            s = _scores(q, kcat, bias, n * qb + b, seg)
            p = jnp.exp(s - l_ref[rows, pl.ds(0, 1)])
            dov = do_ref[rows, :]
            dv2 = _dot(p.astype(BF16), dov, TN)
            dp = _dot(dov, vwin[b * BLK:(b + 2) * BLK], NT)
            dsb = (p * (dp - dl_ref[rows, pl.ds(0, 1)]) * scale).astype(BF16)
            row = pl.ds(pl.multiple_of((n * qb + b) * BLK, BLK), BLK)
            out_ref[0, row, :] = _dot(dsb, kcat, NN).astype(BF16)
            dk2 = _dot(dsb, q, TN)
            dk_acc[row, :] = dk2[BLK:]
            dv_acc[row, :] = dv2[BLK:]

            def add_prev(dk2=dk2, dv2=dv2, b=b):
                prow = pl.ds(pl.multiple_of((n * qb + b - 1) * BLK, BLK), BLK)
                dk_acc[prow, :] += dk2[:BLK]
                dv_acc[prow, :] += dv2[:BLK]

            if b == 0:
                pl.when(n > 0)(add_prev)
            else:
                add_prev()

        @pl.when(n == nb - 1)
        def _():
            out_ref[1] = dk_acc[...].astype(BF16)
            out_ref[2] = dv_acc[...].astype(BF16)

    oblk = pl.BlockSpec((qb * BLK, HEAD), lambda h, n: (n, h))
    return pl.pallas_call(
        body, name=name, grid=(NHG, nb),
        in_specs=_attn_specs(qb) + [oblk, oblk, oblk],
        out_specs=pl.BlockSpec((3, T, HEAD), lambda h, n: (0, 0, h)),
        out_shape=jax.ShapeDtypeStruct((3, T, AW), BF16),
        scratch_shapes=[pltpu.VMEM((T, HEAD), F32), pltpu.VMEM((T, HEAD), F32)],
        compiler_params=_cp(("parallel", "arbitrary")),
    )(qkv, qkv, qkv, qkv, qkv, _band_bias(gi), dob, lse, delta)


def _merge(outs, lses, name):
    tm = PERM_TM
    dils = [d for _, d in GROUPS]
    ng = len(dils)

    def body(*refs):
        in_refs = refs[:2 * ng]
        a_ref, ab_ref = refs[2 * ng:2 * ng + 2]
        lse_refs = refs[2 * ng + 2:3 * ng + 2]
        tile = refs[-1]

        def token_order(ref, dil):
            if dil == 1:
                return ref[...]
            _load_unperm(ref, tile, dil)
            return _get_tile(tile)

        os = [token_order(in_refs[2 * i], d) for i, d in enumerate(dils)]
        ls = [token_order(in_refs[2 * i + 1], d) for i, d in enumerate(dils)]
        mx = jnp.maximum(jnp.maximum(ls[0], ls[1]), ls[2])
        es = [jnp.exp(v - mx) for v in ls]
        tot = es[0] + es[1] + es[2]
        att = (es[0] / tot) * os[0] + (es[1] / tot) * os[1] + (es[2] / tot) * os[2]
        a_ref[...] = att
        ab_ref[...] = att.astype(BF16)
        lse = mx + jnp.log(tot)
        _put_tile(tile, lse)
        for dil, ref in zip(dils, lse_refs):
            if dil == 1:
                ref[...] = lse
            else:
                _store_perm(ref, tile, dil)

    row = pl.BlockSpec((tm, AW), lambda i: (i, 0))
    specs = [row if d == 1 else _perm_spec(d, AW) for d in dils]
    args = []
    for d, o, l in zip(dils, outs, lses):
        args += [o, l] if d == 1 else [o.reshape(d, T // d, AW), l.reshape(d, T // d, AW)]
    out = pl.pallas_call(
        body, name=name, grid=(T // tm,),
        in_specs=[sp for sp in specs for _ in range(2)], out_specs=[row, row] + specs,
        out_shape=[jax.ShapeDtypeStruct((T, AW), F32), jax.ShapeDtypeStruct((T, AW), BF16)]
        + [jax.ShapeDtypeStruct((T, AW), F32) if d == 1 else _perm_shape(d, AW, F32) for d in dils],
        scratch_shapes=[_tile_scratch(AW)],
        compiler_params=_cp(("parallel",)),
    )(*args)
    return out[0], out[1], [o.reshape(T, AW) for o in out[2:]]


def _mix_out(z3b, attnb, gates, wc, wa_t, wo, x1, name):
    tm = 512

    def body(z_ref, a_ref, g_ref, wc_ref, wa_ref, wo_ref, x_ref, xo_ref, yc_ref, ya_ref, mx_ref):
        yc = _dot(z_ref[...], wc_ref[...], NN)
        ya = _dot(a_ref[...], wa_ref[...], NT)
        yc_ref[...] = yc
        ya_ref[...] = ya
        gv = g_ref[...].astype(F32)
        mixed = (_sig(gv[:, :D]) * yc + _sig(gv[:, D:]) * ya).astype(BF16)
        mx_ref[...] = mixed
        xo_ref[...] = x_ref[...] + _dot(mixed, wo_ref[...], NN)

    row = pl.BlockSpec((tm, D), lambda i: (i, 0))
    return pl.pallas_call(
        body, name=name, grid=(T // tm,),
        in_specs=[row, pl.BlockSpec((tm, AW), lambda i: (i, 0)), pl.BlockSpec((tm, 2 * D), lambda i: (i, 0)),
                  _const_spec((D, D)), _const_spec((D, AW)), _const_spec((D, D)), row],
        out_specs=[row, row, row, row],
        out_shape=[jax.ShapeDtypeStruct((T, D), F32), jax.ShapeDtypeStruct((T, D), F32),
                   jax.ShapeDtypeStruct((T, D), F32), jax.ShapeDtypeStruct((T, D), BF16)],
        compiler_params=_cp(("parallel",)),
    )(z3b, attnb, gates, wc, wa_t, wo, x1)


def _mix_out_bwd(dx2, gates, yc, ya, attn, wc, wa_t, wo, name):
    tm = PERM_TM
    dils = [d for _, d in GROUPS]
    ng = len(dils)

    def body(dx_ref, g_ref, yc_ref, ya_ref, at_ref, wc_ref, wa_ref, wo_ref,
             dg_ref, dyc_ref, dya_ref, dxb_ref, dz3_ref, *rest):
        dat_refs, dl_refs, tile = rest[:ng], rest[ng:2 * ng], rest[-1]
        dxb = dx_ref[...].astype(BF16)
        dxb_ref[...] = dxb
        dmix = _dot(dxb, wo_ref[...], NT)
        gv = g_ref[...].astype(F32)
        sc = _sig(gv[:, :D])
        sa = _sig(gv[:, D:])
        ycv, yav = yc_ref[...], ya_ref[...]
        dg_ref[:, pl.ds(0, D)] = (dmix * ycv * sc * (1.0 - sc)).astype(BF16)
        dg_ref[:, pl.ds(D, D)] = (dmix * yav * sa * (1.0 - sa)).astype(BF16)
        dyc = (dmix * sc).astype(BF16)
        dya = (dmix * sa).astype(BF16)
        dyc_ref[...] = dyc
        dya_ref[...] = dya
        dz3_ref[...] = _dot(dyc, wc_ref[...], NT)
        dat = _dot(dya, wa_ref[...], NN)
        prod = dat * at_ref[...]
        delta = jnp.concatenate(
            [jnp.broadcast_to(jnp.sum(prod[:, h * HEAD:(h + 1) * HEAD], axis=-1, keepdims=True), (tm, HEAD))
             for h in range(NHG)], axis=1)
        for value, out_refs in ((dat, dat_refs), (delta, dl_refs)):
            _put_tile(tile, value)
            for dil, ref in zip(dils, out_refs):
                if dil == 1:
                    ref[...] = value.astype(ref.dtype)
                else:
                    _store_perm(ref, tile, dil)

    row = pl.BlockSpec((tm, D), lambda i: (i, 0))
    row2 = pl.BlockSpec((tm, 2 * D), lambda i: (i, 0))
    rowa = pl.BlockSpec((tm, AW), lambda i: (i, 0))
    aspecs = [rowa if d == 1 else _perm_spec(d, AW) for d in dils]

    def ashapes(dtype):
        return [jax.ShapeDtypeStruct((T, AW), dtype) if d == 1 else _perm_shape(d, AW, dtype) for d in dils]

    out = pl.pallas_call(
        body, name=name, grid=(T // tm,),
        in_specs=[row, row2, row, row, rowa, _const_spec((D, D)), _const_spec((D, AW)), _const_spec((D, D))],
        out_specs=[row2, row, row, row, row] + aspecs + aspecs,
        out_shape=[jax.ShapeDtypeStruct((T, 2 * D), BF16), jax.ShapeDtypeStruct((T, D), BF16),
                   jax.ShapeDtypeStruct((T, D), BF16), jax.ShapeDtypeStruct((T, D), BF16),
                   jax.ShapeDtypeStruct((T, D), F32)] + ashapes(BF16) + ashapes(F32),
        scratch_shapes=[_tile_scratch(AW)],
        compiler_params=_cp(("parallel",)),
    )(dx2, gates, yc, ya, attn, wc, wa_t, wo)
    dats = [o.reshape(T, AW) for o in out[5:5 + ng]]
    deltas = [o.reshape(T, AW) for o in out[5 + ng:5 + 2 * ng]]
    return out[0], out[1], out[2], out[3], out[4], dats, deltas


def _peer(k):
    x, y, c = lax.axis_index("x"), lax.axis_index("y"), lax.axis_index("c")
    px = 1 - x if k & 4 else x
    py = 1 - y if k & 2 else y
    pc = 1 - c if k & 1 else c
    return (px, py, pc), 4 * px + 2 * py + pc


HBM_SPEC = pl.BlockSpec(memory_space=pltpu.HBM)
SEM_SPEC = pl.BlockSpec(memory_space=pltpu.SEMAPHORE)
EFFECT = pltpu.SideEffectType.DATAFLOW_SIDE_EFFECTING


def _my_place():
    return 4 * lax.axis_index("x") + 2 * lax.axis_index("y") + lax.axis_index("c")


def _tie(a, order_after, name):
    na = len(order_after)

    def body(*refs):
        del refs

    return pl.pallas_call(
        body, name=name, in_specs=[pl.BlockSpec(memory_space=pl.ANY)] * (1 + na),
        out_specs=pl.BlockSpec(memory_space=pl.ANY), out_shape=jax.ShapeDtypeStruct(a.shape, a.dtype),
        input_output_aliases={0: 0},
    )(a, *order_after)


def _prep_gather(ws, transposed, order_after, name):
    me = jnp.reshape(_my_place(), (1,)).astype(jnp.int32)
    n = len(ws)
    na = len(order_after)
    shapes = []
    for wv, tr in zip(ws, transposed):
        r, c = (wv.shape[1], wv.shape[0]) if tr else wv.shape
        shapes.append(((32, c), F32) if r == CONV_W else ((r, c), BF16))

    def body(me_ref, *refs):
        del me_ref
        ins, outs = refs[:n], refs[n + na:]
        for wv, tr, i_ref, o_ref in zip(ws, transposed, ins, outs):
            if wv.shape[0] == CONV_W:
                o_ref[pl.ds(0, CONV_W), :] = i_ref[...]
                o_ref[pl.ds(CONV_W, 1), :] = jnp.zeros((1, wv.shape[1]), F32)
            elif tr:
                o_ref[...] = jnp.transpose(i_ref[...]).astype(BF16)
            else:
                o_ref[...] = i_ref[...].astype(BF16)

    grid_spec = pltpu.PrefetchScalarGridSpec(
        num_scalar_prefetch=1, grid=(1,),
        in_specs=[pl.BlockSpec(wv.shape, lambda i, m: (0, 0)) for wv in ws]
        + [pl.BlockSpec(memory_space=pl.ANY)] * na,
        out_specs=[pl.BlockSpec(shp, lambda i, m: (m[0], 0)) for shp, _ in shapes])
    return pl.pallas_call(
        body, name=name, grid_spec=grid_spec,
        out_shape=[jax.ShapeDtypeStruct((NDEV * shp[0], shp[1]), dt) for shp, dt in shapes],
        compiler_params=_cp(("arbitrary",)),
    )(me, *ws, *order_after)


GATHER_A = ((1, 0), (2, 0), (4, 0), (6, 0))
GATHER_B = ((1, 2), (1, 4), (1, 6))


def _gather_start(lands, plan, order_after, name):
    n = len(lands)
    na = len(order_after)
    npl = len(plan)

    def body(*refs):
        land_refs = refs[:n]
        send, recv = refs[n + na], refs[n + na + 1]
        token = refs[-1]
        for w in range(n):
            rows = lands[w].shape[0] // NDEV
            for p, (k, j) in enumerate(plan):
                peer, _ = _peer(k)
                _, blk = _peer(j)
                part = land_refs[w].at[pl.ds(blk * rows, rows)]
                i = w * npl + p
                pltpu.make_async_remote_copy(src_ref=part, dst_ref=part, send_sem=send.at[i], recv_sem=recv.at[i],
                                             device_id=peer, device_id_type=MESH_ID).start()
        token[...] = jnp.zeros_like(token)

    nsem = n * npl
    bufs = [pltpu.with_memory_space_constraint(a, pltpu.HBM) for a in lands]
    out = pl.pallas_call(
        body, name=name,
        in_specs=[HBM_SPEC] * n + [pl.BlockSpec(memory_space=pl.ANY)] * na,
        out_specs=[SEM_SPEC, SEM_SPEC] + [HBM_SPEC] * n + [pl.BlockSpec(memory_space=pltpu.VMEM)],
        out_shape=[pltpu.SemaphoreType.DMA((nsem,)), pltpu.SemaphoreType.DMA((nsem,))]
        + [pltpu.HBM(a.shape, a.dtype) for a in bufs] + [jax.ShapeDtypeStruct((8, 128), F32)],
        input_output_aliases={i: 2 + i for i in range(n)},
        compiler_params=pltpu.CompilerParams(has_side_effects=EFFECT),
    )(*bufs, *order_after)
    return out[0], out[1], out[2:2 + n], out[-1]


def _gather_wait(started, plan, order_after, name):
    send, recv, lands, _ = started
    n = len(lands)
    na = len(order_after)
    npl = len(plan)

    def body(*refs):
        land_refs = refs[:n]
        send_ref, recv_ref = refs[n], refs[n + 1]
        for w in range(n):
            rows = lands[w].shape[0] // NDEV
            for p, (k, j) in enumerate(plan):
                peer, _ = _peer(k)
                _, blk = _peer(j)
                part = land_refs[w].at[pl.ds(blk * rows, rows)]
                i = w * npl + p
                cp = pltpu.make_async_remote_copy(src_ref=part, dst_ref=part, send_sem=send_ref.at[i],
                                                  recv_sem=recv_ref.at[i], device_id=peer, device_id_type=MESH_ID)
                cp.wait_send()
                cp.wait_recv()

    out = pl.pallas_call(
        body, name=name,
        in_specs=[HBM_SPEC] * n + [SEM_SPEC, SEM_SPEC] + [pl.BlockSpec(memory_space=pl.ANY)] * na,
        out_specs=[HBM_SPEC] * n,
        out_shape=[pltpu.HBM(a.shape, a.dtype) for a in lands],
        input_output_aliases={i: i for i in range(n)},
        compiler_params=pltpu.CompilerParams(has_side_effects=EFFECT),
    )(*lands, send, recv, *order_after)
    return list(out)


def _copy_ends(kind, src, land, me, plin, k):
    if kind == "scatter":
        rows = src.shape[0] // NDEV
        return src.at[pl.ds(plin * rows, rows)], land.at[k - 1]
    return src, land.at[me]


def _landing(kind, src):
    me = _my_place()
    if kind == "scatter":
        return lax.empty((NDEV - 1, src.shape[0] // NDEV) + src.shape[1:], src.dtype)
    land = lax.empty((NDEV,) + src.shape, src.dtype)
    return lax.dynamic_update_slice(land, src[None], (me,) + (0,) * src.ndim)


def _send_start(kinds, srcs, order_after, name):
    n = len(srcs)
    lands = [_landing(kd, s) for kd, s in zip(kinds, srcs)]
    na = len(order_after)

    def body(*refs):
        src_refs, land_refs = refs[:n], refs[n:2 * n]
        send, recv = refs[2 * n + na], refs[2 * n + na + 1]
        token = refs[-1]
        _, me = _peer(0)
        for w in range(n):
            for k in range(1, NDEV):
                peer, plin = _peer(k)
                s, d = _copy_ends(kinds[w], src_refs[w], land_refs[w], me, plin, k)
                i = w * (NDEV - 1) + k - 1
                pltpu.make_async_remote_copy(src_ref=s, dst_ref=d, send_sem=send.at[i], recv_sem=recv.at[i],
                                             device_id=peer, device_id_type=MESH_ID).start()
        token[...] = jnp.zeros_like(token)

    nsem = n * (NDEV - 1)
    bufs = [pltpu.with_memory_space_constraint(a, pltpu.HBM) for a in list(srcs) + lands]
    out = pl.pallas_call(
        body, name=name,
        in_specs=[HBM_SPEC] * (2 * n) + [pl.BlockSpec(memory_space=pl.ANY)] * na,
        out_specs=[SEM_SPEC, SEM_SPEC] + [HBM_SPEC] * (2 * n) + [pl.BlockSpec(memory_space=pltpu.VMEM)],
        out_shape=[pltpu.SemaphoreType.DMA((nsem,)), pltpu.SemaphoreType.DMA((nsem,))]
        + [pltpu.HBM(a.shape, a.dtype) for a in bufs] + [jax.ShapeDtypeStruct((8, 128), F32)],
        input_output_aliases={i: 2 + i for i in range(2 * n)},
        compiler_params=pltpu.CompilerParams(has_side_effects=EFFECT),
    )(*bufs, *order_after)
    return out[0], out[1], out[2:2 + n], out[2 + n:2 + 2 * n], out[-1]


def _send_wait(kinds, started, order_after, name):
    send, recv, srcs, lands, _ = started
    n = len(srcs)
    na = len(order_after)

    def body(*refs):
        src_refs, land_refs = refs[:n], refs[n:2 * n]
        send_ref, recv_ref = refs[2 * n], refs[2 * n + 1]
        _, me = _peer(0)
        for w in range(n):
            for k in range(1, NDEV):
                peer, plin = _peer(k)
                s, d = _copy_ends(kinds[w], src_refs[w], land_refs[w], me, plin, k)
                i = w * (NDEV - 1) + k - 1
                cp = pltpu.make_async_remote_copy(src_ref=s, dst_ref=d, send_sem=send_ref.at[i],
                                                  recv_sem=recv_ref.at[i], device_id=peer, device_id_type=MESH_ID)
                cp.wait_send()
                cp.wait_recv()

    bufs = list(srcs) + list(lands)
    out = pl.pallas_call(
        body, name=name,
        in_specs=[HBM_SPEC] * (2 * n) + [SEM_SPEC, SEM_SPEC] + [pl.BlockSpec(memory_space=pl.ANY)] * na,
        out_specs=[HBM_SPEC] * (2 * n),
        out_shape=[pltpu.HBM(a.shape, a.dtype) for a in bufs],
        input_output_aliases={i: i for i in range(2 * n)},
        compiler_params=pltpu.CompilerParams(has_side_effects=EFFECT),
    )(*bufs, send, recv, *order_after)
    return out[:n], out[n:]


def _gsum(own, land, name):
    rows, cols = own.shape
    tr = rows // 2 if rows * cols > 512 * 1024 and rows % 32 == 0 else rows

    def body(own_ref, l_ref, o_ref):
        tot = own_ref[...].astype(F32)
        for s in range(NDEV - 1):
            tot = tot + l_ref[s].astype(F32)
        o_ref[...] = tot

    return pl.pallas_call(
        body, name=name, grid=(rows // tr,),
        in_specs=[pl.BlockSpec((tr, cols), lambda i: (i, 0)),
                  pl.BlockSpec((NDEV - 1, tr, cols), lambda i: (0, i, 0))],
        out_specs=pl.BlockSpec((tr, cols), lambda i: (i, 0)),
        out_shape=jax.ShapeDtypeStruct((rows, cols), F32),
        compiler_params=_cp(("parallel",)),
    )(own, land)


def _adamw_math(w, g, m, v):
    m2 = B1 * m + (1.0 - B1) * g
    v2 = B2 * v + (1.0 - B2) * (g * g)
    m_hat = m2 / (1.0 - B1 ** STEP)
    v_hat = v2 / (1.0 - B2 ** STEP)
    delta = -LR * (m_hat / (jnp.sqrt(v_hat) + AEPS) + WD * w)
    return delta, m2, v2


def _adamw(w, g, m, v, name):
    rows, cols = w.shape
    tr = 256 if rows % 256 == 0 and rows > 256 else rows

    def body(w_ref, g_ref, m_ref, v_ref, d_ref, mo_ref, vo_ref):
        d, m2, v2 = _adamw_math(w_ref[...], g_ref[...], m_ref[...], v_ref[...])
        d_ref[...] = d
        mo_ref[...] = m2
        vo_ref[...] = v2

    blk = pl.BlockSpec((tr, cols), lambda i: (i, 0))
    return pl.pallas_call(
        body, name=name, grid=(rows // tr,), in_specs=[blk] * 4, out_specs=[blk] * 3,
        out_shape=[jax.ShapeDtypeStruct((rows, cols), F32)] * 3,
        compiler_params=_cp(("parallel",)),
    )(w, g, m, v)


UPD_TC = 256


def _update(src, land, w, m, v, transposed, name):
    rows, cols = land.shape[1:]
    tc = min(UPD_TC, cols)
    me = jnp.reshape(_my_place(), (1,)).astype(jnp.int32)

    def body(me_ref, own_ref, l_ref, w_ref, m_ref, v_ref, g_ref, d_ref, mo_ref, vo_ref):
        del me_ref
        tot = own_ref[...].astype(F32)
        for s in range(NDEV - 1):
            tot = tot + l_ref[s].astype(F32)
        g = jnp.transpose(tot) if transposed else tot
        g_ref[...] = g
        d, m2, v2 = _adamw_math(w_ref[...], g, m_ref[...], v_ref[...])
        d_ref[...] = d
        mo_ref[...] = m2
        vo_ref[...] = v2

    wblk = pl.BlockSpec((tc, rows), lambda j, p: (j, 0)) if transposed else pl.BlockSpec((rows, tc), lambda j, p: (0, j))
    grid_spec = pltpu.PrefetchScalarGridSpec(
        num_scalar_prefetch=1, grid=(cols // tc,),
        in_specs=[pl.BlockSpec((rows, tc), lambda j, p: (p[0], j)),
                  pl.BlockSpec((NDEV - 1, rows, tc), lambda j, p: (0, 0, j)), wblk, wblk, wblk],
        out_specs=[wblk] * 4)
    return pl.pallas_call(
        body, name=name, grid_spec=grid_spec, out_shape=[jax.ShapeDtypeStruct(w.shape, F32)] * 4,
        compiler_params=_cp(("parallel",)),
    )(me, src, land, w, m, v)


def _small_update(vland, w8, m8, v8, name):
    def body(l_ref, w_ref, m_ref, v_ref, g_ref, d_ref, mo_ref, vo_ref):
        g = l_ref[0]
        for s in range(1, NDEV):
            g = g + l_ref[s]
        g_ref[...] = g
        d, m2, v2 = _adamw_math(w_ref[...], g, m_ref[...], v_ref[...])
        d_ref[...] = d
        mo_ref[...] = m2
        vo_ref[...] = v2

    return pl.pallas_call(
        body, name=name, out_shape=[jax.ShapeDtypeStruct((8, D), F32)] * 4,
        compiler_params=_cp(None),
    )(vland, w8, m8, v8)


def kernel(x, ffn1_norm, ffn1_w_gate, ffn1_w_up, ffn1_w_down, mix_norm, w_in, conv_dw_kernel, conv_dw_bias, conv_ln_gain, conv_ln_bias, conv_w_out, attn_w_out, w_o, ffn2_norm, ffn2_w_gate, ffn2_w_up, ffn2_w_down, final_norm, loss_target, m_ffn1_norm, m_ffn1_w_gate, m_ffn1_w_up, m_ffn1_w_down, m_mix_norm, m_w_in, m_conv_dw_kernel, m_conv_dw_bias, m_conv_ln_gain, m_conv_ln_bias, m_conv_w_out, m_attn_w_out, m_w_o, m_ffn2_norm, m_ffn2_w_gate, m_ffn2_w_up, m_ffn2_w_down, m_final_norm, v_ffn1_norm, v_ffn1_w_gate, v_ffn1_w_up, v_ffn1_w_down, v_mix_norm, v_w_in, v_conv_dw_kernel, v_conv_dw_bias, v_conv_ln_gain, v_conv_ln_bias, v_conv_w_out, v_attn_w_out, v_w_o, v_ffn2_norm, v_ffn2_w_gate, v_ffn2_w_up, v_ffn2_w_down, v_final_norm):
    names = ["ffn1_norm", "ffn1_w_gate", "ffn1_w_up", "ffn1_w_down", "mix_norm", "w_in", "conv_dw_kernel",
             "conv_dw_bias", "conv_ln_gain", "conv_ln_bias", "conv_w_out", "attn_w_out", "w_o", "ffn2_norm",
             "ffn2_w_gate", "ffn2_w_up", "ffn2_w_down", "final_norm"]
    w = dict(ffn1_norm=ffn1_norm, ffn1_w_gate=ffn1_w_gate, ffn1_w_up=ffn1_w_up, ffn1_w_down=ffn1_w_down, mix_norm=mix_norm, w_in=w_in, conv_dw_kernel=conv_dw_kernel, conv_dw_bias=conv_dw_bias, conv_ln_gain=conv_ln_gain, conv_ln_bias=conv_ln_bias, conv_w_out=conv_w_out, attn_w_out=attn_w_out, w_o=w_o, ffn2_norm=ffn2_norm, ffn2_w_gate=ffn2_w_gate, ffn2_w_up=ffn2_w_up, ffn2_w_down=ffn2_w_down, final_norm=final_norm)
    mo = dict(ffn1_norm=m_ffn1_norm, ffn1_w_gate=m_ffn1_w_gate, ffn1_w_up=m_ffn1_w_up, ffn1_w_down=m_ffn1_w_down, mix_norm=m_mix_norm, w_in=m_w_in, conv_dw_kernel=m_conv_dw_kernel, conv_dw_bias=m_conv_dw_bias, conv_ln_gain=m_conv_ln_gain, conv_ln_bias=m_conv_ln_bias, conv_w_out=m_conv_w_out, attn_w_out=m_attn_w_out, w_o=m_w_o, ffn2_norm=m_ffn2_norm, ffn2_w_gate=m_ffn2_w_gate, ffn2_w_up=m_ffn2_w_up, ffn2_w_down=m_ffn2_w_down, final_norm=m_final_norm)
    vo = dict(ffn1_norm=v_ffn1_norm, ffn1_w_gate=v_ffn1_w_gate, ffn1_w_up=v_ffn1_w_up, ffn1_w_down=v_ffn1_w_down, mix_norm=v_mix_norm, w_in=v_w_in, conv_dw_kernel=v_conv_dw_kernel, conv_dw_bias=v_conv_dw_bias, conv_ln_gain=v_conv_ln_gain, conv_ln_bias=v_conv_ln_bias, conv_w_out=v_conv_w_out, attn_w_out=v_attn_w_out, w_o=v_w_o, ffn2_norm=v_ffn2_norm, ffn2_w_gate=v_ffn2_w_gate, ffn2_w_up=v_ffn2_w_up, ffn2_w_down=v_ffn2_w_down, final_norm=v_final_norm)
    col_sharded = ("ffn1_w_gate", "ffn1_w_up", "w_in", "attn_w_out", "ffn2_w_gate", "ffn2_w_up")
    row_sharded = ("ffn1_w_down", "conv_w_out", "w_o", "ffn2_w_down")
    small = ("ffn1_norm", "mix_norm", "ffn2_norm", "final_norm", "conv_dw_bias", "conv_ln_gain", "conv_ln_bias")

    ag_groups = (("ffn1_w_gate", "ffn1_w_up", "ffn1_w_down"),
                 ("w_in", "attn_w_out", "conv_w_out", "w_o", "conv_dw_kernel"),
                 ("ffn2_w_gate", "ffn2_w_up", "ffn2_w_down"))
    ag, order = [], []
    for gi, grp in enumerate(ag_groups):
        lands = _prep_gather([w[n][0] for n in grp], [n in col_sharded for n in grp], order, f"gather_prep{gi}")
        st = _gather_start(lands, GATHER_A, [], f"gather_a_start{gi}")
        ag.append(st)
        order = [st[3]]

    def chips_in(gi, after):
        lands = _gather_wait(ag[gi], GATHER_A, after, f"gather_a_wait{gi}")
        return _gather_start(lands, GATHER_B, [], f"gather_b_start{gi}")

    def all_in(gi, st, after):
        return _gather_wait(st, GATHER_B, after, f"gather_b_wait{gi}")

    x0 = x[0]
    tgt = loss_target[0]
    gf = final_norm.reshape(1, D)

    wg1, wu1, wd1 = all_in(0, chips_in(0, [ag[2][3]]), [])
    x1, gg1, uu1 = _ffn_fwd(x0, ffn1_norm, wg1, wu1, wd1, "ffn1_fwd")
    win_t, wa_t, wc, wo, kern_blocks = all_in(1, chips_in(1, [x1]), [])
    kern = kern_blocks.reshape(NDEV, 32, D // NDEV).transpose(1, 0, 2).reshape(32, D)
    h2p = _norm_cast(x1, mix_norm, "mix_norm_fwd")
    h2 = h2p[0]
    ab = _mm(h2, win_t, mode="nt", m=T, n=2 * D, k=D, tm=1024, tn=512, tk=D, out_dtype=BF16, name="proj_conv")
    gates = _mm(h2, win_t, mode="nt", m=T, n=2 * D, k=D, tm=1024, tn=512, tk=D, out_dtype=BF16,
                b_map=lambda i, j, kk: (13 + j, 0), name="proj_gates")
    qkv = []
    for gi in range(len(GROUPS)):
        qkv.append(_mm(h2p[gi], win_t, mode="nt", m=T, n=3 * AW, k=D, tm=1024, tn=AW, tk=D, out_dtype=BF16,
                       b_map=lambda i, j, kk, gi=gi: (4 + gi + 3 * j, 0), name=f"proj_qkv{gi}"))
    z1, z3b = _conv_fwd(ab, kern, conv_dw_bias, conv_ln_gain, conv_ln_bias, "conv_fwd")
    ffn2_b = chips_in(2, [z3b])
    outs, lses = [], []
    for gi, (_, dil) in enumerate(GROUPS):
        o, l = _attn_fwd(qkv[gi], gi, f"attn_fwd{gi}")
        outs.append(o)
        lses.append(l)
    attn, attnb, lse = _merge(outs, lses, "attn_merge")
    x2, yc, ya, mixedb = _mix_out(z3b, attnb, gates, wc, wa_t, wo, x1, "mix_out_fwd")
    wg2, wu2, wd2 = all_in(2, ffn2_b, [x2])
    x3, gg2, uu2 = _ffn_fwd(x2, ffn2_norm, wg2, wu2, wd2, "ffn2_fwd")

    dx3, dgf, loss_part = _final(x3, gf, tgt, "final_norm_loss")
    dx2, dg3, dgb, dub, actb, hb, dob = _ffn_bwd(x2, ffn2_norm, gg2, uu2, dx3, wg2, wu2, wd2, "ffn2_bwd")
    grads = {}
    grads["ffn2_w_gate"] = _wgrad(dgb, hb, FF, D, "ffn2_dwg")
    grads["ffn2_w_up"] = _wgrad(dub, hb, FF, D, "ffn2_dwu")
    grads["ffn2_w_down"] = _wgrad(actb, dob, FF, D, "ffn2_dwd")
    rs_groups = [("ffn2_w_gate", "ffn2_w_up", "ffn2_w_down"),
                 ("attn_w_out", "conv_w_out", "w_o", "conv_dw_kernel"),
                 ("w_in",),
                 ("ffn1_w_gate",), ("ffn1_w_up",), ("ffn1_w_down",)]
    last = len(rs_groups) - 1
    rs = [_send_start(["scatter"] * 3, [grads[n] for n in rs_groups[0]], [], "scatter_start0")]
    dx2 = _tie(dx2, [rs[0][4]], "tie_after_scatter0")

    dgates, dycb, dyab, dx2b, dz3, dattnb, delta = _mix_out_bwd(dx2, gates, yc, ya, attn, wc, wa_t, wo, "mix_out_bwd")
    grads["w_o"] = _wgrad(mixedb, dx2b, D, D, "dw_o")
    grads["conv_w_out"] = _wgrad(z3b, dycb, D, D, "dw_conv_out")
    grads["attn_w_out"] = _wgrad(dyab, attnb, D, AW, "dw_attn_out")
    dab, dkern, dvec = _conv_bwd(dz3, z1, ab, kern, conv_ln_gain, conv_ln_bias, "conv_bwd")
    grads["conv_dw_kernel"] = dkern.reshape(32, NDEV, D // NDEV).transpose(1, 0, 2).reshape(NDEV * 32, D // NDEV)
    rs.append(_send_start(["scatter"] * 4, [grads[n] for n in rs_groups[1]], [rs[0][4]], "scatter_start1"))
    dattnb = [_tie(a, [rs[1][4]], f"tie_after_scatter1_{i}") for i, a in enumerate(dattnb)]

    dqkv = []
    for gi, (_, dil) in enumerate(GROUPS):
        dq3 = _attn_bwd(qkv[gi], dattnb[gi], lse[gi], delta[gi], gi, f"attn_bwd{gi}")
        dqkv.append(dq3.reshape(3 * T, AW))

    dwin = _mm(dab, h2, mode="tn", m=2 * D, n=D, k=T, tm=2 * D, tn=D, tk=512, out_dtype=BF16, out_rows=IN_W,
               name="dw_in_conv")
    dwin = _mm(dgates, h2, mode="tn", m=2 * D, n=D, k=T, tm=512, tn=D, tk=1024, out_dtype=BF16, out_rows=IN_W,
               o_map=lambda i, j, kk: (13 + i, 0), passthru=dwin, name="dw_in_gates")
    for gi in range(3):
        dwin = _mm(dqkv[gi], h2p[gi], mode="tn", m=3 * AW, n=D, k=T, tm=AW, tn=D, tk=1024, out_dtype=BF16,
                   out_rows=IN_W, a_map=lambda i, j, kk: (i * (T // 1024) + kk, 0),
                   o_map=lambda i, j, kk, gi=gi: (4 + gi + 3 * i, 0), passthru=dwin, name=f"dw_in_qkv{gi}")
    grads["w_in"] = dwin
    rs.append(_send_start(["scatter"], [dwin], [rs[1][4]], "scatter_start2"))
    dab = _tie(dab, [rs[2][4]], "tie_after_scatter2")

    nrow = T // 1024
    dh = _mm(dab, win_t, mode="nn", m=T, n=D, k=2 * D, tm=1024, tn=D, tk=512, out_dtype=F32, name="dproj_conv")
    dh = _mm(dgates, win_t, mode="nn", m=T, n=D, k=2 * D, tm=1024, tn=D, tk=512, out_dtype=F32,
             b_map=lambda i, j, kk: (13 + kk, 0), init=dh, name="dproj_gates")
    dhs = []
    for gi, (_, dil) in enumerate(GROUPS):
        part = _mm(dqkv[gi], win_t, mode="nn", m=T, n=D, k=3 * AW, tm=1024, tn=D, tk=AW, out_dtype=F32,
                   a_map=lambda i, j, kk: (kk * nrow + i, 0), b_map=lambda i, j, kk, gi=gi: (4 + gi + 3 * kk, 0),
                   init=dh if gi == 0 else None, name=f"dproj_qkv{gi}")
        dhs.append(part)
    dx1, dg2 = _rms_bwd(x1, mix_norm, dhs, dx2, "mix_norm_bwd")

    dx0, dg1, dgb, dub, actb, hb, dob = _ffn_bwd(x0, ffn1_norm, gg1, uu1, dx1, wg1, wu1, wd1, "ffn1_bwd")
    grads["ffn1_w_gate"] = _wgrad(dgb, hb, FF, D, "ffn1_dwg")
    rs.append(_send_start(["scatter"], [grads["ffn1_w_gate"]], [rs[2][4]], "scatter_start3"))
    hb = _tie(hb, [rs[3][4]], "tie_after_scatter3")
    grads["ffn1_w_up"] = _wgrad(dub, hb, FF, D, "ffn1_dwu")
    rs.append(_send_start(["scatter"], [grads["ffn1_w_up"]], [rs[3][4]], "scatter_start4"))
    dob = _tie(dob, [rs[4][4]], "tie_after_scatter4")
    grads["ffn1_w_down"] = _wgrad(actb, dob, FF, D, "ffn1_dwd")
    vec = jnp.concatenate([dg1, dg2, dg3, dgf, dvec[0:3], jnp.broadcast_to(loss_part[:, :1], (1, D))], axis=0)
    rs.append(_send_start(["scatter", "bcast"], [grads["ffn1_w_down"], vec], [rs[4][4]], "scatter_start5"))

    g_out, d_out, m_out, v_out = {}, {}, {}, {}
    me = _my_place()
    after = [rs[last][4]]
    for gi, grp in enumerate(rs_groups):
        kinds = ["scatter"] * len(grp) + (["bcast"] if gi == last else [])
        srcs, lands = _send_wait(kinds, rs[gi], after, f"scatter_wait{gi}")
        for n, src, land in zip(grp, srcs, lands):
            if n == "conv_dw_kernel":
                rows = src.shape[0] // NDEV
                own = lax.dynamic_slice(src, (me * rows, 0), (rows, src.shape[1]))
                g = _gsum(own, land, f"gsum_{n}")[:CONV_W]
                d, m2, v2 = _adamw(w[n][0], g, mo[n][0], vo[n][0], f"adamw_{n}")
            else:
                g, d, m2, v2 = _update(src, land, w[n][0], mo[n][0], vo[n][0], n in col_sharded, f"update_{n}")
            g_out[n], d_out[n], m_out[n], v_out[n] = g[None], d[None], m2[None], v2[None]
            after = [d]
    vland = lands[-1]

    def rows8(src):
        return jnp.concatenate([src[n].reshape(1, D) for n in small] + [jnp.ones((1, D), F32)], axis=0)

    g8, d8, m8, v8 = _small_update(vland, rows8(w), rows8(mo), rows8(vo), "small_update")
    for r, n in enumerate(small):
        shp = w[n].shape
        g_out[n], d_out[n], m_out[n], v_out[n] = (a[r].reshape(shp) for a in (g8, d8, m8, v8))
    loss = g8[7, 0]

    return (loss, dx0[None], *[g_out[n] for n in names], *[d_out[n] for n in names],
            *[m_out[n] for n in names], *[v_out[n] for n in names])
```

```python
import numpy as np
import jax
import jax.numpy as jnp
from jax import lax
from jax.experimental import pallas as pl
from jax.experimental.pallas import tpu as pltpu

F32 = jnp.float32
BF16 = jnp.bfloat16

T = 4096
D = 1024
FF = 2816
NDEV = 8
CONV_W = 31
HEAD = 128
BLK = 128
GROUPS = ((128, 1), (512, 4), (2048, 16))
NHG = 4
AW = NHG * HEAD
IN_W = 2 * D + 3 * 3 * AW + 2 * D
EPS = 1e-6
B1, B2, LR, AEPS, WD, STEP = 0.9, 0.999, 0.001, 1e-08, 0.01, 10
NEG = -1e30
VMEM_LIMIT = 56 * 1024 * 1024
MESH_ID = pl.DeviceIdType.MESH

NT = (((1,), (1,)), ((), ()))
NN = (((1,), (0,)), ((), ()))
TN = (((0,), (0,)), ((), ()))
_DIMS = {"nn": NN, "nt": NT, "tn": TN}


def _cp(sem=None):
    return pltpu.CompilerParams(dimension_semantics=sem, vmem_limit_bytes=VMEM_LIMIT)


def _sig(v):
    return 1.0 / (1.0 + jnp.exp(-v))


def _dot(a, b, dims):
    return lax.dot_general(a, b, dims, preferred_element_type=F32)


def _const_spec(shape):
    nd = len(shape)
    return pl.BlockSpec(shape, lambda *_: (0,) * nd)


def _mm(a, b, *, mode, m, n, k, tm, tn, tk, out_dtype, name, a_map=None, b_map=None,
        o_map=None, out_rows=None, init=None, passthru=None):
    gi, gj, gk = m // tm, n // tn, k // tk
    assert gi * tm == m and gj * tn == n and gk * tk == k, (name, m, n, k, tm, tn, tk)
    if mode == "nn":
        a_blk, b_blk = (tm, tk), (tk, tn)
        da, db = (lambda i, j, kk: (i, kk)), (lambda i, j, kk: (kk, j))
    elif mode == "nt":
        a_blk, b_blk = (tm, tk), (tn, tk)
        da, db = (lambda i, j, kk: (i, kk)), (lambda i, j, kk: (j, kk))
    else:
        a_blk, b_blk = (tk, tm), (tk, tn)
        da, db = (lambda i, j, kk: (kk, i)), (lambda i, j, kk: (kk, j))
    a_map = a_map or da
    b_map = b_map or db
    o_map = o_map or (lambda i, j, kk: (i, j))
    dims = _DIMS[mode]
    extra = init if init is not None else passthru
    out_rows = out_rows or m

    def body(*refs):
        if init is not None:
            a_ref, b_ref, i_ref, o_ref = refs[:4]
        elif passthru is not None:
            a_ref, b_ref, _, o_ref = refs[:4]
        else:
            a_ref, b_ref, o_ref = refs[:3]
        if gk == 1:
            prod = _dot(a_ref[...], b_ref[...], dims)
            if init is not None:
                prod = prod + i_ref[...].astype(F32)
            o_ref[...] = prod.astype(out_dtype)
            return
        acc = refs[-1]
        kk = pl.program_id(2)

        @pl.when(kk == 0)
        def _():
            if init is not None:
                acc[...] = i_ref[...].astype(F32)
            else:
                acc[...] = jnp.zeros_like(acc)

        acc[...] += _dot(a_ref[...], b_ref[...], dims)

        @pl.when(kk == gk - 1)
        def _():
            o_ref[...] = acc[...].astype(out_dtype)

    in_specs = [pl.BlockSpec(a_blk, a_map), pl.BlockSpec(b_blk, b_map)]
    args = [a, b]
    aliases = {}
    if init is not None:
        in_specs.append(pl.BlockSpec((tm, tn), o_map))
        args.append(init)
        aliases = {2: 0}
    elif passthru is not None:
        in_specs.append(pl.BlockSpec(memory_space=pl.ANY))
        args.append(passthru)
        aliases = {2: 0}
    out_dt = extra.dtype if extra is not None else out_dtype
    assert out_dt == out_dtype
    return pl.pallas_call(
        body, name=name, grid=(gi, gj, gk),
        in_specs=in_specs, out_specs=pl.BlockSpec((tm, tn), o_map),
        out_shape=jax.ShapeDtypeStruct((out_rows, n), out_dtype),
        scratch_shapes=[pltpu.VMEM((tm, tn), F32)] if gk > 1 else [],
        input_output_aliases=aliases,
        compiler_params=_cp(("parallel", "parallel", "arbitrary")),
    )(*args)


def _ffn_fwd(x, g, wg_t, wu_t, wd, name):
    tm, fc = 512, 256
    nc = FF // fc

    def body(x_ref, g_ref, wg_ref, wu_ref, wd_ref, xo_ref, gg_ref, uu_ref, act_ref):
        xv = x_ref[...]
        r = lax.rsqrt(jnp.mean(xv * xv, axis=-1, keepdims=True) + EPS)
        h = (xv * r * g_ref[...]).astype(BF16)
        for c in range(nc):
            sl = pl.ds(c * fc, fc)
            gg = _dot(h, wg_ref[sl, :], NT)
            uu = _dot(h, wu_ref[sl, :], NT)
            gg_ref[:, sl] = gg.astype(BF16)
            uu_ref[:, sl] = uu.astype(BF16)
            act_ref[:, sl] = (gg * _sig(gg) * uu).astype(BF16)
        xo_ref[...] = xv + 0.5 * _dot(act_ref[...], wd_ref[...], NN)

    wspec = pl.BlockSpec((FF, D), lambda i: (0, 0), pipeline_mode=pl.Buffered(1))
    return pl.pallas_call(
        body, name=name, grid=(T // tm,),
        in_specs=[pl.BlockSpec((tm, D), lambda i: (i, 0)), _const_spec((1, D)), wspec, wspec, wspec],
        out_specs=[pl.BlockSpec((tm, D), lambda i: (i, 0)), pl.BlockSpec((tm, FF), lambda i: (i, 0)),
                   pl.BlockSpec((tm, FF), lambda i: (i, 0))],
        out_shape=[jax.ShapeDtypeStruct((T, D), F32), jax.ShapeDtypeStruct((T, FF), BF16),
                   jax.ShapeDtypeStruct((T, FF), BF16)],
        scratch_shapes=[pltpu.VMEM((tm, FF), BF16)],
        compiler_params=_cp(("parallel",)),
    )(x, g, wg_t, wu_t, wd)


def _ffn_bwd(x, g, gg_all, uu_all, dout, wg_t, wu_t, wd, name):
    tm, fc = 256, 256
    nc = FF // fc

    def body(x_ref, g_ref, gg_ref, uu_ref, do_ref, wg_ref, wu_ref, wd_ref,
             dx_ref, dgam_ref, dg_ref, du_ref, act_ref, h_ref, db_ref):
        i = pl.program_id(0)
        xv = x_ref[...]
        r = lax.rsqrt(jnp.mean(xv * xv, axis=-1, keepdims=True) + EPS)
        xhat = xv * r
        gam = g_ref[...]
        h_ref[...] = (xhat * gam).astype(BF16)
        dov = do_ref[...]
        dbv = (0.5 * dov).astype(BF16)
        db_ref[...] = dbv
        for c in range(nc):
            sl = pl.ds(c * fc, fc)
            da = _dot(dbv, wd_ref[sl, :], NT)
            gg = gg_ref[:, sl].astype(F32)
            uu = uu_ref[:, sl].astype(F32)
            s = _sig(gg)
            si = gg * s
            dgv = (da * uu * (s * (1.0 + gg * (1.0 - s)))).astype(BF16)
            duv = (da * si).astype(BF16)
            dg_ref[:, sl] = dgv
            du_ref[:, sl] = duv
            act_ref[:, sl] = (si * uu).astype(BF16)
        dh = _dot(dg_ref[...], wg_ref[...], NN) + _dot(du_ref[...], wu_ref[...], NN)

        @pl.when(i == 0)
        def _():
            dgam_ref[...] = jnp.zeros_like(dgam_ref)

        dgam_ref[...] += jnp.sum(dh * xhat, axis=0, keepdims=True)
        dxh = dh * gam
        dx_ref[...] = dov + r * (dxh - xhat * jnp.mean(dxh * xhat, axis=-1, keepdims=True))

    wspec = pl.BlockSpec((FF, D), lambda i: (0, 0), pipeline_mode=pl.Buffered(1))
    row_d = pl.BlockSpec((tm, D), lambda i: (i, 0))
    row_f = pl.BlockSpec((tm, FF), lambda i: (i, 0))
    return pl.pallas_call(
        body, name=name, grid=(T // tm,),
        in_specs=[row_d, _const_spec((1, D)), row_f, row_f, row_d, wspec, wspec, wspec],
        out_specs=[row_d, _const_spec((1, D)), row_f, row_f, row_f, row_d, row_d],
        out_shape=[jax.ShapeDtypeStruct((T, D), F32), jax.ShapeDtypeStruct((1, D), F32),
                   jax.ShapeDtypeStruct((T, FF), BF16), jax.ShapeDtypeStruct((T, FF), BF16),
                   jax.ShapeDtypeStruct((T, FF), BF16), jax.ShapeDtypeStruct((T, D), BF16),
                   jax.ShapeDtypeStruct((T, D), BF16)],
        compiler_params=_cp(("arbitrary",)),
    )(x, g, gg_all, uu_all, dout, wg_t, wu_t, wd)


def _wgrad(a, b, m, n, name):
    tm = m // 2 if m == FF else m
    return _mm(a, b, mode="tn", m=m, n=n, k=T, tm=tm, tn=n, tk=min(T, 2048), out_dtype=BF16, name=name)


PERM_TM = 512
DILS = tuple(d for _, d in GROUPS if d > 1)


def _perm_spec(dil, cols):
    return pl.BlockSpec((dil, PERM_TM // dil, cols), lambda i: (0, i, 0))


def _perm_shape(dil, cols, dtype):
    return jax.ShapeDtypeStruct((dil, T // dil, cols), dtype)


LANES = 128


def _tile_scratch(cols):
    return pltpu.VMEM((cols // LANES, PERM_TM, LANES), F32)


def _put_tile(tile, value):
    for c in range(tile.shape[0]):
        tile[c] = value[:, c * LANES:(c + 1) * LANES]


def _get_tile(tile):
    return jnp.concatenate([tile[c] for c in range(tile.shape[0])], axis=1)


def _store_perm(out_ref, tile, dil):
    for r in range(dil):
        for c in range(tile.shape[0]):
            out_ref[r, :, pl.ds(c * LANES, LANES)] = tile[c, pl.ds(r, PERM_TM // dil, stride=dil), :].astype(
                out_ref.dtype)


def _load_unperm(in_ref, tile, dil):
    for r in range(dil):
        for c in range(tile.shape[0]):
            tile[c, pl.ds(r, PERM_TM // dil, stride=dil), :] = in_ref[r, :, pl.ds(c * LANES, LANES)].astype(F32)


def _norm_cast(x, g, name):
    tm = PERM_TM

    def body(x_ref, g_ref, h_ref, *rest):
        p_refs, tile = rest[:-1], rest[-1]
        xv = x_ref[...]
        r = lax.rsqrt(jnp.mean(xv * xv, axis=-1, keepdims=True) + EPS)
        hv = xv * r * g_ref[...]
        h_ref[...] = hv.astype(BF16)
        _put_tile(tile, hv)
        for dil, p_ref in zip(DILS, p_refs):
            _store_perm(p_ref, tile, dil)

    out = pl.pallas_call(
        body, name=name, grid=(T // tm,),
        in_specs=[pl.BlockSpec((tm, D), lambda i: (i, 0)), _const_spec((1, D))],
        out_specs=[pl.BlockSpec((tm, D), lambda i: (i, 0))] + [_perm_spec(d, D) for d in DILS],
        out_shape=[jax.ShapeDtypeStruct((T, D), BF16)] + [_perm_shape(d, D, BF16) for d in DILS],
        scratch_shapes=[_tile_scratch(D)],
        compiler_params=_cp(("parallel",)),
    )(x, g)
    return [out[0]] + [o.reshape(T, D) for o in out[1:]]


def _final(x3, gf, tgt, name):
    tm = 512

    def body(x_ref, g_ref, t_ref, dx_ref, dgam_ref, loss_ref):
        i = pl.program_id(0)
        xv = x_ref[...]
        r = lax.rsqrt(jnp.mean(xv * xv, axis=-1, keepdims=True) + EPS)
        xhat = xv * r
        gam = g_ref[...]
        err = xhat * gam - t_ref[...]
        part = 0.5 * jnp.sum(jnp.mean(err * err, axis=-1, keepdims=True), axis=0, keepdims=True)
        dy = err * (1.0 / D)

        @pl.when(i == 0)
        def _():
            dgam_ref[...] = jnp.zeros_like(dgam_ref)
            loss_ref[...] = jnp.zeros_like(loss_ref)

        dgam_ref[...] += jnp.sum(dy * xhat, axis=0, keepdims=True)
        loss_ref[...] += jnp.broadcast_to(part, loss_ref.shape)
        dxh = dy * gam
        dx_ref[...] = r * (dxh - xhat * jnp.mean(dxh * xhat, axis=-1, keepdims=True))

    row_d = pl.BlockSpec((tm, D), lambda i: (i, 0))
    return pl.pallas_call(
        body, name=name, grid=(T // tm,),
        in_specs=[row_d, _const_spec((1, D)), row_d],
        out_specs=[row_d, _const_spec((1, D)), _const_spec((1, 128))],
        out_shape=[jax.ShapeDtypeStruct((T, D), F32), jax.ShapeDtypeStruct((1, D), F32),
                   jax.ShapeDtypeStruct((1, 128), F32)],
        compiler_params=_cp(("arbitrary",)),
    )(x3, gf, tgt)


def _rms_bwd(x, g, dhs, dres, name):
    tm = PERM_TM
    dils = [d for _, d in GROUPS]
    nh = len(dhs)
    assert nh == len(dils)

    def body(*refs):
        x_ref, g_ref = refs[:2]
        dh_refs = refs[2:2 + nh]
        dr_ref, dx_ref, dgam_ref, tile = refs[2 + nh:]
        i = pl.program_id(0)
        xv = x_ref[...]
        r = lax.rsqrt(jnp.mean(xv * xv, axis=-1, keepdims=True) + EPS)
        xhat = xv * r
        gam = g_ref[...]
        dh = None
        for dil, ref in zip(dils, dh_refs):
            if dil == 1:
                part = ref[...]
            else:
                _load_unperm(ref, tile, dil)
                part = _get_tile(tile)
            dh = part if dh is None else dh + part

        @pl.when(i == 0)
        def _():
            dgam_ref[...] = jnp.zeros_like(dgam_ref)

        dgam_ref[...] += jnp.sum(dh * xhat, axis=0, keepdims=True)
        dxh = dh * gam
        dx_ref[...] = dr_ref[...] + r * (dxh - xhat * jnp.mean(dxh * xhat, axis=-1, keepdims=True))

    row_d = pl.BlockSpec((tm, D), lambda i: (i, 0))
    dh_specs = [row_d if d == 1 else _perm_spec(d, D) for d in dils]
    dh_args = [a if d == 1 else a.reshape(d, T // d, D) for d, a in zip(dils, dhs)]
    return pl.pallas_call(
        body, name=name, grid=(T // tm,),
        in_specs=[row_d, _const_spec((1, D))] + dh_specs + [row_d],
        out_specs=[row_d, _const_spec((1, D))],
        out_shape=[jax.ShapeDtypeStruct((T, D), F32), jax.ShapeDtypeStruct((1, D), F32)],
        scratch_shapes=[_tile_scratch(D)],
        compiler_params=_cp(("arbitrary",)),
    )(x, g, *dh_args, dres)


CONV_TM = 256
CONV_HALO = 32
CONV_RB = 16


def _glu(ab):
    ab = ab.astype(F32)
    return ab[:, :D] * _sig(ab[:, D:])


def _ln_stats(z1):
    mu = jnp.mean(z1, axis=-1, keepdims=True)
    zc = z1 - mu
    rstd = lax.rsqrt(jnp.mean(zc * zc, axis=-1, keepdims=True) + EPS)
    return zc * rstd, rstd


def _fill_shifts(zs):
    n = zs.shape[1] - 8
    for s in range(1, 8):
        zs[s, pl.ds(0, n), :] = zs[0, pl.ds(s, n), :]


def _shifted(zs, start, rows):
    q, s = divmod(start, 8)
    return zs[s, pl.ds(8 * q, rows), :]


def _conv_fwd(ab, kern, dwb, lng, lnb, name):
    tm, hl, rb = CONV_TM, CONV_HALO, CONV_RB
    off = hl - (CONV_W - 1)

    def body(ab_ref, abh_ref, k_ref, dwb_ref, lng_ref, lnb_ref, z1_ref, z3_ref, zs):
        i = pl.program_id(0)
        zs[0, pl.ds(0, hl), :] = jnp.where(i > 0, _glu(abh_ref[...]), 0.0)
        zs[0, pl.ds(hl, tm), :] = _glu(ab_ref[...])
        _fill_shifts(zs)
        for b in range(tm // rb):
            acc = jnp.zeros((rb, D), F32)
            for j in range(CONV_W):
                acc = acc + _shifted(zs, b * rb + off + j, rb) * k_ref[pl.ds(j, 1), :]
            z1 = acc + dwb_ref[...]
            z1_ref[pl.ds(b * rb, rb), :] = z1
            zn, _ = _ln_stats(z1)
            z2 = zn * lng_ref[...] + lnb_ref[...]
            z3_ref[pl.ds(b * rb, rb), :] = (z2 * _sig(z2)).astype(BF16)

    row = pl.BlockSpec((tm, D), lambda i: (i, 0))
    return pl.pallas_call(
        body, name=name, grid=(T // tm,),
        in_specs=[pl.BlockSpec((tm, 2 * D), lambda i: (i, 0)),
                  pl.BlockSpec((hl, 2 * D), lambda i: (jnp.maximum(i * (tm // hl) - 1, 0), 0)),
                  _const_spec((32, D)), _const_spec((1, D)), _const_spec((1, D)), _const_spec((1, D))],
        out_specs=[row, row],
        out_shape=[jax.ShapeDtypeStruct((T, D), F32), jax.ShapeDtypeStruct((T, D), BF16)],
        scratch_shapes=[pltpu.VMEM((8, hl + tm, D), F32)],
        compiler_params=_cp(("parallel",)),
    )(ab, ab, kern, dwb, lng, lnb)


def _conv_bwd(dz3, z1, ab, kern, lng, lnb, name):
    tm, hl, rb = CONV_TM, CONV_HALO, CONV_RB
    off = hl - (CONV_W - 1)
    nsteps = T // tm

    def ln_bwd(dz3v, z1v, lngv, lnbv):
        zn, rstd = _ln_stats(z1v)
        z2 = zn * lngv + lnbv
        s = _sig(z2)
        dz2 = dz3v * (s * (1.0 + z2 * (1.0 - s)))
        dzn = dz2 * lngv
        dz1 = rstd * (dzn - jnp.mean(dzn, axis=-1, keepdims=True)
                      - zn * jnp.mean(dzn * zn, axis=-1, keepdims=True))
        return dz1, dz2, zn

    def body(dz3_ref, dz3h_ref, z1_ref, z1h_ref, ab_ref, abh_ref, k_ref, lng_ref, lnb_ref,
             dab_ref, dk_ref, dvec_ref, zs, dzs):
        i = pl.program_id(0)
        lngv, lnbv = lng_ref[...], lnb_ref[...]

        @pl.when(i == 0)
        def _():
            dk_ref[...] = jnp.zeros_like(dk_ref)
            dvec_ref[...] = jnp.zeros_like(dvec_ref)

        dz1, dz2, zn = ln_bwd(dz3_ref[...], z1_ref[...], lngv, lnbv)
        dvec_ref[pl.ds(0, 1), :] += jnp.sum(dz1, axis=0, keepdims=True)
        dvec_ref[pl.ds(1, 1), :] += jnp.sum(dz2 * zn, axis=0, keepdims=True)
        dvec_ref[pl.ds(2, 1), :] += jnp.sum(dz2, axis=0, keepdims=True)
        dzs[0, pl.ds(0, tm), :] = dz1
        dz1h, _, _ = ln_bwd(dz3h_ref[...], z1h_ref[...], lngv, lnbv)
        dzs[0, pl.ds(tm, hl), :] = jnp.where(i < nsteps - 1, dz1h, 0.0)
        _fill_shifts(dzs)
        zs[0, pl.ds(0, hl), :] = jnp.where(i > 0, _glu(abh_ref[...]), 0.0)
        zs[0, pl.ds(hl, tm), :] = _glu(ab_ref[...])
        _fill_shifts(zs)

        for j in range(CONV_W):
            tot = jnp.zeros((rb, D), F32)
            for b in range(tm // rb):
                tot = tot + dzs[0, pl.ds(b * rb, rb), :] * _shifted(zs, b * rb + off + j, rb)
            dk_ref[pl.ds(j, 1), :] += jnp.sum(tot, axis=0, keepdims=True)

        for b in range(tm // rb):
            acc = jnp.zeros((rb, D), F32)
            for j in range(CONV_W):
                acc = acc + _shifted(dzs, b * rb + (CONV_W - 1) - j, rb) * k_ref[pl.ds(j, 1), :]
            av = ab_ref[pl.ds(b * rb, rb), pl.ds(0, D)].astype(F32)
            sb = _sig(ab_ref[pl.ds(b * rb, rb), pl.ds(D, D)].astype(F32))
            dab_ref[pl.ds(b * rb, rb), pl.ds(0, D)] = (acc * sb).astype(BF16)
            dab_ref[pl.ds(b * rb, rb), pl.ds(D, D)] = (acc * av * sb * (1.0 - sb)).astype(BF16)

    row = pl.BlockSpec((tm, D), lambda i: (i, 0))
    nxt = pl.BlockSpec((hl, D), lambda i: (jnp.minimum((i + 1) * (tm // hl), T // hl - 1), 0))
    return pl.pallas_call(
        body, name=name, grid=(nsteps,),
        in_specs=[row, nxt, row, nxt,
                  pl.BlockSpec((tm, 2 * D), lambda i: (i, 0)),
                  pl.BlockSpec((hl, 2 * D), lambda i: (jnp.maximum(i * (tm // hl) - 1, 0), 0)),
                  _const_spec((32, D)), _const_spec((1, D)), _const_spec((1, D))],
        out_specs=[pl.BlockSpec((tm, 2 * D), lambda i: (i, 0)), _const_spec((32, D)), _const_spec((8, D))],
        out_shape=[jax.ShapeDtypeStruct((T, 2 * D), BF16), jax.ShapeDtypeStruct((32, D), F32),
                   jax.ShapeDtypeStruct((8, D), F32)],
        scratch_shapes=[pltpu.VMEM((8, hl + tm, D), F32), pltpu.VMEM((8, tm + hl, D), F32)],
        compiler_params=_cp(("arbitrary",)),
    )(dz3, dz3, z1, z1, ab, ab, kern, lng, lnb)


def _alibi_slopes():
    h = np.arange(1, 3 * NHG + 1, dtype=np.float32)
    return np.power(np.float32(2.0), -8.0 * h / np.float32(3 * NHG)).astype(np.float32)


def _band_bias(gi):
    _, dil = GROUPS[gi]
    slopes = _alibi_slopes()[gi * NHG:(gi + 1) * NHG]
    qi = np.arange(BLK)[:, None]
    ki = np.arange(2 * BLK)[None, :]
    steps = BLK + qi - ki
    band = (steps >= 0) & (steps <= BLK)
    bias = -slopes[:, None, None] * (dil * steps).astype(np.float32)[None]
    return jnp.asarray(np.where(band[None], bias, np.float32(NEG)).astype(np.float32))


QB_FWD = 4


def _scores(q, kcat, bias, blk, seg):
    s = _dot(q, kcat, NT) * (HEAD ** -0.5) + bias
    col = lax.broadcasted_iota(jnp.int32, s.shape, 1)
    first = (blk % seg) == 0
    return jnp.where(jnp.logical_and(first, col < BLK), NEG, s)


def _attn_fwd(qkv, gi, name):
    seg = (T // GROUPS[gi][1]) // BLK

    qb = QB_FWD

    def body(q_ref, kp_ref, kc_ref, vp_ref, vc_ref, bias_ref, o_ref, l_ref):
        n = pl.program_id(0)
        for h in range(NHG):
            cols = pl.ds(h * HEAD, HEAD)
            kwin = jnp.concatenate([kp_ref[:, cols], kc_ref[:, cols]], axis=0)
            vwin = jnp.concatenate([vp_ref[:, cols], vc_ref[:, cols]], axis=0)
            bias = bias_ref[h]
            for b in range(qb):
                rows = pl.ds(b * BLK, BLK)
                s = _scores(q_ref[rows, cols], kwin[b * BLK:(b + 2) * BLK], bias, n * qb + b, seg)
                mx = jnp.max(s, axis=-1, keepdims=True)
                p = jnp.exp(s - mx)
                den = jnp.sum(p, axis=-1, keepdims=True)
                o_ref[rows, cols] = _dot(p.astype(BF16), vwin[b * BLK:(b + 2) * BLK], NN) / den
                l_ref[rows, cols] = jnp.broadcast_to(mx + jnp.log(den), (BLK, HEAD))

    prev = lambda n: jnp.maximum(n * qb - 1, 0)
    cur = lambda part: pl.BlockSpec((qb * BLK, AW), lambda n: (n, part))
    halo = lambda part: pl.BlockSpec((BLK, AW), lambda n: (prev(n), part))
    return pl.pallas_call(
        body, name=name, grid=(T // (qb * BLK),),
        in_specs=[cur(0), halo(1), cur(1), halo(2), cur(2), _const_spec((NHG, BLK, 2 * BLK))],
        out_specs=[cur(0), cur(0)],
        out_shape=[jax.ShapeDtypeStruct((T, AW), F32), jax.ShapeDtypeStruct((T, AW), F32)],
        compiler_params=_cp(("parallel",)),
    )(qkv, qkv, qkv, qkv, qkv, _band_bias(gi))


def _attn_bwd(qkv, dob, lse, delta, gi, name):
    seg = (T // GROUPS[gi][1]) // BLK
    qb = QB_FWD
    nb = T // (qb * BLK)
    scale = HEAD ** -0.5

    def body(q_ref, kp_ref, kc_ref, vp_ref, vc_ref, bias_ref, do_ref, l_ref, dl_ref, out_ref, dk_acc, dv_acc):
        n = pl.program_id(0)
        for h in range(NHG):
            cols = pl.ds(h * HEAD, HEAD)
            kwin = jnp.concatenate([kp_ref[:, cols], kc_ref[:, cols]], axis=0)
            vwin = jnp.concatenate([vp_ref[:, cols], vc_ref[:, cols]], axis=0)
            bias = bias_ref[h]
            for b in range(qb):
                rows = pl.ds(b * BLK, BLK)
                q = q_ref[rows, cols]
                kcat = kwin[b * BLK:(b + 2) * BLK]
                s = _scores(q, kcat, bias, n * qb + b, seg)
                p = jnp.exp(s - l_ref[rows, pl.ds(h * HEAD, 1)])
                dov = do_ref[rows, cols]
                dv2 = _dot(p.astype(BF16), dov, TN)
                dp = _dot(dov, vwin[b * BLK:(b + 2) * BLK], NT)
                dsb = (p * (dp - dl_ref[rows, pl.ds(h * HEAD, 1)]) * scale).astype(BF16)
                row = pl.ds(pl.multiple_of((n * qb + b) * BLK, BLK), BLK)
                out_ref[0, row, cols] = _dot(dsb, kcat, NN).astype(BF16)
                dk2 = _dot(dsb, q, TN)
                dk_acc[row, cols] = dk2[BLK:]
                dv_acc[row, cols] = dv2[BLK:]

                def add_prev(dk2=dk2, dv2=dv2, b=b, cols=cols):
                    prow = pl.ds(pl.multiple_of((n * qb + b - 1) * BLK, BLK), BLK)
                    dk_acc[prow, cols] += dk2[:BLK]
                    dv_acc[prow, cols] += dv2[:BLK]

                if b == 0:
                    pl.when(n > 0)(add_prev)
                else:
                    add_prev()

        @pl.when(n == nb - 1)
        def _():
            out_ref[1] = dk_acc[...].astype(BF16)
            out_ref[2] = dv_acc[...].astype(BF16)

    prev = lambda n: jnp.maximum(n * qb - 1, 0)
    cur = lambda part: pl.BlockSpec((qb * BLK, AW), lambda n: (n, part))
    halo = lambda part: pl.BlockSpec((BLK, AW), lambda n: (prev(n), part))
    return pl.pallas_call(
        body, name=name, grid=(nb,),
        in_specs=[cur(0), halo(1), cur(1), halo(2), cur(2), _const_spec((NHG, BLK, 2 * BLK)),
                  cur(0), cur(0), cur(0)],
        out_specs=pl.BlockSpec((3, T, AW), lambda n: (0, 0, 0), pipeline_mode=pl.Buffered(1)),
        out_shape=jax.ShapeDtypeStruct((3, T, AW), BF16),
        scratch_shapes=[pltpu.VMEM((T, AW), F32), pltpu.VMEM((T, AW), F32)],
        compiler_params=_cp(("arbitrary",)),
    )(qkv, qkv, qkv, qkv, qkv, _band_bias(gi), dob, lse, delta)


def _merge(outs, lses, name):
    tm = PERM_TM
    dils = [d for _, d in GROUPS]
    ng = len(dils)

    def body(*refs):
        in_refs = refs[:2 * ng]
        a_ref, ab_ref = refs[2 * ng:2 * ng + 2]
        lse_refs = refs[2 * ng + 2:3 * ng + 2]
        tile = refs[-1]

        def token_order(ref, dil):
            if dil == 1:
                return ref[...]
            _load_unperm(ref, tile, dil)
            return _get_tile(tile)

        os = [token_order(in_refs[2 * i], d) for i, d in enumerate(dils)]
        ls = [token_order(in_refs[2 * i + 1], d) for i, d in enumerate(dils)]
        mx = jnp.maximum(jnp.maximum(ls[0], ls[1]), ls[2])
        es = [jnp.exp(v - mx) for v in ls]
        tot = es[0] + es[1] + es[2]
        att = (es[0] / tot) * os[0] + (es[1] / tot) * os[1] + (es[2] / tot) * os[2]
        a_ref[...] = att
        ab_ref[...] = att.astype(BF16)
        lse = mx + jnp.log(tot)
        _put_tile(tile, lse)
        for dil, ref in zip(dils, lse_refs):
            if dil == 1:
                ref[...] = lse
            else:
                _store_perm(ref, tile, dil)

    row = pl.BlockSpec((tm, AW), lambda i: (i, 0))
    specs = [row if d == 1 else _perm_spec(d, AW) for d in dils]
    args = []
    for d, o, l in zip(dils, outs, lses):
        args += [o, l] if d == 1 else [o.reshape(d, T // d, AW), l.reshape(d, T // d, AW)]
    out = pl.pallas_call(
        body, name=name, grid=(T // tm,),
        in_specs=[sp for sp in specs for _ in range(2)], out_specs=[row, row] + specs,
        out_shape=[jax.ShapeDtypeStruct((T, AW), F32), jax.ShapeDtypeStruct((T, AW), BF16)]
        + [jax.ShapeDtypeStruct((T, AW), F32) if d == 1 else _perm_shape(d, AW, F32) for d in dils],
        scratch_shapes=[_tile_scratch(AW)],
        compiler_params=_cp(("parallel",)),
    )(*args)
    return out[0], out[1], [o.reshape(T, AW) for o in out[2:]]


def _mix_out(z3b, attnb, gates, wc, wa_t, wo, x1, name):
    tm = 512

    def body(z_ref, a_ref, g_ref, wc_ref, wa_ref, wo_ref, x_ref, xo_ref, yc_ref, ya_ref, mx_ref):
        yc = _dot(z_ref[...], wc_ref[...], NN)
        ya = _dot(a_ref[...], wa_ref[...], NT)
        yc_ref[...] = yc
        ya_ref[...] = ya
        gv = g_ref[...].astype(F32)
        mixed = (_sig(gv[:, :D]) * yc + _sig(gv[:, D:]) * ya).astype(BF16)
        mx_ref[...] = mixed
        xo_ref[...] = x_ref[...] + _dot(mixed, wo_ref[...], NN)

    row = pl.BlockSpec((tm, D), lambda i: (i, 0))
    return pl.pallas_call(
        body, name=name, grid=(T // tm,),
        in_specs=[row, pl.BlockSpec((tm, AW), lambda i: (i, 0)), pl.BlockSpec((tm, 2 * D), lambda i: (i, 0)),
                  _const_spec((D, D)), _const_spec((D, AW)), _const_spec((D, D)), row],
        out_specs=[row, row, row, row],
        out_shape=[jax.ShapeDtypeStruct((T, D), F32), jax.ShapeDtypeStruct((T, D), F32),
                   jax.ShapeDtypeStruct((T, D), F32), jax.ShapeDtypeStruct((T, D), BF16)],
        compiler_params=_cp(("parallel",)),
    )(z3b, attnb, gates, wc, wa_t, wo, x1)


def _mix_out_bwd(dx2, gates, yc, ya, attn, wc, wa_t, wo, name):
    tm = PERM_TM
    dils = [d for _, d in GROUPS]
    ng = len(dils)

    def body(dx_ref, g_ref, yc_ref, ya_ref, at_ref, wc_ref, wa_ref, wo_ref,
             dg_ref, dyc_ref, dya_ref, dxb_ref, dz3_ref, *rest):
        dat_refs, dl_refs, tile = rest[:ng], rest[ng:2 * ng], rest[-1]
        dxb = dx_ref[...].astype(BF16)
        dxb_ref[...] = dxb
        dmix = _dot(dxb, wo_ref[...], NT)
        gv = g_ref[...].astype(F32)
        sc = _sig(gv[:, :D])
        sa = _sig(gv[:, D:])
        ycv, yav = yc_ref[...], ya_ref[...]
        dg_ref[:, pl.ds(0, D)] = (dmix * ycv * sc * (1.0 - sc)).astype(BF16)
        dg_ref[:, pl.ds(D, D)] = (dmix * yav * sa * (1.0 - sa)).astype(BF16)
        dyc = (dmix * sc).astype(BF16)
        dya = (dmix * sa).astype(BF16)
        dyc_ref[...] = dyc
        dya_ref[...] = dya
        dz3_ref[...] = _dot(dyc, wc_ref[...], NT)
        dat = _dot(dya, wa_ref[...], NN)
        prod = dat * at_ref[...]
        delta = jnp.concatenate(
            [jnp.broadcast_to(jnp.sum(prod[:, h * HEAD:(h + 1) * HEAD], axis=-1, keepdims=True), (tm, HEAD))
             for h in range(NHG)], axis=1)
        for value, out_refs in ((dat, dat_refs), (delta, dl_refs)):
            _put_tile(tile, value)
            for dil, ref in zip(dils, out_refs):
                if dil == 1:
                    ref[...] = value.astype(ref.dtype)
                else:
                    _store_perm(ref, tile, dil)

    row = pl.BlockSpec((tm, D), lambda i: (i, 0))
    row2 = pl.BlockSpec((tm, 2 * D), lambda i: (i, 0))
    rowa = pl.BlockSpec((tm, AW), lambda i: (i, 0))
    aspecs = [rowa if d == 1 else _perm_spec(d, AW) for d in dils]

    def ashapes(dtype):
        return [jax.ShapeDtypeStruct((T, AW), dtype) if d == 1 else _perm_shape(d, AW, dtype) for d in dils]

    out = pl.pallas_call(
        body, name=name, grid=(T // tm,),
        in_specs=[row, row2, row, row, rowa, _const_spec((D, D)), _const_spec((D, AW)), _const_spec((D, D))],
        out_specs=[row2, row, row, row, row] + aspecs + aspecs,
        out_shape=[jax.ShapeDtypeStruct((T, 2 * D), BF16), jax.ShapeDtypeStruct((T, D), BF16),
                   jax.ShapeDtypeStruct((T, D), BF16), jax.ShapeDtypeStruct((T, D), BF16),
                   jax.ShapeDtypeStruct((T, D), F32)] + ashapes(BF16) + ashapes(F32),
        scratch_shapes=[_tile_scratch(AW)],
        compiler_params=_cp(("parallel",)),
    )(dx2, gates, yc, ya, attn, wc, wa_t, wo)
    dats = [o.reshape(T, AW) for o in out[5:5 + ng]]
    deltas = [o.reshape(T, AW) for o in out[5 + ng:5 + 2 * ng]]
    return out[0], out[1], out[2], out[3], out[4], dats, deltas


def _peer(k):
    x, y, c = lax.axis_index("x"), lax.axis_index("y"), lax.axis_index("c")
    px = 1 - x if k & 4 else x
    py = 1 - y if k & 2 else y
    pc = 1 - c if k & 1 else c
    return (px, py, pc), 4 * px + 2 * py + pc


HBM_SPEC = pl.BlockSpec(memory_space=pltpu.HBM)
SEM_SPEC = pl.BlockSpec(memory_space=pltpu.SEMAPHORE)
EFFECT = pltpu.SideEffectType.DATAFLOW_SIDE_EFFECTING


def _my_place():
    return 4 * lax.axis_index("x") + 2 * lax.axis_index("y") + lax.axis_index("c")


def _tie(a, order_after, name):
    na = len(order_after)

    def body(*refs):
        del refs

    return pl.pallas_call(
        body, name=name, in_specs=[pl.BlockSpec(memory_space=pl.ANY)] * (1 + na),
        out_specs=pl.BlockSpec(memory_space=pl.ANY), out_shape=jax.ShapeDtypeStruct(a.shape, a.dtype),
        input_output_aliases={0: 0},
    )(a, *order_after)


def _prep_gather(ws, transposed, order_after, name):
    me = jnp.reshape(_my_place(), (1,)).astype(jnp.int32)
    n = len(ws)
    na = len(order_after)
    shapes = []
    for wv, tr in zip(ws, transposed):
        r, c = (wv.shape[2], wv.shape[1]) if tr else wv.shape[1:]
        shapes.append(((32, c), F32) if r == CONV_W else ((r, c), BF16))

    def body(me_ref, *refs):
        del me_ref
        ins, outs = refs[:n], refs[n + na:]
        for wv, tr, i_ref, o_ref in zip(ws, transposed, ins, outs):
            if wv.shape[1] == CONV_W:
                o_ref[pl.ds(0, CONV_W), :] = i_ref[...]
                o_ref[pl.ds(CONV_W, 1), :] = jnp.zeros((1, wv.shape[2]), F32)
            elif tr:
                o_ref[...] = jnp.transpose(i_ref[...]).astype(BF16)
            else:
                o_ref[...] = i_ref[...].astype(BF16)

    grid_spec = pltpu.PrefetchScalarGridSpec(
        num_scalar_prefetch=1, grid=(1,),
        in_specs=[pl.BlockSpec((None,) + wv.shape[1:], lambda i, m: (0, 0, 0)) for wv in ws]
        + [pl.BlockSpec(memory_space=pl.ANY)] * na,
        out_specs=[pl.BlockSpec(shp, lambda i, m: (m[0], 0)) for shp, _ in shapes])
    return pl.pallas_call(
        body, name=name, grid_spec=grid_spec,
        out_shape=[jax.ShapeDtypeStruct((NDEV * shp[0], shp[1]), dt) for shp, dt in shapes],
        compiler_params=_cp(("arbitrary",)),
    )(me, *ws, *order_after)


GATHER_A = ((1, 0), (2, 0), (4, 0), (6, 0))
GATHER_B = ((1, 2), (1, 4), (1, 6))


def _gather_start(lands, plan, order_after, name):
    n = len(lands)
    na = len(order_after)
    npl = len(plan)

    def body(*refs):
        land_refs = refs[:n]
        send, recv = refs[n + na], refs[n + na + 1]
        token = refs[-1]
        for w in range(n):
            rows = lands[w].shape[0] // NDEV
            for p, (k, j) in enumerate(plan):
                peer, _ = _peer(k)
                _, blk = _peer(j)
                part = land_refs[w].at[pl.ds(blk * rows, rows)]
                i = w * npl + p
                pltpu.make_async_remote_copy(src_ref=part, dst_ref=part, send_sem=send.at[i], recv_sem=recv.at[i],
                                             device_id=peer, device_id_type=MESH_ID).start()
        token[...] = jnp.zeros_like(token)

    nsem = n * npl
    bufs = [pltpu.with_memory_space_constraint(a, pltpu.HBM) for a in lands]
    out = pl.pallas_call(
        body, name=name,
        in_specs=[HBM_SPEC] * n + [pl.BlockSpec(memory_space=pl.ANY)] * na,
        out_specs=[SEM_SPEC, SEM_SPEC] + [HBM_SPEC] * n + [pl.BlockSpec(memory_space=pltpu.VMEM)],
        out_shape=[pltpu.SemaphoreType.DMA((nsem,)), pltpu.SemaphoreType.DMA((nsem,))]
        + [pltpu.HBM(a.shape, a.dtype) for a in bufs] + [jax.ShapeDtypeStruct((8, 128), F32)],
        input_output_aliases={i: 2 + i for i in range(n)},
        compiler_params=pltpu.CompilerParams(has_side_effects=EFFECT),
    )(*bufs, *order_after)
    return out[0], out[1], out[2:2 + n], out[-1]


def _gather_wait(started, plan, order_after, name):
    send, recv, lands, _ = started
    n = len(lands)
    na = len(order_after)
    npl = len(plan)

    def body(*refs):
        land_refs = refs[:n]
        send_ref, recv_ref = refs[n], refs[n + 1]
        for w in range(n):
            rows = lands[w].shape[0] // NDEV
            for p, (k, j) in enumerate(plan):
                peer, _ = _peer(k)
                _, blk = _peer(j)
                part = land_refs[w].at[pl.ds(blk * rows, rows)]
                i = w * npl + p
                cp = pltpu.make_async_remote_copy(src_ref=part, dst_ref=part, send_sem=send_ref.at[i],
                                                  recv_sem=recv_ref.at[i], device_id=peer, device_id_type=MESH_ID)
                cp.wait_send()
                cp.wait_recv()

    out = pl.pallas_call(
        body, name=name,
        in_specs=[HBM_SPEC] * n + [SEM_SPEC, SEM_SPEC] + [pl.BlockSpec(memory_space=pl.ANY)] * na,
        out_specs=[HBM_SPEC] * n,
        out_shape=[pltpu.HBM(a.shape, a.dtype) for a in lands],
        input_output_aliases={i: i for i in range(n)},
        compiler_params=pltpu.CompilerParams(has_side_effects=EFFECT),
    )(*lands, send, recv, *order_after)
    return list(out)


def _copy_ends(kind, src, land, me, plin, k):
    if kind == "scatter":
        rows = src.shape[0] // NDEV
        return src.at[pl.ds(plin * rows, rows)], land.at[k - 1]
    return src, land.at[me]


def _landing(kind, src):
    me = _my_place()
    if kind == "scatter":
        return lax.empty((NDEV - 1, src.shape[0] // NDEV) + src.shape[1:], src.dtype)
    land = lax.empty((NDEV,) + src.shape, src.dtype)
    return lax.dynamic_update_slice(land, src[None], (me,) + (0,) * src.ndim)


def _send_start(kinds, srcs, order_after, name):
    n = len(srcs)
    lands = [_landing(kd, s) for kd, s in zip(kinds, srcs)]
    na = len(order_after)

    def body(*refs):
        src_refs, land_refs = refs[:n], refs[n:2 * n]
        send, recv = refs[2 * n + na], refs[2 * n + na + 1]
        token = refs[-1]
        _, me = _peer(0)
        for w in range(n):
            for k in range(1, NDEV):
                peer, plin = _peer(k)
                s, d = _copy_ends(kinds[w], src_refs[w], land_refs[w], me, plin, k)
                i = w * (NDEV - 1) + k - 1
                pltpu.make_async_remote_copy(src_ref=s, dst_ref=d, send_sem=send.at[i], recv_sem=recv.at[i],
                                             device_id=peer, device_id_type=MESH_ID).start()
        token[...] = jnp.zeros_like(token)

    nsem = n * (NDEV - 1)
    bufs = [pltpu.with_memory_space_constraint(a, pltpu.HBM) for a in list(srcs) + lands]
    out = pl.pallas_call(
        body, name=name,
        in_specs=[HBM_SPEC] * (2 * n) + [pl.BlockSpec(memory_space=pl.ANY)] * na,
        out_specs=[SEM_SPEC, SEM_SPEC] + [HBM_SPEC] * (2 * n) + [pl.BlockSpec(memory_space=pltpu.VMEM)],
        out_shape=[pltpu.SemaphoreType.DMA((nsem,)), pltpu.SemaphoreType.DMA((nsem,))]
        + [pltpu.HBM(a.shape, a.dtype) for a in bufs] + [jax.ShapeDtypeStruct((8, 128), F32)],
        input_output_aliases={i: 2 + i for i in range(2 * n)},
        compiler_params=pltpu.CompilerParams(has_side_effects=EFFECT),
    )(*bufs, *order_after)
    return out[0], out[1], out[2:2 + n], out[2 + n:2 + 2 * n], out[-1]


def _send_wait(kinds, started, order_after, name):
    send, recv, srcs, lands, _ = started
    n = len(srcs)
    na = len(order_after)

    def body(*refs):
        src_refs, land_refs = refs[:n], refs[n:2 * n]
        send_ref, recv_ref = refs[2 * n], refs[2 * n + 1]
        _, me = _peer(0)
        for w in range(n):
            for k in range(1, NDEV):
                peer, plin = _peer(k)
                s, d = _copy_ends(kinds[w], src_refs[w], land_refs[w], me, plin, k)
                i = w * (NDEV - 1) + k - 1
                cp = pltpu.make_async_remote_copy(src_ref=s, dst_ref=d, send_sem=send_ref.at[i],
                                                  recv_sem=recv_ref.at[i], device_id=peer, device_id_type=MESH_ID)
                cp.wait_send()
                cp.wait_recv()

    bufs = list(srcs) + list(lands)
    out = pl.pallas_call(
        body, name=name,
        in_specs=[HBM_SPEC] * (2 * n) + [SEM_SPEC, SEM_SPEC] + [pl.BlockSpec(memory_space=pl.ANY)] * na,
        out_specs=[HBM_SPEC] * (2 * n),
        out_shape=[pltpu.HBM(a.shape, a.dtype) for a in bufs],
        input_output_aliases={i: i for i in range(2 * n)},
        compiler_params=pltpu.CompilerParams(has_side_effects=EFFECT),
    )(*bufs, send, recv, *order_after)
    return out[:n], out[n:]


def _gsum(own, land, name):
    rows, cols = own.shape
    tr = rows // 2 if rows * cols > 512 * 1024 and rows % 32 == 0 else rows

    def body(own_ref, l_ref, o_ref):
        tot = own_ref[...].astype(F32)
        for s in range(NDEV - 1):
            tot = tot + l_ref[s].astype(F32)
        o_ref[...] = tot

    return pl.pallas_call(
        body, name=name, grid=(rows // tr,),
        in_specs=[pl.BlockSpec((tr, cols), lambda i: (i, 0)),
                  pl.BlockSpec((NDEV - 1, tr, cols), lambda i: (0, i, 0))],
        out_specs=pl.BlockSpec((tr, cols), lambda i: (i, 0)),
        out_shape=jax.ShapeDtypeStruct((rows, cols), F32),
        compiler_params=_cp(("parallel",)),
    )(own, land)


def _adamw_math(w, g, m, v):
    m2 = B1 * m + (1.0 - B1) * g
    v2 = B2 * v + (1.0 - B2) * (g * g)
    m_hat = m2 / (1.0 - B1 ** STEP)
    v_hat = v2 / (1.0 - B2 ** STEP)
    delta = -LR * (m_hat / (jnp.sqrt(v_hat) + AEPS) + WD * w)
    return delta, m2, v2


def _adamw(w, g, m, v, name):
    rows, cols = w.shape
    tr = 256 if rows % 256 == 0 and rows > 256 else rows

    def body(w_ref, g_ref, m_ref, v_ref, d_ref, mo_ref, vo_ref):
        d, m2, v2 = _adamw_math(w_ref[...], g_ref[...], m_ref[...], v_ref[...])
        d_ref[...] = d
        mo_ref[...] = m2
        vo_ref[...] = v2

    blk = pl.BlockSpec((tr, cols), lambda i: (i, 0))
    return pl.pallas_call(
        body, name=name, grid=(rows // tr,), in_specs=[blk] * 4, out_specs=[blk] * 3,
        out_shape=[jax.ShapeDtypeStruct((rows, cols), F32)] * 3,
        compiler_params=_cp(("parallel",)),
    )(w, g, m, v)


UPD_TC = 256


def _update(src, land, w, m, v, transposed, name):
    rows, cols = land.shape[1:]
    tc = min(UPD_TC, cols)
    me = jnp.reshape(_my_place(), (1,)).astype(jnp.int32)

    def body(me_ref, own_ref, l_ref, w_ref, m_ref, v_ref, g_ref, d_ref, mo_ref, vo_ref):
        del me_ref
        tot = own_ref[...].astype(F32)
        for s in range(NDEV - 1):
            tot = tot + l_ref[s].astype(F32)
        g = jnp.transpose(tot) if transposed else tot
        g_ref[...] = g
        d, m2, v2 = _adamw_math(w_ref[...], g, m_ref[...], v_ref[...])
        d_ref[...] = d
        mo_ref[...] = m2
        vo_ref[...] = v2

    assert w.ndim == 3 and w.shape[0] == 1
    if transposed:
        wblk = pl.BlockSpec((None, tc, rows), lambda j, p: (0, j, 0))
    else:
        wblk = pl.BlockSpec((None, rows, tc), lambda j, p: (0, 0, j))
    grid_spec = pltpu.PrefetchScalarGridSpec(
        num_scalar_prefetch=1, grid=(cols // tc,),
        in_specs=[pl.BlockSpec((rows, tc), lambda j, p: (p[0], j)),
                  pl.BlockSpec((NDEV - 1, rows, tc), lambda j, p: (0, 0, j)), wblk, wblk, wblk],
        out_specs=[wblk] * 4)
    return pl.pallas_call(
        body, name=name, grid_spec=grid_spec, out_shape=[jax.ShapeDtypeStruct(w.shape, F32)] * 4,
        compiler_params=_cp(("parallel",)),
    )(me, src, land, w, m, v)


def _small_update(vland, w8, m8, v8, name):
    def body(l_ref, w_ref, m_ref, v_ref, g_ref, d_ref, mo_ref, vo_ref):
        g = l_ref[0]
        for s in range(1, NDEV):
            g = g + l_ref[s]
        g_ref[...] = g
        d, m2, v2 = _adamw_math(w_ref[...], g, m_ref[...], v_ref[...])
        d_ref[...] = d
        mo_ref[...] = m2
        vo_ref[...] = v2

    return pl.pallas_call(
        body, name=name, out_shape=[jax.ShapeDtypeStruct((8, D), F32)] * 4,
        compiler_params=_cp(None),
    )(vland, w8, m8, v8)


def kernel(x, ffn1_norm, ffn1_w_gate, ffn1_w_up, ffn1_w_down, mix_norm, w_in, conv_dw_kernel, conv_dw_bias, conv_ln_gain, conv_ln_bias, conv_w_out, attn_w_out, w_o, ffn2_norm, ffn2_w_gate, ffn2_w_up, ffn2_w_down, final_norm, loss_target, m_ffn1_norm, m_ffn1_w_gate, m_ffn1_w_up, m_ffn1_w_down, m_mix_norm, m_w_in, m_conv_dw_kernel, m_conv_dw_bias, m_conv_ln_gain, m_conv_ln_bias, m_conv_w_out, m_attn_w_out, m_w_o, m_ffn2_norm, m_ffn2_w_gate, m_ffn2_w_up, m_ffn2_w_down, m_final_norm, v_ffn1_norm, v_ffn1_w_gate, v_ffn1_w_up, v_ffn1_w_down, v_mix_norm, v_w_in, v_conv_dw_kernel, v_conv_dw_bias, v_conv_ln_gain, v_conv_ln_bias, v_conv_w_out, v_attn_w_out, v_w_o, v_ffn2_norm, v_ffn2_w_gate, v_ffn2_w_up, v_ffn2_w_down, v_final_norm):
    names = ["ffn1_norm", "ffn1_w_gate", "ffn1_w_up", "ffn1_w_down", "mix_norm", "w_in", "conv_dw_kernel",
             "conv_dw_bias", "conv_ln_gain", "conv_ln_bias", "conv_w_out", "attn_w_out", "w_o", "ffn2_norm",
             "ffn2_w_gate", "ffn2_w_up", "ffn2_w_down", "final_norm"]
    w = dict(ffn1_norm=ffn1_norm, ffn1_w_gate=ffn1_w_gate, ffn1_w_up=ffn1_w_up, ffn1_w_down=ffn1_w_down, mix_norm=mix_norm, w_in=w_in, conv_dw_kernel=conv_dw_kernel, conv_dw_bias=conv_dw_bias, conv_ln_gain=conv_ln_gain, conv_ln_bias=conv_ln_bias, conv_w_out=conv_w_out, attn_w_out=attn_w_out, w_o=w_o, ffn2_norm=ffn2_norm, ffn2_w_gate=ffn2_w_gate, ffn2_w_up=ffn2_w_up, ffn2_w_down=ffn2_w_down, final_norm=final_norm)
    mo = dict(ffn1_norm=m_ffn1_norm, ffn1_w_gate=m_ffn1_w_gate, ffn1_w_up=m_ffn1_w_up, ffn1_w_down=m_ffn1_w_down, mix_norm=m_mix_norm, w_in=m_w_in, conv_dw_kernel=m_conv_dw_kernel, conv_dw_bias=m_conv_dw_bias, conv_ln_gain=m_conv_ln_gain, conv_ln_bias=m_conv_ln_bias, conv_w_out=m_conv_w_out, attn_w_out=m_attn_w_out, w_o=m_w_o, ffn2_norm=m_ffn2_norm, ffn2_w_gate=m_ffn2_w_gate, ffn2_w_up=m_ffn2_w_up, ffn2_w_down=m_ffn2_w_down, final_norm=m_final_norm)
    vo = dict(ffn1_norm=v_ffn1_norm, ffn1_w_gate=v_ffn1_w_gate, ffn1_w_up=v_ffn1_w_up, ffn1_w_down=v_ffn1_w_down, mix_norm=v_mix_norm, w_in=v_w_in, conv_dw_kernel=v_conv_dw_kernel, conv_dw_bias=v_conv_dw_bias, conv_ln_gain=v_conv_ln_gain, conv_ln_bias=v_conv_ln_bias, conv_w_out=v_conv_w_out, attn_w_out=v_attn_w_out, w_o=v_w_o, ffn2_norm=v_ffn2_norm, ffn2_w_gate=v_ffn2_w_gate, ffn2_w_up=v_ffn2_w_up, ffn2_w_down=v_ffn2_w_down, final_norm=v_final_norm)
    col_sharded = ("ffn1_w_gate", "ffn1_w_up", "w_in", "attn_w_out", "ffn2_w_gate", "ffn2_w_up")
    row_sharded = ("ffn1_w_down", "conv_w_out", "w_o", "ffn2_w_down")
    small = ("ffn1_norm", "mix_norm", "ffn2_norm", "final_norm", "conv_dw_bias", "conv_ln_gain", "conv_ln_bias")

    ag_groups = (("ffn1_w_gate", "ffn1_w_up", "ffn1_w_down"),
                 ("w_in", "attn_w_out", "conv_w_out", "w_o", "conv_dw_kernel"),
                 ("ffn2_w_gate", "ffn2_w_up", "ffn2_w_down"))
    ag, order = [], []
    for gi, grp in enumerate(ag_groups):
        lands = _prep_gather([w[n] for n in grp], [n in col_sharded for n in grp], order, f"gather_prep{gi}")
        st = _gather_start(lands, GATHER_A, [], f"gather_a_start{gi}")
        ag.append(st)
        order = [st[3]]

    def chips_in(gi, after):
        lands = _gather_wait(ag[gi], GATHER_A, after, f"gather_a_wait{gi}")
        return _gather_start(lands, GATHER_B, [], f"gather_b_start{gi}")

    def all_in(gi, st, after):
        return _gather_wait(st, GATHER_B, after, f"gather_b_wait{gi}")

    x0 = x[0]
    tgt = loss_target[0]
    gf = final_norm.reshape(1, D)

    wg1, wu1, wd1 = all_in(0, chips_in(0, [ag[2][3]]), [])
    x1, gg1, uu1 = _ffn_fwd(x0, ffn1_norm, wg1, wu1, wd1, "ffn1_fwd")
    win_t, wa_t, wc, wo, kern_blocks = all_in(1, chips_in(1, [x1]), [])
    kern = kern_blocks.reshape(NDEV, 32, D // NDEV).transpose(1, 0, 2).reshape(32, D)
    h2p = _norm_cast(x1, mix_norm, "mix_norm_fwd")
    h2 = h2p[0]
    ab = _mm(h2, win_t, mode="nt", m=T, n=2 * D, k=D, tm=1024, tn=512, tk=D, out_dtype=BF16, name="proj_conv")
    gates = _mm(h2, win_t, mode="nt", m=T, n=2 * D, k=D, tm=1024, tn=512, tk=D, out_dtype=BF16,
                b_map=lambda i, j, kk: (13 + j, 0), name="proj_gates")
    qkv = []
    for gi in range(len(GROUPS)):
        qkv.append(_mm(h2p[gi], win_t, mode="nt", m=T, n=3 * AW, k=D, tm=1024, tn=AW, tk=D, out_dtype=BF16,
                       b_map=lambda i, j, kk, gi=gi: (4 + gi + 3 * j, 0), name=f"proj_qkv{gi}"))
    z1, z3b = _conv_fwd(ab, kern, conv_dw_bias, conv_ln_gain, conv_ln_bias, "conv_fwd")
    ffn2_b = chips_in(2, [z3b])
    outs, lses = [], []
    for gi, (_, dil) in enumerate(GROUPS):
        o, l = _attn_fwd(qkv[gi], gi, f"attn_fwd{gi}")
        outs.append(o)
        lses.append(l)
    attn, attnb, lse = _merge(outs, lses, "attn_merge")
    x2, yc, ya, mixedb = _mix_out(z3b, attnb, gates, wc, wa_t, wo, x1, "mix_out_fwd")
    wg2, wu2, wd2 = all_in(2, ffn2_b, [x2])
    x3, gg2, uu2 = _ffn_fwd(x2, ffn2_norm, wg2, wu2, wd2, "ffn2_fwd")

    dx3, dgf, loss_part = _final(x3, gf, tgt, "final_norm_loss")
    dx2, dg3, dgb, dub, actb, hb, dob = _ffn_bwd(x2, ffn2_norm, gg2, uu2, dx3, wg2, wu2, wd2, "ffn2_bwd")
    grads = {}
    grads["ffn2_w_gate"] = _wgrad(dgb, hb, FF, D, "ffn2_dwg")
    grads["ffn2_w_up"] = _wgrad(dub, hb, FF, D, "ffn2_dwu")
    grads["ffn2_w_down"] = _wgrad(actb, dob, FF, D, "ffn2_dwd")
    rs_groups = [("ffn2_w_gate", "ffn2_w_up", "ffn2_w_down"),
                 ("attn_w_out", "conv_w_out", "w_o", "conv_dw_kernel"),
                 ("w_in",),
                 ("ffn1_w_gate",), ("ffn1_w_up",), ("ffn1_w_down",)]
    last = len(rs_groups) - 1
    rs = [_send_start(["scatter"] * 3, [grads[n] for n in rs_groups[0]], [], "scatter_start0")]
    dx2 = _tie(dx2, [rs[0][4]], "tie_after_scatter0")

    dgates, dycb, dyab, dx2b, dz3, dattnb, delta = _mix_out_bwd(dx2, gates, yc, ya, attn, wc, wa_t, wo, "mix_out_bwd")
    grads["w_o"] = _wgrad(mixedb, dx2b, D, D, "dw_o")
    grads["conv_w_out"] = _wgrad(z3b, dycb, D, D, "dw_conv_out")
    grads["attn_w_out"] = _wgrad(dyab, attnb, D, AW, "dw_attn_out")
    dab, dkern, dvec = _conv_bwd(dz3, z1, ab, kern, conv_ln_gain, conv_ln_bias, "conv_bwd")
    grads["conv_dw_kernel"] = dkern.reshape(32, NDEV, D // NDEV).transpose(1, 0, 2).reshape(NDEV * 32, D // NDEV)
    rs.append(_send_start(["scatter"] * 4, [grads[n] for n in rs_groups[1]], [rs[0][4]], "scatter_start1"))
    dattnb = [_tie(a, [rs[1][4]], f"tie_after_scatter1_{i}") for i, a in enumerate(dattnb)]

    dqkv = []
    for gi, (_, dil) in enumerate(GROUPS):
        dq3 = _attn_bwd(qkv[gi], dattnb[gi], lse[gi], delta[gi], gi, f"attn_bwd{gi}")
        dqkv.append(dq3.reshape(3 * T, AW))

    dwin = _mm(dab, h2, mode="tn", m=2 * D, n=D, k=T, tm=2 * D, tn=D, tk=512, out_dtype=BF16, out_rows=IN_W,
               name="dw_in_conv")
    dwin = _mm(dgates, h2, mode="tn", m=2 * D, n=D, k=T, tm=512, tn=D, tk=1024, out_dtype=BF16, out_rows=IN_W,
               o_map=lambda i, j, kk: (13 + i, 0), passthru=dwin, name="dw_in_gates")
    for gi in range(3):
        dwin = _mm(dqkv[gi], h2p[gi], mode="tn", m=3 * AW, n=D, k=T, tm=AW, tn=D, tk=1024, out_dtype=BF16,
                   out_rows=IN_W, a_map=lambda i, j, kk: (i * (T // 1024) + kk, 0),
                   o_map=lambda i, j, kk, gi=gi: (4 + gi + 3 * i, 0), passthru=dwin, name=f"dw_in_qkv{gi}")
    grads["w_in"] = dwin
    rs.append(_send_start(["scatter"], [dwin], [rs[1][4]], "scatter_start2"))
    dab = _tie(dab, [rs[2][4]], "tie_after_scatter2")

    nrow = T // 1024
    dh = _mm(dab, win_t, mode="nn", m=T, n=D, k=2 * D, tm=1024, tn=D, tk=512, out_dtype=F32, name="dproj_conv")
    dh = _mm(dgates, win_t, mode="nn", m=T, n=D, k=2 * D, tm=1024, tn=D, tk=512, out_dtype=F32,
             b_map=lambda i, j, kk: (13 + kk, 0), init=dh, name="dproj_gates")
    dhs = []
    for gi, (_, dil) in enumerate(GROUPS):
        part = _mm(dqkv[gi], win_t, mode="nn", m=T, n=D, k=3 * AW, tm=1024, tn=D, tk=AW, out_dtype=F32,
                   a_map=lambda i, j, kk: (kk * nrow + i, 0), b_map=lambda i, j, kk, gi=gi: (4 + gi + 3 * kk, 0),
                   init=dh if gi == 0 else None, name=f"dproj_qkv{gi}")
        dhs.append(part)
    dx1, dg2 = _rms_bwd(x1, mix_norm, dhs, dx2, "mix_norm_bwd")

    dx0, dg1, dgb, dub, actb, hb, dob = _ffn_bwd(x0, ffn1_norm, gg1, uu1, dx1, wg1, wu1, wd1, "ffn1_bwd")
    grads["ffn1_w_gate"] = _wgrad(dgb, hb, FF, D, "ffn1_dwg")
    rs.append(_send_start(["scatter"], [grads["ffn1_w_gate"]], [rs[2][4]], "scatter_start3"))
    hb = _tie(hb, [rs[3][4]], "tie_after_scatter3")
    grads["ffn1_w_up"] = _wgrad(dub, hb, FF, D, "ffn1_dwu")
    rs.append(_send_start(["scatter"], [grads["ffn1_w_up"]], [rs[3][4]], "scatter_start4"))
    dob = _tie(dob, [rs[4][4]], "tie_after_scatter4")
    grads["ffn1_w_down"] = _wgrad(actb, dob, FF, D, "ffn1_dwd")
    vec = jnp.concatenate([dg1, dg2, dg3, dgf, dvec[0:3], jnp.broadcast_to(loss_part[:, :1], (1, D))], axis=0)
    rs.append(_send_start(["scatter", "bcast"], [grads["ffn1_w_down"], vec], [rs[4][4]], "scatter_start5"))

    g_out, d_out, m_out, v_out = {}, {}, {}, {}
    me = _my_place()
    after = [rs[last][4]]
    for gi, grp in enumerate(rs_groups):
        kinds = ["scatter"] * len(grp) + (["bcast"] if gi == last else [])
        srcs, lands = _send_wait(kinds, rs[gi], after, f"scatter_wait{gi}")
        for n, src, land in zip(grp, srcs, lands):
            if n == "conv_dw_kernel":
                rows = src.shape[0] // NDEV
                own = lax.dynamic_slice(src, (me * rows, 0), (rows, src.shape[1]))
                g = _gsum(own, land, f"gsum_{n}")[:CONV_W]
                d, m2, v2 = _adamw(w[n][0], g, mo[n][0], vo[n][0], f"adamw_{n}")
                g, d, m2, v2 = g[None], d[None], m2[None], v2[None]
            else:
                g, d, m2, v2 = _update(src, land, w[n], mo[n], vo[n], n in col_sharded, f"update_{n}")
            g_out[n], d_out[n], m_out[n], v_out[n] = g, d, m2, v2
            after = [d]
    vland = lands[-1]

    def rows8(src):
        return jnp.concatenate([src[n].reshape(1, D) for n in small] + [jnp.ones((1, D), F32)], axis=0)

    g8, d8, m8, v8 = _small_update(vland, rows8(w), rows8(mo), rows8(vo), "small_update")
    for r, n in enumerate(small):
        shp = w[n].shape
        g_out[n], d_out[n], m_out[n], v_out[n] = (a[r].reshape(shp) for a in (g8, d8, m8, v8))
    loss = g8[7, 0]

    return (loss, dx0[None], *[g_out[n] for n in names], *[d_out[n] for n in names],
            *[m_out[n] for n in names], *[v_out[n] for n in names])
```

```python
import numpy as np
import jax
import jax.numpy as jnp
from jax import lax
from jax.experimental import pallas as pl
from jax.experimental.pallas import tpu as pltpu

F32 = jnp.float32
BF16 = jnp.bfloat16

T = 4096
D = 1024
FF = 2816
NDEV = 8
CONV_W = 31
HEAD = 128
BLK = 128
GROUPS = ((128, 1), (512, 4), (2048, 16))
NHG = 4
AW = NHG * HEAD
IN_W = 2 * D + 3 * 3 * AW + 2 * D
EPS = 1e-6
B1, B2, LR, AEPS, WD, STEP = 0.9, 0.999, 0.001, 1e-08, 0.01, 10
NEG = -1e30
VMEM_LIMIT = 56 * 1024 * 1024
MESH_ID = pl.DeviceIdType.MESH

NT = (((1,), (1,)), ((), ()))
NN = (((1,), (0,)), ((), ()))
TN = (((0,), (0,)), ((), ()))
_DIMS = {"nn": NN, "nt": NT, "tn": TN}


def _cp(sem=None):
    return pltpu.CompilerParams(dimension_semantics=sem, vmem_limit_bytes=VMEM_LIMIT)


def _sig(v):
    return 1.0 / (1.0 + jnp.exp(-v))


def _dot(a, b, dims):
    return lax.dot_general(a, b, dims, preferred_element_type=F32)


def _const_spec(shape):
    nd = len(shape)
    return pl.BlockSpec(shape, lambda *_: (0,) * nd)


def _mm(a, b, *, mode, m, n, k, tm, tn, tk, out_dtype, name, a_map=None, b_map=None,
        o_map=None, out_rows=None, init=None, passthru=None):
    gi, gj, gk = m // tm, n // tn, k // tk
    assert gi * tm == m and gj * tn == n and gk * tk == k, (name, m, n, k, tm, tn, tk)
    if mode == "nn":
        a_blk, b_blk = (tm, tk), (tk, tn)
        da, db = (lambda i, j, kk: (i, kk)), (lambda i, j, kk: (kk, j))
    elif mode == "nt":
        a_blk, b_blk = (tm, tk), (tn, tk)
        da, db = (lambda i, j, kk: (i, kk)), (lambda i, j, kk: (j, kk))
    else:
        a_blk, b_blk = (tk, tm), (tk, tn)
        da, db = (lambda i, j, kk: (kk, i)), (lambda i, j, kk: (kk, j))
    a_map = a_map or da
    b_map = b_map or db
    o_map = o_map or (lambda i, j, kk: (i, j))
    dims = _DIMS[mode]
    extra = init if init is not None else passthru
    out_rows = out_rows or m

    def body(*refs):
        if init is not None:
            a_ref, b_ref, i_ref, o_ref = refs[:4]
        elif passthru is not None:
            a_ref, b_ref, _, o_ref = refs[:4]
        else:
            a_ref, b_ref, o_ref = refs[:3]
        if gk == 1:
            prod = _dot(a_ref[...], b_ref[...], dims)
            if init is not None:
                prod = prod + i_ref[...].astype(F32)
            o_ref[...] = prod.astype(out_dtype)
            return
        acc = refs[-1]
        kk = pl.program_id(2)

        @pl.when(kk == 0)
        def _():
            if init is not None:
                acc[...] = i_ref[...].astype(F32)
            else:
                acc[...] = jnp.zeros_like(acc)

        acc[...] += _dot(a_ref[...], b_ref[...], dims)

        @pl.when(kk == gk - 1)
        def _():
            o_ref[...] = acc[...].astype(out_dtype)

    in_specs = [pl.BlockSpec(a_blk, a_map), pl.BlockSpec(b_blk, b_map)]
    args = [a, b]
    aliases = {}
    if init is not None:
        in_specs.append(pl.BlockSpec((tm, tn), o_map))
        args.append(init)
        aliases = {2: 0}
    elif passthru is not None:
        in_specs.append(pl.BlockSpec(memory_space=pl.ANY))
        args.append(passthru)
        aliases = {2: 0}
    out_dt = extra.dtype if extra is not None else out_dtype
    assert out_dt == out_dtype
    return pl.pallas_call(
        body, name=name, grid=(gi, gj, gk),
        in_specs=in_specs, out_specs=pl.BlockSpec((tm, tn), o_map),
        out_shape=jax.ShapeDtypeStruct((out_rows, n), out_dtype),
        scratch_shapes=[pltpu.VMEM((tm, tn), F32)] if gk > 1 else [],
        input_output_aliases=aliases,
        compiler_params=_cp(("parallel", "parallel", "arbitrary")),
    )(*args)


def _ffn_fwd(x, g, wg_t, wu_t, wd, name):
    tm, fc = 512, 256
    nc = FF // fc

    def body(x_ref, g_ref, wg_ref, wu_ref, wd_ref, xo_ref, gg_ref, uu_ref, act_ref):
        xv = x_ref[...]
        r = lax.rsqrt(jnp.mean(xv * xv, axis=-1, keepdims=True) + EPS)
        h = (xv * r * g_ref[...]).astype(BF16)
        for c in range(nc):
            sl = pl.ds(c * fc, fc)
            gg = _dot(h, wg_ref[sl, :], NT)
            uu = _dot(h, wu_ref[sl, :], NT)
            gg_ref[:, sl] = gg.astype(BF16)
            uu_ref[:, sl] = uu.astype(BF16)
            act_ref[:, sl] = (gg * _sig(gg) * uu).astype(BF16)
        xo_ref[...] = xv + 0.5 * _dot(act_ref[...], wd_ref[...], NN)

    wspec = pl.BlockSpec((FF, D), lambda i: (0, 0), pipeline_mode=pl.Buffered(1))
    return pl.pallas_call(
        body, name=name, grid=(T // tm,),
        in_specs=[pl.BlockSpec((tm, D), lambda i: (i, 0)), _const_spec((1, D)), wspec, wspec, wspec],
        out_specs=[pl.BlockSpec((tm, D), lambda i: (i, 0)), pl.BlockSpec((tm, FF), lambda i: (i, 0)),
                   pl.BlockSpec((tm, FF), lambda i: (i, 0))],
        out_shape=[jax.ShapeDtypeStruct((T, D), F32), jax.ShapeDtypeStruct((T, FF), BF16),
                   jax.ShapeDtypeStruct((T, FF), BF16)],
        scratch_shapes=[pltpu.VMEM((tm, FF), BF16)],
        compiler_params=_cp(("parallel",)),
    )(x, g, wg_t, wu_t, wd)


def _ffn_bwd(x, g, gg_all, uu_all, dout, wg_t, wu_t, wd, name):
    tm, fc = 256, 256
    nc = FF // fc

    def body(x_ref, g_ref, gg_ref, uu_ref, do_ref, wg_ref, wu_ref, wd_ref,
             dx_ref, dgam_ref, dg_ref, du_ref, act_ref, h_ref, db_ref):
        i = pl.program_id(0)
        xv = x_ref[...]
        r = lax.rsqrt(jnp.mean(xv * xv, axis=-1, keepdims=True) + EPS)
        xhat = xv * r
        gam = g_ref[...]
        h_ref[...] = (xhat * gam).astype(BF16)
        dov = do_ref[...]
        dbv = (0.5 * dov).astype(BF16)
        db_ref[...] = dbv
        for c in range(nc):
            sl = pl.ds(c * fc, fc)
            da = _dot(dbv, wd_ref[sl, :], NT)
            gg = gg_ref[:, sl].astype(F32)
            uu = uu_ref[:, sl].astype(F32)
            s = _sig(gg)
            si = gg * s
            dgv = (da * uu * (s * (1.0 + gg * (1.0 - s)))).astype(BF16)
            duv = (da * si).astype(BF16)
            dg_ref[:, sl] = dgv
            du_ref[:, sl] = duv
            act_ref[:, sl] = (si * uu).astype(BF16)
        dh = _dot(dg_ref[...], wg_ref[...], NN) + _dot(du_ref[...], wu_ref[...], NN)

        @pl.when(i == 0)
        def _():
            dgam_ref[...] = jnp.zeros_like(dgam_ref)

        dgam_ref[...] += jnp.sum(dh * xhat, axis=0, keepdims=True)
        dxh = dh * gam
        dx_ref[...] = dov + r * (dxh - xhat * jnp.mean(dxh * xhat, axis=-1, keepdims=True))

    wspec = pl.BlockSpec((FF, D), lambda i: (0, 0), pipeline_mode=pl.Buffered(1))
    row_d = pl.BlockSpec((tm, D), lambda i: (i, 0))
    row_f = pl.BlockSpec((tm, FF), lambda i: (i, 0))
    return pl.pallas_call(
        body, name=name, grid=(T // tm,),
        in_specs=[row_d, _const_spec((1, D)), row_f, row_f, row_d, wspec, wspec, wspec],
        out_specs=[row_d, _const_spec((1, D)), row_f, row_f, row_f, row_d, row_d],
        out_shape=[jax.ShapeDtypeStruct((T, D), F32), jax.ShapeDtypeStruct((1, D), F32),
                   jax.ShapeDtypeStruct((T, FF), BF16), jax.ShapeDtypeStruct((T, FF), BF16),
                   jax.ShapeDtypeStruct((T, FF), BF16), jax.ShapeDtypeStruct((T, D), BF16),
                   jax.ShapeDtypeStruct((T, D), BF16)],
        compiler_params=_cp(("arbitrary",)),
    )(x, g, gg_all, uu_all, dout, wg_t, wu_t, wd)


def _wgrad(a, b, m, n, name):
    tm = m // 2 if m == FF else m
    return _mm(a, b, mode="tn", m=m, n=n, k=T, tm=tm, tn=n, tk=min(T, 2048), out_dtype=BF16, name=name)


PERM_TM = 512
DILS = tuple(d for _, d in GROUPS if d > 1)


def _perm_spec(dil, cols):
    return pl.BlockSpec((dil, PERM_TM // dil, cols), lambda i: (0, i, 0))


def _perm_shape(dil, cols, dtype):
    return jax.ShapeDtypeStruct((dil, T // dil, cols), dtype)


LANES = 128


def _tile_scratch(cols):
    return pltpu.VMEM((cols // LANES, PERM_TM, LANES), F32)


def _put_tile(tile, value):
    for c in range(tile.shape[0]):
        tile[c] = value[:, c * LANES:(c + 1) * LANES]


def _get_tile(tile):
    return jnp.concatenate([tile[c] for c in range(tile.shape[0])], axis=1)


def _store_perm(out_ref, tile, dil):
    for r in range(dil):
        for c in range(tile.shape[0]):
            out_ref[r, :, pl.ds(c * LANES, LANES)] = tile[c, pl.ds(r, PERM_TM // dil, stride=dil), :].astype(
                out_ref.dtype)


def _load_unperm(in_ref, tile, dil):
    for r in range(dil):
        for c in range(tile.shape[0]):
            tile[c, pl.ds(r, PERM_TM // dil, stride=dil), :] = in_ref[r, :, pl.ds(c * LANES, LANES)].astype(F32)


def _norm_cast(x, g, name):
    tm = PERM_TM

    def body(x_ref, g_ref, h_ref, *rest):
        p_refs, tile = rest[:-1], rest[-1]
        xv = x_ref[...]
        r = lax.rsqrt(jnp.mean(xv * xv, axis=-1, keepdims=True) + EPS)
        hv = xv * r * g_ref[...]
        h_ref[...] = hv.astype(BF16)
        _put_tile(tile, hv)
        for dil, p_ref in zip(DILS, p_refs):
            _store_perm(p_ref, tile, dil)

    out = pl.pallas_call(
        body, name=name, grid=(T // tm,),
        in_specs=[pl.BlockSpec((tm, D), lambda i: (i, 0)), _const_spec((1, D))],
        out_specs=[pl.BlockSpec((tm, D), lambda i: (i, 0))] + [_perm_spec(d, D) for d in DILS],
        out_shape=[jax.ShapeDtypeStruct((T, D), BF16)] + [_perm_shape(d, D, BF16) for d in DILS],
        scratch_shapes=[_tile_scratch(D)],
        compiler_params=_cp(("parallel",)),
    )(x, g)
    return [out[0]] + [o.reshape(T, D) for o in out[1:]]


def _final(x3, gf, tgt, name):
    tm = 512

    def body(x_ref, g_ref, t_ref, dx_ref, dgam_ref, loss_ref):
        i = pl.program_id(0)
        xv = x_ref[...]
        r = lax.rsqrt(jnp.mean(xv * xv, axis=-1, keepdims=True) + EPS)
        xhat = xv * r
        gam = g_ref[...]
        err = xhat * gam - t_ref[...]
        part = 0.5 * jnp.sum(jnp.mean(err * err, axis=-1, keepdims=True), axis=0, keepdims=True)
        dy = err * (1.0 / D)

        @pl.when(i == 0)
        def _():
            dgam_ref[...] = jnp.zeros_like(dgam_ref)
            loss_ref[...] = jnp.zeros_like(loss_ref)

        dgam_ref[...] += jnp.sum(dy * xhat, axis=0, keepdims=True)
        loss_ref[...] += jnp.broadcast_to(part, loss_ref.shape)
        dxh = dy * gam
        dx_ref[...] = r * (dxh - xhat * jnp.mean(dxh * xhat, axis=-1, keepdims=True))

    row_d = pl.BlockSpec((tm, D), lambda i: (i, 0))
    return pl.pallas_call(
        body, name=name, grid=(T // tm,),
        in_specs=[row_d, _const_spec((1, D)), row_d],
        out_specs=[row_d, _const_spec((1, D)), _const_spec((1, 128))],
        out_shape=[jax.ShapeDtypeStruct((T, D), F32), jax.ShapeDtypeStruct((1, D), F32),
                   jax.ShapeDtypeStruct((1, 128), F32)],
        compiler_params=_cp(("arbitrary",)),
    )(x3, gf, tgt)


def _rms_bwd(x, g, dhs, dres, name):
    tm = PERM_TM
    dils = [d for _, d in GROUPS]
    nh = len(dhs)
    assert nh == len(dils)

    def body(*refs):
        x_ref, g_ref = refs[:2]
        dh_refs = refs[2:2 + nh]
        dr_ref, dx_ref, dgam_ref, tile = refs[2 + nh:]
        i = pl.program_id(0)
        xv = x_ref[...]
        r = lax.rsqrt(jnp.mean(xv * xv, axis=-1, keepdims=True) + EPS)
        xhat = xv * r
        gam = g_ref[...]
        dh = None
        for dil, ref in zip(dils, dh_refs):
            if dil == 1:
                part = ref[...]
            else:
                _load_unperm(ref, tile, dil)
                part = _get_tile(tile)
            dh = part if dh is None else dh + part

        @pl.when(i == 0)
        def _():
            dgam_ref[...] = jnp.zeros_like(dgam_ref)

        dgam_ref[...] += jnp.sum(dh * xhat, axis=0, keepdims=True)
        dxh = dh * gam
        dx_ref[...] = dr_ref[...] + r * (dxh - xhat * jnp.mean(dxh * xhat, axis=-1, keepdims=True))

    row_d = pl.BlockSpec((tm, D), lambda i: (i, 0))
    dh_specs = [row_d if d == 1 else _perm_spec(d, D) for d in dils]
    dh_args = [a if d == 1 else a.reshape(d, T // d, D) for d, a in zip(dils, dhs)]
    return pl.pallas_call(
        body, name=name, grid=(T // tm,),
        in_specs=[row_d, _const_spec((1, D))] + dh_specs + [row_d],
        out_specs=[row_d, _const_spec((1, D))],
        out_shape=[jax.ShapeDtypeStruct((T, D), F32), jax.ShapeDtypeStruct((1, D), F32)],
        scratch_shapes=[_tile_scratch(D)],
        compiler_params=_cp(("arbitrary",)),
    )(x, g, *dh_args, dres)


CONV_TM = 256
CONV_HALO = 32
CONV_RB = 16


def _glu(ab):
    ab = ab.astype(F32)
    return ab[:, :D] * _sig(ab[:, D:])


def _ln_stats(z1):
    mu = jnp.mean(z1, axis=-1, keepdims=True)
    zc = z1 - mu
    rstd = lax.rsqrt(jnp.mean(zc * zc, axis=-1, keepdims=True) + EPS)
    return zc * rstd, rstd


def _fill_shifts(zs):
    n = zs.shape[1] - 8
    for s in range(1, 8):
        zs[s, pl.ds(0, n), :] = zs[0, pl.ds(s, n), :]


def _shifted(zs, start, rows):
    q, s = divmod(start, 8)
    return zs[s, pl.ds(8 * q, rows), :]


def _conv_fwd(ab, kern, dwb, lng, lnb, name):
    tm, hl, rb = CONV_TM, CONV_HALO, CONV_RB
    off = hl - (CONV_W - 1)

    def body(ab_ref, abh_ref, k_ref, dwb_ref, lng_ref, lnb_ref, z1_ref, z3_ref, zs):
        i = pl.program_id(0)
        zs[0, pl.ds(0, hl), :] = jnp.where(i > 0, _glu(abh_ref[...]), 0.0)
        zs[0, pl.ds(hl, tm), :] = _glu(ab_ref[...])
        _fill_shifts(zs)
        for b in range(tm // rb):
            acc = jnp.zeros((rb, D), F32)
            for j in range(CONV_W):
                acc = acc + _shifted(zs, b * rb + off + j, rb) * k_ref[pl.ds(j, 1), :]
            z1 = acc + dwb_ref[...]
            z1_ref[pl.ds(b * rb, rb), :] = z1
            zn, _ = _ln_stats(z1)
            z2 = zn * lng_ref[...] + lnb_ref[...]
            z3_ref[pl.ds(b * rb, rb), :] = (z2 * _sig(z2)).astype(BF16)

    row = pl.BlockSpec((tm, D), lambda i: (i, 0))
    return pl.pallas_call(
        body, name=name, grid=(T // tm,),
        in_specs=[pl.BlockSpec((tm, 2 * D), lambda i: (i, 0)),
                  pl.BlockSpec((hl, 2 * D), lambda i: (jnp.maximum(i * (tm // hl) - 1, 0), 0)),
                  _const_spec((32, D)), _const_spec((1, D)), _const_spec((1, D)), _const_spec((1, D))],
        out_specs=[row, row],
        out_shape=[jax.ShapeDtypeStruct((T, D), F32), jax.ShapeDtypeStruct((T, D), BF16)],
        scratch_shapes=[pltpu.VMEM((8, hl + tm, D), F32)],
        compiler_params=_cp(("parallel",)),
    )(ab, ab, kern, dwb, lng, lnb)


def _conv_bwd(dz3, z1, ab, kern, lng, lnb, name):
    tm, hl, rb = CONV_TM, CONV_HALO, CONV_RB
    off = hl - (CONV_W - 1)
    nsteps = T // tm

    def ln_bwd(dz3v, z1v, lngv, lnbv):
        zn, rstd = _ln_stats(z1v)
        z2 = zn * lngv + lnbv
        s = _sig(z2)
        dz2 = dz3v * (s * (1.0 + z2 * (1.0 - s)))
        dzn = dz2 * lngv
        dz1 = rstd * (dzn - jnp.mean(dzn, axis=-1, keepdims=True)
                      - zn * jnp.mean(dzn * zn, axis=-1, keepdims=True))
        return dz1, dz2, zn

    def body(dz3_ref, dz3h_ref, z1_ref, z1h_ref, ab_ref, abh_ref, k_ref, lng_ref, lnb_ref,
             dab_ref, dk_ref, dvec_ref, zs, dzs):
        i = pl.program_id(0)
        lngv, lnbv = lng_ref[...], lnb_ref[...]

        @pl.when(i == 0)
        def _():
            dk_ref[...] = jnp.zeros_like(dk_ref)
            dvec_ref[...] = jnp.zeros_like(dvec_ref)

        dz1, dz2, zn = ln_bwd(dz3_ref[...], z1_ref[...], lngv, lnbv)
        dvec_ref[pl.ds(0, 1), :] += jnp.sum(dz1, axis=0, keepdims=True)
        dvec_ref[pl.ds(1, 1), :] += jnp.sum(dz2 * zn, axis=0, keepdims=True)
        dvec_ref[pl.ds(2, 1), :] += jnp.sum(dz2, axis=0, keepdims=True)
        dzs[0, pl.ds(0, tm), :] = dz1
        dz1h, _, _ = ln_bwd(dz3h_ref[...], z1h_ref[...], lngv, lnbv)
        dzs[0, pl.ds(tm, hl), :] = jnp.where(i < nsteps - 1, dz1h, 0.0)
        _fill_shifts(dzs)
        zs[0, pl.ds(0, hl), :] = jnp.where(i > 0, _glu(abh_ref[...]), 0.0)
        zs[0, pl.ds(hl, tm), :] = _glu(ab_ref[...])
        _fill_shifts(zs)

        for j in range(CONV_W):
            tot = jnp.zeros((rb, D), F32)
            for b in range(tm // rb):
                tot = tot + dzs[0, pl.ds(b * rb, rb), :] * _shifted(zs, b * rb + off + j, rb)
            dk_ref[pl.ds(j, 1), :] += jnp.sum(tot, axis=0, keepdims=True)

        for b in range(tm // rb):
            acc = jnp.zeros((rb, D), F32)
            for j in range(CONV_W):
                acc = acc + _shifted(dzs, b * rb + (CONV_W - 1) - j, rb) * k_ref[pl.ds(j, 1), :]
            av = ab_ref[pl.ds(b * rb, rb), pl.ds(0, D)].astype(F32)
            sb = _sig(ab_ref[pl.ds(b * rb, rb), pl.ds(D, D)].astype(F32))
            dab_ref[pl.ds(b * rb, rb), pl.ds(0, D)] = (acc * sb).astype(BF16)
            dab_ref[pl.ds(b * rb, rb), pl.ds(D, D)] = (acc * av * sb * (1.0 - sb)).astype(BF16)

    row = pl.BlockSpec((tm, D), lambda i: (i, 0))
    nxt = pl.BlockSpec((hl, D), lambda i: (jnp.minimum((i + 1) * (tm // hl), T // hl - 1), 0))
    return pl.pallas_call(
        body, name=name, grid=(nsteps,),
        in_specs=[row, nxt, row, nxt,
                  pl.BlockSpec((tm, 2 * D), lambda i: (i, 0)),
                  pl.BlockSpec((hl, 2 * D), lambda i: (jnp.maximum(i * (tm // hl) - 1, 0), 0)),
                  _const_spec((32, D)), _const_spec((1, D)), _const_spec((1, D))],
        out_specs=[pl.BlockSpec((tm, 2 * D), lambda i: (i, 0)), _const_spec((32, D)), _const_spec((8, D))],
        out_shape=[jax.ShapeDtypeStruct((T, 2 * D), BF16), jax.ShapeDtypeStruct((32, D), F32),
                   jax.ShapeDtypeStruct((8, D), F32)],
        scratch_shapes=[pltpu.VMEM((8, hl + tm, D), F32), pltpu.VMEM((8, tm + hl, D), F32)],
        compiler_params=_cp(("arbitrary",)),
    )(dz3, dz3, z1, z1, ab, ab, kern, lng, lnb)


def _alibi_slopes():
    h = np.arange(1, 3 * NHG + 1, dtype=np.float32)
    return np.power(np.float32(2.0), -8.0 * h / np.float32(3 * NHG)).astype(np.float32)


def _band_bias(gi):
    _, dil = GROUPS[gi]
    slopes = _alibi_slopes()[gi * NHG:(gi + 1) * NHG]
    qi = np.arange(BLK)[:, None]
    ki = np.arange(2 * BLK)[None, :]
    steps = BLK + qi - ki
    band = (steps >= 0) & (steps <= BLK)
    bias = -slopes[:, None, None] * (dil * steps).astype(np.float32)[None]
    return jnp.asarray(np.where(band[None], bias, np.float32(NEG)).astype(np.float32))


QB_FWD = 4
QB_BWD = 8


def _attn_specs(qb):
    prev = lambda n: jnp.maximum(n * qb - 1, 0)
    return [pl.BlockSpec((qb * BLK, HEAD), lambda h, n: (n, h)),
            pl.BlockSpec((BLK, HEAD), lambda h, n: (prev(n), NHG + h)),
            pl.BlockSpec((qb * BLK, HEAD), lambda h, n: (n, NHG + h)),
            pl.BlockSpec((BLK, HEAD), lambda h, n: (prev(n), 2 * NHG + h)),
            pl.BlockSpec((qb * BLK, HEAD), lambda h, n: (n, 2 * NHG + h)),
            pl.BlockSpec((None, BLK, 2 * BLK), lambda h, n: (h, 0, 0))]


def _scores(q, kcat, bias, blk, seg):
    s = _dot(q, kcat, NT) * (HEAD ** -0.5) + bias
    col = lax.broadcasted_iota(jnp.int32, s.shape, 1)
    first = (blk % seg) == 0
    return jnp.where(jnp.logical_and(first, col < BLK), NEG, s)


def _attn_fwd(qkv, gi, name):
    seg = (T // GROUPS[gi][1]) // BLK

    qb = QB_FWD

    def body(q_ref, kp_ref, kc_ref, vp_ref, vc_ref, bias_ref, o_ref, l_ref):
        n = pl.program_id(0)
        for h in range(NHG):
            cols = pl.ds(h * HEAD, HEAD)
            kwin = jnp.concatenate([kp_ref[:, cols], kc_ref[:, cols]], axis=0)
            vwin = jnp.concatenate([vp_ref[:, cols], vc_ref[:, cols]], axis=0)
            bias = bias_ref[h]
            for b in range(qb):
                rows = pl.ds(b * BLK, BLK)
                s = _scores(q_ref[rows, cols], kwin[b * BLK:(b + 2) * BLK], bias, n * qb + b, seg)
                mx = jnp.max(s, axis=-1, keepdims=True)
                p = jnp.exp(s - mx)
                den = jnp.sum(p, axis=-1, keepdims=True)
                o_ref[rows, cols] = _dot(p.astype(BF16), vwin[b * BLK:(b + 2) * BLK], NN) / den
                l_ref[rows, cols] = jnp.broadcast_to(mx + jnp.log(den), (BLK, HEAD))

    prev = lambda n: jnp.maximum(n * qb - 1, 0)
    cur = lambda part: pl.BlockSpec((qb * BLK, AW), lambda n: (n, part))
    halo = lambda part: pl.BlockSpec((BLK, AW), lambda n: (prev(n), part))
    return pl.pallas_call(
        body, name=name, grid=(T // (qb * BLK),),
        in_specs=[cur(0), halo(1), cur(1), halo(2), cur(2), _const_spec((NHG, BLK, 2 * BLK))],
        out_specs=[cur(0), cur(0)],
        out_shape=[jax.ShapeDtypeStruct((T, AW), F32), jax.ShapeDtypeStruct((T, AW), F32)],
        compiler_params=_cp(("parallel",)),
    )(qkv, qkv, qkv, qkv, qkv, _band_bias(gi))


def _attn_bwd(qkv, dob, lse, delta, gi, name):
    seg = (T // GROUPS[gi][1]) // BLK
    qb = QB_BWD
    nb = T // (qb * BLK)
    scale = HEAD ** -0.5

    def body(q_ref, kp_ref, kc_ref, vp_ref, vc_ref, bias_ref, do_ref, l_ref, dl_ref, out_ref, dk_acc, dv_acc):
        n = pl.program_id(1)
        kwin = jnp.concatenate([kp_ref[...], kc_ref[...]], axis=0)
        vwin = jnp.concatenate([vp_ref[...], vc_ref[...]], axis=0)
        bias = bias_ref[...]
        for b in range(qb):
            rows = pl.ds(b * BLK, BLK)
            q = q_ref[rows, :]
            kcat = kwin[b * BLK:(b + 2) * BLK]
            s = _scores(q, kcat, bias, n * qb + b, seg)
            p = jnp.exp(s - l_ref[rows, pl.ds(0, 1)])
            dov = do_ref[rows, :]
            dv2 = _dot(p.astype(BF16), dov, TN)
            dp = _dot(dov, vwin[b * BLK:(b + 2) * BLK], NT)
            dsb = (p * (dp - dl_ref[rows, pl.ds(0, 1)]) * scale).astype(BF16)
            row = pl.ds(pl.multiple_of((n * qb + b) * BLK, BLK), BLK)
            out_ref[0, row, :] = _dot(dsb, kcat, NN).astype(BF16)
            dk2 = _dot(dsb, q, TN)
            dk_acc[row, :] = dk2[BLK:]
            dv_acc[row, :] = dv2[BLK:]

            def add_prev(dk2=dk2, dv2=dv2, b=b):
                prow = pl.ds(pl.multiple_of((n * qb + b - 1) * BLK, BLK), BLK)
                dk_acc[prow, :] += dk2[:BLK]
                dv_acc[prow, :] += dv2[:BLK]

            if b == 0:
                pl.when(n > 0)(add_prev)
            else:
                add_prev()

        @pl.when(n == nb - 1)
        def _():
            out_ref[1] = dk_acc[...].astype(BF16)
            out_ref[2] = dv_acc[...].astype(BF16)

    oblk = pl.BlockSpec((qb * BLK, HEAD), lambda h, n: (n, h))
    return pl.pallas_call(
        body, name=name, grid=(NHG, nb),
        in_specs=_attn_specs(qb) + [oblk, oblk, oblk],
        out_specs=pl.BlockSpec((3, T, HEAD), lambda h, n: (0, 0, h)),
        out_shape=jax.ShapeDtypeStruct((3, T, AW), BF16),
        scratch_shapes=[pltpu.VMEM((T, HEAD), F32), pltpu.VMEM((T, HEAD), F32)],
        compiler_params=_cp(("parallel", "arbitrary")),
    )(qkv, qkv, qkv, qkv, qkv, _band_bias(gi), dob, lse, delta)


def _merge(outs, lses, name):
    tm = PERM_TM
    dils = [d for _, d in GROUPS]
    ng = len(dils)

    def body(*refs):
        in_refs = refs[:2 * ng]
        a_ref, ab_ref = refs[2 * ng:2 * ng + 2]
        lse_refs = refs[2 * ng + 2:3 * ng + 2]
        tile = refs[-1]

        def token_order(ref, dil):
            if dil == 1:
                return ref[...]
            _load_unperm(ref, tile, dil)
            return _get_tile(tile)

        os = [token_order(in_refs[2 * i], d) for i, d in enumerate(dils)]
        ls = [token_order(in_refs[2 * i + 1], d) for i, d in enumerate(dils)]
        mx = jnp.maximum(jnp.maximum(ls[0], ls[1]), ls[2])
        es = [jnp.exp(v - mx) for v in ls]
        tot = es[0] + es[1] + es[2]
        att = (es[0] / tot) * os[0] + (es[1] / tot) * os[1] + (es[2] / tot) * os[2]
        a_ref[...] = att
        ab_ref[...] = att.astype(BF16)
        lse = mx + jnp.log(tot)
        _put_tile(tile, lse)
        for dil, ref in zip(dils, lse_refs):
            if dil == 1:
                ref[...] = lse
            else:
                _store_perm(ref, tile, dil)

    row = pl.BlockSpec((tm, AW), lambda i: (i, 0))
    specs = [row if d == 1 else _perm_spec(d, AW) for d in dils]
    args = []
    for d, o, l in zip(dils, outs, lses):
        args += [o, l] if d == 1 else [o.reshape(d, T // d, AW), l.reshape(d, T // d, AW)]
    out = pl.pallas_call(
        body, name=name, grid=(T // tm,),
        in_specs=[sp for sp in specs for _ in range(2)], out_specs=[row, row] + specs,
        out_shape=[jax.ShapeDtypeStruct((T, AW), F32), jax.ShapeDtypeStruct((T, AW), BF16)]
        + [jax.ShapeDtypeStruct((T, AW), F32) if d == 1 else _perm_shape(d, AW, F32) for d in dils],
        scratch_shapes=[_tile_scratch(AW)],
        compiler_params=_cp(("parallel",)),
    )(*args)
    return out[0], out[1], [o.reshape(T, AW) for o in out[2:]]


def _mix_out(z3b, attnb, gates, wc, wa_t, wo, x1, name):
    tm = 512

    def body(z_ref, a_ref, g_ref, wc_ref, wa_ref, wo_ref, x_ref, xo_ref, yc_ref, ya_ref, mx_ref):
        yc = _dot(z_ref[...], wc_ref[...], NN)
        ya = _dot(a_ref[...], wa_ref[...], NT)
        yc_ref[...] = yc
        ya_ref[...] = ya
        gv = g_ref[...].astype(F32)
        mixed = (_sig(gv[:, :D]) * yc + _sig(gv[:, D:]) * ya).astype(BF16)
        mx_ref[...] = mixed
        xo_ref[...] = x_ref[...] + _dot(mixed, wo_ref[...], NN)

    row = pl.BlockSpec((tm, D), lambda i: (i, 0))
    return pl.pallas_call(
        body, name=name, grid=(T // tm,),
        in_specs=[row, pl.BlockSpec((tm, AW), lambda i: (i, 0)), pl.BlockSpec((tm, 2 * D), lambda i: (i, 0)),
                  _const_spec((D, D)), _const_spec((D, AW)), _const_spec((D, D)), row],
        out_specs=[row, row, row, row],
        out_shape=[jax.ShapeDtypeStruct((T, D), F32), jax.ShapeDtypeStruct((T, D), F32),
                   jax.ShapeDtypeStruct((T, D), F32), jax.ShapeDtypeStruct((T, D), BF16)],
        compiler_params=_cp(("parallel",)),
    )(z3b, attnb, gates, wc, wa_t, wo, x1)


def _mix_out_bwd(dx2, gates, yc, ya, attn, wc, wa_t, wo, name):
    tm = PERM_TM
    dils = [d for _, d in GROUPS]
    ng = len(dils)

    def body(dx_ref, g_ref, yc_ref, ya_ref, at_ref, wc_ref, wa_ref, wo_ref,
             dg_ref, dyc_ref, dya_ref, dxb_ref, dz3_ref, *rest):
        dat_refs, dl_refs, tile = rest[:ng], rest[ng:2 * ng], rest[-1]
        dxb = dx_ref[...].astype(BF16)
        dxb_ref[...] = dxb
        dmix = _dot(dxb, wo_ref[...], NT)
        gv = g_ref[...].astype(F32)
        sc = _sig(gv[:, :D])
        sa = _sig(gv[:, D:])
        ycv, yav = yc_ref[...], ya_ref[...]
        dg_ref[:, pl.ds(0, D)] = (dmix * ycv * sc * (1.0 - sc)).astype(BF16)
        dg_ref[:, pl.ds(D, D)] = (dmix * yav * sa * (1.0 - sa)).astype(BF16)
        dyc = (dmix * sc).astype(BF16)
        dya = (dmix * sa).astype(BF16)
        dyc_ref[...] = dyc
        dya_ref[...] = dya
        dz3_ref[...] = _dot(dyc, wc_ref[...], NT)
        dat = _dot(dya, wa_ref[...], NN)
        prod = dat * at_ref[...]
        delta = jnp.concatenate(
            [jnp.broadcast_to(jnp.sum(prod[:, h * HEAD:(h + 1) * HEAD], axis=-1, keepdims=True), (tm, HEAD))
             for h in range(NHG)], axis=1)
        for value, out_refs in ((dat, dat_refs), (delta, dl_refs)):
            _put_tile(tile, value)
            for dil, ref in zip(dils, out_refs):
                if dil == 1:
                    ref[...] = value.astype(ref.dtype)
                else:
                    _store_perm(ref, tile, dil)

    row = pl.BlockSpec((tm, D), lambda i: (i, 0))
    row2 = pl.BlockSpec((tm, 2 * D), lambda i: (i, 0))
    rowa = pl.BlockSpec((tm, AW), lambda i: (i, 0))
    aspecs = [rowa if d == 1 else _perm_spec(d, AW) for d in dils]

    def ashapes(dtype):
        return [jax.ShapeDtypeStruct((T, AW), dtype) if d == 1 else _perm_shape(d, AW, dtype) for d in dils]

    out = pl.pallas_call(
        body, name=name, grid=(T // tm,),
        in_specs=[row, row2, row, row, rowa, _const_spec((D, D)), _const_spec((D, AW)), _const_spec((D, D))],
        out_specs=[row2, row, row, row, row] + aspecs + aspecs,
        out_shape=[jax.ShapeDtypeStruct((T, 2 * D), BF16), jax.ShapeDtypeStruct((T, D), BF16),
                   jax.ShapeDtypeStruct((T, D), BF16), jax.ShapeDtypeStruct((T, D), BF16),
                   jax.ShapeDtypeStruct((T, D), F32)] + ashapes(BF16) + ashapes(F32),
        scratch_shapes=[_tile_scratch(AW)],
        compiler_params=_cp(("parallel",)),
    )(dx2, gates, yc, ya, attn, wc, wa_t, wo)
    dats = [o.reshape(T, AW) for o in out[5:5 + ng]]
    deltas = [o.reshape(T, AW) for o in out[5 + ng:5 + 2 * ng]]
    return out[0], out[1], out[2], out[3], out[4], dats, deltas


def _peer(k):
    x, y, c = lax.axis_index("x"), lax.axis_index("y"), lax.axis_index("c")
    px = 1 - x if k & 4 else x
    py = 1 - y if k & 2 else y
    pc = 1 - c if k & 1 else c
    return (px, py, pc), 4 * px + 2 * py + pc


HBM_SPEC = pl.BlockSpec(memory_space=pltpu.HBM)
SEM_SPEC = pl.BlockSpec(memory_space=pltpu.SEMAPHORE)
EFFECT = pltpu.SideEffectType.DATAFLOW_SIDE_EFFECTING


def _my_place():
    return 4 * lax.axis_index("x") + 2 * lax.axis_index("y") + lax.axis_index("c")


def _tie(a, order_after, name):
    na = len(order_after)

    def body(*refs):
        del refs

    return pl.pallas_call(
        body, name=name, in_specs=[pl.BlockSpec(memory_space=pl.ANY)] * (1 + na),
        out_specs=pl.BlockSpec(memory_space=pl.ANY), out_shape=jax.ShapeDtypeStruct(a.shape, a.dtype),
        input_output_aliases={0: 0},
    )(a, *order_after)


def _prep_gather(ws, order_after, name):
    me = jnp.reshape(_my_place(), (1,)).astype(jnp.int32)
    n = len(ws)
    na = len(order_after)
    shapes = [((32, wv.shape[1]), F32) if wv.shape[0] == CONV_W else (wv.shape, BF16) for wv in ws]

    def body(me_ref, *refs):
        del me_ref
        ins, outs = refs[:n], refs[n + na:]
        for wv, i_ref, o_ref in zip(ws, ins, outs):
            if wv.shape[0] == CONV_W:
                o_ref[pl.ds(0, CONV_W), :] = i_ref[...]
                o_ref[pl.ds(CONV_W, 1), :] = jnp.zeros((1, wv.shape[1]), F32)
            else:
                o_ref[...] = i_ref[...].astype(BF16)

    grid_spec = pltpu.PrefetchScalarGridSpec(
        num_scalar_prefetch=1, grid=(1,),
        in_specs=[pl.BlockSpec(wv.shape, lambda i, m: (0, 0)) for wv in ws]
        + [pl.BlockSpec(memory_space=pl.ANY)] * na,
        out_specs=[pl.BlockSpec(shp, lambda i, m: (m[0], 0)) for shp, _ in shapes])
    return pl.pallas_call(
        body, name=name, grid_spec=grid_spec,
        out_shape=[jax.ShapeDtypeStruct((NDEV * shp[0], shp[1]), dt) for shp, dt in shapes],
        compiler_params=_cp(("arbitrary",)),
    )(me, *ws, *order_after)


GATHER_A = ((1, 0), (2, 0), (4, 0), (6, 0))
GATHER_B = ((1, 2), (1, 4), (1, 6))


def _gather_start(lands, plan, order_after, name):
    n = len(lands)
    na = len(order_after)
    npl = len(plan)

    def body(*refs):
        land_refs = refs[:n]
        send, recv = refs[n + na], refs[n + na + 1]
        token = refs[-1]
        for w in range(n):
            rows = lands[w].shape[0] // NDEV
            for p, (k, j) in enumerate(plan):
                peer, _ = _peer(k)
                _, blk = _peer(j)
                part = land_refs[w].at[pl.ds(blk * rows, rows)]
                i = w * npl + p
                pltpu.make_async_remote_copy(src_ref=part, dst_ref=part, send_sem=send.at[i], recv_sem=recv.at[i],
                                             device_id=peer, device_id_type=MESH_ID).start()
        token[...] = jnp.zeros_like(token)

    nsem = n * npl
    bufs = [pltpu.with_memory_space_constraint(a, pltpu.HBM) for a in lands]
    out = pl.pallas_call(
        body, name=name,
        in_specs=[HBM_SPEC] * n + [pl.BlockSpec(memory_space=pl.ANY)] * na,
        out_specs=[SEM_SPEC, SEM_SPEC] + [HBM_SPEC] * n + [pl.BlockSpec(memory_space=pltpu.VMEM)],
        out_shape=[pltpu.SemaphoreType.DMA((nsem,)), pltpu.SemaphoreType.DMA((nsem,))]
        + [pltpu.HBM(a.shape, a.dtype) for a in bufs] + [jax.ShapeDtypeStruct((8, 128), F32)],
        input_output_aliases={i: 2 + i for i in range(n)},
        compiler_params=pltpu.CompilerParams(has_side_effects=EFFECT),
    )(*bufs, *order_after)
    return out[0], out[1], out[2:2 + n], out[-1]


def _gather_wait(started, plan, order_after, name):
    send, recv, lands, _ = started
    n = len(lands)
    na = len(order_after)
    npl = len(plan)

    def body(*refs):
        land_refs = refs[:n]
        send_ref, recv_ref = refs[n], refs[n + 1]
        for w in range(n):
            rows = lands[w].shape[0] // NDEV
            for p, (k, j) in enumerate(plan):
                peer, _ = _peer(k)
                _, blk = _peer(j)
                part = land_refs[w].at[pl.ds(blk * rows, rows)]
                i = w * npl + p
                cp = pltpu.make_async_remote_copy(src_ref=part, dst_ref=part, send_sem=send_ref.at[i],
                                                  recv_sem=recv_ref.at[i], device_id=peer, device_id_type=MESH_ID)
                cp.wait_send()
                cp.wait_recv()

    out = pl.pallas_call(
        body, name=name,
        in_specs=[HBM_SPEC] * n + [SEM_SPEC, SEM_SPEC] + [pl.BlockSpec(memory_space=pl.ANY)] * na,
        out_specs=[HBM_SPEC] * n,
        out_shape=[pltpu.HBM(a.shape, a.dtype) for a in lands],
        input_output_aliases={i: i for i in range(n)},
        compiler_params=pltpu.CompilerParams(has_side_effects=EFFECT),
    )(*lands, send, recv, *order_after)
    return list(out)


def _copy_ends(kind, src, land, me, plin, k):
    if kind == "scatter":
        rows = src.shape[0] // NDEV
        return src.at[pl.ds(plin * rows, rows)], land.at[k - 1]
    return src, land.at[me]


def _landing(kind, src):
    me = _my_place()
    if kind == "scatter":
        return lax.empty((NDEV - 1, src.shape[0] // NDEV) + src.shape[1:], src.dtype)
    land = lax.empty((NDEV,) + src.shape, src.dtype)
    return lax.dynamic_update_slice(land, src[None], (me,) + (0,) * src.ndim)


def _send_start(kinds, srcs, order_after, name):
    n = len(srcs)
    lands = [_landing(kd, s) for kd, s in zip(kinds, srcs)]
    na = len(order_after)

    def body(*refs):
        src_refs, land_refs = refs[:n], refs[n:2 * n]
        send, recv = refs[2 * n + na], refs[2 * n + na + 1]
        token = refs[-1]
        _, me = _peer(0)
        for w in range(n):
            for k in range(1, NDEV):
                peer, plin = _peer(k)
                s, d = _copy_ends(kinds[w], src_refs[w], land_refs[w], me, plin, k)
                i = w * (NDEV - 1) + k - 1
                pltpu.make_async_remote_copy(src_ref=s, dst_ref=d, send_sem=send.at[i], recv_sem=recv.at[i],
                                             device_id=peer, device_id_type=MESH_ID).start()
        token[...] = jnp.zeros_like(token)

    nsem = n * (NDEV - 1)
    bufs = [pltpu.with_memory_space_constraint(a, pltpu.HBM) for a in list(srcs) + lands]
    out = pl.pallas_call(
        body, name=name,
        in_specs=[HBM_SPEC] * (2 * n) + [pl.BlockSpec(memory_space=pl.ANY)] * na,
        out_specs=[SEM_SPEC, SEM_SPEC] + [HBM_SPEC] * (2 * n) + [pl.BlockSpec(memory_space=pltpu.VMEM)],
        out_shape=[pltpu.SemaphoreType.DMA((nsem,)), pltpu.SemaphoreType.DMA((nsem,))]
        + [pltpu.HBM(a.shape, a.dtype) for a in bufs] + [jax.ShapeDtypeStruct((8, 128), F32)],
        input_output_aliases={i: 2 + i for i in range(2 * n)},
        compiler_params=pltpu.CompilerParams(has_side_effects=EFFECT),
    )(*bufs, *order_after)
    return out[0], out[1], out[2:2 + n], out[2 + n:2 + 2 * n], out[-1]


def _send_wait(kinds, started, order_after, name):
    send, recv, srcs, lands, _ = started
    n = len(srcs)
    na = len(order_after)

    def body(*refs):
        src_refs, land_refs = refs[:n], refs[n:2 * n]
        send_ref, recv_ref = refs[2 * n], refs[2 * n + 1]
        _, me = _peer(0)
        for w in range(n):
            for k in range(1, NDEV):
                peer, plin = _peer(k)
                s, d = _copy_ends(kinds[w], src_refs[w], land_refs[w], me, plin, k)
                i = w * (NDEV - 1) + k - 1
                cp = pltpu.make_async_remote_copy(src_ref=s, dst_ref=d, send_sem=send_ref.at[i],
                                                  recv_sem=recv_ref.at[i], device_id=peer, device_id_type=MESH_ID)
                cp.wait_send()
                cp.wait_recv()

    bufs = list(srcs) + list(lands)
    out = pl.pallas_call(
        body, name=name,
        in_specs=[HBM_SPEC] * (2 * n) + [SEM_SPEC, SEM_SPEC] + [pl.BlockSpec(memory_space=pl.ANY)] * na,
        out_specs=[HBM_SPEC] * (2 * n),
        out_shape=[pltpu.HBM(a.shape, a.dtype) for a in bufs],
        input_output_aliases={i: i for i in range(2 * n)},
        compiler_params=pltpu.CompilerParams(has_side_effects=EFFECT),
    )(*bufs, send, recv, *order_after)
    return out[:n], out[n:]


def _gsum(own, land, name):
    rows, cols = own.shape
    tr = rows // 2 if rows * cols > 512 * 1024 and rows % 32 == 0 else rows

    def body(own_ref, l_ref, o_ref):
        tot = own_ref[...].astype(F32)
        for s in range(NDEV - 1):
            tot = tot + l_ref[s].astype(F32)
        o_ref[...] = tot

    return pl.pallas_call(
        body, name=name, grid=(rows // tr,),
        in_specs=[pl.BlockSpec((tr, cols), lambda i: (i, 0)),
                  pl.BlockSpec((NDEV - 1, tr, cols), lambda i: (0, i, 0))],
        out_specs=pl.BlockSpec((tr, cols), lambda i: (i, 0)),
        out_shape=jax.ShapeDtypeStruct((rows, cols), F32),
        compiler_params=_cp(("parallel",)),
    )(own, land)


def _adamw_math(w, g, m, v):
    m2 = B1 * m + (1.0 - B1) * g
    v2 = B2 * v + (1.0 - B2) * (g * g)
    m_hat = m2 / (1.0 - B1 ** STEP)
    v_hat = v2 / (1.0 - B2 ** STEP)
    delta = -LR * (m_hat / (jnp.sqrt(v_hat) + AEPS) + WD * w)
    return delta, m2, v2


def _adamw(w, g, m, v, name):
    rows, cols = w.shape
    tr = 256 if rows % 256 == 0 and rows > 256 else rows

    def body(w_ref, g_ref, m_ref, v_ref, d_ref, mo_ref, vo_ref):
        d, m2, v2 = _adamw_math(w_ref[...], g_ref[...], m_ref[...], v_ref[...])
        d_ref[...] = d
        mo_ref[...] = m2
        vo_ref[...] = v2

    blk = pl.BlockSpec((tr, cols), lambda i: (i, 0))
    return pl.pallas_call(
        body, name=name, grid=(rows // tr,), in_specs=[blk] * 4, out_specs=[blk] * 3,
        out_shape=[jax.ShapeDtypeStruct((rows, cols), F32)] * 3,
        compiler_params=_cp(("parallel",)),
    )(w, g, m, v)


UPD_TC = 256


def _update(src, land, w, m, v, name):
    rows, cols = land.shape[1:]
    tc = min(UPD_TC, cols)
    me = jnp.reshape(_my_place(), (1,)).astype(jnp.int32)

    def body(me_ref, own_ref, l_ref, w_ref, m_ref, v_ref, g_ref, d_ref, mo_ref, vo_ref):
        del me_ref
        g = own_ref[...].astype(F32)
        for s in range(NDEV - 1):
            g = g + l_ref[s].astype(F32)
        g_ref[...] = g
        d, m2, v2 = _adamw_math(w_ref[...], g, m_ref[...], v_ref[...])
        d_ref[...] = d
        mo_ref[...] = m2
        vo_ref[...] = v2

    wblk = pl.BlockSpec((rows, tc), lambda j, p: (0, j))
    grid_spec = pltpu.PrefetchScalarGridSpec(
        num_scalar_prefetch=1, grid=(cols // tc,),
        in_specs=[pl.BlockSpec((rows, tc), lambda j, p: (p[0], j)),
                  pl.BlockSpec((NDEV - 1, rows, tc), lambda j, p: (0, 0, j)), wblk, wblk, wblk],
        out_specs=[wblk] * 4)
    return pl.pallas_call(
        body, name=name, grid_spec=grid_spec, out_shape=[jax.ShapeDtypeStruct((rows, cols), F32)] * 4,
        compiler_params=_cp(("parallel",)),
    )(me, src, land, w, m, v)


def _small_update(vland, w8, m8, v8, name):
    def body(l_ref, w_ref, m_ref, v_ref, g_ref, d_ref, mo_ref, vo_ref):
        g = l_ref[0]
        for s in range(1, NDEV):
            g = g + l_ref[s]
        g_ref[...] = g
        d, m2, v2 = _adamw_math(w_ref[...], g, m_ref[...], v_ref[...])
        d_ref[...] = d
        mo_ref[...] = m2
        vo_ref[...] = v2

    return pl.pallas_call(
        body, name=name, out_shape=[jax.ShapeDtypeStruct((8, D), F32)] * 4,
        compiler_params=_cp(None),
    )(vland, w8, m8, v8)


def kernel(x, ffn1_norm, ffn1_w_gate, ffn1_w_up, ffn1_w_down, mix_norm, w_in, conv_dw_kernel, conv_dw_bias, conv_ln_gain, conv_ln_bias, conv_w_out, attn_w_out, w_o, ffn2_norm, ffn2_w_gate, ffn2_w_up, ffn2_w_down, final_norm, loss_target, m_ffn1_norm, m_ffn1_w_gate, m_ffn1_w_up, m_ffn1_w_down, m_mix_norm, m_w_in, m_conv_dw_kernel, m_conv_dw_bias, m_conv_ln_gain, m_conv_ln_bias, m_conv_w_out, m_attn_w_out, m_w_o, m_ffn2_norm, m_ffn2_w_gate, m_ffn2_w_up, m_ffn2_w_down, m_final_norm, v_ffn1_norm, v_ffn1_w_gate, v_ffn1_w_up, v_ffn1_w_down, v_mix_norm, v_w_in, v_conv_dw_kernel, v_conv_dw_bias, v_conv_ln_gain, v_conv_ln_bias, v_conv_w_out, v_attn_w_out, v_w_o, v_ffn2_norm, v_ffn2_w_gate, v_ffn2_w_up, v_ffn2_w_down, v_final_norm):
    names = ["ffn1_norm", "ffn1_w_gate", "ffn1_w_up", "ffn1_w_down", "mix_norm", "w_in", "conv_dw_kernel",
             "conv_dw_bias", "conv_ln_gain", "conv_ln_bias", "conv_w_out", "attn_w_out", "w_o", "ffn2_norm",
             "ffn2_w_gate", "ffn2_w_up", "ffn2_w_down", "final_norm"]
    w = dict(ffn1_norm=ffn1_norm, ffn1_w_gate=ffn1_w_gate, ffn1_w_up=ffn1_w_up, ffn1_w_down=ffn1_w_down, mix_norm=mix_norm, w_in=w_in, conv_dw_kernel=conv_dw_kernel, conv_dw_bias=conv_dw_bias, conv_ln_gain=conv_ln_gain, conv_ln_bias=conv_ln_bias, conv_w_out=conv_w_out, attn_w_out=attn_w_out, w_o=w_o, ffn2_norm=ffn2_norm, ffn2_w_gate=ffn2_w_gate, ffn2_w_up=ffn2_w_up, ffn2_w_down=ffn2_w_down, final_norm=final_norm)
    mo = dict(ffn1_norm=m_ffn1_norm, ffn1_w_gate=m_ffn1_w_gate, ffn1_w_up=m_ffn1_w_up, ffn1_w_down=m_ffn1_w_down, mix_norm=m_mix_norm, w_in=m_w_in, conv_dw_kernel=m_conv_dw_kernel, conv_dw_bias=m_conv_dw_bias, conv_ln_gain=m_conv_ln_gain, conv_ln_bias=m_conv_ln_bias, conv_w_out=m_conv_w_out, attn_w_out=m_attn_w_out, w_o=m_w_o, ffn2_norm=m_ffn2_norm, ffn2_w_gate=m_ffn2_w_gate, ffn2_w_up=m_ffn2_w_up, ffn2_w_down=m_ffn2_w_down, final_norm=m_final_norm)
    vo = dict(ffn1_norm=v_ffn1_norm, ffn1_w_gate=v_ffn1_w_gate, ffn1_w_up=v_ffn1_w_up, ffn1_w_down=v_ffn1_w_down, mix_norm=v_mix_norm, w_in=v_w_in, conv_dw_kernel=v_conv_dw_kernel, conv_dw_bias=v_conv_dw_bias, conv_ln_gain=v_conv_ln_gain, conv_ln_bias=v_conv_ln_bias, conv_w_out=v_conv_w_out, attn_w_out=v_attn_w_out, w_o=v_w_o, ffn2_norm=v_ffn2_norm, ffn2_w_gate=v_ffn2_w_gate, ffn2_w_up=v_ffn2_w_up, ffn2_w_down=v_ffn2_w_down, final_norm=v_final_norm)
    col_sharded = ("ffn1_w_gate", "ffn1_w_up", "w_in", "attn_w_out", "ffn2_w_gate", "ffn2_w_up")
    row_sharded = ("ffn1_w_down", "conv_w_out", "w_o", "ffn2_w_down")
    small = ("ffn1_norm", "mix_norm", "ffn2_norm", "final_norm", "conv_dw_bias", "conv_ln_gain", "conv_ln_bias")

    def landing_view(a, n):
        return jnp.transpose(a[0]) if n in col_sharded else a[0]

    def own_view(a, n):
        return jnp.transpose(a)[None] if n in col_sharded else a[None]

    ag_groups = (("ffn1_w_gate", "ffn1_w_up", "ffn1_w_down"),
                 ("w_in", "attn_w_out", "conv_w_out", "w_o", "conv_dw_kernel"),
                 ("ffn2_w_gate", "ffn2_w_up", "ffn2_w_down"))
    ag, order = [], []
    for gi, grp in enumerate(ag_groups):
        lands = _prep_gather([landing_view(w[n], n) for n in grp], order, f"gather_prep{gi}")
        st = _gather_start(lands, GATHER_A, [], f"gather_a_start{gi}")
        ag.append(st)
        order = [st[3]]

    def chips_in(gi, after):
        lands = _gather_wait(ag[gi], GATHER_A, after, f"gather_a_wait{gi}")
        return _gather_start(lands, GATHER_B, [], f"gather_b_start{gi}")

    def all_in(gi, st, after):
        return _gather_wait(st, GATHER_B, after, f"gather_b_wait{gi}")

    x0 = x[0]
    tgt = loss_target[0]
    gf = final_norm.reshape(1, D)

    wg1, wu1, wd1 = all_in(0, chips_in(0, [ag[2][3]]), [])
    x1, gg1, uu1 = _ffn_fwd(x0, ffn1_norm, wg1, wu1, wd1, "ffn1_fwd")
    win_t, wa_t, wc, wo, kern_blocks = all_in(1, chips_in(1, [x1]), [])
    kern = kern_blocks.reshape(NDEV, 32, D // NDEV).transpose(1, 0, 2).reshape(32, D)
    h2p = _norm_cast(x1, mix_norm, "mix_norm_fwd")
    h2 = h2p[0]
    ab = _mm(h2, win_t, mode="nt", m=T, n=2 * D, k=D, tm=1024, tn=512, tk=D, out_dtype=BF16, name="proj_conv")
    gates = _mm(h2, win_t, mode="nt", m=T, n=2 * D, k=D, tm=1024, tn=512, tk=D, out_dtype=BF16,
                b_map=lambda i, j, kk: (13 + j, 0), name="proj_gates")
    qkv = []
    for gi in range(len(GROUPS)):
        qkv.append(_mm(h2p[gi], win_t, mode="nt", m=T, n=3 * AW, k=D, tm=1024, tn=AW, tk=D, out_dtype=BF16,
                       b_map=lambda i, j, kk, gi=gi: (4 + gi + 3 * j, 0), name=f"proj_qkv{gi}"))
    z1, z3b = _conv_fwd(ab, kern, conv_dw_bias, conv_ln_gain, conv_ln_bias, "conv_fwd")
    ffn2_b = chips_in(2, [z3b])
    outs, lses = [], []
    for gi, (_, dil) in enumerate(GROUPS):
        o, l = _attn_fwd(qkv[gi], gi, f"attn_fwd{gi}")
        outs.append(o)
        lses.append(l)
    attn, attnb, lse = _merge(outs, lses, "attn_merge")
    x2, yc, ya, mixedb = _mix_out(z3b, attnb, gates, wc, wa_t, wo, x1, "mix_out_fwd")
    wg2, wu2, wd2 = all_in(2, ffn2_b, [x2])
    x3, gg2, uu2 = _ffn_fwd(x2, ffn2_norm, wg2, wu2, wd2, "ffn2_fwd")

    dx3, dgf, loss_part = _final(x3, gf, tgt, "final_norm_loss")
    dx2, dg3, dgb, dub, actb, hb, dob = _ffn_bwd(x2, ffn2_norm, gg2, uu2, dx3, wg2, wu2, wd2, "ffn2_bwd")
    grads = {}
    grads["ffn2_w_gate"] = _wgrad(dgb, hb, FF, D, "ffn2_dwg")
    grads["ffn2_w_up"] = _wgrad(dub, hb, FF, D, "ffn2_dwu")
    grads["ffn2_w_down"] = _wgrad(actb, dob, FF, D, "ffn2_dwd")
    rs_groups = [("ffn2_w_gate", "ffn2_w_up", "ffn2_w_down"),
                 ("attn_w_out", "conv_w_out", "w_o", "conv_dw_kernel"),
                 ("w_in",),
                 ("ffn1_w_gate",), ("ffn1_w_up",), ("ffn1_w_down",)]
    last = len(rs_groups) - 1
    rs = [_send_start(["scatter"] * 3, [grads[n] for n in rs_groups[0]], [], "scatter_start0")]
    dx2 = _tie(dx2, [rs[0][4]], "tie_after_scatter0")

    dgates, dycb, dyab, dx2b, dz3, dattnb, delta = _mix_out_bwd(dx2, gates, yc, ya, attn, wc, wa_t, wo, "mix_out_bwd")
    grads["w_o"] = _wgrad(mixedb, dx2b, D, D, "dw_o")
    grads["conv_w_out"] = _wgrad(z3b, dycb, D, D, "dw_conv_out")
    grads["attn_w_out"] = _wgrad(dyab, attnb, D, AW, "dw_attn_out")
    dab, dkern, dvec = _conv_bwd(dz3, z1, ab, kern, conv_ln_gain, conv_ln_bias, "conv_bwd")
    grads["conv_dw_kernel"] = dkern.reshape(32, NDEV, D // NDEV).transpose(1, 0, 2).reshape(NDEV * 32, D // NDEV)
    rs.append(_send_start(["scatter"] * 4, [grads[n] for n in rs_groups[1]], [rs[0][4]], "scatter_start1"))
    dattnb = [_tie(a, [rs[1][4]], f"tie_after_scatter1_{i}") for i, a in enumerate(dattnb)]

    dqkv = []
    for gi, (_, dil) in enumerate(GROUPS):
        dq3 = _attn_bwd(qkv[gi], dattnb[gi], lse[gi], delta[gi], gi, f"attn_bwd{gi}")
        dqkv.append(dq3.reshape(3 * T, AW))

    dwin = _mm(dab, h2, mode="tn", m=2 * D, n=D, k=T, tm=2 * D, tn=D, tk=512, out_dtype=BF16, out_rows=IN_W,
               name="dw_in_conv")
    dwin = _mm(dgates, h2, mode="tn", m=2 * D, n=D, k=T, tm=512, tn=D, tk=1024, out_dtype=BF16, out_rows=IN_W,
               o_map=lambda i, j, kk: (13 + i, 0), passthru=dwin, name="dw_in_gates")
    for gi in range(3):
        dwin = _mm(dqkv[gi], h2p[gi], mode="tn", m=3 * AW, n=D, k=T, tm=AW, tn=D, tk=1024, out_dtype=BF16,
                   out_rows=IN_W, a_map=lambda i, j, kk: (i * (T // 1024) + kk, 0),
                   o_map=lambda i, j, kk, gi=gi: (4 + gi + 3 * i, 0), passthru=dwin, name=f"dw_in_qkv{gi}")
    grads["w_in"] = dwin
    rs.append(_send_start(["scatter"], [dwin], [rs[1][4]], "scatter_start2"))
    dab = _tie(dab, [rs[2][4]], "tie_after_scatter2")

    nrow = T // 1024
    dh = _mm(dab, win_t, mode="nn", m=T, n=D, k=2 * D, tm=1024, tn=D, tk=512, out_dtype=F32, name="dproj_conv")
    dh = _mm(dgates, win_t, mode="nn", m=T, n=D, k=2 * D, tm=1024, tn=D, tk=512, out_dtype=F32,
             b_map=lambda i, j, kk: (13 + kk, 0), init=dh, name="dproj_gates")
    dhs = []
    for gi, (_, dil) in enumerate(GROUPS):
        part = _mm(dqkv[gi], win_t, mode="nn", m=T, n=D, k=3 * AW, tm=1024, tn=D, tk=AW, out_dtype=F32,
                   a_map=lambda i, j, kk: (kk * nrow + i, 0), b_map=lambda i, j, kk, gi=gi: (4 + gi + 3 * kk, 0),
                   init=dh if gi == 0 else None, name=f"dproj_qkv{gi}")
        dhs.append(part)
    dx1, dg2 = _rms_bwd(x1, mix_norm, dhs, dx2, "mix_norm_bwd")

    dx0, dg1, dgb, dub, actb, hb, dob = _ffn_bwd(x0, ffn1_norm, gg1, uu1, dx1, wg1, wu1, wd1, "ffn1_bwd")
    grads["ffn1_w_gate"] = _wgrad(dgb, hb, FF, D, "ffn1_dwg")
    rs.append(_send_start(["scatter"], [grads["ffn1_w_gate"]], [rs[2][4]], "scatter_start3"))
    hb = _tie(hb, [rs[3][4]], "tie_after_scatter3")
    grads["ffn1_w_up"] = _wgrad(dub, hb, FF, D, "ffn1_dwu")
    rs.append(_send_start(["scatter"], [grads["ffn1_w_up"]], [rs[3][4]], "scatter_start4"))
    dob = _tie(dob, [rs[4][4]], "tie_after_scatter4")
    grads["ffn1_w_down"] = _wgrad(actb, dob, FF, D, "ffn1_dwd")
    vec = jnp.concatenate([dg1, dg2, dg3, dgf, dvec[0:3], jnp.broadcast_to(loss_part[:, :1], (1, D))], axis=0)
    rs.append(_send_start(["scatter", "bcast"], [grads["ffn1_w_down"], vec], [rs[4][4]], "scatter_start5"))

    g_out, d_out, m_out, v_out = {}, {}, {}, {}
    me = _my_place()
    after = [rs[last][4]]
    for gi, grp in enumerate(rs_groups):
        kinds = ["scatter"] * len(grp) + (["bcast"] if gi == last else [])
        srcs, lands = _send_wait(kinds, rs[gi], after, f"scatter_wait{gi}")
        for n, src, land in zip(grp, srcs, lands):
            if n == "conv_dw_kernel":
                rows = src.shape[0] // NDEV
                own = lax.dynamic_slice(src, (me * rows, 0), (rows, src.shape[1]))
                g = _gsum(own, land, f"gsum_{n}")[:CONV_W]
                d, m2, v2 = _adamw(w[n][0], g, mo[n][0], vo[n][0], f"adamw_{n}")
                after = [d]
                g, d, m2, v2 = g[None], d[None], m2[None], v2[None]
            else:
                res = _update(src, land, landing_view(w[n], n), landing_view(mo[n], n), landing_view(vo[n], n),
                              f"update_{n}")
                after = [res[1]]
                g, d, m2, v2 = (own_view(a, n) for a in res)
            g_out[n], d_out[n], m_out[n], v_out[n] = g, d, m2, v2
    vland = lands[-1]

    def rows8(src):
        return jnp.concatenate([src[n].reshape(1, D) for n in small] + [jnp.ones((1, D), F32)], axis=0)

    g8, d8, m8, v8 = _small_update(vland, rows8(w), rows8(mo), rows8(vo), "small_update")
    for r, n in enumerate(small):
        shp = w[n].shape
        g_out[n], d_out[n], m_out[n], v_out[n] = (a[r].reshape(shp) for a in (g8, d8, m8, v8))
    loss = g8[7, 0]

    return (loss, dx0[None], *[g_out[n] for n in names], *[d_out[n] for n in names],
            *[m_out[n] for n in names], *[v_out[n] for n in names])
```

```python
import numpy as np
import jax
import jax.numpy as jnp
from jax import lax
from jax.experimental import pallas as pl
from jax.experimental.pallas import tpu as pltpu

F32 = jnp.float32
BF16 = jnp.bfloat16

T = 4096
D = 1024
FF = 2816
NDEV = 8
CONV_W = 31
HEAD = 128
BLK = 128
GROUPS = ((128, 1), (512, 4), (2048, 16))
NHG = 4
AW = NHG * HEAD
IN_W = 2 * D + 3 * 3 * AW + 2 * D
EPS = 1e-6
B1, B2, LR, AEPS, WD, STEP = 0.9, 0.999, 0.001, 1e-08, 0.01, 10
NEG = -1e30
VMEM_LIMIT = 56 * 1024 * 1024
MESH_ID = pl.DeviceIdType.MESH

NT = (((1,), (1,)), ((), ()))
NN = (((1,), (0,)), ((), ()))
TN = (((0,), (0,)), ((), ()))
_DIMS = {"nn": NN, "nt": NT, "tn": TN}


def _cp(sem=None):
    return pltpu.CompilerParams(dimension_semantics=sem, vmem_limit_bytes=VMEM_LIMIT)


def _sig(v):
    return 1.0 / (1.0 + jnp.exp(-v))


def _dot(a, b, dims):
    return lax.dot_general(a, b, dims, preferred_element_type=F32)


def _const_spec(shape):
    nd = len(shape)
    return pl.BlockSpec(shape, lambda *_: (0,) * nd)


def _mm(a, b, *, mode, m, n, k, tm, tn, tk, out_dtype, name, a_map=None, b_map=None,
        o_map=None, out_rows=None, init=None, passthru=None):
    gi, gj, gk = m // tm, n // tn, k // tk
    assert gi * tm == m and gj * tn == n and gk * tk == k, (name, m, n, k, tm, tn, tk)
    if mode == "nn":
        a_blk, b_blk = (tm, tk), (tk, tn)
        da, db = (lambda i, j, kk: (i, kk)), (lambda i, j, kk: (kk, j))
    elif mode == "nt":
        a_blk, b_blk = (tm, tk), (tn, tk)
        da, db = (lambda i, j, kk: (i, kk)), (lambda i, j, kk: (j, kk))
    else:
        a_blk, b_blk = (tk, tm), (tk, tn)
        da, db = (lambda i, j, kk: (kk, i)), (lambda i, j, kk: (kk, j))
    a_map = a_map or da
    b_map = b_map or db
    o_map = o_map or (lambda i, j, kk: (i, j))
    dims = _DIMS[mode]
    extra = init if init is not None else passthru
    out_rows = out_rows or m

    def body(*refs):
        if init is not None:
            a_ref, b_ref, i_ref, o_ref = refs[:4]
        elif passthru is not None:
            a_ref, b_ref, _, o_ref = refs[:4]
        else:
            a_ref, b_ref, o_ref = refs[:3]
        if gk == 1:
            prod = _dot(a_ref[...], b_ref[...], dims)
            if init is not None:
                prod = prod + i_ref[...].astype(F32)
            o_ref[...] = prod.astype(out_dtype)
            return
        acc = refs[-1]
        kk = pl.program_id(2)

        @pl.when(kk == 0)
        def _():
            if init is not None:
                acc[...] = i_ref[...].astype(F32)
            else:
                acc[...] = jnp.zeros_like(acc)

        acc[...] += _dot(a_ref[...], b_ref[...], dims)

        @pl.when(kk == gk - 1)
        def _():
            o_ref[...] = acc[...].astype(out_dtype)

    in_specs = [pl.BlockSpec(a_blk, a_map), pl.BlockSpec(b_blk, b_map)]
    args = [a, b]
    aliases = {}
    if init is not None:
        in_specs.append(pl.BlockSpec((tm, tn), o_map))
        args.append(init)
        aliases = {2: 0}
    elif passthru is not None:
        in_specs.append(pl.BlockSpec(memory_space=pl.ANY))
        args.append(passthru)
        aliases = {2: 0}
    out_dt = extra.dtype if extra is not None else out_dtype
    assert out_dt == out_dtype
    return pl.pallas_call(
        body, name=name, grid=(gi, gj, gk),
        in_specs=in_specs, out_specs=pl.BlockSpec((tm, tn), o_map),
        out_shape=jax.ShapeDtypeStruct((out_rows, n), out_dtype),
        scratch_shapes=[pltpu.VMEM((tm, tn), F32)] if gk > 1 else [],
        input_output_aliases=aliases,
        compiler_params=_cp(("parallel", "parallel", "arbitrary")),
    )(*args)


def _ffn_fwd(x, g, wg_t, wu_t, wd, name):
    tm, fc = 512, 256
    nc = FF // fc

    def body(x_ref, g_ref, wg_ref, wu_ref, wd_ref, xo_ref, gg_ref, uu_ref, act_ref):
        xv = x_ref[...]
        r = lax.rsqrt(jnp.mean(xv * xv, axis=-1, keepdims=True) + EPS)
        h = (xv * r * g_ref[...]).astype(BF16)
        for c in range(nc):
            sl = pl.ds(c * fc, fc)
            gg = _dot(h, wg_ref[sl, :], NT)
            uu = _dot(h, wu_ref[sl, :], NT)
            gg_ref[:, sl] = gg.astype(BF16)
            uu_ref[:, sl] = uu.astype(BF16)
            act_ref[:, sl] = (gg * _sig(gg) * uu).astype(BF16)
        xo_ref[...] = xv + 0.5 * _dot(act_ref[...], wd_ref[...], NN)

    wspec = pl.BlockSpec((FF, D), lambda i: (0, 0), pipeline_mode=pl.Buffered(1))
    return pl.pallas_call(
        body, name=name, grid=(T // tm,),
        in_specs=[pl.BlockSpec((tm, D), lambda i: (i, 0)), _const_spec((1, D)), wspec, wspec, wspec],
        out_specs=[pl.BlockSpec((tm, D), lambda i: (i, 0)), pl.BlockSpec((tm, FF), lambda i: (i, 0)),
                   pl.BlockSpec((tm, FF), lambda i: (i, 0))],
        out_shape=[jax.ShapeDtypeStruct((T, D), F32), jax.ShapeDtypeStruct((T, FF), BF16),
                   jax.ShapeDtypeStruct((T, FF), BF16)],
        scratch_shapes=[pltpu.VMEM((tm, FF), BF16)],
        compiler_params=_cp(("parallel",)),
    )(x, g, wg_t, wu_t, wd)


def _ffn_bwd(x, g, gg_all, uu_all, dout, wg_t, wu_t, wd, name):
    tm, fc = 256, 256
    nc = FF // fc

    def body(x_ref, g_ref, gg_ref, uu_ref, do_ref, wg_ref, wu_ref, wd_ref,
             dx_ref, dgam_ref, dg_ref, du_ref, act_ref, h_ref, db_ref):
        i = pl.program_id(0)
        xv = x_ref[...]
        r = lax.rsqrt(jnp.mean(xv * xv, axis=-1, keepdims=True) + EPS)
        xhat = xv * r
        gam = g_ref[...]
        h_ref[...] = (xhat * gam).astype(BF16)
        dov = do_ref[...]
        dbv = (0.5 * dov).astype(BF16)
        db_ref[...] = dbv
        for c in range(nc):
            sl = pl.ds(c * fc, fc)
            da = _dot(dbv, wd_ref[sl, :], NT)
            gg = gg_ref[:, sl].astype(F32)
            uu = uu_ref[:, sl].astype(F32)
            s = _sig(gg)
            si = gg * s
            dgv = (da * uu * (s * (1.0 + gg * (1.0 - s)))).astype(BF16)
            duv = (da * si).astype(BF16)
            dg_ref[:, sl] = dgv
            du_ref[:, sl] = duv
            act_ref[:, sl] = (si * uu).astype(BF16)
        dh = _dot(dg_ref[...], wg_ref[...], NN) + _dot(du_ref[...], wu_ref[...], NN)

        @pl.when(i == 0)
        def _():
            dgam_ref[...] = jnp.zeros_like(dgam_ref)

        dgam_ref[...] += jnp.sum(dh * xhat, axis=0, keepdims=True)
        dxh = dh * gam
        dx_ref[...] = dov + r * (dxh - xhat * jnp.mean(dxh * xhat, axis=-1, keepdims=True))

    wspec = pl.BlockSpec((FF, D), lambda i: (0, 0), pipeline_mode=pl.Buffered(1))
    row_d = pl.BlockSpec((tm, D), lambda i: (i, 0))
    row_f = pl.BlockSpec((tm, FF), lambda i: (i, 0))
    return pl.pallas_call(
        body, name=name, grid=(T // tm,),
        in_specs=[row_d, _const_spec((1, D)), row_f, row_f, row_d, wspec, wspec, wspec],
        out_specs=[row_d, _const_spec((1, D)), row_f, row_f, row_f, row_d, row_d],
        out_shape=[jax.ShapeDtypeStruct((T, D), F32), jax.ShapeDtypeStruct((1, D), F32),
                   jax.ShapeDtypeStruct((T, FF), BF16), jax.ShapeDtypeStruct((T, FF), BF16),
                   jax.ShapeDtypeStruct((T, FF), BF16), jax.ShapeDtypeStruct((T, D), BF16),
                   jax.ShapeDtypeStruct((T, D), BF16)],
        compiler_params=_cp(("arbitrary",)),
    )(x, g, gg_all, uu_all, dout, wg_t, wu_t, wd)


def _ffn_bwd_pre(x, g, gg_all, uu_all, dout, wd, name):
    tm, fc = 512, 256
    nc = FF // fc

    def body(x_ref, g_ref, gg_ref, uu_ref, do_ref, wd_ref, dg_ref, du_ref, act_ref, h_ref, db_ref):
        xv = x_ref[...]
        r = lax.rsqrt(jnp.mean(xv * xv, axis=-1, keepdims=True) + EPS)
        h_ref[...] = (xv * r * g_ref[...]).astype(BF16)
        dbv = (0.5 * do_ref[...]).astype(BF16)
        db_ref[...] = dbv
        for c in range(nc):
            sl = pl.ds(c * fc, fc)
            da = _dot(dbv, wd_ref[sl, :], NT)
            gg = gg_ref[:, sl].astype(F32)
            uu = uu_ref[:, sl].astype(F32)
            s = _sig(gg)
            si = gg * s
            dg_ref[:, sl] = (da * uu * (s * (1.0 + gg * (1.0 - s)))).astype(BF16)
            du_ref[:, sl] = (da * si).astype(BF16)
            act_ref[:, sl] = (si * uu).astype(BF16)

    wspec = pl.BlockSpec((FF, D), lambda i: (0, 0), pipeline_mode=pl.Buffered(1))
    row_d = pl.BlockSpec((tm, D), lambda i: (i, 0))
    row_f = pl.BlockSpec((tm, FF), lambda i: (i, 0))
    return pl.pallas_call(
        body, name=name, grid=(T // tm,),
        in_specs=[row_d, _const_spec((1, D)), row_f, row_f, row_d, wspec],
        out_specs=[row_f, row_f, row_f, row_d, row_d],
        out_shape=[jax.ShapeDtypeStruct((T, FF), BF16), jax.ShapeDtypeStruct((T, FF), BF16),
                   jax.ShapeDtypeStruct((T, FF), BF16), jax.ShapeDtypeStruct((T, D), BF16),
                   jax.ShapeDtypeStruct((T, D), BF16)],
        compiler_params=_cp(("parallel",)),
    )(x, g, gg_all, uu_all, dout, wd)


def _ffn_bwd_dx(x, g, dgb, dub, dout, wg_t, wu_t, name):
    tm = 512

    def body(x_ref, g_ref, dg_ref, du_ref, do_ref, wg_ref, wu_ref, dx_ref, dgam_ref):
        i = pl.program_id(0)
        xv = x_ref[...]
        r = lax.rsqrt(jnp.mean(xv * xv, axis=-1, keepdims=True) + EPS)
        xhat = xv * r
        gam = g_ref[...]
        dh = _dot(dg_ref[...], wg_ref[...], NN) + _dot(du_ref[...], wu_ref[...], NN)

        @pl.when(i == 0)
        def _():
            dgam_ref[...] = jnp.zeros_like(dgam_ref)

        dgam_ref[...] += jnp.sum(dh * xhat, axis=0, keepdims=True)
        dxh = dh * gam
        dx_ref[...] = do_ref[...] + r * (dxh - xhat * jnp.mean(dxh * xhat, axis=-1, keepdims=True))

    wspec = pl.BlockSpec((FF, D), lambda i: (0, 0), pipeline_mode=pl.Buffered(1))
    row_d = pl.BlockSpec((tm, D), lambda i: (i, 0))
    row_f = pl.BlockSpec((tm, FF), lambda i: (i, 0))
    return pl.pallas_call(
        body, name=name, grid=(T // tm,),
        in_specs=[row_d, _const_spec((1, D)), row_f, row_f, row_d, wspec, wspec],
        out_specs=[row_d, _const_spec((1, D))],
        out_shape=[jax.ShapeDtypeStruct((T, D), F32), jax.ShapeDtypeStruct((1, D), F32)],
        compiler_params=_cp(("arbitrary",)),
    )(x, g, dgb, dub, dout, wg_t, wu_t)


def _wgrad(a, b, m, n, name):
    tm = m // 2 if m == FF else m
    return _mm(a, b, mode="tn", m=m, n=n, k=T, tm=tm, tn=n, tk=min(T, 2048), out_dtype=BF16, name=name)


PERM_TM = 512
DILS = tuple(d for _, d in GROUPS if d > 1)


def _perm_spec(dil, cols):
    return pl.BlockSpec((dil, PERM_TM // dil, cols), lambda i: (0, i, 0))


def _perm_shape(dil, cols, dtype):
    return jax.ShapeDtypeStruct((dil, T // dil, cols), dtype)


LANES = 128


def _tile_scratch(cols):
    return pltpu.VMEM((cols // LANES, PERM_TM, LANES), F32)


def _put_tile(tile, value):
    for c in range(tile.shape[0]):
        tile[c] = value[:, c * LANES:(c + 1) * LANES]


def _get_tile(tile):
    return jnp.concatenate([tile[c] for c in range(tile.shape[0])], axis=1)


def _store_perm(out_ref, tile, dil):
    for r in range(dil):
        for c in range(tile.shape[0]):
            out_ref[r, :, pl.ds(c * LANES, LANES)] = tile[c, pl.ds(r, PERM_TM // dil, stride=dil), :].astype(
                out_ref.dtype)


def _load_unperm(in_ref, tile, dil):
    for r in range(dil):
        for c in range(tile.shape[0]):
            tile[c, pl.ds(r, PERM_TM // dil, stride=dil), :] = in_ref[r, :, pl.ds(c * LANES, LANES)].astype(F32)


def _norm_cast(x, g, name):
    tm = PERM_TM

    def body(x_ref, g_ref, h_ref, *rest):
        p_refs, tile = rest[:-1], rest[-1]
        xv = x_ref[...]
        r = lax.rsqrt(jnp.mean(xv * xv, axis=-1, keepdims=True) + EPS)
        hv = xv * r * g_ref[...]
        h_ref[...] = hv.astype(BF16)
        _put_tile(tile, hv)
        for dil, p_ref in zip(DILS, p_refs):
            _store_perm(p_ref, tile, dil)

    out = pl.pallas_call(
        body, name=name, grid=(T // tm,),
        in_specs=[pl.BlockSpec((tm, D), lambda i: (i, 0)), _const_spec((1, D))],
        out_specs=[pl.BlockSpec((tm, D), lambda i: (i, 0))] + [_perm_spec(d, D) for d in DILS],
        out_shape=[jax.ShapeDtypeStruct((T, D), BF16)] + [_perm_shape(d, D, BF16) for d in DILS],
        scratch_shapes=[_tile_scratch(D)],
        compiler_params=_cp(("parallel",)),
    )(x, g)
    return [out[0]] + [o.reshape(T, D) for o in out[1:]]


def _final(x3, gf, tgt, name):
    tm = 512

    def body(x_ref, g_ref, t_ref, dx_ref, dgam_ref, loss_ref):
        i = pl.program_id(0)
        xv = x_ref[...]
        r = lax.rsqrt(jnp.mean(xv * xv, axis=-1, keepdims=True) + EPS)
        xhat = xv * r
        gam = g_ref[...]
        err = xhat * gam - t_ref[...]
        part = 0.5 * jnp.sum(jnp.mean(err * err, axis=-1, keepdims=True), axis=0, keepdims=True)
        dy = err * (1.0 / D)

        @pl.when(i == 0)
        def _():
            dgam_ref[...] = jnp.zeros_like(dgam_ref)
            loss_ref[...] = jnp.zeros_like(loss_ref)

        dgam_ref[...] += jnp.sum(dy * xhat, axis=0, keepdims=True)
        loss_ref[...] += jnp.broadcast_to(part, loss_ref.shape)
        dxh = dy * gam
        dx_ref[...] = r * (dxh - xhat * jnp.mean(dxh * xhat, axis=-1, keepdims=True))

    row_d = pl.BlockSpec((tm, D), lambda i: (i, 0))
    return pl.pallas_call(
        body, name=name, grid=(T // tm,),
        in_specs=[row_d, _const_spec((1, D)), row_d],
        out_specs=[row_d, _const_spec((1, D)), _const_spec((1, 128))],
        out_shape=[jax.ShapeDtypeStruct((T, D), F32), jax.ShapeDtypeStruct((1, D), F32),
                   jax.ShapeDtypeStruct((1, 128), F32)],
        compiler_params=_cp(("arbitrary",)),
    )(x3, gf, tgt)


def _rms_bwd(x, g, dhs, dres, name):
    tm = PERM_TM
    dils = [d for _, d in GROUPS]
    nh = len(dhs)
    assert nh == len(dils)

    def body(*refs):
        x_ref, g_ref = refs[:2]
        dh_refs = refs[2:2 + nh]
        dr_ref, dx_ref, dgam_ref, tile = refs[2 + nh:]
        i = pl.program_id(0)
        xv = x_ref[...]
        r = lax.rsqrt(jnp.mean(xv * xv, axis=-1, keepdims=True) + EPS)
        xhat = xv * r
        gam = g_ref[...]
        dh = None
        for dil, ref in zip(dils, dh_refs):
            if dil == 1:
                part = ref[...]
            else:
                _load_unperm(ref, tile, dil)
                part = _get_tile(tile)
            dh = part if dh is None else dh + part

        @pl.when(i == 0)
        def _():
            dgam_ref[...] = jnp.zeros_like(dgam_ref)

        dgam_ref[...] += jnp.sum(dh * xhat, axis=0, keepdims=True)
        dxh = dh * gam
        dx_ref[...] = dr_ref[...] + r * (dxh - xhat * jnp.mean(dxh * xhat, axis=-1, keepdims=True))

    row_d = pl.BlockSpec((tm, D), lambda i: (i, 0))
    dh_specs = [row_d if d == 1 else _perm_spec(d, D) for d in dils]
    dh_args = [a if d == 1 else a.reshape(d, T // d, D) for d, a in zip(dils, dhs)]
    return pl.pallas_call(
        body, name=name, grid=(T // tm,),
        in_specs=[row_d, _const_spec((1, D))] + dh_specs + [row_d],
        out_specs=[row_d, _const_spec((1, D))],
        out_shape=[jax.ShapeDtypeStruct((T, D), F32), jax.ShapeDtypeStruct((1, D), F32)],
        scratch_shapes=[_tile_scratch(D)],
        compiler_params=_cp(("arbitrary",)),
    )(x, g, *dh_args, dres)


CONV_TM = 256
CONV_HALO = 32
CONV_RB = 16


def _glu(ab):
    ab = ab.astype(F32)
    return ab[:, :D] * _sig(ab[:, D:])


def _ln_stats(z1):
    mu = jnp.mean(z1, axis=-1, keepdims=True)
    zc = z1 - mu
    rstd = lax.rsqrt(jnp.mean(zc * zc, axis=-1, keepdims=True) + EPS)
    return zc * rstd, rstd


def _fill_shifts(zs):
    n = zs.shape[1] - 8
    for s in range(1, 8):
        zs[s, pl.ds(0, n), :] = zs[0, pl.ds(s, n), :]


def _shifted(zs, start, rows):
    q, s = divmod(start, 8)
    return zs[s, pl.ds(8 * q, rows), :]


def _conv_fwd(ab, kern, dwb, lng, lnb, name):
    tm, hl, rb = CONV_TM, CONV_HALO, CONV_RB
    off = hl - (CONV_W - 1)

    def body(ab_ref, abh_ref, k_ref, dwb_ref, lng_ref, lnb_ref, z1_ref, z3_ref, zs):
        i = pl.program_id(0)
        zs[0, pl.ds(0, hl), :] = jnp.where(i > 0, _glu(abh_ref[...]), 0.0)
        zs[0, pl.ds(hl, tm), :] = _glu(ab_ref[...])
        _fill_shifts(zs)
        for b in range(tm // rb):
            acc = jnp.zeros((rb, D), F32)
            for j in range(CONV_W):
                acc = acc + _shifted(zs, b * rb + off + j, rb) * k_ref[pl.ds(j, 1), :]
            z1 = acc + dwb_ref[...]
            z1_ref[pl.ds(b * rb, rb), :] = z1
            zn, _ = _ln_stats(z1)
            z2 = zn * lng_ref[...] + lnb_ref[...]
            z3_ref[pl.ds(b * rb, rb), :] = (z2 * _sig(z2)).astype(BF16)

    row = pl.BlockSpec((tm, D), lambda i: (i, 0))
    return pl.pallas_call(
        body, name=name, grid=(T // tm,),
        in_specs=[pl.BlockSpec((tm, 2 * D), lambda i: (i, 0)),
                  pl.BlockSpec((hl, 2 * D), lambda i: (jnp.maximum(i * (tm // hl) - 1, 0), 0)),
                  _const_spec((32, D)), _const_spec((1, D)), _const_spec((1, D)), _const_spec((1, D))],
        out_specs=[row, row],
        out_shape=[jax.ShapeDtypeStruct((T, D), F32), jax.ShapeDtypeStruct((T, D), BF16)],
        scratch_shapes=[pltpu.VMEM((8, hl + tm, D), F32)],
        compiler_params=_cp(("parallel",)),
    )(ab, ab, kern, dwb, lng, lnb)


def _conv_bwd(dz3, z1, ab, kern, lng, lnb, name):
    tm, hl, rb = CONV_TM, CONV_HALO, CONV_RB
    off = hl - (CONV_W - 1)
    nsteps = T // tm

    def ln_bwd(dz3v, z1v, lngv, lnbv):
        zn, rstd = _ln_stats(z1v)
        z2 = zn * lngv + lnbv
        s = _sig(z2)
        dz2 = dz3v * (s * (1.0 + z2 * (1.0 - s)))
        dzn = dz2 * lngv
        dz1 = rstd * (dzn - jnp.mean(dzn, axis=-1, keepdims=True)
                      - zn * jnp.mean(dzn * zn, axis=-1, keepdims=True))
        return dz1, dz2, zn

    def body(dz3_ref, dz3h_ref, z1_ref, z1h_ref, ab_ref, abh_ref, k_ref, lng_ref, lnb_ref,
             dab_ref, dk_ref, dvec_ref, zs, dzs):
        i = pl.program_id(0)
        lngv, lnbv = lng_ref[...], lnb_ref[...]

        @pl.when(i == 0)
        def _():
            dk_ref[...] = jnp.zeros_like(dk_ref)
            dvec_ref[...] = jnp.zeros_like(dvec_ref)

        dz1, dz2, zn = ln_bwd(dz3_ref[...], z1_ref[...], lngv, lnbv)
        dvec_ref[pl.ds(0, 1), :] += jnp.sum(dz1, axis=0, keepdims=True)
        dvec_ref[pl.ds(1, 1), :] += jnp.sum(dz2 * zn, axis=0, keepdims=True)
        dvec_ref[pl.ds(2, 1), :] += jnp.sum(dz2, axis=0, keepdims=True)
        dzs[0, pl.ds(0, tm), :] = dz1
        dz1h, _, _ = ln_bwd(dz3h_ref[...], z1h_ref[...], lngv, lnbv)
        dzs[0, pl.ds(tm, hl), :] = jnp.where(i < nsteps - 1, dz1h, 0.0)
        _fill_shifts(dzs)
        zs[0, pl.ds(0, hl), :] = jnp.where(i > 0, _glu(abh_ref[...]), 0.0)
        zs[0, pl.ds(hl, tm), :] = _glu(ab_ref[...])
        _fill_shifts(zs)

        for j in range(CONV_W):
            tot = jnp.zeros((rb, D), F32)
            for b in range(tm // rb):
                tot = tot + dzs[0, pl.ds(b * rb, rb), :] * _shifted(zs, b * rb + off + j, rb)
            dk_ref[pl.ds(j, 1), :] += jnp.sum(tot, axis=0, keepdims=True)

        for b in range(tm // rb):
            acc = jnp.zeros((rb, D), F32)
            for j in range(CONV_W):
                acc = acc + _shifted(dzs, b * rb + (CONV_W - 1) - j, rb) * k_ref[pl.ds(j, 1), :]
            av = ab_ref[pl.ds(b * rb, rb), pl.ds(0, D)].astype(F32)
            sb = _sig(ab_ref[pl.ds(b * rb, rb), pl.ds(D, D)].astype(F32))
            dab_ref[pl.ds(b * rb, rb), pl.ds(0, D)] = (acc * sb).astype(BF16)
            dab_ref[pl.ds(b * rb, rb), pl.ds(D, D)] = (acc * av * sb * (1.0 - sb)).astype(BF16)

    row = pl.BlockSpec((tm, D), lambda i: (i, 0))
    nxt = pl.BlockSpec((hl, D), lambda i: (jnp.minimum((i + 1) * (tm // hl), T // hl - 1), 0))
    return pl.pallas_call(
        body, name=name, grid=(nsteps,),
        in_specs=[row, nxt, row, nxt,
                  pl.BlockSpec((tm, 2 * D), lambda i: (i, 0)),
                  pl.BlockSpec((hl, 2 * D), lambda i: (jnp.maximum(i * (tm // hl) - 1, 0), 0)),
                  _const_spec((32, D)), _const_spec((1, D)), _const_spec((1, D))],
        out_specs=[pl.BlockSpec((tm, 2 * D), lambda i: (i, 0)), _const_spec((32, D)), _const_spec((8, D))],
        out_shape=[jax.ShapeDtypeStruct((T, 2 * D), BF16), jax.ShapeDtypeStruct((32, D), F32),
                   jax.ShapeDtypeStruct((8, D), F32)],
        scratch_shapes=[pltpu.VMEM((8, hl + tm, D), F32), pltpu.VMEM((8, tm + hl, D), F32)],
        compiler_params=_cp(("arbitrary",)),
    )(dz3, dz3, z1, z1, ab, ab, kern, lng, lnb)


def _alibi_slopes():
    h = np.arange(1, 3 * NHG + 1, dtype=np.float32)
    return np.power(np.float32(2.0), -8.0 * h / np.float32(3 * NHG)).astype(np.float32)


def _band_bias(gi):
    _, dil = GROUPS[gi]
    slopes = _alibi_slopes()[gi * NHG:(gi + 1) * NHG]
    qi = np.arange(BLK)[:, None]
    ki = np.arange(2 * BLK)[None, :]
    steps = BLK + qi - ki
    band = (steps >= 0) & (steps <= BLK)
    bias = -slopes[:, None, None] * (dil * steps).astype(np.float32)[None]
    return jnp.asarray(np.where(band[None], bias, np.float32(NEG)).astype(np.float32))


QB_FWD = 4
QB_BWD = 8


def _attn_specs(qb):
    prev = lambda n: jnp.maximum(n * qb - 1, 0)
    return [pl.BlockSpec((qb * BLK, HEAD), lambda h, n: (n, h)),
            pl.BlockSpec((BLK, HEAD), lambda h, n: (prev(n), NHG + h)),
            pl.BlockSpec((qb * BLK, HEAD), lambda h, n: (n, NHG + h)),
            pl.BlockSpec((BLK, HEAD), lambda h, n: (prev(n), 2 * NHG + h)),
            pl.BlockSpec((qb * BLK, HEAD), lambda h, n: (n, 2 * NHG + h)),
            pl.BlockSpec((None, BLK, 2 * BLK), lambda h, n: (h, 0, 0))]


def _scores(q, kcat, bias, blk, seg):
    s = _dot(q, kcat, NT) * (HEAD ** -0.5) + bias
    col = lax.broadcasted_iota(jnp.int32, s.shape, 1)
    first = (blk % seg) == 0
    return jnp.where(jnp.logical_and(first, col < BLK), NEG, s)


def _attn_fwd(qkv, gi, name):
    seg = (T // GROUPS[gi][1]) // BLK

    qb = QB_FWD

    def body(q_ref, kp_ref, kc_ref, vp_ref, vc_ref, bias_ref, o_ref, l_ref):
        n = pl.program_id(0)
        for h in range(NHG):
            cols = pl.ds(h * HEAD, HEAD)
            kwin = jnp.concatenate([kp_ref[:, cols], kc_ref[:, cols]], axis=0)
            vwin = jnp.concatenate([vp_ref[:, cols], vc_ref[:, cols]], axis=0)
            bias = bias_ref[h]
            for b in range(qb):
                rows = pl.ds(b * BLK, BLK)
                s = _scores(q_ref[rows, cols], kwin[b * BLK:(b + 2) * BLK], bias, n * qb + b, seg)
                mx = jnp.max(s, axis=-1, keepdims=True)
                p = jnp.exp(s - mx)
                den = jnp.sum(p, axis=-1, keepdims=True)
                o_ref[rows, cols] = _dot(p.astype(BF16), vwin[b * BLK:(b + 2) * BLK], NN) / den
                l_ref[rows, cols] = jnp.broadcast_to(mx + jnp.log(den), (BLK, HEAD))

    prev = lambda n: jnp.maximum(n * qb - 1, 0)
    cur = lambda part: pl.BlockSpec((qb * BLK, AW), lambda n: (n, part))
    halo = lambda part: pl.BlockSpec((BLK, AW), lambda n: (prev(n), part))
    return pl.pallas_call(
        body, name=name, grid=(T // (qb * BLK),),
        in_specs=[cur(0), halo(1), cur(1), halo(2), cur(2), _const_spec((NHG, BLK, 2 * BLK))],
        out_specs=[cur(0), cur(0)],
        out_shape=[jax.ShapeDtypeStruct((T, AW), F32), jax.ShapeDtypeStruct((T, AW), F32)],
        compiler_params=_cp(("parallel",)),
    )(qkv, qkv, qkv, qkv, qkv, _band_bias(gi))


def _attn_bwd(qkv, dob, lse, delta, gi, name):
    seg = (T // GROUPS[gi][1]) // BLK
    qb = QB_BWD
    nb = T // (qb * BLK)
    scale = HEAD ** -0.5

    def body(q_ref, kp_ref, kc_ref, vp_ref, vc_ref, bias_ref, do_ref, l_ref, dl_ref, out_ref, dk_acc, dv_acc):
        n = pl.program_id(1)
        kwin = jnp.concatenate([kp_ref[...], kc_ref[...]], axis=0)
        vwin = jnp.concatenate([vp_ref[...], vc_ref[...]], axis=0)
        bias = bias_ref[...]
        for b in range(qb):
            rows = pl.ds(b * BLK, BLK)
            q = q_ref[rows, :]
            kcat = kwin[b * BLK:(b + 2) * BLK]
            s = _scores(q, kcat, bias, n * qb + b, seg)
            p = jnp.exp(s - l_ref[rows, pl.ds(0, 1)])
            dov = do_ref[rows, :]
            dv2 = _dot(p.astype(BF16), dov, TN)
            dp = _dot(dov, vwin[b * BLK:(b + 2) * BLK], NT)
            dsb = (p * (dp - dl_ref[rows, pl.ds(0, 1)]) * scale).astype(BF16)
            row = pl.ds(pl.multiple_of((n * qb + b) * BLK, BLK), BLK)
            out_ref[0, row, :] = _dot(dsb, kcat, NN).astype(BF16)
            dk2 = _dot(dsb, q, TN)
            dk_acc[row, :] = dk2[BLK:]
            dv_acc[row, :] = dv2[BLK:]

            def add_prev(dk2=dk2, dv2=dv2, b=b):
                prow = pl.ds(pl.multiple_of((n * qb + b - 1) * BLK, BLK), BLK)
                dk_acc[prow, :] += dk2[:BLK]
                dv_acc[prow, :] += dv2[:BLK]

            if b == 0:
                pl.when(n > 0)(add_prev)
            else:
                add_prev()

        @pl.when(n == nb - 1)
        def _():
            out_ref[1] = dk_acc[...].astype(BF16)
            out_ref[2] = dv_acc[...].astype(BF16)

    oblk = pl.BlockSpec((qb * BLK, HEAD), lambda h, n: (n, h))
    return pl.pallas_call(
        body, name=name, grid=(NHG, nb),
        in_specs=_attn_specs(qb) + [oblk, oblk, oblk],
        out_specs=pl.BlockSpec((3, T, HEAD), lambda h, n: (0, 0, h)),
        out_shape=jax.ShapeDtypeStruct((3, T, AW), BF16),
        scratch_shapes=[pltpu.VMEM((T, HEAD), F32), pltpu.VMEM((T, HEAD), F32)],
        compiler_params=_cp(("parallel", "arbitrary")),
    )(qkv, qkv, qkv, qkv, qkv, _band_bias(gi), dob, lse, delta)


def _merge(outs, lses, name):
    tm = PERM_TM
    dils = [d for _, d in GROUPS]
    ng = len(dils)

    def body(*refs):
        in_refs = refs[:2 * ng]
        a_ref, ab_ref = refs[2 * ng:2 * ng + 2]
        lse_refs = refs[2 * ng + 2:3 * ng + 2]
        tile = refs[-1]

        def token_order(ref, dil):
            if dil == 1:
                return ref[...]
            _load_unperm(ref, tile, dil)
            return _get_tile(tile)

        os = [token_order(in_refs[2 * i], d) for i, d in enumerate(dils)]
        ls = [token_order(in_refs[2 * i + 1], d) for i, d in enumerate(dils)]
        mx = jnp.maximum(jnp.maximum(ls[0], ls[1]), ls[2])
        es = [jnp.exp(v - mx) for v in ls]
        tot = es[0] + es[1] + es[2]
        att = (es[0] / tot) * os[0] + (es[1] / tot) * os[1] + (es[2] / tot) * os[2]
        a_ref[...] = att
        ab_ref[...] = att.astype(BF16)
        lse = mx + jnp.log(tot)
        _put_tile(tile, lse)
        for dil, ref in zip(dils, lse_refs):
            if dil == 1:
                ref[...] = lse
            else:
                _store_perm(ref, tile, dil)

    row = pl.BlockSpec((tm, AW), lambda i: (i, 0))
    specs = [row if d == 1 else _perm_spec(d, AW) for d in dils]
    args = []
    for d, o, l in zip(dils, outs, lses):
        args += [o, l] if d == 1 else [o.reshape(d, T // d, AW), l.reshape(d, T // d, AW)]
    out = pl.pallas_call(
        body, name=name, grid=(T // tm,),
        in_specs=[sp for sp in specs for _ in range(2)], out_specs=[row, row] + specs,
        out_shape=[jax.ShapeDtypeStruct((T, AW), F32), jax.ShapeDtypeStruct((T, AW), BF16)]
        + [jax.ShapeDtypeStruct((T, AW), F32) if d == 1 else _perm_shape(d, AW, F32) for d in dils],
        scratch_shapes=[_tile_scratch(AW)],
        compiler_params=_cp(("parallel",)),
    )(*args)
    return out[0], out[1], [o.reshape(T, AW) for o in out[2:]]


def _mix_out(z3b, attnb, gates, wc, wa_t, wo, x1, name):
    tm = 512

    def body(z_ref, a_ref, g_ref, wc_ref, wa_ref, wo_ref, x_ref, xo_ref, yc_ref, ya_ref, mx_ref):
        yc = _dot(z_ref[...], wc_ref[...], NN)
        ya = _dot(a_ref[...], wa_ref[...], NT)
        yc_ref[...] = yc
        ya_ref[...] = ya
        gv = g_ref[...].astype(F32)
        mixed = (_sig(gv[:, :D]) * yc + _sig(gv[:, D:]) * ya).astype(BF16)
        mx_ref[...] = mixed
        xo_ref[...] = x_ref[...] + _dot(mixed, wo_ref[...], NN)

    row = pl.BlockSpec((tm, D), lambda i: (i, 0))
    return pl.pallas_call(
        body, name=name, grid=(T // tm,),
        in_specs=[row, pl.BlockSpec((tm, AW), lambda i: (i, 0)), pl.BlockSpec((tm, 2 * D), lambda i: (i, 0)),
                  _const_spec((D, D)), _const_spec((D, AW)), _const_spec((D, D)), row],
        out_specs=[row, row, row, row],
        out_shape=[jax.ShapeDtypeStruct((T, D), F32), jax.ShapeDtypeStruct((T, D), F32),
                   jax.ShapeDtypeStruct((T, D), F32), jax.ShapeDtypeStruct((T, D), BF16)],
        compiler_params=_cp(("parallel",)),
    )(z3b, attnb, gates, wc, wa_t, wo, x1)


def _mix_out_bwd(dx2, gates, yc, ya, attn, wc, wa_t, wo, name):
    tm = PERM_TM
    dils = [d for _, d in GROUPS]
    ng = len(dils)

    def body(dx_ref, g_ref, yc_ref, ya_ref, at_ref, wc_ref, wa_ref, wo_ref,
             dg_ref, dyc_ref, dya_ref, dxb_ref, dz3_ref, *rest):
        dat_refs, dl_refs, tile = rest[:ng], rest[ng:2 * ng], rest[-1]
        dxb = dx_ref[...].astype(BF16)
        dxb_ref[...] = dxb
        dmix = _dot(dxb, wo_ref[...], NT)
        gv = g_ref[...].astype(F32)
        sc = _sig(gv[:, :D])
        sa = _sig(gv[:, D:])
        ycv, yav = yc_ref[...], ya_ref[...]
        dg_ref[:, pl.ds(0, D)] = (dmix * ycv * sc * (1.0 - sc)).astype(BF16)
        dg_ref[:, pl.ds(D, D)] = (dmix * yav * sa * (1.0 - sa)).astype(BF16)
        dyc = (dmix * sc).astype(BF16)
        dya = (dmix * sa).astype(BF16)
        dyc_ref[...] = dyc
        dya_ref[...] = dya
        dz3_ref[...] = _dot(dyc, wc_ref[...], NT)
        dat = _dot(dya, wa_ref[...], NN)
        prod = dat * at_ref[...]
        delta = jnp.concatenate(
            [jnp.broadcast_to(jnp.sum(prod[:, h * HEAD:(h + 1) * HEAD], axis=-1, keepdims=True), (tm, HEAD))
             for h in range(NHG)], axis=1)
        for value, out_refs in ((dat, dat_refs), (delta, dl_refs)):
            _put_tile(tile, value)
            for dil, ref in zip(dils, out_refs):
                if dil == 1:
                    ref[...] = value.astype(ref.dtype)
                else:
                    _store_perm(ref, tile, dil)

    row = pl.BlockSpec((tm, D), lambda i: (i, 0))
    row2 = pl.BlockSpec((tm, 2 * D), lambda i: (i, 0))
    rowa = pl.BlockSpec((tm, AW), lambda i: (i, 0))
    aspecs = [rowa if d == 1 else _perm_spec(d, AW) for d in dils]

    def ashapes(dtype):
        return [jax.ShapeDtypeStruct((T, AW), dtype) if d == 1 else _perm_shape(d, AW, dtype) for d in dils]

    out = pl.pallas_call(
        body, name=name, grid=(T // tm,),
        in_specs=[row, row2, row, row, rowa, _const_spec((D, D)), _const_spec((D, AW)), _const_spec((D, D))],
        out_specs=[row2, row, row, row, row] + aspecs + aspecs,
        out_shape=[jax.ShapeDtypeStruct((T, 2 * D), BF16), jax.ShapeDtypeStruct((T, D), BF16),
                   jax.ShapeDtypeStruct((T, D), BF16), jax.ShapeDtypeStruct((T, D), BF16),
                   jax.ShapeDtypeStruct((T, D), F32)] + ashapes(BF16) + ashapes(F32),
        scratch_shapes=[_tile_scratch(AW)],
        compiler_params=_cp(("parallel",)),
    )(dx2, gates, yc, ya, attn, wc, wa_t, wo)
    dats = [o.reshape(T, AW) for o in out[5:5 + ng]]
    deltas = [o.reshape(T, AW) for o in out[5 + ng:5 + 2 * ng]]
    return out[0], out[1], out[2], out[3], out[4], dats, deltas


def _peer(k):
    x, y, c = lax.axis_index("x"), lax.axis_index("y"), lax.axis_index("c")
    px = 1 - x if k & 4 else x
    py = 1 - y if k & 2 else y
    pc = 1 - c if k & 1 else c
    return (px, py, pc), 4 * px + 2 * py + pc


HBM_SPEC = pl.BlockSpec(memory_space=pltpu.HBM)
SEM_SPEC = pl.BlockSpec(memory_space=pltpu.SEMAPHORE)
EFFECT = pltpu.SideEffectType.DATAFLOW_SIDE_EFFECTING


def _my_place():
    return 4 * lax.axis_index("x") + 2 * lax.axis_index("y") + lax.axis_index("c")


def _tie(a, order_after, name):
    na = len(order_after)

    def body(*refs):
        del refs

    return pl.pallas_call(
        body, name=name, in_specs=[pl.BlockSpec(memory_space=pl.ANY)] * (1 + na),
        out_specs=pl.BlockSpec(memory_space=pl.ANY), out_shape=jax.ShapeDtypeStruct(a.shape, a.dtype),
        input_output_aliases={0: 0},
    )(a, *order_after)


def _prep_gather(ws, order_after, name):
    me = jnp.reshape(_my_place(), (1,)).astype(jnp.int32)
    n = len(ws)
    na = len(order_after)
    shapes = [((32, wv.shape[1]), F32) if wv.shape[0] == CONV_W else (wv.shape, BF16) for wv in ws]

    def body(me_ref, *refs):
        del me_ref
        ins, outs = refs[:n], refs[n + na:]
        for wv, i_ref, o_ref in zip(ws, ins, outs):
            if wv.shape[0] == CONV_W:
                o_ref[pl.ds(0, CONV_W), :] = i_ref[...]
                o_ref[pl.ds(CONV_W, 1), :] = jnp.zeros((1, wv.shape[1]), F32)
            else:
                o_ref[...] = i_ref[...].astype(BF16)

    grid_spec = pltpu.PrefetchScalarGridSpec(
        num_scalar_prefetch=1, grid=(1,),
        in_specs=[pl.BlockSpec(wv.shape, lambda i, m: (0, 0)) for wv in ws]
        + [pl.BlockSpec(memory_space=pl.ANY)] * na,
        out_specs=[pl.BlockSpec(shp, lambda i, m: (m[0], 0)) for shp, _ in shapes])
    return pl.pallas_call(
        body, name=name, grid_spec=grid_spec,
        out_shape=[jax.ShapeDtypeStruct((NDEV * shp[0], shp[1]), dt) for shp, dt in shapes],
        compiler_params=_cp(("arbitrary",)),
    )(me, *ws, *order_after)


GATHER_A = ((1, 0), (2, 0), (4, 0), (6, 0))
GATHER_B = ((1, 2), (1, 4), (1, 6))


def _gather_start(lands, plan, order_after, name):
    n = len(lands)
    na = len(order_after)
    npl = len(plan)

    def body(*refs):
        land_refs = refs[:n]
        send, recv = refs[n + na], refs[n + na + 1]
        token = refs[-1]
        for w in range(n):
            rows = lands[w].shape[0] // NDEV
            for p, (k, j) in enumerate(plan):
                peer, _ = _peer(k)
                _, blk = _peer(j)
                part = land_refs[w].at[pl.ds(blk * rows, rows)]
                i = w * npl + p
                pltpu.make_async_remote_copy(src_ref=part, dst_ref=part, send_sem=send.at[i], recv_sem=recv.at[i],
                                             device_id=peer, device_id_type=MESH_ID).start()
        token[...] = jnp.zeros_like(token)

    nsem = n * npl
    bufs = [pltpu.with_memory_space_constraint(a, pltpu.HBM) for a in lands]
    out = pl.pallas_call(
        body, name=name,
        in_specs=[HBM_SPEC] * n + [pl.BlockSpec(memory_space=pl.ANY)] * na,
        out_specs=[SEM_SPEC, SEM_SPEC] + [HBM_SPEC] * n + [pl.BlockSpec(memory_space=pltpu.VMEM)],
        out_shape=[pltpu.SemaphoreType.DMA((nsem,)), pltpu.SemaphoreType.DMA((nsem,))]
        + [pltpu.HBM(a.shape, a.dtype) for a in bufs] + [jax.ShapeDtypeStruct((8, 128), F32)],
        input_output_aliases={i: 2 + i for i in range(n)},
        compiler_params=pltpu.CompilerParams(has_side_effects=EFFECT),
    )(*bufs, *order_after)
    return out[0], out[1], out[2:2 + n], out[-1]


def _gather_wait(started, plan, order_after, name):
    send, recv, lands, _ = started
    n = len(lands)
    na = len(order_after)
    npl = len(plan)

    def body(*refs):
        land_refs = refs[:n]
        send_ref, recv_ref = refs[n], refs[n + 1]
        for w in range(n):
            rows = lands[w].shape[0] // NDEV
            for p, (k, j) in enumerate(plan):
                peer, _ = _peer(k)
                _, blk = _peer(j)
                part = land_refs[w].at[pl.ds(blk * rows, rows)]
                i = w * npl + p
                cp = pltpu.make_async_remote_copy(src_ref=part, dst_ref=part, send_sem=send_ref.at[i],
                                                  recv_sem=recv_ref.at[i], device_id=peer, device_id_type=MESH_ID)
                cp.wait_send()
                cp.wait_recv()

    out = pl.pallas_call(
        body, name=name,
        in_specs=[HBM_SPEC] * n + [SEM_SPEC, SEM_SPEC] + [pl.BlockSpec(memory_space=pl.ANY)] * na,
        out_specs=[HBM_SPEC] * n,
        out_shape=[pltpu.HBM(a.shape, a.dtype) for a in lands],
        input_output_aliases={i: i for i in range(n)},
        compiler_params=pltpu.CompilerParams(has_side_effects=EFFECT),
    )(*lands, send, recv, *order_after)
    return list(out)


def _copy_ends(kind, src, land, me, plin, k):
    if kind == "scatter":
        rows = src.shape[0] // NDEV
        return src.at[pl.ds(plin * rows, rows)], land.at[k - 1]
    return src, land.at[me]


def _landing(kind, src):
    me = _my_place()
    if kind == "scatter":
        return lax.empty((NDEV - 1, src.shape[0] // NDEV) + src.shape[1:], src.dtype)
    land = lax.empty((NDEV,) + src.shape, src.dtype)
    return lax.dynamic_update_slice(land, src[None], (me,) + (0,) * src.ndim)


def _send_start(kinds, srcs, order_after, name):
    n = len(srcs)
    lands = [_landing(kd, s) for kd, s in zip(kinds, srcs)]
    na = len(order_after)

    def body(*refs):
        src_refs, land_refs = refs[:n], refs[n:2 * n]
        send, recv = refs[2 * n + na], refs[2 * n + na + 1]
        token = refs[-1]
        _, me = _peer(0)
        for w in range(n):
            for k in range(1, NDEV):
                peer, plin = _peer(k)
                s, d = _copy_ends(kinds[w], src_refs[w], land_refs[w], me, plin, k)
                i = w * (NDEV - 1) + k - 1
                pltpu.make_async_remote_copy(src_ref=s, dst_ref=d, send_sem=send.at[i], recv_sem=recv.at[i],
                                             device_id=peer, device_id_type=MESH_ID).start()
        token[...] = jnp.zeros_like(token)

    nsem = n * (NDEV - 1)
    bufs = [pltpu.with_memory_space_constraint(a, pltpu.HBM) for a in list(srcs) + lands]
    out = pl.pallas_call(
        body, name=name,
        in_specs=[HBM_SPEC] * (2 * n) + [pl.BlockSpec(memory_space=pl.ANY)] * na,
        out_specs=[SEM_SPEC, SEM_SPEC] + [HBM_SPEC] * (2 * n) + [pl.BlockSpec(memory_space=pltpu.VMEM)],
        out_shape=[pltpu.SemaphoreType.DMA((nsem,)), pltpu.SemaphoreType.DMA((nsem,))]
        + [pltpu.HBM(a.shape, a.dtype) for a in bufs] + [jax.ShapeDtypeStruct((8, 128), F32)],
        input_output_aliases={i: 2 + i for i in range(2 * n)},
        compiler_params=pltpu.CompilerParams(has_side_effects=EFFECT),
    )(*bufs, *order_after)
    return out[0], out[1], out[2:2 + n], out[2 + n:2 + 2 * n], out[-1]


def _send_wait(kinds, started, order_after, name):
    send, recv, srcs, lands, _ = started
    n = len(srcs)
    na = len(order_after)

    def body(*refs):
        src_refs, land_refs = refs[:n], refs[n:2 * n]
        send_ref, recv_ref = refs[2 * n], refs[2 * n + 1]
        _, me = _peer(0)
        for w in range(n):
            for k in range(1, NDEV):
                peer, plin = _peer(k)
                s, d = _copy_ends(kinds[w], src_refs[w], land_refs[w], me, plin, k)
                i = w * (NDEV - 1) + k - 1
                cp = pltpu.make_async_remote_copy(src_ref=s, dst_ref=d, send_sem=send_ref.at[i],
                                                  recv_sem=recv_ref.at[i], device_id=peer, device_id_type=MESH_ID)
                cp.wait_send()
                cp.wait_recv()

    bufs = list(srcs) + list(lands)
    out = pl.pallas_call(
        body, name=name,
        in_specs=[HBM_SPEC] * (2 * n) + [SEM_SPEC, SEM_SPEC] + [pl.BlockSpec(memory_space=pl.ANY)] * na,
        out_specs=[HBM_SPEC] * (2 * n),
        out_shape=[pltpu.HBM(a.shape, a.dtype) for a in bufs],
        input_output_aliases={i: i for i in range(2 * n)},
        compiler_params=pltpu.CompilerParams(has_side_effects=EFFECT),
    )(*bufs, send, recv, *order_after)
    return out[:n], out[n:]


def _gsum(own, land, name):
    rows, cols = own.shape
    tr = rows // 2 if rows * cols > 512 * 1024 and rows % 32 == 0 else rows

    def body(own_ref, l_ref, o_ref):
        tot = own_ref[...].astype(F32)
        for s in range(NDEV - 1):
            tot = tot + l_ref[s].astype(F32)
        o_ref[...] = tot

    return pl.pallas_call(
        body, name=name, grid=(rows // tr,),
        in_specs=[pl.BlockSpec((tr, cols), lambda i: (i, 0)),
                  pl.BlockSpec((NDEV - 1, tr, cols), lambda i: (0, i, 0))],
        out_specs=pl.BlockSpec((tr, cols), lambda i: (i, 0)),
        out_shape=jax.ShapeDtypeStruct((rows, cols), F32),
        compiler_params=_cp(("parallel",)),
    )(own, land)


def _adamw_math(w, g, m, v):
    m2 = B1 * m + (1.0 - B1) * g
    v2 = B2 * v + (1.0 - B2) * (g * g)
    m_hat = m2 / (1.0 - B1 ** STEP)
    v_hat = v2 / (1.0 - B2 ** STEP)
    delta = -LR * (m_hat / (jnp.sqrt(v_hat) + AEPS) + WD * w)
    return delta, m2, v2


def _adamw(w, g, m, v, name):
    rows, cols = w.shape
    tr = 256 if rows % 256 == 0 and rows > 256 else rows

    def body(w_ref, g_ref, m_ref, v_ref, d_ref, mo_ref, vo_ref):
        d, m2, v2 = _adamw_math(w_ref[...], g_ref[...], m_ref[...], v_ref[...])
        d_ref[...] = d
        mo_ref[...] = m2
        vo_ref[...] = v2

    blk = pl.BlockSpec((tr, cols), lambda i: (i, 0))
    return pl.pallas_call(
        body, name=name, grid=(rows // tr,), in_specs=[blk] * 4, out_specs=[blk] * 3,
        out_shape=[jax.ShapeDtypeStruct((rows, cols), F32)] * 3,
        compiler_params=_cp(("parallel",)),
    )(w, g, m, v)


UPD_TC = 256


def _update(src, land, w, m, v, name):
    rows, cols = land.shape[1:]
    tc = min(UPD_TC, cols)
    me = jnp.reshape(_my_place(), (1,)).astype(jnp.int32)

    def body(me_ref, own_ref, l_ref, w_ref, m_ref, v_ref, g_ref, d_ref, mo_ref, vo_ref):
        del me_ref
        g = own_ref[...].astype(F32)
        for s in range(NDEV - 1):
            g = g + l_ref[s].astype(F32)
        g_ref[...] = g
        d, m2, v2 = _adamw_math(w_ref[...], g, m_ref[...], v_ref[...])
        d_ref[...] = d
        mo_ref[...] = m2
        vo_ref[...] = v2

    wblk = pl.BlockSpec((rows, tc), lambda j, p: (0, j))
    grid_spec = pltpu.PrefetchScalarGridSpec(
        num_scalar_prefetch=1, grid=(cols // tc,),
        in_specs=[pl.BlockSpec((rows, tc), lambda j, p: (p[0], j)),
                  pl.BlockSpec((NDEV - 1, rows, tc), lambda j, p: (0, 0, j)), wblk, wblk, wblk],
        out_specs=[wblk] * 4)
    return pl.pallas_call(
        body, name=name, grid_spec=grid_spec, out_shape=[jax.ShapeDtypeStruct((rows, cols), F32)] * 4,
        compiler_params=_cp(("parallel",)),
    )(me, src, land, w, m, v)


def _small_update(vland, w8, m8, v8, name):
    def body(l_ref, w_ref, m_ref, v_ref, g_ref, d_ref, mo_ref, vo_ref):
        g = l_ref[0]
        for s in range(1, NDEV):
            g = g + l_ref[s]
        g_ref[...] = g
        d, m2, v2 = _adamw_math(w_ref[...], g, m_ref[...], v_ref[...])
        d_ref[...] = d
        mo_ref[...] = m2
        vo_ref[...] = v2

    return pl.pallas_call(
        body, name=name, out_shape=[jax.ShapeDtypeStruct((8, D), F32)] * 4,
        compiler_params=_cp(None),
    )(vland, w8, m8, v8)


def kernel(x, ffn1_norm, ffn1_w_gate, ffn1_w_up, ffn1_w_down, mix_norm, w_in, conv_dw_kernel, conv_dw_bias, conv_ln_gain, conv_ln_bias, conv_w_out, attn_w_out, w_o, ffn2_norm, ffn2_w_gate, ffn2_w_up, ffn2_w_down, final_norm, loss_target, m_ffn1_norm, m_ffn1_w_gate, m_ffn1_w_up, m_ffn1_w_down, m_mix_norm, m_w_in, m_conv_dw_kernel, m_conv_dw_bias, m_conv_ln_gain, m_conv_ln_bias, m_conv_w_out, m_attn_w_out, m_w_o, m_ffn2_norm, m_ffn2_w_gate, m_ffn2_w_up, m_ffn2_w_down, m_final_norm, v_ffn1_norm, v_ffn1_w_gate, v_ffn1_w_up, v_ffn1_w_down, v_mix_norm, v_w_in, v_conv_dw_kernel, v_conv_dw_bias, v_conv_ln_gain, v_conv_ln_bias, v_conv_w_out, v_attn_w_out, v_w_o, v_ffn2_norm, v_ffn2_w_gate, v_ffn2_w_up, v_ffn2_w_down, v_final_norm):
    names = ["ffn1_norm", "ffn1_w_gate", "ffn1_w_up", "ffn1_w_down", "mix_norm", "w_in", "conv_dw_kernel",
             "conv_dw_bias", "conv_ln_gain", "conv_ln_bias", "conv_w_out", "attn_w_out", "w_o", "ffn2_norm",
             "ffn2_w_gate", "ffn2_w_up", "ffn2_w_down", "final_norm"]
    w = dict(ffn1_norm=ffn1_norm, ffn1_w_gate=ffn1_w_gate, ffn1_w_up=ffn1_w_up, ffn1_w_down=ffn1_w_down, mix_norm=mix_norm, w_in=w_in, conv_dw_kernel=conv_dw_kernel, conv_dw_bias=conv_dw_bias, conv_ln_gain=conv_ln_gain, conv_ln_bias=conv_ln_bias, conv_w_out=conv_w_out, attn_w_out=attn_w_out, w_o=w_o, ffn2_norm=ffn2_norm, ffn2_w_gate=ffn2_w_gate, ffn2_w_up=ffn2_w_up, ffn2_w_down=ffn2_w_down, final_norm=final_norm)
    mo = dict(ffn1_norm=m_ffn1_norm, ffn1_w_gate=m_ffn1_w_gate, ffn1_w_up=m_ffn1_w_up, ffn1_w_down=m_ffn1_w_down, mix_norm=m_mix_norm, w_in=m_w_in, conv_dw_kernel=m_conv_dw_kernel, conv_dw_bias=m_conv_dw_bias, conv_ln_gain=m_conv_ln_gain, conv_ln_bias=m_conv_ln_bias, conv_w_out=m_conv_w_out, attn_w_out=m_attn_w_out, w_o=m_w_o, ffn2_norm=m_ffn2_norm, ffn2_w_gate=m_ffn2_w_gate, ffn2_w_up=m_ffn2_w_up, ffn2_w_down=m_ffn2_w_down, final_norm=m_final_norm)
    vo = dict(ffn1_norm=v_ffn1_norm, ffn1_w_gate=v_ffn1_w_gate, ffn1_w_up=v_ffn1_w_up, ffn1_w_down=v_ffn1_w_down, mix_norm=v_mix_norm, w_in=v_w_in, conv_dw_kernel=v_conv_dw_kernel, conv_dw_bias=v_conv_dw_bias, conv_ln_gain=v_conv_ln_gain, conv_ln_bias=v_conv_ln_bias, conv_w_out=v_conv_w_out, attn_w_out=v_attn_w_out, w_o=v_w_o, ffn2_norm=v_ffn2_norm, ffn2_w_gate=v_ffn2_w_gate, ffn2_w_up=v_ffn2_w_up, ffn2_w_down=v_ffn2_w_down, final_norm=v_final_norm)
    col_sharded = ("ffn1_w_gate", "ffn1_w_up", "w_in", "attn_w_out", "ffn2_w_gate", "ffn2_w_up")
    row_sharded = ("ffn1_w_down", "conv_w_out", "w_o", "ffn2_w_down")
    small = ("ffn1_norm", "mix_norm", "ffn2_norm", "final_norm", "conv_dw_bias", "conv_ln_gain", "conv_ln_bias")

    def landing_view(a, n):
        return jnp.transpose(a[0]) if n in col_sharded else a[0]

    def own_view(a, n):
        return jnp.transpose(a)[None] if n in col_sharded else a[None]

    ag_groups = (("ffn1_w_gate", "ffn1_w_up", "ffn1_w_down"),
                 ("w_in", "attn_w_out", "conv_w_out", "w_o", "conv_dw_kernel"),
                 ("ffn2_w_gate", "ffn2_w_up", "ffn2_w_down"))
    ag, order = [], []
    for gi, grp in enumerate(ag_groups):
        lands = _prep_gather([landing_view(w[n], n) for n in grp], order, f"gather_prep{gi}")
        st = _gather_start(lands, GATHER_A, [], f"gather_a_start{gi}")
        ag.append(st)
        order = [st[3]]

    def chips_in(gi, after):
        lands = _gather_wait(ag[gi], GATHER_A, after, f"gather_a_wait{gi}")
        return _gather_start(lands, GATHER_B, [], f"gather_b_start{gi}")

    def all_in(gi, st, after):
        return _gather_wait(st, GATHER_B, after, f"gather_b_wait{gi}")

    x0 = x[0]
    tgt = loss_target[0]
    gf = final_norm.reshape(1, D)

    wg1, wu1, wd1 = all_in(0, chips_in(0, [ag[2][3]]), [])
    x1, gg1, uu1 = _ffn_fwd(x0, ffn1_norm, wg1, wu1, wd1, "ffn1_fwd")
    win_t, wa_t, wc, wo, kern_blocks = all_in(1, chips_in(1, [x1]), [])
    kern = kern_blocks.reshape(NDEV, 32, D // NDEV).transpose(1, 0, 2).reshape(32, D)
    h2p = _norm_cast(x1, mix_norm, "mix_norm_fwd")
    h2 = h2p[0]
    ab = _mm(h2, win_t, mode="nt", m=T, n=2 * D, k=D, tm=1024, tn=512, tk=D, out_dtype=BF16, name="proj_conv")
    gates = _mm(h2, win_t, mode="nt", m=T, n=2 * D, k=D, tm=1024, tn=512, tk=D, out_dtype=BF16,
                b_map=lambda i, j, kk: (13 + j, 0), name="proj_gates")
    qkv = []
    for gi in range(len(GROUPS)):
        qkv.append(_mm(h2p[gi], win_t, mode="nt", m=T, n=3 * AW, k=D, tm=1024, tn=AW, tk=D, out_dtype=BF16,
                       b_map=lambda i, j, kk, gi=gi: (4 + gi + 3 * j, 0), name=f"proj_qkv{gi}"))
    z1, z3b = _conv_fwd(ab, kern, conv_dw_bias, conv_ln_gain, conv_ln_bias, "conv_fwd")
    ffn2_b = chips_in(2, [z3b])
    outs, lses = [], []
    for gi, (_, dil) in enumerate(GROUPS):
        o, l = _attn_fwd(qkv[gi], gi, f"attn_fwd{gi}")
        outs.append(o)
        lses.append(l)
    attn, attnb, lse = _merge(outs, lses, "attn_merge")
    x2, yc, ya, mixedb = _mix_out(z3b, attnb, gates, wc, wa_t, wo, x1, "mix_out_fwd")
    wg2, wu2, wd2 = all_in(2, ffn2_b, [x2])
    x3, gg2, uu2 = _ffn_fwd(x2, ffn2_norm, wg2, wu2, wd2, "ffn2_fwd")

    dx3, dgf, loss_part = _final(x3, gf, tgt, "final_norm_loss")
    dx2, dg3, dgb, dub, actb, hb, dob = _ffn_bwd(x2, ffn2_norm, gg2, uu2, dx3, wg2, wu2, wd2, "ffn2_bwd")
    grads = {}
    grads["ffn2_w_gate"] = _wgrad(dgb, hb, FF, D, "ffn2_dwg")
    grads["ffn2_w_up"] = _wgrad(dub, hb, FF, D, "ffn2_dwu")
    grads["ffn2_w_down"] = _wgrad(actb, dob, FF, D, "ffn2_dwd")
    rs_groups = [("ffn2_w_gate", "ffn2_w_up", "ffn2_w_down"),
                 ("attn_w_out", "conv_w_out", "w_o", "conv_dw_kernel"),
                 ("w_in",),
                 ("ffn1_w_gate",), ("ffn1_w_up",), ("ffn1_w_down",), ()]
    last = len(rs_groups) - 1
    rs = [_send_start(["scatter"] * 3, [grads[n] for n in rs_groups[0]], [], "scatter_start0")]
    dx2 = _tie(dx2, [rs[0][4]], "tie_after_scatter0")

    dgates, dycb, dyab, dx2b, dz3, dattnb, delta = _mix_out_bwd(dx2, gates, yc, ya, attn, wc, wa_t, wo, "mix_out_bwd")
    grads["w_o"] = _wgrad(mixedb, dx2b, D, D, "dw_o")
    grads["conv_w_out"] = _wgrad(z3b, dycb, D, D, "dw_conv_out")
    grads["attn_w_out"] = _wgrad(dyab, attnb, D, AW, "dw_attn_out")
    dab, dkern, dvec = _conv_bwd(dz3, z1, ab, kern, conv_ln_gain, conv_ln_bias, "conv_bwd")
    grads["conv_dw_kernel"] = dkern.reshape(32, NDEV, D // NDEV).transpose(1, 0, 2).reshape(NDEV * 32, D // NDEV)
    rs.append(_send_start(["scatter"] * 4, [grads[n] for n in rs_groups[1]], [rs[0][4]], "scatter_start1"))
    dattnb = [_tie(a, [rs[1][4]], f"tie_after_scatter1_{i}") for i, a in enumerate(dattnb)]

    dqkv = []
    for gi, (_, dil) in enumerate(GROUPS):
        dq3 = _attn_bwd(qkv[gi], dattnb[gi], lse[gi], delta[gi], gi, f"attn_bwd{gi}")
        dqkv.append(dq3.reshape(3 * T, AW))

    dwin = _mm(dab, h2, mode="tn", m=2 * D, n=D, k=T, tm=2 * D, tn=D, tk=512, out_dtype=BF16, out_rows=IN_W,
               name="dw_in_conv")
    dwin = _mm(dgates, h2, mode="tn", m=2 * D, n=D, k=T, tm=512, tn=D, tk=1024, out_dtype=BF16, out_rows=IN_W,
               o_map=lambda i, j, kk: (13 + i, 0), passthru=dwin, name="dw_in_gates")
    for gi in range(3):
        dwin = _mm(dqkv[gi], h2p[gi], mode="tn", m=3 * AW, n=D, k=T, tm=AW, tn=D, tk=1024, out_dtype=BF16,
                   out_rows=IN_W, a_map=lambda i, j, kk: (i * (T // 1024) + kk, 0),
                   o_map=lambda i, j, kk, gi=gi: (4 + gi + 3 * i, 0), passthru=dwin, name=f"dw_in_qkv{gi}")
    grads["w_in"] = dwin
    rs.append(_send_start(["scatter"], [dwin], [rs[1][4]], "scatter_start2"))
    dab = _tie(dab, [rs[2][4]], "tie_after_scatter2")

    nrow = T // 1024
    dh = _mm(dab, win_t, mode="nn", m=T, n=D, k=2 * D, tm=1024, tn=D, tk=512, out_dtype=F32, name="dproj_conv")
    dh = _mm(dgates, win_t, mode="nn", m=T, n=D, k=2 * D, tm=1024, tn=D, tk=512, out_dtype=F32,
             b_map=lambda i, j, kk: (13 + kk, 0), init=dh, name="dproj_gates")
    dhs = []
    for gi, (_, dil) in enumerate(GROUPS):
        part = _mm(dqkv[gi], win_t, mode="nn", m=T, n=D, k=3 * AW, tm=1024, tn=D, tk=AW, out_dtype=F32,
                   a_map=lambda i, j, kk: (kk * nrow + i, 0), b_map=lambda i, j, kk, gi=gi: (4 + gi + 3 * kk, 0),
                   init=dh if gi == 0 else None, name=f"dproj_qkv{gi}")
        dhs.append(part)
    dx1, dg2 = _rms_bwd(x1, mix_norm, dhs, dx2, "mix_norm_bwd")

    dgb, dub, actb, hb, dob = _ffn_bwd_pre(x0, ffn1_norm, gg1, uu1, dx1, wd1, "ffn1_bwd_pre")
    grads["ffn1_w_gate"] = _wgrad(dgb, hb, FF, D, "ffn1_dwg")
    rs.append(_send_start(["scatter"], [grads["ffn1_w_gate"]], [rs[2][4]], "scatter_start3"))
    hb = _tie(hb, [rs[3][4]], "tie_after_scatter3")
    grads["ffn1_w_up"] = _wgrad(dub, hb, FF, D, "ffn1_dwu")
    rs.append(_send_start(["scatter"], [grads["ffn1_w_up"]], [rs[3][4]], "scatter_start4"))
    dob = _tie(dob, [rs[4][4]], "tie_after_scatter4")
    grads["ffn1_w_down"] = _wgrad(actb, dob, FF, D, "ffn1_dwd")
    rs.append(_send_start(["scatter"], [grads["ffn1_w_down"]], [rs[4][4]], "scatter_start5"))
    dgb = _tie(dgb, [rs[5][4]], "tie_after_scatter5")
    dx0, dg1 = _ffn_bwd_dx(x0, ffn1_norm, dgb, dub, dx1, wg1, wu1, "ffn1_bwd_dx")
    vec = jnp.concatenate([dg1, dg2, dg3, dgf, dvec[0:3], jnp.broadcast_to(loss_part[:, :1], (1, D))], axis=0)
    rs.append(_send_start(["bcast"], [vec], [rs[5][4]], "scatter_start6"))

    g_out, d_out, m_out, v_out = {}, {}, {}, {}
    me = _my_place()
    after = [rs[last][4]]
    for gi, grp in enumerate(rs_groups):
        kinds = ["scatter"] * len(grp) + (["bcast"] if gi == last else [])
        srcs, lands = _send_wait(kinds, rs[gi], after, f"scatter_wait{gi}")
        for n, src, land in zip(grp, srcs, lands):
            if n == "conv_dw_kernel":
                rows = src.shape[0] // NDEV
                own = lax.dynamic_slice(src, (me * rows, 0), (rows, src.shape[1]))
                g = _gsum(own, land, f"gsum_{n}")[:CONV_W]
                d, m2, v2 = _adamw(w[n][0], g, mo[n][0], vo[n][0], f"adamw_{n}")
                after = [d]
                g, d, m2, v2 = g[None], d[None], m2[None], v2[None]
            else:
                res = _update(src, land, landing_view(w[n], n), landing_view(mo[n], n), landing_view(vo[n], n),
                              f"update_{n}")
                after = [res[1]]
                g, d, m2, v2 = (own_view(a, n) for a in res)
            g_out[n], d_out[n], m_out[n], v_out[n] = g, d, m2, v2
    vland = lands[-1]

    def rows8(src):
        return jnp.concatenate([src[n].reshape(1, D) for n in small] + [jnp.ones((1, D), F32)], axis=0)

    g8, d8, m8, v8 = _small_update(vland, rows8(w), rows8(mo), rows8(vo), "small_update")
    for r, n in enumerate(small):
        shp = w[n].shape
        g_out[n], d_out[n], m_out[n], v_out[n] = (a[r].reshape(shp) for a in (g8, d8, m8, v8))
    loss = g8[7, 0]

    return (loss, dx0[None], *[g_out[n] for n in names], *[d_out[n] for n in names],
            *[m_out[n] for n in names], *[v_out[n] for n in names])
```

```python
import numpy as np
import jax
import jax.numpy as jnp
from jax import lax
from jax.experimental import pallas as pl
from jax.experimental.pallas import tpu as pltpu

F32 = jnp.float32
BF16 = jnp.bfloat16

T = 4096
D = 1024
FF = 2816
NDEV = 8
CONV_W = 31
HEAD = 128
BLK = 128
GROUPS = ((128, 1), (512, 4), (2048, 16))
NHG = 4
AW = NHG * HEAD
IN_W = 2 * D + 3 * 3 * AW + 2 * D
EPS = 1e-6
B1, B2, LR, AEPS, WD, STEP = 0.9, 0.999, 0.001, 1e-08, 0.01, 10
NEG = -1e30
VMEM_LIMIT = 56 * 1024 * 1024
MESH_ID = pl.DeviceIdType.MESH

NT = (((1,), (1,)), ((), ()))
NN = (((1,), (0,)), ((), ()))
TN = (((0,), (0,)), ((), ()))
_DIMS = {"nn": NN, "nt": NT, "tn": TN}


def _cp(sem=None):
    return pltpu.CompilerParams(dimension_semantics=sem, vmem_limit_bytes=VMEM_LIMIT)


def _sig(v):
    return 1.0 / (1.0 + jnp.exp(-v))


def _dot(a, b, dims):
    return lax.dot_general(a, b, dims, preferred_element_type=F32)


def _const_spec(shape):
    nd = len(shape)
    return pl.BlockSpec(shape, lambda *_: (0,) * nd)


def _mm(a, b, *, mode, m, n, k, tm, tn, tk, out_dtype, name, a_map=None, b_map=None,
        o_map=None, out_rows=None, init=None, passthru=None):
    gi, gj, gk = m // tm, n // tn, k // tk
    assert gi * tm == m and gj * tn == n and gk * tk == k, (name, m, n, k, tm, tn, tk)
    if mode == "nn":
        a_blk, b_blk = (tm, tk), (tk, tn)
        da, db = (lambda i, j, kk: (i, kk)), (lambda i, j, kk: (kk, j))
    elif mode == "nt":
        a_blk, b_blk = (tm, tk), (tn, tk)
        da, db = (lambda i, j, kk: (i, kk)), (lambda i, j, kk: (j, kk))
    else:
        a_blk, b_blk = (tk, tm), (tk, tn)
        da, db = (lambda i, j, kk: (kk, i)), (lambda i, j, kk: (kk, j))
    a_map = a_map or da
    b_map = b_map or db
    o_map = o_map or (lambda i, j, kk: (i, j))
    dims = _DIMS[mode]
    extra = init if init is not None else passthru
    out_rows = out_rows or m

    def body(*refs):
        if init is not None:
            a_ref, b_ref, i_ref, o_ref = refs[:4]
        elif passthru is not None:
            a_ref, b_ref, _, o_ref = refs[:4]
        else:
            a_ref, b_ref, o_ref = refs[:3]
        if gk == 1:
            prod = _dot(a_ref[...], b_ref[...], dims)
            if init is not None:
                prod = prod + i_ref[...].astype(F32)
            o_ref[...] = prod.astype(out_dtype)
            return
        acc = refs[-1]
        kk = pl.program_id(2)

        @pl.when(kk == 0)
        def _():
            if init is not None:
                acc[...] = i_ref[...].astype(F32)
            else:
                acc[...] = jnp.zeros_like(acc)

        acc[...] += _dot(a_ref[...], b_ref[...], dims)

        @pl.when(kk == gk - 1)
        def _():
            o_ref[...] = acc[...].astype(out_dtype)

    in_specs = [pl.BlockSpec(a_blk, a_map), pl.BlockSpec(b_blk, b_map)]
    args = [a, b]
    aliases = {}
    if init is not None:
        in_specs.append(pl.BlockSpec((tm, tn), o_map))
        args.append(init)
        aliases = {2: 0}
    elif passthru is not None:
        in_specs.append(pl.BlockSpec(memory_space=pl.ANY))
        args.append(passthru)
        aliases = {2: 0}
    out_dt = extra.dtype if extra is not None else out_dtype
    assert out_dt == out_dtype
    return pl.pallas_call(
        body, name=name, grid=(gi, gj, gk),
        in_specs=in_specs, out_specs=pl.BlockSpec((tm, tn), o_map),
        out_shape=jax.ShapeDtypeStruct((out_rows, n), out_dtype),
        scratch_shapes=[pltpu.VMEM((tm, tn), F32)] if gk > 1 else [],
        input_output_aliases=aliases,
        compiler_params=_cp(("parallel", "parallel", "arbitrary")),
    )(*args)


def _ffn_fwd(x, g, wg_t, wu_t, wd, name, next_gain=None, loss_of=None):
    tm, fc = PERM_TM, 256
    nc = FF // fc
    n_in = 5 + (1 if next_gain is not None else 0) + (2 if loss_of is not None else 0)

    def body(*refs):
        x_ref, g_ref, wg_ref, wu_ref, wd_ref = refs[:5]
        extra_in, outs = refs[5:n_in], refs[n_in:]
        act_ref = outs[-1]
        xv = x_ref[...]
        r = lax.rsqrt(jnp.mean(xv * xv, axis=-1, keepdims=True) + EPS)
        h = (xv * r * g_ref[...]).astype(BF16)
        gg_ref, uu_ref = (outs[0], outs[1]) if loss_of is not None else (outs[1], outs[2])
        for c in range(nc):
            sl = pl.ds(c * fc, fc)
            gg = _dot(h, wg_ref[sl, :], NT)
            uu = _dot(h, wu_ref[sl, :], NT)
            gg_ref[:, sl] = gg.astype(BF16)
            uu_ref[:, sl] = uu.astype(BF16)
            act_ref[:, sl] = (gg * _sig(gg) * uu).astype(BF16)
        y = xv + 0.5 * _dot(act_ref[...], wd_ref[...], NN)
        if loss_of is not None:
            _final_math(y, extra_in[0][...], extra_in[1][...], outs[2], outs[3], outs[4], pl.program_id(0))
            return
        outs[0][...] = y
        if next_gain is not None:
            tile = outs[-2]
            r2 = lax.rsqrt(jnp.mean(y * y, axis=-1, keepdims=True) + EPS)
            hv = y * r2 * extra_in[0][...]
            outs[3][...] = hv.astype(BF16)
            _put_tile(tile, hv)
            for dil, p_ref in zip(DILS, outs[4:4 + len(DILS)]):
                _store_perm(p_ref, tile, dil)

    wspec = pl.BlockSpec((FF, D), lambda i: (0, 0), pipeline_mode=pl.Buffered(1))
    row_d = pl.BlockSpec((tm, D), lambda i: (i, 0))
    row_f = pl.BlockSpec((tm, FF), lambda i: (i, 0))
    in_specs = [row_d, _const_spec((1, D)), wspec, wspec, wspec]
    args = [x, g, wg_t, wu_t, wd]
    f_shape = jax.ShapeDtypeStruct((T, FF), BF16)
    scratch = [pltpu.VMEM((tm, FF), BF16)]
    if loss_of is not None:
        in_specs += [_const_spec((1, D)), row_d]
        args += list(loss_of)
        out_specs = [row_f, row_f, row_d, _const_spec((1, D)), _const_spec((1, 128))]
        out_shape = [f_shape, f_shape, jax.ShapeDtypeStruct((T, D), F32), jax.ShapeDtypeStruct((1, D), F32),
                     jax.ShapeDtypeStruct((1, 128), F32)]
    else:
        out_specs = [row_d, row_f, row_f]
        out_shape = [jax.ShapeDtypeStruct((T, D), F32), f_shape, f_shape]
        if next_gain is not None:
            in_specs.append(_const_spec((1, D)))
            args.append(next_gain)
            out_specs += [row_d] + [_perm_spec(d, D) for d in DILS]
            out_shape += [jax.ShapeDtypeStruct((T, D), BF16)] + [_perm_shape(d, D, BF16) for d in DILS]
            scratch = [_tile_scratch(D)] + scratch
    out = pl.pallas_call(
        body, name=name, grid=(T // tm,), in_specs=in_specs, out_specs=out_specs, out_shape=out_shape,
        scratch_shapes=scratch,
        compiler_params=_cp(("arbitrary",) if loss_of is not None else ("parallel",)),
    )(*args)
    if next_gain is not None:
        return out[0], out[1], out[2], [out[3]] + [o.reshape(T, D) for o in out[4:]]
    return tuple(out)


def _ffn_bwd(x, g, gg_all, uu_all, dout, wg_t, wu_t, wd, name):
    tm, fc = 256, 256
    nc = FF // fc

    def body(x_ref, g_ref, gg_ref, uu_ref, do_ref, wg_ref, wu_ref, wd_ref,
             dx_ref, dgam_ref, dg_ref, du_ref, act_ref, h_ref, db_ref):
        i = pl.program_id(0)
        xv = x_ref[...]
        r = lax.rsqrt(jnp.mean(xv * xv, axis=-1, keepdims=True) + EPS)
        xhat = xv * r
        gam = g_ref[...]
        h_ref[...] = (xhat * gam).astype(BF16)
        dov = do_ref[...]
        dbv = (0.5 * dov).astype(BF16)
        db_ref[...] = dbv
        for c in range(nc):
            sl = pl.ds(c * fc, fc)
            da = _dot(dbv, wd_ref[sl, :], NT)
            gg = gg_ref[:, sl].astype(F32)
            uu = uu_ref[:, sl].astype(F32)
            s = _sig(gg)
            si = gg * s
            dgv = (da * uu * (s * (1.0 + gg * (1.0 - s)))).astype(BF16)
            duv = (da * si).astype(BF16)
            dg_ref[:, sl] = dgv
            du_ref[:, sl] = duv
            act_ref[:, sl] = (si * uu).astype(BF16)
        dh = _dot(dg_ref[...], wg_ref[...], NN) + _dot(du_ref[...], wu_ref[...], NN)

        @pl.when(i == 0)
        def _():
            dgam_ref[...] = jnp.zeros_like(dgam_ref)

        dgam_ref[...] += jnp.sum(dh * xhat, axis=0, keepdims=True)
        dxh = dh * gam
        dx_ref[...] = dov + r * (dxh - xhat * jnp.mean(dxh * xhat, axis=-1, keepdims=True))

    wspec = pl.BlockSpec((FF, D), lambda i: (0, 0), pipeline_mode=pl.Buffered(1))
    row_d = pl.BlockSpec((tm, D), lambda i: (i, 0))
    row_f = pl.BlockSpec((tm, FF), lambda i: (i, 0))
    return pl.pallas_call(
        body, name=name, grid=(T // tm,),
        in_specs=[row_d, _const_spec((1, D)), row_f, row_f, row_d, wspec, wspec, wspec],
        out_specs=[row_d, _const_spec((1, D)), row_f, row_f, row_f, row_d, row_d],
        out_shape=[jax.ShapeDtypeStruct((T, D), F32), jax.ShapeDtypeStruct((1, D), F32),
                   jax.ShapeDtypeStruct((T, FF), BF16), jax.ShapeDtypeStruct((T, FF), BF16),
                   jax.ShapeDtypeStruct((T, FF), BF16), jax.ShapeDtypeStruct((T, D), BF16),
                   jax.ShapeDtypeStruct((T, D), BF16)],
        compiler_params=_cp(("arbitrary",)),
    )(x, g, gg_all, uu_all, dout, wg_t, wu_t, wd)


def _ffn_bwd_pre(x, g, gg_all, uu_all, dout, wd, name):
    tm, fc = 512, 256
    nc = FF // fc

    def body(x_ref, g_ref, gg_ref, uu_ref, do_ref, wd_ref, dg_ref, du_ref, act_ref, h_ref, db_ref):
        xv = x_ref[...]
        r = lax.rsqrt(jnp.mean(xv * xv, axis=-1, keepdims=True) + EPS)
        h_ref[...] = (xv * r * g_ref[...]).astype(BF16)
        dbv = (0.5 * do_ref[...]).astype(BF16)
        db_ref[...] = dbv
        for c in range(nc):
            sl = pl.ds(c * fc, fc)
            da = _dot(dbv, wd_ref[sl, :], NT)
            gg = gg_ref[:, sl].astype(F32)
            uu = uu_ref[:, sl].astype(F32)
            s = _sig(gg)
            si = gg * s
            dg_ref[:, sl] = (da * uu * (s * (1.0 + gg * (1.0 - s)))).astype(BF16)
            du_ref[:, sl] = (da * si).astype(BF16)
            act_ref[:, sl] = (si * uu).astype(BF16)

    wspec = pl.BlockSpec((FF, D), lambda i: (0, 0), pipeline_mode=pl.Buffered(1))
    row_d = pl.BlockSpec((tm, D), lambda i: (i, 0))
    row_f = pl.BlockSpec((tm, FF), lambda i: (i, 0))
    return pl.pallas_call(
        body, name=name, grid=(T // tm,),
        in_specs=[row_d, _const_spec((1, D)), row_f, row_f, row_d, wspec],
        out_specs=[row_f, row_f, row_f, row_d, row_d],
        out_shape=[jax.ShapeDtypeStruct((T, FF), BF16), jax.ShapeDtypeStruct((T, FF), BF16),
                   jax.ShapeDtypeStruct((T, FF), BF16), jax.ShapeDtypeStruct((T, D), BF16),
                   jax.ShapeDtypeStruct((T, D), BF16)],
        compiler_params=_cp(("parallel",)),
    )(x, g, gg_all, uu_all, dout, wd)


def _ffn_bwd_dx(x, g, dgb, dub, dout, wg_t, wu_t, name):
    tm = 512

    def body(x_ref, g_ref, dg_ref, du_ref, do_ref, wg_ref, wu_ref, dx_ref, dgam_ref):
        i = pl.program_id(0)
        xv = x_ref[...]
        r = lax.rsqrt(jnp.mean(xv * xv, axis=-1, keepdims=True) + EPS)
        xhat = xv * r
        gam = g_ref[...]
        dh = _dot(dg_ref[...], wg_ref[...], NN) + _dot(du_ref[...], wu_ref[...], NN)

        @pl.when(i == 0)
        def _():
            dgam_ref[...] = jnp.zeros_like(dgam_ref)

        dgam_ref[...] += jnp.sum(dh * xhat, axis=0, keepdims=True)
        dxh = dh * gam
        dx_ref[...] = do_ref[...] + r * (dxh - xhat * jnp.mean(dxh * xhat, axis=-1, keepdims=True))

    wspec = pl.BlockSpec((FF, D), lambda i: (0, 0), pipeline_mode=pl.Buffered(1))
    row_d = pl.BlockSpec((tm, D), lambda i: (i, 0))
    row_f = pl.BlockSpec((tm, FF), lambda i: (i, 0))
    return pl.pallas_call(
        body, name=name, grid=(T // tm,),
        in_specs=[row_d, _const_spec((1, D)), row_f, row_f, row_d, wspec, wspec],
        out_specs=[row_d, _const_spec((1, D))],
        out_shape=[jax.ShapeDtypeStruct((T, D), F32), jax.ShapeDtypeStruct((1, D), F32)],
        compiler_params=_cp(("arbitrary",)),
    )(x, g, dgb, dub, dout, wg_t, wu_t)


def _wgrad(a, b, m, n, name):
    tm = m // 2 if m == FF else m
    return _mm(a, b, mode="tn", m=m, n=n, k=T, tm=tm, tn=n, tk=min(T, 2048), out_dtype=BF16, name=name)


PERM_TM = 512
DILS = tuple(d for _, d in GROUPS if d > 1)


def _perm_spec(dil, cols):
    return pl.BlockSpec((dil, PERM_TM // dil, cols), lambda i: (0, i, 0))


def _perm_shape(dil, cols, dtype):
    return jax.ShapeDtypeStruct((dil, T // dil, cols), dtype)


LANES = 128


def _tile_scratch(cols):
    return pltpu.VMEM((cols // LANES, PERM_TM, LANES), F32)


def _put_tile(tile, value):
    for c in range(tile.shape[0]):
        tile[c] = value[:, c * LANES:(c + 1) * LANES]


def _get_tile(tile):
    return jnp.concatenate([tile[c] for c in range(tile.shape[0])], axis=1)


def _store_perm(out_ref, tile, dil):
    for r in range(dil):
        for c in range(tile.shape[0]):
            out_ref[r, :, pl.ds(c * LANES, LANES)] = tile[c, pl.ds(r, PERM_TM // dil, stride=dil), :].astype(
                out_ref.dtype)


def _load_unperm(in_ref, tile, dil):
    for r in range(dil):
        for c in range(tile.shape[0]):
            tile[c, pl.ds(r, PERM_TM // dil, stride=dil), :] = in_ref[r, :, pl.ds(c * LANES, LANES)].astype(F32)


def _norm_cast(x, g, name):
    tm = PERM_TM

    def body(x_ref, g_ref, h_ref, *rest):
        p_refs, tile = rest[:-1], rest[-1]
        xv = x_ref[...]
        r = lax.rsqrt(jnp.mean(xv * xv, axis=-1, keepdims=True) + EPS)
        hv = xv * r * g_ref[...]
        h_ref[...] = hv.astype(BF16)
        _put_tile(tile, hv)
        for dil, p_ref in zip(DILS, p_refs):
            _store_perm(p_ref, tile, dil)

    out = pl.pallas_call(
        body, name=name, grid=(T // tm,),
        in_specs=[pl.BlockSpec((tm, D), lambda i: (i, 0)), _const_spec((1, D))],
        out_specs=[pl.BlockSpec((tm, D), lambda i: (i, 0))] + [_perm_spec(d, D) for d in DILS],
        out_shape=[jax.ShapeDtypeStruct((T, D), BF16)] + [_perm_shape(d, D, BF16) for d in DILS],
        scratch_shapes=[_tile_scratch(D)],
        compiler_params=_cp(("parallel",)),
    )(x, g)
    return [out[0]] + [o.reshape(T, D) for o in out[1:]]


def _final_math(xv, gam, tgt, dx_ref, dgam_ref, loss_ref, i):
    r = lax.rsqrt(jnp.mean(xv * xv, axis=-1, keepdims=True) + EPS)
    xhat = xv * r
    err = xhat * gam - tgt
    part = 0.5 * jnp.sum(jnp.mean(err * err, axis=-1, keepdims=True), axis=0, keepdims=True)
    dy = err * (1.0 / D)

    @pl.when(i == 0)
    def _():
        dgam_ref[...] = jnp.zeros_like(dgam_ref)
        loss_ref[...] = jnp.zeros_like(loss_ref)

    dgam_ref[...] += jnp.sum(dy * xhat, axis=0, keepdims=True)
    loss_ref[...] += jnp.broadcast_to(part, loss_ref.shape)
    dxh = dy * gam
    dx_ref[...] = r * (dxh - xhat * jnp.mean(dxh * xhat, axis=-1, keepdims=True))


def _rms_bwd(x, g, dhs, dres, name):
    tm = PERM_TM
    dils = [d for _, d in GROUPS]
    nh = len(dhs)
    assert nh == len(dils)

    def body(*refs):
        x_ref, g_ref = refs[:2]
        dh_refs = refs[2:2 + nh]
        dr_ref, dx_ref, dgam_ref, tile = refs[2 + nh:]
        i = pl.program_id(0)
        xv = x_ref[...]
        r = lax.rsqrt(jnp.mean(xv * xv, axis=-1, keepdims=True) + EPS)
        xhat = xv * r
        gam = g_ref[...]
        dh = None
        for dil, ref in zip(dils, dh_refs):
            if dil == 1:
                part = ref[...]
            else:
                _load_unperm(ref, tile, dil)
                part = _get_tile(tile)
            dh = part if dh is None else dh + part

        @pl.when(i == 0)
        def _():
            dgam_ref[...] = jnp.zeros_like(dgam_ref)

        dgam_ref[...] += jnp.sum(dh * xhat, axis=0, keepdims=True)
        dxh = dh * gam
        dx_ref[...] = dr_ref[...] + r * (dxh - xhat * jnp.mean(dxh * xhat, axis=-1, keepdims=True))

    row_d = pl.BlockSpec((tm, D), lambda i: (i, 0))
    dh_specs = [row_d if d == 1 else _perm_spec(d, D) for d in dils]
    dh_args = [a if d == 1 else a.reshape(d, T // d, D) for d, a in zip(dils, dhs)]
    return pl.pallas_call(
        body, name=name, grid=(T // tm,),
        in_specs=[row_d, _const_spec((1, D))] + dh_specs + [row_d],
        out_specs=[row_d, _const_spec((1, D))],
        out_shape=[jax.ShapeDtypeStruct((T, D), F32), jax.ShapeDtypeStruct((1, D), F32)],
        scratch_shapes=[_tile_scratch(D)],
        compiler_params=_cp(("arbitrary",)),
    )(x, g, *dh_args, dres)


CONV_TM = 256
CONV_HALO = 32
CONV_RB = 16


def _glu(ab):
    ab = ab.astype(F32)
    return ab[:, :D] * _sig(ab[:, D:])


def _ln_stats(z1):
    mu = jnp.mean(z1, axis=-1, keepdims=True)
    zc = z1 - mu
    rstd = lax.rsqrt(jnp.mean(zc * zc, axis=-1, keepdims=True) + EPS)
    return zc * rstd, rstd


def _fill_shifts(zs):
    n = zs.shape[1] - 8
    for s in range(1, 8):
        zs[s, pl.ds(0, n), :] = zs[0, pl.ds(s, n), :]


def _shifted(zs, start, rows):
    q, s = divmod(start, 8)
    return zs[s, pl.ds(8 * q, rows), :]


def _conv_fwd(ab, kern, dwb, lng, lnb, name):
    tm, hl, rb = CONV_TM, CONV_HALO, CONV_RB
    off = hl - (CONV_W - 1)

    def body(ab_ref, abh_ref, k_ref, dwb_ref, lng_ref, lnb_ref, z1_ref, z3_ref, zs):
        i = pl.program_id(0)
        zs[0, pl.ds(0, hl), :] = jnp.where(i > 0, _glu(abh_ref[...]), 0.0)
        zs[0, pl.ds(hl, tm), :] = _glu(ab_ref[...])
        _fill_shifts(zs)
        for b in range(tm // rb):
            acc = jnp.zeros((rb, D), F32)
            for j in range(CONV_W):
                acc = acc + _shifted(zs, b * rb + off + j, rb) * k_ref[pl.ds(j, 1), :]
            z1 = acc + dwb_ref[...]
            z1_ref[pl.ds(b * rb, rb), :] = z1
            zn, _ = _ln_stats(z1)
            z2 = zn * lng_ref[...] + lnb_ref[...]
            z3_ref[pl.ds(b * rb, rb), :] = (z2 * _sig(z2)).astype(BF16)

    row = pl.BlockSpec((tm, D), lambda i: (i, 0))
    return pl.pallas_call(
        body, name=name, grid=(T // tm,),
        in_specs=[pl.BlockSpec((tm, 2 * D), lambda i: (i, 0)),
                  pl.BlockSpec((hl, 2 * D), lambda i: (jnp.maximum(i * (tm // hl) - 1, 0), 0)),
                  _const_spec((32, D)), _const_spec((1, D)), _const_spec((1, D)), _const_spec((1, D))],
        out_specs=[row, row],
        out_shape=[jax.ShapeDtypeStruct((T, D), F32), jax.ShapeDtypeStruct((T, D), BF16)],
        scratch_shapes=[pltpu.VMEM((8, hl + tm, D), F32)],
        compiler_params=_cp(("parallel",)),
    )(ab, ab, kern, dwb, lng, lnb)


def _conv_bwd(dz3, z1, ab, kern, lng, lnb, name):
    tm, hl, rb = CONV_TM, CONV_HALO, CONV_RB
    off = hl - (CONV_W - 1)
    nsteps = T // tm

    def ln_bwd(dz3v, z1v, lngv, lnbv):
        zn, rstd = _ln_stats(z1v)
        z2 = zn * lngv + lnbv
        s = _sig(z2)
        dz2 = dz3v * (s * (1.0 + z2 * (1.0 - s)))
        dzn = dz2 * lngv
        dz1 = rstd * (dzn - jnp.mean(dzn, axis=-1, keepdims=True)
                      - zn * jnp.mean(dzn * zn, axis=-1, keepdims=True))
        return dz1, dz2, zn

    def body(dz3_ref, dz3h_ref, z1_ref, z1h_ref, ab_ref, abh_ref, k_ref, lng_ref, lnb_ref,
             dab_ref, dk_ref, dvec_ref, zs, dzs):
        i = pl.program_id(0)
        lngv, lnbv = lng_ref[...], lnb_ref[...]

        @pl.when(i == 0)
        def _():
            dk_ref[...] = jnp.zeros_like(dk_ref)
            dvec_ref[...] = jnp.zeros_like(dvec_ref)

        dz1, dz2, zn = ln_bwd(dz3_ref[...], z1_ref[...], lngv, lnbv)
        dvec_ref[pl.ds(0, 1), :] += jnp.sum(dz1, axis=0, keepdims=True)
        dvec_ref[pl.ds(1, 1), :] += jnp.sum(dz2 * zn, axis=0, keepdims=True)
        dvec_ref[pl.ds(2, 1), :] += jnp.sum(dz2, axis=0, keepdims=True)
        dzs[0, pl.ds(0, tm), :] = dz1
        dz1h, _, _ = ln_bwd(dz3h_ref[...], z1h_ref[...], lngv, lnbv)
        dzs[0, pl.ds(tm, hl), :] = jnp.where(i < nsteps - 1, dz1h, 0.0)
        _fill_shifts(dzs)
        zs[0, pl.ds(0, hl), :] = jnp.where(i > 0, _glu(abh_ref[...]), 0.0)
        zs[0, pl.ds(hl, tm), :] = _glu(ab_ref[...])
        _fill_shifts(zs)

        for j in range(CONV_W):
            tot = jnp.zeros((rb, D), F32)
            for b in range(tm // rb):
                tot = tot + dzs[0, pl.ds(b * rb, rb), :] * _shifted(zs, b * rb + off + j, rb)
            dk_ref[pl.ds(j, 1), :] += jnp.sum(tot, axis=0, keepdims=True)

        for b in range(tm // rb):
            acc = jnp.zeros((rb, D), F32)
            for j in range(CONV_W):
                acc = acc + _shifted(dzs, b * rb + (CONV_W - 1) - j, rb) * k_ref[pl.ds(j, 1), :]
            av = ab_ref[pl.ds(b * rb, rb), pl.ds(0, D)].astype(F32)
            sb = _sig(ab_ref[pl.ds(b * rb, rb), pl.ds(D, D)].astype(F32))
            dab_ref[pl.ds(b * rb, rb), pl.ds(0, D)] = (acc * sb).astype(BF16)
            dab_ref[pl.ds(b * rb, rb), pl.ds(D, D)] = (acc * av * sb * (1.0 - sb)).astype(BF16)

    row = pl.BlockSpec((tm, D), lambda i: (i, 0))
    nxt = pl.BlockSpec((hl, D), lambda i: (jnp.minimum((i + 1) * (tm // hl), T // hl - 1), 0))
    return pl.pallas_call(
        body, name=name, grid=(nsteps,),
        in_specs=[row, nxt, row, nxt,
                  pl.BlockSpec((tm, 2 * D), lambda i: (i, 0)),
                  pl.BlockSpec((hl, 2 * D), lambda i: (jnp.maximum(i * (tm // hl) - 1, 0), 0)),
                  _const_spec((32, D)), _const_spec((1, D)), _const_spec((1, D))],
        out_specs=[pl.BlockSpec((tm, 2 * D), lambda i: (i, 0)), _const_spec((32, D)), _const_spec((8, D))],
        out_shape=[jax.ShapeDtypeStruct((T, 2 * D), BF16), jax.ShapeDtypeStruct((32, D), F32),
                   jax.ShapeDtypeStruct((8, D), F32)],
        scratch_shapes=[pltpu.VMEM((8, hl + tm, D), F32), pltpu.VMEM((8, tm + hl, D), F32)],
        compiler_params=_cp(("arbitrary",)),
    )(dz3, dz3, z1, z1, ab, ab, kern, lng, lnb)


def _alibi_slopes():
    h = np.arange(1, 3 * NHG + 1, dtype=np.float32)
    return np.power(np.float32(2.0), -8.0 * h / np.float32(3 * NHG)).astype(np.float32)


def _band_bias(gi):
    _, dil = GROUPS[gi]
    slopes = _alibi_slopes()[gi * NHG:(gi + 1) * NHG]
    qi = np.arange(BLK)[:, None]
    ki = np.arange(2 * BLK)[None, :]
    steps = BLK + qi - ki
    band = (steps >= 0) & (steps <= BLK)
    bias = -slopes[:, None, None] * (dil * steps).astype(np.float32)[None]
    return jnp.asarray(np.where(band[None], bias, np.float32(NEG)).astype(np.float32))


QB_FWD = 4
QB_BWD = 8


def _attn_specs(qb):
    prev = lambda n: jnp.maximum(n * qb - 1, 0)
    return [pl.BlockSpec((qb * BLK, HEAD), lambda h, n: (n, h)),
            pl.BlockSpec((BLK, HEAD), lambda h, n: (prev(n), NHG + h)),
            pl.BlockSpec((qb * BLK, HEAD), lambda h, n: (n, NHG + h)),
            pl.BlockSpec((BLK, HEAD), lambda h, n: (prev(n), 2 * NHG + h)),
            pl.BlockSpec((qb * BLK, HEAD), lambda h, n: (n, 2 * NHG + h)),
            pl.BlockSpec((None, BLK, 2 * BLK), lambda h, n: (h, 0, 0))]


def _scores(q, kcat, bias, blk, seg):
    s = _dot(q, kcat, NT) * (HEAD ** -0.5) + bias
    col = lax.broadcasted_iota(jnp.int32, s.shape, 1)
    first = (blk % seg) == 0
    return jnp.where(jnp.logical_and(first, col < BLK), NEG, s)


def _attn_fwd(qkv, gi, name):
    seg = (T // GROUPS[gi][1]) // BLK

    qb = QB_FWD

    def body(q_ref, kp_ref, kc_ref, vp_ref, vc_ref, bias_ref, o_ref, l_ref):
        n = pl.program_id(0)
        for h in range(NHG):
            cols = pl.ds(h * HEAD, HEAD)
            kwin = jnp.concatenate([kp_ref[:, cols], kc_ref[:, cols]], axis=0)
            vwin = jnp.concatenate([vp_ref[:, cols], vc_ref[:, cols]], axis=0)
            bias = bias_ref[h]
            for b in range(qb):
                rows = pl.ds(b * BLK, BLK)
                s = _scores(q_ref[rows, cols], kwin[b * BLK:(b + 2) * BLK], bias, n * qb + b, seg)
                mx = jnp.max(s, axis=-1, keepdims=True)
                p = jnp.exp(s - mx)
                den = jnp.sum(p, axis=-1, keepdims=True)
                o_ref[rows, cols] = _dot(p.astype(BF16), vwin[b * BLK:(b + 2) * BLK], NN) / den
                l_ref[rows, cols] = jnp.broadcast_to(mx + jnp.log(den), (BLK, HEAD))

    prev = lambda n: jnp.maximum(n * qb - 1, 0)
    cur = lambda part: pl.BlockSpec((qb * BLK, AW), lambda n: (n, part))
    halo = lambda part: pl.BlockSpec((BLK, AW), lambda n: (prev(n), part))
    return pl.pallas_call(
        body, name=name, grid=(T // (qb * BLK),),
        in_specs=[cur(0), halo(1), cur(1), halo(2), cur(2), _const_spec((NHG, BLK, 2 * BLK))],
        out_specs=[cur(0), cur(0)],
        out_shape=[jax.ShapeDtypeStruct((T, AW), F32), jax.ShapeDtypeStruct((T, AW), F32)],
        compiler_params=_cp(("parallel",)),
    )(qkv, qkv, qkv, qkv, qkv, _band_bias(gi))


def _attn_bwd(qkv, dob, lse, delta, gi, name):
    seg = (T // GROUPS[gi][1]) // BLK
    qb = QB_BWD
    nb = T // (qb * BLK)
    scale = HEAD ** -0.5

    def body(q_ref, kp_ref, kc_ref, vp_ref, vc_ref, bias_ref, do_ref, l_ref, dl_ref, out_ref, dk_acc, dv_acc):
        n = pl.program_id(1)
        kwin = jnp.concatenate([kp_ref[...], kc_ref[...]], axis=0)
        vwin = jnp.concatenate([vp_ref[...], vc_ref[...]], axis=0)
        bias = bias_ref[...]
        for b in range(qb):
            rows = pl.ds(b * BLK, BLK)
            q = q_ref[rows, :]
            kcat = kwin[b * BLK:(b + 2) * BLK]
            s = _scores(q, kcat, bias, n * qb + b, seg)
            p = jnp.exp(s - l_ref[rows, pl.ds(0, 1)])
            dov = do_ref[rows, :]
            dv2 = _dot(p.astype(BF16), dov, TN)
            dp = _dot(dov, vwin[b * BLK:(b + 2) * BLK], NT)
            dsb = (p * (dp - dl_ref[rows, pl.ds(0, 1)]) * scale).astype(BF16)
            row = pl.ds(pl.multiple_of((n * qb + b) * BLK, BLK), BLK)
            out_ref[0, row, :] = _dot(dsb, kcat, NN).astype(BF16)
            dk2 = _dot(dsb, q, TN)
            dk_acc[row, :] = dk2[BLK:]
            dv_acc[row, :] = dv2[BLK:]

            def add_prev(dk2=dk2, dv2=dv2, b=b):
                prow = pl.ds(pl.multiple_of((n * qb + b - 1) * BLK, BLK), BLK)
                dk_acc[prow, :] += dk2[:BLK]
                dv_acc[prow, :] += dv2[:BLK]

            if b == 0:
                pl.when(n > 0)(add_prev)
            else:
                add_prev()

        @pl.when(n == nb - 1)
        def _():
            out_ref[1] = dk_acc[...].astype(BF16)
            out_ref[2] = dv_acc[...].astype(BF16)

    oblk = pl.BlockSpec((qb * BLK, HEAD), lambda h, n: (n, h))
    return pl.pallas_call(
        body, name=name, grid=(NHG, nb),
        in_specs=_attn_specs(qb) + [oblk, oblk, oblk],
        out_specs=pl.BlockSpec((3, T, HEAD), lambda h, n: (0, 0, h)),
        out_shape=jax.ShapeDtypeStruct((3, T, AW), BF16),
        scratch_shapes=[pltpu.VMEM((T, HEAD), F32), pltpu.VMEM((T, HEAD), F32)],
        compiler_params=_cp(("parallel", "arbitrary")),
    )(qkv, qkv, qkv, qkv, qkv, _band_bias(gi), dob, lse, delta)


def _merge(outs, lses, name):
    tm = PERM_TM
    dils = [d for _, d in GROUPS]
    ng = len(dils)

    def body(*refs):
        in_refs = refs[:2 * ng]
        a_ref, ab_ref = refs[2 * ng:2 * ng + 2]
        lse_refs = refs[2 * ng + 2:3 * ng + 2]
        tile = refs[-1]

        def token_order(ref, dil):
            if dil == 1:
                return ref[...]
            _load_unperm(ref, tile, dil)
            return _get_tile(tile)

        os = [token_order(in_refs[2 * i], d) for i, d in enumerate(dils)]
        ls = [token_order(in_refs[2 * i + 1], d) for i, d in enumerate(dils)]
        mx = jnp.maximum(jnp.maximum(ls[0], ls[1]), ls[2])
        es = [jnp.exp(v - mx) for v in ls]
        tot = es[0] + es[1] + es[2]
        att = (es[0] / tot) * os[0] + (es[1] / tot) * os[1] + (es[2] / tot) * os[2]
        a_ref[...] = att
        ab_ref[...] = att.astype(BF16)
        lse = mx + jnp.log(tot)
        _put_tile(tile, lse)
        for dil, ref in zip(dils, lse_refs):
            if dil == 1:
                ref[...] = lse
            else:
                _store_perm(ref, tile, dil)

    row = pl.BlockSpec((tm, AW), lambda i: (i, 0))
    specs = [row if d == 1 else _perm_spec(d, AW) for d in dils]
    args = []
    for d, o, l in zip(dils, outs, lses):
        args += [o, l] if d == 1 else [o.reshape(d, T // d, AW), l.reshape(d, T // d, AW)]
    out = pl.pallas_call(
        body, name=name, grid=(T // tm,),
        in_specs=[sp for sp in specs for _ in range(2)], out_specs=[row, row] + specs,
        out_shape=[jax.ShapeDtypeStruct((T, AW), F32), jax.ShapeDtypeStruct((T, AW), BF16)]
        + [jax.ShapeDtypeStruct((T, AW), F32) if d == 1 else _perm_shape(d, AW, F32) for d in dils],
        scratch_shapes=[_tile_scratch(AW)],
        compiler_params=_cp(("parallel",)),
    )(*args)
    return out[0], out[1], [o.reshape(T, AW) for o in out[2:]]


def _mix_out(z3b, attnb, gates, wc, wa_t, wo, x1, name):
    tm = 512

    def body(z_ref, a_ref, g_ref, wc_ref, wa_ref, wo_ref, x_ref, xo_ref, yc_ref, ya_ref, mx_ref):
        yc = _dot(z_ref[...], wc_ref[...], NN)
        ya = _dot(a_ref[...], wa_ref[...], NT)
        yc_ref[...] = yc
        ya_ref[...] = ya
        gv = g_ref[...].astype(F32)
        mixed = (_sig(gv[:, :D]) * yc + _sig(gv[:, D:]) * ya).astype(BF16)
        mx_ref[...] = mixed
        xo_ref[...] = x_ref[...] + _dot(mixed, wo_ref[...], NN)

    row = pl.BlockSpec((tm, D), lambda i: (i, 0))
    return pl.pallas_call(
        body, name=name, grid=(T // tm,),
        in_specs=[row, pl.BlockSpec((tm, AW), lambda i: (i, 0)), pl.BlockSpec((tm, 2 * D), lambda i: (i, 0)),
                  _const_spec((D, D)), _const_spec((D, AW)), _const_spec((D, D)), row],
        out_specs=[row, row, row, row],
        out_shape=[jax.ShapeDtypeStruct((T, D), F32), jax.ShapeDtypeStruct((T, D), F32),
                   jax.ShapeDtypeStruct((T, D), F32), jax.ShapeDtypeStruct((T, D), BF16)],
        compiler_params=_cp(("parallel",)),
    )(z3b, attnb, gates, wc, wa_t, wo, x1)


def _mix_out_bwd(dx2, gates, yc, ya, attn, wc, wa_t, wo, name):
    tm = PERM_TM
    dils = [d for _, d in GROUPS]
    ng = len(dils)

    def body(dx_ref, g_ref, yc_ref, ya_ref, at_ref, wc_ref, wa_ref, wo_ref,
             dg_ref, dyc_ref, dya_ref, dxb_ref, dz3_ref, *rest):
        dat_refs, dl_refs, tile = rest[:ng], rest[ng:2 * ng], rest[-1]
        dxb = dx_ref[...].astype(BF16)
        dxb_ref[...] = dxb
        dmix = _dot(dxb, wo_ref[...], NT)
        gv = g_ref[...].astype(F32)
        sc = _sig(gv[:, :D])
        sa = _sig(gv[:, D:])
        ycv, yav = yc_ref[...], ya_ref[...]
        dg_ref[:, pl.ds(0, D)] = (dmix * ycv * sc * (1.0 - sc)).astype(BF16)
        dg_ref[:, pl.ds(D, D)] = (dmix * yav * sa * (1.0 - sa)).astype(BF16)
        dyc = (dmix * sc).astype(BF16)
        dya = (dmix * sa).astype(BF16)
        dyc_ref[...] = dyc
        dya_ref[...] = dya
        dz3_ref[...] = _dot(dyc, wc_ref[...], NT)
        dat = _dot(dya, wa_ref[...], NN)
        prod = dat * at_ref[...]
        delta = jnp.concatenate(
            [jnp.broadcast_to(jnp.sum(prod[:, h * HEAD:(h + 1) * HEAD], axis=-1, keepdims=True), (tm, HEAD))
             for h in range(NHG)], axis=1)
        for value, out_refs in ((dat, dat_refs), (delta, dl_refs)):
            _put_tile(tile, value)
            for dil, ref in zip(dils, out_refs):
                if dil == 1:
                    ref[...] = value.astype(ref.dtype)
                else:
                    _store_perm(ref, tile, dil)

    row = pl.BlockSpec((tm, D), lambda i: (i, 0))
    row2 = pl.BlockSpec((tm, 2 * D), lambda i: (i, 0))
    rowa = pl.BlockSpec((tm, AW), lambda i: (i, 0))
    aspecs = [rowa if d == 1 else _perm_spec(d, AW) for d in dils]

    def ashapes(dtype):
        return [jax.ShapeDtypeStruct((T, AW), dtype) if d == 1 else _perm_shape(d, AW, dtype) for d in dils]

    out = pl.pallas_call(
        body, name=name, grid=(T // tm,),
        in_specs=[row, row2, row, row, rowa, _const_spec((D, D)), _const_spec((D, AW)), _const_spec((D, D))],
        out_specs=[row2, row, row, row, row] + aspecs + aspecs,
        out_shape=[jax.ShapeDtypeStruct((T, 2 * D), BF16), jax.ShapeDtypeStruct((T, D), BF16),
                   jax.ShapeDtypeStruct((T, D), BF16), jax.ShapeDtypeStruct((T, D), BF16),
                   jax.ShapeDtypeStruct((T, D), F32)] + ashapes(BF16) + ashapes(F32),
        scratch_shapes=[_tile_scratch(AW)],
        compiler_params=_cp(("parallel",)),
    )(dx2, gates, yc, ya, attn, wc, wa_t, wo)
    dats = [o.reshape(T, AW) for o in out[5:5 + ng]]
    deltas = [o.reshape(T, AW) for o in out[5 + ng:5 + 2 * ng]]
    return out[0], out[1], out[2], out[3], out[4], dats, deltas


def _peer(k):
    x, y, c = lax.axis_index("x"), lax.axis_index("y"), lax.axis_index("c")
    px = 1 - x if k & 4 else x
    py = 1 - y if k & 2 else y
    pc = 1 - c if k & 1 else c
    return (px, py, pc), 4 * px + 2 * py + pc


HBM_SPEC = pl.BlockSpec(memory_space=pltpu.HBM)
SEM_SPEC = pl.BlockSpec(memory_space=pltpu.SEMAPHORE)
EFFECT = pltpu.SideEffectType.DATAFLOW_SIDE_EFFECTING


def _my_place():
    return 4 * lax.axis_index("x") + 2 * lax.axis_index("y") + lax.axis_index("c")


def _tie(a, order_after, name):
    na = len(order_after)

    def body(*refs):
        del refs

    return pl.pallas_call(
        body, name=name, in_specs=[pl.BlockSpec(memory_space=pl.ANY)] * (1 + na),
        out_specs=pl.BlockSpec(memory_space=pl.ANY), out_shape=jax.ShapeDtypeStruct(a.shape, a.dtype),
        input_output_aliases={0: 0},
    )(a, *order_after)


def _prep_gather(ws, order_after, name):
    me = jnp.reshape(_my_place(), (1,)).astype(jnp.int32)
    n = len(ws)
    na = len(order_after)
    shapes = [((32, wv.shape[1]), F32) if wv.shape[0] == CONV_W else (wv.shape, BF16) for wv in ws]

    def body(me_ref, *refs):
        del me_ref
        ins, outs = refs[:n], refs[n + na:]
        for wv, i_ref, o_ref in zip(ws, ins, outs):
            if wv.shape[0] == CONV_W:
                o_ref[pl.ds(0, CONV_W), :] = i_ref[...]
                o_ref[pl.ds(CONV_W, 1), :] = jnp.zeros((1, wv.shape[1]), F32)
            else:
                o_ref[...] = i_ref[...].astype(BF16)

    grid_spec = pltpu.PrefetchScalarGridSpec(
        num_scalar_prefetch=1, grid=(1,),
        in_specs=[pl.BlockSpec(wv.shape, lambda i, m: (0, 0)) for wv in ws]
        + [pl.BlockSpec(memory_space=pl.ANY)] * na,
        out_specs=[pl.BlockSpec(shp, lambda i, m: (m[0], 0)) for shp, _ in shapes])
    return pl.pallas_call(
        body, name=name, grid_spec=grid_spec,
        out_shape=[jax.ShapeDtypeStruct((NDEV * shp[0], shp[1]), dt) for shp, dt in shapes],
        compiler_params=_cp(("arbitrary",)),
    )(me, *ws, *order_after)


GATHER_A = ((1, 0), (2, 0), (4, 0), (6, 0))
GATHER_B = ((1, 2), (1, 4), (1, 6))


def _gather_start(lands, plan, order_after, name):
    n = len(lands)
    na = len(order_after)
    npl = len(plan)

    def body(*refs):
        land_refs = refs[:n]
        send, recv = refs[n + na], refs[n + na + 1]
        token = refs[-1]
        for w in range(n):
            rows = lands[w].shape[0] // NDEV
            for p, (k, j) in enumerate(plan):
                peer, _ = _peer(k)
                _, blk = _peer(j)
                part = land_refs[w].at[pl.ds(blk * rows, rows)]
                i = w * npl + p
                pltpu.make_async_remote_copy(src_ref=part, dst_ref=part, send_sem=send.at[i], recv_sem=recv.at[i],
                                             device_id=peer, device_id_type=MESH_ID).start()
        token[...] = jnp.zeros_like(token)

    nsem = n * npl
    bufs = [pltpu.with_memory_space_constraint(a, pltpu.HBM) for a in lands]
    out = pl.pallas_call(
        body, name=name,
        in_specs=[HBM_SPEC] * n + [pl.BlockSpec(memory_space=pl.ANY)] * na,
        out_specs=[SEM_SPEC, SEM_SPEC] + [HBM_SPEC] * n + [pl.BlockSpec(memory_space=pltpu.VMEM)],
        out_shape=[pltpu.SemaphoreType.DMA((nsem,)), pltpu.SemaphoreType.DMA((nsem,))]
        + [pltpu.HBM(a.shape, a.dtype) for a in bufs] + [jax.ShapeDtypeStruct((8, 128), F32)],
        input_output_aliases={i: 2 + i for i in range(n)},
        compiler_params=pltpu.CompilerParams(has_side_effects=EFFECT),
    )(*bufs, *order_after)
    return out[0], out[1], out[2:2 + n], out[-1]


def _gather_wait(started, plan, order_after, name):
    send, recv, lands, _ = started
    n = len(lands)
    na = len(order_after)
    npl = len(plan)

    def body(*refs):
        land_refs = refs[:n]
        send_ref, recv_ref = refs[n], refs[n + 1]
        for w in range(n):
            rows = lands[w].shape[0] // NDEV
            for p, (k, j) in enumerate(plan):
                peer, _ = _peer(k)
                _, blk = _peer(j)
                part = land_refs[w].at[pl.ds(blk * rows, rows)]
                i = w * npl + p
                cp = pltpu.make_async_remote_copy(src_ref=part, dst_ref=part, send_sem=send_ref.at[i],
                                                  recv_sem=recv_ref.at[i], device_id=peer, device_id_type=MESH_ID)
                cp.wait_send()
                cp.wait_recv()

    out = pl.pallas_call(
        body, name=name,
        in_specs=[HBM_SPEC] * n + [SEM_SPEC, SEM_SPEC] + [pl.BlockSpec(memory_space=pl.ANY)] * na,
        out_specs=[HBM_SPEC] * n,
        out_shape=[pltpu.HBM(a.shape, a.dtype) for a in lands],
        input_output_aliases={i: i for i in range(n)},
        compiler_params=pltpu.CompilerParams(has_side_effects=EFFECT),
    )(*lands, send, recv, *order_after)
    return list(out)


def _copy_ends(kind, src, land, me, plin, k):
    if kind == "scatter":
        rows = src.shape[0] // NDEV
        return src.at[pl.ds(plin * rows, rows)], land.at[k - 1]
    return src, land.at[me]


def _landing(kind, src):
    me = _my_place()
    if kind == "scatter":
        return lax.empty((NDEV - 1, src.shape[0] // NDEV) + src.shape[1:], src.dtype)
    land = lax.empty((NDEV,) + src.shape, src.dtype)
    return lax.dynamic_update_slice(land, src[None], (me,) + (0,) * src.ndim)


def _send_start(kinds, srcs, order_after, name):
    n = len(srcs)
    lands = [_landing(kd, s) for kd, s in zip(kinds, srcs)]
    na = len(order_after)

    def body(*refs):
        src_refs, land_refs = refs[:n], refs[n:2 * n]
        send, recv = refs[2 * n + na], refs[2 * n + na + 1]
        token = refs[-1]
        _, me = _peer(0)
        for w in range(n):
            for k in range(1, NDEV):
                peer, plin = _peer(k)
                s, d = _copy_ends(kinds[w], src_refs[w], land_refs[w], me, plin, k)
                i = w * (NDEV - 1) + k - 1
                pltpu.make_async_remote_copy(src_ref=s, dst_ref=d, send_sem=send.at[i], recv_sem=recv.at[i],
                                             device_id=peer, device_id_type=MESH_ID).start()
        token[...] = jnp.zeros_like(token)

    nsem = n * (NDEV - 1)
    bufs = [pltpu.with_memory_space_constraint(a, pltpu.HBM) for a in list(srcs) + lands]
    out = pl.pallas_call(
        body, name=name,
        in_specs=[HBM_SPEC] * (2 * n) + [pl.BlockSpec(memory_space=pl.ANY)] * na,
        out_specs=[SEM_SPEC, SEM_SPEC] + [HBM_SPEC] * (2 * n) + [pl.BlockSpec(memory_space=pltpu.VMEM)],
        out_shape=[pltpu.SemaphoreType.DMA((nsem,)), pltpu.SemaphoreType.DMA((nsem,))]
        + [pltpu.HBM(a.shape, a.dtype) for a in bufs] + [jax.ShapeDtypeStruct((8, 128), F32)],
        input_output_aliases={i: 2 + i for i in range(2 * n)},
        compiler_params=pltpu.CompilerParams(has_side_effects=EFFECT),
    )(*bufs, *order_after)
    return out[0], out[1], out[2:2 + n], out[2 + n:2 + 2 * n], out[-1]


def _send_wait(kinds, started, order_after, name):
    send, recv, srcs, lands, _ = started
    n = len(srcs)
    na = len(order_after)

    def body(*refs):
        src_refs, land_refs = refs[:n], refs[n:2 * n]
        send_ref, recv_ref = refs[2 * n], refs[2 * n + 1]
        _, me = _peer(0)
        for w in range(n):
            for k in range(1, NDEV):
                peer, plin = _peer(k)
                s, d = _copy_ends(kinds[w], src_refs[w], land_refs[w], me, plin, k)
                i = w * (NDEV - 1) + k - 1
                cp = pltpu.make_async_remote_copy(src_ref=s, dst_ref=d, send_sem=send_ref.at[i],
                                                  recv_sem=recv_ref.at[i], device_id=peer, device_id_type=MESH_ID)
                cp.wait_send()
                cp.wait_recv()

    bufs = list(srcs) + list(lands)
    out = pl.pallas_call(
        body, name=name,
        in_specs=[HBM_SPEC] * (2 * n) + [SEM_SPEC, SEM_SPEC] + [pl.BlockSpec(memory_space=pl.ANY)] * na,
        out_specs=[HBM_SPEC] * (2 * n),
        out_shape=[pltpu.HBM(a.shape, a.dtype) for a in bufs],
        input_output_aliases={i: i for i in range(2 * n)},
        compiler_params=pltpu.CompilerParams(has_side_effects=EFFECT),
    )(*bufs, send, recv, *order_after)
    return out[:n], out[n:]


def _gsum(own, land, name):
    rows, cols = own.shape
    tr = rows // 2 if rows * cols > 512 * 1024 and rows % 32 == 0 else rows

    def body(own_ref, l_ref, o_ref):
        tot = own_ref[...].astype(F32)
        for s in range(NDEV - 1):
            tot = tot + l_ref[s].astype(F32)
        o_ref[...] = tot

    return pl.pallas_call(
        body, name=name, grid=(rows // tr,),
        in_specs=[pl.BlockSpec((tr, cols), lambda i: (i, 0)),
                  pl.BlockSpec((NDEV - 1, tr, cols), lambda i: (0, i, 0))],
        out_specs=pl.BlockSpec((tr, cols), lambda i: (i, 0)),
        out_shape=jax.ShapeDtypeStruct((rows, cols), F32),
        compiler_params=_cp(("parallel",)),
    )(own, land)


def _adamw_math(w, g, m, v):
    m2 = B1 * m + (1.0 - B1) * g
    v2 = B2 * v + (1.0 - B2) * (g * g)
    m_hat = m2 / (1.0 - B1 ** STEP)
    v_hat = v2 / (1.0 - B2 ** STEP)
    delta = -LR * (m_hat / (jnp.sqrt(v_hat) + AEPS) + WD * w)
    return delta, m2, v2


def _adamw(w, g, m, v, name):
    rows, cols = w.shape
    tr = 256 if rows % 256 == 0 and rows > 256 else rows

    def body(w_ref, g_ref, m_ref, v_ref, d_ref, mo_ref, vo_ref):
        d, m2, v2 = _adamw_math(w_ref[...], g_ref[...], m_ref[...], v_ref[...])
        d_ref[...] = d
        mo_ref[...] = m2
        vo_ref[...] = v2

    blk = pl.BlockSpec((tr, cols), lambda i: (i, 0))
    return pl.pallas_call(
        body, name=name, grid=(rows // tr,), in_specs=[blk] * 4, out_specs=[blk] * 3,
        out_shape=[jax.ShapeDtypeStruct((rows, cols), F32)] * 3,
        compiler_params=_cp(("parallel",)),
    )(w, g, m, v)


UPD_TC = 256


def _update(src, land, w, m, v, name):
    rows, cols = land.shape[1:]
    tc = min(UPD_TC, cols)
    me = jnp.reshape(_my_place(), (1,)).astype(jnp.int32)

    def body(me_ref, own_ref, l_ref, w_ref, m_ref, v_ref, g_ref, d_ref, mo_ref, vo_ref):
        del me_ref
        g = own_ref[...].astype(F32)
        for s in range(NDEV - 1):
            g = g + l_ref[s].astype(F32)
        g_ref[...] = g
        d, m2, v2 = _adamw_math(w_ref[...], g, m_ref[...], v_ref[...])
        d_ref[...] = d
        mo_ref[...] = m2
        vo_ref[...] = v2

    wblk = pl.BlockSpec((rows, tc), lambda j, p: (0, j))
    grid_spec = pltpu.PrefetchScalarGridSpec(
        num_scalar_prefetch=1, grid=(cols // tc,),
        in_specs=[pl.BlockSpec((rows, tc), lambda j, p: (p[0], j)),
                  pl.BlockSpec((NDEV - 1, rows, tc), lambda j, p: (0, 0, j)), wblk, wblk, wblk],
        out_specs=[wblk] * 4)
    return pl.pallas_call(
        body, name=name, grid_spec=grid_spec, out_shape=[jax.ShapeDtypeStruct((rows, cols), F32)] * 4,
        compiler_params=_cp(("parallel",)),
    )(me, src, land, w, m, v)


def _small_update(vland, w8, m8, v8, name):
    def body(l_ref, w_ref, m_ref, v_ref, g_ref, d_ref, mo_ref, vo_ref):
        g = l_ref[0]
        for s in range(1, NDEV):
            g = g + l_ref[s]
        g_ref[...] = g
        d, m2, v2 = _adamw_math(w_ref[...], g, m_ref[...], v_ref[...])
        d_ref[...] = d
        mo_ref[...] = m2
        vo_ref[...] = v2

    return pl.pallas_call(
        body, name=name, out_shape=[jax.ShapeDtypeStruct((8, D), F32)] * 4,
        compiler_params=_cp(None),
    )(vland, w8, m8, v8)


def kernel(x, ffn1_norm, ffn1_w_gate, ffn1_w_up, ffn1_w_down, mix_norm, w_in, conv_dw_kernel, conv_dw_bias, conv_ln_gain, conv_ln_bias, conv_w_out, attn_w_out, w_o, ffn2_norm, ffn2_w_gate, ffn2_w_up, ffn2_w_down, final_norm, loss_target, m_ffn1_norm, m_ffn1_w_gate, m_ffn1_w_up, m_ffn1_w_down, m_mix_norm, m_w_in, m_conv_dw_kernel, m_conv_dw_bias, m_conv_ln_gain, m_conv_ln_bias, m_conv_w_out, m_attn_w_out, m_w_o, m_ffn2_norm, m_ffn2_w_gate, m_ffn2_w_up, m_ffn2_w_down, m_final_norm, v_ffn1_norm, v_ffn1_w_gate, v_ffn1_w_up, v_ffn1_w_down, v_mix_norm, v_w_in, v_conv_dw_kernel, v_conv_dw_bias, v_conv_ln_gain, v_conv_ln_bias, v_conv_w_out, v_attn_w_out, v_w_o, v_ffn2_norm, v_ffn2_w_gate, v_ffn2_w_up, v_ffn2_w_down, v_final_norm):
    names = ["ffn1_norm", "ffn1_w_gate", "ffn1_w_up", "ffn1_w_down", "mix_norm", "w_in", "conv_dw_kernel",
             "conv_dw_bias", "conv_ln_gain", "conv_ln_bias", "conv_w_out", "attn_w_out", "w_o", "ffn2_norm",
             "ffn2_w_gate", "ffn2_w_up", "ffn2_w_down", "final_norm"]
    w = dict(ffn1_norm=ffn1_norm, ffn1_w_gate=ffn1_w_gate, ffn1_w_up=ffn1_w_up, ffn1_w_down=ffn1_w_down, mix_norm=mix_norm, w_in=w_in, conv_dw_kernel=conv_dw_kernel, conv_dw_bias=conv_dw_bias, conv_ln_gain=conv_ln_gain, conv_ln_bias=conv_ln_bias, conv_w_out=conv_w_out, attn_w_out=attn_w_out, w_o=w_o, ffn2_norm=ffn2_norm, ffn2_w_gate=ffn2_w_gate, ffn2_w_up=ffn2_w_up, ffn2_w_down=ffn2_w_down, final_norm=final_norm)
    mo = dict(ffn1_norm=m_ffn1_norm, ffn1_w_gate=m_ffn1_w_gate, ffn1_w_up=m_ffn1_w_up, ffn1_w_down=m_ffn1_w_down, mix_norm=m_mix_norm, w_in=m_w_in, conv_dw_kernel=m_conv_dw_kernel, conv_dw_bias=m_conv_dw_bias, conv_ln_gain=m_conv_ln_gain, conv_ln_bias=m_conv_ln_bias, conv_w_out=m_conv_w_out, attn_w_out=m_attn_w_out, w_o=m_w_o, ffn2_norm=m_ffn2_norm, ffn2_w_gate=m_ffn2_w_gate, ffn2_w_up=m_ffn2_w_up, ffn2_w_down=m_ffn2_w_down, final_norm=m_final_norm)
    vo = dict(ffn1_norm=v_ffn1_norm, ffn1_w_gate=v_ffn1_w_gate, ffn1_w_up=v_ffn1_w_up, ffn1_w_down=v_ffn1_w_down, mix_norm=v_mix_norm, w_in=v_w_in, conv_dw_kernel=v_conv_dw_kernel, conv_dw_bias=v_conv_dw_bias, conv_ln_gain=v_conv_ln_gain, conv_ln_bias=v_conv_ln_bias, conv_w_out=v_conv_w_out, attn_w_out=v_attn_w_out, w_o=v_w_o, ffn2_norm=v_ffn2_norm, ffn2_w_gate=v_ffn2_w_gate, ffn2_w_up=v_ffn2_w_up, ffn2_w_down=v_ffn2_w_down, final_norm=v_final_norm)
    col_sharded = ("ffn1_w_gate", "ffn1_w_up", "w_in", "attn_w_out", "ffn2_w_gate", "ffn2_w_up")
    row_sharded = ("ffn1_w_down", "conv_w_out", "w_o", "ffn2_w_down")
    small = ("ffn1_norm", "mix_norm", "ffn2_norm", "final_norm", "conv_dw_bias", "conv_ln_gain", "conv_ln_bias")

    def landing_view(a, n):
        return jnp.transpose(a[0]) if n in col_sharded else a[0]

    def own_view(a, n):
        return jnp.transpose(a)[None] if n in col_sharded else a[None]

    ag_groups = (("ffn1_w_gate", "ffn1_w_up", "ffn1_w_down"),
                 ("w_in", "attn_w_out", "conv_w_out", "w_o", "conv_dw_kernel"),
                 ("ffn2_w_gate", "ffn2_w_up", "ffn2_w_down"))
    ag, order = [], []
    for gi, grp in enumerate(ag_groups):
        lands = _prep_gather([landing_view(w[n], n) for n in grp], order, f"gather_prep{gi}")
        st = _gather_start(lands, GATHER_A, [], f"gather_a_start{gi}")
        ag.append(st)
        order = [st[3]]

    def chips_in(gi, after):
        lands = _gather_wait(ag[gi], GATHER_A, after, f"gather_a_wait{gi}")
        return _gather_start(lands, GATHER_B, [], f"gather_b_start{gi}")

    def all_in(gi, st, after):
        return _gather_wait(st, GATHER_B, after, f"gather_b_wait{gi}")

    x0 = x[0]
    tgt = loss_target[0]
    gf = final_norm.reshape(1, D)

    wg1, wu1, wd1 = all_in(0, chips_in(0, [ag[2][3]]), [])
    x1, gg1, uu1, h2p = _ffn_fwd(x0, ffn1_norm, wg1, wu1, wd1, "ffn1_fwd", next_gain=mix_norm)
    h2 = h2p[0]
    win_t, wa_t, wc, wo, kern_blocks = all_in(1, chips_in(1, [x1]), [])
    kern = kern_blocks.reshape(NDEV, 32, D // NDEV).transpose(1, 0, 2).reshape(32, D)
    ptm = min(T, 2048)
    ab = _mm(h2, win_t, mode="nt", m=T, n=2 * D, k=D, tm=ptm, tn=512, tk=D, out_dtype=BF16, name="proj_conv")
    gates = _mm(h2, win_t, mode="nt", m=T, n=2 * D, k=D, tm=ptm, tn=512, tk=D, out_dtype=BF16,
                b_map=lambda i, j, kk: (13 + j, 0), name="proj_gates")
    qkv = []
    for gi in range(len(GROUPS)):
        qkv.append(_mm(h2p[gi], win_t, mode="nt", m=T, n=3 * AW, k=D, tm=ptm, tn=AW, tk=D, out_dtype=BF16,
                       b_map=lambda i, j, kk, gi=gi: (4 + gi + 3 * j, 0), name=f"proj_qkv{gi}"))
    z1, z3b = _conv_fwd(ab, kern, conv_dw_bias, conv_ln_gain, conv_ln_bias, "conv_fwd")
    ffn2_b = chips_in(2, [z3b])
    outs, lses = [], []
    for gi, (_, dil) in enumerate(GROUPS):
        o, l = _attn_fwd(qkv[gi], gi, f"attn_fwd{gi}")
        outs.append(o)
        lses.append(l)
    attn, attnb, lse = _merge(outs, lses, "attn_merge")
    x2, yc, ya, mixedb = _mix_out(z3b, attnb, gates, wc, wa_t, wo, x1, "mix_out_fwd")
    wg2, wu2, wd2 = all_in(2, ffn2_b, [x2])
    gg2, uu2, dx3, dgf, loss_part = _ffn_fwd(x2, ffn2_norm, wg2, wu2, wd2, "ffn2_fwd", loss_of=(gf, tgt))

    dx2, dg3, dgb, dub, actb, hb, dob = _ffn_bwd(x2, ffn2_norm, gg2, uu2, dx3, wg2, wu2, wd2, "ffn2_bwd")
    grads = {}
    grads["ffn2_w_gate"] = _wgrad(dgb, hb, FF, D, "ffn2_dwg")
    grads["ffn2_w_up"] = _wgrad(dub, hb, FF, D, "ffn2_dwu")
    grads["ffn2_w_down"] = _wgrad(actb, dob, FF, D, "ffn2_dwd")
    rs_groups = [("ffn2_w_gate", "ffn2_w_up", "ffn2_w_down"),
                 ("attn_w_out", "conv_w_out", "w_o", "conv_dw_kernel"),
                 ("w_in",),
                 ("ffn1_w_gate",), ("ffn1_w_up",), ("ffn1_w_down",), ()]
    last = len(rs_groups) - 1
    rs = [_send_start(["scatter"] * 3, [grads[n] for n in rs_groups[0]], [], "scatter_start0")]
    dx2 = _tie(dx2, [rs[0][4]], "tie_after_scatter0")

    dgates, dycb, dyab, dx2b, dz3, dattnb, delta = _mix_out_bwd(dx2, gates, yc, ya, attn, wc, wa_t, wo, "mix_out_bwd")
    grads["w_o"] = _wgrad(mixedb, dx2b, D, D, "dw_o")
    grads["conv_w_out"] = _wgrad(z3b, dycb, D, D, "dw_conv_out")
    grads["attn_w_out"] = _wgrad(dyab, attnb, D, AW, "dw_attn_out")
    dab, dkern, dvec = _conv_bwd(dz3, z1, ab, kern, conv_ln_gain, conv_ln_bias, "conv_bwd")
    grads["conv_dw_kernel"] = dkern.reshape(32, NDEV, D // NDEV).transpose(1, 0, 2).reshape(NDEV * 32, D // NDEV)
    rs.append(_send_start(["scatter"] * 4, [grads[n] for n in rs_groups[1]], [rs[0][4]], "scatter_start1"))
    dattnb = [_tie(a, [rs[1][4]], f"tie_after_scatter1_{i}") for i, a in enumerate(dattnb)]

    dqkv = []
    for gi, (_, dil) in enumerate(GROUPS):
        dq3 = _attn_bwd(qkv[gi], dattnb[gi], lse[gi], delta[gi], gi, f"attn_bwd{gi}")
        dqkv.append(dq3.reshape(3 * T, AW))

    dwin = _mm(dab, h2, mode="tn", m=2 * D, n=D, k=T, tm=2 * D, tn=D, tk=512, out_dtype=BF16, out_rows=IN_W,
               name="dw_in_conv")
    dwin = _mm(dgates, h2, mode="tn", m=2 * D, n=D, k=T, tm=512, tn=D, tk=1024, out_dtype=BF16, out_rows=IN_W,
               o_map=lambda i, j, kk: (13 + i, 0), passthru=dwin, name="dw_in_gates")
    for gi in range(3):
        dwin = _mm(dqkv[gi], h2p[gi], mode="tn", m=3 * AW, n=D, k=T, tm=AW, tn=D, tk=1024, out_dtype=BF16,
                   out_rows=IN_W, a_map=lambda i, j, kk: (i * (T // 1024) + kk, 0),
                   o_map=lambda i, j, kk, gi=gi: (4 + gi + 3 * i, 0), passthru=dwin, name=f"dw_in_qkv{gi}")
    grads["w_in"] = dwin
    rs.append(_send_start(["scatter"], [dwin], [rs[1][4]], "scatter_start2"))
    dab = _tie(dab, [rs[2][4]], "tie_after_scatter2")

    nrow = T // 1024
    dh = _mm(dab, win_t, mode="nn", m=T, n=D, k=2 * D, tm=1024, tn=D, tk=1024, out_dtype=F32, name="dproj_conv")
    dh = _mm(dgates, win_t, mode="nn", m=T, n=D, k=2 * D, tm=1024, tn=D, tk=512, out_dtype=F32,
             b_map=lambda i, j, kk: (13 + kk, 0), init=dh, name="dproj_gates")
    dhs = []
    for gi, (_, dil) in enumerate(GROUPS):
        part = _mm(dqkv[gi], win_t, mode="nn", m=T, n=D, k=3 * AW, tm=1024, tn=D, tk=AW, out_dtype=F32,
                   a_map=lambda i, j, kk: (kk * nrow + i, 0), b_map=lambda i, j, kk, gi=gi: (4 + gi + 3 * kk, 0),
                   init=dh if gi == 0 else None, name=f"dproj_qkv{gi}")
        dhs.append(part)
    dx1, dg2 = _rms_bwd(x1, mix_norm, dhs, dx2, "mix_norm_bwd")

    dgb, dub, actb, hb, dob = _ffn_bwd_pre(x0, ffn1_norm, gg1, uu1, dx1, wd1, "ffn1_bwd_pre")
    grads["ffn1_w_gate"] = _wgrad(dgb, hb, FF, D, "ffn1_dwg")
    rs.append(_send_start(["scatter"], [grads["ffn1_w_gate"]], [rs[2][4]], "scatter_start3"))
    hb = _tie(hb, [rs[3][4]], "tie_after_scatter3")
    grads["ffn1_w_up"] = _wgrad(dub, hb, FF, D, "ffn1_dwu")
    rs.append(_send_start(["scatter"], [grads["ffn1_w_up"]], [rs[3][4]], "scatter_start4"))
    dob = _tie(dob, [rs[4][4]], "tie_after_scatter4")
    grads["ffn1_w_down"] = _wgrad(actb, dob, FF, D, "ffn1_dwd")
    rs.append(_send_start(["scatter"], [grads["ffn1_w_down"]], [rs[4][4]], "scatter_start5"))
    dgb = _tie(dgb, [rs[5][4]], "tie_after_scatter5")
    dx0, dg1 = _ffn_bwd_dx(x0, ffn1_norm, dgb, dub, dx1, wg1, wu1, "ffn1_bwd_dx")
    vec = jnp.concatenate([dg1, dg2, dg3, dgf, dvec[0:3], jnp.broadcast_to(loss_part[:, :1], (1, D))], axis=0)
    rs.append(_send_start(["bcast"], [vec], [rs[5][4]], "scatter_start6"))

    g_out, d_out, m_out, v_out = {}, {}, {}, {}
    me = _my_place()
    after = [rs[last][4]]
    for gi, grp in enumerate(rs_groups):
        kinds = ["scatter"] * len(grp) + (["bcast"] if gi == last else [])
        srcs, lands = _send_wait(kinds, rs[gi], after, f"scatter_wait{gi}")
        for n, src, land in zip(grp, srcs, lands):
            if n == "conv_dw_kernel":
                rows = src.shape[0] // NDEV
                own = lax.dynamic_slice(src, (me * rows, 0), (rows, src.shape[1]))
                g = _gsum(own, land, f"gsum_{n}")[:CONV_W]
                d, m2, v2 = _adamw(w[n][0], g, mo[n][0], vo[n][0], f"adamw_{n}")
                after = [d]
                g, d, m2, v2 = g[None], d[None], m2[None], v2[None]
            else:
                res = _update(src, land, landing_view(w[n], n), landing_view(mo[n], n), landing_view(vo[n], n),
                              f"update_{n}")
                after = [res[1]]
                g, d, m2, v2 = (own_view(a, n) for a in res)
            g_out[n], d_out[n], m_out[n], v_out[n] = g, d, m2, v2
    vland = lands[-1]

    def rows8(src):
        return jnp.concatenate([src[n].reshape(1, D) for n in small] + [jnp.ones((1, D), F32)], axis=0)

    g8, d8, m8, v8 = _small_update(vland, rows8(w), rows8(mo), rows8(vo), "small_update")
    for r, n in enumerate(small):
        shp = w[n].shape
        g_out[n], d_out[n], m_out[n], v_out[n] = (a[r].reshape(shp) for a in (g8, d8, m8, v8))
    loss = g8[7, 0]

    return (loss, dx0[None], *[g_out[n] for n in names], *[d_out[n] for n in names],
            *[m_out[n] for n in names], *[v_out[n] for n in names])
```

```python
import numpy as np
import jax
import jax.numpy as jnp
from jax import lax
from jax.experimental import pallas as pl
from jax.experimental.pallas import tpu as pltpu

F32 = jnp.float32
BF16 = jnp.bfloat16

T = 4096
D = 1024
FF = 2816
NDEV = 8
CONV_W = 31
HEAD = 128
BLK = 128
GROUPS = ((128, 1), (512, 4), (2048, 16))
NHG = 4
AW = NHG * HEAD
IN_W = 2 * D + 3 * 3 * AW + 2 * D
EPS = 1e-6
B1, B2, LR, AEPS, WD, STEP = 0.9, 0.999, 0.001, 1e-08, 0.01, 10
NEG = -1e30
VMEM_LIMIT = 56 * 1024 * 1024
MESH_ID = pl.DeviceIdType.MESH

NT = (((1,), (1,)), ((), ()))
NN = (((1,), (0,)), ((), ()))
TN = (((0,), (0,)), ((), ()))
_DIMS = {"nn": NN, "nt": NT, "tn": TN}


def _cp(sem=None):
    return pltpu.CompilerParams(dimension_semantics=sem, vmem_limit_bytes=VMEM_LIMIT)


def _sig(v):
    return 1.0 / (1.0 + jnp.exp(-v))


def _dot(a, b, dims):
    return lax.dot_general(a, b, dims, preferred_element_type=F32)


def _const_spec(shape):
    nd = len(shape)
    return pl.BlockSpec(shape, lambda *_: (0,) * nd)


def _mm(a, b, *, mode, m, n, k, tm, tn, tk, out_dtype, name, a_map=None, b_map=None,
        o_map=None, out_rows=None, init=None, passthru=None):
    gi, gj, gk = m // tm, n // tn, k // tk
    assert gi * tm == m and gj * tn == n and gk * tk == k, (name, m, n, k, tm, tn, tk)
    if mode == "nn":
        a_blk, b_blk = (tm, tk), (tk, tn)
        da, db = (lambda i, j, kk: (i, kk)), (lambda i, j, kk: (kk, j))
    elif mode == "nt":
        a_blk, b_blk = (tm, tk), (tn, tk)
        da, db = (lambda i, j, kk: (i, kk)), (lambda i, j, kk: (j, kk))
    else:
        a_blk, b_blk = (tk, tm), (tk, tn)
        da, db = (lambda i, j, kk: (kk, i)), (lambda i, j, kk: (kk, j))
    a_map = a_map or da
    b_map = b_map or db
    o_map = o_map or (lambda i, j, kk: (i, j))
    dims = _DIMS[mode]
    extra = init if init is not None else passthru
    out_rows = out_rows or m

    def body(*refs):
        if init is not None:
            a_ref, b_ref, i_ref, o_ref = refs[:4]
        elif passthru is not None:
            a_ref, b_ref, _, o_ref = refs[:4]
        else:
            a_ref, b_ref, o_ref = refs[:3]
        if gk == 1:
            prod = _dot(a_ref[...], b_ref[...], dims)
            if init is not None:
                prod = prod + i_ref[...].astype(F32)
            o_ref[...] = prod.astype(out_dtype)
            return
        acc = refs[-1]
        kk = pl.program_id(2)

        @pl.when(kk == 0)
        def _():
            if init is not None:
                acc[...] = i_ref[...].astype(F32)
            else:
                acc[...] = jnp.zeros_like(acc)

        acc[...] += _dot(a_ref[...], b_ref[...], dims)

        @pl.when(kk == gk - 1)
        def _():
            o_ref[...] = acc[...].astype(out_dtype)

    in_specs = [pl.BlockSpec(a_blk, a_map), pl.BlockSpec(b_blk, b_map)]
    args = [a, b]
    aliases = {}
    if init is not None:
        in_specs.append(pl.BlockSpec((tm, tn), o_map))
        args.append(init)
        aliases = {2: 0}
    elif passthru is not None:
        in_specs.append(pl.BlockSpec(memory_space=pl.ANY))
        args.append(passthru)
        aliases = {2: 0}
    out_dt = extra.dtype if extra is not None else out_dtype
    assert out_dt == out_dtype
    return pl.pallas_call(
        body, name=name, grid=(gi, gj, gk),
        in_specs=in_specs, out_specs=pl.BlockSpec((tm, tn), o_map),
        out_shape=jax.ShapeDtypeStruct((out_rows, n), out_dtype),
        scratch_shapes=[pltpu.VMEM((tm, tn), F32)] if gk > 1 else [],
        input_output_aliases=aliases,
        compiler_params=_cp(("parallel", "parallel", "arbitrary")),
    )(*args)


def _ffn_fwd(x, g, wg_t, wu_t, wd, name, next_gain=None, loss_of=None):
    tm, fc = PERM_TM, 256
    nc = FF // fc
    n_in = 5 + (1 if next_gain is not None else 0) + (2 if loss_of is not None else 0)

    def body(*refs):
        x_ref, g_ref, wg_ref, wu_ref, wd_ref = refs[:5]
        extra_in, outs = refs[5:n_in], refs[n_in:]
        act_ref = outs[-1]
        xv = x_ref[...]
        r = lax.rsqrt(jnp.mean(xv * xv, axis=-1, keepdims=True) + EPS)
        h = (xv * r * g_ref[...]).astype(BF16)
        gg_ref, uu_ref = (outs[0], outs[1]) if loss_of is not None else (outs[1], outs[2])
        for c in range(nc):
            sl = pl.ds(c * fc, fc)
            gg = _dot(h, wg_ref[sl, :], NT)
            uu = _dot(h, wu_ref[sl, :], NT)
            gg_ref[:, sl] = gg.astype(BF16)
            uu_ref[:, sl] = uu.astype(BF16)
            act_ref[:, sl] = (gg * _sig(gg) * uu).astype(BF16)
        y = xv + 0.5 * _dot(act_ref[...], wd_ref[...], NN)
        if loss_of is not None:
            _final_math(y, extra_in[0][...], extra_in[1][...], outs[2], outs[3], outs[4], pl.program_id(0))
            return
        outs[0][...] = y
        if next_gain is not None:
            tile = outs[-2]
            r2 = lax.rsqrt(jnp.mean(y * y, axis=-1, keepdims=True) + EPS)
            hv = y * r2 * extra_in[0][...]
            outs[3][...] = hv.astype(BF16)
            _put_tile(tile, hv)
            for dil, p_ref in zip(DILS, outs[4:4 + len(DILS)]):
                _store_perm(p_ref, tile, dil)

    wspec = pl.BlockSpec((FF, D), lambda i: (0, 0), pipeline_mode=pl.Buffered(1))
    row_d = pl.BlockSpec((tm, D), lambda i: (i, 0))
    row_f = pl.BlockSpec((tm, FF), lambda i: (i, 0))
    in_specs = [row_d, _const_spec((1, D)), wspec, wspec, wspec]
    args = [x, g, wg_t, wu_t, wd]
    f_shape = jax.ShapeDtypeStruct((T, FF), BF16)
    scratch = [pltpu.VMEM((tm, FF), BF16)]
    if loss_of is not None:
        in_specs += [_const_spec((1, D)), row_d]
        args += list(loss_of)
        out_specs = [row_f, row_f, row_d, _const_spec((1, D)), _const_spec((1, 128))]
        out_shape = [f_shape, f_shape, jax.ShapeDtypeStruct((T, D), F32), jax.ShapeDtypeStruct((1, D), F32),
                     jax.ShapeDtypeStruct((1, 128), F32)]
    else:
        out_specs = [row_d, row_f, row_f]
        out_shape = [jax.ShapeDtypeStruct((T, D), F32), f_shape, f_shape]
        if next_gain is not None:
            in_specs.append(_const_spec((1, D)))
            args.append(next_gain)
            out_specs += [row_d] + [_perm_spec(d, D) for d in DILS]
            out_shape += [jax.ShapeDtypeStruct((T, D), BF16)] + [_perm_shape(d, D, BF16) for d in DILS]
            scratch = [_tile_scratch(D)] + scratch
    out = pl.pallas_call(
        body, name=name, grid=(T // tm,), in_specs=in_specs, out_specs=out_specs, out_shape=out_shape,
        scratch_shapes=scratch,
        compiler_params=_cp(("arbitrary",) if loss_of is not None else ("parallel",)),
    )(*args)
    if next_gain is not None:
        return out[0], out[1], out[2], [out[3]] + [o.reshape(T, D) for o in out[4:]]
    return tuple(out)


def _ffn_bwd(x, g, gg_all, uu_all, dout, wg_t, wu_t, wd, name):
    tm, fc = 256, 256
    nc = FF // fc

    def body(x_ref, g_ref, gg_ref, uu_ref, do_ref, wg_ref, wu_ref, wd_ref,
             dx_ref, dgam_ref, dg_ref, du_ref, act_ref, h_ref, db_ref):
        i = pl.program_id(0)
        xv = x_ref[...]
        r = lax.rsqrt(jnp.mean(xv * xv, axis=-1, keepdims=True) + EPS)
        xhat = xv * r
        gam = g_ref[...]
        h_ref[...] = (xhat * gam).astype(BF16)
        dov = do_ref[...]
        dbv = (0.5 * dov).astype(BF16)
        db_ref[...] = dbv
        for c in range(nc):
            sl = pl.ds(c * fc, fc)
            da = _dot(dbv, wd_ref[sl, :], NT)
            gg = gg_ref[:, sl].astype(F32)
            uu = uu_ref[:, sl].astype(F32)
            s = _sig(gg)
            si = gg * s
            dgv = (da * uu * (s * (1.0 + gg * (1.0 - s)))).astype(BF16)
            duv = (da * si).astype(BF16)
            dg_ref[:, sl] = dgv
            du_ref[:, sl] = duv
            act_ref[:, sl] = (si * uu).astype(BF16)
        dh = _dot(dg_ref[...], wg_ref[...], NN) + _dot(du_ref[...], wu_ref[...], NN)

        @pl.when(i == 0)
        def _():
            dgam_ref[...] = jnp.zeros_like(dgam_ref)

        dgam_ref[...] += jnp.sum(dh * xhat, axis=0, keepdims=True)
        dxh = dh * gam
        dx_ref[...] = dov + r * (dxh - xhat * jnp.mean(dxh * xhat, axis=-1, keepdims=True))

    wspec = pl.BlockSpec((FF, D), lambda i: (0, 0), pipeline_mode=pl.Buffered(1))
    row_d = pl.BlockSpec((tm, D), lambda i: (i, 0))
    row_f = pl.BlockSpec((tm, FF), lambda i: (i, 0))
    return pl.pallas_call(
        body, name=name, grid=(T // tm,),
        in_specs=[row_d, _const_spec((1, D)), row_f, row_f, row_d, wspec, wspec, wspec],
        out_specs=[row_d, _const_spec((1, D)), row_f, row_f, row_f, row_d, row_d],
        out_shape=[jax.ShapeDtypeStruct((T, D), F32), jax.ShapeDtypeStruct((1, D), F32),
                   jax.ShapeDtypeStruct((T, FF), BF16), jax.ShapeDtypeStruct((T, FF), BF16),
                   jax.ShapeDtypeStruct((T, FF), BF16), jax.ShapeDtypeStruct((T, D), BF16),
                   jax.ShapeDtypeStruct((T, D), BF16)],
        compiler_params=_cp(("arbitrary",)),
    )(x, g, gg_all, uu_all, dout, wg_t, wu_t, wd)


def _ffn_bwd_pre(x, g, gg_all, uu_all, dout, wd, name):
    tm, fc = 512, 256
    nc = FF // fc

    def body(x_ref, g_ref, gg_ref, uu_ref, do_ref, wd_ref, dg_ref, du_ref, act_ref, h_ref, db_ref):
        xv = x_ref[...]
        r = lax.rsqrt(jnp.mean(xv * xv, axis=-1, keepdims=True) + EPS)
        h_ref[...] = (xv * r * g_ref[...]).astype(BF16)
        dbv = (0.5 * do_ref[...]).astype(BF16)
        db_ref[...] = dbv
        for c in range(nc):
            sl = pl.ds(c * fc, fc)
            da = _dot(dbv, wd_ref[sl, :], NT)
            gg = gg_ref[:, sl].astype(F32)
            uu = uu_ref[:, sl].astype(F32)
            s = _sig(gg)
            si = gg * s
            dg_ref[:, sl] = (da * uu * (s * (1.0 + gg * (1.0 - s)))).astype(BF16)
            du_ref[:, sl] = (da * si).astype(BF16)
            act_ref[:, sl] = (si * uu).astype(BF16)

    wspec = pl.BlockSpec((FF, D), lambda i: (0, 0), pipeline_mode=pl.Buffered(1))
    row_d = pl.BlockSpec((tm, D), lambda i: (i, 0))
    row_f = pl.BlockSpec((tm, FF), lambda i: (i, 0))
    return pl.pallas_call(
        body, name=name, grid=(T // tm,),
        in_specs=[row_d, _const_spec((1, D)), row_f, row_f, row_d, wspec],
        out_specs=[row_f, row_f, row_f, row_d, row_d],
        out_shape=[jax.ShapeDtypeStruct((T, FF), BF16), jax.ShapeDtypeStruct((T, FF), BF16),
                   jax.ShapeDtypeStruct((T, FF), BF16), jax.ShapeDtypeStruct((T, D), BF16),
                   jax.ShapeDtypeStruct((T, D), BF16)],
        compiler_params=_cp(("parallel",)),
    )(x, g, gg_all, uu_all, dout, wd)


def _ffn_bwd_dx(x, g, dgb, dub, dout, wg_t, wu_t, name):
    tm = 512

    def body(x_ref, g_ref, dg_ref, du_ref, do_ref, wg_ref, wu_ref, dx_ref, dgam_ref):
        i = pl.program_id(0)
        xv = x_ref[...]
        r = lax.rsqrt(jnp.mean(xv * xv, axis=-1, keepdims=True) + EPS)
        xhat = xv * r
        gam = g_ref[...]
        dh = _dot(dg_ref[...], wg_ref[...], NN) + _dot(du_ref[...], wu_ref[...], NN)

        @pl.when(i == 0)
        def _():
            dgam_ref[...] = jnp.zeros_like(dgam_ref)

        dgam_ref[...] += jnp.sum(dh * xhat, axis=0, keepdims=True)
        dxh = dh * gam
        dx_ref[...] = do_ref[...] + r * (dxh - xhat * jnp.mean(dxh * xhat, axis=-1, keepdims=True))

    wspec = pl.BlockSpec((FF, D), lambda i: (0, 0), pipeline_mode=pl.Buffered(1))
    row_d = pl.BlockSpec((tm, D), lambda i: (i, 0))
    row_f = pl.BlockSpec((tm, FF), lambda i: (i, 0))
    return pl.pallas_call(
        body, name=name, grid=(T // tm,),
        in_specs=[row_d, _const_spec((1, D)), row_f, row_f, row_d, wspec, wspec],
        out_specs=[row_d, _const_spec((1, D))],
        out_shape=[jax.ShapeDtypeStruct((T, D), F32), jax.ShapeDtypeStruct((1, D), F32)],
        compiler_params=_cp(("arbitrary",)),
    )(x, g, dgb, dub, dout, wg_t, wu_t)


def _wgrad(a, b, m, n, name):
    tm = m // 2 if m == FF else m
    return _mm(a, b, mode="tn", m=m, n=n, k=T, tm=tm, tn=n, tk=min(T, 2048), out_dtype=BF16, name=name)


PERM_TM = 512
DILS = tuple(d for _, d in GROUPS if d > 1)


def _perm_spec(dil, cols):
    return pl.BlockSpec((dil, PERM_TM // dil, cols), lambda i: (0, i, 0))


def _perm_shape(dil, cols, dtype):
    return jax.ShapeDtypeStruct((dil, T // dil, cols), dtype)


LANES = 128


def _tile_scratch(cols):
    return pltpu.VMEM((cols // LANES, PERM_TM, LANES), F32)


def _put_tile(tile, value):
    for c in range(tile.shape[0]):
        tile[c] = value[:, c * LANES:(c + 1) * LANES]


def _get_tile(tile):
    return jnp.concatenate([tile[c] for c in range(tile.shape[0])], axis=1)


def _store_perm(out_ref, tile, dil):
    for r in range(dil):
        for c in range(tile.shape[0]):
            out_ref[r, :, pl.ds(c * LANES, LANES)] = tile[c, pl.ds(r, PERM_TM // dil, stride=dil), :].astype(
                out_ref.dtype)


def _load_unperm(in_ref, tile, dil):
    for r in range(dil):
        for c in range(tile.shape[0]):
            tile[c, pl.ds(r, PERM_TM // dil, stride=dil), :] = in_ref[r, :, pl.ds(c * LANES, LANES)].astype(F32)


def _norm_cast(x, g, name):
    tm = PERM_TM

    def body(x_ref, g_ref, h_ref, *rest):
        p_refs, tile = rest[:-1], rest[-1]
        xv = x_ref[...]
        r = lax.rsqrt(jnp.mean(xv * xv, axis=-1, keepdims=True) + EPS)
        hv = xv * r * g_ref[...]
        h_ref[...] = hv.astype(BF16)
        _put_tile(tile, hv)
        for dil, p_ref in zip(DILS, p_refs):
            _store_perm(p_ref, tile, dil)

    out = pl.pallas_call(
        body, name=name, grid=(T // tm,),
        in_specs=[pl.BlockSpec((tm, D), lambda i: (i, 0)), _const_spec((1, D))],
        out_specs=[pl.BlockSpec((tm, D), lambda i: (i, 0))] + [_perm_spec(d, D) for d in DILS],
        out_shape=[jax.ShapeDtypeStruct((T, D), BF16)] + [_perm_shape(d, D, BF16) for d in DILS],
        scratch_shapes=[_tile_scratch(D)],
        compiler_params=_cp(("parallel",)),
    )(x, g)
    return [out[0]] + [o.reshape(T, D) for o in out[1:]]


def _final_math(xv, gam, tgt, dx_ref, dgam_ref, loss_ref, i):
    r = lax.rsqrt(jnp.mean(xv * xv, axis=-1, keepdims=True) + EPS)
    xhat = xv * r
    err = xhat * gam - tgt
    part = 0.5 * jnp.sum(jnp.mean(err * err, axis=-1, keepdims=True), axis=0, keepdims=True)
    dy = err * (1.0 / D)

    @pl.when(i == 0)
    def _():
        dgam_ref[...] = jnp.zeros_like(dgam_ref)
        loss_ref[...] = jnp.zeros_like(loss_ref)

    dgam_ref[...] += jnp.sum(dy * xhat, axis=0, keepdims=True)
    loss_ref[...] += jnp.broadcast_to(part, loss_ref.shape)
    dxh = dy * gam
    dx_ref[...] = r * (dxh - xhat * jnp.mean(dxh * xhat, axis=-1, keepdims=True))


def _rms_bwd(x, g, dhs, dres, name):
    tm = PERM_TM
    dils = [d for _, d in GROUPS]
    nh = len(dhs)
    assert nh == len(dils)

    def body(*refs):
        x_ref, g_ref = refs[:2]
        dh_refs = refs[2:2 + nh]
        dr_ref, dx_ref, dgam_ref, tile = refs[2 + nh:]
        i = pl.program_id(0)
        xv = x_ref[...]
        r = lax.rsqrt(jnp.mean(xv * xv, axis=-1, keepdims=True) + EPS)
        xhat = xv * r
        gam = g_ref[...]
        dh = None
        for dil, ref in zip(dils, dh_refs):
            if dil == 1:
                part = ref[...]
            else:
                _load_unperm(ref, tile, dil)
                part = _get_tile(tile)
            dh = part if dh is None else dh + part

        @pl.when(i == 0)
        def _():
            dgam_ref[...] = jnp.zeros_like(dgam_ref)

        dgam_ref[...] += jnp.sum(dh * xhat, axis=0, keepdims=True)
        dxh = dh * gam
        dx_ref[...] = dr_ref[...] + r * (dxh - xhat * jnp.mean(dxh * xhat, axis=-1, keepdims=True))

    row_d = pl.BlockSpec((tm, D), lambda i: (i, 0))
    dh_specs = [row_d if d == 1 else _perm_spec(d, D) for d in dils]
    dh_args = [a if d == 1 else a.reshape(d, T // d, D) for d, a in zip(dils, dhs)]
    return pl.pallas_call(
        body, name=name, grid=(T // tm,),
        in_specs=[row_d, _const_spec((1, D))] + dh_specs + [row_d],
        out_specs=[row_d, _const_spec((1, D))],
        out_shape=[jax.ShapeDtypeStruct((T, D), F32), jax.ShapeDtypeStruct((1, D), F32)],
        scratch_shapes=[_tile_scratch(D)],
        compiler_params=_cp(("arbitrary",)),
    )(x, g, *dh_args, dres)


CONV_TM = 256
CONV_HALO = 32
CONV_RB = 16


def _glu(ab):
    ab = ab.astype(F32)
    return ab[:, :D] * _sig(ab[:, D:])


def _ln_stats(z1):
    mu = jnp.mean(z1, axis=-1, keepdims=True)
    zc = z1 - mu
    rstd = lax.rsqrt(jnp.mean(zc * zc, axis=-1, keepdims=True) + EPS)
    return zc * rstd, rstd


def _fill_shifts(zs):
    n = zs.shape[1] - 8
    for s in range(1, 8):
        zs[s, pl.ds(0, n), :] = zs[0, pl.ds(s, n), :]


def _shifted(zs, start, rows):
    q, s = divmod(start, 8)
    return zs[s, pl.ds(8 * q, rows), :]


def _conv_fwd(ab, kern, dwb, lng, lnb, name):
    tm, hl, rb = CONV_TM, CONV_HALO, CONV_RB
    off = hl - (CONV_W - 1)

    def body(ab_ref, abh_ref, k_ref, dwb_ref, lng_ref, lnb_ref, z1_ref, z3_ref, zs):
        i = pl.program_id(0)
        zs[0, pl.ds(0, hl), :] = jnp.where(i > 0, _glu(abh_ref[...]), 0.0)
        zs[0, pl.ds(hl, tm), :] = _glu(ab_ref[...])
        _fill_shifts(zs)
        for b in range(tm // rb):
            acc = jnp.zeros((rb, D), F32)
            for j in range(CONV_W):
                acc = acc + _shifted(zs, b * rb + off + j, rb) * k_ref[pl.ds(j, 1), :]
            z1 = acc + dwb_ref[...]
            z1_ref[pl.ds(b * rb, rb), :] = z1
            zn, _ = _ln_stats(z1)
            z2 = zn * lng_ref[...] + lnb_ref[...]
            z3_ref[pl.ds(b * rb, rb), :] = (z2 * _sig(z2)).astype(BF16)

    row = pl.BlockSpec((tm, D), lambda i: (i, 0))
    return pl.pallas_call(
        body, name=name, grid=(T // tm,),
        in_specs=[pl.BlockSpec((tm, 2 * D), lambda i: (i, 0)),
                  pl.BlockSpec((hl, 2 * D), lambda i: (jnp.maximum(i * (tm // hl) - 1, 0), 0)),
                  _const_spec((32, D)), _const_spec((1, D)), _const_spec((1, D)), _const_spec((1, D))],
        out_specs=[row, row],
        out_shape=[jax.ShapeDtypeStruct((T, D), F32), jax.ShapeDtypeStruct((T, D), BF16)],
        scratch_shapes=[pltpu.VMEM((8, hl + tm, D), F32)],
        compiler_params=_cp(("parallel",)),
    )(ab, ab, kern, dwb, lng, lnb)


def _conv_bwd(dz3, z1, ab, kern, lng, lnb, name):
    tm, hl, rb = CONV_TM, CONV_HALO, CONV_RB
    off = hl - (CONV_W - 1)
    nsteps = T // tm

    def ln_bwd(dz3v, z1v, lngv, lnbv):
        zn, rstd = _ln_stats(z1v)
        z2 = zn * lngv + lnbv
        s = _sig(z2)
        dz2 = dz3v * (s * (1.0 + z2 * (1.0 - s)))
        dzn = dz2 * lngv
        dz1 = rstd * (dzn - jnp.mean(dzn, axis=-1, keepdims=True)
                      - zn * jnp.mean(dzn * zn, axis=-1, keepdims=True))
        return dz1, dz2, zn

    def body(dz3_ref, dz3h_ref, z1_ref, z1h_ref, ab_ref, abh_ref, k_ref, lng_ref, lnb_ref,
             dab_ref, dk_ref, dvec_ref, zs, dzs):
        i = pl.program_id(0)
        lngv, lnbv = lng_ref[...], lnb_ref[...]

        @pl.when(i == 0)
        def _():
            dk_ref[...] = jnp.zeros_like(dk_ref)
            dvec_ref[...] = jnp.zeros_like(dvec_ref)

        dz1, dz2, zn = ln_bwd(dz3_ref[...].astype(F32), z1_ref[...], lngv, lnbv)
        dvec_ref[pl.ds(0, 1), :] += jnp.sum(dz1, axis=0, keepdims=True)
        dvec_ref[pl.ds(1, 1), :] += jnp.sum(dz2 * zn, axis=0, keepdims=True)
        dvec_ref[pl.ds(2, 1), :] += jnp.sum(dz2, axis=0, keepdims=True)
        dzs[0, pl.ds(0, tm), :] = dz1
        dz1h, _, _ = ln_bwd(dz3h_ref[...].astype(F32), z1h_ref[...], lngv, lnbv)
        dzs[0, pl.ds(tm, hl), :] = jnp.where(i < nsteps - 1, dz1h, 0.0)
        _fill_shifts(dzs)
        zs[0, pl.ds(0, hl), :] = jnp.where(i > 0, _glu(abh_ref[...]), 0.0)
        zs[0, pl.ds(hl, tm), :] = _glu(ab_ref[...])
        _fill_shifts(zs)

        for j in range(CONV_W):
            tot = jnp.zeros((rb, D), F32)
            for b in range(tm // rb):
                tot = tot + dzs[0, pl.ds(b * rb, rb), :] * _shifted(zs, b * rb + off + j, rb)
            dk_ref[pl.ds(j, 1), :] += jnp.sum(tot, axis=0, keepdims=True)

        for b in range(tm // rb):
            acc = jnp.zeros((rb, D), F32)
            for j in range(CONV_W):
                acc = acc + _shifted(dzs, b * rb + (CONV_W - 1) - j, rb) * k_ref[pl.ds(j, 1), :]
            av = ab_ref[pl.ds(b * rb, rb), pl.ds(0, D)].astype(F32)
            sb = _sig(ab_ref[pl.ds(b * rb, rb), pl.ds(D, D)].astype(F32))
            dab_ref[pl.ds(b * rb, rb), pl.ds(0, D)] = (acc * sb).astype(BF16)
            dab_ref[pl.ds(b * rb, rb), pl.ds(D, D)] = (acc * av * sb * (1.0 - sb)).astype(BF16)

    row = pl.BlockSpec((tm, D), lambda i: (i, 0))
    nxt = pl.BlockSpec((hl, D), lambda i: (jnp.minimum((i + 1) * (tm // hl), T // hl - 1), 0))
    return pl.pallas_call(
        body, name=name, grid=(nsteps,),
        in_specs=[row, nxt, row, nxt,
                  pl.BlockSpec((tm, 2 * D), lambda i: (i, 0)),
                  pl.BlockSpec((hl, 2 * D), lambda i: (jnp.maximum(i * (tm // hl) - 1, 0), 0)),
                  _const_spec((32, D)), _const_spec((1, D)), _const_spec((1, D))],
        out_specs=[pl.BlockSpec((tm, 2 * D), lambda i: (i, 0)), _const_spec((32, D)), _const_spec((8, D))],
        out_shape=[jax.ShapeDtypeStruct((T, 2 * D), BF16), jax.ShapeDtypeStruct((32, D), F32),
                   jax.ShapeDtypeStruct((8, D), F32)],
        scratch_shapes=[pltpu.VMEM((8, hl + tm, D), F32), pltpu.VMEM((8, tm + hl, D), F32)],
        compiler_params=_cp(("arbitrary",)),
    )(dz3, dz3, z1, z1, ab, ab, kern, lng, lnb)


def _alibi_slopes():
    h = np.arange(1, 3 * NHG + 1, dtype=np.float32)
    return np.power(np.float32(2.0), -8.0 * h / np.float32(3 * NHG)).astype(np.float32)


def _band_bias(gi):
    _, dil = GROUPS[gi]
    slopes = _alibi_slopes()[gi * NHG:(gi + 1) * NHG]
    qi = np.arange(BLK)[:, None]
    ki = np.arange(2 * BLK)[None, :]
    steps = BLK + qi - ki
    band = (steps >= 0) & (steps <= BLK)
    bias = -slopes[:, None, None] * (dil * steps).astype(np.float32)[None]
    return jnp.asarray(np.where(band[None], bias, np.float32(NEG)).astype(np.float32))


QB_FWD = 4
QB_BWD = 8


def _attn_specs(qb):
    prev = lambda n: jnp.maximum(n * qb - 1, 0)
    return [pl.BlockSpec((qb * BLK, HEAD), lambda h, n: (n, h)),
            pl.BlockSpec((BLK, HEAD), lambda h, n: (prev(n), NHG + h)),
            pl.BlockSpec((qb * BLK, HEAD), lambda h, n: (n, NHG + h)),
            pl.BlockSpec((BLK, HEAD), lambda h, n: (prev(n), 2 * NHG + h)),
            pl.BlockSpec((qb * BLK, HEAD), lambda h, n: (n, 2 * NHG + h)),
            pl.BlockSpec((None, BLK, 2 * BLK), lambda h, n: (h, 0, 0))]


def _scores(q, kcat, bias, blk, seg):
    s = _dot(q, kcat, NT) * (HEAD ** -0.5) + bias
    col = lax.broadcasted_iota(jnp.int32, s.shape, 1)
    first = (blk % seg) == 0
    return jnp.where(jnp.logical_and(first, col < BLK), NEG, s)


def _attn_fwd(qkv, gi, name):
    seg = (T // GROUPS[gi][1]) // BLK

    qb = QB_FWD

    def body(q_ref, kp_ref, kc_ref, vp_ref, vc_ref, bias_ref, o_ref, l_ref):
        n = pl.program_id(0)
        for h in range(NHG):
            cols = pl.ds(h * HEAD, HEAD)
            kwin = jnp.concatenate([kp_ref[:, cols], kc_ref[:, cols]], axis=0)
            vwin = jnp.concatenate([vp_ref[:, cols], vc_ref[:, cols]], axis=0)
            bias = bias_ref[h]
            for b in range(qb):
                rows = pl.ds(b * BLK, BLK)
                s = _scores(q_ref[rows, cols], kwin[b * BLK:(b + 2) * BLK], bias, n * qb + b, seg)
                mx = jnp.max(s, axis=-1, keepdims=True)
                p = jnp.exp(s - mx)
                den = jnp.sum(p, axis=-1, keepdims=True)
                o_ref[rows, cols] = (_dot(p.astype(BF16), vwin[b * BLK:(b + 2) * BLK], NN) / den).astype(BF16)
                l_ref[rows, cols] = jnp.broadcast_to(mx + jnp.log(den), (BLK, HEAD))

    prev = lambda n: jnp.maximum(n * qb - 1, 0)
    cur = lambda part: pl.BlockSpec((qb * BLK, AW), lambda n: (n, part))
    halo = lambda part: pl.BlockSpec((BLK, AW), lambda n: (prev(n), part))
    return pl.pallas_call(
        body, name=name, grid=(T // (qb * BLK),),
        in_specs=[cur(0), halo(1), cur(1), halo(2), cur(2), _const_spec((NHG, BLK, 2 * BLK))],
        out_specs=[cur(0), cur(0)],
        out_shape=[jax.ShapeDtypeStruct((T, AW), BF16), jax.ShapeDtypeStruct((T, AW), F32)],
        compiler_params=_cp(("parallel",)),
    )(qkv, qkv, qkv, qkv, qkv, _band_bias(gi))


def _attn_bwd(qkv, dob, lse, delta, gi, name):
    seg = (T // GROUPS[gi][1]) // BLK
    qb = QB_BWD
    nb = T // (qb * BLK)
    scale = HEAD ** -0.5

    def body(q_ref, kp_ref, kc_ref, vp_ref, vc_ref, bias_ref, do_ref, l_ref, dl_ref, out_ref, dk_acc, dv_acc):
        n = pl.program_id(1)
        kwin = jnp.concatenate([kp_ref[...], kc_ref[...]], axis=0)
        vwin = jnp.concatenate([vp_ref[...], vc_ref[...]], axis=0)
        bias = bias_ref[...]
        for b in range(qb):
            rows = pl.ds(b * BLK, BLK)
            q = q_ref[rows, :]
            kcat = kwin[b * BLK:(b + 2) * BLK]
            s = _scores(q, kcat, bias, n * qb + b, seg)
            p = jnp.exp(s - l_ref[rows, pl.ds(0, 1)])
            dov = do_ref[rows, :]
            dv2 = _dot(p.astype(BF16), dov, TN)
            dp = _dot(dov, vwin[b * BLK:(b + 2) * BLK], NT)
            dsb = (p * (dp - dl_ref[rows, pl.ds(0, 1)]) * scale).astype(BF16)
            row = pl.ds(pl.multiple_of((n * qb + b) * BLK, BLK), BLK)
            out_ref[0, row, :] = _dot(dsb, kcat, NN).astype(BF16)
            dk2 = _dot(dsb, q, TN)
            dk_acc[row, :] = dk2[BLK:]
            dv_acc[row, :] = dv2[BLK:]

            def add_prev(dk2=dk2, dv2=dv2, b=b):
                prow = pl.ds(pl.multiple_of((n * qb + b - 1) * BLK, BLK), BLK)
                dk_acc[prow, :] += dk2[:BLK]
                dv_acc[prow, :] += dv2[:BLK]

            if b == 0:
                pl.when(n > 0)(add_prev)
            else:
                add_prev()

        @pl.when(n == nb - 1)
        def _():
            out_ref[1] = dk_acc[...].astype(BF16)
            out_ref[2] = dv_acc[...].astype(BF16)

    oblk = pl.BlockSpec((qb * BLK, HEAD), lambda h, n: (n, h))
    return pl.pallas_call(
        body, name=name, grid=(NHG, nb),
        in_specs=_attn_specs(qb) + [oblk, oblk, oblk],
        out_specs=pl.BlockSpec((3, T, HEAD), lambda h, n: (0, 0, h)),
        out_shape=jax.ShapeDtypeStruct((3, T, AW), BF16),
        scratch_shapes=[pltpu.VMEM((T, HEAD), F32), pltpu.VMEM((T, HEAD), F32)],
        compiler_params=_cp(("parallel", "arbitrary")),
    )(qkv, qkv, qkv, qkv, qkv, _band_bias(gi), dob, lse, delta)


def _merge(outs, lses, name):
    tm = PERM_TM
    dils = [d for _, d in GROUPS]
    ng = len(dils)

    def body(*refs):
        in_refs = refs[:2 * ng]
        ab_ref = refs[2 * ng]
        lse_refs = refs[2 * ng + 1:3 * ng + 1]
        tile = refs[-1]

        def token_order(ref, dil):
            if dil == 1:
                return ref[...].astype(F32)
            _load_unperm(ref, tile, dil)
            return _get_tile(tile)

        os = [token_order(in_refs[2 * i], d) for i, d in enumerate(dils)]
        ls = [token_order(in_refs[2 * i + 1], d) for i, d in enumerate(dils)]
        mx = jnp.maximum(jnp.maximum(ls[0], ls[1]), ls[2])
        es = [jnp.exp(v - mx) for v in ls]
        tot = es[0] + es[1] + es[2]
        att = (es[0] / tot) * os[0] + (es[1] / tot) * os[1] + (es[2] / tot) * os[2]
        ab_ref[...] = att.astype(BF16)
        lse = mx + jnp.log(tot)
        _put_tile(tile, lse)
        for dil, ref in zip(dils, lse_refs):
            if dil == 1:
                ref[...] = lse
            else:
                _store_perm(ref, tile, dil)

    row = pl.BlockSpec((tm, AW), lambda i: (i, 0))
    specs = [row if d == 1 else _perm_spec(d, AW) for d in dils]
    args = []
    for d, o, l in zip(dils, outs, lses):
        args += [o, l] if d == 1 else [o.reshape(d, T // d, AW), l.reshape(d, T // d, AW)]
    out = pl.pallas_call(
        body, name=name, grid=(T // tm,),
        in_specs=[sp for sp in specs for _ in range(2)], out_specs=[row] + specs,
        out_shape=[jax.ShapeDtypeStruct((T, AW), BF16)]
        + [jax.ShapeDtypeStruct((T, AW), F32) if d == 1 else _perm_shape(d, AW, F32) for d in dils],
        scratch_shapes=[_tile_scratch(AW)],
        compiler_params=_cp(("parallel",)),
    )(*args)
    return out[0], [o.reshape(T, AW) for o in out[1:]]


def _mix_out(z3b, attnb, gates, wc, wa_t, wo, x1, name):
    tm = 512

    def body(z_ref, a_ref, g_ref, wc_ref, wa_ref, wo_ref, x_ref, xo_ref, yc_ref, ya_ref, mx_ref):
        yc = _dot(z_ref[...], wc_ref[...], NN)
        ya = _dot(a_ref[...], wa_ref[...], NT)
        yc_ref[...] = yc.astype(BF16)
        ya_ref[...] = ya.astype(BF16)
        gv = g_ref[...].astype(F32)
        mixed = (_sig(gv[:, :D]) * yc + _sig(gv[:, D:]) * ya).astype(BF16)
        mx_ref[...] = mixed
        xo_ref[...] = x_ref[...] + _dot(mixed, wo_ref[...], NN)

    row = pl.BlockSpec((tm, D), lambda i: (i, 0))
    return pl.pallas_call(
        body, name=name, grid=(T // tm,),
        in_specs=[row, pl.BlockSpec((tm, AW), lambda i: (i, 0)), pl.BlockSpec((tm, 2 * D), lambda i: (i, 0)),
                  _const_spec((D, D)), _const_spec((D, AW)), _const_spec((D, D)), row],
        out_specs=[row, row, row, row],
        out_shape=[jax.ShapeDtypeStruct((T, D), F32), jax.ShapeDtypeStruct((T, D), BF16),
                   jax.ShapeDtypeStruct((T, D), BF16), jax.ShapeDtypeStruct((T, D), BF16)],
        compiler_params=_cp(("parallel",)),
    )(z3b, attnb, gates, wc, wa_t, wo, x1)


def _mix_out_bwd(dx2, gates, yc, ya, attn, wc, wa_t, wo, name):
    tm = PERM_TM
    dils = [d for _, d in GROUPS]
    ng = len(dils)

    def body(dx_ref, g_ref, yc_ref, ya_ref, at_ref, wc_ref, wa_ref, wo_ref,
             dg_ref, dyc_ref, dya_ref, dxb_ref, dz3_ref, *rest):
        dat_refs, dl_refs, tile = rest[:ng], rest[ng:2 * ng], rest[-1]
        dxb = dx_ref[...].astype(BF16)
        dxb_ref[...] = dxb
        dmix = _dot(dxb, wo_ref[...], NT)
        gv = g_ref[...].astype(F32)
        sc = _sig(gv[:, :D])
        sa = _sig(gv[:, D:])
        ycv, yav = yc_ref[...].astype(F32), ya_ref[...].astype(F32)
        dg_ref[:, pl.ds(0, D)] = (dmix * ycv * sc * (1.0 - sc)).astype(BF16)
        dg_ref[:, pl.ds(D, D)] = (dmix * yav * sa * (1.0 - sa)).astype(BF16)
        dyc = (dmix * sc).astype(BF16)
        dya = (dmix * sa).astype(BF16)
        dyc_ref[...] = dyc
        dya_ref[...] = dya
        dz3_ref[...] = _dot(dyc, wc_ref[...], NT).astype(BF16)
        dat = _dot(dya, wa_ref[...], NN)
        prod = dat * at_ref[...].astype(F32)
        delta = jnp.concatenate(
            [jnp.broadcast_to(jnp.sum(prod[:, h * HEAD:(h + 1) * HEAD], axis=-1, keepdims=True), (tm, HEAD))
             for h in range(NHG)], axis=1)
        for value, out_refs in ((dat, dat_refs), (delta, dl_refs)):
            _put_tile(tile, value)
            for dil, ref in zip(dils, out_refs):
                if dil == 1:
                    ref[...] = value.astype(ref.dtype)
                else:
                    _store_perm(ref, tile, dil)

    row = pl.BlockSpec((tm, D), lambda i: (i, 0))
    row2 = pl.BlockSpec((tm, 2 * D), lambda i: (i, 0))
    rowa = pl.BlockSpec((tm, AW), lambda i: (i, 0))
    aspecs = [rowa if d == 1 else _perm_spec(d, AW) for d in dils]

    def ashapes(dtype):
        return [jax.ShapeDtypeStruct((T, AW), dtype) if d == 1 else _perm_shape(d, AW, dtype) for d in dils]

    out = pl.pallas_call(
        body, name=name, grid=(T // tm,),
        in_specs=[row, row2, row, row, rowa, _const_spec((D, D)), _const_spec((D, AW)), _const_spec((D, D))],
        out_specs=[row2, row, row, row, row] + aspecs + aspecs,
        out_shape=[jax.ShapeDtypeStruct((T, 2 * D), BF16), jax.ShapeDtypeStruct((T, D), BF16),
                   jax.ShapeDtypeStruct((T, D), BF16), jax.ShapeDtypeStruct((T, D), BF16),
                   jax.ShapeDtypeStruct((T, D), BF16)] + ashapes(BF16) + ashapes(F32),
        scratch_shapes=[_tile_scratch(AW)],
        compiler_params=_cp(("parallel",)),
    )(dx2, gates, yc, ya, attn, wc, wa_t, wo)
    dats = [o.reshape(T, AW) for o in out[5:5 + ng]]
    deltas = [o.reshape(T, AW) for o in out[5 + ng:5 + 2 * ng]]
    return out[0], out[1], out[2], out[3], out[4], dats, deltas


def _peer(k):
    x, y, c = lax.axis_index("x"), lax.axis_index("y"), lax.axis_index("c")
    px = 1 - x if k & 4 else x
    py = 1 - y if k & 2 else y
    pc = 1 - c if k & 1 else c
    return (px, py, pc), 4 * px + 2 * py + pc


HBM_SPEC = pl.BlockSpec(memory_space=pltpu.HBM)
SEM_SPEC = pl.BlockSpec(memory_space=pltpu.SEMAPHORE)
EFFECT = pltpu.SideEffectType.DATAFLOW_SIDE_EFFECTING


def _my_place():
    return 4 * lax.axis_index("x") + 2 * lax.axis_index("y") + lax.axis_index("c")


def _tie(a, order_after, name):
    na = len(order_after)

    def body(*refs):
        del refs

    return pl.pallas_call(
        body, name=name, in_specs=[pl.BlockSpec(memory_space=pl.ANY)] * (1 + na),
        out_specs=pl.BlockSpec(memory_space=pl.ANY), out_shape=jax.ShapeDtypeStruct(a.shape, a.dtype),
        input_output_aliases={0: 0},
    )(a, *order_after)


def _prep_gather(ws, order_after, name):
    me = jnp.reshape(_my_place(), (1,)).astype(jnp.int32)
    n = len(ws)
    na = len(order_after)
    shapes = [((32, wv.shape[1]), F32) if wv.shape[0] == CONV_W else (wv.shape, BF16) for wv in ws]

    def body(me_ref, *refs):
        del me_ref
        ins, outs = refs[:n], refs[n + na:]
        for wv, i_ref, o_ref in zip(ws, ins, outs):
            if wv.shape[0] == CONV_W:
                o_ref[pl.ds(0, CONV_W), :] = i_ref[...]
                o_ref[pl.ds(CONV_W, 1), :] = jnp.zeros((1, wv.shape[1]), F32)
            else:
                o_ref[...] = i_ref[...].astype(BF16)

    grid_spec = pltpu.PrefetchScalarGridSpec(
        num_scalar_prefetch=1, grid=(1,),
        in_specs=[pl.BlockSpec(wv.shape, lambda i, m: (0, 0)) for wv in ws]
        + [pl.BlockSpec(memory_space=pl.ANY)] * na,
        out_specs=[pl.BlockSpec(shp, lambda i, m: (m[0], 0)) for shp, _ in shapes])
    return pl.pallas_call(
        body, name=name, grid_spec=grid_spec,
        out_shape=[jax.ShapeDtypeStruct((NDEV * shp[0], shp[1]), dt) for shp, dt in shapes],
        compiler_params=_cp(("arbitrary",)),
    )(me, *ws, *order_after)


GATHER_A = ((1, 0), (2, 0), (4, 0), (6, 0))
GATHER_B = ((1, 2), (1, 4), (1, 6))


def _gather_start(lands, plan, order_after, name):
    n = len(lands)
    na = len(order_after)
    npl = len(plan)

    def body(*refs):
        land_refs = refs[:n]
        send, recv = refs[n + na], refs[n + na + 1]
        token = refs[-1]
        for w in range(n):
            rows = lands[w].shape[0] // NDEV
            for p, (k, j) in enumerate(plan):
                peer, _ = _peer(k)
                _, blk = _peer(j)
                part = land_refs[w].at[pl.ds(blk * rows, rows)]
                i = w * npl + p
                pltpu.make_async_remote_copy(src_ref=part, dst_ref=part, send_sem=send.at[i], recv_sem=recv.at[i],
                                             device_id=peer, device_id_type=MESH_ID).start()
        token[...] = jnp.zeros_like(token)

    nsem = n * npl
    bufs = [pltpu.with_memory_space_constraint(a, pltpu.HBM) for a in lands]
    out = pl.pallas_call(
        body, name=name,
        in_specs=[HBM_SPEC] * n + [pl.BlockSpec(memory_space=pl.ANY)] * na,
        out_specs=[SEM_SPEC, SEM_SPEC] + [HBM_SPEC] * n + [pl.BlockSpec(memory_space=pltpu.VMEM)],
        out_shape=[pltpu.SemaphoreType.DMA((nsem,)), pltpu.SemaphoreType.DMA((nsem,))]
        + [pltpu.HBM(a.shape, a.dtype) for a in bufs] + [jax.ShapeDtypeStruct((8, 128), F32)],
        input_output_aliases={i: 2 + i for i in range(n)},
        compiler_params=pltpu.CompilerParams(has_side_effects=EFFECT),
    )(*bufs, *order_after)
    return out[0], out[1], out[2:2 + n], out[-1]


def _gather_wait(started, plan, order_after, name):
    send, recv, lands, _ = started
    n = len(lands)
    na = len(order_after)
    npl = len(plan)

    def body(*refs):
        land_refs = refs[:n]
        send_ref, recv_ref = refs[n], refs[n + 1]
        for w in range(n):
            rows = lands[w].shape[0] // NDEV
            for p, (k, j) in enumerate(plan):
                peer, _ = _peer(k)
                _, blk = _peer(j)
                part = land_refs[w].at[pl.ds(blk * rows, rows)]
                i = w * npl + p
                cp = pltpu.make_async_remote_copy(src_ref=part, dst_ref=part, send_sem=send_ref.at[i],
                                                  recv_sem=recv_ref.at[i], device_id=peer, device_id_type=MESH_ID)
                cp.wait_send()
                cp.wait_recv()

    out = pl.pallas_call(
        body, name=name,
        in_specs=[HBM_SPEC] * n + [SEM_SPEC, SEM_SPEC] + [pl.BlockSpec(memory_space=pl.ANY)] * na,
        out_specs=[HBM_SPEC] * n,
        out_shape=[pltpu.HBM(a.shape, a.dtype) for a in lands],
        input_output_aliases={i: i for i in range(n)},
        compiler_params=pltpu.CompilerParams(has_side_effects=EFFECT),
    )(*lands, send, recv, *order_after)
    return list(out)


def _copy_ends(kind, src, land, me, plin, k):
    if kind == "scatter":
        rows = src.shape[0] // NDEV
        return src.at[pl.ds(plin * rows, rows)], land.at[k - 1]
    return src, land.at[me]


def _landing(kind, src):
    me = _my_place()
    if kind == "scatter":
        return lax.empty((NDEV - 1, src.shape[0] // NDEV) + src.shape[1:], src.dtype)
    land = lax.empty((NDEV,) + src.shape, src.dtype)
    return lax.dynamic_update_slice(land, src[None], (me,) + (0,) * src.ndim)


def _send_start(kinds, srcs, order_after, name):
    n = len(srcs)
    lands = [_landing(kd, s) for kd, s in zip(kinds, srcs)]
    na = len(order_after)

    def body(*refs):
        src_refs, land_refs = refs[:n], refs[n:2 * n]
        send, recv = refs[2 * n + na], refs[2 * n + na + 1]
        token = refs[-1]
        _, me = _peer(0)
        for w in range(n):
            for k in range(1, NDEV):
                peer, plin = _peer(k)
                s, d = _copy_ends(kinds[w], src_refs[w], land_refs[w], me, plin, k)
                i = w * (NDEV - 1) + k - 1
                pltpu.make_async_remote_copy(src_ref=s, dst_ref=d, send_sem=send.at[i], recv_sem=recv.at[i],
                                             device_id=peer, device_id_type=MESH_ID).start()
        token[...] = jnp.zeros_like(token)

    nsem = n * (NDEV - 1)
    bufs = [pltpu.with_memory_space_constraint(a, pltpu.HBM) for a in list(srcs) + lands]
    out = pl.pallas_call(
        body, name=name,
        in_specs=[HBM_SPEC] * (2 * n) + [pl.BlockSpec(memory_space=pl.ANY)] * na,
        out_specs=[SEM_SPEC, SEM_SPEC] + [HBM_SPEC] * (2 * n) + [pl.BlockSpec(memory_space=pltpu.VMEM)],
        out_shape=[pltpu.SemaphoreType.DMA((nsem,)), pltpu.SemaphoreType.DMA((nsem,))]
        + [pltpu.HBM(a.shape, a.dtype) for a in bufs] + [jax.ShapeDtypeStruct((8, 128), F32)],
        input_output_aliases={i: 2 + i for i in range(2 * n)},
        compiler_params=pltpu.CompilerParams(has_side_effects=EFFECT),
    )(*bufs, *order_after)
    return out[0], out[1], out[2:2 + n], out[2 + n:2 + 2 * n], out[-1]


def _send_wait(kinds, started, order_after, name):
    send, recv, srcs, lands, _ = started
    n = len(srcs)
    na = len(order_after)

    def body(*refs):
        src_refs, land_refs = refs[:n], refs[n:2 * n]
        send_ref, recv_ref = refs[2 * n], refs[2 * n + 1]
        _, me = _peer(0)
        for w in range(n):
            for k in range(1, NDEV):
                peer, plin = _peer(k)
                s, d = _copy_ends(kinds[w], src_refs[w], land_refs[w], me, plin, k)
                i = w * (NDEV - 1) + k - 1
                cp = pltpu.make_async_remote_copy(src_ref=s, dst_ref=d, send_sem=send_ref.at[i],
                                                  recv_sem=recv_ref.at[i], device_id=peer, device_id_type=MESH_ID)
                cp.wait_send()
                cp.wait_recv()

    bufs = list(srcs) + list(lands)
    out = pl.pallas_call(
        body, name=name,
        in_specs=[HBM_SPEC] * (2 * n) + [SEM_SPEC, SEM_SPEC] + [pl.BlockSpec(memory_space=pl.ANY)] * na,
        out_specs=[HBM_SPEC] * (2 * n),
        out_shape=[pltpu.HBM(a.shape, a.dtype) for a in bufs],
        input_output_aliases={i: i for i in range(2 * n)},
        compiler_params=pltpu.CompilerParams(has_side_effects=EFFECT),
    )(*bufs, send, recv, *order_after)
    return out[:n], out[n:]


def _gsum(own, land, name):
    rows, cols = own.shape
    tr = rows // 2 if rows * cols > 512 * 1024 and rows % 32 == 0 else rows

    def body(own_ref, l_ref, o_ref):
        tot = own_ref[...].astype(F32)
        for s in range(NDEV - 1):
            tot = tot + l_ref[s].astype(F32)
        o_ref[...] = tot

    return pl.pallas_call(
        body, name=name, grid=(rows // tr,),
        in_specs=[pl.BlockSpec((tr, cols), lambda i: (i, 0)),
                  pl.BlockSpec((NDEV - 1, tr, cols), lambda i: (0, i, 0))],
        out_specs=pl.BlockSpec((tr, cols), lambda i: (i, 0)),
        out_shape=jax.ShapeDtypeStruct((rows, cols), F32),
        compiler_params=_cp(("parallel",)),
    )(own, land)


def _adamw_math(w, g, m, v):
    m2 = B1 * m + (1.0 - B1) * g
    v2 = B2 * v + (1.0 - B2) * (g * g)
    m_hat = m2 / (1.0 - B1 ** STEP)
    v_hat = v2 / (1.0 - B2 ** STEP)
    delta = -LR * (m_hat / (jnp.sqrt(v_hat) + AEPS) + WD * w)
    return delta, m2, v2


def _adamw(w, g, m, v, name):
    rows, cols = w.shape
    tr = 256 if rows % 256 == 0 and rows > 256 else rows

    def body(w_ref, g_ref, m_ref, v_ref, d_ref, mo_ref, vo_ref):
        d, m2, v2 = _adamw_math(w_ref[...], g_ref[...], m_ref[...], v_ref[...])
        d_ref[...] = d
        mo_ref[...] = m2
        vo_ref[...] = v2

    blk = pl.BlockSpec((tr, cols), lambda i: (i, 0))
    return pl.pallas_call(
        body, name=name, grid=(rows // tr,), in_specs=[blk] * 4, out_specs=[blk] * 3,
        out_shape=[jax.ShapeDtypeStruct((rows, cols), F32)] * 3,
        compiler_params=_cp(("parallel",)),
    )(w, g, m, v)


UPD_TC = 256


def _update(src, land, w, m, v, name):
    rows, cols = land.shape[1:]
    tc = min(UPD_TC, cols)
    me = jnp.reshape(_my_place(), (1,)).astype(jnp.int32)

    def body(me_ref, own_ref, l_ref, w_ref, m_ref, v_ref, g_ref, d_ref, mo_ref, vo_ref):
        del me_ref
        g = own_ref[...].astype(F32)
        for s in range(NDEV - 1):
            g = g + l_ref[s].astype(F32)
        g_ref[...] = g
        d, m2, v2 = _adamw_math(w_ref[...], g, m_ref[...], v_ref[...])
        d_ref[...] = d
        mo_ref[...] = m2
        vo_ref[...] = v2

    wblk = pl.BlockSpec((rows, tc), lambda j, p: (0, j))
    grid_spec = pltpu.PrefetchScalarGridSpec(
        num_scalar_prefetch=1, grid=(cols // tc,),
        in_specs=[pl.BlockSpec((rows, tc), lambda j, p: (p[0], j)),
                  pl.BlockSpec((NDEV - 1, rows, tc), lambda j, p: (0, 0, j)), wblk, wblk, wblk],
        out_specs=[wblk] * 4)
    return pl.pallas_call(
        body, name=name, grid_spec=grid_spec, out_shape=[jax.ShapeDtypeStruct((rows, cols), F32)] * 4,
        compiler_params=_cp(("parallel",)),
    )(me, src, land, w, m, v)


def _small_update(vland, w8, m8, v8, name):
    def body(l_ref, w_ref, m_ref, v_ref, g_ref, d_ref, mo_ref, vo_ref):
        g = l_ref[0]
        for s in range(1, NDEV):
            g = g + l_ref[s]
        g_ref[...] = g
        d, m2, v2 = _adamw_math(w_ref[...], g, m_ref[...], v_ref[...])
        d_ref[...] = d
        mo_ref[...] = m2
        vo_ref[...] = v2

    return pl.pallas_call(
        body, name=name, out_shape=[jax.ShapeDtypeStruct((8, D), F32)] * 4,
        compiler_params=_cp(None),
    )(vland, w8, m8, v8)


def kernel(x, ffn1_norm, ffn1_w_gate, ffn1_w_up, ffn1_w_down, mix_norm, w_in, conv_dw_kernel, conv_dw_bias, conv_ln_gain, conv_ln_bias, conv_w_out, attn_w_out, w_o, ffn2_norm, ffn2_w_gate, ffn2_w_up, ffn2_w_down, final_norm, loss_target, m_ffn1_norm, m_ffn1_w_gate, m_ffn1_w_up, m_ffn1_w_down, m_mix_norm, m_w_in, m_conv_dw_kernel, m_conv_dw_bias, m_conv_ln_gain, m_conv_ln_bias, m_conv_w_out, m_attn_w_out, m_w_o, m_ffn2_norm, m_ffn2_w_gate, m_ffn2_w_up, m_ffn2_w_down, m_final_norm, v_ffn1_norm, v_ffn1_w_gate, v_ffn1_w_up, v_ffn1_w_down, v_mix_norm, v_w_in, v_conv_dw_kernel, v_conv_dw_bias, v_conv_ln_gain, v_conv_ln_bias, v_conv_w_out, v_attn_w_out, v_w_o, v_ffn2_norm, v_ffn2_w_gate, v_ffn2_w_up, v_ffn2_w_down, v_final_norm):
    names = ["ffn1_norm", "ffn1_w_gate", "ffn1_w_up", "ffn1_w_down", "mix_norm", "w_in", "conv_dw_kernel",
             "conv_dw_bias", "conv_ln_gain", "conv_ln_bias", "conv_w_out", "attn_w_out", "w_o", "ffn2_norm",
             "ffn2_w_gate", "ffn2_w_up", "ffn2_w_down", "final_norm"]
    w = dict(ffn1_norm=ffn1_norm, ffn1_w_gate=ffn1_w_gate, ffn1_w_up=ffn1_w_up, ffn1_w_down=ffn1_w_down, mix_norm=mix_norm, w_in=w_in, conv_dw_kernel=conv_dw_kernel, conv_dw_bias=conv_dw_bias, conv_ln_gain=conv_ln_gain, conv_ln_bias=conv_ln_bias, conv_w_out=conv_w_out, attn_w_out=attn_w_out, w_o=w_o, ffn2_norm=ffn2_norm, ffn2_w_gate=ffn2_w_gate, ffn2_w_up=ffn2_w_up, ffn2_w_down=ffn2_w_down, final_norm=final_norm)
    mo = dict(ffn1_norm=m_ffn1_norm, ffn1_w_gate=m_ffn1_w_gate, ffn1_w_up=m_ffn1_w_up, ffn1_w_down=m_ffn1_w_down, mix_norm=m_mix_norm, w_in=m_w_in, conv_dw_kernel=m_conv_dw_kernel, conv_dw_bias=m_conv_dw_bias, conv_ln_gain=m_conv_ln_gain, conv_ln_bias=m_conv_ln_bias, conv_w_out=m_conv_w_out, attn_w_out=m_attn_w_out, w_o=m_w_o, ffn2_norm=m_ffn2_norm, ffn2_w_gate=m_ffn2_w_gate, ffn2_w_up=m_ffn2_w_up, ffn2_w_down=m_ffn2_w_down, final_norm=m_final_norm)
    vo = dict(ffn1_norm=v_ffn1_norm, ffn1_w_gate=v_ffn1_w_gate, ffn1_w_up=v_ffn1_w_up, ffn1_w_down=v_ffn1_w_down, mix_norm=v_mix_norm, w_in=v_w_in, conv_dw_kernel=v_conv_dw_kernel, conv_dw_bias=v_conv_dw_bias, conv_ln_gain=v_conv_ln_gain, conv_ln_bias=v_conv_ln_bias, conv_w_out=v_conv_w_out, attn_w_out=v_attn_w_out, w_o=v_w_o, ffn2_norm=v_ffn2_norm, ffn2_w_gate=v_ffn2_w_gate, ffn2_w_up=v_ffn2_w_up, ffn2_w_down=v_ffn2_w_down, final_norm=v_final_norm)
    col_sharded = ("ffn1_w_gate", "ffn1_w_up", "w_in", "attn_w_out", "ffn2_w_gate", "ffn2_w_up")
    row_sharded = ("ffn1_w_down", "conv_w_out", "w_o", "ffn2_w_down")
    small = ("ffn1_norm", "mix_norm", "ffn2_norm", "final_norm", "conv_dw_bias", "conv_ln_gain", "conv_ln_bias")

    def landing_view(a, n):
        return jnp.transpose(a[0]) if n in col_sharded else a[0]

    def own_view(a, n):
        return jnp.transpose(a)[None] if n in col_sharded else a[None]

    ag_groups = (("ffn1_w_gate", "ffn1_w_up", "ffn1_w_down"),
                 ("w_in", "attn_w_out", "conv_w_out", "w_o", "conv_dw_kernel"),
                 ("ffn2_w_gate", "ffn2_w_up", "ffn2_w_down"))
    ag, order = [], []
    for gi, grp in enumerate(ag_groups):
        lands = _prep_gather([landing_view(w[n], n) for n in grp], order, f"gather_prep{gi}")
        st = _gather_start(lands, GATHER_A, [], f"gather_a_start{gi}")
        ag.append(st)
        order = [st[3]]

    def chips_in(gi, after):
        lands = _gather_wait(ag[gi], GATHER_A, after, f"gather_a_wait{gi}")
        return _gather_start(lands, GATHER_B, [], f"gather_b_start{gi}")

    def all_in(gi, st, after):
        return _gather_wait(st, GATHER_B, after, f"gather_b_wait{gi}")

    x0 = x[0]
    tgt = loss_target[0]
    gf = final_norm.reshape(1, D)

    wg1, wu1, wd1 = all_in(0, chips_in(0, [ag[2][3]]), [])
    x1, gg1, uu1, h2p = _ffn_fwd(x0, ffn1_norm, wg1, wu1, wd1, "ffn1_fwd", next_gain=mix_norm)
    h2 = h2p[0]
    win_t, wa_t, wc, wo, kern_blocks = all_in(1, chips_in(1, [x1]), [])
    kern = kern_blocks.reshape(NDEV, 32, D // NDEV).transpose(1, 0, 2).reshape(32, D)
    ptm = min(T, 2048)
    ab = _mm(h2, win_t, mode="nt", m=T, n=2 * D, k=D, tm=ptm, tn=512, tk=D, out_dtype=BF16, name="proj_conv")
    gates = _mm(h2, win_t, mode="nt", m=T, n=2 * D, k=D, tm=ptm, tn=512, tk=D, out_dtype=BF16,
                b_map=lambda i, j, kk: (13 + j, 0), name="proj_gates")
    qkv = []
    for gi in range(len(GROUPS)):
        qkv.append(_mm(h2p[gi], win_t, mode="nt", m=T, n=3 * AW, k=D, tm=ptm, tn=AW, tk=D, out_dtype=BF16,
                       b_map=lambda i, j, kk, gi=gi: (4 + gi + 3 * j, 0), name=f"proj_qkv{gi}"))
    z1, z3b = _conv_fwd(ab, kern, conv_dw_bias, conv_ln_gain, conv_ln_bias, "conv_fwd")
    ffn2_b = chips_in(2, [z3b])
    outs, lses = [], []
    for gi, (_, dil) in enumerate(GROUPS):
        o, l = _attn_fwd(qkv[gi], gi, f"attn_fwd{gi}")
        outs.append(o)
        lses.append(l)
    attnb, lse = _merge(outs, lses, "attn_merge")
    x2, yc, ya, mixedb = _mix_out(z3b, attnb, gates, wc, wa_t, wo, x1, "mix_out_fwd")
    wg2, wu2, wd2 = all_in(2, ffn2_b, [x2])
    gg2, uu2, dx3, dgf, loss_part = _ffn_fwd(x2, ffn2_norm, wg2, wu2, wd2, "ffn2_fwd", loss_of=(gf, tgt))

    dx2, dg3, dgb, dub, actb, hb, dob = _ffn_bwd(x2, ffn2_norm, gg2, uu2, dx3, wg2, wu2, wd2, "ffn2_bwd")
    grads = {}
    grads["ffn2_w_gate"] = _wgrad(dgb, hb, FF, D, "ffn2_dwg")
    grads["ffn2_w_up"] = _wgrad(dub, hb, FF, D, "ffn2_dwu")
    grads["ffn2_w_down"] = _wgrad(actb, dob, FF, D, "ffn2_dwd")
    rs_groups = [("ffn2_w_gate", "ffn2_w_up", "ffn2_w_down"),
                 ("attn_w_out", "conv_w_out", "w_o", "conv_dw_kernel"),
                 ("w_in",),
                 ("ffn1_w_gate",), ("ffn1_w_up",), ("ffn1_w_down",), ()]
    last = len(rs_groups) - 1
    rs = [_send_start(["scatter"] * 3, [grads[n] for n in rs_groups[0]], [], "scatter_start0")]
    dx2 = _tie(dx2, [rs[0][4]], "tie_after_scatter0")

    dgates, dycb, dyab, dx2b, dz3, dattnb, delta = _mix_out_bwd(dx2, gates, yc, ya, attnb, wc, wa_t, wo, "mix_out_bwd")
    grads["w_o"] = _wgrad(mixedb, dx2b, D, D, "dw_o")
    grads["conv_w_out"] = _wgrad(z3b, dycb, D, D, "dw_conv_out")
    grads["attn_w_out"] = _wgrad(dyab, attnb, D, AW, "dw_attn_out")
    dab, dkern, dvec = _conv_bwd(dz3, z1, ab, kern, conv_ln_gain, conv_ln_bias, "conv_bwd")
    grads["conv_dw_kernel"] = dkern.reshape(32, NDEV, D // NDEV).transpose(1, 0, 2).reshape(NDEV * 32, D // NDEV)
    rs.append(_send_start(["scatter"] * 4, [grads[n] for n in rs_groups[1]], [rs[0][4]], "scatter_start1"))
    dattnb = [_tie(a, [rs[1][4]], f"tie_after_scatter1_{i}") for i, a in enumerate(dattnb)]

    dqkv = []
    for gi, (_, dil) in enumerate(GROUPS):
        dq3 = _attn_bwd(qkv[gi], dattnb[gi], lse[gi], delta[gi], gi, f"attn_bwd{gi}")
        dqkv.append(dq3.reshape(3 * T, AW))

    dwin = _mm(dab, h2, mode="tn", m=2 * D, n=D, k=T, tm=2 * D, tn=D, tk=512, out_dtype=BF16, out_rows=IN_W,
               name="dw_in_conv")
    dwin = _mm(dgates, h2, mode="tn", m=2 * D, n=D, k=T, tm=512, tn=D, tk=1024, out_dtype=BF16, out_rows=IN_W,
               o_map=lambda i, j, kk: (13 + i, 0), passthru=dwin, name="dw_in_gates")
    for gi in range(3):
        dwin = _mm(dqkv[gi], h2p[gi], mode="tn", m=3 * AW, n=D, k=T, tm=AW, tn=D, tk=1024, out_dtype=BF16,
                   out_rows=IN_W, a_map=lambda i, j, kk: (i * (T // 1024) + kk, 0),
                   o_map=lambda i, j, kk, gi=gi: (4 + gi + 3 * i, 0), passthru=dwin, name=f"dw_in_qkv{gi}")
    grads["w_in"] = dwin
    rs.append(_send_start(["scatter"], [dwin], [rs[1][4]], "scatter_start2"))
    dab = _tie(dab, [rs[2][4]], "tie_after_scatter2")

    nrow = T // 1024
    dh = _mm(dab, win_t, mode="nn", m=T, n=D, k=2 * D, tm=1024, tn=D, tk=1024, out_dtype=F32, name="dproj_conv")
    dh = _mm(dgates, win_t, mode="nn", m=T, n=D, k=2 * D, tm=1024, tn=D, tk=512, out_dtype=F32,
             b_map=lambda i, j, kk: (13 + kk, 0), init=dh, name="dproj_gates")
    dhs = []
    for gi, (_, dil) in enumerate(GROUPS):
        part = _mm(dqkv[gi], win_t, mode="nn", m=T, n=D, k=3 * AW, tm=1024, tn=D, tk=AW,
                   out_dtype=F32 if gi == 0 else BF16,
                   a_map=lambda i, j, kk: (kk * nrow + i, 0), b_map=lambda i, j, kk, gi=gi: (4 + gi + 3 * kk, 0),
                   init=dh if gi == 0 else None, name=f"dproj_qkv{gi}")
        dhs.append(part)
    dx1, dg2 = _rms_bwd(x1, mix_norm, dhs, dx2, "mix_norm_bwd")

    dgb, dub, actb, hb, dob = _ffn_bwd_pre(x0, ffn1_norm, gg1, uu1, dx1, wd1, "ffn1_bwd_pre")
    grads["ffn1_w_gate"] = _wgrad(dgb, hb, FF, D, "ffn1_dwg")
    rs.append(_send_start(["scatter"], [grads["ffn1_w_gate"]], [rs[2][4]], "scatter_start3"))
    hb = _tie(hb, [rs[3][4]], "tie_after_scatter3")
    grads["ffn1_w_up"] = _wgrad(dub, hb, FF, D, "ffn1_dwu")
    rs.append(_send_start(["scatter"], [grads["ffn1_w_up"]], [rs[3][4]], "scatter_start4"))
    dob = _tie(dob, [rs[4][4]], "tie_after_scatter4")
    grads["ffn1_w_down"] = _wgrad(actb, dob, FF, D, "ffn1_dwd")
    rs.append(_send_start(["scatter"], [grads["ffn1_w_down"]], [rs[4][4]], "scatter_start5"))
    dgb = _tie(dgb, [rs[5][4]], "tie_after_scatter5")
    dx0, dg1 = _ffn_bwd_dx(x0, ffn1_norm, dgb, dub, dx1, wg1, wu1, "ffn1_bwd_dx")
    vec = jnp.concatenate([dg1, dg2, dg3, dgf, dvec[0:3], jnp.broadcast_to(loss_part[:, :1], (1, D))], axis=0)
    rs.append(_send_start(["bcast"], [vec], [rs[5][4]], "scatter_start6"))

    g_out, d_out, m_out, v_out = {}, {}, {}, {}
    me = _my_place()
    after = [rs[last][4]]
    for gi, grp in enumerate(rs_groups):
        kinds = ["scatter"] * len(grp) + (["bcast"] if gi == last else [])
        srcs, lands = _send_wait(kinds, rs[gi], after, f"scatter_wait{gi}")
        for n, src, land in zip(grp, srcs, lands):
            if n == "conv_dw_kernel":
                rows = src.shape[0] // NDEV
                own = lax.dynamic_slice(src, (me * rows, 0), (rows, src.shape[1]))
                g = _gsum(own, land, f"gsum_{n}")[:CONV_W]
                d, m2, v2 = _adamw(w[n][0], g, mo[n][0], vo[n][0], f"adamw_{n}")
                after = [d]
                g, d, m2, v2 = g[None], d[None], m2[None], v2[None]
            else:
                res = _update(src, land, landing_view(w[n], n), landing_view(mo[n], n), landing_view(vo[n], n),
                              f"update_{n}")
                after = [res[1]]
                g, d, m2, v2 = (own_view(a, n) for a in res)
            g_out[n], d_out[n], m_out[n], v_out[n] = g, d, m2, v2
    vland = lands[-1]

    def rows8(src):
        return jnp.concatenate([src[n].reshape(1, D) for n in small] + [jnp.ones((1, D), F32)], axis=0)

    g8, d8, m8, v8 = _small_update(vland, rows8(w), rows8(mo), rows8(vo), "small_update")
    for r, n in enumerate(small):
        shp = w[n].shape
        g_out[n], d_out[n], m_out[n], v_out[n] = (a[r].reshape(shp) for a in (g8, d8, m8, v8))
    loss = g8[7, 0]

    return (loss, dx0[None], *[g_out[n] for n in names], *[d_out[n] for n in names],
            *[m_out[n] for n in names], *[v_out[n] for n in names])
```

```python
import numpy as np
import jax
import jax.numpy as jnp
from jax import lax
from jax.experimental import pallas as pl
from jax.experimental.pallas import tpu as pltpu

F32 = jnp.float32
BF16 = jnp.bfloat16

T = 4096
D = 1024
FF = 2816
NDEV = 8
CONV_W = 31
HEAD = 128
BLK = 128
GROUPS = ((128, 1), (512, 4), (2048, 16))
NHG = 4
AW = NHG * HEAD
IN_W = 2 * D + 3 * 3 * AW + 2 * D
EPS = 1e-6
B1, B2, LR, AEPS, WD, STEP = 0.9, 0.999, 0.001, 1e-08, 0.01, 10
NEG = -1e30
VMEM_LIMIT = 56 * 1024 * 1024
MESH_ID = pl.DeviceIdType.MESH

NT = (((1,), (1,)), ((), ()))
NN = (((1,), (0,)), ((), ()))
TN = (((0,), (0,)), ((), ()))
_DIMS = {"nn": NN, "nt": NT, "tn": TN}


def _cp(sem=None):
    return pltpu.CompilerParams(dimension_semantics=sem, vmem_limit_bytes=VMEM_LIMIT)


def _sig(v):
    return 1.0 / (1.0 + jnp.exp(-v))


def _dot(a, b, dims):
    return lax.dot_general(a, b, dims, preferred_element_type=F32)


def _const_spec(shape):
    nd = len(shape)
    return pl.BlockSpec(shape, lambda *_: (0,) * nd)


def _mm(a, b, *, mode, m, n, k, tm, tn, tk, out_dtype, name, a_map=None, b_map=None,
        o_map=None, out_rows=None, init=None, passthru=None):
    gi, gj, gk = m // tm, n // tn, k // tk
    assert gi * tm == m and gj * tn == n and gk * tk == k, (name, m, n, k, tm, tn, tk)
    if mode == "nn":
        a_blk, b_blk = (tm, tk), (tk, tn)
        da, db = (lambda i, j, kk: (i, kk)), (lambda i, j, kk: (kk, j))
    elif mode == "nt":
        a_blk, b_blk = (tm, tk), (tn, tk)
        da, db = (lambda i, j, kk: (i, kk)), (lambda i, j, kk: (j, kk))
    else:
        a_blk, b_blk = (tk, tm), (tk, tn)
        da, db = (lambda i, j, kk: (kk, i)), (lambda i, j, kk: (kk, j))
    a_map = a_map or da
    b_map = b_map or db
    o_map = o_map or (lambda i, j, kk: (i, j))
    dims = _DIMS[mode]
    extra = init if init is not None else passthru
    out_rows = out_rows or m

    def body(*refs):
        if init is not None:
            a_ref, b_ref, i_ref, o_ref = refs[:4]
        elif passthru is not None:
            a_ref, b_ref, _, o_ref = refs[:4]
        else:
            a_ref, b_ref, o_ref = refs[:3]
        if gk == 1:
            prod = _dot(a_ref[...], b_ref[...], dims)
            if init is not None:
                prod = prod + i_ref[...].astype(F32)
            o_ref[...] = prod.astype(out_dtype)
            return
        acc = refs[-1]
        kk = pl.program_id(2)

        @pl.when(kk == 0)
        def _():
            if init is not None:
                acc[...] = i_ref[...].astype(F32)
            else:
                acc[...] = jnp.zeros_like(acc)

        acc[...] += _dot(a_ref[...], b_ref[...], dims)

        @pl.when(kk == gk - 1)
        def _():
            o_ref[...] = acc[...].astype(out_dtype)

    in_specs = [pl.BlockSpec(a_blk, a_map), pl.BlockSpec(b_blk, b_map)]
    args = [a, b]
    aliases = {}
    if init is not None:
        in_specs.append(pl.BlockSpec((tm, tn), o_map))
        args.append(init)
        aliases = {2: 0}
    elif passthru is not None:
        in_specs.append(pl.BlockSpec(memory_space=pl.ANY))
        args.append(passthru)
        aliases = {2: 0}
    out_dt = extra.dtype if extra is not None else out_dtype
    assert out_dt == out_dtype
    return pl.pallas_call(
        body, name=name, grid=(gi, gj, gk),
        in_specs=in_specs, out_specs=pl.BlockSpec((tm, tn), o_map),
        out_shape=jax.ShapeDtypeStruct((out_rows, n), out_dtype),
        scratch_shapes=[pltpu.VMEM((tm, tn), F32)] if gk > 1 else [],
        input_output_aliases=aliases,
        compiler_params=_cp(("parallel", "parallel", "arbitrary")),
    )(*args)


def _ffn_fwd(x, g, wg_t, wu_t, wd, name, next_gain=None, loss_of=None):
    tm, fc = PERM_TM, 256
    nc = FF // fc
    n_in = 5 + (1 if next_gain is not None else 0) + (2 if loss_of is not None else 0)

    def body(*refs):
        x_ref, g_ref, wg_ref, wu_ref, wd_ref = refs[:5]
        extra_in, outs = refs[5:n_in], refs[n_in:]
        act_ref = outs[-1]
        xv = x_ref[...]
        r = lax.rsqrt(jnp.mean(xv * xv, axis=-1, keepdims=True) + EPS)
        h = (xv * r * g_ref[...]).astype(BF16)
        gg_ref, uu_ref = (outs[0], outs[1]) if loss_of is not None else (outs[1], outs[2])
        for c in range(nc):
            sl = pl.ds(c * fc, fc)
            gg = _dot(h, wg_ref[sl, :], NT)
            uu = _dot(h, wu_ref[sl, :], NT)
            gg_ref[:, sl] = gg.astype(BF16)
            uu_ref[:, sl] = uu.astype(BF16)
            act_ref[:, sl] = (gg * _sig(gg) * uu).astype(BF16)
        y = xv + 0.5 * _dot(act_ref[...], wd_ref[...], NN)
        if loss_of is not None:
            _final_math(y, extra_in[0][...], extra_in[1][...], outs[2], outs[3], outs[4], pl.program_id(0))
            return
        outs[0][...] = y
        if next_gain is not None:
            tile = outs[-2]
            r2 = lax.rsqrt(jnp.mean(y * y, axis=-1, keepdims=True) + EPS)
            hv = y * r2 * extra_in[0][...]
            outs[3][...] = hv.astype(BF16)
            _put_tile(tile, hv)
            for dil, p_ref in zip(DILS, outs[4:4 + len(DILS)]):
                _store_perm(p_ref, tile, dil)

    wspec = pl.BlockSpec((FF, D), lambda i: (0, 0), pipeline_mode=pl.Buffered(1))
    row_d = pl.BlockSpec((tm, D), lambda i: (i, 0))
    row_f = pl.BlockSpec((tm, FF), lambda i: (i, 0))
    in_specs = [row_d, _const_spec((1, D)), wspec, wspec, wspec]
    args = [x, g, wg_t, wu_t, wd]
    f_shape = jax.ShapeDtypeStruct((T, FF), BF16)
    scratch = [pltpu.VMEM((tm, FF), BF16)]
    if loss_of is not None:
        in_specs += [_const_spec((1, D)), row_d]
        args += list(loss_of)
        out_specs = [row_f, row_f, row_d, _const_spec((1, D)), _const_spec((1, 128))]
        out_shape = [f_shape, f_shape, jax.ShapeDtypeStruct((T, D), F32), jax.ShapeDtypeStruct((1, D), F32),
                     jax.ShapeDtypeStruct((1, 128), F32)]
    else:
        out_specs = [row_d, row_f, row_f]
        out_shape = [jax.ShapeDtypeStruct((T, D), F32), f_shape, f_shape]
        if next_gain is not None:
            in_specs.append(_const_spec((1, D)))
            args.append(next_gain)
            out_specs += [row_d] + [_perm_spec(d, D) for d in DILS]
            out_shape += [jax.ShapeDtypeStruct((T, D), BF16)] + [_perm_shape(d, D, BF16) for d in DILS]
            scratch = [_tile_scratch(D)] + scratch
    out = pl.pallas_call(
        body, name=name, grid=(T // tm,), in_specs=in_specs, out_specs=out_specs, out_shape=out_shape,
        scratch_shapes=scratch,
        compiler_params=_cp(("arbitrary",) if loss_of is not None else ("parallel",)),
    )(*args)
    if next_gain is not None:
        return out[0], out[1], out[2], [out[3]] + [o.reshape(T, D) for o in out[4:]]
    return tuple(out)


def _ffn_bwd(x, g, gg_all, uu_all, dout, wg_t, wu_t, wd, name):
    tm, fc = 256, 256
    nc = FF // fc

    def body(x_ref, g_ref, gg_ref, uu_ref, do_ref, wg_ref, wu_ref, wd_ref,
             dx_ref, dgam_ref, dg_ref, du_ref, act_ref, h_ref, db_ref):
        i = pl.program_id(0)
        xv = x_ref[...]
        r = lax.rsqrt(jnp.mean(xv * xv, axis=-1, keepdims=True) + EPS)
        xhat = xv * r
        gam = g_ref[...]
        h_ref[...] = (xhat * gam).astype(BF16)
        dov = do_ref[...]
        dbv = (0.5 * dov).astype(BF16)
        db_ref[...] = dbv
        for c in range(nc):
            sl = pl.ds(c * fc, fc)
            da = _dot(dbv, wd_ref[sl, :], NT)
            gg = gg_ref[:, sl].astype(F32)
            uu = uu_ref[:, sl].astype(F32)
            s = _sig(gg)
            si = gg * s
            dgv = (da * uu * (s * (1.0 + gg * (1.0 - s)))).astype(BF16)
            duv = (da * si).astype(BF16)
            dg_ref[:, sl] = dgv
            du_ref[:, sl] = duv
            act_ref[:, sl] = (si * uu).astype(BF16)
        dh = _dot(dg_ref[...], wg_ref[...], NN) + _dot(du_ref[...], wu_ref[...], NN)

        @pl.when(i == 0)
        def _():
            dgam_ref[...] = jnp.zeros_like(dgam_ref)

        dgam_ref[...] += jnp.sum(dh * xhat, axis=0, keepdims=True)
        dxh = dh * gam
        dx_ref[...] = dov + r * (dxh - xhat * jnp.mean(dxh * xhat, axis=-1, keepdims=True))

    wspec = pl.BlockSpec((FF, D), lambda i: (0, 0), pipeline_mode=pl.Buffered(1))
    row_d = pl.BlockSpec((tm, D), lambda i: (i, 0))
    row_f = pl.BlockSpec((tm, FF), lambda i: (i, 0))
    return pl.pallas_call(
        body, name=name, grid=(T // tm,),
        in_specs=[row_d, _const_spec((1, D)), row_f, row_f, row_d, wspec, wspec, wspec],
        out_specs=[row_d, _const_spec((1, D)), row_f, row_f, row_f, row_d, row_d],
        out_shape=[jax.ShapeDtypeStruct((T, D), F32), jax.ShapeDtypeStruct((1, D), F32),
                   jax.ShapeDtypeStruct((T, FF), BF16), jax.ShapeDtypeStruct((T, FF), BF16),
                   jax.ShapeDtypeStruct((T, FF), BF16), jax.ShapeDtypeStruct((T, D), BF16),
                   jax.ShapeDtypeStruct((T, D), BF16)],
        compiler_params=_cp(("arbitrary",)),
    )(x, g, gg_all, uu_all, dout, wg_t, wu_t, wd)


def _ffn_bwd_pre(x, g, gg_all, uu_all, dout, wd, name):
    tm, fc = 512, 256
    nc = FF // fc

    def body(x_ref, g_ref, gg_ref, uu_ref, do_ref, wd_ref, dg_ref, du_ref, act_ref, h_ref, db_ref):
        xv = x_ref[...]
        r = lax.rsqrt(jnp.mean(xv * xv, axis=-1, keepdims=True) + EPS)
        h_ref[...] = (xv * r * g_ref[...]).astype(BF16)
        dbv = (0.5 * do_ref[...]).astype(BF16)
        db_ref[...] = dbv
        for c in range(nc):
            sl = pl.ds(c * fc, fc)
            da = _dot(dbv, wd_ref[sl, :], NT)
            gg = gg_ref[:, sl].astype(F32)
            uu = uu_ref[:, sl].astype(F32)
            s = _sig(gg)
            si = gg * s
            dg_ref[:, sl] = (da * uu * (s * (1.0 + gg * (1.0 - s)))).astype(BF16)
            du_ref[:, sl] = (da * si).astype(BF16)
            act_ref[:, sl] = (si * uu).astype(BF16)

    wspec = pl.BlockSpec((FF, D), lambda i: (0, 0), pipeline_mode=pl.Buffered(1))
    row_d = pl.BlockSpec((tm, D), lambda i: (i, 0))
    row_f = pl.BlockSpec((tm, FF), lambda i: (i, 0))
    return pl.pallas_call(
        body, name=name, grid=(T // tm,),
        in_specs=[row_d, _const_spec((1, D)), row_f, row_f, row_d, wspec],
        out_specs=[row_f, row_f, row_f, row_d, row_d],
        out_shape=[jax.ShapeDtypeStruct((T, FF), BF16), jax.ShapeDtypeStruct((T, FF), BF16),
                   jax.ShapeDtypeStruct((T, FF), BF16), jax.ShapeDtypeStruct((T, D), BF16),
                   jax.ShapeDtypeStruct((T, D), BF16)],
        compiler_params=_cp(("parallel",)),
    )(x, g, gg_all, uu_all, dout, wd)


def _ffn_bwd_dx(x, g, dgb, dub, dout, wg_t, wu_t, name):
    tm = 512

    def body(x_ref, g_ref, dg_ref, du_ref, do_ref, wg_ref, wu_ref, dx_ref, dgam_ref):
        i = pl.program_id(0)
        xv = x_ref[...]
        r = lax.rsqrt(jnp.mean(xv * xv, axis=-1, keepdims=True) + EPS)
        xhat = xv * r
        gam = g_ref[...]
        dh = _dot(dg_ref[...], wg_ref[...], NN) + _dot(du_ref[...], wu_ref[...], NN)

        @pl.when(i == 0)
        def _():
            dgam_ref[...] = jnp.zeros_like(dgam_ref)

        dgam_ref[...] += jnp.sum(dh * xhat, axis=0, keepdims=True)
        dxh = dh * gam
        dx_ref[...] = do_ref[...] + r * (dxh - xhat * jnp.mean(dxh * xhat, axis=-1, keepdims=True))

    wspec = pl.BlockSpec((FF, D), lambda i: (0, 0), pipeline_mode=pl.Buffered(1))
    row_d = pl.BlockSpec((tm, D), lambda i: (i, 0))
    row_f = pl.BlockSpec((tm, FF), lambda i: (i, 0))
    return pl.pallas_call(
        body, name=name, grid=(T // tm,),
        in_specs=[row_d, _const_spec((1, D)), row_f, row_f, row_d, wspec, wspec],
        out_specs=[row_d, _const_spec((1, D))],
        out_shape=[jax.ShapeDtypeStruct((T, D), F32), jax.ShapeDtypeStruct((1, D), F32)],
        compiler_params=_cp(("arbitrary",)),
    )(x, g, dgb, dub, dout, wg_t, wu_t)


def _wgrad(a, b, m, n, name):
    tm = m // 2 if m == FF else m
    return _mm(a, b, mode="tn", m=m, n=n, k=T, tm=tm, tn=n, tk=min(T, 2048), out_dtype=BF16, name=name)


PERM_TM = 512
DILS = tuple(d for _, d in GROUPS if d > 1)


def _perm_spec(dil, cols):
    return pl.BlockSpec((dil, PERM_TM // dil, cols), lambda i: (0, i, 0))


def _perm_shape(dil, cols, dtype):
    return jax.ShapeDtypeStruct((dil, T // dil, cols), dtype)


LANES = 128


def _tile_scratch(cols):
    return pltpu.VMEM((cols // LANES, PERM_TM, LANES), F32)


def _put_tile(tile, value):
    for c in range(tile.shape[0]):
        tile[c] = value[:, c * LANES:(c + 1) * LANES]


def _get_tile(tile):
    return jnp.concatenate([tile[c] for c in range(tile.shape[0])], axis=1)


def _store_perm(out_ref, tile, dil):
    for r in range(dil):
        for c in range(tile.shape[0]):
            out_ref[r, :, pl.ds(c * LANES, LANES)] = tile[c, pl.ds(r, PERM_TM // dil, stride=dil), :].astype(
                out_ref.dtype)


def _load_unperm(in_ref, tile, dil):
    for r in range(dil):
        for c in range(tile.shape[0]):
            tile[c, pl.ds(r, PERM_TM // dil, stride=dil), :] = in_ref[r, :, pl.ds(c * LANES, LANES)].astype(F32)


def _norm_cast(x, g, name):
    tm = PERM_TM

    def body(x_ref, g_ref, h_ref, *rest):
        p_refs, tile = rest[:-1], rest[-1]
        xv = x_ref[...]
        r = lax.rsqrt(jnp.mean(xv * xv, axis=-1, keepdims=True) + EPS)
        hv = xv * r * g_ref[...]
        h_ref[...] = hv.astype(BF16)
        _put_tile(tile, hv)
        for dil, p_ref in zip(DILS, p_refs):
            _store_perm(p_ref, tile, dil)

    out = pl.pallas_call(
        body, name=name, grid=(T // tm,),
        in_specs=[pl.BlockSpec((tm, D), lambda i: (i, 0)), _const_spec((1, D))],
        out_specs=[pl.BlockSpec((tm, D), lambda i: (i, 0))] + [_perm_spec(d, D) for d in DILS],
        out_shape=[jax.ShapeDtypeStruct((T, D), BF16)] + [_perm_shape(d, D, BF16) for d in DILS],
        scratch_shapes=[_tile_scratch(D)],
        compiler_params=_cp(("parallel",)),
    )(x, g)
    return [out[0]] + [o.reshape(T, D) for o in out[1:]]


def _final_math(xv, gam, tgt, dx_ref, dgam_ref, loss_ref, i):
    r = lax.rsqrt(jnp.mean(xv * xv, axis=-1, keepdims=True) + EPS)
    xhat = xv * r
    err = xhat * gam - tgt
    part = 0.5 * jnp.sum(jnp.mean(err * err, axis=-1, keepdims=True), axis=0, keepdims=True)
    dy = err * (1.0 / D)

    @pl.when(i == 0)
    def _():
        dgam_ref[...] = jnp.zeros_like(dgam_ref)
        loss_ref[...] = jnp.zeros_like(loss_ref)

    dgam_ref[...] += jnp.sum(dy * xhat, axis=0, keepdims=True)
    loss_ref[...] += jnp.broadcast_to(part, loss_ref.shape)
    dxh = dy * gam
    dx_ref[...] = r * (dxh - xhat * jnp.mean(dxh * xhat, axis=-1, keepdims=True))


def _rms_bwd(x, g, dhs, dres, name):
    tm = PERM_TM
    dils = [d for _, d in GROUPS]
    nh = len(dhs)
    assert nh == len(dils)

    def body(*refs):
        x_ref, g_ref = refs[:2]
        dh_refs = refs[2:2 + nh]
        dr_ref, dx_ref, dgam_ref, tile = refs[2 + nh:]
        i = pl.program_id(0)
        xv = x_ref[...]
        r = lax.rsqrt(jnp.mean(xv * xv, axis=-1, keepdims=True) + EPS)
        xhat = xv * r
        gam = g_ref[...]
        dh = None
        for dil, ref in zip(dils, dh_refs):
            if dil == 1:
                part = ref[...]
            else:
                _load_unperm(ref, tile, dil)
                part = _get_tile(tile)
            dh = part if dh is None else dh + part

        @pl.when(i == 0)
        def _():
            dgam_ref[...] = jnp.zeros_like(dgam_ref)

        dgam_ref[...] += jnp.sum(dh * xhat, axis=0, keepdims=True)
        dxh = dh * gam
        dx_ref[...] = dr_ref[...] + r * (dxh - xhat * jnp.mean(dxh * xhat, axis=-1, keepdims=True))

    row_d = pl.BlockSpec((tm, D), lambda i: (i, 0))
    dh_specs = [row_d if d == 1 else _perm_spec(d, D) for d in dils]
    dh_args = [a if d == 1 else a.reshape(d, T // d, D) for d, a in zip(dils, dhs)]
    return pl.pallas_call(
        body, name=name, grid=(T // tm,),
        in_specs=[row_d, _const_spec((1, D))] + dh_specs + [row_d],
        out_specs=[row_d, _const_spec((1, D))],
        out_shape=[jax.ShapeDtypeStruct((T, D), F32), jax.ShapeDtypeStruct((1, D), F32)],
        scratch_shapes=[_tile_scratch(D)],
        compiler_params=_cp(("arbitrary",)),
    )(x, g, *dh_args, dres)


CONV_TM = 256
CONV_HALO = 32
CONV_RB = 16


def _glu(ab):
    ab = ab.astype(F32)
    return ab[:, :D] * _sig(ab[:, D:])


def _ln_stats(z1):
    mu = jnp.mean(z1, axis=-1, keepdims=True)
    zc = z1 - mu
    rstd = lax.rsqrt(jnp.mean(zc * zc, axis=-1, keepdims=True) + EPS)
    return zc * rstd, rstd


def _fill_shifts(zs):
    n = zs.shape[1] - 8
    for s in range(1, 8):
        zs[s, pl.ds(0, n), :] = zs[0, pl.ds(s, n), :]


def _shifted(zs, start, rows):
    q, s = divmod(start, 8)
    return zs[s, pl.ds(8 * q, rows), :]


def _conv_fwd(ab, kern, dwb, lng, lnb, name):
    tm, hl, rb = CONV_TM, CONV_HALO, CONV_RB
    off = hl - (CONV_W - 1)

    def body(ab_ref, abh_ref, k_ref, dwb_ref, lng_ref, lnb_ref, z1_ref, z3_ref, zs):
        i = pl.program_id(0)
        zs[0, pl.ds(0, hl), :] = jnp.where(i > 0, _glu(abh_ref[...]), 0.0)
        zs[0, pl.ds(hl, tm), :] = _glu(ab_ref[...])
        _fill_shifts(zs)
        for b in range(tm // rb):
            acc = jnp.zeros((rb, D), F32)
            for j in range(CONV_W):
                acc = acc + _shifted(zs, b * rb + off + j, rb) * k_ref[pl.ds(j, 1), :]
            z1 = acc + dwb_ref[...]
            z1_ref[pl.ds(b * rb, rb), :] = z1
            zn, _ = _ln_stats(z1)
            z2 = zn * lng_ref[...] + lnb_ref[...]
            z3_ref[pl.ds(b * rb, rb), :] = (z2 * _sig(z2)).astype(BF16)

    row = pl.BlockSpec((tm, D), lambda i: (i, 0))
    return pl.pallas_call(
        body, name=name, grid=(T // tm,),
        in_specs=[pl.BlockSpec((tm, 2 * D), lambda i: (i, 0)),
                  pl.BlockSpec((hl, 2 * D), lambda i: (jnp.maximum(i * (tm // hl) - 1, 0), 0)),
                  _const_spec((32, D)), _const_spec((1, D)), _const_spec((1, D)), _const_spec((1, D))],
        out_specs=[row, row],
        out_shape=[jax.ShapeDtypeStruct((T, D), F32), jax.ShapeDtypeStruct((T, D), BF16)],
        scratch_shapes=[pltpu.VMEM((8, hl + tm, D), F32)],
        compiler_params=_cp(("parallel",)),
    )(ab, ab, kern, dwb, lng, lnb)


def _conv_bwd(dz3, z1, ab, kern, lng, lnb, name):
    tm, hl, rb = CONV_TM, CONV_HALO, CONV_RB
    off = hl - (CONV_W - 1)
    nsteps = T // tm

    def ln_bwd(dz3v, z1v, lngv, lnbv):
        zn, rstd = _ln_stats(z1v)
        z2 = zn * lngv + lnbv
        s = _sig(z2)
        dz2 = dz3v * (s * (1.0 + z2 * (1.0 - s)))
        dzn = dz2 * lngv
        dz1 = rstd * (dzn - jnp.mean(dzn, axis=-1, keepdims=True)
                      - zn * jnp.mean(dzn * zn, axis=-1, keepdims=True))
        return dz1, dz2, zn

    def body(dz3_ref, dz3h_ref, z1_ref, z1h_ref, ab_ref, abh_ref, k_ref, lng_ref, lnb_ref,
             dab_ref, dk_ref, dvec_ref, zs, dzs):
        i = pl.program_id(0)
        lngv, lnbv = lng_ref[...], lnb_ref[...]

        @pl.when(i == 0)
        def _():
            dk_ref[...] = jnp.zeros_like(dk_ref)
            dvec_ref[...] = jnp.zeros_like(dvec_ref)

        dz1, dz2, zn = ln_bwd(dz3_ref[...].astype(F32), z1_ref[...], lngv, lnbv)
        dvec_ref[pl.ds(0, 1), :] += jnp.sum(dz1, axis=0, keepdims=True)
        dvec_ref[pl.ds(1, 1), :] += jnp.sum(dz2 * zn, axis=0, keepdims=True)
        dvec_ref[pl.ds(2, 1), :] += jnp.sum(dz2, axis=0, keepdims=True)
        dzs[0, pl.ds(0, tm), :] = dz1
        dz1h, _, _ = ln_bwd(dz3h_ref[...].astype(F32), z1h_ref[...], lngv, lnbv)
        dzs[0, pl.ds(tm, hl), :] = jnp.where(i < nsteps - 1, dz1h, 0.0)
        _fill_shifts(dzs)
        zs[0, pl.ds(0, hl), :] = jnp.where(i > 0, _glu(abh_ref[...]), 0.0)
        zs[0, pl.ds(hl, tm), :] = _glu(ab_ref[...])
        _fill_shifts(zs)

        for j in range(CONV_W):
            tot = jnp.zeros((rb, D), F32)
            for b in range(tm // rb):
                tot = tot + dzs[0, pl.ds(b * rb, rb), :] * _shifted(zs, b * rb + off + j, rb)
            dk_ref[pl.ds(j, 1), :] += jnp.sum(tot, axis=0, keepdims=True)

        for b in range(tm // rb):
            acc = jnp.zeros((rb, D), F32)
            for j in range(CONV_W):
                acc = acc + _shifted(dzs, b * rb + (CONV_W - 1) - j, rb) * k_ref[pl.ds(j, 1), :]
            av = ab_ref[pl.ds(b * rb, rb), pl.ds(0, D)].astype(F32)
            sb = _sig(ab_ref[pl.ds(b * rb, rb), pl.ds(D, D)].astype(F32))
            dab_ref[pl.ds(b * rb, rb), pl.ds(0, D)] = (acc * sb).astype(BF16)
            dab_ref[pl.ds(b * rb, rb), pl.ds(D, D)] = (acc * av * sb * (1.0 - sb)).astype(BF16)

    row = pl.BlockSpec((tm, D), lambda i: (i, 0))
    nxt = pl.BlockSpec((hl, D), lambda i: (jnp.minimum((i + 1) * (tm // hl), T // hl - 1), 0))
    return pl.pallas_call(
        body, name=name, grid=(nsteps,),
        in_specs=[row, nxt, row, nxt,
                  pl.BlockSpec((tm, 2 * D), lambda i: (i, 0)),
                  pl.BlockSpec((hl, 2 * D), lambda i: (jnp.maximum(i * (tm // hl) - 1, 0), 0)),
                  _const_spec((32, D)), _const_spec((1, D)), _const_spec((1, D))],
        out_specs=[pl.BlockSpec((tm, 2 * D), lambda i: (i, 0)), _const_spec((32, D)), _const_spec((8, D))],
        out_shape=[jax.ShapeDtypeStruct((T, 2 * D), BF16), jax.ShapeDtypeStruct((32, D), F32),
                   jax.ShapeDtypeStruct((8, D), F32)],
        scratch_shapes=[pltpu.VMEM((8, hl + tm, D), F32), pltpu.VMEM((8, tm + hl, D), F32)],
        compiler_params=_cp(("arbitrary",)),
    )(dz3, dz3, z1, z1, ab, ab, kern, lng, lnb)


def _alibi_slopes():
    h = np.arange(1, 3 * NHG + 1, dtype=np.float32)
    return np.power(np.float32(2.0), -8.0 * h / np.float32(3 * NHG)).astype(np.float32)


def _band_bias(gi):
    _, dil = GROUPS[gi]
    slopes = _alibi_slopes()[gi * NHG:(gi + 1) * NHG]
    qi = np.arange(BLK)[:, None]
    ki = np.arange(2 * BLK)[None, :]
    steps = BLK + qi - ki
    band = (steps >= 0) & (steps <= BLK)
    bias = -slopes[:, None, None] * (dil * steps).astype(np.float32)[None]
    return jnp.asarray(np.where(band[None], bias, np.float32(NEG)).astype(np.float32))


QB_FWD = 4
QB_BWD = 16


def _attn_specs(qb):
    prev = lambda n: jnp.maximum(n * qb - 1, 0)
    return [pl.BlockSpec((qb * BLK, HEAD), lambda h, n: (n, h)),
            pl.BlockSpec((BLK, HEAD), lambda h, n: (prev(n), NHG + h)),
            pl.BlockSpec((qb * BLK, HEAD), lambda h, n: (n, NHG + h)),
            pl.BlockSpec((BLK, HEAD), lambda h, n: (prev(n), 2 * NHG + h)),
            pl.BlockSpec((qb * BLK, HEAD), lambda h, n: (n, 2 * NHG + h)),
            pl.BlockSpec((None, BLK, 2 * BLK), lambda h, n: (h, 0, 0))]


def _scores(q, kcat, bias, blk, seg):
    s = _dot(q, kcat, NT) * (HEAD ** -0.5) + bias
    col = lax.broadcasted_iota(jnp.int32, s.shape, 1)
    first = (blk % seg) == 0
    return jnp.where(jnp.logical_and(first, col < BLK), NEG, s)


def _attn_fwd(qkv, gi, name):
    seg = (T // GROUPS[gi][1]) // BLK

    qb = QB_FWD

    def body(q_ref, kp_ref, kc_ref, vp_ref, vc_ref, bias_ref, o_ref, l_ref):
        n = pl.program_id(0)
        for h in range(NHG):
            cols = pl.ds(h * HEAD, HEAD)
            kwin = jnp.concatenate([kp_ref[:, cols], kc_ref[:, cols]], axis=0)
            vwin = jnp.concatenate([vp_ref[:, cols], vc_ref[:, cols]], axis=0)
            bias = bias_ref[h]
            for b in range(qb):
                rows = pl.ds(b * BLK, BLK)
                s = _scores(q_ref[rows, cols], kwin[b * BLK:(b + 2) * BLK], bias, n * qb + b, seg)
                mx = jnp.max(s, axis=-1, keepdims=True)
                p = jnp.exp(s - mx)
                den = jnp.sum(p, axis=-1, keepdims=True)
                o_ref[rows, cols] = (_dot(p.astype(BF16), vwin[b * BLK:(b + 2) * BLK], NN) / den).astype(BF16)
                l_ref[rows, cols] = jnp.broadcast_to(mx + jnp.log(den), (BLK, HEAD))

    prev = lambda n: jnp.maximum(n * qb - 1, 0)
    cur = lambda part: pl.BlockSpec((qb * BLK, AW), lambda n: (n, part))
    halo = lambda part: pl.BlockSpec((BLK, AW), lambda n: (prev(n), part))
    return pl.pallas_call(
        body, name=name, grid=(T // (qb * BLK),),
        in_specs=[cur(0), halo(1), cur(1), halo(2), cur(2), _const_spec((NHG, BLK, 2 * BLK))],
        out_specs=[cur(0), cur(0)],
        out_shape=[jax.ShapeDtypeStruct((T, AW), BF16), jax.ShapeDtypeStruct((T, AW), F32)],
        compiler_params=_cp(("parallel",)),
    )(qkv, qkv, qkv, qkv, qkv, _band_bias(gi))


def _attn_bwd(qkv, dob, lse, delta, gi, name):
    seg = (T // GROUPS[gi][1]) // BLK
    qb = QB_BWD
    nb = T // (qb * BLK)
    scale = HEAD ** -0.5

    def body(q_ref, kp_ref, kc_ref, vp_ref, vc_ref, bias_ref, do_ref, l_ref, dl_ref, out_ref, dk_acc, dv_acc):
        n = pl.program_id(1)
        kwin = jnp.concatenate([kp_ref[...], kc_ref[...]], axis=0)
        vwin = jnp.concatenate([vp_ref[...], vc_ref[...]], axis=0)
        bias = bias_ref[...]
        dks, dvs = [], []
        for b in range(qb):
            rows = pl.ds(b * BLK, BLK)
            q = q_ref[rows, :]
            kcat = kwin[b * BLK:(b + 2) * BLK]
            s = _scores(q, kcat, bias, n * qb + b, seg)
            p = jnp.exp(s - l_ref[rows, pl.ds(0, 1)])
            dov = do_ref[rows, :]
            dvs.append(_dot(p.astype(BF16), dov, TN))
            dp = _dot(dov, vwin[b * BLK:(b + 2) * BLK], NT)
            dsb = (p * (dp - dl_ref[rows, pl.ds(0, 1)]) * scale).astype(BF16)
            row = pl.ds(pl.multiple_of((n * qb + b) * BLK, BLK), BLK)
            out_ref[0, row, :] = _dot(dsb, kcat, NN).astype(BF16)
            dks.append(_dot(dsb, q, TN))
        for b in range(qb):
            row = pl.ds(pl.multiple_of((n * qb + b) * BLK, BLK), BLK)
            if b + 1 < qb:
                dk_acc[row, :] = dks[b][BLK:] + dks[b + 1][:BLK]
                dv_acc[row, :] = dvs[b][BLK:] + dvs[b + 1][:BLK]
            else:
                dk_acc[row, :] = dks[b][BLK:]
                dv_acc[row, :] = dvs[b][BLK:]

        @pl.when(n > 0)
        def _():
            prow = pl.ds(pl.multiple_of((n * qb - 1) * BLK, BLK), BLK)
            dk_acc[prow, :] += dks[0][:BLK]
            dv_acc[prow, :] += dvs[0][:BLK]

        @pl.when(n == nb - 1)
        def _():
            out_ref[1] = dk_acc[...].astype(BF16)
            out_ref[2] = dv_acc[...].astype(BF16)

    oblk = pl.BlockSpec((qb * BLK, HEAD), lambda h, n: (n, h))
    return pl.pallas_call(
        body, name=name, grid=(NHG, nb),
        in_specs=_attn_specs(qb) + [oblk, oblk, oblk],
        out_specs=pl.BlockSpec((3, T, HEAD), lambda h, n: (0, 0, h)),
        out_shape=jax.ShapeDtypeStruct((3, T, AW), BF16),
        scratch_shapes=[pltpu.VMEM((T, HEAD), F32), pltpu.VMEM((T, HEAD), F32)],
        compiler_params=_cp(("parallel", "arbitrary")),
    )(qkv, qkv, qkv, qkv, qkv, _band_bias(gi), dob, lse, delta)


def _merge(outs, lses, name):
    tm = PERM_TM
    dils = [d for _, d in GROUPS]
    ng = len(dils)

    def body(*refs):
        in_refs = refs[:2 * ng]
        ab_ref = refs[2 * ng]
        lse_refs = refs[2 * ng + 1:3 * ng + 1]
        tile = refs[-1]

        def token_order(ref, dil):
            if dil == 1:
                return ref[...].astype(F32)
            _load_unperm(ref, tile, dil)
            return _get_tile(tile)

        os = [token_order(in_refs[2 * i], d) for i, d in enumerate(dils)]
        ls = [token_order(in_refs[2 * i + 1], d) for i, d in enumerate(dils)]
        mx = jnp.maximum(jnp.maximum(ls[0], ls[1]), ls[2])
        es = [jnp.exp(v - mx) for v in ls]
        tot = es[0] + es[1] + es[2]
        att = (es[0] / tot) * os[0] + (es[1] / tot) * os[1] + (es[2] / tot) * os[2]
        ab_ref[...] = att.astype(BF16)
        lse = mx + jnp.log(tot)
        _put_tile(tile, lse)
        for dil, ref in zip(dils, lse_refs):
            if dil == 1:
                ref[...] = lse
            else:
                _store_perm(ref, tile, dil)

    row = pl.BlockSpec((tm, AW), lambda i: (i, 0))
    specs = [row if d == 1 else _perm_spec(d, AW) for d in dils]
    args = []
    for d, o, l in zip(dils, outs, lses):
        args += [o, l] if d == 1 else [o.reshape(d, T // d, AW), l.reshape(d, T // d, AW)]
    out = pl.pallas_call(
        body, name=name, grid=(T // tm,),
        in_specs=[sp for sp in specs for _ in range(2)], out_specs=[row] + specs,
        out_shape=[jax.ShapeDtypeStruct((T, AW), BF16)]
        + [jax.ShapeDtypeStruct((T, AW), F32) if d == 1 else _perm_shape(d, AW, F32) for d in dils],
        scratch_shapes=[_tile_scratch(AW)],
        compiler_params=_cp(("parallel",)),
    )(*args)
    return out[0], [o.reshape(T, AW) for o in out[1:]]


def _mix_out(z3b, attnb, gates, wc, wa_t, wo, x1, name):
    tm = 512

    def body(z_ref, a_ref, g_ref, wc_ref, wa_ref, wo_ref, x_ref, xo_ref, yc_ref, ya_ref, mx_ref):
        yc = _dot(z_ref[...], wc_ref[...], NN)
        ya = _dot(a_ref[...], wa_ref[...], NT)
        yc_ref[...] = yc.astype(BF16)
        ya_ref[...] = ya.astype(BF16)
        gv = g_ref[...].astype(F32)
        mixed = (_sig(gv[:, :D]) * yc + _sig(gv[:, D:]) * ya).astype(BF16)
        mx_ref[...] = mixed
        xo_ref[...] = x_ref[...] + _dot(mixed, wo_ref[...], NN)

    row = pl.BlockSpec((tm, D), lambda i: (i, 0))
    return pl.pallas_call(
        body, name=name, grid=(T // tm,),
        in_specs=[row, pl.BlockSpec((tm, AW), lambda i: (i, 0)), pl.BlockSpec((tm, 2 * D), lambda i: (i, 0)),
                  _const_spec((D, D)), _const_spec((D, AW)), _const_spec((D, D)), row],
        out_specs=[row, row, row, row],
        out_shape=[jax.ShapeDtypeStruct((T, D), F32), jax.ShapeDtypeStruct((T, D), BF16),
                   jax.ShapeDtypeStruct((T, D), BF16), jax.ShapeDtypeStruct((T, D), BF16)],
        compiler_params=_cp(("parallel",)),
    )(z3b, attnb, gates, wc, wa_t, wo, x1)


def _mix_out_bwd(dx2, gates, yc, ya, attn, wc, wa_t, wo, name):
    tm = PERM_TM
    dils = [d for _, d in GROUPS]
    ng = len(dils)

    def body(dx_ref, g_ref, yc_ref, ya_ref, at_ref, wc_ref, wa_ref, wo_ref,
             dg_ref, dyc_ref, dya_ref, dxb_ref, dz3_ref, *rest):
        dat_refs, dl_refs, tile = rest[:ng], rest[ng:2 * ng], rest[-1]
        dxb = dx_ref[...].astype(BF16)
        dxb_ref[...] = dxb
        dmix = _dot(dxb, wo_ref[...], NT)
        gv = g_ref[...].astype(F32)
        sc = _sig(gv[:, :D])
        sa = _sig(gv[:, D:])
        ycv, yav = yc_ref[...].astype(F32), ya_ref[...].astype(F32)
        dg_ref[:, pl.ds(0, D)] = (dmix * ycv * sc * (1.0 - sc)).astype(BF16)
        dg_ref[:, pl.ds(D, D)] = (dmix * yav * sa * (1.0 - sa)).astype(BF16)
        dyc = (dmix * sc).astype(BF16)
        dya = (dmix * sa).astype(BF16)
        dyc_ref[...] = dyc
        dya_ref[...] = dya
        dz3_ref[...] = _dot(dyc, wc_ref[...], NT).astype(BF16)
        dat = _dot(dya, wa_ref[...], NN)
        prod = dat * at_ref[...].astype(F32)
        delta = jnp.concatenate(
            [jnp.broadcast_to(jnp.sum(prod[:, h * HEAD:(h + 1) * HEAD], axis=-1, keepdims=True), (tm, HEAD))
             for h in range(NHG)], axis=1)
        for value, out_refs in ((dat, dat_refs), (delta, dl_refs)):
            _put_tile(tile, value)
            for dil, ref in zip(dils, out_refs):
                if dil == 1:
                    ref[...] = value.astype(ref.dtype)
                else:
                    _store_perm(ref, tile, dil)

    row = pl.BlockSpec((tm, D), lambda i: (i, 0))
    row2 = pl.BlockSpec((tm, 2 * D), lambda i: (i, 0))
    rowa = pl.BlockSpec((tm, AW), lambda i: (i, 0))
    aspecs = [rowa if d == 1 else _perm_spec(d, AW) for d in dils]

    def ashapes(dtype):
        return [jax.ShapeDtypeStruct((T, AW), dtype) if d == 1 else _perm_shape(d, AW, dtype) for d in dils]

    out = pl.pallas_call(
        body, name=name, grid=(T // tm,),
        in_specs=[row, row2, row, row, rowa, _const_spec((D, D)), _const_spec((D, AW)), _const_spec((D, D))],
        out_specs=[row2, row, row, row, row] + aspecs + aspecs,
        out_shape=[jax.ShapeDtypeStruct((T, 2 * D), BF16), jax.ShapeDtypeStruct((T, D), BF16),
                   jax.ShapeDtypeStruct((T, D), BF16), jax.ShapeDtypeStruct((T, D), BF16),
                   jax.ShapeDtypeStruct((T, D), BF16)] + ashapes(BF16) + ashapes(F32),
        scratch_shapes=[_tile_scratch(AW)],
        compiler_params=_cp(("parallel",)),
    )(dx2, gates, yc, ya, attn, wc, wa_t, wo)
    dats = [o.reshape(T, AW) for o in out[5:5 + ng]]
    deltas = [o.reshape(T, AW) for o in out[5 + ng:5 + 2 * ng]]
    return out[0], out[1], out[2], out[3], out[4], dats, deltas


def _peer(k):
    x, y, c = lax.axis_index("x"), lax.axis_index("y"), lax.axis_index("c")
    px = 1 - x if k & 4 else x
    py = 1 - y if k & 2 else y
    pc = 1 - c if k & 1 else c
    return (px, py, pc), 4 * px + 2 * py + pc


HBM_SPEC = pl.BlockSpec(memory_space=pltpu.HBM)
SEM_SPEC = pl.BlockSpec(memory_space=pltpu.SEMAPHORE)
EFFECT = pltpu.SideEffectType.DATAFLOW_SIDE_EFFECTING


def _my_place():
    return 4 * lax.axis_index("x") + 2 * lax.axis_index("y") + lax.axis_index("c")


def _tie(a, order_after, name):
    na = len(order_after)

    def body(*refs):
        del refs

    return pl.pallas_call(
        body, name=name, in_specs=[pl.BlockSpec(memory_space=pl.ANY)] * (1 + na),
        out_specs=pl.BlockSpec(memory_space=pl.ANY), out_shape=jax.ShapeDtypeStruct(a.shape, a.dtype),
        input_output_aliases={0: 0},
    )(a, *order_after)


def _prep_gather(ws, order_after, name):
    me = jnp.reshape(_my_place(), (1,)).astype(jnp.int32)
    n = len(ws)
    na = len(order_after)
    shapes = [((32, wv.shape[1]), F32) if wv.shape[0] == CONV_W else (wv.shape, BF16) for wv in ws]

    def body(me_ref, *refs):
        del me_ref
        ins, outs = refs[:n], refs[n + na:]
        for wv, i_ref, o_ref in zip(ws, ins, outs):
            if wv.shape[0] == CONV_W:
                o_ref[pl.ds(0, CONV_W), :] = i_ref[...]
                o_ref[pl.ds(CONV_W, 1), :] = jnp.zeros((1, wv.shape[1]), F32)
            else:
                o_ref[...] = i_ref[...].astype(BF16)

    grid_spec = pltpu.PrefetchScalarGridSpec(
        num_scalar_prefetch=1, grid=(1,),
        in_specs=[pl.BlockSpec(wv.shape, lambda i, m: (0, 0)) for wv in ws]
        + [pl.BlockSpec(memory_space=pl.ANY)] * na,
        out_specs=[pl.BlockSpec(shp, lambda i, m: (m[0], 0)) for shp, _ in shapes])
    return pl.pallas_call(
        body, name=name, grid_spec=grid_spec,
        out_shape=[jax.ShapeDtypeStruct((NDEV * shp[0], shp[1]), dt) for shp, dt in shapes],
        compiler_params=_cp(("arbitrary",)),
    )(me, *ws, *order_after)


GATHER_A = ((1, 0), (2, 0), (4, 0), (6, 0))
GATHER_B = ((1, 2), (1, 4), (1, 6))


def _gather_start(lands, plan, order_after, name):
    n = len(lands)
    na = len(order_after)
    npl = len(plan)

    def body(*refs):
        land_refs = refs[:n]
        send, recv = refs[n + na], refs[n + na + 1]
        token = refs[-1]
        for w in range(n):
            rows = lands[w].shape[0] // NDEV
            for p, (k, j) in enumerate(plan):
                peer, _ = _peer(k)
                _, blk = _peer(j)
                part = land_refs[w].at[pl.ds(blk * rows, rows)]
                i = w * npl + p
                pltpu.make_async_remote_copy(src_ref=part, dst_ref=part, send_sem=send.at[i], recv_sem=recv.at[i],
                                             device_id=peer, device_id_type=MESH_ID).start()
        token[...] = jnp.zeros_like(token)

    nsem = n * npl
    bufs = [pltpu.with_memory_space_constraint(a, pltpu.HBM) for a in lands]
    out = pl.pallas_call(
        body, name=name,
        in_specs=[HBM_SPEC] * n + [pl.BlockSpec(memory_space=pl.ANY)] * na,
        out_specs=[SEM_SPEC, SEM_SPEC] + [HBM_SPEC] * n + [pl.BlockSpec(memory_space=pltpu.VMEM)],
        out_shape=[pltpu.SemaphoreType.DMA((nsem,)), pltpu.SemaphoreType.DMA((nsem,))]
        + [pltpu.HBM(a.shape, a.dtype) for a in bufs] + [jax.ShapeDtypeStruct((8, 128), F32)],
        input_output_aliases={i: 2 + i for i in range(n)},
        compiler_params=pltpu.CompilerParams(has_side_effects=EFFECT),
    )(*bufs, *order_after)
    return out[0], out[1], out[2:2 + n], out[-1]


def _gather_wait(started, plan, order_after, name):
    send, recv, lands, _ = started
    n = len(lands)
    na = len(order_after)
    npl = len(plan)

    def body(*refs):
        land_refs = refs[:n]
        send_ref, recv_ref = refs[n], refs[n + 1]
        for w in range(n):
            rows = lands[w].shape[0] // NDEV
            for p, (k, j) in enumerate(plan):
                peer, _ = _peer(k)
                _, blk = _peer(j)
                part = land_refs[w].at[pl.ds(blk * rows, rows)]
                i = w * npl + p
                cp = pltpu.make_async_remote_copy(src_ref=part, dst_ref=part, send_sem=send_ref.at[i],
                                                  recv_sem=recv_ref.at[i], device_id=peer, device_id_type=MESH_ID)
                cp.wait_send()
                cp.wait_recv()

    out = pl.pallas_call(
        body, name=name,
        in_specs=[HBM_SPEC] * n + [SEM_SPEC, SEM_SPEC] + [pl.BlockSpec(memory_space=pl.ANY)] * na,
        out_specs=[HBM_SPEC] * n,
        out_shape=[pltpu.HBM(a.shape, a.dtype) for a in lands],
        input_output_aliases={i: i for i in range(n)},
        compiler_params=pltpu.CompilerParams(has_side_effects=EFFECT),
    )(*lands, send, recv, *order_after)
    return list(out)


def _copy_ends(kind, src, land, me, plin, k):
    if kind == "scatter":
        rows = src.shape[0] // NDEV
        return src.at[pl.ds(plin * rows, rows)], land.at[k - 1]
    return src, land.at[me]


def _landing(kind, src):
    me = _my_place()
    if kind == "scatter":
        return lax.empty((NDEV - 1, src.shape[0] // NDEV) + src.shape[1:], src.dtype)
    land = lax.empty((NDEV,) + src.shape, src.dtype)
    return lax.dynamic_update_slice(land, src[None], (me,) + (0,) * src.ndim)


def _send_start(kinds, srcs, order_after, name):
    n = len(srcs)
    lands = [_landing(kd, s) for kd, s in zip(kinds, srcs)]
    na = len(order_after)

    def body(*refs):
        src_refs, land_refs = refs[:n], refs[n:2 * n]
        send, recv = refs[2 * n + na], refs[2 * n + na + 1]
        token = refs[-1]
        _, me = _peer(0)
        for w in range(n):
            for k in range(1, NDEV):
                peer, plin = _peer(k)
                s, d = _copy_ends(kinds[w], src_refs[w], land_refs[w], me, plin, k)
                i = w * (NDEV - 1) + k - 1
                pltpu.make_async_remote_copy(src_ref=s, dst_ref=d, send_sem=send.at[i], recv_sem=recv.at[i],
                                             device_id=peer, device_id_type=MESH_ID).start()
        token[...] = jnp.zeros_like(token)

    nsem = n * (NDEV - 1)
    bufs = [pltpu.with_memory_space_constraint(a, pltpu.HBM) for a in list(srcs) + lands]
    out = pl.pallas_call(
        body, name=name,
        in_specs=[HBM_SPEC] * (2 * n) + [pl.BlockSpec(memory_space=pl.ANY)] * na,
        out_specs=[SEM_SPEC, SEM_SPEC] + [HBM_SPEC] * (2 * n) + [pl.BlockSpec(memory_space=pltpu.VMEM)],
        out_shape=[pltpu.SemaphoreType.DMA((nsem,)), pltpu.SemaphoreType.DMA((nsem,))]
        + [pltpu.HBM(a.shape, a.dtype) for a in bufs] + [jax.ShapeDtypeStruct((8, 128), F32)],
        input_output_aliases={i: 2 + i for i in range(2 * n)},
        compiler_params=pltpu.CompilerParams(has_side_effects=EFFECT),
    )(*bufs, *order_after)
    return out[0], out[1], out[2:2 + n], out[2 + n:2 + 2 * n], out[-1]


def _send_wait(kinds, started, order_after, name):
    send, recv, srcs, lands, _ = started
    n = len(srcs)
    na = len(order_after)

    def body(*refs):
        src_refs, land_refs = refs[:n], refs[n:2 * n]
        send_ref, recv_ref = refs[2 * n], refs[2 * n + 1]
        _, me = _peer(0)
        for w in range(n):
            for k in range(1, NDEV):
                peer, plin = _peer(k)
                s, d = _copy_ends(kinds[w], src_refs[w], land_refs[w], me, plin, k)
                i = w * (NDEV - 1) + k - 1
                cp = pltpu.make_async_remote_copy(src_ref=s, dst_ref=d, send_sem=send_ref.at[i],
                                                  recv_sem=recv_ref.at[i], device_id=peer, device_id_type=MESH_ID)
                cp.wait_send()
                cp.wait_recv()

    bufs = list(srcs) + list(lands)
    out = pl.pallas_call(
        body, name=name,
        in_specs=[HBM_SPEC] * (2 * n) + [SEM_SPEC, SEM_SPEC] + [pl.BlockSpec(memory_space=pl.ANY)] * na,
        out_specs=[HBM_SPEC] * (2 * n),
        out_shape=[pltpu.HBM(a.shape, a.dtype) for a in bufs],
        input_output_aliases={i: i for i in range(2 * n)},
        compiler_params=pltpu.CompilerParams(has_side_effects=EFFECT),
    )(*bufs, send, recv, *order_after)
    return out[:n], out[n:]


def _gsum(own, land, name):
    rows, cols = own.shape
    tr = rows // 2 if rows * cols > 512 * 1024 and rows % 32 == 0 else rows

    def body(own_ref, l_ref, o_ref):
        tot = own_ref[...].astype(F32)
        for s in range(NDEV - 1):
            tot = tot + l_ref[s].astype(F32)
        o_ref[...] = tot

    return pl.pallas_call(
        body, name=name, grid=(rows // tr,),
        in_specs=[pl.BlockSpec((tr, cols), lambda i: (i, 0)),
                  pl.BlockSpec((NDEV - 1, tr, cols), lambda i: (0, i, 0))],
        out_specs=pl.BlockSpec((tr, cols), lambda i: (i, 0)),
        out_shape=jax.ShapeDtypeStruct((rows, cols), F32),
        compiler_params=_cp(("parallel",)),
    )(own, land)


def _adamw_math(w, g, m, v):
    m2 = B1 * m + (1.0 - B1) * g
    v2 = B2 * v + (1.0 - B2) * (g * g)
    m_hat = m2 / (1.0 - B1 ** STEP)
    v_hat = v2 / (1.0 - B2 ** STEP)
    delta = -LR * (m_hat / (jnp.sqrt(v_hat) + AEPS) + WD * w)
    return delta, m2, v2


def _adamw(w, g, m, v, name):
    rows, cols = w.shape
    tr = 256 if rows % 256 == 0 and rows > 256 else rows

    def body(w_ref, g_ref, m_ref, v_ref, d_ref, mo_ref, vo_ref):
        d, m2, v2 = _adamw_math(w_ref[...], g_ref[...], m_ref[...], v_ref[...])
        d_ref[...] = d
        mo_ref[...] = m2
        vo_ref[...] = v2

    blk = pl.BlockSpec((tr, cols), lambda i: (i, 0))
    return pl.pallas_call(
        body, name=name, grid=(rows // tr,), in_specs=[blk] * 4, out_specs=[blk] * 3,
        out_shape=[jax.ShapeDtypeStruct((rows, cols), F32)] * 3,
        compiler_params=_cp(("parallel",)),
    )(w, g, m, v)


UPD_TC = 256


def _update(src, land, w, m, v, name):
    rows, cols = land.shape[1:]
    tc = min(UPD_TC, cols)
    me = jnp.reshape(_my_place(), (1,)).astype(jnp.int32)

    def body(me_ref, own_ref, l_ref, w_ref, m_ref, v_ref, g_ref, d_ref, mo_ref, vo_ref):
        del me_ref
        g = own_ref[...].astype(F32)
        for s in range(NDEV - 1):
            g = g + l_ref[s].astype(F32)
        g_ref[...] = g
        d, m2, v2 = _adamw_math(w_ref[...], g, m_ref[...], v_ref[...])
        d_ref[...] = d
        mo_ref[...] = m2
        vo_ref[...] = v2

    wblk = pl.BlockSpec((rows, tc), lambda j, p: (0, j))
    grid_spec = pltpu.PrefetchScalarGridSpec(
        num_scalar_prefetch=1, grid=(cols // tc,),
        in_specs=[pl.BlockSpec((rows, tc), lambda j, p: (p[0], j)),
                  pl.BlockSpec((NDEV - 1, rows, tc), lambda j, p: (0, 0, j)), wblk, wblk, wblk],
        out_specs=[wblk] * 4)
    return pl.pallas_call(
        body, name=name, grid_spec=grid_spec, out_shape=[jax.ShapeDtypeStruct((rows, cols), F32)] * 4,
        compiler_params=_cp(("parallel",)),
    )(me, src, land, w, m, v)


def _small_update(vland, w8, m8, v8, name):
    def body(l_ref, w_ref, m_ref, v_ref, g_ref, d_ref, mo_ref, vo_ref):
        g = l_ref[0]
        for s in range(1, NDEV):
            g = g + l_ref[s]
        g_ref[...] = g
        d, m2, v2 = _adamw_math(w_ref[...], g, m_ref[...], v_ref[...])
        d_ref[...] = d
        mo_ref[...] = m2
        vo_ref[...] = v2

    return pl.pallas_call(
        body, name=name, out_shape=[jax.ShapeDtypeStruct((8, D), F32)] * 4,
        compiler_params=_cp(None),
    )(vland, w8, m8, v8)


def kernel(x, ffn1_norm, ffn1_w_gate, ffn1_w_up, ffn1_w_down, mix_norm, w_in, conv_dw_kernel, conv_dw_bias, conv_ln_gain, conv_ln_bias, conv_w_out, attn_w_out, w_o, ffn2_norm, ffn2_w_gate, ffn2_w_up, ffn2_w_down, final_norm, loss_target, m_ffn1_norm, m_ffn1_w_gate, m_ffn1_w_up, m_ffn1_w_down, m_mix_norm, m_w_in, m_conv_dw_kernel, m_conv_dw_bias, m_conv_ln_gain, m_conv_ln_bias, m_conv_w_out, m_attn_w_out, m_w_o, m_ffn2_norm, m_ffn2_w_gate, m_ffn2_w_up, m_ffn2_w_down, m_final_norm, v_ffn1_norm, v_ffn1_w_gate, v_ffn1_w_up, v_ffn1_w_down, v_mix_norm, v_w_in, v_conv_dw_kernel, v_conv_dw_bias, v_conv_ln_gain, v_conv_ln_bias, v_conv_w_out, v_attn_w_out, v_w_o, v_ffn2_norm, v_ffn2_w_gate, v_ffn2_w_up, v_ffn2_w_down, v_final_norm):
    names = ["ffn1_norm", "ffn1_w_gate", "ffn1_w_up", "ffn1_w_down", "mix_norm", "w_in", "conv_dw_kernel",
             "conv_dw_bias", "conv_ln_gain", "conv_ln_bias", "conv_w_out", "attn_w_out", "w_o", "ffn2_norm",
             "ffn2_w_gate", "ffn2_w_up", "ffn2_w_down", "final_norm"]
    w = dict(ffn1_norm=ffn1_norm, ffn1_w_gate=ffn1_w_gate, ffn1_w_up=ffn1_w_up, ffn1_w_down=ffn1_w_down, mix_norm=mix_norm, w_in=w_in, conv_dw_kernel=conv_dw_kernel, conv_dw_bias=conv_dw_bias, conv_ln_gain=conv_ln_gain, conv_ln_bias=conv_ln_bias, conv_w_out=conv_w_out, attn_w_out=attn_w_out, w_o=w_o, ffn2_norm=ffn2_norm, ffn2_w_gate=ffn2_w_gate, ffn2_w_up=ffn2_w_up, ffn2_w_down=ffn2_w_down, final_norm=final_norm)
    mo = dict(ffn1_norm=m_ffn1_norm, ffn1_w_gate=m_ffn1_w_gate, ffn1_w_up=m_ffn1_w_up, ffn1_w_down=m_ffn1_w_down, mix_norm=m_mix_norm, w_in=m_w_in, conv_dw_kernel=m_conv_dw_kernel, conv_dw_bias=m_conv_dw_bias, conv_ln_gain=m_conv_ln_gain, conv_ln_bias=m_conv_ln_bias, conv_w_out=m_conv_w_out, attn_w_out=m_attn_w_out, w_o=m_w_o, ffn2_norm=m_ffn2_norm, ffn2_w_gate=m_ffn2_w_gate, ffn2_w_up=m_ffn2_w_up, ffn2_w_down=m_ffn2_w_down, final_norm=m_final_norm)
    vo = dict(ffn1_norm=v_ffn1_norm, ffn1_w_gate=v_ffn1_w_gate, ffn1_w_up=v_ffn1_w_up, ffn1_w_down=v_ffn1_w_down, mix_norm=v_mix_norm, w_in=v_w_in, conv_dw_kernel=v_conv_dw_kernel, conv_dw_bias=v_conv_dw_bias, conv_ln_gain=v_conv_ln_gain, conv_ln_bias=v_conv_ln_bias, conv_w_out=v_conv_w_out, attn_w_out=v_attn_w_out, w_o=v_w_o, ffn2_norm=v_ffn2_norm, ffn2_w_gate=v_ffn2_w_gate, ffn2_w_up=v_ffn2_w_up, ffn2_w_down=v_ffn2_w_down, final_norm=v_final_norm)
    col_sharded = ("ffn1_w_gate", "ffn1_w_up", "w_in", "attn_w_out", "ffn2_w_gate", "ffn2_w_up")
    row_sharded = ("ffn1_w_down", "conv_w_out", "w_o", "ffn2_w_down")
    small = ("ffn1_norm", "mix_norm", "ffn2_norm", "final_norm", "conv_dw_bias", "conv_ln_gain", "conv_ln_bias")

    def landing_view(a, n):
        return jnp.transpose(a[0]) if n in col_sharded else a[0]

    def own_view(a, n):
        return jnp.transpose(a)[None] if n in col_sharded else a[None]

    ag_groups = (("ffn1_w_gate", "ffn1_w_up", "ffn1_w_down"),
                 ("w_in", "attn_w_out", "conv_w_out", "w_o", "conv_dw_kernel"),
                 ("ffn2_w_gate", "ffn2_w_up", "ffn2_w_down"))
    ag, order = [], []
    for gi, grp in enumerate(ag_groups):
        lands = _prep_gather([landing_view(w[n], n) for n in grp], order, f"gather_prep{gi}")
        st = _gather_start(lands, GATHER_A, [], f"gather_a_start{gi}")
        ag.append(st)
        order = [st[3]]

    def chips_in(gi, after):
        lands = _gather_wait(ag[gi], GATHER_A, after, f"gather_a_wait{gi}")
        return _gather_start(lands, GATHER_B, [], f"gather_b_start{gi}")

    def all_in(gi, st, after):
        return _gather_wait(st, GATHER_B, after, f"gather_b_wait{gi}")

    x0 = x[0]
    tgt = loss_target[0]
    gf = final_norm.reshape(1, D)

    wg1, wu1, wd1 = all_in(0, chips_in(0, [ag[2][3]]), [])
    x1, gg1, uu1, h2p = _ffn_fwd(x0, ffn1_norm, wg1, wu1, wd1, "ffn1_fwd", next_gain=mix_norm)
    h2 = h2p[0]
    win_t, wa_t, wc, wo, kern_blocks = all_in(1, chips_in(1, [x1]), [])
    kern = kern_blocks.reshape(NDEV, 32, D // NDEV).transpose(1, 0, 2).reshape(32, D)
    ptm = min(T, 2048)
    ab = _mm(h2, win_t, mode="nt", m=T, n=2 * D, k=D, tm=ptm, tn=512, tk=D, out_dtype=BF16, name="proj_conv")
    gates = _mm(h2, win_t, mode="nt", m=T, n=2 * D, k=D, tm=ptm, tn=512, tk=D, out_dtype=BF16,
                b_map=lambda i, j, kk: (13 + j, 0), name="proj_gates")
    qkv = []
    for gi in range(len(GROUPS)):
        qkv.append(_mm(h2p[gi], win_t, mode="nt", m=T, n=3 * AW, k=D, tm=ptm, tn=AW, tk=D, out_dtype=BF16,
                       b_map=lambda i, j, kk, gi=gi: (4 + gi + 3 * j, 0), name=f"proj_qkv{gi}"))
    z1, z3b = _conv_fwd(ab, kern, conv_dw_bias, conv_ln_gain, conv_ln_bias, "conv_fwd")
    ffn2_b = chips_in(2, [z3b])
    outs, lses = [], []
    for gi, (_, dil) in enumerate(GROUPS):
        o, l = _attn_fwd(qkv[gi], gi, f"attn_fwd{gi}")
        outs.append(o)
        lses.append(l)
    attnb, lse = _merge(outs, lses, "attn_merge")
    x2, yc, ya, mixedb = _mix_out(z3b, attnb, gates, wc, wa_t, wo, x1, "mix_out_fwd")
    wg2, wu2, wd2 = all_in(2, ffn2_b, [x2])
    gg2, uu2, dx3, dgf, loss_part = _ffn_fwd(x2, ffn2_norm, wg2, wu2, wd2, "ffn2_fwd", loss_of=(gf, tgt))

    dx2, dg3, dgb, dub, actb, hb, dob = _ffn_bwd(x2, ffn2_norm, gg2, uu2, dx3, wg2, wu2, wd2, "ffn2_bwd")
    grads = {}
    grads["ffn2_w_gate"] = _wgrad(dgb, hb, FF, D, "ffn2_dwg")
    grads["ffn2_w_up"] = _wgrad(dub, hb, FF, D, "ffn2_dwu")
    grads["ffn2_w_down"] = _wgrad(actb, dob, FF, D, "ffn2_dwd")
    rs_groups = [("ffn2_w_gate", "ffn2_w_up", "ffn2_w_down"),
                 ("attn_w_out", "conv_w_out", "w_o", "conv_dw_kernel"),
                 ("w_in",),
                 ("ffn1_w_gate",), ("ffn1_w_up",), ("ffn1_w_down",), ()]
    last = len(rs_groups) - 1
    rs = [_send_start(["scatter"] * 3, [grads[n] for n in rs_groups[0]], [], "scatter_start0")]
    dx2 = _tie(dx2, [rs[0][4]], "tie_after_scatter0")

    dgates, dycb, dyab, dx2b, dz3, dattnb, delta = _mix_out_bwd(dx2, gates, yc, ya, attnb, wc, wa_t, wo, "mix_out_bwd")
    grads["w_o"] = _wgrad(mixedb, dx2b, D, D, "dw_o")
    grads["conv_w_out"] = _wgrad(z3b, dycb, D, D, "dw_conv_out")
    grads["attn_w_out"] = _wgrad(dyab, attnb, D, AW, "dw_attn_out")
    dab, dkern, dvec = _conv_bwd(dz3, z1, ab, kern, conv_ln_gain, conv_ln_bias, "conv_bwd")
    grads["conv_dw_kernel"] = dkern.reshape(32, NDEV, D // NDEV).transpose(1, 0, 2).reshape(NDEV * 32, D // NDEV)
    rs.append(_send_start(["scatter"] * 4, [grads[n] for n in rs_groups[1]], [rs[0][4]], "scatter_start1"))
    dattnb = [_tie(a, [rs[1][4]], f"tie_after_scatter1_{i}") for i, a in enumerate(dattnb)]

    dqkv = []
    for gi, (_, dil) in enumerate(GROUPS):
        dq3 = _attn_bwd(qkv[gi], dattnb[gi], lse[gi], delta[gi], gi, f"attn_bwd{gi}")
        dqkv.append(dq3.reshape(3 * T, AW))

    dwin = _mm(dab, h2, mode="tn", m=2 * D, n=D, k=T, tm=2 * D, tn=D, tk=512, out_dtype=BF16, out_rows=IN_W,
               name="dw_in_conv")
    dwin = _mm(dgates, h2, mode="tn", m=2 * D, n=D, k=T, tm=512, tn=D, tk=1024, out_dtype=BF16, out_rows=IN_W,
               o_map=lambda i, j, kk: (13 + i, 0), passthru=dwin, name="dw_in_gates")
    for gi in range(3):
        dwin = _mm(dqkv[gi], h2p[gi], mode="tn", m=3 * AW, n=D, k=T, tm=AW, tn=D, tk=1024, out_dtype=BF16,
                   out_rows=IN_W, a_map=lambda i, j, kk: (i * (T // 1024) + kk, 0),
                   o_map=lambda i, j, kk, gi=gi: (4 + gi + 3 * i, 0), passthru=dwin, name=f"dw_in_qkv{gi}")
    grads["w_in"] = dwin
    rs.append(_send_start(["scatter"], [dwin], [rs[1][4]], "scatter_start2"))
    dab = _tie(dab, [rs[2][4]], "tie_after_scatter2")

    nrow = T // 1024
    dh = _mm(dab, win_t, mode="nn", m=T, n=D, k=2 * D, tm=1024, tn=D, tk=1024, out_dtype=F32, name="dproj_conv")
    dh = _mm(dgates, win_t, mode="nn", m=T, n=D, k=2 * D, tm=1024, tn=D, tk=512, out_dtype=F32,
             b_map=lambda i, j, kk: (13 + kk, 0), init=dh, name="dproj_gates")
    dhs = []
    for gi, (_, dil) in enumerate(GROUPS):
        part = _mm(dqkv[gi], win_t, mode="nn", m=T, n=D, k=3 * AW, tm=1024, tn=D, tk=AW,
                   out_dtype=F32 if gi == 0 else BF16,
                   a_map=lambda i, j, kk: (kk * nrow + i, 0), b_map=lambda i, j, kk, gi=gi: (4 + gi + 3 * kk, 0),
                   init=dh if gi == 0 else None, name=f"dproj_qkv{gi}")
        dhs.append(part)
    dx1, dg2 = _rms_bwd(x1, mix_norm, dhs, dx2, "mix_norm_bwd")

    dgb, dub, actb, hb, dob = _ffn_bwd_pre(x0, ffn1_norm, gg1, uu1, dx1, wd1, "ffn1_bwd_pre")
    grads["ffn1_w_gate"] = _wgrad(dgb, hb, FF, D, "ffn1_dwg")
    rs.append(_send_start(["scatter"], [grads["ffn1_w_gate"]], [rs[2][4]], "scatter_start3"))
    hb = _tie(hb, [rs[3][4]], "tie_after_scatter3")
    grads["ffn1_w_up"] = _wgrad(dub, hb, FF, D, "ffn1_dwu")
    rs.append(_send_start(["scatter"], [grads["ffn1_w_up"]], [rs[3][4]], "scatter_start4"))
    dob = _tie(dob, [rs[4][4]], "tie_after_scatter4")
    grads["ffn1_w_down"] = _wgrad(actb, dob, FF, D, "ffn1_dwd")
    rs.append(_send_start(["scatter"], [grads["ffn1_w_down"]], [rs[4][4]], "scatter_start5"))
    dgb = _tie(dgb, [rs[5][4]], "tie_after_scatter5")
    dx0, dg1 = _ffn_bwd_dx(x0, ffn1_norm, dgb, dub, dx1, wg1, wu1, "ffn1_bwd_dx")
    vec = jnp.concatenate([dg1, dg2, dg3, dgf, dvec[0:3], jnp.broadcast_to(loss_part[:, :1], (1, D))], axis=0)
    rs.append(_send_start(["bcast"], [vec], [rs[5][4]], "scatter_start6"))

    g_out, d_out, m_out, v_out = {}, {}, {}, {}
    me = _my_place()
    after = [rs[last][4]]
    for gi, grp in enumerate(rs_groups):
        kinds = ["scatter"] * len(grp) + (["bcast"] if gi == last else [])
        srcs, lands = _send_wait(kinds, rs[gi], after, f"scatter_wait{gi}")
        for n, src, land in zip(grp, srcs, lands):
            if n == "conv_dw_kernel":
                rows = src.shape[0] // NDEV
                own = lax.dynamic_slice(src, (me * rows, 0), (rows, src.shape[1]))
                g = _gsum(own, land, f"gsum_{n}")[:CONV_W]
                d, m2, v2 = _adamw(w[n][0], g, mo[n][0], vo[n][0], f"adamw_{n}")
                after = [d]
                g, d, m2, v2 = g[None], d[None], m2[None], v2[None]
            else:
                res = _update(src, land, landing_view(w[n], n), landing_view(mo[n], n), landing_view(vo[n], n),
                              f"update_{n}")
                after = [res[1]]
                g, d, m2, v2 = (own_view(a, n) for a in res)
            g_out[n], d_out[n], m_out[n], v_out[n] = g, d, m2, v2
    vland = lands[-1]

    def rows8(src):
        return jnp.concatenate([src[n].reshape(1, D) for n in small] + [jnp.ones((1, D), F32)], axis=0)

    g8, d8, m8, v8 = _small_update(vland, rows8(w), rows8(mo), rows8(vo), "small_update")
    for r, n in enumerate(small):
        shp = w[n].shape
        g_out[n], d_out[n], m_out[n], v_out[n] = (a[r].reshape(shp) for a in (g8, d8, m8, v8))
    loss = g8[7, 0]

    return (loss, dx0[None], *[g_out[n] for n in names], *[d_out[n] for n in names],
            *[m_out[n] for n in names], *[v_out[n] for n in names])
```

```python
import numpy as np
import jax
import jax.numpy as jnp
from jax import lax
from jax.experimental import pallas as pl
from jax.experimental.pallas import tpu as pltpu

F32 = jnp.float32
BF16 = jnp.bfloat16

T = 4096
D = 1024
FF = 2816
NDEV = 8
CONV_W = 31
HEAD = 128
BLK = 128
GROUPS = ((128, 1), (512, 4), (2048, 16))
NHG = 4
AW = NHG * HEAD
IN_W = 2 * D + 3 * 3 * AW + 2 * D
EPS = 1e-6
B1, B2, LR, AEPS, WD, STEP = 0.9, 0.999, 0.001, 1e-08, 0.01, 10
NEG = -1e30
VMEM_LIMIT = 56 * 1024 * 1024
MESH_ID = pl.DeviceIdType.MESH

NT = (((1,), (1,)), ((), ()))
NN = (((1,), (0,)), ((), ()))
TN = (((0,), (0,)), ((), ()))
_DIMS = {"nn": NN, "nt": NT, "tn": TN}


def _cp(sem=None):
    return pltpu.CompilerParams(dimension_semantics=sem, vmem_limit_bytes=VMEM_LIMIT)


def _sig(v):
    return 1.0 / (1.0 + jnp.exp(-v))


def _dot(a, b, dims):
    return lax.dot_general(a, b, dims, preferred_element_type=F32)


def _const_spec(shape):
    nd = len(shape)
    return pl.BlockSpec(shape, lambda *_: (0,) * nd)


def _mm(a, b, *, mode, m, n, k, tm, tn, tk, out_dtype, name, a_map=None, b_map=None,
        o_map=None, out_rows=None, init=None, passthru=None):
    gi, gj, gk = m // tm, n // tn, k // tk
    assert gi * tm == m and gj * tn == n and gk * tk == k, (name, m, n, k, tm, tn, tk)
    if mode == "nn":
        a_blk, b_blk = (tm, tk), (tk, tn)
        da, db = (lambda i, j, kk: (i, kk)), (lambda i, j, kk: (kk, j))
    elif mode == "nt":
        a_blk, b_blk = (tm, tk), (tn, tk)
        da, db = (lambda i, j, kk: (i, kk)), (lambda i, j, kk: (j, kk))
    else:
        a_blk, b_blk = (tk, tm), (tk, tn)
        da, db = (lambda i, j, kk: (kk, i)), (lambda i, j, kk: (kk, j))
    a_map = a_map or da
    b_map = b_map or db
    o_map = o_map or (lambda i, j, kk: (i, j))
    dims = _DIMS[mode]
    extra = init if init is not None else passthru
    out_rows = out_rows or m

    def body(*refs):
        if init is not None:
            a_ref, b_ref, i_ref, o_ref = refs[:4]
        elif passthru is not None:
            a_ref, b_ref, _, o_ref = refs[:4]
        else:
            a_ref, b_ref, o_ref = refs[:3]
        if gk == 1:
            prod = _dot(a_ref[...], b_ref[...], dims)
            if init is not None:
                prod = prod + i_ref[...].astype(F32)
            o_ref[...] = prod.astype(out_dtype)
            return
        acc = refs[-1]
        kk = pl.program_id(2)

        @pl.when(kk == 0)
        def _():
            if init is not None:
                acc[...] = i_ref[...].astype(F32)
            else:
                acc[...] = jnp.zeros_like(acc)

        acc[...] += _dot(a_ref[...], b_ref[...], dims)

        @pl.when(kk == gk - 1)
        def _():
            o_ref[...] = acc[...].astype(out_dtype)

    in_specs = [pl.BlockSpec(a_blk, a_map), pl.BlockSpec(b_blk, b_map)]
    args = [a, b]
    aliases = {}
    if init is not None:
        in_specs.append(pl.BlockSpec((tm, tn), o_map))
        args.append(init)
        aliases = {2: 0}
    elif passthru is not None:
        in_specs.append(pl.BlockSpec(memory_space=pl.ANY))
        args.append(passthru)
        aliases = {2: 0}
    out_dt = extra.dtype if extra is not None else out_dtype
    assert out_dt == out_dtype
    return pl.pallas_call(
        body, name=name, grid=(gi, gj, gk),
        in_specs=in_specs, out_specs=pl.BlockSpec((tm, tn), o_map),
        out_shape=jax.ShapeDtypeStruct((out_rows, n), out_dtype),
        scratch_shapes=[pltpu.VMEM((tm, tn), F32)] if gk > 1 else [],
        input_output_aliases=aliases,
        compiler_params=_cp(("parallel", "parallel", "arbitrary")),
    )(*args)


def _ffn_fwd(x, g, wg_t, wu_t, wd, name, next_gain=None, loss_of=None):
    tm, fc = PERM_TM, 256
    nc = FF // fc
    n_in = 5 + (1 if next_gain is not None else 0) + (2 if loss_of is not None else 0)

    def body(*refs):
        x_ref, g_ref, wg_ref, wu_ref, wd_ref = refs[:5]
        extra_in, outs = refs[5:n_in], refs[n_in:]
        act_ref = outs[-1]
        xv = x_ref[...]
        r = lax.rsqrt(jnp.mean(xv * xv, axis=-1, keepdims=True) + EPS)
        h = (xv * r * g_ref[...]).astype(BF16)
        gg_ref, uu_ref = (outs[0], outs[1]) if loss_of is not None else (outs[1], outs[2])
        for c in range(nc):
            sl = pl.ds(c * fc, fc)
            gg = _dot(h, wg_ref[sl, :], NT)
            uu = _dot(h, wu_ref[sl, :], NT)
            gg_ref[:, sl] = gg.astype(BF16)
            uu_ref[:, sl] = uu.astype(BF16)
            act_ref[:, sl] = (gg * _sig(gg) * uu).astype(BF16)
        y = xv + 0.5 * _dot(act_ref[...], wd_ref[...], NN)
        if loss_of is not None:
            _final_math(y, extra_in[0][...], extra_in[1][...], outs[2], outs[3], outs[4], pl.program_id(0))
            return
        outs[0][...] = y
        if next_gain is not None:
            tile = outs[-2]
            r2 = lax.rsqrt(jnp.mean(y * y, axis=-1, keepdims=True) + EPS)
            hv = y * r2 * extra_in[0][...]
            outs[3][...] = hv.astype(BF16)
            _put_tile(tile, hv)
            for dil, p_ref in zip(DILS, outs[4:4 + len(DILS)]):
                _store_perm(p_ref, tile, dil)

    wspec = pl.BlockSpec((FF, D), lambda i: (0, 0), pipeline_mode=pl.Buffered(1))
    row_d = pl.BlockSpec((tm, D), lambda i: (i, 0))
    row_f = pl.BlockSpec((tm, FF), lambda i: (i, 0))
    in_specs = [row_d, _const_spec((1, D)), wspec, wspec, wspec]
    args = [x, g, wg_t, wu_t, wd]
    f_shape = jax.ShapeDtypeStruct((T, FF), BF16)
    scratch = [pltpu.VMEM((tm, FF), BF16)]
    if loss_of is not None:
        in_specs += [_const_spec((1, D)), row_d]
        args += list(loss_of)
        out_specs = [row_f, row_f, row_d, _const_spec((1, D)), _const_spec((1, 128))]
        out_shape = [f_shape, f_shape, jax.ShapeDtypeStruct((T, D), F32), jax.ShapeDtypeStruct((1, D), F32),
                     jax.ShapeDtypeStruct((1, 128), F32)]
    else:
        out_specs = [row_d, row_f, row_f]
        out_shape = [jax.ShapeDtypeStruct((T, D), F32), f_shape, f_shape]
        if next_gain is not None:
            in_specs.append(_const_spec((1, D)))
            args.append(next_gain)
            out_specs += [row_d] + [_perm_spec(d, D) for d in DILS]
            out_shape += [jax.ShapeDtypeStruct((T, D), BF16)] + [_perm_shape(d, D, BF16) for d in DILS]
            scratch = [_tile_scratch(D)] + scratch
    out = pl.pallas_call(
        body, name=name, grid=(T // tm,), in_specs=in_specs, out_specs=out_specs, out_shape=out_shape,
        scratch_shapes=scratch,
        compiler_params=_cp(("arbitrary",) if loss_of is not None else ("parallel",)),
    )(*args)
    if next_gain is not None:
        return out[0], out[1], out[2], [out[3]] + [o.reshape(T, D) for o in out[4:]]
    return tuple(out)


def _ffn_bwd(x, g, gg_all, uu_all, dout, wg_t, wu_t, wd, name):
    tm, fc = 256, 256
    nc = FF // fc

    def body(x_ref, g_ref, gg_ref, uu_ref, do_ref, wg_ref, wu_ref, wd_ref,
             dx_ref, dgam_ref, dg_ref, du_ref, act_ref, h_ref, db_ref):
        i = pl.program_id(0)
        xv = x_ref[...]
        r = lax.rsqrt(jnp.mean(xv * xv, axis=-1, keepdims=True) + EPS)
        xhat = xv * r
        gam = g_ref[...]
        h_ref[...] = (xhat * gam).astype(BF16)
        dov = do_ref[...]
        dbv = (0.5 * dov).astype(BF16)
        db_ref[...] = dbv
        for c in range(nc):
            sl = pl.ds(c * fc, fc)
            da = _dot(dbv, wd_ref[sl, :], NT)
            gg = gg_ref[:, sl].astype(F32)
            uu = uu_ref[:, sl].astype(F32)
            s = _sig(gg)
            si = gg * s
            dgv = (da * uu * (s * (1.0 + gg * (1.0 - s)))).astype(BF16)
            duv = (da * si).astype(BF16)
            dg_ref[:, sl] = dgv
            du_ref[:, sl] = duv
            act_ref[:, sl] = (si * uu).astype(BF16)
        dh = _dot(dg_ref[...], wg_ref[...], NN) + _dot(du_ref[...], wu_ref[...], NN)

        @pl.when(i == 0)
        def _():
            dgam_ref[...] = jnp.zeros_like(dgam_ref)

        dgam_ref[...] += jnp.sum(dh * xhat, axis=0, keepdims=True)
        dxh = dh * gam
        dx_ref[...] = dov + r * (dxh - xhat * jnp.mean(dxh * xhat, axis=-1, keepdims=True))

    wspec = pl.BlockSpec((FF, D), lambda i: (0, 0), pipeline_mode=pl.Buffered(1))
    row_d = pl.BlockSpec((tm, D), lambda i: (i, 0))
    row_f = pl.BlockSpec((tm, FF), lambda i: (i, 0))
    return pl.pallas_call(
        body, name=name, grid=(T // tm,),
        in_specs=[row_d, _const_spec((1, D)), row_f, row_f, row_d, wspec, wspec, wspec],
        out_specs=[row_d, _const_spec((1, D)), row_f, row_f, row_f, row_d, row_d],
        out_shape=[jax.ShapeDtypeStruct((T, D), F32), jax.ShapeDtypeStruct((1, D), F32),
                   jax.ShapeDtypeStruct((T, FF), BF16), jax.ShapeDtypeStruct((T, FF), BF16),
                   jax.ShapeDtypeStruct((T, FF), BF16), jax.ShapeDtypeStruct((T, D), BF16),
                   jax.ShapeDtypeStruct((T, D), BF16)],
        compiler_params=_cp(("arbitrary",)),
    )(x, g, gg_all, uu_all, dout, wg_t, wu_t, wd)


def _ffn_bwd_pre(x, g, gg_all, uu_all, dout, wd, name):
    tm, fc = 512, 256
    nc = FF // fc

    def body(x_ref, g_ref, gg_ref, uu_ref, do_ref, wd_ref, dg_ref, du_ref, act_ref, h_ref, db_ref):
        xv = x_ref[...]
        r = lax.rsqrt(jnp.mean(xv * xv, axis=-1, keepdims=True) + EPS)
        h_ref[...] = (xv * r * g_ref[...]).astype(BF16)
        dbv = (0.5 * do_ref[...]).astype(BF16)
        db_ref[...] = dbv
        for c in range(nc):
            sl = pl.ds(c * fc, fc)
            da = _dot(dbv, wd_ref[sl, :], NT)
            gg = gg_ref[:, sl].astype(F32)
            uu = uu_ref[:, sl].astype(F32)
            s = _sig(gg)
            si = gg * s
            dg_ref[:, sl] = (da * uu * (s * (1.0 + gg * (1.0 - s)))).astype(BF16)
            du_ref[:, sl] = (da * si).astype(BF16)
            act_ref[:, sl] = (si * uu).astype(BF16)

    wspec = pl.BlockSpec((FF, D), lambda i: (0, 0), pipeline_mode=pl.Buffered(1))
    row_d = pl.BlockSpec((tm, D), lambda i: (i, 0))
    row_f = pl.BlockSpec((tm, FF), lambda i: (i, 0))
    return pl.pallas_call(
        body, name=name, grid=(T // tm,),
        in_specs=[row_d, _const_spec((1, D)), row_f, row_f, row_d, wspec],
        out_specs=[row_f, row_f, row_f, row_d, row_d],
        out_shape=[jax.ShapeDtypeStruct((T, FF), BF16), jax.ShapeDtypeStruct((T, FF), BF16),
                   jax.ShapeDtypeStruct((T, FF), BF16), jax.ShapeDtypeStruct((T, D), BF16),
                   jax.ShapeDtypeStruct((T, D), BF16)],
        compiler_params=_cp(("parallel",)),
    )(x, g, gg_all, uu_all, dout, wd)


def _ffn_bwd_dx(x, g, dgb, dub, dout, wg_t, wu_t, name):
    tm = 512

    def body(x_ref, g_ref, dg_ref, du_ref, do_ref, wg_ref, wu_ref, dx_ref, dgam_ref):
        i = pl.program_id(0)
        xv = x_ref[...]
        r = lax.rsqrt(jnp.mean(xv * xv, axis=-1, keepdims=True) + EPS)
        xhat = xv * r
        gam = g_ref[...]
        dh = _dot(dg_ref[...], wg_ref[...], NN) + _dot(du_ref[...], wu_ref[...], NN)

        @pl.when(i == 0)
        def _():
            dgam_ref[...] = jnp.zeros_like(dgam_ref)

        dgam_ref[...] += jnp.sum(dh * xhat, axis=0, keepdims=True)
        dxh = dh * gam
        dx_ref[...] = do_ref[...] + r * (dxh - xhat * jnp.mean(dxh * xhat, axis=-1, keepdims=True))

    wspec = pl.BlockSpec((FF, D), lambda i: (0, 0), pipeline_mode=pl.Buffered(1))
    row_d = pl.BlockSpec((tm, D), lambda i: (i, 0))
    row_f = pl.BlockSpec((tm, FF), lambda i: (i, 0))
    return pl.pallas_call(
        body, name=name, grid=(T // tm,),
        in_specs=[row_d, _const_spec((1, D)), row_f, row_f, row_d, wspec, wspec],
        out_specs=[row_d, _const_spec((1, D))],
        out_shape=[jax.ShapeDtypeStruct((T, D), F32), jax.ShapeDtypeStruct((1, D), F32)],
        compiler_params=_cp(("arbitrary",)),
    )(x, g, dgb, dub, dout, wg_t, wu_t)


def _wgrad(a, b, m, n, name):
    tm = m // 2 if m == FF else m
    return _mm(a, b, mode="tn", m=m, n=n, k=T, tm=tm, tn=n, tk=min(T, 2048), out_dtype=BF16, name=name)


PERM_TM = 512
DILS = tuple(d for _, d in GROUPS if d > 1)


def _perm_spec(dil, cols):
    return pl.BlockSpec((dil, PERM_TM // dil, cols), lambda i: (0, i, 0))


def _perm_shape(dil, cols, dtype):
    return jax.ShapeDtypeStruct((dil, T // dil, cols), dtype)


LANES = 128


def _tile_scratch(cols):
    return pltpu.VMEM((cols // LANES, PERM_TM, LANES), F32)


def _put_tile(tile, value):
    for c in range(tile.shape[0]):
        tile[c] = value[:, c * LANES:(c + 1) * LANES]


def _get_tile(tile):
    return jnp.concatenate([tile[c] for c in range(tile.shape[0])], axis=1)


def _store_perm(out_ref, tile, dil):
    for r in range(dil):
        for c in range(tile.shape[0]):
            out_ref[r, :, pl.ds(c * LANES, LANES)] = tile[c, pl.ds(r, PERM_TM // dil, stride=dil), :].astype(
                out_ref.dtype)


def _load_unperm(in_ref, tile, dil):
    for r in range(dil):
        for c in range(tile.shape[0]):
            tile[c, pl.ds(r, PERM_TM // dil, stride=dil), :] = in_ref[r, :, pl.ds(c * LANES, LANES)].astype(F32)


def _norm_cast(x, g, name):
    tm = PERM_TM

    def body(x_ref, g_ref, h_ref, *rest):
        p_refs, tile = rest[:-1], rest[-1]
        xv = x_ref[...]
        r = lax.rsqrt(jnp.mean(xv * xv, axis=-1, keepdims=True) + EPS)
        hv = xv * r * g_ref[...]
        h_ref[...] = hv.astype(BF16)
        _put_tile(tile, hv)
        for dil, p_ref in zip(DILS, p_refs):
            _store_perm(p_ref, tile, dil)

    out = pl.pallas_call(
        body, name=name, grid=(T // tm,),
        in_specs=[pl.BlockSpec((tm, D), lambda i: (i, 0)), _const_spec((1, D))],
        out_specs=[pl.BlockSpec((tm, D), lambda i: (i, 0))] + [_perm_spec(d, D) for d in DILS],
        out_shape=[jax.ShapeDtypeStruct((T, D), BF16)] + [_perm_shape(d, D, BF16) for d in DILS],
        scratch_shapes=[_tile_scratch(D)],
        compiler_params=_cp(("parallel",)),
    )(x, g)
    return [out[0]] + [o.reshape(T, D) for o in out[1:]]


def _final_math(xv, gam, tgt, dx_ref, dgam_ref, loss_ref, i):
    r = lax.rsqrt(jnp.mean(xv * xv, axis=-1, keepdims=True) + EPS)
    xhat = xv * r
    err = xhat * gam - tgt
    part = 0.5 * jnp.sum(jnp.mean(err * err, axis=-1, keepdims=True), axis=0, keepdims=True)
    dy = err * (1.0 / D)

    @pl.when(i == 0)
    def _():
        dgam_ref[...] = jnp.zeros_like(dgam_ref)
        loss_ref[...] = jnp.zeros_like(loss_ref)

    dgam_ref[...] += jnp.sum(dy * xhat, axis=0, keepdims=True)
    loss_ref[...] += jnp.broadcast_to(part, loss_ref.shape)
    dxh = dy * gam
    dx_ref[...] = r * (dxh - xhat * jnp.mean(dxh * xhat, axis=-1, keepdims=True))


def _rms_bwd(x, g, dhs, dres, name):
    tm = PERM_TM
    dils = [d for _, d in GROUPS]
    nh = len(dhs)
    assert nh == len(dils)

    def body(*refs):
        x_ref, g_ref = refs[:2]
        dh_refs = refs[2:2 + nh]
        dr_ref, dx_ref, dgam_ref, tile = refs[2 + nh:]
        i = pl.program_id(0)
        xv = x_ref[...]
        r = lax.rsqrt(jnp.mean(xv * xv, axis=-1, keepdims=True) + EPS)
        xhat = xv * r
        gam = g_ref[...]
        dh = None
        for dil, ref in zip(dils, dh_refs):
            if dil == 1:
                part = ref[...]
            else:
                _load_unperm(ref, tile, dil)
                part = _get_tile(tile)
            dh = part if dh is None else dh + part

        @pl.when(i == 0)
        def _():
            dgam_ref[...] = jnp.zeros_like(dgam_ref)

        dgam_ref[...] += jnp.sum(dh * xhat, axis=0, keepdims=True)
        dxh = dh * gam
        dx_ref[...] = dr_ref[...] + r * (dxh - xhat * jnp.mean(dxh * xhat, axis=-1, keepdims=True))

    row_d = pl.BlockSpec((tm, D), lambda i: (i, 0))
    dh_specs = [row_d if d == 1 else _perm_spec(d, D) for d in dils]
    dh_args = [a if d == 1 else a.reshape(d, T // d, D) for d, a in zip(dils, dhs)]
    return pl.pallas_call(
        body, name=name, grid=(T // tm,),
        in_specs=[row_d, _const_spec((1, D))] + dh_specs + [row_d],
        out_specs=[row_d, _const_spec((1, D))],
        out_shape=[jax.ShapeDtypeStruct((T, D), F32), jax.ShapeDtypeStruct((1, D), F32)],
        scratch_shapes=[_tile_scratch(D)],
        compiler_params=_cp(("arbitrary",)),
    )(x, g, *dh_args, dres)


CONV_TM = 256
CONV_HALO = 32
CONV_RB = 16


def _glu(ab):
    ab = ab.astype(F32)
    return ab[:, :D] * _sig(ab[:, D:])


def _ln_stats(z1):
    mu = jnp.mean(z1, axis=-1, keepdims=True)
    zc = z1 - mu
    rstd = lax.rsqrt(jnp.mean(zc * zc, axis=-1, keepdims=True) + EPS)
    return zc * rstd, rstd


def _fill_shifts(zs):
    n = zs.shape[1] - 8
    for s in range(1, 8):
        zs[s, pl.ds(0, n), :] = zs[0, pl.ds(s, n), :]


def _shifted(zs, start, rows):
    q, s = divmod(start, 8)
    return zs[s, pl.ds(8 * q, rows), :]


def _conv_fwd(ab, kern, dwb, lng, lnb, name):
    tm, hl, rb = CONV_TM, CONV_HALO, CONV_RB
    off = hl - (CONV_W - 1)

    def body(ab_ref, abh_ref, k_ref, dwb_ref, lng_ref, lnb_ref, z1_ref, z3_ref, zs):
        i = pl.program_id(0)
        zs[0, pl.ds(0, hl), :] = jnp.where(i > 0, _glu(abh_ref[...]), 0.0)
        zs[0, pl.ds(hl, tm), :] = _glu(ab_ref[...])
        _fill_shifts(zs)
        for b in range(tm // rb):
            acc = jnp.zeros((rb, D), F32)
            for j in range(CONV_W):
                acc = acc + _shifted(zs, b * rb + off + j, rb) * k_ref[pl.ds(j, 1), :]
            z1 = acc + dwb_ref[...]
            z1_ref[pl.ds(b * rb, rb), :] = z1
            zn, _ = _ln_stats(z1)
            z2 = zn * lng_ref[...] + lnb_ref[...]
            z3_ref[pl.ds(b * rb, rb), :] = (z2 * _sig(z2)).astype(BF16)

    row = pl.BlockSpec((tm, D), lambda i: (i, 0))
    return pl.pallas_call(
        body, name=name, grid=(T // tm,),
        in_specs=[pl.BlockSpec((tm, 2 * D), lambda i: (i, 0)),
                  pl.BlockSpec((hl, 2 * D), lambda i: (jnp.maximum(i * (tm // hl) - 1, 0), 0)),
                  _const_spec((32, D)), _const_spec((1, D)), _const_spec((1, D)), _const_spec((1, D))],
        out_specs=[row, row],
        out_shape=[jax.ShapeDtypeStruct((T, D), F32), jax.ShapeDtypeStruct((T, D), BF16)],
        scratch_shapes=[pltpu.VMEM((8, hl + tm, D), F32)],
        compiler_params=_cp(("parallel",)),
    )(ab, ab, kern, dwb, lng, lnb)


def _conv_bwd(dz3, z1, ab, kern, lng, lnb, name):
    tm, hl, rb = CONV_TM, CONV_HALO, CONV_RB
    off = hl - (CONV_W - 1)
    nsteps = T // tm

    def ln_bwd(dz3v, z1v, lngv, lnbv):
        zn, rstd = _ln_stats(z1v)
        z2 = zn * lngv + lnbv
        s = _sig(z2)
        dz2 = dz3v * (s * (1.0 + z2 * (1.0 - s)))
        dzn = dz2 * lngv
        dz1 = rstd * (dzn - jnp.mean(dzn, axis=-1, keepdims=True)
                      - zn * jnp.mean(dzn * zn, axis=-1, keepdims=True))
        return dz1, dz2, zn

    def body(dz3_ref, dz3h_ref, z1_ref, z1h_ref, ab_ref, abh_ref, k_ref, lng_ref, lnb_ref,
             dab_ref, dk_ref, dvec_ref, zs, dzs):
        i = pl.program_id(0)
        lngv, lnbv = lng_ref[...], lnb_ref[...]

        @pl.when(i == 0)
        def _():
            dk_ref[...] = jnp.zeros_like(dk_ref)
            dvec_ref[...] = jnp.zeros_like(dvec_ref)

        dz1, dz2, zn = ln_bwd(dz3_ref[...].astype(F32), z1_ref[...], lngv, lnbv)
        dvec_ref[pl.ds(0, 1), :] += jnp.sum(dz1, axis=0, keepdims=True)
        dvec_ref[pl.ds(1, 1), :] += jnp.sum(dz2 * zn, axis=0, keepdims=True)
        dvec_ref[pl.ds(2, 1), :] += jnp.sum(dz2, axis=0, keepdims=True)
        dzs[0, pl.ds(0, tm), :] = dz1
        dz1h, _, _ = ln_bwd(dz3h_ref[...].astype(F32), z1h_ref[...], lngv, lnbv)
        dzs[0, pl.ds(tm, hl), :] = jnp.where(i < nsteps - 1, dz1h, 0.0)
        _fill_shifts(dzs)
        zs[0, pl.ds(0, hl), :] = jnp.where(i > 0, _glu(abh_ref[...]), 0.0)
        zs[0, pl.ds(hl, tm), :] = _glu(ab_ref[...])
        _fill_shifts(zs)

        for j in range(CONV_W):
            tot = jnp.zeros((rb, D), F32)
            for b in range(tm // rb):
                tot = tot + dzs[0, pl.ds(b * rb, rb), :] * _shifted(zs, b * rb + off + j, rb)
            dk_ref[pl.ds(j, 1), :] += jnp.sum(tot, axis=0, keepdims=True)

        for b in range(tm // rb):
            acc = jnp.zeros((rb, D), F32)
            for j in range(CONV_W):
                acc = acc + _shifted(dzs, b * rb + (CONV_W - 1) - j, rb) * k_ref[pl.ds(j, 1), :]
            av = ab_ref[pl.ds(b * rb, rb), pl.ds(0, D)].astype(F32)
            sb = _sig(ab_ref[pl.ds(b * rb, rb), pl.ds(D, D)].astype(F32))
            dab_ref[pl.ds(b * rb, rb), pl.ds(0, D)] = (acc * sb).astype(BF16)
            dab_ref[pl.ds(b * rb, rb), pl.ds(D, D)] = (acc * av * sb * (1.0 - sb)).astype(BF16)

    row = pl.BlockSpec((tm, D), lambda i: (i, 0))
    nxt = pl.BlockSpec((hl, D), lambda i: (jnp.minimum((i + 1) * (tm // hl), T // hl - 1), 0))
    return pl.pallas_call(
        body, name=name, grid=(nsteps,),
        in_specs=[row, nxt, row, nxt,
                  pl.BlockSpec((tm, 2 * D), lambda i: (i, 0)),
                  pl.BlockSpec((hl, 2 * D), lambda i: (jnp.maximum(i * (tm // hl) - 1, 0), 0)),
                  _const_spec((32, D)), _const_spec((1, D)), _const_spec((1, D))],
        out_specs=[pl.BlockSpec((tm, 2 * D), lambda i: (i, 0)), _const_spec((32, D)), _const_spec((8, D))],
        out_shape=[jax.ShapeDtypeStruct((T, 2 * D), BF16), jax.ShapeDtypeStruct((32, D), F32),
                   jax.ShapeDtypeStruct((8, D), F32)],
        scratch_shapes=[pltpu.VMEM((8, hl + tm, D), F32), pltpu.VMEM((8, tm + hl, D), F32)],
        compiler_params=_cp(("arbitrary",)),
    )(dz3, dz3, z1, z1, ab, ab, kern, lng, lnb)


def _alibi_slopes():
    h = np.arange(1, 3 * NHG + 1, dtype=np.float32)
    return np.power(np.float32(2.0), -8.0 * h / np.float32(3 * NHG)).astype(np.float32)


def _band_bias(gi):
    _, dil = GROUPS[gi]
    slopes = _alibi_slopes()[gi * NHG:(gi + 1) * NHG]
    qi = np.arange(BLK)[:, None]
    ki = np.arange(2 * BLK)[None, :]
    steps = BLK + qi - ki
    band = (steps >= 0) & (steps <= BLK)
    bias = -slopes[:, None, None] * (dil * steps).astype(np.float32)[None]
    return jnp.asarray(np.where(band[None], bias, np.float32(NEG)).astype(np.float32))


QB_FWD = 4
QB_BWD = 16


def _attn_specs(qb):
    prev = lambda n: jnp.maximum(n * qb - 1, 0)
    return [pl.BlockSpec((qb * BLK, HEAD), lambda h, n: (n, h)),
            pl.BlockSpec((BLK, HEAD), lambda h, n: (prev(n), NHG + h)),
            pl.BlockSpec((qb * BLK, HEAD), lambda h, n: (n, NHG + h)),
            pl.BlockSpec((BLK, HEAD), lambda h, n: (prev(n), 2 * NHG + h)),
            pl.BlockSpec((qb * BLK, HEAD), lambda h, n: (n, 2 * NHG + h)),
            pl.BlockSpec((None, BLK, 2 * BLK), lambda h, n: (h, 0, 0))]


def _scores(q, kcat, bias, blk, seg):
    s = _dot(q, kcat, NT) * (HEAD ** -0.5) + bias
    col = lax.broadcasted_iota(jnp.int32, s.shape, 1)
    first = (blk % seg) == 0
    return jnp.where(jnp.logical_and(first, col < BLK), NEG, s)


def _attn_fwd(qkv, gi, name):
    seg = (T // GROUPS[gi][1]) // BLK

    qb = QB_FWD

    def body(q_ref, kp_ref, kc_ref, vp_ref, vc_ref, bias_ref, o_ref, l_ref):
        n = pl.program_id(0)
        for h in range(NHG):
            cols = pl.ds(h * HEAD, HEAD)
            kwin = jnp.concatenate([kp_ref[:, cols], kc_ref[:, cols]], axis=0)
            vwin = jnp.concatenate([vp_ref[:, cols], vc_ref[:, cols]], axis=0)
            bias = bias_ref[h]
            for b in range(qb):
                rows = pl.ds(b * BLK, BLK)
                s = _scores(q_ref[rows, cols], kwin[b * BLK:(b + 2) * BLK], bias, n * qb + b, seg)
                mx = jnp.max(s, axis=-1, keepdims=True)
                p = jnp.exp(s - mx)
                den = jnp.sum(p, axis=-1, keepdims=True)
                o_ref[rows, cols] = (_dot(p.astype(BF16), vwin[b * BLK:(b + 2) * BLK], NN) / den).astype(BF16)
                l_ref[rows, cols] = jnp.broadcast_to(mx + jnp.log(den), (BLK, HEAD))

    prev = lambda n: jnp.maximum(n * qb - 1, 0)
    cur = lambda part: pl.BlockSpec((qb * BLK, AW), lambda n: (n, part))
    halo = lambda part: pl.BlockSpec((BLK, AW), lambda n: (prev(n), part))
    return pl.pallas_call(
        body, name=name, grid=(T // (qb * BLK),),
        in_specs=[cur(0), halo(1), cur(1), halo(2), cur(2), _const_spec((NHG, BLK, 2 * BLK))],
        out_specs=[cur(0), cur(0)],
        out_shape=[jax.ShapeDtypeStruct((T, AW), BF16), jax.ShapeDtypeStruct((T, AW), F32)],
        compiler_params=_cp(("parallel",)),
    )(qkv, qkv, qkv, qkv, qkv, _band_bias(gi))


def _attn_bwd(qkv, dob, lse, delta, gi, name):
    seg = (T // GROUPS[gi][1]) // BLK
    qb = QB_BWD
    nb = T // (qb * BLK)
    scale = HEAD ** -0.5

    def body(q_ref, kp_ref, kc_ref, vp_ref, vc_ref, bias_ref, do_ref, l_ref, dl_ref, out_ref, dk_acc, dv_acc):
        n = pl.program_id(1)
        kwin = jnp.concatenate([kp_ref[...], kc_ref[...]], axis=0)
        vwin = jnp.concatenate([vp_ref[...], vc_ref[...]], axis=0)
        bias = bias_ref[...]
        dks, dvs = [], []
        for b in range(qb):
            rows = pl.ds(b * BLK, BLK)
            q = q_ref[rows, :]
            kcat = kwin[b * BLK:(b + 2) * BLK]
            s = _scores(q, kcat, bias, n * qb + b, seg)
            p = jnp.exp(s - l_ref[rows, pl.ds(0, 1)])
            dov = do_ref[rows, :]
            dvs.append(_dot(p.astype(BF16), dov, TN))
            dp = _dot(dov, vwin[b * BLK:(b + 2) * BLK], NT)
            dsb = (p * (dp - dl_ref[rows, pl.ds(0, 1)]) * scale).astype(BF16)
            row = pl.ds(pl.multiple_of((n * qb + b) * BLK, BLK), BLK)
            out_ref[0, row, :] = _dot(dsb, kcat, NN).astype(BF16)
            dks.append(_dot(dsb, q, TN))
        for b in range(qb):
            row = pl.ds(pl.multiple_of((n * qb + b) * BLK, BLK), BLK)
            if b + 1 < qb:
                dk_acc[row, :] = dks[b][BLK:] + dks[b + 1][:BLK]
                dv_acc[row, :] = dvs[b][BLK:] + dvs[b + 1][:BLK]
            else:
                dk_acc[row, :] = dks[b][BLK:]
                dv_acc[row, :] = dvs[b][BLK:]

        @pl.when(n > 0)
        def _():
            prow = pl.ds(pl.multiple_of((n * qb - 1) * BLK, BLK), BLK)
            dk_acc[prow, :] += dks[0][:BLK]
            dv_acc[prow, :] += dvs[0][:BLK]

        @pl.when(n == nb - 1)
        def _():
            out_ref[1] = dk_acc[...].astype(BF16)
            out_ref[2] = dv_acc[...].astype(BF16)

    oblk = pl.BlockSpec((qb * BLK, HEAD), lambda h, n: (n, h))
    return pl.pallas_call(
        body, name=name, grid=(NHG, nb),
        in_specs=_attn_specs(qb) + [oblk, oblk, oblk],
        out_specs=pl.BlockSpec((3, T, HEAD), lambda h, n: (0, 0, h)),
        out_shape=jax.ShapeDtypeStruct((3, T, AW), BF16),
        scratch_shapes=[pltpu.VMEM((T, HEAD), F32), pltpu.VMEM((T, HEAD), F32)],
        compiler_params=_cp(("parallel", "arbitrary")),
    )(qkv, qkv, qkv, qkv, qkv, _band_bias(gi), dob, lse, delta)


def _merge(outs, lses, name):
    tm = PERM_TM
    dils = [d for _, d in GROUPS]
    ng = len(dils)

    def body(*refs):
        in_refs = refs[:2 * ng]
        ab_ref = refs[2 * ng]
        lse_refs = refs[2 * ng + 1:3 * ng + 1]
        tile = refs[-1]

        def token_order(ref, dil):
            if dil == 1:
                return ref[...].astype(F32)
            _load_unperm(ref, tile, dil)
            return _get_tile(tile)

        os = [token_order(in_refs[2 * i], d) for i, d in enumerate(dils)]
        ls = [token_order(in_refs[2 * i + 1], d) for i, d in enumerate(dils)]
        mx = jnp.maximum(jnp.maximum(ls[0], ls[1]), ls[2])
        es = [jnp.exp(v - mx) for v in ls]
        tot = es[0] + es[1] + es[2]
        att = (es[0] / tot) * os[0] + (es[1] / tot) * os[1] + (es[2] / tot) * os[2]
        ab_ref[...] = att.astype(BF16)
        lse = mx + jnp.log(tot)
        _put_tile(tile, lse)
        for dil, ref in zip(dils, lse_refs):
            if dil == 1:
                ref[...] = lse
            else:
                _store_perm(ref, tile, dil)

    row = pl.BlockSpec((tm, AW), lambda i: (i, 0))
    specs = [row if d == 1 else _perm_spec(d, AW) for d in dils]
    args = []
    for d, o, l in zip(dils, outs, lses):
        args += [o, l] if d == 1 else [o.reshape(d, T // d, AW), l.reshape(d, T // d, AW)]
    out = pl.pallas_call(
        body, name=name, grid=(T // tm,),
        in_specs=[sp for sp in specs for _ in range(2)], out_specs=[row] + specs,
        out_shape=[jax.ShapeDtypeStruct((T, AW), BF16)]
        + [jax.ShapeDtypeStruct((T, AW), F32) if d == 1 else _perm_shape(d, AW, F32) for d in dils],
        scratch_shapes=[_tile_scratch(AW)],
        compiler_params=_cp(("parallel",)),
    )(*args)
    return out[0], [o.reshape(T, AW) for o in out[1:]]


def _mix_out(z3b, attnb, gates, wc, wa_t, wo, x1, name):
    tm = 512

    def body(z_ref, a_ref, g_ref, wc_ref, wa_ref, wo_ref, x_ref, xo_ref, yc_ref, ya_ref, mx_ref):
        yc = _dot(z_ref[...], wc_ref[...], NN)
        ya = _dot(a_ref[...], wa_ref[...], NT)
        yc_ref[...] = yc.astype(BF16)
        ya_ref[...] = ya.astype(BF16)
        gv = g_ref[...].astype(F32)
        mixed = (_sig(gv[:, :D]) * yc + _sig(gv[:, D:]) * ya).astype(BF16)
        mx_ref[...] = mixed
        xo_ref[...] = x_ref[...] + _dot(mixed, wo_ref[...], NN)

    row = pl.BlockSpec((tm, D), lambda i: (i, 0))
    return pl.pallas_call(
        body, name=name, grid=(T // tm,),
        in_specs=[row, pl.BlockSpec((tm, AW), lambda i: (i, 0)), pl.BlockSpec((tm, 2 * D), lambda i: (i, 0)),
                  _const_spec((D, D)), _const_spec((D, AW)), _const_spec((D, D)), row],
        out_specs=[row, row, row, row],
        out_shape=[jax.ShapeDtypeStruct((T, D), F32), jax.ShapeDtypeStruct((T, D), BF16),
                   jax.ShapeDtypeStruct((T, D), BF16), jax.ShapeDtypeStruct((T, D), BF16)],
        compiler_params=_cp(("parallel",)),
    )(z3b, attnb, gates, wc, wa_t, wo, x1)


def _mix_out_bwd(dx2, gates, yc, ya, attn, wc, wa_t, wo, name):
    tm = PERM_TM
    dils = [d for _, d in GROUPS]
    ng = len(dils)

    def body(dx_ref, g_ref, yc_ref, ya_ref, at_ref, wc_ref, wa_ref, wo_ref,
             dg_ref, dyc_ref, dya_ref, dxb_ref, dz3_ref, *rest):
        dat_refs, dl_refs, tile = rest[:ng], rest[ng:2 * ng], rest[-1]
        dxb = dx_ref[...].astype(BF16)
        dxb_ref[...] = dxb
        dmix = _dot(dxb, wo_ref[...], NT)
        gv = g_ref[...].astype(F32)
        sc = _sig(gv[:, :D])
        sa = _sig(gv[:, D:])
        ycv, yav = yc_ref[...].astype(F32), ya_ref[...].astype(F32)
        dg_ref[:, pl.ds(0, D)] = (dmix * ycv * sc * (1.0 - sc)).astype(BF16)
        dg_ref[:, pl.ds(D, D)] = (dmix * yav * sa * (1.0 - sa)).astype(BF16)
        dyc = (dmix * sc).astype(BF16)
        dya = (dmix * sa).astype(BF16)
        dyc_ref[...] = dyc
        dya_ref[...] = dya
        dz3_ref[...] = _dot(dyc, wc_ref[...], NT).astype(BF16)
        dat = _dot(dya, wa_ref[...], NN)
        prod = dat * at_ref[...].astype(F32)
        delta = jnp.concatenate(
            [jnp.broadcast_to(jnp.sum(prod[:, h * HEAD:(h + 1) * HEAD], axis=-1, keepdims=True), (tm, HEAD))
             for h in range(NHG)], axis=1)
        for value, out_refs in ((dat, dat_refs), (delta, dl_refs)):
            _put_tile(tile, value)
            for dil, ref in zip(dils, out_refs):
                if dil == 1:
                    ref[...] = value.astype(ref.dtype)
                else:
                    _store_perm(ref, tile, dil)

    row = pl.BlockSpec((tm, D), lambda i: (i, 0))
    row2 = pl.BlockSpec((tm, 2 * D), lambda i: (i, 0))
    rowa = pl.BlockSpec((tm, AW), lambda i: (i, 0))
    aspecs = [rowa if d == 1 else _perm_spec(d, AW) for d in dils]

    def ashapes(dtype):
        return [jax.ShapeDtypeStruct((T, AW), dtype) if d == 1 else _perm_shape(d, AW, dtype) for d in dils]

    out = pl.pallas_call(
        body, name=name, grid=(T // tm,),
        in_specs=[row, row2, row, row, rowa, _const_spec((D, D)), _const_spec((D, AW)), _const_spec((D, D))],
        out_specs=[row2, row, row, row, row] + aspecs + aspecs,
        out_shape=[jax.ShapeDtypeStruct((T, 2 * D), BF16), jax.ShapeDtypeStruct((T, D), BF16),
                   jax.ShapeDtypeStruct((T, D), BF16), jax.ShapeDtypeStruct((T, D), BF16),
                   jax.ShapeDtypeStruct((T, D), BF16)] + ashapes(BF16) + ashapes(F32),
        scratch_shapes=[_tile_scratch(AW)],
        compiler_params=_cp(("parallel",)),
    )(dx2, gates, yc, ya, attn, wc, wa_t, wo)
    dats = [o.reshape(T, AW) for o in out[5:5 + ng]]
    deltas = [o.reshape(T, AW) for o in out[5 + ng:5 + 2 * ng]]
    return out[0], out[1], out[2], out[3], out[4], dats, deltas


def _peer(k):
    x, y, c = lax.axis_index("x"), lax.axis_index("y"), lax.axis_index("c")
    px = 1 - x if k & 4 else x
    py = 1 - y if k & 2 else y
    pc = 1 - c if k & 1 else c
    return (px, py, pc), 4 * px + 2 * py + pc


HBM_SPEC = pl.BlockSpec(memory_space=pltpu.HBM)
SEM_SPEC = pl.BlockSpec(memory_space=pltpu.SEMAPHORE)
EFFECT = pltpu.SideEffectType.DATAFLOW_SIDE_EFFECTING


def _my_place():
    return 4 * lax.axis_index("x") + 2 * lax.axis_index("y") + lax.axis_index("c")


def _tie(a, order_after, name):
    na = len(order_after)

    def body(*refs):
        del refs

    return pl.pallas_call(
        body, name=name, in_specs=[pl.BlockSpec(memory_space=pl.ANY)] * (1 + na),
        out_specs=pl.BlockSpec(memory_space=pl.ANY), out_shape=jax.ShapeDtypeStruct(a.shape, a.dtype),
        input_output_aliases={0: 0},
    )(a, *order_after)


def _prep_gather(ws, order_after, name):
    me = jnp.reshape(_my_place(), (1,)).astype(jnp.int32)
    n = len(ws)
    na = len(order_after)
    shapes = [((32, wv.shape[1]), F32) if wv.shape[0] == CONV_W else (wv.shape, BF16) for wv in ws]

    def body(me_ref, *refs):
        del me_ref
        ins, outs = refs[:n], refs[n + na:]
        for wv, i_ref, o_ref in zip(ws, ins, outs):
            if wv.shape[0] == CONV_W:
                o_ref[pl.ds(0, CONV_W), :] = i_ref[...]
                o_ref[pl.ds(CONV_W, 1), :] = jnp.zeros((1, wv.shape[1]), F32)
            else:
                o_ref[...] = i_ref[...].astype(BF16)

    grid_spec = pltpu.PrefetchScalarGridSpec(
        num_scalar_prefetch=1, grid=(1,),
        in_specs=[pl.BlockSpec(wv.shape, lambda i, m: (0, 0)) for wv in ws]
        + [pl.BlockSpec(memory_space=pl.ANY)] * na,
        out_specs=[pl.BlockSpec(shp, lambda i, m: (m[0], 0)) for shp, _ in shapes])
    return pl.pallas_call(
        body, name=name, grid_spec=grid_spec,
        out_shape=[jax.ShapeDtypeStruct((NDEV * shp[0], shp[1]), dt) for shp, dt in shapes],
        compiler_params=_cp(("arbitrary",)),
    )(me, *ws, *order_after)


GATHER_A = ((1, 0), (2, 0), (4, 0), (6, 0))
GATHER_B = ((1, 2), (1, 4), (1, 6))


def _gather_start(lands, plan, order_after, name):
    n = len(lands)
    na = len(order_after)
    npl = len(plan)

    def body(*refs):
        land_refs = refs[:n]
        send, recv = refs[n + na], refs[n + na + 1]
        token = refs[-1]
        for w in range(n):
            rows = lands[w].shape[0] // NDEV
            for p, (k, j) in enumerate(plan):
                peer, _ = _peer(k)
                _, blk = _peer(j)
                part = land_refs[w].at[pl.ds(blk * rows, rows)]
                i = w * npl + p
                pltpu.make_async_remote_copy(src_ref=part, dst_ref=part, send_sem=send.at[i], recv_sem=recv.at[i],
                                             device_id=peer, device_id_type=MESH_ID).start()
        token[...] = jnp.zeros_like(token)

    nsem = n * npl
    bufs = [pltpu.with_memory_space_constraint(a, pltpu.HBM) for a in lands]
    out = pl.pallas_call(
        body, name=name,
        in_specs=[HBM_SPEC] * n + [pl.BlockSpec(memory_space=pl.ANY)] * na,
        out_specs=[SEM_SPEC, SEM_SPEC] + [HBM_SPEC] * n + [pl.BlockSpec(memory_space=pltpu.VMEM)],
        out_shape=[pltpu.SemaphoreType.DMA((nsem,)), pltpu.SemaphoreType.DMA((nsem,))]
        + [pltpu.HBM(a.shape, a.dtype) for a in bufs] + [jax.ShapeDtypeStruct((8, 128), F32)],
        input_output_aliases={i: 2 + i for i in range(n)},
        compiler_params=pltpu.CompilerParams(has_side_effects=EFFECT),
    )(*bufs, *order_after)
    return out[0], out[1], out[2:2 + n], out[-1]


def _gather_wait(started, plan, order_after, name):
    send, recv, lands, _ = started
    n = len(lands)
    na = len(order_after)
    npl = len(plan)

    def body(*refs):
        land_refs = refs[:n]
        send_ref, recv_ref = refs[n], refs[n + 1]
        for w in range(n):
            rows = lands[w].shape[0] // NDEV
            for p, (k, j) in enumerate(plan):
                peer, _ = _peer(k)
                _, blk = _peer(j)
                part = land_refs[w].at[pl.ds(blk * rows, rows)]
                i = w * npl + p
                cp = pltpu.make_async_remote_copy(src_ref=part, dst_ref=part, send_sem=send_ref.at[i],
                                                  recv_sem=recv_ref.at[i], device_id=peer, device_id_type=MESH_ID)
                cp.wait_send()
                cp.wait_recv()

    out = pl.pallas_call(
        body, name=name,
        in_specs=[HBM_SPEC] * n + [SEM_SPEC, SEM_SPEC] + [pl.BlockSpec(memory_space=pl.ANY)] * na,
        out_specs=[HBM_SPEC] * n,
        out_shape=[pltpu.HBM(a.shape, a.dtype) for a in lands],
        input_output_aliases={i: i for i in range(n)},
        compiler_params=pltpu.CompilerParams(has_side_effects=EFFECT),
    )(*lands, send, recv, *order_after)
    return list(out)


def _copy_ends(kind, src, land, me, plin, k):
    if kind == "scatter":
        rows = src.shape[0] // NDEV
        return src.at[pl.ds(plin * rows, rows)], land.at[k - 1]
    return src, land.at[me]


def _landing(kind, src):
    me = _my_place()
    if kind == "scatter":
        return lax.empty((NDEV - 1, src.shape[0] // NDEV) + src.shape[1:], src.dtype)
    land = lax.empty((NDEV,) + src.shape, src.dtype)
    return lax.dynamic_update_slice(land, src[None], (me,) + (0,) * src.ndim)


def _send_start(kinds, srcs, order_after, name):
    n = len(srcs)
    lands = [_landing(kd, s) for kd, s in zip(kinds, srcs)]
    na = len(order_after)

    def body(*refs):
        src_refs, land_refs = refs[:n], refs[n:2 * n]
        send, recv = refs[2 * n + na], refs[2 * n + na + 1]
        token = refs[-1]
        _, me = _peer(0)
        for w in range(n):
            for k in range(1, NDEV):
                peer, plin = _peer(k)
                s, d = _copy_ends(kinds[w], src_refs[w], land_refs[w], me, plin, k)
                i = w * (NDEV - 1) + k - 1
                pltpu.make_async_remote_copy(src_ref=s, dst_ref=d, send_sem=send.at[i], recv_sem=recv.at[i],
                                             device_id=peer, device_id_type=MESH_ID).start()
        token[...] = jnp.zeros_like(token)

    nsem = n * (NDEV - 1)
    bufs = [pltpu.with_memory_space_constraint(a, pltpu.HBM) for a in list(srcs) + lands]
    out = pl.pallas_call(
        body, name=name,
        in_specs=[HBM_SPEC] * (2 * n) + [pl.BlockSpec(memory_space=pl.ANY)] * na,
        out_specs=[SEM_SPEC, SEM_SPEC] + [HBM_SPEC] * (2 * n) + [pl.BlockSpec(memory_space=pltpu.VMEM)],
        out_shape=[pltpu.SemaphoreType.DMA((nsem,)), pltpu.SemaphoreType.DMA((nsem,))]
        + [pltpu.HBM(a.shape, a.dtype) for a in bufs] + [jax.ShapeDtypeStruct((8, 128), F32)],
        input_output_aliases={i: 2 + i for i in range(2 * n)},
        compiler_params=pltpu.CompilerParams(has_side_effects=EFFECT),
    )(*bufs, *order_after)
    return out[0], out[1], out[2:2 + n], out[2 + n:2 + 2 * n], out[-1]


def _send_wait(kinds, started, order_after, name):
    send, recv, srcs, lands, _ = started
    n = len(srcs)
    na = len(order_after)

    def body(*refs):
        src_refs, land_refs = refs[:n], refs[n:2 * n]
        send_ref, recv_ref = refs[2 * n], refs[2 * n + 1]
        _, me = _peer(0)
        for w in range(n):
            for k in range(1, NDEV):
                peer, plin = _peer(k)
                s, d = _copy_ends(kinds[w], src_refs[w], land_refs[w], me, plin, k)
                i = w * (NDEV - 1) + k - 1
                cp = pltpu.make_async_remote_copy(src_ref=s, dst_ref=d, send_sem=send_ref.at[i],
                                                  recv_sem=recv_ref.at[i], device_id=peer, device_id_type=MESH_ID)
                cp.wait_send()
                cp.wait_recv()

    bufs = list(srcs) + list(lands)
    out = pl.pallas_call(
        body, name=name,
        in_specs=[HBM_SPEC] * (2 * n) + [SEM_SPEC, SEM_SPEC] + [pl.BlockSpec(memory_space=pl.ANY)] * na,
        out_specs=[HBM_SPEC] * (2 * n),
        out_shape=[pltpu.HBM(a.shape, a.dtype) for a in bufs],
        input_output_aliases={i: i for i in range(2 * n)},
        compiler_params=pltpu.CompilerParams(has_side_effects=EFFECT),
    )(*bufs, send, recv, *order_after)
    return out[:n], out[n:]


def _gsum(own, land, name):
    rows, cols = own.shape
    tr = rows // 2 if rows * cols > 512 * 1024 and rows % 32 == 0 else rows

    def body(own_ref, l_ref, o_ref):
        tot = own_ref[...].astype(F32)
        for s in range(NDEV - 1):
            tot = tot + l_ref[s].astype(F32)
        o_ref[...] = tot

    return pl.pallas_call(
        body, name=name, grid=(rows // tr,),
        in_specs=[pl.BlockSpec((tr, cols), lambda i: (i, 0)),
                  pl.BlockSpec((NDEV - 1, tr, cols), lambda i: (0, i, 0))],
        out_specs=pl.BlockSpec((tr, cols), lambda i: (i, 0)),
        out_shape=jax.ShapeDtypeStruct((rows, cols), F32),
        compiler_params=_cp(("parallel",)),
    )(own, land)


def _adamw_math(w, g, m, v):
    m2 = B1 * m + (1.0 - B1) * g
    v2 = B2 * v + (1.0 - B2) * (g * g)
    m_hat = m2 / (1.0 - B1 ** STEP)
    v_hat = v2 / (1.0 - B2 ** STEP)
    delta = -LR * (m_hat / (jnp.sqrt(v_hat) + AEPS) + WD * w)
    return delta, m2, v2


def _adamw(w, g, m, v, name):
    rows, cols = w.shape
    tr = 256 if rows % 256 == 0 and rows > 256 else rows

    def body(w_ref, g_ref, m_ref, v_ref, d_ref, mo_ref, vo_ref):
        d, m2, v2 = _adamw_math(w_ref[...], g_ref[...], m_ref[...], v_ref[...])
        d_ref[...] = d
        mo_ref[...] = m2
        vo_ref[...] = v2

    blk = pl.BlockSpec((tr, cols), lambda i: (i, 0))
    return pl.pallas_call(
        body, name=name, grid=(rows // tr,), in_specs=[blk] * 4, out_specs=[blk] * 3,
        out_shape=[jax.ShapeDtypeStruct((rows, cols), F32)] * 3,
        compiler_params=_cp(("parallel",)),
    )(w, g, m, v)


UPD_TC = 256


def _update(src, land, w, m, v, name):
    rows, cols = land.shape[1:]
    tc = min(UPD_TC, cols)
    me = jnp.reshape(_my_place(), (1,)).astype(jnp.int32)

    def body(me_ref, own_ref, l_ref, w_ref, m_ref, v_ref, g_ref, d_ref, mo_ref, vo_ref):
        del me_ref
        g = own_ref[...].astype(F32)
        for s in range(NDEV - 1):
            g = g + l_ref[s].astype(F32)
        g_ref[...] = g
        d, m2, v2 = _adamw_math(w_ref[...], g, m_ref[...], v_ref[...])
        d_ref[...] = d
        mo_ref[...] = m2
        vo_ref[...] = v2

    wblk = pl.BlockSpec((rows, tc), lambda j, p: (0, j))
    grid_spec = pltpu.PrefetchScalarGridSpec(
        num_scalar_prefetch=1, grid=(cols // tc,),
        in_specs=[pl.BlockSpec((rows, tc), lambda j, p: (p[0], j)),
                  pl.BlockSpec((NDEV - 1, rows, tc), lambda j, p: (0, 0, j)), wblk, wblk, wblk],
        out_specs=[wblk] * 4)
    return pl.pallas_call(
        body, name=name, grid_spec=grid_spec, out_shape=[jax.ShapeDtypeStruct((rows, cols), F32)] * 4,
        compiler_params=_cp(("parallel",)),
    )(me, src, land, w, m, v)


def _small_update(vland, w8, m8, v8, name):
    def body(l_ref, w_ref, m_ref, v_ref, g_ref, d_ref, mo_ref, vo_ref):
        g = l_ref[0]
        for s in range(1, NDEV):
            g = g + l_ref[s]
        g_ref[...] = g
        d, m2, v2 = _adamw_math(w_ref[...], g, m_ref[...], v_ref[...])
        d_ref[...] = d
        mo_ref[...] = m2
        vo_ref[...] = v2

    return pl.pallas_call(
        body, name=name, out_shape=[jax.ShapeDtypeStruct((8, D), F32)] * 4,
        compiler_params=_cp(None),
    )(vland, w8, m8, v8)


def kernel(x, ffn1_norm, ffn1_w_gate, ffn1_w_up, ffn1_w_down, mix_norm, w_in, conv_dw_kernel, conv_dw_bias, conv_ln_gain, conv_ln_bias, conv_w_out, attn_w_out, w_o, ffn2_norm, ffn2_w_gate, ffn2_w_up, ffn2_w_down, final_norm, loss_target, m_ffn1_norm, m_ffn1_w_gate, m_ffn1_w_up, m_ffn1_w_down, m_mix_norm, m_w_in, m_conv_dw_kernel, m_conv_dw_bias, m_conv_ln_gain, m_conv_ln_bias, m_conv_w_out, m_attn_w_out, m_w_o, m_ffn2_norm, m_ffn2_w_gate, m_ffn2_w_up, m_ffn2_w_down, m_final_norm, v_ffn1_norm, v_ffn1_w_gate, v_ffn1_w_up, v_ffn1_w_down, v_mix_norm, v_w_in, v_conv_dw_kernel, v_conv_dw_bias, v_conv_ln_gain, v_conv_ln_bias, v_conv_w_out, v_attn_w_out, v_w_o, v_ffn2_norm, v_ffn2_w_gate, v_ffn2_w_up, v_ffn2_w_down, v_final_norm):
    names = ["ffn1_norm", "ffn1_w_gate", "ffn1_w_up", "ffn1_w_down", "mix_norm", "w_in", "conv_dw_kernel",
             "conv_dw_bias", "conv_ln_gain", "conv_ln_bias", "conv_w_out", "attn_w_out", "w_o", "ffn2_norm",
             "ffn2_w_gate", "ffn2_w_up", "ffn2_w_down", "final_norm"]
    w = dict(ffn1_norm=ffn1_norm, ffn1_w_gate=ffn1_w_gate, ffn1_w_up=ffn1_w_up, ffn1_w_down=ffn1_w_down, mix_norm=mix_norm, w_in=w_in, conv_dw_kernel=conv_dw_kernel, conv_dw_bias=conv_dw_bias, conv_ln_gain=conv_ln_gain, conv_ln_bias=conv_ln_bias, conv_w_out=conv_w_out, attn_w_out=attn_w_out, w_o=w_o, ffn2_norm=ffn2_norm, ffn2_w_gate=ffn2_w_gate, ffn2_w_up=ffn2_w_up, ffn2_w_down=ffn2_w_down, final_norm=final_norm)
    mo = dict(ffn1_norm=m_ffn1_norm, ffn1_w_gate=m_ffn1_w_gate, ffn1_w_up=m_ffn1_w_up, ffn1_w_down=m_ffn1_w_down, mix_norm=m_mix_norm, w_in=m_w_in, conv_dw_kernel=m_conv_dw_kernel, conv_dw_bias=m_conv_dw_bias, conv_ln_gain=m_conv_ln_gain, conv_ln_bias=m_conv_ln_bias, conv_w_out=m_conv_w_out, attn_w_out=m_attn_w_out, w_o=m_w_o, ffn2_norm=m_ffn2_norm, ffn2_w_gate=m_ffn2_w_gate, ffn2_w_up=m_ffn2_w_up, ffn2_w_down=m_ffn2_w_down, final_norm=m_final_norm)
    vo = dict(ffn1_norm=v_ffn1_norm, ffn1_w_gate=v_ffn1_w_gate, ffn1_w_up=v_ffn1_w_up, ffn1_w_down=v_ffn1_w_down, mix_norm=v_mix_norm, w_in=v_w_in, conv_dw_kernel=v_conv_dw_kernel, conv_dw_bias=v_conv_dw_bias, conv_ln_gain=v_conv_ln_gain, conv_ln_bias=v_conv_ln_bias, conv_w_out=v_conv_w_out, attn_w_out=v_attn_w_out, w_o=v_w_o, ffn2_norm=v_ffn2_norm, ffn2_w_gate=v_ffn2_w_gate, ffn2_w_up=v_ffn2_w_up, ffn2_w_down=v_ffn2_w_down, final_norm=v_final_norm)
    col_sharded = ("ffn1_w_gate", "ffn1_w_up", "w_in", "attn_w_out", "ffn2_w_gate", "ffn2_w_up")
    row_sharded = ("ffn1_w_down", "conv_w_out", "w_o", "ffn2_w_down")
    small = ("ffn1_norm", "mix_norm", "ffn2_norm", "final_norm", "conv_dw_bias", "conv_ln_gain", "conv_ln_bias")

    def landing_view(a, n):
        return jnp.transpose(a[0]) if n in col_sharded else a[0]

    def own_view(a, n):
        return jnp.transpose(a)[None] if n in col_sharded else a[None]

    ag_groups = (("ffn1_w_gate", "ffn1_w_up", "ffn1_w_down"),
                 ("w_in", "attn_w_out", "conv_w_out", "w_o", "conv_dw_kernel"),
                 ("ffn2_w_gate", "ffn2_w_up", "ffn2_w_down"))
    ag, order = [], []
    for gi, grp in enumerate(ag_groups):
        lands = _prep_gather([landing_view(w[n], n) for n in grp], order, f"gather_prep{gi}")
        st = _gather_start(lands, GATHER_A, [], f"gather_a_start{gi}")
        ag.append(st)
        order = [st[3]]

    def chips_in(gi, after):
        lands = _gather_wait(ag[gi], GATHER_A, after, f"gather_a_wait{gi}")
        return _gather_start(lands, GATHER_B, [], f"gather_b_start{gi}")

    def all_in(gi, st, after):
        return _gather_wait(st, GATHER_B, after, f"gather_b_wait{gi}")

    x0 = x[0]
    tgt = loss_target[0]
    gf = final_norm.reshape(1, D)

    wg1, wu1, wd1 = all_in(0, chips_in(0, [ag[2][3]]), [])
    x1, gg1, uu1, h2p = _ffn_fwd(x0, ffn1_norm, wg1, wu1, wd1, "ffn1_fwd", next_gain=mix_norm)
    h2 = h2p[0]
    win_t, wa_t, wc, wo, kern_blocks = all_in(1, chips_in(1, [x1]), [])
    kern = kern_blocks.reshape(NDEV, 32, D // NDEV).transpose(1, 0, 2).reshape(32, D)
    ptm = min(T, 2048)
    ab = _mm(h2, win_t, mode="nt", m=T, n=2 * D, k=D, tm=ptm, tn=512, tk=D, out_dtype=BF16, name="proj_conv")
    gates = _mm(h2, win_t, mode="nt", m=T, n=2 * D, k=D, tm=ptm, tn=512, tk=D, out_dtype=BF16,
                b_map=lambda i, j, kk: (13 + j, 0), name="proj_gates")
    qkv = []
    for gi in range(len(GROUPS)):
        qkv.append(_mm(h2p[gi], win_t, mode="nt", m=T, n=3 * AW, k=D, tm=ptm, tn=AW, tk=D, out_dtype=BF16,
                       b_map=lambda i, j, kk, gi=gi: (4 + gi + 3 * j, 0), name=f"proj_qkv{gi}"))
    z1, z3b = _conv_fwd(ab, kern, conv_dw_bias, conv_ln_gain, conv_ln_bias, "conv_fwd")
    ffn2_b = chips_in(2, [z3b])
    outs, lses = [], []
    for gi, (_, dil) in enumerate(GROUPS):
        o, l = _attn_fwd(qkv[gi], gi, f"attn_fwd{gi}")
        outs.append(o)
        lses.append(l)
    attnb, lse = _merge(outs, lses, "attn_merge")
    x2, yc, ya, mixedb = _mix_out(z3b, attnb, gates, wc, wa_t, wo, x1, "mix_out_fwd")
    wg2, wu2, wd2 = all_in(2, ffn2_b, [x2])
    gg2, uu2, dx3, dgf, loss_part = _ffn_fwd(x2, ffn2_norm, wg2, wu2, wd2, "ffn2_fwd", loss_of=(gf, tgt))

    dx2, dg3, dgb, dub, actb, hb, dob = _ffn_bwd(x2, ffn2_norm, gg2, uu2, dx3, wg2, wu2, wd2, "ffn2_bwd")
    grads = {}
    grads["ffn2_w_gate"] = _wgrad(dgb, hb, FF, D, "ffn2_dwg")
    grads["ffn2_w_up"] = _wgrad(dub, hb, FF, D, "ffn2_dwu")
    grads["ffn2_w_down"] = _wgrad(actb, dob, FF, D, "ffn2_dwd")
    rs_groups = [("ffn2_w_gate", "ffn2_w_up", "ffn2_w_down"),
                 ("attn_w_out", "conv_w_out", "w_o", "conv_dw_kernel"),
                 ("w_in",),
                 ("ffn1_w_gate",), ("ffn1_w_up",), ("ffn1_w_down",), ()]
    last = len(rs_groups) - 1
    rs = [_send_start(["scatter"] * 3, [grads[n] for n in rs_groups[0]], [], "scatter_start0")]
    dx2 = _tie(dx2, [rs[0][4]], "tie_after_scatter0")

    dgates, dycb, dyab, dx2b, dz3, dattnb, delta = _mix_out_bwd(dx2, gates, yc, ya, attnb, wc, wa_t, wo, "mix_out_bwd")
    grads["w_o"] = _wgrad(mixedb, dx2b, D, D, "dw_o")
    grads["conv_w_out"] = _wgrad(z3b, dycb, D, D, "dw_conv_out")
    grads["attn_w_out"] = _wgrad(dyab, attnb, D, AW, "dw_attn_out")
    dab, dkern, dvec = _conv_bwd(dz3, z1, ab, kern, conv_ln_gain, conv_ln_bias, "conv_bwd")
    grads["conv_dw_kernel"] = dkern.reshape(32, NDEV, D // NDEV).transpose(1, 0, 2).reshape(NDEV * 32, D // NDEV)
    rs.append(_send_start(["scatter"] * 4, [grads[n] for n in rs_groups[1]], [rs[0][4]], "scatter_start1"))
    dattnb = [_tie(a, [rs[1][4]], f"tie_after_scatter1_{i}") for i, a in enumerate(dattnb)]

    dqkv = []
    for gi, (_, dil) in enumerate(GROUPS):
        dq3 = _attn_bwd(qkv[gi], dattnb[gi], lse[gi], delta[gi], gi, f"attn_bwd{gi}")
        dqkv.append(dq3.reshape(3 * T, AW))

    wtk = min(T, 2048)
    dwin = _mm(dab, h2, mode="tn", m=2 * D, n=D, k=T, tm=2 * D, tn=D, tk=wtk, out_dtype=BF16, out_rows=IN_W,
               name="dw_in_conv")
    dwin = _mm(dgates, h2, mode="tn", m=2 * D, n=D, k=T, tm=512, tn=D, tk=wtk, out_dtype=BF16, out_rows=IN_W,
               o_map=lambda i, j, kk: (13 + i, 0), passthru=dwin, name="dw_in_gates")
    for gi in range(3):
        dwin = _mm(dqkv[gi], h2p[gi], mode="tn", m=3 * AW, n=D, k=T, tm=AW, tn=D, tk=wtk, out_dtype=BF16,
                   out_rows=IN_W, a_map=lambda i, j, kk: (i * (T // wtk) + kk, 0),
                   o_map=lambda i, j, kk, gi=gi: (4 + gi + 3 * i, 0), passthru=dwin, name=f"dw_in_qkv{gi}")
    grads["w_in"] = dwin
    rs.append(_send_start(["scatter"], [dwin], [rs[1][4]], "scatter_start2"))
    dab = _tie(dab, [rs[2][4]], "tie_after_scatter2")

    nrow = T // 1024
    dh = _mm(dab, win_t, mode="nn", m=T, n=D, k=2 * D, tm=1024, tn=D, tk=2 * D, out_dtype=F32, name="dproj_conv")
    dh = _mm(dgates, win_t[IN_W - 2 * D:], mode="nn", m=T, n=D, k=2 * D, tm=1024, tn=D, tk=2 * D, out_dtype=F32,
             init=dh, name="dproj_gates")
    dhs = []
    for gi, (_, dil) in enumerate(GROUPS):
        part = _mm(dqkv[gi], win_t, mode="nn", m=T, n=D, k=3 * AW, tm=1024, tn=D, tk=AW,
                   out_dtype=F32 if gi == 0 else BF16,
                   a_map=lambda i, j, kk: (kk * nrow + i, 0), b_map=lambda i, j, kk, gi=gi: (4 + gi + 3 * kk, 0),
                   init=dh if gi == 0 else None, name=f"dproj_qkv{gi}")
        dhs.append(part)
    dx1, dg2 = _rms_bwd(x1, mix_norm, dhs, dx2, "mix_norm_bwd")

    dgb, dub, actb, hb, dob = _ffn_bwd_pre(x0, ffn1_norm, gg1, uu1, dx1, wd1, "ffn1_bwd_pre")
    grads["ffn1_w_gate"] = _wgrad(dgb, hb, FF, D, "ffn1_dwg")
    rs.append(_send_start(["scatter"], [grads["ffn1_w_gate"]], [rs[2][4]], "scatter_start3"))
    hb = _tie(hb, [rs[3][4]], "tie_after_scatter3")
    grads["ffn1_w_up"] = _wgrad(dub, hb, FF, D, "ffn1_dwu")
    rs.append(_send_start(["scatter"], [grads["ffn1_w_up"]], [rs[3][4]], "scatter_start4"))
    dob = _tie(dob, [rs[4][4]], "tie_after_scatter4")
    grads["ffn1_w_down"] = _wgrad(actb, dob, FF, D, "ffn1_dwd")
    rs.append(_send_start(["scatter"], [grads["ffn1_w_down"]], [rs[4][4]], "scatter_start5"))
    dgb = _tie(dgb, [rs[5][4]], "tie_after_scatter5")
    dx0, dg1 = _ffn_bwd_dx(x0, ffn1_norm, dgb, dub, dx1, wg1, wu1, "ffn1_bwd_dx")
    vec = jnp.concatenate([dg1, dg2, dg3, dgf, dvec[0:3], jnp.broadcast_to(loss_part[:, :1], (1, D))], axis=0)
    rs.append(_send_start(["bcast"], [vec], [rs[5][4]], "scatter_start6"))

    g_out, d_out, m_out, v_out = {}, {}, {}, {}
    me = _my_place()
    after = [rs[last][4]]
    for gi, grp in enumerate(rs_groups):
        kinds = ["scatter"] * len(grp) + (["bcast"] if gi == last else [])
        srcs, lands = _send_wait(kinds, rs[gi], after, f"scatter_wait{gi}")
        for n, src, land in zip(grp, srcs, lands):
            if n == "conv_dw_kernel":
                rows = src.shape[0] // NDEV
                own = lax.dynamic_slice(src, (me * rows, 0), (rows, src.shape[1]))
                g = _gsum(own, land, f"gsum_{n}")[:CONV_W]
                d, m2, v2 = _adamw(w[n][0], g, mo[n][0], vo[n][0], f"adamw_{n}")
                after = [d]
                g, d, m2, v2 = g[None], d[None], m2[None], v2[None]
            else:
                res = _update(src, land, landing_view(w[n], n), landing_view(mo[n], n), landing_view(vo[n], n),
                              f"update_{n}")
                after = [res[1]]
                g, d, m2, v2 = (own_view(a, n) for a in res)
            g_out[n], d_out[n], m_out[n], v_out[n] = g, d, m2, v2
    vland = lands[-1]

    def rows8(src):
        return jnp.concatenate([src[n].reshape(1, D) for n in small] + [jnp.ones((1, D), F32)], axis=0)

    g8, d8, m8, v8 = _small_update(vland, rows8(w), rows8(mo), rows8(vo), "small_update")
    for r, n in enumerate(small):
        shp = w[n].shape
        g_out[n], d_out[n], m_out[n], v_out[n] = (a[r].reshape(shp) for a in (g8, d8, m8, v8))
    loss = g8[7, 0]

    return (loss, dx0[None], *[g_out[n] for n in names], *[d_out[n] for n in names],
            *[m_out[n] for n in names], *[v_out[n] for n in names])
```

```python
import numpy as np
import jax
import jax.numpy as jnp
from jax import lax
from jax.experimental import pallas as pl
from jax.experimental.pallas import tpu as pltpu

F32 = jnp.float32
BF16 = jnp.bfloat16

T = 4096
D = 1024
FF = 2816
NDEV = 8
CONV_W = 31
HEAD = 128
BLK = 128
GROUPS = ((128, 1), (512, 4), (2048, 16))
NHG = 4
AW = NHG * HEAD
IN_W = 2 * D + 3 * 3 * AW + 2 * D
EPS = 1e-6
B1, B2, LR, AEPS, WD, STEP = 0.9, 0.999, 0.001, 1e-08, 0.01, 10
NEG = -1e30
VMEM_LIMIT = 56 * 1024 * 1024
MESH_ID = pl.DeviceIdType.MESH

NT = (((1,), (1,)), ((), ()))
NN = (((1,), (0,)), ((), ()))
TN = (((0,), (0,)), ((), ()))
_DIMS = {"nn": NN, "nt": NT, "tn": TN}


def _cp(sem=None):
    return pltpu.CompilerParams(dimension_semantics=sem, vmem_limit_bytes=VMEM_LIMIT)


def _sig(v):
    return 1.0 / (1.0 + jnp.exp(-v))


def _dot(a, b, dims):
    return lax.dot_general(a, b, dims, preferred_element_type=F32)


def _const_spec(shape):
    nd = len(shape)
    return pl.BlockSpec(shape, lambda *_: (0,) * nd)


def _mm(a, b, *, mode, m, n, k, tm, tn, tk, out_dtype, name, a_map=None, b_map=None,
        o_map=None, out_rows=None, init=None, passthru=None):
    gi, gj, gk = m // tm, n // tn, k // tk
    assert gi * tm == m and gj * tn == n and gk * tk == k, (name, m, n, k, tm, tn, tk)
    if mode == "nn":
        a_blk, b_blk = (tm, tk), (tk, tn)
        da, db = (lambda i, j, kk: (i, kk)), (lambda i, j, kk: (kk, j))
    elif mode == "nt":
        a_blk, b_blk = (tm, tk), (tn, tk)
        da, db = (lambda i, j, kk: (i, kk)), (lambda i, j, kk: (j, kk))
    else:
        a_blk, b_blk = (tk, tm), (tk, tn)
        da, db = (lambda i, j, kk: (kk, i)), (lambda i, j, kk: (kk, j))
    a_map = a_map or da
    b_map = b_map or db
    o_map = o_map or (lambda i, j, kk: (i, j))
    dims = _DIMS[mode]
    extra = init if init is not None else passthru
    out_rows = out_rows or m

    def body(*refs):
        if init is not None:
            a_ref, b_ref, i_ref, o_ref = refs[:4]
        elif passthru is not None:
            a_ref, b_ref, _, o_ref = refs[:4]
        else:
            a_ref, b_ref, o_ref = refs[:3]
        if gk == 1:
            prod = _dot(a_ref[...], b_ref[...], dims)
            if init is not None:
                prod = prod + i_ref[...].astype(F32)
            o_ref[...] = prod.astype(out_dtype)
            return
        acc = refs[-1]
        kk = pl.program_id(2)

        @pl.when(kk == 0)
        def _():
            if init is not None:
                acc[...] = i_ref[...].astype(F32)
            else:
                acc[...] = jnp.zeros_like(acc)

        acc[...] += _dot(a_ref[...], b_ref[...], dims)

        @pl.when(kk == gk - 1)
        def _():
            o_ref[...] = acc[...].astype(out_dtype)

    in_specs = [pl.BlockSpec(a_blk, a_map), pl.BlockSpec(b_blk, b_map)]
    args = [a, b]
    aliases = {}
    if init is not None:
        in_specs.append(pl.BlockSpec((tm, tn), o_map))
        args.append(init)
        aliases = {2: 0}
    elif passthru is not None:
        in_specs.append(pl.BlockSpec(memory_space=pl.ANY))
        args.append(passthru)
        aliases = {2: 0}
    out_dt = extra.dtype if extra is not None else out_dtype
    assert out_dt == out_dtype
    return pl.pallas_call(
        body, name=name, grid=(gi, gj, gk),
        in_specs=in_specs, out_specs=pl.BlockSpec((tm, tn), o_map),
        out_shape=jax.ShapeDtypeStruct((out_rows, n), out_dtype),
        scratch_shapes=[pltpu.VMEM((tm, tn), F32)] if gk > 1 else [],
        input_output_aliases=aliases,
        compiler_params=_cp(("parallel", "parallel", "arbitrary")),
    )(*args)


def _ffn_fwd(x, g, wg_t, wu_t, wd, name, next_gain=None, loss_of=None):
    tm, fc = PERM_TM, 256
    nc = FF // fc
    n_in = 5 + (1 if next_gain is not None else 0) + (2 if loss_of is not None else 0)

    def body(*refs):
        x_ref, g_ref, wg_ref, wu_ref, wd_ref = refs[:5]
        extra_in, outs = refs[5:n_in], refs[n_in:]
        act_ref = outs[-1]
        xv = x_ref[...]
        r = lax.rsqrt(jnp.mean(xv * xv, axis=-1, keepdims=True) + EPS)
        h = (xv * r * g_ref[...]).astype(BF16)
        gg_ref, uu_ref = (outs[0], outs[1]) if loss_of is not None else (outs[1], outs[2])
        for c in range(nc):
            sl = pl.ds(c * fc, fc)
            gg = _dot(h, wg_ref[sl, :], NT)
            uu = _dot(h, wu_ref[sl, :], NT)
            gg_ref[:, sl] = gg.astype(BF16)
            uu_ref[:, sl] = uu.astype(BF16)
            act_ref[:, sl] = (gg * _sig(gg) * uu).astype(BF16)
        y = xv + 0.5 * _dot(act_ref[...], wd_ref[...], NN)
        if loss_of is not None:
            _final_math(y, extra_in[0][...], extra_in[1][...], outs[2], outs[3], outs[4], pl.program_id(0))
            return
        outs[0][...] = y
        if next_gain is not None:
            tile = outs[-2]
            r2 = lax.rsqrt(jnp.mean(y * y, axis=-1, keepdims=True) + EPS)
            hv = y * r2 * extra_in[0][...]
            outs[3][...] = hv.astype(BF16)
            _put_tile(tile, hv)
            for dil, p_ref in zip(DILS, outs[4:4 + len(DILS)]):
                _store_perm(p_ref, tile, dil)

    wspec = pl.BlockSpec((FF, D), lambda i: (0, 0), pipeline_mode=pl.Buffered(1))
    row_d = pl.BlockSpec((tm, D), lambda i: (i, 0))
    row_f = pl.BlockSpec((tm, FF), lambda i: (i, 0))
    in_specs = [row_d, _const_spec((1, D)), wspec, wspec, wspec]
    args = [x, g, wg_t, wu_t, wd]
    f_shape = jax.ShapeDtypeStruct((T, FF), BF16)
    scratch = [pltpu.VMEM((tm, FF), BF16)]
    if loss_of is not None:
        in_specs += [_const_spec((1, D)), row_d]
        args += list(loss_of)
        out_specs = [row_f, row_f, row_d, _const_spec((1, D)), _const_spec((1, 128))]
        out_shape = [f_shape, f_shape, jax.ShapeDtypeStruct((T, D), F32), jax.ShapeDtypeStruct((1, D), F32),
                     jax.ShapeDtypeStruct((1, 128), F32)]
    else:
        out_specs = [row_d, row_f, row_f]
        out_shape = [jax.ShapeDtypeStruct((T, D), F32), f_shape, f_shape]
        if next_gain is not None:
            in_specs.append(_const_spec((1, D)))
            args.append(next_gain)
            out_specs += [row_d] + [_perm_spec(d, D) for d in DILS]
            out_shape += [jax.ShapeDtypeStruct((T, D), BF16)] + [_perm_shape(d, D, BF16) for d in DILS]
            scratch = [_tile_scratch(D)] + scratch
    out = pl.pallas_call(
        body, name=name, grid=(T // tm,), in_specs=in_specs, out_specs=out_specs, out_shape=out_shape,
        scratch_shapes=scratch,
        compiler_params=_cp(("arbitrary",) if loss_of is not None else ("parallel",)),
    )(*args)
    if next_gain is not None:
        return out[0], out[1], out[2], [out[3]] + [o.reshape(T, D) for o in out[4:]]
    return tuple(out)


def _ffn_bwd(x, g, gg_all, uu_all, dout, wg_t, wu_t, wd, name):
    tm, fc = 256, 256
    nc = FF // fc

    def body(x_ref, g_ref, gg_ref, uu_ref, do_ref, wg_ref, wu_ref, wd_ref,
             dx_ref, dgam_ref, dg_ref, du_ref, act_ref, h_ref, db_ref):
        i = pl.program_id(0)
        xv = x_ref[...]
        r = lax.rsqrt(jnp.mean(xv * xv, axis=-1, keepdims=True) + EPS)
        xhat = xv * r
        gam = g_ref[...]
        h_ref[...] = (xhat * gam).astype(BF16)
        dov = do_ref[...]
        dbv = (0.5 * dov).astype(BF16)
        db_ref[...] = dbv
        for c in range(nc):
            sl = pl.ds(c * fc, fc)
            da = _dot(dbv, wd_ref[sl, :], NT)
            gg = gg_ref[:, sl].astype(F32)
            uu = uu_ref[:, sl].astype(F32)
            s = _sig(gg)
            si = gg * s
            dgv = (da * uu * (s * (1.0 + gg * (1.0 - s)))).astype(BF16)
            duv = (da * si).astype(BF16)
            dg_ref[:, sl] = dgv
            du_ref[:, sl] = duv
            act_ref[:, sl] = (si * uu).astype(BF16)
        dh = _dot(dg_ref[...], wg_ref[...], NN) + _dot(du_ref[...], wu_ref[...], NN)

        @pl.when(i == 0)
        def _():
            dgam_ref[...] = jnp.zeros_like(dgam_ref)

        dgam_ref[...] += jnp.sum(dh * xhat, axis=0, keepdims=True)
        dxh = dh * gam
        dx_ref[...] = dov + r * (dxh - xhat * jnp.mean(dxh * xhat, axis=-1, keepdims=True))

    wspec = pl.BlockSpec((FF, D), lambda i: (0, 0), pipeline_mode=pl.Buffered(1))
    row_d = pl.BlockSpec((tm, D), lambda i: (i, 0))
    row_f = pl.BlockSpec((tm, FF), lambda i: (i, 0))
    return pl.pallas_call(
        body, name=name, grid=(T // tm,),
        in_specs=[row_d, _const_spec((1, D)), row_f, row_f, row_d, wspec, wspec, wspec],
        out_specs=[row_d, _const_spec((1, D)), row_f, row_f, row_f, row_d, row_d],
        out_shape=[jax.ShapeDtypeStruct((T, D), F32), jax.ShapeDtypeStruct((1, D), F32),
                   jax.ShapeDtypeStruct((T, FF), BF16), jax.ShapeDtypeStruct((T, FF), BF16),
                   jax.ShapeDtypeStruct((T, FF), BF16), jax.ShapeDtypeStruct((T, D), BF16),
                   jax.ShapeDtypeStruct((T, D), BF16)],
        compiler_params=_cp(("arbitrary",)),
    )(x, g, gg_all, uu_all, dout, wg_t, wu_t, wd)


def _ffn_bwd_pre(x, g, gg_all, uu_all, dout, wd, name):
    tm, fc = 512, 256
    nc = FF // fc

    def body(x_ref, g_ref, gg_ref, uu_ref, do_ref, wd_ref, dg_ref, du_ref, act_ref, h_ref, db_ref):
        xv = x_ref[...]
        r = lax.rsqrt(jnp.mean(xv * xv, axis=-1, keepdims=True) + EPS)
        h_ref[...] = (xv * r * g_ref[...]).astype(BF16)
        dbv = (0.5 * do_ref[...]).astype(BF16)
        db_ref[...] = dbv
        for c in range(nc):
            sl = pl.ds(c * fc, fc)
            da = _dot(dbv, wd_ref[sl, :], NT)
            gg = gg_ref[:, sl].astype(F32)
            uu = uu_ref[:, sl].astype(F32)
            s = _sig(gg)
            si = gg * s
            dg_ref[:, sl] = (da * uu * (s * (1.0 + gg * (1.0 - s)))).astype(BF16)
            du_ref[:, sl] = (da * si).astype(BF16)
            act_ref[:, sl] = (si * uu).astype(BF16)

    wspec = pl.BlockSpec((FF, D), lambda i: (0, 0), pipeline_mode=pl.Buffered(1))
    row_d = pl.BlockSpec((tm, D), lambda i: (i, 0))
    row_f = pl.BlockSpec((tm, FF), lambda i: (i, 0))
    return pl.pallas_call(
        body, name=name, grid=(T // tm,),
        in_specs=[row_d, _const_spec((1, D)), row_f, row_f, row_d, wspec],
        out_specs=[row_f, row_f, row_f, row_d, row_d],
        out_shape=[jax.ShapeDtypeStruct((T, FF), BF16), jax.ShapeDtypeStruct((T, FF), BF16),
                   jax.ShapeDtypeStruct((T, FF), BF16), jax.ShapeDtypeStruct((T, D), BF16),
                   jax.ShapeDtypeStruct((T, D), BF16)],
        compiler_params=_cp(("parallel",)),
    )(x, g, gg_all, uu_all, dout, wd)


def _ffn_bwd_dx(x, g, dgb, dub, dout, wg_t, wu_t, name):
    tm = 512

    def body(x_ref, g_ref, dg_ref, du_ref, do_ref, wg_ref, wu_ref, dx_ref, dgam_ref):
        i = pl.program_id(0)
        xv = x_ref[...]
        r = lax.rsqrt(jnp.mean(xv * xv, axis=-1, keepdims=True) + EPS)
        xhat = xv * r
        gam = g_ref[...]
        dh = _dot(dg_ref[...], wg_ref[...], NN) + _dot(du_ref[...], wu_ref[...], NN)

        @pl.when(i == 0)
        def _():
            dgam_ref[...] = jnp.zeros_like(dgam_ref)

        dgam_ref[...] += jnp.sum(dh * xhat, axis=0, keepdims=True)
        dxh = dh * gam
        dx_ref[...] = do_ref[...] + r * (dxh - xhat * jnp.mean(dxh * xhat, axis=-1, keepdims=True))

    wspec = pl.BlockSpec((FF, D), lambda i: (0, 0), pipeline_mode=pl.Buffered(1))
    row_d = pl.BlockSpec((tm, D), lambda i: (i, 0))
    row_f = pl.BlockSpec((tm, FF), lambda i: (i, 0))
    return pl.pallas_call(
        body, name=name, grid=(T // tm,),
        in_specs=[row_d, _const_spec((1, D)), row_f, row_f, row_d, wspec, wspec],
        out_specs=[row_d, _const_spec((1, D))],
        out_shape=[jax.ShapeDtypeStruct((T, D), F32), jax.ShapeDtypeStruct((1, D), F32)],
        compiler_params=_cp(("arbitrary",)),
    )(x, g, dgb, dub, dout, wg_t, wu_t)


def _wgrad(a, b, m, n, name):
    tm = m // 2 if m == FF else m
    return _mm(a, b, mode="tn", m=m, n=n, k=T, tm=tm, tn=n, tk=min(T, 2048), out_dtype=BF16, name=name)


PERM_TM = 512
DILS = tuple(d for _, d in GROUPS if d > 1)


def _perm_spec(dil, cols):
    return pl.BlockSpec((dil, PERM_TM // dil, cols), lambda i: (0, i, 0))


def _perm_shape(dil, cols, dtype):
    return jax.ShapeDtypeStruct((dil, T // dil, cols), dtype)


LANES = 128


def _tile_scratch(cols):
    return pltpu.VMEM((cols // LANES, PERM_TM, LANES), F32)


def _put_tile(tile, value):
    for c in range(tile.shape[0]):
        tile[c] = value[:, c * LANES:(c + 1) * LANES]


def _get_tile(tile):
    return jnp.concatenate([tile[c] for c in range(tile.shape[0])], axis=1)


def _store_perm(out_ref, tile, dil):
    for r in range(dil):
        for c in range(tile.shape[0]):
            out_ref[r, :, pl.ds(c * LANES, LANES)] = tile[c, pl.ds(r, PERM_TM // dil, stride=dil), :].astype(
                out_ref.dtype)


def _load_unperm(in_ref, tile, dil):
    for r in range(dil):
        for c in range(tile.shape[0]):
            tile[c, pl.ds(r, PERM_TM // dil, stride=dil), :] = in_ref[r, :, pl.ds(c * LANES, LANES)].astype(F32)


def _norm_cast(x, g, name):
    tm = PERM_TM

    def body(x_ref, g_ref, h_ref, *rest):
        p_refs, tile = rest[:-1], rest[-1]
        xv = x_ref[...]
        r = lax.rsqrt(jnp.mean(xv * xv, axis=-1, keepdims=True) + EPS)
        hv = xv * r * g_ref[...]
        h_ref[...] = hv.astype(BF16)
        _put_tile(tile, hv)
        for dil, p_ref in zip(DILS, p_refs):
            _store_perm(p_ref, tile, dil)

    out = pl.pallas_call(
        body, name=name, grid=(T // tm,),
        in_specs=[pl.BlockSpec((tm, D), lambda i: (i, 0)), _const_spec((1, D))],
        out_specs=[pl.BlockSpec((tm, D), lambda i: (i, 0))] + [_perm_spec(d, D) for d in DILS],
        out_shape=[jax.ShapeDtypeStruct((T, D), BF16)] + [_perm_shape(d, D, BF16) for d in DILS],
        scratch_shapes=[_tile_scratch(D)],
        compiler_params=_cp(("parallel",)),
    )(x, g)
    return [out[0]] + [o.reshape(T, D) for o in out[1:]]


def _final_math(xv, gam, tgt, dx_ref, dgam_ref, loss_ref, i):
    r = lax.rsqrt(jnp.mean(xv * xv, axis=-1, keepdims=True) + EPS)
    xhat = xv * r
    err = xhat * gam - tgt
    part = 0.5 * jnp.sum(jnp.mean(err * err, axis=-1, keepdims=True), axis=0, keepdims=True)
    dy = err * (1.0 / D)

    @pl.when(i == 0)
    def _():
        dgam_ref[...] = jnp.zeros_like(dgam_ref)
        loss_ref[...] = jnp.zeros_like(loss_ref)

    dgam_ref[...] += jnp.sum(dy * xhat, axis=0, keepdims=True)
    loss_ref[...] += jnp.broadcast_to(part, loss_ref.shape)
    dxh = dy * gam
    dx_ref[...] = r * (dxh - xhat * jnp.mean(dxh * xhat, axis=-1, keepdims=True))


def _rms_bwd(x, g, dhs, dres, name):
    tm = PERM_TM
    dils = [d for _, d in GROUPS]
    nh = len(dhs)
    assert nh == len(dils)

    def body(*refs):
        x_ref, g_ref = refs[:2]
        dh_refs = refs[2:2 + nh]
        dr_ref, dx_ref, dgam_ref, tile = refs[2 + nh:]
        i = pl.program_id(0)
        xv = x_ref[...]
        r = lax.rsqrt(jnp.mean(xv * xv, axis=-1, keepdims=True) + EPS)
        xhat = xv * r
        gam = g_ref[...]
        dh = None
        for dil, ref in zip(dils, dh_refs):
            if dil == 1:
                part = ref[...]
            else:
                _load_unperm(ref, tile, dil)
                part = _get_tile(tile)
            dh = part if dh is None else dh + part

        @pl.when(i == 0)
        def _():
            dgam_ref[...] = jnp.zeros_like(dgam_ref)

        dgam_ref[...] += jnp.sum(dh * xhat, axis=0, keepdims=True)
        dxh = dh * gam
        dx_ref[...] = dr_ref[...] + r * (dxh - xhat * jnp.mean(dxh * xhat, axis=-1, keepdims=True))

    row_d = pl.BlockSpec((tm, D), lambda i: (i, 0))
    dh_specs = [row_d if d == 1 else _perm_spec(d, D) for d in dils]
    dh_args = [a if d == 1 else a.reshape(d, T // d, D) for d, a in zip(dils, dhs)]
    return pl.pallas_call(
        body, name=name, grid=(T // tm,),
        in_specs=[row_d, _const_spec((1, D))] + dh_specs + [row_d],
        out_specs=[row_d, _const_spec((1, D))],
        out_shape=[jax.ShapeDtypeStruct((T, D), F32), jax.ShapeDtypeStruct((1, D), F32)],
        scratch_shapes=[_tile_scratch(D)],
        compiler_params=_cp(("arbitrary",)),
    )(x, g, *dh_args, dres)


CONV_TM = 256
CONV_HALO = 32
CONV_RB = 16


def _glu(ab):
    ab = ab.astype(F32)
    return ab[:, :D] * _sig(ab[:, D:])


def _ln_stats(z1):
    mu = jnp.mean(z1, axis=-1, keepdims=True)
    zc = z1 - mu
    rstd = lax.rsqrt(jnp.mean(zc * zc, axis=-1, keepdims=True) + EPS)
    return zc * rstd, rstd


def _fill_shifts(zs):
    n = zs.shape[1] - 8
    for s in range(1, 8):
        zs[s, pl.ds(0, n), :] = zs[0, pl.ds(s, n), :]


def _shifted(zs, start, rows):
    q, s = divmod(start, 8)
    return zs[s, pl.ds(8 * q, rows), :]


def _conv_fwd(ab, kern, dwb, lng, lnb, name):
    tm, hl, rb = CONV_TM, CONV_HALO, CONV_RB
    off = hl - (CONV_W - 1)

    def body(ab_ref, abh_ref, k_ref, dwb_ref, lng_ref, lnb_ref, z1_ref, z3_ref, zs):
        i = pl.program_id(0)
        zs[0, pl.ds(0, hl), :] = jnp.where(i > 0, _glu(abh_ref[...]), 0.0)
        zs[0, pl.ds(hl, tm), :] = _glu(ab_ref[...])
        _fill_shifts(zs)
        for b in range(tm // rb):
            acc = jnp.zeros((rb, D), F32)
            for j in range(CONV_W):
                acc = acc + _shifted(zs, b * rb + off + j, rb) * k_ref[pl.ds(j, 1), :]
            z1 = acc + dwb_ref[...]
            z1_ref[pl.ds(b * rb, rb), :] = z1
            zn, _ = _ln_stats(z1)
            z2 = zn * lng_ref[...] + lnb_ref[...]
            z3_ref[pl.ds(b * rb, rb), :] = (z2 * _sig(z2)).astype(BF16)

    row = pl.BlockSpec((tm, D), lambda i: (i, 0))
    return pl.pallas_call(
        body, name=name, grid=(T // tm,),
        in_specs=[pl.BlockSpec((tm, 2 * D), lambda i: (i, 0)),
                  pl.BlockSpec((hl, 2 * D), lambda i: (jnp.maximum(i * (tm // hl) - 1, 0), 0)),
                  _const_spec((32, D)), _const_spec((1, D)), _const_spec((1, D)), _const_spec((1, D))],
        out_specs=[row, row],
        out_shape=[jax.ShapeDtypeStruct((T, D), F32), jax.ShapeDtypeStruct((T, D), BF16)],
        scratch_shapes=[pltpu.VMEM((8, hl + tm, D), F32)],
        compiler_params=_cp(("parallel",)),
    )(ab, ab, kern, dwb, lng, lnb)


GUEST_TM = 256


def _conv_bwd(dz3, z1, ab, kern, lng, lnb, name, guest_lhs=(), guest_rhs=None):
    tm, hl, rb = CONV_TM, CONV_HALO, CONV_RB
    off = hl - (CONV_W - 1)
    nsteps = T // tm
    ng = len(guest_lhs)
    gblocks = [a.shape[1] // GUEST_TM for a in guest_lhs]
    assert all(gb <= nsteps and gb * GUEST_TM == a.shape[1] for gb, a in zip(gblocks, guest_lhs))

    def ln_bwd(dz3v, z1v, lngv, lnbv):
        zn, rstd = _ln_stats(z1v)
        z2 = zn * lngv + lnbv
        s = _sig(z2)
        dz2 = dz3v * (s * (1.0 + z2 * (1.0 - s)))
        dzn = dz2 * lngv
        dz1 = rstd * (dzn - jnp.mean(dzn, axis=-1, keepdims=True)
                      - zn * jnp.mean(dzn * zn, axis=-1, keepdims=True))
        return dz1, dz2, zn

    def body(dz3_ref, dz3h_ref, z1_ref, z1h_ref, ab_ref, abh_ref, k_ref, lng_ref, lnb_ref, *rest):
        g_in, rest = rest[:ng + (1 if ng else 0)], rest[ng + (1 if ng else 0):]
        dab_ref, dk_ref, dvec_ref = rest[:3]
        g_out, (zs, dzs) = rest[3:3 + ng], rest[3 + ng:]
        i = pl.program_id(0)
        lngv, lnbv = lng_ref[...], lnb_ref[...]

        for a_ref, o_ref, gb in zip(g_in[:ng], g_out, gblocks):
            @pl.when(i < gb)
            def _(a_ref=a_ref, o_ref=o_ref):
                o_ref[...] = _dot(a_ref[...], g_in[ng][...], TN).astype(BF16)

        @pl.when(i == 0)
        def _():
            dk_ref[...] = jnp.zeros_like(dk_ref)
            dvec_ref[...] = jnp.zeros_like(dvec_ref)

        dz1, dz2, zn = ln_bwd(dz3_ref[...].astype(F32), z1_ref[...], lngv, lnbv)
        dvec_ref[pl.ds(0, 1), :] += jnp.sum(dz1, axis=0, keepdims=True)
        dvec_ref[pl.ds(1, 1), :] += jnp.sum(dz2 * zn, axis=0, keepdims=True)
        dvec_ref[pl.ds(2, 1), :] += jnp.sum(dz2, axis=0, keepdims=True)
        dzs[0, pl.ds(0, tm), :] = dz1
        dz1h, _, _ = ln_bwd(dz3h_ref[...].astype(F32), z1h_ref[...], lngv, lnbv)
        dzs[0, pl.ds(tm, hl), :] = jnp.where(i < nsteps - 1, dz1h, 0.0)
        _fill_shifts(dzs)
        zs[0, pl.ds(0, hl), :] = jnp.where(i > 0, _glu(abh_ref[...]), 0.0)
        zs[0, pl.ds(hl, tm), :] = _glu(ab_ref[...])
        _fill_shifts(zs)

        for j in range(CONV_W):
            tot = jnp.zeros((rb, D), F32)
            for b in range(tm // rb):
                tot = tot + dzs[0, pl.ds(b * rb, rb), :] * _shifted(zs, b * rb + off + j, rb)
            dk_ref[pl.ds(j, 1), :] += jnp.sum(tot, axis=0, keepdims=True)

        for b in range(tm // rb):
            acc = jnp.zeros((rb, D), F32)
            for j in range(CONV_W):
                acc = acc + _shifted(dzs, b * rb + (CONV_W - 1) - j, rb) * k_ref[pl.ds(j, 1), :]
            av = ab_ref[pl.ds(b * rb, rb), pl.ds(0, D)].astype(F32)
            sb = _sig(ab_ref[pl.ds(b * rb, rb), pl.ds(D, D)].astype(F32))
            dab_ref[pl.ds(b * rb, rb), pl.ds(0, D)] = (acc * sb).astype(BF16)
            dab_ref[pl.ds(b * rb, rb), pl.ds(D, D)] = (acc * av * sb * (1.0 - sb)).astype(BF16)

    row = pl.BlockSpec((tm, D), lambda i: (i, 0))
    nxt = pl.BlockSpec((hl, D), lambda i: (jnp.minimum((i + 1) * (tm // hl), T // hl - 1), 0))
    g_specs, g_args, g_ospecs, g_oshapes = [], [], [], []
    for a, gb in zip(guest_lhs, gblocks):
        g_specs.append(pl.BlockSpec((T, GUEST_TM), lambda i, gb=gb: (0, jnp.minimum(i, gb - 1))))
        g_args.append(a)
        g_ospecs.append(pl.BlockSpec((GUEST_TM, guest_rhs.shape[1]), lambda i, gb=gb: (jnp.minimum(i, gb - 1), 0)))
        g_oshapes.append(jax.ShapeDtypeStruct((a.shape[1], guest_rhs.shape[1]), BF16))
    if ng:
        g_specs.append(pl.BlockSpec(guest_rhs.shape, lambda i: (0, 0), pipeline_mode=pl.Buffered(1)))
        g_args.append(guest_rhs)
    return pl.pallas_call(
        body, name=name, grid=(nsteps,),
        in_specs=[row, nxt, row, nxt,
                  pl.BlockSpec((tm, 2 * D), lambda i: (i, 0)),
                  pl.BlockSpec((hl, 2 * D), lambda i: (jnp.maximum(i * (tm // hl) - 1, 0), 0)),
                  _const_spec((32, D)), _const_spec((1, D)), _const_spec((1, D))] + g_specs,
        out_specs=[pl.BlockSpec((tm, 2 * D), lambda i: (i, 0)), _const_spec((32, D)), _const_spec((8, D))]
        + g_ospecs,
        out_shape=[jax.ShapeDtypeStruct((T, 2 * D), BF16), jax.ShapeDtypeStruct((32, D), F32),
                   jax.ShapeDtypeStruct((8, D), F32)] + g_oshapes,
        scratch_shapes=[pltpu.VMEM((8, hl + tm, D), F32), pltpu.VMEM((8, tm + hl, D), F32)],
        compiler_params=_cp(("arbitrary",)),
    )(dz3, dz3, z1, z1, ab, ab, kern, lng, lnb, *g_args)


def _alibi_slopes():
    h = np.arange(1, 3 * NHG + 1, dtype=np.float32)
    return np.power(np.float32(2.0), -8.0 * h / np.float32(3 * NHG)).astype(np.float32)


def _band_bias(gi):
    _, dil = GROUPS[gi]
    slopes = _alibi_slopes()[gi * NHG:(gi + 1) * NHG]
    qi = np.arange(BLK)[:, None]
    ki = np.arange(2 * BLK)[None, :]
    steps = BLK + qi - ki
    band = (steps >= 0) & (steps <= BLK)
    bias = -slopes[:, None, None] * (dil * steps).astype(np.float32)[None]
    return jnp.asarray(np.where(band[None], bias, np.float32(NEG)).astype(np.float32))


QB_FWD = 4
QB_BWD = 16


def _attn_specs(qb):
    prev = lambda n: jnp.maximum(n * qb - 1, 0)
    return [pl.BlockSpec((qb * BLK, HEAD), lambda h, n: (n, h)),
            pl.BlockSpec((BLK, HEAD), lambda h, n: (prev(n), NHG + h)),
            pl.BlockSpec((qb * BLK, HEAD), lambda h, n: (n, NHG + h)),
            pl.BlockSpec((BLK, HEAD), lambda h, n: (prev(n), 2 * NHG + h)),
            pl.BlockSpec((qb * BLK, HEAD), lambda h, n: (n, 2 * NHG + h)),
            pl.BlockSpec((None, BLK, 2 * BLK), lambda h, n: (h, 0, 0))]


def _scores(q, kcat, bias, blk, seg):
    s = _dot(q, kcat, NT) * (HEAD ** -0.5) + bias
    col = lax.broadcasted_iota(jnp.int32, s.shape, 1)
    first = (blk % seg) == 0
    return jnp.where(jnp.logical_and(first, col < BLK), NEG, s)


def _attn_fwd(qkv, gi, name):
    seg = (T // GROUPS[gi][1]) // BLK

    qb = QB_FWD

    def body(q_ref, kp_ref, kc_ref, vp_ref, vc_ref, bias_ref, o_ref, l_ref):
        n = pl.program_id(0)
        for h in range(NHG):
            cols = pl.ds(h * HEAD, HEAD)
            kwin = jnp.concatenate([kp_ref[:, cols], kc_ref[:, cols]], axis=0)
            vwin = jnp.concatenate([vp_ref[:, cols], vc_ref[:, cols]], axis=0)
            bias = bias_ref[h]
            for b in range(qb):
                rows = pl.ds(b * BLK, BLK)
                s = _scores(q_ref[rows, cols], kwin[b * BLK:(b + 2) * BLK], bias, n * qb + b, seg)
                mx = jnp.max(s, axis=-1, keepdims=True)
                p = jnp.exp(s - mx)
                den = jnp.sum(p, axis=-1, keepdims=True)
                o_ref[rows, cols] = (_dot(p.astype(BF16), vwin[b * BLK:(b + 2) * BLK], NN) / den).astype(BF16)
                l_ref[rows, cols] = jnp.broadcast_to(mx + jnp.log(den), (BLK, HEAD))

    prev = lambda n: jnp.maximum(n * qb - 1, 0)
    cur = lambda part: pl.BlockSpec((qb * BLK, AW), lambda n: (n, part))
    halo = lambda part: pl.BlockSpec((BLK, AW), lambda n: (prev(n), part))
    return pl.pallas_call(
        body, name=name, grid=(T // (qb * BLK),),
        in_specs=[cur(0), halo(1), cur(1), halo(2), cur(2), _const_spec((NHG, BLK, 2 * BLK))],
        out_specs=[cur(0), cur(0)],
        out_shape=[jax.ShapeDtypeStruct((T, AW), BF16), jax.ShapeDtypeStruct((T, AW), F32)],
        compiler_params=_cp(("parallel",)),
    )(qkv, qkv, qkv, qkv, qkv, _band_bias(gi))


def _attn_bwd(qkv, dob, lse, delta, gi, name):
    seg = (T // GROUPS[gi][1]) // BLK
    qb = QB_BWD
    nb = T // (qb * BLK)
    scale = HEAD ** -0.5

    def body(q_ref, kp_ref, kc_ref, vp_ref, vc_ref, bias_ref, do_ref, l_ref, dl_ref, out_ref, dk_acc, dv_acc):
        n = pl.program_id(1)
        kwin = jnp.concatenate([kp_ref[...], kc_ref[...]], axis=0)
        vwin = jnp.concatenate([vp_ref[...], vc_ref[...]], axis=0)
        bias = bias_ref[...]
        dks, dvs = [], []
        for b in range(qb):
            rows = pl.ds(b * BLK, BLK)
            q = q_ref[rows, :]
            kcat = kwin[b * BLK:(b + 2) * BLK]
            s = _scores(q, kcat, bias, n * qb + b, seg)
            p = jnp.exp(s - l_ref[rows, pl.ds(0, 1)])
            dov = do_ref[rows, :]
            dvs.append(_dot(p.astype(BF16), dov, TN))
            dp = _dot(dov, vwin[b * BLK:(b + 2) * BLK], NT)
            dsb = (p * (dp - dl_ref[rows, pl.ds(0, 1)]) * scale).astype(BF16)
            row = pl.ds(pl.multiple_of((n * qb + b) * BLK, BLK), BLK)
            out_ref[0, row, :] = _dot(dsb, kcat, NN).astype(BF16)
            dks.append(_dot(dsb, q, TN))
        for b in range(qb):
            row = pl.ds(pl.multiple_of((n * qb + b) * BLK, BLK), BLK)
            if b + 1 < qb:
                dk_acc[row, :] = dks[b][BLK:] + dks[b + 1][:BLK]
                dv_acc[row, :] = dvs[b][BLK:] + dvs[b + 1][:BLK]
            else:
                dk_acc[row, :] = dks[b][BLK:]
                dv_acc[row, :] = dvs[b][BLK:]

        @pl.when(n > 0)
        def _():
            prow = pl.ds(pl.multiple_of((n * qb - 1) * BLK, BLK), BLK)
            dk_acc[prow, :] += dks[0][:BLK]
            dv_acc[prow, :] += dvs[0][:BLK]

        @pl.when(n == nb - 1)
        def _():
            out_ref[1] = dk_acc[...].astype(BF16)
            out_ref[2] = dv_acc[...].astype(BF16)

    oblk = pl.BlockSpec((qb * BLK, HEAD), lambda h, n: (n, h))
    return pl.pallas_call(
        body, name=name, grid=(NHG, nb),
        in_specs=_attn_specs(qb) + [oblk, oblk, oblk],
        out_specs=pl.BlockSpec((3, T, HEAD), lambda h, n: (0, 0, h)),
        out_shape=jax.ShapeDtypeStruct((3, T, AW), BF16),
        scratch_shapes=[pltpu.VMEM((T, HEAD), F32), pltpu.VMEM((T, HEAD), F32)],
        compiler_params=_cp(("parallel", "arbitrary")),
    )(qkv, qkv, qkv, qkv, qkv, _band_bias(gi), dob, lse, delta)


def _merge(outs, lses, name):
    tm = PERM_TM
    dils = [d for _, d in GROUPS]
    ng = len(dils)

    def body(*refs):
        in_refs = refs[:2 * ng]
        ab_ref = refs[2 * ng]
        lse_refs = refs[2 * ng + 1:3 * ng + 1]
        tile = refs[-1]

        def token_order(ref, dil):
            if dil == 1:
                return ref[...].astype(F32)
            _load_unperm(ref, tile, dil)
            return _get_tile(tile)

        os = [token_order(in_refs[2 * i], d) for i, d in enumerate(dils)]
        ls = [token_order(in_refs[2 * i + 1], d) for i, d in enumerate(dils)]
        mx = jnp.maximum(jnp.maximum(ls[0], ls[1]), ls[2])
        es = [jnp.exp(v - mx) for v in ls]
        tot = es[0] + es[1] + es[2]
        att = (es[0] / tot) * os[0] + (es[1] / tot) * os[1] + (es[2] / tot) * os[2]
        ab_ref[...] = att.astype(BF16)
        lse = mx + jnp.log(tot)
        _put_tile(tile, lse)
        for dil, ref in zip(dils, lse_refs):
            if dil == 1:
                ref[...] = lse
            else:
                _store_perm(ref, tile, dil)

    row = pl.BlockSpec((tm, AW), lambda i: (i, 0))
    specs = [row if d == 1 else _perm_spec(d, AW) for d in dils]
    args = []
    for d, o, l in zip(dils, outs, lses):
        args += [o, l] if d == 1 else [o.reshape(d, T // d, AW), l.reshape(d, T // d, AW)]
    out = pl.pallas_call(
        body, name=name, grid=(T // tm,),
        in_specs=[sp for sp in specs for _ in range(2)], out_specs=[row] + specs,
        out_shape=[jax.ShapeDtypeStruct((T, AW), BF16)]
        + [jax.ShapeDtypeStruct((T, AW), F32) if d == 1 else _perm_shape(d, AW, F32) for d in dils],
        scratch_shapes=[_tile_scratch(AW)],
        compiler_params=_cp(("parallel",)),
    )(*args)
    return out[0], [o.reshape(T, AW) for o in out[1:]]


def _mix_out(z3b, attnb, gates, wc, wa_t, wo, x1, name):
    tm = 512

    def body(z_ref, a_ref, g_ref, wc_ref, wa_ref, wo_ref, x_ref, xo_ref, yc_ref, ya_ref, mx_ref):
        yc = _dot(z_ref[...], wc_ref[...], NN)
        ya = _dot(a_ref[...], wa_ref[...], NT)
        yc_ref[...] = yc.astype(BF16)
        ya_ref[...] = ya.astype(BF16)
        gv = g_ref[...].astype(F32)
        mixed = (_sig(gv[:, :D]) * yc + _sig(gv[:, D:]) * ya).astype(BF16)
        mx_ref[...] = mixed
        xo_ref[...] = x_ref[...] + _dot(mixed, wo_ref[...], NN)

    row = pl.BlockSpec((tm, D), lambda i: (i, 0))
    return pl.pallas_call(
        body, name=name, grid=(T // tm,),
        in_specs=[row, pl.BlockSpec((tm, AW), lambda i: (i, 0)), pl.BlockSpec((tm, 2 * D), lambda i: (i, 0)),
                  _const_spec((D, D)), _const_spec((D, AW)), _const_spec((D, D)), row],
        out_specs=[row, row, row, row],
        out_shape=[jax.ShapeDtypeStruct((T, D), F32), jax.ShapeDtypeStruct((T, D), BF16),
                   jax.ShapeDtypeStruct((T, D), BF16), jax.ShapeDtypeStruct((T, D), BF16)],
        compiler_params=_cp(("parallel",)),
    )(z3b, attnb, gates, wc, wa_t, wo, x1)


def _mix_out_bwd(dx2, gates, yc, ya, attn, wc, wa_t, wo, name):
    tm = PERM_TM
    dils = [d for _, d in GROUPS]
    ng = len(dils)

    def body(dx_ref, g_ref, yc_ref, ya_ref, at_ref, wc_ref, wa_ref, wo_ref,
             dg_ref, dyc_ref, dya_ref, dxb_ref, dz3_ref, *rest):
        dat_refs, dl_refs, tile = rest[:ng], rest[ng:2 * ng], rest[-1]
        dxb = dx_ref[...].astype(BF16)
        dxb_ref[...] = dxb
        dmix = _dot(dxb, wo_ref[...], NT)
        gv = g_ref[...].astype(F32)
        sc = _sig(gv[:, :D])
        sa = _sig(gv[:, D:])
        ycv, yav = yc_ref[...].astype(F32), ya_ref[...].astype(F32)
        dg_ref[:, pl.ds(0, D)] = (dmix * ycv * sc * (1.0 - sc)).astype(BF16)
        dg_ref[:, pl.ds(D, D)] = (dmix * yav * sa * (1.0 - sa)).astype(BF16)
        dyc = (dmix * sc).astype(BF16)
        dya = (dmix * sa).astype(BF16)
        dyc_ref[...] = dyc
        dya_ref[...] = dya
        dz3_ref[...] = _dot(dyc, wc_ref[...], NT).astype(BF16)
        dat = _dot(dya, wa_ref[...], NN)
        prod = dat * at_ref[...].astype(F32)
        delta = jnp.concatenate(
            [jnp.broadcast_to(jnp.sum(prod[:, h * HEAD:(h + 1) * HEAD], axis=-1, keepdims=True), (tm, HEAD))
             for h in range(NHG)], axis=1)
        for value, out_refs in ((dat, dat_refs), (delta, dl_refs)):
            _put_tile(tile, value)
            for dil, ref in zip(dils, out_refs):
                if dil == 1:
                    ref[...] = value.astype(ref.dtype)
                else:
                    _store_perm(ref, tile, dil)

    row = pl.BlockSpec((tm, D), lambda i: (i, 0))
    row2 = pl.BlockSpec((tm, 2 * D), lambda i: (i, 0))
    rowa = pl.BlockSpec((tm, AW), lambda i: (i, 0))
    aspecs = [rowa if d == 1 else _perm_spec(d, AW) for d in dils]

    def ashapes(dtype):
        return [jax.ShapeDtypeStruct((T, AW), dtype) if d == 1 else _perm_shape(d, AW, dtype) for d in dils]

    out = pl.pallas_call(
        body, name=name, grid=(T // tm,),
        in_specs=[row, row2, row, row, rowa, _const_spec((D, D)), _const_spec((D, AW)), _const_spec((D, D))],
        out_specs=[row2, row, row, row, row] + aspecs + aspecs,
        out_shape=[jax.ShapeDtypeStruct((T, 2 * D), BF16), jax.ShapeDtypeStruct((T, D), BF16),
                   jax.ShapeDtypeStruct((T, D), BF16), jax.ShapeDtypeStruct((T, D), BF16),
                   jax.ShapeDtypeStruct((T, D), BF16)] + ashapes(BF16) + ashapes(F32),
        scratch_shapes=[_tile_scratch(AW)],
        compiler_params=_cp(("parallel",)),
    )(dx2, gates, yc, ya, attn, wc, wa_t, wo)
    dats = [o.reshape(T, AW) for o in out[5:5 + ng]]
    deltas = [o.reshape(T, AW) for o in out[5 + ng:5 + 2 * ng]]
    return out[0], out[1], out[2], out[3], out[4], dats, deltas


def _peer(k):
    x, y, c = lax.axis_index("x"), lax.axis_index("y"), lax.axis_index("c")
    px = 1 - x if k & 4 else x
    py = 1 - y if k & 2 else y
    pc = 1 - c if k & 1 else c
    return (px, py, pc), 4 * px + 2 * py + pc


HBM_SPEC = pl.BlockSpec(memory_space=pltpu.HBM)
SEM_SPEC = pl.BlockSpec(memory_space=pltpu.SEMAPHORE)
EFFECT = pltpu.SideEffectType.DATAFLOW_SIDE_EFFECTING


def _my_place():
    return 4 * lax.axis_index("x") + 2 * lax.axis_index("y") + lax.axis_index("c")


def _tie(a, order_after, name):
    na = len(order_after)

    def body(*refs):
        del refs

    return pl.pallas_call(
        body, name=name, in_specs=[pl.BlockSpec(memory_space=pl.ANY)] * (1 + na),
        out_specs=pl.BlockSpec(memory_space=pl.ANY), out_shape=jax.ShapeDtypeStruct(a.shape, a.dtype),
        input_output_aliases={0: 0},
    )(a, *order_after)


def _prep_gather(ws, order_after, name):
    me = jnp.reshape(_my_place(), (1,)).astype(jnp.int32)
    n = len(ws)
    na = len(order_after)
    shapes = [((32, wv.shape[1]), F32) if wv.shape[0] == CONV_W else (wv.shape, BF16) for wv in ws]

    def body(me_ref, *refs):
        del me_ref
        ins, outs = refs[:n], refs[n + na:]
        for wv, i_ref, o_ref in zip(ws, ins, outs):
            if wv.shape[0] == CONV_W:
                o_ref[pl.ds(0, CONV_W), :] = i_ref[...]
                o_ref[pl.ds(CONV_W, 1), :] = jnp.zeros((1, wv.shape[1]), F32)
            else:
                o_ref[...] = i_ref[...].astype(BF16)

    grid_spec = pltpu.PrefetchScalarGridSpec(
        num_scalar_prefetch=1, grid=(1,),
        in_specs=[pl.BlockSpec(wv.shape, lambda i, m: (0, 0)) for wv in ws]
        + [pl.BlockSpec(memory_space=pl.ANY)] * na,
        out_specs=[pl.BlockSpec(shp, lambda i, m: (m[0], 0)) for shp, _ in shapes])
    return pl.pallas_call(
        body, name=name, grid_spec=grid_spec,
        out_shape=[jax.ShapeDtypeStruct((NDEV * shp[0], shp[1]), dt) for shp, dt in shapes],
        compiler_params=_cp(("arbitrary",)),
    )(me, *ws, *order_after)


GATHER_A = ((1, 0), (2, 0), (4, 0), (6, 0))
GATHER_B = ((1, 2), (1, 4), (1, 6))


def _gather_start(lands, plan, order_after, name):
    n = len(lands)
    na = len(order_after)
    npl = len(plan)

    def body(*refs):
        land_refs = refs[:n]
        send, recv = refs[n + na], refs[n + na + 1]
        token = refs[-1]
        for w in range(n):
            rows = lands[w].shape[0] // NDEV
            for p, (k, j) in enumerate(plan):
                peer, _ = _peer(k)
                _, blk = _peer(j)
                part = land_refs[w].at[pl.ds(blk * rows, rows)]
                i = w * npl + p
                pltpu.make_async_remote_copy(src_ref=part, dst_ref=part, send_sem=send.at[i], recv_sem=recv.at[i],
                                             device_id=peer, device_id_type=MESH_ID).start()
        token[...] = jnp.zeros_like(token)

    nsem = n * npl
    bufs = [pltpu.with_memory_space_constraint(a, pltpu.HBM) for a in lands]
    out = pl.pallas_call(
        body, name=name,
        in_specs=[HBM_SPEC] * n + [pl.BlockSpec(memory_space=pl.ANY)] * na,
        out_specs=[SEM_SPEC, SEM_SPEC] + [HBM_SPEC] * n + [pl.BlockSpec(memory_space=pltpu.VMEM)],
        out_shape=[pltpu.SemaphoreType.DMA((nsem,)), pltpu.SemaphoreType.DMA((nsem,))]
        + [pltpu.HBM(a.shape, a.dtype) for a in bufs] + [jax.ShapeDtypeStruct((8, 128), F32)],
        input_output_aliases={i: 2 + i for i in range(n)},
        compiler_params=pltpu.CompilerParams(has_side_effects=EFFECT),
    )(*bufs, *order_after)
    return out[0], out[1], out[2:2 + n], out[-1]


def _gather_wait(started, plan, order_after, name):
    send, recv, lands, _ = started
    n = len(lands)
    na = len(order_after)
    npl = len(plan)

    def body(*refs):
        land_refs = refs[:n]
        send_ref, recv_ref = refs[n], refs[n + 1]
        for w in range(n):
            rows = lands[w].shape[0] // NDEV
            for p, (k, j) in enumerate(plan):
                peer, _ = _peer(k)
                _, blk = _peer(j)
                part = land_refs[w].at[pl.ds(blk * rows, rows)]
                i = w * npl + p
                cp = pltpu.make_async_remote_copy(src_ref=part, dst_ref=part, send_sem=send_ref.at[i],
                                                  recv_sem=recv_ref.at[i], device_id=peer, device_id_type=MESH_ID)
                cp.wait_send()
                cp.wait_recv()

    out = pl.pallas_call(
        body, name=name,
        in_specs=[HBM_SPEC] * n + [SEM_SPEC, SEM_SPEC] + [pl.BlockSpec(memory_space=pl.ANY)] * na,
        out_specs=[HBM_SPEC] * n,
        out_shape=[pltpu.HBM(a.shape, a.dtype) for a in lands],
        input_output_aliases={i: i for i in range(n)},
        compiler_params=pltpu.CompilerParams(has_side_effects=EFFECT),
    )(*lands, send, recv, *order_after)
    return list(out)


def _copy_ends(kind, src, land, me, plin, k):
    if kind == "scatter":
        rows = src.shape[0] // NDEV
        return src.at[pl.ds(plin * rows, rows)], land.at[k - 1]
    return src, land.at[me]


def _landing(kind, src):
    me = _my_place()
    if kind == "scatter":
        return lax.empty((NDEV - 1, src.shape[0] // NDEV) + src.shape[1:], src.dtype)
    land = lax.empty((NDEV,) + src.shape, src.dtype)
    return lax.dynamic_update_slice(land, src[None], (me,) + (0,) * src.ndim)


def _send_start(kinds, srcs, order_after, name):
    n = len(srcs)
    lands = [_landing(kd, s) for kd, s in zip(kinds, srcs)]
    na = len(order_after)

    def body(*refs):
        src_refs, land_refs = refs[:n], refs[n:2 * n]
        send, recv = refs[2 * n + na], refs[2 * n + na + 1]
        token = refs[-1]
        _, me = _peer(0)
        for w in range(n):
            for k in range(1, NDEV):
                peer, plin = _peer(k)
                s, d = _copy_ends(kinds[w], src_refs[w], land_refs[w], me, plin, k)
                i = w * (NDEV - 1) + k - 1
                pltpu.make_async_remote_copy(src_ref=s, dst_ref=d, send_sem=send.at[i], recv_sem=recv.at[i],
                                             device_id=peer, device_id_type=MESH_ID).start()
        token[...] = jnp.zeros_like(token)

    nsem = n * (NDEV - 1)
    bufs = [pltpu.with_memory_space_constraint(a, pltpu.HBM) for a in list(srcs) + lands]
    out = pl.pallas_call(
        body, name=name,
        in_specs=[HBM_SPEC] * (2 * n) + [pl.BlockSpec(memory_space=pl.ANY)] * na,
        out_specs=[SEM_SPEC, SEM_SPEC] + [HBM_SPEC] * (2 * n) + [pl.BlockSpec(memory_space=pltpu.VMEM)],
        out_shape=[pltpu.SemaphoreType.DMA((nsem,)), pltpu.SemaphoreType.DMA((nsem,))]
        + [pltpu.HBM(a.shape, a.dtype) for a in bufs] + [jax.ShapeDtypeStruct((8, 128), F32)],
        input_output_aliases={i: 2 + i for i in range(2 * n)},
        compiler_params=pltpu.CompilerParams(has_side_effects=EFFECT),
    )(*bufs, *order_after)
    return out[0], out[1], out[2:2 + n], out[2 + n:2 + 2 * n], out[-1]


def _send_wait(kinds, started, order_after, name):
    send, recv, srcs, lands, _ = started
    n = len(srcs)
    na = len(order_after)

    def body(*refs):
        src_refs, land_refs = refs[:n], refs[n:2 * n]
        send_ref, recv_ref = refs[2 * n], refs[2 * n + 1]
        _, me = _peer(0)
        for w in range(n):
            for k in range(1, NDEV):
                peer, plin = _peer(k)
                s, d = _copy_ends(kinds[w], src_refs[w], land_refs[w], me, plin, k)
                i = w * (NDEV - 1) + k - 1
                cp = pltpu.make_async_remote_copy(src_ref=s, dst_ref=d, send_sem=send_ref.at[i],
                                                  recv_sem=recv_ref.at[i], device_id=peer, device_id_type=MESH_ID)
                cp.wait_send()
                cp.wait_recv()

    bufs = list(srcs) + list(lands)
    out = pl.pallas_call(
        body, name=name,
        in_specs=[HBM_SPEC] * (2 * n) + [SEM_SPEC, SEM_SPEC] + [pl.BlockSpec(memory_space=pl.ANY)] * na,
        out_specs=[HBM_SPEC] * (2 * n),
        out_shape=[pltpu.HBM(a.shape, a.dtype) for a in bufs],
        input_output_aliases={i: i for i in range(2 * n)},
        compiler_params=pltpu.CompilerParams(has_side_effects=EFFECT),
    )(*bufs, send, recv, *order_after)
    return out[:n], out[n:]


def _gsum(own, land, name):
    rows, cols = own.shape
    tr = rows // 2 if rows * cols > 512 * 1024 and rows % 32 == 0 else rows

    def body(own_ref, l_ref, o_ref):
        tot = own_ref[...].astype(F32)
        for s in range(NDEV - 1):
            tot = tot + l_ref[s].astype(F32)
        o_ref[...] = tot

    return pl.pallas_call(
        body, name=name, grid=(rows // tr,),
        in_specs=[pl.BlockSpec((tr, cols), lambda i: (i, 0)),
                  pl.BlockSpec((NDEV - 1, tr, cols), lambda i: (0, i, 0))],
        out_specs=pl.BlockSpec((tr, cols), lambda i: (i, 0)),
        out_shape=jax.ShapeDtypeStruct((rows, cols), F32),
        compiler_params=_cp(("parallel",)),
    )(own, land)


def _adamw_math(w, g, m, v):
    m2 = B1 * m + (1.0 - B1) * g
    v2 = B2 * v + (1.0 - B2) * (g * g)
    m_hat = m2 / (1.0 - B1 ** STEP)
    v_hat = v2 / (1.0 - B2 ** STEP)
    delta = -LR * (m_hat / (jnp.sqrt(v_hat) + AEPS) + WD * w)
    return delta, m2, v2


def _adamw(w, g, m, v, name):
    rows, cols = w.shape
    tr = 256 if rows % 256 == 0 and rows > 256 else rows

    def body(w_ref, g_ref, m_ref, v_ref, d_ref, mo_ref, vo_ref):
        d, m2, v2 = _adamw_math(w_ref[...], g_ref[...], m_ref[...], v_ref[...])
        d_ref[...] = d
        mo_ref[...] = m2
        vo_ref[...] = v2

    blk = pl.BlockSpec((tr, cols), lambda i: (i, 0))
    return pl.pallas_call(
        body, name=name, grid=(rows // tr,), in_specs=[blk] * 4, out_specs=[blk] * 3,
        out_shape=[jax.ShapeDtypeStruct((rows, cols), F32)] * 3,
        compiler_params=_cp(("parallel",)),
    )(w, g, m, v)


UPD_TC = 256


def _update(src, land, w, m, v, name):
    rows, cols = land.shape[1:]
    tc = min(UPD_TC, cols)
    me = jnp.reshape(_my_place(), (1,)).astype(jnp.int32)

    def body(me_ref, own_ref, l_ref, w_ref, m_ref, v_ref, g_ref, d_ref, mo_ref, vo_ref):
        del me_ref
        g = own_ref[...].astype(F32)
        for s in range(NDEV - 1):
            g = g + l_ref[s].astype(F32)
        g_ref[...] = g
        d, m2, v2 = _adamw_math(w_ref[...], g, m_ref[...], v_ref[...])
        d_ref[...] = d
        mo_ref[...] = m2
        vo_ref[...] = v2

    wblk = pl.BlockSpec((rows, tc), lambda j, p: (0, j))
    grid_spec = pltpu.PrefetchScalarGridSpec(
        num_scalar_prefetch=1, grid=(cols // tc,),
        in_specs=[pl.BlockSpec((rows, tc), lambda j, p: (p[0], j)),
                  pl.BlockSpec((NDEV - 1, rows, tc), lambda j, p: (0, 0, j)), wblk, wblk, wblk],
        out_specs=[wblk] * 4)
    return pl.pallas_call(
        body, name=name, grid_spec=grid_spec, out_shape=[jax.ShapeDtypeStruct((rows, cols), F32)] * 4,
        compiler_params=_cp(("parallel",)),
    )(me, src, land, w, m, v)


def _small_update(vland, w8, m8, v8, name):
    def body(l_ref, w_ref, m_ref, v_ref, g_ref, d_ref, mo_ref, vo_ref):
        g = l_ref[0]
        for s in range(1, NDEV):
            g = g + l_ref[s]
        g_ref[...] = g
        d, m2, v2 = _adamw_math(w_ref[...], g, m_ref[...], v_ref[...])
        d_ref[...] = d
        mo_ref[...] = m2
        vo_ref[...] = v2

    return pl.pallas_call(
        body, name=name, out_shape=[jax.ShapeDtypeStruct((8, D), F32)] * 4,
        compiler_params=_cp(None),
    )(vland, w8, m8, v8)


def kernel(x, ffn1_norm, ffn1_w_gate, ffn1_w_up, ffn1_w_down, mix_norm, w_in, conv_dw_kernel, conv_dw_bias, conv_ln_gain, conv_ln_bias, conv_w_out, attn_w_out, w_o, ffn2_norm, ffn2_w_gate, ffn2_w_up, ffn2_w_down, final_norm, loss_target, m_ffn1_norm, m_ffn1_w_gate, m_ffn1_w_up, m_ffn1_w_down, m_mix_norm, m_w_in, m_conv_dw_kernel, m_conv_dw_bias, m_conv_ln_gain, m_conv_ln_bias, m_conv_w_out, m_attn_w_out, m_w_o, m_ffn2_norm, m_ffn2_w_gate, m_ffn2_w_up, m_ffn2_w_down, m_final_norm, v_ffn1_norm, v_ffn1_w_gate, v_ffn1_w_up, v_ffn1_w_down, v_mix_norm, v_w_in, v_conv_dw_kernel, v_conv_dw_bias, v_conv_ln_gain, v_conv_ln_bias, v_conv_w_out, v_attn_w_out, v_w_o, v_ffn2_norm, v_ffn2_w_gate, v_ffn2_w_up, v_ffn2_w_down, v_final_norm):
    names = ["ffn1_norm", "ffn1_w_gate", "ffn1_w_up", "ffn1_w_down", "mix_norm", "w_in", "conv_dw_kernel",
             "conv_dw_bias", "conv_ln_gain", "conv_ln_bias", "conv_w_out", "attn_w_out", "w_o", "ffn2_norm",
             "ffn2_w_gate", "ffn2_w_up", "ffn2_w_down", "final_norm"]
    w = dict(ffn1_norm=ffn1_norm, ffn1_w_gate=ffn1_w_gate, ffn1_w_up=ffn1_w_up, ffn1_w_down=ffn1_w_down, mix_norm=mix_norm, w_in=w_in, conv_dw_kernel=conv_dw_kernel, conv_dw_bias=conv_dw_bias, conv_ln_gain=conv_ln_gain, conv_ln_bias=conv_ln_bias, conv_w_out=conv_w_out, attn_w_out=attn_w_out, w_o=w_o, ffn2_norm=ffn2_norm, ffn2_w_gate=ffn2_w_gate, ffn2_w_up=ffn2_w_up, ffn2_w_down=ffn2_w_down, final_norm=final_norm)
    mo = dict(ffn1_norm=m_ffn1_norm, ffn1_w_gate=m_ffn1_w_gate, ffn1_w_up=m_ffn1_w_up, ffn1_w_down=m_ffn1_w_down, mix_norm=m_mix_norm, w_in=m_w_in, conv_dw_kernel=m_conv_dw_kernel, conv_dw_bias=m_conv_dw_bias, conv_ln_gain=m_conv_ln_gain, conv_ln_bias=m_conv_ln_bias, conv_w_out=m_conv_w_out, attn_w_out=m_attn_w_out, w_o=m_w_o, ffn2_norm=m_ffn2_norm, ffn2_w_gate=m_ffn2_w_gate, ffn2_w_up=m_ffn2_w_up, ffn2_w_down=m_ffn2_w_down, final_norm=m_final_norm)
    vo = dict(ffn1_norm=v_ffn1_norm, ffn1_w_gate=v_ffn1_w_gate, ffn1_w_up=v_ffn1_w_up, ffn1_w_down=v_ffn1_w_down, mix_norm=v_mix_norm, w_in=v_w_in, conv_dw_kernel=v_conv_dw_kernel, conv_dw_bias=v_conv_dw_bias, conv_ln_gain=v_conv_ln_gain, conv_ln_bias=v_conv_ln_bias, conv_w_out=v_conv_w_out, attn_w_out=v_attn_w_out, w_o=v_w_o, ffn2_norm=v_ffn2_norm, ffn2_w_gate=v_ffn2_w_gate, ffn2_w_up=v_ffn2_w_up, ffn2_w_down=v_ffn2_w_down, final_norm=v_final_norm)
    col_sharded = ("ffn1_w_gate", "ffn1_w_up", "w_in", "attn_w_out", "ffn2_w_gate", "ffn2_w_up")
    row_sharded = ("ffn1_w_down", "conv_w_out", "w_o", "ffn2_w_down")
    small = ("ffn1_norm", "mix_norm", "ffn2_norm", "final_norm", "conv_dw_bias", "conv_ln_gain", "conv_ln_bias")

    def landing_view(a, n):
        return jnp.transpose(a[0]) if n in col_sharded else a[0]

    def own_view(a, n):
        return jnp.transpose(a)[None] if n in col_sharded else a[None]

    ag_groups = (("ffn1_w_gate", "ffn1_w_up", "ffn1_w_down"),
                 ("w_in", "attn_w_out", "conv_w_out", "w_o", "conv_dw_kernel"),
                 ("ffn2_w_gate", "ffn2_w_up", "ffn2_w_down"))
    ag, order = [], []
    for gi, grp in enumerate(ag_groups):
        lands = _prep_gather([landing_view(w[n], n) for n in grp], order, f"gather_prep{gi}")
        st = _gather_start(lands, GATHER_A, [], f"gather_a_start{gi}")
        ag.append(st)
        order = [st[3]]

    def chips_in(gi, after):
        lands = _gather_wait(ag[gi], GATHER_A, after, f"gather_a_wait{gi}")
        return _gather_start(lands, GATHER_B, [], f"gather_b_start{gi}")

    def all_in(gi, st, after):
        return _gather_wait(st, GATHER_B, after, f"gather_b_wait{gi}")

    x0 = x[0]
    tgt = loss_target[0]
    gf = final_norm.reshape(1, D)

    wg1, wu1, wd1 = all_in(0, chips_in(0, [ag[2][3]]), [])
    x1, gg1, uu1, h2p = _ffn_fwd(x0, ffn1_norm, wg1, wu1, wd1, "ffn1_fwd", next_gain=mix_norm)
    h2 = h2p[0]
    win_t, wa_t, wc, wo, kern_blocks = all_in(1, chips_in(1, [x1]), [])
    kern = kern_blocks.reshape(NDEV, 32, D // NDEV).transpose(1, 0, 2).reshape(32, D)
    ptm = min(T, 2048)
    ab = _mm(h2, win_t, mode="nt", m=T, n=2 * D, k=D, tm=ptm, tn=512, tk=D, out_dtype=BF16, name="proj_conv")
    gates = _mm(h2, win_t, mode="nt", m=T, n=2 * D, k=D, tm=ptm, tn=512, tk=D, out_dtype=BF16,
                b_map=lambda i, j, kk: (13 + j, 0), name="proj_gates")
    qkv = []
    for gi in range(len(GROUPS)):
        qkv.append(_mm(h2p[gi], win_t, mode="nt", m=T, n=3 * AW, k=D, tm=ptm, tn=AW, tk=D, out_dtype=BF16,
                       b_map=lambda i, j, kk, gi=gi: (4 + gi + 3 * j, 0), name=f"proj_qkv{gi}"))
    z1, z3b = _conv_fwd(ab, kern, conv_dw_bias, conv_ln_gain, conv_ln_bias, "conv_fwd")
    ffn2_b = chips_in(2, [z3b])
    outs, lses = [], []
    for gi, (_, dil) in enumerate(GROUPS):
        o, l = _attn_fwd(qkv[gi], gi, f"attn_fwd{gi}")
        outs.append(o)
        lses.append(l)
    attnb, lse = _merge(outs, lses, "attn_merge")
    x2, yc, ya, mixedb = _mix_out(z3b, attnb, gates, wc, wa_t, wo, x1, "mix_out_fwd")
    wg2, wu2, wd2 = all_in(2, ffn2_b, [x2])
    gg2, uu2, dx3, dgf, loss_part = _ffn_fwd(x2, ffn2_norm, wg2, wu2, wd2, "ffn2_fwd", loss_of=(gf, tgt))

    dx2, dg3, dgb, dub, actb, hb, dob = _ffn_bwd(x2, ffn2_norm, gg2, uu2, dx3, wg2, wu2, wd2, "ffn2_bwd")
    grads = {}
    grads["ffn2_w_down"] = _wgrad(actb, dob, FF, D, "ffn2_dwd")
    rs_groups = [("ffn2_w_gate", "ffn2_w_up", "ffn2_w_down"),
                 ("attn_w_out", "conv_w_out", "w_o", "conv_dw_kernel"),
                 ("w_in",),
                 ("ffn1_w_gate",), ("ffn1_w_up",), ("ffn1_w_down",), ()]
    last = len(rs_groups) - 1
    rs = []

    dgates, dycb, dyab, dx2b, dz3, dattnb, delta = _mix_out_bwd(dx2, gates, yc, ya, attnb, wc, wa_t, wo, "mix_out_bwd")
    grads["w_o"] = _wgrad(mixedb, dx2b, D, D, "dw_o")
    grads["conv_w_out"] = _wgrad(z3b, dycb, D, D, "dw_conv_out")
    grads["attn_w_out"] = _wgrad(dyab, attnb, D, AW, "dw_attn_out")
    dab, dkern, dvec, grads["ffn2_w_gate"], grads["ffn2_w_up"] = _conv_bwd(
        dz3, z1, ab, kern, conv_ln_gain, conv_ln_bias, "conv_bwd", guest_lhs=(dgb, dub), guest_rhs=hb)
    grads["conv_dw_kernel"] = dkern.reshape(32, NDEV, D // NDEV).transpose(1, 0, 2).reshape(NDEV * 32, D // NDEV)
    rs.append(_send_start(["scatter"] * 3, [grads[n] for n in rs_groups[0]], [], "scatter_start0"))
    rs.append(_send_start(["scatter"] * 4, [grads[n] for n in rs_groups[1]], [rs[0][4]], "scatter_start1"))
    dattnb = [_tie(a, [rs[1][4]], f"tie_after_scatter1_{i}") for i, a in enumerate(dattnb)]

    dqkv = []
    for gi, (_, dil) in enumerate(GROUPS):
        dq3 = _attn_bwd(qkv[gi], dattnb[gi], lse[gi], delta[gi], gi, f"attn_bwd{gi}")
        dqkv.append(dq3.reshape(3 * T, AW))

    wtk = min(T, 2048)
    dwin = _mm(dab, h2, mode="tn", m=2 * D, n=D, k=T, tm=2 * D, tn=D, tk=wtk, out_dtype=BF16, out_rows=IN_W,
               name="dw_in_conv")
    dwin = _mm(dgates, h2, mode="tn", m=2 * D, n=D, k=T, tm=512, tn=D, tk=wtk, out_dtype=BF16, out_rows=IN_W,
               o_map=lambda i, j, kk: (13 + i, 0), passthru=dwin, name="dw_in_gates")
    for gi in range(3):
        dwin = _mm(dqkv[gi], h2p[gi], mode="tn", m=3 * AW, n=D, k=T, tm=AW, tn=D, tk=wtk, out_dtype=BF16,
                   out_rows=IN_W, a_map=lambda i, j, kk: (i * (T // wtk) + kk, 0),
                   o_map=lambda i, j, kk, gi=gi: (4 + gi + 3 * i, 0), passthru=dwin, name=f"dw_in_qkv{gi}")
    grads["w_in"] = dwin
    rs.append(_send_start(["scatter"], [dwin], [rs[1][4]], "scatter_start2"))
    dab = _tie(dab, [rs[2][4]], "tie_after_scatter2")

    nrow = T // 1024
    dh = _mm(dab, win_t, mode="nn", m=T, n=D, k=2 * D, tm=1024, tn=D, tk=2 * D, out_dtype=F32, name="dproj_conv")
    dh = _mm(dgates, win_t[IN_W - 2 * D:], mode="nn", m=T, n=D, k=2 * D, tm=1024, tn=D, tk=2 * D, out_dtype=F32,
             init=dh, name="dproj_gates")
    dhs = []
    for gi, (_, dil) in enumerate(GROUPS):
        part = _mm(dqkv[gi], win_t, mode="nn", m=T, n=D, k=3 * AW, tm=1024, tn=D, tk=AW,
                   out_dtype=F32 if gi == 0 else BF16,
                   a_map=lambda i, j, kk: (kk * nrow + i, 0), b_map=lambda i, j, kk, gi=gi: (4 + gi + 3 * kk, 0),
                   init=dh if gi == 0 else None, name=f"dproj_qkv{gi}")
        dhs.append(part)
    dx1, dg2 = _rms_bwd(x1, mix_norm, dhs, dx2, "mix_norm_bwd")

    dgb, dub, actb, hb, dob = _ffn_bwd_pre(x0, ffn1_norm, gg1, uu1, dx1, wd1, "ffn1_bwd_pre")
    grads["ffn1_w_gate"] = _wgrad(dgb, hb, FF, D, "ffn1_dwg")
    rs.append(_send_start(["scatter"], [grads["ffn1_w_gate"]], [rs[2][4]], "scatter_start3"))
    hb = _tie(hb, [rs[3][4]], "tie_after_scatter3")
    grads["ffn1_w_up"] = _wgrad(dub, hb, FF, D, "ffn1_dwu")
    rs.append(_send_start(["scatter"], [grads["ffn1_w_up"]], [rs[3][4]], "scatter_start4"))
    dob = _tie(dob, [rs[4][4]], "tie_after_scatter4")
    grads["ffn1_w_down"] = _wgrad(actb, dob, FF, D, "ffn1_dwd")
    rs.append(_send_start(["scatter"], [grads["ffn1_w_down"]], [rs[4][4]], "scatter_start5"))
    dgb = _tie(dgb, [rs[5][4]], "tie_after_scatter5")
    dx0, dg1 = _ffn_bwd_dx(x0, ffn1_norm, dgb, dub, dx1, wg1, wu1, "ffn1_bwd_dx")
    vec = jnp.concatenate([dg1, dg2, dg3, dgf, dvec[0:3], jnp.broadcast_to(loss_part[:, :1], (1, D))], axis=0)
    rs.append(_send_start(["bcast"], [vec], [rs[5][4]], "scatter_start6"))

    g_out, d_out, m_out, v_out = {}, {}, {}, {}
    me = _my_place()
    after = [rs[last][4]]
    for gi, grp in enumerate(rs_groups):
        kinds = ["scatter"] * len(grp) + (["bcast"] if gi == last else [])
        srcs, lands = _send_wait(kinds, rs[gi], after, f"scatter_wait{gi}")
        for n, src, land in zip(grp, srcs, lands):
            if n == "conv_dw_kernel":
                rows = src.shape[0] // NDEV
                own = lax.dynamic_slice(src, (me * rows, 0), (rows, src.shape[1]))
                g = _gsum(own, land, f"gsum_{n}")[:CONV_W]
                d, m2, v2 = _adamw(w[n][0], g, mo[n][0], vo[n][0], f"adamw_{n}")
                after = [d]
                g, d, m2, v2 = g[None], d[None], m2[None], v2[None]
            else:
                res = _update(src, land, landing_view(w[n], n), landing_view(mo[n], n), landing_view(vo[n], n),
                              f"update_{n}")
                after = [res[1]]
                g, d, m2, v2 = (own_view(a, n) for a in res)
            g_out[n], d_out[n], m_out[n], v_out[n] = g, d, m2, v2
    vland = lands[-1]

    def rows8(src):
        return jnp.concatenate([src[n].reshape(1, D) for n in small] + [jnp.ones((1, D), F32)], axis=0)

    g8, d8, m8, v8 = _small_update(vland, rows8(w), rows8(mo), rows8(vo), "small_update")
    for r, n in enumerate(small):
        shp = w[n].shape
        g_out[n], d_out[n], m_out[n], v_out[n] = (a[r].reshape(shp) for a in (g8, d8, m8, v8))
    loss = g8[7, 0]

    return (loss, dx0[None], *[g_out[n] for n in names], *[d_out[n] for n in names],
            *[m_out[n] for n in names], *[v_out[n] for n in names])
```

```python
import numpy as np
import jax
import jax.numpy as jnp
from jax import lax
from jax.experimental import pallas as pl
from jax.experimental.pallas import tpu as pltpu

F32 = jnp.float32
BF16 = jnp.bfloat16

T = 4096
D = 1024
FF = 2816
NDEV = 8
CONV_W = 31
HEAD = 128
BLK = 128
GROUPS = ((128, 1), (512, 4), (2048, 16))
NHG = 4
AW = NHG * HEAD
IN_W = 2 * D + 3 * 3 * AW + 2 * D
EPS = 1e-6
B1, B2, LR, AEPS, WD, STEP = 0.9, 0.999, 0.001, 1e-08, 0.01, 10
NEG = -1e30
VMEM_LIMIT = 56 * 1024 * 1024
MESH_ID = pl.DeviceIdType.MESH

NT = (((1,), (1,)), ((), ()))
NN = (((1,), (0,)), ((), ()))
TN = (((0,), (0,)), ((), ()))
_DIMS = {"nn": NN, "nt": NT, "tn": TN}


def _cp(sem=None):
    return pltpu.CompilerParams(dimension_semantics=sem, vmem_limit_bytes=VMEM_LIMIT)


def _sig(v):
    return 1.0 / (1.0 + jnp.exp(-v))


def _dot(a, b, dims):
    return lax.dot_general(a, b, dims, preferred_element_type=F32)


def _const_spec(shape):
    nd = len(shape)
    return pl.BlockSpec(shape, lambda *_: (0,) * nd)


def _mm(a, b, *, mode, m, n, k, tm, tn, tk, out_dtype, name, a_map=None, b_map=None,
        o_map=None, out_rows=None, init=None, passthru=None):
    gi, gj, gk = m // tm, n // tn, k // tk
    assert gi * tm == m and gj * tn == n and gk * tk == k, (name, m, n, k, tm, tn, tk)
    if mode == "nn":
        a_blk, b_blk = (tm, tk), (tk, tn)
        da, db = (lambda i, j, kk: (i, kk)), (lambda i, j, kk: (kk, j))
    elif mode == "nt":
        a_blk, b_blk = (tm, tk), (tn, tk)
        da, db = (lambda i, j, kk: (i, kk)), (lambda i, j, kk: (j, kk))
    else:
        a_blk, b_blk = (tk, tm), (tk, tn)
        da, db = (lambda i, j, kk: (kk, i)), (lambda i, j, kk: (kk, j))
    a_map = a_map or da
    b_map = b_map or db
    o_map = o_map or (lambda i, j, kk: (i, j))
    dims = _DIMS[mode]
    extra = init if init is not None else passthru
    out_rows = out_rows or m

    def body(*refs):
        if init is not None:
            a_ref, b_ref, i_ref, o_ref = refs[:4]
        elif passthru is not None:
            a_ref, b_ref, _, o_ref = refs[:4]
        else:
            a_ref, b_ref, o_ref = refs[:3]
        if gk == 1:
            prod = _dot(a_ref[...], b_ref[...], dims)
            if init is not None:
                prod = prod + i_ref[...].astype(F32)
            o_ref[...] = prod.astype(out_dtype)
            return
        acc = refs[-1]
        kk = pl.program_id(2)

        @pl.when(kk == 0)
        def _():
            if init is not None:
                acc[...] = i_ref[...].astype(F32)
            else:
                acc[...] = jnp.zeros_like(acc)

        acc[...] += _dot(a_ref[...], b_ref[...], dims)

        @pl.when(kk == gk - 1)
        def _():
            o_ref[...] = acc[...].astype(out_dtype)

    in_specs = [pl.BlockSpec(a_blk, a_map), pl.BlockSpec(b_blk, b_map)]
    args = [a, b]
    aliases = {}
    if init is not None:
        in_specs.append(pl.BlockSpec((tm, tn), o_map))
        args.append(init)
        aliases = {2: 0}
    elif passthru is not None:
        in_specs.append(pl.BlockSpec(memory_space=pl.ANY))
        args.append(passthru)
        aliases = {2: 0}
    out_dt = extra.dtype if extra is not None else out_dtype
    assert out_dt == out_dtype
    return pl.pallas_call(
        body, name=name, grid=(gi, gj, gk),
        in_specs=in_specs, out_specs=pl.BlockSpec((tm, tn), o_map),
        out_shape=jax.ShapeDtypeStruct((out_rows, n), out_dtype),
        scratch_shapes=[pltpu.VMEM((tm, tn), F32)] if gk > 1 else [],
        input_output_aliases=aliases,
        compiler_params=_cp(("parallel", "parallel", "arbitrary")),
    )(*args)


def _ffn_fwd(x, g, wg_t, wu_t, wd, name, next_gain=None, loss_of=None):
    tm, fc = PERM_TM, 256
    nc = FF // fc
    n_in = 5 + (1 if next_gain is not None else 0) + (2 if loss_of is not None else 0)

    def body(*refs):
        x_ref, g_ref, wg_ref, wu_ref, wd_ref = refs[:5]
        extra_in, outs = refs[5:n_in], refs[n_in:]
        act_ref = outs[-1]
        xv = x_ref[...]
        r = lax.rsqrt(jnp.mean(xv * xv, axis=-1, keepdims=True) + EPS)
        h = (xv * r * g_ref[...]).astype(BF16)
        gg_ref, uu_ref = (outs[0], outs[1]) if loss_of is not None else (outs[1], outs[2])
        for c in range(nc):
            sl = pl.ds(c * fc, fc)
            gg = _dot(h, wg_ref[sl, :], NT)
            uu = _dot(h, wu_ref[sl, :], NT)
            gg_ref[:, sl] = gg.astype(BF16)
            uu_ref[:, sl] = uu.astype(BF16)
            act_ref[:, sl] = (gg * _sig(gg) * uu).astype(BF16)
        y = xv + 0.5 * _dot(act_ref[...], wd_ref[...], NN)
        if loss_of is not None:
            _final_math(y, extra_in[0][...], extra_in[1][...], outs[2], outs[3], outs[4], pl.program_id(0))
            return
        outs[0][...] = y
        if next_gain is not None:
            tile = outs[-2]
            r2 = lax.rsqrt(jnp.mean(y * y, axis=-1, keepdims=True) + EPS)
            hv = y * r2 * extra_in[0][...]
            outs[3][...] = hv.astype(BF16)
            _put_tile(tile, hv)
            for dil, p_ref in zip(DILS, outs[4:4 + len(DILS)]):
                _store_perm(p_ref, tile, dil)

    wspec = pl.BlockSpec((FF, D), lambda i: (0, 0), pipeline_mode=pl.Buffered(1))
    row_d = pl.BlockSpec((tm, D), lambda i: (i, 0))
    row_f = pl.BlockSpec((tm, FF), lambda i: (i, 0))
    in_specs = [row_d, _const_spec((1, D)), wspec, wspec, wspec]
    args = [x, g, wg_t, wu_t, wd]
    f_shape = jax.ShapeDtypeStruct((T, FF), BF16)
    scratch = [pltpu.VMEM((tm, FF), BF16)]
    if loss_of is not None:
        in_specs += [_const_spec((1, D)), row_d]
        args += list(loss_of)
        out_specs = [row_f, row_f, row_d, _const_spec((1, D)), _const_spec((1, 128))]
        out_shape = [f_shape, f_shape, jax.ShapeDtypeStruct((T, D), F32), jax.ShapeDtypeStruct((1, D), F32),
                     jax.ShapeDtypeStruct((1, 128), F32)]
    else:
        out_specs = [row_d, row_f, row_f]
        out_shape = [jax.ShapeDtypeStruct((T, D), F32), f_shape, f_shape]
        if next_gain is not None:
            in_specs.append(_const_spec((1, D)))
            args.append(next_gain)
            out_specs += [row_d] + [_perm_spec(d, D) for d in DILS]
            out_shape += [jax.ShapeDtypeStruct((T, D), BF16)] + [_perm_shape(d, D, BF16) for d in DILS]
            scratch = [_tile_scratch(D)] + scratch
    out = pl.pallas_call(
        body, name=name, grid=(T // tm,), in_specs=in_specs, out_specs=out_specs, out_shape=out_shape,
        scratch_shapes=scratch,
        compiler_params=_cp(("arbitrary",) if loss_of is not None else ("parallel",)),
    )(*args)
    if next_gain is not None:
        return out[0], out[1], out[2], [out[3]] + [o.reshape(T, D) for o in out[4:]]
    return tuple(out)


def _ffn_bwd(x, g, gg_all, uu_all, dout, wg_t, wu_t, wd, name):
    tm, fc = 256, 256
    nc = FF // fc

    def body(x_ref, g_ref, gg_ref, uu_ref, do_ref, wg_ref, wu_ref, wd_ref,
             dx_ref, dgam_ref, dg_ref, du_ref, act_ref, h_ref, db_ref):
        i = pl.program_id(0)
        xv = x_ref[...]
        r = lax.rsqrt(jnp.mean(xv * xv, axis=-1, keepdims=True) + EPS)
        xhat = xv * r
        gam = g_ref[...]
        h_ref[...] = (xhat * gam).astype(BF16)
        dov = do_ref[...]
        dbv = (0.5 * dov).astype(BF16)
        db_ref[...] = dbv
        for c in range(nc):
            sl = pl.ds(c * fc, fc)
            da = _dot(dbv, wd_ref[sl, :], NT)
            gg = gg_ref[:, sl].astype(F32)
            uu = uu_ref[:, sl].astype(F32)
            s = _sig(gg)
            si = gg * s
            dgv = (da * uu * (s * (1.0 + gg * (1.0 - s)))).astype(BF16)
            duv = (da * si).astype(BF16)
            dg_ref[:, sl] = dgv
            du_ref[:, sl] = duv
            act_ref[:, sl] = (si * uu).astype(BF16)
        dh = _dot(dg_ref[...], wg_ref[...], NN) + _dot(du_ref[...], wu_ref[...], NN)

        @pl.when(i == 0)
        def _():
            dgam_ref[...] = jnp.zeros_like(dgam_ref)

        dgam_ref[...] += jnp.sum(dh * xhat, axis=0, keepdims=True)
        dxh = dh * gam
        dx_ref[...] = dov + r * (dxh - xhat * jnp.mean(dxh * xhat, axis=-1, keepdims=True))

    wspec = pl.BlockSpec((FF, D), lambda i: (0, 0), pipeline_mode=pl.Buffered(1))
    row_d = pl.BlockSpec((tm, D), lambda i: (i, 0))
    row_f = pl.BlockSpec((tm, FF), lambda i: (i, 0))
    return pl.pallas_call(
        body, name=name, grid=(T // tm,),
        in_specs=[row_d, _const_spec((1, D)), row_f, row_f, row_d, wspec, wspec, wspec],
        out_specs=[row_d, _const_spec((1, D)), row_f, row_f, row_f, row_d, row_d],
        out_shape=[jax.ShapeDtypeStruct((T, D), F32), jax.ShapeDtypeStruct((1, D), F32),
                   jax.ShapeDtypeStruct((T, FF), BF16), jax.ShapeDtypeStruct((T, FF), BF16),
                   jax.ShapeDtypeStruct((T, FF), BF16), jax.ShapeDtypeStruct((T, D), BF16),
                   jax.ShapeDtypeStruct((T, D), BF16)],
        compiler_params=_cp(("arbitrary",)),
    )(x, g, gg_all, uu_all, dout, wg_t, wu_t, wd)


def _ffn_bwd_pre(x, g, gg_all, uu_all, dout, wd, name):
    tm, fc = 512, 256
    nc = FF // fc

    def body(x_ref, g_ref, gg_ref, uu_ref, do_ref, wd_ref, dg_ref, du_ref, act_ref, h_ref, db_ref):
        xv = x_ref[...]
        r = lax.rsqrt(jnp.mean(xv * xv, axis=-1, keepdims=True) + EPS)
        h_ref[...] = (xv * r * g_ref[...]).astype(BF16)
        dbv = (0.5 * do_ref[...]).astype(BF16)
        db_ref[...] = dbv
        for c in range(nc):
            sl = pl.ds(c * fc, fc)
            da = _dot(dbv, wd_ref[sl, :], NT)
            gg = gg_ref[:, sl].astype(F32)
            uu = uu_ref[:, sl].astype(F32)
            s = _sig(gg)
            si = gg * s
            dg_ref[:, sl] = (da * uu * (s * (1.0 + gg * (1.0 - s)))).astype(BF16)
            du_ref[:, sl] = (da * si).astype(BF16)
            act_ref[:, sl] = (si * uu).astype(BF16)

    wspec = pl.BlockSpec((FF, D), lambda i: (0, 0), pipeline_mode=pl.Buffered(1))
    row_d = pl.BlockSpec((tm, D), lambda i: (i, 0))
    row_f = pl.BlockSpec((tm, FF), lambda i: (i, 0))
    return pl.pallas_call(
        body, name=name, grid=(T // tm,),
        in_specs=[row_d, _const_spec((1, D)), row_f, row_f, row_d, wspec],
        out_specs=[row_f, row_f, row_f, row_d, row_d],
        out_shape=[jax.ShapeDtypeStruct((T, FF), BF16), jax.ShapeDtypeStruct((T, FF), BF16),
                   jax.ShapeDtypeStruct((T, FF), BF16), jax.ShapeDtypeStruct((T, D), BF16),
                   jax.ShapeDtypeStruct((T, D), BF16)],
        compiler_params=_cp(("parallel",)),
    )(x, g, gg_all, uu_all, dout, wd)


def _ffn_bwd_dx(x, g, dgb, dub, dout, wg_t, wu_t, name):
    tm = 512

    def body(x_ref, g_ref, dg_ref, du_ref, do_ref, wg_ref, wu_ref, dx_ref, dgam_ref):
        i = pl.program_id(0)
        xv = x_ref[...]
        r = lax.rsqrt(jnp.mean(xv * xv, axis=-1, keepdims=True) + EPS)
        xhat = xv * r
        gam = g_ref[...]
        dh = _dot(dg_ref[...], wg_ref[...], NN) + _dot(du_ref[...], wu_ref[...], NN)

        @pl.when(i == 0)
        def _():
            dgam_ref[...] = jnp.zeros_like(dgam_ref)

        dgam_ref[...] += jnp.sum(dh * xhat, axis=0, keepdims=True)
        dxh = dh * gam
        dx_ref[...] = do_ref[...] + r * (dxh - xhat * jnp.mean(dxh * xhat, axis=-1, keepdims=True))

    wspec = pl.BlockSpec((FF, D), lambda i: (0, 0), pipeline_mode=pl.Buffered(1))
    row_d = pl.BlockSpec((tm, D), lambda i: (i, 0))
    row_f = pl.BlockSpec((tm, FF), lambda i: (i, 0))
    return pl.pallas_call(
        body, name=name, grid=(T // tm,),
        in_specs=[row_d, _const_spec((1, D)), row_f, row_f, row_d, wspec, wspec],
        out_specs=[row_d, _const_spec((1, D))],
        out_shape=[jax.ShapeDtypeStruct((T, D), F32), jax.ShapeDtypeStruct((1, D), F32)],
        compiler_params=_cp(("arbitrary",)),
    )(x, g, dgb, dub, dout, wg_t, wu_t)


def _wgrad(a, b, m, n, name):
    tm = m // 2 if m == FF else m
    return _mm(a, b, mode="tn", m=m, n=n, k=T, tm=tm, tn=n, tk=min(T, 2048), out_dtype=BF16, name=name)


PERM_TM = 512
DILS = tuple(d for _, d in GROUPS if d > 1)


def _perm_spec(dil, cols):
    return pl.BlockSpec((dil, PERM_TM // dil, cols), lambda i: (0, i, 0))


def _perm_shape(dil, cols, dtype):
    return jax.ShapeDtypeStruct((dil, T // dil, cols), dtype)


LANES = 128


def _tile_scratch(cols):
    return pltpu.VMEM((cols // LANES, PERM_TM, LANES), F32)


def _put_tile(tile, value):
    for c in range(tile.shape[0]):
        tile[c] = value[:, c * LANES:(c + 1) * LANES]


def _get_tile(tile):
    return jnp.concatenate([tile[c] for c in range(tile.shape[0])], axis=1)


def _store_perm(out_ref, tile, dil):
    for r in range(dil):
        for c in range(tile.shape[0]):
            out_ref[r, :, pl.ds(c * LANES, LANES)] = tile[c, pl.ds(r, PERM_TM // dil, stride=dil), :].astype(
                out_ref.dtype)


def _load_unperm(in_ref, tile, dil):
    for r in range(dil):
        for c in range(tile.shape[0]):
            tile[c, pl.ds(r, PERM_TM // dil, stride=dil), :] = in_ref[r, :, pl.ds(c * LANES, LANES)].astype(F32)


def _norm_cast(x, g, name):
    tm = PERM_TM

    def body(x_ref, g_ref, h_ref, *rest):
        p_refs, tile = rest[:-1], rest[-1]
        xv = x_ref[...]
        r = lax.rsqrt(jnp.mean(xv * xv, axis=-1, keepdims=True) + EPS)
        hv = xv * r * g_ref[...]
        h_ref[...] = hv.astype(BF16)
        _put_tile(tile, hv)
        for dil, p_ref in zip(DILS, p_refs):
            _store_perm(p_ref, tile, dil)

    out = pl.pallas_call(
        body, name=name, grid=(T // tm,),
        in_specs=[pl.BlockSpec((tm, D), lambda i: (i, 0)), _const_spec((1, D))],
        out_specs=[pl.BlockSpec((tm, D), lambda i: (i, 0))] + [_perm_spec(d, D) for d in DILS],
        out_shape=[jax.ShapeDtypeStruct((T, D), BF16)] + [_perm_shape(d, D, BF16) for d in DILS],
        scratch_shapes=[_tile_scratch(D)],
        compiler_params=_cp(("parallel",)),
    )(x, g)
    return [out[0]] + [o.reshape(T, D) for o in out[1:]]


def _final_math(xv, gam, tgt, dx_ref, dgam_ref, loss_ref, i):
    r = lax.rsqrt(jnp.mean(xv * xv, axis=-1, keepdims=True) + EPS)
    xhat = xv * r
    err = xhat * gam - tgt
    part = 0.5 * jnp.sum(jnp.mean(err * err, axis=-1, keepdims=True), axis=0, keepdims=True)
    dy = err * (1.0 / D)

    @pl.when(i == 0)
    def _():
        dgam_ref[...] = jnp.zeros_like(dgam_ref)
        loss_ref[...] = jnp.zeros_like(loss_ref)

    dgam_ref[...] += jnp.sum(dy * xhat, axis=0, keepdims=True)
    loss_ref[...] += jnp.broadcast_to(part, loss_ref.shape)
    dxh = dy * gam
    dx_ref[...] = r * (dxh - xhat * jnp.mean(dxh * xhat, axis=-1, keepdims=True))


def _rms_bwd(x, g, dhs, dres, name):
    tm = PERM_TM
    dils = [d for _, d in GROUPS]
    nh = len(dhs)
    assert nh == len(dils)

    def body(*refs):
        x_ref, g_ref = refs[:2]
        dh_refs = refs[2:2 + nh]
        dr_ref, dx_ref, dgam_ref, tile = refs[2 + nh:]
        i = pl.program_id(0)
        xv = x_ref[...]
        r = lax.rsqrt(jnp.mean(xv * xv, axis=-1, keepdims=True) + EPS)
        xhat = xv * r
        gam = g_ref[...]
        dh = None
        for dil, ref in zip(dils, dh_refs):
            if dil == 1:
                part = ref[...]
            else:
                _load_unperm(ref, tile, dil)
                part = _get_tile(tile)
            dh = part if dh is None else dh + part

        @pl.when(i == 0)
        def _():
            dgam_ref[...] = jnp.zeros_like(dgam_ref)

        dgam_ref[...] += jnp.sum(dh * xhat, axis=0, keepdims=True)
        dxh = dh * gam
        dx_ref[...] = dr_ref[...] + r * (dxh - xhat * jnp.mean(dxh * xhat, axis=-1, keepdims=True))

    row_d = pl.BlockSpec((tm, D), lambda i: (i, 0))
    dh_specs = [row_d if d == 1 else _perm_spec(d, D) for d in dils]
    dh_args = [a if d == 1 else a.reshape(d, T // d, D) for d, a in zip(dils, dhs)]
    return pl.pallas_call(
        body, name=name, grid=(T // tm,),
        in_specs=[row_d, _const_spec((1, D))] + dh_specs + [row_d],
        out_specs=[row_d, _const_spec((1, D))],
        out_shape=[jax.ShapeDtypeStruct((T, D), F32), jax.ShapeDtypeStruct((1, D), F32)],
        scratch_shapes=[_tile_scratch(D)],
        compiler_params=_cp(("arbitrary",)),
    )(x, g, *dh_args, dres)


CONV_TM = 256
CONV_HALO = 32
CONV_RB = 16


def _glu(ab):
    ab = ab.astype(F32)
    return ab[:, :D] * _sig(ab[:, D:])


def _ln_stats(z1):
    mu = jnp.mean(z1, axis=-1, keepdims=True)
    zc = z1 - mu
    rstd = lax.rsqrt(jnp.mean(zc * zc, axis=-1, keepdims=True) + EPS)
    return zc * rstd, rstd


def _fill_shifts(zs):
    n = zs.shape[1] - 8
    for s in range(1, 8):
        zs[s, pl.ds(0, n), :] = zs[0, pl.ds(s, n), :]


def _shifted(zs, start, rows):
    q, s = divmod(start, 8)
    return zs[s, pl.ds(8 * q, rows), :]


def _conv_fwd(ab, kern, dwb, lng, lnb, name):
    tm, hl, rb = CONV_TM, CONV_HALO, CONV_RB
    off = hl - (CONV_W - 1)

    def body(ab_ref, abh_ref, k_ref, dwb_ref, lng_ref, lnb_ref, z1_ref, z3_ref, zs):
        i = pl.program_id(0)
        zs[0, pl.ds(0, hl), :] = jnp.where(i > 0, _glu(abh_ref[...]), 0.0)
        zs[0, pl.ds(hl, tm), :] = _glu(ab_ref[...])
        _fill_shifts(zs)
        for b in range(tm // rb):
            acc = jnp.zeros((rb, D), F32)
            for j in range(CONV_W):
                acc = acc + _shifted(zs, b * rb + off + j, rb) * k_ref[pl.ds(j, 1), :]
            z1 = acc + dwb_ref[...]
            z1_ref[pl.ds(b * rb, rb), :] = z1
            zn, _ = _ln_stats(z1)
            z2 = zn * lng_ref[...] + lnb_ref[...]
            z3_ref[pl.ds(b * rb, rb), :] = (z2 * _sig(z2)).astype(BF16)

    row = pl.BlockSpec((tm, D), lambda i: (i, 0))
    return pl.pallas_call(
        body, name=name, grid=(T // tm,),
        in_specs=[pl.BlockSpec((tm, 2 * D), lambda i: (i, 0)),
                  pl.BlockSpec((hl, 2 * D), lambda i: (jnp.maximum(i * (tm // hl) - 1, 0), 0)),
                  _const_spec((32, D)), _const_spec((1, D)), _const_spec((1, D)), _const_spec((1, D))],
        out_specs=[row, row],
        out_shape=[jax.ShapeDtypeStruct((T, D), F32), jax.ShapeDtypeStruct((T, D), BF16)],
        scratch_shapes=[pltpu.VMEM((8, hl + tm, D), F32)],
        compiler_params=_cp(("parallel",)),
    )(ab, ab, kern, dwb, lng, lnb)


GUEST_TM = 256


def _conv_bwd(dz3, z1, ab, kern, lng, lnb, name, guest_lhs=(), guest_rhs=None):
    tm, hl, rb = CONV_TM, CONV_HALO, CONV_RB
    off = hl - (CONV_W - 1)
    nsteps = T // tm
    ng = len(guest_lhs)
    gblocks = [a.shape[1] // GUEST_TM for a in guest_lhs]
    assert all(gb <= nsteps and gb * GUEST_TM == a.shape[1] for gb, a in zip(gblocks, guest_lhs))

    def ln_bwd(dz3v, z1v, lngv, lnbv):
        zn, rstd = _ln_stats(z1v)
        z2 = zn * lngv + lnbv
        s = _sig(z2)
        dz2 = dz3v * (s * (1.0 + z2 * (1.0 - s)))
        dzn = dz2 * lngv
        dz1 = rstd * (dzn - jnp.mean(dzn, axis=-1, keepdims=True)
                      - zn * jnp.mean(dzn * zn, axis=-1, keepdims=True))
        return dz1, dz2, zn

    def body(dz3_ref, dz3h_ref, z1_ref, z1h_ref, ab_ref, abh_ref, k_ref, lng_ref, lnb_ref, *rest):
        g_in, rest = rest[:ng + (1 if ng else 0)], rest[ng + (1 if ng else 0):]
        dab_ref, dk_ref, dvec_ref = rest[:3]
        g_out, (zs, dzs) = rest[3:3 + ng], rest[3 + ng:]
        i = pl.program_id(0)
        lngv, lnbv = lng_ref[...], lnb_ref[...]

        for a_ref, o_ref, gb in zip(g_in[:ng], g_out, gblocks):
            @pl.when(i < gb)
            def _(a_ref=a_ref, o_ref=o_ref):
                o_ref[...] = _dot(a_ref[...], g_in[ng][...], TN).astype(BF16)

        @pl.when(i == 0)
        def _():
            dk_ref[...] = jnp.zeros_like(dk_ref)
            dvec_ref[...] = jnp.zeros_like(dvec_ref)

        dz1, dz2, zn = ln_bwd(dz3_ref[...].astype(F32), z1_ref[...], lngv, lnbv)
        dvec_ref[pl.ds(0, 1), :] += jnp.sum(dz1, axis=0, keepdims=True)
        dvec_ref[pl.ds(1, 1), :] += jnp.sum(dz2 * zn, axis=0, keepdims=True)
        dvec_ref[pl.ds(2, 1), :] += jnp.sum(dz2, axis=0, keepdims=True)
        dzs[0, pl.ds(0, tm), :] = dz1
        dz1h, _, _ = ln_bwd(dz3h_ref[...].astype(F32), z1h_ref[...], lngv, lnbv)
        dzs[0, pl.ds(tm, hl), :] = jnp.where(i < nsteps - 1, dz1h, 0.0)
        _fill_shifts(dzs)
        zs[0, pl.ds(0, hl), :] = jnp.where(i > 0, _glu(abh_ref[...]), 0.0)
        zs[0, pl.ds(hl, tm), :] = _glu(ab_ref[...])
        _fill_shifts(zs)

        for j in range(CONV_W):
            tot = jnp.zeros((rb, D), F32)
            for b in range(tm // rb):
                tot = tot + dzs[0, pl.ds(b * rb, rb), :] * _shifted(zs, b * rb + off + j, rb)
            dk_ref[pl.ds(j, 1), :] += jnp.sum(tot, axis=0, keepdims=True)

        for b in range(tm // rb):
            acc = jnp.zeros((rb, D), F32)
            for j in range(CONV_W):
                acc = acc + _shifted(dzs, b * rb + (CONV_W - 1) - j, rb) * k_ref[pl.ds(j, 1), :]
            av = ab_ref[pl.ds(b * rb, rb), pl.ds(0, D)].astype(F32)
            sb = _sig(ab_ref[pl.ds(b * rb, rb), pl.ds(D, D)].astype(F32))
            dab_ref[pl.ds(b * rb, rb), pl.ds(0, D)] = (acc * sb).astype(BF16)
            dab_ref[pl.ds(b * rb, rb), pl.ds(D, D)] = (acc * av * sb * (1.0 - sb)).astype(BF16)

    row = pl.BlockSpec((tm, D), lambda i: (i, 0))
    nxt = pl.BlockSpec((hl, D), lambda i: (jnp.minimum((i + 1) * (tm // hl), T // hl - 1), 0))
    g_specs, g_args, g_ospecs, g_oshapes = [], [], [], []
    for a, gb in zip(guest_lhs, gblocks):
        g_specs.append(pl.BlockSpec((T, GUEST_TM), lambda i, gb=gb: (0, jnp.minimum(i, gb - 1))))
        g_args.append(a)
        g_ospecs.append(pl.BlockSpec((GUEST_TM, guest_rhs.shape[1]), lambda i, gb=gb: (jnp.minimum(i, gb - 1), 0)))
        g_oshapes.append(jax.ShapeDtypeStruct((a.shape[1], guest_rhs.shape[1]), BF16))
    if ng:
        g_specs.append(pl.BlockSpec(guest_rhs.shape, lambda i: (0, 0), pipeline_mode=pl.Buffered(1)))
        g_args.append(guest_rhs)
    return pl.pallas_call(
        body, name=name, grid=(nsteps,),
        in_specs=[row, nxt, row, nxt,
                  pl.BlockSpec((tm, 2 * D), lambda i: (i, 0)),
                  pl.BlockSpec((hl, 2 * D), lambda i: (jnp.maximum(i * (tm // hl) - 1, 0), 0)),
                  _const_spec((32, D)), _const_spec((1, D)), _const_spec((1, D))] + g_specs,
        out_specs=[pl.BlockSpec((tm, 2 * D), lambda i: (i, 0)), _const_spec((32, D)), _const_spec((8, D))]
        + g_ospecs,
        out_shape=[jax.ShapeDtypeStruct((T, 2 * D), BF16), jax.ShapeDtypeStruct((32, D), F32),
                   jax.ShapeDtypeStruct((8, D), F32)] + g_oshapes,
        scratch_shapes=[pltpu.VMEM((8, hl + tm, D), F32), pltpu.VMEM((8, tm + hl, D), F32)],
        compiler_params=_cp(("arbitrary",)),
    )(dz3, dz3, z1, z1, ab, ab, kern, lng, lnb, *g_args)


def _alibi_slopes():
    h = np.arange(1, 3 * NHG + 1, dtype=np.float32)
    return np.power(np.float32(2.0), -8.0 * h / np.float32(3 * NHG)).astype(np.float32)


def _band_bias(gi):
    _, dil = GROUPS[gi]
    slopes = _alibi_slopes()[gi * NHG:(gi + 1) * NHG]
    qi = np.arange(BLK)[:, None]
    ki = np.arange(2 * BLK)[None, :]
    steps = BLK + qi - ki
    band = (steps >= 0) & (steps <= BLK)
    bias = -slopes[:, None, None] * (dil * steps).astype(np.float32)[None]
    return jnp.asarray(np.where(band[None], bias, np.float32(NEG)).astype(np.float32))


QB_FWD = 8
QB_BWD = 32


def _attn_specs(qb):
    prev = lambda n: jnp.maximum(n * qb - 1, 0)
    return [pl.BlockSpec((qb * BLK, HEAD), lambda h, n: (n, h)),
            pl.BlockSpec((BLK, HEAD), lambda h, n: (prev(n), NHG + h)),
            pl.BlockSpec((qb * BLK, HEAD), lambda h, n: (n, NHG + h)),
            pl.BlockSpec((BLK, HEAD), lambda h, n: (prev(n), 2 * NHG + h)),
            pl.BlockSpec((qb * BLK, HEAD), lambda h, n: (n, 2 * NHG + h)),
            pl.BlockSpec((None, BLK, 2 * BLK), lambda h, n: (h, 0, 0))]


def _scores(q, kcat, bias, blk, seg):
    s = _dot(q, kcat, NT) * (HEAD ** -0.5) + bias
    col = lax.broadcasted_iota(jnp.int32, s.shape, 1)
    first = (blk % seg) == 0
    return jnp.where(jnp.logical_and(first, col < BLK), NEG, s)


def _attn_fwd(qkv, gi, name):
    seg = (T // GROUPS[gi][1]) // BLK

    qb = min(QB_FWD, T // BLK)

    def body(q_ref, kp_ref, kc_ref, vp_ref, vc_ref, bias_ref, o_ref, l_ref):
        n = pl.program_id(0)
        for h in range(NHG):
            cols = pl.ds(h * HEAD, HEAD)
            kwin = jnp.concatenate([kp_ref[:, cols], kc_ref[:, cols]], axis=0)
            vwin = jnp.concatenate([vp_ref[:, cols], vc_ref[:, cols]], axis=0)
            bias = bias_ref[h]
            for b in range(qb):
                rows = pl.ds(b * BLK, BLK)
                s = _scores(q_ref[rows, cols], kwin[b * BLK:(b + 2) * BLK], bias, n * qb + b, seg)
                mx = jnp.max(s, axis=-1, keepdims=True)
                p = jnp.exp(s - mx)
                den = jnp.sum(p, axis=-1, keepdims=True)
                o_ref[rows, cols] = (_dot(p.astype(BF16), vwin[b * BLK:(b + 2) * BLK], NN) / den).astype(BF16)
                l_ref[rows, cols] = jnp.broadcast_to(mx + jnp.log(den), (BLK, HEAD))

    prev = lambda n: jnp.maximum(n * qb - 1, 0)
    cur = lambda part: pl.BlockSpec((qb * BLK, AW), lambda n: (n, part))
    halo = lambda part: pl.BlockSpec((BLK, AW), lambda n: (prev(n), part))
    return pl.pallas_call(
        body, name=name, grid=(T // (qb * BLK),),
        in_specs=[cur(0), halo(1), cur(1), halo(2), cur(2), _const_spec((NHG, BLK, 2 * BLK))],
        out_specs=[cur(0), cur(0)],
        out_shape=[jax.ShapeDtypeStruct((T, AW), BF16), jax.ShapeDtypeStruct((T, AW), F32)],
        compiler_params=_cp(("parallel",)),
    )(qkv, qkv, qkv, qkv, qkv, _band_bias(gi))


def _attn_bwd(qkv, dob, lse, delta, gi, name):
    seg = (T // GROUPS[gi][1]) // BLK
    qb = min(QB_BWD, T // BLK)
    nb = T // (qb * BLK)
    scale = HEAD ** -0.5

    def body(q_ref, kp_ref, kc_ref, vp_ref, vc_ref, bias_ref, do_ref, l_ref, dl_ref, out_ref, dk_acc, dv_acc):
        n = pl.program_id(1)
        kwin = jnp.concatenate([kp_ref[...], kc_ref[...]], axis=0)
        vwin = jnp.concatenate([vp_ref[...], vc_ref[...]], axis=0)
        bias = bias_ref[...]
        dks, dvs = [], []
        for b in range(qb):
            rows = pl.ds(b * BLK, BLK)
            q = q_ref[rows, :]
            kcat = kwin[b * BLK:(b + 2) * BLK]
            s = _scores(q, kcat, bias, n * qb + b, seg)
            p = jnp.exp(s - l_ref[rows, pl.ds(0, 1)])
            dov = do_ref[rows, :]
            dvs.append(_dot(p.astype(BF16), dov, TN))
            dp = _dot(dov, vwin[b * BLK:(b + 2) * BLK], NT)
            dsb = (p * (dp - dl_ref[rows, pl.ds(0, 1)]) * scale).astype(BF16)
            row = pl.ds(pl.multiple_of((n * qb + b) * BLK, BLK), BLK)
            out_ref[0, row, :] = _dot(dsb, kcat, NN).astype(BF16)
            dks.append(_dot(dsb, q, TN))
        for b in range(qb):
            row = pl.ds(pl.multiple_of((n * qb + b) * BLK, BLK), BLK)
            if b + 1 < qb:
                dk_acc[row, :] = dks[b][BLK:] + dks[b + 1][:BLK]
                dv_acc[row, :] = dvs[b][BLK:] + dvs[b + 1][:BLK]
            else:
                dk_acc[row, :] = dks[b][BLK:]
                dv_acc[row, :] = dvs[b][BLK:]

        @pl.when(n > 0)
        def _():
            prow = pl.ds(pl.multiple_of((n * qb - 1) * BLK, BLK), BLK)
            dk_acc[prow, :] += dks[0][:BLK]
            dv_acc[prow, :] += dvs[0][:BLK]

        @pl.when(n == nb - 1)
        def _():
            out_ref[1] = dk_acc[...].astype(BF16)
            out_ref[2] = dv_acc[...].astype(BF16)

    oblk = pl.BlockSpec((qb * BLK, HEAD), lambda h, n: (n, h))
    return pl.pallas_call(
        body, name=name, grid=(NHG, nb),
        in_specs=_attn_specs(qb) + [oblk, oblk, oblk],
        out_specs=pl.BlockSpec((3, T, HEAD), lambda h, n: (0, 0, h)),
        out_shape=jax.ShapeDtypeStruct((3, T, AW), BF16),
        scratch_shapes=[pltpu.VMEM((T, HEAD), F32), pltpu.VMEM((T, HEAD), F32)],
        compiler_params=_cp(("parallel", "arbitrary")),
    )(qkv, qkv, qkv, qkv, qkv, _band_bias(gi), dob, lse, delta)


def _merge(outs, lses, name):
    tm = PERM_TM
    dils = [d for _, d in GROUPS]
    ng = len(dils)

    def body(*refs):
        in_refs = refs[:2 * ng]
        ab_ref = refs[2 * ng]
        lse_refs = refs[2 * ng + 1:3 * ng + 1]
        tile = refs[-1]

        def token_order(ref, dil):
            if dil == 1:
                return ref[...].astype(F32)
            _load_unperm(ref, tile, dil)
            return _get_tile(tile)

        os = [token_order(in_refs[2 * i], d) for i, d in enumerate(dils)]
        ls = [token_order(in_refs[2 * i + 1], d) for i, d in enumerate(dils)]
        mx = jnp.maximum(jnp.maximum(ls[0], ls[1]), ls[2])
        es = [jnp.exp(v - mx) for v in ls]
        tot = es[0] + es[1] + es[2]
        att = (es[0] / tot) * os[0] + (es[1] / tot) * os[1] + (es[2] / tot) * os[2]
        ab_ref[...] = att.astype(BF16)
        lse = mx + jnp.log(tot)
        _put_tile(tile, lse)
        for dil, ref in zip(dils, lse_refs):
            if dil == 1:
                ref[...] = lse
            else:
                _store_perm(ref, tile, dil)

    row = pl.BlockSpec((tm, AW), lambda i: (i, 0))
    specs = [row if d == 1 else _perm_spec(d, AW) for d in dils]
    args = []
    for d, o, l in zip(dils, outs, lses):
        args += [o, l] if d == 1 else [o.reshape(d, T // d, AW), l.reshape(d, T // d, AW)]
    out = pl.pallas_call(
        body, name=name, grid=(T // tm,),
        in_specs=[sp for sp in specs for _ in range(2)], out_specs=[row] + specs,
        out_shape=[jax.ShapeDtypeStruct((T, AW), BF16)]
        + [jax.ShapeDtypeStruct((T, AW), F32) if d == 1 else _perm_shape(d, AW, F32) for d in dils],
        scratch_shapes=[_tile_scratch(AW)],
        compiler_params=_cp(("parallel",)),
    )(*args)
    return out[0], [o.reshape(T, AW) for o in out[1:]]


def _mix_out(z3b, attnb, gates, wc, wa_t, wo, x1, name):
    tm = 512

    def body(z_ref, a_ref, g_ref, wc_ref, wa_ref, wo_ref, x_ref, xo_ref, yc_ref, ya_ref, mx_ref):
        yc = _dot(z_ref[...], wc_ref[...], NN)
        ya = _dot(a_ref[...], wa_ref[...], NT)
        yc_ref[...] = yc.astype(BF16)
        ya_ref[...] = ya.astype(BF16)
        gv = g_ref[...].astype(F32)
        mixed = (_sig(gv[:, :D]) * yc + _sig(gv[:, D:]) * ya).astype(BF16)
        mx_ref[...] = mixed
        xo_ref[...] = x_ref[...] + _dot(mixed, wo_ref[...], NN)

    row = pl.BlockSpec((tm, D), lambda i: (i, 0))
    return pl.pallas_call(
        body, name=name, grid=(T // tm,),
        in_specs=[row, pl.BlockSpec((tm, AW), lambda i: (i, 0)), pl.BlockSpec((tm, 2 * D), lambda i: (i, 0)),
                  _const_spec((D, D)), _const_spec((D, AW)), _const_spec((D, D)), row],
        out_specs=[row, row, row, row],
        out_shape=[jax.ShapeDtypeStruct((T, D), F32), jax.ShapeDtypeStruct((T, D), BF16),
                   jax.ShapeDtypeStruct((T, D), BF16), jax.ShapeDtypeStruct((T, D), BF16)],
        compiler_params=_cp(("parallel",)),
    )(z3b, attnb, gates, wc, wa_t, wo, x1)


def _mix_out_bwd(dx2, gates, yc, ya, attn, wc, wa_t, wo, name):
    tm = PERM_TM
    dils = [d for _, d in GROUPS]
    ng = len(dils)

    def body(dx_ref, g_ref, yc_ref, ya_ref, at_ref, wc_ref, wa_ref, wo_ref,
             dg_ref, dyc_ref, dya_ref, dxb_ref, dz3_ref, *rest):
        dat_refs, dl_refs, tile = rest[:ng], rest[ng:2 * ng], rest[-1]
        dxb = dx_ref[...].astype(BF16)
        dxb_ref[...] = dxb
        dmix = _dot(dxb, wo_ref[...], NT)
        gv = g_ref[...].astype(F32)
        sc = _sig(gv[:, :D])
        sa = _sig(gv[:, D:])
        ycv, yav = yc_ref[...].astype(F32), ya_ref[...].astype(F32)
        dg_ref[:, pl.ds(0, D)] = (dmix * ycv * sc * (1.0 - sc)).astype(BF16)
        dg_ref[:, pl.ds(D, D)] = (dmix * yav * sa * (1.0 - sa)).astype(BF16)
        dyc = (dmix * sc).astype(BF16)
        dya = (dmix * sa).astype(BF16)
        dyc_ref[...] = dyc
        dya_ref[...] = dya
        dz3_ref[...] = _dot(dyc, wc_ref[...], NT).astype(BF16)
        dat = _dot(dya, wa_ref[...], NN)
        prod = dat * at_ref[...].astype(F32)
        delta = jnp.concatenate(
            [jnp.broadcast_to(jnp.sum(prod[:, h * HEAD:(h + 1) * HEAD], axis=-1, keepdims=True), (tm, HEAD))
             for h in range(NHG)], axis=1)
        for value, out_refs in ((dat, dat_refs), (delta, dl_refs)):
            _put_tile(tile, value)
            for dil, ref in zip(dils, out_refs):
                if dil == 1:
                    ref[...] = value.astype(ref.dtype)
                else:
                    _store_perm(ref, tile, dil)

    row = pl.BlockSpec((tm, D), lambda i: (i, 0))
    row2 = pl.BlockSpec((tm, 2 * D), lambda i: (i, 0))
    rowa = pl.BlockSpec((tm, AW), lambda i: (i, 0))
    aspecs = [rowa if d == 1 else _perm_spec(d, AW) for d in dils]

    def ashapes(dtype):
        return [jax.ShapeDtypeStruct((T, AW), dtype) if d == 1 else _perm_shape(d, AW, dtype) for d in dils]

    out = pl.pallas_call(
        body, name=name, grid=(T // tm,),
        in_specs=[row, row2, row, row, rowa, _const_spec((D, D)), _const_spec((D, AW)), _const_spec((D, D))],
        out_specs=[row2, row, row, row, row] + aspecs + aspecs,
        out_shape=[jax.ShapeDtypeStruct((T, 2 * D), BF16), jax.ShapeDtypeStruct((T, D), BF16),
                   jax.ShapeDtypeStruct((T, D), BF16), jax.ShapeDtypeStruct((T, D), BF16),
                   jax.ShapeDtypeStruct((T, D), BF16)] + ashapes(BF16) + ashapes(F32),
        scratch_shapes=[_tile_scratch(AW)],
        compiler_params=_cp(("parallel",)),
    )(dx2, gates, yc, ya, attn, wc, wa_t, wo)
    dats = [o.reshape(T, AW) for o in out[5:5 + ng]]
    deltas = [o.reshape(T, AW) for o in out[5 + ng:5 + 2 * ng]]
    return out[0], out[1], out[2], out[3], out[4], dats, deltas


def _peer(k):
    x, y, c = lax.axis_index("x"), lax.axis_index("y"), lax.axis_index("c")
    px = 1 - x if k & 4 else x
    py = 1 - y if k & 2 else y
    pc = 1 - c if k & 1 else c
    return (px, py, pc), 4 * px + 2 * py + pc


HBM_SPEC = pl.BlockSpec(memory_space=pltpu.HBM)
SEM_SPEC = pl.BlockSpec(memory_space=pltpu.SEMAPHORE)
EFFECT = pltpu.SideEffectType.DATAFLOW_SIDE_EFFECTING


def _my_place():
    return 4 * lax.axis_index("x") + 2 * lax.axis_index("y") + lax.axis_index("c")


def _tie(a, order_after, name):
    na = len(order_after)

    def body(*refs):
        del refs

    return pl.pallas_call(
        body, name=name, in_specs=[pl.BlockSpec(memory_space=pl.ANY)] * (1 + na),
        out_specs=pl.BlockSpec(memory_space=pl.ANY), out_shape=jax.ShapeDtypeStruct(a.shape, a.dtype),
        input_output_aliases={0: 0},
    )(a, *order_after)


def _prep_gather(ws, order_after, name):
    me = jnp.reshape(_my_place(), (1,)).astype(jnp.int32)
    n = len(ws)
    na = len(order_after)
    shapes = [((32, wv.shape[1]), F32) if wv.shape[0] == CONV_W else (wv.shape, BF16) for wv in ws]

    def body(me_ref, *refs):
        del me_ref
        ins, outs = refs[:n], refs[n + na:]
        for wv, i_ref, o_ref in zip(ws, ins, outs):
            if wv.shape[0] == CONV_W:
                o_ref[pl.ds(0, CONV_W), :] = i_ref[...]
                o_ref[pl.ds(CONV_W, 1), :] = jnp.zeros((1, wv.shape[1]), F32)
            else:
                o_ref[...] = i_ref[...].astype(BF16)

    grid_spec = pltpu.PrefetchScalarGridSpec(
        num_scalar_prefetch=1, grid=(1,),
        in_specs=[pl.BlockSpec(wv.shape, lambda i, m: (0, 0)) for wv in ws]
        + [pl.BlockSpec(memory_space=pl.ANY)] * na,
        out_specs=[pl.BlockSpec(shp, lambda i, m: (m[0], 0)) for shp, _ in shapes])
    return pl.pallas_call(
        body, name=name, grid_spec=grid_spec,
        out_shape=[jax.ShapeDtypeStruct((NDEV * shp[0], shp[1]), dt) for shp, dt in shapes],
        compiler_params=_cp(("arbitrary",)),
    )(me, *ws, *order_after)


GATHER_A = ((1, 0), (2, 0), (4, 0), (6, 0))
GATHER_B = ((1, 2), (1, 4), (1, 6))


def _gather_start(lands, plan, order_after, name):
    n = len(lands)
    na = len(order_after)
    npl = len(plan)

    def body(*refs):
        land_refs = refs[:n]
        send, recv = refs[n + na], refs[n + na + 1]
        token = refs[-1]
        for w in range(n):
            rows = lands[w].shape[0] // NDEV
            for p, (k, j) in enumerate(plan):
                peer, _ = _peer(k)
                _, blk = _peer(j)
                part = land_refs[w].at[pl.ds(blk * rows, rows)]
                i = w * npl + p
                pltpu.make_async_remote_copy(src_ref=part, dst_ref=part, send_sem=send.at[i], recv_sem=recv.at[i],
                                             device_id=peer, device_id_type=MESH_ID).start()
        token[...] = jnp.zeros_like(token)

    nsem = n * npl
    bufs = [pltpu.with_memory_space_constraint(a, pltpu.HBM) for a in lands]
    out = pl.pallas_call(
        body, name=name,
        in_specs=[HBM_SPEC] * n + [pl.BlockSpec(memory_space=pl.ANY)] * na,
        out_specs=[SEM_SPEC, SEM_SPEC] + [HBM_SPEC] * n + [pl.BlockSpec(memory_space=pltpu.VMEM)],
        out_shape=[pltpu.SemaphoreType.DMA((nsem,)), pltpu.SemaphoreType.DMA((nsem,))]
        + [pltpu.HBM(a.shape, a.dtype) for a in bufs] + [jax.ShapeDtypeStruct((8, 128), F32)],
        input_output_aliases={i: 2 + i for i in range(n)},
        compiler_params=pltpu.CompilerParams(has_side_effects=EFFECT),
    )(*bufs, *order_after)
    return out[0], out[1], out[2:2 + n], out[-1]


def _gather_wait(started, plan, order_after, name):
    send, recv, lands, _ = started
    n = len(lands)
    na = len(order_after)
    npl = len(plan)

    def body(*refs):
        land_refs = refs[:n]
        send_ref, recv_ref = refs[n], refs[n + 1]
        for w in range(n):
            rows = lands[w].shape[0] // NDEV
            for p, (k, j) in enumerate(plan):
                peer, _ = _peer(k)
                _, blk = _peer(j)
                part = land_refs[w].at[pl.ds(blk * rows, rows)]
                i = w * npl + p
                cp = pltpu.make_async_remote_copy(src_ref=part, dst_ref=part, send_sem=send_ref.at[i],
                                                  recv_sem=recv_ref.at[i], device_id=peer, device_id_type=MESH_ID)
                cp.wait_send()
                cp.wait_recv()

    out = pl.pallas_call(
        body, name=name,
        in_specs=[HBM_SPEC] * n + [SEM_SPEC, SEM_SPEC] + [pl.BlockSpec(memory_space=pl.ANY)] * na,
        out_specs=[HBM_SPEC] * n,
        out_shape=[pltpu.HBM(a.shape, a.dtype) for a in lands],
        input_output_aliases={i: i for i in range(n)},
        compiler_params=pltpu.CompilerParams(has_side_effects=EFFECT),
    )(*lands, send, recv, *order_after)
    return list(out)


def _copy_ends(kind, src, land, me, plin, k):
    if kind == "scatter":
        rows = src.shape[0] // NDEV
        return src.at[pl.ds(plin * rows, rows)], land.at[k - 1]
    return src, land.at[me]


def _landing(kind, src):
    me = _my_place()
    if kind == "scatter":
        return lax.empty((NDEV - 1, src.shape[0] // NDEV) + src.shape[1:], src.dtype)
    land = lax.empty((NDEV,) + src.shape, src.dtype)
    return lax.dynamic_update_slice(land, src[None], (me,) + (0,) * src.ndim)


def _send_start(kinds, srcs, order_after, name):
    n = len(srcs)
    lands = [_landing(kd, s) for kd, s in zip(kinds, srcs)]
    na = len(order_after)

    def body(*refs):
        src_refs, land_refs = refs[:n], refs[n:2 * n]
        send, recv = refs[2 * n + na], refs[2 * n + na + 1]
        token = refs[-1]
        _, me = _peer(0)
        for w in range(n):
            for k in range(1, NDEV):
                peer, plin = _peer(k)
                s, d = _copy_ends(kinds[w], src_refs[w], land_refs[w], me, plin, k)
                i = w * (NDEV - 1) + k - 1
                pltpu.make_async_remote_copy(src_ref=s, dst_ref=d, send_sem=send.at[i], recv_sem=recv.at[i],
                                             device_id=peer, device_id_type=MESH_ID).start()
        token[...] = jnp.zeros_like(token)

    nsem = n * (NDEV - 1)
    bufs = [pltpu.with_memory_space_constraint(a, pltpu.HBM) for a in list(srcs) + lands]
    out = pl.pallas_call(
        body, name=name,
        in_specs=[HBM_SPEC] * (2 * n) + [pl.BlockSpec(memory_space=pl.ANY)] * na,
        out_specs=[SEM_SPEC, SEM_SPEC] + [HBM_SPEC] * (2 * n) + [pl.BlockSpec(memory_space=pltpu.VMEM)],
        out_shape=[pltpu.SemaphoreType.DMA((nsem,)), pltpu.SemaphoreType.DMA((nsem,))]
        + [pltpu.HBM(a.shape, a.dtype) for a in bufs] + [jax.ShapeDtypeStruct((8, 128), F32)],
        input_output_aliases={i: 2 + i for i in range(2 * n)},
        compiler_params=pltpu.CompilerParams(has_side_effects=EFFECT),
    )(*bufs, *order_after)
    return out[0], out[1], out[2:2 + n], out[2 + n:2 + 2 * n], out[-1]


def _send_wait(kinds, started, order_after, name):
    send, recv, srcs, lands, _ = started
    n = len(srcs)
    na = len(order_after)

    def body(*refs):
        src_refs, land_refs = refs[:n], refs[n:2 * n]
        send_ref, recv_ref = refs[2 * n], refs[2 * n + 1]
        _, me = _peer(0)
        for w in range(n):
            for k in range(1, NDEV):
                peer, plin = _peer(k)
                s, d = _copy_ends(kinds[w], src_refs[w], land_refs[w], me, plin, k)
                i = w * (NDEV - 1) + k - 1
                cp = pltpu.make_async_remote_copy(src_ref=s, dst_ref=d, send_sem=send_ref.at[i],
                                                  recv_sem=recv_ref.at[i], device_id=peer, device_id_type=MESH_ID)
                cp.wait_send()
                cp.wait_recv()

    bufs = list(srcs) + list(lands)
    out = pl.pallas_call(
        body, name=name,
        in_specs=[HBM_SPEC] * (2 * n) + [SEM_SPEC, SEM_SPEC] + [pl.BlockSpec(memory_space=pl.ANY)] * na,
        out_specs=[HBM_SPEC] * (2 * n),
        out_shape=[pltpu.HBM(a.shape, a.dtype) for a in bufs],
        input_output_aliases={i: i for i in range(2 * n)},
        compiler_params=pltpu.CompilerParams(has_side_effects=EFFECT),
    )(*bufs, send, recv, *order_after)
    return out[:n], out[n:]


def _gsum(own, land, name):
    rows, cols = own.shape
    tr = rows // 2 if rows * cols > 512 * 1024 and rows % 32 == 0 else rows

    def body(own_ref, l_ref, o_ref):
        tot = own_ref[...].astype(F32)
        for s in range(NDEV - 1):
            tot = tot + l_ref[s].astype(F32)
        o_ref[...] = tot

    return pl.pallas_call(
        body, name=name, grid=(rows // tr,),
        in_specs=[pl.BlockSpec((tr, cols), lambda i: (i, 0)),
                  pl.BlockSpec((NDEV - 1, tr, cols), lambda i: (0, i, 0))],
        out_specs=pl.BlockSpec((tr, cols), lambda i: (i, 0)),
        out_shape=jax.ShapeDtypeStruct((rows, cols), F32),
        compiler_params=_cp(("parallel",)),
    )(own, land)


def _adamw_math(w, g, m, v):
    m2 = B1 * m + (1.0 - B1) * g
    v2 = B2 * v + (1.0 - B2) * (g * g)
    m_hat = m2 / (1.0 - B1 ** STEP)
    v_hat = v2 / (1.0 - B2 ** STEP)
    delta = -LR * (m_hat / (jnp.sqrt(v_hat) + AEPS) + WD * w)
    return delta, m2, v2


def _adamw(w, g, m, v, name):
    rows, cols = w.shape
    tr = 256 if rows % 256 == 0 and rows > 256 else rows

    def body(w_ref, g_ref, m_ref, v_ref, d_ref, mo_ref, vo_ref):
        d, m2, v2 = _adamw_math(w_ref[...], g_ref[...], m_ref[...], v_ref[...])
        d_ref[...] = d
        mo_ref[...] = m2
        vo_ref[...] = v2

    blk = pl.BlockSpec((tr, cols), lambda i: (i, 0))
    return pl.pallas_call(
        body, name=name, grid=(rows // tr,), in_specs=[blk] * 4, out_specs=[blk] * 3,
        out_shape=[jax.ShapeDtypeStruct((rows, cols), F32)] * 3,
        compiler_params=_cp(("parallel",)),
    )(w, g, m, v)


UPD_TC = 256


def _update(src, land, w, m, v, name):
    rows, cols = land.shape[1:]
    tc = min(UPD_TC if rows > 512 else 2 * UPD_TC, cols)
    me = jnp.reshape(_my_place(), (1,)).astype(jnp.int32)

    def body(me_ref, own_ref, l_ref, w_ref, m_ref, v_ref, g_ref, d_ref, mo_ref, vo_ref):
        del me_ref
        g = own_ref[...].astype(F32)
        for s in range(NDEV - 1):
            g = g + l_ref[s].astype(F32)
        g_ref[...] = g
        d, m2, v2 = _adamw_math(w_ref[...], g, m_ref[...], v_ref[...])
        d_ref[...] = d
        mo_ref[...] = m2
        vo_ref[...] = v2

    wblk = pl.BlockSpec((rows, tc), lambda j, p: (0, j))
    grid_spec = pltpu.PrefetchScalarGridSpec(
        num_scalar_prefetch=1, grid=(cols // tc,),
        in_specs=[pl.BlockSpec((rows, tc), lambda j, p: (p[0], j)),
                  pl.BlockSpec((NDEV - 1, rows, tc), lambda j, p: (0, 0, j)), wblk, wblk, wblk],
        out_specs=[wblk] * 4)
    return pl.pallas_call(
        body, name=name, grid_spec=grid_spec, out_shape=[jax.ShapeDtypeStruct((rows, cols), F32)] * 4,
        compiler_params=_cp(("parallel",)),
    )(me, src, land, w, m, v)


def _small_update(vland, w8, m8, v8, name):
    def body(l_ref, w_ref, m_ref, v_ref, g_ref, d_ref, mo_ref, vo_ref):
        g = l_ref[0]
        for s in range(1, NDEV):
            g = g + l_ref[s]
        g_ref[...] = g
        d, m2, v2 = _adamw_math(w_ref[...], g, m_ref[...], v_ref[...])
        d_ref[...] = d
        mo_ref[...] = m2
        vo_ref[...] = v2

    return pl.pallas_call(
        body, name=name, out_shape=[jax.ShapeDtypeStruct((8, D), F32)] * 4,
        compiler_params=_cp(None),
    )(vland, w8, m8, v8)


def kernel(x, ffn1_norm, ffn1_w_gate, ffn1_w_up, ffn1_w_down, mix_norm, w_in, conv_dw_kernel, conv_dw_bias, conv_ln_gain, conv_ln_bias, conv_w_out, attn_w_out, w_o, ffn2_norm, ffn2_w_gate, ffn2_w_up, ffn2_w_down, final_norm, loss_target, m_ffn1_norm, m_ffn1_w_gate, m_ffn1_w_up, m_ffn1_w_down, m_mix_norm, m_w_in, m_conv_dw_kernel, m_conv_dw_bias, m_conv_ln_gain, m_conv_ln_bias, m_conv_w_out, m_attn_w_out, m_w_o, m_ffn2_norm, m_ffn2_w_gate, m_ffn2_w_up, m_ffn2_w_down, m_final_norm, v_ffn1_norm, v_ffn1_w_gate, v_ffn1_w_up, v_ffn1_w_down, v_mix_norm, v_w_in, v_conv_dw_kernel, v_conv_dw_bias, v_conv_ln_gain, v_conv_ln_bias, v_conv_w_out, v_attn_w_out, v_w_o, v_ffn2_norm, v_ffn2_w_gate, v_ffn2_w_up, v_ffn2_w_down, v_final_norm):
    names = ["ffn1_norm", "ffn1_w_gate", "ffn1_w_up", "ffn1_w_down", "mix_norm", "w_in", "conv_dw_kernel",
             "conv_dw_bias", "conv_ln_gain", "conv_ln_bias", "conv_w_out", "attn_w_out", "w_o", "ffn2_norm",
             "ffn2_w_gate", "ffn2_w_up", "ffn2_w_down", "final_norm"]
    w = dict(ffn1_norm=ffn1_norm, ffn1_w_gate=ffn1_w_gate, ffn1_w_up=ffn1_w_up, ffn1_w_down=ffn1_w_down, mix_norm=mix_norm, w_in=w_in, conv_dw_kernel=conv_dw_kernel, conv_dw_bias=conv_dw_bias, conv_ln_gain=conv_ln_gain, conv_ln_bias=conv_ln_bias, conv_w_out=conv_w_out, attn_w_out=attn_w_out, w_o=w_o, ffn2_norm=ffn2_norm, ffn2_w_gate=ffn2_w_gate, ffn2_w_up=ffn2_w_up, ffn2_w_down=ffn2_w_down, final_norm=final_norm)
    mo = dict(ffn1_norm=m_ffn1_norm, ffn1_w_gate=m_ffn1_w_gate, ffn1_w_up=m_ffn1_w_up, ffn1_w_down=m_ffn1_w_down, mix_norm=m_mix_norm, w_in=m_w_in, conv_dw_kernel=m_conv_dw_kernel, conv_dw_bias=m_conv_dw_bias, conv_ln_gain=m_conv_ln_gain, conv_ln_bias=m_conv_ln_bias, conv_w_out=m_conv_w_out, attn_w_out=m_attn_w_out, w_o=m_w_o, ffn2_norm=m_ffn2_norm, ffn2_w_gate=m_ffn2_w_gate, ffn2_w_up=m_ffn2_w_up, ffn2_w_down=m_ffn2_w_down, final_norm=m_final_norm)
    vo = dict(ffn1_norm=v_ffn1_norm, ffn1_w_gate=v_ffn1_w_gate, ffn1_w_up=v_ffn1_w_up, ffn1_w_down=v_ffn1_w_down, mix_norm=v_mix_norm, w_in=v_w_in, conv_dw_kernel=v_conv_dw_kernel, conv_dw_bias=v_conv_dw_bias, conv_ln_gain=v_conv_ln_gain, conv_ln_bias=v_conv_ln_bias, conv_w_out=v_conv_w_out, attn_w_out=v_attn_w_out, w_o=v_w_o, ffn2_norm=v_ffn2_norm, ffn2_w_gate=v_ffn2_w_gate, ffn2_w_up=v_ffn2_w_up, ffn2_w_down=v_ffn2_w_down, final_norm=v_final_norm)
    col_sharded = ("ffn1_w_gate", "ffn1_w_up", "w_in", "attn_w_out", "ffn2_w_gate", "ffn2_w_up")
    row_sharded = ("ffn1_w_down", "conv_w_out", "w_o", "ffn2_w_down")
    small = ("ffn1_norm", "mix_norm", "ffn2_norm", "final_norm", "conv_dw_bias", "conv_ln_gain", "conv_ln_bias")

    def landing_view(a, n):
        return jnp.transpose(a[0]) if n in col_sharded else a[0]

    def own_view(a, n):
        return jnp.transpose(a)[None] if n in col_sharded else a[None]

    ag_groups = (("ffn1_w_gate", "ffn1_w_up", "ffn1_w_down"),
                 ("w_in", "attn_w_out", "conv_w_out", "w_o", "conv_dw_kernel"),
                 ("ffn2_w_gate", "ffn2_w_up", "ffn2_w_down"))
    ag, order = [], []
    for gi, grp in enumerate(ag_groups):
        lands = _prep_gather([landing_view(w[n], n) for n in grp], order, f"gather_prep{gi}")
        st = _gather_start(lands, GATHER_A, [], f"gather_a_start{gi}")
        ag.append(st)
        order = [st[3]]

    def chips_in(gi, after):
        lands = _gather_wait(ag[gi], GATHER_A, after, f"gather_a_wait{gi}")
        return _gather_start(lands, GATHER_B, [], f"gather_b_start{gi}")

    def all_in(gi, st, after):
        return _gather_wait(st, GATHER_B, after, f"gather_b_wait{gi}")

    x0 = x[0]
    tgt = loss_target[0]
    gf = final_norm.reshape(1, D)

    wg1, wu1, wd1 = all_in(0, chips_in(0, [ag[2][3]]), [])
    x1, gg1, uu1, h2p = _ffn_fwd(x0, ffn1_norm, wg1, wu1, wd1, "ffn1_fwd", next_gain=mix_norm)
    h2 = h2p[0]
    win_t, wa_t, wc, wo, kern_blocks = all_in(1, chips_in(1, [x1]), [])
    kern = kern_blocks.reshape(NDEV, 32, D // NDEV).transpose(1, 0, 2).reshape(32, D)
    ptm = min(T, 2048)
    ab = _mm(h2, win_t, mode="nt", m=T, n=2 * D, k=D, tm=ptm, tn=512, tk=D, out_dtype=BF16, name="proj_conv")
    gates = _mm(h2, win_t, mode="nt", m=T, n=2 * D, k=D, tm=ptm, tn=512, tk=D, out_dtype=BF16,
                b_map=lambda i, j, kk: (13 + j, 0), name="proj_gates")
    qkv = []
    for gi in range(len(GROUPS)):
        qkv.append(_mm(h2p[gi], win_t, mode="nt", m=T, n=3 * AW, k=D, tm=ptm, tn=AW, tk=D, out_dtype=BF16,
                       b_map=lambda i, j, kk, gi=gi: (4 + gi + 3 * j, 0), name=f"proj_qkv{gi}"))
    z1, z3b = _conv_fwd(ab, kern, conv_dw_bias, conv_ln_gain, conv_ln_bias, "conv_fwd")
    ffn2_b = chips_in(2, [z3b])
    outs, lses = [], []
    for gi, (_, dil) in enumerate(GROUPS):
        o, l = _attn_fwd(qkv[gi], gi, f"attn_fwd{gi}")
        outs.append(o)
        lses.append(l)
    attnb, lse = _merge(outs, lses, "attn_merge")
    x2, yc, ya, mixedb = _mix_out(z3b, attnb, gates, wc, wa_t, wo, x1, "mix_out_fwd")
    wg2, wu2, wd2 = all_in(2, ffn2_b, [x2])
    gg2, uu2, dx3, dgf, loss_part = _ffn_fwd(x2, ffn2_norm, wg2, wu2, wd2, "ffn2_fwd", loss_of=(gf, tgt))

    dx2, dg3, dgb, dub, actb, hb, dob = _ffn_bwd(x2, ffn2_norm, gg2, uu2, dx3, wg2, wu2, wd2, "ffn2_bwd")
    grads = {}
    grads["ffn2_w_down"] = _wgrad(actb, dob, FF, D, "ffn2_dwd")
    rs_groups = [("ffn2_w_gate", "ffn2_w_up", "ffn2_w_down"),
                 ("attn_w_out", "conv_w_out", "w_o", "conv_dw_kernel"),
                 ("w_in",),
                 ("ffn1_w_gate",), ("ffn1_w_up",), ("ffn1_w_down",), ()]
    last = len(rs_groups) - 1
    rs = []

    dgates, dycb, dyab, dx2b, dz3, dattnb, delta = _mix_out_bwd(dx2, gates, yc, ya, attnb, wc, wa_t, wo, "mix_out_bwd")
    grads["w_o"] = _wgrad(mixedb, dx2b, D, D, "dw_o")
    grads["conv_w_out"] = _wgrad(z3b, dycb, D, D, "dw_conv_out")
    grads["attn_w_out"] = _wgrad(dyab, attnb, D, AW, "dw_attn_out")
    dab, dkern, dvec, grads["ffn2_w_gate"], grads["ffn2_w_up"] = _conv_bwd(
        dz3, z1, ab, kern, conv_ln_gain, conv_ln_bias, "conv_bwd", guest_lhs=(dgb, dub), guest_rhs=hb)
    grads["conv_dw_kernel"] = dkern.reshape(32, NDEV, D // NDEV).transpose(1, 0, 2).reshape(NDEV * 32, D // NDEV)
    rs.append(_send_start(["scatter"] * 3, [grads[n] for n in rs_groups[0]], [], "scatter_start0"))
    rs.append(_send_start(["scatter"] * 4, [grads[n] for n in rs_groups[1]], [rs[0][4]], "scatter_start1"))
    dattnb = [_tie(a, [rs[1][4]], f"tie_after_scatter1_{i}") for i, a in enumerate(dattnb)]

    dqkv = []
    for gi, (_, dil) in enumerate(GROUPS):
        dq3 = _attn_bwd(qkv[gi], dattnb[gi], lse[gi], delta[gi], gi, f"attn_bwd{gi}")
        dqkv.append(dq3.reshape(3 * T, AW))

    wtk = min(T, 2048)
    dwin = _mm(dab, h2, mode="tn", m=2 * D, n=D, k=T, tm=2 * D, tn=D, tk=wtk, out_dtype=BF16, out_rows=IN_W,
               name="dw_in_conv")
    dwin = _mm(dgates, h2, mode="tn", m=2 * D, n=D, k=T, tm=512, tn=D, tk=wtk, out_dtype=BF16, out_rows=IN_W,
               o_map=lambda i, j, kk: (13 + i, 0), passthru=dwin, name="dw_in_gates")
    for gi in range(3):
        dwin = _mm(dqkv[gi], h2p[gi], mode="tn", m=3 * AW, n=D, k=T, tm=AW, tn=D, tk=wtk, out_dtype=BF16,
                   out_rows=IN_W, a_map=lambda i, j, kk: (i * (T // wtk) + kk, 0),
                   o_map=lambda i, j, kk, gi=gi: (4 + gi + 3 * i, 0), passthru=dwin, name=f"dw_in_qkv{gi}")
    grads["w_in"] = dwin
    rs.append(_send_start(["scatter"], [dwin], [rs[1][4]], "scatter_start2"))
    dab = _tie(dab, [rs[2][4]], "tie_after_scatter2")

    nrow = T // 1024
    dh = _mm(dab, win_t, mode="nn", m=T, n=D, k=2 * D, tm=1024, tn=D, tk=2 * D, out_dtype=F32, name="dproj_conv")
    dh = _mm(dgates, win_t[IN_W - 2 * D:], mode="nn", m=T, n=D, k=2 * D, tm=1024, tn=D, tk=2 * D, out_dtype=F32,
             init=dh, name="dproj_gates")
    dhs = []
    for gi, (_, dil) in enumerate(GROUPS):
        part = _mm(dqkv[gi], win_t, mode="nn", m=T, n=D, k=3 * AW, tm=1024, tn=D, tk=AW,
                   out_dtype=F32 if gi == 0 else BF16,
                   a_map=lambda i, j, kk: (kk * nrow + i, 0), b_map=lambda i, j, kk, gi=gi: (4 + gi + 3 * kk, 0),
                   init=dh if gi == 0 else None, name=f"dproj_qkv{gi}")
        dhs.append(part)
    dx1, dg2 = _rms_bwd(x1, mix_norm, dhs, dx2, "mix_norm_bwd")

    dgb, dub, actb, hb, dob = _ffn_bwd_pre(x0, ffn1_norm, gg1, uu1, dx1, wd1, "ffn1_bwd_pre")
    grads["ffn1_w_gate"] = _wgrad(dgb, hb, FF, D, "ffn1_dwg")
    rs.append(_send_start(["scatter"], [grads["ffn1_w_gate"]], [rs[2][4]], "scatter_start3"))
    hb = _tie(hb, [rs[3][4]], "tie_after_scatter3")
    grads["ffn1_w_up"] = _wgrad(dub, hb, FF, D, "ffn1_dwu")
    rs.append(_send_start(["scatter"], [grads["ffn1_w_up"]], [rs[3][4]], "scatter_start4"))
    dob = _tie(dob, [rs[4][4]], "tie_after_scatter4")
    grads["ffn1_w_down"] = _wgrad(actb, dob, FF, D, "ffn1_dwd")
    rs.append(_send_start(["scatter"], [grads["ffn1_w_down"]], [rs[4][4]], "scatter_start5"))
    dgb = _tie(dgb, [rs[5][4]], "tie_after_scatter5")
    dx0, dg1 = _ffn_bwd_dx(x0, ffn1_norm, dgb, dub, dx1, wg1, wu1, "ffn1_bwd_dx")
    vec = jnp.concatenate([dg1, dg2, dg3, dgf, dvec[0:3], jnp.broadcast_to(loss_part[:, :1], (1, D))], axis=0)
    rs.append(_send_start(["bcast"], [vec], [rs[5][4]], "scatter_start6"))

    g_out, d_out, m_out, v_out = {}, {}, {}, {}
    me = _my_place()
    after = [rs[last][4]]
    for gi, grp in enumerate(rs_groups):
        kinds = ["scatter"] * len(grp) + (["bcast"] if gi == last else [])
        srcs, lands = _send_wait(kinds, rs[gi], after, f"scatter_wait{gi}")
        for n, src, land in zip(grp, srcs, lands):
            if n == "conv_dw_kernel":
                rows = src.shape[0] // NDEV
                own = lax.dynamic_slice(src, (me * rows, 0), (rows, src.shape[1]))
                g = _gsum(own, land, f"gsum_{n}")[:CONV_W]
                d, m2, v2 = _adamw(w[n][0], g, mo[n][0], vo[n][0], f"adamw_{n}")
                after = [d]
                g, d, m2, v2 = g[None], d[None], m2[None], v2[None]
            else:
                res = _update(src, land, landing_view(w[n], n), landing_view(mo[n], n), landing_view(vo[n], n),
                              f"update_{n}")
                after = [res[1]]
                g, d, m2, v2 = (own_view(a, n) for a in res)
            g_out[n], d_out[n], m_out[n], v_out[n] = g, d, m2, v2
    vland = lands[-1]

    def rows8(src):
        return jnp.concatenate([src[n].reshape(1, D) for n in small] + [jnp.ones((1, D), F32)], axis=0)

    g8, d8, m8, v8 = _small_update(vland, rows8(w), rows8(mo), rows8(vo), "small_update")
    for r, n in enumerate(small):
        shp = w[n].shape
        g_out[n], d_out[n], m_out[n], v_out[n] = (a[r].reshape(shp) for a in (g8, d8, m8, v8))
    loss = g8[7, 0]

    return (loss, dx0[None], *[g_out[n] for n in names], *[d_out[n] for n in names],
            *[m_out[n] for n in names], *[v_out[n] for n in names])
```

```python
import numpy as np
import jax
import jax.numpy as jnp
from jax import lax
from jax.experimental import pallas as pl
from jax.experimental.pallas import tpu as pltpu

F32 = jnp.float32
BF16 = jnp.bfloat16

T = 4096
D = 1024
FF = 2816
NDEV = 8
CONV_W = 31
HEAD = 128
BLK = 128
GROUPS = ((128, 1), (512, 4), (2048, 16))
NHG = 4
AW = NHG * HEAD
IN_W = 2 * D + 3 * 3 * AW + 2 * D
EPS = 1e-6
B1, B2, LR, AEPS, WD, STEP = 0.9, 0.999, 0.001, 1e-08, 0.01, 10
NEG = -1e30
VMEM_LIMIT = 56 * 1024 * 1024
MESH_ID = pl.DeviceIdType.MESH

NT = (((1,), (1,)), ((), ()))
NN = (((1,), (0,)), ((), ()))
TN = (((0,), (0,)), ((), ()))
_DIMS = {"nn": NN, "nt": NT, "tn": TN}


def _cp(sem=None):
    return pltpu.CompilerParams(dimension_semantics=sem, vmem_limit_bytes=VMEM_LIMIT)


def _sig(v):
    return 1.0 / (1.0 + jnp.exp(-v))


def _dot(a, b, dims):
    return lax.dot_general(a, b, dims, preferred_element_type=F32)


def _const_spec(shape):
    nd = len(shape)
    return pl.BlockSpec(shape, lambda *_: (0,) * nd)


def _mm(a, b, *, mode, m, n, k, tm, tn, tk, out_dtype, name, a_map=None, b_map=None,
        o_map=None, out_rows=None, init=None, passthru=None):
    gi, gj, gk = m // tm, n // tn, k // tk
    assert gi * tm == m and gj * tn == n and gk * tk == k, (name, m, n, k, tm, tn, tk)
    if mode == "nn":
        a_blk, b_blk = (tm, tk), (tk, tn)
        da, db = (lambda i, j, kk: (i, kk)), (lambda i, j, kk: (kk, j))
    elif mode == "nt":
        a_blk, b_blk = (tm, tk), (tn, tk)
        da, db = (lambda i, j, kk: (i, kk)), (lambda i, j, kk: (j, kk))
    else:
        a_blk, b_blk = (tk, tm), (tk, tn)
        da, db = (lambda i, j, kk: (kk, i)), (lambda i, j, kk: (kk, j))
    a_map = a_map or da
    b_map = b_map or db
    o_map = o_map or (lambda i, j, kk: (i, j))
    dims = _DIMS[mode]
    extra = init if init is not None else passthru
    out_rows = out_rows or m

    def body(*refs):
        if init is not None:
            a_ref, b_ref, i_ref, o_ref = refs[:4]
        elif passthru is not None:
            a_ref, b_ref, _, o_ref = refs[:4]
        else:
            a_ref, b_ref, o_ref = refs[:3]
        if gk == 1:
            prod = _dot(a_ref[...], b_ref[...], dims)
            if init is not None:
                prod = prod + i_ref[...].astype(F32)
            o_ref[...] = prod.astype(out_dtype)
            return
        acc = refs[-1]
        kk = pl.program_id(2)

        @pl.when(kk == 0)
        def _():
            if init is not None:
                acc[...] = i_ref[...].astype(F32)
            else:
                acc[...] = jnp.zeros_like(acc)

        acc[...] += _dot(a_ref[...], b_ref[...], dims)

        @pl.when(kk == gk - 1)
        def _():
            o_ref[...] = acc[...].astype(out_dtype)

    in_specs = [pl.BlockSpec(a_blk, a_map), pl.BlockSpec(b_blk, b_map)]
    args = [a, b]
    aliases = {}
    if init is not None:
        in_specs.append(pl.BlockSpec((tm, tn), o_map))
        args.append(init)
        aliases = {2: 0}
    elif passthru is not None:
        in_specs.append(pl.BlockSpec(memory_space=pl.ANY))
        args.append(passthru)
        aliases = {2: 0}
    out_dt = extra.dtype if extra is not None else out_dtype
    assert out_dt == out_dtype
    return pl.pallas_call(
        body, name=name, grid=(gi, gj, gk),
        in_specs=in_specs, out_specs=pl.BlockSpec((tm, tn), o_map),
        out_shape=jax.ShapeDtypeStruct((out_rows, n), out_dtype),
        scratch_shapes=[pltpu.VMEM((tm, tn), F32)] if gk > 1 else [],
        input_output_aliases=aliases,
        compiler_params=_cp(("parallel", "parallel", "arbitrary")),
    )(*args)


def _ffn_fwd(x, g, wg_t, wu_t, wd, name, next_gain=None, loss_of=None):
    tm, fc = PERM_TM, 256
    nc = FF // fc
    n_in = 5 + (1 if next_gain is not None else 0) + (2 if loss_of is not None else 0)

    def body(*refs):
        x_ref, g_ref, wg_ref, wu_ref, wd_ref = refs[:5]
        extra_in, outs = refs[5:n_in], refs[n_in:]
        act_ref = outs[-1]
        xv = x_ref[...]
        r = lax.rsqrt(jnp.mean(xv * xv, axis=-1, keepdims=True) + EPS)
        h = (xv * r * g_ref[...]).astype(BF16)
        gg_ref, uu_ref = (outs[0], outs[1]) if loss_of is not None else (outs[1], outs[2])
        for c in range(nc):
            sl = pl.ds(c * fc, fc)
            gg = _dot(h, wg_ref[sl, :], NT)
            uu = _dot(h, wu_ref[sl, :], NT)
            gg_ref[:, sl] = gg.astype(BF16)
            uu_ref[:, sl] = uu.astype(BF16)
            act_ref[:, sl] = (gg * _sig(gg) * uu).astype(BF16)
        y = xv + 0.5 * _dot(act_ref[...], wd_ref[...], NN)
        if loss_of is not None:
            _final_math(y, extra_in[0][...], extra_in[1][...], outs[2], outs[3], outs[4], pl.program_id(0))
            return
        outs[0][...] = y
        if next_gain is not None:
            tile = outs[-2]
            r2 = lax.rsqrt(jnp.mean(y * y, axis=-1, keepdims=True) + EPS)
            hv = y * r2 * extra_in[0][...]
            outs[3][...] = hv.astype(BF16)
            _put_tile(tile, hv)
            for dil, p_ref in zip(DILS, outs[4:4 + len(DILS)]):
                _store_perm(p_ref, tile, dil)

    wspec = pl.BlockSpec((FF, D), lambda i: (0, 0), pipeline_mode=pl.Buffered(1))
    row_d = pl.BlockSpec((tm, D), lambda i: (i, 0))
    row_f = pl.BlockSpec((tm, FF), lambda i: (i, 0))
    in_specs = [row_d, _const_spec((1, D)), wspec, wspec, wspec]
    args = [x, g, wg_t, wu_t, wd]
    f_shape = jax.ShapeDtypeStruct((T, FF), BF16)
    scratch = [pltpu.VMEM((tm, FF), BF16)]
    if loss_of is not None:
        in_specs += [_const_spec((1, D)), row_d]
        args += list(loss_of)
        out_specs = [row_f, row_f, row_d, _const_spec((1, D)), _const_spec((1, 128))]
        out_shape = [f_shape, f_shape, jax.ShapeDtypeStruct((T, D), F32), jax.ShapeDtypeStruct((1, D), F32),
                     jax.ShapeDtypeStruct((1, 128), F32)]
    else:
        out_specs = [row_d, row_f, row_f]
        out_shape = [jax.ShapeDtypeStruct((T, D), F32), f_shape, f_shape]
        if next_gain is not None:
            in_specs.append(_const_spec((1, D)))
            args.append(next_gain)
            out_specs += [row_d] + [_perm_spec(d, D) for d in DILS]
            out_shape += [jax.ShapeDtypeStruct((T, D), BF16)] + [_perm_shape(d, D, BF16) for d in DILS]
            scratch = [_tile_scratch(D)] + scratch
    out = pl.pallas_call(
        body, name=name, grid=(T // tm,), in_specs=in_specs, out_specs=out_specs, out_shape=out_shape,
        scratch_shapes=scratch,
        compiler_params=_cp(("arbitrary",) if loss_of is not None else ("parallel",)),
    )(*args)
    if next_gain is not None:
        return out[0], out[1], out[2], [out[3]] + [o.reshape(T, D) for o in out[4:]]
    return tuple(out)


def _ffn_bwd(x, g, gg_all, uu_all, dout, wg_t, wu_t, wd, name):
    tm, fc = 256, 256
    nc = FF // fc

    def body(x_ref, g_ref, gg_ref, uu_ref, do_ref, wg_ref, wu_ref, wd_ref,
             dx_ref, dgam_ref, dg_ref, du_ref, act_ref, h_ref, db_ref):
        i = pl.program_id(0)
        xv = x_ref[...]
        r = lax.rsqrt(jnp.mean(xv * xv, axis=-1, keepdims=True) + EPS)
        xhat = xv * r
        gam = g_ref[...]
        h_ref[...] = (xhat * gam).astype(BF16)
        dov = do_ref[...]
        dbv = (0.5 * dov).astype(BF16)
        db_ref[...] = dbv
        for c in range(nc):
            sl = pl.ds(c * fc, fc)
            da = _dot(dbv, wd_ref[sl, :], NT)
            gg = gg_ref[:, sl].astype(F32)
            uu = uu_ref[:, sl].astype(F32)
            s = _sig(gg)
            si = gg * s
            dgv = (da * uu * (s * (1.0 + gg * (1.0 - s)))).astype(BF16)
            duv = (da * si).astype(BF16)
            dg_ref[:, sl] = dgv
            du_ref[:, sl] = duv
            act_ref[:, sl] = (si * uu).astype(BF16)
        dh = _dot(dg_ref[...], wg_ref[...], NN) + _dot(du_ref[...], wu_ref[...], NN)

        @pl.when(i == 0)
        def _():
            dgam_ref[...] = jnp.zeros_like(dgam_ref)

        dgam_ref[...] += jnp.sum(dh * xhat, axis=0, keepdims=True)
        dxh = dh * gam
        dx_ref[...] = dov + r * (dxh - xhat * jnp.mean(dxh * xhat, axis=-1, keepdims=True))

    wspec = pl.BlockSpec((FF, D), lambda i: (0, 0), pipeline_mode=pl.Buffered(1))
    row_d = pl.BlockSpec((tm, D), lambda i: (i, 0))
    row_f = pl.BlockSpec((tm, FF), lambda i: (i, 0))
    return pl.pallas_call(
        body, name=name, grid=(T // tm,),
        in_specs=[row_d, _const_spec((1, D)), row_f, row_f, row_d, wspec, wspec, wspec],
        out_specs=[row_d, _const_spec((1, D)), row_f, row_f, row_f, row_d, row_d],
        out_shape=[jax.ShapeDtypeStruct((T, D), F32), jax.ShapeDtypeStruct((1, D), F32),
                   jax.ShapeDtypeStruct((T, FF), BF16), jax.ShapeDtypeStruct((T, FF), BF16),
                   jax.ShapeDtypeStruct((T, FF), BF16), jax.ShapeDtypeStruct((T, D), BF16),
                   jax.ShapeDtypeStruct((T, D), BF16)],
        compiler_params=_cp(("arbitrary",)),
    )(x, g, gg_all, uu_all, dout, wg_t, wu_t, wd)


def _ffn_bwd_pre(x, g, gg_all, uu_all, dout, wd, name):
    tm, fc = 512, 256
    nc = FF // fc

    def body(x_ref, g_ref, gg_ref, uu_ref, do_ref, wd_ref, dg_ref, du_ref, act_ref, h_ref, db_ref):
        xv = x_ref[...]
        r = lax.rsqrt(jnp.mean(xv * xv, axis=-1, keepdims=True) + EPS)
        h_ref[...] = (xv * r * g_ref[...]).astype(BF16)
        dbv = (0.5 * do_ref[...]).astype(BF16)
        db_ref[...] = dbv
        for c in range(nc):
            sl = pl.ds(c * fc, fc)
            da = _dot(dbv, wd_ref[sl, :], NT)
            gg = gg_ref[:, sl].astype(F32)
            uu = uu_ref[:, sl].astype(F32)
            s = _sig(gg)
            si = gg * s
            dg_ref[:, sl] = (da * uu * (s * (1.0 + gg * (1.0 - s)))).astype(BF16)
            du_ref[:, sl] = (da * si).astype(BF16)
            act_ref[:, sl] = (si * uu).astype(BF16)

    wspec = pl.BlockSpec((FF, D), lambda i: (0, 0), pipeline_mode=pl.Buffered(1))
    row_d = pl.BlockSpec((tm, D), lambda i: (i, 0))
    row_f = pl.BlockSpec((tm, FF), lambda i: (i, 0))
    return pl.pallas_call(
        body, name=name, grid=(T // tm,),
        in_specs=[row_d, _const_spec((1, D)), row_f, row_f, row_d, wspec],
        out_specs=[row_f, row_f, row_f, row_d, row_d],
        out_shape=[jax.ShapeDtypeStruct((T, FF), BF16), jax.ShapeDtypeStruct((T, FF), BF16),
                   jax.ShapeDtypeStruct((T, FF), BF16), jax.ShapeDtypeStruct((T, D), BF16),
                   jax.ShapeDtypeStruct((T, D), BF16)],
        compiler_params=_cp(("parallel",)),
    )(x, g, gg_all, uu_all, dout, wd)


def _ffn_bwd_dx(x, g, dgb, dub, dout, wg_t, wu_t, name):
    tm = 512

    def body(x_ref, g_ref, dg_ref, du_ref, do_ref, wg_ref, wu_ref, dx_ref, dgam_ref):
        i = pl.program_id(0)
        xv = x_ref[...]
        r = lax.rsqrt(jnp.mean(xv * xv, axis=-1, keepdims=True) + EPS)
        xhat = xv * r
        gam = g_ref[...]
        dh = _dot(dg_ref[...], wg_ref[...], NN) + _dot(du_ref[...], wu_ref[...], NN)

        @pl.when(i == 0)
        def _():
            dgam_ref[...] = jnp.zeros_like(dgam_ref)

        dgam_ref[...] += jnp.sum(dh * xhat, axis=0, keepdims=True)
        dxh = dh * gam
        dx_ref[...] = do_ref[...] + r * (dxh - xhat * jnp.mean(dxh * xhat, axis=-1, keepdims=True))

    wspec = pl.BlockSpec((FF, D), lambda i: (0, 0), pipeline_mode=pl.Buffered(1))
    row_d = pl.BlockSpec((tm, D), lambda i: (i, 0))
    row_f = pl.BlockSpec((tm, FF), lambda i: (i, 0))
    return pl.pallas_call(
        body, name=name, grid=(T // tm,),
        in_specs=[row_d, _const_spec((1, D)), row_f, row_f, row_d, wspec, wspec],
        out_specs=[row_d, _const_spec((1, D))],
        out_shape=[jax.ShapeDtypeStruct((T, D), F32), jax.ShapeDtypeStruct((1, D), F32)],
        compiler_params=_cp(("arbitrary",)),
    )(x, g, dgb, dub, dout, wg_t, wu_t)


def _wgrad(a, b, m, n, name):
    tm = m // 2 if m == FF else m
    return _mm(a, b, mode="tn", m=m, n=n, k=T, tm=tm, tn=n, tk=min(T, 2048), out_dtype=BF16, name=name)


PERM_TM = 512
DILS = tuple(d for _, d in GROUPS if d > 1)


def _perm_spec(dil, cols):
    return pl.BlockSpec((dil, PERM_TM // dil, cols), lambda i: (0, i, 0))


def _perm_shape(dil, cols, dtype):
    return jax.ShapeDtypeStruct((dil, T // dil, cols), dtype)


LANES = 128


def _tile_scratch(cols):
    return pltpu.VMEM((cols // LANES, PERM_TM, LANES), F32)


def _put_tile(tile, value):
    for c in range(tile.shape[0]):
        tile[c] = value[:, c * LANES:(c + 1) * LANES]


def _get_tile(tile):
    return jnp.concatenate([tile[c] for c in range(tile.shape[0])], axis=1)


def _store_perm(out_ref, tile, dil):
    for r in range(dil):
        for c in range(tile.shape[0]):
            out_ref[r, :, pl.ds(c * LANES, LANES)] = tile[c, pl.ds(r, PERM_TM // dil, stride=dil), :].astype(
                out_ref.dtype)


def _load_unperm(in_ref, tile, dil):
    for r in range(dil):
        for c in range(tile.shape[0]):
            tile[c, pl.ds(r, PERM_TM // dil, stride=dil), :] = in_ref[r, :, pl.ds(c * LANES, LANES)].astype(F32)


def _norm_cast(x, g, name):
    tm = PERM_TM

    def body(x_ref, g_ref, h_ref, *rest):
        p_refs, tile = rest[:-1], rest[-1]
        xv = x_ref[...]
        r = lax.rsqrt(jnp.mean(xv * xv, axis=-1, keepdims=True) + EPS)
        hv = xv * r * g_ref[...]
        h_ref[...] = hv.astype(BF16)
        _put_tile(tile, hv)
        for dil, p_ref in zip(DILS, p_refs):
            _store_perm(p_ref, tile, dil)

    out = pl.pallas_call(
        body, name=name, grid=(T // tm,),
        in_specs=[pl.BlockSpec((tm, D), lambda i: (i, 0)), _const_spec((1, D))],
        out_specs=[pl.BlockSpec((tm, D), lambda i: (i, 0))] + [_perm_spec(d, D) for d in DILS],
        out_shape=[jax.ShapeDtypeStruct((T, D), BF16)] + [_perm_shape(d, D, BF16) for d in DILS],
        scratch_shapes=[_tile_scratch(D)],
        compiler_params=_cp(("parallel",)),
    )(x, g)
    return [out[0]] + [o.reshape(T, D) for o in out[1:]]


def _final_math(xv, gam, tgt, dx_ref, dgam_ref, loss_ref, i):
    r = lax.rsqrt(jnp.mean(xv * xv, axis=-1, keepdims=True) + EPS)
    xhat = xv * r
    err = xhat * gam - tgt
    part = 0.5 * jnp.sum(jnp.mean(err * err, axis=-1, keepdims=True), axis=0, keepdims=True)
    dy = err * (1.0 / D)

    @pl.when(i == 0)
    def _():
        dgam_ref[...] = jnp.zeros_like(dgam_ref)
        loss_ref[...] = jnp.zeros_like(loss_ref)

    dgam_ref[...] += jnp.sum(dy * xhat, axis=0, keepdims=True)
    loss_ref[...] += jnp.broadcast_to(part, loss_ref.shape)
    dxh = dy * gam
    dx_ref[...] = r * (dxh - xhat * jnp.mean(dxh * xhat, axis=-1, keepdims=True))


def _rms_bwd(x, g, dhs, dres, name):
    tm = PERM_TM
    dils = [d for _, d in GROUPS]
    nh = len(dhs)
    assert nh == len(dils)

    def body(*refs):
        x_ref, g_ref = refs[:2]
        dh_refs = refs[2:2 + nh]
        dr_ref, dx_ref, dgam_ref, tile = refs[2 + nh:]
        i = pl.program_id(0)
        xv = x_ref[...]
        r = lax.rsqrt(jnp.mean(xv * xv, axis=-1, keepdims=True) + EPS)
        xhat = xv * r
        gam = g_ref[...]
        dh = None
        for dil, ref in zip(dils, dh_refs):
            if dil == 1:
                part = ref[...]
            else:
                _load_unperm(ref, tile, dil)
                part = _get_tile(tile)
            dh = part if dh is None else dh + part

        @pl.when(i == 0)
        def _():
            dgam_ref[...] = jnp.zeros_like(dgam_ref)

        dgam_ref[...] += jnp.sum(dh * xhat, axis=0, keepdims=True)
        dxh = dh * gam
        dx_ref[...] = dr_ref[...] + r * (dxh - xhat * jnp.mean(dxh * xhat, axis=-1, keepdims=True))

    row_d = pl.BlockSpec((tm, D), lambda i: (i, 0))
    dh_specs = [row_d if d == 1 else _perm_spec(d, D) for d in dils]
    dh_args = [a if d == 1 else a.reshape(d, T // d, D) for d, a in zip(dils, dhs)]
    return pl.pallas_call(
        body, name=name, grid=(T // tm,),
        in_specs=[row_d, _const_spec((1, D))] + dh_specs + [row_d],
        out_specs=[row_d, _const_spec((1, D))],
        out_shape=[jax.ShapeDtypeStruct((T, D), F32), jax.ShapeDtypeStruct((1, D), F32)],
        scratch_shapes=[_tile_scratch(D)],
        compiler_params=_cp(("arbitrary",)),
    )(x, g, *dh_args, dres)


CONV_TM = 256
CONV_HALO = 32
CONV_RB = 16


def _glu(ab):
    ab = ab.astype(F32)
    return ab[:, :D] * _sig(ab[:, D:])


def _ln_stats(z1):
    mu = jnp.mean(z1, axis=-1, keepdims=True)
    zc = z1 - mu
    rstd = lax.rsqrt(jnp.mean(zc * zc, axis=-1, keepdims=True) + EPS)
    return zc * rstd, rstd


def _fill_shifts(zs):
    n = zs.shape[1] - 8
    for s in range(1, 8):
        zs[s, pl.ds(0, n), :] = zs[0, pl.ds(s, n), :]


def _shifted(zs, start, rows):
    q, s = divmod(start, 8)
    return zs[s, pl.ds(8 * q, rows), :]


def _conv_fwd(ab, kern, dwb, lng, lnb, name):
    tm, hl, rb = CONV_TM, CONV_HALO, CONV_RB
    off = hl - (CONV_W - 1)

    def body(ab_ref, abh_ref, k_ref, dwb_ref, lng_ref, lnb_ref, z1_ref, z3_ref, zs):
        i = pl.program_id(0)
        zs[0, pl.ds(0, hl), :] = jnp.where(i > 0, _glu(abh_ref[...]), 0.0)
        zs[0, pl.ds(hl, tm), :] = _glu(ab_ref[...])
        _fill_shifts(zs)
        for b in range(tm // rb):
            acc = jnp.zeros((rb, D), F32)
            for j in range(CONV_W):
                acc = acc + _shifted(zs, b * rb + off + j, rb) * k_ref[pl.ds(j, 1), :]
            z1 = acc + dwb_ref[...]
            z1_ref[pl.ds(b * rb, rb), :] = z1
            zn, _ = _ln_stats(z1)
            z2 = zn * lng_ref[...] + lnb_ref[...]
            z3_ref[pl.ds(b * rb, rb), :] = (z2 * _sig(z2)).astype(BF16)

    row = pl.BlockSpec((tm, D), lambda i: (i, 0))
    return pl.pallas_call(
        body, name=name, grid=(T // tm,),
        in_specs=[pl.BlockSpec((tm, 2 * D), lambda i: (i, 0)),
                  pl.BlockSpec((hl, 2 * D), lambda i: (jnp.maximum(i * (tm // hl) - 1, 0), 0)),
                  _const_spec((32, D)), _const_spec((1, D)), _const_spec((1, D)), _const_spec((1, D))],
        out_specs=[row, row],
        out_shape=[jax.ShapeDtypeStruct((T, D), F32), jax.ShapeDtypeStruct((T, D), BF16)],
        scratch_shapes=[pltpu.VMEM((8, hl + tm, D), F32)],
        compiler_params=_cp(("parallel",)),
    )(ab, ab, kern, dwb, lng, lnb)


GUEST_TM = 256


def _conv_bwd(dz3, z1, ab, kern, lng, lnb, name, guest_lhs=(), guest_rhs=None):
    tm, hl, rb = CONV_TM, CONV_HALO, CONV_RB
    off = hl - (CONV_W - 1)
    nsteps = T // tm
    ng = len(guest_lhs)
    gblocks = [a.shape[1] // GUEST_TM for a in guest_lhs]
    assert all(gb <= nsteps and gb * GUEST_TM == a.shape[1] for gb, a in zip(gblocks, guest_lhs))

    def ln_bwd(dz3v, z1v, lngv, lnbv):
        zn, rstd = _ln_stats(z1v)
        z2 = zn * lngv + lnbv
        s = _sig(z2)
        dz2 = dz3v * (s * (1.0 + z2 * (1.0 - s)))
        dzn = dz2 * lngv
        dz1 = rstd * (dzn - jnp.mean(dzn, axis=-1, keepdims=True)
                      - zn * jnp.mean(dzn * zn, axis=-1, keepdims=True))
        return dz1, dz2, zn

    def body(dz3_ref, dz3h_ref, z1_ref, z1h_ref, ab_ref, abh_ref, k_ref, lng_ref, lnb_ref, *rest):
        g_in, rest = rest[:ng + (1 if ng else 0)], rest[ng + (1 if ng else 0):]
        dab_ref, dk_ref, dvec_ref = rest[:3]
        g_out, (zs, dzs) = rest[3:3 + ng], rest[3 + ng:]
        i = pl.program_id(0)
        lngv, lnbv = lng_ref[...], lnb_ref[...]

        for a_ref, o_ref, gb in zip(g_in[:ng], g_out, gblocks):
            @pl.when(i < gb)
            def _(a_ref=a_ref, o_ref=o_ref):
                o_ref[...] = _dot(a_ref[...], g_in[ng][...], TN).astype(BF16)

        @pl.when(i == 0)
        def _():
            dk_ref[...] = jnp.zeros_like(dk_ref)
            dvec_ref[...] = jnp.zeros_like(dvec_ref)

        dz1, dz2, zn = ln_bwd(dz3_ref[...].astype(F32), z1_ref[...], lngv, lnbv)
        dvec_ref[pl.ds(0, 1), :] += jnp.sum(dz1, axis=0, keepdims=True)
        dvec_ref[pl.ds(1, 1), :] += jnp.sum(dz2 * zn, axis=0, keepdims=True)
        dvec_ref[pl.ds(2, 1), :] += jnp.sum(dz2, axis=0, keepdims=True)
        dzs[0, pl.ds(0, tm), :] = dz1
        dz1h, _, _ = ln_bwd(dz3h_ref[...].astype(F32), z1h_ref[...], lngv, lnbv)
        dzs[0, pl.ds(tm, hl), :] = jnp.where(i < nsteps - 1, dz1h, 0.0)
        _fill_shifts(dzs)
        zs[0, pl.ds(0, hl), :] = jnp.where(i > 0, _glu(abh_ref[...]), 0.0)
        zs[0, pl.ds(hl, tm), :] = _glu(ab_ref[...])
        _fill_shifts(zs)

        for j in range(CONV_W):
            tot = jnp.zeros((rb, D), F32)
            for b in range(tm // rb):
                tot = tot + dzs[0, pl.ds(b * rb, rb), :] * _shifted(zs, b * rb + off + j, rb)
            dk_ref[pl.ds(j, 1), :] += jnp.sum(tot, axis=0, keepdims=True)

        for b in range(tm // rb):
            acc = jnp.zeros((rb, D), F32)
            for j in range(CONV_W):
                acc = acc + _shifted(dzs, b * rb + (CONV_W - 1) - j, rb) * k_ref[pl.ds(j, 1), :]
            av = ab_ref[pl.ds(b * rb, rb), pl.ds(0, D)].astype(F32)
            sb = _sig(ab_ref[pl.ds(b * rb, rb), pl.ds(D, D)].astype(F32))
            dab_ref[pl.ds(b * rb, rb), pl.ds(0, D)] = (acc * sb).astype(BF16)
            dab_ref[pl.ds(b * rb, rb), pl.ds(D, D)] = (acc * av * sb * (1.0 - sb)).astype(BF16)

    row = pl.BlockSpec((tm, D), lambda i: (i, 0))
    nxt = pl.BlockSpec((hl, D), lambda i: (jnp.minimum((i + 1) * (tm // hl), T // hl - 1), 0))
    g_specs, g_args, g_ospecs, g_oshapes = [], [], [], []
    for a, gb in zip(guest_lhs, gblocks):
        g_specs.append(pl.BlockSpec((T, GUEST_TM), lambda i, gb=gb: (0, jnp.minimum(i, gb - 1))))
        g_args.append(a)
        g_ospecs.append(pl.BlockSpec((GUEST_TM, guest_rhs.shape[1]), lambda i, gb=gb: (jnp.minimum(i, gb - 1), 0)))
        g_oshapes.append(jax.ShapeDtypeStruct((a.shape[1], guest_rhs.shape[1]), BF16))
    if ng:
        g_specs.append(pl.BlockSpec(guest_rhs.shape, lambda i: (0, 0), pipeline_mode=pl.Buffered(1)))
        g_args.append(guest_rhs)
    return pl.pallas_call(
        body, name=name, grid=(nsteps,),
        in_specs=[row, nxt, row, nxt,
                  pl.BlockSpec((tm, 2 * D), lambda i: (i, 0)),
                  pl.BlockSpec((hl, 2 * D), lambda i: (jnp.maximum(i * (tm // hl) - 1, 0), 0)),
                  _const_spec((32, D)), _const_spec((1, D)), _const_spec((1, D))] + g_specs,
        out_specs=[pl.BlockSpec((tm, 2 * D), lambda i: (i, 0)), _const_spec((32, D)), _const_spec((8, D))]
        + g_ospecs,
        out_shape=[jax.ShapeDtypeStruct((T, 2 * D), BF16), jax.ShapeDtypeStruct((32, D), F32),
                   jax.ShapeDtypeStruct((8, D), F32)] + g_oshapes,
        scratch_shapes=[pltpu.VMEM((8, hl + tm, D), F32), pltpu.VMEM((8, tm + hl, D), F32)],
        compiler_params=_cp(("arbitrary",)),
    )(dz3, dz3, z1, z1, ab, ab, kern, lng, lnb, *g_args)


def _alibi_slopes():
    h = np.arange(1, 3 * NHG + 1, dtype=np.float32)
    return np.power(np.float32(2.0), -8.0 * h / np.float32(3 * NHG)).astype(np.float32)


def _band_bias(gi):
    _, dil = GROUPS[gi]
    slopes = _alibi_slopes()[gi * NHG:(gi + 1) * NHG]
    qi = np.arange(BLK)[:, None]
    ki = np.arange(2 * BLK)[None, :]
    steps = BLK + qi - ki
    band = (steps >= 0) & (steps <= BLK)
    bias = -slopes[:, None, None] * (dil * steps).astype(np.float32)[None]
    return jnp.asarray(np.where(band[None], bias, np.float32(NEG)).astype(np.float32))


QB_FWD = 8
QB_BWD = 32


def _attn_specs(qb):
    prev = lambda n: jnp.maximum(n * qb - 1, 0)
    return [pl.BlockSpec((qb * BLK, HEAD), lambda h, n: (n, h)),
            pl.BlockSpec((BLK, HEAD), lambda h, n: (prev(n), NHG + h)),
            pl.BlockSpec((qb * BLK, HEAD), lambda h, n: (n, NHG + h)),
            pl.BlockSpec((BLK, HEAD), lambda h, n: (prev(n), 2 * NHG + h)),
            pl.BlockSpec((qb * BLK, HEAD), lambda h, n: (n, 2 * NHG + h)),
            pl.BlockSpec((None, BLK, 2 * BLK), lambda h, n: (h, 0, 0))]


def _scores(q, kcat, bias, blk, seg):
    s = _dot(q, kcat, NT) * (HEAD ** -0.5) + bias
    col = lax.broadcasted_iota(jnp.int32, s.shape, 1)
    first = (blk % seg) == 0
    return jnp.where(jnp.logical_and(first, col < BLK), NEG, s)


def _attn_fwd(qkv, gi, name):
    seg = (T // GROUPS[gi][1]) // BLK

    qb = min(QB_FWD, T // BLK)

    def body(q_ref, kp_ref, kc_ref, vp_ref, vc_ref, bias_ref, o_ref, l_ref):
        n = pl.program_id(0)
        for h in range(NHG):
            cols = pl.ds(h * HEAD, HEAD)
            kwin = jnp.concatenate([kp_ref[:, cols], kc_ref[:, cols]], axis=0)
            vwin = jnp.concatenate([vp_ref[:, cols], vc_ref[:, cols]], axis=0)
            bias = bias_ref[h]
            for b in range(qb):
                rows = pl.ds(b * BLK, BLK)
                s = _scores(q_ref[rows, cols], kwin[b * BLK:(b + 2) * BLK], bias, n * qb + b, seg)
                mx = jnp.max(s, axis=-1, keepdims=True)
                p = jnp.exp(s - mx)
                den = jnp.sum(p, axis=-1, keepdims=True)
                o_ref[rows, cols] = (_dot(p.astype(BF16), vwin[b * BLK:(b + 2) * BLK], NN) / den).astype(BF16)
                l_ref[rows, cols] = jnp.broadcast_to(mx + jnp.log(den), (BLK, HEAD))

    prev = lambda n: jnp.maximum(n * qb - 1, 0)
    cur = lambda part: pl.BlockSpec((qb * BLK, AW), lambda n: (n, part))
    halo = lambda part: pl.BlockSpec((BLK, AW), lambda n: (prev(n), part))
    return pl.pallas_call(
        body, name=name, grid=(T // (qb * BLK),),
        in_specs=[cur(0), halo(1), cur(1), halo(2), cur(2), _const_spec((NHG, BLK, 2 * BLK))],
        out_specs=[cur(0), cur(0)],
        out_shape=[jax.ShapeDtypeStruct((T, AW), BF16), jax.ShapeDtypeStruct((T, AW), F32)],
        compiler_params=_cp(("parallel",)),
    )(qkv, qkv, qkv, qkv, qkv, _band_bias(gi))


def _attn_bwd(qkv, dob, lse, delta, gi, name):
    seg = (T // GROUPS[gi][1]) // BLK
    qb = min(QB_BWD, T // BLK)
    nb = T // (qb * BLK)
    scale = HEAD ** -0.5

    def body(q_ref, kp_ref, kc_ref, vp_ref, vc_ref, bias_ref, do_ref, l_ref, dl_ref, out_ref, dk_acc, dv_acc):
        n = pl.program_id(1)
        kwin = jnp.concatenate([kp_ref[...], kc_ref[...]], axis=0)
        vwin = jnp.concatenate([vp_ref[...], vc_ref[...]], axis=0)
        bias = bias_ref[...]
        dks, dvs = [], []
        for b in range(qb):
            rows = pl.ds(b * BLK, BLK)
            q = q_ref[rows, :]
            kcat = kwin[b * BLK:(b + 2) * BLK]
            s = _scores(q, kcat, bias, n * qb + b, seg)
            p = jnp.exp(s - l_ref[rows, pl.ds(0, 1)])
            dov = do_ref[rows, :]
            dvs.append(_dot(p.astype(BF16), dov, TN))
            dp = _dot(dov, vwin[b * BLK:(b + 2) * BLK], NT)
            dsb = (p * (dp - dl_ref[rows, pl.ds(0, 1)]) * scale).astype(BF16)
            row = pl.ds(pl.multiple_of((n * qb + b) * BLK, BLK), BLK)
            out_ref[0, row, :] = _dot(dsb, kcat, NN).astype(BF16)
            dks.append(_dot(dsb, q, TN))
        for b in range(qb):
            row = pl.ds(pl.multiple_of((n * qb + b) * BLK, BLK), BLK)
            if b + 1 < qb:
                dk_acc[row, :] = dks[b][BLK:] + dks[b + 1][:BLK]
                dv_acc[row, :] = dvs[b][BLK:] + dvs[b + 1][:BLK]
            else:
                dk_acc[row, :] = dks[b][BLK:]
                dv_acc[row, :] = dvs[b][BLK:]

        @pl.when(n > 0)
        def _():
            prow = pl.ds(pl.multiple_of((n * qb - 1) * BLK, BLK), BLK)
            dk_acc[prow, :] += dks[0][:BLK]
            dv_acc[prow, :] += dvs[0][:BLK]

        @pl.when(n == nb - 1)
        def _():
            out_ref[1] = dk_acc[...].astype(BF16)
            out_ref[2] = dv_acc[...].astype(BF16)

    oblk = pl.BlockSpec((qb * BLK, HEAD), lambda h, n: (n, h))
    return pl.pallas_call(
        body, name=name, grid=(NHG, nb),
        in_specs=_attn_specs(qb) + [oblk, oblk, oblk],
        out_specs=pl.BlockSpec((3, T, HEAD), lambda h, n: (0, 0, h)),
        out_shape=jax.ShapeDtypeStruct((3, T, AW), BF16),
        scratch_shapes=[pltpu.VMEM((T, HEAD), F32), pltpu.VMEM((T, HEAD), F32)],
        compiler_params=_cp(("parallel", "arbitrary")),
    )(qkv, qkv, qkv, qkv, qkv, _band_bias(gi), dob, lse, delta)


def _merge(outs, lses, name):
    tm = PERM_TM
    dils = [d for _, d in GROUPS]
    ng = len(dils)

    def body(*refs):
        in_refs = refs[:2 * ng]
        ab_ref = refs[2 * ng]
        lse_refs = refs[2 * ng + 1:3 * ng + 1]
        tile = refs[-1]

        def token_order(ref, dil):
            if dil == 1:
                return ref[...].astype(F32)
            _load_unperm(ref, tile, dil)
            return _get_tile(tile)

        os = [token_order(in_refs[2 * i], d) for i, d in enumerate(dils)]
        ls = [token_order(in_refs[2 * i + 1], d) for i, d in enumerate(dils)]
        mx = jnp.maximum(jnp.maximum(ls[0], ls[1]), ls[2])
        es = [jnp.exp(v - mx) for v in ls]
        tot = es[0] + es[1] + es[2]
        att = (es[0] / tot) * os[0] + (es[1] / tot) * os[1] + (es[2] / tot) * os[2]
        ab_ref[...] = att.astype(BF16)
        lse = mx + jnp.log(tot)
        _put_tile(tile, lse)
        for dil, ref in zip(dils, lse_refs):
            if dil == 1:
                ref[...] = lse
            else:
                _store_perm(ref, tile, dil)

    row = pl.BlockSpec((tm, AW), lambda i: (i, 0))
    specs = [row if d == 1 else _perm_spec(d, AW) for d in dils]
    args = []
    for d, o, l in zip(dils, outs, lses):
        args += [o, l] if d == 1 else [o.reshape(d, T // d, AW), l.reshape(d, T // d, AW)]
    out = pl.pallas_call(
        body, name=name, grid=(T // tm,),
        in_specs=[sp for sp in specs for _ in range(2)], out_specs=[row] + specs,
        out_shape=[jax.ShapeDtypeStruct((T, AW), BF16)]
        + [jax.ShapeDtypeStruct((T, AW), F32) if d == 1 else _perm_shape(d, AW, F32) for d in dils],
        scratch_shapes=[_tile_scratch(AW)],
        compiler_params=_cp(("parallel",)),
    )(*args)
    return out[0], [o.reshape(T, AW) for o in out[1:]]


def _mix_out(z3b, attnb, gates, wc, wa_t, wo, x1, name):
    tm = 512

    def body(z_ref, a_ref, g_ref, wc_ref, wa_ref, wo_ref, x_ref, xo_ref, yc_ref, ya_ref, mx_ref):
        yc = _dot(z_ref[...], wc_ref[...], NN)
        ya = _dot(a_ref[...], wa_ref[...], NT)
        yc_ref[...] = yc.astype(BF16)
        ya_ref[...] = ya.astype(BF16)
        gv = g_ref[...].astype(F32)
        mixed = (_sig(gv[:, :D]) * yc + _sig(gv[:, D:]) * ya).astype(BF16)
        mx_ref[...] = mixed
        xo_ref[...] = x_ref[...] + _dot(mixed, wo_ref[...], NN)

    row = pl.BlockSpec((tm, D), lambda i: (i, 0))
    return pl.pallas_call(
        body, name=name, grid=(T // tm,),
        in_specs=[row, pl.BlockSpec((tm, AW), lambda i: (i, 0)), pl.BlockSpec((tm, 2 * D), lambda i: (i, 0)),
                  _const_spec((D, D)), _const_spec((D, AW)), _const_spec((D, D)), row],
        out_specs=[row, row, row, row],
        out_shape=[jax.ShapeDtypeStruct((T, D), F32), jax.ShapeDtypeStruct((T, D), BF16),
                   jax.ShapeDtypeStruct((T, D), BF16), jax.ShapeDtypeStruct((T, D), BF16)],
        compiler_params=_cp(("parallel",)),
    )(z3b, attnb, gates, wc, wa_t, wo, x1)


def _mix_out_bwd(dx2, gates, yc, ya, attn, wc, wa_t, wo, name):
    tm = PERM_TM
    dils = [d for _, d in GROUPS]
    ng = len(dils)

    def body(dx_ref, g_ref, yc_ref, ya_ref, at_ref, wc_ref, wa_ref, wo_ref,
             dg_ref, dyc_ref, dya_ref, dxb_ref, dz3_ref, *rest):
        dat_refs, dl_refs, tile = rest[:ng], rest[ng:2 * ng], rest[-1]
        dxb = dx_ref[...].astype(BF16)
        dxb_ref[...] = dxb
        dmix = _dot(dxb, wo_ref[...], NT)
        gv = g_ref[...].astype(F32)
        sc = _sig(gv[:, :D])
        sa = _sig(gv[:, D:])
        ycv, yav = yc_ref[...].astype(F32), ya_ref[...].astype(F32)
        dg_ref[:, pl.ds(0, D)] = (dmix * ycv * sc * (1.0 - sc)).astype(BF16)
        dg_ref[:, pl.ds(D, D)] = (dmix * yav * sa * (1.0 - sa)).astype(BF16)
        dyc = (dmix * sc).astype(BF16)
        dya = (dmix * sa).astype(BF16)
        dyc_ref[...] = dyc
        dya_ref[...] = dya
        dz3_ref[...] = _dot(dyc, wc_ref[...], NT).astype(BF16)
        dat = _dot(dya, wa_ref[...], NN)
        prod = dat * at_ref[...].astype(F32)
        delta = jnp.concatenate(
            [jnp.broadcast_to(jnp.sum(prod[:, h * HEAD:(h + 1) * HEAD], axis=-1, keepdims=True), (tm, HEAD))
             for h in range(NHG)], axis=1)
        for value, out_refs in ((dat, dat_refs), (delta, dl_refs)):
            _put_tile(tile, value)
            for dil, ref in zip(dils, out_refs):
                if dil == 1:
                    ref[...] = value.astype(ref.dtype)
                else:
                    _store_perm(ref, tile, dil)

    row = pl.BlockSpec((tm, D), lambda i: (i, 0))
    row2 = pl.BlockSpec((tm, 2 * D), lambda i: (i, 0))
    rowa = pl.BlockSpec((tm, AW), lambda i: (i, 0))
    aspecs = [rowa if d == 1 else _perm_spec(d, AW) for d in dils]

    def ashapes(dtype):
        return [jax.ShapeDtypeStruct((T, AW), dtype) if d == 1 else _perm_shape(d, AW, dtype) for d in dils]

    out = pl.pallas_call(
        body, name=name, grid=(T // tm,),
        in_specs=[row, row2, row, row, rowa, _const_spec((D, D)), _const_spec((D, AW)), _const_spec((D, D))],
        out_specs=[row2, row, row, row, row] + aspecs + aspecs,
        out_shape=[jax.ShapeDtypeStruct((T, 2 * D), BF16), jax.ShapeDtypeStruct((T, D), BF16),
                   jax.ShapeDtypeStruct((T, D), BF16), jax.ShapeDtypeStruct((T, D), BF16),
                   jax.ShapeDtypeStruct((T, D), BF16)] + ashapes(BF16) + ashapes(F32),
        scratch_shapes=[_tile_scratch(AW)],
        compiler_params=_cp(("parallel",)),
    )(dx2, gates, yc, ya, attn, wc, wa_t, wo)
    dats = [o.reshape(T, AW) for o in out[5:5 + ng]]
    deltas = [o.reshape(T, AW) for o in out[5 + ng:5 + 2 * ng]]
    return out[0], out[1], out[2], out[3], out[4], dats, deltas


def _peer(k):
    x, y, c = lax.axis_index("x"), lax.axis_index("y"), lax.axis_index("c")
    px = 1 - x if k & 4 else x
    py = 1 - y if k & 2 else y
    pc = 1 - c if k & 1 else c
    return (px, py, pc), 4 * px + 2 * py + pc


HBM_SPEC = pl.BlockSpec(memory_space=pltpu.HBM)
SEM_SPEC = pl.BlockSpec(memory_space=pltpu.SEMAPHORE)
EFFECT = pltpu.SideEffectType.DATAFLOW_SIDE_EFFECTING


def _my_place():
    return 4 * lax.axis_index("x") + 2 * lax.axis_index("y") + lax.axis_index("c")


def _tie(a, order_after, name):
    na = len(order_after)

    def body(*refs):
        del refs

    return pl.pallas_call(
        body, name=name, in_specs=[pl.BlockSpec(memory_space=pl.ANY)] * (1 + na),
        out_specs=pl.BlockSpec(memory_space=pl.ANY), out_shape=jax.ShapeDtypeStruct(a.shape, a.dtype),
        input_output_aliases={0: 0},
    )(a, *order_after)


def _prep_gather(ws, order_after, name):
    me = jnp.reshape(_my_place(), (1,)).astype(jnp.int32)
    n = len(ws)
    na = len(order_after)
    shapes = [((32, wv.shape[1]), F32) if wv.shape[0] == CONV_W else (wv.shape, BF16) for wv in ws]

    def body(me_ref, *refs):
        del me_ref
        ins, outs = refs[:n], refs[n + na:]
        for wv, i_ref, o_ref in zip(ws, ins, outs):
            if wv.shape[0] == CONV_W:
                o_ref[pl.ds(0, CONV_W), :] = i_ref[...]
                o_ref[pl.ds(CONV_W, 1), :] = jnp.zeros((1, wv.shape[1]), F32)
            else:
                o_ref[...] = i_ref[...].astype(BF16)

    grid_spec = pltpu.PrefetchScalarGridSpec(
        num_scalar_prefetch=1, grid=(1,),
        in_specs=[pl.BlockSpec(wv.shape, lambda i, m: (0, 0)) for wv in ws]
        + [pl.BlockSpec(memory_space=pl.ANY)] * na,
        out_specs=[pl.BlockSpec(shp, lambda i, m: (m[0], 0)) for shp, _ in shapes])
    return pl.pallas_call(
        body, name=name, grid_spec=grid_spec,
        out_shape=[jax.ShapeDtypeStruct((NDEV * shp[0], shp[1]), dt) for shp, dt in shapes],
        compiler_params=_cp(("arbitrary",)),
    )(me, *ws, *order_after)


GATHER_A = ((1, 0), (2, 0), (4, 0), (6, 0))
GATHER_B = ((1, 2), (1, 4), (1, 6))


def _gather_start(lands, plan, order_after, name):
    n = len(lands)
    na = len(order_after)
    npl = len(plan)

    def body(*refs):
        land_refs = refs[:n]
        send, recv = refs[n + na], refs[n + na + 1]
        token = refs[-1]
        for w in range(n):
            rows = lands[w].shape[0] // NDEV
            for p, (k, j) in enumerate(plan):
                peer, _ = _peer(k)
                _, blk = _peer(j)
                part = land_refs[w].at[pl.ds(blk * rows, rows)]
                i = w * npl + p
                pltpu.make_async_remote_copy(src_ref=part, dst_ref=part, send_sem=send.at[i], recv_sem=recv.at[i],
                                             device_id=peer, device_id_type=MESH_ID).start()
        token[...] = jnp.zeros_like(token)

    nsem = n * npl
    bufs = [pltpu.with_memory_space_constraint(a, pltpu.HBM) for a in lands]
    out = pl.pallas_call(
        body, name=name,
        in_specs=[HBM_SPEC] * n + [pl.BlockSpec(memory_space=pl.ANY)] * na,
        out_specs=[SEM_SPEC, SEM_SPEC] + [HBM_SPEC] * n + [pl.BlockSpec(memory_space=pltpu.VMEM)],
        out_shape=[pltpu.SemaphoreType.DMA((nsem,)), pltpu.SemaphoreType.DMA((nsem,))]
        + [pltpu.HBM(a.shape, a.dtype) for a in bufs] + [jax.ShapeDtypeStruct((8, 128), F32)],
        input_output_aliases={i: 2 + i for i in range(n)},
        compiler_params=pltpu.CompilerParams(has_side_effects=EFFECT),
    )(*bufs, *order_after)
    return out[0], out[1], out[2:2 + n], out[-1]


def _gather_wait(started, plan, order_after, name):
    send, recv, lands, _ = started
    n = len(lands)
    na = len(order_after)
    npl = len(plan)

    def body(*refs):
        land_refs = refs[:n]
        send_ref, recv_ref = refs[n], refs[n + 1]
        for w in range(n):
            rows = lands[w].shape[0] // NDEV
            for p, (k, j) in enumerate(plan):
                peer, _ = _peer(k)
                _, blk = _peer(j)
                part = land_refs[w].at[pl.ds(blk * rows, rows)]
                i = w * npl + p
                cp = pltpu.make_async_remote_copy(src_ref=part, dst_ref=part, send_sem=send_ref.at[i],
                                                  recv_sem=recv_ref.at[i], device_id=peer, device_id_type=MESH_ID)
                cp.wait_send()
                cp.wait_recv()

    out = pl.pallas_call(
        body, name=name,
        in_specs=[HBM_SPEC] * n + [SEM_SPEC, SEM_SPEC] + [pl.BlockSpec(memory_space=pl.ANY)] * na,
        out_specs=[HBM_SPEC] * n,
        out_shape=[pltpu.HBM(a.shape, a.dtype) for a in lands],
        input_output_aliases={i: i for i in range(n)},
        compiler_params=pltpu.CompilerParams(has_side_effects=EFFECT),
    )(*lands, send, recv, *order_after)
    return list(out)


def _copy_ends(kind, src, land, me, plin, k):
    if kind == "scatter":
        rows = src.shape[0] // NDEV
        return src.at[pl.ds(plin * rows, rows)], land.at[k - 1]
    return src, land.at[me]


def _landing(kind, src):
    me = _my_place()
    if kind == "scatter":
        return lax.empty((NDEV - 1, src.shape[0] // NDEV) + src.shape[1:], src.dtype)
    land = lax.empty((NDEV,) + src.shape, src.dtype)
    return lax.dynamic_update_slice(land, src[None], (me,) + (0,) * src.ndim)


def _send_start(kinds, srcs, order_after, name):
    n = len(srcs)
    lands = [_landing(kd, s) for kd, s in zip(kinds, srcs)]
    na = len(order_after)

    def body(*refs):
        src_refs, land_refs = refs[:n], refs[n:2 * n]
        send, recv = refs[2 * n + na], refs[2 * n + na + 1]
        token = refs[-1]
        _, me = _peer(0)
        for w in range(n):
            for k in range(1, NDEV):
                peer, plin = _peer(k)
                s, d = _copy_ends(kinds[w], src_refs[w], land_refs[w], me, plin, k)
                i = w * (NDEV - 1) + k - 1
                pltpu.make_async_remote_copy(src_ref=s, dst_ref=d, send_sem=send.at[i], recv_sem=recv.at[i],
                                             device_id=peer, device_id_type=MESH_ID).start()
        token[...] = jnp.zeros_like(token)

    nsem = n * (NDEV - 1)
    bufs = [pltpu.with_memory_space_constraint(a, pltpu.HBM) for a in list(srcs) + lands]
    out = pl.pallas_call(
        body, name=name,
        in_specs=[HBM_SPEC] * (2 * n) + [pl.BlockSpec(memory_space=pl.ANY)] * na,
        out_specs=[SEM_SPEC, SEM_SPEC] + [HBM_SPEC] * (2 * n) + [pl.BlockSpec(memory_space=pltpu.VMEM)],
        out_shape=[pltpu.SemaphoreType.DMA((nsem,)), pltpu.SemaphoreType.DMA((nsem,))]
        + [pltpu.HBM(a.shape, a.dtype) for a in bufs] + [jax.ShapeDtypeStruct((8, 128), F32)],
        input_output_aliases={i: 2 + i for i in range(2 * n)},
        compiler_params=pltpu.CompilerParams(has_side_effects=EFFECT),
    )(*bufs, *order_after)
    return out[0], out[1], out[2:2 + n], out[2 + n:2 + 2 * n], out[-1]


def _send_wait(kinds, started, order_after, name):
    send, recv, srcs, lands, _ = started
    n = len(srcs)
    na = len(order_after)

    def body(*refs):
        src_refs, land_refs = refs[:n], refs[n:2 * n]
        send_ref, recv_ref = refs[2 * n], refs[2 * n + 1]
        _, me = _peer(0)
        for w in range(n):
            for k in range(1, NDEV):
                peer, plin = _peer(k)
                s, d = _copy_ends(kinds[w], src_refs[w], land_refs[w], me, plin, k)
                i = w * (NDEV - 1) + k - 1
                cp = pltpu.make_async_remote_copy(src_ref=s, dst_ref=d, send_sem=send_ref.at[i],
                                                  recv_sem=recv_ref.at[i], device_id=peer, device_id_type=MESH_ID)
                cp.wait_send()
                cp.wait_recv()

    bufs = list(srcs) + list(lands)
    out = pl.pallas_call(
        body, name=name,
        in_specs=[HBM_SPEC] * (2 * n) + [SEM_SPEC, SEM_SPEC] + [pl.BlockSpec(memory_space=pl.ANY)] * na,
        out_specs=[HBM_SPEC] * (2 * n),
        out_shape=[pltpu.HBM(a.shape, a.dtype) for a in bufs],
        input_output_aliases={i: i for i in range(2 * n)},
        compiler_params=pltpu.CompilerParams(has_side_effects=EFFECT),
    )(*bufs, send, recv, *order_after)
    return out[:n], out[n:]


def _gsum(own, land, name):
    rows, cols = own.shape
    tr = rows // 2 if rows * cols > 512 * 1024 and rows % 32 == 0 else rows

    def body(own_ref, l_ref, o_ref):
        tot = own_ref[...].astype(F32)
        for s in range(NDEV - 1):
            tot = tot + l_ref[s].astype(F32)
        o_ref[...] = tot

    return pl.pallas_call(
        body, name=name, grid=(rows // tr,),
        in_specs=[pl.BlockSpec((tr, cols), lambda i: (i, 0)),
                  pl.BlockSpec((NDEV - 1, tr, cols), lambda i: (0, i, 0))],
        out_specs=pl.BlockSpec((tr, cols), lambda i: (i, 0)),
        out_shape=jax.ShapeDtypeStruct((rows, cols), F32),
        compiler_params=_cp(("parallel",)),
    )(own, land)


def _adamw_math(w, g, m, v):
    m2 = B1 * m + (1.0 - B1) * g
    v2 = B2 * v + (1.0 - B2) * (g * g)
    m_hat = m2 / (1.0 - B1 ** STEP)
    v_hat = v2 / (1.0 - B2 ** STEP)
    delta = -LR * (m_hat / (jnp.sqrt(v_hat) + AEPS) + WD * w)
    return delta, m2, v2


def _adamw(w, g, m, v, name):
    rows, cols = w.shape
    tr = 256 if rows % 256 == 0 and rows > 256 else rows

    def body(w_ref, g_ref, m_ref, v_ref, d_ref, mo_ref, vo_ref):
        d, m2, v2 = _adamw_math(w_ref[...], g_ref[...], m_ref[...], v_ref[...])
        d_ref[...] = d
        mo_ref[...] = m2
        vo_ref[...] = v2

    blk = pl.BlockSpec((tr, cols), lambda i: (i, 0))
    return pl.pallas_call(
        body, name=name, grid=(rows // tr,), in_specs=[blk] * 4, out_specs=[blk] * 3,
        out_shape=[jax.ShapeDtypeStruct((rows, cols), F32)] * 3,
        compiler_params=_cp(("parallel",)),
    )(w, g, m, v)


UPD_TC = 256


def _update(src, land, w, m, v, name):
    rows, cols = land.shape[1:]
    tc = min(UPD_TC if rows > 512 else 2 * UPD_TC, cols)
    me = jnp.reshape(_my_place(), (1,)).astype(jnp.int32)

    def body(me_ref, own_ref, l_ref, w_ref, m_ref, v_ref, g_ref, d_ref, mo_ref, vo_ref):
        del me_ref
        g = own_ref[...].astype(F32)
        for s in range(NDEV - 1):
            g = g + l_ref[s].astype(F32)
        g_ref[...] = g
        d, m2, v2 = _adamw_math(w_ref[...], g, m_ref[...], v_ref[...])
        d_ref[...] = d
        mo_ref[...] = m2
        vo_ref[...] = v2

    wblk = pl.BlockSpec((rows, tc), lambda j, p: (0, j))
    grid_spec = pltpu.PrefetchScalarGridSpec(
        num_scalar_prefetch=1, grid=(cols // tc,),
        in_specs=[pl.BlockSpec((rows, tc), lambda j, p: (p[0], j)),
                  pl.BlockSpec((NDEV - 1, rows, tc), lambda j, p: (0, 0, j)), wblk, wblk, wblk],
        out_specs=[wblk] * 4)
    return pl.pallas_call(
        body, name=name, grid_spec=grid_spec, out_shape=[jax.ShapeDtypeStruct((rows, cols), F32)] * 4,
        compiler_params=_cp(("parallel",)),
    )(me, src, land, w, m, v)


def _small_update(vland, w8, m8, v8, name):
    def body(l_ref, w_ref, m_ref, v_ref, g_ref, d_ref, mo_ref, vo_ref):
        g = l_ref[0]
        for s in range(1, NDEV):
            g = g + l_ref[s]
        g_ref[...] = g
        d, m2, v2 = _adamw_math(w_ref[...], g, m_ref[...], v_ref[...])
        d_ref[...] = d
        mo_ref[...] = m2
        vo_ref[...] = v2

    return pl.pallas_call(
        body, name=name, out_shape=[jax.ShapeDtypeStruct((8, D), F32)] * 4,
        compiler_params=_cp(None),
    )(vland, w8, m8, v8)


def kernel(x, ffn1_norm, ffn1_w_gate, ffn1_w_up, ffn1_w_down, mix_norm, w_in, conv_dw_kernel, conv_dw_bias, conv_ln_gain, conv_ln_bias, conv_w_out, attn_w_out, w_o, ffn2_norm, ffn2_w_gate, ffn2_w_up, ffn2_w_down, final_norm, loss_target, m_ffn1_norm, m_ffn1_w_gate, m_ffn1_w_up, m_ffn1_w_down, m_mix_norm, m_w_in, m_conv_dw_kernel, m_conv_dw_bias, m_conv_ln_gain, m_conv_ln_bias, m_conv_w_out, m_attn_w_out, m_w_o, m_ffn2_norm, m_ffn2_w_gate, m_ffn2_w_up, m_ffn2_w_down, m_final_norm, v_ffn1_norm, v_ffn1_w_gate, v_ffn1_w_up, v_ffn1_w_down, v_mix_norm, v_w_in, v_conv_dw_kernel, v_conv_dw_bias, v_conv_ln_gain, v_conv_ln_bias, v_conv_w_out, v_attn_w_out, v_w_o, v_ffn2_norm, v_ffn2_w_gate, v_ffn2_w_up, v_ffn2_w_down, v_final_norm):
    names = ["ffn1_norm", "ffn1_w_gate", "ffn1_w_up", "ffn1_w_down", "mix_norm", "w_in", "conv_dw_kernel",
             "conv_dw_bias", "conv_ln_gain", "conv_ln_bias", "conv_w_out", "attn_w_out", "w_o", "ffn2_norm",
             "ffn2_w_gate", "ffn2_w_up", "ffn2_w_down", "final_norm"]
    w = dict(ffn1_norm=ffn1_norm, ffn1_w_gate=ffn1_w_gate, ffn1_w_up=ffn1_w_up, ffn1_w_down=ffn1_w_down, mix_norm=mix_norm, w_in=w_in, conv_dw_kernel=conv_dw_kernel, conv_dw_bias=conv_dw_bias, conv_ln_gain=conv_ln_gain, conv_ln_bias=conv_ln_bias, conv_w_out=conv_w_out, attn_w_out=attn_w_out, w_o=w_o, ffn2_norm=ffn2_norm, ffn2_w_gate=ffn2_w_gate, ffn2_w_up=ffn2_w_up, ffn2_w_down=ffn2_w_down, final_norm=final_norm)
    mo = dict(ffn1_norm=m_ffn1_norm, ffn1_w_gate=m_ffn1_w_gate, ffn1_w_up=m_ffn1_w_up, ffn1_w_down=m_ffn1_w_down, mix_norm=m_mix_norm, w_in=m_w_in, conv_dw_kernel=m_conv_dw_kernel, conv_dw_bias=m_conv_dw_bias, conv_ln_gain=m_conv_ln_gain, conv_ln_bias=m_conv_ln_bias, conv_w_out=m_conv_w_out, attn_w_out=m_attn_w_out, w_o=m_w_o, ffn2_norm=m_ffn2_norm, ffn2_w_gate=m_ffn2_w_gate, ffn2_w_up=m_ffn2_w_up, ffn2_w_down=m_ffn2_w_down, final_norm=m_final_norm)
    vo = dict(ffn1_norm=v_ffn1_norm, ffn1_w_gate=v_ffn1_w_gate, ffn1_w_up=v_ffn1_w_up, ffn1_w_down=v_ffn1_w_down, mix_norm=v_mix_norm, w_in=v_w_in, conv_dw_kernel=v_conv_dw_kernel, conv_dw_bias=v_conv_dw_bias, conv_ln_gain=v_conv_ln_gain, conv_ln_bias=v_conv_ln_bias, conv_w_out=v_conv_w_out, attn_w_out=v_attn_w_out, w_o=v_w_o, ffn2_norm=v_ffn2_norm, ffn2_w_gate=v_ffn2_w_gate, ffn2_w_up=v_ffn2_w_up, ffn2_w_down=v_ffn2_w_down, final_norm=v_final_norm)
    col_sharded = ("ffn1_w_gate", "ffn1_w_up", "w_in", "attn_w_out", "ffn2_w_gate", "ffn2_w_up")
    row_sharded = ("ffn1_w_down", "conv_w_out", "w_o", "ffn2_w_down")
    small = ("ffn1_norm", "mix_norm", "ffn2_norm", "final_norm", "conv_dw_bias", "conv_ln_gain", "conv_ln_bias")

    def landing_view(a, n):
        return jnp.transpose(a[0]) if n in col_sharded else a[0]

    def own_view(a, n):
        return jnp.transpose(a)[None] if n in col_sharded else a[None]

    ag_groups = (("ffn1_w_gate", "ffn1_w_up", "ffn1_w_down"),
                 ("w_in", "conv_dw_kernel"),
                 ("attn_w_out", "conv_w_out", "w_o", "ffn2_w_gate", "ffn2_w_up", "ffn2_w_down"))
    ag, order = [], []
    for gi, grp in enumerate(ag_groups):
        lands = _prep_gather([landing_view(w[n], n) for n in grp], order, f"gather_prep{gi}")
        st = _gather_start(lands, GATHER_A, [], f"gather_a_start{gi}")
        ag.append(st)
        order = [st[3]]

    def chips_in(gi, after):
        lands = _gather_wait(ag[gi], GATHER_A, after, f"gather_a_wait{gi}")
        return _gather_start(lands, GATHER_B, [], f"gather_b_start{gi}")

    def all_in(gi, st, after):
        return _gather_wait(st, GATHER_B, after, f"gather_b_wait{gi}")

    x0 = x[0]
    tgt = loss_target[0]
    gf = final_norm.reshape(1, D)

    wg1, wu1, wd1 = all_in(0, chips_in(0, [ag[2][3]]), [])
    x1, gg1, uu1, h2p = _ffn_fwd(x0, ffn1_norm, wg1, wu1, wd1, "ffn1_fwd", next_gain=mix_norm)
    h2 = h2p[0]
    win_t, kern_blocks = all_in(1, chips_in(1, [x1]), [])
    kern = kern_blocks.reshape(NDEV, 32, D // NDEV).transpose(1, 0, 2).reshape(32, D)
    ptm = min(T, 2048)
    ab = _mm(h2, win_t, mode="nt", m=T, n=2 * D, k=D, tm=ptm, tn=512, tk=D, out_dtype=BF16, name="proj_conv")
    gates = _mm(h2, win_t, mode="nt", m=T, n=2 * D, k=D, tm=ptm, tn=512, tk=D, out_dtype=BF16,
                b_map=lambda i, j, kk: (13 + j, 0), name="proj_gates")
    qkv = []
    for gi in range(len(GROUPS)):
        qkv.append(_mm(h2p[gi], win_t, mode="nt", m=T, n=3 * AW, k=D, tm=ptm, tn=AW, tk=D, out_dtype=BF16,
                       b_map=lambda i, j, kk, gi=gi: (4 + gi + 3 * j, 0), name=f"proj_qkv{gi}"))
    z1, z3b = _conv_fwd(ab, kern, conv_dw_bias, conv_ln_gain, conv_ln_bias, "conv_fwd")
    late_b = chips_in(2, [z3b])
    outs, lses = [], []
    for gi, (_, dil) in enumerate(GROUPS):
        o, l = _attn_fwd(qkv[gi], gi, f"attn_fwd{gi}")
        outs.append(o)
        lses.append(l)
    attnb, lse = _merge(outs, lses, "attn_merge")
    wa_t, wc, wo, wg2, wu2, wd2 = all_in(2, late_b, [attnb])
    x2, yc, ya, mixedb = _mix_out(z3b, attnb, gates, wc, wa_t, wo, x1, "mix_out_fwd")
    gg2, uu2, dx3, dgf, loss_part = _ffn_fwd(x2, ffn2_norm, wg2, wu2, wd2, "ffn2_fwd", loss_of=(gf, tgt))

    dx2, dg3, dgb, dub, actb, hb, dob = _ffn_bwd(x2, ffn2_norm, gg2, uu2, dx3, wg2, wu2, wd2, "ffn2_bwd")
    grads = {}
    grads["ffn2_w_down"] = _wgrad(actb, dob, FF, D, "ffn2_dwd")
    rs_groups = [("ffn2_w_gate", "ffn2_w_up", "ffn2_w_down"),
                 ("attn_w_out", "conv_w_out", "w_o", "conv_dw_kernel"),
                 ("w_in",),
                 ("ffn1_w_gate",), ("ffn1_w_up",), ("ffn1_w_down",), ()]
    last = len(rs_groups) - 1
    rs = []

    dgates, dycb, dyab, dx2b, dz3, dattnb, delta = _mix_out_bwd(dx2, gates, yc, ya, attnb, wc, wa_t, wo, "mix_out_bwd")
    grads["w_o"] = _wgrad(mixedb, dx2b, D, D, "dw_o")
    grads["conv_w_out"] = _wgrad(z3b, dycb, D, D, "dw_conv_out")
    grads["attn_w_out"] = _wgrad(dyab, attnb, D, AW, "dw_attn_out")
    dab, dkern, dvec, grads["ffn2_w_gate"], grads["ffn2_w_up"] = _conv_bwd(
        dz3, z1, ab, kern, conv_ln_gain, conv_ln_bias, "conv_bwd", guest_lhs=(dgb, dub), guest_rhs=hb)
    grads["conv_dw_kernel"] = dkern.reshape(32, NDEV, D // NDEV).transpose(1, 0, 2).reshape(NDEV * 32, D // NDEV)
    rs.append(_send_start(["scatter"] * 3, [grads[n] for n in rs_groups[0]], [], "scatter_start0"))
    rs.append(_send_start(["scatter"] * 4, [grads[n] for n in rs_groups[1]], [rs[0][4]], "scatter_start1"))
    dattnb = [_tie(a, [rs[1][4]], f"tie_after_scatter1_{i}") for i, a in enumerate(dattnb)]

    dqkv = []
    for gi, (_, dil) in enumerate(GROUPS):
        dq3 = _attn_bwd(qkv[gi], dattnb[gi], lse[gi], delta[gi], gi, f"attn_bwd{gi}")
        dqkv.append(dq3.reshape(3 * T, AW))

    wtk = min(T, 2048)
    dwin = _mm(dab, h2, mode="tn", m=2 * D, n=D, k=T, tm=2 * D, tn=D, tk=wtk, out_dtype=BF16, out_rows=IN_W,
               name="dw_in_conv")
    dwin = _mm(dgates, h2, mode="tn", m=2 * D, n=D, k=T, tm=512, tn=D, tk=wtk, out_dtype=BF16, out_rows=IN_W,
               o_map=lambda i, j, kk: (13 + i, 0), passthru=dwin, name="dw_in_gates")
    for gi in range(3):
        dwin = _mm(dqkv[gi], h2p[gi], mode="tn", m=3 * AW, n=D, k=T, tm=AW, tn=D, tk=wtk, out_dtype=BF16,
                   out_rows=IN_W, a_map=lambda i, j, kk: (i * (T // wtk) + kk, 0),
                   o_map=lambda i, j, kk, gi=gi: (4 + gi + 3 * i, 0), passthru=dwin, name=f"dw_in_qkv{gi}")
    grads["w_in"] = dwin
    rs.append(_send_start(["scatter"], [dwin], [rs[1][4]], "scatter_start2"))
    dab = _tie(dab, [rs[2][4]], "tie_after_scatter2")

    nrow = T // 1024
    dh = _mm(dab, win_t, mode="nn", m=T, n=D, k=2 * D, tm=1024, tn=D, tk=2 * D, out_dtype=F32, name="dproj_conv")
    dh = _mm(dgates, win_t[IN_W - 2 * D:], mode="nn", m=T, n=D, k=2 * D, tm=1024, tn=D, tk=2 * D, out_dtype=F32,
             init=dh, name="dproj_gates")
    dhs = []
    for gi, (_, dil) in enumerate(GROUPS):
        part = _mm(dqkv[gi], win_t, mode="nn", m=T, n=D, k=3 * AW, tm=1024, tn=D, tk=AW,
                   out_dtype=F32 if gi == 0 else BF16,
                   a_map=lambda i, j, kk: (kk * nrow + i, 0), b_map=lambda i, j, kk, gi=gi: (4 + gi + 3 * kk, 0),
                   init=dh if gi == 0 else None, name=f"dproj_qkv{gi}")
        dhs.append(part)
    dx1, dg2 = _rms_bwd(x1, mix_norm, dhs, dx2, "mix_norm_bwd")

    dgb, dub, actb, hb, dob = _ffn_bwd_pre(x0, ffn1_norm, gg1, uu1, dx1, wd1, "ffn1_bwd_pre")
    grads["ffn1_w_gate"] = _wgrad(dgb, hb, FF, D, "ffn1_dwg")
    rs.append(_send_start(["scatter"], [grads["ffn1_w_gate"]], [rs[2][4]], "scatter_start3"))
    hb = _tie(hb, [rs[3][4]], "tie_after_scatter3")
    grads["ffn1_w_up"] = _wgrad(dub, hb, FF, D, "ffn1_dwu")
    rs.append(_send_start(["scatter"], [grads["ffn1_w_up"]], [rs[3][4]], "scatter_start4"))
    dob = _tie(dob, [rs[4][4]], "tie_after_scatter4")
    grads["ffn1_w_down"] = _wgrad(actb, dob, FF, D, "ffn1_dwd")
    rs.append(_send_start(["scatter"], [grads["ffn1_w_down"]], [rs[4][4]], "scatter_start5"))
    dgb = _tie(dgb, [rs[5][4]], "tie_after_scatter5")
    dx0, dg1 = _ffn_bwd_dx(x0, ffn1_norm, dgb, dub, dx1, wg1, wu1, "ffn1_bwd_dx")
    vec = jnp.concatenate([dg1, dg2, dg3, dgf, dvec[0:3], jnp.broadcast_to(loss_part[:, :1], (1, D))], axis=0)
    rs.append(_send_start(["bcast"], [vec], [rs[5][4]], "scatter_start6"))

    g_out, d_out, m_out, v_out = {}, {}, {}, {}
    me = _my_place()
    after = [rs[last][4]]
    for gi, grp in enumerate(rs_groups):
        kinds = ["scatter"] * len(grp) + (["bcast"] if gi == last else [])
        srcs, lands = _send_wait(kinds, rs[gi], after, f"scatter_wait{gi}")
        for n, src, land in zip(grp, srcs, lands):
            if n == "conv_dw_kernel":
                rows = src.shape[0] // NDEV
                own = lax.dynamic_slice(src, (me * rows, 0), (rows, src.shape[1]))
                g = _gsum(own, land, f"gsum_{n}")[:CONV_W]
                d, m2, v2 = _adamw(w[n][0], g, mo[n][0], vo[n][0], f"adamw_{n}")
                after = [d]
                g, d, m2, v2 = g[None], d[None], m2[None], v2[None]
            else:
                res = _update(src, land, landing_view(w[n], n), landing_view(mo[n], n), landing_view(vo[n], n),
                              f"update_{n}")
                after = [res[1]]
                g, d, m2, v2 = (own_view(a, n) for a in res)
            g_out[n], d_out[n], m_out[n], v_out[n] = g, d, m2, v2
    vland = lands[-1]

    def rows8(src):
        return jnp.concatenate([src[n].reshape(1, D) for n in small] + [jnp.ones((1, D), F32)], axis=0)

    g8, d8, m8, v8 = _small_update(vland, rows8(w), rows8(mo), rows8(vo), "small_update")
    for r, n in enumerate(small):
        shp = w[n].shape
        g_out[n], d_out[n], m_out[n], v_out[n] = (a[r].reshape(shp) for a in (g8, d8, m8, v8))
    loss = g8[7, 0]

    return (loss, dx0[None], *[g_out[n] for n in names], *[d_out[n] for n in names],
            *[m_out[n] for n in names], *[v_out[n] for n in names])
```

```python
import numpy as np
import jax
import jax.numpy as jnp
from jax import lax
from jax.experimental import pallas as pl
from jax.experimental.pallas import tpu as pltpu

F32 = jnp.float32
BF16 = jnp.bfloat16

T = 4096
D = 1024
FF = 2816
NDEV = 8
CONV_W = 31
HEAD = 128
BLK = 128
GROUPS = ((128, 1), (512, 4), (2048, 16))
NHG = 4
AW = NHG * HEAD
IN_W = 2 * D + 3 * 3 * AW + 2 * D
EPS = 1e-6
B1, B2, LR, AEPS, WD, STEP = 0.9, 0.999, 0.001, 1e-08, 0.01, 10
NEG = -1e30
VMEM_LIMIT = 56 * 1024 * 1024
MESH_ID = pl.DeviceIdType.MESH

NT = (((1,), (1,)), ((), ()))
NN = (((1,), (0,)), ((), ()))
TN = (((0,), (0,)), ((), ()))
_DIMS = {"nn": NN, "nt": NT, "tn": TN}


def _cp(sem=None):
    return pltpu.CompilerParams(dimension_semantics=sem, vmem_limit_bytes=VMEM_LIMIT)


def _sig(v):
    return 1.0 / (1.0 + jnp.exp(-v))


def _dot(a, b, dims):
    return lax.dot_general(a, b, dims, preferred_element_type=F32)


def _const_spec(shape):
    nd = len(shape)
    return pl.BlockSpec(shape, lambda *_: (0,) * nd)


def _mm(a, b, *, mode, m, n, k, tm, tn, tk, out_dtype, name, a_map=None, b_map=None,
        o_map=None, out_rows=None, init=None, passthru=None):
    gi, gj, gk = m // tm, n // tn, k // tk
    assert gi * tm == m and gj * tn == n and gk * tk == k, (name, m, n, k, tm, tn, tk)
    if mode == "nn":
        a_blk, b_blk = (tm, tk), (tk, tn)
        da, db = (lambda i, j, kk: (i, kk)), (lambda i, j, kk: (kk, j))
    elif mode == "nt":
        a_blk, b_blk = (tm, tk), (tn, tk)
        da, db = (lambda i, j, kk: (i, kk)), (lambda i, j, kk: (j, kk))
    else:
        a_blk, b_blk = (tk, tm), (tk, tn)
        da, db = (lambda i, j, kk: (kk, i)), (lambda i, j, kk: (kk, j))
    a_map = a_map or da
    b_map = b_map or db
    o_map = o_map or (lambda i, j, kk: (i, j))
    dims = _DIMS[mode]
    extra = init if init is not None else passthru
    out_rows = out_rows or m

    def body(*refs):
        if init is not None:
            a_ref, b_ref, i_ref, o_ref = refs[:4]
        elif passthru is not None:
            a_ref, b_ref, _, o_ref = refs[:4]
        else:
            a_ref, b_ref, o_ref = refs[:3]
        if gk == 1:
            prod = _dot(a_ref[...], b_ref[...], dims)
            if init is not None:
                prod = prod + i_ref[...].astype(F32)
            o_ref[...] = prod.astype(out_dtype)
            return
        acc = refs[-1]
        kk = pl.program_id(2)

        @pl.when(kk == 0)
        def _():
            if init is not None:
                acc[...] = i_ref[...].astype(F32)
            else:
                acc[...] = jnp.zeros_like(acc)

        acc[...] += _dot(a_ref[...], b_ref[...], dims)

        @pl.when(kk == gk - 1)
        def _():
            o_ref[...] = acc[...].astype(out_dtype)

    in_specs = [pl.BlockSpec(a_blk, a_map), pl.BlockSpec(b_blk, b_map)]
    args = [a, b]
    aliases = {}
    if init is not None:
        in_specs.append(pl.BlockSpec((tm, tn), o_map))
        args.append(init)
        aliases = {2: 0}
    elif passthru is not None:
        in_specs.append(pl.BlockSpec(memory_space=pl.ANY))
        args.append(passthru)
        aliases = {2: 0}
    out_dt = extra.dtype if extra is not None else out_dtype
    assert out_dt == out_dtype
    return pl.pallas_call(
        body, name=name, grid=(gi, gj, gk),
        in_specs=in_specs, out_specs=pl.BlockSpec((tm, tn), o_map),
        out_shape=jax.ShapeDtypeStruct((out_rows, n), out_dtype),
        scratch_shapes=[pltpu.VMEM((tm, tn), F32)] if gk > 1 else [],
        input_output_aliases=aliases,
        compiler_params=_cp(("parallel", "parallel", "arbitrary")),
    )(*args)


def _ffn_fwd(x, g, wg_t, wu_t, wd, name, next_gain=None, loss_of=None):
    tm, fc = PERM_TM, 256
    nc = FF // fc
    n_in = 5 + (1 if next_gain is not None else 0) + (2 if loss_of is not None else 0)

    def body(*refs):
        x_ref, g_ref, wg_ref, wu_ref, wd_ref = refs[:5]
        extra_in, outs = refs[5:n_in], refs[n_in:]
        act_ref = outs[-1]
        xv = x_ref[...]
        r = lax.rsqrt(jnp.mean(xv * xv, axis=-1, keepdims=True) + EPS)
        h = (xv * r * g_ref[...]).astype(BF16)
        gg_ref, uu_ref = (outs[0], outs[1]) if loss_of is not None else (outs[1], outs[2])
        for c in range(nc):
            sl = pl.ds(c * fc, fc)
            gg = _dot(h, wg_ref[sl, :], NT)
            uu = _dot(h, wu_ref[sl, :], NT)
            gg_ref[:, sl] = gg.astype(BF16)
            uu_ref[:, sl] = uu.astype(BF16)
            act_ref[:, sl] = (gg * _sig(gg) * uu).astype(BF16)
        y = xv + 0.5 * _dot(act_ref[...], wd_ref[...], NN)
        if loss_of is not None:
            _final_math(y, extra_in[0][...], extra_in[1][...], outs[2], outs[3], outs[4], pl.program_id(0))
            return
        outs[0][...] = y
        if next_gain is not None:
            tile = outs[-2]
            r2 = lax.rsqrt(jnp.mean(y * y, axis=-1, keepdims=True) + EPS)
            hv = y * r2 * extra_in[0][...]
            outs[3][...] = hv.astype(BF16)
            _put_tile(tile, hv)
            for dil, p_ref in zip(DILS, outs[4:4 + len(DILS)]):
                _store_perm(p_ref, tile, dil)

    wspec = pl.BlockSpec((FF, D), lambda i: (0, 0), pipeline_mode=pl.Buffered(1))
    row_d = pl.BlockSpec((tm, D), lambda i: (i, 0))
    row_f = pl.BlockSpec((tm, FF), lambda i: (i, 0))
    in_specs = [row_d, _const_spec((1, D)), wspec, wspec, wspec]
    args = [x, g, wg_t, wu_t, wd]
    f_shape = jax.ShapeDtypeStruct((T, FF), BF16)
    scratch = [pltpu.VMEM((tm, FF), BF16)]
    if loss_of is not None:
        in_specs += [_const_spec((1, D)), row_d]
        args += list(loss_of)
        out_specs = [row_f, row_f, row_d, _const_spec((1, D)), _const_spec((1, 128))]
        out_shape = [f_shape, f_shape, jax.ShapeDtypeStruct((T, D), F32), jax.ShapeDtypeStruct((1, D), F32),
                     jax.ShapeDtypeStruct((1, 128), F32)]
    else:
        out_specs = [row_d, row_f, row_f]
        out_shape = [jax.ShapeDtypeStruct((T, D), F32), f_shape, f_shape]
        if next_gain is not None:
            in_specs.append(_const_spec((1, D)))
            args.append(next_gain)
            out_specs += [row_d] + [_perm_spec(d, D) for d in DILS]
            out_shape += [jax.ShapeDtypeStruct((T, D), BF16)] + [_perm_shape(d, D, BF16) for d in DILS]
            scratch = [_tile_scratch(D)] + scratch
    out = pl.pallas_call(
        body, name=name, grid=(T // tm,), in_specs=in_specs, out_specs=out_specs, out_shape=out_shape,
        scratch_shapes=scratch,
        compiler_params=_cp(("arbitrary",) if loss_of is not None else ("parallel",)),
    )(*args)
    if next_gain is not None:
        return out[0], out[1], out[2], [out[3]] + [o.reshape(T, D) for o in out[4:]]
    return tuple(out)


def _ffn_bwd(x, g, gg_all, uu_all, dout, wg_t, wu_t, wd, name):
    tm, fc = 256, 256
    nc = FF // fc

    def body(x_ref, g_ref, gg_ref, uu_ref, do_ref, wg_ref, wu_ref, wd_ref,
             dx_ref, dgam_ref, dg_ref, du_ref, act_ref, h_ref, db_ref):
        i = pl.program_id(0)
        xv = x_ref[...]
        r = lax.rsqrt(jnp.mean(xv * xv, axis=-1, keepdims=True) + EPS)
        xhat = xv * r
        gam = g_ref[...]
        h_ref[...] = (xhat * gam).astype(BF16)
        dov = do_ref[...]
        dbv = (0.5 * dov).astype(BF16)
        db_ref[...] = dbv
        for c in range(nc):
            sl = pl.ds(c * fc, fc)
            da = _dot(dbv, wd_ref[sl, :], NT)
            gg = gg_ref[:, sl].astype(F32)
            uu = uu_ref[:, sl].astype(F32)
            s = _sig(gg)
            si = gg * s
            dgv = (da * uu * (s * (1.0 + gg * (1.0 - s)))).astype(BF16)
            duv = (da * si).astype(BF16)
            dg_ref[:, sl] = dgv
            du_ref[:, sl] = duv
            act_ref[:, sl] = (si * uu).astype(BF16)
        dh = _dot(dg_ref[...], wg_ref[...], NN) + _dot(du_ref[...], wu_ref[...], NN)

        @pl.when(i == 0)
        def _():
            dgam_ref[...] = jnp.zeros_like(dgam_ref)

        dgam_ref[...] += jnp.sum(dh * xhat, axis=0, keepdims=True)
        dxh = dh * gam
        dx_ref[...] = dov + r * (dxh - xhat * jnp.mean(dxh * xhat, axis=-1, keepdims=True))

    wspec = pl.BlockSpec((FF, D), lambda i: (0, 0), pipeline_mode=pl.Buffered(1))
    row_d = pl.BlockSpec((tm, D), lambda i: (i, 0))
    row_f = pl.BlockSpec((tm, FF), lambda i: (i, 0))
    return pl.pallas_call(
        body, name=name, grid=(T // tm,),
        in_specs=[row_d, _const_spec((1, D)), row_f, row_f, row_d, wspec, wspec, wspec],
        out_specs=[row_d, _const_spec((1, D)), row_f, row_f, row_f, row_d, row_d],
        out_shape=[jax.ShapeDtypeStruct((T, D), F32), jax.ShapeDtypeStruct((1, D), F32),
                   jax.ShapeDtypeStruct((T, FF), BF16), jax.ShapeDtypeStruct((T, FF), BF16),
                   jax.ShapeDtypeStruct((T, FF), BF16), jax.ShapeDtypeStruct((T, D), BF16),
                   jax.ShapeDtypeStruct((T, D), BF16)],
        compiler_params=_cp(("arbitrary",)),
    )(x, g, gg_all, uu_all, dout, wg_t, wu_t, wd)


def _ffn_bwd_pre(x, g, gg_all, uu_all, dout, wd, name):
    tm, fc = 512, 256
    nc = FF // fc

    def body(x_ref, g_ref, gg_ref, uu_ref, do_ref, wd_ref, dg_ref, du_ref, act_ref, h_ref, db_ref):
        xv = x_ref[...]
        r = lax.rsqrt(jnp.mean(xv * xv, axis=-1, keepdims=True) + EPS)
        h_ref[...] = (xv * r * g_ref[...]).astype(BF16)
        dbv = (0.5 * do_ref[...]).astype(BF16)
        db_ref[...] = dbv
        for c in range(nc):
            sl = pl.ds(c * fc, fc)
            da = _dot(dbv, wd_ref[sl, :], NT)
            gg = gg_ref[:, sl].astype(F32)
            uu = uu_ref[:, sl].astype(F32)
            s = _sig(gg)
            si = gg * s
            dg_ref[:, sl] = (da * uu * (s * (1.0 + gg * (1.0 - s)))).astype(BF16)
            du_ref[:, sl] = (da * si).astype(BF16)
            act_ref[:, sl] = (si * uu).astype(BF16)

    wspec = pl.BlockSpec((FF, D), lambda i: (0, 0), pipeline_mode=pl.Buffered(1))
    row_d = pl.BlockSpec((tm, D), lambda i: (i, 0))
    row_f = pl.BlockSpec((tm, FF), lambda i: (i, 0))
    return pl.pallas_call(
        body, name=name, grid=(T // tm,),
        in_specs=[row_d, _const_spec((1, D)), row_f, row_f, row_d, wspec],
        out_specs=[row_f, row_f, row_f, row_d, row_d],
        out_shape=[jax.ShapeDtypeStruct((T, FF), BF16), jax.ShapeDtypeStruct((T, FF), BF16),
                   jax.ShapeDtypeStruct((T, FF), BF16), jax.ShapeDtypeStruct((T, D), BF16),
                   jax.ShapeDtypeStruct((T, D), BF16)],
        compiler_params=_cp(("parallel",)),
    )(x, g, gg_all, uu_all, dout, wd)


def _ffn_bwd_dx(x, g, dgb, dub, dout, wg_t, wu_t, name):
    tm = 512

    def body(x_ref, g_ref, dg_ref, du_ref, do_ref, wg_ref, wu_ref, dx_ref, dgam_ref):
        i = pl.program_id(0)
        xv = x_ref[...]
        r = lax.rsqrt(jnp.mean(xv * xv, axis=-1, keepdims=True) + EPS)
        xhat = xv * r
        gam = g_ref[...]
        dh = _dot(dg_ref[...], wg_ref[...], NN) + _dot(du_ref[...], wu_ref[...], NN)

        @pl.when(i == 0)
        def _():
            dgam_ref[...] = jnp.zeros_like(dgam_ref)

        dgam_ref[...] += jnp.sum(dh * xhat, axis=0, keepdims=True)
        dxh = dh * gam
        dx_ref[...] = do_ref[...] + r * (dxh - xhat * jnp.mean(dxh * xhat, axis=-1, keepdims=True))

    wspec = pl.BlockSpec((FF, D), lambda i: (0, 0), pipeline_mode=pl.Buffered(1))
    row_d = pl.BlockSpec((tm, D), lambda i: (i, 0))
    row_f = pl.BlockSpec((tm, FF), lambda i: (i, 0))
    return pl.pallas_call(
        body, name=name, grid=(T // tm,),
        in_specs=[row_d, _const_spec((1, D)), row_f, row_f, row_d, wspec, wspec],
        out_specs=[row_d, _const_spec((1, D))],
        out_shape=[jax.ShapeDtypeStruct((T, D), F32), jax.ShapeDtypeStruct((1, D), F32)],
        compiler_params=_cp(("arbitrary",)),
    )(x, g, dgb, dub, dout, wg_t, wu_t)


def _wgrad(a, b, m, n, name):
    tm = m // 2 if m == FF else m
    return _mm(a, b, mode="tn", m=m, n=n, k=T, tm=tm, tn=n, tk=min(T, 2048), out_dtype=BF16, name=name)


PERM_TM = 512
DILS = tuple(d for _, d in GROUPS if d > 1)


def _perm_spec(dil, cols):
    return pl.BlockSpec((dil, PERM_TM // dil, cols), lambda i: (0, i, 0))


def _perm_shape(dil, cols, dtype):
    return jax.ShapeDtypeStruct((dil, T // dil, cols), dtype)


LANES = 128


def _tile_scratch(cols):
    return pltpu.VMEM((cols // LANES, PERM_TM, LANES), F32)


def _put_tile(tile, value):
    for c in range(tile.shape[0]):
        tile[c] = value[:, c * LANES:(c + 1) * LANES]


def _get_tile(tile):
    return jnp.concatenate([tile[c] for c in range(tile.shape[0])], axis=1)


def _store_perm(out_ref, tile, dil):
    for r in range(dil):
        for c in range(tile.shape[0]):
            out_ref[r, :, pl.ds(c * LANES, LANES)] = tile[c, pl.ds(r, PERM_TM // dil, stride=dil), :].astype(
                out_ref.dtype)


def _load_unperm(in_ref, tile, dil):
    for r in range(dil):
        for c in range(tile.shape[0]):
            tile[c, pl.ds(r, PERM_TM // dil, stride=dil), :] = in_ref[r, :, pl.ds(c * LANES, LANES)].astype(F32)


def _final_math(xv, gam, tgt, dx_ref, dgam_ref, loss_ref, i):
    r = lax.rsqrt(jnp.mean(xv * xv, axis=-1, keepdims=True) + EPS)
    xhat = xv * r
    err = xhat * gam - tgt
    part = 0.5 * jnp.sum(jnp.mean(err * err, axis=-1, keepdims=True), axis=0, keepdims=True)
    dy = err * (1.0 / D)

    @pl.when(i == 0)
    def _():
        dgam_ref[...] = jnp.zeros_like(dgam_ref)
        loss_ref[...] = jnp.zeros_like(loss_ref)

    dgam_ref[...] += jnp.sum(dy * xhat, axis=0, keepdims=True)
    loss_ref[...] += jnp.broadcast_to(part, loss_ref.shape)
    dxh = dy * gam
    dx_ref[...] = r * (dxh - xhat * jnp.mean(dxh * xhat, axis=-1, keepdims=True))


def _rms_bwd(x, g, dhs, dres, name):
    tm = PERM_TM
    dils = [d for _, d in GROUPS]
    nh = len(dhs)
    assert nh == len(dils)

    def body(*refs):
        x_ref, g_ref = refs[:2]
        dh_refs = refs[2:2 + nh]
        dr_ref, dx_ref, dgam_ref, tile = refs[2 + nh:]
        i = pl.program_id(0)
        xv = x_ref[...]
        r = lax.rsqrt(jnp.mean(xv * xv, axis=-1, keepdims=True) + EPS)
        xhat = xv * r
        gam = g_ref[...]
        dh = None
        for dil, ref in zip(dils, dh_refs):
            if dil == 1:
                part = ref[...]
            else:
                _load_unperm(ref, tile, dil)
                part = _get_tile(tile)
            dh = part if dh is None else dh + part

        @pl.when(i == 0)
        def _():
            dgam_ref[...] = jnp.zeros_like(dgam_ref)

        dgam_ref[...] += jnp.sum(dh * xhat, axis=0, keepdims=True)
        dxh = dh * gam
        dx_ref[...] = dr_ref[...] + r * (dxh - xhat * jnp.mean(dxh * xhat, axis=-1, keepdims=True))

    row_d = pl.BlockSpec((tm, D), lambda i: (i, 0))
    dh_specs = [row_d if d == 1 else _perm_spec(d, D) for d in dils]
    dh_args = [a if d == 1 else a.reshape(d, T // d, D) for d, a in zip(dils, dhs)]
    return pl.pallas_call(
        body, name=name, grid=(T // tm,),
        in_specs=[row_d, _const_spec((1, D))] + dh_specs + [row_d],
        out_specs=[row_d, _const_spec((1, D))],
        out_shape=[jax.ShapeDtypeStruct((T, D), F32), jax.ShapeDtypeStruct((1, D), F32)],
        scratch_shapes=[_tile_scratch(D)],
        compiler_params=_cp(("arbitrary",)),
    )(x, g, *dh_args, dres)


CONV_TM = 256
CONV_HALO = 32
CONV_RB = 16


def _glu(ab):
    ab = ab.astype(F32)
    return ab[:, :D] * _sig(ab[:, D:])


def _ln_stats(z1):
    mu = jnp.mean(z1, axis=-1, keepdims=True)
    zc = z1 - mu
    rstd = lax.rsqrt(jnp.mean(zc * zc, axis=-1, keepdims=True) + EPS)
    return zc * rstd, rstd


def _fill_shifts(zs):
    n = zs.shape[1] - 8
    for s in range(1, 8):
        zs[s, pl.ds(0, n), :] = zs[0, pl.ds(s, n), :]


def _shifted(zs, start, rows):
    q, s = divmod(start, 8)
    return zs[s, pl.ds(8 * q, rows), :]


def _conv_fwd(ab, kern, dwb, lng, lnb, name):
    tm, hl, rb = CONV_TM, CONV_HALO, CONV_RB
    off = hl - (CONV_W - 1)

    def body(ab_ref, abh_ref, k_ref, dwb_ref, lng_ref, lnb_ref, z1_ref, z3_ref, zs):
        i = pl.program_id(0)
        zs[0, pl.ds(0, hl), :] = jnp.where(i > 0, _glu(abh_ref[...]), 0.0)
        zs[0, pl.ds(hl, tm), :] = _glu(ab_ref[...])
        _fill_shifts(zs)
        for b in range(tm // rb):
            acc = jnp.zeros((rb, D), F32)
            for j in range(CONV_W):
                acc = acc + _shifted(zs, b * rb + off + j, rb) * k_ref[pl.ds(j, 1), :]
            z1 = acc + dwb_ref[...]
            z1_ref[pl.ds(b * rb, rb), :] = z1
            zn, _ = _ln_stats(z1)
            z2 = zn * lng_ref[...] + lnb_ref[...]
            z3_ref[pl.ds(b * rb, rb), :] = (z2 * _sig(z2)).astype(BF16)

    row = pl.BlockSpec((tm, D), lambda i: (i, 0))
    return pl.pallas_call(
        body, name=name, grid=(T // tm,),
        in_specs=[pl.BlockSpec((tm, 2 * D), lambda i: (i, 0)),
                  pl.BlockSpec((hl, 2 * D), lambda i: (jnp.maximum(i * (tm // hl) - 1, 0), 0)),
                  _const_spec((32, D)), _const_spec((1, D)), _const_spec((1, D)), _const_spec((1, D))],
        out_specs=[row, row],
        out_shape=[jax.ShapeDtypeStruct((T, D), F32), jax.ShapeDtypeStruct((T, D), BF16)],
        scratch_shapes=[pltpu.VMEM((8, hl + tm, D), F32)],
        compiler_params=_cp(("parallel",)),
    )(ab, ab, kern, dwb, lng, lnb)


GUEST_TM = 256


def _conv_bwd(dz3, z1, ab, kern, lng, lnb, name, guest_lhs=(), guest_rhs=None):
    tm, hl, rb = CONV_TM, CONV_HALO, CONV_RB
    off = hl - (CONV_W - 1)
    nsteps = T // tm
    ng = len(guest_lhs)
    gblocks = [a.shape[1] // GUEST_TM for a in guest_lhs]
    assert all(gb <= nsteps and gb * GUEST_TM == a.shape[1] for gb, a in zip(gblocks, guest_lhs))

    def ln_bwd(dz3v, z1v, lngv, lnbv):
        zn, rstd = _ln_stats(z1v)
        z2 = zn * lngv + lnbv
        s = _sig(z2)
        dz2 = dz3v * (s * (1.0 + z2 * (1.0 - s)))
        dzn = dz2 * lngv
        dz1 = rstd * (dzn - jnp.mean(dzn, axis=-1, keepdims=True)
                      - zn * jnp.mean(dzn * zn, axis=-1, keepdims=True))
        return dz1, dz2, zn

    def body(dz3_ref, dz3h_ref, z1_ref, z1h_ref, ab_ref, abh_ref, k_ref, lng_ref, lnb_ref, *rest):
        g_in, rest = rest[:ng + (1 if ng else 0)], rest[ng + (1 if ng else 0):]
        dab_ref, dk_ref, dvec_ref = rest[:3]
        g_out, (zs, dzs) = rest[3:3 + ng], rest[3 + ng:]
        i = pl.program_id(0)
        lngv, lnbv = lng_ref[...], lnb_ref[...]

        for a_ref, o_ref, gb in zip(g_in[:ng], g_out, gblocks):
            @pl.when(i < gb)
            def _(a_ref=a_ref, o_ref=o_ref):
                o_ref[...] = _dot(a_ref[...], g_in[ng][...], TN).astype(BF16)

        @pl.when(i == 0)
        def _():
            dk_ref[...] = jnp.zeros_like(dk_ref)
            dvec_ref[...] = jnp.zeros_like(dvec_ref)

        dz1, dz2, zn = ln_bwd(dz3_ref[...].astype(F32), z1_ref[...], lngv, lnbv)
        dvec_ref[pl.ds(0, 1), :] += jnp.sum(dz1, axis=0, keepdims=True)
        dvec_ref[pl.ds(1, 1), :] += jnp.sum(dz2 * zn, axis=0, keepdims=True)
        dvec_ref[pl.ds(2, 1), :] += jnp.sum(dz2, axis=0, keepdims=True)
        dzs[0, pl.ds(0, tm), :] = dz1
        dz1h, _, _ = ln_bwd(dz3h_ref[...].astype(F32), z1h_ref[...], lngv, lnbv)
        dzs[0, pl.ds(tm, hl), :] = jnp.where(i < nsteps - 1, dz1h, 0.0)
        _fill_shifts(dzs)
        zs[0, pl.ds(0, hl), :] = jnp.where(i > 0, _glu(abh_ref[...]), 0.0)
        zs[0, pl.ds(hl, tm), :] = _glu(ab_ref[...])
        _fill_shifts(zs)

        for j in range(CONV_W):
            tot = jnp.zeros((rb, D), F32)
            for b in range(tm // rb):
                tot = tot + dzs[0, pl.ds(b * rb, rb), :] * _shifted(zs, b * rb + off + j, rb)
            dk_ref[pl.ds(j, 1), :] += jnp.sum(tot, axis=0, keepdims=True)

        for b in range(tm // rb):
            acc = jnp.zeros((rb, D), F32)
            for j in range(CONV_W):
                acc = acc + _shifted(dzs, b * rb + (CONV_W - 1) - j, rb) * k_ref[pl.ds(j, 1), :]
            av = ab_ref[pl.ds(b * rb, rb), pl.ds(0, D)].astype(F32)
            sb = _sig(ab_ref[pl.ds(b * rb, rb), pl.ds(D, D)].astype(F32))
            dab_ref[pl.ds(b * rb, rb), pl.ds(0, D)] = (acc * sb).astype(BF16)
            dab_ref[pl.ds(b * rb, rb), pl.ds(D, D)] = (acc * av * sb * (1.0 - sb)).astype(BF16)

    row = pl.BlockSpec((tm, D), lambda i: (i, 0))
    nxt = pl.BlockSpec((hl, D), lambda i: (jnp.minimum((i + 1) * (tm // hl), T // hl - 1), 0))
    g_specs, g_args, g_ospecs, g_oshapes = [], [], [], []
    for a, gb in zip(guest_lhs, gblocks):
        g_specs.append(pl.BlockSpec((T, GUEST_TM), lambda i, gb=gb: (0, jnp.minimum(i, gb - 1))))
        g_args.append(a)
        g_ospecs.append(pl.BlockSpec((GUEST_TM, guest_rhs.shape[1]), lambda i, gb=gb: (jnp.minimum(i, gb - 1), 0)))
        g_oshapes.append(jax.ShapeDtypeStruct((a.shape[1], guest_rhs.shape[1]), BF16))
    if ng:
        g_specs.append(pl.BlockSpec(guest_rhs.shape, lambda i: (0, 0), pipeline_mode=pl.Buffered(1)))
        g_args.append(guest_rhs)
    return pl.pallas_call(
        body, name=name, grid=(nsteps,),
        in_specs=[row, nxt, row, nxt,
                  pl.BlockSpec((tm, 2 * D), lambda i: (i, 0)),
                  pl.BlockSpec((hl, 2 * D), lambda i: (jnp.maximum(i * (tm // hl) - 1, 0), 0)),
                  _const_spec((32, D)), _const_spec((1, D)), _const_spec((1, D))] + g_specs,
        out_specs=[pl.BlockSpec((tm, 2 * D), lambda i: (i, 0)), _const_spec((32, D)), _const_spec((8, D))]
        + g_ospecs,
        out_shape=[jax.ShapeDtypeStruct((T, 2 * D), BF16), jax.ShapeDtypeStruct((32, D), F32),
                   jax.ShapeDtypeStruct((8, D), F32)] + g_oshapes,
        scratch_shapes=[pltpu.VMEM((8, hl + tm, D), F32), pltpu.VMEM((8, tm + hl, D), F32)],
        compiler_params=_cp(("arbitrary",)),
    )(dz3, dz3, z1, z1, ab, ab, kern, lng, lnb, *g_args)


def _alibi_slopes():
    h = np.arange(1, 3 * NHG + 1, dtype=np.float32)
    return np.power(np.float32(2.0), -8.0 * h / np.float32(3 * NHG)).astype(np.float32)


def _band_bias(gi):
    _, dil = GROUPS[gi]
    slopes = _alibi_slopes()[gi * NHG:(gi + 1) * NHG]
    qi = np.arange(BLK)[:, None]
    ki = np.arange(2 * BLK)[None, :]
    steps = BLK + qi - ki
    band = (steps >= 0) & (steps <= BLK)
    bias = -slopes[:, None, None] * (dil * steps).astype(np.float32)[None]
    return jnp.asarray(np.where(band[None], bias, np.float32(NEG)).astype(np.float32))


QB_FWD = 8
QB_BWD = 32


def _attn_specs(qb):
    prev = lambda n: jnp.maximum(n * qb - 1, 0)
    return [pl.BlockSpec((qb * BLK, HEAD), lambda h, n: (n, h)),
            pl.BlockSpec((BLK, HEAD), lambda h, n: (prev(n), NHG + h)),
            pl.BlockSpec((qb * BLK, HEAD), lambda h, n: (n, NHG + h)),
            pl.BlockSpec((BLK, HEAD), lambda h, n: (prev(n), 2 * NHG + h)),
            pl.BlockSpec((qb * BLK, HEAD), lambda h, n: (n, 2 * NHG + h)),
            pl.BlockSpec((None, BLK, 2 * BLK), lambda h, n: (h, 0, 0))]


def _scores(q, kcat, bias, blk, seg):
    s = _dot(q, kcat, NT) * (HEAD ** -0.5) + bias
    col = lax.broadcasted_iota(jnp.int32, s.shape, 1)
    first = (blk % seg) == 0
    return jnp.where(jnp.logical_and(first, col < BLK), NEG, s)


def _attn_fwd(qkv, gi, name):
    seg = (T // GROUPS[gi][1]) // BLK

    qb = min(QB_FWD, T // BLK)

    def body(q_ref, kp_ref, kc_ref, vp_ref, vc_ref, bias_ref, o_ref, l_ref):
        n = pl.program_id(0)
        for h in range(NHG):
            cols = pl.ds(h * HEAD, HEAD)
            kwin = jnp.concatenate([kp_ref[:, cols], kc_ref[:, cols]], axis=0)
            vwin = jnp.concatenate([vp_ref[:, cols], vc_ref[:, cols]], axis=0)
            bias = bias_ref[h]
            for b in range(qb):
                rows = pl.ds(b * BLK, BLK)
                s = _scores(q_ref[rows, cols], kwin[b * BLK:(b + 2) * BLK], bias, n * qb + b, seg)
                mx = jnp.max(s, axis=-1, keepdims=True)
                p = jnp.exp(s - mx)
                den = jnp.sum(p, axis=-1, keepdims=True)
                o_ref[rows, cols] = (_dot(p.astype(BF16), vwin[b * BLK:(b + 2) * BLK], NN) / den).astype(BF16)
                l_ref[rows, cols] = jnp.broadcast_to(mx + jnp.log(den), (BLK, HEAD))

    prev = lambda n: jnp.maximum(n * qb - 1, 0)
    cur = lambda part: pl.BlockSpec((qb * BLK, AW), lambda n: (n, part))
    halo = lambda part: pl.BlockSpec((BLK, AW), lambda n: (prev(n), part))
    return pl.pallas_call(
        body, name=name, grid=(T // (qb * BLK),),
        in_specs=[cur(0), halo(1), cur(1), halo(2), cur(2), _const_spec((NHG, BLK, 2 * BLK))],
        out_specs=[cur(0), cur(0)],
        out_shape=[jax.ShapeDtypeStruct((T, AW), BF16), jax.ShapeDtypeStruct((T, AW), F32)],
        compiler_params=_cp(("parallel",)),
    )(qkv, qkv, qkv, qkv, qkv, _band_bias(gi))


def _attn_bwd(qkv, dob, lse, delta, gi, name):
    seg = (T // GROUPS[gi][1]) // BLK
    qb = min(QB_BWD, T // BLK)
    nb = T // (qb * BLK)
    scale = HEAD ** -0.5

    def body(q_ref, kp_ref, kc_ref, vp_ref, vc_ref, bias_ref, do_ref, l_ref, dl_ref, out_ref, dk_acc, dv_acc):
        n = pl.program_id(1)
        kwin = jnp.concatenate([kp_ref[...], kc_ref[...]], axis=0)
        vwin = jnp.concatenate([vp_ref[...], vc_ref[...]], axis=0)
        bias = bias_ref[...]
        dks, dvs = [], []
        for b in range(qb):
            rows = pl.ds(b * BLK, BLK)
            q = q_ref[rows, :]
            kcat = kwin[b * BLK:(b + 2) * BLK]
            s = _scores(q, kcat, bias, n * qb + b, seg)
            p = jnp.exp(s - l_ref[rows, pl.ds(0, 1)])
            dov = do_ref[rows, :]
            dvs.append(_dot(p.astype(BF16), dov, TN))
            dp = _dot(dov, vwin[b * BLK:(b + 2) * BLK], NT)
            dsb = (p * (dp - dl_ref[rows, pl.ds(0, 1)]) * scale).astype(BF16)
            row = pl.ds(pl.multiple_of((n * qb + b) * BLK, BLK), BLK)
            out_ref[0, row, :] = _dot(dsb, kcat, NN).astype(BF16)
            dks.append(_dot(dsb, q, TN))
        for b in range(qb):
            row = pl.ds(pl.multiple_of((n * qb + b) * BLK, BLK), BLK)
            if b + 1 < qb:
                dk_acc[row, :] = dks[b][BLK:] + dks[b + 1][:BLK]
                dv_acc[row, :] = dvs[b][BLK:] + dvs[b + 1][:BLK]
            else:
                dk_acc[row, :] = dks[b][BLK:]
                dv_acc[row, :] = dvs[b][BLK:]

        @pl.when(n > 0)
        def _():
            prow = pl.ds(pl.multiple_of((n * qb - 1) * BLK, BLK), BLK)
            dk_acc[prow, :] += dks[0][:BLK]
            dv_acc[prow, :] += dvs[0][:BLK]

        @pl.when(n == nb - 1)
        def _():
            out_ref[1] = dk_acc[...].astype(BF16)
            out_ref[2] = dv_acc[...].astype(BF16)

    oblk = pl.BlockSpec((qb * BLK, HEAD), lambda h, n: (n, h))
    return pl.pallas_call(
        body, name=name, grid=(NHG, nb),
        in_specs=_attn_specs(qb) + [oblk, oblk, oblk],
        out_specs=pl.BlockSpec((3, T, HEAD), lambda h, n: (0, 0, h)),
        out_shape=jax.ShapeDtypeStruct((3, T, AW), BF16),
        scratch_shapes=[pltpu.VMEM((T, HEAD), F32), pltpu.VMEM((T, HEAD), F32)],
        compiler_params=_cp(("parallel", "arbitrary")),
    )(qkv, qkv, qkv, qkv, qkv, _band_bias(gi), dob, lse, delta)


def _merge(outs, lses, name):
    tm = PERM_TM
    dils = [d for _, d in GROUPS]
    ng = len(dils)

    def body(*refs):
        in_refs = refs[:2 * ng]
        ab_ref = refs[2 * ng]
        lse_refs = refs[2 * ng + 1:3 * ng + 1]
        tile = refs[-1]

        def token_order(ref, dil):
            if dil == 1:
                return ref[...].astype(F32)
            _load_unperm(ref, tile, dil)
            return _get_tile(tile)

        os = [token_order(in_refs[2 * i], d) for i, d in enumerate(dils)]
        ls = [token_order(in_refs[2 * i + 1], d) for i, d in enumerate(dils)]
        mx = jnp.maximum(jnp.maximum(ls[0], ls[1]), ls[2])
        es = [jnp.exp(v - mx) for v in ls]
        tot = es[0] + es[1] + es[2]
        att = (es[0] / tot) * os[0] + (es[1] / tot) * os[1] + (es[2] / tot) * os[2]
        ab_ref[...] = att.astype(BF16)
        lse = mx + jnp.log(tot)
        _put_tile(tile, lse)
        for dil, ref in zip(dils, lse_refs):
            if dil == 1:
                ref[...] = lse
            else:
                _store_perm(ref, tile, dil)

    row = pl.BlockSpec((tm, AW), lambda i: (i, 0))
    specs = [row if d == 1 else _perm_spec(d, AW) for d in dils]
    args = []
    for d, o, l in zip(dils, outs, lses):
        args += [o, l] if d == 1 else [o.reshape(d, T // d, AW), l.reshape(d, T // d, AW)]
    out = pl.pallas_call(
        body, name=name, grid=(T // tm,),
        in_specs=[sp for sp in specs for _ in range(2)], out_specs=[row] + specs,
        out_shape=[jax.ShapeDtypeStruct((T, AW), BF16)]
        + [jax.ShapeDtypeStruct((T, AW), F32) if d == 1 else _perm_shape(d, AW, F32) for d in dils],
        scratch_shapes=[_tile_scratch(AW)],
        compiler_params=_cp(("parallel",)),
    )(*args)
    return out[0], [o.reshape(T, AW) for o in out[1:]]


def _mix_out(z3b, attnb, gates, wc, wa_t, wo, x1, name):
    tm = 512

    def body(z_ref, a_ref, g_ref, wc_ref, wa_ref, wo_ref, x_ref, xo_ref, yc_ref, ya_ref, mx_ref):
        yc = _dot(z_ref[...], wc_ref[...], NN)
        ya = _dot(a_ref[...], wa_ref[...], NT)
        yc_ref[...] = yc.astype(BF16)
        ya_ref[...] = ya.astype(BF16)
        gv = g_ref[...].astype(F32)
        mixed = (_sig(gv[:, :D]) * yc + _sig(gv[:, D:]) * ya).astype(BF16)
        mx_ref[...] = mixed
        xo_ref[...] = x_ref[...] + _dot(mixed, wo_ref[...], NN)

    row = pl.BlockSpec((tm, D), lambda i: (i, 0))
    return pl.pallas_call(
        body, name=name, grid=(T // tm,),
        in_specs=[row, pl.BlockSpec((tm, AW), lambda i: (i, 0)), pl.BlockSpec((tm, 2 * D), lambda i: (i, 0)),
                  _const_spec((D, D)), _const_spec((D, AW)), _const_spec((D, D)), row],
        out_specs=[row, row, row, row],
        out_shape=[jax.ShapeDtypeStruct((T, D), F32), jax.ShapeDtypeStruct((T, D), BF16),
                   jax.ShapeDtypeStruct((T, D), BF16), jax.ShapeDtypeStruct((T, D), BF16)],
        compiler_params=_cp(("parallel",)),
    )(z3b, attnb, gates, wc, wa_t, wo, x1)


def _mix_out_bwd(dx2, gates, yc, ya, attn, wc, wa_t, wo, name):
    tm = PERM_TM
    dils = [d for _, d in GROUPS]
    ng = len(dils)

    def body(dx_ref, g_ref, yc_ref, ya_ref, at_ref, wc_ref, wa_ref, wo_ref,
             dg_ref, dyc_ref, dya_ref, dxb_ref, dz3_ref, *rest):
        dat_refs, dl_refs, tile = rest[:ng], rest[ng:2 * ng], rest[-1]
        dxb = dx_ref[...].astype(BF16)
        dxb_ref[...] = dxb
        dmix = _dot(dxb, wo_ref[...], NT)
        gv = g_ref[...].astype(F32)
        sc = _sig(gv[:, :D])
        sa = _sig(gv[:, D:])
        ycv, yav = yc_ref[...].astype(F32), ya_ref[...].astype(F32)
        dg_ref[:, pl.ds(0, D)] = (dmix * ycv * sc * (1.0 - sc)).astype(BF16)
        dg_ref[:, pl.ds(D, D)] = (dmix * yav * sa * (1.0 - sa)).astype(BF16)
        dyc = (dmix * sc).astype(BF16)
        dya = (dmix * sa).astype(BF16)
        dyc_ref[...] = dyc
        dya_ref[...] = dya
        dz3_ref[...] = _dot(dyc, wc_ref[...], NT).astype(BF16)
        dat = _dot(dya, wa_ref[...], NN)
        prod = dat * at_ref[...].astype(F32)
        delta = jnp.concatenate(
            [jnp.broadcast_to(jnp.sum(prod[:, h * HEAD:(h + 1) * HEAD], axis=-1, keepdims=True), (tm, HEAD))
             for h in range(NHG)], axis=1)
        for value, out_refs in ((dat, dat_refs), (delta, dl_refs)):
            _put_tile(tile, value)
            for dil, ref in zip(dils, out_refs):
                if dil == 1:
                    ref[...] = value.astype(ref.dtype)
                else:
                    _store_perm(ref, tile, dil)

    row = pl.BlockSpec((tm, D), lambda i: (i, 0))
    row2 = pl.BlockSpec((tm, 2 * D), lambda i: (i, 0))
    rowa = pl.BlockSpec((tm, AW), lambda i: (i, 0))
    aspecs = [rowa if d == 1 else _perm_spec(d, AW) for d in dils]

    def ashapes(dtype):
        return [jax.ShapeDtypeStruct((T, AW), dtype) if d == 1 else _perm_shape(d, AW, dtype) for d in dils]

    out = pl.pallas_call(
        body, name=name, grid=(T // tm,),
        in_specs=[row, row2, row, row, rowa, _const_spec((D, D)), _const_spec((D, AW)), _const_spec((D, D))],
        out_specs=[row2, row, row, row, row] + aspecs + aspecs,
        out_shape=[jax.ShapeDtypeStruct((T, 2 * D), BF16), jax.ShapeDtypeStruct((T, D), BF16),
                   jax.ShapeDtypeStruct((T, D), BF16), jax.ShapeDtypeStruct((T, D), BF16),
                   jax.ShapeDtypeStruct((T, D), BF16)] + ashapes(BF16) + ashapes(F32),
        scratch_shapes=[_tile_scratch(AW)],
        compiler_params=_cp(("parallel",)),
    )(dx2, gates, yc, ya, attn, wc, wa_t, wo)
    dats = [o.reshape(T, AW) for o in out[5:5 + ng]]
    deltas = [o.reshape(T, AW) for o in out[5 + ng:5 + 2 * ng]]
    return out[0], out[1], out[2], out[3], out[4], dats, deltas


def _peer(k):
    x, y, c = lax.axis_index("x"), lax.axis_index("y"), lax.axis_index("c")
    px = 1 - x if k & 4 else x
    py = 1 - y if k & 2 else y
    pc = 1 - c if k & 1 else c
    return (px, py, pc), 4 * px + 2 * py + pc


HBM_SPEC = pl.BlockSpec(memory_space=pltpu.HBM)
SEM_SPEC = pl.BlockSpec(memory_space=pltpu.SEMAPHORE)
EFFECT = pltpu.SideEffectType.DATAFLOW_SIDE_EFFECTING


def _my_place():
    return 4 * lax.axis_index("x") + 2 * lax.axis_index("y") + lax.axis_index("c")


def _tie(a, order_after, name):
    na = len(order_after)

    def body(*refs):
        del refs

    return pl.pallas_call(
        body, name=name, in_specs=[pl.BlockSpec(memory_space=pl.ANY)] * (1 + na),
        out_specs=pl.BlockSpec(memory_space=pl.ANY), out_shape=jax.ShapeDtypeStruct(a.shape, a.dtype),
        input_output_aliases={0: 0},
    )(a, *order_after)


def _prep_gather(ws, order_after, name):
    me = jnp.reshape(_my_place(), (1,)).astype(jnp.int32)
    n = len(ws)
    na = len(order_after)
    shapes = [((32, wv.shape[1]), F32) if wv.shape[0] == CONV_W else (wv.shape, BF16) for wv in ws]

    def body(me_ref, *refs):
        del me_ref
        ins, outs = refs[:n], refs[n + na:]
        for wv, i_ref, o_ref in zip(ws, ins, outs):
            if wv.shape[0] == CONV_W:
                o_ref[pl.ds(0, CONV_W), :] = i_ref[...]
                o_ref[pl.ds(CONV_W, 1), :] = jnp.zeros((1, wv.shape[1]), F32)
            else:
                o_ref[...] = i_ref[...].astype(BF16)

    grid_spec = pltpu.PrefetchScalarGridSpec(
        num_scalar_prefetch=1, grid=(1,),
        in_specs=[pl.BlockSpec(wv.shape, lambda i, m: (0, 0)) for wv in ws]
        + [pl.BlockSpec(memory_space=pl.ANY)] * na,
        out_specs=[pl.BlockSpec(shp, lambda i, m: (m[0], 0)) for shp, _ in shapes])
    return pl.pallas_call(
        body, name=name, grid_spec=grid_spec,
        out_shape=[jax.ShapeDtypeStruct((NDEV * shp[0], shp[1]), dt) for shp, dt in shapes],
        compiler_params=_cp(("arbitrary",)),
    )(me, *ws, *order_after)


GATHER_A = ((1, 0), (2, 0), (4, 0), (6, 0))
GATHER_B = ((1, 2), (1, 4), (1, 6))
GATHER_DIRECT = tuple((k, 0) for k in range(1, NDEV))


def _gather_start(lands, plan, order_after, name):
    n = len(lands)
    na = len(order_after)
    npl = len(plan)

    def body(*refs):
        land_refs = refs[:n]
        send, recv = refs[n + na], refs[n + na + 1]
        token = refs[-1]
        for w in range(n):
            rows = lands[w].shape[0] // NDEV
            for p, (k, j) in enumerate(plan):
                peer, _ = _peer(k)
                _, blk = _peer(j)
                part = land_refs[w].at[pl.ds(blk * rows, rows)]
                i = w * npl + p
                pltpu.make_async_remote_copy(src_ref=part, dst_ref=part, send_sem=send.at[i], recv_sem=recv.at[i],
                                             device_id=peer, device_id_type=MESH_ID).start()
        token[...] = jnp.zeros_like(token)

    nsem = n * npl
    bufs = [pltpu.with_memory_space_constraint(a, pltpu.HBM) for a in lands]
    out = pl.pallas_call(
        body, name=name,
        in_specs=[HBM_SPEC] * n + [pl.BlockSpec(memory_space=pl.ANY)] * na,
        out_specs=[SEM_SPEC, SEM_SPEC] + [HBM_SPEC] * n + [pl.BlockSpec(memory_space=pltpu.VMEM)],
        out_shape=[pltpu.SemaphoreType.DMA((nsem,)), pltpu.SemaphoreType.DMA((nsem,))]
        + [pltpu.HBM(a.shape, a.dtype) for a in bufs] + [jax.ShapeDtypeStruct((8, 128), F32)],
        input_output_aliases={i: 2 + i for i in range(n)},
        compiler_params=pltpu.CompilerParams(has_side_effects=EFFECT),
    )(*bufs, *order_after)
    return out[0], out[1], out[2:2 + n], out[-1]


def _gather_wait(started, plan, order_after, name):
    send, recv, lands, _ = started
    n = len(lands)
    na = len(order_after)
    npl = len(plan)

    def body(*refs):
        land_refs = refs[:n]
        send_ref, recv_ref = refs[n], refs[n + 1]
        for w in range(n):
            rows = lands[w].shape[0] // NDEV
            for p, (k, j) in enumerate(plan):
                peer, _ = _peer(k)
                _, blk = _peer(j)
                part = land_refs[w].at[pl.ds(blk * rows, rows)]
                i = w * npl + p
                cp = pltpu.make_async_remote_copy(src_ref=part, dst_ref=part, send_sem=send_ref.at[i],
                                                  recv_sem=recv_ref.at[i], device_id=peer, device_id_type=MESH_ID)
                cp.wait_send()
                cp.wait_recv()

    out = pl.pallas_call(
        body, name=name,
        in_specs=[HBM_SPEC] * n + [SEM_SPEC, SEM_SPEC] + [pl.BlockSpec(memory_space=pl.ANY)] * na,
        out_specs=[HBM_SPEC] * n,
        out_shape=[pltpu.HBM(a.shape, a.dtype) for a in lands],
        input_output_aliases={i: i for i in range(n)},
        compiler_params=pltpu.CompilerParams(has_side_effects=EFFECT),
    )(*lands, send, recv, *order_after)
    return list(out)


def _copy_ends(kind, src, land, me, plin, k):
    if kind == "scatter":
        rows = src.shape[0] // NDEV
        return src.at[pl.ds(plin * rows, rows)], land.at[k - 1]
    return src, land.at[me]


def _landing(kind, src):
    me = _my_place()
    if kind == "scatter":
        return lax.empty((NDEV - 1, src.shape[0] // NDEV) + src.shape[1:], src.dtype)
    land = lax.empty((NDEV,) + src.shape, src.dtype)
    return lax.dynamic_update_slice(land, src[None], (me,) + (0,) * src.ndim)


def _send_start(kinds, srcs, order_after, name):
    n = len(srcs)
    lands = [_landing(kd, s) for kd, s in zip(kinds, srcs)]
    na = len(order_after)

    def body(*refs):
        src_refs, land_refs = refs[:n], refs[n:2 * n]
        send, recv = refs[2 * n + na], refs[2 * n + na + 1]
        token = refs[-1]
        _, me = _peer(0)
        for w in range(n):
            for k in range(1, NDEV):
                peer, plin = _peer(k)
                s, d = _copy_ends(kinds[w], src_refs[w], land_refs[w], me, plin, k)
                i = w * (NDEV - 1) + k - 1
                pltpu.make_async_remote_copy(src_ref=s, dst_ref=d, send_sem=send.at[i], recv_sem=recv.at[i],
                                             device_id=peer, device_id_type=MESH_ID).start()
        token[...] = jnp.zeros_like(token)

    nsem = n * (NDEV - 1)
    bufs = [pltpu.with_memory_space_constraint(a, pltpu.HBM) for a in list(srcs) + lands]
    out = pl.pallas_call(
        body, name=name,
        in_specs=[HBM_SPEC] * (2 * n) + [pl.BlockSpec(memory_space=pl.ANY)] * na,
        out_specs=[SEM_SPEC, SEM_SPEC] + [HBM_SPEC] * (2 * n) + [pl.BlockSpec(memory_space=pltpu.VMEM)],
        out_shape=[pltpu.SemaphoreType.DMA((nsem,)), pltpu.SemaphoreType.DMA((nsem,))]
        + [pltpu.HBM(a.shape, a.dtype) for a in bufs] + [jax.ShapeDtypeStruct((8, 128), F32)],
        input_output_aliases={i: 2 + i for i in range(2 * n)},
        compiler_params=pltpu.CompilerParams(has_side_effects=EFFECT),
    )(*bufs, *order_after)
    return out[0], out[1], out[2:2 + n], out[2 + n:2 + 2 * n], out[-1]


def _send_wait(kinds, started, order_after, name):
    send, recv, srcs, lands, _ = started
    n = len(srcs)
    na = len(order_after)

    def body(*refs):
        src_refs, land_refs = refs[:n], refs[n:2 * n]
        send_ref, recv_ref = refs[2 * n], refs[2 * n + 1]
        _, me = _peer(0)
        for w in range(n):
            for k in range(1, NDEV):
                peer, plin = _peer(k)
                s, d = _copy_ends(kinds[w], src_refs[w], land_refs[w], me, plin, k)
                i = w * (NDEV - 1) + k - 1
                cp = pltpu.make_async_remote_copy(src_ref=s, dst_ref=d, send_sem=send_ref.at[i],
                                                  recv_sem=recv_ref.at[i], device_id=peer, device_id_type=MESH_ID)
                cp.wait_send()
                cp.wait_recv()

    bufs = list(srcs) + list(lands)
    out = pl.pallas_call(
        body, name=name,
        in_specs=[HBM_SPEC] * (2 * n) + [SEM_SPEC, SEM_SPEC] + [pl.BlockSpec(memory_space=pl.ANY)] * na,
        out_specs=[HBM_SPEC] * (2 * n),
        out_shape=[pltpu.HBM(a.shape, a.dtype) for a in bufs],
        input_output_aliases={i: i for i in range(2 * n)},
        compiler_params=pltpu.CompilerParams(has_side_effects=EFFECT),
    )(*bufs, send, recv, *order_after)
    return out[:n], out[n:]


def _gsum(own, land, name):
    rows, cols = own.shape
    tr = rows // 2 if rows * cols > 512 * 1024 and rows % 32 == 0 else rows

    def body(own_ref, l_ref, o_ref):
        tot = own_ref[...].astype(F32)
        for s in range(NDEV - 1):
            tot = tot + l_ref[s].astype(F32)
        o_ref[...] = tot

    return pl.pallas_call(
        body, name=name, grid=(rows // tr,),
        in_specs=[pl.BlockSpec((tr, cols), lambda i: (i, 0)),
                  pl.BlockSpec((NDEV - 1, tr, cols), lambda i: (0, i, 0))],
        out_specs=pl.BlockSpec((tr, cols), lambda i: (i, 0)),
        out_shape=jax.ShapeDtypeStruct((rows, cols), F32),
        compiler_params=_cp(("parallel",)),
    )(own, land)


def _adamw_math(w, g, m, v):
    m2 = B1 * m + (1.0 - B1) * g
    v2 = B2 * v + (1.0 - B2) * (g * g)
    m_hat = m2 / (1.0 - B1 ** STEP)
    v_hat = v2 / (1.0 - B2 ** STEP)
    delta = -LR * (m_hat / (jnp.sqrt(v_hat) + AEPS) + WD * w)
    return delta, m2, v2


def _adamw(w, g, m, v, name):
    rows, cols = w.shape
    tr = 256 if rows % 256 == 0 and rows > 256 else rows

    def body(w_ref, g_ref, m_ref, v_ref, d_ref, mo_ref, vo_ref):
        d, m2, v2 = _adamw_math(w_ref[...], g_ref[...], m_ref[...], v_ref[...])
        d_ref[...] = d
        mo_ref[...] = m2
        vo_ref[...] = v2

    blk = pl.BlockSpec((tr, cols), lambda i: (i, 0))
    return pl.pallas_call(
        body, name=name, grid=(rows // tr,), in_specs=[blk] * 4, out_specs=[blk] * 3,
        out_shape=[jax.ShapeDtypeStruct((rows, cols), F32)] * 3,
        compiler_params=_cp(("parallel",)),
    )(w, g, m, v)


UPD_TC = 256


def _update(src, land, w, m, v, name):
    rows, cols = land.shape[1:]
    tc = min(UPD_TC if rows > 512 else 2 * UPD_TC, cols)
    me = jnp.reshape(_my_place(), (1,)).astype(jnp.int32)

    def body(me_ref, own_ref, l_ref, w_ref, m_ref, v_ref, g_ref, d_ref, mo_ref, vo_ref):
        del me_ref
        g = own_ref[...].astype(F32)
        for s in range(NDEV - 1):
            g = g + l_ref[s].astype(F32)
        g_ref[...] = g
        d, m2, v2 = _adamw_math(w_ref[...], g, m_ref[...], v_ref[...])
        d_ref[...] = d
        mo_ref[...] = m2
        vo_ref[...] = v2

    wblk = pl.BlockSpec((rows, tc), lambda j, p: (0, j))
    grid_spec = pltpu.PrefetchScalarGridSpec(
        num_scalar_prefetch=1, grid=(cols // tc,),
        in_specs=[pl.BlockSpec((rows, tc), lambda j, p: (p[0], j)),
                  pl.BlockSpec((NDEV - 1, rows, tc), lambda j, p: (0, 0, j)), wblk, wblk, wblk],
        out_specs=[wblk] * 4)
    return pl.pallas_call(
        body, name=name, grid_spec=grid_spec, out_shape=[jax.ShapeDtypeStruct((rows, cols), F32)] * 4,
        compiler_params=_cp(("parallel",)),
    )(me, src, land, w, m, v)


def _small_update(vland, w8, m8, v8, name):
    def body(l_ref, w_ref, m_ref, v_ref, g_ref, d_ref, mo_ref, vo_ref):
        g = l_ref[0]
        for s in range(1, NDEV):
            g = g + l_ref[s]
        g_ref[...] = g
        d, m2, v2 = _adamw_math(w_ref[...], g, m_ref[...], v_ref[...])
        d_ref[...] = d
        mo_ref[...] = m2
        vo_ref[...] = v2

    return pl.pallas_call(
        body, name=name, out_shape=[jax.ShapeDtypeStruct((8, D), F32)] * 4,
        compiler_params=_cp(None),
    )(vland, w8, m8, v8)


def kernel(x, ffn1_norm, ffn1_w_gate, ffn1_w_up, ffn1_w_down, mix_norm, w_in, conv_dw_kernel, conv_dw_bias, conv_ln_gain, conv_ln_bias, conv_w_out, attn_w_out, w_o, ffn2_norm, ffn2_w_gate, ffn2_w_up, ffn2_w_down, final_norm, loss_target, m_ffn1_norm, m_ffn1_w_gate, m_ffn1_w_up, m_ffn1_w_down, m_mix_norm, m_w_in, m_conv_dw_kernel, m_conv_dw_bias, m_conv_ln_gain, m_conv_ln_bias, m_conv_w_out, m_attn_w_out, m_w_o, m_ffn2_norm, m_ffn2_w_gate, m_ffn2_w_up, m_ffn2_w_down, m_final_norm, v_ffn1_norm, v_ffn1_w_gate, v_ffn1_w_up, v_ffn1_w_down, v_mix_norm, v_w_in, v_conv_dw_kernel, v_conv_dw_bias, v_conv_ln_gain, v_conv_ln_bias, v_conv_w_out, v_attn_w_out, v_w_o, v_ffn2_norm, v_ffn2_w_gate, v_ffn2_w_up, v_ffn2_w_down, v_final_norm):
    names = ["ffn1_norm", "ffn1_w_gate", "ffn1_w_up", "ffn1_w_down", "mix_norm", "w_in", "conv_dw_kernel",
             "conv_dw_bias", "conv_ln_gain", "conv_ln_bias", "conv_w_out", "attn_w_out", "w_o", "ffn2_norm",
             "ffn2_w_gate", "ffn2_w_up", "ffn2_w_down", "final_norm"]
    w = dict(ffn1_norm=ffn1_norm, ffn1_w_gate=ffn1_w_gate, ffn1_w_up=ffn1_w_up, ffn1_w_down=ffn1_w_down, mix_norm=mix_norm, w_in=w_in, conv_dw_kernel=conv_dw_kernel, conv_dw_bias=conv_dw_bias, conv_ln_gain=conv_ln_gain, conv_ln_bias=conv_ln_bias, conv_w_out=conv_w_out, attn_w_out=attn_w_out, w_o=w_o, ffn2_norm=ffn2_norm, ffn2_w_gate=ffn2_w_gate, ffn2_w_up=ffn2_w_up, ffn2_w_down=ffn2_w_down, final_norm=final_norm)
    mo = dict(ffn1_norm=m_ffn1_norm, ffn1_w_gate=m_ffn1_w_gate, ffn1_w_up=m_ffn1_w_up, ffn1_w_down=m_ffn1_w_down, mix_norm=m_mix_norm, w_in=m_w_in, conv_dw_kernel=m_conv_dw_kernel, conv_dw_bias=m_conv_dw_bias, conv_ln_gain=m_conv_ln_gain, conv_ln_bias=m_conv_ln_bias, conv_w_out=m_conv_w_out, attn_w_out=m_attn_w_out, w_o=m_w_o, ffn2_norm=m_ffn2_norm, ffn2_w_gate=m_ffn2_w_gate, ffn2_w_up=m_ffn2_w_up, ffn2_w_down=m_ffn2_w_down, final_norm=m_final_norm)
    vo = dict(ffn1_norm=v_ffn1_norm, ffn1_w_gate=v_ffn1_w_gate, ffn1_w_up=v_ffn1_w_up, ffn1_w_down=v_ffn1_w_down, mix_norm=v_mix_norm, w_in=v_w_in, conv_dw_kernel=v_conv_dw_kernel, conv_dw_bias=v_conv_dw_bias, conv_ln_gain=v_conv_ln_gain, conv_ln_bias=v_conv_ln_bias, conv_w_out=v_conv_w_out, attn_w_out=v_attn_w_out, w_o=v_w_o, ffn2_norm=v_ffn2_norm, ffn2_w_gate=v_ffn2_w_gate, ffn2_w_up=v_ffn2_w_up, ffn2_w_down=v_ffn2_w_down, final_norm=v_final_norm)
    col_sharded = ("ffn1_w_gate", "ffn1_w_up", "w_in", "attn_w_out", "ffn2_w_gate", "ffn2_w_up")
    row_sharded = ("ffn1_w_down", "conv_w_out", "w_o", "ffn2_w_down")
    small = ("ffn1_norm", "mix_norm", "ffn2_norm", "final_norm", "conv_dw_bias", "conv_ln_gain", "conv_ln_bias")

    def landing_view(a, n):
        return jnp.transpose(a[0]) if n in col_sharded else a[0]

    def own_view(a, n):
        return jnp.transpose(a)[None] if n in col_sharded else a[None]

    ag_groups = (("ffn1_w_gate", "ffn1_w_up", "ffn1_w_down"),
                 ("w_in", "attn_w_out", "conv_w_out", "w_o", "conv_dw_kernel"),
                 ("ffn2_w_gate", "ffn2_w_up", "ffn2_w_down"))
    ag, order = [], []
    for gi, grp in enumerate(ag_groups):
        lands = _prep_gather([landing_view(w[n], n) for n in grp], order, f"gather_prep{gi}")
        st = _gather_start(lands, GATHER_DIRECT if gi == 2 else GATHER_A, [], f"gather_a_start{gi}")
        ag.append(st)
        order = [st[3]]

    def chips_in(gi, after):
        lands = _gather_wait(ag[gi], GATHER_A, after, f"gather_a_wait{gi}")
        return _gather_start(lands, GATHER_B, [], f"gather_b_start{gi}")

    def all_in(gi, st, after):
        return _gather_wait(st, GATHER_B, after, f"gather_b_wait{gi}")

    x0 = x[0]
    tgt = loss_target[0]
    gf = final_norm.reshape(1, D)

    wg1, wu1, wd1 = all_in(0, chips_in(0, [ag[2][3]]), [])
    x1, gg1, uu1, h2p = _ffn_fwd(x0, ffn1_norm, wg1, wu1, wd1, "ffn1_fwd", next_gain=mix_norm)
    h2 = h2p[0]
    win_t, wa_t, wc, wo, kern_blocks = all_in(1, chips_in(1, [x1]), [])
    kern = kern_blocks.reshape(NDEV, 32, D // NDEV).transpose(1, 0, 2).reshape(32, D)
    ptm = min(T, 2048)
    ab = _mm(h2, win_t, mode="nt", m=T, n=2 * D, k=D, tm=ptm, tn=512, tk=D, out_dtype=BF16, name="proj_conv")
    gates = _mm(h2, win_t, mode="nt", m=T, n=2 * D, k=D, tm=ptm, tn=512, tk=D, out_dtype=BF16,
                b_map=lambda i, j, kk: (13 + j, 0), name="proj_gates")
    qkv = []
    for gi in range(len(GROUPS)):
        qkv.append(_mm(h2p[gi], win_t, mode="nt", m=T, n=3 * AW, k=D, tm=ptm, tn=AW, tk=D, out_dtype=BF16,
                       b_map=lambda i, j, kk, gi=gi: (4 + gi + 3 * j, 0), name=f"proj_qkv{gi}"))
    z1, z3b = _conv_fwd(ab, kern, conv_dw_bias, conv_ln_gain, conv_ln_bias, "conv_fwd")
    outs, lses = [], []
    for gi, (_, dil) in enumerate(GROUPS):
        o, l = _attn_fwd(qkv[gi], gi, f"attn_fwd{gi}")
        outs.append(o)
        lses.append(l)
    attnb, lse = _merge(outs, lses, "attn_merge")
    x2, yc, ya, mixedb = _mix_out(z3b, attnb, gates, wc, wa_t, wo, x1, "mix_out_fwd")
    wg2, wu2, wd2 = _gather_wait(ag[2], GATHER_DIRECT, [x2], "gather_a_wait2")
    gg2, uu2, dx3, dgf, loss_part = _ffn_fwd(x2, ffn2_norm, wg2, wu2, wd2, "ffn2_fwd", loss_of=(gf, tgt))

    dx2, dg3, dgb, dub, actb, hb, dob = _ffn_bwd(x2, ffn2_norm, gg2, uu2, dx3, wg2, wu2, wd2, "ffn2_bwd")
    grads = {}
    grads["ffn2_w_down"] = _wgrad(actb, dob, FF, D, "ffn2_dwd")
    rs_groups = [("ffn2_w_gate", "ffn2_w_up", "ffn2_w_down"),
                 ("attn_w_out", "conv_w_out", "w_o", "conv_dw_kernel"),
                 ("w_in",),
                 ("ffn1_w_gate",), ("ffn1_w_up",), ("ffn1_w_down",), ()]
    last = len(rs_groups) - 1
    rs = []

    dgates, dycb, dyab, dx2b, dz3, dattnb, delta = _mix_out_bwd(dx2, gates, yc, ya, attnb, wc, wa_t, wo, "mix_out_bwd")
    grads["w_o"] = _wgrad(mixedb, dx2b, D, D, "dw_o")
    grads["conv_w_out"] = _wgrad(z3b, dycb, D, D, "dw_conv_out")
    grads["attn_w_out"] = _wgrad(dyab, attnb, D, AW, "dw_attn_out")
    dab, dkern, dvec, grads["ffn2_w_gate"], grads["ffn2_w_up"] = _conv_bwd(
        dz3, z1, ab, kern, conv_ln_gain, conv_ln_bias, "conv_bwd", guest_lhs=(dgb, dub), guest_rhs=hb)
    grads["conv_dw_kernel"] = dkern.reshape(32, NDEV, D // NDEV).transpose(1, 0, 2).reshape(NDEV * 32, D // NDEV)
    rs.append(_send_start(["scatter"] * 3, [grads[n] for n in rs_groups[0]], [], "scatter_start0"))
    rs.append(_send_start(["scatter"] * 4, [grads[n] for n in rs_groups[1]], [rs[0][4]], "scatter_start1"))
    dattnb = [_tie(a, [rs[1][4]], f"tie_after_scatter1_{i}") for i, a in enumerate(dattnb)]

    dqkv = []
    for gi, (_, dil) in enumerate(GROUPS):
        dq3 = _attn_bwd(qkv[gi], dattnb[gi], lse[gi], delta[gi], gi, f"attn_bwd{gi}")
        dqkv.append(dq3.reshape(3 * T, AW))

    wtk = min(T, 2048)
    dwin = _mm(dab, h2, mode="tn", m=2 * D, n=D, k=T, tm=2 * D, tn=D, tk=wtk, out_dtype=BF16, out_rows=IN_W,
               name="dw_in_conv")
    dwin = _mm(dgates, h2, mode="tn", m=2 * D, n=D, k=T, tm=512, tn=D, tk=wtk, out_dtype=BF16, out_rows=IN_W,
               o_map=lambda i, j, kk: (13 + i, 0), passthru=dwin, name="dw_in_gates")
    for gi in range(3):
        dwin = _mm(dqkv[gi], h2p[gi], mode="tn", m=3 * AW, n=D, k=T, tm=AW, tn=D, tk=wtk, out_dtype=BF16,
                   out_rows=IN_W, a_map=lambda i, j, kk: (i * (T // wtk) + kk, 0),
                   o_map=lambda i, j, kk, gi=gi: (4 + gi + 3 * i, 0), passthru=dwin, name=f"dw_in_qkv{gi}")
    grads["w_in"] = dwin
    rs.append(_send_start(["scatter"], [dwin], [rs[1][4]], "scatter_start2"))
    dab = _tie(dab, [rs[2][4]], "tie_after_scatter2")

    nrow = T // 1024
    dh = _mm(dab, win_t, mode="nn", m=T, n=D, k=2 * D, tm=1024, tn=D, tk=2 * D, out_dtype=F32, name="dproj_conv")
    dh = _mm(dgates, win_t[IN_W - 2 * D:], mode="nn", m=T, n=D, k=2 * D, tm=1024, tn=D, tk=2 * D, out_dtype=F32,
             init=dh, name="dproj_gates")
    dhs = []
    for gi, (_, dil) in enumerate(GROUPS):
        part = _mm(dqkv[gi], win_t, mode="nn", m=T, n=D, k=3 * AW, tm=1024, tn=D, tk=AW,
                   out_dtype=F32 if gi == 0 else BF16,
                   a_map=lambda i, j, kk: (kk * nrow + i, 0), b_map=lambda i, j, kk, gi=gi: (4 + gi + 3 * kk, 0),
                   init=dh if gi == 0 else None, name=f"dproj_qkv{gi}")
        dhs.append(part)
    dx1, dg2 = _rms_bwd(x1, mix_norm, dhs, dx2, "mix_norm_bwd")

    dgb, dub, actb, hb, dob = _ffn_bwd_pre(x0, ffn1_norm, gg1, uu1, dx1, wd1, "ffn1_bwd_pre")
    grads["ffn1_w_gate"] = _wgrad(dgb, hb, FF, D, "ffn1_dwg")
    rs.append(_send_start(["scatter"], [grads["ffn1_w_gate"]], [rs[2][4]], "scatter_start3"))
    hb = _tie(hb, [rs[3][4]], "tie_after_scatter3")
    grads["ffn1_w_up"] = _wgrad(dub, hb, FF, D, "ffn1_dwu")
    rs.append(_send_start(["scatter"], [grads["ffn1_w_up"]], [rs[3][4]], "scatter_start4"))
    dob = _tie(dob, [rs[4][4]], "tie_after_scatter4")
    grads["ffn1_w_down"] = _wgrad(actb, dob, FF, D, "ffn1_dwd")
    rs.append(_send_start(["scatter"], [grads["ffn1_w_down"]], [rs[4][4]], "scatter_start5"))
    dgb = _tie(dgb, [rs[5][4]], "tie_after_scatter5")
    dx0, dg1 = _ffn_bwd_dx(x0, ffn1_norm, dgb, dub, dx1, wg1, wu1, "ffn1_bwd_dx")
    vec = jnp.concatenate([dg1, dg2, dg3, dgf, dvec[0:3], jnp.broadcast_to(loss_part[:, :1], (1, D))], axis=0)
    rs.append(_send_start(["bcast"], [vec], [rs[5][4]], "scatter_start6"))

    g_out, d_out, m_out, v_out = {}, {}, {}, {}
    me = _my_place()
    after = [rs[last][4]]
    for gi, grp in enumerate(rs_groups):
        kinds = ["scatter"] * len(grp) + (["bcast"] if gi == last else [])
        srcs, lands = _send_wait(kinds, rs[gi], after, f"scatter_wait{gi}")
        for n, src, land in zip(grp, srcs, lands):
            if n == "conv_dw_kernel":
                rows = src.shape[0] // NDEV
                own = lax.dynamic_slice(src, (me * rows, 0), (rows, src.shape[1]))
                g = _gsum(own, land, f"gsum_{n}")[:CONV_W]
                d, m2, v2 = _adamw(w[n][0], g, mo[n][0], vo[n][0], f"adamw_{n}")
                after = [d]
                g, d, m2, v2 = g[None], d[None], m2[None], v2[None]
            else:
                res = _update(src, land, landing_view(w[n], n), landing_view(mo[n], n), landing_view(vo[n], n),
                              f"update_{n}")
                after = [res[1]]
                g, d, m2, v2 = (own_view(a, n) for a in res)
            g_out[n], d_out[n], m_out[n], v_out[n] = g, d, m2, v2
    vland = lands[-1]

    def rows8(src):
        return jnp.concatenate([src[n].reshape(1, D) for n in small] + [jnp.ones((1, D), F32)], axis=0)

    g8, d8, m8, v8 = _small_update(vland, rows8(w), rows8(mo), rows8(vo), "small_update")
    for r, n in enumerate(small):
        shp = w[n].shape
        g_out[n], d_out[n], m_out[n], v_out[n] = (a[r].reshape(shp) for a in (g8, d8, m8, v8))
    loss = g8[7, 0]

    return (loss, dx0[None], *[g_out[n] for n in names], *[d_out[n] for n in names],
            *[m_out[n] for n in names], *[v_out[n] for n in names])
```

```python
import numpy as np
import jax
import jax.numpy as jnp
from jax import lax
from jax.experimental import pallas as pl
from jax.experimental.pallas import tpu as pltpu

F32 = jnp.float32
BF16 = jnp.bfloat16

T = 4096
D = 1024
FF = 2816
NDEV = 8
CONV_W = 31
HEAD = 128
BLK = 128
GROUPS = ((128, 1), (512, 4), (2048, 16))
NHG = 4
AW = NHG * HEAD
IN_W = 2 * D + 3 * 3 * AW + 2 * D
EPS = 1e-6
B1, B2, LR, AEPS, WD, STEP = 0.9, 0.999, 0.001, 1e-08, 0.01, 10
NEG = -1e30
VMEM_LIMIT = 56 * 1024 * 1024
MESH_ID = pl.DeviceIdType.MESH

NT = (((1,), (1,)), ((), ()))
NN = (((1,), (0,)), ((), ()))
TN = (((0,), (0,)), ((), ()))
_DIMS = {"nn": NN, "nt": NT, "tn": TN}


def _cp(sem=None):
    return pltpu.CompilerParams(dimension_semantics=sem, vmem_limit_bytes=VMEM_LIMIT)


def _sig(v):
    return 1.0 / (1.0 + jnp.exp(-v))


def _dot(a, b, dims):
    return lax.dot_general(a, b, dims, preferred_element_type=F32)


def _const_spec(shape):
    nd = len(shape)
    return pl.BlockSpec(shape, lambda *_: (0,) * nd)


def _mm(a, b, *, mode, m, n, k, tm, tn, tk, out_dtype, name, a_map=None, b_map=None,
        o_map=None, out_rows=None, init=None, passthru=None):
    gi, gj, gk = m // tm, n // tn, k // tk
    assert gi * tm == m and gj * tn == n and gk * tk == k, (name, m, n, k, tm, tn, tk)
    if mode == "nn":
        a_blk, b_blk = (tm, tk), (tk, tn)
        da, db = (lambda i, j, kk: (i, kk)), (lambda i, j, kk: (kk, j))
    elif mode == "nt":
        a_blk, b_blk = (tm, tk), (tn, tk)
        da, db = (lambda i, j, kk: (i, kk)), (lambda i, j, kk: (j, kk))
    else:
        a_blk, b_blk = (tk, tm), (tk, tn)
        da, db = (lambda i, j, kk: (kk, i)), (lambda i, j, kk: (kk, j))
    a_map = a_map or da
    b_map = b_map or db
    o_map = o_map or (lambda i, j, kk: (i, j))
    dims = _DIMS[mode]
    extra = init if init is not None else passthru
    out_rows = out_rows or m

    def body(*refs):
        if init is not None:
            a_ref, b_ref, i_ref, o_ref = refs[:4]
        elif passthru is not None:
            a_ref, b_ref, _, o_ref = refs[:4]
        else:
            a_ref, b_ref, o_ref = refs[:3]
        if gk == 1:
            prod = _dot(a_ref[...], b_ref[...], dims)
            if init is not None:
                prod = prod + i_ref[...].astype(F32)
            o_ref[...] = prod.astype(out_dtype)
            return
        acc = refs[-1]
        kk = pl.program_id(2)

        @pl.when(kk == 0)
        def _():
            if init is not None:
                acc[...] = i_ref[...].astype(F32)
            else:
                acc[...] = jnp.zeros_like(acc)

        acc[...] += _dot(a_ref[...], b_ref[...], dims)

        @pl.when(kk == gk - 1)
        def _():
            o_ref[...] = acc[...].astype(out_dtype)

    in_specs = [pl.BlockSpec(a_blk, a_map), pl.BlockSpec(b_blk, b_map)]
    args = [a, b]
    aliases = {}
    if init is not None:
        in_specs.append(pl.BlockSpec((tm, tn), o_map))
        args.append(init)
        aliases = {2: 0}
    elif passthru is not None:
        in_specs.append(pl.BlockSpec(memory_space=pl.ANY))
        args.append(passthru)
        aliases = {2: 0}
    out_dt = extra.dtype if extra is not None else out_dtype
    assert out_dt == out_dtype
    return pl.pallas_call(
        body, name=name, grid=(gi, gj, gk),
        in_specs=in_specs, out_specs=pl.BlockSpec((tm, tn), o_map),
        out_shape=jax.ShapeDtypeStruct((out_rows, n), out_dtype),
        scratch_shapes=[pltpu.VMEM((tm, tn), F32)] if gk > 1 else [],
        input_output_aliases=aliases,
        compiler_params=_cp(("parallel", "parallel", "arbitrary")),
    )(*args)


def _ffn_fwd(x, g, wg_t, wu_t, wd, name, next_gain=None, loss_of=None):
    tm, fc = PERM_TM, 256
    nc = FF // fc
    n_in = 5 + (1 if next_gain is not None else 0) + (2 if loss_of is not None else 0)

    def body(*refs):
        x_ref, g_ref, wg_ref, wu_ref, wd_ref = refs[:5]
        extra_in, outs = refs[5:n_in], refs[n_in:]
        act_ref = outs[-1]
        xv = x_ref[...]
        r = lax.rsqrt(jnp.mean(xv * xv, axis=-1, keepdims=True) + EPS)
        h = (xv * r * g_ref[...]).astype(BF16)
        gg_ref, uu_ref = (outs[0], outs[1]) if loss_of is not None else (outs[1], outs[2])
        for c in range(nc):
            sl = pl.ds(c * fc, fc)
            gg = _dot(h, wg_ref[sl, :], NT)
            uu = _dot(h, wu_ref[sl, :], NT)
            gg_ref[:, sl] = gg.astype(BF16)
            uu_ref[:, sl] = uu.astype(BF16)
            act_ref[:, sl] = (gg * _sig(gg) * uu).astype(BF16)
        y = xv + 0.5 * _dot(act_ref[...], wd_ref[...], NN)
        if loss_of is not None:
            _final_math(y, extra_in[0][...], extra_in[1][...], outs[2], outs[3], outs[4], pl.program_id(0))
            return
        outs[0][...] = y
        if next_gain is not None:
            tile = outs[-2]
            r2 = lax.rsqrt(jnp.mean(y * y, axis=-1, keepdims=True) + EPS)
            hv = y * r2 * extra_in[0][...]
            outs[3][...] = hv.astype(BF16)
            _put_tile(tile, hv)
            for dil, p_ref in zip(DILS, outs[4:4 + len(DILS)]):
                _store_perm(p_ref, tile, dil)

    wspec = pl.BlockSpec((FF, D), lambda i: (0, 0), pipeline_mode=pl.Buffered(1))
    row_d = pl.BlockSpec((tm, D), lambda i: (i, 0))
    row_f = pl.BlockSpec((tm, FF), lambda i: (i, 0))
    in_specs = [row_d, _const_spec((1, D)), wspec, wspec, wspec]
    args = [x, g, wg_t, wu_t, wd]
    f_shape = jax.ShapeDtypeStruct((T, FF), BF16)
    scratch = [pltpu.VMEM((tm, FF), BF16)]
    if loss_of is not None:
        in_specs += [_const_spec((1, D)), row_d]
        args += list(loss_of)
        out_specs = [row_f, row_f, row_d, _const_spec((1, D)), _const_spec((1, 128))]
        out_shape = [f_shape, f_shape, jax.ShapeDtypeStruct((T, D), F32), jax.ShapeDtypeStruct((1, D), F32),
                     jax.ShapeDtypeStruct((1, 128), F32)]
    else:
        out_specs = [row_d, row_f, row_f]
        out_shape = [jax.ShapeDtypeStruct((T, D), F32), f_shape, f_shape]
        if next_gain is not None:
            in_specs.append(_const_spec((1, D)))
            args.append(next_gain)
            out_specs += [row_d] + [_perm_spec(d, D) for d in DILS]
            out_shape += [jax.ShapeDtypeStruct((T, D), BF16)] + [_perm_shape(d, D, BF16) for d in DILS]
            scratch = [_tile_scratch(D)] + scratch
    out = pl.pallas_call(
        body, name=name, grid=(T // tm,), in_specs=in_specs, out_specs=out_specs, out_shape=out_shape,
        scratch_shapes=scratch,
        compiler_params=_cp(("arbitrary",) if loss_of is not None else ("parallel",)),
    )(*args)
    if next_gain is not None:
        return out[0], out[1], out[2], [out[3]] + [o.reshape(T, D) for o in out[4:]]
    return tuple(out)


def _ffn_bwd(x, g, gg_all, uu_all, dout, wg_t, wu_t, wd, name):
    tm, fc = 256, 256
    nc = FF // fc

    def body(x_ref, g_ref, gg_ref, uu_ref, do_ref, wg_ref, wu_ref, wd_ref,
             dx_ref, dgam_ref, dg_ref, du_ref, act_ref, h_ref, db_ref):
        i = pl.program_id(0)
        xv = x_ref[...]
        r = lax.rsqrt(jnp.mean(xv * xv, axis=-1, keepdims=True) + EPS)
        xhat = xv * r
        gam = g_ref[...]
        h_ref[...] = (xhat * gam).astype(BF16)
        dov = do_ref[...]
        dbv = (0.5 * dov).astype(BF16)
        db_ref[...] = dbv
        for c in range(nc):
            sl = pl.ds(c * fc, fc)
            da = _dot(dbv, wd_ref[sl, :], NT)
            gg = gg_ref[:, sl].astype(F32)
            uu = uu_ref[:, sl].astype(F32)
            s = _sig(gg)
            si = gg * s
            dgv = (da * uu * (s * (1.0 + gg * (1.0 - s)))).astype(BF16)
            duv = (da * si).astype(BF16)
            dg_ref[:, sl] = dgv
            du_ref[:, sl] = duv
            act_ref[:, sl] = (si * uu).astype(BF16)
        dh = _dot(dg_ref[...], wg_ref[...], NN) + _dot(du_ref[...], wu_ref[...], NN)

        @pl.when(i == 0)
        def _():
            dgam_ref[...] = jnp.zeros_like(dgam_ref)

        dgam_ref[...] += jnp.sum(dh * xhat, axis=0, keepdims=True)
        dxh = dh * gam
        dx_ref[...] = dov + r * (dxh - xhat * jnp.mean(dxh * xhat, axis=-1, keepdims=True))

    wspec = pl.BlockSpec((FF, D), lambda i: (0, 0), pipeline_mode=pl.Buffered(1))
    row_d = pl.BlockSpec((tm, D), lambda i: (i, 0))
    row_f = pl.BlockSpec((tm, FF), lambda i: (i, 0))
    return pl.pallas_call(
        body, name=name, grid=(T // tm,),
        in_specs=[row_d, _const_spec((1, D)), row_f, row_f, row_d, wspec, wspec, wspec],
        out_specs=[row_d, _const_spec((1, D)), row_f, row_f, row_f, row_d, row_d],
        out_shape=[jax.ShapeDtypeStruct((T, D), F32), jax.ShapeDtypeStruct((1, D), F32),
                   jax.ShapeDtypeStruct((T, FF), BF16), jax.ShapeDtypeStruct((T, FF), BF16),
                   jax.ShapeDtypeStruct((T, FF), BF16), jax.ShapeDtypeStruct((T, D), BF16),
                   jax.ShapeDtypeStruct((T, D), BF16)],
        compiler_params=_cp(("arbitrary",)),
    )(x, g, gg_all, uu_all, dout, wg_t, wu_t, wd)


def _ffn_bwd_pre(x, g, gg_all, uu_all, dout, wd, name):
    tm, fc = 512, 256
    nc = FF // fc

    def body(x_ref, g_ref, gg_ref, uu_ref, do_ref, wd_ref, dg_ref, du_ref, act_ref, h_ref, db_ref):
        xv = x_ref[...]
        r = lax.rsqrt(jnp.mean(xv * xv, axis=-1, keepdims=True) + EPS)
        h_ref[...] = (xv * r * g_ref[...]).astype(BF16)
        dbv = (0.5 * do_ref[...]).astype(BF16)
        db_ref[...] = dbv
        for c in range(nc):
            sl = pl.ds(c * fc, fc)
            da = _dot(dbv, wd_ref[sl, :], NT)
            gg = gg_ref[:, sl].astype(F32)
            uu = uu_ref[:, sl].astype(F32)
            s = _sig(gg)
            si = gg * s
            dg_ref[:, sl] = (da * uu * (s * (1.0 + gg * (1.0 - s)))).astype(BF16)
            du_ref[:, sl] = (da * si).astype(BF16)
            act_ref[:, sl] = (si * uu).astype(BF16)

    wspec = pl.BlockSpec((FF, D), lambda i: (0, 0), pipeline_mode=pl.Buffered(1))
    row_d = pl.BlockSpec((tm, D), lambda i: (i, 0))
    row_f = pl.BlockSpec((tm, FF), lambda i: (i, 0))
    return pl.pallas_call(
        body, name=name, grid=(T // tm,),
        in_specs=[row_d, _const_spec((1, D)), row_f, row_f, row_d, wspec],
        out_specs=[row_f, row_f, row_f, row_d, row_d],
        out_shape=[jax.ShapeDtypeStruct((T, FF), BF16), jax.ShapeDtypeStruct((T, FF), BF16),
                   jax.ShapeDtypeStruct((T, FF), BF16), jax.ShapeDtypeStruct((T, D), BF16),
                   jax.ShapeDtypeStruct((T, D), BF16)],
        compiler_params=_cp(("parallel",)),
    )(x, g, gg_all, uu_all, dout, wd)


def _ffn_bwd_dx(x, g, dgb, dub, dout, wg_t, wu_t, name):
    tm = 512

    def body(x_ref, g_ref, dg_ref, du_ref, do_ref, wg_ref, wu_ref, dx_ref, dgam_ref):
        i = pl.program_id(0)
        xv = x_ref[...]
        r = lax.rsqrt(jnp.mean(xv * xv, axis=-1, keepdims=True) + EPS)
        xhat = xv * r
        gam = g_ref[...]
        dh = _dot(dg_ref[...], wg_ref[...], NN) + _dot(du_ref[...], wu_ref[...], NN)

        @pl.when(i == 0)
        def _():
            dgam_ref[...] = jnp.zeros_like(dgam_ref)

        dgam_ref[...] += jnp.sum(dh * xhat, axis=0, keepdims=True)
        dxh = dh * gam
        dx_ref[...] = do_ref[...] + r * (dxh - xhat * jnp.mean(dxh * xhat, axis=-1, keepdims=True))

    wspec = pl.BlockSpec((FF, D), lambda i: (0, 0), pipeline_mode=pl.Buffered(1))
    row_d = pl.BlockSpec((tm, D), lambda i: (i, 0))
    row_f = pl.BlockSpec((tm, FF), lambda i: (i, 0))
    return pl.pallas_call(
        body, name=name, grid=(T // tm,),
        in_specs=[row_d, _const_spec((1, D)), row_f, row_f, row_d, wspec, wspec],
        out_specs=[row_d, _const_spec((1, D))],
        out_shape=[jax.ShapeDtypeStruct((T, D), F32), jax.ShapeDtypeStruct((1, D), F32)],
        compiler_params=_cp(("arbitrary",)),
    )(x, g, dgb, dub, dout, wg_t, wu_t)


def _wgrad(a, b, m, n, name):
    tm = m // 2 if m == FF else m
    return _mm(a, b, mode="tn", m=m, n=n, k=T, tm=tm, tn=n, tk=min(T, 2048), out_dtype=BF16, name=name)


PERM_TM = 512
DILS = tuple(d for _, d in GROUPS if d > 1)


def _perm_spec(dil, cols):
    return pl.BlockSpec((dil, PERM_TM // dil, cols), lambda i: (0, i, 0))


def _perm_shape(dil, cols, dtype):
    return jax.ShapeDtypeStruct((dil, T // dil, cols), dtype)


LANES = 128


def _tile_scratch(cols):
    return pltpu.VMEM((cols // LANES, PERM_TM, LANES), F32)


def _put_tile(tile, value):
    for c in range(tile.shape[0]):
        tile[c] = value[:, c * LANES:(c + 1) * LANES]


def _get_tile(tile):
    return jnp.concatenate([tile[c] for c in range(tile.shape[0])], axis=1)


def _store_perm(out_ref, tile, dil):
    for r in range(dil):
        for c in range(tile.shape[0]):
            out_ref[r, :, pl.ds(c * LANES, LANES)] = tile[c, pl.ds(r, PERM_TM // dil, stride=dil), :].astype(
                out_ref.dtype)


def _load_unperm(in_ref, tile, dil):
    for r in range(dil):
        for c in range(tile.shape[0]):
            tile[c, pl.ds(r, PERM_TM // dil, stride=dil), :] = in_ref[r, :, pl.ds(c * LANES, LANES)].astype(F32)


def _final_math(xv, gam, tgt, dx_ref, dgam_ref, loss_ref, i):
    r = lax.rsqrt(jnp.mean(xv * xv, axis=-1, keepdims=True) + EPS)
    xhat = xv * r
    err = xhat * gam - tgt
    part = 0.5 * jnp.sum(jnp.mean(err * err, axis=-1, keepdims=True), axis=0, keepdims=True)
    dy = err * (1.0 / D)

    @pl.when(i == 0)
    def _():
        dgam_ref[...] = jnp.zeros_like(dgam_ref)
        loss_ref[...] = jnp.zeros_like(loss_ref)

    dgam_ref[...] += jnp.sum(dy * xhat, axis=0, keepdims=True)
    loss_ref[...] += jnp.broadcast_to(part, loss_ref.shape)
    dxh = dy * gam
    dx_ref[...] = r * (dxh - xhat * jnp.mean(dxh * xhat, axis=-1, keepdims=True))


def _rms_bwd(x, g, dhs, dres, name):
    tm = PERM_TM
    dils = [d for _, d in GROUPS]
    nh = len(dhs)
    assert nh == len(dils)

    def body(*refs):
        x_ref, g_ref = refs[:2]
        dh_refs = refs[2:2 + nh]
        dr_ref, dx_ref, dgam_ref, tile = refs[2 + nh:]
        i = pl.program_id(0)
        xv = x_ref[...]
        r = lax.rsqrt(jnp.mean(xv * xv, axis=-1, keepdims=True) + EPS)
        xhat = xv * r
        gam = g_ref[...]
        dh = None
        for dil, ref in zip(dils, dh_refs):
            if dil == 1:
                part = ref[...]
            else:
                _load_unperm(ref, tile, dil)
                part = _get_tile(tile)
            dh = part if dh is None else dh + part

        @pl.when(i == 0)
        def _():
            dgam_ref[...] = jnp.zeros_like(dgam_ref)

        dgam_ref[...] += jnp.sum(dh * xhat, axis=0, keepdims=True)
        dxh = dh * gam
        dx_ref[...] = dr_ref[...] + r * (dxh - xhat * jnp.mean(dxh * xhat, axis=-1, keepdims=True))

    row_d = pl.BlockSpec((tm, D), lambda i: (i, 0))
    dh_specs = [row_d if d == 1 else _perm_spec(d, D) for d in dils]
    dh_args = [a if d == 1 else a.reshape(d, T // d, D) for d, a in zip(dils, dhs)]
    return pl.pallas_call(
        body, name=name, grid=(T // tm,),
        in_specs=[row_d, _const_spec((1, D))] + dh_specs + [row_d],
        out_specs=[row_d, _const_spec((1, D))],
        out_shape=[jax.ShapeDtypeStruct((T, D), F32), jax.ShapeDtypeStruct((1, D), F32)],
        scratch_shapes=[_tile_scratch(D)],
        compiler_params=_cp(("arbitrary",)),
    )(x, g, *dh_args, dres)


CONV_TM = 256
CONV_HALO = 32
CONV_RB = 16


def _glu(ab):
    ab = ab.astype(F32)
    return ab[:, :D] * _sig(ab[:, D:])


def _ln_stats(z1):
    mu = jnp.mean(z1, axis=-1, keepdims=True)
    zc = z1 - mu
    rstd = lax.rsqrt(jnp.mean(zc * zc, axis=-1, keepdims=True) + EPS)
    return zc * rstd, rstd


def _fill_shifts(zs):
    n = zs.shape[1] - 8
    for s in range(1, 8):
        zs[s, pl.ds(0, n), :] = zs[0, pl.ds(s, n), :]


def _shifted(zs, start, rows):
    q, s = divmod(start, 8)
    return zs[s, pl.ds(8 * q, rows), :]


def _conv_fwd(ab, kern, dwb, lng, lnb, name, guest=None):
    tm, hl, rb = CONV_TM, CONV_HALO, CONV_RB
    off = hl - (CONV_W - 1)
    if guest is not None:
        g_a, g_b, g_first, g_nblk, g_rows = guest

    def body(ab_ref, abh_ref, k_ref, dwb_ref, lng_ref, lnb_ref, *rest):
        if guest is not None:
            ga_ref, gb_refs, rest = rest[0], rest[1:1 + g_nblk], rest[1 + g_nblk:]
            z1_ref, z3_ref, go_ref, zs = rest
            for q, gb_ref in enumerate(gb_refs):
                go_ref[:, pl.ds(q * g_rows, g_rows)] = _dot(ga_ref[...], gb_ref[...], NT).astype(BF16)
        else:
            z1_ref, z3_ref, zs = rest
        i = pl.program_id(0)
        zs[0, pl.ds(0, hl), :] = jnp.where(i > 0, _glu(abh_ref[...]), 0.0)
        zs[0, pl.ds(hl, tm), :] = _glu(ab_ref[...])
        _fill_shifts(zs)
        for b in range(tm // rb):
            acc = jnp.zeros((rb, D), F32)
            for j in range(CONV_W):
                acc = acc + _shifted(zs, b * rb + off + j, rb) * k_ref[pl.ds(j, 1), :]
            z1 = acc + dwb_ref[...]
            z1_ref[pl.ds(b * rb, rb), :] = z1
            zn, _ = _ln_stats(z1)
            z2 = zn * lng_ref[...] + lnb_ref[...]
            z3_ref[pl.ds(b * rb, rb), :] = (z2 * _sig(z2)).astype(BF16)

    row = pl.BlockSpec((tm, D), lambda i: (i, 0))
    g_specs, g_args, g_ospecs, g_oshapes = [], [], [], []
    if guest is not None:
        kdim = g_a.shape[1]
        g_specs = [pl.BlockSpec((tm, kdim), lambda i: (i, 0))]
        g_specs += [pl.BlockSpec((g_rows, kdim), lambda i, q=q: (g_first + q, 0), pipeline_mode=pl.Buffered(1))
                    for q in range(g_nblk)]
        g_args = [g_a] + [g_b] * g_nblk
        g_ospecs = [pl.BlockSpec((tm, g_nblk * g_rows), lambda i: (i, 0))]
        g_oshapes = [jax.ShapeDtypeStruct((T, g_nblk * g_rows), BF16)]
    return pl.pallas_call(
        body, name=name, grid=(T // tm,),
        in_specs=[pl.BlockSpec((tm, 2 * D), lambda i: (i, 0)),
                  pl.BlockSpec((hl, 2 * D), lambda i: (jnp.maximum(i * (tm // hl) - 1, 0), 0)),
                  _const_spec((32, D)), _const_spec((1, D)), _const_spec((1, D)), _const_spec((1, D))] + g_specs,
        out_specs=[row, row] + g_ospecs,
        out_shape=[jax.ShapeDtypeStruct((T, D), F32), jax.ShapeDtypeStruct((T, D), BF16)] + g_oshapes,
        scratch_shapes=[pltpu.VMEM((8, hl + tm, D), F32)],
        compiler_params=_cp(("parallel",)),
    )(ab, ab, kern, dwb, lng, lnb, *g_args)


GUEST_TM = 256


def _conv_bwd(dz3, z1, ab, kern, lng, lnb, name, guest_lhs=(), guest_rhs=None):
    tm, hl, rb = CONV_TM, CONV_HALO, CONV_RB
    off = hl - (CONV_W - 1)
    nsteps = T // tm
    ng = len(guest_lhs)
    gblocks = [a.shape[1] // GUEST_TM for a in guest_lhs]
    assert all(gb <= nsteps and gb * GUEST_TM == a.shape[1] for gb, a in zip(gblocks, guest_lhs))

    def ln_bwd(dz3v, z1v, lngv, lnbv):
        zn, rstd = _ln_stats(z1v)
        z2 = zn * lngv + lnbv
        s = _sig(z2)
        dz2 = dz3v * (s * (1.0 + z2 * (1.0 - s)))
        dzn = dz2 * lngv
        dz1 = rstd * (dzn - jnp.mean(dzn, axis=-1, keepdims=True)
                      - zn * jnp.mean(dzn * zn, axis=-1, keepdims=True))
        return dz1, dz2, zn

    def body(dz3_ref, dz3h_ref, z1_ref, z1h_ref, ab_ref, abh_ref, k_ref, lng_ref, lnb_ref, *rest):
        g_in, rest = rest[:ng + (1 if ng else 0)], rest[ng + (1 if ng else 0):]
        dab_ref, dk_ref, dvec_ref = rest[:3]
        g_out, (zs, dzs) = rest[3:3 + ng], rest[3 + ng:]
        i = pl.program_id(0)
        lngv, lnbv = lng_ref[...], lnb_ref[...]

        for a_ref, o_ref, gb in zip(g_in[:ng], g_out, gblocks):
            @pl.when(i < gb)
            def _(a_ref=a_ref, o_ref=o_ref):
                o_ref[...] = _dot(a_ref[...], g_in[ng][...], TN).astype(BF16)

        @pl.when(i == 0)
        def _():
            dk_ref[...] = jnp.zeros_like(dk_ref)
            dvec_ref[...] = jnp.zeros_like(dvec_ref)

        dz1, dz2, zn = ln_bwd(dz3_ref[...].astype(F32), z1_ref[...], lngv, lnbv)
        dvec_ref[pl.ds(0, 1), :] += jnp.sum(dz1, axis=0, keepdims=True)
        dvec_ref[pl.ds(1, 1), :] += jnp.sum(dz2 * zn, axis=0, keepdims=True)
        dvec_ref[pl.ds(2, 1), :] += jnp.sum(dz2, axis=0, keepdims=True)
        dzs[0, pl.ds(0, tm), :] = dz1
        dz1h, _, _ = ln_bwd(dz3h_ref[...].astype(F32), z1h_ref[...], lngv, lnbv)
        dzs[0, pl.ds(tm, hl), :] = jnp.where(i < nsteps - 1, dz1h, 0.0)
        _fill_shifts(dzs)
        zs[0, pl.ds(0, hl), :] = jnp.where(i > 0, _glu(abh_ref[...]), 0.0)
        zs[0, pl.ds(hl, tm), :] = _glu(ab_ref[...])
        _fill_shifts(zs)

        for j in range(CONV_W):
            tot = jnp.zeros((rb, D), F32)
            for b in range(tm // rb):
                tot = tot + dzs[0, pl.ds(b * rb, rb), :] * _shifted(zs, b * rb + off + j, rb)
            dk_ref[pl.ds(j, 1), :] += jnp.sum(tot, axis=0, keepdims=True)

        for b in range(tm // rb):
            acc = jnp.zeros((rb, D), F32)
            for j in range(CONV_W):
                acc = acc + _shifted(dzs, b * rb + (CONV_W - 1) - j, rb) * k_ref[pl.ds(j, 1), :]
            av = ab_ref[pl.ds(b * rb, rb), pl.ds(0, D)].astype(F32)
            sb = _sig(ab_ref[pl.ds(b * rb, rb), pl.ds(D, D)].astype(F32))
            dab_ref[pl.ds(b * rb, rb), pl.ds(0, D)] = (acc * sb).astype(BF16)
            dab_ref[pl.ds(b * rb, rb), pl.ds(D, D)] = (acc * av * sb * (1.0 - sb)).astype(BF16)

    row = pl.BlockSpec((tm, D), lambda i: (i, 0))
    nxt = pl.BlockSpec((hl, D), lambda i: (jnp.minimum((i + 1) * (tm // hl), T // hl - 1), 0))
    g_specs, g_args, g_ospecs, g_oshapes = [], [], [], []
    for a, gb in zip(guest_lhs, gblocks):
        g_specs.append(pl.BlockSpec((T, GUEST_TM), lambda i, gb=gb: (0, jnp.minimum(i, gb - 1))))
        g_args.append(a)
        g_ospecs.append(pl.BlockSpec((GUEST_TM, guest_rhs.shape[1]), lambda i, gb=gb: (jnp.minimum(i, gb - 1), 0)))
        g_oshapes.append(jax.ShapeDtypeStruct((a.shape[1], guest_rhs.shape[1]), BF16))
    if ng:
        g_specs.append(pl.BlockSpec(guest_rhs.shape, lambda i: (0, 0), pipeline_mode=pl.Buffered(1)))
        g_args.append(guest_rhs)
    return pl.pallas_call(
        body, name=name, grid=(nsteps,),
        in_specs=[row, nxt, row, nxt,
                  pl.BlockSpec((tm, 2 * D), lambda i: (i, 0)),
                  pl.BlockSpec((hl, 2 * D), lambda i: (jnp.maximum(i * (tm // hl) - 1, 0), 0)),
                  _const_spec((32, D)), _const_spec((1, D)), _const_spec((1, D))] + g_specs,
        out_specs=[pl.BlockSpec((tm, 2 * D), lambda i: (i, 0)), _const_spec((32, D)), _const_spec((8, D))]
        + g_ospecs,
        out_shape=[jax.ShapeDtypeStruct((T, 2 * D), BF16), jax.ShapeDtypeStruct((32, D), F32),
                   jax.ShapeDtypeStruct((8, D), F32)] + g_oshapes,
        scratch_shapes=[pltpu.VMEM((8, hl + tm, D), F32), pltpu.VMEM((8, tm + hl, D), F32)],
        compiler_params=_cp(("arbitrary",)),
    )(dz3, dz3, z1, z1, ab, ab, kern, lng, lnb, *g_args)


def _alibi_slopes():
    h = np.arange(1, 3 * NHG + 1, dtype=np.float32)
    return np.power(np.float32(2.0), -8.0 * h / np.float32(3 * NHG)).astype(np.float32)


def _band_bias(gi):
    _, dil = GROUPS[gi]
    slopes = _alibi_slopes()[gi * NHG:(gi + 1) * NHG]
    qi = np.arange(BLK)[:, None]
    ki = np.arange(2 * BLK)[None, :]
    steps = BLK + qi - ki
    band = (steps >= 0) & (steps <= BLK)
    bias = -slopes[:, None, None] * (dil * steps).astype(np.float32)[None]
    return jnp.asarray(np.where(band[None], bias, np.float32(NEG)).astype(np.float32))


QB_FWD = 8
QB_BWD = 32


def _attn_specs(qb):
    prev = lambda n: jnp.maximum(n * qb - 1, 0)
    return [pl.BlockSpec((qb * BLK, HEAD), lambda h, n: (n, h)),
            pl.BlockSpec((BLK, HEAD), lambda h, n: (prev(n), NHG + h)),
            pl.BlockSpec((qb * BLK, HEAD), lambda h, n: (n, NHG + h)),
            pl.BlockSpec((BLK, HEAD), lambda h, n: (prev(n), 2 * NHG + h)),
            pl.BlockSpec((qb * BLK, HEAD), lambda h, n: (n, 2 * NHG + h)),
            pl.BlockSpec((None, BLK, 2 * BLK), lambda h, n: (h, 0, 0))]


def _scores(q, kcat, bias, blk, seg):
    s = _dot(q, kcat, NT) * (HEAD ** -0.5) + bias
    col = lax.broadcasted_iota(jnp.int32, s.shape, 1)
    first = (blk % seg) == 0
    return jnp.where(jnp.logical_and(first, col < BLK), NEG, s)


def _attn_fwd(qkv, gi, name):
    seg = (T // GROUPS[gi][1]) // BLK

    qb = min(QB_FWD, T // BLK)

    def body(q_ref, kp_ref, kc_ref, vp_ref, vc_ref, bias_ref, o_ref, l_ref):
        n = pl.program_id(0)
        for h in range(NHG):
            cols = pl.ds(h * HEAD, HEAD)
            kwin = jnp.concatenate([kp_ref[:, cols], kc_ref[:, cols]], axis=0)
            vwin = jnp.concatenate([vp_ref[:, cols], vc_ref[:, cols]], axis=0)
            bias = bias_ref[h]
            for b in range(qb):
                rows = pl.ds(b * BLK, BLK)
                s = _scores(q_ref[rows, cols], kwin[b * BLK:(b + 2) * BLK], bias, n * qb + b, seg)
                mx = jnp.max(s, axis=-1, keepdims=True)
                p = jnp.exp(s - mx)
                den = jnp.sum(p, axis=-1, keepdims=True)
                o_ref[rows, cols] = (_dot(p.astype(BF16), vwin[b * BLK:(b + 2) * BLK], NN) / den).astype(BF16)
                l_ref[rows, cols] = jnp.broadcast_to(mx + jnp.log(den), (BLK, HEAD))

    prev = lambda n: jnp.maximum(n * qb - 1, 0)
    cur = lambda part: pl.BlockSpec((qb * BLK, AW), lambda n: (n, part))
    halo = lambda part: pl.BlockSpec((BLK, AW), lambda n: (prev(n), part))
    return pl.pallas_call(
        body, name=name, grid=(T // (qb * BLK),),
        in_specs=[cur(0), halo(1), cur(1), halo(2), cur(2), _const_spec((NHG, BLK, 2 * BLK))],
        out_specs=[cur(0), cur(0)],
        out_shape=[jax.ShapeDtypeStruct((T, AW), BF16), jax.ShapeDtypeStruct((T, AW), F32)],
        compiler_params=_cp(("parallel",)),
    )(qkv, qkv, qkv, qkv, qkv, _band_bias(gi))


def _attn_bwd(qkv, dob, lse, delta, gi, name):
    seg = (T // GROUPS[gi][1]) // BLK
    qb = min(QB_BWD, T // BLK)
    nb = T // (qb * BLK)
    scale = HEAD ** -0.5

    def body(q_ref, kp_ref, kc_ref, vp_ref, vc_ref, bias_ref, do_ref, l_ref, dl_ref, out_ref, dk_acc, dv_acc):
        n = pl.program_id(1)
        kwin = jnp.concatenate([kp_ref[...], kc_ref[...]], axis=0)
        vwin = jnp.concatenate([vp_ref[...], vc_ref[...]], axis=0)
        bias = bias_ref[...]
        dks, dvs = [], []
        for b in range(qb):
            rows = pl.ds(b * BLK, BLK)
            q = q_ref[rows, :]
            kcat = kwin[b * BLK:(b + 2) * BLK]
            s = _scores(q, kcat, bias, n * qb + b, seg)
            p = jnp.exp(s - l_ref[rows, pl.ds(0, 1)])
            dov = do_ref[rows, :]
            dvs.append(_dot(p.astype(BF16), dov, TN))
            dp = _dot(dov, vwin[b * BLK:(b + 2) * BLK], NT)
            dsb = (p * (dp - dl_ref[rows, pl.ds(0, 1)]) * scale).astype(BF16)
            row = pl.ds(pl.multiple_of((n * qb + b) * BLK, BLK), BLK)
            out_ref[0, row, :] = _dot(dsb, kcat, NN).astype(BF16)
            dks.append(_dot(dsb, q, TN))
        for b in range(qb):
            row = pl.ds(pl.multiple_of((n * qb + b) * BLK, BLK), BLK)
            if b + 1 < qb:
                dk_acc[row, :] = dks[b][BLK:] + dks[b + 1][:BLK]
                dv_acc[row, :] = dvs[b][BLK:] + dvs[b + 1][:BLK]
            else:
                dk_acc[row, :] = dks[b][BLK:]
                dv_acc[row, :] = dvs[b][BLK:]

        @pl.when(n > 0)
        def _():
            prow = pl.ds(pl.multiple_of((n * qb - 1) * BLK, BLK), BLK)
            dk_acc[prow, :] += dks[0][:BLK]
            dv_acc[prow, :] += dvs[0][:BLK]

        @pl.when(n == nb - 1)
        def _():
            out_ref[1] = dk_acc[...].astype(BF16)
            out_ref[2] = dv_acc[...].astype(BF16)

    oblk = pl.BlockSpec((qb * BLK, HEAD), lambda h, n: (n, h))
    return pl.pallas_call(
        body, name=name, grid=(NHG, nb),
        in_specs=_attn_specs(qb) + [oblk, oblk, oblk],
        out_specs=pl.BlockSpec((3, T, HEAD), lambda h, n: (0, 0, h)),
        out_shape=jax.ShapeDtypeStruct((3, T, AW), BF16),
        scratch_shapes=[pltpu.VMEM((T, HEAD), F32), pltpu.VMEM((T, HEAD), F32)],
        compiler_params=_cp(("parallel", "arbitrary")),
    )(qkv, qkv, qkv, qkv, qkv, _band_bias(gi), dob, lse, delta)


def _merge(outs, lses, name):
    tm = PERM_TM
    dils = [d for _, d in GROUPS]
    ng = len(dils)

    def body(*refs):
        in_refs = refs[:2 * ng]
        ab_ref = refs[2 * ng]
        lse_refs = refs[2 * ng + 1:3 * ng + 1]
        tile = refs[-1]

        def token_order(ref, dil):
            if dil == 1:
                return ref[...].astype(F32)
            _load_unperm(ref, tile, dil)
            return _get_tile(tile)

        os = [token_order(in_refs[2 * i], d) for i, d in enumerate(dils)]
        ls = [token_order(in_refs[2 * i + 1], d) for i, d in enumerate(dils)]
        mx = jnp.maximum(jnp.maximum(ls[0], ls[1]), ls[2])
        es = [jnp.exp(v - mx) for v in ls]
        tot = es[0] + es[1] + es[2]
        att = (es[0] / tot) * os[0] + (es[1] / tot) * os[1] + (es[2] / tot) * os[2]
        ab_ref[...] = att.astype(BF16)
        lse = mx + jnp.log(tot)
        _put_tile(tile, lse)
        for dil, ref in zip(dils, lse_refs):
            if dil == 1:
                ref[...] = lse
            else:
                _store_perm(ref, tile, dil)

    row = pl.BlockSpec((tm, AW), lambda i: (i, 0))
    specs = [row if d == 1 else _perm_spec(d, AW) for d in dils]
    args = []
    for d, o, l in zip(dils, outs, lses):
        args += [o, l] if d == 1 else [o.reshape(d, T // d, AW), l.reshape(d, T // d, AW)]
    out = pl.pallas_call(
        body, name=name, grid=(T // tm,),
        in_specs=[sp for sp in specs for _ in range(2)], out_specs=[row] + specs,
        out_shape=[jax.ShapeDtypeStruct((T, AW), BF16)]
        + [jax.ShapeDtypeStruct((T, AW), F32) if d == 1 else _perm_shape(d, AW, F32) for d in dils],
        scratch_shapes=[_tile_scratch(AW)],
        compiler_params=_cp(("parallel",)),
    )(*args)
    return out[0], [o.reshape(T, AW) for o in out[1:]]


def _mix_out(z3b, attnb, gates, wc, wa_t, wo, x1, name):
    tm = 512

    def body(z_ref, a_ref, g_ref, wc_ref, wa_ref, wo_ref, x_ref, xo_ref, yc_ref, ya_ref, mx_ref):
        yc = _dot(z_ref[...], wc_ref[...], NN)
        ya = _dot(a_ref[...], wa_ref[...], NT)
        yc_ref[...] = yc.astype(BF16)
        ya_ref[...] = ya.astype(BF16)
        gv = g_ref[...].astype(F32)
        mixed = (_sig(gv[:, :D]) * yc + _sig(gv[:, D:]) * ya).astype(BF16)
        mx_ref[...] = mixed
        xo_ref[...] = x_ref[...] + _dot(mixed, wo_ref[...], NN)

    row = pl.BlockSpec((tm, D), lambda i: (i, 0))
    return pl.pallas_call(
        body, name=name, grid=(T // tm,),
        in_specs=[row, pl.BlockSpec((tm, AW), lambda i: (i, 0)), pl.BlockSpec((tm, 2 * D), lambda i: (i, 0)),
                  _const_spec((D, D)), _const_spec((D, AW)), _const_spec((D, D)), row],
        out_specs=[row, row, row, row],
        out_shape=[jax.ShapeDtypeStruct((T, D), F32), jax.ShapeDtypeStruct((T, D), BF16),
                   jax.ShapeDtypeStruct((T, D), BF16), jax.ShapeDtypeStruct((T, D), BF16)],
        compiler_params=_cp(("parallel",)),
    )(z3b, attnb, gates, wc, wa_t, wo, x1)


def _mix_out_bwd(dx2, gates, yc, ya, attn, wc, wa_t, wo, name):
    tm = PERM_TM
    dils = [d for _, d in GROUPS]
    ng = len(dils)

    def body(dx_ref, g_ref, yc_ref, ya_ref, at_ref, wc_ref, wa_ref, wo_ref,
             dg_ref, dyc_ref, dya_ref, dxb_ref, dz3_ref, *rest):
        dat_refs, dl_refs, tile = rest[:ng], rest[ng:2 * ng], rest[-1]
        dxb = dx_ref[...].astype(BF16)
        dxb_ref[...] = dxb
        dmix = _dot(dxb, wo_ref[...], NT)
        gv = g_ref[...].astype(F32)
        sc = _sig(gv[:, :D])
        sa = _sig(gv[:, D:])
        ycv, yav = yc_ref[...].astype(F32), ya_ref[...].astype(F32)
        dg_ref[:, pl.ds(0, D)] = (dmix * ycv * sc * (1.0 - sc)).astype(BF16)
        dg_ref[:, pl.ds(D, D)] = (dmix * yav * sa * (1.0 - sa)).astype(BF16)
        dyc = (dmix * sc).astype(BF16)
        dya = (dmix * sa).astype(BF16)
        dyc_ref[...] = dyc
        dya_ref[...] = dya
        dz3_ref[...] = _dot(dyc, wc_ref[...], NT).astype(BF16)
        dat = _dot(dya, wa_ref[...], NN)
        prod = dat * at_ref[...].astype(F32)
        delta = jnp.concatenate(
            [jnp.broadcast_to(jnp.sum(prod[:, h * HEAD:(h + 1) * HEAD], axis=-1, keepdims=True), (tm, HEAD))
             for h in range(NHG)], axis=1)
        for value, out_refs in ((dat, dat_refs), (delta, dl_refs)):
            _put_tile(tile, value)
            for dil, ref in zip(dils, out_refs):
                if dil == 1:
                    ref[...] = value.astype(ref.dtype)
                else:
                    _store_perm(ref, tile, dil)

    row = pl.BlockSpec((tm, D), lambda i: (i, 0))
    row2 = pl.BlockSpec((tm, 2 * D), lambda i: (i, 0))
    rowa = pl.BlockSpec((tm, AW), lambda i: (i, 0))
    aspecs = [rowa if d == 1 else _perm_spec(d, AW) for d in dils]

    def ashapes(dtype):
        return [jax.ShapeDtypeStruct((T, AW), dtype) if d == 1 else _perm_shape(d, AW, dtype) for d in dils]

    out = pl.pallas_call(
        body, name=name, grid=(T // tm,),
        in_specs=[row, row2, row, row, rowa, _const_spec((D, D)), _const_spec((D, AW)), _const_spec((D, D))],
        out_specs=[row2, row, row, row, row] + aspecs + aspecs,
        out_shape=[jax.ShapeDtypeStruct((T, 2 * D), BF16), jax.ShapeDtypeStruct((T, D), BF16),
                   jax.ShapeDtypeStruct((T, D), BF16), jax.ShapeDtypeStruct((T, D), BF16),
                   jax.ShapeDtypeStruct((T, D), BF16)] + ashapes(BF16) + ashapes(F32),
        scratch_shapes=[_tile_scratch(AW)],
        compiler_params=_cp(("parallel",)),
    )(dx2, gates, yc, ya, attn, wc, wa_t, wo)
    dats = [o.reshape(T, AW) for o in out[5:5 + ng]]
    deltas = [o.reshape(T, AW) for o in out[5 + ng:5 + 2 * ng]]
    return out[0], out[1], out[2], out[3], out[4], dats, deltas


def _peer(k):
    x, y, c = lax.axis_index("x"), lax.axis_index("y"), lax.axis_index("c")
    px = 1 - x if k & 4 else x
    py = 1 - y if k & 2 else y
    pc = 1 - c if k & 1 else c
    return (px, py, pc), 4 * px + 2 * py + pc


HBM_SPEC = pl.BlockSpec(memory_space=pltpu.HBM)
SEM_SPEC = pl.BlockSpec(memory_space=pltpu.SEMAPHORE)
EFFECT = pltpu.SideEffectType.DATAFLOW_SIDE_EFFECTING


def _my_place():
    return 4 * lax.axis_index("x") + 2 * lax.axis_index("y") + lax.axis_index("c")


def _tie(a, order_after, name):
    na = len(order_after)

    def body(*refs):
        del refs

    return pl.pallas_call(
        body, name=name, in_specs=[pl.BlockSpec(memory_space=pl.ANY)] * (1 + na),
        out_specs=pl.BlockSpec(memory_space=pl.ANY), out_shape=jax.ShapeDtypeStruct(a.shape, a.dtype),
        input_output_aliases={0: 0},
    )(a, *order_after)


def _prep_gather(ws, order_after, name):
    me = jnp.reshape(_my_place(), (1,)).astype(jnp.int32)
    n = len(ws)
    na = len(order_after)
    shapes = [((32, wv.shape[1]), F32) if wv.shape[0] == CONV_W else (wv.shape, BF16) for wv in ws]

    def body(me_ref, *refs):
        del me_ref
        ins, outs = refs[:n], refs[n + na:]
        for wv, i_ref, o_ref in zip(ws, ins, outs):
            if wv.shape[0] == CONV_W:
                o_ref[pl.ds(0, CONV_W), :] = i_ref[...]
                o_ref[pl.ds(CONV_W, 1), :] = jnp.zeros((1, wv.shape[1]), F32)
            else:
                o_ref[...] = i_ref[...].astype(BF16)

    grid_spec = pltpu.PrefetchScalarGridSpec(
        num_scalar_prefetch=1, grid=(1,),
        in_specs=[pl.BlockSpec(wv.shape, lambda i, m: (0, 0)) for wv in ws]
        + [pl.BlockSpec(memory_space=pl.ANY)] * na,
        out_specs=[pl.BlockSpec(shp, lambda i, m: (m[0], 0)) for shp, _ in shapes])
    return pl.pallas_call(
        body, name=name, grid_spec=grid_spec,
        out_shape=[jax.ShapeDtypeStruct((NDEV * shp[0], shp[1]), dt) for shp, dt in shapes],
        compiler_params=_cp(("arbitrary",)),
    )(me, *ws, *order_after)


GATHER_A = ((1, 0), (2, 0), (4, 0), (6, 0))
GATHER_B = ((1, 2), (1, 4), (1, 6))
GATHER_DIRECT = tuple((k, 0) for k in range(1, NDEV))


def _gather_start(lands, plan, order_after, name):
    n = len(lands)
    na = len(order_after)
    npl = len(plan)

    def body(*refs):
        land_refs = refs[:n]
        send, recv = refs[n + na], refs[n + na + 1]
        token = refs[-1]
        for w in range(n):
            rows = lands[w].shape[0] // NDEV
            for p, (k, j) in enumerate(plan):
                peer, _ = _peer(k)
                _, blk = _peer(j)
                part = land_refs[w].at[pl.ds(blk * rows, rows)]
                i = w * npl + p
                pltpu.make_async_remote_copy(src_ref=part, dst_ref=part, send_sem=send.at[i], recv_sem=recv.at[i],
                                             device_id=peer, device_id_type=MESH_ID).start()
        token[...] = jnp.zeros_like(token)

    nsem = n * npl
    bufs = [pltpu.with_memory_space_constraint(a, pltpu.HBM) for a in lands]
    out = pl.pallas_call(
        body, name=name,
        in_specs=[HBM_SPEC] * n + [pl.BlockSpec(memory_space=pl.ANY)] * na,
        out_specs=[SEM_SPEC, SEM_SPEC] + [HBM_SPEC] * n + [pl.BlockSpec(memory_space=pltpu.VMEM)],
        out_shape=[pltpu.SemaphoreType.DMA((nsem,)), pltpu.SemaphoreType.DMA((nsem,))]
        + [pltpu.HBM(a.shape, a.dtype) for a in bufs] + [jax.ShapeDtypeStruct((8, 128), F32)],
        input_output_aliases={i: 2 + i for i in range(n)},
        compiler_params=pltpu.CompilerParams(has_side_effects=EFFECT),
    )(*bufs, *order_after)
    return out[0], out[1], out[2:2 + n], out[-1]


def _gather_wait(started, plan, order_after, name):
    send, recv, lands, _ = started
    n = len(lands)
    na = len(order_after)
    npl = len(plan)

    def body(*refs):
        land_refs = refs[:n]
        send_ref, recv_ref = refs[n], refs[n + 1]
        for w in range(n):
            rows = lands[w].shape[0] // NDEV
            for p, (k, j) in enumerate(plan):
                peer, _ = _peer(k)
                _, blk = _peer(j)
                part = land_refs[w].at[pl.ds(blk * rows, rows)]
                i = w * npl + p
                cp = pltpu.make_async_remote_copy(src_ref=part, dst_ref=part, send_sem=send_ref.at[i],
                                                  recv_sem=recv_ref.at[i], device_id=peer, device_id_type=MESH_ID)
                cp.wait_send()
                cp.wait_recv()

    out = pl.pallas_call(
        body, name=name,
        in_specs=[HBM_SPEC] * n + [SEM_SPEC, SEM_SPEC] + [pl.BlockSpec(memory_space=pl.ANY)] * na,
        out_specs=[HBM_SPEC] * n,
        out_shape=[pltpu.HBM(a.shape, a.dtype) for a in lands],
        input_output_aliases={i: i for i in range(n)},
        compiler_params=pltpu.CompilerParams(has_side_effects=EFFECT),
    )(*lands, send, recv, *order_after)
    return list(out)


def _copy_ends(kind, src, land, me, plin, k):
    if kind == "scatter":
        rows = src.shape[0] // NDEV
        return src.at[pl.ds(plin * rows, rows)], land.at[k - 1]
    return src, land.at[me]


def _landing(kind, src):
    me = _my_place()
    if kind == "scatter":
        return lax.empty((NDEV - 1, src.shape[0] // NDEV) + src.shape[1:], src.dtype)
    land = lax.empty((NDEV,) + src.shape, src.dtype)
    return lax.dynamic_update_slice(land, src[None], (me,) + (0,) * src.ndim)


def _send_start(kinds, srcs, order_after, name):
    n = len(srcs)
    lands = [_landing(kd, s) for kd, s in zip(kinds, srcs)]
    na = len(order_after)

    def body(*refs):
        src_refs, land_refs = refs[:n], refs[n:2 * n]
        send, recv = refs[2 * n + na], refs[2 * n + na + 1]
        token = refs[-1]
        _, me = _peer(0)
        for w in range(n):
            for k in range(1, NDEV):
                peer, plin = _peer(k)
                s, d = _copy_ends(kinds[w], src_refs[w], land_refs[w], me, plin, k)
                i = w * (NDEV - 1) + k - 1
                pltpu.make_async_remote_copy(src_ref=s, dst_ref=d, send_sem=send.at[i], recv_sem=recv.at[i],
                                             device_id=peer, device_id_type=MESH_ID).start()
        token[...] = jnp.zeros_like(token)

    nsem = n * (NDEV - 1)
    bufs = [pltpu.with_memory_space_constraint(a, pltpu.HBM) for a in list(srcs) + lands]
    out = pl.pallas_call(
        body, name=name,
        in_specs=[HBM_SPEC] * (2 * n) + [pl.BlockSpec(memory_space=pl.ANY)] * na,
        out_specs=[SEM_SPEC, SEM_SPEC] + [HBM_SPEC] * (2 * n) + [pl.BlockSpec(memory_space=pltpu.VMEM)],
        out_shape=[pltpu.SemaphoreType.DMA((nsem,)), pltpu.SemaphoreType.DMA((nsem,))]
        + [pltpu.HBM(a.shape, a.dtype) for a in bufs] + [jax.ShapeDtypeStruct((8, 128), F32)],
        input_output_aliases={i: 2 + i for i in range(2 * n)},
        compiler_params=pltpu.CompilerParams(has_side_effects=EFFECT),
    )(*bufs, *order_after)
    return out[0], out[1], out[2:2 + n], out[2 + n:2 + 2 * n], out[-1]


def _send_wait(kinds, started, order_after, name):
    send, recv, srcs, lands, _ = started
    n = len(srcs)
    na = len(order_after)

    def body(*refs):
        src_refs, land_refs = refs[:n], refs[n:2 * n]
        send_ref, recv_ref = refs[2 * n], refs[2 * n + 1]
        _, me = _peer(0)
        for w in range(n):
            for k in range(1, NDEV):
                peer, plin = _peer(k)
                s, d = _copy_ends(kinds[w], src_refs[w], land_refs[w], me, plin, k)
                i = w * (NDEV - 1) + k - 1
                cp = pltpu.make_async_remote_copy(src_ref=s, dst_ref=d, send_sem=send_ref.at[i],
                                                  recv_sem=recv_ref.at[i], device_id=peer, device_id_type=MESH_ID)
                cp.wait_send()
                cp.wait_recv()

    bufs = list(srcs) + list(lands)
    out = pl.pallas_call(
        body, name=name,
        in_specs=[HBM_SPEC] * (2 * n) + [SEM_SPEC, SEM_SPEC] + [pl.BlockSpec(memory_space=pl.ANY)] * na,
        out_specs=[HBM_SPEC] * (2 * n),
        out_shape=[pltpu.HBM(a.shape, a.dtype) for a in bufs],
        input_output_aliases={i: i for i in range(2 * n)},
        compiler_params=pltpu.CompilerParams(has_side_effects=EFFECT),
    )(*bufs, send, recv, *order_after)
    return out[:n], out[n:]


def _gsum(own, land, name):
    rows, cols = own.shape
    tr = rows // 2 if rows * cols > 512 * 1024 and rows % 32 == 0 else rows

    def body(own_ref, l_ref, o_ref):
        tot = own_ref[...].astype(F32)
        for s in range(NDEV - 1):
            tot = tot + l_ref[s].astype(F32)
        o_ref[...] = tot

    return pl.pallas_call(
        body, name=name, grid=(rows // tr,),
        in_specs=[pl.BlockSpec((tr, cols), lambda i: (i, 0)),
                  pl.BlockSpec((NDEV - 1, tr, cols), lambda i: (0, i, 0))],
        out_specs=pl.BlockSpec((tr, cols), lambda i: (i, 0)),
        out_shape=jax.ShapeDtypeStruct((rows, cols), F32),
        compiler_params=_cp(("parallel",)),
    )(own, land)


def _adamw_math(w, g, m, v):
    m2 = B1 * m + (1.0 - B1) * g
    v2 = B2 * v + (1.0 - B2) * (g * g)
    m_hat = m2 / (1.0 - B1 ** STEP)
    v_hat = v2 / (1.0 - B2 ** STEP)
    delta = -LR * (m_hat / (jnp.sqrt(v_hat) + AEPS) + WD * w)
    return delta, m2, v2


def _adamw(w, g, m, v, name):
    rows, cols = w.shape
    tr = 256 if rows % 256 == 0 and rows > 256 else rows

    def body(w_ref, g_ref, m_ref, v_ref, d_ref, mo_ref, vo_ref):
        d, m2, v2 = _adamw_math(w_ref[...], g_ref[...], m_ref[...], v_ref[...])
        d_ref[...] = d
        mo_ref[...] = m2
        vo_ref[...] = v2

    blk = pl.BlockSpec((tr, cols), lambda i: (i, 0))
    return pl.pallas_call(
        body, name=name, grid=(rows // tr,), in_specs=[blk] * 4, out_specs=[blk] * 3,
        out_shape=[jax.ShapeDtypeStruct((rows, cols), F32)] * 3,
        compiler_params=_cp(("parallel",)),
    )(w, g, m, v)


UPD_TC = 256


def _update(src, land, w, m, v, name):
    rows, cols = land.shape[1:]
    tc = min(UPD_TC if rows > 512 else 2 * UPD_TC, cols)
    me = jnp.reshape(_my_place(), (1,)).astype(jnp.int32)

    def body(me_ref, own_ref, l_ref, w_ref, m_ref, v_ref, g_ref, d_ref, mo_ref, vo_ref):
        del me_ref
        g = own_ref[...].astype(F32)
        for s in range(NDEV - 1):
            g = g + l_ref[s].astype(F32)
        g_ref[...] = g
        d, m2, v2 = _adamw_math(w_ref[...], g, m_ref[...], v_ref[...])
        d_ref[...] = d
        mo_ref[...] = m2
        vo_ref[...] = v2

    wblk = pl.BlockSpec((rows, tc), lambda j, p: (0, j))
    grid_spec = pltpu.PrefetchScalarGridSpec(
        num_scalar_prefetch=1, grid=(cols // tc,),
        in_specs=[pl.BlockSpec((rows, tc), lambda j, p: (p[0], j)),
                  pl.BlockSpec((NDEV - 1, rows, tc), lambda j, p: (0, 0, j)), wblk, wblk, wblk],
        out_specs=[wblk] * 4)
    return pl.pallas_call(
        body, name=name, grid_spec=grid_spec, out_shape=[jax.ShapeDtypeStruct((rows, cols), F32)] * 4,
        compiler_params=_cp(("parallel",)),
    )(me, src, land, w, m, v)


def _small_update(vland, w8, m8, v8, name):
    def body(l_ref, w_ref, m_ref, v_ref, g_ref, d_ref, mo_ref, vo_ref):
        g = l_ref[0]
        for s in range(1, NDEV):
            g = g + l_ref[s]
        g_ref[...] = g
        d, m2, v2 = _adamw_math(w_ref[...], g, m_ref[...], v_ref[...])
        d_ref[...] = d
        mo_ref[...] = m2
        vo_ref[...] = v2

    return pl.pallas_call(
        body, name=name, out_shape=[jax.ShapeDtypeStruct((8, D), F32)] * 4,
        compiler_params=_cp(None),
    )(vland, w8, m8, v8)


def kernel(x, ffn1_norm, ffn1_w_gate, ffn1_w_up, ffn1_w_down, mix_norm, w_in, conv_dw_kernel, conv_dw_bias, conv_ln_gain, conv_ln_bias, conv_w_out, attn_w_out, w_o, ffn2_norm, ffn2_w_gate, ffn2_w_up, ffn2_w_down, final_norm, loss_target, m_ffn1_norm, m_ffn1_w_gate, m_ffn1_w_up, m_ffn1_w_down, m_mix_norm, m_w_in, m_conv_dw_kernel, m_conv_dw_bias, m_conv_ln_gain, m_conv_ln_bias, m_conv_w_out, m_attn_w_out, m_w_o, m_ffn2_norm, m_ffn2_w_gate, m_ffn2_w_up, m_ffn2_w_down, m_final_norm, v_ffn1_norm, v_ffn1_w_gate, v_ffn1_w_up, v_ffn1_w_down, v_mix_norm, v_w_in, v_conv_dw_kernel, v_conv_dw_bias, v_conv_ln_gain, v_conv_ln_bias, v_conv_w_out, v_attn_w_out, v_w_o, v_ffn2_norm, v_ffn2_w_gate, v_ffn2_w_up, v_ffn2_w_down, v_final_norm):
    names = ["ffn1_norm", "ffn1_w_gate", "ffn1_w_up", "ffn1_w_down", "mix_norm", "w_in", "conv_dw_kernel",
             "conv_dw_bias", "conv_ln_gain", "conv_ln_bias", "conv_w_out", "attn_w_out", "w_o", "ffn2_norm",
             "ffn2_w_gate", "ffn2_w_up", "ffn2_w_down", "final_norm"]
    w = dict(ffn1_norm=ffn1_norm, ffn1_w_gate=ffn1_w_gate, ffn1_w_up=ffn1_w_up, ffn1_w_down=ffn1_w_down, mix_norm=mix_norm, w_in=w_in, conv_dw_kernel=conv_dw_kernel, conv_dw_bias=conv_dw_bias, conv_ln_gain=conv_ln_gain, conv_ln_bias=conv_ln_bias, conv_w_out=conv_w_out, attn_w_out=attn_w_out, w_o=w_o, ffn2_norm=ffn2_norm, ffn2_w_gate=ffn2_w_gate, ffn2_w_up=ffn2_w_up, ffn2_w_down=ffn2_w_down, final_norm=final_norm)
    mo = dict(ffn1_norm=m_ffn1_norm, ffn1_w_gate=m_ffn1_w_gate, ffn1_w_up=m_ffn1_w_up, ffn1_w_down=m_ffn1_w_down, mix_norm=m_mix_norm, w_in=m_w_in, conv_dw_kernel=m_conv_dw_kernel, conv_dw_bias=m_conv_dw_bias, conv_ln_gain=m_conv_ln_gain, conv_ln_bias=m_conv_ln_bias, conv_w_out=m_conv_w_out, attn_w_out=m_attn_w_out, w_o=m_w_o, ffn2_norm=m_ffn2_norm, ffn2_w_gate=m_ffn2_w_gate, ffn2_w_up=m_ffn2_w_up, ffn2_w_down=m_ffn2_w_down, final_norm=m_final_norm)
    vo = dict(ffn1_norm=v_ffn1_norm, ffn1_w_gate=v_ffn1_w_gate, ffn1_w_up=v_ffn1_w_up, ffn1_w_down=v_ffn1_w_down, mix_norm=v_mix_norm, w_in=v_w_in, conv_dw_kernel=v_conv_dw_kernel, conv_dw_bias=v_conv_dw_bias, conv_ln_gain=v_conv_ln_gain, conv_ln_bias=v_conv_ln_bias, conv_w_out=v_conv_w_out, attn_w_out=v_attn_w_out, w_o=v_w_o, ffn2_norm=v_ffn2_norm, ffn2_w_gate=v_ffn2_w_gate, ffn2_w_up=v_ffn2_w_up, ffn2_w_down=v_ffn2_w_down, final_norm=v_final_norm)
    col_sharded = ("ffn1_w_gate", "ffn1_w_up", "w_in", "attn_w_out", "ffn2_w_gate", "ffn2_w_up")
    row_sharded = ("ffn1_w_down", "conv_w_out", "w_o", "ffn2_w_down")
    small = ("ffn1_norm", "mix_norm", "ffn2_norm", "final_norm", "conv_dw_bias", "conv_ln_gain", "conv_ln_bias")

    def landing_view(a, n):
        return jnp.transpose(a[0]) if n in col_sharded else a[0]

    def own_view(a, n):
        return jnp.transpose(a)[None] if n in col_sharded else a[None]

    ag_groups = (("ffn1_w_gate", "ffn1_w_up", "ffn1_w_down"),
                 ("w_in", "attn_w_out", "conv_w_out", "w_o", "conv_dw_kernel"),
                 ("ffn2_w_gate", "ffn2_w_up", "ffn2_w_down"))
    ag, order = [], []
    for gi, grp in enumerate(ag_groups):
        lands = _prep_gather([landing_view(w[n], n) for n in grp], order, f"gather_prep{gi}")
        st = _gather_start(lands, GATHER_DIRECT if gi == 2 else GATHER_A, [], f"gather_a_start{gi}")
        ag.append(st)
        order = [st[3]]

    def chips_in(gi, after):
        lands = _gather_wait(ag[gi], GATHER_A, after, f"gather_a_wait{gi}")
        return _gather_start(lands, GATHER_B, [], f"gather_b_start{gi}")

    def all_in(gi, st, after):
        return _gather_wait(st, GATHER_B, after, f"gather_b_wait{gi}")

    x0 = x[0]
    tgt = loss_target[0]
    gf = final_norm.reshape(1, D)

    wg1, wu1, wd1 = all_in(0, chips_in(0, [ag[2][3]]), [])
    x1, gg1, uu1, h2p = _ffn_fwd(x0, ffn1_norm, wg1, wu1, wd1, "ffn1_fwd", next_gain=mix_norm)
    h2 = h2p[0]
    win_t, wa_t, wc, wo, kern_blocks = all_in(1, chips_in(1, [x1]), [])
    kern = kern_blocks.reshape(NDEV, 32, D // NDEV).transpose(1, 0, 2).reshape(32, D)
    ptm = min(T, 2048)
    ab = _mm(h2, win_t, mode="nt", m=T, n=2 * D, k=D, tm=ptm, tn=512, tk=D, out_dtype=BF16, name="proj_conv")
    qkv = []
    for gi in range(len(GROUPS)):
        qkv.append(_mm(h2p[gi], win_t, mode="nt", m=T, n=3 * AW, k=D, tm=ptm, tn=AW, tk=D, out_dtype=BF16,
                       b_map=lambda i, j, kk, gi=gi: (4 + gi + 3 * j, 0), name=f"proj_qkv{gi}"))
    z1, z3b, gates = _conv_fwd(ab, kern, conv_dw_bias, conv_ln_gain, conv_ln_bias, "conv_fwd",
                               guest=(h2, win_t, 13, 4, 512))
    outs, lses = [], []
    for gi, (_, dil) in enumerate(GROUPS):
        o, l = _attn_fwd(qkv[gi], gi, f"attn_fwd{gi}")
        outs.append(o)
        lses.append(l)
    attnb, lse = _merge(outs, lses, "attn_merge")
    x2, yc, ya, mixedb = _mix_out(z3b, attnb, gates, wc, wa_t, wo, x1, "mix_out_fwd")
    wg2, wu2, wd2 = _gather_wait(ag[2], GATHER_DIRECT, [x2], "gather_a_wait2")
    gg2, uu2, dx3, dgf, loss_part = _ffn_fwd(x2, ffn2_norm, wg2, wu2, wd2, "ffn2_fwd", loss_of=(gf, tgt))

    dx2, dg3, dgb, dub, actb, hb, dob = _ffn_bwd(x2, ffn2_norm, gg2, uu2, dx3, wg2, wu2, wd2, "ffn2_bwd")
    grads = {}
    grads["ffn2_w_down"] = _wgrad(actb, dob, FF, D, "ffn2_dwd")
    rs_groups = [("ffn2_w_gate", "ffn2_w_up", "ffn2_w_down"),
                 ("attn_w_out", "conv_w_out", "w_o", "conv_dw_kernel"),
                 ("w_in",),
                 ("ffn1_w_gate",), ("ffn1_w_up",), ("ffn1_w_down",), ()]
    last = len(rs_groups) - 1
    rs = []

    dgates, dycb, dyab, dx2b, dz3, dattnb, delta = _mix_out_bwd(dx2, gates, yc, ya, attnb, wc, wa_t, wo, "mix_out_bwd")
    grads["w_o"] = _wgrad(mixedb, dx2b, D, D, "dw_o")
    grads["conv_w_out"] = _wgrad(z3b, dycb, D, D, "dw_conv_out")
    grads["attn_w_out"] = _wgrad(dyab, attnb, D, AW, "dw_attn_out")
    dab, dkern, dvec, grads["ffn2_w_gate"], grads["ffn2_w_up"] = _conv_bwd(
        dz3, z1, ab, kern, conv_ln_gain, conv_ln_bias, "conv_bwd", guest_lhs=(dgb, dub), guest_rhs=hb)
    grads["conv_dw_kernel"] = dkern.reshape(32, NDEV, D // NDEV).transpose(1, 0, 2).reshape(NDEV * 32, D // NDEV)
    rs.append(_send_start(["scatter"] * 3, [grads[n] for n in rs_groups[0]], [], "scatter_start0"))
    rs.append(_send_start(["scatter"] * 4, [grads[n] for n in rs_groups[1]], [rs[0][4]], "scatter_start1"))
    dattnb = [_tie(a, [rs[1][4]], f"tie_after_scatter1_{i}") for i, a in enumerate(dattnb)]

    dqkv = []
    for gi, (_, dil) in enumerate(GROUPS):
        dq3 = _attn_bwd(qkv[gi], dattnb[gi], lse[gi], delta[gi], gi, f"attn_bwd{gi}")
        dqkv.append(dq3.reshape(3 * T, AW))

    wtk = min(T, 2048)
    dwin = _mm(dab, h2, mode="tn", m=2 * D, n=D, k=T, tm=2 * D, tn=D, tk=wtk, out_dtype=BF16, out_rows=IN_W,
               name="dw_in_conv")
    dwin = _mm(dgates, h2, mode="tn", m=2 * D, n=D, k=T, tm=512, tn=D, tk=wtk, out_dtype=BF16, out_rows=IN_W,
               o_map=lambda i, j, kk: (13 + i, 0), passthru=dwin, name="dw_in_gates")
    for gi in range(3):
        dwin = _mm(dqkv[gi], h2p[gi], mode="tn", m=3 * AW, n=D, k=T, tm=AW, tn=D, tk=wtk, out_dtype=BF16,
                   out_rows=IN_W, a_map=lambda i, j, kk: (i * (T // wtk) + kk, 0),
                   o_map=lambda i, j, kk, gi=gi: (4 + gi + 3 * i, 0), passthru=dwin, name=f"dw_in_qkv{gi}")
    grads["w_in"] = dwin
    rs.append(_send_start(["scatter"], [dwin], [rs[1][4]], "scatter_start2"))
    dab = _tie(dab, [rs[2][4]], "tie_after_scatter2")

    nrow = T // 1024
    dh = _mm(dab, win_t, mode="nn", m=T, n=D, k=2 * D, tm=1024, tn=D, tk=2 * D, out_dtype=F32, name="dproj_conv")
    dh = _mm(dgates, win_t[IN_W - 2 * D:], mode="nn", m=T, n=D, k=2 * D, tm=1024, tn=D, tk=2 * D, out_dtype=F32,
             init=dh, name="dproj_gates")
    dhs = []
    for gi, (_, dil) in enumerate(GROUPS):
        part = _mm(dqkv[gi], win_t, mode="nn", m=T, n=D, k=3 * AW, tm=1024, tn=D, tk=AW,
                   out_dtype=F32 if gi == 0 else BF16,
                   a_map=lambda i, j, kk: (kk * nrow + i, 0), b_map=lambda i, j, kk, gi=gi: (4 + gi + 3 * kk, 0),
                   init=dh if gi == 0 else None, name=f"dproj_qkv{gi}")
        dhs.append(part)
    dx1, dg2 = _rms_bwd(x1, mix_norm, dhs, dx2, "mix_norm_bwd")

    dgb, dub, actb, hb, dob = _ffn_bwd_pre(x0, ffn1_norm, gg1, uu1, dx1, wd1, "ffn1_bwd_pre")
    grads["ffn1_w_gate"] = _wgrad(dgb, hb, FF, D, "ffn1_dwg")
    rs.append(_send_start(["scatter"], [grads["ffn1_w_gate"]], [rs[2][4]], "scatter_start3"))
    hb = _tie(hb, [rs[3][4]], "tie_after_scatter3")
    grads["ffn1_w_up"] = _wgrad(dub, hb, FF, D, "ffn1_dwu")
    rs.append(_send_start(["scatter"], [grads["ffn1_w_up"]], [rs[3][4]], "scatter_start4"))
    dob = _tie(dob, [rs[4][4]], "tie_after_scatter4")
    grads["ffn1_w_down"] = _wgrad(actb, dob, FF, D, "ffn1_dwd")
    rs.append(_send_start(["scatter"], [grads["ffn1_w_down"]], [rs[4][4]], "scatter_start5"))
    dgb = _tie(dgb, [rs[5][4]], "tie_after_scatter5")
    dx0, dg1 = _ffn_bwd_dx(x0, ffn1_norm, dgb, dub, dx1, wg1, wu1, "ffn1_bwd_dx")
    vec = jnp.concatenate([dg1, dg2, dg3, dgf, dvec[0:3], jnp.broadcast_to(loss_part[:, :1], (1, D))], axis=0)
    rs.append(_send_start(["bcast"], [vec], [rs[5][4]], "scatter_start6"))

    g_out, d_out, m_out, v_out = {}, {}, {}, {}
    me = _my_place()
    after = [rs[last][4]]
    for gi, grp in enumerate(rs_groups):
        kinds = ["scatter"] * len(grp) + (["bcast"] if gi == last else [])
        srcs, lands = _send_wait(kinds, rs[gi], after, f"scatter_wait{gi}")
        for n, src, land in zip(grp, srcs, lands):
            if n == "conv_dw_kernel":
                rows = src.shape[0] // NDEV
                own = lax.dynamic_slice(src, (me * rows, 0), (rows, src.shape[1]))
                g = _gsum(own, land, f"gsum_{n}")[:CONV_W]
                d, m2, v2 = _adamw(w[n][0], g, mo[n][0], vo[n][0], f"adamw_{n}")
                after = [d]
                g, d, m2, v2 = g[None], d[None], m2[None], v2[None]
            else:
                res = _update(src, land, landing_view(w[n], n), landing_view(mo[n], n), landing_view(vo[n], n),
                              f"update_{n}")
                after = [res[1]]
                g, d, m2, v2 = (own_view(a, n) for a in res)
            g_out[n], d_out[n], m_out[n], v_out[n] = g, d, m2, v2
    vland = lands[-1]

    def rows8(src):
        return jnp.concatenate([src[n].reshape(1, D) for n in small] + [jnp.ones((1, D), F32)], axis=0)

    g8, d8, m8, v8 = _small_update(vland, rows8(w), rows8(mo), rows8(vo), "small_update")
    for r, n in enumerate(small):
        shp = w[n].shape
        g_out[n], d_out[n], m_out[n], v_out[n] = (a[r].reshape(shp) for a in (g8, d8, m8, v8))
    loss = g8[7, 0]

    return (loss, dx0[None], *[g_out[n] for n in names], *[d_out[n] for n in names],
            *[m_out[n] for n in names], *[v_out[n] for n in names])
```

```python
import numpy as np
import jax
import jax.numpy as jnp
from jax import lax
from jax.experimental import pallas as pl
from jax.experimental.pallas import tpu as pltpu

F32 = jnp.float32
BF16 = jnp.bfloat16

T = 4096
D = 1024
FF = 2816
NDEV = 8
CONV_W = 31
HEAD = 128
BLK = 128
GROUPS = ((128, 1), (512, 4), (2048, 16))
NHG = 4
AW = NHG * HEAD
IN_W = 2 * D + 3 * 3 * AW + 2 * D
EPS = 1e-6
B1, B2, LR, AEPS, WD, STEP = 0.9, 0.999, 0.001, 1e-08, 0.01, 10
NEG = -1e30
VMEM_LIMIT = 56 * 1024 * 1024
MESH_ID = pl.DeviceIdType.MESH

NT = (((1,), (1,)), ((), ()))
NN = (((1,), (0,)), ((), ()))
TN = (((0,), (0,)), ((), ()))
_DIMS = {"nn": NN, "nt": NT, "tn": TN}


def _cp(sem=None):
    return pltpu.CompilerParams(dimension_semantics=sem, vmem_limit_bytes=VMEM_LIMIT)


def _sig(v):
    return 1.0 / (1.0 + jnp.exp(-v))


def _dot(a, b, dims):
    return lax.dot_general(a, b, dims, preferred_element_type=F32)


def _const_spec(shape):
    nd = len(shape)
    return pl.BlockSpec(shape, lambda *_: (0,) * nd)


def _mm(a, b, *, mode, m, n, k, tm, tn, tk, out_dtype, name, a_map=None, b_map=None,
        o_map=None, out_rows=None, init=None, passthru=None):
    gi, gj, gk = m // tm, n // tn, k // tk
    assert gi * tm == m and gj * tn == n and gk * tk == k, (name, m, n, k, tm, tn, tk)
    if mode == "nn":
        a_blk, b_blk = (tm, tk), (tk, tn)
        da, db = (lambda i, j, kk: (i, kk)), (lambda i, j, kk: (kk, j))
    elif mode == "nt":
        a_blk, b_blk = (tm, tk), (tn, tk)
        da, db = (lambda i, j, kk: (i, kk)), (lambda i, j, kk: (j, kk))
    else:
        a_blk, b_blk = (tk, tm), (tk, tn)
        da, db = (lambda i, j, kk: (kk, i)), (lambda i, j, kk: (kk, j))
    a_map = a_map or da
    b_map = b_map or db
    o_map = o_map or (lambda i, j, kk: (i, j))
    dims = _DIMS[mode]
    extra = init if init is not None else passthru
    out_rows = out_rows or m

    def body(*refs):
        if init is not None:
            a_ref, b_ref, i_ref, o_ref = refs[:4]
        elif passthru is not None:
            a_ref, b_ref, _, o_ref = refs[:4]
        else:
            a_ref, b_ref, o_ref = refs[:3]
        if gk == 1:
            prod = _dot(a_ref[...], b_ref[...], dims)
            if init is not None:
                prod = prod + i_ref[...].astype(F32)
            o_ref[...] = prod.astype(out_dtype)
            return
        acc = refs[-1]
        kk = pl.program_id(2)

        @pl.when(kk == 0)
        def _():
            if init is not None:
                acc[...] = i_ref[...].astype(F32)
            else:
                acc[...] = jnp.zeros_like(acc)

        acc[...] += _dot(a_ref[...], b_ref[...], dims)

        @pl.when(kk == gk - 1)
        def _():
            o_ref[...] = acc[...].astype(out_dtype)

    in_specs = [pl.BlockSpec(a_blk, a_map), pl.BlockSpec(b_blk, b_map)]
    args = [a, b]
    aliases = {}
    if init is not None:
        in_specs.append(pl.BlockSpec((tm, tn), o_map))
        args.append(init)
        aliases = {2: 0}
    elif passthru is not None:
        in_specs.append(pl.BlockSpec(memory_space=pl.ANY))
        args.append(passthru)
        aliases = {2: 0}
    out_dt = extra.dtype if extra is not None else out_dtype
    assert out_dt == out_dtype
    return pl.pallas_call(
        body, name=name, grid=(gi, gj, gk),
        in_specs=in_specs, out_specs=pl.BlockSpec((tm, tn), o_map),
        out_shape=jax.ShapeDtypeStruct((out_rows, n), out_dtype),
        scratch_shapes=[pltpu.VMEM((tm, tn), F32)] if gk > 1 else [],
        input_output_aliases=aliases,
        compiler_params=_cp(("parallel", "parallel", "arbitrary")),
    )(*args)


def _ffn_fwd(x, g, wg_t, wu_t, wd, name, next_gain=None, loss_of=None):
    tm, fc = PERM_TM, 256
    nc = FF // fc
    n_in = 5 + (1 if next_gain is not None else 0) + (2 if loss_of is not None else 0)

    def body(*refs):
        x_ref, g_ref, wg_ref, wu_ref, wd_ref = refs[:5]
        extra_in, outs = refs[5:n_in], refs[n_in:]
        act_ref = outs[-1]
        xv = x_ref[...]
        r = lax.rsqrt(jnp.mean(xv * xv, axis=-1, keepdims=True) + EPS)
        h = (xv * r * g_ref[...]).astype(BF16)
        gg_ref, uu_ref = (outs[0], outs[1]) if loss_of is not None else (outs[1], outs[2])
        for c in range(nc):
            sl = pl.ds(c * fc, fc)
            gg = _dot(h, wg_ref[sl, :], NT)
            uu = _dot(h, wu_ref[sl, :], NT)
            gg_ref[:, sl] = gg.astype(BF16)
            uu_ref[:, sl] = uu.astype(BF16)
            act_ref[:, sl] = (gg * _sig(gg) * uu).astype(BF16)
        y = xv + 0.5 * _dot(act_ref[...], wd_ref[...], NN)
        if loss_of is not None:
            _final_math(y, extra_in[0][...], extra_in[1][...], outs[2], outs[3], outs[4], pl.program_id(0))
            return
        outs[0][...] = y
        if next_gain is not None:
            tile = outs[-2]
            r2 = lax.rsqrt(jnp.mean(y * y, axis=-1, keepdims=True) + EPS)
            hv = y * r2 * extra_in[0][...]
            outs[3][...] = hv.astype(BF16)
            _put_tile(tile, hv)
            for dil, p_ref in zip(DILS, outs[4:4 + len(DILS)]):
                _store_perm(p_ref, tile, dil)

    wspec = pl.BlockSpec((FF, D), lambda i: (0, 0), pipeline_mode=pl.Buffered(1))
    row_d = pl.BlockSpec((tm, D), lambda i: (i, 0))
    row_f = pl.BlockSpec((tm, FF), lambda i: (i, 0))
    in_specs = [row_d, _const_spec((1, D)), wspec, wspec, wspec]
    args = [x, g, wg_t, wu_t, wd]
    f_shape = jax.ShapeDtypeStruct((T, FF), BF16)
    scratch = [pltpu.VMEM((tm, FF), BF16)]
    if loss_of is not None:
        in_specs += [_const_spec((1, D)), row_d]
        args += list(loss_of)
        out_specs = [row_f, row_f, row_d, _const_spec((1, D)), _const_spec((1, 128))]
        out_shape = [f_shape, f_shape, jax.ShapeDtypeStruct((T, D), F32), jax.ShapeDtypeStruct((1, D), F32),
                     jax.ShapeDtypeStruct((1, 128), F32)]
    else:
        out_specs = [row_d, row_f, row_f]
        out_shape = [jax.ShapeDtypeStruct((T, D), F32), f_shape, f_shape]
        if next_gain is not None:
            in_specs.append(_const_spec((1, D)))
            args.append(next_gain)
            out_specs += [row_d] + [_perm_spec(d, D) for d in DILS]
            out_shape += [jax.ShapeDtypeStruct((T, D), BF16)] + [_perm_shape(d, D, BF16) for d in DILS]
            scratch = [_tile_scratch(D)] + scratch
    out = pl.pallas_call(
        body, name=name, grid=(T // tm,), in_specs=in_specs, out_specs=out_specs, out_shape=out_shape,
        scratch_shapes=scratch,
        compiler_params=_cp(("arbitrary",) if loss_of is not None else ("parallel",)),
    )(*args)
    if next_gain is not None:
        return out[0], out[1], out[2], [out[3]] + [o.reshape(T, D) for o in out[4:]]
    return tuple(out)


def _ffn_bwd(x, g, gg_all, uu_all, dout, wg_t, wu_t, wd, name):
    tm, fc = 256, 256
    nc = FF // fc

    def body(x_ref, g_ref, gg_ref, uu_ref, do_ref, wg_ref, wu_ref, wd_ref,
             dx_ref, dgam_ref, dg_ref, du_ref, act_ref, h_ref, db_ref):
        i = pl.program_id(0)
        xv = x_ref[...]
        r = lax.rsqrt(jnp.mean(xv * xv, axis=-1, keepdims=True) + EPS)
        xhat = xv * r
        gam = g_ref[...]
        h_ref[...] = (xhat * gam).astype(BF16)
        dov = do_ref[...]
        dbv = (0.5 * dov).astype(BF16)
        db_ref[...] = dbv
        for c in range(nc):
            sl = pl.ds(c * fc, fc)
            da = _dot(dbv, wd_ref[sl, :], NT)
            gg = gg_ref[:, sl].astype(F32)
            uu = uu_ref[:, sl].astype(F32)
            s = _sig(gg)
            si = gg * s
            dgv = (da * uu * (s * (1.0 + gg * (1.0 - s)))).astype(BF16)
            duv = (da * si).astype(BF16)
            dg_ref[:, sl] = dgv
            du_ref[:, sl] = duv
            act_ref[:, sl] = (si * uu).astype(BF16)
        dh = _dot(dg_ref[...], wg_ref[...], NN) + _dot(du_ref[...], wu_ref[...], NN)

        @pl.when(i == 0)
        def _():
            dgam_ref[...] = jnp.zeros_like(dgam_ref)

        dgam_ref[...] += jnp.sum(dh * xhat, axis=0, keepdims=True)
        dxh = dh * gam
        dx_ref[...] = dov + r * (dxh - xhat * jnp.mean(dxh * xhat, axis=-1, keepdims=True))

    wspec = pl.BlockSpec((FF, D), lambda i: (0, 0), pipeline_mode=pl.Buffered(1))
    row_d = pl.BlockSpec((tm, D), lambda i: (i, 0))
    row_f = pl.BlockSpec((tm, FF), lambda i: (i, 0))
    return pl.pallas_call(
        body, name=name, grid=(T // tm,),
        in_specs=[row_d, _const_spec((1, D)), row_f, row_f, row_d, wspec, wspec, wspec],
        out_specs=[row_d, _const_spec((1, D)), row_f, row_f, row_f, row_d, row_d],
        out_shape=[jax.ShapeDtypeStruct((T, D), F32), jax.ShapeDtypeStruct((1, D), F32),
                   jax.ShapeDtypeStruct((T, FF), BF16), jax.ShapeDtypeStruct((T, FF), BF16),
                   jax.ShapeDtypeStruct((T, FF), BF16), jax.ShapeDtypeStruct((T, D), BF16),
                   jax.ShapeDtypeStruct((T, D), BF16)],
        compiler_params=_cp(("arbitrary",)),
    )(x, g, gg_all, uu_all, dout, wg_t, wu_t, wd)


def _ffn_bwd_pre(x, g, gg_all, uu_all, dout, wd, name):
    tm, fc = 512, 256
    nc = FF // fc

    def body(x_ref, g_ref, gg_ref, uu_ref, do_ref, wd_ref, dg_ref, du_ref, act_ref, h_ref, db_ref):
        xv = x_ref[...]
        r = lax.rsqrt(jnp.mean(xv * xv, axis=-1, keepdims=True) + EPS)
        h_ref[...] = (xv * r * g_ref[...]).astype(BF16)
        dbv = (0.5 * do_ref[...]).astype(BF16)
        db_ref[...] = dbv
        for c in range(nc):
            sl = pl.ds(c * fc, fc)
            da = _dot(dbv, wd_ref[sl, :], NT)
            gg = gg_ref[:, sl].astype(F32)
            uu = uu_ref[:, sl].astype(F32)
            s = _sig(gg)
            si = gg * s
            dg_ref[:, sl] = (da * uu * (s * (1.0 + gg * (1.0 - s)))).astype(BF16)
            du_ref[:, sl] = (da * si).astype(BF16)
            act_ref[:, sl] = (si * uu).astype(BF16)

    wspec = pl.BlockSpec((FF, D), lambda i: (0, 0), pipeline_mode=pl.Buffered(1))
    row_d = pl.BlockSpec((tm, D), lambda i: (i, 0))
    row_f = pl.BlockSpec((tm, FF), lambda i: (i, 0))
    return pl.pallas_call(
        body, name=name, grid=(T // tm,),
        in_specs=[row_d, _const_spec((1, D)), row_f, row_f, row_d, wspec],
        out_specs=[row_f, row_f, row_f, row_d, row_d],
        out_shape=[jax.ShapeDtypeStruct((T, FF), BF16), jax.ShapeDtypeStruct((T, FF), BF16),
                   jax.ShapeDtypeStruct((T, FF), BF16), jax.ShapeDtypeStruct((T, D), BF16),
                   jax.ShapeDtypeStruct((T, D), BF16)],
        compiler_params=_cp(("parallel",)),
    )(x, g, gg_all, uu_all, dout, wd)


def _ffn_bwd_dx(x, g, dgb, dub, dout, wg_t, wu_t, name):
    tm = 512

    def body(x_ref, g_ref, dg_ref, du_ref, do_ref, wg_ref, wu_ref, dx_ref, dgam_ref):
        i = pl.program_id(0)
        xv = x_ref[...]
        r = lax.rsqrt(jnp.mean(xv * xv, axis=-1, keepdims=True) + EPS)
        xhat = xv * r
        gam = g_ref[...]
        dh = _dot(dg_ref[...], wg_ref[...], NN) + _dot(du_ref[...], wu_ref[...], NN)

        @pl.when(i == 0)
        def _():
            dgam_ref[...] = jnp.zeros_like(dgam_ref)

        dgam_ref[...] += jnp.sum(dh * xhat, axis=0, keepdims=True)
        dxh = dh * gam
        dx_ref[...] = do_ref[...] + r * (dxh - xhat * jnp.mean(dxh * xhat, axis=-1, keepdims=True))

    wspec = pl.BlockSpec((FF, D), lambda i: (0, 0), pipeline_mode=pl.Buffered(1))
    row_d = pl.BlockSpec((tm, D), lambda i: (i, 0))
    row_f = pl.BlockSpec((tm, FF), lambda i: (i, 0))
    return pl.pallas_call(
        body, name=name, grid=(T // tm,),
        in_specs=[row_d, _const_spec((1, D)), row_f, row_f, row_d, wspec, wspec],
        out_specs=[row_d, _const_spec((1, D))],
        out_shape=[jax.ShapeDtypeStruct((T, D), F32), jax.ShapeDtypeStruct((1, D), F32)],
        compiler_params=_cp(("arbitrary",)),
    )(x, g, dgb, dub, dout, wg_t, wu_t)


def _wgrad(a, b, m, n, name):
    tm = m // 2 if m == FF else m
    return _mm(a, b, mode="tn", m=m, n=n, k=T, tm=tm, tn=n, tk=min(T, 2048), out_dtype=BF16, name=name)


PERM_TM = 512
DILS = tuple(d for _, d in GROUPS if d > 1)


def _perm_spec(dil, cols):
    return pl.BlockSpec((dil, PERM_TM // dil, cols), lambda i: (0, i, 0))


def _perm_shape(dil, cols, dtype):
    return jax.ShapeDtypeStruct((dil, T // dil, cols), dtype)


LANES = 128


def _tile_scratch(cols):
    return pltpu.VMEM((cols // LANES, PERM_TM, LANES), F32)


def _put_tile(tile, value):
    for c in range(tile.shape[0]):
        tile[c] = value[:, c * LANES:(c + 1) * LANES]


def _get_tile(tile):
    return jnp.concatenate([tile[c] for c in range(tile.shape[0])], axis=1)


def _store_perm(out_ref, tile, dil):
    for r in range(dil):
        for c in range(tile.shape[0]):
            out_ref[r, :, pl.ds(c * LANES, LANES)] = tile[c, pl.ds(r, PERM_TM // dil, stride=dil), :].astype(
                out_ref.dtype)


def _load_unperm(in_ref, tile, dil):
    for r in range(dil):
        for c in range(tile.shape[0]):
            tile[c, pl.ds(r, PERM_TM // dil, stride=dil), :] = in_ref[r, :, pl.ds(c * LANES, LANES)].astype(F32)


def _final_math(xv, gam, tgt, dx_ref, dgam_ref, loss_ref, i):
    r = lax.rsqrt(jnp.mean(xv * xv, axis=-1, keepdims=True) + EPS)
    xhat = xv * r
    err = xhat * gam - tgt
    part = 0.5 * jnp.sum(jnp.mean(err * err, axis=-1, keepdims=True), axis=0, keepdims=True)
    dy = err * (1.0 / D)

    @pl.when(i == 0)
    def _():
        dgam_ref[...] = jnp.zeros_like(dgam_ref)
        loss_ref[...] = jnp.zeros_like(loss_ref)

    dgam_ref[...] += jnp.sum(dy * xhat, axis=0, keepdims=True)
    loss_ref[...] += jnp.broadcast_to(part, loss_ref.shape)
    dxh = dy * gam
    dx_ref[...] = r * (dxh - xhat * jnp.mean(dxh * xhat, axis=-1, keepdims=True))


def _rms_bwd(x, g, dhs, dres, name):
    tm = PERM_TM
    dils = [d for _, d in GROUPS]
    nh = len(dhs)
    assert nh == len(dils)

    def body(*refs):
        x_ref, g_ref = refs[:2]
        dh_refs = refs[2:2 + nh]
        dr_ref, dx_ref, dgam_ref, tile = refs[2 + nh:]
        i = pl.program_id(0)
        xv = x_ref[...]
        r = lax.rsqrt(jnp.mean(xv * xv, axis=-1, keepdims=True) + EPS)
        xhat = xv * r
        gam = g_ref[...]
        dh = None
        for dil, ref in zip(dils, dh_refs):
            if dil == 1:
                part = ref[...]
            else:
                _load_unperm(ref, tile, dil)
                part = _get_tile(tile)
            dh = part if dh is None else dh + part

        @pl.when(i == 0)
        def _():
            dgam_ref[...] = jnp.zeros_like(dgam_ref)

        dgam_ref[...] += jnp.sum(dh * xhat, axis=0, keepdims=True)
        dxh = dh * gam
        dx_ref[...] = dr_ref[...] + r * (dxh - xhat * jnp.mean(dxh * xhat, axis=-1, keepdims=True))

    row_d = pl.BlockSpec((tm, D), lambda i: (i, 0))
    dh_specs = [row_d if d == 1 else _perm_spec(d, D) for d in dils]
    dh_args = [a if d == 1 else a.reshape(d, T // d, D) for d, a in zip(dils, dhs)]
    return pl.pallas_call(
        body, name=name, grid=(T // tm,),
        in_specs=[row_d, _const_spec((1, D))] + dh_specs + [row_d],
        out_specs=[row_d, _const_spec((1, D))],
        out_shape=[jax.ShapeDtypeStruct((T, D), F32), jax.ShapeDtypeStruct((1, D), F32)],
        scratch_shapes=[_tile_scratch(D)],
        compiler_params=_cp(("arbitrary",)),
    )(x, g, *dh_args, dres)


CONV_TM = 256
CONV_HALO = 32
CONV_RB = 16


def _glu(ab):
    ab = ab.astype(F32)
    return ab[:, :D] * _sig(ab[:, D:])


def _ln_stats(z1):
    mu = jnp.mean(z1, axis=-1, keepdims=True)
    zc = z1 - mu
    rstd = lax.rsqrt(jnp.mean(zc * zc, axis=-1, keepdims=True) + EPS)
    return zc * rstd, rstd


def _fill_shifts(zs):
    n = zs.shape[1] - 8
    for s in range(1, 8):
        zs[s, pl.ds(0, n), :] = zs[0, pl.ds(s, n), :]


def _shifted(zs, start, rows):
    q, s = divmod(start, 8)
    return zs[s, pl.ds(8 * q, rows), :]


def _conv_fwd(ab, kern, dwb, lng, lnb, name, guest=None):
    tm, hl, rb = CONV_TM, CONV_HALO, CONV_RB
    off = hl - (CONV_W - 1)
    if guest is not None:
        g_a, g_b, g_first, g_nblk, g_rows = guest

    def body(ab_ref, abh_ref, k_ref, dwb_ref, lng_ref, lnb_ref, *rest):
        if guest is not None:
            ga_ref, gb_refs, rest = rest[0], rest[1:1 + g_nblk], rest[1 + g_nblk:]
            z1_ref, z3_ref, go_ref, zs = rest
            for q, gb_ref in enumerate(gb_refs):
                go_ref[:, pl.ds(q * g_rows, g_rows)] = _dot(ga_ref[...], gb_ref[...], NT).astype(BF16)
        else:
            z1_ref, z3_ref, zs = rest
        i = pl.program_id(0)
        zs[0, pl.ds(0, hl), :] = jnp.where(i > 0, _glu(abh_ref[...]), 0.0)
        zs[0, pl.ds(hl, tm), :] = _glu(ab_ref[...])
        _fill_shifts(zs)
        for b in range(tm // rb):
            acc = jnp.zeros((rb, D), F32)
            for j in range(CONV_W):
                acc = acc + _shifted(zs, b * rb + off + j, rb) * k_ref[pl.ds(j, 1), :]
            z1 = acc + dwb_ref[...]
            z1_ref[pl.ds(b * rb, rb), :] = z1
            zn, _ = _ln_stats(z1)
            z2 = zn * lng_ref[...] + lnb_ref[...]
            z3_ref[pl.ds(b * rb, rb), :] = (z2 * _sig(z2)).astype(BF16)

    row = pl.BlockSpec((tm, D), lambda i: (i, 0))
    g_specs, g_args, g_ospecs, g_oshapes = [], [], [], []
    if guest is not None:
        kdim = g_a.shape[1]
        g_specs = [pl.BlockSpec((tm, kdim), lambda i: (i, 0))]
        g_specs += [pl.BlockSpec((g_rows, kdim), lambda i, q=q: (g_first + q, 0), pipeline_mode=pl.Buffered(1))
                    for q in range(g_nblk)]
        g_args = [g_a] + [g_b] * g_nblk
        g_ospecs = [pl.BlockSpec((tm, g_nblk * g_rows), lambda i: (i, 0))]
        g_oshapes = [jax.ShapeDtypeStruct((T, g_nblk * g_rows), BF16)]
    return pl.pallas_call(
        body, name=name, grid=(T // tm,),
        in_specs=[pl.BlockSpec((tm, 2 * D), lambda i: (i, 0)),
                  pl.BlockSpec((hl, 2 * D), lambda i: (jnp.maximum(i * (tm // hl) - 1, 0), 0)),
                  _const_spec((32, D)), _const_spec((1, D)), _const_spec((1, D)), _const_spec((1, D))] + g_specs,
        out_specs=[row, row] + g_ospecs,
        out_shape=[jax.ShapeDtypeStruct((T, D), F32), jax.ShapeDtypeStruct((T, D), BF16)] + g_oshapes,
        scratch_shapes=[pltpu.VMEM((8, hl + tm, D), F32)],
        compiler_params=_cp(("parallel",)),
    )(ab, ab, kern, dwb, lng, lnb, *g_args)


GUEST_TM = 256


def _conv_bwd(dz3, z1, ab, kern, lng, lnb, name, guest_lhs=(), guest_rhs=None):
    tm, hl, rb = CONV_TM, CONV_HALO, CONV_RB
    off = hl - (CONV_W - 1)
    nsteps = T // tm
    ng = len(guest_lhs)
    gblocks = [a.shape[1] // GUEST_TM for a in guest_lhs]
    assert all(gb <= nsteps and gb * GUEST_TM == a.shape[1] for gb, a in zip(gblocks, guest_lhs))

    def ln_bwd(dz3v, z1v, lngv, lnbv):
        zn, rstd = _ln_stats(z1v)
        z2 = zn * lngv + lnbv
        s = _sig(z2)
        dz2 = dz3v * (s * (1.0 + z2 * (1.0 - s)))
        dzn = dz2 * lngv
        dz1 = rstd * (dzn - jnp.mean(dzn, axis=-1, keepdims=True)
                      - zn * jnp.mean(dzn * zn, axis=-1, keepdims=True))
        return dz1, dz2, zn

    def body(dz3_ref, dz3h_ref, z1_ref, z1h_ref, ab_ref, abh_ref, k_ref, lng_ref, lnb_ref, *rest):
        g_in, rest = rest[:ng + (1 if ng else 0)], rest[ng + (1 if ng else 0):]
        dab_ref, dk_ref, dvec_ref = rest[:3]
        g_out, (zs, dzs) = rest[3:3 + ng], rest[3 + ng:]
        i = pl.program_id(0)
        lngv, lnbv = lng_ref[...], lnb_ref[...]

        for a_ref, o_ref, gb in zip(g_in[:ng], g_out, gblocks):
            @pl.when(i < gb)
            def _(a_ref=a_ref, o_ref=o_ref):
                o_ref[...] = _dot(a_ref[...], g_in[ng][...], TN).astype(BF16)

        @pl.when(i == 0)
        def _():
            dk_ref[...] = jnp.zeros_like(dk_ref)
            dvec_ref[...] = jnp.zeros_like(dvec_ref)

        dz1, dz2, zn = ln_bwd(dz3_ref[...].astype(F32), z1_ref[...], lngv, lnbv)
        dvec_ref[pl.ds(0, 1), :] += jnp.sum(dz1, axis=0, keepdims=True)
        dvec_ref[pl.ds(1, 1), :] += jnp.sum(dz2 * zn, axis=0, keepdims=True)
        dvec_ref[pl.ds(2, 1), :] += jnp.sum(dz2, axis=0, keepdims=True)
        dzs[0, pl.ds(0, tm), :] = dz1
        dz1h, _, _ = ln_bwd(dz3h_ref[...].astype(F32), z1h_ref[...], lngv, lnbv)
        dzs[0, pl.ds(tm, hl), :] = jnp.where(i < nsteps - 1, dz1h, 0.0)
        _fill_shifts(dzs)
        zs[0, pl.ds(0, hl), :] = jnp.where(i > 0, _glu(abh_ref[...]), 0.0)
        zs[0, pl.ds(hl, tm), :] = _glu(ab_ref[...])
        _fill_shifts(zs)

        for j in range(CONV_W):
            tot = jnp.zeros((rb, D), F32)
            for b in range(tm // rb):
                tot = tot + dzs[0, pl.ds(b * rb, rb), :] * _shifted(zs, b * rb + off + j, rb)
            dk_ref[pl.ds(j, 1), :] += jnp.sum(tot, axis=0, keepdims=True)

        for b in range(tm // rb):
            acc = jnp.zeros((rb, D), F32)
            for j in range(CONV_W):
                acc = acc + _shifted(dzs, b * rb + (CONV_W - 1) - j, rb) * k_ref[pl.ds(j, 1), :]
            av = ab_ref[pl.ds(b * rb, rb), pl.ds(0, D)].astype(F32)
            sb = _sig(ab_ref[pl.ds(b * rb, rb), pl.ds(D, D)].astype(F32))
            dab_ref[pl.ds(b * rb, rb), pl.ds(0, D)] = (acc * sb).astype(BF16)
            dab_ref[pl.ds(b * rb, rb), pl.ds(D, D)] = (acc * av * sb * (1.0 - sb)).astype(BF16)

    row = pl.BlockSpec((tm, D), lambda i: (i, 0))
    nxt = pl.BlockSpec((hl, D), lambda i: (jnp.minimum((i + 1) * (tm // hl), T // hl - 1), 0))
    g_specs, g_args, g_ospecs, g_oshapes = [], [], [], []
    for a, gb in zip(guest_lhs, gblocks):
        g_specs.append(pl.BlockSpec((T, GUEST_TM), lambda i, gb=gb: (0, jnp.minimum(i, gb - 1))))
        g_args.append(a)
        g_ospecs.append(pl.BlockSpec((GUEST_TM, guest_rhs.shape[1]), lambda i, gb=gb: (jnp.minimum(i, gb - 1), 0)))
        g_oshapes.append(jax.ShapeDtypeStruct((a.shape[1], guest_rhs.shape[1]), BF16))
    if ng:
        g_specs.append(pl.BlockSpec(guest_rhs.shape, lambda i: (0, 0), pipeline_mode=pl.Buffered(1)))
        g_args.append(guest_rhs)
    return pl.pallas_call(
        body, name=name, grid=(nsteps,),
        in_specs=[row, nxt, row, nxt,
                  pl.BlockSpec((tm, 2 * D), lambda i: (i, 0)),
                  pl.BlockSpec((hl, 2 * D), lambda i: (jnp.maximum(i * (tm // hl) - 1, 0), 0)),
                  _const_spec((32, D)), _const_spec((1, D)), _const_spec((1, D))] + g_specs,
        out_specs=[pl.BlockSpec((tm, 2 * D), lambda i: (i, 0)), _const_spec((32, D)), _const_spec((8, D))]
        + g_ospecs,
        out_shape=[jax.ShapeDtypeStruct((T, 2 * D), BF16), jax.ShapeDtypeStruct((32, D), F32),
                   jax.ShapeDtypeStruct((8, D), F32)] + g_oshapes,
        scratch_shapes=[pltpu.VMEM((8, hl + tm, D), F32), pltpu.VMEM((8, tm + hl, D), F32)],
        compiler_params=_cp(("arbitrary",)),
    )(dz3, dz3, z1, z1, ab, ab, kern, lng, lnb, *g_args)


def _alibi_slopes():
    h = np.arange(1, 3 * NHG + 1, dtype=np.float32)
    return np.power(np.float32(2.0), -8.0 * h / np.float32(3 * NHG)).astype(np.float32)


def _band_bias(gi):
    _, dil = GROUPS[gi]
    slopes = _alibi_slopes()[gi * NHG:(gi + 1) * NHG]
    qi = np.arange(BLK)[:, None]
    ki = np.arange(2 * BLK)[None, :]
    steps = BLK + qi - ki
    band = (steps >= 0) & (steps <= BLK)
    bias = -slopes[:, None, None] * (dil * steps).astype(np.float32)[None]
    return jnp.asarray(np.where(band[None], bias, np.float32(NEG)).astype(np.float32))


QB_FWD = 8
QB_BWD = 32


def _attn_specs(qb):
    prev = lambda n: jnp.maximum(n * qb - 1, 0)
    return [pl.BlockSpec((qb * BLK, HEAD), lambda h, n: (n, h)),
            pl.BlockSpec((BLK, HEAD), lambda h, n: (prev(n), NHG + h)),
            pl.BlockSpec((qb * BLK, HEAD), lambda h, n: (n, NHG + h)),
            pl.BlockSpec((BLK, HEAD), lambda h, n: (prev(n), 2 * NHG + h)),
            pl.BlockSpec((qb * BLK, HEAD), lambda h, n: (n, 2 * NHG + h)),
            pl.BlockSpec((None, BLK, 2 * BLK), lambda h, n: (h, 0, 0))]


def _scores(q, kcat, bias, blk, seg):
    s = _dot(q, kcat, NT) * (HEAD ** -0.5) + bias
    col = lax.broadcasted_iota(jnp.int32, s.shape, 1)
    first = (blk % seg) == 0
    return jnp.where(jnp.logical_and(first, col < BLK), NEG, s)


def _attn_fwd(qkv, gi, name):
    seg = (T // GROUPS[gi][1]) // BLK

    qb = min(QB_FWD, T // BLK)

    def body(q_ref, kp_ref, kc_ref, vp_ref, vc_ref, bias_ref, o_ref, l_ref):
        n = pl.program_id(0)
        for h in range(NHG):
            cols = pl.ds(h * HEAD, HEAD)
            kwin = jnp.concatenate([kp_ref[:, cols], kc_ref[:, cols]], axis=0)
            vwin = jnp.concatenate([vp_ref[:, cols], vc_ref[:, cols]], axis=0)
            bias = bias_ref[h]
            for b in range(qb):
                rows = pl.ds(b * BLK, BLK)
                s = _scores(q_ref[rows, cols], kwin[b * BLK:(b + 2) * BLK], bias, n * qb + b, seg)
                mx = jnp.max(s, axis=-1, keepdims=True)
                p = jnp.exp(s - mx)
                den = jnp.sum(p, axis=-1, keepdims=True)
                o_ref[rows, cols] = (_dot(p.astype(BF16), vwin[b * BLK:(b + 2) * BLK], NN) / den).astype(BF16)
                l_ref[rows, cols] = jnp.broadcast_to(mx + jnp.log(den), (BLK, HEAD))

    prev = lambda n: jnp.maximum(n * qb - 1, 0)
    cur = lambda part: pl.BlockSpec((qb * BLK, AW), lambda n: (n, part))
    halo = lambda part: pl.BlockSpec((BLK, AW), lambda n: (prev(n), part))
    return pl.pallas_call(
        body, name=name, grid=(T // (qb * BLK),),
        in_specs=[cur(0), halo(1), cur(1), halo(2), cur(2), _const_spec((NHG, BLK, 2 * BLK))],
        out_specs=[cur(0), cur(0)],
        out_shape=[jax.ShapeDtypeStruct((T, AW), BF16), jax.ShapeDtypeStruct((T, AW), F32)],
        compiler_params=_cp(("parallel",)),
    )(qkv, qkv, qkv, qkv, qkv, _band_bias(gi))


def _attn_bwd(qkv, dob, lse, delta, gi, name):
    seg = (T // GROUPS[gi][1]) // BLK
    qb = min(QB_BWD, T // BLK)
    nb = T // (qb * BLK)
    scale = HEAD ** -0.5

    def body(q_ref, kp_ref, kc_ref, vp_ref, vc_ref, bias_ref, do_ref, l_ref, dl_ref, out_ref, dk_acc, dv_acc):
        n = pl.program_id(1)
        kwin = jnp.concatenate([kp_ref[...], kc_ref[...]], axis=0)
        vwin = jnp.concatenate([vp_ref[...], vc_ref[...]], axis=0)
        bias = bias_ref[...]
        dks, dvs = [], []
        for b in range(qb):
            rows = pl.ds(b * BLK, BLK)
            q = q_ref[rows, :]
            kcat = kwin[b * BLK:(b + 2) * BLK]
            s = _scores(q, kcat, bias, n * qb + b, seg)
            p = jnp.exp(s - l_ref[rows, pl.ds(0, 1)])
            dov = do_ref[rows, :]
            dvs.append(_dot(p.astype(BF16), dov, TN))
            dp = _dot(dov, vwin[b * BLK:(b + 2) * BLK], NT)
            dsb = (p * (dp - dl_ref[rows, pl.ds(0, 1)]) * scale).astype(BF16)
            row = pl.ds(pl.multiple_of((n * qb + b) * BLK, BLK), BLK)
            out_ref[0, row, :] = _dot(dsb, kcat, NN).astype(BF16)
            dks.append(_dot(dsb, q, TN))
        for b in range(qb):
            row = pl.ds(pl.multiple_of((n * qb + b) * BLK, BLK), BLK)
            if b + 1 < qb:
                dk_acc[row, :] = dks[b][BLK:] + dks[b + 1][:BLK]
                dv_acc[row, :] = dvs[b][BLK:] + dvs[b + 1][:BLK]
            else:
                dk_acc[row, :] = dks[b][BLK:]
                dv_acc[row, :] = dvs[b][BLK:]

        @pl.when(n > 0)
        def _():
            prow = pl.ds(pl.multiple_of((n * qb - 1) * BLK, BLK), BLK)
            dk_acc[prow, :] += dks[0][:BLK]
            dv_acc[prow, :] += dvs[0][:BLK]

        @pl.when(n == nb - 1)
        def _():
            out_ref[1] = dk_acc[...].astype(BF16)
            out_ref[2] = dv_acc[...].astype(BF16)

    oblk = pl.BlockSpec((qb * BLK, HEAD), lambda h, n: (n, h))
    return pl.pallas_call(
        body, name=name, grid=(NHG, nb),
        in_specs=_attn_specs(qb) + [oblk, oblk, oblk],
        out_specs=pl.BlockSpec((3, T, HEAD), lambda h, n: (0, 0, h)),
        out_shape=jax.ShapeDtypeStruct((3, T, AW), BF16),
        scratch_shapes=[pltpu.VMEM((T, HEAD), F32), pltpu.VMEM((T, HEAD), F32)],
        compiler_params=_cp(("parallel", "arbitrary")),
    )(qkv, qkv, qkv, qkv, qkv, _band_bias(gi), dob, lse, delta)


def _merge(outs, lses, name):
    tm = PERM_TM
    dils = [d for _, d in GROUPS]
    ng = len(dils)

    def body(*refs):
        in_refs = refs[:2 * ng]
        ab_ref = refs[2 * ng]
        lse_refs = refs[2 * ng + 1:3 * ng + 1]
        tile = refs[-1]

        def token_order(ref, dil):
            if dil == 1:
                return ref[...].astype(F32)
            _load_unperm(ref, tile, dil)
            return _get_tile(tile)

        os = [token_order(in_refs[2 * i], d) for i, d in enumerate(dils)]
        ls = [token_order(in_refs[2 * i + 1], d) for i, d in enumerate(dils)]
        mx = jnp.maximum(jnp.maximum(ls[0], ls[1]), ls[2])
        es = [jnp.exp(v - mx) for v in ls]
        tot = es[0] + es[1] + es[2]
        att = (es[0] / tot) * os[0] + (es[1] / tot) * os[1] + (es[2] / tot) * os[2]
        ab_ref[...] = att.astype(BF16)
        lse = mx + jnp.log(tot)
        _put_tile(tile, lse)
        for dil, ref in zip(dils, lse_refs):
            if dil == 1:
                ref[...] = lse
            else:
                _store_perm(ref, tile, dil)

    row = pl.BlockSpec((tm, AW), lambda i: (i, 0))
    specs = [row if d == 1 else _perm_spec(d, AW) for d in dils]
    args = []
    for d, o, l in zip(dils, outs, lses):
        args += [o, l] if d == 1 else [o.reshape(d, T // d, AW), l.reshape(d, T // d, AW)]
    out = pl.pallas_call(
        body, name=name, grid=(T // tm,),
        in_specs=[sp for sp in specs for _ in range(2)], out_specs=[row] + specs,
        out_shape=[jax.ShapeDtypeStruct((T, AW), BF16)]
        + [jax.ShapeDtypeStruct((T, AW), F32) if d == 1 else _perm_shape(d, AW, F32) for d in dils],
        scratch_shapes=[_tile_scratch(AW)],
        compiler_params=_cp(("parallel",)),
    )(*args)
    return out[0], [o.reshape(T, AW) for o in out[1:]]


GATE_BLOCK0 = (IN_W - 2 * D) // (D // 2)


def _mix_out(z3b, attnb, gates, wc, wa_t, wo, x1, name):
    tm = 512

    def body(z_ref, a_ref, g_ref, wc_ref, wa_ref, wo_ref, x_ref, xo_ref, yc_ref, ya_ref, mx_ref):
        yc = _dot(z_ref[...], wc_ref[...], NN)
        ya = _dot(a_ref[...], wa_ref[...], NT)
        yc_ref[...] = yc.astype(BF16)
        ya_ref[...] = ya.astype(BF16)
        gv = g_ref[...].astype(F32)
        mixed = (_sig(gv[:, :D]) * yc + _sig(gv[:, D:]) * ya).astype(BF16)
        mx_ref[...] = mixed
        xo_ref[...] = x_ref[...] + _dot(mixed, wo_ref[...], NN)

    row = pl.BlockSpec((tm, D), lambda i: (i, 0))
    return pl.pallas_call(
        body, name=name, grid=(T // tm,),
        in_specs=[row, pl.BlockSpec((tm, AW), lambda i: (i, 0)), pl.BlockSpec((tm, 2 * D), lambda i: (i, 0)),
                  _const_spec((D, D)), _const_spec((D, AW)), _const_spec((D, D)), row],
        out_specs=[row, row, row, row],
        out_shape=[jax.ShapeDtypeStruct((T, D), F32), jax.ShapeDtypeStruct((T, D), BF16),
                   jax.ShapeDtypeStruct((T, D), BF16), jax.ShapeDtypeStruct((T, D), BF16)],
        compiler_params=_cp(("parallel",)),
    )(z3b, attnb, gates, wc, wa_t, wo, x1)


def _mix_out_bwd(dx2, gates, yc, ya, attn, wc, wa_t, wo, win_t, name):
    tm = PERM_TM
    dils = [d for _, d in GROUPS]
    ng = len(dils)

    def body(dx_ref, g_ref, yc_ref, ya_ref, at_ref, wc_ref, wa_ref, wo_ref, wg0_ref, wg1_ref, wg2_ref, wg3_ref,
             dg_ref, dyc_ref, dya_ref, dxb_ref, dz3_ref, dhg_ref, *rest):
        dat_refs, dl_refs, tile = rest[:ng], rest[ng:2 * ng], rest[-1]
        dxb = dx_ref[...].astype(BF16)
        dxb_ref[...] = dxb
        dmix = _dot(dxb, wo_ref[...], NT)
        gv = g_ref[...].astype(F32)
        sc = _sig(gv[:, :D])
        sa = _sig(gv[:, D:])
        ycv, yav = yc_ref[...].astype(F32), ya_ref[...].astype(F32)
        dgc = (dmix * ycv * sc * (1.0 - sc)).astype(BF16)
        dga = (dmix * yav * sa * (1.0 - sa)).astype(BF16)
        dg_ref[:, pl.ds(0, D)] = dgc
        dg_ref[:, pl.ds(D, D)] = dga
        half = D // 2
        dhg_ref[...] = (_dot(dgc[:, :half], wg0_ref[...], NN) + _dot(dgc[:, half:], wg1_ref[...], NN)
                        + _dot(dga[:, :half], wg2_ref[...], NN) + _dot(dga[:, half:], wg3_ref[...], NN))
        dyc = (dmix * sc).astype(BF16)
        dya = (dmix * sa).astype(BF16)
        dyc_ref[...] = dyc
        dya_ref[...] = dya
        dz3_ref[...] = _dot(dyc, wc_ref[...], NT).astype(BF16)
        dat = _dot(dya, wa_ref[...], NN)
        prod = dat * at_ref[...].astype(F32)
        delta = jnp.concatenate(
            [jnp.broadcast_to(jnp.sum(prod[:, h * HEAD:(h + 1) * HEAD], axis=-1, keepdims=True), (tm, HEAD))
             for h in range(NHG)], axis=1)
        for value, out_refs in ((dat, dat_refs), (delta, dl_refs)):
            _put_tile(tile, value)
            for dil, ref in zip(dils, out_refs):
                if dil == 1:
                    ref[...] = value.astype(ref.dtype)
                else:
                    _store_perm(ref, tile, dil)

    row = pl.BlockSpec((tm, D), lambda i: (i, 0))
    row2 = pl.BlockSpec((tm, 2 * D), lambda i: (i, 0))
    rowa = pl.BlockSpec((tm, AW), lambda i: (i, 0))
    aspecs = [rowa if d == 1 else _perm_spec(d, AW) for d in dils]

    def ashapes(dtype):
        return [jax.ShapeDtypeStruct((T, AW), dtype) if d == 1 else _perm_shape(d, AW, dtype) for d in dils]

    out = pl.pallas_call(
        body, name=name, grid=(T // tm,),
        in_specs=[row, row2, row, row, rowa, _const_spec((D, D)), _const_spec((D, AW)), _const_spec((D, D))]
        + [pl.BlockSpec((D // 2, D), lambda i, q=q: (GATE_BLOCK0 + q, 0), pipeline_mode=pl.Buffered(1))
           for q in range(4)],
        out_specs=[row2, row, row, row, row, row] + aspecs + aspecs,
        out_shape=[jax.ShapeDtypeStruct((T, 2 * D), BF16), jax.ShapeDtypeStruct((T, D), BF16),
                   jax.ShapeDtypeStruct((T, D), BF16), jax.ShapeDtypeStruct((T, D), BF16),
                   jax.ShapeDtypeStruct((T, D), BF16), jax.ShapeDtypeStruct((T, D), F32)]
        + ashapes(BF16) + ashapes(F32),
        scratch_shapes=[_tile_scratch(AW)],
        compiler_params=_cp(("parallel",)),
    )(dx2, gates, yc, ya, attn, wc, wa_t, wo, win_t, win_t, win_t, win_t)
    dats = [o.reshape(T, AW) for o in out[6:6 + ng]]
    deltas = [o.reshape(T, AW) for o in out[6 + ng:6 + 2 * ng]]
    return out[0], out[1], out[2], out[3], out[4], out[5], dats, deltas


def _peer(k):
    x, y, c = lax.axis_index("x"), lax.axis_index("y"), lax.axis_index("c")
    px = 1 - x if k & 4 else x
    py = 1 - y if k & 2 else y
    pc = 1 - c if k & 1 else c
    return (px, py, pc), 4 * px + 2 * py + pc


HBM_SPEC = pl.BlockSpec(memory_space=pltpu.HBM)
SEM_SPEC = pl.BlockSpec(memory_space=pltpu.SEMAPHORE)
EFFECT = pltpu.SideEffectType.DATAFLOW_SIDE_EFFECTING


def _my_place():
    return 4 * lax.axis_index("x") + 2 * lax.axis_index("y") + lax.axis_index("c")


def _tie(a, order_after, name):
    na = len(order_after)

    def body(*refs):
        del refs

    return pl.pallas_call(
        body, name=name, in_specs=[pl.BlockSpec(memory_space=pl.ANY)] * (1 + na),
        out_specs=pl.BlockSpec(memory_space=pl.ANY), out_shape=jax.ShapeDtypeStruct(a.shape, a.dtype),
        input_output_aliases={0: 0},
    )(a, *order_after)


def _prep_gather(ws, order_after, name):
    me = jnp.reshape(_my_place(), (1,)).astype(jnp.int32)
    n = len(ws)
    na = len(order_after)
    shapes = [((32, wv.shape[1]), F32) if wv.shape[0] == CONV_W else (wv.shape, BF16) for wv in ws]

    def body(me_ref, *refs):
        del me_ref
        ins, outs = refs[:n], refs[n + na:]
        for wv, i_ref, o_ref in zip(ws, ins, outs):
            if wv.shape[0] == CONV_W:
                o_ref[pl.ds(0, CONV_W), :] = i_ref[...]
                o_ref[pl.ds(CONV_W, 1), :] = jnp.zeros((1, wv.shape[1]), F32)
            else:
                o_ref[...] = i_ref[...].astype(BF16)

    grid_spec = pltpu.PrefetchScalarGridSpec(
        num_scalar_prefetch=1, grid=(1,),
        in_specs=[pl.BlockSpec(wv.shape, lambda i, m: (0, 0)) for wv in ws]
        + [pl.BlockSpec(memory_space=pl.ANY)] * na,
        out_specs=[pl.BlockSpec(shp, lambda i, m: (m[0], 0)) for shp, _ in shapes])
    return pl.pallas_call(
        body, name=name, grid_spec=grid_spec,
        out_shape=[jax.ShapeDtypeStruct((NDEV * shp[0], shp[1]), dt) for shp, dt in shapes],
        compiler_params=_cp(("arbitrary",)),
    )(me, *ws, *order_after)


GATHER_A = ((1, 0), (2, 0), (4, 0), (6, 0))
GATHER_B = ((1, 2), (1, 4), (1, 6))
GATHER_DIRECT = tuple((k, 0) for k in range(1, NDEV))


def _gather_start(lands, plan, order_after, name):
    n = len(lands)
    na = len(order_after)
    npl = len(plan)

    def body(*refs):
        land_refs = refs[:n]
        send, recv = refs[n + na], refs[n + na + 1]
        token = refs[-1]
        for w in range(n):
            rows = lands[w].shape[0] // NDEV
            for p, (k, j) in enumerate(plan):
                peer, _ = _peer(k)
                _, blk = _peer(j)
                part = land_refs[w].at[pl.ds(blk * rows, rows)]
                i = w * npl + p
                pltpu.make_async_remote_copy(src_ref=part, dst_ref=part, send_sem=send.at[i], recv_sem=recv.at[i],
                                             device_id=peer, device_id_type=MESH_ID).start()
        token[...] = jnp.zeros_like(token)

    nsem = n * npl
    bufs = [pltpu.with_memory_space_constraint(a, pltpu.HBM) for a in lands]
    out = pl.pallas_call(
        body, name=name,
        in_specs=[HBM_SPEC] * n + [pl.BlockSpec(memory_space=pl.ANY)] * na,
        out_specs=[SEM_SPEC, SEM_SPEC] + [HBM_SPEC] * n + [pl.BlockSpec(memory_space=pltpu.VMEM)],
        out_shape=[pltpu.SemaphoreType.DMA((nsem,)), pltpu.SemaphoreType.DMA((nsem,))]
        + [pltpu.HBM(a.shape, a.dtype) for a in bufs] + [jax.ShapeDtypeStruct((8, 128), F32)],
        input_output_aliases={i: 2 + i for i in range(n)},
        compiler_params=pltpu.CompilerParams(has_side_effects=EFFECT),
    )(*bufs, *order_after)
    return out[0], out[1], out[2:2 + n], out[-1]


def _gather_wait(started, plan, order_after, name):
    send, recv, lands, _ = started
    n = len(lands)
    na = len(order_after)
    npl = len(plan)

    def body(*refs):
        land_refs = refs[:n]
        send_ref, recv_ref = refs[n], refs[n + 1]
        for w in range(n):
            rows = lands[w].shape[0] // NDEV
            for p, (k, j) in enumerate(plan):
                peer, _ = _peer(k)
                _, blk = _peer(j)
                part = land_refs[w].at[pl.ds(blk * rows, rows)]
                i = w * npl + p
                cp = pltpu.make_async_remote_copy(src_ref=part, dst_ref=part, send_sem=send_ref.at[i],
                                                  recv_sem=recv_ref.at[i], device_id=peer, device_id_type=MESH_ID)
                cp.wait_send()
                cp.wait_recv()

    out = pl.pallas_call(
        body, name=name,
        in_specs=[HBM_SPEC] * n + [SEM_SPEC, SEM_SPEC] + [pl.BlockSpec(memory_space=pl.ANY)] * na,
        out_specs=[HBM_SPEC] * n,
        out_shape=[pltpu.HBM(a.shape, a.dtype) for a in lands],
        input_output_aliases={i: i for i in range(n)},
        compiler_params=pltpu.CompilerParams(has_side_effects=EFFECT),
    )(*lands, send, recv, *order_after)
    return list(out)


def _copy_ends(kind, src, land, me, plin, k):
    if kind == "scatter":
        rows = src.shape[0] // NDEV
        return src.at[pl.ds(plin * rows, rows)], land.at[k - 1]
    return src, land.at[me]


def _landing(kind, src):
    me = _my_place()
    if kind == "scatter":
        return lax.empty((NDEV - 1, src.shape[0] // NDEV) + src.shape[1:], src.dtype)
    land = lax.empty((NDEV,) + src.shape, src.dtype)
    return lax.dynamic_update_slice(land, src[None], (me,) + (0,) * src.ndim)


def _send_start(kinds, srcs, order_after, name):
    n = len(srcs)
    lands = [_landing(kd, s) for kd, s in zip(kinds, srcs)]
    na = len(order_after)

    def body(*refs):
        src_refs, land_refs = refs[:n], refs[n:2 * n]
        send, recv = refs[2 * n + na], refs[2 * n + na + 1]
        token = refs[-1]
        _, me = _peer(0)
        for w in range(n):
            for k in range(1, NDEV):
                peer, plin = _peer(k)
                s, d = _copy_ends(kinds[w], src_refs[w], land_refs[w], me, plin, k)
                i = w * (NDEV - 1) + k - 1
                pltpu.make_async_remote_copy(src_ref=s, dst_ref=d, send_sem=send.at[i], recv_sem=recv.at[i],
                                             device_id=peer, device_id_type=MESH_ID).start()
        token[...] = jnp.zeros_like(token)

    nsem = n * (NDEV - 1)
    bufs = [pltpu.with_memory_space_constraint(a, pltpu.HBM) for a in list(srcs) + lands]
    out = pl.pallas_call(
        body, name=name,
        in_specs=[HBM_SPEC] * (2 * n) + [pl.BlockSpec(memory_space=pl.ANY)] * na,
        out_specs=[SEM_SPEC, SEM_SPEC] + [HBM_SPEC] * (2 * n) + [pl.BlockSpec(memory_space=pltpu.VMEM)],
        out_shape=[pltpu.SemaphoreType.DMA((nsem,)), pltpu.SemaphoreType.DMA((nsem,))]
        + [pltpu.HBM(a.shape, a.dtype) for a in bufs] + [jax.ShapeDtypeStruct((8, 128), F32)],
        input_output_aliases={i: 2 + i for i in range(2 * n)},
        compiler_params=pltpu.CompilerParams(has_side_effects=EFFECT),
    )(*bufs, *order_after)
    return out[0], out[1], out[2:2 + n], out[2 + n:2 + 2 * n], out[-1]


def _send_wait(kinds, started, order_after, name):
    send, recv, srcs, lands, _ = started
    n = len(srcs)
    na = len(order_after)

    def body(*refs):
        src_refs, land_refs = refs[:n], refs[n:2 * n]
        send_ref, recv_ref = refs[2 * n], refs[2 * n + 1]
        _, me = _peer(0)
        for w in range(n):
            for k in range(1, NDEV):
                peer, plin = _peer(k)
                s, d = _copy_ends(kinds[w], src_refs[w], land_refs[w], me, plin, k)
                i = w * (NDEV - 1) + k - 1
                cp = pltpu.make_async_remote_copy(src_ref=s, dst_ref=d, send_sem=send_ref.at[i],
                                                  recv_sem=recv_ref.at[i], device_id=peer, device_id_type=MESH_ID)
                cp.wait_send()
                cp.wait_recv()

    bufs = list(srcs) + list(lands)
    out = pl.pallas_call(
        body, name=name,
        in_specs=[HBM_SPEC] * (2 * n) + [SEM_SPEC, SEM_SPEC] + [pl.BlockSpec(memory_space=pl.ANY)] * na,
        out_specs=[HBM_SPEC] * (2 * n),
        out_shape=[pltpu.HBM(a.shape, a.dtype) for a in bufs],
        input_output_aliases={i: i for i in range(2 * n)},
        compiler_params=pltpu.CompilerParams(has_side_effects=EFFECT),
    )(*bufs, send, recv, *order_after)
    return out[:n], out[n:]


def _gsum(own, land, name):
    rows, cols = own.shape
    tr = rows // 2 if rows * cols > 512 * 1024 and rows % 32 == 0 else rows

    def body(own_ref, l_ref, o_ref):
        tot = own_ref[...].astype(F32)
        for s in range(NDEV - 1):
            tot = tot + l_ref[s].astype(F32)
        o_ref[...] = tot

    return pl.pallas_call(
        body, name=name, grid=(rows // tr,),
        in_specs=[pl.BlockSpec((tr, cols), lambda i: (i, 0)),
                  pl.BlockSpec((NDEV - 1, tr, cols), lambda i: (0, i, 0))],
        out_specs=pl.BlockSpec((tr, cols), lambda i: (i, 0)),
        out_shape=jax.ShapeDtypeStruct((rows, cols), F32),
        compiler_params=_cp(("parallel",)),
    )(own, land)


def _adamw_math(w, g, m, v):
    m2 = B1 * m + (1.0 - B1) * g
    v2 = B2 * v + (1.0 - B2) * (g * g)
    m_hat = m2 / (1.0 - B1 ** STEP)
    v_hat = v2 / (1.0 - B2 ** STEP)
    delta = -LR * (m_hat / (jnp.sqrt(v_hat) + AEPS) + WD * w)
    return delta, m2, v2


def _adamw(w, g, m, v, name):
    rows, cols = w.shape
    tr = 256 if rows % 256 == 0 and rows > 256 else rows

    def body(w_ref, g_ref, m_ref, v_ref, d_ref, mo_ref, vo_ref):
        d, m2, v2 = _adamw_math(w_ref[...], g_ref[...], m_ref[...], v_ref[...])
        d_ref[...] = d
        mo_ref[...] = m2
        vo_ref[...] = v2

    blk = pl.BlockSpec((tr, cols), lambda i: (i, 0))
    return pl.pallas_call(
        body, name=name, grid=(rows // tr,), in_specs=[blk] * 4, out_specs=[blk] * 3,
        out_shape=[jax.ShapeDtypeStruct((rows, cols), F32)] * 3,
        compiler_params=_cp(("parallel",)),
    )(w, g, m, v)


UPD_TC = 256


def _update(src, land, w, m, v, name):
    rows, cols = land.shape[1:]
    tc = min(UPD_TC if rows > 512 else 2 * UPD_TC, cols)
    me = jnp.reshape(_my_place(), (1,)).astype(jnp.int32)

    def body(me_ref, own_ref, l_ref, w_ref, m_ref, v_ref, g_ref, d_ref, mo_ref, vo_ref):
        del me_ref
        g = own_ref[...].astype(F32)
        for s in range(NDEV - 1):
            g = g + l_ref[s].astype(F32)
        g_ref[...] = g
        d, m2, v2 = _adamw_math(w_ref[...], g, m_ref[...], v_ref[...])
        d_ref[...] = d
        mo_ref[...] = m2
        vo_ref[...] = v2

    wblk = pl.BlockSpec((rows, tc), lambda j, p: (0, j))
    grid_spec = pltpu.PrefetchScalarGridSpec(
        num_scalar_prefetch=1, grid=(cols // tc,),
        in_specs=[pl.BlockSpec((rows, tc), lambda j, p: (p[0], j)),
                  pl.BlockSpec((NDEV - 1, rows, tc), lambda j, p: (0, 0, j)), wblk, wblk, wblk],
        out_specs=[wblk] * 4)
    return pl.pallas_call(
        body, name=name, grid_spec=grid_spec, out_shape=[jax.ShapeDtypeStruct((rows, cols), F32)] * 4,
        compiler_params=_cp(("parallel",)),
    )(me, src, land, w, m, v)


def _small_update(vland, w8, m8, v8, name):
    def body(l_ref, w_ref, m_ref, v_ref, g_ref, d_ref, mo_ref, vo_ref):
        g = l_ref[0]
        for s in range(1, NDEV):
            g = g + l_ref[s]
        g_ref[...] = g
        d, m2, v2 = _adamw_math(w_ref[...], g, m_ref[...], v_ref[...])
        d_ref[...] = d
        mo_ref[...] = m2
        vo_ref[...] = v2

    return pl.pallas_call(
        body, name=name, out_shape=[jax.ShapeDtypeStruct((8, D), F32)] * 4,
        compiler_params=_cp(None),
    )(vland, w8, m8, v8)


def kernel(x, ffn1_norm, ffn1_w_gate, ffn1_w_up, ffn1_w_down, mix_norm, w_in, conv_dw_kernel, conv_dw_bias, conv_ln_gain, conv_ln_bias, conv_w_out, attn_w_out, w_o, ffn2_norm, ffn2_w_gate, ffn2_w_up, ffn2_w_down, final_norm, loss_target, m_ffn1_norm, m_ffn1_w_gate, m_ffn1_w_up, m_ffn1_w_down, m_mix_norm, m_w_in, m_conv_dw_kernel, m_conv_dw_bias, m_conv_ln_gain, m_conv_ln_bias, m_conv_w_out, m_attn_w_out, m_w_o, m_ffn2_norm, m_ffn2_w_gate, m_ffn2_w_up, m_ffn2_w_down, m_final_norm, v_ffn1_norm, v_ffn1_w_gate, v_ffn1_w_up, v_ffn1_w_down, v_mix_norm, v_w_in, v_conv_dw_kernel, v_conv_dw_bias, v_conv_ln_gain, v_conv_ln_bias, v_conv_w_out, v_attn_w_out, v_w_o, v_ffn2_norm, v_ffn2_w_gate, v_ffn2_w_up, v_ffn2_w_down, v_final_norm):
    names = ["ffn1_norm", "ffn1_w_gate", "ffn1_w_up", "ffn1_w_down", "mix_norm", "w_in", "conv_dw_kernel",
             "conv_dw_bias", "conv_ln_gain", "conv_ln_bias", "conv_w_out", "attn_w_out", "w_o", "ffn2_norm",
             "ffn2_w_gate", "ffn2_w_up", "ffn2_w_down", "final_norm"]
    w = dict(ffn1_norm=ffn1_norm, ffn1_w_gate=ffn1_w_gate, ffn1_w_up=ffn1_w_up, ffn1_w_down=ffn1_w_down, mix_norm=mix_norm, w_in=w_in, conv_dw_kernel=conv_dw_kernel, conv_dw_bias=conv_dw_bias, conv_ln_gain=conv_ln_gain, conv_ln_bias=conv_ln_bias, conv_w_out=conv_w_out, attn_w_out=attn_w_out, w_o=w_o, ffn2_norm=ffn2_norm, ffn2_w_gate=ffn2_w_gate, ffn2_w_up=ffn2_w_up, ffn2_w_down=ffn2_w_down, final_norm=final_norm)
    mo = dict(ffn1_norm=m_ffn1_norm, ffn1_w_gate=m_ffn1_w_gate, ffn1_w_up=m_ffn1_w_up, ffn1_w_down=m_ffn1_w_down, mix_norm=m_mix_norm, w_in=m_w_in, conv_dw_kernel=m_conv_dw_kernel, conv_dw_bias=m_conv_dw_bias, conv_ln_gain=m_conv_ln_gain, conv_ln_bias=m_conv_ln_bias, conv_w_out=m_conv_w_out, attn_w_out=m_attn_w_out, w_o=m_w_o, ffn2_norm=m_ffn2_norm, ffn2_w_gate=m_ffn2_w_gate, ffn2_w_up=m_ffn2_w_up, ffn2_w_down=m_ffn2_w_down, final_norm=m_final_norm)
    vo = dict(ffn1_norm=v_ffn1_norm, ffn1_w_gate=v_ffn1_w_gate, ffn1_w_up=v_ffn1_w_up, ffn1_w_down=v_ffn1_w_down, mix_norm=v_mix_norm, w_in=v_w_in, conv_dw_kernel=v_conv_dw_kernel, conv_dw_bias=v_conv_dw_bias, conv_ln_gain=v_conv_ln_gain, conv_ln_bias=v_conv_ln_bias, conv_w_out=v_conv_w_out, attn_w_out=v_attn_w_out, w_o=v_w_o, ffn2_norm=v_ffn2_norm, ffn2_w_gate=v_ffn2_w_gate, ffn2_w_up=v_ffn2_w_up, ffn2_w_down=v_ffn2_w_down, final_norm=v_final_norm)
    col_sharded = ("ffn1_w_gate", "ffn1_w_up", "w_in", "attn_w_out", "ffn2_w_gate", "ffn2_w_up")
    row_sharded = ("ffn1_w_down", "conv_w_out", "w_o", "ffn2_w_down")
    small = ("ffn1_norm", "mix_norm", "ffn2_norm", "final_norm", "conv_dw_bias", "conv_ln_gain", "conv_ln_bias")

    def landing_view(a, n):
        return jnp.transpose(a[0]) if n in col_sharded else a[0]

    def own_view(a, n):
        return jnp.transpose(a)[None] if n in col_sharded else a[None]

    ag_groups = (("ffn1_w_gate", "ffn1_w_up", "ffn1_w_down"),
                 ("w_in", "attn_w_out", "conv_w_out", "w_o", "conv_dw_kernel"),
                 ("ffn2_w_gate", "ffn2_w_up", "ffn2_w_down"))
    ag, order = [], []
    for gi, grp in enumerate(ag_groups):
        lands = _prep_gather([landing_view(w[n], n) for n in grp], order, f"gather_prep{gi}")
        st = _gather_start(lands, GATHER_DIRECT if gi == 2 else GATHER_A, [], f"gather_a_start{gi}")
        ag.append(st)
        order = [st[3]]

    def chips_in(gi, after):
        lands = _gather_wait(ag[gi], GATHER_A, after, f"gather_a_wait{gi}")
        return _gather_start(lands, GATHER_B, [], f"gather_b_start{gi}")

    def all_in(gi, st, after):
        return _gather_wait(st, GATHER_B, after, f"gather_b_wait{gi}")

    x0 = x[0]
    tgt = loss_target[0]
    gf = final_norm.reshape(1, D)

    wg1, wu1, wd1 = all_in(0, chips_in(0, [ag[2][3]]), [])
    x1, gg1, uu1, h2p = _ffn_fwd(x0, ffn1_norm, wg1, wu1, wd1, "ffn1_fwd", next_gain=mix_norm)
    h2 = h2p[0]
    win_t, wa_t, wc, wo, kern_blocks = all_in(1, chips_in(1, [x1]), [])
    kern = kern_blocks.reshape(NDEV, 32, D // NDEV).transpose(1, 0, 2).reshape(32, D)
    ptm = min(T, 2048)
    ab = _mm(h2, win_t, mode="nt", m=T, n=2 * D, k=D, tm=ptm, tn=512, tk=D, out_dtype=BF16, name="proj_conv")
    qkv = []
    for gi in range(len(GROUPS)):
        qkv.append(_mm(h2p[gi], win_t, mode="nt", m=T, n=3 * AW, k=D, tm=ptm, tn=AW, tk=D, out_dtype=BF16,
                       b_map=lambda i, j, kk, gi=gi: (4 + gi + 3 * j, 0), name=f"proj_qkv{gi}"))
    z1, z3b, gates = _conv_fwd(ab, kern, conv_dw_bias, conv_ln_gain, conv_ln_bias, "conv_fwd",
                               guest=(h2, win_t, 13, 4, 512))
    outs, lses = [], []
    for gi, (_, dil) in enumerate(GROUPS):
        o, l = _attn_fwd(qkv[gi], gi, f"attn_fwd{gi}")
        outs.append(o)
        lses.append(l)
    attnb, lse = _merge(outs, lses, "attn_merge")
    x2, yc, ya, mixedb = _mix_out(z3b, attnb, gates, wc, wa_t, wo, x1, "mix_out_fwd")
    wg2, wu2, wd2 = _gather_wait(ag[2], GATHER_DIRECT, [x2], "gather_a_wait2")
    gg2, uu2, dx3, dgf, loss_part = _ffn_fwd(x2, ffn2_norm, wg2, wu2, wd2, "ffn2_fwd", loss_of=(gf, tgt))

    dx2, dg3, dgb, dub, actb, hb, dob = _ffn_bwd(x2, ffn2_norm, gg2, uu2, dx3, wg2, wu2, wd2, "ffn2_bwd")
    grads = {}
    grads["ffn2_w_down"] = _wgrad(actb, dob, FF, D, "ffn2_dwd")
    rs_groups = [("ffn2_w_gate", "ffn2_w_up", "ffn2_w_down"),
                 ("attn_w_out", "conv_w_out", "w_o", "conv_dw_kernel"),
                 ("w_in",),
                 ("ffn1_w_gate",), ("ffn1_w_up",), ("ffn1_w_down",), ()]
    last = len(rs_groups) - 1
    rs = []

    dgates, dycb, dyab, dx2b, dz3, dh_gates, dattnb, delta = _mix_out_bwd(dx2, gates, yc, ya, attnb, wc, wa_t, wo,
                                                                          win_t, "mix_out_bwd")
    grads["w_o"] = _wgrad(mixedb, dx2b, D, D, "dw_o")
    grads["conv_w_out"] = _wgrad(z3b, dycb, D, D, "dw_conv_out")
    grads["attn_w_out"] = _wgrad(dyab, attnb, D, AW, "dw_attn_out")
    dab, dkern, dvec, grads["ffn2_w_gate"], grads["ffn2_w_up"] = _conv_bwd(
        dz3, z1, ab, kern, conv_ln_gain, conv_ln_bias, "conv_bwd", guest_lhs=(dgb, dub), guest_rhs=hb)
    grads["conv_dw_kernel"] = dkern.reshape(32, NDEV, D // NDEV).transpose(1, 0, 2).reshape(NDEV * 32, D // NDEV)
    rs.append(_send_start(["scatter"] * 3, [grads[n] for n in rs_groups[0]], [], "scatter_start0"))
    rs.append(_send_start(["scatter"] * 4, [grads[n] for n in rs_groups[1]], [rs[0][4]], "scatter_start1"))
    dattnb = [_tie(a, [rs[1][4]], f"tie_after_scatter1_{i}") for i, a in enumerate(dattnb)]

    dqkv = []
    for gi, (_, dil) in enumerate(GROUPS):
        dq3 = _attn_bwd(qkv[gi], dattnb[gi], lse[gi], delta[gi], gi, f"attn_bwd{gi}")
        dqkv.append(dq3.reshape(3 * T, AW))

    wtk = min(T, 2048)
    dwin = _mm(dab, h2, mode="tn", m=2 * D, n=D, k=T, tm=2 * D, tn=D, tk=wtk, out_dtype=BF16, out_rows=IN_W,
               name="dw_in_conv")
    dwin = _mm(dgates, h2, mode="tn", m=2 * D, n=D, k=T, tm=512, tn=D, tk=wtk, out_dtype=BF16, out_rows=IN_W,
               o_map=lambda i, j, kk: (13 + i, 0), passthru=dwin, name="dw_in_gates")
    for gi in range(3):
        dwin = _mm(dqkv[gi], h2p[gi], mode="tn", m=3 * AW, n=D, k=T, tm=AW, tn=D, tk=wtk, out_dtype=BF16,
                   out_rows=IN_W, a_map=lambda i, j, kk: (i * (T // wtk) + kk, 0),
                   o_map=lambda i, j, kk, gi=gi: (4 + gi + 3 * i, 0), passthru=dwin, name=f"dw_in_qkv{gi}")
    grads["w_in"] = dwin
    rs.append(_send_start(["scatter"], [dwin], [rs[1][4]], "scatter_start2"))
    dab = _tie(dab, [rs[2][4]], "tie_after_scatter2")

    nrow = T // 1024
    dh = _mm(dab, win_t, mode="nn", m=T, n=D, k=2 * D, tm=1024, tn=D, tk=2 * D, out_dtype=F32, init=dh_gates,
             name="dproj_conv")
    dhs = []
    for gi, (_, dil) in enumerate(GROUPS):
        part = _mm(dqkv[gi], win_t, mode="nn", m=T, n=D, k=3 * AW, tm=1024, tn=D, tk=AW,
                   out_dtype=F32 if gi == 0 else BF16,
                   a_map=lambda i, j, kk: (kk * nrow + i, 0), b_map=lambda i, j, kk, gi=gi: (4 + gi + 3 * kk, 0),
                   init=dh if gi == 0 else None, name=f"dproj_qkv{gi}")
        dhs.append(part)
    dx1, dg2 = _rms_bwd(x1, mix_norm, dhs, dx2, "mix_norm_bwd")

    dgb, dub, actb, hb, dob = _ffn_bwd_pre(x0, ffn1_norm, gg1, uu1, dx1, wd1, "ffn1_bwd_pre")
    grads["ffn1_w_gate"] = _wgrad(dgb, hb, FF, D, "ffn1_dwg")
    rs.append(_send_start(["scatter"], [grads["ffn1_w_gate"]], [rs[2][4]], "scatter_start3"))
    hb = _tie(hb, [rs[3][4]], "tie_after_scatter3")
    grads["ffn1_w_up"] = _wgrad(dub, hb, FF, D, "ffn1_dwu")
    rs.append(_send_start(["scatter"], [grads["ffn1_w_up"]], [rs[3][4]], "scatter_start4"))
    dob = _tie(dob, [rs[4][4]], "tie_after_scatter4")
    grads["ffn1_w_down"] = _wgrad(actb, dob, FF, D, "ffn1_dwd")
    rs.append(_send_start(["scatter"], [grads["ffn1_w_down"]], [rs[4][4]], "scatter_start5"))
    dgb = _tie(dgb, [rs[5][4]], "tie_after_scatter5")
    dx0, dg1 = _ffn_bwd_dx(x0, ffn1_norm, dgb, dub, dx1, wg1, wu1, "ffn1_bwd_dx")
    vec = jnp.concatenate([dg1, dg2, dg3, dgf, dvec[0:3], jnp.broadcast_to(loss_part[:, :1], (1, D))], axis=0)
    rs.append(_send_start(["bcast"], [vec], [rs[5][4]], "scatter_start6"))

    g_out, d_out, m_out, v_out = {}, {}, {}, {}
    me = _my_place()
    after = [rs[last][4]]
    for gi, grp in enumerate(rs_groups):
        kinds = ["scatter"] * len(grp) + (["bcast"] if gi == last else [])
        srcs, lands = _send_wait(kinds, rs[gi], after, f"scatter_wait{gi}")
        for n, src, land in zip(grp, srcs, lands):
            if n == "conv_dw_kernel":
                rows = src.shape[0] // NDEV
                own = lax.dynamic_slice(src, (me * rows, 0), (rows, src.shape[1]))
                g = _gsum(own, land, f"gsum_{n}")[:CONV_W]
                d, m2, v2 = _adamw(w[n][0], g, mo[n][0], vo[n][0], f"adamw_{n}")
                after = [d]
                g, d, m2, v2 = g[None], d[None], m2[None], v2[None]
            else:
                res = _update(src, land, landing_view(w[n], n), landing_view(mo[n], n), landing_view(vo[n], n),
                              f"update_{n}")
                after = [res[1]]
                g, d, m2, v2 = (own_view(a, n) for a in res)
            g_out[n], d_out[n], m_out[n], v_out[n] = g, d, m2, v2
    vland = lands[-1]

    def rows8(src):
        return jnp.concatenate([src[n].reshape(1, D) for n in small] + [jnp.ones((1, D), F32)], axis=0)

    g8, d8, m8, v8 = _small_update(vland, rows8(w), rows8(mo), rows8(vo), "small_update")
    for r, n in enumerate(small):
        shp = w[n].shape
        g_out[n], d_out[n], m_out[n], v_out[n] = (a[r].reshape(shp) for a in (g8, d8, m8, v8))
    loss = g8[7, 0]

    return (loss, dx0[None], *[g_out[n] for n in names], *[d_out[n] for n in names],
            *[m_out[n] for n in names], *[v_out[n] for n in names])
```

```python
import numpy as np
import jax
import jax.numpy as jnp
from jax import lax
from jax.experimental import pallas as pl
from jax.experimental.pallas import tpu as pltpu

F32 = jnp.float32
BF16 = jnp.bfloat16

T = 4096
D = 1024
FF = 2816
NDEV = 8
CONV_W = 31
HEAD = 128
BLK = 128
GROUPS = ((128, 1), (512, 4), (2048, 16))
NHG = 4
AW = NHG * HEAD
IN_W = 2 * D + 3 * 3 * AW + 2 * D
EPS = 1e-6
B1, B2, LR, AEPS, WD, STEP = 0.9, 0.999, 0.001, 1e-08, 0.01, 10
NEG = -1e30
VMEM_LIMIT = 56 * 1024 * 1024
MESH_ID = pl.DeviceIdType.MESH

NT = (((1,), (1,)), ((), ()))
NN = (((1,), (0,)), ((), ()))
TN = (((0,), (0,)), ((), ()))
_DIMS = {"nn": NN, "nt": NT, "tn": TN}


def _cp(sem=None):
    return pltpu.CompilerParams(dimension_semantics=sem, vmem_limit_bytes=VMEM_LIMIT)


def _sig(v):
    return 1.0 / (1.0 + jnp.exp(-v))


def _dot(a, b, dims):
    return lax.dot_general(a, b, dims, preferred_element_type=F32)


def _const_spec(shape):
    nd = len(shape)
    return pl.BlockSpec(shape, lambda *_: (0,) * nd)


def _mm(a, b, *, mode, m, n, k, tm, tn, tk, out_dtype, name, a_map=None, b_map=None,
        o_map=None, out_rows=None, init=None, passthru=None):
    gi, gj, gk = m // tm, n // tn, k // tk
    assert gi * tm == m and gj * tn == n and gk * tk == k, (name, m, n, k, tm, tn, tk)
    if mode == "nn":
        a_blk, b_blk = (tm, tk), (tk, tn)
        da, db = (lambda i, j, kk: (i, kk)), (lambda i, j, kk: (kk, j))
    elif mode == "nt":
        a_blk, b_blk = (tm, tk), (tn, tk)
        da, db = (lambda i, j, kk: (i, kk)), (lambda i, j, kk: (j, kk))
    else:
        a_blk, b_blk = (tk, tm), (tk, tn)
        da, db = (lambda i, j, kk: (kk, i)), (lambda i, j, kk: (kk, j))
    a_map = a_map or da
    b_map = b_map or db
    o_map = o_map or (lambda i, j, kk: (i, j))
    dims = _DIMS[mode]
    extra = init if init is not None else passthru
    out_rows = out_rows or m

    def body(*refs):
        if init is not None:
            a_ref, b_ref, i_ref, o_ref = refs[:4]
        elif passthru is not None:
            a_ref, b_ref, _, o_ref = refs[:4]
        else:
            a_ref, b_ref, o_ref = refs[:3]
        if gk == 1:
            prod = _dot(a_ref[...], b_ref[...], dims)
            if init is not None:
                prod = prod + i_ref[...].astype(F32)
            o_ref[...] = prod.astype(out_dtype)
            return
        acc = refs[-1]
        kk = pl.program_id(2)

        @pl.when(kk == 0)
        def _():
            if init is not None:
                acc[...] = i_ref[...].astype(F32)
            else:
                acc[...] = jnp.zeros_like(acc)

        acc[...] += _dot(a_ref[...], b_ref[...], dims)

        @pl.when(kk == gk - 1)
        def _():
            o_ref[...] = acc[...].astype(out_dtype)

    in_specs = [pl.BlockSpec(a_blk, a_map), pl.BlockSpec(b_blk, b_map)]
    args = [a, b]
    aliases = {}
    if init is not None:
        in_specs.append(pl.BlockSpec((tm, tn), o_map))
        args.append(init)
        aliases = {2: 0}
    elif passthru is not None:
        in_specs.append(pl.BlockSpec(memory_space=pl.ANY))
        args.append(passthru)
        aliases = {2: 0}
    out_dt = extra.dtype if extra is not None else out_dtype
    assert out_dt == out_dtype
    return pl.pallas_call(
        body, name=name, grid=(gi, gj, gk),
        in_specs=in_specs, out_specs=pl.BlockSpec((tm, tn), o_map),
        out_shape=jax.ShapeDtypeStruct((out_rows, n), out_dtype),
        scratch_shapes=[pltpu.VMEM((tm, tn), F32)] if gk > 1 else [],
        input_output_aliases=aliases,
        compiler_params=_cp(("parallel", "parallel", "arbitrary")),
    )(*args)


def _ffn_fwd(x, g, wg_t, wu_t, wd, name, next_gain=None, loss_of=None):
    tm, fc = PERM_TM, 256
    nc = FF // fc
    n_in = 5 + (1 if next_gain is not None else 0) + (2 if loss_of is not None else 0)

    def body(*refs):
        x_ref, g_ref, wg_ref, wu_ref, wd_ref = refs[:5]
        extra_in, outs = refs[5:n_in], refs[n_in:]
        act_ref = outs[-1]
        xv = x_ref[...]
        r = lax.rsqrt(jnp.mean(xv * xv, axis=-1, keepdims=True) + EPS)
        h = (xv * r * g_ref[...]).astype(BF16)
        gg_ref, uu_ref = (outs[0], outs[1]) if loss_of is not None else (outs[1], outs[2])
        for c in range(nc):
            sl = pl.ds(c * fc, fc)
            gg = _dot(h, wg_ref[sl, :], NT)
            uu = _dot(h, wu_ref[sl, :], NT)
            gg_ref[:, sl] = gg.astype(BF16)
            uu_ref[:, sl] = uu.astype(BF16)
            act_ref[:, sl] = (gg * _sig(gg) * uu).astype(BF16)
        y = xv + 0.5 * _dot(act_ref[...], wd_ref[...], NN)
        if loss_of is not None:
            _final_math(y, extra_in[0][...], extra_in[1][...], outs[2], outs[3], outs[4], pl.program_id(0))
            return
        outs[0][...] = y
        if next_gain is not None:
            tile = outs[-2]
            r2 = lax.rsqrt(jnp.mean(y * y, axis=-1, keepdims=True) + EPS)
            hv = y * r2 * extra_in[0][...]
            outs[3][...] = hv.astype(BF16)
            _put_tile(tile, hv)
            for dil, p_ref in zip(DILS, outs[4:4 + len(DILS)]):
                _store_perm(p_ref, tile, dil)

    wspec = pl.BlockSpec((FF, D), lambda i: (0, 0), pipeline_mode=pl.Buffered(1))
    row_d = pl.BlockSpec((tm, D), lambda i: (i, 0))
    row_f = pl.BlockSpec((tm, FF), lambda i: (i, 0))
    in_specs = [row_d, _const_spec((1, D)), wspec, wspec, wspec]
    args = [x, g, wg_t, wu_t, wd]
    f_shape = jax.ShapeDtypeStruct((T, FF), BF16)
    scratch = [pltpu.VMEM((tm, FF), BF16)]
    if loss_of is not None:
        in_specs += [_const_spec((1, D)), row_d]
        args += list(loss_of)
        out_specs = [row_f, row_f, row_d, _const_spec((1, D)), _const_spec((1, 128))]
        out_shape = [f_shape, f_shape, jax.ShapeDtypeStruct((T, D), F32), jax.ShapeDtypeStruct((1, D), F32),
                     jax.ShapeDtypeStruct((1, 128), F32)]
    else:
        out_specs = [row_d, row_f, row_f]
        out_shape = [jax.ShapeDtypeStruct((T, D), F32), f_shape, f_shape]
        if next_gain is not None:
            in_specs.append(_const_spec((1, D)))
            args.append(next_gain)
            out_specs += [row_d] + [_perm_spec(d, D) for d in DILS]
            out_shape += [jax.ShapeDtypeStruct((T, D), BF16)] + [_perm_shape(d, D, BF16) for d in DILS]
            scratch = [_tile_scratch(D)] + scratch
    out = pl.pallas_call(
        body, name=name, grid=(T // tm,), in_specs=in_specs, out_specs=out_specs, out_shape=out_shape,
        scratch_shapes=scratch,
        compiler_params=_cp(("arbitrary",) if loss_of is not None else ("parallel",)),
    )(*args)
    if next_gain is not None:
        return out[0], out[1], out[2], [out[3]] + [o.reshape(T, D) for o in out[4:]]
    return tuple(out)


def _ffn_bwd(x, g, gg_all, uu_all, dout, wg_t, wu_t, wd, name):
    tm, fc = 256, 256
    nc = FF // fc

    def body(x_ref, g_ref, gg_ref, uu_ref, do_ref, wg_ref, wu_ref, wd_ref,
             dx_ref, dgam_ref, dg_ref, du_ref, act_ref, h_ref, db_ref):
        i = pl.program_id(0)
        xv = x_ref[...]
        r = lax.rsqrt(jnp.mean(xv * xv, axis=-1, keepdims=True) + EPS)
        xhat = xv * r
        gam = g_ref[...]
        h_ref[...] = (xhat * gam).astype(BF16)
        dov = do_ref[...]
        dbv = (0.5 * dov).astype(BF16)
        db_ref[...] = dbv
        for c in range(nc):
            sl = pl.ds(c * fc, fc)
            da = _dot(dbv, wd_ref[sl, :], NT)
            gg = gg_ref[:, sl].astype(F32)
            uu = uu_ref[:, sl].astype(F32)
            s = _sig(gg)
            si = gg * s
            dgv = (da * uu * (s * (1.0 + gg * (1.0 - s)))).astype(BF16)
            duv = (da * si).astype(BF16)
            dg_ref[:, sl] = dgv
            du_ref[:, sl] = duv
            act_ref[:, sl] = (si * uu).astype(BF16)
        dh = _dot(dg_ref[...], wg_ref[...], NN) + _dot(du_ref[...], wu_ref[...], NN)

        @pl.when(i == 0)
        def _():
            dgam_ref[...] = jnp.zeros_like(dgam_ref)

        dgam_ref[...] += jnp.sum(dh * xhat, axis=0, keepdims=True)
        dxh = dh * gam
        dx_ref[...] = dov + r * (dxh - xhat * jnp.mean(dxh * xhat, axis=-1, keepdims=True))

    wspec = pl.BlockSpec((FF, D), lambda i: (0, 0), pipeline_mode=pl.Buffered(1))
    row_d = pl.BlockSpec((tm, D), lambda i: (i, 0))
    row_f = pl.BlockSpec((tm, FF), lambda i: (i, 0))
    return pl.pallas_call(
        body, name=name, grid=(T // tm,),
        in_specs=[row_d, _const_spec((1, D)), row_f, row_f, row_d, wspec, wspec, wspec],
        out_specs=[row_d, _const_spec((1, D)), row_f, row_f, row_f, row_d, row_d],
        out_shape=[jax.ShapeDtypeStruct((T, D), F32), jax.ShapeDtypeStruct((1, D), F32),
                   jax.ShapeDtypeStruct((T, FF), BF16), jax.ShapeDtypeStruct((T, FF), BF16),
                   jax.ShapeDtypeStruct((T, FF), BF16), jax.ShapeDtypeStruct((T, D), BF16),
                   jax.ShapeDtypeStruct((T, D), BF16)],
        compiler_params=_cp(("arbitrary",)),
    )(x, g, gg_all, uu_all, dout, wg_t, wu_t, wd)


def _ffn_bwd_pre(x, g, gg_all, uu_all, dout, wd, name):
    tm, fc = 512, 256
    nc = FF // fc

    def body(x_ref, g_ref, gg_ref, uu_ref, do_ref, wd_ref, dg_ref, du_ref, act_ref, h_ref, db_ref):
        xv = x_ref[...]
        r = lax.rsqrt(jnp.mean(xv * xv, axis=-1, keepdims=True) + EPS)
        h_ref[...] = (xv * r * g_ref[...]).astype(BF16)
        dbv = (0.5 * do_ref[...]).astype(BF16)
        db_ref[...] = dbv
        for c in range(nc):
            sl = pl.ds(c * fc, fc)
            da = _dot(dbv, wd_ref[sl, :], NT)
            gg = gg_ref[:, sl].astype(F32)
            uu = uu_ref[:, sl].astype(F32)
            s = _sig(gg)
            si = gg * s
            dg_ref[:, sl] = (da * uu * (s * (1.0 + gg * (1.0 - s)))).astype(BF16)
            du_ref[:, sl] = (da * si).astype(BF16)
            act_ref[:, sl] = (si * uu).astype(BF16)

    wspec = pl.BlockSpec((FF, D), lambda i: (0, 0), pipeline_mode=pl.Buffered(1))
    row_d = pl.BlockSpec((tm, D), lambda i: (i, 0))
    row_f = pl.BlockSpec((tm, FF), lambda i: (i, 0))
    return pl.pallas_call(
        body, name=name, grid=(T // tm,),
        in_specs=[row_d, _const_spec((1, D)), row_f, row_f, row_d, wspec],
        out_specs=[row_f, row_f, row_f, row_d, row_d],
        out_shape=[jax.ShapeDtypeStruct((T, FF), BF16), jax.ShapeDtypeStruct((T, FF), BF16),
                   jax.ShapeDtypeStruct((T, FF), BF16), jax.ShapeDtypeStruct((T, D), BF16),
                   jax.ShapeDtypeStruct((T, D), BF16)],
        compiler_params=_cp(("parallel",)),
    )(x, g, gg_all, uu_all, dout, wd)


def _ffn_bwd_dx(x, g, dgb, dub, dout, wg_t, wu_t, name):
    tm = 512

    def body(x_ref, g_ref, dg_ref, du_ref, do_ref, wg_ref, wu_ref, dx_ref, dgam_ref):
        i = pl.program_id(0)
        xv = x_ref[...]
        r = lax.rsqrt(jnp.mean(xv * xv, axis=-1, keepdims=True) + EPS)
        xhat = xv * r
        gam = g_ref[...]
        dh = _dot(dg_ref[...], wg_ref[...], NN) + _dot(du_ref[...], wu_ref[...], NN)

        @pl.when(i == 0)
        def _():
            dgam_ref[...] = jnp.zeros_like(dgam_ref)

        dgam_ref[...] += jnp.sum(dh * xhat, axis=0, keepdims=True)
        dxh = dh * gam
        dx_ref[...] = do_ref[...] + r * (dxh - xhat * jnp.mean(dxh * xhat, axis=-1, keepdims=True))

    wspec = pl.BlockSpec((FF, D), lambda i: (0, 0), pipeline_mode=pl.Buffered(1))
    row_d = pl.BlockSpec((tm, D), lambda i: (i, 0))
    row_f = pl.BlockSpec((tm, FF), lambda i: (i, 0))
    return pl.pallas_call(
        body, name=name, grid=(T // tm,),
        in_specs=[row_d, _const_spec((1, D)), row_f, row_f, row_d, wspec, wspec],
        out_specs=[row_d, _const_spec((1, D))],
        out_shape=[jax.ShapeDtypeStruct((T, D), F32), jax.ShapeDtypeStruct((1, D), F32)],
        compiler_params=_cp(("arbitrary",)),
    )(x, g, dgb, dub, dout, wg_t, wu_t)


def _wgrad(a, b, m, n, name):
    tm = m // 2 if m == FF else m
    return _mm(a, b, mode="tn", m=m, n=n, k=T, tm=tm, tn=n, tk=min(T, 2048), out_dtype=BF16, name=name)


PERM_TM = 512
DILS = tuple(d for _, d in GROUPS if d > 1)


def _perm_spec(dil, cols):
    return pl.BlockSpec((dil, PERM_TM // dil, cols), lambda i: (0, i, 0))


def _perm_shape(dil, cols, dtype):
    return jax.ShapeDtypeStruct((dil, T // dil, cols), dtype)


LANES = 128


def _tile_scratch(cols):
    return pltpu.VMEM((cols // LANES, PERM_TM, LANES), F32)


def _put_tile(tile, value):
    for c in range(tile.shape[0]):
        tile[c] = value[:, c * LANES:(c + 1) * LANES]


def _get_tile(tile):
    return jnp.concatenate([tile[c] for c in range(tile.shape[0])], axis=1)


def _store_perm(out_ref, tile, dil):
    for r in range(dil):
        for c in range(tile.shape[0]):
            out_ref[r, :, pl.ds(c * LANES, LANES)] = tile[c, pl.ds(r, PERM_TM // dil, stride=dil), :].astype(
                out_ref.dtype)


def _load_unperm(in_ref, tile, dil):
    for r in range(dil):
        for c in range(tile.shape[0]):
            tile[c, pl.ds(r, PERM_TM // dil, stride=dil), :] = in_ref[r, :, pl.ds(c * LANES, LANES)].astype(F32)


def _final_math(xv, gam, tgt, dx_ref, dgam_ref, loss_ref, i):
    r = lax.rsqrt(jnp.mean(xv * xv, axis=-1, keepdims=True) + EPS)
    xhat = xv * r
    err = xhat * gam - tgt
    part = 0.5 * jnp.sum(jnp.mean(err * err, axis=-1, keepdims=True), axis=0, keepdims=True)
    dy = err * (1.0 / D)

    @pl.when(i == 0)
    def _():
        dgam_ref[...] = jnp.zeros_like(dgam_ref)
        loss_ref[...] = jnp.zeros_like(loss_ref)

    dgam_ref[...] += jnp.sum(dy * xhat, axis=0, keepdims=True)
    loss_ref[...] += jnp.broadcast_to(part, loss_ref.shape)
    dxh = dy * gam
    dx_ref[...] = r * (dxh - xhat * jnp.mean(dxh * xhat, axis=-1, keepdims=True))


def _rms_bwd(x, g, dhs, dres, name):
    tm = PERM_TM
    dils = [d for _, d in GROUPS]
    nh = len(dhs)
    assert nh == len(dils)

    def body(*refs):
        x_ref, g_ref = refs[:2]
        dh_refs = refs[2:2 + nh]
        dr_ref, dx_ref, dgam_ref, tile = refs[2 + nh:]
        i = pl.program_id(0)
        xv = x_ref[...]
        r = lax.rsqrt(jnp.mean(xv * xv, axis=-1, keepdims=True) + EPS)
        xhat = xv * r
        gam = g_ref[...]
        dh = None
        for dil, ref in zip(dils, dh_refs):
            if dil == 1:
                part = ref[...]
            else:
                _load_unperm(ref, tile, dil)
                part = _get_tile(tile)
            dh = part if dh is None else dh + part

        @pl.when(i == 0)
        def _():
            dgam_ref[...] = jnp.zeros_like(dgam_ref)

        dgam_ref[...] += jnp.sum(dh * xhat, axis=0, keepdims=True)
        dxh = dh * gam
        dx_ref[...] = dr_ref[...] + r * (dxh - xhat * jnp.mean(dxh * xhat, axis=-1, keepdims=True))

    row_d = pl.BlockSpec((tm, D), lambda i: (i, 0))
    dh_specs = [row_d if d == 1 else _perm_spec(d, D) for d in dils]
    dh_args = [a if d == 1 else a.reshape(d, T // d, D) for d, a in zip(dils, dhs)]
    return pl.pallas_call(
        body, name=name, grid=(T // tm,),
        in_specs=[row_d, _const_spec((1, D))] + dh_specs + [row_d],
        out_specs=[row_d, _const_spec((1, D))],
        out_shape=[jax.ShapeDtypeStruct((T, D), F32), jax.ShapeDtypeStruct((1, D), F32)],
        scratch_shapes=[_tile_scratch(D)],
        compiler_params=_cp(("arbitrary",)),
    )(x, g, *dh_args, dres)


CONV_TM = 256
CONV_HALO = 32
CONV_RB = 16


def _glu(ab):
    ab = ab.astype(F32)
    return ab[:, :D] * _sig(ab[:, D:])


def _ln_stats(z1):
    mu = jnp.mean(z1, axis=-1, keepdims=True)
    zc = z1 - mu
    rstd = lax.rsqrt(jnp.mean(zc * zc, axis=-1, keepdims=True) + EPS)
    return zc * rstd, rstd


def _fill_shifts(zs):
    n = zs.shape[1] - 8
    for s in range(1, 8):
        zs[s, pl.ds(0, n), :] = zs[0, pl.ds(s, n), :]


def _shifted(zs, start, rows):
    q, s = divmod(start, 8)
    return zs[s, pl.ds(8 * q, rows), :]


def _conv_fwd(ab, kern, dwb, lng, lnb, name, guest=None):
    tm, hl, rb = CONV_TM, CONV_HALO, CONV_RB
    off = hl - (CONV_W - 1)
    if guest is not None:
        g_a, g_b, g_blocks, g_rows = guest
        g_nblk = len(g_blocks)

    def body(ab_ref, abh_ref, k_ref, dwb_ref, lng_ref, lnb_ref, *rest):
        if guest is not None:
            ga_ref, gb_refs, rest = rest[0], rest[1:1 + g_nblk], rest[1 + g_nblk:]
            z1_ref, z3_ref, go_ref, zs = rest
            for q, gb_ref in enumerate(gb_refs):
                go_ref[:, pl.ds(q * g_rows, g_rows)] = _dot(ga_ref[...], gb_ref[...], NT).astype(BF16)
        else:
            z1_ref, z3_ref, zs = rest
        i = pl.program_id(0)
        zs[0, pl.ds(0, hl), :] = jnp.where(i > 0, _glu(abh_ref[...]), 0.0)
        zs[0, pl.ds(hl, tm), :] = _glu(ab_ref[...])
        _fill_shifts(zs)
        for b in range(tm // rb):
            acc = jnp.zeros((rb, D), F32)
            for j in range(CONV_W):
                acc = acc + _shifted(zs, b * rb + off + j, rb) * k_ref[pl.ds(j, 1), :]
            z1 = acc + dwb_ref[...]
            z1_ref[pl.ds(b * rb, rb), :] = z1
            zn, _ = _ln_stats(z1)
            z2 = zn * lng_ref[...] + lnb_ref[...]
            z3_ref[pl.ds(b * rb, rb), :] = (z2 * _sig(z2)).astype(BF16)

    row = pl.BlockSpec((tm, D), lambda i: (i, 0))
    g_specs, g_args, g_ospecs, g_oshapes = [], [], [], []
    if guest is not None:
        kdim = g_a.shape[1]
        g_specs = [pl.BlockSpec((tm, kdim), lambda i: (i, 0))]
        g_specs += [pl.BlockSpec((g_rows, kdim), lambda i, q=q: (q, 0), pipeline_mode=pl.Buffered(1))
                    for q in g_blocks]
        g_args = [g_a] + [g_b] * g_nblk
        g_ospecs = [pl.BlockSpec((tm, g_nblk * g_rows), lambda i: (i, 0))]
        g_oshapes = [jax.ShapeDtypeStruct((T, g_nblk * g_rows), BF16)]
    return pl.pallas_call(
        body, name=name, grid=(T // tm,),
        in_specs=[pl.BlockSpec((tm, 2 * D), lambda i: (i, 0)),
                  pl.BlockSpec((hl, 2 * D), lambda i: (jnp.maximum(i * (tm // hl) - 1, 0), 0)),
                  _const_spec((32, D)), _const_spec((1, D)), _const_spec((1, D)), _const_spec((1, D))] + g_specs,
        out_specs=[row, row] + g_ospecs,
        out_shape=[jax.ShapeDtypeStruct((T, D), F32), jax.ShapeDtypeStruct((T, D), BF16)] + g_oshapes,
        scratch_shapes=[pltpu.VMEM((8, hl + tm, D), F32)],
        compiler_params=_cp(("parallel",)),
    )(ab, ab, kern, dwb, lng, lnb, *g_args)


GUEST_TM = 256


def _conv_bwd(dz3, z1, ab, kern, lng, lnb, name, guest_lhs=(), guest_rhs=None):
    tm, hl, rb = CONV_TM, CONV_HALO, CONV_RB
    off = hl - (CONV_W - 1)
    nsteps = T // tm
    ng = len(guest_lhs)
    gblocks = [a.shape[1] // GUEST_TM for a in guest_lhs]
    assert all(gb <= nsteps and gb * GUEST_TM == a.shape[1] for gb, a in zip(gblocks, guest_lhs))

    def ln_bwd(dz3v, z1v, lngv, lnbv):
        zn, rstd = _ln_stats(z1v)
        z2 = zn * lngv + lnbv
        s = _sig(z2)
        dz2 = dz3v * (s * (1.0 + z2 * (1.0 - s)))
        dzn = dz2 * lngv
        dz1 = rstd * (dzn - jnp.mean(dzn, axis=-1, keepdims=True)
                      - zn * jnp.mean(dzn * zn, axis=-1, keepdims=True))
        return dz1, dz2, zn

    def body(dz3_ref, dz3h_ref, z1_ref, z1h_ref, ab_ref, abh_ref, k_ref, lng_ref, lnb_ref, *rest):
        g_in, rest = rest[:ng + (1 if ng else 0)], rest[ng + (1 if ng else 0):]
        dab_ref, dk_ref, dvec_ref = rest[:3]
        g_out, (zs, dzs) = rest[3:3 + ng], rest[3 + ng:]
        i = pl.program_id(0)
        lngv, lnbv = lng_ref[...], lnb_ref[...]

        for a_ref, o_ref, gb in zip(g_in[:ng], g_out, gblocks):
            @pl.when(i < gb)
            def _(a_ref=a_ref, o_ref=o_ref):
                o_ref[...] = _dot(a_ref[...], g_in[ng][...], TN).astype(BF16)

        @pl.when(i == 0)
        def _():
            dk_ref[...] = jnp.zeros_like(dk_ref)
            dvec_ref[...] = jnp.zeros_like(dvec_ref)

        dz1, dz2, zn = ln_bwd(dz3_ref[...].astype(F32), z1_ref[...], lngv, lnbv)
        dvec_ref[pl.ds(0, 1), :] += jnp.sum(dz1, axis=0, keepdims=True)
        dvec_ref[pl.ds(1, 1), :] += jnp.sum(dz2 * zn, axis=0, keepdims=True)
        dvec_ref[pl.ds(2, 1), :] += jnp.sum(dz2, axis=0, keepdims=True)
        dzs[0, pl.ds(0, tm), :] = dz1
        dz1h, _, _ = ln_bwd(dz3h_ref[...].astype(F32), z1h_ref[...], lngv, lnbv)
        dzs[0, pl.ds(tm, hl), :] = jnp.where(i < nsteps - 1, dz1h, 0.0)
        _fill_shifts(dzs)
        zs[0, pl.ds(0, hl), :] = jnp.where(i > 0, _glu(abh_ref[...]), 0.0)
        zs[0, pl.ds(hl, tm), :] = _glu(ab_ref[...])
        _fill_shifts(zs)

        for j in range(CONV_W):
            tot = jnp.zeros((rb, D), F32)
            for b in range(tm // rb):
                tot = tot + dzs[0, pl.ds(b * rb, rb), :] * _shifted(zs, b * rb + off + j, rb)
            dk_ref[pl.ds(j, 1), :] += jnp.sum(tot, axis=0, keepdims=True)

        for b in range(tm // rb):
            acc = jnp.zeros((rb, D), F32)
            for j in range(CONV_W):
                acc = acc + _shifted(dzs, b * rb + (CONV_W - 1) - j, rb) * k_ref[pl.ds(j, 1), :]
            av = ab_ref[pl.ds(b * rb, rb), pl.ds(0, D)].astype(F32)
            sb = _sig(ab_ref[pl.ds(b * rb, rb), pl.ds(D, D)].astype(F32))
            dab_ref[pl.ds(b * rb, rb), pl.ds(0, D)] = (acc * sb).astype(BF16)
            dab_ref[pl.ds(b * rb, rb), pl.ds(D, D)] = (acc * av * sb * (1.0 - sb)).astype(BF16)

    row = pl.BlockSpec((tm, D), lambda i: (i, 0))
    nxt = pl.BlockSpec((hl, D), lambda i: (jnp.minimum((i + 1) * (tm // hl), T // hl - 1), 0))
    g_specs, g_args, g_ospecs, g_oshapes = [], [], [], []
    for a, gb in zip(guest_lhs, gblocks):
        g_specs.append(pl.BlockSpec((T, GUEST_TM), lambda i, gb=gb: (0, jnp.minimum(i, gb - 1))))
        g_args.append(a)
        g_ospecs.append(pl.BlockSpec((GUEST_TM, guest_rhs.shape[1]), lambda i, gb=gb: (jnp.minimum(i, gb - 1), 0)))
        g_oshapes.append(jax.ShapeDtypeStruct((a.shape[1], guest_rhs.shape[1]), BF16))
    if ng:
        g_specs.append(pl.BlockSpec(guest_rhs.shape, lambda i: (0, 0), pipeline_mode=pl.Buffered(1)))
        g_args.append(guest_rhs)
    return pl.pallas_call(
        body, name=name, grid=(nsteps,),
        in_specs=[row, nxt, row, nxt,
                  pl.BlockSpec((tm, 2 * D), lambda i: (i, 0)),
                  pl.BlockSpec((hl, 2 * D), lambda i: (jnp.maximum(i * (tm // hl) - 1, 0), 0)),
                  _const_spec((32, D)), _const_spec((1, D)), _const_spec((1, D))] + g_specs,
        out_specs=[pl.BlockSpec((tm, 2 * D), lambda i: (i, 0)), _const_spec((32, D)), _const_spec((8, D))]
        + g_ospecs,
        out_shape=[jax.ShapeDtypeStruct((T, 2 * D), BF16), jax.ShapeDtypeStruct((32, D), F32),
                   jax.ShapeDtypeStruct((8, D), F32)] + g_oshapes,
        scratch_shapes=[pltpu.VMEM((8, hl + tm, D), F32), pltpu.VMEM((8, tm + hl, D), F32)],
        compiler_params=_cp(("arbitrary",)),
    )(dz3, dz3, z1, z1, ab, ab, kern, lng, lnb, *g_args)


def _alibi_slopes():
    h = np.arange(1, 3 * NHG + 1, dtype=np.float32)
    return np.power(np.float32(2.0), -8.0 * h / np.float32(3 * NHG)).astype(np.float32)


def _band_bias(gi):
    _, dil = GROUPS[gi]
    slopes = _alibi_slopes()[gi * NHG:(gi + 1) * NHG]
    qi = np.arange(BLK)[:, None]
    ki = np.arange(2 * BLK)[None, :]
    steps = BLK + qi - ki
    band = (steps >= 0) & (steps <= BLK)
    bias = -slopes[:, None, None] * (dil * steps).astype(np.float32)[None]
    return jnp.asarray(np.where(band[None], bias, np.float32(NEG)).astype(np.float32))


QB_FWD = 8
QB_BWD = 32


def _attn_specs(qb, c0):
    prev = lambda n: jnp.maximum(n * qb - 1, 0)
    return [pl.BlockSpec((qb * BLK, HEAD), lambda h, n: (n, c0 + h)),
            pl.BlockSpec((BLK, HEAD), lambda h, n: (prev(n), c0 + NHG + h)),
            pl.BlockSpec((qb * BLK, HEAD), lambda h, n: (n, c0 + NHG + h)),
            pl.BlockSpec((BLK, HEAD), lambda h, n: (prev(n), c0 + 2 * NHG + h)),
            pl.BlockSpec((qb * BLK, HEAD), lambda h, n: (n, c0 + 2 * NHG + h)),
            pl.BlockSpec((None, BLK, 2 * BLK), lambda h, n: (h, 0, 0))]


def _scores(q, kcat, bias, blk, seg):
    s = _dot(q, kcat, NT) * (HEAD ** -0.5) + bias
    col = lax.broadcasted_iota(jnp.int32, s.shape, 1)
    first = (blk % seg) == 0
    return jnp.where(jnp.logical_and(first, col < BLK), NEG, s)


def _attn_fwd(qkv, gi, name, col0=0):
    seg = (T // GROUPS[gi][1]) // BLK

    qb = min(QB_FWD, T // BLK)

    def body(q_ref, kp_ref, kc_ref, vp_ref, vc_ref, bias_ref, o_ref, l_ref):
        n = pl.program_id(0)
        for h in range(NHG):
            cols = pl.ds(h * HEAD, HEAD)
            kwin = jnp.concatenate([kp_ref[:, cols], kc_ref[:, cols]], axis=0)
            vwin = jnp.concatenate([vp_ref[:, cols], vc_ref[:, cols]], axis=0)
            bias = bias_ref[h]
            for b in range(qb):
                rows = pl.ds(b * BLK, BLK)
                s = _scores(q_ref[rows, cols], kwin[b * BLK:(b + 2) * BLK], bias, n * qb + b, seg)
                mx = jnp.max(s, axis=-1, keepdims=True)
                p = jnp.exp(s - mx)
                den = jnp.sum(p, axis=-1, keepdims=True)
                o_ref[rows, cols] = (_dot(p.astype(BF16), vwin[b * BLK:(b + 2) * BLK], NN) / den).astype(BF16)
                l_ref[rows, cols] = jnp.broadcast_to(mx + jnp.log(den), (BLK, HEAD))

    prev = lambda n: jnp.maximum(n * qb - 1, 0)
    c0 = col0 // AW
    cur = lambda part: pl.BlockSpec((qb * BLK, AW), lambda n: (n, part))
    halo = lambda part: pl.BlockSpec((BLK, AW), lambda n: (prev(n), part))
    return pl.pallas_call(
        body, name=name, grid=(T // (qb * BLK),),
        in_specs=[cur(c0), halo(c0 + 1), cur(c0 + 1), halo(c0 + 2), cur(c0 + 2),
                  _const_spec((NHG, BLK, 2 * BLK))],
        out_specs=[cur(0), cur(0)],
        out_shape=[jax.ShapeDtypeStruct((T, AW), BF16), jax.ShapeDtypeStruct((T, AW), F32)],
        compiler_params=_cp(("parallel",)),
    )(qkv, qkv, qkv, qkv, qkv, _band_bias(gi))


def _attn_bwd(qkv, dob, lse, delta, gi, name, col0=0):
    seg = (T // GROUPS[gi][1]) // BLK
    qb = min(QB_BWD, T // BLK)
    nb = T // (qb * BLK)
    scale = HEAD ** -0.5

    def body(q_ref, kp_ref, kc_ref, vp_ref, vc_ref, bias_ref, do_ref, l_ref, dl_ref, out_ref, dk_acc, dv_acc):
        n = pl.program_id(1)
        kwin = jnp.concatenate([kp_ref[...], kc_ref[...]], axis=0)
        vwin = jnp.concatenate([vp_ref[...], vc_ref[...]], axis=0)
        bias = bias_ref[...]
        dks, dvs = [], []
        for b in range(qb):
            rows = pl.ds(b * BLK, BLK)
            q = q_ref[rows, :]
            kcat = kwin[b * BLK:(b + 2) * BLK]
            s = _scores(q, kcat, bias, n * qb + b, seg)
            p = jnp.exp(s - l_ref[rows, pl.ds(0, 1)])
            dov = do_ref[rows, :]
            dvs.append(_dot(p.astype(BF16), dov, TN))
            dp = _dot(dov, vwin[b * BLK:(b + 2) * BLK], NT)
            dsb = (p * (dp - dl_ref[rows, pl.ds(0, 1)]) * scale).astype(BF16)
            row = pl.ds(pl.multiple_of((n * qb + b) * BLK, BLK), BLK)
            out_ref[0, row, :] = _dot(dsb, kcat, NN).astype(BF16)
            dks.append(_dot(dsb, q, TN))
        for b in range(qb):
            row = pl.ds(pl.multiple_of((n * qb + b) * BLK, BLK), BLK)
            if b + 1 < qb:
                dk_acc[row, :] = dks[b][BLK:] + dks[b + 1][:BLK]
                dv_acc[row, :] = dvs[b][BLK:] + dvs[b + 1][:BLK]
            else:
                dk_acc[row, :] = dks[b][BLK:]
                dv_acc[row, :] = dvs[b][BLK:]

        @pl.when(n > 0)
        def _():
            prow = pl.ds(pl.multiple_of((n * qb - 1) * BLK, BLK), BLK)
            dk_acc[prow, :] += dks[0][:BLK]
            dv_acc[prow, :] += dvs[0][:BLK]

        @pl.when(n == nb - 1)
        def _():
            out_ref[1] = dk_acc[...].astype(BF16)
            out_ref[2] = dv_acc[...].astype(BF16)

    oblk = pl.BlockSpec((qb * BLK, HEAD), lambda h, n: (n, h))
    return pl.pallas_call(
        body, name=name, grid=(NHG, nb),
        in_specs=_attn_specs(qb, col0 // HEAD) + [oblk, oblk, oblk],
        out_specs=pl.BlockSpec((3, T, HEAD), lambda h, n: (0, 0, h)),
        out_shape=jax.ShapeDtypeStruct((3, T, AW), BF16),
        scratch_shapes=[pltpu.VMEM((T, HEAD), F32), pltpu.VMEM((T, HEAD), F32)],
        compiler_params=_cp(("parallel", "arbitrary")),
    )(qkv, qkv, qkv, qkv, qkv, _band_bias(gi), dob, lse, delta)


def _merge(outs, lses, name):
    tm = PERM_TM
    dils = [d for _, d in GROUPS]
    ng = len(dils)

    def body(*refs):
        in_refs = refs[:2 * ng]
        ab_ref = refs[2 * ng]
        lse_refs = refs[2 * ng + 1:3 * ng + 1]
        tile = refs[-1]

        def token_order(ref, dil):
            if dil == 1:
                return ref[...].astype(F32)
            _load_unperm(ref, tile, dil)
            return _get_tile(tile)

        os = [token_order(in_refs[2 * i], d) for i, d in enumerate(dils)]
        ls = [token_order(in_refs[2 * i + 1], d) for i, d in enumerate(dils)]
        mx = jnp.maximum(jnp.maximum(ls[0], ls[1]), ls[2])
        es = [jnp.exp(v - mx) for v in ls]
        tot = es[0] + es[1] + es[2]
        att = (es[0] / tot) * os[0] + (es[1] / tot) * os[1] + (es[2] / tot) * os[2]
        ab_ref[...] = att.astype(BF16)
        lse = mx + jnp.log(tot)
        _put_tile(tile, lse)
        for dil, ref in zip(dils, lse_refs):
            if dil == 1:
                ref[...] = lse
            else:
                _store_perm(ref, tile, dil)

    row = pl.BlockSpec((tm, AW), lambda i: (i, 0))
    specs = [row if d == 1 else _perm_spec(d, AW) for d in dils]
    args = []
    for d, o, l in zip(dils, outs, lses):
        args += [o, l] if d == 1 else [o.reshape(d, T // d, AW), l.reshape(d, T // d, AW)]
    out = pl.pallas_call(
        body, name=name, grid=(T // tm,),
        in_specs=[sp for sp in specs for _ in range(2)], out_specs=[row] + specs,
        out_shape=[jax.ShapeDtypeStruct((T, AW), BF16)]
        + [jax.ShapeDtypeStruct((T, AW), F32) if d == 1 else _perm_shape(d, AW, F32) for d in dils],
        scratch_shapes=[_tile_scratch(AW)],
        compiler_params=_cp(("parallel",)),
    )(*args)
    return out[0], [o.reshape(T, AW) for o in out[1:]]


GATE_BLOCK0 = (IN_W - 2 * D) // (D // 2)


def _mix_out(z3b, attnb, gates, wc, wa_t, wo, x1, name):
    tm = 512

    def body(z_ref, a_ref, g_ref, wc_ref, wa_ref, wo_ref, x_ref, xo_ref, yc_ref, ya_ref, mx_ref):
        yc = _dot(z_ref[...], wc_ref[...], NN)
        ya = _dot(a_ref[...], wa_ref[...], NT)
        yc_ref[...] = yc.astype(BF16)
        ya_ref[...] = ya.astype(BF16)
        gv = g_ref[...].astype(F32)
        mixed = (_sig(gv[:, :D]) * yc + _sig(gv[:, D:]) * ya).astype(BF16)
        mx_ref[...] = mixed
        xo_ref[...] = x_ref[...] + _dot(mixed, wo_ref[...], NN)

    row = pl.BlockSpec((tm, D), lambda i: (i, 0))
    return pl.pallas_call(
        body, name=name, grid=(T // tm,),
        in_specs=[row, pl.BlockSpec((tm, AW), lambda i: (i, 0)), pl.BlockSpec((tm, 2 * D), lambda i: (i, 0)),
                  _const_spec((D, D)), _const_spec((D, AW)), _const_spec((D, D)), row],
        out_specs=[row, row, row, row],
        out_shape=[jax.ShapeDtypeStruct((T, D), F32), jax.ShapeDtypeStruct((T, D), BF16),
                   jax.ShapeDtypeStruct((T, D), BF16), jax.ShapeDtypeStruct((T, D), BF16)],
        compiler_params=_cp(("parallel",)),
    )(z3b, attnb, gates, wc, wa_t, wo, x1)


def _mix_out_bwd(dx2, gates, yc, ya, attn, wc, wa_t, wo, win_t, name):
    tm = PERM_TM
    dils = [d for _, d in GROUPS]
    ng = len(dils)

    def body(dx_ref, g_ref, yc_ref, ya_ref, at_ref, wc_ref, wa_ref, wo_ref, wg0_ref, wg1_ref, wg2_ref, wg3_ref,
             dg_ref, dyc_ref, dya_ref, dxb_ref, dz3_ref, dhg_ref, *rest):
        dat_refs, dl_refs, tile = rest[:ng], rest[ng:2 * ng], rest[-1]
        dxb = dx_ref[...].astype(BF16)
        dxb_ref[...] = dxb
        dmix = _dot(dxb, wo_ref[...], NT)
        gv = g_ref[...].astype(F32)
        sc = _sig(gv[:, :D])
        sa = _sig(gv[:, D:])
        ycv, yav = yc_ref[...].astype(F32), ya_ref[...].astype(F32)
        dgc = (dmix * ycv * sc * (1.0 - sc)).astype(BF16)
        dga = (dmix * yav * sa * (1.0 - sa)).astype(BF16)
        dg_ref[:, pl.ds(0, D)] = dgc
        dg_ref[:, pl.ds(D, D)] = dga
        half = D // 2
        dhg_ref[...] = (_dot(dgc[:, :half], wg0_ref[...], NN) + _dot(dgc[:, half:], wg1_ref[...], NN)
                        + _dot(dga[:, :half], wg2_ref[...], NN) + _dot(dga[:, half:], wg3_ref[...], NN))
        dyc = (dmix * sc).astype(BF16)
        dya = (dmix * sa).astype(BF16)
        dyc_ref[...] = dyc
        dya_ref[...] = dya
        dz3_ref[...] = _dot(dyc, wc_ref[...], NT).astype(BF16)
        dat = _dot(dya, wa_ref[...], NN)
        prod = dat * at_ref[...].astype(F32)
        delta = jnp.concatenate(
            [jnp.broadcast_to(jnp.sum(prod[:, h * HEAD:(h + 1) * HEAD], axis=-1, keepdims=True), (tm, HEAD))
             for h in range(NHG)], axis=1)
        for value, out_refs in ((dat, dat_refs), (delta, dl_refs)):
            _put_tile(tile, value)
            for dil, ref in zip(dils, out_refs):
                if dil == 1:
                    ref[...] = value.astype(ref.dtype)
                else:
                    _store_perm(ref, tile, dil)

    row = pl.BlockSpec((tm, D), lambda i: (i, 0))
    row2 = pl.BlockSpec((tm, 2 * D), lambda i: (i, 0))
    rowa = pl.BlockSpec((tm, AW), lambda i: (i, 0))
    aspecs = [rowa if d == 1 else _perm_spec(d, AW) for d in dils]

    def ashapes(dtype):
        return [jax.ShapeDtypeStruct((T, AW), dtype) if d == 1 else _perm_shape(d, AW, dtype) for d in dils]

    out = pl.pallas_call(
        body, name=name, grid=(T // tm,),
        in_specs=[row, row2, row, row, rowa, _const_spec((D, D)), _const_spec((D, AW)), _const_spec((D, D))]
        + [pl.BlockSpec((D // 2, D), lambda i, q=q: (GATE_BLOCK0 + q, 0), pipeline_mode=pl.Buffered(1))
           for q in range(4)],
        out_specs=[row2, row, row, row, row, row] + aspecs + aspecs,
        out_shape=[jax.ShapeDtypeStruct((T, 2 * D), BF16), jax.ShapeDtypeStruct((T, D), BF16),
                   jax.ShapeDtypeStruct((T, D), BF16), jax.ShapeDtypeStruct((T, D), BF16),
                   jax.ShapeDtypeStruct((T, D), BF16), jax.ShapeDtypeStruct((T, D), F32)]
        + ashapes(BF16) + ashapes(F32),
        scratch_shapes=[_tile_scratch(AW)],
        compiler_params=_cp(("parallel",)),
    )(dx2, gates, yc, ya, attn, wc, wa_t, wo, win_t, win_t, win_t, win_t)
    dats = [o.reshape(T, AW) for o in out[6:6 + ng]]
    deltas = [o.reshape(T, AW) for o in out[6 + ng:6 + 2 * ng]]
    return out[0], out[1], out[2], out[3], out[4], out[5], dats, deltas


def _peer(k):
    x, y, c = lax.axis_index("x"), lax.axis_index("y"), lax.axis_index("c")
    px = 1 - x if k & 4 else x
    py = 1 - y if k & 2 else y
    pc = 1 - c if k & 1 else c
    return (px, py, pc), 4 * px + 2 * py + pc


HBM_SPEC = pl.BlockSpec(memory_space=pltpu.HBM)
SEM_SPEC = pl.BlockSpec(memory_space=pltpu.SEMAPHORE)
EFFECT = pltpu.SideEffectType.DATAFLOW_SIDE_EFFECTING


def _my_place():
    return 4 * lax.axis_index("x") + 2 * lax.axis_index("y") + lax.axis_index("c")


def _tie(a, order_after, name):
    na = len(order_after)

    def body(*refs):
        del refs

    return pl.pallas_call(
        body, name=name, in_specs=[pl.BlockSpec(memory_space=pl.ANY)] * (1 + na),
        out_specs=pl.BlockSpec(memory_space=pl.ANY), out_shape=jax.ShapeDtypeStruct(a.shape, a.dtype),
        input_output_aliases={0: 0},
    )(a, *order_after)


def _prep_gather(ws, order_after, name):
    me = jnp.reshape(_my_place(), (1,)).astype(jnp.int32)
    n = len(ws)
    na = len(order_after)
    shapes = [((32, wv.shape[1]), F32) if wv.shape[0] == CONV_W else (wv.shape, BF16) for wv in ws]

    def body(me_ref, *refs):
        del me_ref
        ins, outs = refs[:n], refs[n + na:]
        for wv, i_ref, o_ref in zip(ws, ins, outs):
            if wv.shape[0] == CONV_W:
                o_ref[pl.ds(0, CONV_W), :] = i_ref[...]
                o_ref[pl.ds(CONV_W, 1), :] = jnp.zeros((1, wv.shape[1]), F32)
            else:
                o_ref[...] = i_ref[...].astype(BF16)

    grid_spec = pltpu.PrefetchScalarGridSpec(
        num_scalar_prefetch=1, grid=(1,),
        in_specs=[pl.BlockSpec(wv.shape, lambda i, m: (0, 0)) for wv in ws]
        + [pl.BlockSpec(memory_space=pl.ANY)] * na,
        out_specs=[pl.BlockSpec(shp, lambda i, m: (m[0], 0)) for shp, _ in shapes])
    return pl.pallas_call(
        body, name=name, grid_spec=grid_spec,
        out_shape=[jax.ShapeDtypeStruct((NDEV * shp[0], shp[1]), dt) for shp, dt in shapes],
        compiler_params=_cp(("arbitrary",)),
    )(me, *ws, *order_after)


GATHER_A = ((1, 0), (2, 0), (4, 0), (6, 0))
GATHER_B = ((1, 2), (1, 4), (1, 6))
GATHER_DIRECT = tuple((k, 0) for k in range(1, NDEV))


def _gather_start(lands, plan, order_after, name):
    n = len(lands)
    na = len(order_after)
    npl = len(plan)

    def body(*refs):
        land_refs = refs[:n]
        send, recv = refs[n + na], refs[n + na + 1]
        token = refs[-1]
        for w in range(n):
            rows = lands[w].shape[0] // NDEV
            for p, (k, j) in enumerate(plan):
                peer, _ = _peer(k)
                _, blk = _peer(j)
                part = land_refs[w].at[pl.ds(blk * rows, rows)]
                i = w * npl + p
                pltpu.make_async_remote_copy(src_ref=part, dst_ref=part, send_sem=send.at[i], recv_sem=recv.at[i],
                                             device_id=peer, device_id_type=MESH_ID).start()
        token[...] = jnp.zeros_like(token)

    nsem = n * npl
    bufs = [pltpu.with_memory_space_constraint(a, pltpu.HBM) for a in lands]
    out = pl.pallas_call(
        body, name=name,
        in_specs=[HBM_SPEC] * n + [pl.BlockSpec(memory_space=pl.ANY)] * na,
        out_specs=[SEM_SPEC, SEM_SPEC] + [HBM_SPEC] * n + [pl.BlockSpec(memory_space=pltpu.VMEM)],
        out_shape=[pltpu.SemaphoreType.DMA((nsem,)), pltpu.SemaphoreType.DMA((nsem,))]
        + [pltpu.HBM(a.shape, a.dtype) for a in bufs] + [jax.ShapeDtypeStruct((8, 128), F32)],
        input_output_aliases={i: 2 + i for i in range(n)},
        compiler_params=pltpu.CompilerParams(has_side_effects=EFFECT),
    )(*bufs, *order_after)
    return out[0], out[1], out[2:2 + n], out[-1]


def _gather_wait(started, plan, order_after, name):
    send, recv, lands, _ = started
    n = len(lands)
    na = len(order_after)
    npl = len(plan)

    def body(*refs):
        land_refs = refs[:n]
        send_ref, recv_ref = refs[n], refs[n + 1]
        for w in range(n):
            rows = lands[w].shape[0] // NDEV
            for p, (k, j) in enumerate(plan):
                peer, _ = _peer(k)
                _, blk = _peer(j)
                part = land_refs[w].at[pl.ds(blk * rows, rows)]
                i = w * npl + p
                cp = pltpu.make_async_remote_copy(src_ref=part, dst_ref=part, send_sem=send_ref.at[i],
                                                  recv_sem=recv_ref.at[i], device_id=peer, device_id_type=MESH_ID)
                cp.wait_send()
                cp.wait_recv()

    out = pl.pallas_call(
        body, name=name,
        in_specs=[HBM_SPEC] * n + [SEM_SPEC, SEM_SPEC] + [pl.BlockSpec(memory_space=pl.ANY)] * na,
        out_specs=[HBM_SPEC] * n,
        out_shape=[pltpu.HBM(a.shape, a.dtype) for a in lands],
        input_output_aliases={i: i for i in range(n)},
        compiler_params=pltpu.CompilerParams(has_side_effects=EFFECT),
    )(*lands, send, recv, *order_after)
    return list(out)


def _copy_ends(kind, src, land, me, plin, k):
    if kind == "scatter":
        rows = src.shape[0] // NDEV
        return src.at[pl.ds(plin * rows, rows)], land.at[k - 1]
    return src, land.at[me]


def _landing(kind, src):
    me = _my_place()
    if kind == "scatter":
        return lax.empty((NDEV - 1, src.shape[0] // NDEV) + src.shape[1:], src.dtype)
    land = lax.empty((NDEV,) + src.shape, src.dtype)
    return lax.dynamic_update_slice(land, src[None], (me,) + (0,) * src.ndim)


def _send_start(kinds, srcs, order_after, name):
    n = len(srcs)
    lands = [_landing(kd, s) for kd, s in zip(kinds, srcs)]
    na = len(order_after)

    def body(*refs):
        src_refs, land_refs = refs[:n], refs[n:2 * n]
        send, recv = refs[2 * n + na], refs[2 * n + na + 1]
        token = refs[-1]
        _, me = _peer(0)
        for w in range(n):
            for k in range(1, NDEV):
                peer, plin = _peer(k)
                s, d = _copy_ends(kinds[w], src_refs[w], land_refs[w], me, plin, k)
                i = w * (NDEV - 1) + k - 1
                pltpu.make_async_remote_copy(src_ref=s, dst_ref=d, send_sem=send.at[i], recv_sem=recv.at[i],
                                             device_id=peer, device_id_type=MESH_ID).start()
        token[...] = jnp.zeros_like(token)

    nsem = n * (NDEV - 1)
    bufs = [pltpu.with_memory_space_constraint(a, pltpu.HBM) for a in list(srcs) + lands]
    out = pl.pallas_call(
        body, name=name,
        in_specs=[HBM_SPEC] * (2 * n) + [pl.BlockSpec(memory_space=pl.ANY)] * na,
        out_specs=[SEM_SPEC, SEM_SPEC] + [HBM_SPEC] * (2 * n) + [pl.BlockSpec(memory_space=pltpu.VMEM)],
        out_shape=[pltpu.SemaphoreType.DMA((nsem,)), pltpu.SemaphoreType.DMA((nsem,))]
        + [pltpu.HBM(a.shape, a.dtype) for a in bufs] + [jax.ShapeDtypeStruct((8, 128), F32)],
        input_output_aliases={i: 2 + i for i in range(2 * n)},
        compiler_params=pltpu.CompilerParams(has_side_effects=EFFECT),
    )(*bufs, *order_after)
    return out[0], out[1], out[2:2 + n], out[2 + n:2 + 2 * n], out[-1]


def _send_wait(kinds, started, order_after, name):
    send, recv, srcs, lands, _ = started
    n = len(srcs)
    na = len(order_after)

    def body(*refs):
        src_refs, land_refs = refs[:n], refs[n:2 * n]
        send_ref, recv_ref = refs[2 * n], refs[2 * n + 1]
        _, me = _peer(0)
        for w in range(n):
            for k in range(1, NDEV):
                peer, plin = _peer(k)
                s, d = _copy_ends(kinds[w], src_refs[w], land_refs[w], me, plin, k)
                i = w * (NDEV - 1) + k - 1
                cp = pltpu.make_async_remote_copy(src_ref=s, dst_ref=d, send_sem=send_ref.at[i],
                                                  recv_sem=recv_ref.at[i], device_id=peer, device_id_type=MESH_ID)
                cp.wait_send()
                cp.wait_recv()

    bufs = list(srcs) + list(lands)
    out = pl.pallas_call(
        body, name=name,
        in_specs=[HBM_SPEC] * (2 * n) + [SEM_SPEC, SEM_SPEC] + [pl.BlockSpec(memory_space=pl.ANY)] * na,
        out_specs=[HBM_SPEC] * (2 * n),
        out_shape=[pltpu.HBM(a.shape, a.dtype) for a in bufs],
        input_output_aliases={i: i for i in range(2 * n)},
        compiler_params=pltpu.CompilerParams(has_side_effects=EFFECT),
    )(*bufs, send, recv, *order_after)
    return out[:n], out[n:]


def _gsum(own, land, name):
    rows, cols = own.shape
    tr = rows // 2 if rows * cols > 512 * 1024 and rows % 32 == 0 else rows

    def body(own_ref, l_ref, o_ref):
        tot = own_ref[...].astype(F32)
        for s in range(NDEV - 1):
            tot = tot + l_ref[s].astype(F32)
        o_ref[...] = tot

    return pl.pallas_call(
        body, name=name, grid=(rows // tr,),
        in_specs=[pl.BlockSpec((tr, cols), lambda i: (i, 0)),
                  pl.BlockSpec((NDEV - 1, tr, cols), lambda i: (0, i, 0))],
        out_specs=pl.BlockSpec((tr, cols), lambda i: (i, 0)),
        out_shape=jax.ShapeDtypeStruct((rows, cols), F32),
        compiler_params=_cp(("parallel",)),
    )(own, land)


def _adamw_math(w, g, m, v):
    m2 = B1 * m + (1.0 - B1) * g
    v2 = B2 * v + (1.0 - B2) * (g * g)
    m_hat = m2 / (1.0 - B1 ** STEP)
    v_hat = v2 / (1.0 - B2 ** STEP)
    delta = -LR * (m_hat / (jnp.sqrt(v_hat) + AEPS) + WD * w)
    return delta, m2, v2


def _adamw(w, g, m, v, name):
    rows, cols = w.shape
    tr = 256 if rows % 256 == 0 and rows > 256 else rows

    def body(w_ref, g_ref, m_ref, v_ref, d_ref, mo_ref, vo_ref):
        d, m2, v2 = _adamw_math(w_ref[...], g_ref[...], m_ref[...], v_ref[...])
        d_ref[...] = d
        mo_ref[...] = m2
        vo_ref[...] = v2

    blk = pl.BlockSpec((tr, cols), lambda i: (i, 0))
    return pl.pallas_call(
        body, name=name, grid=(rows // tr,), in_specs=[blk] * 4, out_specs=[blk] * 3,
        out_shape=[jax.ShapeDtypeStruct((rows, cols), F32)] * 3,
        compiler_params=_cp(("parallel",)),
    )(w, g, m, v)


UPD_TC = 256


def _update(src, land, w, m, v, name):
    rows, cols = land.shape[1:]
    tc = min(UPD_TC if rows > 512 else 2 * UPD_TC, cols)
    me = jnp.reshape(_my_place(), (1,)).astype(jnp.int32)

    def body(me_ref, own_ref, l_ref, w_ref, m_ref, v_ref, g_ref, d_ref, mo_ref, vo_ref):
        del me_ref
        g = own_ref[...].astype(F32)
        for s in range(NDEV - 1):
            g = g + l_ref[s].astype(F32)
        g_ref[...] = g
        d, m2, v2 = _adamw_math(w_ref[...], g, m_ref[...], v_ref[...])
        d_ref[...] = d
        mo_ref[...] = m2
        vo_ref[...] = v2

    wblk = pl.BlockSpec((rows, tc), lambda j, p: (0, j))
    grid_spec = pltpu.PrefetchScalarGridSpec(
        num_scalar_prefetch=1, grid=(cols // tc,),
        in_specs=[pl.BlockSpec((rows, tc), lambda j, p: (p[0], j)),
                  pl.BlockSpec((NDEV - 1, rows, tc), lambda j, p: (0, 0, j)), wblk, wblk, wblk],
        out_specs=[wblk] * 4)
    return pl.pallas_call(
        body, name=name, grid_spec=grid_spec, out_shape=[jax.ShapeDtypeStruct((rows, cols), F32)] * 4,
        compiler_params=_cp(("parallel",)),
    )(me, src, land, w, m, v)


def _small_update(vland, w8, m8, v8, name):
    def body(l_ref, w_ref, m_ref, v_ref, g_ref, d_ref, mo_ref, vo_ref):
        g = l_ref[0]
        for s in range(1, NDEV):
            g = g + l_ref[s]
        g_ref[...] = g
        d, m2, v2 = _adamw_math(w_ref[...], g, m_ref[...], v_ref[...])
        d_ref[...] = d
        mo_ref[...] = m2
        vo_ref[...] = v2

    return pl.pallas_call(
        body, name=name, out_shape=[jax.ShapeDtypeStruct((8, D), F32)] * 4,
        compiler_params=_cp(None),
    )(vland, w8, m8, v8)


def kernel(x, ffn1_norm, ffn1_w_gate, ffn1_w_up, ffn1_w_down, mix_norm, w_in, conv_dw_kernel, conv_dw_bias, conv_ln_gain, conv_ln_bias, conv_w_out, attn_w_out, w_o, ffn2_norm, ffn2_w_gate, ffn2_w_up, ffn2_w_down, final_norm, loss_target, m_ffn1_norm, m_ffn1_w_gate, m_ffn1_w_up, m_ffn1_w_down, m_mix_norm, m_w_in, m_conv_dw_kernel, m_conv_dw_bias, m_conv_ln_gain, m_conv_ln_bias, m_conv_w_out, m_attn_w_out, m_w_o, m_ffn2_norm, m_ffn2_w_gate, m_ffn2_w_up, m_ffn2_w_down, m_final_norm, v_ffn1_norm, v_ffn1_w_gate, v_ffn1_w_up, v_ffn1_w_down, v_mix_norm, v_w_in, v_conv_dw_kernel, v_conv_dw_bias, v_conv_ln_gain, v_conv_ln_bias, v_conv_w_out, v_attn_w_out, v_w_o, v_ffn2_norm, v_ffn2_w_gate, v_ffn2_w_up, v_ffn2_w_down, v_final_norm):
    names = ["ffn1_norm", "ffn1_w_gate", "ffn1_w_up", "ffn1_w_down", "mix_norm", "w_in", "conv_dw_kernel",
             "conv_dw_bias", "conv_ln_gain", "conv_ln_bias", "conv_w_out", "attn_w_out", "w_o", "ffn2_norm",
             "ffn2_w_gate", "ffn2_w_up", "ffn2_w_down", "final_norm"]
    w = dict(ffn1_norm=ffn1_norm, ffn1_w_gate=ffn1_w_gate, ffn1_w_up=ffn1_w_up, ffn1_w_down=ffn1_w_down, mix_norm=mix_norm, w_in=w_in, conv_dw_kernel=conv_dw_kernel, conv_dw_bias=conv_dw_bias, conv_ln_gain=conv_ln_gain, conv_ln_bias=conv_ln_bias, conv_w_out=conv_w_out, attn_w_out=attn_w_out, w_o=w_o, ffn2_norm=ffn2_norm, ffn2_w_gate=ffn2_w_gate, ffn2_w_up=ffn2_w_up, ffn2_w_down=ffn2_w_down, final_norm=final_norm)
    mo = dict(ffn1_norm=m_ffn1_norm, ffn1_w_gate=m_ffn1_w_gate, ffn1_w_up=m_ffn1_w_up, ffn1_w_down=m_ffn1_w_down, mix_norm=m_mix_norm, w_in=m_w_in, conv_dw_kernel=m_conv_dw_kernel, conv_dw_bias=m_conv_dw_bias, conv_ln_gain=m_conv_ln_gain, conv_ln_bias=m_conv_ln_bias, conv_w_out=m_conv_w_out, attn_w_out=m_attn_w_out, w_o=m_w_o, ffn2_norm=m_ffn2_norm, ffn2_w_gate=m_ffn2_w_gate, ffn2_w_up=m_ffn2_w_up, ffn2_w_down=m_ffn2_w_down, final_norm=m_final_norm)
    vo = dict(ffn1_norm=v_ffn1_norm, ffn1_w_gate=v_ffn1_w_gate, ffn1_w_up=v_ffn1_w_up, ffn1_w_down=v_ffn1_w_down, mix_norm=v_mix_norm, w_in=v_w_in, conv_dw_kernel=v_conv_dw_kernel, conv_dw_bias=v_conv_dw_bias, conv_ln_gain=v_conv_ln_gain, conv_ln_bias=v_conv_ln_bias, conv_w_out=v_conv_w_out, attn_w_out=v_attn_w_out, w_o=v_w_o, ffn2_norm=v_ffn2_norm, ffn2_w_gate=v_ffn2_w_gate, ffn2_w_up=v_ffn2_w_up, ffn2_w_down=v_ffn2_w_down, final_norm=v_final_norm)
    col_sharded = ("ffn1_w_gate", "ffn1_w_up", "w_in", "attn_w_out", "ffn2_w_gate", "ffn2_w_up")
    row_sharded = ("ffn1_w_down", "conv_w_out", "w_o", "ffn2_w_down")
    small = ("ffn1_norm", "mix_norm", "ffn2_norm", "final_norm", "conv_dw_bias", "conv_ln_gain", "conv_ln_bias")

    def landing_view(a, n):
        return jnp.transpose(a[0]) if n in col_sharded else a[0]

    def own_view(a, n):
        return jnp.transpose(a)[None] if n in col_sharded else a[None]

    ag_groups = (("ffn1_w_gate", "ffn1_w_up", "ffn1_w_down"),
                 ("w_in", "attn_w_out", "conv_w_out", "w_o", "conv_dw_kernel"),
                 ("ffn2_w_gate", "ffn2_w_up", "ffn2_w_down"))
    ag, order = [], []
    for gi, grp in enumerate(ag_groups):
        lands = _prep_gather([landing_view(w[n], n) for n in grp], order, f"gather_prep{gi}")
        st = _gather_start(lands, GATHER_DIRECT if gi == 2 else GATHER_A, [], f"gather_a_start{gi}")
        ag.append(st)
        order = [st[3]]

    def chips_in(gi, after):
        lands = _gather_wait(ag[gi], GATHER_A, after, f"gather_a_wait{gi}")
        return _gather_start(lands, GATHER_B, [], f"gather_b_start{gi}")

    def all_in(gi, st, after):
        return _gather_wait(st, GATHER_B, after, f"gather_b_wait{gi}")

    x0 = x[0]
    tgt = loss_target[0]
    gf = final_norm.reshape(1, D)

    wg1, wu1, wd1 = all_in(0, chips_in(0, [ag[2][3]]), [])
    x1, gg1, uu1, h2p = _ffn_fwd(x0, ffn1_norm, wg1, wu1, wd1, "ffn1_fwd", next_gain=mix_norm)
    h2 = h2p[0]
    win_t, wa_t, wc, wo, kern_blocks = all_in(1, chips_in(1, [x1]), [])
    kern = kern_blocks.reshape(NDEV, 32, D // NDEV).transpose(1, 0, 2).reshape(32, D)
    ptm = min(T, 2048)
    ab = _mm(h2, win_t, mode="nt", m=T, n=2 * D, k=D, tm=ptm, tn=512, tk=D, out_dtype=BF16, name="proj_conv")
    z1, z3b, gates = _conv_fwd(ab, kern, conv_dw_bias, conv_ln_gain, conv_ln_bias, "conv_fwd",
                               guest=(h2, win_t, (13, 14, 15, 16, 4, 7, 10), 512))
    qkv, qkv_col0 = [gates], [2 * D]
    for gi in range(1, len(GROUPS)):
        qkv.append(_mm(h2p[gi], win_t, mode="nt", m=T, n=3 * AW, k=D, tm=ptm, tn=AW, tk=D, out_dtype=BF16,
                       b_map=lambda i, j, kk, gi=gi: (4 + gi + 3 * j, 0), name=f"proj_qkv{gi}"))
        qkv_col0.append(0)
    outs, lses = [], []
    for gi, (_, dil) in enumerate(GROUPS):
        o, l = _attn_fwd(qkv[gi], gi, f"attn_fwd{gi}", col0=qkv_col0[gi])
        outs.append(o)
        lses.append(l)
    attnb, lse = _merge(outs, lses, "attn_merge")
    x2, yc, ya, mixedb = _mix_out(z3b, attnb, gates, wc, wa_t, wo, x1, "mix_out_fwd")
    wg2, wu2, wd2 = _gather_wait(ag[2], GATHER_DIRECT, [x2], "gather_a_wait2")
    gg2, uu2, dx3, dgf, loss_part = _ffn_fwd(x2, ffn2_norm, wg2, wu2, wd2, "ffn2_fwd", loss_of=(gf, tgt))

    dx2, dg3, dgb, dub, actb, hb, dob = _ffn_bwd(x2, ffn2_norm, gg2, uu2, dx3, wg2, wu2, wd2, "ffn2_bwd")
    grads = {}
    grads["ffn2_w_down"] = _wgrad(actb, dob, FF, D, "ffn2_dwd")
    rs_groups = [("ffn2_w_gate", "ffn2_w_up", "ffn2_w_down"),
                 ("attn_w_out", "conv_w_out", "w_o", "conv_dw_kernel"),
                 ("w_in",),
                 ("ffn1_w_gate",), ("ffn1_w_up",), ("ffn1_w_down",), ()]
    last = len(rs_groups) - 1
    rs = []

    dgates, dycb, dyab, dx2b, dz3, dh_gates, dattnb, delta = _mix_out_bwd(dx2, gates, yc, ya, attnb, wc, wa_t, wo,
                                                                          win_t, "mix_out_bwd")
    grads["w_o"] = _wgrad(mixedb, dx2b, D, D, "dw_o")
    grads["conv_w_out"] = _wgrad(z3b, dycb, D, D, "dw_conv_out")
    grads["attn_w_out"] = _wgrad(dyab, attnb, D, AW, "dw_attn_out")
    dab, dkern, dvec, grads["ffn2_w_gate"], grads["ffn2_w_up"] = _conv_bwd(
        dz3, z1, ab, kern, conv_ln_gain, conv_ln_bias, "conv_bwd", guest_lhs=(dgb, dub), guest_rhs=hb)
    grads["conv_dw_kernel"] = dkern.reshape(32, NDEV, D // NDEV).transpose(1, 0, 2).reshape(NDEV * 32, D // NDEV)
    rs.append(_send_start(["scatter"] * 3, [grads[n] for n in rs_groups[0]], [], "scatter_start0"))
    rs.append(_send_start(["scatter"] * 4, [grads[n] for n in rs_groups[1]], [rs[0][4]], "scatter_start1"))
    dattnb = [_tie(a, [rs[1][4]], f"tie_after_scatter1_{i}") for i, a in enumerate(dattnb)]

    dqkv = []
    for gi, (_, dil) in enumerate(GROUPS):
        dq3 = _attn_bwd(qkv[gi], dattnb[gi], lse[gi], delta[gi], gi, f"attn_bwd{gi}", col0=qkv_col0[gi])
        dqkv.append(dq3.reshape(3 * T, AW))

    wtk = min(T, 2048)
    dwin = _mm(dab, h2, mode="tn", m=2 * D, n=D, k=T, tm=2 * D, tn=D, tk=wtk, out_dtype=BF16, out_rows=IN_W,
               name="dw_in_conv")
    dwin = _mm(dgates, h2, mode="tn", m=2 * D, n=D, k=T, tm=512, tn=D, tk=wtk, out_dtype=BF16, out_rows=IN_W,
               o_map=lambda i, j, kk: (13 + i, 0), passthru=dwin, name="dw_in_gates")
    for gi in range(3):
        dwin = _mm(dqkv[gi], h2p[gi], mode="tn", m=3 * AW, n=D, k=T, tm=AW, tn=D, tk=wtk, out_dtype=BF16,
                   out_rows=IN_W, a_map=lambda i, j, kk: (i * (T // wtk) + kk, 0),
                   o_map=lambda i, j, kk, gi=gi: (4 + gi + 3 * i, 0), passthru=dwin, name=f"dw_in_qkv{gi}")
    grads["w_in"] = dwin
    rs.append(_send_start(["scatter"], [dwin], [rs[1][4]], "scatter_start2"))
    dab = _tie(dab, [rs[2][4]], "tie_after_scatter2")

    nrow = T // 1024
    dh = _mm(dab, win_t, mode="nn", m=T, n=D, k=2 * D, tm=1024, tn=D, tk=2 * D, out_dtype=F32, init=dh_gates,
             name="dproj_conv")
    dhs = []
    for gi, (_, dil) in enumerate(GROUPS):
        part = _mm(dqkv[gi], win_t, mode="nn", m=T, n=D, k=3 * AW, tm=1024, tn=D, tk=AW,
                   out_dtype=F32 if gi == 0 else BF16,
                   a_map=lambda i, j, kk: (kk * nrow + i, 0), b_map=lambda i, j, kk, gi=gi: (4 + gi + 3 * kk, 0),
                   init=dh if gi == 0 else None, name=f"dproj_qkv{gi}")
        dhs.append(part)
    dx1, dg2 = _rms_bwd(x1, mix_norm, dhs, dx2, "mix_norm_bwd")

    dgb, dub, actb, hb, dob = _ffn_bwd_pre(x0, ffn1_norm, gg1, uu1, dx1, wd1, "ffn1_bwd_pre")
    grads["ffn1_w_gate"] = _wgrad(dgb, hb, FF, D, "ffn1_dwg")
    rs.append(_send_start(["scatter"], [grads["ffn1_w_gate"]], [rs[2][4]], "scatter_start3"))
    hb = _tie(hb, [rs[3][4]], "tie_after_scatter3")
    grads["ffn1_w_up"] = _wgrad(dub, hb, FF, D, "ffn1_dwu")
    rs.append(_send_start(["scatter"], [grads["ffn1_w_up"]], [rs[3][4]], "scatter_start4"))
    dob = _tie(dob, [rs[4][4]], "tie_after_scatter4")
    grads["ffn1_w_down"] = _wgrad(actb, dob, FF, D, "ffn1_dwd")
    rs.append(_send_start(["scatter"], [grads["ffn1_w_down"]], [rs[4][4]], "scatter_start5"))
    dgb = _tie(dgb, [rs[5][4]], "tie_after_scatter5")
    dx0, dg1 = _ffn_bwd_dx(x0, ffn1_norm, dgb, dub, dx1, wg1, wu1, "ffn1_bwd_dx")
    vec = jnp.concatenate([dg1, dg2, dg3, dgf, dvec[0:3], jnp.broadcast_to(loss_part[:, :1], (1, D))], axis=0)
    rs.append(_send_start(["bcast"], [vec], [rs[5][4]], "scatter_start6"))

    g_out, d_out, m_out, v_out = {}, {}, {}, {}
    me = _my_place()
    after = [rs[last][4]]
    for gi, grp in enumerate(rs_groups):
        kinds = ["scatter"] * len(grp) + (["bcast"] if gi == last else [])
        srcs, lands = _send_wait(kinds, rs[gi], after, f"scatter_wait{gi}")
        for n, src, land in zip(grp, srcs, lands):
            if n == "conv_dw_kernel":
                rows = src.shape[0] // NDEV
                own = lax.dynamic_slice(src, (me * rows, 0), (rows, src.shape[1]))
                g = _gsum(own, land, f"gsum_{n}")[:CONV_W]
                d, m2, v2 = _adamw(w[n][0], g, mo[n][0], vo[n][0], f"adamw_{n}")
                after = [d]
                g, d, m2, v2 = g[None], d[None], m2[None], v2[None]
            else:
                res = _update(src, land, landing_view(w[n], n), landing_view(mo[n], n), landing_view(vo[n], n),
                              f"update_{n}")
                after = [res[1]]
                g, d, m2, v2 = (own_view(a, n) for a in res)
            g_out[n], d_out[n], m_out[n], v_out[n] = g, d, m2, v2
    vland = lands[-1]

    def rows8(src):
        return jnp.concatenate([src[n].reshape(1, D) for n in small] + [jnp.ones((1, D), F32)], axis=0)

    g8, d8, m8, v8 = _small_update(vland, rows8(w), rows8(mo), rows8(vo), "small_update")
    for r, n in enumerate(small):
        shp = w[n].shape
        g_out[n], d_out[n], m_out[n], v_out[n] = (a[r].reshape(shp) for a in (g8, d8, m8, v8))
    loss = g8[7, 0]

    return (loss, dx0[None], *[g_out[n] for n in names], *[d_out[n] for n in names],
            *[m_out[n] for n in names], *[v_out[n] for n in names])
```

```python
import numpy as np
import jax
import jax.numpy as jnp
from jax import lax
from jax.experimental import pallas as pl
from jax.experimental.pallas import tpu as pltpu

F32 = jnp.float32
BF16 = jnp.bfloat16

T = 4096
D = 1024
FF = 2816
NDEV = 8
CONV_W = 31
HEAD = 128
BLK = 128
GROUPS = ((128, 1), (512, 4), (2048, 16))
NHG = 4
AW = NHG * HEAD
IN_W = 2 * D + 3 * 3 * AW + 2 * D
EPS = 1e-6
B1, B2, LR, AEPS, WD, STEP = 0.9, 0.999, 0.001, 1e-08, 0.01, 10
NEG = -1e30
VMEM_LIMIT = 56 * 1024 * 1024
MESH_ID = pl.DeviceIdType.MESH

NT = (((1,), (1,)), ((), ()))
NN = (((1,), (0,)), ((), ()))
TN = (((0,), (0,)), ((), ()))
_DIMS = {"nn": NN, "nt": NT, "tn": TN}


def _cp(sem=None):
    return pltpu.CompilerParams(dimension_semantics=sem, vmem_limit_bytes=VMEM_LIMIT)


def _sig(v):
    return 1.0 / (1.0 + jnp.exp(-v))


def _dot(a, b, dims):
    return lax.dot_general(a, b, dims, preferred_element_type=F32)


def _const_spec(shape):
    nd = len(shape)
    return pl.BlockSpec(shape, lambda *_: (0,) * nd)


def _mm(a, b, *, mode, m, n, k, tm, tn, tk, out_dtype, name, a_map=None, b_map=None,
        o_map=None, out_rows=None, init=None, passthru=None):
    gi, gj, gk = m // tm, n // tn, k // tk
    assert gi * tm == m and gj * tn == n and gk * tk == k, (name, m, n, k, tm, tn, tk)
    if mode == "nn":
        a_blk, b_blk = (tm, tk), (tk, tn)
        da, db = (lambda i, j, kk: (i, kk)), (lambda i, j, kk: (kk, j))
    elif mode == "nt":
        a_blk, b_blk = (tm, tk), (tn, tk)
        da, db = (lambda i, j, kk: (i, kk)), (lambda i, j, kk: (j, kk))
    else:
        a_blk, b_blk = (tk, tm), (tk, tn)
        da, db = (lambda i, j, kk: (kk, i)), (lambda i, j, kk: (kk, j))
    a_map = a_map or da
    b_map = b_map or db
    o_map = o_map or (lambda i, j, kk: (i, j))
    dims = _DIMS[mode]
    extra = init if init is not None else passthru
    out_rows = out_rows or m

    def body(*refs):
        if init is not None:
            a_ref, b_ref, i_ref, o_ref = refs[:4]
        elif passthru is not None:
            a_ref, b_ref, _, o_ref = refs[:4]
        else:
            a_ref, b_ref, o_ref = refs[:3]
        if gk == 1:
            prod = _dot(a_ref[...], b_ref[...], dims)
            if init is not None:
                prod = prod + i_ref[...].astype(F32)
            o_ref[...] = prod.astype(out_dtype)
            return
        acc = refs[-1]
        kk = pl.program_id(2)

        @pl.when(kk == 0)
        def _():
            if init is not None:
                acc[...] = i_ref[...].astype(F32)
            else:
                acc[...] = jnp.zeros_like(acc)

        acc[...] += _dot(a_ref[...], b_ref[...], dims)

        @pl.when(kk == gk - 1)
        def _():
            o_ref[...] = acc[...].astype(out_dtype)

    in_specs = [pl.BlockSpec(a_blk, a_map), pl.BlockSpec(b_blk, b_map)]
    args = [a, b]
    aliases = {}
    if init is not None:
        in_specs.append(pl.BlockSpec((tm, tn), o_map))
        args.append(init)
        aliases = {2: 0}
    elif passthru is not None:
        in_specs.append(pl.BlockSpec(memory_space=pl.ANY))
        args.append(passthru)
        aliases = {2: 0}
    out_dt = extra.dtype if extra is not None else out_dtype
    assert out_dt == out_dtype
    return pl.pallas_call(
        body, name=name, grid=(gi, gj, gk),
        in_specs=in_specs, out_specs=pl.BlockSpec((tm, tn), o_map),
        out_shape=jax.ShapeDtypeStruct((out_rows, n), out_dtype),
        scratch_shapes=[pltpu.VMEM((tm, tn), F32)] if gk > 1 else [],
        input_output_aliases=aliases,
        compiler_params=_cp(("parallel", "parallel", "arbitrary")),
    )(*args)


def _ffn_fwd(x, g, wg_t, wu_t, wd, name, next_gain=None, loss_of=None):
    tm, fc = PERM_TM, 256
    nc = FF // fc
    n_in = 5 + (1 if next_gain is not None else 0) + (2 if loss_of is not None else 0)

    def body(*refs):
        x_ref, g_ref, wg_ref, wu_ref, wd_ref = refs[:5]
        extra_in, outs = refs[5:n_in], refs[n_in:]
        act_ref = outs[-1]
        xv = x_ref[...]
        r = lax.rsqrt(jnp.mean(xv * xv, axis=-1, keepdims=True) + EPS)
        h = (xv * r * g_ref[...]).astype(BF16)
        gg_ref, uu_ref = (outs[0], outs[1]) if loss_of is not None else (outs[1], outs[2])
        for c in range(nc):
            sl = pl.ds(c * fc, fc)
            gg = _dot(h, wg_ref[sl, :], NT)
            uu = _dot(h, wu_ref[sl, :], NT)
            gg_ref[:, sl] = gg.astype(BF16)
            uu_ref[:, sl] = uu.astype(BF16)
            act_ref[:, sl] = (gg * _sig(gg) * uu).astype(BF16)
        y = xv + 0.5 * _dot(act_ref[...], wd_ref[...], NN)
        if loss_of is not None:
            _final_math(y, extra_in[0][...], extra_in[1][...], outs[2], outs[3], outs[4], pl.program_id(0))
            return
        outs[0][...] = y
        if next_gain is not None:
            tile = outs[-2]
            r2 = lax.rsqrt(jnp.mean(y * y, axis=-1, keepdims=True) + EPS)
            hv = y * r2 * extra_in[0][...]
            outs[3][...] = hv.astype(BF16)
            _put_tile(tile, hv)
            for dil, p_ref in zip(DILS, outs[4:4 + len(DILS)]):
                _store_perm(p_ref, tile, dil)

    wspec = pl.BlockSpec((FF, D), lambda i: (0, 0), pipeline_mode=pl.Buffered(1))
    row_d = pl.BlockSpec((tm, D), lambda i: (i, 0))
    row_f = pl.BlockSpec((tm, FF), lambda i: (i, 0))
    in_specs = [row_d, _const_spec((1, D)), wspec, wspec, wspec]
    args = [x, g, wg_t, wu_t, wd]
    f_shape = jax.ShapeDtypeStruct((T, FF), BF16)
    scratch = [pltpu.VMEM((tm, FF), BF16)]
    if loss_of is not None:
        in_specs += [_const_spec((1, D)), row_d]
        args += list(loss_of)
        out_specs = [row_f, row_f, row_d, _const_spec((1, D)), _const_spec((1, 128))]
        out_shape = [f_shape, f_shape, jax.ShapeDtypeStruct((T, D), F32), jax.ShapeDtypeStruct((1, D), F32),
                     jax.ShapeDtypeStruct((1, 128), F32)]
    else:
        out_specs = [row_d, row_f, row_f]
        out_shape = [jax.ShapeDtypeStruct((T, D), F32), f_shape, f_shape]
        if next_gain is not None:
            in_specs.append(_const_spec((1, D)))
            args.append(next_gain)
            out_specs += [row_d] + [_perm_spec(d, D) for d in DILS]
            out_shape += [jax.ShapeDtypeStruct((T, D), BF16)] + [_perm_shape(d, D, BF16) for d in DILS]
            scratch = [_tile_scratch(D)] + scratch
    out = pl.pallas_call(
        body, name=name, grid=(T // tm,), in_specs=in_specs, out_specs=out_specs, out_shape=out_shape,
        scratch_shapes=scratch,
        compiler_params=_cp(("arbitrary",) if loss_of is not None else ("parallel",)),
    )(*args)
    if next_gain is not None:
        return out[0], out[1], out[2], [out[3]] + [o.reshape(T, D) for o in out[4:]]
    return tuple(out)


def _ffn_bwd(x, g, gg_all, uu_all, dout, wg_t, wu_t, wd, name):
    tm, fc = 256, 256
    nc = FF // fc

    def body(x_ref, g_ref, gg_ref, uu_ref, do_ref, wg_ref, wu_ref, wd_ref,
             dx_ref, dgam_ref, dg_ref, du_ref, act_ref, h_ref, db_ref):
        i = pl.program_id(0)
        xv = x_ref[...]
        r = lax.rsqrt(jnp.mean(xv * xv, axis=-1, keepdims=True) + EPS)
        xhat = xv * r
        gam = g_ref[...]
        h_ref[...] = (xhat * gam).astype(BF16)
        dov = do_ref[...]
        dbv = (0.5 * dov).astype(BF16)
        db_ref[...] = dbv
        for c in range(nc):
            sl = pl.ds(c * fc, fc)
            da = _dot(dbv, wd_ref[sl, :], NT)
            gg = gg_ref[:, sl].astype(F32)
            uu = uu_ref[:, sl].astype(F32)
            s = _sig(gg)
            si = gg * s
            dgv = (da * uu * (s * (1.0 + gg * (1.0 - s)))).astype(BF16)
            duv = (da * si).astype(BF16)
            dg_ref[:, sl] = dgv
            du_ref[:, sl] = duv
            act_ref[:, sl] = (si * uu).astype(BF16)
        dh = _dot(dg_ref[...], wg_ref[...], NN) + _dot(du_ref[...], wu_ref[...], NN)

        @pl.when(i == 0)
        def _():
            dgam_ref[...] = jnp.zeros_like(dgam_ref)

        dgam_ref[...] += jnp.sum(dh * xhat, axis=0, keepdims=True)
        dxh = dh * gam
        dx_ref[...] = dov + r * (dxh - xhat * jnp.mean(dxh * xhat, axis=-1, keepdims=True))

    wspec = pl.BlockSpec((FF, D), lambda i: (0, 0), pipeline_mode=pl.Buffered(1))
    row_d = pl.BlockSpec((tm, D), lambda i: (i, 0))
    row_f = pl.BlockSpec((tm, FF), lambda i: (i, 0))
    return pl.pallas_call(
        body, name=name, grid=(T // tm,),
        in_specs=[row_d, _const_spec((1, D)), row_f, row_f, row_d, wspec, wspec, wspec],
        out_specs=[row_d, _const_spec((1, D)), row_f, row_f, row_f, row_d, row_d],
        out_shape=[jax.ShapeDtypeStruct((T, D), F32), jax.ShapeDtypeStruct((1, D), F32),
                   jax.ShapeDtypeStruct((T, FF), BF16), jax.ShapeDtypeStruct((T, FF), BF16),
                   jax.ShapeDtypeStruct((T, FF), BF16), jax.ShapeDtypeStruct((T, D), BF16),
                   jax.ShapeDtypeStruct((T, D), BF16)],
        compiler_params=_cp(("arbitrary",)),
    )(x, g, gg_all, uu_all, dout, wg_t, wu_t, wd)


def _ffn_bwd_pre(x, g, gg_all, uu_all, dout, wd, name):
    tm, fc = 512, 256
    nc = FF // fc

    def body(x_ref, g_ref, gg_ref, uu_ref, do_ref, wd_ref, dg_ref, du_ref, act_ref, h_ref, db_ref):
        xv = x_ref[...]
        r = lax.rsqrt(jnp.mean(xv * xv, axis=-1, keepdims=True) + EPS)
        h_ref[...] = (xv * r * g_ref[...]).astype(BF16)
        dbv = (0.5 * do_ref[...]).astype(BF16)
        db_ref[...] = dbv
        for c in range(nc):
            sl = pl.ds(c * fc, fc)
            da = _dot(dbv, wd_ref[sl, :], NT)
            gg = gg_ref[:, sl].astype(F32)
            uu = uu_ref[:, sl].astype(F32)
            s = _sig(gg)
            si = gg * s
            dg_ref[:, sl] = (da * uu * (s * (1.0 + gg * (1.0 - s)))).astype(BF16)
            du_ref[:, sl] = (da * si).astype(BF16)
            act_ref[:, sl] = (si * uu).astype(BF16)

    wspec = pl.BlockSpec((FF, D), lambda i: (0, 0), pipeline_mode=pl.Buffered(1))
    row_d = pl.BlockSpec((tm, D), lambda i: (i, 0))
    row_f = pl.BlockSpec((tm, FF), lambda i: (i, 0))
    return pl.pallas_call(
        body, name=name, grid=(T // tm,),
        in_specs=[row_d, _const_spec((1, D)), row_f, row_f, row_d, wspec],
        out_specs=[row_f, row_f, row_f, row_d, row_d],
        out_shape=[jax.ShapeDtypeStruct((T, FF), BF16), jax.ShapeDtypeStruct((T, FF), BF16),
                   jax.ShapeDtypeStruct((T, FF), BF16), jax.ShapeDtypeStruct((T, D), BF16),
                   jax.ShapeDtypeStruct((T, D), BF16)],
        compiler_params=_cp(("parallel",)),
    )(x, g, gg_all, uu_all, dout, wd)


def _ffn_bwd_dx(x, g, dgb, dub, dout, wg_t, wu_t, name):
    tm = 512

    def body(x_ref, g_ref, dg_ref, du_ref, do_ref, wg_ref, wu_ref, dx_ref, dgam_ref):
        i = pl.program_id(0)
        xv = x_ref[...]
        r = lax.rsqrt(jnp.mean(xv * xv, axis=-1, keepdims=True) + EPS)
        xhat = xv * r
        gam = g_ref[...]
        dh = _dot(dg_ref[...], wg_ref[...], NN) + _dot(du_ref[...], wu_ref[...], NN)

        @pl.when(i == 0)
        def _():
            dgam_ref[...] = jnp.zeros_like(dgam_ref)

        dgam_ref[...] += jnp.sum(dh * xhat, axis=0, keepdims=True)
        dxh = dh * gam
        dx_ref[...] = do_ref[...] + r * (dxh - xhat * jnp.mean(dxh * xhat, axis=-1, keepdims=True))

    wspec = pl.BlockSpec((FF, D), lambda i: (0, 0), pipeline_mode=pl.Buffered(1))
    row_d = pl.BlockSpec((tm, D), lambda i: (i, 0))
    row_f = pl.BlockSpec((tm, FF), lambda i: (i, 0))
    return pl.pallas_call(
        body, name=name, grid=(T // tm,),
        in_specs=[row_d, _const_spec((1, D)), row_f, row_f, row_d, wspec, wspec],
        out_specs=[row_d, _const_spec((1, D))],
        out_shape=[jax.ShapeDtypeStruct((T, D), F32), jax.ShapeDtypeStruct((1, D), F32)],
        compiler_params=_cp(("arbitrary",)),
    )(x, g, dgb, dub, dout, wg_t, wu_t)


def _wgrad(a, b, m, n, name):
    tm = m // 2 if m == FF else m
    return _mm(a, b, mode="tn", m=m, n=n, k=T, tm=tm, tn=n, tk=min(T, 2048), out_dtype=BF16, name=name)


PERM_TM = 512
DILS = tuple(d for _, d in GROUPS if d > 1)


def _perm_spec(dil, cols):
    return pl.BlockSpec((dil, PERM_TM // dil, cols), lambda i: (0, i, 0))


def _perm_shape(dil, cols, dtype):
    return jax.ShapeDtypeStruct((dil, T // dil, cols), dtype)


LANES = 128


def _tile_scratch(cols):
    return pltpu.VMEM((cols // LANES, PERM_TM, LANES), F32)


def _put_tile(tile, value):
    for c in range(tile.shape[0]):
        tile[c] = value[:, c * LANES:(c + 1) * LANES]


def _get_tile(tile):
    return jnp.concatenate([tile[c] for c in range(tile.shape[0])], axis=1)


def _store_perm(out_ref, tile, dil):
    for r in range(dil):
        for c in range(tile.shape[0]):
            out_ref[r, :, pl.ds(c * LANES, LANES)] = tile[c, pl.ds(r, PERM_TM // dil, stride=dil), :].astype(
                out_ref.dtype)


def _load_unperm(in_ref, tile, dil):
    for r in range(dil):
        for c in range(tile.shape[0]):
            tile[c, pl.ds(r, PERM_TM // dil, stride=dil), :] = in_ref[r, :, pl.ds(c * LANES, LANES)].astype(F32)


def _final_math(xv, gam, tgt, dx_ref, dgam_ref, loss_ref, i):
    r = lax.rsqrt(jnp.mean(xv * xv, axis=-1, keepdims=True) + EPS)
    xhat = xv * r
    err = xhat * gam - tgt
    part = 0.5 * jnp.sum(jnp.mean(err * err, axis=-1, keepdims=True), axis=0, keepdims=True)
    dy = err * (1.0 / D)

    @pl.when(i == 0)
    def _():
        dgam_ref[...] = jnp.zeros_like(dgam_ref)
        loss_ref[...] = jnp.zeros_like(loss_ref)

    dgam_ref[...] += jnp.sum(dy * xhat, axis=0, keepdims=True)
    loss_ref[...] += jnp.broadcast_to(part, loss_ref.shape)
    dxh = dy * gam
    dx_ref[...] = r * (dxh - xhat * jnp.mean(dxh * xhat, axis=-1, keepdims=True))


QKV_BLOCK0 = 2 * D // AW


def _rms_bwd(x, g, dh0, dqkvs, win_t, dres, name):
    tm = PERM_TM
    dils = [d for _, d in GROUPS]
    ng = len(dils)
    assert len(dqkvs) == ng

    def body(*refs):
        x_ref, g_ref, dh0_ref = refs[:3]
        dq_refs = refs[3:3 + 3 * ng]
        w_refs = refs[3 + 3 * ng:3 + 6 * ng]
        dr_ref, dx_ref, dgam_ref = refs[3 + 6 * ng:6 + 6 * ng]
        tile = refs[6 + 6 * ng]
        stages = refs[7 + 6 * ng:]
        i = pl.program_id(0)
        xv = x_ref[...]
        r = lax.rsqrt(jnp.mean(xv * xv, axis=-1, keepdims=True) + EPS)
        xhat = xv * r
        gam = g_ref[...]
        dh = dh0_ref[...]
        si = 0
        for gi, dil in enumerate(dils):
            part = None
            for p in range(3):
                blk = dq_refs[3 * gi + p][...]
                term = _dot(blk.reshape(tm, AW), w_refs[3 * gi + p][...], NN)
                part = term if part is None else part + term
            if dil > 1:
                stage = stages[si]
                si += 1
                stage[...] = part.reshape(dil, tm // dil, D)
                _load_unperm(stage, tile, dil)
                part = _get_tile(tile)
            dh = dh + part

        @pl.when(i == 0)
        def _():
            dgam_ref[...] = jnp.zeros_like(dgam_ref)

        dgam_ref[...] += jnp.sum(dh * xhat, axis=0, keepdims=True)
        dxh = dh * gam
        dx_ref[...] = dr_ref[...] + r * (dxh - xhat * jnp.mean(dxh * xhat, axis=-1, keepdims=True))

    row_d = pl.BlockSpec((tm, D), lambda i: (i, 0))
    dq_specs, dq_args, w_specs = [], [], []
    for gi, (d, a) in enumerate(zip(dils, dqkvs)):
        for p in range(3):
            if d == 1:
                dq_specs.append(pl.BlockSpec((None, tm, AW), lambda i, p=p: (p, i, 0)))
                dq_args.append(a)
            else:
                dq_specs.append(pl.BlockSpec((None, d, tm // d, AW), lambda i, p=p: (p, 0, i, 0)))
                dq_args.append(a.reshape(3, d, T // d, AW))
            w_specs.append(pl.BlockSpec((AW, D), lambda i, q=QKV_BLOCK0 + gi + 3 * p: (q, 0),
                                        pipeline_mode=pl.Buffered(1)))
    return pl.pallas_call(
        body, name=name, grid=(T // tm,),
        in_specs=[row_d, _const_spec((1, D)), row_d] + dq_specs + w_specs + [row_d],
        out_specs=[row_d, _const_spec((1, D))],
        out_shape=[jax.ShapeDtypeStruct((T, D), F32), jax.ShapeDtypeStruct((1, D), F32)],
        scratch_shapes=[_tile_scratch(D)] + [pltpu.VMEM((d, tm // d, D), F32) for d in dils if d > 1],
        compiler_params=_cp(("arbitrary",)),
    )(x, g, dh0, *dq_args, *([win_t] * (3 * ng)), dres)


CONV_TM = 256
CONV_HALO = 32
CONV_RB = 16


def _glu(ab):
    ab = ab.astype(F32)
    return ab[:, :D] * _sig(ab[:, D:])


def _ln_stats(z1):
    mu = jnp.mean(z1, axis=-1, keepdims=True)
    zc = z1 - mu
    rstd = lax.rsqrt(jnp.mean(zc * zc, axis=-1, keepdims=True) + EPS)
    return zc * rstd, rstd


def _fill_shifts(zs):
    n = zs.shape[1] - 8
    for s in range(1, 8):
        zs[s, pl.ds(0, n), :] = zs[0, pl.ds(s, n), :]


def _shifted(zs, start, rows):
    q, s = divmod(start, 8)
    return zs[s, pl.ds(8 * q, rows), :]


def _conv_fwd(ab, kern, dwb, lng, lnb, name, guest=None):
    tm, hl, rb = CONV_TM, CONV_HALO, CONV_RB
    off = hl - (CONV_W - 1)
    if guest is not None:
        g_a, g_b, g_blocks, g_rows = guest
        g_nblk = len(g_blocks)

    def body(ab_ref, abh_ref, k_ref, dwb_ref, lng_ref, lnb_ref, *rest):
        if guest is not None:
            ga_ref, gb_refs, rest = rest[0], rest[1:1 + g_nblk], rest[1 + g_nblk:]
            z1_ref, z3_ref, go_ref, zs = rest
            for q, gb_ref in enumerate(gb_refs):
                go_ref[:, pl.ds(q * g_rows, g_rows)] = _dot(ga_ref[...], gb_ref[...], NT).astype(BF16)
        else:
            z1_ref, z3_ref, zs = rest
        i = pl.program_id(0)
        zs[0, pl.ds(0, hl), :] = jnp.where(i > 0, _glu(abh_ref[...]), 0.0)
        zs[0, pl.ds(hl, tm), :] = _glu(ab_ref[...])
        _fill_shifts(zs)
        for b in range(tm // rb):
            acc = jnp.zeros((rb, D), F32)
            for j in range(CONV_W):
                acc = acc + _shifted(zs, b * rb + off + j, rb) * k_ref[pl.ds(j, 1), :]
            z1 = acc + dwb_ref[...]
            z1_ref[pl.ds(b * rb, rb), :] = z1
            zn, _ = _ln_stats(z1)
            z2 = zn * lng_ref[...] + lnb_ref[...]
            z3_ref[pl.ds(b * rb, rb), :] = (z2 * _sig(z2)).astype(BF16)

    row = pl.BlockSpec((tm, D), lambda i: (i, 0))
    g_specs, g_args, g_ospecs, g_oshapes = [], [], [], []
    if guest is not None:
        kdim = g_a.shape[1]
        g_specs = [pl.BlockSpec((tm, kdim), lambda i: (i, 0))]
        g_specs += [pl.BlockSpec((g_rows, kdim), lambda i, q=q: (q, 0), pipeline_mode=pl.Buffered(1))
                    for q in g_blocks]
        g_args = [g_a] + [g_b] * g_nblk
        g_ospecs = [pl.BlockSpec((tm, g_nblk * g_rows), lambda i: (i, 0))]
        g_oshapes = [jax.ShapeDtypeStruct((T, g_nblk * g_rows), BF16)]
    return pl.pallas_call(
        body, name=name, grid=(T // tm,),
        in_specs=[pl.BlockSpec((tm, 2 * D), lambda i: (i, 0)),
                  pl.BlockSpec((hl, 2 * D), lambda i: (jnp.maximum(i * (tm // hl) - 1, 0), 0)),
                  _const_spec((32, D)), _const_spec((1, D)), _const_spec((1, D)), _const_spec((1, D))] + g_specs,
        out_specs=[row, row] + g_ospecs,
        out_shape=[jax.ShapeDtypeStruct((T, D), F32), jax.ShapeDtypeStruct((T, D), BF16)] + g_oshapes,
        scratch_shapes=[pltpu.VMEM((8, hl + tm, D), F32)],
        compiler_params=_cp(("parallel",)),
    )(ab, ab, kern, dwb, lng, lnb, *g_args)


GUEST_TM = 256


def _conv_bwd(dz3, z1, ab, kern, lng, lnb, name, guest_lhs=(), guest_rhs=None):
    tm, hl, rb = CONV_TM, CONV_HALO, CONV_RB
    off = hl - (CONV_W - 1)
    nsteps = T // tm
    ng = len(guest_lhs)
    gblocks = [a.shape[1] // GUEST_TM for a in guest_lhs]
    assert all(gb <= nsteps and gb * GUEST_TM == a.shape[1] for gb, a in zip(gblocks, guest_lhs))

    def ln_bwd(dz3v, z1v, lngv, lnbv):
        zn, rstd = _ln_stats(z1v)
        z2 = zn * lngv + lnbv
        s = _sig(z2)
        dz2 = dz3v * (s * (1.0 + z2 * (1.0 - s)))
        dzn = dz2 * lngv
        dz1 = rstd * (dzn - jnp.mean(dzn, axis=-1, keepdims=True)
                      - zn * jnp.mean(dzn * zn, axis=-1, keepdims=True))
        return dz1, dz2, zn

    def body(dz3_ref, dz3h_ref, z1_ref, z1h_ref, ab_ref, abh_ref, k_ref, lng_ref, lnb_ref, *rest):
        g_in, rest = rest[:ng + (1 if ng else 0)], rest[ng + (1 if ng else 0):]
        dab_ref, dk_ref, dvec_ref = rest[:3]
        g_out, (zs, dzs) = rest[3:3 + ng], rest[3 + ng:]
        i = pl.program_id(0)
        lngv, lnbv = lng_ref[...], lnb_ref[...]

        for a_ref, o_ref, gb in zip(g_in[:ng], g_out, gblocks):
            @pl.when(i < gb)
            def _(a_ref=a_ref, o_ref=o_ref):
                o_ref[...] = _dot(a_ref[...], g_in[ng][...], TN).astype(BF16)

        @pl.when(i == 0)
        def _():
            dk_ref[...] = jnp.zeros_like(dk_ref)
            dvec_ref[...] = jnp.zeros_like(dvec_ref)

        dz1, dz2, zn = ln_bwd(dz3_ref[...].astype(F32), z1_ref[...], lngv, lnbv)
        dvec_ref[pl.ds(0, 1), :] += jnp.sum(dz1, axis=0, keepdims=True)
        dvec_ref[pl.ds(1, 1), :] += jnp.sum(dz2 * zn, axis=0, keepdims=True)
        dvec_ref[pl.ds(2, 1), :] += jnp.sum(dz2, axis=0, keepdims=True)
        dzs[0, pl.ds(0, tm), :] = dz1
        dz1h, _, _ = ln_bwd(dz3h_ref[...].astype(F32), z1h_ref[...], lngv, lnbv)
        dzs[0, pl.ds(tm, hl), :] = jnp.where(i < nsteps - 1, dz1h, 0.0)
        _fill_shifts(dzs)
        zs[0, pl.ds(0, hl), :] = jnp.where(i > 0, _glu(abh_ref[...]), 0.0)
        zs[0, pl.ds(hl, tm), :] = _glu(ab_ref[...])
        _fill_shifts(zs)

        for j in range(CONV_W):
            tot = jnp.zeros((rb, D), F32)
            for b in range(tm // rb):
                tot = tot + dzs[0, pl.ds(b * rb, rb), :] * _shifted(zs, b * rb + off + j, rb)
            dk_ref[pl.ds(j, 1), :] += jnp.sum(tot, axis=0, keepdims=True)

        for b in range(tm // rb):
            acc = jnp.zeros((rb, D), F32)
            for j in range(CONV_W):
                acc = acc + _shifted(dzs, b * rb + (CONV_W - 1) - j, rb) * k_ref[pl.ds(j, 1), :]
            av = ab_ref[pl.ds(b * rb, rb), pl.ds(0, D)].astype(F32)
            sb = _sig(ab_ref[pl.ds(b * rb, rb), pl.ds(D, D)].astype(F32))
            dab_ref[pl.ds(b * rb, rb), pl.ds(0, D)] = (acc * sb).astype(BF16)
            dab_ref[pl.ds(b * rb, rb), pl.ds(D, D)] = (acc * av * sb * (1.0 - sb)).astype(BF16)

    row = pl.BlockSpec((tm, D), lambda i: (i, 0))
    nxt = pl.BlockSpec((hl, D), lambda i: (jnp.minimum((i + 1) * (tm // hl), T // hl - 1), 0))
    g_specs, g_args, g_ospecs, g_oshapes = [], [], [], []
    for a, gb in zip(guest_lhs, gblocks):
        g_specs.append(pl.BlockSpec((T, GUEST_TM), lambda i, gb=gb: (0, jnp.minimum(i, gb - 1))))
        g_args.append(a)
        g_ospecs.append(pl.BlockSpec((GUEST_TM, guest_rhs.shape[1]), lambda i, gb=gb: (jnp.minimum(i, gb - 1), 0)))
        g_oshapes.append(jax.ShapeDtypeStruct((a.shape[1], guest_rhs.shape[1]), BF16))
    if ng:
        g_specs.append(pl.BlockSpec(guest_rhs.shape, lambda i: (0, 0), pipeline_mode=pl.Buffered(1)))
        g_args.append(guest_rhs)
    return pl.pallas_call(
        body, name=name, grid=(nsteps,),
        in_specs=[row, nxt, row, nxt,
                  pl.BlockSpec((tm, 2 * D), lambda i: (i, 0)),
                  pl.BlockSpec((hl, 2 * D), lambda i: (jnp.maximum(i * (tm // hl) - 1, 0), 0)),
                  _const_spec((32, D)), _const_spec((1, D)), _const_spec((1, D))] + g_specs,
        out_specs=[pl.BlockSpec((tm, 2 * D), lambda i: (i, 0)), _const_spec((32, D)), _const_spec((8, D))]
        + g_ospecs,
        out_shape=[jax.ShapeDtypeStruct((T, 2 * D), BF16), jax.ShapeDtypeStruct((32, D), F32),
                   jax.ShapeDtypeStruct((8, D), F32)] + g_oshapes,
        scratch_shapes=[pltpu.VMEM((8, hl + tm, D), F32), pltpu.VMEM((8, tm + hl, D), F32)],
        compiler_params=_cp(("arbitrary",)),
    )(dz3, dz3, z1, z1, ab, ab, kern, lng, lnb, *g_args)


def _alibi_slopes():
    h = np.arange(1, 3 * NHG + 1, dtype=np.float32)
    return np.power(np.float32(2.0), -8.0 * h / np.float32(3 * NHG)).astype(np.float32)


def _band_bias(gi):
    _, dil = GROUPS[gi]
    slopes = _alibi_slopes()[gi * NHG:(gi + 1) * NHG]
    qi = np.arange(BLK)[:, None]
    ki = np.arange(2 * BLK)[None, :]
    steps = BLK + qi - ki
    band = (steps >= 0) & (steps <= BLK)
    bias = -slopes[:, None, None] * (dil * steps).astype(np.float32)[None]
    return jnp.asarray(np.where(band[None], bias, np.float32(NEG)).astype(np.float32))


QB_FWD = 8
QB_BWD = 32


def _attn_specs(qb, c0):
    prev = lambda n: jnp.maximum(n * qb - 1, 0)
    return [pl.BlockSpec((qb * BLK, HEAD), lambda h, n: (n, c0 + h)),
            pl.BlockSpec((BLK, HEAD), lambda h, n: (prev(n), c0 + NHG + h)),
            pl.BlockSpec((qb * BLK, HEAD), lambda h, n: (n, c0 + NHG + h)),
            pl.BlockSpec((BLK, HEAD), lambda h, n: (prev(n), c0 + 2 * NHG + h)),
            pl.BlockSpec((qb * BLK, HEAD), lambda h, n: (n, c0 + 2 * NHG + h)),
            pl.BlockSpec((None, BLK, 2 * BLK), lambda h, n: (h, 0, 0))]


def _scores(q, kcat, bias, blk, seg):
    s = _dot(q, kcat, NT) * (HEAD ** -0.5) + bias
    col = lax.broadcasted_iota(jnp.int32, s.shape, 1)
    first = (blk % seg) == 0
    return jnp.where(jnp.logical_and(first, col < BLK), NEG, s)


def _attn_fwd(qkv, gi, name, col0=0):
    seg = (T // GROUPS[gi][1]) // BLK

    qb = min(QB_FWD, T // BLK)

    def body(q_ref, kp_ref, kc_ref, vp_ref, vc_ref, bias_ref, o_ref, l_ref):
        n = pl.program_id(0)
        for h in range(NHG):
            cols = pl.ds(h * HEAD, HEAD)
            kwin = jnp.concatenate([kp_ref[:, cols], kc_ref[:, cols]], axis=0)
            vwin = jnp.concatenate([vp_ref[:, cols], vc_ref[:, cols]], axis=0)
            bias = bias_ref[h]
            for b in range(qb):
                rows = pl.ds(b * BLK, BLK)
                s = _scores(q_ref[rows, cols], kwin[b * BLK:(b + 2) * BLK], bias, n * qb + b, seg)
                mx = jnp.max(s, axis=-1, keepdims=True)
                p = jnp.exp(s - mx)
                den = jnp.sum(p, axis=-1, keepdims=True)
                o_ref[rows, cols] = (_dot(p.astype(BF16), vwin[b * BLK:(b + 2) * BLK], NN) / den).astype(BF16)
                l_ref[rows, cols] = jnp.broadcast_to(mx + jnp.log(den), (BLK, HEAD))

    prev = lambda n: jnp.maximum(n * qb - 1, 0)
    c0 = col0 // AW
    cur = lambda part: pl.BlockSpec((qb * BLK, AW), lambda n: (n, part))
    halo = lambda part: pl.BlockSpec((BLK, AW), lambda n: (prev(n), part))
    return pl.pallas_call(
        body, name=name, grid=(T // (qb * BLK),),
        in_specs=[cur(c0), halo(c0 + 1), cur(c0 + 1), halo(c0 + 2), cur(c0 + 2),
                  _const_spec((NHG, BLK, 2 * BLK))],
        out_specs=[cur(0), cur(0)],
        out_shape=[jax.ShapeDtypeStruct((T, AW), BF16), jax.ShapeDtypeStruct((T, AW), F32)],
        compiler_params=_cp(("parallel",)),
    )(qkv, qkv, qkv, qkv, qkv, _band_bias(gi))


def _attn_bwd(qkv, dob, lse, delta, gi, name, col0=0):
    seg = (T // GROUPS[gi][1]) // BLK
    qb = min(QB_BWD, T // BLK)
    nb = T // (qb * BLK)
    scale = HEAD ** -0.5

    def body(q_ref, kp_ref, kc_ref, vp_ref, vc_ref, bias_ref, do_ref, l_ref, dl_ref, out_ref, dk_acc, dv_acc):
        n = pl.program_id(1)
        kwin = jnp.concatenate([kp_ref[...], kc_ref[...]], axis=0)
        vwin = jnp.concatenate([vp_ref[...], vc_ref[...]], axis=0)
        bias = bias_ref[...]
        dks, dvs = [], []
        for b in range(qb):
            rows = pl.ds(b * BLK, BLK)
            q = q_ref[rows, :]
            kcat = kwin[b * BLK:(b + 2) * BLK]
            s = _scores(q, kcat, bias, n * qb + b, seg)
            p = jnp.exp(s - l_ref[rows, pl.ds(0, 1)])
            dov = do_ref[rows, :]
            dvs.append(_dot(p.astype(BF16), dov, TN))
            dp = _dot(dov, vwin[b * BLK:(b + 2) * BLK], NT)
            dsb = (p * (dp - dl_ref[rows, pl.ds(0, 1)]) * scale).astype(BF16)
            row = pl.ds(pl.multiple_of((n * qb + b) * BLK, BLK), BLK)
            out_ref[0, row, :] = _dot(dsb, kcat, NN).astype(BF16)
            dks.append(_dot(dsb, q, TN))
        for b in range(qb):
            row = pl.ds(pl.multiple_of((n * qb + b) * BLK, BLK), BLK)
            if b + 1 < qb:
                dk_acc[row, :] = dks[b][BLK:] + dks[b + 1][:BLK]
                dv_acc[row, :] = dvs[b][BLK:] + dvs[b + 1][:BLK]
            else:
                dk_acc[row, :] = dks[b][BLK:]
                dv_acc[row, :] = dvs[b][BLK:]

        @pl.when(n > 0)
        def _():
            prow = pl.ds(pl.multiple_of((n * qb - 1) * BLK, BLK), BLK)
            dk_acc[prow, :] += dks[0][:BLK]
            dv_acc[prow, :] += dvs[0][:BLK]

        @pl.when(n == nb - 1)
        def _():
            out_ref[1] = dk_acc[...].astype(BF16)
            out_ref[2] = dv_acc[...].astype(BF16)

    oblk = pl.BlockSpec((qb * BLK, HEAD), lambda h, n: (n, h))
    return pl.pallas_call(
        body, name=name, grid=(NHG, nb),
        in_specs=_attn_specs(qb, col0 // HEAD) + [oblk, oblk, oblk],
        out_specs=pl.BlockSpec((3, T, HEAD), lambda h, n: (0, 0, h)),
        out_shape=jax.ShapeDtypeStruct((3, T, AW), BF16),
        scratch_shapes=[pltpu.VMEM((T, HEAD), F32), pltpu.VMEM((T, HEAD), F32)],
        compiler_params=_cp(("parallel", "arbitrary")),
    )(qkv, qkv, qkv, qkv, qkv, _band_bias(gi), dob, lse, delta)


def _merge(outs, lses, name):
    tm = PERM_TM
    dils = [d for _, d in GROUPS]
    ng = len(dils)

    def body(*refs):
        in_refs = refs[:2 * ng]
        ab_ref = refs[2 * ng]
        lse_refs = refs[2 * ng + 1:3 * ng + 1]
        tile = refs[-1]

        def token_order(ref, dil):
            if dil == 1:
                return ref[...].astype(F32)
            _load_unperm(ref, tile, dil)
            return _get_tile(tile)

        os = [token_order(in_refs[2 * i], d) for i, d in enumerate(dils)]
        ls = [token_order(in_refs[2 * i + 1], d) for i, d in enumerate(dils)]
        mx = jnp.maximum(jnp.maximum(ls[0], ls[1]), ls[2])
        es = [jnp.exp(v - mx) for v in ls]
        tot = es[0] + es[1] + es[2]
        att = (es[0] / tot) * os[0] + (es[1] / tot) * os[1] + (es[2] / tot) * os[2]
        ab_ref[...] = att.astype(BF16)
        lse = mx + jnp.log(tot)
        _put_tile(tile, lse)
        for dil, ref in zip(dils, lse_refs):
            if dil == 1:
                ref[...] = lse
            else:
                _store_perm(ref, tile, dil)

    row = pl.BlockSpec((tm, AW), lambda i: (i, 0))
    specs = [row if d == 1 else _perm_spec(d, AW) for d in dils]
    args = []
    for d, o, l in zip(dils, outs, lses):
        args += [o, l] if d == 1 else [o.reshape(d, T // d, AW), l.reshape(d, T // d, AW)]
    out = pl.pallas_call(
        body, name=name, grid=(T // tm,),
        in_specs=[sp for sp in specs for _ in range(2)], out_specs=[row] + specs,
        out_shape=[jax.ShapeDtypeStruct((T, AW), BF16)]
        + [jax.ShapeDtypeStruct((T, AW), F32) if d == 1 else _perm_shape(d, AW, F32) for d in dils],
        scratch_shapes=[_tile_scratch(AW)],
        compiler_params=_cp(("parallel",)),
    )(*args)
    return out[0], [o.reshape(T, AW) for o in out[1:]]


GATE_BLOCK0 = (IN_W - 2 * D) // (D // 2)


def _mix_out(z3b, attnb, gates, wc, wa_t, wo, x1, name):
    tm = 512

    def body(z_ref, a_ref, g_ref, wc_ref, wa_ref, wo_ref, x_ref, xo_ref, yc_ref, ya_ref, mx_ref):
        yc = _dot(z_ref[...], wc_ref[...], NN)
        ya = _dot(a_ref[...], wa_ref[...], NT)
        yc_ref[...] = yc.astype(BF16)
        ya_ref[...] = ya.astype(BF16)
        gv = g_ref[...].astype(F32)
        mixed = (_sig(gv[:, :D]) * yc + _sig(gv[:, D:]) * ya).astype(BF16)
        mx_ref[...] = mixed
        xo_ref[...] = x_ref[...] + _dot(mixed, wo_ref[...], NN)

    row = pl.BlockSpec((tm, D), lambda i: (i, 0))
    return pl.pallas_call(
        body, name=name, grid=(T // tm,),
        in_specs=[row, pl.BlockSpec((tm, AW), lambda i: (i, 0)), pl.BlockSpec((tm, 2 * D), lambda i: (i, 0)),
                  _const_spec((D, D)), _const_spec((D, AW)), _const_spec((D, D)), row],
        out_specs=[row, row, row, row],
        out_shape=[jax.ShapeDtypeStruct((T, D), F32), jax.ShapeDtypeStruct((T, D), BF16),
                   jax.ShapeDtypeStruct((T, D), BF16), jax.ShapeDtypeStruct((T, D), BF16)],
        compiler_params=_cp(("parallel",)),
    )(z3b, attnb, gates, wc, wa_t, wo, x1)


def _mix_out_bwd(dx2, gates, yc, ya, attn, wc, wa_t, wo, win_t, name):
    tm = PERM_TM
    dils = [d for _, d in GROUPS]
    ng = len(dils)

    def body(dx_ref, g_ref, yc_ref, ya_ref, at_ref, wc_ref, wa_ref, wo_ref, wg0_ref, wg1_ref, wg2_ref, wg3_ref,
             dg_ref, dyc_ref, dya_ref, dxb_ref, dz3_ref, dhg_ref, *rest):
        dat_refs, dl_refs, tile = rest[:ng], rest[ng:2 * ng], rest[-1]
        dxb = dx_ref[...].astype(BF16)
        dxb_ref[...] = dxb
        dmix = _dot(dxb, wo_ref[...], NT)
        gv = g_ref[...].astype(F32)
        sc = _sig(gv[:, :D])
        sa = _sig(gv[:, D:])
        ycv, yav = yc_ref[...].astype(F32), ya_ref[...].astype(F32)
        dgc = (dmix * ycv * sc * (1.0 - sc)).astype(BF16)
        dga = (dmix * yav * sa * (1.0 - sa)).astype(BF16)
        dg_ref[:, pl.ds(0, D)] = dgc
        dg_ref[:, pl.ds(D, D)] = dga
        half = D // 2
        dhg_ref[...] = (_dot(dgc[:, :half], wg0_ref[...], NN) + _dot(dgc[:, half:], wg1_ref[...], NN)
                        + _dot(dga[:, :half], wg2_ref[...], NN) + _dot(dga[:, half:], wg3_ref[...], NN))
        dyc = (dmix * sc).astype(BF16)
        dya = (dmix * sa).astype(BF16)
        dyc_ref[...] = dyc
        dya_ref[...] = dya
        dz3_ref[...] = _dot(dyc, wc_ref[...], NT).astype(BF16)
        dat = _dot(dya, wa_ref[...], NN)
        prod = dat * at_ref[...].astype(F32)
        delta = jnp.concatenate(
            [jnp.broadcast_to(jnp.sum(prod[:, h * HEAD:(h + 1) * HEAD], axis=-1, keepdims=True), (tm, HEAD))
             for h in range(NHG)], axis=1)
        for value, out_refs in ((dat, dat_refs), (delta, dl_refs)):
            _put_tile(tile, value)
            for dil, ref in zip(dils, out_refs):
                if dil == 1:
                    ref[...] = value.astype(ref.dtype)
                else:
                    _store_perm(ref, tile, dil)

    row = pl.BlockSpec((tm, D), lambda i: (i, 0))
    row2 = pl.BlockSpec((tm, 2 * D), lambda i: (i, 0))
    rowa = pl.BlockSpec((tm, AW), lambda i: (i, 0))
    aspecs = [rowa if d == 1 else _perm_spec(d, AW) for d in dils]

    def ashapes(dtype):
        return [jax.ShapeDtypeStruct((T, AW), dtype) if d == 1 else _perm_shape(d, AW, dtype) for d in dils]

    out = pl.pallas_call(
        body, name=name, grid=(T // tm,),
        in_specs=[row, row2, row, row, rowa, _const_spec((D, D)), _const_spec((D, AW)), _const_spec((D, D))]
        + [pl.BlockSpec((D // 2, D), lambda i, q=q: (GATE_BLOCK0 + q, 0), pipeline_mode=pl.Buffered(1))
           for q in range(4)],
        out_specs=[row2, row, row, row, row, row] + aspecs + aspecs,
        out_shape=[jax.ShapeDtypeStruct((T, 2 * D), BF16), jax.ShapeDtypeStruct((T, D), BF16),
                   jax.ShapeDtypeStruct((T, D), BF16), jax.ShapeDtypeStruct((T, D), BF16),
                   jax.ShapeDtypeStruct((T, D), BF16), jax.ShapeDtypeStruct((T, D), F32)]
        + ashapes(BF16) + ashapes(F32),
        scratch_shapes=[_tile_scratch(AW)],
        compiler_params=_cp(("parallel",)),
    )(dx2, gates, yc, ya, attn, wc, wa_t, wo, win_t, win_t, win_t, win_t)
    dats = [o.reshape(T, AW) for o in out[6:6 + ng]]
    deltas = [o.reshape(T, AW) for o in out[6 + ng:6 + 2 * ng]]
    return out[0], out[1], out[2], out[3], out[4], out[5], dats, deltas


def _peer(k):
    x, y, c = lax.axis_index("x"), lax.axis_index("y"), lax.axis_index("c")
    px = 1 - x if k & 4 else x
    py = 1 - y if k & 2 else y
    pc = 1 - c if k & 1 else c
    return (px, py, pc), 4 * px + 2 * py + pc


HBM_SPEC = pl.BlockSpec(memory_space=pltpu.HBM)
SEM_SPEC = pl.BlockSpec(memory_space=pltpu.SEMAPHORE)
EFFECT = pltpu.SideEffectType.DATAFLOW_SIDE_EFFECTING


def _my_place():
    return 4 * lax.axis_index("x") + 2 * lax.axis_index("y") + lax.axis_index("c")


def _tie(a, order_after, name):
    na = len(order_after)

    def body(*refs):
        del refs

    return pl.pallas_call(
        body, name=name, in_specs=[pl.BlockSpec(memory_space=pl.ANY)] * (1 + na),
        out_specs=pl.BlockSpec(memory_space=pl.ANY), out_shape=jax.ShapeDtypeStruct(a.shape, a.dtype),
        input_output_aliases={0: 0},
    )(a, *order_after)


def _prep_gather(ws, order_after, name):
    me = jnp.reshape(_my_place(), (1,)).astype(jnp.int32)
    n = len(ws)
    na = len(order_after)
    shapes = [((32, wv.shape[1]), F32) if wv.shape[0] == CONV_W else (wv.shape, BF16) for wv in ws]

    def body(me_ref, *refs):
        del me_ref
        ins, outs = refs[:n], refs[n + na:]
        for wv, i_ref, o_ref in zip(ws, ins, outs):
            if wv.shape[0] == CONV_W:
                o_ref[pl.ds(0, CONV_W), :] = i_ref[...]
                o_ref[pl.ds(CONV_W, 1), :] = jnp.zeros((1, wv.shape[1]), F32)
            else:
                o_ref[...] = i_ref[...].astype(BF16)

    grid_spec = pltpu.PrefetchScalarGridSpec(
        num_scalar_prefetch=1, grid=(1,),
        in_specs=[pl.BlockSpec(wv.shape, lambda i, m: (0, 0)) for wv in ws]
        + [pl.BlockSpec(memory_space=pl.ANY)] * na,
        out_specs=[pl.BlockSpec(shp, lambda i, m: (m[0], 0)) for shp, _ in shapes])
    return pl.pallas_call(
        body, name=name, grid_spec=grid_spec,
        out_shape=[jax.ShapeDtypeStruct((NDEV * shp[0], shp[1]), dt) for shp, dt in shapes],
        compiler_params=_cp(("arbitrary",)),
    )(me, *ws, *order_after)


GATHER_A = ((1, 0), (2, 0), (4, 0), (6, 0))
GATHER_B = ((1, 2), (1, 4), (1, 6))
GATHER_DIRECT = tuple((k, 0) for k in range(1, NDEV))


def _gather_start(lands, plan, order_after, name):
    n = len(lands)
    na = len(order_after)
    npl = len(plan)

    def body(*refs):
        land_refs = refs[:n]
        send, recv = refs[n + na], refs[n + na + 1]
        token = refs[-1]
        for w in range(n):
            rows = lands[w].shape[0] // NDEV
            for p, (k, j) in enumerate(plan):
                peer, _ = _peer(k)
                _, blk = _peer(j)
                part = land_refs[w].at[pl.ds(blk * rows, rows)]
                i = w * npl + p
                pltpu.make_async_remote_copy(src_ref=part, dst_ref=part, send_sem=send.at[i], recv_sem=recv.at[i],
                                             device_id=peer, device_id_type=MESH_ID).start()
        token[...] = jnp.zeros_like(token)

    nsem = n * npl
    bufs = [pltpu.with_memory_space_constraint(a, pltpu.HBM) for a in lands]
    out = pl.pallas_call(
        body, name=name,
        in_specs=[HBM_SPEC] * n + [pl.BlockSpec(memory_space=pl.ANY)] * na,
        out_specs=[SEM_SPEC, SEM_SPEC] + [HBM_SPEC] * n + [pl.BlockSpec(memory_space=pltpu.VMEM)],
        out_shape=[pltpu.SemaphoreType.DMA((nsem,)), pltpu.SemaphoreType.DMA((nsem,))]
        + [pltpu.HBM(a.shape, a.dtype) for a in bufs] + [jax.ShapeDtypeStruct((8, 128), F32)],
        input_output_aliases={i: 2 + i for i in range(n)},
        compiler_params=pltpu.CompilerParams(has_side_effects=EFFECT),
    )(*bufs, *order_after)
    return out[0], out[1], out[2:2 + n], out[-1]


def _gather_wait(started, plan, order_after, name):
    send, recv, lands, _ = started
    n = len(lands)
    na = len(order_after)
    npl = len(plan)

    def body(*refs):
        land_refs = refs[:n]
        send_ref, recv_ref = refs[n], refs[n + 1]
        for w in range(n):
            rows = lands[w].shape[0] // NDEV
            for p, (k, j) in enumerate(plan):
                peer, _ = _peer(k)
                _, blk = _peer(j)
                part = land_refs[w].at[pl.ds(blk * rows, rows)]
                i = w * npl + p
                cp = pltpu.make_async_remote_copy(src_ref=part, dst_ref=part, send_sem=send_ref.at[i],
                                                  recv_sem=recv_ref.at[i], device_id=peer, device_id_type=MESH_ID)
                cp.wait_send()
                cp.wait_recv()

    out = pl.pallas_call(
        body, name=name,
        in_specs=[HBM_SPEC] * n + [SEM_SPEC, SEM_SPEC] + [pl.BlockSpec(memory_space=pl.ANY)] * na,
        out_specs=[HBM_SPEC] * n,
        out_shape=[pltpu.HBM(a.shape, a.dtype) for a in lands],
        input_output_aliases={i: i for i in range(n)},
        compiler_params=pltpu.CompilerParams(has_side_effects=EFFECT),
    )(*lands, send, recv, *order_after)
    return list(out)


def _copy_ends(kind, src, land, me, plin, k):
    if kind == "scatter":
        rows = src.shape[0] // NDEV
        return src.at[pl.ds(plin * rows, rows)], land.at[k - 1]
    return src, land.at[me]


def _landing(kind, src):
    me = _my_place()
    if kind == "scatter":
        return lax.empty((NDEV - 1, src.shape[0] // NDEV) + src.shape[1:], src.dtype)
    land = lax.empty((NDEV,) + src.shape, src.dtype)
    return lax.dynamic_update_slice(land, src[None], (me,) + (0,) * src.ndim)


def _send_start(kinds, srcs, order_after, name):
    n = len(srcs)
    lands = [_landing(kd, s) for kd, s in zip(kinds, srcs)]
    na = len(order_after)

    def body(*refs):
        src_refs, land_refs = refs[:n], refs[n:2 * n]
        send, recv = refs[2 * n + na], refs[2 * n + na + 1]
        token = refs[-1]
        _, me = _peer(0)
        for w in range(n):
            for k in range(1, NDEV):
                peer, plin = _peer(k)
                s, d = _copy_ends(kinds[w], src_refs[w], land_refs[w], me, plin, k)
                i = w * (NDEV - 1) + k - 1
                pltpu.make_async_remote_copy(src_ref=s, dst_ref=d, send_sem=send.at[i], recv_sem=recv.at[i],
                                             device_id=peer, device_id_type=MESH_ID).start()
        token[...] = jnp.zeros_like(token)

    nsem = n * (NDEV - 1)
    bufs = [pltpu.with_memory_space_constraint(a, pltpu.HBM) for a in list(srcs) + lands]
    out = pl.pallas_call(
        body, name=name,
        in_specs=[HBM_SPEC] * (2 * n) + [pl.BlockSpec(memory_space=pl.ANY)] * na,
        out_specs=[SEM_SPEC, SEM_SPEC] + [HBM_SPEC] * (2 * n) + [pl.BlockSpec(memory_space=pltpu.VMEM)],
        out_shape=[pltpu.SemaphoreType.DMA((nsem,)), pltpu.SemaphoreType.DMA((nsem,))]
        + [pltpu.HBM(a.shape, a.dtype) for a in bufs] + [jax.ShapeDtypeStruct((8, 128), F32)],
        input_output_aliases={i: 2 + i for i in range(2 * n)},
        compiler_params=pltpu.CompilerParams(has_side_effects=EFFECT),
    )(*bufs, *order_after)
    return out[0], out[1], out[2:2 + n], out[2 + n:2 + 2 * n], out[-1]


def _send_wait(kinds, started, order_after, name):
    send, recv, srcs, lands, _ = started
    n = len(srcs)
    na = len(order_after)

    def body(*refs):
        src_refs, land_refs = refs[:n], refs[n:2 * n]
        send_ref, recv_ref = refs[2 * n], refs[2 * n + 1]
        _, me = _peer(0)
        for w in range(n):
            for k in range(1, NDEV):
                peer, plin = _peer(k)
                s, d = _copy_ends(kinds[w], src_refs[w], land_refs[w], me, plin, k)
                i = w * (NDEV - 1) + k - 1
                cp = pltpu.make_async_remote_copy(src_ref=s, dst_ref=d, send_sem=send_ref.at[i],
                                                  recv_sem=recv_ref.at[i], device_id=peer, device_id_type=MESH_ID)
                cp.wait_send()
                cp.wait_recv()

    bufs = list(srcs) + list(lands)
    out = pl.pallas_call(
        body, name=name,
        in_specs=[HBM_SPEC] * (2 * n) + [SEM_SPEC, SEM_SPEC] + [pl.BlockSpec(memory_space=pl.ANY)] * na,
        out_specs=[HBM_SPEC] * (2 * n),
        out_shape=[pltpu.HBM(a.shape, a.dtype) for a in bufs],
        input_output_aliases={i: i for i in range(2 * n)},
        compiler_params=pltpu.CompilerParams(has_side_effects=EFFECT),
    )(*bufs, send, recv, *order_after)
    return out[:n], out[n:]


def _gsum(own, land, name):
    rows, cols = own.shape
    tr = rows // 2 if rows * cols > 512 * 1024 and rows % 32 == 0 else rows

    def body(own_ref, l_ref, o_ref):
        tot = own_ref[...].astype(F32)
        for s in range(NDEV - 1):
            tot = tot + l_ref[s].astype(F32)
        o_ref[...] = tot

    return pl.pallas_call(
        body, name=name, grid=(rows // tr,),
        in_specs=[pl.BlockSpec((tr, cols), lambda i: (i, 0)),
                  pl.BlockSpec((NDEV - 1, tr, cols), lambda i: (0, i, 0))],
        out_specs=pl.BlockSpec((tr, cols), lambda i: (i, 0)),
        out_shape=jax.ShapeDtypeStruct((rows, cols), F32),
        compiler_params=_cp(("parallel",)),
    )(own, land)


def _adamw_math(w, g, m, v):
    m2 = B1 * m + (1.0 - B1) * g
    v2 = B2 * v + (1.0 - B2) * (g * g)
    m_hat = m2 / (1.0 - B1 ** STEP)
    v_hat = v2 / (1.0 - B2 ** STEP)
    delta = -LR * (m_hat / (jnp.sqrt(v_hat) + AEPS) + WD * w)
    return delta, m2, v2


def _adamw(w, g, m, v, name):
    rows, cols = w.shape
    tr = 256 if rows % 256 == 0 and rows > 256 else rows

    def body(w_ref, g_ref, m_ref, v_ref, d_ref, mo_ref, vo_ref):
        d, m2, v2 = _adamw_math(w_ref[...], g_ref[...], m_ref[...], v_ref[...])
        d_ref[...] = d
        mo_ref[...] = m2
        vo_ref[...] = v2

    blk = pl.BlockSpec((tr, cols), lambda i: (i, 0))
    return pl.pallas_call(
        body, name=name, grid=(rows // tr,), in_specs=[blk] * 4, out_specs=[blk] * 3,
        out_shape=[jax.ShapeDtypeStruct((rows, cols), F32)] * 3,
        compiler_params=_cp(("parallel",)),
    )(w, g, m, v)


UPD_TC = 256


def _update(src, land, w, m, v, name):
    rows, cols = land.shape[1:]
    tc = min(UPD_TC if rows > 512 else 2 * UPD_TC, cols)
    me = jnp.reshape(_my_place(), (1,)).astype(jnp.int32)

    def body(me_ref, own_ref, l_ref, w_ref, m_ref, v_ref, g_ref, d_ref, mo_ref, vo_ref):
        del me_ref
        g = own_ref[...].astype(F32)
        for s in range(NDEV - 1):
            g = g + l_ref[s].astype(F32)
        g_ref[...] = g
        d, m2, v2 = _adamw_math(w_ref[...], g, m_ref[...], v_ref[...])
        d_ref[...] = d
        mo_ref[...] = m2
        vo_ref[...] = v2

    wblk = pl.BlockSpec((rows, tc), lambda j, p: (0, j))
    grid_spec = pltpu.PrefetchScalarGridSpec(
        num_scalar_prefetch=1, grid=(cols // tc,),
        in_specs=[pl.BlockSpec((rows, tc), lambda j, p: (p[0], j)),
                  pl.BlockSpec((NDEV - 1, rows, tc), lambda j, p: (0, 0, j)), wblk, wblk, wblk],
        out_specs=[wblk] * 4)
    return pl.pallas_call(
        body, name=name, grid_spec=grid_spec, out_shape=[jax.ShapeDtypeStruct((rows, cols), F32)] * 4,
        compiler_params=_cp(("parallel",)),
    )(me, src, land, w, m, v)


def _small_update(vland, w8, m8, v8, name):
    def body(l_ref, w_ref, m_ref, v_ref, g_ref, d_ref, mo_ref, vo_ref):
        g = l_ref[0]
        for s in range(1, NDEV):
            g = g + l_ref[s]
        g_ref[...] = g
        d, m2, v2 = _adamw_math(w_ref[...], g, m_ref[...], v_ref[...])
        d_ref[...] = d
        mo_ref[...] = m2
        vo_ref[...] = v2

    return pl.pallas_call(
        body, name=name, out_shape=[jax.ShapeDtypeStruct((8, D), F32)] * 4,
        compiler_params=_cp(None),
    )(vland, w8, m8, v8)


def kernel(x, ffn1_norm, ffn1_w_gate, ffn1_w_up, ffn1_w_down, mix_norm, w_in, conv_dw_kernel, conv_dw_bias, conv_ln_gain, conv_ln_bias, conv_w_out, attn_w_out, w_o, ffn2_norm, ffn2_w_gate, ffn2_w_up, ffn2_w_down, final_norm, loss_target, m_ffn1_norm, m_ffn1_w_gate, m_ffn1_w_up, m_ffn1_w_down, m_mix_norm, m_w_in, m_conv_dw_kernel, m_conv_dw_bias, m_conv_ln_gain, m_conv_ln_bias, m_conv_w_out, m_attn_w_out, m_w_o, m_ffn2_norm, m_ffn2_w_gate, m_ffn2_w_up, m_ffn2_w_down, m_final_norm, v_ffn1_norm, v_ffn1_w_gate, v_ffn1_w_up, v_ffn1_w_down, v_mix_norm, v_w_in, v_conv_dw_kernel, v_conv_dw_bias, v_conv_ln_gain, v_conv_ln_bias, v_conv_w_out, v_attn_w_out, v_w_o, v_ffn2_norm, v_ffn2_w_gate, v_ffn2_w_up, v_ffn2_w_down, v_final_norm):
    names = ["ffn1_norm", "ffn1_w_gate", "ffn1_w_up", "ffn1_w_down", "mix_norm", "w_in", "conv_dw_kernel",
             "conv_dw_bias", "conv_ln_gain", "conv_ln_bias", "conv_w_out", "attn_w_out", "w_o", "ffn2_norm",
             "ffn2_w_gate", "ffn2_w_up", "ffn2_w_down", "final_norm"]
    w = dict(ffn1_norm=ffn1_norm, ffn1_w_gate=ffn1_w_gate, ffn1_w_up=ffn1_w_up, ffn1_w_down=ffn1_w_down, mix_norm=mix_norm, w_in=w_in, conv_dw_kernel=conv_dw_kernel, conv_dw_bias=conv_dw_bias, conv_ln_gain=conv_ln_gain, conv_ln_bias=conv_ln_bias, conv_w_out=conv_w_out, attn_w_out=attn_w_out, w_o=w_o, ffn2_norm=ffn2_norm, ffn2_w_gate=ffn2_w_gate, ffn2_w_up=ffn2_w_up, ffn2_w_down=ffn2_w_down, final_norm=final_norm)
    mo = dict(ffn1_norm=m_ffn1_norm, ffn1_w_gate=m_ffn1_w_gate, ffn1_w_up=m_ffn1_w_up, ffn1_w_down=m_ffn1_w_down, mix_norm=m_mix_norm, w_in=m_w_in, conv_dw_kernel=m_conv_dw_kernel, conv_dw_bias=m_conv_dw_bias, conv_ln_gain=m_conv_ln_gain, conv_ln_bias=m_conv_ln_bias, conv_w_out=m_conv_w_out, attn_w_out=m_attn_w_out, w_o=m_w_o, ffn2_norm=m_ffn2_norm, ffn2_w_gate=m_ffn2_w_gate, ffn2_w_up=m_ffn2_w_up, ffn2_w_down=m_ffn2_w_down, final_norm=m_final_norm)
    vo = dict(ffn1_norm=v_ffn1_norm, ffn1_w_gate=v_ffn1_w_gate, ffn1_w_up=v_ffn1_w_up, ffn1_w_down=v_ffn1_w_down, mix_norm=v_mix_norm, w_in=v_w_in, conv_dw_kernel=v_conv_dw_kernel, conv_dw_bias=v_conv_dw_bias, conv_ln_gain=v_conv_ln_gain, conv_ln_bias=v_conv_ln_bias, conv_w_out=v_conv_w_out, attn_w_out=v_attn_w_out, w_o=v_w_o, ffn2_norm=v_ffn2_norm, ffn2_w_gate=v_ffn2_w_gate, ffn2_w_up=v_ffn2_w_up, ffn2_w_down=v_ffn2_w_down, final_norm=v_final_norm)
    col_sharded = ("ffn1_w_gate", "ffn1_w_up", "w_in", "attn_w_out", "ffn2_w_gate", "ffn2_w_up")
    row_sharded = ("ffn1_w_down", "conv_w_out", "w_o", "ffn2_w_down")
    small = ("ffn1_norm", "mix_norm", "ffn2_norm", "final_norm", "conv_dw_bias", "conv_ln_gain", "conv_ln_bias")

    def landing_view(a, n):
        return jnp.transpose(a[0]) if n in col_sharded else a[0]

    def own_view(a, n):
        return jnp.transpose(a)[None] if n in col_sharded else a[None]

    ag_groups = (("ffn1_w_gate", "ffn1_w_up", "ffn1_w_down"),
                 ("w_in", "attn_w_out", "conv_w_out", "w_o", "conv_dw_kernel"),
                 ("ffn2_w_gate", "ffn2_w_up", "ffn2_w_down"))
    ag, order = [], []
    for gi, grp in enumerate(ag_groups):
        lands = _prep_gather([landing_view(w[n], n) for n in grp], order, f"gather_prep{gi}")
        st = _gather_start(lands, GATHER_DIRECT if gi == 2 else GATHER_A, [], f"gather_a_start{gi}")
        ag.append(st)
        order = [st[3]]

    def chips_in(gi, after):
        lands = _gather_wait(ag[gi], GATHER_A, after, f"gather_a_wait{gi}")
        return _gather_start(lands, GATHER_B, [], f"gather_b_start{gi}")

    def all_in(gi, st, after):
        return _gather_wait(st, GATHER_B, after, f"gather_b_wait{gi}")

    x0 = x[0]
    tgt = loss_target[0]
    gf = final_norm.reshape(1, D)

    wg1, wu1, wd1 = all_in(0, chips_in(0, [ag[2][3]]), [])
    x1, gg1, uu1, h2p = _ffn_fwd(x0, ffn1_norm, wg1, wu1, wd1, "ffn1_fwd", next_gain=mix_norm)
    h2 = h2p[0]
    win_t, wa_t, wc, wo, kern_blocks = all_in(1, chips_in(1, [x1]), [])
    kern = kern_blocks.reshape(NDEV, 32, D // NDEV).transpose(1, 0, 2).reshape(32, D)
    ptm = min(T, 2048)
    ab = _mm(h2, win_t, mode="nt", m=T, n=2 * D, k=D, tm=ptm, tn=512, tk=D, out_dtype=BF16, name="proj_conv")
    z1, z3b, gates = _conv_fwd(ab, kern, conv_dw_bias, conv_ln_gain, conv_ln_bias, "conv_fwd",
                               guest=(h2, win_t, (13, 14, 15, 16, 4, 7, 10), 512))
    qkv, qkv_col0 = [gates], [2 * D]
    for gi in range(1, len(GROUPS)):
        qkv.append(_mm(h2p[gi], win_t, mode="nt", m=T, n=3 * AW, k=D, tm=ptm, tn=AW, tk=D, out_dtype=BF16,
                       b_map=lambda i, j, kk, gi=gi: (4 + gi + 3 * j, 0), name=f"proj_qkv{gi}"))
        qkv_col0.append(0)
    outs, lses = [], []
    for gi, (_, dil) in enumerate(GROUPS):
        o, l = _attn_fwd(qkv[gi], gi, f"attn_fwd{gi}", col0=qkv_col0[gi])
        outs.append(o)
        lses.append(l)
    attnb, lse = _merge(outs, lses, "attn_merge")
    x2, yc, ya, mixedb = _mix_out(z3b, attnb, gates, wc, wa_t, wo, x1, "mix_out_fwd")
    wg2, wu2, wd2 = _gather_wait(ag[2], GATHER_DIRECT, [x2], "gather_a_wait2")
    gg2, uu2, dx3, dgf, loss_part = _ffn_fwd(x2, ffn2_norm, wg2, wu2, wd2, "ffn2_fwd", loss_of=(gf, tgt))

    dx2, dg3, dgb, dub, actb, hb, dob = _ffn_bwd(x2, ffn2_norm, gg2, uu2, dx3, wg2, wu2, wd2, "ffn2_bwd")
    grads = {}
    grads["ffn2_w_down"] = _wgrad(actb, dob, FF, D, "ffn2_dwd")
    rs_groups = [("ffn2_w_gate", "ffn2_w_up", "ffn2_w_down"),
                 ("attn_w_out", "conv_w_out", "w_o", "conv_dw_kernel"),
                 ("w_in",),
                 ("ffn1_w_gate",), ("ffn1_w_up",), ("ffn1_w_down",), ()]
    last = len(rs_groups) - 1
    rs = []

    dgates, dycb, dyab, dx2b, dz3, dh_gates, dattnb, delta = _mix_out_bwd(dx2, gates, yc, ya, attnb, wc, wa_t, wo,
                                                                          win_t, "mix_out_bwd")
    grads["w_o"] = _wgrad(mixedb, dx2b, D, D, "dw_o")
    grads["conv_w_out"] = _wgrad(z3b, dycb, D, D, "dw_conv_out")
    grads["attn_w_out"] = _wgrad(dyab, attnb, D, AW, "dw_attn_out")
    dab, dkern, dvec, grads["ffn2_w_gate"], grads["ffn2_w_up"] = _conv_bwd(
        dz3, z1, ab, kern, conv_ln_gain, conv_ln_bias, "conv_bwd", guest_lhs=(dgb, dub), guest_rhs=hb)
    grads["conv_dw_kernel"] = dkern.reshape(32, NDEV, D // NDEV).transpose(1, 0, 2).reshape(NDEV * 32, D // NDEV)
    rs.append(_send_start(["scatter"] * 3, [grads[n] for n in rs_groups[0]], [], "scatter_start0"))
    rs.append(_send_start(["scatter"] * 4, [grads[n] for n in rs_groups[1]], [rs[0][4]], "scatter_start1"))
    dattnb = [_tie(a, [rs[1][4]], f"tie_after_scatter1_{i}") for i, a in enumerate(dattnb)]

    dqkv, dq3s = [], []
    for gi, (_, dil) in enumerate(GROUPS):
        dq3 = _attn_bwd(qkv[gi], dattnb[gi], lse[gi], delta[gi], gi, f"attn_bwd{gi}", col0=qkv_col0[gi])
        dq3s.append(dq3)
        dqkv.append(dq3.reshape(3 * T, AW))

    wtk = min(T, 2048)
    dwin = _mm(dab, h2, mode="tn", m=2 * D, n=D, k=T, tm=2 * D, tn=D, tk=wtk, out_dtype=BF16, out_rows=IN_W,
               name="dw_in_conv")
    dwin = _mm(dgates, h2, mode="tn", m=2 * D, n=D, k=T, tm=512, tn=D, tk=wtk, out_dtype=BF16, out_rows=IN_W,
               o_map=lambda i, j, kk: (13 + i, 0), passthru=dwin, name="dw_in_gates")
    for gi in range(3):
        dwin = _mm(dqkv[gi], h2p[gi], mode="tn", m=3 * AW, n=D, k=T, tm=AW, tn=D, tk=wtk, out_dtype=BF16,
                   out_rows=IN_W, a_map=lambda i, j, kk: (i * (T // wtk) + kk, 0),
                   o_map=lambda i, j, kk, gi=gi: (4 + gi + 3 * i, 0), passthru=dwin, name=f"dw_in_qkv{gi}")
    grads["w_in"] = dwin
    rs.append(_send_start(["scatter"], [dwin], [rs[1][4]], "scatter_start2"))
    dab = _tie(dab, [rs[2][4]], "tie_after_scatter2")

    nrow = T // 1024
    dh = _mm(dab, win_t, mode="nn", m=T, n=D, k=2 * D, tm=1024, tn=D, tk=2 * D, out_dtype=F32, init=dh_gates,
             name="dproj_conv")
    dx1, dg2 = _rms_bwd(x1, mix_norm, dh, dq3s, win_t, dx2, "mix_norm_bwd")

    dgb, dub, actb, hb, dob = _ffn_bwd_pre(x0, ffn1_norm, gg1, uu1, dx1, wd1, "ffn1_bwd_pre")
    grads["ffn1_w_gate"] = _wgrad(dgb, hb, FF, D, "ffn1_dwg")
    rs.append(_send_start(["scatter"], [grads["ffn1_w_gate"]], [rs[2][4]], "scatter_start3"))
    hb = _tie(hb, [rs[3][4]], "tie_after_scatter3")
    grads["ffn1_w_up"] = _wgrad(dub, hb, FF, D, "ffn1_dwu")
    rs.append(_send_start(["scatter"], [grads["ffn1_w_up"]], [rs[3][4]], "scatter_start4"))
    dob = _tie(dob, [rs[4][4]], "tie_after_scatter4")
    grads["ffn1_w_down"] = _wgrad(actb, dob, FF, D, "ffn1_dwd")
    rs.append(_send_start(["scatter"], [grads["ffn1_w_down"]], [rs[4][4]], "scatter_start5"))
    dgb = _tie(dgb, [rs[5][4]], "tie_after_scatter5")
    dx0, dg1 = _ffn_bwd_dx(x0, ffn1_norm, dgb, dub, dx1, wg1, wu1, "ffn1_bwd_dx")
    vec = jnp.concatenate([dg1, dg2, dg3, dgf, dvec[0:3], jnp.broadcast_to(loss_part[:, :1], (1, D))], axis=0)
    rs.append(_send_start(["bcast"], [vec], [rs[5][4]], "scatter_start6"))

    g_out, d_out, m_out, v_out = {}, {}, {}, {}
    me = _my_place()
    after = [rs[last][4]]
    for gi, grp in enumerate(rs_groups):
        kinds = ["scatter"] * len(grp) + (["bcast"] if gi == last else [])
        srcs, lands = _send_wait(kinds, rs[gi], after, f"scatter_wait{gi}")
        for n, src, land in zip(grp, srcs, lands):
            if n == "conv_dw_kernel":
                rows = src.shape[0] // NDEV
                own = lax.dynamic_slice(src, (me * rows, 0), (rows, src.shape[1]))
                g = _gsum(own, land, f"gsum_{n}")[:CONV_W]
                d, m2, v2 = _adamw(w[n][0], g, mo[n][0], vo[n][0], f"adamw_{n}")
                after = [d]
                g, d, m2, v2 = g[None], d[None], m2[None], v2[None]
            else:
                res = _update(src, land, landing_view(w[n], n), landing_view(mo[n], n), landing_view(vo[n], n),
                              f"update_{n}")
                after = [res[1]]
                g, d, m2, v2 = (own_view(a, n) for a in res)
            g_out[n], d_out[n], m_out[n], v_out[n] = g, d, m2, v2
    vland = lands[-1]

    def rows8(src):
        return jnp.concatenate([src[n].reshape(1, D) for n in small] + [jnp.ones((1, D), F32)], axis=0)

    g8, d8, m8, v8 = _small_update(vland, rows8(w), rows8(mo), rows8(vo), "small_update")
    for r, n in enumerate(small):
        shp = w[n].shape
        g_out[n], d_out[n], m_out[n], v_out[n] = (a[r].reshape(shp) for a in (g8, d8, m8, v8))
    loss = g8[7, 0]

    return (loss, dx0[None], *[g_out[n] for n in names], *[d_out[n] for n in names],
            *[m_out[n] for n in names], *[v_out[n] for n in names])
```

```python
import numpy as np
import jax
import jax.numpy as jnp
from jax import lax
from jax.experimental import pallas as pl
from jax.experimental.pallas import tpu as pltpu

F32 = jnp.float32
BF16 = jnp.bfloat16

T = 4096
D = 1024
FF = 2816
NDEV = 8
CONV_W = 31
HEAD = 128
BLK = 128
GROUPS = ((128, 1), (512, 4), (2048, 16))
NHG = 4
AW = NHG * HEAD
IN_W = 2 * D + 3 * 3 * AW + 2 * D
EPS = 1e-6
B1, B2, LR, AEPS, WD, STEP = 0.9, 0.999, 0.001, 1e-08, 0.01, 10
NEG = -1e30
VMEM_LIMIT = 56 * 1024 * 1024
MESH_ID = pl.DeviceIdType.MESH

NT = (((1,), (1,)), ((), ()))
NN = (((1,), (0,)), ((), ()))
TN = (((0,), (0,)), ((), ()))
_DIMS = {"nn": NN, "nt": NT, "tn": TN}


def _cp(sem=None):
    return pltpu.CompilerParams(dimension_semantics=sem, vmem_limit_bytes=VMEM_LIMIT)


def _sig(v):
    return 1.0 / (1.0 + jnp.exp(-v))


def _dot(a, b, dims):
    return lax.dot_general(a, b, dims, preferred_element_type=F32)


def _const_spec(shape):
    nd = len(shape)
    return pl.BlockSpec(shape, lambda *_: (0,) * nd)


def _mm(a, b, *, mode, m, n, k, tm, tn, tk, out_dtype, name, a_map=None, b_map=None,
        o_map=None, out_rows=None, init=None, passthru=None):
    gi, gj, gk = m // tm, n // tn, k // tk
    assert gi * tm == m and gj * tn == n and gk * tk == k, (name, m, n, k, tm, tn, tk)
    if mode == "nn":
        a_blk, b_blk = (tm, tk), (tk, tn)
        da, db = (lambda i, j, kk: (i, kk)), (lambda i, j, kk: (kk, j))
    elif mode == "nt":
        a_blk, b_blk = (tm, tk), (tn, tk)
        da, db = (lambda i, j, kk: (i, kk)), (lambda i, j, kk: (j, kk))
    else:
        a_blk, b_blk = (tk, tm), (tk, tn)
        da, db = (lambda i, j, kk: (kk, i)), (lambda i, j, kk: (kk, j))
    a_map = a_map or da
    b_map = b_map or db
    o_map = o_map or (lambda i, j, kk: (i, j))
    dims = _DIMS[mode]
    extra = init if init is not None else passthru
    out_rows = out_rows or m

    def body(*refs):
        if init is not None:
            a_ref, b_ref, i_ref, o_ref = refs[:4]
        elif passthru is not None:
            a_ref, b_ref, _, o_ref = refs[:4]
        else:
            a_ref, b_ref, o_ref = refs[:3]
        if gk == 1:
            prod = _dot(a_ref[...], b_ref[...], dims)
            if init is not None:
                prod = prod + i_ref[...].astype(F32)
            o_ref[...] = prod.astype(out_dtype)
            return
        acc = refs[-1]
        kk = pl.program_id(2)

        @pl.when(kk == 0)
        def _():
            if init is not None:
                acc[...] = i_ref[...].astype(F32)
            else:
                acc[...] = jnp.zeros_like(acc)

        acc[...] += _dot(a_ref[...], b_ref[...], dims)

        @pl.when(kk == gk - 1)
        def _():
            o_ref[...] = acc[...].astype(out_dtype)

    in_specs = [pl.BlockSpec(a_blk, a_map), pl.BlockSpec(b_blk, b_map)]
    args = [a, b]
    aliases = {}
    if init is not None:
        in_specs.append(pl.BlockSpec((tm, tn), o_map))
        args.append(init)
        aliases = {2: 0}
    elif passthru is not None:
        in_specs.append(pl.BlockSpec(memory_space=pl.ANY))
        args.append(passthru)
        aliases = {2: 0}
    out_dt = extra.dtype if extra is not None else out_dtype
    assert out_dt == out_dtype
    return pl.pallas_call(
        body, name=name, grid=(gi, gj, gk),
        in_specs=in_specs, out_specs=pl.BlockSpec((tm, tn), o_map),
        out_shape=jax.ShapeDtypeStruct((out_rows, n), out_dtype),
        scratch_shapes=[pltpu.VMEM((tm, tn), F32)] if gk > 1 else [],
        input_output_aliases=aliases,
        compiler_params=_cp(("parallel", "parallel", "arbitrary")),
    )(*args)


def _ffn_fwd(x, g, wg_t, wu_t, wd, name, next_gain=None, loss_of=None):
    tm, fc = PERM_TM, 256
    nc = FF // fc
    n_in = 5 + (1 if next_gain is not None else 0) + (2 if loss_of is not None else 0)

    def body(*refs):
        x_ref, g_ref, wg_ref, wu_ref, wd_ref = refs[:5]
        extra_in, outs = refs[5:n_in], refs[n_in:]
        act_ref = outs[-1]
        xv = x_ref[...]
        r = lax.rsqrt(jnp.mean(xv * xv, axis=-1, keepdims=True) + EPS)
        h = (xv * r * g_ref[...]).astype(BF16)
        gg_ref, uu_ref = (outs[0], outs[1]) if loss_of is not None else (outs[1], outs[2])
        for c in range(nc):
            sl = pl.ds(c * fc, fc)
            gg = _dot(h, wg_ref[sl, :], NT)
            uu = _dot(h, wu_ref[sl, :], NT)
            gg_ref[:, sl] = gg.astype(BF16)
            uu_ref[:, sl] = uu.astype(BF16)
            act_ref[:, sl] = (gg * _sig(gg) * uu).astype(BF16)
        y = xv + 0.5 * _dot(act_ref[...], wd_ref[...], NN)
        if loss_of is not None:
            _final_math(y, extra_in[0][...], extra_in[1][...], outs[2], outs[3], outs[4], pl.program_id(0))
            return
        outs[0][...] = y
        if next_gain is not None:
            tile = outs[-2]
            r2 = lax.rsqrt(jnp.mean(y * y, axis=-1, keepdims=True) + EPS)
            hv = y * r2 * extra_in[0][...]
            outs[3][...] = hv.astype(BF16)
            _put_tile(tile, hv)
            for dil, p_ref in zip(DILS, outs[4:4 + len(DILS)]):
                _store_perm(p_ref, tile, dil)

    wspec = pl.BlockSpec((FF, D), lambda i: (0, 0), pipeline_mode=pl.Buffered(1))
    row_d = pl.BlockSpec((tm, D), lambda i: (i, 0))
    row_f = pl.BlockSpec((tm, FF), lambda i: (i, 0))
    in_specs = [row_d, _const_spec((1, D)), wspec, wspec, wspec]
    args = [x, g, wg_t, wu_t, wd]
    f_shape = jax.ShapeDtypeStruct((T, FF), BF16)
    scratch = [pltpu.VMEM((tm, FF), BF16)]
    if loss_of is not None:
        in_specs += [_const_spec((1, D)), row_d]
        args += list(loss_of)
        out_specs = [row_f, row_f, row_d, _const_spec((1, D)), _const_spec((1, 128))]
        out_shape = [f_shape, f_shape, jax.ShapeDtypeStruct((T, D), F32), jax.ShapeDtypeStruct((1, D), F32),
                     jax.ShapeDtypeStruct((1, 128), F32)]
    else:
        out_specs = [row_d, row_f, row_f]
        out_shape = [jax.ShapeDtypeStruct((T, D), F32), f_shape, f_shape]
        if next_gain is not None:
            in_specs.append(_const_spec((1, D)))
            args.append(next_gain)
            out_specs += [row_d] + [_perm_spec(d, D) for d in DILS]
            out_shape += [jax.ShapeDtypeStruct((T, D), BF16)] + [_perm_shape(d, D, BF16) for d in DILS]
            scratch = [_tile_scratch(D)] + scratch
    out = pl.pallas_call(
        body, name=name, grid=(T // tm,), in_specs=in_specs, out_specs=out_specs, out_shape=out_shape,
        scratch_shapes=scratch,
        compiler_params=_cp(("arbitrary",) if loss_of is not None else ("parallel",)),
    )(*args)
    if next_gain is not None:
        return out[0], out[1], out[2], [out[3]] + [o.reshape(T, D) for o in out[4:]]
    return tuple(out)


def _ffn_bwd(x, g, gg_all, uu_all, dout, wg_t, wu_t, wd, name):
    tm, fc = 256, 256
    nc = FF // fc

    def body(x_ref, g_ref, gg_ref, uu_ref, do_ref, wg_ref, wu_ref, wd_ref,
             dx_ref, dgam_ref, dg_ref, du_ref, act_ref, h_ref, db_ref):
        i = pl.program_id(0)
        xv = x_ref[...]
        r = lax.rsqrt(jnp.mean(xv * xv, axis=-1, keepdims=True) + EPS)
        xhat = xv * r
        gam = g_ref[...]
        h_ref[...] = (xhat * gam).astype(BF16)
        dov = do_ref[...]
        dbv = (0.5 * dov).astype(BF16)
        db_ref[...] = dbv
        for c in range(nc):
            sl = pl.ds(c * fc, fc)
            da = _dot(dbv, wd_ref[sl, :], NT)
            gg = gg_ref[:, sl].astype(F32)
            uu = uu_ref[:, sl].astype(F32)
            s = _sig(gg)
            si = gg * s
            dgv = (da * uu * (s * (1.0 + gg * (1.0 - s)))).astype(BF16)
            duv = (da * si).astype(BF16)
            dg_ref[:, sl] = dgv
            du_ref[:, sl] = duv
            act_ref[:, sl] = (si * uu).astype(BF16)
        dh = _dot(dg_ref[...], wg_ref[...], NN) + _dot(du_ref[...], wu_ref[...], NN)

        @pl.when(i == 0)
        def _():
            dgam_ref[...] = jnp.zeros_like(dgam_ref)

        dgam_ref[...] += jnp.sum(dh * xhat, axis=0, keepdims=True)
        dxh = dh * gam
        dx_ref[...] = dov + r * (dxh - xhat * jnp.mean(dxh * xhat, axis=-1, keepdims=True))

    wspec = pl.BlockSpec((FF, D), lambda i: (0, 0), pipeline_mode=pl.Buffered(1))
    row_d = pl.BlockSpec((tm, D), lambda i: (i, 0))
    row_f = pl.BlockSpec((tm, FF), lambda i: (i, 0))
    return pl.pallas_call(
        body, name=name, grid=(T // tm,),
        in_specs=[row_d, _const_spec((1, D)), row_f, row_f, row_d, wspec, wspec, wspec],
        out_specs=[row_d, _const_spec((1, D)), row_f, row_f, row_f, row_d, row_d],
        out_shape=[jax.ShapeDtypeStruct((T, D), F32), jax.ShapeDtypeStruct((1, D), F32),
                   jax.ShapeDtypeStruct((T, FF), BF16), jax.ShapeDtypeStruct((T, FF), BF16),
                   jax.ShapeDtypeStruct((T, FF), BF16), jax.ShapeDtypeStruct((T, D), BF16),
                   jax.ShapeDtypeStruct((T, D), BF16)],
        compiler_params=_cp(("arbitrary",)),
    )(x, g, gg_all, uu_all, dout, wg_t, wu_t, wd)


def _ffn_bwd_pre(x, g, gg_all, uu_all, dout, wd, name):
    tm, fc = 512, 256
    nc = FF // fc

    def body(x_ref, g_ref, gg_ref, uu_ref, do_ref, wd_ref, dg_ref, du_ref, act_ref, h_ref, db_ref):
        xv = x_ref[...]
        r = lax.rsqrt(jnp.mean(xv * xv, axis=-1, keepdims=True) + EPS)
        h_ref[...] = (xv * r * g_ref[...]).astype(BF16)
        dbv = (0.5 * do_ref[...]).astype(BF16)
        db_ref[...] = dbv
        for c in range(nc):
            sl = pl.ds(c * fc, fc)
            da = _dot(dbv, wd_ref[sl, :], NT)
            gg = gg_ref[:, sl].astype(F32)
            uu = uu_ref[:, sl].astype(F32)
            s = _sig(gg)
            si = gg * s
            dg_ref[:, sl] = (da * uu * (s * (1.0 + gg * (1.0 - s)))).astype(BF16)
            du_ref[:, sl] = (da * si).astype(BF16)
            act_ref[:, sl] = (si * uu).astype(BF16)

    wspec = pl.BlockSpec((FF, D), lambda i: (0, 0), pipeline_mode=pl.Buffered(1))
    row_d = pl.BlockSpec((tm, D), lambda i: (i, 0))
    row_f = pl.BlockSpec((tm, FF), lambda i: (i, 0))
    return pl.pallas_call(
        body, name=name, grid=(T // tm,),
        in_specs=[row_d, _const_spec((1, D)), row_f, row_f, row_d, wspec],
        out_specs=[row_f, row_f, row_f, row_d, row_d],
        out_shape=[jax.ShapeDtypeStruct((T, FF), BF16), jax.ShapeDtypeStruct((T, FF), BF16),
                   jax.ShapeDtypeStruct((T, FF), BF16), jax.ShapeDtypeStruct((T, D), BF16),
                   jax.ShapeDtypeStruct((T, D), BF16)],
        compiler_params=_cp(("parallel",)),
    )(x, g, gg_all, uu_all, dout, wd)


def _ffn_bwd_dx(x, g, dgb, dub, dout, wg_t, wu_t, name):
    tm = 512

    def body(x_ref, g_ref, dg_ref, du_ref, do_ref, wg_ref, wu_ref, dx_ref, dgam_ref):
        i = pl.program_id(0)
        xv = x_ref[...]
        r = lax.rsqrt(jnp.mean(xv * xv, axis=-1, keepdims=True) + EPS)
        xhat = xv * r
        gam = g_ref[...]
        dh = _dot(dg_ref[...], wg_ref[...], NN) + _dot(du_ref[...], wu_ref[...], NN)

        @pl.when(i == 0)
        def _():
            dgam_ref[...] = jnp.zeros_like(dgam_ref)

        dgam_ref[...] += jnp.sum(dh * xhat, axis=0, keepdims=True)
        dxh = dh * gam
        dx_ref[...] = do_ref[...] + r * (dxh - xhat * jnp.mean(dxh * xhat, axis=-1, keepdims=True))

    wspec = pl.BlockSpec((FF, D), lambda i: (0, 0), pipeline_mode=pl.Buffered(1))
    row_d = pl.BlockSpec((tm, D), lambda i: (i, 0))
    row_f = pl.BlockSpec((tm, FF), lambda i: (i, 0))
    return pl.pallas_call(
        body, name=name, grid=(T // tm,),
        in_specs=[row_d, _const_spec((1, D)), row_f, row_f, row_d, wspec, wspec],
        out_specs=[row_d, _const_spec((1, D))],
        out_shape=[jax.ShapeDtypeStruct((T, D), F32), jax.ShapeDtypeStruct((1, D), F32)],
        compiler_params=_cp(("arbitrary",)),
    )(x, g, dgb, dub, dout, wg_t, wu_t)


def _wgrad(a, b, m, n, name):
    tm = m // 2 if m == FF else m
    return _mm(a, b, mode="tn", m=m, n=n, k=T, tm=tm, tn=n, tk=min(T, 2048), out_dtype=BF16, name=name)


PERM_TM = 512
DILS = tuple(d for _, d in GROUPS if d > 1)


def _perm_spec(dil, cols):
    return pl.BlockSpec((dil, PERM_TM // dil, cols), lambda i: (0, i, 0))


def _perm_shape(dil, cols, dtype):
    return jax.ShapeDtypeStruct((dil, T // dil, cols), dtype)


LANES = 128


def _tile_scratch(cols):
    return pltpu.VMEM((cols // LANES, PERM_TM, LANES), F32)


def _put_tile(tile, value):
    for c in range(tile.shape[0]):
        tile[c] = value[:, c * LANES:(c + 1) * LANES]


def _get_tile(tile):
    return jnp.concatenate([tile[c] for c in range(tile.shape[0])], axis=1)


def _store_perm(out_ref, tile, dil):
    for r in range(dil):
        for c in range(tile.shape[0]):
            out_ref[r, :, pl.ds(c * LANES, LANES)] = tile[c, pl.ds(r, PERM_TM // dil, stride=dil), :].astype(
                out_ref.dtype)


def _load_unperm(in_ref, tile, dil):
    for r in range(dil):
        for c in range(tile.shape[0]):
            tile[c, pl.ds(r, PERM_TM // dil, stride=dil), :] = in_ref[r, :, pl.ds(c * LANES, LANES)].astype(F32)


def _final_math(xv, gam, tgt, dx_ref, dgam_ref, loss_ref, i):
    r = lax.rsqrt(jnp.mean(xv * xv, axis=-1, keepdims=True) + EPS)
    xhat = xv * r
    err = xhat * gam - tgt
    part = 0.5 * jnp.sum(jnp.mean(err * err, axis=-1, keepdims=True), axis=0, keepdims=True)
    dy = err * (1.0 / D)

    @pl.when(i == 0)
    def _():
        dgam_ref[...] = jnp.zeros_like(dgam_ref)
        loss_ref[...] = jnp.zeros_like(loss_ref)

    dgam_ref[...] += jnp.sum(dy * xhat, axis=0, keepdims=True)
    loss_ref[...] += jnp.broadcast_to(part, loss_ref.shape)
    dxh = dy * gam
    dx_ref[...] = r * (dxh - xhat * jnp.mean(dxh * xhat, axis=-1, keepdims=True))


QKV_BLOCK0 = 2 * D // AW


def _rms_bwd(x, g, dh0, dlin, lin_blocks, dqkvs, win_t, dres, name):
    tm = PERM_TM
    dils = [d for _, d in GROUPS]
    ng = len(dils)
    nl = len(lin_blocks)
    assert len(dqkvs) == ng

    def body(*refs):
        x_ref, g_ref, dh0_ref, dl_ref = refs[:4]
        dq_refs = refs[4:4 + 3 * ng]
        w_refs = refs[4 + 3 * ng:4 + 6 * ng]
        wl_refs = refs[4 + 6 * ng:4 + 6 * ng + nl]
        dr_ref, dx_ref, dgam_ref = refs[4 + 6 * ng + nl:7 + 6 * ng + nl]
        tile = refs[7 + 6 * ng + nl]
        stages = refs[8 + 6 * ng + nl:]
        i = pl.program_id(0)
        xv = x_ref[...]
        r = lax.rsqrt(jnp.mean(xv * xv, axis=-1, keepdims=True) + EPS)
        xhat = xv * r
        gam = g_ref[...]
        dh = dh0_ref[...]
        for q in range(nl):
            dh = dh + _dot(dl_ref[:, pl.ds(q * AW, AW)], wl_refs[q][...], NN)
        si = 0
        for gi, dil in enumerate(dils):
            part = None
            for p in range(3):
                blk = dq_refs[3 * gi + p][...]
                term = _dot(blk.reshape(tm, AW), w_refs[3 * gi + p][...], NN)
                part = term if part is None else part + term
            if dil > 1:
                stage = stages[si]
                si += 1
                stage[...] = part.reshape(dil, tm // dil, D)
                _load_unperm(stage, tile, dil)
                part = _get_tile(tile)
            dh = dh + part

        @pl.when(i == 0)
        def _():
            dgam_ref[...] = jnp.zeros_like(dgam_ref)

        dgam_ref[...] += jnp.sum(dh * xhat, axis=0, keepdims=True)
        dxh = dh * gam
        dx_ref[...] = dr_ref[...] + r * (dxh - xhat * jnp.mean(dxh * xhat, axis=-1, keepdims=True))

    row_d = pl.BlockSpec((tm, D), lambda i: (i, 0))
    dq_specs, dq_args, w_specs = [], [], []
    for gi, (d, a) in enumerate(zip(dils, dqkvs)):
        for p in range(3):
            if d == 1:
                dq_specs.append(pl.BlockSpec((None, tm, AW), lambda i, p=p: (p, i, 0)))
                dq_args.append(a)
            else:
                dq_specs.append(pl.BlockSpec((None, d, tm // d, AW), lambda i, p=p: (p, 0, i, 0)))
                dq_args.append(a.reshape(3, d, T // d, AW))
            w_specs.append(pl.BlockSpec((AW, D), lambda i, q=QKV_BLOCK0 + gi + 3 * p: (q, 0),
                                        pipeline_mode=pl.Buffered(1)))
    return pl.pallas_call(
        body, name=name, grid=(T // tm,),
        in_specs=[row_d, _const_spec((1, D)), row_d, pl.BlockSpec((tm, nl * AW), lambda i: (i, 0))] + dq_specs + w_specs
        + [pl.BlockSpec((AW, D), lambda i, q=q: (q, 0), pipeline_mode=pl.Buffered(1)) for q in lin_blocks] + [row_d],
        out_specs=[row_d, _const_spec((1, D))],
        out_shape=[jax.ShapeDtypeStruct((T, D), F32), jax.ShapeDtypeStruct((1, D), F32)],
        scratch_shapes=[_tile_scratch(D)] + [pltpu.VMEM((d, tm // d, D), F32) for d in dils if d > 1],
        compiler_params=_cp(("arbitrary",)),
    )(x, g, dh0, dlin, *dq_args, *([win_t] * (3 * ng + nl)), dres)


CONV_TM = 256
CONV_HALO = 32
CONV_RB = 16


def _glu(ab):
    ab = ab.astype(F32)
    return ab[:, :D] * _sig(ab[:, D:])


def _ln_stats(z1):
    mu = jnp.mean(z1, axis=-1, keepdims=True)
    zc = z1 - mu
    rstd = lax.rsqrt(jnp.mean(zc * zc, axis=-1, keepdims=True) + EPS)
    return zc * rstd, rstd


def _fill_shifts(zs):
    n = zs.shape[1] - 8
    for s in range(1, 8):
        zs[s, pl.ds(0, n), :] = zs[0, pl.ds(s, n), :]


def _shifted(zs, start, rows):
    q, s = divmod(start, 8)
    return zs[s, pl.ds(8 * q, rows), :]


def _conv_fwd(ab, kern, dwb, lng, lnb, name, guest=None):
    tm, hl, rb = CONV_TM, CONV_HALO, CONV_RB
    off = hl - (CONV_W - 1)
    if guest is not None:
        g_a, g_b, g_blocks, g_rows = guest
        g_nblk = len(g_blocks)

    def body(ab_ref, abh_ref, k_ref, dwb_ref, lng_ref, lnb_ref, *rest):
        if guest is not None:
            ga_ref, gb_refs, rest = rest[0], rest[1:1 + g_nblk], rest[1 + g_nblk:]
            z1_ref, z3_ref, go_ref, zs = rest
            for q, gb_ref in enumerate(gb_refs):
                go_ref[:, pl.ds(q * g_rows, g_rows)] = _dot(ga_ref[...], gb_ref[...], NT).astype(BF16)
        else:
            z1_ref, z3_ref, zs = rest
        i = pl.program_id(0)
        zs[0, pl.ds(0, hl), :] = jnp.where(i > 0, _glu(abh_ref[...]), 0.0)
        zs[0, pl.ds(hl, tm), :] = _glu(ab_ref[...])
        _fill_shifts(zs)
        for b in range(tm // rb):
            acc = jnp.zeros((rb, D), F32)
            for j in range(CONV_W):
                acc = acc + _shifted(zs, b * rb + off + j, rb) * k_ref[pl.ds(j, 1), :]
            z1 = acc + dwb_ref[...]
            z1_ref[pl.ds(b * rb, rb), :] = z1
            zn, _ = _ln_stats(z1)
            z2 = zn * lng_ref[...] + lnb_ref[...]
            z3_ref[pl.ds(b * rb, rb), :] = (z2 * _sig(z2)).astype(BF16)

    row = pl.BlockSpec((tm, D), lambda i: (i, 0))
    g_specs, g_args, g_ospecs, g_oshapes = [], [], [], []
    if guest is not None:
        kdim = g_a.shape[1]
        g_specs = [pl.BlockSpec((tm, kdim), lambda i: (i, 0))]
        g_specs += [pl.BlockSpec((g_rows, kdim), lambda i, q=q: (q, 0), pipeline_mode=pl.Buffered(1))
                    for q in g_blocks]
        g_args = [g_a] + [g_b] * g_nblk
        g_ospecs = [pl.BlockSpec((tm, g_nblk * g_rows), lambda i: (i, 0))]
        g_oshapes = [jax.ShapeDtypeStruct((T, g_nblk * g_rows), BF16)]
    return pl.pallas_call(
        body, name=name, grid=(T // tm,),
        in_specs=[pl.BlockSpec((tm, 2 * D), lambda i: (i, 0)),
                  pl.BlockSpec((hl, 2 * D), lambda i: (jnp.maximum(i * (tm // hl) - 1, 0), 0)),
                  _const_spec((32, D)), _const_spec((1, D)), _const_spec((1, D)), _const_spec((1, D))] + g_specs,
        out_specs=[row, row] + g_ospecs,
        out_shape=[jax.ShapeDtypeStruct((T, D), F32), jax.ShapeDtypeStruct((T, D), BF16)] + g_oshapes,
        scratch_shapes=[pltpu.VMEM((8, hl + tm, D), F32)],
        compiler_params=_cp(("parallel",)),
    )(ab, ab, kern, dwb, lng, lnb, *g_args)


GUEST_TM = 256


def _conv_bwd(dz3, z1, ab, kern, lng, lnb, name, guest_lhs=(), guest_rhs=None):
    tm, hl, rb = CONV_TM, CONV_HALO, CONV_RB
    off = hl - (CONV_W - 1)
    nsteps = T // tm
    ng = len(guest_lhs)
    gblocks = [a.shape[1] // GUEST_TM for a in guest_lhs]
    assert all(gb <= nsteps and gb * GUEST_TM == a.shape[1] for gb, a in zip(gblocks, guest_lhs))

    def ln_bwd(dz3v, z1v, lngv, lnbv):
        zn, rstd = _ln_stats(z1v)
        z2 = zn * lngv + lnbv
        s = _sig(z2)
        dz2 = dz3v * (s * (1.0 + z2 * (1.0 - s)))
        dzn = dz2 * lngv
        dz1 = rstd * (dzn - jnp.mean(dzn, axis=-1, keepdims=True)
                      - zn * jnp.mean(dzn * zn, axis=-1, keepdims=True))
        return dz1, dz2, zn

    def body(dz3_ref, dz3h_ref, z1_ref, z1h_ref, ab_ref, abh_ref, k_ref, lng_ref, lnb_ref, *rest):
        g_in, rest = rest[:ng + (1 if ng else 0)], rest[ng + (1 if ng else 0):]
        dab_ref, dk_ref, dvec_ref = rest[:3]
        g_out, (zs, dzs) = rest[3:3 + ng], rest[3 + ng:]
        i = pl.program_id(0)
        lngv, lnbv = lng_ref[...], lnb_ref[...]

        for a_ref, o_ref, gb in zip(g_in[:ng], g_out, gblocks):
            @pl.when(i < gb)
            def _(a_ref=a_ref, o_ref=o_ref):
                o_ref[...] = _dot(a_ref[...], g_in[ng][...], TN).astype(BF16)

        @pl.when(i == 0)
        def _():
            dk_ref[...] = jnp.zeros_like(dk_ref)
            dvec_ref[...] = jnp.zeros_like(dvec_ref)

        dz1, dz2, zn = ln_bwd(dz3_ref[...].astype(F32), z1_ref[...], lngv, lnbv)
        dvec_ref[pl.ds(0, 1), :] += jnp.sum(dz1, axis=0, keepdims=True)
        dvec_ref[pl.ds(1, 1), :] += jnp.sum(dz2 * zn, axis=0, keepdims=True)
        dvec_ref[pl.ds(2, 1), :] += jnp.sum(dz2, axis=0, keepdims=True)
        dzs[0, pl.ds(0, tm), :] = dz1
        dz1h, _, _ = ln_bwd(dz3h_ref[...].astype(F32), z1h_ref[...], lngv, lnbv)
        dzs[0, pl.ds(tm, hl), :] = jnp.where(i < nsteps - 1, dz1h, 0.0)
        _fill_shifts(dzs)
        zs[0, pl.ds(0, hl), :] = jnp.where(i > 0, _glu(abh_ref[...]), 0.0)
        zs[0, pl.ds(hl, tm), :] = _glu(ab_ref[...])
        _fill_shifts(zs)

        for j in range(CONV_W):
            tot = jnp.zeros((rb, D), F32)
            for b in range(tm // rb):
                tot = tot + dzs[0, pl.ds(b * rb, rb), :] * _shifted(zs, b * rb + off + j, rb)
            dk_ref[pl.ds(j, 1), :] += jnp.sum(tot, axis=0, keepdims=True)

        for b in range(tm // rb):
            acc = jnp.zeros((rb, D), F32)
            for j in range(CONV_W):
                acc = acc + _shifted(dzs, b * rb + (CONV_W - 1) - j, rb) * k_ref[pl.ds(j, 1), :]
            av = ab_ref[pl.ds(b * rb, rb), pl.ds(0, D)].astype(F32)
            sb = _sig(ab_ref[pl.ds(b * rb, rb), pl.ds(D, D)].astype(F32))
            dab_ref[pl.ds(b * rb, rb), pl.ds(0, D)] = (acc * sb).astype(BF16)
            dab_ref[pl.ds(b * rb, rb), pl.ds(D, D)] = (acc * av * sb * (1.0 - sb)).astype(BF16)

    row = pl.BlockSpec((tm, D), lambda i: (i, 0))
    nxt = pl.BlockSpec((hl, D), lambda i: (jnp.minimum((i + 1) * (tm // hl), T // hl - 1), 0))
    g_specs, g_args, g_ospecs, g_oshapes = [], [], [], []
    for a, gb in zip(guest_lhs, gblocks):
        g_specs.append(pl.BlockSpec((T, GUEST_TM), lambda i, gb=gb: (0, jnp.minimum(i, gb - 1))))
        g_args.append(a)
        g_ospecs.append(pl.BlockSpec((GUEST_TM, guest_rhs.shape[1]), lambda i, gb=gb: (jnp.minimum(i, gb - 1), 0)))
        g_oshapes.append(jax.ShapeDtypeStruct((a.shape[1], guest_rhs.shape[1]), BF16))
    if ng:
        g_specs.append(pl.BlockSpec(guest_rhs.shape, lambda i: (0, 0), pipeline_mode=pl.Buffered(1)))
        g_args.append(guest_rhs)
    return pl.pallas_call(
        body, name=name, grid=(nsteps,),
        in_specs=[row, nxt, row, nxt,
                  pl.BlockSpec((tm, 2 * D), lambda i: (i, 0)),
                  pl.BlockSpec((hl, 2 * D), lambda i: (jnp.maximum(i * (tm // hl) - 1, 0), 0)),
                  _const_spec((32, D)), _const_spec((1, D)), _const_spec((1, D))] + g_specs,
        out_specs=[pl.BlockSpec((tm, 2 * D), lambda i: (i, 0)), _const_spec((32, D)), _const_spec((8, D))]
        + g_ospecs,
        out_shape=[jax.ShapeDtypeStruct((T, 2 * D), BF16), jax.ShapeDtypeStruct((32, D), F32),
                   jax.ShapeDtypeStruct((8, D), F32)] + g_oshapes,
        scratch_shapes=[pltpu.VMEM((8, hl + tm, D), F32), pltpu.VMEM((8, tm + hl, D), F32)],
        compiler_params=_cp(("arbitrary",)),
    )(dz3, dz3, z1, z1, ab, ab, kern, lng, lnb, *g_args)


def _alibi_slopes():
    h = np.arange(1, 3 * NHG + 1, dtype=np.float32)
    return np.power(np.float32(2.0), -8.0 * h / np.float32(3 * NHG)).astype(np.float32)


def _band_bias(gi):
    _, dil = GROUPS[gi]
    slopes = _alibi_slopes()[gi * NHG:(gi + 1) * NHG]
    qi = np.arange(BLK)[:, None]
    ki = np.arange(2 * BLK)[None, :]
    steps = BLK + qi - ki
    band = (steps >= 0) & (steps <= BLK)
    bias = -slopes[:, None, None] * (dil * steps).astype(np.float32)[None]
    return jnp.asarray(np.where(band[None], bias, np.float32(NEG)).astype(np.float32))


QB_FWD = 8
QB_BWD = 32


def _attn_specs(qb, c0):
    prev = lambda n: jnp.maximum(n * qb - 1, 0)
    return [pl.BlockSpec((qb * BLK, HEAD), lambda h, n: (n, c0 + h)),
            pl.BlockSpec((BLK, HEAD), lambda h, n: (prev(n), c0 + NHG + h)),
            pl.BlockSpec((qb * BLK, HEAD), lambda h, n: (n, c0 + NHG + h)),
            pl.BlockSpec((BLK, HEAD), lambda h, n: (prev(n), c0 + 2 * NHG + h)),
            pl.BlockSpec((qb * BLK, HEAD), lambda h, n: (n, c0 + 2 * NHG + h)),
            pl.BlockSpec((None, BLK, 2 * BLK), lambda h, n: (h, 0, 0))]


def _scores(q, kcat, bias, blk, seg):
    s = _dot(q, kcat, NT) * (HEAD ** -0.5) + bias
    col = lax.broadcasted_iota(jnp.int32, s.shape, 1)
    first = (blk % seg) == 0
    return jnp.where(jnp.logical_and(first, col < BLK), NEG, s)


def _attn_fwd(qkv, gi, name, col0=0):
    seg = (T // GROUPS[gi][1]) // BLK

    qb = min(QB_FWD, T // BLK)

    def body(q_ref, kp_ref, kc_ref, vp_ref, vc_ref, bias_ref, o_ref, l_ref):
        n = pl.program_id(0)
        for h in range(NHG):
            cols = pl.ds(h * HEAD, HEAD)
            kwin = jnp.concatenate([kp_ref[:, cols], kc_ref[:, cols]], axis=0)
            vwin = jnp.concatenate([vp_ref[:, cols], vc_ref[:, cols]], axis=0)
            bias = bias_ref[h]
            for b in range(qb):
                rows = pl.ds(b * BLK, BLK)
                s = _scores(q_ref[rows, cols], kwin[b * BLK:(b + 2) * BLK], bias, n * qb + b, seg)
                mx = jnp.max(s, axis=-1, keepdims=True)
                p = jnp.exp(s - mx)
                den = jnp.sum(p, axis=-1, keepdims=True)
                o_ref[rows, cols] = (_dot(p.astype(BF16), vwin[b * BLK:(b + 2) * BLK], NN) / den).astype(BF16)
                l_ref[rows, cols] = jnp.broadcast_to(mx + jnp.log(den), (BLK, HEAD))

    prev = lambda n: jnp.maximum(n * qb - 1, 0)
    c0 = col0 // AW
    cur = lambda part: pl.BlockSpec((qb * BLK, AW), lambda n: (n, part))
    halo = lambda part: pl.BlockSpec((BLK, AW), lambda n: (prev(n), part))
    return pl.pallas_call(
        body, name=name, grid=(T // (qb * BLK),),
        in_specs=[cur(c0), halo(c0 + 1), cur(c0 + 1), halo(c0 + 2), cur(c0 + 2),
                  _const_spec((NHG, BLK, 2 * BLK))],
        out_specs=[cur(0), cur(0)],
        out_shape=[jax.ShapeDtypeStruct((T, AW), BF16), jax.ShapeDtypeStruct((T, AW), F32)],
        compiler_params=_cp(("parallel",)),
    )(qkv, qkv, qkv, qkv, qkv, _band_bias(gi))


def _attn_bwd(qkv, dob, lse, delta, gi, name, col0=0):
    seg = (T // GROUPS[gi][1]) // BLK
    qb = min(QB_BWD, T // BLK)
    nb = T // (qb * BLK)
    scale = HEAD ** -0.5

    def body(q_ref, kp_ref, kc_ref, vp_ref, vc_ref, bias_ref, do_ref, l_ref, dl_ref, out_ref, dk_acc, dv_acc):
        n = pl.program_id(1)
        kwin = jnp.concatenate([kp_ref[...], kc_ref[...]], axis=0)
        vwin = jnp.concatenate([vp_ref[...], vc_ref[...]], axis=0)
        bias = bias_ref[...]
        dks, dvs = [], []
        for b in range(qb):
            rows = pl.ds(b * BLK, BLK)
            q = q_ref[rows, :]
            kcat = kwin[b * BLK:(b + 2) * BLK]
            s = _scores(q, kcat, bias, n * qb + b, seg)
            p = jnp.exp(s - l_ref[rows, pl.ds(0, 1)])
            dov = do_ref[rows, :]
            dvs.append(_dot(p.astype(BF16), dov, TN))
            dp = _dot(dov, vwin[b * BLK:(b + 2) * BLK], NT)
            dsb = (p * (dp - dl_ref[rows, pl.ds(0, 1)]) * scale).astype(BF16)
            row = pl.ds(pl.multiple_of((n * qb + b) * BLK, BLK), BLK)
            out_ref[0, row, :] = _dot(dsb, kcat, NN).astype(BF16)
            dks.append(_dot(dsb, q, TN))
        for b in range(qb):
            row = pl.ds(pl.multiple_of((n * qb + b) * BLK, BLK), BLK)
            if b + 1 < qb:
                dk_acc[row, :] = dks[b][BLK:] + dks[b + 1][:BLK]
                dv_acc[row, :] = dvs[b][BLK:] + dvs[b + 1][:BLK]
            else:
                dk_acc[row, :] = dks[b][BLK:]
                dv_acc[row, :] = dvs[b][BLK:]

        @pl.when(n > 0)
        def _():
            prow = pl.ds(pl.multiple_of((n * qb - 1) * BLK, BLK), BLK)
            dk_acc[prow, :] += dks[0][:BLK]
            dv_acc[prow, :] += dvs[0][:BLK]

        @pl.when(n == nb - 1)
        def _():
            out_ref[1] = dk_acc[...].astype(BF16)
            out_ref[2] = dv_acc[...].astype(BF16)

    oblk = pl.BlockSpec((qb * BLK, HEAD), lambda h, n: (n, h))
    return pl.pallas_call(
        body, name=name, grid=(NHG, nb),
        in_specs=_attn_specs(qb, col0 // HEAD) + [oblk, oblk, oblk],
        out_specs=pl.BlockSpec((3, T, HEAD), lambda h, n: (0, 0, h)),
        out_shape=jax.ShapeDtypeStruct((3, T, AW), BF16),
        scratch_shapes=[pltpu.VMEM((T, HEAD), F32), pltpu.VMEM((T, HEAD), F32)],
        compiler_params=_cp(("parallel", "arbitrary")),
    )(qkv, qkv, qkv, qkv, qkv, _band_bias(gi), dob, lse, delta)


def _merge(outs, lses, name):
    tm = PERM_TM
    dils = [d for _, d in GROUPS]
    ng = len(dils)

    def body(*refs):
        in_refs = refs[:2 * ng]
        ab_ref = refs[2 * ng]
        lse_refs = refs[2 * ng + 1:3 * ng + 1]
        tile = refs[-1]

        def token_order(ref, dil):
            if dil == 1:
                return ref[...].astype(F32)
            _load_unperm(ref, tile, dil)
            return _get_tile(tile)

        os = [token_order(in_refs[2 * i], d) for i, d in enumerate(dils)]
        ls = [token_order(in_refs[2 * i + 1], d) for i, d in enumerate(dils)]
        mx = jnp.maximum(jnp.maximum(ls[0], ls[1]), ls[2])
        es = [jnp.exp(v - mx) for v in ls]
        tot = es[0] + es[1] + es[2]
        att = (es[0] / tot) * os[0] + (es[1] / tot) * os[1] + (es[2] / tot) * os[2]
        ab_ref[...] = att.astype(BF16)
        lse = mx + jnp.log(tot)
        _put_tile(tile, lse)
        for dil, ref in zip(dils, lse_refs):
            if dil == 1:
                ref[...] = lse
            else:
                _store_perm(ref, tile, dil)

    row = pl.BlockSpec((tm, AW), lambda i: (i, 0))
    specs = [row if d == 1 else _perm_spec(d, AW) for d in dils]
    args = []
    for d, o, l in zip(dils, outs, lses):
        args += [o, l] if d == 1 else [o.reshape(d, T // d, AW), l.reshape(d, T // d, AW)]
    out = pl.pallas_call(
        body, name=name, grid=(T // tm,),
        in_specs=[sp for sp in specs for _ in range(2)], out_specs=[row] + specs,
        out_shape=[jax.ShapeDtypeStruct((T, AW), BF16)]
        + [jax.ShapeDtypeStruct((T, AW), F32) if d == 1 else _perm_shape(d, AW, F32) for d in dils],
        scratch_shapes=[_tile_scratch(AW)],
        compiler_params=_cp(("parallel",)),
    )(*args)
    return out[0], [o.reshape(T, AW) for o in out[1:]]


GATE_BLOCK0 = (IN_W - 2 * D) // (D // 2)


def _mix_out(z3b, attnb, gates, wc, wa_t, wo, x1, name):
    tm = 512

    def body(z_ref, a_ref, g_ref, wc_ref, wa_ref, wo_ref, x_ref, xo_ref, yc_ref, ya_ref, mx_ref):
        yc = _dot(z_ref[...], wc_ref[...], NN)
        ya = _dot(a_ref[...], wa_ref[...], NT)
        yc_ref[...] = yc.astype(BF16)
        ya_ref[...] = ya.astype(BF16)
        gv = g_ref[...].astype(F32)
        mixed = (_sig(gv[:, :D]) * yc + _sig(gv[:, D:]) * ya).astype(BF16)
        mx_ref[...] = mixed
        xo_ref[...] = x_ref[...] + _dot(mixed, wo_ref[...], NN)

    row = pl.BlockSpec((tm, D), lambda i: (i, 0))
    return pl.pallas_call(
        body, name=name, grid=(T // tm,),
        in_specs=[row, pl.BlockSpec((tm, AW), lambda i: (i, 0)), pl.BlockSpec((tm, 2 * D), lambda i: (i, 0)),
                  _const_spec((D, D)), _const_spec((D, AW)), _const_spec((D, D)), row],
        out_specs=[row, row, row, row],
        out_shape=[jax.ShapeDtypeStruct((T, D), F32), jax.ShapeDtypeStruct((T, D), BF16),
                   jax.ShapeDtypeStruct((T, D), BF16), jax.ShapeDtypeStruct((T, D), BF16)],
        compiler_params=_cp(("parallel",)),
    )(z3b, attnb, gates, wc, wa_t, wo, x1)


def _mix_out_bwd(dx2, gates, yc, ya, attn, wc, wa_t, wo, win_t, name):
    tm = PERM_TM
    dils = [d for _, d in GROUPS]
    ng = len(dils)

    def body(dx_ref, g_ref, yc_ref, ya_ref, at_ref, wc_ref, wa_ref, wo_ref, wg0_ref, wg1_ref, wg2_ref, wg3_ref,
             dg_ref, dyc_ref, dya_ref, dxb_ref, dz3_ref, dhg_ref, *rest):
        dat_refs, dl_refs, tile = rest[:ng], rest[ng:2 * ng], rest[-1]
        dxb = dx_ref[...].astype(BF16)
        dxb_ref[...] = dxb
        dmix = _dot(dxb, wo_ref[...], NT)
        gv = g_ref[...].astype(F32)
        sc = _sig(gv[:, :D])
        sa = _sig(gv[:, D:])
        ycv, yav = yc_ref[...].astype(F32), ya_ref[...].astype(F32)
        dgc = (dmix * ycv * sc * (1.0 - sc)).astype(BF16)
        dga = (dmix * yav * sa * (1.0 - sa)).astype(BF16)
        dg_ref[:, pl.ds(0, D)] = dgc
        dg_ref[:, pl.ds(D, D)] = dga
        half = D // 2
        dhg_ref[...] = (_dot(dgc[:, :half], wg0_ref[...], NN) + _dot(dgc[:, half:], wg1_ref[...], NN)
                        + _dot(dga[:, :half], wg2_ref[...], NN) + _dot(dga[:, half:], wg3_ref[...], NN))
        dyc = (dmix * sc).astype(BF16)
        dya = (dmix * sa).astype(BF16)
        dyc_ref[...] = dyc
        dya_ref[...] = dya
        dz3_ref[...] = _dot(dyc, wc_ref[...], NT).astype(BF16)
        dat = _dot(dya, wa_ref[...], NN)
        prod = dat * at_ref[...].astype(F32)
        delta = jnp.concatenate(
            [jnp.broadcast_to(jnp.sum(prod[:, h * HEAD:(h + 1) * HEAD], axis=-1, keepdims=True), (tm, HEAD))
             for h in range(NHG)], axis=1)
        for value, out_refs in ((dat, dat_refs), (delta, dl_refs)):
            _put_tile(tile, value)
            for dil, ref in zip(dils, out_refs):
                if dil == 1:
                    ref[...] = value.astype(ref.dtype)
                else:
                    _store_perm(ref, tile, dil)

    row = pl.BlockSpec((tm, D), lambda i: (i, 0))
    row2 = pl.BlockSpec((tm, 2 * D), lambda i: (i, 0))
    rowa = pl.BlockSpec((tm, AW), lambda i: (i, 0))
    aspecs = [rowa if d == 1 else _perm_spec(d, AW) for d in dils]

    def ashapes(dtype):
        return [jax.ShapeDtypeStruct((T, AW), dtype) if d == 1 else _perm_shape(d, AW, dtype) for d in dils]

    out = pl.pallas_call(
        body, name=name, grid=(T // tm,),
        in_specs=[row, row2, row, row, rowa, _const_spec((D, D)), _const_spec((D, AW)), _const_spec((D, D))]
        + [pl.BlockSpec((D // 2, D), lambda i, q=q: (GATE_BLOCK0 + q, 0), pipeline_mode=pl.Buffered(1))
           for q in range(4)],
        out_specs=[row2, row, row, row, row, row] + aspecs + aspecs,
        out_shape=[jax.ShapeDtypeStruct((T, 2 * D), BF16), jax.ShapeDtypeStruct((T, D), BF16),
                   jax.ShapeDtypeStruct((T, D), BF16), jax.ShapeDtypeStruct((T, D), BF16),
                   jax.ShapeDtypeStruct((T, D), BF16), jax.ShapeDtypeStruct((T, D), F32)]
        + ashapes(BF16) + ashapes(F32),
        scratch_shapes=[_tile_scratch(AW)],
        compiler_params=_cp(("parallel",)),
    )(dx2, gates, yc, ya, attn, wc, wa_t, wo, win_t, win_t, win_t, win_t)
    dats = [o.reshape(T, AW) for o in out[6:6 + ng]]
    deltas = [o.reshape(T, AW) for o in out[6 + ng:6 + 2 * ng]]
    return out[0], out[1], out[2], out[3], out[4], out[5], dats, deltas


def _peer(k):
    x, y, c = lax.axis_index("x"), lax.axis_index("y"), lax.axis_index("c")
    px = 1 - x if k & 4 else x
    py = 1 - y if k & 2 else y
    pc = 1 - c if k & 1 else c
    return (px, py, pc), 4 * px + 2 * py + pc


HBM_SPEC = pl.BlockSpec(memory_space=pltpu.HBM)
SEM_SPEC = pl.BlockSpec(memory_space=pltpu.SEMAPHORE)
EFFECT = pltpu.SideEffectType.DATAFLOW_SIDE_EFFECTING


def _my_place():
    return 4 * lax.axis_index("x") + 2 * lax.axis_index("y") + lax.axis_index("c")


def _tie(a, order_after, name):
    na = len(order_after)

    def body(*refs):
        del refs

    return pl.pallas_call(
        body, name=name, in_specs=[pl.BlockSpec(memory_space=pl.ANY)] * (1 + na),
        out_specs=pl.BlockSpec(memory_space=pl.ANY), out_shape=jax.ShapeDtypeStruct(a.shape, a.dtype),
        input_output_aliases={0: 0},
    )(a, *order_after)


def _prep_gather(ws, order_after, name):
    me = jnp.reshape(_my_place(), (1,)).astype(jnp.int32)
    n = len(ws)
    na = len(order_after)
    shapes = [((32, wv.shape[1]), F32) if wv.shape[0] == CONV_W else (wv.shape, BF16) for wv in ws]

    def body(me_ref, *refs):
        del me_ref
        ins, outs = refs[:n], refs[n + na:]
        for wv, i_ref, o_ref in zip(ws, ins, outs):
            if wv.shape[0] == CONV_W:
                o_ref[pl.ds(0, CONV_W), :] = i_ref[...]
                o_ref[pl.ds(CONV_W, 1), :] = jnp.zeros((1, wv.shape[1]), F32)
            else:
                o_ref[...] = i_ref[...].astype(BF16)

    grid_spec = pltpu.PrefetchScalarGridSpec(
        num_scalar_prefetch=1, grid=(1,),
        in_specs=[pl.BlockSpec(wv.shape, lambda i, m: (0, 0)) for wv in ws]
        + [pl.BlockSpec(memory_space=pl.ANY)] * na,
        out_specs=[pl.BlockSpec(shp, lambda i, m: (m[0], 0)) for shp, _ in shapes])
    return pl.pallas_call(
        body, name=name, grid_spec=grid_spec,
        out_shape=[jax.ShapeDtypeStruct((NDEV * shp[0], shp[1]), dt) for shp, dt in shapes],
        compiler_params=_cp(("arbitrary",)),
    )(me, *ws, *order_after)


GATHER_A = ((1, 0), (2, 0), (4, 0), (6, 0))
GATHER_B = ((1, 2), (1, 4), (1, 6))
GATHER_DIRECT = tuple((k, 0) for k in range(1, NDEV))


def _gather_start(lands, plan, order_after, name):
    n = len(lands)
    na = len(order_after)
    npl = len(plan)

    def body(*refs):
        land_refs = refs[:n]
        send, recv = refs[n + na], refs[n + na + 1]
        token = refs[-1]
        for w in range(n):
            rows = lands[w].shape[0] // NDEV
            for p, (k, j) in enumerate(plan):
                peer, _ = _peer(k)
                _, blk = _peer(j)
                part = land_refs[w].at[pl.ds(blk * rows, rows)]
                i = w * npl + p
                pltpu.make_async_remote_copy(src_ref=part, dst_ref=part, send_sem=send.at[i], recv_sem=recv.at[i],
                                             device_id=peer, device_id_type=MESH_ID).start()
        token[...] = jnp.zeros_like(token)

    nsem = n * npl
    bufs = [pltpu.with_memory_space_constraint(a, pltpu.HBM) for a in lands]
    out = pl.pallas_call(
        body, name=name,
        in_specs=[HBM_SPEC] * n + [pl.BlockSpec(memory_space=pl.ANY)] * na,
        out_specs=[SEM_SPEC, SEM_SPEC] + [HBM_SPEC] * n + [pl.BlockSpec(memory_space=pltpu.VMEM)],
        out_shape=[pltpu.SemaphoreType.DMA((nsem,)), pltpu.SemaphoreType.DMA((nsem,))]
        + [pltpu.HBM(a.shape, a.dtype) for a in bufs] + [jax.ShapeDtypeStruct((8, 128), F32)],
        input_output_aliases={i: 2 + i for i in range(n)},
        compiler_params=pltpu.CompilerParams(has_side_effects=EFFECT),
    )(*bufs, *order_after)
    return out[0], out[1], out[2:2 + n], out[-1]


def _gather_wait(started, plan, order_after, name):
    send, recv, lands, _ = started
    n = len(lands)
    na = len(order_after)
    npl = len(plan)

    def body(*refs):
        land_refs = refs[:n]
        send_ref, recv_ref = refs[n], refs[n + 1]
        for w in range(n):
            rows = lands[w].shape[0] // NDEV
            for p, (k, j) in enumerate(plan):
                peer, _ = _peer(k)
                _, blk = _peer(j)
                part = land_refs[w].at[pl.ds(blk * rows, rows)]
                i = w * npl + p
                cp = pltpu.make_async_remote_copy(src_ref=part, dst_ref=part, send_sem=send_ref.at[i],
                                                  recv_sem=recv_ref.at[i], device_id=peer, device_id_type=MESH_ID)
                cp.wait_send()
                cp.wait_recv()

    out = pl.pallas_call(
        body, name=name,
        in_specs=[HBM_SPEC] * n + [SEM_SPEC, SEM_SPEC] + [pl.BlockSpec(memory_space=pl.ANY)] * na,
        out_specs=[HBM_SPEC] * n,
        out_shape=[pltpu.HBM(a.shape, a.dtype) for a in lands],
        input_output_aliases={i: i for i in range(n)},
        compiler_params=pltpu.CompilerParams(has_side_effects=EFFECT),
    )(*lands, send, recv, *order_after)
    return list(out)


def _copy_ends(kind, src, land, me, plin, k):
    if kind == "scatter":
        rows = src.shape[0] // NDEV
        return src.at[pl.ds(plin * rows, rows)], land.at[k - 1]
    return src, land.at[me]


def _landing(kind, src):
    me = _my_place()
    if kind == "scatter":
        return lax.empty((NDEV - 1, src.shape[0] // NDEV) + src.shape[1:], src.dtype)
    land = lax.empty((NDEV,) + src.shape, src.dtype)
    return lax.dynamic_update_slice(land, src[None], (me,) + (0,) * src.ndim)


def _send_start(kinds, srcs, order_after, name):
    n = len(srcs)
    lands = [_landing(kd, s) for kd, s in zip(kinds, srcs)]
    na = len(order_after)

    def body(*refs):
        src_refs, land_refs = refs[:n], refs[n:2 * n]
        send, recv = refs[2 * n + na], refs[2 * n + na + 1]
        token = refs[-1]
        _, me = _peer(0)
        for w in range(n):
            for k in range(1, NDEV):
                peer, plin = _peer(k)
                s, d = _copy_ends(kinds[w], src_refs[w], land_refs[w], me, plin, k)
                i = w * (NDEV - 1) + k - 1
                pltpu.make_async_remote_copy(src_ref=s, dst_ref=d, send_sem=send.at[i], recv_sem=recv.at[i],
                                             device_id=peer, device_id_type=MESH_ID).start()
        token[...] = jnp.zeros_like(token)

    nsem = n * (NDEV - 1)
    bufs = [pltpu.with_memory_space_constraint(a, pltpu.HBM) for a in list(srcs) + lands]
    out = pl.pallas_call(
        body, name=name,
        in_specs=[HBM_SPEC] * (2 * n) + [pl.BlockSpec(memory_space=pl.ANY)] * na,
        out_specs=[SEM_SPEC, SEM_SPEC] + [HBM_SPEC] * (2 * n) + [pl.BlockSpec(memory_space=pltpu.VMEM)],
        out_shape=[pltpu.SemaphoreType.DMA((nsem,)), pltpu.SemaphoreType.DMA((nsem,))]
        + [pltpu.HBM(a.shape, a.dtype) for a in bufs] + [jax.ShapeDtypeStruct((8, 128), F32)],
        input_output_aliases={i: 2 + i for i in range(2 * n)},
        compiler_params=pltpu.CompilerParams(has_side_effects=EFFECT),
    )(*bufs, *order_after)
    return out[0], out[1], out[2:2 + n], out[2 + n:2 + 2 * n], out[-1]


def _send_wait(kinds, started, order_after, name):
    send, recv, srcs, lands, _ = started
    n = len(srcs)
    na = len(order_after)

    def body(*refs):
        src_refs, land_refs = refs[:n], refs[n:2 * n]
        send_ref, recv_ref = refs[2 * n], refs[2 * n + 1]
        _, me = _peer(0)
        for w in range(n):
            for k in range(1, NDEV):
                peer, plin = _peer(k)
                s, d = _copy_ends(kinds[w], src_refs[w], land_refs[w], me, plin, k)
                i = w * (NDEV - 1) + k - 1
                cp = pltpu.make_async_remote_copy(src_ref=s, dst_ref=d, send_sem=send_ref.at[i],
                                                  recv_sem=recv_ref.at[i], device_id=peer, device_id_type=MESH_ID)
                cp.wait_send()
                cp.wait_recv()

    bufs = list(srcs) + list(lands)
    out = pl.pallas_call(
        body, name=name,
        in_specs=[HBM_SPEC] * (2 * n) + [SEM_SPEC, SEM_SPEC] + [pl.BlockSpec(memory_space=pl.ANY)] * na,
        out_specs=[HBM_SPEC] * (2 * n),
        out_shape=[pltpu.HBM(a.shape, a.dtype) for a in bufs],
        input_output_aliases={i: i for i in range(2 * n)},
        compiler_params=pltpu.CompilerParams(has_side_effects=EFFECT),
    )(*bufs, send, recv, *order_after)
    return out[:n], out[n:]


def _gsum(own, land, name):
    rows, cols = own.shape
    tr = rows // 2 if rows * cols > 512 * 1024 and rows % 32 == 0 else rows

    def body(own_ref, l_ref, o_ref):
        tot = own_ref[...].astype(F32)
        for s in range(NDEV - 1):
            tot = tot + l_ref[s].astype(F32)
        o_ref[...] = tot

    return pl.pallas_call(
        body, name=name, grid=(rows // tr,),
        in_specs=[pl.BlockSpec((tr, cols), lambda i: (i, 0)),
                  pl.BlockSpec((NDEV - 1, tr, cols), lambda i: (0, i, 0))],
        out_specs=pl.BlockSpec((tr, cols), lambda i: (i, 0)),
        out_shape=jax.ShapeDtypeStruct((rows, cols), F32),
        compiler_params=_cp(("parallel",)),
    )(own, land)


def _adamw_math(w, g, m, v):
    m2 = B1 * m + (1.0 - B1) * g
    v2 = B2 * v + (1.0 - B2) * (g * g)
    m_hat = m2 / (1.0 - B1 ** STEP)
    v_hat = v2 / (1.0 - B2 ** STEP)
    delta = -LR * (m_hat / (jnp.sqrt(v_hat) + AEPS) + WD * w)
    return delta, m2, v2


def _adamw(w, g, m, v, name):
    rows, cols = w.shape
    tr = 256 if rows % 256 == 0 and rows > 256 else rows

    def body(w_ref, g_ref, m_ref, v_ref, d_ref, mo_ref, vo_ref):
        d, m2, v2 = _adamw_math(w_ref[...], g_ref[...], m_ref[...], v_ref[...])
        d_ref[...] = d
        mo_ref[...] = m2
        vo_ref[...] = v2

    blk = pl.BlockSpec((tr, cols), lambda i: (i, 0))
    return pl.pallas_call(
        body, name=name, grid=(rows // tr,), in_specs=[blk] * 4, out_specs=[blk] * 3,
        out_shape=[jax.ShapeDtypeStruct((rows, cols), F32)] * 3,
        compiler_params=_cp(("parallel",)),
    )(w, g, m, v)


UPD_TC = 256


def _update(src, land, w, m, v, name):
    rows, cols = land.shape[1:]
    tc = min(UPD_TC if rows > 512 else 2 * UPD_TC, cols)
    me = jnp.reshape(_my_place(), (1,)).astype(jnp.int32)

    def body(me_ref, own_ref, l_ref, w_ref, m_ref, v_ref, g_ref, d_ref, mo_ref, vo_ref):
        del me_ref
        g = own_ref[...].astype(F32)
        for s in range(NDEV - 1):
            g = g + l_ref[s].astype(F32)
        g_ref[...] = g
        d, m2, v2 = _adamw_math(w_ref[...], g, m_ref[...], v_ref[...])
        d_ref[...] = d
        mo_ref[...] = m2
        vo_ref[...] = v2

    wblk = pl.BlockSpec((rows, tc), lambda j, p: (0, j))
    grid_spec = pltpu.PrefetchScalarGridSpec(
        num_scalar_prefetch=1, grid=(cols // tc,),
        in_specs=[pl.BlockSpec((rows, tc), lambda j, p: (p[0], j)),
                  pl.BlockSpec((NDEV - 1, rows, tc), lambda j, p: (0, 0, j)), wblk, wblk, wblk],
        out_specs=[wblk] * 4)
    return pl.pallas_call(
        body, name=name, grid_spec=grid_spec, out_shape=[jax.ShapeDtypeStruct((rows, cols), F32)] * 4,
        compiler_params=_cp(("parallel",)),
    )(me, src, land, w, m, v)


def _small_update(vland, w8, m8, v8, name):
    def body(l_ref, w_ref, m_ref, v_ref, g_ref, d_ref, mo_ref, vo_ref):
        g = l_ref[0]
        for s in range(1, NDEV):
            g = g + l_ref[s]
        g_ref[...] = g
        d, m2, v2 = _adamw_math(w_ref[...], g, m_ref[...], v_ref[...])
        d_ref[...] = d
        mo_ref[...] = m2
        vo_ref[...] = v2

    return pl.pallas_call(
        body, name=name, out_shape=[jax.ShapeDtypeStruct((8, D), F32)] * 4,
        compiler_params=_cp(None),
    )(vland, w8, m8, v8)


def kernel(x, ffn1_norm, ffn1_w_gate, ffn1_w_up, ffn1_w_down, mix_norm, w_in, conv_dw_kernel, conv_dw_bias, conv_ln_gain, conv_ln_bias, conv_w_out, attn_w_out, w_o, ffn2_norm, ffn2_w_gate, ffn2_w_up, ffn2_w_down, final_norm, loss_target, m_ffn1_norm, m_ffn1_w_gate, m_ffn1_w_up, m_ffn1_w_down, m_mix_norm, m_w_in, m_conv_dw_kernel, m_conv_dw_bias, m_conv_ln_gain, m_conv_ln_bias, m_conv_w_out, m_attn_w_out, m_w_o, m_ffn2_norm, m_ffn2_w_gate, m_ffn2_w_up, m_ffn2_w_down, m_final_norm, v_ffn1_norm, v_ffn1_w_gate, v_ffn1_w_up, v_ffn1_w_down, v_mix_norm, v_w_in, v_conv_dw_kernel, v_conv_dw_bias, v_conv_ln_gain, v_conv_ln_bias, v_conv_w_out, v_attn_w_out, v_w_o, v_ffn2_norm, v_ffn2_w_gate, v_ffn2_w_up, v_ffn2_w_down, v_final_norm):
    names = ["ffn1_norm", "ffn1_w_gate", "ffn1_w_up", "ffn1_w_down", "mix_norm", "w_in", "conv_dw_kernel",
             "conv_dw_bias", "conv_ln_gain", "conv_ln_bias", "conv_w_out", "attn_w_out", "w_o", "ffn2_norm",
             "ffn2_w_gate", "ffn2_w_up", "ffn2_w_down", "final_norm"]
    w = dict(ffn1_norm=ffn1_norm, ffn1_w_gate=ffn1_w_gate, ffn1_w_up=ffn1_w_up, ffn1_w_down=ffn1_w_down, mix_norm=mix_norm, w_in=w_in, conv_dw_kernel=conv_dw_kernel, conv_dw_bias=conv_dw_bias, conv_ln_gain=conv_ln_gain, conv_ln_bias=conv_ln_bias, conv_w_out=conv_w_out, attn_w_out=attn_w_out, w_o=w_o, ffn2_norm=ffn2_norm, ffn2_w_gate=ffn2_w_gate, ffn2_w_up=ffn2_w_up, ffn2_w_down=ffn2_w_down, final_norm=final_norm)
    mo = dict(ffn1_norm=m_ffn1_norm, ffn1_w_gate=m_ffn1_w_gate, ffn1_w_up=m_ffn1_w_up, ffn1_w_down=m_ffn1_w_down, mix_norm=m_mix_norm, w_in=m_w_in, conv_dw_kernel=m_conv_dw_kernel, conv_dw_bias=m_conv_dw_bias, conv_ln_gain=m_conv_ln_gain, conv_ln_bias=m_conv_ln_bias, conv_w_out=m_conv_w_out, attn_w_out=m_attn_w_out, w_o=m_w_o, ffn2_norm=m_ffn2_norm, ffn2_w_gate=m_ffn2_w_gate, ffn2_w_up=m_ffn2_w_up, ffn2_w_down=m_ffn2_w_down, final_norm=m_final_norm)
    vo = dict(ffn1_norm=v_ffn1_norm, ffn1_w_gate=v_ffn1_w_gate, ffn1_w_up=v_ffn1_w_up, ffn1_w_down=v_ffn1_w_down, mix_norm=v_mix_norm, w_in=v_w_in, conv_dw_kernel=v_conv_dw_kernel, conv_dw_bias=v_conv_dw_bias, conv_ln_gain=v_conv_ln_gain, conv_ln_bias=v_conv_ln_bias, conv_w_out=v_conv_w_out, attn_w_out=v_attn_w_out, w_o=v_w_o, ffn2_norm=v_ffn2_norm, ffn2_w_gate=v_ffn2_w_gate, ffn2_w_up=v_ffn2_w_up, ffn2_w_down=v_ffn2_w_down, final_norm=v_final_norm)
    col_sharded = ("ffn1_w_gate", "ffn1_w_up", "w_in", "attn_w_out", "ffn2_w_gate", "ffn2_w_up")
    row_sharded = ("ffn1_w_down", "conv_w_out", "w_o", "ffn2_w_down")
    small = ("ffn1_norm", "mix_norm", "ffn2_norm", "final_norm", "conv_dw_bias", "conv_ln_gain", "conv_ln_bias")

    def landing_view(a, n):
        return jnp.transpose(a[0]) if n in col_sharded else a[0]

    def own_view(a, n):
        return jnp.transpose(a)[None] if n in col_sharded else a[None]

    ag_groups = (("ffn1_w_gate", "ffn1_w_up", "ffn1_w_down"),
                 ("w_in", "attn_w_out", "conv_w_out", "w_o", "conv_dw_kernel"),
                 ("ffn2_w_gate", "ffn2_w_up", "ffn2_w_down"))
    ag, order = [], []
    for gi, grp in enumerate(ag_groups):
        lands = _prep_gather([landing_view(w[n], n) for n in grp], order, f"gather_prep{gi}")
        st = _gather_start(lands, GATHER_DIRECT if gi == 2 else GATHER_A, [], f"gather_a_start{gi}")
        ag.append(st)
        order = [st[3]]

    def chips_in(gi, after):
        lands = _gather_wait(ag[gi], GATHER_A, after, f"gather_a_wait{gi}")
        return _gather_start(lands, GATHER_B, [], f"gather_b_start{gi}")

    def all_in(gi, st, after):
        return _gather_wait(st, GATHER_B, after, f"gather_b_wait{gi}")

    x0 = x[0]
    tgt = loss_target[0]
    gf = final_norm.reshape(1, D)

    wg1, wu1, wd1 = all_in(0, chips_in(0, [ag[2][3]]), [])
    x1, gg1, uu1, h2p = _ffn_fwd(x0, ffn1_norm, wg1, wu1, wd1, "ffn1_fwd", next_gain=mix_norm)
    h2 = h2p[0]
    win_t, wa_t, wc, wo, kern_blocks = all_in(1, chips_in(1, [x1]), [])
    kern = kern_blocks.reshape(NDEV, 32, D // NDEV).transpose(1, 0, 2).reshape(32, D)
    ptm = min(T, 2048)
    ab = _mm(h2, win_t, mode="nt", m=T, n=2 * D, k=D, tm=ptm, tn=512, tk=D, out_dtype=BF16, name="proj_conv")
    z1, z3b, gates = _conv_fwd(ab, kern, conv_dw_bias, conv_ln_gain, conv_ln_bias, "conv_fwd",
                               guest=(h2, win_t, (13, 14, 15, 16, 4, 7, 10), 512))
    qkv, qkv_col0 = [gates], [2 * D]
    for gi in range(1, len(GROUPS)):
        qkv.append(_mm(h2p[gi], win_t, mode="nt", m=T, n=3 * AW, k=D, tm=ptm, tn=AW, tk=D, out_dtype=BF16,
                       b_map=lambda i, j, kk, gi=gi: (4 + gi + 3 * j, 0), name=f"proj_qkv{gi}"))
        qkv_col0.append(0)
    outs, lses = [], []
    for gi, (_, dil) in enumerate(GROUPS):
        o, l = _attn_fwd(qkv[gi], gi, f"attn_fwd{gi}", col0=qkv_col0[gi])
        outs.append(o)
        lses.append(l)
    attnb, lse = _merge(outs, lses, "attn_merge")
    x2, yc, ya, mixedb = _mix_out(z3b, attnb, gates, wc, wa_t, wo, x1, "mix_out_fwd")
    wg2, wu2, wd2 = _gather_wait(ag[2], GATHER_DIRECT, [x2], "gather_a_wait2")
    gg2, uu2, dx3, dgf, loss_part = _ffn_fwd(x2, ffn2_norm, wg2, wu2, wd2, "ffn2_fwd", loss_of=(gf, tgt))

    dx2, dg3, dgb, dub, actb, hb, dob = _ffn_bwd(x2, ffn2_norm, gg2, uu2, dx3, wg2, wu2, wd2, "ffn2_bwd")
    grads = {}
    grads["ffn2_w_down"] = _wgrad(actb, dob, FF, D, "ffn2_dwd")
    rs_groups = [("ffn2_w_gate", "ffn2_w_up", "ffn2_w_down"),
                 ("attn_w_out", "conv_w_out", "w_o", "conv_dw_kernel"),
                 ("w_in",),
                 ("ffn1_w_gate",), ("ffn1_w_up",), ("ffn1_w_down",), ()]
    last = len(rs_groups) - 1
    rs = []

    dgates, dycb, dyab, dx2b, dz3, dh_gates, dattnb, delta = _mix_out_bwd(dx2, gates, yc, ya, attnb, wc, wa_t, wo,
                                                                          win_t, "mix_out_bwd")
    grads["w_o"] = _wgrad(mixedb, dx2b, D, D, "dw_o")
    grads["conv_w_out"] = _wgrad(z3b, dycb, D, D, "dw_conv_out")
    grads["attn_w_out"] = _wgrad(dyab, attnb, D, AW, "dw_attn_out")
    dab, dkern, dvec, grads["ffn2_w_gate"], grads["ffn2_w_up"] = _conv_bwd(
        dz3, z1, ab, kern, conv_ln_gain, conv_ln_bias, "conv_bwd", guest_lhs=(dgb, dub), guest_rhs=hb)
    grads["conv_dw_kernel"] = dkern.reshape(32, NDEV, D // NDEV).transpose(1, 0, 2).reshape(NDEV * 32, D // NDEV)
    rs.append(_send_start(["scatter"] * 3, [grads[n] for n in rs_groups[0]], [], "scatter_start0"))
    rs.append(_send_start(["scatter"] * 4, [grads[n] for n in rs_groups[1]], [rs[0][4]], "scatter_start1"))
    dattnb = [_tie(a, [rs[1][4]], f"tie_after_scatter1_{i}") for i, a in enumerate(dattnb)]

    dqkv, dq3s = [], []
    for gi, (_, dil) in enumerate(GROUPS):
        dq3 = _attn_bwd(qkv[gi], dattnb[gi], lse[gi], delta[gi], gi, f"attn_bwd{gi}", col0=qkv_col0[gi])
        dq3s.append(dq3)
        dqkv.append(dq3.reshape(3 * T, AW))

    wtk = min(T, 2048)
    dwin = _mm(dab, h2, mode="tn", m=2 * D, n=D, k=T, tm=2 * D, tn=D, tk=wtk, out_dtype=BF16, out_rows=IN_W,
               name="dw_in_conv")
    dwin = _mm(dgates, h2, mode="tn", m=2 * D, n=D, k=T, tm=512, tn=D, tk=wtk, out_dtype=BF16, out_rows=IN_W,
               o_map=lambda i, j, kk: (13 + i, 0), passthru=dwin, name="dw_in_gates")
    for gi in range(3):
        dwin = _mm(dqkv[gi], h2p[gi], mode="tn", m=3 * AW, n=D, k=T, tm=AW, tn=D, tk=wtk, out_dtype=BF16,
                   out_rows=IN_W, a_map=lambda i, j, kk: (i * (T // wtk) + kk, 0),
                   o_map=lambda i, j, kk, gi=gi: (4 + gi + 3 * i, 0), passthru=dwin, name=f"dw_in_qkv{gi}")
    grads["w_in"] = dwin
    rs.append(_send_start(["scatter"], [dwin], [rs[1][4]], "scatter_start2"))
    dab = _tie(dab, [rs[2][4]], "tie_after_scatter2")

    dx1, dg2 = _rms_bwd(x1, mix_norm, dh_gates, dab, (0, 1, 2, 3), dq3s, win_t, dx2, "mix_norm_bwd")

    dgb, dub, actb, hb, dob = _ffn_bwd_pre(x0, ffn1_norm, gg1, uu1, dx1, wd1, "ffn1_bwd_pre")
    grads["ffn1_w_gate"] = _wgrad(dgb, hb, FF, D, "ffn1_dwg")
    rs.append(_send_start(["scatter"], [grads["ffn1_w_gate"]], [rs[2][4]], "scatter_start3"))
    hb = _tie(hb, [rs[3][4]], "tie_after_scatter3")
    grads["ffn1_w_up"] = _wgrad(dub, hb, FF, D, "ffn1_dwu")
    rs.append(_send_start(["scatter"], [grads["ffn1_w_up"]], [rs[3][4]], "scatter_start4"))
    dob = _tie(dob, [rs[4][4]], "tie_after_scatter4")
    grads["ffn1_w_down"] = _wgrad(actb, dob, FF, D, "ffn1_dwd")
    rs.append(_send_start(["scatter"], [grads["ffn1_w_down"]], [rs[4][4]], "scatter_start5"))
    dgb = _tie(dgb, [rs[5][4]], "tie_after_scatter5")
    dx0, dg1 = _ffn_bwd_dx(x0, ffn1_norm, dgb, dub, dx1, wg1, wu1, "ffn1_bwd_dx")
    vec = jnp.concatenate([dg1, dg2, dg3, dgf, dvec[0:3], jnp.broadcast_to(loss_part[:, :1], (1, D))], axis=0)
    rs.append(_send_start(["bcast"], [vec], [rs[5][4]], "scatter_start6"))

    g_out, d_out, m_out, v_out = {}, {}, {}, {}
    me = _my_place()
    after = [rs[last][4]]
    for gi, grp in enumerate(rs_groups):
        kinds = ["scatter"] * len(grp) + (["bcast"] if gi == last else [])
        srcs, lands = _send_wait(kinds, rs[gi], after, f"scatter_wait{gi}")
        for n, src, land in zip(grp, srcs, lands):
            if n == "conv_dw_kernel":
                rows = src.shape[0] // NDEV
                own = lax.dynamic_slice(src, (me * rows, 0), (rows, src.shape[1]))
                g = _gsum(own, land, f"gsum_{n}")[:CONV_W]
                d, m2, v2 = _adamw(w[n][0], g, mo[n][0], vo[n][0], f"adamw_{n}")
                after = [d]
                g, d, m2, v2 = g[None], d[None], m2[None], v2[None]
            else:
                res = _update(src, land, landing_view(w[n], n), landing_view(mo[n], n), landing_view(vo[n], n),
                              f"update_{n}")
                after = [res[1]]
                g, d, m2, v2 = (own_view(a, n) for a in res)
            g_out[n], d_out[n], m_out[n], v_out[n] = g, d, m2, v2
    vland = lands[-1]

    def rows8(src):
        return jnp.concatenate([src[n].reshape(1, D) for n in small] + [jnp.ones((1, D), F32)], axis=0)

    g8, d8, m8, v8 = _small_update(vland, rows8(w), rows8(mo), rows8(vo), "small_update")
    for r, n in enumerate(small):
        shp = w[n].shape
        g_out[n], d_out[n], m_out[n], v_out[n] = (a[r].reshape(shp) for a in (g8, d8, m8, v8))
    loss = g8[7, 0]

    return (loss, dx0[None], *[g_out[n] for n in names], *[d_out[n] for n in names],
            *[m_out[n] for n in names], *[v_out[n] for n in names])
```

```python
import numpy as np
import jax
import jax.numpy as jnp
from jax import lax
from jax.experimental import pallas as pl
from jax.experimental.pallas import tpu as pltpu

F32 = jnp.float32
BF16 = jnp.bfloat16

T = 4096
D = 1024
FF = 2816
NDEV = 8
CONV_W = 31
HEAD = 128
BLK = 128
GROUPS = ((128, 1), (512, 4), (2048, 16))
NHG = 4
AW = NHG * HEAD
IN_W = 2 * D + 3 * 3 * AW + 2 * D
EPS = 1e-6
B1, B2, LR, AEPS, WD, STEP = 0.9, 0.999, 0.001, 1e-08, 0.01, 10
NEG = -1e30
VMEM_LIMIT = 56 * 1024 * 1024
MESH_ID = pl.DeviceIdType.MESH

NT = (((1,), (1,)), ((), ()))
NN = (((1,), (0,)), ((), ()))
TN = (((0,), (0,)), ((), ()))
_DIMS = {"nn": NN, "nt": NT, "tn": TN}


def _cp(sem=None):
    return pltpu.CompilerParams(dimension_semantics=sem, vmem_limit_bytes=VMEM_LIMIT)


def _sig(v):
    return 1.0 / (1.0 + jnp.exp(-v))


def _dot(a, b, dims):
    return lax.dot_general(a, b, dims, preferred_element_type=F32)


def _const_spec(shape):
    nd = len(shape)
    return pl.BlockSpec(shape, lambda *_: (0,) * nd)


def _mm(a, b, *, mode, m, n, k, tm, tn, tk, out_dtype, name, a_map=None, b_map=None,
        o_map=None, out_rows=None, init=None, passthru=None):
    gi, gj, gk = m // tm, n // tn, k // tk
    assert gi * tm == m and gj * tn == n and gk * tk == k, (name, m, n, k, tm, tn, tk)
    if mode == "nn":
        a_blk, b_blk = (tm, tk), (tk, tn)
        da, db = (lambda i, j, kk: (i, kk)), (lambda i, j, kk: (kk, j))
    elif mode == "nt":
        a_blk, b_blk = (tm, tk), (tn, tk)
        da, db = (lambda i, j, kk: (i, kk)), (lambda i, j, kk: (j, kk))
    else:
        a_blk, b_blk = (tk, tm), (tk, tn)
        da, db = (lambda i, j, kk: (kk, i)), (lambda i, j, kk: (kk, j))
    a_map = a_map or da
    b_map = b_map or db
    o_map = o_map or (lambda i, j, kk: (i, j))
    dims = _DIMS[mode]
    extra = init if init is not None else passthru
    out_rows = out_rows or m

    def body(*refs):
        if init is not None:
            a_ref, b_ref, i_ref, o_ref = refs[:4]
        elif passthru is not None:
            a_ref, b_ref, _, o_ref = refs[:4]
        else:
            a_ref, b_ref, o_ref = refs[:3]
        if gk == 1:
            prod = _dot(a_ref[...], b_ref[...], dims)
            if init is not None:
                prod = prod + i_ref[...].astype(F32)
            o_ref[...] = prod.astype(out_dtype)
            return
        acc = refs[-1]
        kk = pl.program_id(2)

        @pl.when(kk == 0)
        def _():
            if init is not None:
                acc[...] = i_ref[...].astype(F32)
            else:
                acc[...] = jnp.zeros_like(acc)

        acc[...] += _dot(a_ref[...], b_ref[...], dims)

        @pl.when(kk == gk - 1)
        def _():
            o_ref[...] = acc[...].astype(out_dtype)

    in_specs = [pl.BlockSpec(a_blk, a_map), pl.BlockSpec(b_blk, b_map)]
    args = [a, b]
    aliases = {}
    if init is not None:
        in_specs.append(pl.BlockSpec((tm, tn), o_map))
        args.append(init)
        aliases = {2: 0}
    elif passthru is not None:
        in_specs.append(pl.BlockSpec(memory_space=pl.ANY))
        args.append(passthru)
        aliases = {2: 0}
    out_dt = extra.dtype if extra is not None else out_dtype
    assert out_dt == out_dtype
    return pl.pallas_call(
        body, name=name, grid=(gi, gj, gk),
        in_specs=in_specs, out_specs=pl.BlockSpec((tm, tn), o_map),
        out_shape=jax.ShapeDtypeStruct((out_rows, n), out_dtype),
        scratch_shapes=[pltpu.VMEM((tm, tn), F32)] if gk > 1 else [],
        input_output_aliases=aliases,
        compiler_params=_cp(("parallel", "parallel", "arbitrary")),
    )(*args)


def _ffn_fwd(x, g, wg_t, wu_t, wd, name, next_gain=None, loss_of=None):
    tm, fc = PERM_TM, 256
    nc = FF // fc
    n_in = 5 + (1 if next_gain is not None else 0) + (2 if loss_of is not None else 0)

    def body(*refs):
        x_ref, g_ref, wg_ref, wu_ref, wd_ref = refs[:5]
        extra_in, outs = refs[5:n_in], refs[n_in:]
        act_ref = outs[-1]
        xv = x_ref[...]
        r = lax.rsqrt(jnp.mean(xv * xv, axis=-1, keepdims=True) + EPS)
        h = (xv * r * g_ref[...]).astype(BF16)
        gg_ref, uu_ref = (outs[0], outs[1]) if loss_of is not None else (outs[1], outs[2])
        for c in range(nc):
            sl = pl.ds(c * fc, fc)
            gg = _dot(h, wg_ref[sl, :], NT)
            uu = _dot(h, wu_ref[sl, :], NT)
            gg_ref[:, sl] = gg.astype(BF16)
            uu_ref[:, sl] = uu.astype(BF16)
            act_ref[:, sl] = (gg * _sig(gg) * uu).astype(BF16)
        y = xv + 0.5 * _dot(act_ref[...], wd_ref[...], NN)
        if loss_of is not None:
            _final_math(y, extra_in[0][...], extra_in[1][...], outs[2], outs[3], outs[4], pl.program_id(0))
            return
        outs[0][...] = y
        if next_gain is not None:
            tile = outs[-2]
            r2 = lax.rsqrt(jnp.mean(y * y, axis=-1, keepdims=True) + EPS)
            hv = y * r2 * extra_in[0][...]
            outs[3][...] = hv.astype(BF16)
            _put_tile(tile, hv)
            for dil, p_ref in zip(DILS, outs[4:4 + len(DILS)]):
                _store_perm(p_ref, tile, dil)

    wspec = pl.BlockSpec((FF, D), lambda i: (0, 0), pipeline_mode=pl.Buffered(1))
    row_d = pl.BlockSpec((tm, D), lambda i: (i, 0))
    row_f = pl.BlockSpec((tm, FF), lambda i: (i, 0))
    in_specs = [row_d, _const_spec((1, D)), wspec, wspec, wspec]
    args = [x, g, wg_t, wu_t, wd]
    f_shape = jax.ShapeDtypeStruct((T, FF), BF16)
    scratch = [pltpu.VMEM((tm, FF), BF16)]
    if loss_of is not None:
        in_specs += [_const_spec((1, D)), row_d]
        args += list(loss_of)
        out_specs = [row_f, row_f, row_d, _const_spec((1, D)), _const_spec((1, 128))]
        out_shape = [f_shape, f_shape, jax.ShapeDtypeStruct((T, D), F32), jax.ShapeDtypeStruct((1, D), F32),
                     jax.ShapeDtypeStruct((1, 128), F32)]
    else:
        out_specs = [row_d, row_f, row_f]
        out_shape = [jax.ShapeDtypeStruct((T, D), F32), f_shape, f_shape]
        if next_gain is not None:
            in_specs.append(_const_spec((1, D)))
            args.append(next_gain)
            out_specs += [row_d] + [_perm_spec(d, D) for d in DILS]
            out_shape += [jax.ShapeDtypeStruct((T, D), BF16)] + [_perm_shape(d, D, BF16) for d in DILS]
            scratch = [_tile_scratch(D)] + scratch
    out = pl.pallas_call(
        body, name=name, grid=(T // tm,), in_specs=in_specs, out_specs=out_specs, out_shape=out_shape,
        scratch_shapes=scratch,
        compiler_params=_cp(("arbitrary",) if loss_of is not None else ("parallel",)),
    )(*args)
    if next_gain is not None:
        return out[0], out[1], out[2], [out[3]] + [o.reshape(T, D) for o in out[4:]]
    return tuple(out)


def _ffn_bwd(x, g, gg_all, uu_all, dout, wg_t, wu_t, wd, name):
    tm, fc = 256, 256
    nc = FF // fc

    def body(x_ref, g_ref, gg_ref, uu_ref, do_ref, wg_ref, wu_ref, wd_ref,
             dx_ref, dgam_ref, dg_ref, du_ref, act_ref, h_ref, db_ref):
        i = pl.program_id(0)
        xv = x_ref[...]
        r = lax.rsqrt(jnp.mean(xv * xv, axis=-1, keepdims=True) + EPS)
        xhat = xv * r
        gam = g_ref[...]
        h_ref[...] = (xhat * gam).astype(BF16)
        dov = do_ref[...]
        dbv = (0.5 * dov).astype(BF16)
        db_ref[...] = dbv
        for c in range(nc):
            sl = pl.ds(c * fc, fc)
            da = _dot(dbv, wd_ref[sl, :], NT)
            gg = gg_ref[:, sl].astype(F32)
            uu = uu_ref[:, sl].astype(F32)
            s = _sig(gg)
            si = gg * s
            dgv = (da * uu * (s * (1.0 + gg * (1.0 - s)))).astype(BF16)
            duv = (da * si).astype(BF16)
            dg_ref[:, sl] = dgv
            du_ref[:, sl] = duv
            act_ref[:, sl] = (si * uu).astype(BF16)
        dh = _dot(dg_ref[...], wg_ref[...], NN) + _dot(du_ref[...], wu_ref[...], NN)

        @pl.when(i == 0)
        def _():
            dgam_ref[...] = jnp.zeros_like(dgam_ref)

        dgam_ref[...] += jnp.sum(dh * xhat, axis=0, keepdims=True)
        dxh = dh * gam
        dx_ref[...] = dov + r * (dxh - xhat * jnp.mean(dxh * xhat, axis=-1, keepdims=True))

    wspec = pl.BlockSpec((FF, D), lambda i: (0, 0), pipeline_mode=pl.Buffered(1))
    row_d = pl.BlockSpec((tm, D), lambda i: (i, 0))
    row_f = pl.BlockSpec((tm, FF), lambda i: (i, 0))
    return pl.pallas_call(
        body, name=name, grid=(T // tm,),
        in_specs=[row_d, _const_spec((1, D)), row_f, row_f, row_d, wspec, wspec, wspec],
        out_specs=[row_d, _const_spec((1, D)), row_f, row_f, row_f, row_d, row_d],
        out_shape=[jax.ShapeDtypeStruct((T, D), F32), jax.ShapeDtypeStruct((1, D), F32),
                   jax.ShapeDtypeStruct((T, FF), BF16), jax.ShapeDtypeStruct((T, FF), BF16),
                   jax.ShapeDtypeStruct((T, FF), BF16), jax.ShapeDtypeStruct((T, D), BF16),
                   jax.ShapeDtypeStruct((T, D), BF16)],
        compiler_params=_cp(("arbitrary",)),
    )(x, g, gg_all, uu_all, dout, wg_t, wu_t, wd)


def _ffn_bwd_pre(x, g, gg_all, uu_all, dout, wd, name):
    tm, fc = 512, 256
    nc = FF // fc

    def body(x_ref, g_ref, gg_ref, uu_ref, do_ref, wd_ref, dg_ref, du_ref, act_ref, h_ref, db_ref):
        xv = x_ref[...]
        r = lax.rsqrt(jnp.mean(xv * xv, axis=-1, keepdims=True) + EPS)
        h_ref[...] = (xv * r * g_ref[...]).astype(BF16)
        dbv = (0.5 * do_ref[...]).astype(BF16)
        db_ref[...] = dbv
        for c in range(nc):
            sl = pl.ds(c * fc, fc)
            da = _dot(dbv, wd_ref[sl, :], NT)
            gg = gg_ref[:, sl].astype(F32)
            uu = uu_ref[:, sl].astype(F32)
            s = _sig(gg)
            si = gg * s
            dg_ref[:, sl] = (da * uu * (s * (1.0 + gg * (1.0 - s)))).astype(BF16)
            du_ref[:, sl] = (da * si).astype(BF16)
            act_ref[:, sl] = (si * uu).astype(BF16)

    wspec = pl.BlockSpec((FF, D), lambda i: (0, 0), pipeline_mode=pl.Buffered(1))
    row_d = pl.BlockSpec((tm, D), lambda i: (i, 0))
    row_f = pl.BlockSpec((tm, FF), lambda i: (i, 0))
    return pl.pallas_call(
        body, name=name, grid=(T // tm,),
        in_specs=[row_d, _const_spec((1, D)), row_f, row_f, row_d, wspec],
        out_specs=[row_f, row_f, row_f, row_d, row_d],
        out_shape=[jax.ShapeDtypeStruct((T, FF), BF16), jax.ShapeDtypeStruct((T, FF), BF16),
                   jax.ShapeDtypeStruct((T, FF), BF16), jax.ShapeDtypeStruct((T, D), BF16),
                   jax.ShapeDtypeStruct((T, D), BF16)],
        compiler_params=_cp(("parallel",)),
    )(x, g, gg_all, uu_all, dout, wd)


def _ffn_bwd_dx(x, g, dgb, dub, dout, wg_t, wu_t, name):
    tm = 512

    def body(x_ref, g_ref, dg_ref, du_ref, do_ref, wg_ref, wu_ref, dx_ref, dgam_ref):
        i = pl.program_id(0)
        xv = x_ref[...]
        r = lax.rsqrt(jnp.mean(xv * xv, axis=-1, keepdims=True) + EPS)
        xhat = xv * r
        gam = g_ref[...]
        dh = _dot(dg_ref[...], wg_ref[...], NN) + _dot(du_ref[...], wu_ref[...], NN)

        @pl.when(i == 0)
        def _():
            dgam_ref[...] = jnp.zeros_like(dgam_ref)

        dgam_ref[...] += jnp.sum(dh * xhat, axis=0, keepdims=True)
        dxh = dh * gam
        dx_ref[...] = do_ref[...] + r * (dxh - xhat * jnp.mean(dxh * xhat, axis=-1, keepdims=True))

    wspec = pl.BlockSpec((FF, D), lambda i: (0, 0), pipeline_mode=pl.Buffered(1))
    row_d = pl.BlockSpec((tm, D), lambda i: (i, 0))
    row_f = pl.BlockSpec((tm, FF), lambda i: (i, 0))
    return pl.pallas_call(
        body, name=name, grid=(T // tm,),
        in_specs=[row_d, _const_spec((1, D)), row_f, row_f, row_d, wspec, wspec],
        out_specs=[row_d, _const_spec((1, D))],
        out_shape=[jax.ShapeDtypeStruct((T, D), F32), jax.ShapeDtypeStruct((1, D), F32)],
        compiler_params=_cp(("arbitrary",)),
    )(x, g, dgb, dub, dout, wg_t, wu_t)


def _wgrad(a, b, m, n, name):
    tm = m // 2 if m == FF else m
    return _mm(a, b, mode="tn", m=m, n=n, k=T, tm=tm, tn=n, tk=min(T, 2048), out_dtype=BF16, name=name)


PERM_TM = 512
DILS = tuple(d for _, d in GROUPS if d > 1)


def _perm_spec(dil, cols):
    return pl.BlockSpec((dil, PERM_TM // dil, cols), lambda i: (0, i, 0))


def _perm_shape(dil, cols, dtype):
    return jax.ShapeDtypeStruct((dil, T // dil, cols), dtype)


LANES = 128


def _tile_scratch(cols):
    return pltpu.VMEM((cols // LANES, PERM_TM, LANES), F32)


def _put_tile(tile, value):
    for c in range(tile.shape[0]):
        tile[c] = value[:, c * LANES:(c + 1) * LANES]


def _get_tile(tile):
    return jnp.concatenate([tile[c] for c in range(tile.shape[0])], axis=1)


def _store_perm(out_ref, tile, dil):
    for r in range(dil):
        for c in range(tile.shape[0]):
            out_ref[r, :, pl.ds(c * LANES, LANES)] = tile[c, pl.ds(r, PERM_TM // dil, stride=dil), :].astype(
                out_ref.dtype)


def _load_unperm(in_ref, tile, dil):
    for r in range(dil):
        for c in range(tile.shape[0]):
            tile[c, pl.ds(r, PERM_TM // dil, stride=dil), :] = in_ref[r, :, pl.ds(c * LANES, LANES)].astype(F32)


def _final_math(xv, gam, tgt, dx_ref, dgam_ref, loss_ref, i):
    r = lax.rsqrt(jnp.mean(xv * xv, axis=-1, keepdims=True) + EPS)
    xhat = xv * r
    err = xhat * gam - tgt
    part = 0.5 * jnp.sum(jnp.mean(err * err, axis=-1, keepdims=True), axis=0, keepdims=True)
    dy = err * (1.0 / D)

    @pl.when(i == 0)
    def _():
        dgam_ref[...] = jnp.zeros_like(dgam_ref)
        loss_ref[...] = jnp.zeros_like(loss_ref)

    dgam_ref[...] += jnp.sum(dy * xhat, axis=0, keepdims=True)
    loss_ref[...] += jnp.broadcast_to(part, loss_ref.shape)
    dxh = dy * gam
    dx_ref[...] = r * (dxh - xhat * jnp.mean(dxh * xhat, axis=-1, keepdims=True))


QKV_BLOCK0 = 2 * D // AW


def _rms_bwd(x, g, dh0, dqkvs, win_t, dres, name):
    tm = PERM_TM
    dils = [d for _, d in GROUPS]
    ng = len(dils)
    assert len(dqkvs) == ng

    def body(*refs):
        x_ref, g_ref, dh0_ref = refs[:3]
        dq_refs = refs[3:3 + 3 * ng]
        w_refs = refs[3 + 3 * ng:3 + 6 * ng]
        dr_ref, dx_ref, dgam_ref = refs[3 + 6 * ng:6 + 6 * ng]
        tile = refs[6 + 6 * ng]
        stages = refs[7 + 6 * ng:]
        i = pl.program_id(0)
        xv = x_ref[...]
        r = lax.rsqrt(jnp.mean(xv * xv, axis=-1, keepdims=True) + EPS)
        xhat = xv * r
        gam = g_ref[...]
        dh = dh0_ref[...]
        si = 0
        for gi, dil in enumerate(dils):
            part = None
            for p in range(3):
                blk = dq_refs[3 * gi + p][...]
                term = _dot(blk.reshape(tm, AW), w_refs[3 * gi + p][...], NN)
                part = term if part is None else part + term
            if dil > 1:
                stage = stages[si]
                si += 1
                stage[...] = part.reshape(dil, tm // dil, D)
                _load_unperm(stage, tile, dil)
                part = _get_tile(tile)
            dh = dh + part

        @pl.when(i == 0)
        def _():
            dgam_ref[...] = jnp.zeros_like(dgam_ref)

        dgam_ref[...] += jnp.sum(dh * xhat, axis=0, keepdims=True)
        dxh = dh * gam
        dx_ref[...] = dr_ref[...] + r * (dxh - xhat * jnp.mean(dxh * xhat, axis=-1, keepdims=True))

    row_d = pl.BlockSpec((tm, D), lambda i: (i, 0))
    dq_specs, dq_args, w_specs = [], [], []
    for gi, (d, a) in enumerate(zip(dils, dqkvs)):
        for p in range(3):
            if d == 1:
                dq_specs.append(pl.BlockSpec((None, tm, AW), lambda i, p=p: (p, i, 0)))
                dq_args.append(a)
            else:
                dq_specs.append(pl.BlockSpec((None, d, tm // d, AW), lambda i, p=p: (p, 0, i, 0)))
                dq_args.append(a.reshape(3, d, T // d, AW))
            w_specs.append(pl.BlockSpec((AW, D), lambda i, q=QKV_BLOCK0 + gi + 3 * p: (q, 0),
                                        pipeline_mode=pl.Buffered(1)))
    return pl.pallas_call(
        body, name=name, grid=(T // tm,),
        in_specs=[row_d, _const_spec((1, D)), row_d] + dq_specs + w_specs + [row_d],
        out_specs=[row_d, _const_spec((1, D))],
        out_shape=[jax.ShapeDtypeStruct((T, D), F32), jax.ShapeDtypeStruct((1, D), F32)],
        scratch_shapes=[_tile_scratch(D)] + [pltpu.VMEM((d, tm // d, D), F32) for d in dils if d > 1],
        compiler_params=_cp(("arbitrary",)),
    )(x, g, dh0, *dq_args, *([win_t] * (3 * ng)), dres)


CONV_TM = 256
CONV_HALO = 32
CONV_RB = 16


def _glu(ab):
    ab = ab.astype(F32)
    return ab[:, :D] * _sig(ab[:, D:])


def _ln_stats(z1):
    mu = jnp.mean(z1, axis=-1, keepdims=True)
    zc = z1 - mu
    rstd = lax.rsqrt(jnp.mean(zc * zc, axis=-1, keepdims=True) + EPS)
    return zc * rstd, rstd


def _fill_shifts(zs):
    n = zs.shape[1] - 8
    for s in range(1, 8):
        zs[s, pl.ds(0, n), :] = zs[0, pl.ds(s, n), :]


def _shifted(zs, start, rows):
    q, s = divmod(start, 8)
    return zs[s, pl.ds(8 * q, rows), :]


def _conv_fwd(ab, kern, dwb, lng, lnb, name, guest=None):
    tm, hl, rb = CONV_TM, CONV_HALO, CONV_RB
    off = hl - (CONV_W - 1)
    if guest is not None:
        g_a, g_b, g_blocks, g_rows = guest
        g_nblk = len(g_blocks)

    def body(ab_ref, abh_ref, k_ref, dwb_ref, lng_ref, lnb_ref, *rest):
        if guest is not None:
            ga_ref, gb_refs, rest = rest[0], rest[1:1 + g_nblk], rest[1 + g_nblk:]
            z1_ref, z3_ref, go_ref, zs = rest
            for q, gb_ref in enumerate(gb_refs):
                go_ref[:, pl.ds(q * g_rows, g_rows)] = _dot(ga_ref[...], gb_ref[...], NT).astype(BF16)
        else:
            z1_ref, z3_ref, zs = rest
        i = pl.program_id(0)
        zs[0, pl.ds(0, hl), :] = jnp.where(i > 0, _glu(abh_ref[...]), 0.0)
        zs[0, pl.ds(hl, tm), :] = _glu(ab_ref[...])
        _fill_shifts(zs)
        for b in range(tm // rb):
            acc = jnp.zeros((rb, D), F32)
            for j in range(CONV_W):
                acc = acc + _shifted(zs, b * rb + off + j, rb) * k_ref[pl.ds(j, 1), :]
            z1 = acc + dwb_ref[...]
            z1_ref[pl.ds(b * rb, rb), :] = z1
            zn, _ = _ln_stats(z1)
            z2 = zn * lng_ref[...] + lnb_ref[...]
            z3_ref[pl.ds(b * rb, rb), :] = (z2 * _sig(z2)).astype(BF16)

    row = pl.BlockSpec((tm, D), lambda i: (i, 0))
    g_specs, g_args, g_ospecs, g_oshapes = [], [], [], []
    if guest is not None:
        kdim = g_a.shape[1]
        g_specs = [pl.BlockSpec((tm, kdim), lambda i: (i, 0))]
        g_specs += [pl.BlockSpec((g_rows, kdim), lambda i, q=q: (q, 0), pipeline_mode=pl.Buffered(1))
                    for q in g_blocks]
        g_args = [g_a] + [g_b] * g_nblk
        g_ospecs = [pl.BlockSpec((tm, g_nblk * g_rows), lambda i: (i, 0))]
        g_oshapes = [jax.ShapeDtypeStruct((T, g_nblk * g_rows), BF16)]
    return pl.pallas_call(
        body, name=name, grid=(T // tm,),
        in_specs=[pl.BlockSpec((tm, 2 * D), lambda i: (i, 0)),
                  pl.BlockSpec((hl, 2 * D), lambda i: (jnp.maximum(i * (tm // hl) - 1, 0), 0)),
                  _const_spec((32, D)), _const_spec((1, D)), _const_spec((1, D)), _const_spec((1, D))] + g_specs,
        out_specs=[row, row] + g_ospecs,
        out_shape=[jax.ShapeDtypeStruct((T, D), F32), jax.ShapeDtypeStruct((T, D), BF16)] + g_oshapes,
        scratch_shapes=[pltpu.VMEM((8, hl + tm, D), F32)],
        compiler_params=_cp(("parallel",)),
    )(ab, ab, kern, dwb, lng, lnb, *g_args)


GUEST_TM = 256


def _conv_bwd(dz3, z1, ab, kern, lng, lnb, name, guest_lhs=(), guest_rhs=None):
    tm, hl, rb = CONV_TM, CONV_HALO, CONV_RB
    off = hl - (CONV_W - 1)
    nsteps = T // tm
    ng = len(guest_lhs)
    gblocks = [a.shape[1] // GUEST_TM for a in guest_lhs]
    assert all(gb <= nsteps and gb * GUEST_TM == a.shape[1] for gb, a in zip(gblocks, guest_lhs))

    def ln_bwd(dz3v, z1v, lngv, lnbv):
        zn, rstd = _ln_stats(z1v)
        z2 = zn * lngv + lnbv
        s = _sig(z2)
        dz2 = dz3v * (s * (1.0 + z2 * (1.0 - s)))
        dzn = dz2 * lngv
        dz1 = rstd * (dzn - jnp.mean(dzn, axis=-1, keepdims=True)
                      - zn * jnp.mean(dzn * zn, axis=-1, keepdims=True))
        return dz1, dz2, zn

    def body(dz3_ref, dz3h_ref, z1_ref, z1h_ref, ab_ref, abh_ref, k_ref, lng_ref, lnb_ref, *rest):
        g_in, rest = rest[:ng + (1 if ng else 0)], rest[ng + (1 if ng else 0):]
        dab_ref, dk_ref, dvec_ref = rest[:3]
        g_out, (zs, dzs) = rest[3:3 + ng], rest[3 + ng:]
        i = pl.program_id(0)
        lngv, lnbv = lng_ref[...], lnb_ref[...]

        for a_ref, o_ref, gb in zip(g_in[:ng], g_out, gblocks):
            @pl.when(i < gb)
            def _(a_ref=a_ref, o_ref=o_ref):
                o_ref[...] = _dot(a_ref[...], g_in[ng][...], TN).astype(BF16)

        @pl.when(i == 0)
        def _():
            dk_ref[...] = jnp.zeros_like(dk_ref)
            dvec_ref[...] = jnp.zeros_like(dvec_ref)

        dz1, dz2, zn = ln_bwd(dz3_ref[...].astype(F32), z1_ref[...], lngv, lnbv)
        dvec_ref[pl.ds(0, 1), :] += jnp.sum(dz1, axis=0, keepdims=True)
        dvec_ref[pl.ds(1, 1), :] += jnp.sum(dz2 * zn, axis=0, keepdims=True)
        dvec_ref[pl.ds(2, 1), :] += jnp.sum(dz2, axis=0, keepdims=True)
        dzs[0, pl.ds(0, tm), :] = dz1
        dz1h, _, _ = ln_bwd(dz3h_ref[...].astype(F32), z1h_ref[...], lngv, lnbv)
        dzs[0, pl.ds(tm, hl), :] = jnp.where(i < nsteps - 1, dz1h, 0.0)
        _fill_shifts(dzs)
        zs[0, pl.ds(0, hl), :] = jnp.where(i > 0, _glu(abh_ref[...]), 0.0)
        zs[0, pl.ds(hl, tm), :] = _glu(ab_ref[...])
        _fill_shifts(zs)

        for j in range(CONV_W):
            tot = jnp.zeros((rb, D), F32)
            for b in range(tm // rb):
                tot = tot + dzs[0, pl.ds(b * rb, rb), :] * _shifted(zs, b * rb + off + j, rb)
            dk_ref[pl.ds(j, 1), :] += jnp.sum(tot, axis=0, keepdims=True)

        for b in range(tm // rb):
            acc = jnp.zeros((rb, D), F32)
            for j in range(CONV_W):
                acc = acc + _shifted(dzs, b * rb + (CONV_W - 1) - j, rb) * k_ref[pl.ds(j, 1), :]
            av = ab_ref[pl.ds(b * rb, rb), pl.ds(0, D)].astype(F32)
            sb = _sig(ab_ref[pl.ds(b * rb, rb), pl.ds(D, D)].astype(F32))
            dab_ref[pl.ds(b * rb, rb), pl.ds(0, D)] = (acc * sb).astype(BF16)
            dab_ref[pl.ds(b * rb, rb), pl.ds(D, D)] = (acc * av * sb * (1.0 - sb)).astype(BF16)

    row = pl.BlockSpec((tm, D), lambda i: (i, 0))
    nxt = pl.BlockSpec((hl, D), lambda i: (jnp.minimum((i + 1) * (tm // hl), T // hl - 1), 0))
    g_specs, g_args, g_ospecs, g_oshapes = [], [], [], []
    for a, gb in zip(guest_lhs, gblocks):
        g_specs.append(pl.BlockSpec((T, GUEST_TM), lambda i, gb=gb: (0, jnp.minimum(i, gb - 1))))
        g_args.append(a)
        g_ospecs.append(pl.BlockSpec((GUEST_TM, guest_rhs.shape[1]), lambda i, gb=gb: (jnp.minimum(i, gb - 1), 0)))
        g_oshapes.append(jax.ShapeDtypeStruct((a.shape[1], guest_rhs.shape[1]), BF16))
    if ng:
        g_specs.append(pl.BlockSpec(guest_rhs.shape, lambda i: (0, 0), pipeline_mode=pl.Buffered(1)))
        g_args.append(guest_rhs)
    return pl.pallas_call(
        body, name=name, grid=(nsteps,),
        in_specs=[row, nxt, row, nxt,
                  pl.BlockSpec((tm, 2 * D), lambda i: (i, 0)),
                  pl.BlockSpec((hl, 2 * D), lambda i: (jnp.maximum(i * (tm // hl) - 1, 0), 0)),
                  _const_spec((32, D)), _const_spec((1, D)), _const_spec((1, D))] + g_specs,
        out_specs=[pl.BlockSpec((tm, 2 * D), lambda i: (i, 0)), _const_spec((32, D)), _const_spec((8, D))]
        + g_ospecs,
        out_shape=[jax.ShapeDtypeStruct((T, 2 * D), BF16), jax.ShapeDtypeStruct((32, D), F32),
                   jax.ShapeDtypeStruct((8, D), F32)] + g_oshapes,
        scratch_shapes=[pltpu.VMEM((8, hl + tm, D), F32), pltpu.VMEM((8, tm + hl, D), F32)],
        compiler_params=_cp(("arbitrary",)),
    )(dz3, dz3, z1, z1, ab, ab, kern, lng, lnb, *g_args)


def _alibi_slopes():
    h = np.arange(1, 3 * NHG + 1, dtype=np.float32)
    return np.power(np.float32(2.0), -8.0 * h / np.float32(3 * NHG)).astype(np.float32)


def _band_bias(gi):
    _, dil = GROUPS[gi]
    slopes = _alibi_slopes()[gi * NHG:(gi + 1) * NHG]
    qi = np.arange(BLK)[:, None]
    ki = np.arange(2 * BLK)[None, :]
    steps = BLK + qi - ki
    band = (steps >= 0) & (steps <= BLK)
    bias = -slopes[:, None, None] * (dil * steps).astype(np.float32)[None]
    return jnp.asarray(np.where(band[None], bias, np.float32(NEG)).astype(np.float32))


QB_FWD = 8
QB_BWD = 32


def _attn_specs(qb, c0):
    prev = lambda n: jnp.maximum(n * qb - 1, 0)
    return [pl.BlockSpec((qb * BLK, HEAD), lambda h, n: (n, c0 + h)),
            pl.BlockSpec((BLK, HEAD), lambda h, n: (prev(n), c0 + NHG + h)),
            pl.BlockSpec((qb * BLK, HEAD), lambda h, n: (n, c0 + NHG + h)),
            pl.BlockSpec((BLK, HEAD), lambda h, n: (prev(n), c0 + 2 * NHG + h)),
            pl.BlockSpec((qb * BLK, HEAD), lambda h, n: (n, c0 + 2 * NHG + h)),
            pl.BlockSpec((None, BLK, 2 * BLK), lambda h, n: (h, 0, 0))]


def _scores(q, kcat, bias, blk, seg):
    s = _dot(q, kcat, NT) * (HEAD ** -0.5) + bias
    col = lax.broadcasted_iota(jnp.int32, s.shape, 1)
    first = (blk % seg) == 0
    return jnp.where(jnp.logical_and(first, col < BLK), NEG, s)


def _attn_fwd(qkv, gi, name, col0=0):
    seg = (T // GROUPS[gi][1]) // BLK

    qb = min(QB_FWD, T // BLK)

    def body(q_ref, kp_ref, kc_ref, vp_ref, vc_ref, bias_ref, o_ref, l_ref):
        n = pl.program_id(0)
        for h in range(NHG):
            cols = pl.ds(h * HEAD, HEAD)
            kwin = jnp.concatenate([kp_ref[:, cols], kc_ref[:, cols]], axis=0)
            vwin = jnp.concatenate([vp_ref[:, cols], vc_ref[:, cols]], axis=0)
            bias = bias_ref[h]
            for b in range(qb):
                rows = pl.ds(b * BLK, BLK)
                s = _scores(q_ref[rows, cols], kwin[b * BLK:(b + 2) * BLK], bias, n * qb + b, seg)
                mx = jnp.max(s, axis=-1, keepdims=True)
                p = jnp.exp(s - mx)
                den = jnp.sum(p, axis=-1, keepdims=True)
                o_ref[rows, cols] = (_dot(p.astype(BF16), vwin[b * BLK:(b + 2) * BLK], NN) / den).astype(BF16)
                l_ref[rows, cols] = jnp.broadcast_to(mx + jnp.log(den), (BLK, HEAD))

    prev = lambda n: jnp.maximum(n * qb - 1, 0)
    c0 = col0 // AW
    cur = lambda part: pl.BlockSpec((qb * BLK, AW), lambda n: (n, part))
    halo = lambda part: pl.BlockSpec((BLK, AW), lambda n: (prev(n), part))
    return pl.pallas_call(
        body, name=name, grid=(T // (qb * BLK),),
        in_specs=[cur(c0), halo(c0 + 1), cur(c0 + 1), halo(c0 + 2), cur(c0 + 2),
                  _const_spec((NHG, BLK, 2 * BLK))],
        out_specs=[cur(0), cur(0)],
        out_shape=[jax.ShapeDtypeStruct((T, AW), BF16), jax.ShapeDtypeStruct((T, AW), F32)],
        compiler_params=_cp(("parallel",)),
    )(qkv, qkv, qkv, qkv, qkv, _band_bias(gi))


def _attn_bwd(qkv, dob, lse, delta, gi, name, col0=0):
    seg = (T // GROUPS[gi][1]) // BLK
    qb = min(QB_BWD, T // BLK)
    nb = T // (qb * BLK)
    scale = HEAD ** -0.5

    def body(q_ref, kp_ref, kc_ref, vp_ref, vc_ref, bias_ref, do_ref, l_ref, dl_ref, out_ref, dk_acc, dv_acc):
        n = pl.program_id(1)
        kwin = jnp.concatenate([kp_ref[...], kc_ref[...]], axis=0)
        vwin = jnp.concatenate([vp_ref[...], vc_ref[...]], axis=0)
        bias = bias_ref[...]
        dks, dvs = [], []
        for b in range(qb):
            rows = pl.ds(b * BLK, BLK)
            q = q_ref[rows, :]
            kcat = kwin[b * BLK:(b + 2) * BLK]
            s = _scores(q, kcat, bias, n * qb + b, seg)
            p = jnp.exp(s - l_ref[rows, pl.ds(0, 1)])
            dov = do_ref[rows, :]
            dvs.append(_dot(p.astype(BF16), dov, TN))
            dp = _dot(dov, vwin[b * BLK:(b + 2) * BLK], NT)
            dsb = (p * (dp - dl_ref[rows, pl.ds(0, 1)]) * scale).astype(BF16)
            row = pl.ds(pl.multiple_of((n * qb + b) * BLK, BLK), BLK)
            out_ref[0, row, :] = _dot(dsb, kcat, NN).astype(BF16)
            dks.append(_dot(dsb, q, TN))
        for b in range(qb):
            row = pl.ds(pl.multiple_of((n * qb + b) * BLK, BLK), BLK)
            if b + 1 < qb:
                dk_acc[row, :] = dks[b][BLK:] + dks[b + 1][:BLK]
                dv_acc[row, :] = dvs[b][BLK:] + dvs[b + 1][:BLK]
            else:
                dk_acc[row, :] = dks[b][BLK:]
                dv_acc[row, :] = dvs[b][BLK:]

        @pl.when(n > 0)
        def _():
            prow = pl.ds(pl.multiple_of((n * qb - 1) * BLK, BLK), BLK)
            dk_acc[prow, :] += dks[0][:BLK]
            dv_acc[prow, :] += dvs[0][:BLK]

        @pl.when(n == nb - 1)
        def _():
            out_ref[1] = dk_acc[...].astype(BF16)
            out_ref[2] = dv_acc[...].astype(BF16)

    oblk = pl.BlockSpec((qb * BLK, HEAD), lambda h, n: (n, h))
    return pl.pallas_call(
        body, name=name, grid=(NHG, nb),
        in_specs=_attn_specs(qb, col0 // HEAD) + [oblk, oblk, oblk],
        out_specs=pl.BlockSpec((3, T, HEAD), lambda h, n: (0, 0, h)),
        out_shape=jax.ShapeDtypeStruct((3, T, AW), BF16),
        scratch_shapes=[pltpu.VMEM((T, HEAD), F32), pltpu.VMEM((T, HEAD), F32)],
        compiler_params=_cp(("parallel", "arbitrary")),
    )(qkv, qkv, qkv, qkv, qkv, _band_bias(gi), dob, lse, delta)


def _merge(outs, lses, name):
    tm = PERM_TM
    dils = [d for _, d in GROUPS]
    ng = len(dils)

    def body(*refs):
        in_refs = refs[:2 * ng]
        ab_ref = refs[2 * ng]
        lse_refs = refs[2 * ng + 1:3 * ng + 1]
        tile = refs[-1]

        def token_order(ref, dil):
            if dil == 1:
                return ref[...].astype(F32)
            _load_unperm(ref, tile, dil)
            return _get_tile(tile)

        os = [token_order(in_refs[2 * i], d) for i, d in enumerate(dils)]
        ls = [token_order(in_refs[2 * i + 1], d) for i, d in enumerate(dils)]
        mx = jnp.maximum(jnp.maximum(ls[0], ls[1]), ls[2])
        es = [jnp.exp(v - mx) for v in ls]
        tot = es[0] + es[1] + es[2]
        att = (es[0] / tot) * os[0] + (es[1] / tot) * os[1] + (es[2] / tot) * os[2]
        ab_ref[...] = att.astype(BF16)
        lse = mx + jnp.log(tot)
        _put_tile(tile, lse)
        for dil, ref in zip(dils, lse_refs):
            if dil == 1:
                ref[...] = lse
            else:
                _store_perm(ref, tile, dil)

    row = pl.BlockSpec((tm, AW), lambda i: (i, 0))
    specs = [row if d == 1 else _perm_spec(d, AW) for d in dils]
    args = []
    for d, o, l in zip(dils, outs, lses):
        args += [o, l] if d == 1 else [o.reshape(d, T // d, AW), l.reshape(d, T // d, AW)]
    out = pl.pallas_call(
        body, name=name, grid=(T // tm,),
        in_specs=[sp for sp in specs for _ in range(2)], out_specs=[row] + specs,
        out_shape=[jax.ShapeDtypeStruct((T, AW), BF16)]
        + [jax.ShapeDtypeStruct((T, AW), F32) if d == 1 else _perm_shape(d, AW, F32) for d in dils],
        scratch_shapes=[_tile_scratch(AW)],
        compiler_params=_cp(("parallel",)),
    )(*args)
    return out[0], [o.reshape(T, AW) for o in out[1:]]


GATE_BLOCK0 = (IN_W - 2 * D) // (D // 2)


def _mix_out(z3b, attnb, gates, wc, wa_t, wo, x1, name):
    tm = 512

    def body(z_ref, a_ref, g_ref, wc_ref, wa_ref, wo_ref, x_ref, xo_ref, yc_ref, ya_ref, mx_ref):
        yc = _dot(z_ref[...], wc_ref[...], NN)
        ya = _dot(a_ref[...], wa_ref[...], NT)
        yc_ref[...] = yc.astype(BF16)
        ya_ref[...] = ya.astype(BF16)
        gv = g_ref[...].astype(F32)
        mixed = (_sig(gv[:, :D]) * yc + _sig(gv[:, D:]) * ya).astype(BF16)
        mx_ref[...] = mixed
        xo_ref[...] = x_ref[...] + _dot(mixed, wo_ref[...], NN)

    row = pl.BlockSpec((tm, D), lambda i: (i, 0))
    return pl.pallas_call(
        body, name=name, grid=(T // tm,),
        in_specs=[row, pl.BlockSpec((tm, AW), lambda i: (i, 0)), pl.BlockSpec((tm, 2 * D), lambda i: (i, 0)),
                  _const_spec((D, D)), _const_spec((D, AW)), _const_spec((D, D)), row],
        out_specs=[row, row, row, row],
        out_shape=[jax.ShapeDtypeStruct((T, D), F32), jax.ShapeDtypeStruct((T, D), BF16),
                   jax.ShapeDtypeStruct((T, D), BF16), jax.ShapeDtypeStruct((T, D), BF16)],
        compiler_params=_cp(("parallel",)),
    )(z3b, attnb, gates, wc, wa_t, wo, x1)


def _mix_out_bwd(dx2, gates, yc, ya, attn, wc, wa_t, wo, win_t, name):
    tm = PERM_TM
    dils = [d for _, d in GROUPS]
    ng = len(dils)

    def body(dx_ref, g_ref, yc_ref, ya_ref, at_ref, wc_ref, wa_ref, wo_ref, wg0_ref, wg1_ref, wg2_ref, wg3_ref,
             dg_ref, dyc_ref, dya_ref, dxb_ref, dz3_ref, dhg_ref, *rest):
        dat_refs, dl_refs, tile = rest[:ng], rest[ng:2 * ng], rest[-1]
        dxb = dx_ref[...].astype(BF16)
        dxb_ref[...] = dxb
        dmix = _dot(dxb, wo_ref[...], NT)
        gv = g_ref[...].astype(F32)
        sc = _sig(gv[:, :D])
        sa = _sig(gv[:, D:])
        ycv, yav = yc_ref[...].astype(F32), ya_ref[...].astype(F32)
        dgc = (dmix * ycv * sc * (1.0 - sc)).astype(BF16)
        dga = (dmix * yav * sa * (1.0 - sa)).astype(BF16)
        dg_ref[:, pl.ds(0, D)] = dgc
        dg_ref[:, pl.ds(D, D)] = dga
        half = D // 2
        dhg_ref[...] = (_dot(dgc[:, :half], wg0_ref[...], NN) + _dot(dgc[:, half:], wg1_ref[...], NN)
                        + _dot(dga[:, :half], wg2_ref[...], NN) + _dot(dga[:, half:], wg3_ref[...], NN))
        dyc = (dmix * sc).astype(BF16)
        dya = (dmix * sa).astype(BF16)
        dyc_ref[...] = dyc
        dya_ref[...] = dya
        dz3_ref[...] = _dot(dyc, wc_ref[...], NT).astype(BF16)
        dat = _dot(dya, wa_ref[...], NN)
        prod = dat * at_ref[...].astype(F32)
        delta = jnp.concatenate(
            [jnp.broadcast_to(jnp.sum(prod[:, h * HEAD:(h + 1) * HEAD], axis=-1, keepdims=True), (tm, HEAD))
             for h in range(NHG)], axis=1)
        for value, out_refs in ((dat, dat_refs), (delta, dl_refs)):
            _put_tile(tile, value)
            for dil, ref in zip(dils, out_refs):
                if dil == 1:
                    ref[...] = value.astype(ref.dtype)
                else:
                    _store_perm(ref, tile, dil)

    row = pl.BlockSpec((tm, D), lambda i: (i, 0))
    row2 = pl.BlockSpec((tm, 2 * D), lambda i: (i, 0))
    rowa = pl.BlockSpec((tm, AW), lambda i: (i, 0))
    aspecs = [rowa if d == 1 else _perm_spec(d, AW) for d in dils]

    def ashapes(dtype):
        return [jax.ShapeDtypeStruct((T, AW), dtype) if d == 1 else _perm_shape(d, AW, dtype) for d in dils]

    out = pl.pallas_call(
        body, name=name, grid=(T // tm,),
        in_specs=[row, row2, row, row, rowa, _const_spec((D, D)), _const_spec((D, AW)), _const_spec((D, D))]
        + [pl.BlockSpec((D // 2, D), lambda i, q=q: (GATE_BLOCK0 + q, 0), pipeline_mode=pl.Buffered(1))
           for q in range(4)],
        out_specs=[row2, row, row, row, row, row] + aspecs + aspecs,
        out_shape=[jax.ShapeDtypeStruct((T, 2 * D), BF16), jax.ShapeDtypeStruct((T, D), BF16),
                   jax.ShapeDtypeStruct((T, D), BF16), jax.ShapeDtypeStruct((T, D), BF16),
                   jax.ShapeDtypeStruct((T, D), BF16), jax.ShapeDtypeStruct((T, D), F32)]
        + ashapes(BF16) + ashapes(F32),
        scratch_shapes=[_tile_scratch(AW)],
        compiler_params=_cp(("parallel",)),
    )(dx2, gates, yc, ya, attn, wc, wa_t, wo, win_t, win_t, win_t, win_t)
    dats = [o.reshape(T, AW) for o in out[6:6 + ng]]
    deltas = [o.reshape(T, AW) for o in out[6 + ng:6 + 2 * ng]]
    return out[0], out[1], out[2], out[3], out[4], out[5], dats, deltas


def _peer(k):
    x, y, c = lax.axis_index("x"), lax.axis_index("y"), lax.axis_index("c")
    px = 1 - x if k & 4 else x
    py = 1 - y if k & 2 else y
    pc = 1 - c if k & 1 else c
    return (px, py, pc), 4 * px + 2 * py + pc


HBM_SPEC = pl.BlockSpec(memory_space=pltpu.HBM)
SEM_SPEC = pl.BlockSpec(memory_space=pltpu.SEMAPHORE)
EFFECT = pltpu.SideEffectType.DATAFLOW_SIDE_EFFECTING


def _my_place():
    return 4 * lax.axis_index("x") + 2 * lax.axis_index("y") + lax.axis_index("c")


def _tie(a, order_after, name):
    na = len(order_after)

    def body(*refs):
        del refs

    return pl.pallas_call(
        body, name=name, in_specs=[pl.BlockSpec(memory_space=pl.ANY)] * (1 + na),
        out_specs=pl.BlockSpec(memory_space=pl.ANY), out_shape=jax.ShapeDtypeStruct(a.shape, a.dtype),
        input_output_aliases={0: 0},
    )(a, *order_after)


def _prep_gather(ws, order_after, name):
    me = jnp.reshape(_my_place(), (1,)).astype(jnp.int32)
    n = len(ws)
    na = len(order_after)
    shapes = [((32, wv.shape[1]), F32) if wv.shape[0] == CONV_W else (wv.shape, BF16) for wv in ws]

    def body(me_ref, *refs):
        del me_ref
        ins, outs = refs[:n], refs[n + na:]
        for wv, i_ref, o_ref in zip(ws, ins, outs):
            if wv.shape[0] == CONV_W:
                o_ref[pl.ds(0, CONV_W), :] = i_ref[...]
                o_ref[pl.ds(CONV_W, 1), :] = jnp.zeros((1, wv.shape[1]), F32)
            else:
                o_ref[...] = i_ref[...].astype(BF16)

    grid_spec = pltpu.PrefetchScalarGridSpec(
        num_scalar_prefetch=1, grid=(1,),
        in_specs=[pl.BlockSpec(wv.shape, lambda i, m: (0, 0)) for wv in ws]
        + [pl.BlockSpec(memory_space=pl.ANY)] * na,
        out_specs=[pl.BlockSpec(shp, lambda i, m: (m[0], 0)) for shp, _ in shapes])
    return pl.pallas_call(
        body, name=name, grid_spec=grid_spec,
        out_shape=[jax.ShapeDtypeStruct((NDEV * shp[0], shp[1]), dt) for shp, dt in shapes],
        compiler_params=_cp(("arbitrary",)),
    )(me, *ws, *order_after)


GATHER_A = ((1, 0), (2, 0), (4, 0), (6, 0))
GATHER_B = ((1, 2), (1, 4), (1, 6))
GATHER_DIRECT = tuple((k, 0) for k in range(1, NDEV))


def _gather_start(lands, plan, order_after, name):
    n = len(lands)
    na = len(order_after)
    npl = len(plan)

    def body(*refs):
        land_refs = refs[:n]
        send, recv = refs[n + na], refs[n + na + 1]
        token = refs[-1]
        for w in range(n):
            rows = lands[w].shape[0] // NDEV
            for p, (k, j) in enumerate(plan):
                peer, _ = _peer(k)
                _, blk = _peer(j)
                part = land_refs[w].at[pl.ds(blk * rows, rows)]
                i = w * npl + p
                pltpu.make_async_remote_copy(src_ref=part, dst_ref=part, send_sem=send.at[i], recv_sem=recv.at[i],
                                             device_id=peer, device_id_type=MESH_ID).start()
        token[...] = jnp.zeros_like(token)

    nsem = n * npl
    bufs = [pltpu.with_memory_space_constraint(a, pltpu.HBM) for a in lands]
    out = pl.pallas_call(
        body, name=name,
        in_specs=[HBM_SPEC] * n + [pl.BlockSpec(memory_space=pl.ANY)] * na,
        out_specs=[SEM_SPEC, SEM_SPEC] + [HBM_SPEC] * n + [pl.BlockSpec(memory_space=pltpu.VMEM)],
        out_shape=[pltpu.SemaphoreType.DMA((nsem,)), pltpu.SemaphoreType.DMA((nsem,))]
        + [pltpu.HBM(a.shape, a.dtype) for a in bufs] + [jax.ShapeDtypeStruct((8, 128), F32)],
        input_output_aliases={i: 2 + i for i in range(n)},
        compiler_params=pltpu.CompilerParams(has_side_effects=EFFECT),
    )(*bufs, *order_after)
    return out[0], out[1], out[2:2 + n], out[-1]


def _gather_wait(started, plan, order_after, name):
    send, recv, lands, _ = started
    n = len(lands)
    na = len(order_after)
    npl = len(plan)

    def body(*refs):
        land_refs = refs[:n]
        send_ref, recv_ref = refs[n], refs[n + 1]
        for w in range(n):
            rows = lands[w].shape[0] // NDEV
            for p, (k, j) in enumerate(plan):
                peer, _ = _peer(k)
                _, blk = _peer(j)
                part = land_refs[w].at[pl.ds(blk * rows, rows)]
                i = w * npl + p
                cp = pltpu.make_async_remote_copy(src_ref=part, dst_ref=part, send_sem=send_ref.at[i],
                                                  recv_sem=recv_ref.at[i], device_id=peer, device_id_type=MESH_ID)
                cp.wait_send()
                cp.wait_recv()

    out = pl.pallas_call(
        body, name=name,
        in_specs=[HBM_SPEC] * n + [SEM_SPEC, SEM_SPEC] + [pl.BlockSpec(memory_space=pl.ANY)] * na,
        out_specs=[HBM_SPEC] * n,
        out_shape=[pltpu.HBM(a.shape, a.dtype) for a in lands],
        input_output_aliases={i: i for i in range(n)},
        compiler_params=pltpu.CompilerParams(has_side_effects=EFFECT),
    )(*lands, send, recv, *order_after)
    return list(out)


def _copy_ends(kind, src, land, me, plin, k):
    if kind == "scatter":
        rows = src.shape[0] // NDEV
        return src.at[pl.ds(plin * rows, rows)], land.at[k - 1]
    return src, land.at[me]


def _landing(kind, src):
    me = _my_place()
    if kind == "scatter":
        return lax.empty((NDEV - 1, src.shape[0] // NDEV) + src.shape[1:], src.dtype)
    land = lax.empty((NDEV,) + src.shape, src.dtype)
    return lax.dynamic_update_slice(land, src[None], (me,) + (0,) * src.ndim)


def _send_start(kinds, srcs, order_after, name):
    n = len(srcs)
    lands = [_landing(kd, s) for kd, s in zip(kinds, srcs)]
    na = len(order_after)

    def body(*refs):
        src_refs, land_refs = refs[:n], refs[n:2 * n]
        send, recv = refs[2 * n + na], refs[2 * n + na + 1]
        token = refs[-1]
        _, me = _peer(0)
        for w in range(n):
            for k in range(1, NDEV):
                peer, plin = _peer(k)
                s, d = _copy_ends(kinds[w], src_refs[w], land_refs[w], me, plin, k)
                i = w * (NDEV - 1) + k - 1
                pltpu.make_async_remote_copy(src_ref=s, dst_ref=d, send_sem=send.at[i], recv_sem=recv.at[i],
                                             device_id=peer, device_id_type=MESH_ID).start()
        token[...] = jnp.zeros_like(token)

    nsem = n * (NDEV - 1)
    bufs = [pltpu.with_memory_space_constraint(a, pltpu.HBM) for a in list(srcs) + lands]
    out = pl.pallas_call(
        body, name=name,
        in_specs=[HBM_SPEC] * (2 * n) + [pl.BlockSpec(memory_space=pl.ANY)] * na,
        out_specs=[SEM_SPEC, SEM_SPEC] + [HBM_SPEC] * (2 * n) + [pl.BlockSpec(memory_space=pltpu.VMEM)],
        out_shape=[pltpu.SemaphoreType.DMA((nsem,)), pltpu.SemaphoreType.DMA((nsem,))]
        + [pltpu.HBM(a.shape, a.dtype) for a in bufs] + [jax.ShapeDtypeStruct((8, 128), F32)],
        input_output_aliases={i: 2 + i for i in range(2 * n)},
        compiler_params=pltpu.CompilerParams(has_side_effects=EFFECT),
    )(*bufs, *order_after)
    return out[0], out[1], out[2:2 + n], out[2 + n:2 + 2 * n], out[-1]


def _send_wait(kinds, started, order_after, name):
    send, recv, srcs, lands, _ = started
    n = len(srcs)
    na = len(order_after)

    def body(*refs):
        src_refs, land_refs = refs[:n], refs[n:2 * n]
        send_ref, recv_ref = refs[2 * n], refs[2 * n + 1]
        _, me = _peer(0)
        for w in range(n):
            for k in range(1, NDEV):
                peer, plin = _peer(k)
                s, d = _copy_ends(kinds[w], src_refs[w], land_refs[w], me, plin, k)
                i = w * (NDEV - 1) + k - 1
                cp = pltpu.make_async_remote_copy(src_ref=s, dst_ref=d, send_sem=send_ref.at[i],
                                                  recv_sem=recv_ref.at[i], device_id=peer, device_id_type=MESH_ID)
                cp.wait_send()
                cp.wait_recv()

    bufs = list(srcs) + list(lands)
    out = pl.pallas_call(
        body, name=name,
        in_specs=[HBM_SPEC] * (2 * n) + [SEM_SPEC, SEM_SPEC] + [pl.BlockSpec(memory_space=pl.ANY)] * na,
        out_specs=[HBM_SPEC] * (2 * n),
        out_shape=[pltpu.HBM(a.shape, a.dtype) for a in bufs],
        input_output_aliases={i: i for i in range(2 * n)},
        compiler_params=pltpu.CompilerParams(has_side_effects=EFFECT),
    )(*bufs, send, recv, *order_after)
    return out[:n], out[n:]


def _gsum(own, land, name):
    rows, cols = own.shape
    tr = rows // 2 if rows * cols > 512 * 1024 and rows % 32 == 0 else rows

    def body(own_ref, l_ref, o_ref):
        tot = own_ref[...].astype(F32)
        for s in range(NDEV - 1):
            tot = tot + l_ref[s].astype(F32)
        o_ref[...] = tot

    return pl.pallas_call(
        body, name=name, grid=(rows // tr,),
        in_specs=[pl.BlockSpec((tr, cols), lambda i: (i, 0)),
                  pl.BlockSpec((NDEV - 1, tr, cols), lambda i: (0, i, 0))],
        out_specs=pl.BlockSpec((tr, cols), lambda i: (i, 0)),
        out_shape=jax.ShapeDtypeStruct((rows, cols), F32),
        compiler_params=_cp(("parallel",)),
    )(own, land)


def _adamw_math(w, g, m, v):
    m2 = B1 * m + (1.0 - B1) * g
    v2 = B2 * v + (1.0 - B2) * (g * g)
    m_hat = m2 / (1.0 - B1 ** STEP)
    v_hat = v2 / (1.0 - B2 ** STEP)
    delta = -LR * (m_hat / (jnp.sqrt(v_hat) + AEPS) + WD * w)
    return delta, m2, v2


def _adamw(w, g, m, v, name):
    rows, cols = w.shape
    tr = 256 if rows % 256 == 0 and rows > 256 else rows

    def body(w_ref, g_ref, m_ref, v_ref, d_ref, mo_ref, vo_ref):
        d, m2, v2 = _adamw_math(w_ref[...], g_ref[...], m_ref[...], v_ref[...])
        d_ref[...] = d
        mo_ref[...] = m2
        vo_ref[...] = v2

    blk = pl.BlockSpec((tr, cols), lambda i: (i, 0))
    return pl.pallas_call(
        body, name=name, grid=(rows // tr,), in_specs=[blk] * 4, out_specs=[blk] * 3,
        out_shape=[jax.ShapeDtypeStruct((rows, cols), F32)] * 3,
        compiler_params=_cp(("parallel",)),
    )(w, g, m, v)


UPD_TC = 256


def _update(src, land, w, m, v, name):
    rows, cols = land.shape[1:]
    tc = min(UPD_TC if rows > 512 else 2 * UPD_TC, cols)
    me = jnp.reshape(_my_place(), (1,)).astype(jnp.int32)

    def body(me_ref, own_ref, l_ref, w_ref, m_ref, v_ref, g_ref, d_ref, mo_ref, vo_ref):
        del me_ref
        g = own_ref[...].astype(F32)
        for s in range(NDEV - 1):
            g = g + l_ref[s].astype(F32)
        g_ref[...] = g
        d, m2, v2 = _adamw_math(w_ref[...], g, m_ref[...], v_ref[...])
        d_ref[...] = d
        mo_ref[...] = m2
        vo_ref[...] = v2

    wblk = pl.BlockSpec((rows, tc), lambda j, p: (0, j))
    grid_spec = pltpu.PrefetchScalarGridSpec(
        num_scalar_prefetch=1, grid=(cols // tc,),
        in_specs=[pl.BlockSpec((rows, tc), lambda j, p: (p[0], j)),
                  pl.BlockSpec((NDEV - 1, rows, tc), lambda j, p: (0, 0, j)), wblk, wblk, wblk],
        out_specs=[wblk] * 4)
    return pl.pallas_call(
        body, name=name, grid_spec=grid_spec, out_shape=[jax.ShapeDtypeStruct((rows, cols), F32)] * 4,
        compiler_params=_cp(("parallel",)),
    )(me, src, land, w, m, v)


def _small_update(vland, w8, m8, v8, name):
    def body(l_ref, w_ref, m_ref, v_ref, g_ref, d_ref, mo_ref, vo_ref):
        g = l_ref[0]
        for s in range(1, NDEV):
            g = g + l_ref[s]
        g_ref[...] = g
        d, m2, v2 = _adamw_math(w_ref[...], g, m_ref[...], v_ref[...])
        d_ref[...] = d
        mo_ref[...] = m2
        vo_ref[...] = v2

    return pl.pallas_call(
        body, name=name, out_shape=[jax.ShapeDtypeStruct((8, D), F32)] * 4,
        compiler_params=_cp(None),
    )(vland, w8, m8, v8)


def kernel(x, ffn1_norm, ffn1_w_gate, ffn1_w_up, ffn1_w_down, mix_norm, w_in, conv_dw_kernel, conv_dw_bias, conv_ln_gain, conv_ln_bias, conv_w_out, attn_w_out, w_o, ffn2_norm, ffn2_w_gate, ffn2_w_up, ffn2_w_down, final_norm, loss_target, m_ffn1_norm, m_ffn1_w_gate, m_ffn1_w_up, m_ffn1_w_down, m_mix_norm, m_w_in, m_conv_dw_kernel, m_conv_dw_bias, m_conv_ln_gain, m_conv_ln_bias, m_conv_w_out, m_attn_w_out, m_w_o, m_ffn2_norm, m_ffn2_w_gate, m_ffn2_w_up, m_ffn2_w_down, m_final_norm, v_ffn1_norm, v_ffn1_w_gate, v_ffn1_w_up, v_ffn1_w_down, v_mix_norm, v_w_in, v_conv_dw_kernel, v_conv_dw_bias, v_conv_ln_gain, v_conv_ln_bias, v_conv_w_out, v_attn_w_out, v_w_o, v_ffn2_norm, v_ffn2_w_gate, v_ffn2_w_up, v_ffn2_w_down, v_final_norm):
    names = ["ffn1_norm", "ffn1_w_gate", "ffn1_w_up", "ffn1_w_down", "mix_norm", "w_in", "conv_dw_kernel",
             "conv_dw_bias", "conv_ln_gain", "conv_ln_bias", "conv_w_out", "attn_w_out", "w_o", "ffn2_norm",
             "ffn2_w_gate", "ffn2_w_up", "ffn2_w_down", "final_norm"]
    w = dict(ffn1_norm=ffn1_norm, ffn1_w_gate=ffn1_w_gate, ffn1_w_up=ffn1_w_up, ffn1_w_down=ffn1_w_down, mix_norm=mix_norm, w_in=w_in, conv_dw_kernel=conv_dw_kernel, conv_dw_bias=conv_dw_bias, conv_ln_gain=conv_ln_gain, conv_ln_bias=conv_ln_bias, conv_w_out=conv_w_out, attn_w_out=attn_w_out, w_o=w_o, ffn2_norm=ffn2_norm, ffn2_w_gate=ffn2_w_gate, ffn2_w_up=ffn2_w_up, ffn2_w_down=ffn2_w_down, final_norm=final_norm)
    mo = dict(ffn1_norm=m_ffn1_norm, ffn1_w_gate=m_ffn1_w_gate, ffn1_w_up=m_ffn1_w_up, ffn1_w_down=m_ffn1_w_down, mix_norm=m_mix_norm, w_in=m_w_in, conv_dw_kernel=m_conv_dw_kernel, conv_dw_bias=m_conv_dw_bias, conv_ln_gain=m_conv_ln_gain, conv_ln_bias=m_conv_ln_bias, conv_w_out=m_conv_w_out, attn_w_out=m_attn_w_out, w_o=m_w_o, ffn2_norm=m_ffn2_norm, ffn2_w_gate=m_ffn2_w_gate, ffn2_w_up=m_ffn2_w_up, ffn2_w_down=m_ffn2_w_down, final_norm=m_final_norm)
    vo = dict(ffn1_norm=v_ffn1_norm, ffn1_w_gate=v_ffn1_w_gate, ffn1_w_up=v_ffn1_w_up, ffn1_w_down=v_ffn1_w_down, mix_norm=v_mix_norm, w_in=v_w_in, conv_dw_kernel=v_conv_dw_kernel, conv_dw_bias=v_conv_dw_bias, conv_ln_gain=v_conv_ln_gain, conv_ln_bias=v_conv_ln_bias, conv_w_out=v_conv_w_out, attn_w_out=v_attn_w_out, w_o=v_w_o, ffn2_norm=v_ffn2_norm, ffn2_w_gate=v_ffn2_w_gate, ffn2_w_up=v_ffn2_w_up, ffn2_w_down=v_ffn2_w_down, final_norm=v_final_norm)
    col_sharded = ("ffn1_w_gate", "ffn1_w_up", "w_in", "attn_w_out", "ffn2_w_gate", "ffn2_w_up")
    row_sharded = ("ffn1_w_down", "conv_w_out", "w_o", "ffn2_w_down")
    small = ("ffn1_norm", "mix_norm", "ffn2_norm", "final_norm", "conv_dw_bias", "conv_ln_gain", "conv_ln_bias")

    def landing_view(a, n):
        return jnp.transpose(a[0]) if n in col_sharded else a[0]

    def own_view(a, n):
        return jnp.transpose(a)[None] if n in col_sharded else a[None]

    ag_groups = (("ffn1_w_gate", "ffn1_w_up", "ffn1_w_down"),
                 ("w_in", "attn_w_out", "conv_w_out", "w_o", "conv_dw_kernel"),
                 ("ffn2_w_gate", "ffn2_w_up", "ffn2_w_down"))
    ag, order = [], []
    for gi, grp in enumerate(ag_groups):
        lands = _prep_gather([landing_view(w[n], n) for n in grp], order, f"gather_prep{gi}")
        st = _gather_start(lands, GATHER_DIRECT if gi == 2 else GATHER_A, [], f"gather_a_start{gi}")
        ag.append(st)
        order = [st[3]]

    def chips_in(gi, after):
        lands = _gather_wait(ag[gi], GATHER_A, after, f"gather_a_wait{gi}")
        return _gather_start(lands, GATHER_B, [], f"gather_b_start{gi}")

    def all_in(gi, st, after):
        return _gather_wait(st, GATHER_B, after, f"gather_b_wait{gi}")

    x0 = x[0]
    tgt = loss_target[0]
    gf = final_norm.reshape(1, D)

    wg1, wu1, wd1 = all_in(0, chips_in(0, [ag[2][3]]), [])
    x1, gg1, uu1, h2p = _ffn_fwd(x0, ffn1_norm, wg1, wu1, wd1, "ffn1_fwd", next_gain=mix_norm)
    h2 = h2p[0]
    win_t, wa_t, wc, wo, kern_blocks = all_in(1, chips_in(1, [x1]), [])
    kern = kern_blocks.reshape(NDEV, 32, D // NDEV).transpose(1, 0, 2).reshape(32, D)
    ptm = min(T, 2048)
    ab = _mm(h2, win_t, mode="nt", m=T, n=2 * D, k=D, tm=ptm, tn=512, tk=D, out_dtype=BF16, name="proj_conv")
    z1, z3b, gates = _conv_fwd(ab, kern, conv_dw_bias, conv_ln_gain, conv_ln_bias, "conv_fwd",
                               guest=(h2, win_t, (13, 14, 15, 16, 4, 7, 10), 512))
    qkv, qkv_col0 = [gates], [2 * D]
    for gi in range(1, len(GROUPS)):
        qkv.append(_mm(h2p[gi], win_t, mode="nt", m=T, n=3 * AW, k=D, tm=ptm, tn=AW, tk=D, out_dtype=BF16,
                       b_map=lambda i, j, kk, gi=gi: (4 + gi + 3 * j, 0), name=f"proj_qkv{gi}"))
        qkv_col0.append(0)
    outs, lses = [], []
    for gi, (_, dil) in enumerate(GROUPS):
        o, l = _attn_fwd(qkv[gi], gi, f"attn_fwd{gi}", col0=qkv_col0[gi])
        outs.append(o)
        lses.append(l)
    attnb, lse = _merge(outs, lses, "attn_merge")
    x2, yc, ya, mixedb = _mix_out(z3b, attnb, gates, wc, wa_t, wo, x1, "mix_out_fwd")
    wg2, wu2, wd2 = _gather_wait(ag[2], GATHER_DIRECT, [x2], "gather_a_wait2")
    gg2, uu2, dx3, dgf, loss_part = _ffn_fwd(x2, ffn2_norm, wg2, wu2, wd2, "ffn2_fwd", loss_of=(gf, tgt))

    dx2, dg3, dgb, dub, actb, hb, dob = _ffn_bwd(x2, ffn2_norm, gg2, uu2, dx3, wg2, wu2, wd2, "ffn2_bwd")
    grads = {}
    grads["ffn2_w_down"] = _wgrad(actb, dob, FF, D, "ffn2_dwd")
    rs_groups = [("ffn2_w_gate", "ffn2_w_up", "ffn2_w_down"),
                 ("attn_w_out", "conv_w_out", "w_o", "conv_dw_kernel"),
                 ("w_in",),
                 ("ffn1_w_gate",), ("ffn1_w_up",), ("ffn1_w_down",), ()]
    last = len(rs_groups) - 1
    rs = []

    dgates, dycb, dyab, dx2b, dz3, dh_gates, dattnb, delta = _mix_out_bwd(dx2, gates, yc, ya, attnb, wc, wa_t, wo,
                                                                          win_t, "mix_out_bwd")
    grads["w_o"] = _wgrad(mixedb, dx2b, D, D, "dw_o")
    grads["conv_w_out"] = _wgrad(z3b, dycb, D, D, "dw_conv_out")
    grads["attn_w_out"] = _wgrad(dyab, attnb, D, AW, "dw_attn_out")
    dab, dkern, dvec, grads["ffn2_w_gate"], grads["ffn2_w_up"] = _conv_bwd(
        dz3, z1, ab, kern, conv_ln_gain, conv_ln_bias, "conv_bwd", guest_lhs=(dgb, dub), guest_rhs=hb)
    grads["conv_dw_kernel"] = dkern.reshape(32, NDEV, D // NDEV).transpose(1, 0, 2).reshape(NDEV * 32, D // NDEV)
    rs.append(_send_start(["scatter"] * 3, [grads[n] for n in rs_groups[0]], [], "scatter_start0"))
    rs.append(_send_start(["scatter"] * 4, [grads[n] for n in rs_groups[1]], [rs[0][4]], "scatter_start1"))
    dattnb = [_tie(a, [rs[1][4]], f"tie_after_scatter1_{i}") for i, a in enumerate(dattnb)]

    dqkv, dq3s = [], []
    for gi, (_, dil) in enumerate(GROUPS):
        dq3 = _attn_bwd(qkv[gi], dattnb[gi], lse[gi], delta[gi], gi, f"attn_bwd{gi}", col0=qkv_col0[gi])
        dq3s.append(dq3)
        dqkv.append(dq3.reshape(3 * T, AW))

    wtk = min(T, 2048)
    dwin = _mm(dab, h2, mode="tn", m=2 * D, n=D, k=T, tm=2 * D, tn=D, tk=wtk, out_dtype=BF16, out_rows=IN_W,
               name="dw_in_conv")
    dwin = _mm(dgates, h2, mode="tn", m=2 * D, n=D, k=T, tm=512, tn=D, tk=wtk, out_dtype=BF16, out_rows=IN_W,
               o_map=lambda i, j, kk: (13 + i, 0), passthru=dwin, name="dw_in_gates")
    for gi in range(3):
        dwin = _mm(dqkv[gi], h2p[gi], mode="tn", m=3 * AW, n=D, k=T, tm=AW, tn=D, tk=wtk, out_dtype=BF16,
                   out_rows=IN_W, a_map=lambda i, j, kk: (i * (T // wtk) + kk, 0),
                   o_map=lambda i, j, kk, gi=gi: (4 + gi + 3 * i, 0), passthru=dwin, name=f"dw_in_qkv{gi}")
    grads["w_in"] = dwin
    rs.append(_send_start(["scatter"], [dwin], [rs[1][4]], "scatter_start2"))
    dab = _tie(dab, [rs[2][4]], "tie_after_scatter2")

    nrow = T // 1024
    dh = _mm(dab, win_t, mode="nn", m=T, n=D, k=2 * D, tm=1024, tn=D, tk=2 * D, out_dtype=F32, init=dh_gates,
             name="dproj_conv")
    dx1, dg2 = _rms_bwd(x1, mix_norm, dh, dq3s, win_t, dx2, "mix_norm_bwd")

    dgb, dub, actb, hb, dob = _ffn_bwd_pre(x0, ffn1_norm, gg1, uu1, dx1, wd1, "ffn1_bwd_pre")
    grads["ffn1_w_gate"] = _wgrad(dgb, hb, FF, D, "ffn1_dwg")
    rs.append(_send_start(["scatter"], [grads["ffn1_w_gate"]], [rs[2][4]], "scatter_start3"))
    hb = _tie(hb, [rs[3][4]], "tie_after_scatter3")
    grads["ffn1_w_up"] = _wgrad(dub, hb, FF, D, "ffn1_dwu")
    rs.append(_send_start(["scatter"], [grads["ffn1_w_up"]], [rs[3][4]], "scatter_start4"))
    dob = _tie(dob, [rs[4][4]], "tie_after_scatter4")
    grads["ffn1_w_down"] = _wgrad(actb, dob, FF, D, "ffn1_dwd")
    rs.append(_send_start(["scatter"], [grads["ffn1_w_down"]], [rs[4][4]], "scatter_start5"))
    dgb = _tie(dgb, [rs[5][4]], "tie_after_scatter5")
    dx0, dg1 = _ffn_bwd_dx(x0, ffn1_norm, dgb, dub, dx1, wg1, wu1, "ffn1_bwd_dx")
    vec = jnp.concatenate([dg1, dg2, dg3, dgf, dvec[0:3], jnp.broadcast_to(loss_part[:, :1], (1, D))], axis=0)
    rs.append(_send_start(["bcast"], [vec], [rs[5][4]], "scatter_start6"))

    g_out, d_out, m_out, v_out = {}, {}, {}, {}
    me = _my_place()
    after, done = [rs[last][4]], []
    for gi, grp in enumerate(rs_groups):
        kinds = ["scatter"] * len(grp) + (["bcast"] if gi == last else [])
        srcs, lands = _send_wait(kinds, rs[gi], after + (done if gi >= last - 1 else []), f"scatter_wait{gi}")
        for n, src, land in zip(grp, srcs, lands):
            if n == "conv_dw_kernel":
                rows = src.shape[0] // NDEV
                own = lax.dynamic_slice(src, (me * rows, 0), (rows, src.shape[1]))
                g = _gsum(own, land, f"gsum_{n}")[:CONV_W]
                d, m2, v2 = _adamw(w[n][0], g, mo[n][0], vo[n][0], f"adamw_{n}")
                after = [d]
                done.append(d)
                g, d, m2, v2 = g[None], d[None], m2[None], v2[None]
            else:
                res = _update(src, land, landing_view(w[n], n), landing_view(mo[n], n), landing_view(vo[n], n),
                              f"update_{n}")
                after = [res[1]]
                done.append(res[1])
                g, d, m2, v2 = (own_view(a, n) for a in res)
            g_out[n], d_out[n], m_out[n], v_out[n] = g, d, m2, v2
    vland = lands[-1]

    def rows8(src):
        return jnp.concatenate([src[n].reshape(1, D) for n in small] + [jnp.ones((1, D), F32)], axis=0)

    g8, d8, m8, v8 = _small_update(vland, rows8(w), rows8(mo), rows8(vo), "small_update")
    for r, n in enumerate(small):
        shp = w[n].shape
        g_out[n], d_out[n], m_out[n], v_out[n] = (a[r].reshape(shp) for a in (g8, d8, m8, v8))
    loss = g8[7, 0]

    return (loss, dx0[None], *[g_out[n] for n in names], *[d_out[n] for n in names],
            *[m_out[n] for n in names], *[v_out[n] for n in names])
```

```python
import numpy as np
import jax
import jax.numpy as jnp
from jax import lax
from jax.experimental import pallas as pl
from jax.experimental.pallas import tpu as pltpu

F32 = jnp.float32
BF16 = jnp.bfloat16

T = 4096
D = 1024
FF = 2816
NDEV = 8
CONV_W = 31
HEAD = 128
BLK = 128
GROUPS = ((128, 1), (512, 4), (2048, 16))
NHG = 4
AW = NHG * HEAD
IN_W = 2 * D + 3 * 3 * AW + 2 * D
EPS = 1e-6
B1, B2, LR, AEPS, WD, STEP = 0.9, 0.999, 0.001, 1e-08, 0.01, 10
NEG = -1e30
VMEM_LIMIT = 56 * 1024 * 1024
MESH_ID = pl.DeviceIdType.MESH

NT = (((1,), (1,)), ((), ()))
NN = (((1,), (0,)), ((), ()))
TN = (((0,), (0,)), ((), ()))
_DIMS = {"nn": NN, "nt": NT, "tn": TN}


def _cp(sem=None):
    return pltpu.CompilerParams(dimension_semantics=sem, vmem_limit_bytes=VMEM_LIMIT)


def _sig(v):
    return 1.0 / (1.0 + jnp.exp(-v))


def _dot(a, b, dims):
    return lax.dot_general(a, b, dims, preferred_element_type=F32)


def _const_spec(shape):
    nd = len(shape)
    return pl.BlockSpec(shape, lambda *_: (0,) * nd)


def _mm(a, b, *, mode, m, n, k, tm, tn, tk, out_dtype, name, a_map=None, b_map=None,
        o_map=None, out_rows=None, init=None, passthru=None):
    gi, gj, gk = m // tm, n // tn, k // tk
    assert gi * tm == m and gj * tn == n and gk * tk == k, (name, m, n, k, tm, tn, tk)
    if mode == "nn":
        a_blk, b_blk = (tm, tk), (tk, tn)
        da, db = (lambda i, j, kk: (i, kk)), (lambda i, j, kk: (kk, j))
    elif mode == "nt":
        a_blk, b_blk = (tm, tk), (tn, tk)
        da, db = (lambda i, j, kk: (i, kk)), (lambda i, j, kk: (j, kk))
    else:
        a_blk, b_blk = (tk, tm), (tk, tn)
        da, db = (lambda i, j, kk: (kk, i)), (lambda i, j, kk: (kk, j))
    a_map = a_map or da
    b_map = b_map or db
    o_map = o_map or (lambda i, j, kk: (i, j))
    dims = _DIMS[mode]
    extra = init if init is not None else passthru
    out_rows = out_rows or m

    def body(*refs):
        if init is not None:
            a_ref, b_ref, i_ref, o_ref = refs[:4]
        elif passthru is not None:
            a_ref, b_ref, _, o_ref = refs[:4]
        else:
            a_ref, b_ref, o_ref = refs[:3]
        if gk == 1:
            prod = _dot(a_ref[...], b_ref[...], dims)
            if init is not None:
                prod = prod + i_ref[...].astype(F32)
            o_ref[...] = prod.astype(out_dtype)
            return
        acc = refs[-1]
        kk = pl.program_id(2)

        @pl.when(kk == 0)
        def _():
            if init is not None:
                acc[...] = i_ref[...].astype(F32)
            else:
                acc[...] = jnp.zeros_like(acc)

        acc[...] += _dot(a_ref[...], b_ref[...], dims)

        @pl.when(kk == gk - 1)
        def _():
            o_ref[...] = acc[...].astype(out_dtype)

    in_specs = [pl.BlockSpec(a_blk, a_map), pl.BlockSpec(b_blk, b_map)]
    args = [a, b]
    aliases = {}
    if init is not None:
        in_specs.append(pl.BlockSpec((tm, tn), o_map))
        args.append(init)
        aliases = {2: 0}
    elif passthru is not None:
        in_specs.append(pl.BlockSpec(memory_space=pl.ANY))
        args.append(passthru)
        aliases = {2: 0}
    out_dt = extra.dtype if extra is not None else out_dtype
    assert out_dt == out_dtype
    return pl.pallas_call(
        body, name=name, grid=(gi, gj, gk),
        in_specs=in_specs, out_specs=pl.BlockSpec((tm, tn), o_map),
        out_shape=jax.ShapeDtypeStruct((out_rows, n), out_dtype),
        scratch_shapes=[pltpu.VMEM((tm, tn), F32)] if gk > 1 else [],
        input_output_aliases=aliases,
        compiler_params=_cp(("parallel", "parallel", "arbitrary")),
    )(*args)


def _ffn_fwd(x, g, wg_t, wu_t, wd, name, next_gain=None, loss_of=None):
    tm, fc = PERM_TM, 256
    nc = FF // fc
    n_in = 5 + (1 if next_gain is not None else 0) + (2 if loss_of is not None else 0)

    def body(*refs):
        x_ref, g_ref, wg_ref, wu_ref, wd_ref = refs[:5]
        extra_in, outs = refs[5:n_in], refs[n_in:]
        act_ref = outs[-1]
        xv = x_ref[...]
        r = lax.rsqrt(jnp.mean(xv * xv, axis=-1, keepdims=True) + EPS)
        h = (xv * r * g_ref[...]).astype(BF16)
        gg_ref, uu_ref = (outs[0], outs[1]) if loss_of is not None else (outs[1], outs[2])
        for c in range(nc):
            sl = pl.ds(c * fc, fc)
            gg = _dot(h, wg_ref[sl, :], NT)
            uu = _dot(h, wu_ref[sl, :], NT)
            gg_ref[:, sl] = gg.astype(BF16)
            uu_ref[:, sl] = uu.astype(BF16)
            act_ref[:, sl] = (gg * _sig(gg) * uu).astype(BF16)
        y = xv + 0.5 * _dot(act_ref[...], wd_ref[...], NN)
        if loss_of is not None:
            _final_math(y, extra_in[0][...], extra_in[1][...], outs[2], outs[3], outs[4], pl.program_id(0))
            return
        outs[0][...] = y
        if next_gain is not None:
            tile = outs[-2]
            r2 = lax.rsqrt(jnp.mean(y * y, axis=-1, keepdims=True) + EPS)
            hv = y * r2 * extra_in[0][...]
            outs[3][...] = hv.astype(BF16)
            _put_tile(tile, hv)
            for dil, p_ref in zip(DILS, outs[4:4 + len(DILS)]):
                _store_perm(p_ref, tile, dil)

    wspec = pl.BlockSpec((FF, D), lambda i: (0, 0), pipeline_mode=pl.Buffered(1))
    row_d = pl.BlockSpec((tm, D), lambda i: (i, 0))
    row_f = pl.BlockSpec((tm, FF), lambda i: (i, 0))
    in_specs = [row_d, _const_spec((1, D)), wspec, wspec, wspec]
    args = [x, g, wg_t, wu_t, wd]
    f_shape = jax.ShapeDtypeStruct((T, FF), BF16)
    scratch = [pltpu.VMEM((tm, FF), BF16)]
    if loss_of is not None:
        in_specs += [_const_spec((1, D)), row_d]
        args += list(loss_of)
        out_specs = [row_f, row_f, row_d, _const_spec((1, D)), _const_spec((1, 128))]
        out_shape = [f_shape, f_shape, jax.ShapeDtypeStruct((T, D), F32), jax.ShapeDtypeStruct((1, D), F32),
                     jax.ShapeDtypeStruct((1, 128), F32)]
    else:
        out_specs = [row_d, row_f, row_f]
        out_shape = [jax.ShapeDtypeStruct((T, D), F32), f_shape, f_shape]
        if next_gain is not None:
            in_specs.append(_const_spec((1, D)))
            args.append(next_gain)
            out_specs += [row_d] + [_perm_spec(d, D) for d in DILS]
            out_shape += [jax.ShapeDtypeStruct((T, D), BF16)] + [_perm_shape(d, D, BF16) for d in DILS]
            scratch = [_tile_scratch(D)] + scratch
    out = pl.pallas_call(
        body, name=name, grid=(T // tm,), in_specs=in_specs, out_specs=out_specs, out_shape=out_shape,
        scratch_shapes=scratch,
        compiler_params=_cp(("arbitrary",) if loss_of is not None else ("parallel",)),
    )(*args)
    if next_gain is not None:
        return out[0], out[1], out[2], [out[3]] + [o.reshape(T, D) for o in out[4:]]
    return tuple(out)


def _ffn_bwd(x, g, gg_all, uu_all, dout, wg_t, wu_t, wd, name):
    tm, fc = 256, 256
    nc = FF // fc

    def body(x_ref, g_ref, gg_ref, uu_ref, do_ref, wg_ref, wu_ref, wd_ref,
             dx_ref, dgam_ref, dg_ref, du_ref, act_ref, h_ref, db_ref):
        i = pl.program_id(0)
        xv = x_ref[...]
        r = lax.rsqrt(jnp.mean(xv * xv, axis=-1, keepdims=True) + EPS)
        xhat = xv * r
        gam = g_ref[...]
        h_ref[...] = (xhat * gam).astype(BF16)
        dov = do_ref[...]
        dbv = (0.5 * dov).astype(BF16)
        db_ref[...] = dbv
        for c in range(nc):
            sl = pl.ds(c * fc, fc)
            da = _dot(dbv, wd_ref[sl, :], NT)
            gg = gg_ref[:, sl].astype(F32)
            uu = uu_ref[:, sl].astype(F32)
            s = _sig(gg)
            si = gg * s
            dgv = (da * uu * (s * (1.0 + gg * (1.0 - s)))).astype(BF16)
            duv = (da * si).astype(BF16)
            dg_ref[:, sl] = dgv
            du_ref[:, sl] = duv
            act_ref[:, sl] = (si * uu).astype(BF16)
        dh = _dot(dg_ref[...], wg_ref[...], NN) + _dot(du_ref[...], wu_ref[...], NN)

        @pl.when(i == 0)
        def _():
            dgam_ref[...] = jnp.zeros_like(dgam_ref)

        dgam_ref[...] += jnp.sum(dh * xhat, axis=0, keepdims=True)
        dxh = dh * gam
        dx_ref[...] = dov + r * (dxh - xhat * jnp.mean(dxh * xhat, axis=-1, keepdims=True))

    wspec = pl.BlockSpec((FF, D), lambda i: (0, 0), pipeline_mode=pl.Buffered(1))
    row_d = pl.BlockSpec((tm, D), lambda i: (i, 0))
    row_f = pl.BlockSpec((tm, FF), lambda i: (i, 0))
    return pl.pallas_call(
        body, name=name, grid=(T // tm,),
        in_specs=[row_d, _const_spec((1, D)), row_f, row_f, row_d, wspec, wspec, wspec],
        out_specs=[row_d, _const_spec((1, D)), row_f, row_f, row_f, row_d, row_d],
        out_shape=[jax.ShapeDtypeStruct((T, D), F32), jax.ShapeDtypeStruct((1, D), F32),
                   jax.ShapeDtypeStruct((T, FF), BF16), jax.ShapeDtypeStruct((T, FF), BF16),
                   jax.ShapeDtypeStruct((T, FF), BF16), jax.ShapeDtypeStruct((T, D), BF16),
                   jax.ShapeDtypeStruct((T, D), BF16)],
        compiler_params=_cp(("arbitrary",)),
    )(x, g, gg_all, uu_all, dout, wg_t, wu_t, wd)


def _ffn_bwd_pre(x, g, gg_all, uu_all, dout, wd, name):
    tm, fc = 512, 256
    nc = FF // fc

    def body(x_ref, g_ref, gg_ref, uu_ref, do_ref, wd_ref, dg_ref, du_ref, act_ref, h_ref, db_ref):
        xv = x_ref[...]
        r = lax.rsqrt(jnp.mean(xv * xv, axis=-1, keepdims=True) + EPS)
        h_ref[...] = (xv * r * g_ref[...]).astype(BF16)
        dbv = (0.5 * do_ref[...]).astype(BF16)
        db_ref[...] = dbv
        for c in range(nc):
            sl = pl.ds(c * fc, fc)
            da = _dot(dbv, wd_ref[sl, :], NT)
            gg = gg_ref[:, sl].astype(F32)
            uu = uu_ref[:, sl].astype(F32)
            s = _sig(gg)
            si = gg * s
            dg_ref[:, sl] = (da * uu * (s * (1.0 + gg * (1.0 - s)))).astype(BF16)
            du_ref[:, sl] = (da * si).astype(BF16)
            act_ref[:, sl] = (si * uu).astype(BF16)

    wspec = pl.BlockSpec((FF, D), lambda i: (0, 0), pipeline_mode=pl.Buffered(1))
    row_d = pl.BlockSpec((tm, D), lambda i: (i, 0))
    row_f = pl.BlockSpec((tm, FF), lambda i: (i, 0))
    return pl.pallas_call(
        body, name=name, grid=(T // tm,),
        in_specs=[row_d, _const_spec((1, D)), row_f, row_f, row_d, wspec],
        out_specs=[row_f, row_f, row_f, row_d, row_d],
        out_shape=[jax.ShapeDtypeStruct((T, FF), BF16), jax.ShapeDtypeStruct((T, FF), BF16),
                   jax.ShapeDtypeStruct((T, FF), BF16), jax.ShapeDtypeStruct((T, D), BF16),
                   jax.ShapeDtypeStruct((T, D), BF16)],
        compiler_params=_cp(("parallel",)),
    )(x, g, gg_all, uu_all, dout, wd)


def _ffn_bwd_dx(x, g, dgb, dub, dout, wg_t, wu_t, name):
    tm = 512

    def body(x_ref, g_ref, dg_ref, du_ref, do_ref, wg_ref, wu_ref, dx_ref, dgam_ref):
        i = pl.program_id(0)
        xv = x_ref[...]
        r = lax.rsqrt(jnp.mean(xv * xv, axis=-1, keepdims=True) + EPS)
        xhat = xv * r
        gam = g_ref[...]
        dh = _dot(dg_ref[...], wg_ref[...], NN) + _dot(du_ref[...], wu_ref[...], NN)

        @pl.when(i == 0)
        def _():
            dgam_ref[...] = jnp.zeros_like(dgam_ref)

        dgam_ref[...] += jnp.sum(dh * xhat, axis=0, keepdims=True)
        dxh = dh * gam
        dx_ref[...] = do_ref[...] + r * (dxh - xhat * jnp.mean(dxh * xhat, axis=-1, keepdims=True))

    wspec = pl.BlockSpec((FF, D), lambda i: (0, 0), pipeline_mode=pl.Buffered(1))
    row_d = pl.BlockSpec((tm, D), lambda i: (i, 0))
    row_f = pl.BlockSpec((tm, FF), lambda i: (i, 0))
    return pl.pallas_call(
        body, name=name, grid=(T // tm,),
        in_specs=[row_d, _const_spec((1, D)), row_f, row_f, row_d, wspec, wspec],
        out_specs=[row_d, _const_spec((1, D))],
        out_shape=[jax.ShapeDtypeStruct((T, D), F32), jax.ShapeDtypeStruct((1, D), F32)],
        compiler_params=_cp(("arbitrary",)),
    )(x, g, dgb, dub, dout, wg_t, wu_t)


def _wgrad(a, b, m, n, name):
    tm = m // 2 if m == FF else m
    return _mm(a, b, mode="tn", m=m, n=n, k=T, tm=tm, tn=n, tk=min(T, 2048), out_dtype=BF16, name=name)


PERM_TM = 512
DILS = tuple(d for _, d in GROUPS if d > 1)


def _perm_spec(dil, cols):
    return pl.BlockSpec((dil, PERM_TM // dil, cols), lambda i: (0, i, 0))


def _perm_shape(dil, cols, dtype):
    return jax.ShapeDtypeStruct((dil, T // dil, cols), dtype)


LANES = 128


def _tile_scratch(cols):
    return pltpu.VMEM((cols // LANES, PERM_TM, LANES), F32)


def _put_tile(tile, value):
    for c in range(tile.shape[0]):
        tile[c] = value[:, c * LANES:(c + 1) * LANES]


def _get_tile(tile):
    return jnp.concatenate([tile[c] for c in range(tile.shape[0])], axis=1)


def _store_perm(out_ref, tile, dil):
    for r in range(dil):
        for c in range(tile.shape[0]):
            out_ref[r, :, pl.ds(c * LANES, LANES)] = tile[c, pl.ds(r, PERM_TM // dil, stride=dil), :].astype(
                out_ref.dtype)


def _load_unperm(in_ref, tile, dil):
    for r in range(dil):
        for c in range(tile.shape[0]):
            tile[c, pl.ds(r, PERM_TM // dil, stride=dil), :] = in_ref[r, :, pl.ds(c * LANES, LANES)].astype(F32)


def _final_math(xv, gam, tgt, dx_ref, dgam_ref, loss_ref, i):
    r = lax.rsqrt(jnp.mean(xv * xv, axis=-1, keepdims=True) + EPS)
    xhat = xv * r
    err = xhat * gam - tgt
    part = 0.5 * jnp.sum(jnp.mean(err * err, axis=-1, keepdims=True), axis=0, keepdims=True)
    dy = err * (1.0 / D)

    @pl.when(i == 0)
    def _():
        dgam_ref[...] = jnp.zeros_like(dgam_ref)
        loss_ref[...] = jnp.zeros_like(loss_ref)

    dgam_ref[...] += jnp.sum(dy * xhat, axis=0, keepdims=True)
    loss_ref[...] += jnp.broadcast_to(part, loss_ref.shape)
    dxh = dy * gam
    dx_ref[...] = r * (dxh - xhat * jnp.mean(dxh * xhat, axis=-1, keepdims=True))


QKV_BLOCK0 = 2 * D // AW


def _rms_bwd(x, g, dh0, dlin, lin_blocks, dqkvs, win_t, dres, name):
    tm = PERM_TM
    dils = [d for _, d in GROUPS]
    ng = len(dils)
    nl = len(lin_blocks)
    assert len(dqkvs) == ng

    def body(*refs):
        x_ref, g_ref, dh0_ref, dl_ref = refs[:4]
        dq_refs = refs[4:4 + 3 * ng]
        w_refs = refs[4 + 3 * ng:4 + 6 * ng]
        wl_refs = refs[4 + 6 * ng:4 + 6 * ng + nl]
        dr_ref, dx_ref, dgam_ref = refs[4 + 6 * ng + nl:7 + 6 * ng + nl]
        tile = refs[7 + 6 * ng + nl]
        stages = refs[8 + 6 * ng + nl:]
        i = pl.program_id(0)
        xv = x_ref[...]
        r = lax.rsqrt(jnp.mean(xv * xv, axis=-1, keepdims=True) + EPS)
        xhat = xv * r
        gam = g_ref[...]
        dh = dh0_ref[...]
        for q in range(nl):
            dh = dh + _dot(dl_ref[:, pl.ds(q * AW, AW)], wl_refs[q][...], NN)
        si = 0
        for gi, dil in enumerate(dils):
            part = None
            for p in range(3):
                blk = dq_refs[3 * gi + p][...]
                term = _dot(blk.reshape(tm, AW), w_refs[3 * gi + p][...], NN)
                part = term if part is None else part + term
            if dil > 1:
                stage = stages[si]
                si += 1
                stage[...] = part.reshape(dil, tm // dil, D)
                _load_unperm(stage, tile, dil)
                part = _get_tile(tile)
            dh = dh + part

        @pl.when(i == 0)
        def _():
            dgam_ref[...] = jnp.zeros_like(dgam_ref)

        dgam_ref[...] += jnp.sum(dh * xhat, axis=0, keepdims=True)
        dxh = dh * gam
        dx_ref[...] = dr_ref[...] + r * (dxh - xhat * jnp.mean(dxh * xhat, axis=-1, keepdims=True))

    row_d = pl.BlockSpec((tm, D), lambda i: (i, 0))
    dq_specs, dq_args, w_specs = [], [], []
    for gi, (d, a) in enumerate(zip(dils, dqkvs)):
        for p in range(3):
            if d == 1:
                dq_specs.append(pl.BlockSpec((None, tm, AW), lambda i, p=p: (p, i, 0)))
                dq_args.append(a)
            else:
                dq_specs.append(pl.BlockSpec((None, d, tm // d, AW), lambda i, p=p: (p, 0, i, 0)))
                dq_args.append(a.reshape(3, d, T // d, AW))
            w_specs.append(pl.BlockSpec((AW, D), lambda i, q=QKV_BLOCK0 + gi + 3 * p: (q, 0),
                                        pipeline_mode=pl.Buffered(1)))
    return pl.pallas_call(
        body, name=name, grid=(T // tm,),
        in_specs=[row_d, _const_spec((1, D)), row_d, pl.BlockSpec((tm, nl * AW), lambda i: (i, 0))] + dq_specs + w_specs
        + [pl.BlockSpec((AW, D), lambda i, q=q: (q, 0), pipeline_mode=pl.Buffered(1)) for q in lin_blocks] + [row_d],
        out_specs=[row_d, _const_spec((1, D))],
        out_shape=[jax.ShapeDtypeStruct((T, D), F32), jax.ShapeDtypeStruct((1, D), F32)],
        scratch_shapes=[_tile_scratch(D)] + [pltpu.VMEM((d, tm // d, D), F32) for d in dils if d > 1],
        compiler_params=_cp(("arbitrary",)),
    )(x, g, dh0, dlin, *dq_args, *([win_t] * (3 * ng + nl)), dres)


CONV_TM = 256
CONV_HALO = 32
CONV_RB = 16


def _glu(ab):
    ab = ab.astype(F32)
    return ab[:, :D] * _sig(ab[:, D:])


def _ln_stats(z1):
    mu = jnp.mean(z1, axis=-1, keepdims=True)
    zc = z1 - mu
    rstd = lax.rsqrt(jnp.mean(zc * zc, axis=-1, keepdims=True) + EPS)
    return zc * rstd, rstd


def _fill_shifts(zs):
    n = zs.shape[1] - 8
    for s in range(1, 8):
        zs[s, pl.ds(0, n), :] = zs[0, pl.ds(s, n), :]


def _shifted(zs, start, rows):
    q, s = divmod(start, 8)
    return zs[s, pl.ds(8 * q, rows), :]


def _conv_fwd(ab, kern, dwb, lng, lnb, name, guest=None):
    tm, hl, rb = CONV_TM, CONV_HALO, CONV_RB
    off = hl - (CONV_W - 1)
    if guest is not None:
        g_a, g_b, g_blocks, g_rows = guest
        g_nblk = len(g_blocks)

    def body(ab_ref, abh_ref, k_ref, dwb_ref, lng_ref, lnb_ref, *rest):
        if guest is not None:
            ga_ref, gb_refs, rest = rest[0], rest[1:1 + g_nblk], rest[1 + g_nblk:]
            z1_ref, z3_ref, go_ref, zs = rest
            for q, gb_ref in enumerate(gb_refs):
                go_ref[:, pl.ds(q * g_rows, g_rows)] = _dot(ga_ref[...], gb_ref[...], NT).astype(BF16)
        else:
            z1_ref, z3_ref, zs = rest
        i = pl.program_id(0)
        zs[0, pl.ds(0, hl), :] = jnp.where(i > 0, _glu(abh_ref[...]), 0.0)
        zs[0, pl.ds(hl, tm), :] = _glu(ab_ref[...])
        _fill_shifts(zs)
        for b in range(tm // rb):
            acc = jnp.zeros((rb, D), F32)
            for j in range(CONV_W):
                acc = acc + _shifted(zs, b * rb + off + j, rb) * k_ref[pl.ds(j, 1), :]
            z1 = acc + dwb_ref[...]
            z1_ref[pl.ds(b * rb, rb), :] = z1
            zn, _ = _ln_stats(z1)
            z2 = zn * lng_ref[...] + lnb_ref[...]
            z3_ref[pl.ds(b * rb, rb), :] = (z2 * _sig(z2)).astype(BF16)

    row = pl.BlockSpec((tm, D), lambda i: (i, 0))
    g_specs, g_args, g_ospecs, g_oshapes = [], [], [], []
    if guest is not None:
        kdim = g_a.shape[1]
        g_specs = [pl.BlockSpec((tm, kdim), lambda i: (i, 0))]
        g_specs += [pl.BlockSpec((g_rows, kdim), lambda i, q=q: (q, 0), pipeline_mode=pl.Buffered(1))
                    for q in g_blocks]
        g_args = [g_a] + [g_b] * g_nblk
        g_ospecs = [pl.BlockSpec((tm, g_nblk * g_rows), lambda i: (i, 0))]
        g_oshapes = [jax.ShapeDtypeStruct((T, g_nblk * g_rows), BF16)]
    return pl.pallas_call(
        body, name=name, grid=(T // tm,),
        in_specs=[pl.BlockSpec((tm, 2 * D), lambda i: (i, 0)),
                  pl.BlockSpec((hl, 2 * D), lambda i: (jnp.maximum(i * (tm // hl) - 1, 0), 0)),
                  _const_spec((32, D)), _const_spec((1, D)), _const_spec((1, D)), _const_spec((1, D))] + g_specs,
        out_specs=[row, row] + g_ospecs,
        out_shape=[jax.ShapeDtypeStruct((T, D), F32), jax.ShapeDtypeStruct((T, D), BF16)] + g_oshapes,
        scratch_shapes=[pltpu.VMEM((8, hl + tm, D), F32)],
        compiler_params=_cp(("parallel",)),
    )(ab, ab, kern, dwb, lng, lnb, *g_args)


GUEST_TM = 256


def _conv_bwd(dz3, z1, ab, kern, lng, lnb, name, guest_lhs=(), guest_rhs=None):
    tm, hl, rb = CONV_TM, CONV_HALO, CONV_RB
    off = hl - (CONV_W - 1)
    nsteps = T // tm
    ng = len(guest_lhs)
    gblocks = [a.shape[1] // GUEST_TM for a in guest_lhs]
    assert all(gb <= nsteps and gb * GUEST_TM == a.shape[1] for gb, a in zip(gblocks, guest_lhs))

    def ln_bwd(dz3v, z1v, lngv, lnbv):
        zn, rstd = _ln_stats(z1v)
        z2 = zn * lngv + lnbv
        s = _sig(z2)
        dz2 = dz3v * (s * (1.0 + z2 * (1.0 - s)))
        dzn = dz2 * lngv
        dz1 = rstd * (dzn - jnp.mean(dzn, axis=-1, keepdims=True)
                      - zn * jnp.mean(dzn * zn, axis=-1, keepdims=True))
        return dz1, dz2, zn

    def body(dz3_ref, dz3h_ref, z1_ref, z1h_ref, ab_ref, abh_ref, k_ref, lng_ref, lnb_ref, *rest):
        g_in, rest = rest[:ng + (1 if ng else 0)], rest[ng + (1 if ng else 0):]
        dab_ref, dk_ref, dvec_ref = rest[:3]
        g_out, (zs, dzs) = rest[3:3 + ng], rest[3 + ng:]
        i = pl.program_id(0)
        lngv, lnbv = lng_ref[...], lnb_ref[...]

        for a_ref, o_ref, gb in zip(g_in[:ng], g_out, gblocks):
            @pl.when(i < gb)
            def _(a_ref=a_ref, o_ref=o_ref):
                o_ref[...] = _dot(a_ref[...], g_in[ng][...], TN).astype(BF16)

        @pl.when(i == 0)
        def _():
            dk_ref[...] = jnp.zeros_like(dk_ref)
            dvec_ref[...] = jnp.zeros_like(dvec_ref)

        dz1, dz2, zn = ln_bwd(dz3_ref[...].astype(F32), z1_ref[...], lngv, lnbv)
        dvec_ref[pl.ds(0, 1), :] += jnp.sum(dz1, axis=0, keepdims=True)
        dvec_ref[pl.ds(1, 1), :] += jnp.sum(dz2 * zn, axis=0, keepdims=True)
        dvec_ref[pl.ds(2, 1), :] += jnp.sum(dz2, axis=0, keepdims=True)
        dzs[0, pl.ds(0, tm), :] = dz1
        dz1h, _, _ = ln_bwd(dz3h_ref[...].astype(F32), z1h_ref[...], lngv, lnbv)
        dzs[0, pl.ds(tm, hl), :] = jnp.where(i < nsteps - 1, dz1h, 0.0)
        _fill_shifts(dzs)
        zs[0, pl.ds(0, hl), :] = jnp.where(i > 0, _glu(abh_ref[...]), 0.0)
        zs[0, pl.ds(hl, tm), :] = _glu(ab_ref[...])
        _fill_shifts(zs)

        for j in range(CONV_W):
            tot = jnp.zeros((rb, D), F32)
            for b in range(tm // rb):
                tot = tot + dzs[0, pl.ds(b * rb, rb), :] * _shifted(zs, b * rb + off + j, rb)
            dk_ref[pl.ds(j, 1), :] += jnp.sum(tot, axis=0, keepdims=True)

        for b in range(tm // rb):
            acc = jnp.zeros((rb, D), F32)
            for j in range(CONV_W):
                acc = acc + _shifted(dzs, b * rb + (CONV_W - 1) - j, rb) * k_ref[pl.ds(j, 1), :]
            av = ab_ref[pl.ds(b * rb, rb), pl.ds(0, D)].astype(F32)
            sb = _sig(ab_ref[pl.ds(b * rb, rb), pl.ds(D, D)].astype(F32))
            dab_ref[pl.ds(b * rb, rb), pl.ds(0, D)] = (acc * sb).astype(BF16)
            dab_ref[pl.ds(b * rb, rb), pl.ds(D, D)] = (acc * av * sb * (1.0 - sb)).astype(BF16)

    row = pl.BlockSpec((tm, D), lambda i: (i, 0))
    nxt = pl.BlockSpec((hl, D), lambda i: (jnp.minimum((i + 1) * (tm // hl), T // hl - 1), 0))
    g_specs, g_args, g_ospecs, g_oshapes = [], [], [], []
    for a, gb in zip(guest_lhs, gblocks):
        g_specs.append(pl.BlockSpec((T, GUEST_TM), lambda i, gb=gb: (0, jnp.minimum(i, gb - 1))))
        g_args.append(a)
        g_ospecs.append(pl.BlockSpec((GUEST_TM, guest_rhs.shape[1]), lambda i, gb=gb: (jnp.minimum(i, gb - 1), 0)))
        g_oshapes.append(jax.ShapeDtypeStruct((a.shape[1], guest_rhs.shape[1]), BF16))
    if ng:
        g_specs.append(pl.BlockSpec(guest_rhs.shape, lambda i: (0, 0), pipeline_mode=pl.Buffered(1)))
        g_args.append(guest_rhs)
    return pl.pallas_call(
        body, name=name, grid=(nsteps,),
        in_specs=[row, nxt, row, nxt,
                  pl.BlockSpec((tm, 2 * D), lambda i: (i, 0)),
                  pl.BlockSpec((hl, 2 * D), lambda i: (jnp.maximum(i * (tm // hl) - 1, 0), 0)),
                  _const_spec((32, D)), _const_spec((1, D)), _const_spec((1, D))] + g_specs,
        out_specs=[pl.BlockSpec((tm, 2 * D), lambda i: (i, 0)), _const_spec((32, D)), _const_spec((8, D))]
        + g_ospecs,
        out_shape=[jax.ShapeDtypeStruct((T, 2 * D), BF16), jax.ShapeDtypeStruct((32, D), F32),
                   jax.ShapeDtypeStruct((8, D), F32)] + g_oshapes,
        scratch_shapes=[pltpu.VMEM((8, hl + tm, D), F32), pltpu.VMEM((8, tm + hl, D), F32)],
        compiler_params=_cp(("arbitrary",)),
    )(dz3, dz3, z1, z1, ab, ab, kern, lng, lnb, *g_args)


def _alibi_slopes():
    h = np.arange(1, 3 * NHG + 1, dtype=np.float32)
    return np.power(np.float32(2.0), -8.0 * h / np.float32(3 * NHG)).astype(np.float32)


def _band_bias(gi):
    _, dil = GROUPS[gi]
    slopes = _alibi_slopes()[gi * NHG:(gi + 1) * NHG]
    qi = np.arange(BLK)[:, None]
    ki = np.arange(2 * BLK)[None, :]
    steps = BLK + qi - ki
    band = (steps >= 0) & (steps <= BLK)
    bias = -slopes[:, None, None] * (dil * steps).astype(np.float32)[None]
    return jnp.asarray(np.where(band[None], bias, np.float32(NEG)).astype(np.float32))


QB_FWD = 8
QB_BWD = 32


def _attn_specs(qb, c0):
    prev = lambda n: jnp.maximum(n * qb - 1, 0)
    return [pl.BlockSpec((qb * BLK, HEAD), lambda h, n: (n, c0 + h)),
            pl.BlockSpec((BLK, HEAD), lambda h, n: (prev(n), c0 + NHG + h)),
            pl.BlockSpec((qb * BLK, HEAD), lambda h, n: (n, c0 + NHG + h)),
            pl.BlockSpec((BLK, HEAD), lambda h, n: (prev(n), c0 + 2 * NHG + h)),
            pl.BlockSpec((qb * BLK, HEAD), lambda h, n: (n, c0 + 2 * NHG + h)),
            pl.BlockSpec((None, BLK, 2 * BLK), lambda h, n: (h, 0, 0))]


def _scores(q, kcat, bias, blk, seg):
    s = _dot(q, kcat, NT) * (HEAD ** -0.5) + bias
    col = lax.broadcasted_iota(jnp.int32, s.shape, 1)
    first = (blk % seg) == 0
    return jnp.where(jnp.logical_and(first, col < BLK), NEG, s)


def _attn_fwd(qkv, gi, name, col0=0):
    seg = (T // GROUPS[gi][1]) // BLK

    qb = min(QB_FWD, T // BLK)

    def body(q_ref, kp_ref, kc_ref, vp_ref, vc_ref, bias_ref, o_ref, l_ref):
        n = pl.program_id(0)
        for h in range(NHG):
            cols = pl.ds(h * HEAD, HEAD)
            kwin = jnp.concatenate([kp_ref[:, cols], kc_ref[:, cols]], axis=0)
            vwin = jnp.concatenate([vp_ref[:, cols], vc_ref[:, cols]], axis=0)
            bias = bias_ref[h]
            for b in range(qb):
                rows = pl.ds(b * BLK, BLK)
                s = _scores(q_ref[rows, cols], kwin[b * BLK:(b + 2) * BLK], bias, n * qb + b, seg)
                mx = jnp.max(s, axis=-1, keepdims=True)
                p = jnp.exp(s - mx)
                den = jnp.sum(p, axis=-1, keepdims=True)
                o_ref[rows, cols] = (_dot(p.astype(BF16), vwin[b * BLK:(b + 2) * BLK], NN) / den).astype(BF16)
                l_ref[rows, cols] = jnp.broadcast_to(mx + jnp.log(den), (BLK, HEAD))

    prev = lambda n: jnp.maximum(n * qb - 1, 0)
    c0 = col0 // AW
    cur = lambda part: pl.BlockSpec((qb * BLK, AW), lambda n: (n, part))
    halo = lambda part: pl.BlockSpec((BLK, AW), lambda n: (prev(n), part))
    return pl.pallas_call(
        body, name=name, grid=(T // (qb * BLK),),
        in_specs=[cur(c0), halo(c0 + 1), cur(c0 + 1), halo(c0 + 2), cur(c0 + 2),
                  _const_spec((NHG, BLK, 2 * BLK))],
        out_specs=[cur(0), cur(0)],
        out_shape=[jax.ShapeDtypeStruct((T, AW), BF16), jax.ShapeDtypeStruct((T, AW), F32)],
        compiler_params=_cp(("parallel",)),
    )(qkv, qkv, qkv, qkv, qkv, _band_bias(gi))


def _attn_bwd(qkv, dob, lse, delta, gi, name, col0=0):
    seg = (T // GROUPS[gi][1]) // BLK
    qb = min(QB_BWD, T // BLK)
    nb = T // (qb * BLK)
    scale = HEAD ** -0.5

    def body(q_ref, kp_ref, kc_ref, vp_ref, vc_ref, bias_ref, do_ref, l_ref, dl_ref, out_ref, dk_acc, dv_acc):
        n = pl.program_id(1)
        kwin = jnp.concatenate([kp_ref[...], kc_ref[...]], axis=0)
        vwin = jnp.concatenate([vp_ref[...], vc_ref[...]], axis=0)
        bias = bias_ref[...]
        dks, dvs = [], []
        for b in range(qb):
            rows = pl.ds(b * BLK, BLK)
            q = q_ref[rows, :]
            kcat = kwin[b * BLK:(b + 2) * BLK]
            s = _scores(q, kcat, bias, n * qb + b, seg)
            p = jnp.exp(s - l_ref[rows, pl.ds(0, 1)])
            dov = do_ref[rows, :]
            dvs.append(_dot(p.astype(BF16), dov, TN))
            dp = _dot(dov, vwin[b * BLK:(b + 2) * BLK], NT)
            dsb = (p * (dp - dl_ref[rows, pl.ds(0, 1)]) * scale).astype(BF16)
            row = pl.ds(pl.multiple_of((n * qb + b) * BLK, BLK), BLK)
            out_ref[0, row, :] = _dot(dsb, kcat, NN).astype(BF16)
            dks.append(_dot(dsb, q, TN))
        for b in range(qb):
            row = pl.ds(pl.multiple_of((n * qb + b) * BLK, BLK), BLK)
            if b + 1 < qb:
                dk_acc[row, :] = dks[b][BLK:] + dks[b + 1][:BLK]
                dv_acc[row, :] = dvs[b][BLK:] + dvs[b + 1][:BLK]
            else:
                dk_acc[row, :] = dks[b][BLK:]
                dv_acc[row, :] = dvs[b][BLK:]

        @pl.when(n > 0)
        def _():
            prow = pl.ds(pl.multiple_of((n * qb - 1) * BLK, BLK), BLK)
            dk_acc[prow, :] += dks[0][:BLK]
            dv_acc[prow, :] += dvs[0][:BLK]

        @pl.when(n == nb - 1)
        def _():
            out_ref[1] = dk_acc[...].astype(BF16)
            out_ref[2] = dv_acc[...].astype(BF16)

    oblk = pl.BlockSpec((qb * BLK, HEAD), lambda h, n: (n, h))
    return pl.pallas_call(
        body, name=name, grid=(NHG, nb),
        in_specs=_attn_specs(qb, col0 // HEAD) + [oblk, oblk, oblk],
        out_specs=pl.BlockSpec((3, T, HEAD), lambda h, n: (0, 0, h)),
        out_shape=jax.ShapeDtypeStruct((3, T, AW), BF16),
        scratch_shapes=[pltpu.VMEM((T, HEAD), F32), pltpu.VMEM((T, HEAD), F32)],
        compiler_params=_cp(("parallel", "arbitrary")),
    )(qkv, qkv, qkv, qkv, qkv, _band_bias(gi), dob, lse, delta)


def _merge(outs, lses, name):
    tm = PERM_TM
    dils = [d for _, d in GROUPS]
    ng = len(dils)

    def body(*refs):
        in_refs = refs[:2 * ng]
        ab_ref = refs[2 * ng]
        lse_refs = refs[2 * ng + 1:3 * ng + 1]
        tile = refs[-1]

        def token_order(ref, dil):
            if dil == 1:
                return ref[...].astype(F32)
            _load_unperm(ref, tile, dil)
            return _get_tile(tile)

        os = [token_order(in_refs[2 * i], d) for i, d in enumerate(dils)]
        ls = [token_order(in_refs[2 * i + 1], d) for i, d in enumerate(dils)]
        mx = jnp.maximum(jnp.maximum(ls[0], ls[1]), ls[2])
        es = [jnp.exp(v - mx) for v in ls]
        tot = es[0] + es[1] + es[2]
        att = (es[0] / tot) * os[0] + (es[1] / tot) * os[1] + (es[2] / tot) * os[2]
        ab_ref[...] = att.astype(BF16)
        lse = mx + jnp.log(tot)
        _put_tile(tile, lse)
        for dil, ref in zip(dils, lse_refs):
            if dil == 1:
                ref[...] = lse
            else:
                _store_perm(ref, tile, dil)

    row = pl.BlockSpec((tm, AW), lambda i: (i, 0))
    specs = [row if d == 1 else _perm_spec(d, AW) for d in dils]
    args = []
    for d, o, l in zip(dils, outs, lses):
        args += [o, l] if d == 1 else [o.reshape(d, T // d, AW), l.reshape(d, T // d, AW)]
    out = pl.pallas_call(
        body, name=name, grid=(T // tm,),
        in_specs=[sp for sp in specs for _ in range(2)], out_specs=[row] + specs,
        out_shape=[jax.ShapeDtypeStruct((T, AW), BF16)]
        + [jax.ShapeDtypeStruct((T, AW), F32) if d == 1 else _perm_shape(d, AW, F32) for d in dils],
        scratch_shapes=[_tile_scratch(AW)],
        compiler_params=_cp(("parallel",)),
    )(*args)
    return out[0], [o.reshape(T, AW) for o in out[1:]]


GATE_BLOCK0 = (IN_W - 2 * D) // (D // 2)


def _mix_out(z3b, attnb, gates, wc, wa_t, wo, x1, name):
    tm = 512

    def body(z_ref, a_ref, g_ref, wc_ref, wa_ref, wo_ref, x_ref, xo_ref, yc_ref, ya_ref, mx_ref):
        yc = _dot(z_ref[...], wc_ref[...], NN)
        ya = _dot(a_ref[...], wa_ref[...], NT)
        yc_ref[...] = yc.astype(BF16)
        ya_ref[...] = ya.astype(BF16)
        gv = g_ref[...].astype(F32)
        mixed = (_sig(gv[:, :D]) * yc + _sig(gv[:, D:]) * ya).astype(BF16)
        mx_ref[...] = mixed
        xo_ref[...] = x_ref[...] + _dot(mixed, wo_ref[...], NN)

    row = pl.BlockSpec((tm, D), lambda i: (i, 0))
    return pl.pallas_call(
        body, name=name, grid=(T // tm,),
        in_specs=[row, pl.BlockSpec((tm, AW), lambda i: (i, 0)), pl.BlockSpec((tm, 2 * D), lambda i: (i, 0)),
                  _const_spec((D, D)), _const_spec((D, AW)), _const_spec((D, D)), row],
        out_specs=[row, row, row, row],
        out_shape=[jax.ShapeDtypeStruct((T, D), F32), jax.ShapeDtypeStruct((T, D), BF16),
                   jax.ShapeDtypeStruct((T, D), BF16), jax.ShapeDtypeStruct((T, D), BF16)],
        compiler_params=_cp(("parallel",)),
    )(z3b, attnb, gates, wc, wa_t, wo, x1)


def _mix_out_bwd(dx2, gates, yc, ya, attn, wc, wa_t, wo, win_t, name):
    tm = PERM_TM
    dils = [d for _, d in GROUPS]
    ng = len(dils)

    def body(dx_ref, g_ref, yc_ref, ya_ref, at_ref, wc_ref, wa_ref, wo_ref, wg0_ref, wg1_ref, wg2_ref, wg3_ref,
             dg_ref, dyc_ref, dya_ref, dxb_ref, dz3_ref, dhg_ref, *rest):
        dat_refs, dl_refs, tile = rest[:ng], rest[ng:2 * ng], rest[-1]
        dxb = dx_ref[...].astype(BF16)
        dxb_ref[...] = dxb
        dmix = _dot(dxb, wo_ref[...], NT)
        gv = g_ref[...].astype(F32)
        sc = _sig(gv[:, :D])
        sa = _sig(gv[:, D:])
        ycv, yav = yc_ref[...].astype(F32), ya_ref[...].astype(F32)
        dgc = (dmix * ycv * sc * (1.0 - sc)).astype(BF16)
        dga = (dmix * yav * sa * (1.0 - sa)).astype(BF16)
        dg_ref[:, pl.ds(0, D)] = dgc
        dg_ref[:, pl.ds(D, D)] = dga
        half = D // 2
        dhg_ref[...] = (_dot(dgc[:, :half], wg0_ref[...], NN) + _dot(dgc[:, half:], wg1_ref[...], NN)
                        + _dot(dga[:, :half], wg2_ref[...], NN) + _dot(dga[:, half:], wg3_ref[...], NN))
        dyc = (dmix * sc).astype(BF16)
        dya = (dmix * sa).astype(BF16)
        dyc_ref[...] = dyc
        dya_ref[...] = dya
        dz3_ref[...] = _dot(dyc, wc_ref[...], NT).astype(BF16)
        dat = _dot(dya, wa_ref[...], NN)
        prod = dat * at_ref[...].astype(F32)
        delta = jnp.concatenate(
            [jnp.broadcast_to(jnp.sum(prod[:, h * HEAD:(h + 1) * HEAD], axis=-1, keepdims=True), (tm, HEAD))
             for h in range(NHG)], axis=1)
        for value, out_refs in ((dat, dat_refs), (delta, dl_refs)):
            _put_tile(tile, value)
            for dil, ref in zip(dils, out_refs):
                if dil == 1:
                    ref[...] = value.astype(ref.dtype)
                else:
                    _store_perm(ref, tile, dil)

    row = pl.BlockSpec((tm, D), lambda i: (i, 0))
    row2 = pl.BlockSpec((tm, 2 * D), lambda i: (i, 0))
    rowa = pl.BlockSpec((tm, AW), lambda i: (i, 0))
    aspecs = [rowa if d == 1 else _perm_spec(d, AW) for d in dils]

    def ashapes(dtype):
        return [jax.ShapeDtypeStruct((T, AW), dtype) if d == 1 else _perm_shape(d, AW, dtype) for d in dils]

    out = pl.pallas_call(
        body, name=name, grid=(T // tm,),
        in_specs=[row, row2, row, row, rowa, _const_spec((D, D)), _const_spec((D, AW)), _const_spec((D, D))]
        + [pl.BlockSpec((D // 2, D), lambda i, q=q: (GATE_BLOCK0 + q, 0), pipeline_mode=pl.Buffered(1))
           for q in range(4)],
        out_specs=[row2, row, row, row, row, row] + aspecs + aspecs,
        out_shape=[jax.ShapeDtypeStruct((T, 2 * D), BF16), jax.ShapeDtypeStruct((T, D), BF16),
                   jax.ShapeDtypeStruct((T, D), BF16), jax.ShapeDtypeStruct((T, D), BF16),
                   jax.ShapeDtypeStruct((T, D), BF16), jax.ShapeDtypeStruct((T, D), F32)]
        + ashapes(BF16) + ashapes(F32),
        scratch_shapes=[_tile_scratch(AW)],
        compiler_params=_cp(("parallel",)),
    )(dx2, gates, yc, ya, attn, wc, wa_t, wo, win_t, win_t, win_t, win_t)
    dats = [o.reshape(T, AW) for o in out[6:6 + ng]]
    deltas = [o.reshape(T, AW) for o in out[6 + ng:6 + 2 * ng]]
    return out[0], out[1], out[2], out[3], out[4], out[5], dats, deltas


def _peer(k):
    x, y, c = lax.axis_index("x"), lax.axis_index("y"), lax.axis_index("c")
    px = 1 - x if k & 4 else x
    py = 1 - y if k & 2 else y
    pc = 1 - c if k & 1 else c
    return (px, py, pc), 4 * px + 2 * py + pc


HBM_SPEC = pl.BlockSpec(memory_space=pltpu.HBM)
SEM_SPEC = pl.BlockSpec(memory_space=pltpu.SEMAPHORE)
EFFECT = pltpu.SideEffectType.DATAFLOW_SIDE_EFFECTING


def _my_place():
    return 4 * lax.axis_index("x") + 2 * lax.axis_index("y") + lax.axis_index("c")


def _tie(a, order_after, name):
    na = len(order_after)

    def body(*refs):
        del refs

    return pl.pallas_call(
        body, name=name, in_specs=[pl.BlockSpec(memory_space=pl.ANY)] * (1 + na),
        out_specs=pl.BlockSpec(memory_space=pl.ANY), out_shape=jax.ShapeDtypeStruct(a.shape, a.dtype),
        input_output_aliases={0: 0},
    )(a, *order_after)


def _prep_gather(ws, order_after, name):
    me = jnp.reshape(_my_place(), (1,)).astype(jnp.int32)
    n = len(ws)
    na = len(order_after)
    shapes = [((32, wv.shape[1]), F32) if wv.shape[0] == CONV_W else (wv.shape, BF16) for wv in ws]

    def body(me_ref, *refs):
        del me_ref
        ins, outs = refs[:n], refs[n + na:]
        for wv, i_ref, o_ref in zip(ws, ins, outs):
            if wv.shape[0] == CONV_W:
                o_ref[pl.ds(0, CONV_W), :] = i_ref[...]
                o_ref[pl.ds(CONV_W, 1), :] = jnp.zeros((1, wv.shape[1]), F32)
            else:
                o_ref[...] = i_ref[...].astype(BF16)

    grid_spec = pltpu.PrefetchScalarGridSpec(
        num_scalar_prefetch=1, grid=(1,),
        in_specs=[pl.BlockSpec(wv.shape, lambda i, m: (0, 0)) for wv in ws]
        + [pl.BlockSpec(memory_space=pl.ANY)] * na,
        out_specs=[pl.BlockSpec(shp, lambda i, m: (m[0], 0)) for shp, _ in shapes])
    return pl.pallas_call(
        body, name=name, grid_spec=grid_spec,
        out_shape=[jax.ShapeDtypeStruct((NDEV * shp[0], shp[1]), dt) for shp, dt in shapes],
        compiler_params=_cp(("arbitrary",)),
    )(me, *ws, *order_after)


GATHER_A = ((1, 0), (2, 0), (4, 0), (6, 0))
GATHER_B = ((1, 2), (1, 4), (1, 6))
GATHER_DIRECT = tuple((k, 0) for k in range(1, NDEV))


def _gather_start(lands, plan, order_after, name):
    n = len(lands)
    na = len(order_after)
    npl = len(plan)

    def body(*refs):
        land_refs = refs[:n]
        send, recv = refs[n + na], refs[n + na + 1]
        token = refs[-1]
        for w in range(n):
            rows = lands[w].shape[0] // NDEV
            for p, (k, j) in enumerate(plan):
                peer, _ = _peer(k)
                _, blk = _peer(j)
                part = land_refs[w].at[pl.ds(blk * rows, rows)]
                i = w * npl + p
                pltpu.make_async_remote_copy(src_ref=part, dst_ref=part, send_sem=send.at[i], recv_sem=recv.at[i],
                                             device_id=peer, device_id_type=MESH_ID).start()
        token[...] = jnp.zeros_like(token)

    nsem = n * npl
    bufs = [pltpu.with_memory_space_constraint(a, pltpu.HBM) for a in lands]
    out = pl.pallas_call(
        body, name=name,
        in_specs=[HBM_SPEC] * n + [pl.BlockSpec(memory_space=pl.ANY)] * na,
        out_specs=[SEM_SPEC, SEM_SPEC] + [HBM_SPEC] * n + [pl.BlockSpec(memory_space=pltpu.VMEM)],
        out_shape=[pltpu.SemaphoreType.DMA((nsem,)), pltpu.SemaphoreType.DMA((nsem,))]
        + [pltpu.HBM(a.shape, a.dtype) for a in bufs] + [jax.ShapeDtypeStruct((8, 128), F32)],
        input_output_aliases={i: 2 + i for i in range(n)},
        compiler_params=pltpu.CompilerParams(has_side_effects=EFFECT),
    )(*bufs, *order_after)
    return out[0], out[1], out[2:2 + n], out[-1]


def _gather_wait(started, plan, order_after, name):
    send, recv, lands, _ = started
    n = len(lands)
    na = len(order_after)
    npl = len(plan)

    def body(*refs):
        land_refs = refs[:n]
        send_ref, recv_ref = refs[n], refs[n + 1]
        for w in range(n):
            rows = lands[w].shape[0] // NDEV
            for p, (k, j) in enumerate(plan):
                peer, _ = _peer(k)
                _, blk = _peer(j)
                part = land_refs[w].at[pl.ds(blk * rows, rows)]
                i = w * npl + p
                cp = pltpu.make_async_remote_copy(src_ref=part, dst_ref=part, send_sem=send_ref.at[i],
                                                  recv_sem=recv_ref.at[i], device_id=peer, device_id_type=MESH_ID)
                cp.wait_send()
                cp.wait_recv()

    out = pl.pallas_call(
        body, name=name,
        in_specs=[HBM_SPEC] * n + [SEM_SPEC, SEM_SPEC] + [pl.BlockSpec(memory_space=pl.ANY)] * na,
        out_specs=[HBM_SPEC] * n,
        out_shape=[pltpu.HBM(a.shape, a.dtype) for a in lands],
        input_output_aliases={i: i for i in range(n)},
        compiler_params=pltpu.CompilerParams(has_side_effects=EFFECT),
    )(*lands, send, recv, *order_after)
    return list(out)


def _copy_ends(kind, src, land, me, plin, k):
    if kind == "scatter":
        rows = src.shape[0] // NDEV
        return src.at[pl.ds(plin * rows, rows)], land.at[k - 1]
    return src, land.at[me]


def _landing(kind, src):
    me = _my_place()
    if kind == "scatter":
        return lax.empty((NDEV - 1, src.shape[0] // NDEV) + src.shape[1:], src.dtype)
    land = lax.empty((NDEV,) + src.shape, src.dtype)
    return lax.dynamic_update_slice(land, src[None], (me,) + (0,) * src.ndim)


def _send_start(kinds, srcs, order_after, name):
    n = len(srcs)
    lands = [_landing(kd, s) for kd, s in zip(kinds, srcs)]
    na = len(order_after)

    def body(*refs):
        src_refs, land_refs = refs[:n], refs[n:2 * n]
        send, recv = refs[2 * n + na], refs[2 * n + na + 1]
        token = refs[-1]
        _, me = _peer(0)
        for w in range(n):
            for k in range(1, NDEV):
                peer, plin = _peer(k)
                s, d = _copy_ends(kinds[w], src_refs[w], land_refs[w], me, plin, k)
                i = w * (NDEV - 1) + k - 1
                pltpu.make_async_remote_copy(src_ref=s, dst_ref=d, send_sem=send.at[i], recv_sem=recv.at[i],
                                             device_id=peer, device_id_type=MESH_ID).start()
        token[...] = jnp.zeros_like(token)

    nsem = n * (NDEV - 1)
    bufs = [pltpu.with_memory_space_constraint(a, pltpu.HBM) for a in list(srcs) + lands]
    out = pl.pallas_call(
        body, name=name,
        in_specs=[HBM_SPEC] * (2 * n) + [pl.BlockSpec(memory_space=pl.ANY)] * na,
        out_specs=[SEM_SPEC, SEM_SPEC] + [HBM_SPEC] * (2 * n) + [pl.BlockSpec(memory_space=pltpu.VMEM)],
        out_shape=[pltpu.SemaphoreType.DMA((nsem,)), pltpu.SemaphoreType.DMA((nsem,))]
        + [pltpu.HBM(a.shape, a.dtype) for a in bufs] + [jax.ShapeDtypeStruct((8, 128), F32)],
        input_output_aliases={i: 2 + i for i in range(2 * n)},
        compiler_params=pltpu.CompilerParams(has_side_effects=EFFECT),
    )(*bufs, *order_after)
    return out[0], out[1], out[2:2 + n], out[2 + n:2 + 2 * n], out[-1]


def _send_wait(kinds, started, order_after, name):
    send, recv, srcs, lands, _ = started
    n = len(srcs)
    na = len(order_after)

    def body(*refs):
        src_refs, land_refs = refs[:n], refs[n:2 * n]
        send_ref, recv_ref = refs[2 * n], refs[2 * n + 1]
        _, me = _peer(0)
        for w in range(n):
            for k in range(1, NDEV):
                peer, plin = _peer(k)
                s, d = _copy_ends(kinds[w], src_refs[w], land_refs[w], me, plin, k)
                i = w * (NDEV - 1) + k - 1
                cp = pltpu.make_async_remote_copy(src_ref=s, dst_ref=d, send_sem=send_ref.at[i],
                                                  recv_sem=recv_ref.at[i], device_id=peer, device_id_type=MESH_ID)
                cp.wait_send()
                cp.wait_recv()

    bufs = list(srcs) + list(lands)
    out = pl.pallas_call(
        body, name=name,
        in_specs=[HBM_SPEC] * (2 * n) + [SEM_SPEC, SEM_SPEC] + [pl.BlockSpec(memory_space=pl.ANY)] * na,
        out_specs=[HBM_SPEC] * (2 * n),
        out_shape=[pltpu.HBM(a.shape, a.dtype) for a in bufs],
        input_output_aliases={i: i for i in range(2 * n)},
        compiler_params=pltpu.CompilerParams(has_side_effects=EFFECT),
    )(*bufs, send, recv, *order_after)
    return out[:n], out[n:]


def _gsum(own, land, name):
    rows, cols = own.shape
    tr = rows // 2 if rows * cols > 512 * 1024 and rows % 32 == 0 else rows

    def body(own_ref, l_ref, o_ref):
        tot = own_ref[...].astype(F32)
        for s in range(NDEV - 1):
            tot = tot + l_ref[s].astype(F32)
        o_ref[...] = tot

    return pl.pallas_call(
        body, name=name, grid=(rows // tr,),
        in_specs=[pl.BlockSpec((tr, cols), lambda i: (i, 0)),
                  pl.BlockSpec((NDEV - 1, tr, cols), lambda i: (0, i, 0))],
        out_specs=pl.BlockSpec((tr, cols), lambda i: (i, 0)),
        out_shape=jax.ShapeDtypeStruct((rows, cols), F32),
        compiler_params=_cp(("parallel",)),
    )(own, land)


def _adamw_math(w, g, m, v):
    m2 = B1 * m + (1.0 - B1) * g
    v2 = B2 * v + (1.0 - B2) * (g * g)
    m_hat = m2 / (1.0 - B1 ** STEP)
    v_hat = v2 / (1.0 - B2 ** STEP)
    delta = -LR * (m_hat / (jnp.sqrt(v_hat) + AEPS) + WD * w)
    return delta, m2, v2


def _adamw(w, g, m, v, name):
    rows, cols = w.shape
    tr = 256 if rows % 256 == 0 and rows > 256 else rows

    def body(w_ref, g_ref, m_ref, v_ref, d_ref, mo_ref, vo_ref):
        d, m2, v2 = _adamw_math(w_ref[...], g_ref[...], m_ref[...], v_ref[...])
        d_ref[...] = d
        mo_ref[...] = m2
        vo_ref[...] = v2

    blk = pl.BlockSpec((tr, cols), lambda i: (i, 0))
    return pl.pallas_call(
        body, name=name, grid=(rows // tr,), in_specs=[blk] * 4, out_specs=[blk] * 3,
        out_shape=[jax.ShapeDtypeStruct((rows, cols), F32)] * 3,
        compiler_params=_cp(("parallel",)),
    )(w, g, m, v)


UPD_TC = 256


def _update(src, land, w, m, v, name):
    rows, cols = land.shape[1:]
    tc = min(UPD_TC if rows > 512 else 2 * UPD_TC, cols)
    me = jnp.reshape(_my_place(), (1,)).astype(jnp.int32)

    def body(me_ref, own_ref, l_ref, w_ref, m_ref, v_ref, g_ref, d_ref, mo_ref, vo_ref):
        del me_ref
        g = own_ref[...].astype(F32)
        for s in range(NDEV - 1):
            g = g + l_ref[s].astype(F32)
        g_ref[...] = g
        d, m2, v2 = _adamw_math(w_ref[...], g, m_ref[...], v_ref[...])
        d_ref[...] = d
        mo_ref[...] = m2
        vo_ref[...] = v2

    wblk = pl.BlockSpec((rows, tc), lambda j, p: (0, j))
    grid_spec = pltpu.PrefetchScalarGridSpec(
        num_scalar_prefetch=1, grid=(cols // tc,),
        in_specs=[pl.BlockSpec((rows, tc), lambda j, p: (p[0], j)),
                  pl.BlockSpec((NDEV - 1, rows, tc), lambda j, p: (0, 0, j)), wblk, wblk, wblk],
        out_specs=[wblk] * 4)
    return pl.pallas_call(
        body, name=name, grid_spec=grid_spec, out_shape=[jax.ShapeDtypeStruct((rows, cols), F32)] * 4,
        compiler_params=_cp(("parallel",)),
    )(me, src, land, w, m, v)


def _small_update(vland, w8, m8, v8, name):
    def body(l_ref, w_ref, m_ref, v_ref, g_ref, d_ref, mo_ref, vo_ref):
        g = l_ref[0]
        for s in range(1, NDEV):
            g = g + l_ref[s]
        g_ref[...] = g
        d, m2, v2 = _adamw_math(w_ref[...], g, m_ref[...], v_ref[...])
        d_ref[...] = d
        mo_ref[...] = m2
        vo_ref[...] = v2

    return pl.pallas_call(
        body, name=name, out_shape=[jax.ShapeDtypeStruct((8, D), F32)] * 4,
        compiler_params=_cp(None),
    )(vland, w8, m8, v8)


def kernel(x, ffn1_norm, ffn1_w_gate, ffn1_w_up, ffn1_w_down, mix_norm, w_in, conv_dw_kernel, conv_dw_bias, conv_ln_gain, conv_ln_bias, conv_w_out, attn_w_out, w_o, ffn2_norm, ffn2_w_gate, ffn2_w_up, ffn2_w_down, final_norm, loss_target, m_ffn1_norm, m_ffn1_w_gate, m_ffn1_w_up, m_ffn1_w_down, m_mix_norm, m_w_in, m_conv_dw_kernel, m_conv_dw_bias, m_conv_ln_gain, m_conv_ln_bias, m_conv_w_out, m_attn_w_out, m_w_o, m_ffn2_norm, m_ffn2_w_gate, m_ffn2_w_up, m_ffn2_w_down, m_final_norm, v_ffn1_norm, v_ffn1_w_gate, v_ffn1_w_up, v_ffn1_w_down, v_mix_norm, v_w_in, v_conv_dw_kernel, v_conv_dw_bias, v_conv_ln_gain, v_conv_ln_bias, v_conv_w_out, v_attn_w_out, v_w_o, v_ffn2_norm, v_ffn2_w_gate, v_ffn2_w_up, v_ffn2_w_down, v_final_norm):
    names = ["ffn1_norm", "ffn1_w_gate", "ffn1_w_up", "ffn1_w_down", "mix_norm", "w_in", "conv_dw_kernel",
             "conv_dw_bias", "conv_ln_gain", "conv_ln_bias", "conv_w_out", "attn_w_out", "w_o", "ffn2_norm",
             "ffn2_w_gate", "ffn2_w_up", "ffn2_w_down", "final_norm"]
    w = dict(ffn1_norm=ffn1_norm, ffn1_w_gate=ffn1_w_gate, ffn1_w_up=ffn1_w_up, ffn1_w_down=ffn1_w_down, mix_norm=mix_norm, w_in=w_in, conv_dw_kernel=conv_dw_kernel, conv_dw_bias=conv_dw_bias, conv_ln_gain=conv_ln_gain, conv_ln_bias=conv_ln_bias, conv_w_out=conv_w_out, attn_w_out=attn_w_out, w_o=w_o, ffn2_norm=ffn2_norm, ffn2_w_gate=ffn2_w_gate, ffn2_w_up=ffn2_w_up, ffn2_w_down=ffn2_w_down, final_norm=final_norm)
    mo = dict(ffn1_norm=m_ffn1_norm, ffn1_w_gate=m_ffn1_w_gate, ffn1_w_up=m_ffn1_w_up, ffn1_w_down=m_ffn1_w_down, mix_norm=m_mix_norm, w_in=m_w_in, conv_dw_kernel=m_conv_dw_kernel, conv_dw_bias=m_conv_dw_bias, conv_ln_gain=m_conv_ln_gain, conv_ln_bias=m_conv_ln_bias, conv_w_out=m_conv_w_out, attn_w_out=m_attn_w_out, w_o=m_w_o, ffn2_norm=m_ffn2_norm, ffn2_w_gate=m_ffn2_w_gate, ffn2_w_up=m_ffn2_w_up, ffn2_w_down=m_ffn2_w_down, final_norm=m_final_norm)
    vo = dict(ffn1_norm=v_ffn1_norm, ffn1_w_gate=v_ffn1_w_gate, ffn1_w_up=v_ffn1_w_up, ffn1_w_down=v_ffn1_w_down, mix_norm=v_mix_norm, w_in=v_w_in, conv_dw_kernel=v_conv_dw_kernel, conv_dw_bias=v_conv_dw_bias, conv_ln_gain=v_conv_ln_gain, conv_ln_bias=v_conv_ln_bias, conv_w_out=v_conv_w_out, attn_w_out=v_attn_w_out, w_o=v_w_o, ffn2_norm=v_ffn2_norm, ffn2_w_gate=v_ffn2_w_gate, ffn2_w_up=v_ffn2_w_up, ffn2_w_down=v_ffn2_w_down, final_norm=v_final_norm)
    col_sharded = ("ffn1_w_gate", "ffn1_w_up", "w_in", "attn_w_out", "ffn2_w_gate", "ffn2_w_up")
    row_sharded = ("ffn1_w_down", "conv_w_out", "w_o", "ffn2_w_down")
    small = ("ffn1_norm", "mix_norm", "ffn2_norm", "final_norm", "conv_dw_bias", "conv_ln_gain", "conv_ln_bias")

    def landing_view(a, n):
        return jnp.transpose(a[0]) if n in col_sharded else a[0]

    def own_view(a, n):
        return jnp.transpose(a)[None] if n in col_sharded else a[None]

    ag_groups = (("ffn1_w_gate", "ffn1_w_up", "ffn1_w_down"),
                 ("w_in", "attn_w_out", "conv_w_out", "w_o", "conv_dw_kernel"),
                 ("ffn2_w_gate", "ffn2_w_up", "ffn2_w_down"))
    ag, order = [], []
    for gi, grp in enumerate(ag_groups):
        lands = _prep_gather([landing_view(w[n], n) for n in grp], order, f"gather_prep{gi}")
        st = _gather_start(lands, GATHER_DIRECT if gi == 2 else GATHER_A, [], f"gather_a_start{gi}")
        ag.append(st)
        order = [st[3]]

    def chips_in(gi, after):
        lands = _gather_wait(ag[gi], GATHER_A, after, f"gather_a_wait{gi}")
        return _gather_start(lands, GATHER_B, [], f"gather_b_start{gi}")

    def all_in(gi, st, after):
        return _gather_wait(st, GATHER_B, after, f"gather_b_wait{gi}")

    x0 = x[0]
    tgt = loss_target[0]
    gf = final_norm.reshape(1, D)

    wg1, wu1, wd1 = all_in(0, chips_in(0, [ag[2][3]]), [])
    x1, gg1, uu1, h2p = _ffn_fwd(x0, ffn1_norm, wg1, wu1, wd1, "ffn1_fwd", next_gain=mix_norm)
    h2 = h2p[0]
    win_t, wa_t, wc, wo, kern_blocks = all_in(1, chips_in(1, [x1]), [])
    kern = kern_blocks.reshape(NDEV, 32, D // NDEV).transpose(1, 0, 2).reshape(32, D)
    ptm = min(T, 2048)
    ab = _mm(h2, win_t, mode="nt", m=T, n=2 * D, k=D, tm=ptm, tn=512, tk=D, out_dtype=BF16, name="proj_conv")
    z1, z3b, gates = _conv_fwd(ab, kern, conv_dw_bias, conv_ln_gain, conv_ln_bias, "conv_fwd",
                               guest=(h2, win_t, (13, 14, 15, 16, 4, 7, 10), 512))
    qkv, qkv_col0 = [gates], [2 * D]
    for gi in range(1, len(GROUPS)):
        qkv.append(_mm(h2p[gi], win_t, mode="nt", m=T, n=3 * AW, k=D, tm=ptm, tn=AW, tk=D, out_dtype=BF16,
                       b_map=lambda i, j, kk, gi=gi: (4 + gi + 3 * j, 0), name=f"proj_qkv{gi}"))
        qkv_col0.append(0)
    outs, lses = [], []
    for gi, (_, dil) in enumerate(GROUPS):
        o, l = _attn_fwd(qkv[gi], gi, f"attn_fwd{gi}", col0=qkv_col0[gi])
        outs.append(o)
        lses.append(l)
    attnb, lse = _merge(outs, lses, "attn_merge")
    x2, yc, ya, mixedb = _mix_out(z3b, attnb, gates, wc, wa_t, wo, x1, "mix_out_fwd")
    wg2, wu2, wd2 = _gather_wait(ag[2], GATHER_DIRECT, [x2], "gather_a_wait2")
    gg2, uu2, dx3, dgf, loss_part = _ffn_fwd(x2, ffn2_norm, wg2, wu2, wd2, "ffn2_fwd", loss_of=(gf, tgt))

    dx2, dg3, dgb, dub, actb, hb, dob = _ffn_bwd(x2, ffn2_norm, gg2, uu2, dx3, wg2, wu2, wd2, "ffn2_bwd")
    grads = {}
    grads["ffn2_w_down"] = _wgrad(actb, dob, FF, D, "ffn2_dwd")
    rs_groups = [("ffn2_w_gate", "ffn2_w_up", "ffn2_w_down"),
                 ("attn_w_out", "conv_w_out", "w_o", "conv_dw_kernel"),
                 ("w_in",),
                 ("ffn1_w_gate",), ("ffn1_w_up",), ("ffn1_w_down",), ()]
    last = len(rs_groups) - 1
    rs = []

    dgates, dycb, dyab, dx2b, dz3, dh_gates, dattnb, delta = _mix_out_bwd(dx2, gates, yc, ya, attnb, wc, wa_t, wo,
                                                                          win_t, "mix_out_bwd")
    grads["w_o"] = _wgrad(mixedb, dx2b, D, D, "dw_o")
    grads["conv_w_out"] = _wgrad(z3b, dycb, D, D, "dw_conv_out")
    grads["attn_w_out"] = _wgrad(dyab, attnb, D, AW, "dw_attn_out")
    dab, dkern, dvec, grads["ffn2_w_gate"], grads["ffn2_w_up"] = _conv_bwd(
        dz3, z1, ab, kern, conv_ln_gain, conv_ln_bias, "conv_bwd", guest_lhs=(dgb, dub), guest_rhs=hb)
    grads["conv_dw_kernel"] = dkern.reshape(32, NDEV, D // NDEV).transpose(1, 0, 2).reshape(NDEV * 32, D // NDEV)
    rs.append(_send_start(["scatter"] * 3, [grads[n] for n in rs_groups[0]], [], "scatter_start0"))
    rs.append(_send_start(["scatter"] * 4, [grads[n] for n in rs_groups[1]], [rs[0][4]], "scatter_start1"))
    dattnb = [_tie(a, [rs[1][4]], f"tie_after_scatter1_{i}") for i, a in enumerate(dattnb)]

    dqkv, dq3s = [], []
    for gi, (_, dil) in enumerate(GROUPS):
        dq3 = _attn_bwd(qkv[gi], dattnb[gi], lse[gi], delta[gi], gi, f"attn_bwd{gi}", col0=qkv_col0[gi])
        dq3s.append(dq3)
        dqkv.append(dq3.reshape(3 * T, AW))

    wtk = min(T, 2048)
    dwin = _mm(dab, h2, mode="tn", m=2 * D, n=D, k=T, tm=2 * D, tn=D, tk=wtk, out_dtype=BF16, out_rows=IN_W,
               name="dw_in_conv")
    dwin = _mm(dgates, h2, mode="tn", m=2 * D, n=D, k=T, tm=512, tn=D, tk=wtk, out_dtype=BF16, out_rows=IN_W,
               o_map=lambda i, j, kk: (13 + i, 0), passthru=dwin, name="dw_in_gates")
    for gi in range(3):
        dwin = _mm(dqkv[gi], h2p[gi], mode="tn", m=3 * AW, n=D, k=T, tm=AW, tn=D, tk=wtk, out_dtype=BF16,
                   out_rows=IN_W, a_map=lambda i, j, kk: (i * (T // wtk) + kk, 0),
                   o_map=lambda i, j, kk, gi=gi: (4 + gi + 3 * i, 0), passthru=dwin, name=f"dw_in_qkv{gi}")
    grads["w_in"] = dwin
    rs.append(_send_start(["scatter"], [dwin], [rs[1][4]], "scatter_start2"))
    dab = _tie(dab, [rs[2][4]], "tie_after_scatter2")

    dx1, dg2 = _rms_bwd(x1, mix_norm, dh_gates, dab, (0, 1, 2, 3), dq3s, win_t, dx2, "mix_norm_bwd")

    dgb, dub, actb, hb, dob = _ffn_bwd_pre(x0, ffn1_norm, gg1, uu1, dx1, wd1, "ffn1_bwd_pre")
    grads["ffn1_w_gate"] = _wgrad(dgb, hb, FF, D, "ffn1_dwg")
    rs.append(_send_start(["scatter"], [grads["ffn1_w_gate"]], [rs[2][4]], "scatter_start3"))
    hb = _tie(hb, [rs[3][4]], "tie_after_scatter3")
    grads["ffn1_w_up"] = _wgrad(dub, hb, FF, D, "ffn1_dwu")
    rs.append(_send_start(["scatter"], [grads["ffn1_w_up"]], [rs[3][4]], "scatter_start4"))
    dob = _tie(dob, [rs[4][4]], "tie_after_scatter4")
    grads["ffn1_w_down"] = _wgrad(actb, dob, FF, D, "ffn1_dwd")
    rs.append(_send_start(["scatter"], [grads["ffn1_w_down"]], [rs[4][4]], "scatter_start5"))
    dgb = _tie(dgb, [rs[5][4]], "tie_after_scatter5")
    dx0, dg1 = _ffn_bwd_dx(x0, ffn1_norm, dgb, dub, dx1, wg1, wu1, "ffn1_bwd_dx")
    vec = jnp.concatenate([dg1, dg2, dg3, dgf, dvec[0:3], jnp.broadcast_to(loss_part[:, :1], (1, D))], axis=0)
    rs.append(_send_start(["bcast"], [vec], [rs[5][4]], "scatter_start6"))

    g_out, d_out, m_out, v_out = {}, {}, {}, {}
    me = _my_place()
    after, done = [rs[last][4]], []
    for gi, grp in enumerate(rs_groups):
        kinds = ["scatter"] * len(grp) + (["bcast"] if gi == last else [])
        srcs, lands = _send_wait(kinds, rs[gi], after + (done if gi >= last - 1 else []), f"scatter_wait{gi}")
        for n, src, land in zip(grp, srcs, lands):
            if n == "conv_dw_kernel":
                rows = src.shape[0] // NDEV
                own = lax.dynamic_slice(src, (me * rows, 0), (rows, src.shape[1]))
                g = _gsum(own, land, f"gsum_{n}")[:CONV_W]
                d, m2, v2 = _adamw(w[n][0], g, mo[n][0], vo[n][0], f"adamw_{n}")
                after = [d]
                done.append(d)
                g, d, m2, v2 = g[None], d[None], m2[None], v2[None]
            else:
                res = _update(src, land, landing_view(w[n], n), landing_view(mo[n], n), landing_view(vo[n], n),
                              f"update_{n}")
                after = [res[1]]
                done.append(res[1])
                g, d, m2, v2 = (own_view(a, n) for a in res)
            g_out[n], d_out[n], m_out[n], v_out[n] = g, d, m2, v2
    vland = lands[-1]

    def rows8(src):
        return jnp.concatenate([src[n].reshape(1, D) for n in small] + [jnp.ones((1, D), F32)], axis=0)

    g8, d8, m8, v8 = _small_update(vland, rows8(w), rows8(mo), rows8(vo), "small_update")
    for r, n in enumerate(small):
        shp = w[n].shape
        g_out[n], d_out[n], m_out[n], v_out[n] = (a[r].reshape(shp) for a in (g8, d8, m8, v8))
    loss = g8[7, 0]

    return (loss, dx0[None], *[g_out[n] for n in names], *[d_out[n] for n in names],
            *[m_out[n] for n in names], *[v_out[n] for n in names])
```

```python
import numpy as np
import jax
import jax.numpy as jnp
from jax import lax
from jax.experimental import pallas as pl
from jax.experimental.pallas import tpu as pltpu

F32 = jnp.float32
BF16 = jnp.bfloat16

T = 4096
D = 1024
FF = 2816
NDEV = 8
CONV_W = 31
HEAD = 128
BLK = 128
GROUPS = ((128, 1), (512, 4), (2048, 16))
NHG = 4
AW = NHG * HEAD
IN_W = 2 * D + 3 * 3 * AW + 2 * D
EPS = 1e-6
B1, B2, LR, AEPS, WD, STEP = 0.9, 0.999, 0.001, 1e-08, 0.01, 10
NEG = -1e30
VMEM_LIMIT = 56 * 1024 * 1024
MESH_ID = pl.DeviceIdType.MESH

NT = (((1,), (1,)), ((), ()))
NN = (((1,), (0,)), ((), ()))
TN = (((0,), (0,)), ((), ()))
_DIMS = {"nn": NN, "nt": NT, "tn": TN}


def _cp(sem=None):
    return pltpu.CompilerParams(dimension_semantics=sem, vmem_limit_bytes=VMEM_LIMIT)


def _sig(v):
    return 1.0 / (1.0 + jnp.exp(-v))


def _dot(a, b, dims):
    return lax.dot_general(a, b, dims, preferred_element_type=F32)


def _const_spec(shape):
    nd = len(shape)
    return pl.BlockSpec(shape, lambda *_: (0,) * nd)


def _mm(a, b, *, mode, m, n, k, tm, tn, tk, out_dtype, name, a_map=None, b_map=None,
        o_map=None, out_rows=None, init=None, passthru=None):
    gi, gj, gk = m // tm, n // tn, k // tk
    assert gi * tm == m and gj * tn == n and gk * tk == k, (name, m, n, k, tm, tn, tk)
    if mode == "nn":
        a_blk, b_blk = (tm, tk), (tk, tn)
        da, db = (lambda i, j, kk: (i, kk)), (lambda i, j, kk: (kk, j))
    elif mode == "nt":
        a_blk, b_blk = (tm, tk), (tn, tk)
        da, db = (lambda i, j, kk: (i, kk)), (lambda i, j, kk: (j, kk))
    else:
        a_blk, b_blk = (tk, tm), (tk, tn)
        da, db = (lambda i, j, kk: (kk, i)), (lambda i, j, kk: (kk, j))
    a_map = a_map or da
    b_map = b_map or db
    o_map = o_map or (lambda i, j, kk: (i, j))
    dims = _DIMS[mode]
    extra = init if init is not None else passthru
    out_rows = out_rows or m

    def body(*refs):
        if init is not None:
            a_ref, b_ref, i_ref, o_ref = refs[:4]
        elif passthru is not None:
            a_ref, b_ref, _, o_ref = refs[:4]
        else:
            a_ref, b_ref, o_ref = refs[:3]
        if gk == 1:
            prod = _dot(a_ref[...], b_ref[...], dims)
            if init is not None:
                prod = prod + i_ref[...].astype(F32)
            o_ref[...] = prod.astype(out_dtype)
            return
        acc = refs[-1]
        kk = pl.program_id(2)

        @pl.when(kk == 0)
        def _():
            if init is not None:
                acc[...] = i_ref[...].astype(F32)
            else:
                acc[...] = jnp.zeros_like(acc)

        acc[...] += _dot(a_ref[...], b_ref[...], dims)

        @pl.when(kk == gk - 1)
        def _():
            o_ref[...] = acc[...].astype(out_dtype)

    in_specs = [pl.BlockSpec(a_blk, a_map), pl.BlockSpec(b_blk, b_map)]
    args = [a, b]
    aliases = {}
    if init is not None:
        in_specs.append(pl.BlockSpec((tm, tn), o_map))
        args.append(init)
        aliases = {2: 0}
    elif passthru is not None:
        in_specs.append(pl.BlockSpec(memory_space=pl.ANY))
        args.append(passthru)
        aliases = {2: 0}
    out_dt = extra.dtype if extra is not None else out_dtype
    assert out_dt == out_dtype
    return pl.pallas_call(
        body, name=name, grid=(gi, gj, gk),
        in_specs=in_specs, out_specs=pl.BlockSpec((tm, tn), o_map),
        out_shape=jax.ShapeDtypeStruct((out_rows, n), out_dtype),
        scratch_shapes=[pltpu.VMEM((tm, tn), F32)] if gk > 1 else [],
        input_output_aliases=aliases,
        compiler_params=_cp(("parallel", "parallel", "arbitrary")),
    )(*args)


def _ffn_fwd(x, g, wg_t, wu_t, wd, name, next_gain=None, loss_of=None):
    tm, fc = PERM_TM, 256
    nc = FF // fc
    n_in = 5 + (1 if next_gain is not None else 0) + (2 if loss_of is not None else 0)

    def body(*refs):
        x_ref, g_ref, wg_ref, wu_ref, wd_ref = refs[:5]
        extra_in, outs = refs[5:n_in], refs[n_in:]
        act_ref = outs[-1]
        xv = x_ref[...]
        r = lax.rsqrt(jnp.mean(xv * xv, axis=-1, keepdims=True) + EPS)
        h = (xv * r * g_ref[...]).astype(BF16)
        gg_ref, uu_ref = (outs[0], outs[1]) if loss_of is not None else (outs[1], outs[2])
        for c in range(nc):
            sl = pl.ds(c * fc, fc)
            gg = _dot(h, wg_ref[sl, :], NT)
            uu = _dot(h, wu_ref[sl, :], NT)
            gg_ref[:, sl] = gg.astype(BF16)
            uu_ref[:, sl] = uu.astype(BF16)
            act_ref[:, sl] = (gg * _sig(gg) * uu).astype(BF16)
        y = xv + 0.5 * _dot(act_ref[...], wd_ref[...], NN)
        if loss_of is not None:
            _final_math(y, extra_in[0][...], extra_in[1][...], outs[2], outs[3], outs[4], pl.program_id(0))
            return
        outs[0][...] = y
        if next_gain is not None:
            tile = outs[-2]
            r2 = lax.rsqrt(jnp.mean(y * y, axis=-1, keepdims=True) + EPS)
            hv = y * r2 * extra_in[0][...]
            outs[3][...] = hv.astype(BF16)
            _put_tile(tile, hv)
            for dil, p_ref in zip(DILS, outs[4:4 + len(DILS)]):
                _store_perm(p_ref, tile, dil)

    wspec = pl.BlockSpec((FF, D), lambda i: (0, 0), pipeline_mode=pl.Buffered(1))
    row_d = pl.BlockSpec((tm, D), lambda i: (i, 0))
    row_f = pl.BlockSpec((tm, FF), lambda i: (i, 0))
    in_specs = [row_d, _const_spec((1, D)), wspec, wspec, wspec]
    args = [x, g, wg_t, wu_t, wd]
    f_shape = jax.ShapeDtypeStruct((T, FF), BF16)
    scratch = [pltpu.VMEM((tm, FF), BF16)]
    if loss_of is not None:
        in_specs += [_const_spec((1, D)), row_d]
        args += list(loss_of)
        out_specs = [row_f, row_f, row_d, _const_spec((1, D)), _const_spec((1, 128))]
        out_shape = [f_shape, f_shape, jax.ShapeDtypeStruct((T, D), F32), jax.ShapeDtypeStruct((1, D), F32),
                     jax.ShapeDtypeStruct((1, 128), F32)]
    else:
        out_specs = [row_d, row_f, row_f]
        out_shape = [jax.ShapeDtypeStruct((T, D), F32), f_shape, f_shape]
        if next_gain is not None:
            in_specs.append(_const_spec((1, D)))
            args.append(next_gain)
            out_specs += [row_d] + [_perm_spec(d, D) for d in DILS]
            out_shape += [jax.ShapeDtypeStruct((T, D), BF16)] + [_perm_shape(d, D, BF16) for d in DILS]
            scratch = [_tile_scratch(D)] + scratch
    out = pl.pallas_call(
        body, name=name, grid=(T // tm,), in_specs=in_specs, out_specs=out_specs, out_shape=out_shape,
        scratch_shapes=scratch,
        compiler_params=_cp(("arbitrary",) if loss_of is not None else ("parallel",)),
    )(*args)
    if next_gain is not None:
        return out[0], out[1], out[2], [out[3]] + [o.reshape(T, D) for o in out[4:]]
    return tuple(out)


def _ffn_bwd(x, g, gg_all, uu_all, dout, wg_t, wu_t, wd, name):
    tm, fc = 256, 256
    nc = FF // fc

    def body(x_ref, g_ref, gg_ref, uu_ref, do_ref, wg_ref, wu_ref, wd_ref,
             dx_ref, dgam_ref, dg_ref, du_ref, act_ref, h_ref, db_ref):
        i = pl.program_id(0)
        xv = x_ref[...]
        r = lax.rsqrt(jnp.mean(xv * xv, axis=-1, keepdims=True) + EPS)
        xhat = xv * r
        gam = g_ref[...]
        h_ref[...] = (xhat * gam).astype(BF16)
        dov = do_ref[...]
        dbv = (0.5 * dov).astype(BF16)
        db_ref[...] = dbv
        for c in range(nc):
            sl = pl.ds(c * fc, fc)
            da = _dot(dbv, wd_ref[sl, :], NT)
            gg = gg_ref[:, sl].astype(F32)
            uu = uu_ref[:, sl].astype(F32)
            s = _sig(gg)
            si = gg * s
            dgv = (da * uu * (s * (1.0 + gg * (1.0 - s)))).astype(BF16)
            duv = (da * si).astype(BF16)
            dg_ref[:, sl] = dgv
            du_ref[:, sl] = duv
            act_ref[:, sl] = (si * uu).astype(BF16)
        dh = _dot(dg_ref[...], wg_ref[...], NN) + _dot(du_ref[...], wu_ref[...], NN)

        @pl.when(i == 0)
        def _():
            dgam_ref[...] = jnp.zeros_like(dgam_ref)

        dgam_ref[...] += jnp.sum(dh * xhat, axis=0, keepdims=True)
        dxh = dh * gam
        dx_ref[...] = dov + r * (dxh - xhat * jnp.mean(dxh * xhat, axis=-1, keepdims=True))

    wspec = pl.BlockSpec((FF, D), lambda i: (0, 0), pipeline_mode=pl.Buffered(1))
    row_d = pl.BlockSpec((tm, D), lambda i: (i, 0))
    row_f = pl.BlockSpec((tm, FF), lambda i: (i, 0))
    return pl.pallas_call(
        body, name=name, grid=(T // tm,),
        in_specs=[row_d, _const_spec((1, D)), row_f, row_f, row_d, wspec, wspec, wspec],
        out_specs=[row_d, _const_spec((1, D)), row_f, row_f, row_f, row_d, row_d],
        out_shape=[jax.ShapeDtypeStruct((T, D), F32), jax.ShapeDtypeStruct((1, D), F32),
                   jax.ShapeDtypeStruct((T, FF), BF16), jax.ShapeDtypeStruct((T, FF), BF16),
                   jax.ShapeDtypeStruct((T, FF), BF16), jax.ShapeDtypeStruct((T, D), BF16),
                   jax.ShapeDtypeStruct((T, D), BF16)],
        compiler_params=_cp(("arbitrary",)),
    )(x, g, gg_all, uu_all, dout, wg_t, wu_t, wd)


def _ffn_bwd_pre(x, g, gg_all, uu_all, dout, wd, name):
    tm, fc = 512, 256
    nc = FF // fc

    def body(x_ref, g_ref, gg_ref, uu_ref, do_ref, wd_ref, dg_ref, du_ref, act_ref, h_ref, db_ref):
        xv = x_ref[...]
        r = lax.rsqrt(jnp.mean(xv * xv, axis=-1, keepdims=True) + EPS)
        h_ref[...] = (xv * r * g_ref[...]).astype(BF16)
        dbv = (0.5 * do_ref[...]).astype(BF16)
        db_ref[...] = dbv
        for c in range(nc):
            sl = pl.ds(c * fc, fc)
            da = _dot(dbv, wd_ref[sl, :], NT)
            gg = gg_ref[:, sl].astype(F32)
            uu = uu_ref[:, sl].astype(F32)
            s = _sig(gg)
            si = gg * s
            dg_ref[:, sl] = (da * uu * (s * (1.0 + gg * (1.0 - s)))).astype(BF16)
            du_ref[:, sl] = (da * si).astype(BF16)
            act_ref[:, sl] = (si * uu).astype(BF16)

    wspec = pl.BlockSpec((FF, D), lambda i: (0, 0), pipeline_mode=pl.Buffered(1))
    row_d = pl.BlockSpec((tm, D), lambda i: (i, 0))
    row_f = pl.BlockSpec((tm, FF), lambda i: (i, 0))
    return pl.pallas_call(
        body, name=name, grid=(T // tm,),
        in_specs=[row_d, _const_spec((1, D)), row_f, row_f, row_d, wspec],
        out_specs=[row_f, row_f, row_f, row_d, row_d],
        out_shape=[jax.ShapeDtypeStruct((T, FF), BF16), jax.ShapeDtypeStruct((T, FF), BF16),
                   jax.ShapeDtypeStruct((T, FF), BF16), jax.ShapeDtypeStruct((T, D), BF16),
                   jax.ShapeDtypeStruct((T, D), BF16)],
        compiler_params=_cp(("parallel",)),
    )(x, g, gg_all, uu_all, dout, wd)


def _ffn_bwd_dx(x, g, dgb, dub, dout, wg_t, wu_t, name):
    tm = 512

    def body(x_ref, g_ref, dg_ref, du_ref, do_ref, wg_ref, wu_ref, dx_ref, dgam_ref):
        i = pl.program_id(0)
        xv = x_ref[...]
        r = lax.rsqrt(jnp.mean(xv * xv, axis=-1, keepdims=True) + EPS)
        xhat = xv * r
        gam = g_ref[...]
        dh = _dot(dg_ref[...], wg_ref[...], NN) + _dot(du_ref[...], wu_ref[...], NN)

        @pl.when(i == 0)
        def _():
            dgam_ref[...] = jnp.zeros_like(dgam_ref)

        dgam_ref[...] += jnp.sum(dh * xhat, axis=0, keepdims=True)
        dxh = dh * gam
        dx_ref[...] = do_ref[...] + r * (dxh - xhat * jnp.mean(dxh * xhat, axis=-1, keepdims=True))

    wspec = pl.BlockSpec((FF, D), lambda i: (0, 0), pipeline_mode=pl.Buffered(1))
    row_d = pl.BlockSpec((tm, D), lambda i: (i, 0))
    row_f = pl.BlockSpec((tm, FF), lambda i: (i, 0))
    return pl.pallas_call(
        body, name=name, grid=(T // tm,),
        in_specs=[row_d, _const_spec((1, D)), row_f, row_f, row_d, wspec, wspec],
        out_specs=[row_d, _const_spec((1, D))],
        out_shape=[jax.ShapeDtypeStruct((T, D), F32), jax.ShapeDtypeStruct((1, D), F32)],
        compiler_params=_cp(("arbitrary",)),
    )(x, g, dgb, dub, dout, wg_t, wu_t)


def _wgrad(a, b, m, n, name):
    tm = m // 2 if m == FF else m
    return _mm(a, b, mode="tn", m=m, n=n, k=T, tm=tm, tn=n, tk=min(T, 2048), out_dtype=BF16, name=name)


PERM_TM = 512
DILS = tuple(d for _, d in GROUPS if d > 1)


def _perm_spec(dil, cols):
    return pl.BlockSpec((dil, PERM_TM // dil, cols), lambda i: (0, i, 0))


def _perm_shape(dil, cols, dtype):
    return jax.ShapeDtypeStruct((dil, T // dil, cols), dtype)


LANES = 128


def _tile_scratch(cols):
    return pltpu.VMEM((cols // LANES, PERM_TM, LANES), F32)


def _put_tile(tile, value):
    for c in range(tile.shape[0]):
        tile[c] = value[:, c * LANES:(c + 1) * LANES]


def _get_tile(tile):
    return jnp.concatenate([tile[c] for c in range(tile.shape[0])], axis=1)


def _store_perm(out_ref, tile, dil):
    for r in range(dil):
        for c in range(tile.shape[0]):
            out_ref[r, :, pl.ds(c * LANES, LANES)] = tile[c, pl.ds(r, PERM_TM // dil, stride=dil), :].astype(
                out_ref.dtype)


def _load_unperm(in_ref, tile, dil):
    for r in range(dil):
        for c in range(tile.shape[0]):
            tile[c, pl.ds(r, PERM_TM // dil, stride=dil), :] = in_ref[r, :, pl.ds(c * LANES, LANES)].astype(F32)


def _final_math(xv, gam, tgt, dx_ref, dgam_ref, loss_ref, i):
    r = lax.rsqrt(jnp.mean(xv * xv, axis=-1, keepdims=True) + EPS)
    xhat = xv * r
    err = xhat * gam - tgt
    part = 0.5 * jnp.sum(jnp.mean(err * err, axis=-1, keepdims=True), axis=0, keepdims=True)
    dy = err * (1.0 / D)

    @pl.when(i == 0)
    def _():
        dgam_ref[...] = jnp.zeros_like(dgam_ref)
        loss_ref[...] = jnp.zeros_like(loss_ref)

    dgam_ref[...] += jnp.sum(dy * xhat, axis=0, keepdims=True)
    loss_ref[...] += jnp.broadcast_to(part, loss_ref.shape)
    dxh = dy * gam
    dx_ref[...] = r * (dxh - xhat * jnp.mean(dxh * xhat, axis=-1, keepdims=True))


QKV_BLOCK0 = 2 * D // AW


def _rms_bwd(x, g, dh0, dlin, lin_blocks, dqkvs, win_t, dres, name):
    tm = PERM_TM
    dils = [d for _, d in GROUPS]
    ng = len(dils)
    nl = len(lin_blocks)
    assert len(dqkvs) == ng

    def body(*refs):
        x_ref, g_ref, dh0_ref, dl_ref = refs[:4]
        dq_refs = refs[4:4 + 3 * ng]
        w_refs = refs[4 + 3 * ng:4 + 6 * ng]
        wl_refs = refs[4 + 6 * ng:4 + 6 * ng + nl]
        dr_ref, dx_ref, dgam_ref = refs[4 + 6 * ng + nl:7 + 6 * ng + nl]
        tile = refs[7 + 6 * ng + nl]
        stages = refs[8 + 6 * ng + nl:]
        i = pl.program_id(0)
        xv = x_ref[...]
        r = lax.rsqrt(jnp.mean(xv * xv, axis=-1, keepdims=True) + EPS)
        xhat = xv * r
        gam = g_ref[...]
        dh = dh0_ref[...]
        for q in range(nl):
            dh = dh + _dot(dl_ref[:, pl.ds(q * AW, AW)], wl_refs[q][...], NN)
        si = 0
        for gi, dil in enumerate(dils):
            part = None
            for p in range(3):
                blk = dq_refs[3 * gi + p][...]
                term = _dot(blk.reshape(tm, AW), w_refs[3 * gi + p][...], NN)
                part = term if part is None else part + term
            if dil > 1:
                stage = stages[si]
                si += 1
                stage[...] = part.reshape(dil, tm // dil, D)
                _load_unperm(stage, tile, dil)
                part = _get_tile(tile)
            dh = dh + part

        @pl.when(i == 0)
        def _():
            dgam_ref[...] = jnp.zeros_like(dgam_ref)

        dgam_ref[...] += jnp.sum(dh * xhat, axis=0, keepdims=True)
        dxh = dh * gam
        dx_ref[...] = dr_ref[...] + r * (dxh - xhat * jnp.mean(dxh * xhat, axis=-1, keepdims=True))

    row_d = pl.BlockSpec((tm, D), lambda i: (i, 0))
    dq_specs, dq_args, w_specs = [], [], []
    for gi, (d, a) in enumerate(zip(dils, dqkvs)):
        for p in range(3):
            if d == 1:
                dq_specs.append(pl.BlockSpec((None, tm, AW), lambda i, p=p: (p, i, 0)))
                dq_args.append(a)
            else:
                dq_specs.append(pl.BlockSpec((None, d, tm // d, AW), lambda i, p=p: (p, 0, i, 0)))
                dq_args.append(a.reshape(3, d, T // d, AW))
            w_specs.append(pl.BlockSpec((AW, D), lambda i, q=QKV_BLOCK0 + gi + 3 * p: (q, 0),
                                        pipeline_mode=pl.Buffered(1)))
    return pl.pallas_call(
        body, name=name, grid=(T // tm,),
        in_specs=[row_d, _const_spec((1, D)), row_d, pl.BlockSpec((tm, nl * AW), lambda i: (i, 0))] + dq_specs + w_specs
        + [pl.BlockSpec((AW, D), lambda i, q=q: (q, 0), pipeline_mode=pl.Buffered(1)) for q in lin_blocks] + [row_d],
        out_specs=[row_d, _const_spec((1, D))],
        out_shape=[jax.ShapeDtypeStruct((T, D), F32), jax.ShapeDtypeStruct((1, D), F32)],
        scratch_shapes=[_tile_scratch(D)] + [pltpu.VMEM((d, tm // d, D), F32) for d in dils if d > 1],
        compiler_params=_cp(("arbitrary",)),
    )(x, g, dh0, dlin, *dq_args, *([win_t] * (3 * ng + nl)), dres)


CONV_TM = 256
CONV_HALO = 32
CONV_RB = 16


def _glu(ab):
    ab = ab.astype(F32)
    return ab[:, :D] * _sig(ab[:, D:])


def _ln_stats(z1):
    mu = jnp.mean(z1, axis=-1, keepdims=True)
    zc = z1 - mu
    rstd = lax.rsqrt(jnp.mean(zc * zc, axis=-1, keepdims=True) + EPS)
    return zc * rstd, rstd


def _fill_shifts(zs):
    n = zs.shape[1] - 8
    for s in range(1, 8):
        zs[s, pl.ds(0, n), :] = zs[0, pl.ds(s, n), :]


def _shifted(zs, start, rows):
    q, s = divmod(start, 8)
    return zs[s, pl.ds(8 * q, rows), :]


def _conv_fwd(ab, kern, dwb, lng, lnb, name, guest=None):
    tm, hl, rb = CONV_TM, CONV_HALO, CONV_RB
    off = hl - (CONV_W - 1)
    if guest is not None:
        g_a, g_b, g_blocks, g_rows = guest
        g_nblk = len(g_blocks)

    def body(ab_ref, abh_ref, k_ref, dwb_ref, lng_ref, lnb_ref, *rest):
        if guest is not None:
            ga_ref, gb_refs, rest = rest[0], rest[1:1 + g_nblk], rest[1 + g_nblk:]
            z1_ref, z3_ref, go_ref, zs = rest
            for q, gb_ref in enumerate(gb_refs):
                go_ref[:, pl.ds(q * g_rows, g_rows)] = _dot(ga_ref[...], gb_ref[...], NT).astype(BF16)
        else:
            z1_ref, z3_ref, zs = rest
        i = pl.program_id(0)
        zs[0, pl.ds(0, hl), :] = jnp.where(i > 0, _glu(abh_ref[...]), 0.0)
        zs[0, pl.ds(hl, tm), :] = _glu(ab_ref[...])
        _fill_shifts(zs)
        for b in range(tm // rb):
            acc = jnp.zeros((rb, D), F32)
            for j in range(CONV_W):
                acc = acc + _shifted(zs, b * rb + off + j, rb) * k_ref[pl.ds(j, 1), :]
            z1 = acc + dwb_ref[...]
            z1_ref[pl.ds(b * rb, rb), :] = z1
            zn, _ = _ln_stats(z1)
            z2 = zn * lng_ref[...] + lnb_ref[...]
            z3_ref[pl.ds(b * rb, rb), :] = (z2 * _sig(z2)).astype(BF16)

    row = pl.BlockSpec((tm, D), lambda i: (i, 0))
    g_specs, g_args, g_ospecs, g_oshapes = [], [], [], []
    if guest is not None:
        kdim = g_a.shape[1]
        g_specs = [pl.BlockSpec((tm, kdim), lambda i: (i, 0))]
        g_specs += [pl.BlockSpec((g_rows, kdim), lambda i, q=q: (q, 0), pipeline_mode=pl.Buffered(1))
                    for q in g_blocks]
        g_args = [g_a] + [g_b] * g_nblk
        g_ospecs = [pl.BlockSpec((tm, g_nblk * g_rows), lambda i: (i, 0))]
        g_oshapes = [jax.ShapeDtypeStruct((T, g_nblk * g_rows), BF16)]
    return pl.pallas_call(
        body, name=name, grid=(T // tm,),
        in_specs=[pl.BlockSpec((tm, 2 * D), lambda i: (i, 0)),
                  pl.BlockSpec((hl, 2 * D), lambda i: (jnp.maximum(i * (tm // hl) - 1, 0), 0)),
                  _const_spec((32, D)), _const_spec((1, D)), _const_spec((1, D)), _const_spec((1, D))] + g_specs,
        out_specs=[row, row] + g_ospecs,
        out_shape=[jax.ShapeDtypeStruct((T, D), F32), jax.ShapeDtypeStruct((T, D), BF16)] + g_oshapes,
        scratch_shapes=[pltpu.VMEM((8, hl + tm, D), F32)],
        compiler_params=_cp(("parallel",)),
    )(ab, ab, kern, dwb, lng, lnb, *g_args)


GUEST_TM = 256


def _conv_bwd(dz3, z1, ab, kern, lng, lnb, name, guest_lhs=(), guest_rhs=None):
    tm, hl, rb = CONV_TM, CONV_HALO, CONV_RB
    off = hl - (CONV_W - 1)
    nsteps = T // tm
    ng = len(guest_lhs)
    gblocks = [a.shape[1] // GUEST_TM for a in guest_lhs]
    assert all(gb <= nsteps and gb * GUEST_TM == a.shape[1] for gb, a in zip(gblocks, guest_lhs))

    def ln_bwd(dz3v, z1v, lngv, lnbv):
        zn, rstd = _ln_stats(z1v)
        z2 = zn * lngv + lnbv
        s = _sig(z2)
        dz2 = dz3v * (s * (1.0 + z2 * (1.0 - s)))
        dzn = dz2 * lngv
        dz1 = rstd * (dzn - jnp.mean(dzn, axis=-1, keepdims=True)
                      - zn * jnp.mean(dzn * zn, axis=-1, keepdims=True))
        return dz1, dz2, zn

    def body(dz3_ref, dz3h_ref, z1_ref, z1h_ref, ab_ref, abh_ref, k_ref, lng_ref, lnb_ref, *rest):
        g_in, rest = rest[:ng + (1 if ng else 0)], rest[ng + (1 if ng else 0):]
        dab_ref, dk_ref, dvec_ref = rest[:3]
        g_out, (zs, dzs) = rest[3:3 + ng], rest[3 + ng:]
        i = pl.program_id(0)
        lngv, lnbv = lng_ref[...], lnb_ref[...]

        for a_ref, o_ref, gb in zip(g_in[:ng], g_out, gblocks):
            @pl.when(i < gb)
            def _(a_ref=a_ref, o_ref=o_ref):
                o_ref[...] = _dot(a_ref[...], g_in[ng][...], TN).astype(BF16)

        @pl.when(i == 0)
        def _():
            dk_ref[...] = jnp.zeros_like(dk_ref)
            dvec_ref[...] = jnp.zeros_like(dvec_ref)

        dz1, dz2, zn = ln_bwd(dz3_ref[...].astype(F32), z1_ref[...], lngv, lnbv)
        dvec_ref[pl.ds(0, 1), :] += jnp.sum(dz1, axis=0, keepdims=True)
        dvec_ref[pl.ds(1, 1), :] += jnp.sum(dz2 * zn, axis=0, keepdims=True)
        dvec_ref[pl.ds(2, 1), :] += jnp.sum(dz2, axis=0, keepdims=True)
        dzs[0, pl.ds(0, tm), :] = dz1
        dz1h, _, _ = ln_bwd(dz3h_ref[...].astype(F32), z1h_ref[...], lngv, lnbv)
        dzs[0, pl.ds(tm, hl), :] = jnp.where(i < nsteps - 1, dz1h, 0.0)
        _fill_shifts(dzs)
        zs[0, pl.ds(0, hl), :] = jnp.where(i > 0, _glu(abh_ref[...]), 0.0)
        zs[0, pl.ds(hl, tm), :] = _glu(ab_ref[...])
        _fill_shifts(zs)

        for j in range(CONV_W):
            tot = jnp.zeros((rb, D), F32)
            for b in range(tm // rb):
                tot = tot + dzs[0, pl.ds(b * rb, rb), :] * _shifted(zs, b * rb + off + j, rb)
            dk_ref[pl.ds(j, 1), :] += jnp.sum(tot, axis=0, keepdims=True)

        for b in range(tm // rb):
            acc = jnp.zeros((rb, D), F32)
            for j in range(CONV_W):
                acc = acc + _shifted(dzs, b * rb + (CONV_W - 1) - j, rb) * k_ref[pl.ds(j, 1), :]
            av = ab_ref[pl.ds(b * rb, rb), pl.ds(0, D)].astype(F32)
            sb = _sig(ab_ref[pl.ds(b * rb, rb), pl.ds(D, D)].astype(F32))
            dab_ref[pl.ds(b * rb, rb), pl.ds(0, D)] = (acc * sb).astype(BF16)
            dab_ref[pl.ds(b * rb, rb), pl.ds(D, D)] = (acc * av * sb * (1.0 - sb)).astype(BF16)

    row = pl.BlockSpec((tm, D), lambda i: (i, 0))
    nxt = pl.BlockSpec((hl, D), lambda i: (jnp.minimum((i + 1) * (tm // hl), T // hl - 1), 0))
    g_specs, g_args, g_ospecs, g_oshapes = [], [], [], []
    for a, gb in zip(guest_lhs, gblocks):
        g_specs.append(pl.BlockSpec((T, GUEST_TM), lambda i, gb=gb: (0, jnp.minimum(i, gb - 1))))
        g_args.append(a)
        g_ospecs.append(pl.BlockSpec((GUEST_TM, guest_rhs.shape[1]), lambda i, gb=gb: (jnp.minimum(i, gb - 1), 0)))
        g_oshapes.append(jax.ShapeDtypeStruct((a.shape[1], guest_rhs.shape[1]), BF16))
    if ng:
        g_specs.append(pl.BlockSpec(guest_rhs.shape, lambda i: (0, 0), pipeline_mode=pl.Buffered(1)))
        g_args.append(guest_rhs)
    return pl.pallas_call(
        body, name=name, grid=(nsteps,),
        in_specs=[row, nxt, row, nxt,
                  pl.BlockSpec((tm, 2 * D), lambda i: (i, 0)),
                  pl.BlockSpec((hl, 2 * D), lambda i: (jnp.maximum(i * (tm // hl) - 1, 0), 0)),
                  _const_spec((32, D)), _const_spec((1, D)), _const_spec((1, D))] + g_specs,
        out_specs=[pl.BlockSpec((tm, 2 * D), lambda i: (i, 0)), _const_spec((32, D)), _const_spec((8, D))]
        + g_ospecs,
        out_shape=[jax.ShapeDtypeStruct((T, 2 * D), BF16), jax.ShapeDtypeStruct((32, D), F32),
                   jax.ShapeDtypeStruct((8, D), F32)] + g_oshapes,
        scratch_shapes=[pltpu.VMEM((8, hl + tm, D), F32), pltpu.VMEM((8, tm + hl, D), F32)],
        compiler_params=_cp(("arbitrary",)),
    )(dz3, dz3, z1, z1, ab, ab, kern, lng, lnb, *g_args)


def _alibi_slopes():
    h = np.arange(1, 3 * NHG + 1, dtype=np.float32)
    return np.power(np.float32(2.0), -8.0 * h / np.float32(3 * NHG)).astype(np.float32)


def _band_bias(gi):
    _, dil = GROUPS[gi]
    slopes = _alibi_slopes()[gi * NHG:(gi + 1) * NHG]
    qi = np.arange(BLK)[:, None]
    ki = np.arange(2 * BLK)[None, :]
    steps = BLK + qi - ki
    band = (steps >= 0) & (steps <= BLK)
    bias = -slopes[:, None, None] * (dil * steps).astype(np.float32)[None]
    return jnp.asarray(np.where(band[None], bias, np.float32(NEG)).astype(np.float32))


QB_FWD = 8
QB_BWD = 32


def _attn_specs(qb, c0):
    prev = lambda n: jnp.maximum(n * qb - 1, 0)
    return [pl.BlockSpec((qb * BLK, HEAD), lambda h, n: (n, c0 + h)),
            pl.BlockSpec((BLK, HEAD), lambda h, n: (prev(n), c0 + NHG + h)),
            pl.BlockSpec((qb * BLK, HEAD), lambda h, n: (n, c0 + NHG + h)),
            pl.BlockSpec((BLK, HEAD), lambda h, n: (prev(n), c0 + 2 * NHG + h)),
            pl.BlockSpec((qb * BLK, HEAD), lambda h, n: (n, c0 + 2 * NHG + h)),
            pl.BlockSpec((None, BLK, 2 * BLK), lambda h, n: (h, 0, 0))]


def _scores(q, kcat, bias, blk, seg):
    s = _dot(q, kcat, NT) * (HEAD ** -0.5) + bias
    col = lax.broadcasted_iota(jnp.int32, s.shape, 1)
    first = (blk % seg) == 0
    return jnp.where(jnp.logical_and(first, col < BLK), NEG, s)


def _attn_fwd(qkv, gi, name, col0=0):
    seg = (T // GROUPS[gi][1]) // BLK

    qb = min(QB_FWD, T // BLK)

    def body(q_ref, kp_ref, kc_ref, vp_ref, vc_ref, bias_ref, o_ref, l_ref):
        n = pl.program_id(0)
        for h in range(NHG):
            cols = pl.ds(h * HEAD, HEAD)
            kwin = jnp.concatenate([kp_ref[:, cols], kc_ref[:, cols]], axis=0)
            vwin = jnp.concatenate([vp_ref[:, cols], vc_ref[:, cols]], axis=0)
            bias = bias_ref[h]
            for b in range(qb):
                rows = pl.ds(b * BLK, BLK)
                s = _scores(q_ref[rows, cols], kwin[b * BLK:(b + 2) * BLK], bias, n * qb + b, seg)
                mx = jnp.max(s, axis=-1, keepdims=True)
                p = jnp.exp(s - mx)
                den = jnp.sum(p, axis=-1, keepdims=True)
                o_ref[rows, cols] = (_dot(p.astype(BF16), vwin[b * BLK:(b + 2) * BLK], NN) / den).astype(BF16)
                l_ref[rows, cols] = jnp.broadcast_to(mx + jnp.log(den), (BLK, HEAD))

    prev = lambda n: jnp.maximum(n * qb - 1, 0)
    c0 = col0 // AW
    cur = lambda part: pl.BlockSpec((qb * BLK, AW), lambda n: (n, part))
    halo = lambda part: pl.BlockSpec((BLK, AW), lambda n: (prev(n), part))
    return pl.pallas_call(
        body, name=name, grid=(T // (qb * BLK),),
        in_specs=[cur(c0), halo(c0 + 1), cur(c0 + 1), halo(c0 + 2), cur(c0 + 2),
                  _const_spec((NHG, BLK, 2 * BLK))],
        out_specs=[cur(0), cur(0)],
        out_shape=[jax.ShapeDtypeStruct((T, AW), BF16), jax.ShapeDtypeStruct((T, AW), F32)],
        compiler_params=_cp(("parallel",)),
    )(qkv, qkv, qkv, qkv, qkv, _band_bias(gi))


def _attn_bwd(qkv, dob, lse, delta, gi, name, col0=0):
    seg = (T // GROUPS[gi][1]) // BLK
    qb = min(QB_BWD, T // BLK)
    nb = T // (qb * BLK)
    scale = HEAD ** -0.5

    def body(q_ref, kp_ref, kc_ref, vp_ref, vc_ref, bias_ref, do_ref, l_ref, dl_ref, out_ref, dk_acc, dv_acc):
        n = pl.program_id(1)
        kwin = jnp.concatenate([kp_ref[...], kc_ref[...]], axis=0)
        vwin = jnp.concatenate([vp_ref[...], vc_ref[...]], axis=0)
        bias = bias_ref[...]
        dks, dvs = [], []
        for b in range(qb):
            rows = pl.ds(b * BLK, BLK)
            q = q_ref[rows, :]
            kcat = kwin[b * BLK:(b + 2) * BLK]
            s = _scores(q, kcat, bias, n * qb + b, seg)
            p = jnp.exp(s - l_ref[rows, pl.ds(0, 1)])
            dov = do_ref[rows, :]
            dvs.append(_dot(p.astype(BF16), dov, TN))
            dp = _dot(dov, vwin[b * BLK:(b + 2) * BLK], NT)
            dsb = (p * (dp - dl_ref[rows, pl.ds(0, 1)]) * scale).astype(BF16)
            row = pl.ds(pl.multiple_of((n * qb + b) * BLK, BLK), BLK)
            out_ref[0, row, :] = _dot(dsb, kcat, NN).astype(BF16)
            dks.append(_dot(dsb, q, TN))
        for b in range(qb):
            row = pl.ds(pl.multiple_of((n * qb + b) * BLK, BLK), BLK)
            if b + 1 < qb:
                dk_acc[row, :] = dks[b][BLK:] + dks[b + 1][:BLK]
                dv_acc[row, :] = dvs[b][BLK:] + dvs[b + 1][:BLK]
            else:
                dk_acc[row, :] = dks[b][BLK:]
                dv_acc[row, :] = dvs[b][BLK:]

        @pl.when(n > 0)
        def _():
            prow = pl.ds(pl.multiple_of((n * qb - 1) * BLK, BLK), BLK)
            dk_acc[prow, :] += dks[0][:BLK]
            dv_acc[prow, :] += dvs[0][:BLK]

        @pl.when(n == nb - 1)
        def _():
            out_ref[1] = dk_acc[...].astype(BF16)
            out_ref[2] = dv_acc[...].astype(BF16)

    oblk = pl.BlockSpec((qb * BLK, HEAD), lambda h, n: (n, h))
    return pl.pallas_call(
        body, name=name, grid=(NHG, nb),
        in_specs=_attn_specs(qb, col0 // HEAD) + [oblk, oblk, oblk],
        out_specs=pl.BlockSpec((3, T, HEAD), lambda h, n: (0, 0, h)),
        out_shape=jax.ShapeDtypeStruct((3, T, AW), BF16),
        scratch_shapes=[pltpu.VMEM((T, HEAD), F32), pltpu.VMEM((T, HEAD), F32)],
        compiler_params=_cp(("parallel", "arbitrary")),
    )(qkv, qkv, qkv, qkv, qkv, _band_bias(gi), dob, lse, delta)


def _merge(outs, lses, name):
    tm = PERM_TM
    dils = [d for _, d in GROUPS]
    ng = len(dils)

    def body(*refs):
        in_refs = refs[:2 * ng]
        ab_ref = refs[2 * ng]
        lse_refs = refs[2 * ng + 1:3 * ng + 1]
        tile = refs[-1]

        def token_order(ref, dil):
            if dil == 1:
                return ref[...].astype(F32)
            _load_unperm(ref, tile, dil)
            return _get_tile(tile)

        os = [token_order(in_refs[2 * i], d) for i, d in enumerate(dils)]
        ls = [token_order(in_refs[2 * i + 1], d) for i, d in enumerate(dils)]
        mx = jnp.maximum(jnp.maximum(ls[0], ls[1]), ls[2])
        es = [jnp.exp(v - mx) for v in ls]
        tot = es[0] + es[1] + es[2]
        att = (es[0] / tot) * os[0] + (es[1] / tot) * os[1] + (es[2] / tot) * os[2]
        ab_ref[...] = att.astype(BF16)
        lse = mx + jnp.log(tot)
        _put_tile(tile, lse)
        for dil, ref in zip(dils, lse_refs):
            if dil == 1:
                ref[...] = lse
            else:
                _store_perm(ref, tile, dil)

    row = pl.BlockSpec((tm, AW), lambda i: (i, 0))
    specs = [row if d == 1 else _perm_spec(d, AW) for d in dils]
    args = []
    for d, o, l in zip(dils, outs, lses):
        args += [o, l] if d == 1 else [o.reshape(d, T // d, AW), l.reshape(d, T // d, AW)]
    out = pl.pallas_call(
        body, name=name, grid=(T // tm,),
        in_specs=[sp for sp in specs for _ in range(2)], out_specs=[row] + specs,
        out_shape=[jax.ShapeDtypeStruct((T, AW), BF16)]
        + [jax.ShapeDtypeStruct((T, AW), F32) if d == 1 else _perm_shape(d, AW, F32) for d in dils],
        scratch_shapes=[_tile_scratch(AW)],
        compiler_params=_cp(("parallel",)),
    )(*args)
    return out[0], [o.reshape(T, AW) for o in out[1:]]


GATE_BLOCK0 = (IN_W - 2 * D) // (D // 2)


def _mix_out(z3b, attnb, gates, wc, wa_t, wo, x1, name):
    tm = 512

    def body(z_ref, a_ref, g_ref, wc_ref, wa_ref, wo_ref, x_ref, xo_ref, yc_ref, ya_ref, mx_ref):
        yc = _dot(z_ref[...], wc_ref[...], NN)
        ya = _dot(a_ref[...], wa_ref[...], NT)
        yc_ref[...] = yc.astype(BF16)
        ya_ref[...] = ya.astype(BF16)
        gv = g_ref[...].astype(F32)
        mixed = (_sig(gv[:, :D]) * yc + _sig(gv[:, D:]) * ya).astype(BF16)
        mx_ref[...] = mixed
        xo_ref[...] = x_ref[...] + _dot(mixed, wo_ref[...], NN)

    row = pl.BlockSpec((tm, D), lambda i: (i, 0))
    return pl.pallas_call(
        body, name=name, grid=(T // tm,),
        in_specs=[row, pl.BlockSpec((tm, AW), lambda i: (i, 0)), pl.BlockSpec((tm, 2 * D), lambda i: (i, 0)),
                  _const_spec((D, D)), _const_spec((D, AW)), _const_spec((D, D)), row],
        out_specs=[row, row, row, row],
        out_shape=[jax.ShapeDtypeStruct((T, D), F32), jax.ShapeDtypeStruct((T, D), BF16),
                   jax.ShapeDtypeStruct((T, D), BF16), jax.ShapeDtypeStruct((T, D), BF16)],
        compiler_params=_cp(("parallel",)),
    )(z3b, attnb, gates, wc, wa_t, wo, x1)


def _mix_out_bwd(dx2, gates, yc, ya, attn, wc, wa_t, wo, win_t, name):
    tm = PERM_TM
    dils = [d for _, d in GROUPS]
    ng = len(dils)

    def body(dx_ref, g_ref, yc_ref, ya_ref, at_ref, wc_ref, wa_ref, wo_ref, wg0_ref, wg1_ref, wg2_ref, wg3_ref,
             dg_ref, dyc_ref, dya_ref, dxb_ref, dz3_ref, dhg_ref, *rest):
        dat_refs, dl_refs, tile = rest[:ng], rest[ng:2 * ng], rest[-1]
        dxb = dx_ref[...].astype(BF16)
        dxb_ref[...] = dxb
        dmix = _dot(dxb, wo_ref[...], NT)
        gv = g_ref[...].astype(F32)
        sc = _sig(gv[:, :D])
        sa = _sig(gv[:, D:])
        ycv, yav = yc_ref[...].astype(F32), ya_ref[...].astype(F32)
        dgc = (dmix * ycv * sc * (1.0 - sc)).astype(BF16)
        dga = (dmix * yav * sa * (1.0 - sa)).astype(BF16)
        dg_ref[:, pl.ds(0, D)] = dgc
        dg_ref[:, pl.ds(D, D)] = dga
        half = D // 2
        dhg_ref[...] = (_dot(dgc[:, :half], wg0_ref[...], NN) + _dot(dgc[:, half:], wg1_ref[...], NN)
                        + _dot(dga[:, :half], wg2_ref[...], NN) + _dot(dga[:, half:], wg3_ref[...], NN))
        dyc = (dmix * sc).astype(BF16)
        dya = (dmix * sa).astype(BF16)
        dyc_ref[...] = dyc
        dya_ref[...] = dya
        dz3_ref[...] = _dot(dyc, wc_ref[...], NT).astype(BF16)
        dat = _dot(dya, wa_ref[...], NN)
        prod = dat * at_ref[...].astype(F32)
        delta = jnp.concatenate(
            [jnp.broadcast_to(jnp.sum(prod[:, h * HEAD:(h + 1) * HEAD], axis=-1, keepdims=True), (tm, HEAD))
             for h in range(NHG)], axis=1)
        for value, out_refs in ((dat, dat_refs), (delta, dl_refs)):
            _put_tile(tile, value)
            for dil, ref in zip(dils, out_refs):
                if dil == 1:
                    ref[...] = value.astype(ref.dtype)
                else:
                    _store_perm(ref, tile, dil)

    row = pl.BlockSpec((tm, D), lambda i: (i, 0))
    row2 = pl.BlockSpec((tm, 2 * D), lambda i: (i, 0))
    rowa = pl.BlockSpec((tm, AW), lambda i: (i, 0))
    aspecs = [rowa if d == 1 else _perm_spec(d, AW) for d in dils]

    def ashapes(dtype):
        return [jax.ShapeDtypeStruct((T, AW), dtype) if d == 1 else _perm_shape(d, AW, dtype) for d in dils]

    out = pl.pallas_call(
        body, name=name, grid=(T // tm,),
        in_specs=[row, row2, row, row, rowa, _const_spec((D, D)), _const_spec((D, AW)), _const_spec((D, D))]
        + [pl.BlockSpec((D // 2, D), lambda i, q=q: (GATE_BLOCK0 + q, 0), pipeline_mode=pl.Buffered(1))
           for q in range(4)],
        out_specs=[row2, row, row, row, row, row] + aspecs + aspecs,
        out_shape=[jax.ShapeDtypeStruct((T, 2 * D), BF16), jax.ShapeDtypeStruct((T, D), BF16),
                   jax.ShapeDtypeStruct((T, D), BF16), jax.ShapeDtypeStruct((T, D), BF16),
                   jax.ShapeDtypeStruct((T, D), BF16), jax.ShapeDtypeStruct((T, D), F32)]
        + ashapes(BF16) + ashapes(F32),
        scratch_shapes=[_tile_scratch(AW)],
        compiler_params=_cp(("parallel",)),
    )(dx2, gates, yc, ya, attn, wc, wa_t, wo, win_t, win_t, win_t, win_t)
    dats = [o.reshape(T, AW) for o in out[6:6 + ng]]
    deltas = [o.reshape(T, AW) for o in out[6 + ng:6 + 2 * ng]]
    return out[0], out[1], out[2], out[3], out[4], out[5], dats, deltas


def _peer(k):
    x, y, c = lax.axis_index("x"), lax.axis_index("y"), lax.axis_index("c")
    px = 1 - x if k & 4 else x
    py = 1 - y if k & 2 else y
    pc = 1 - c if k & 1 else c
    return (px, py, pc), 4 * px + 2 * py + pc


HBM_SPEC = pl.BlockSpec(memory_space=pltpu.HBM)
SEM_SPEC = pl.BlockSpec(memory_space=pltpu.SEMAPHORE)
EFFECT = pltpu.SideEffectType.DATAFLOW_SIDE_EFFECTING


def _my_place():
    return 4 * lax.axis_index("x") + 2 * lax.axis_index("y") + lax.axis_index("c")


def _tie(a, order_after, name):
    na = len(order_after)

    def body(*refs):
        del refs

    return pl.pallas_call(
        body, name=name, in_specs=[pl.BlockSpec(memory_space=pl.ANY)] * (1 + na),
        out_specs=pl.BlockSpec(memory_space=pl.ANY), out_shape=jax.ShapeDtypeStruct(a.shape, a.dtype),
        input_output_aliases={0: 0},
    )(a, *order_after)


def _prep_gather(ws, order_after, name):
    me = jnp.reshape(_my_place(), (1,)).astype(jnp.int32)
    n = len(ws)
    na = len(order_after)
    shapes = [((32, wv.shape[1]), F32) if wv.shape[0] == CONV_W else (wv.shape, BF16) for wv in ws]

    def body(me_ref, *refs):
        del me_ref
        ins, outs = refs[:n], refs[n + na:]
        for wv, i_ref, o_ref in zip(ws, ins, outs):
            if wv.shape[0] == CONV_W:
                o_ref[pl.ds(0, CONV_W), :] = i_ref[...]
                o_ref[pl.ds(CONV_W, 1), :] = jnp.zeros((1, wv.shape[1]), F32)
            else:
                o_ref[...] = i_ref[...].astype(BF16)

    grid_spec = pltpu.PrefetchScalarGridSpec(
        num_scalar_prefetch=1, grid=(1,),
        in_specs=[pl.BlockSpec(wv.shape, lambda i, m: (0, 0)) for wv in ws]
        + [pl.BlockSpec(memory_space=pl.ANY)] * na,
        out_specs=[pl.BlockSpec(shp, lambda i, m: (m[0], 0)) for shp, _ in shapes])
    return pl.pallas_call(
        body, name=name, grid_spec=grid_spec,
        out_shape=[jax.ShapeDtypeStruct((NDEV * shp[0], shp[1]), dt) for shp, dt in shapes],
        compiler_params=_cp(("arbitrary",)),
    )(me, *ws, *order_after)


GATHER_A = ((1, 0), (2, 0), (4, 0), (6, 0))
GATHER_B = ((1, 2), (1, 4), (1, 6))
GATHER_DIRECT = tuple((k, 0) for k in range(1, NDEV))


def _gather_start(lands, plan, order_after, name):
    n = len(lands)
    na = len(order_after)
    npl = len(plan)

    def body(*refs):
        land_refs = refs[:n]
        send, recv = refs[n + na], refs[n + na + 1]
        token = refs[-1]
        for w in range(n):
            rows = lands[w].shape[0] // NDEV
            for p, (k, j) in enumerate(plan):
                peer, _ = _peer(k)
                _, blk = _peer(j)
                part = land_refs[w].at[pl.ds(blk * rows, rows)]
                i = w * npl + p
                pltpu.make_async_remote_copy(src_ref=part, dst_ref=part, send_sem=send.at[i], recv_sem=recv.at[i],
                                             device_id=peer, device_id_type=MESH_ID).start()
        token[...] = jnp.zeros_like(token)

    nsem = n * npl
    bufs = [pltpu.with_memory_space_constraint(a, pltpu.HBM) for a in lands]
    out = pl.pallas_call(
        body, name=name,
        in_specs=[HBM_SPEC] * n + [pl.BlockSpec(memory_space=pl.ANY)] * na,
        out_specs=[SEM_SPEC, SEM_SPEC] + [HBM_SPEC] * n + [pl.BlockSpec(memory_space=pltpu.VMEM)],
        out_shape=[pltpu.SemaphoreType.DMA((nsem,)), pltpu.SemaphoreType.DMA((nsem,))]
        + [pltpu.HBM(a.shape, a.dtype) for a in bufs] + [jax.ShapeDtypeStruct((8, 128), F32)],
        input_output_aliases={i: 2 + i for i in range(n)},
        compiler_params=pltpu.CompilerParams(has_side_effects=EFFECT),
    )(*bufs, *order_after)
    return out[0], out[1], out[2:2 + n], out[-1]


def _gather_wait(started, plan, order_after, name):
    send, recv, lands, _ = started
    n = len(lands)
    na = len(order_after)
    npl = len(plan)

    def body(*refs):
        land_refs = refs[:n]
        send_ref, recv_ref = refs[n], refs[n + 1]
        for w in range(n):
            rows = lands[w].shape[0] // NDEV
            for p, (k, j) in enumerate(plan):
                peer, _ = _peer(k)
                _, blk = _peer(j)
                part = land_refs[w].at[pl.ds(blk * rows, rows)]
                i = w * npl + p
                cp = pltpu.make_async_remote_copy(src_ref=part, dst_ref=part, send_sem=send_ref.at[i],
                                                  recv_sem=recv_ref.at[i], device_id=peer, device_id_type=MESH_ID)
                cp.wait_send()
                cp.wait_recv()

    out = pl.pallas_call(
        body, name=name,
        in_specs=[HBM_SPEC] * n + [SEM_SPEC, SEM_SPEC] + [pl.BlockSpec(memory_space=pl.ANY)] * na,
        out_specs=[HBM_SPEC] * n,
        out_shape=[pltpu.HBM(a.shape, a.dtype) for a in lands],
        input_output_aliases={i: i for i in range(n)},
        compiler_params=pltpu.CompilerParams(has_side_effects=EFFECT),
    )(*lands, send, recv, *order_after)
    return list(out)


def _copy_ends(kind, src, land, me, plin, k):
    if kind == "scatter":
        rows = src.shape[0] // NDEV
        return src.at[pl.ds(plin * rows, rows)], land.at[k - 1]
    return src, land.at[me]


def _landing(kind, src):
    me = _my_place()
    if kind == "scatter":
        return lax.empty((NDEV - 1, src.shape[0] // NDEV) + src.shape[1:], src.dtype)
    land = lax.empty((NDEV,) + src.shape, src.dtype)
    return lax.dynamic_update_slice(land, src[None], (me,) + (0,) * src.ndim)


def _send_start(kinds, srcs, order_after, name):
    n = len(srcs)
    lands = [_landing(kd, s) for kd, s in zip(kinds, srcs)]
    na = len(order_after)

    def body(*refs):
        src_refs, land_refs = refs[:n], refs[n:2 * n]
        send, recv = refs[2 * n + na], refs[2 * n + na + 1]
        token = refs[-1]
        _, me = _peer(0)
        for w in range(n):
            for k in range(1, NDEV):
                peer, plin = _peer(k)
                s, d = _copy_ends(kinds[w], src_refs[w], land_refs[w], me, plin, k)
                i = w * (NDEV - 1) + k - 1
                pltpu.make_async_remote_copy(src_ref=s, dst_ref=d, send_sem=send.at[i], recv_sem=recv.at[i],
                                             device_id=peer, device_id_type=MESH_ID).start()
        token[...] = jnp.zeros_like(token)

    nsem = n * (NDEV - 1)
    bufs = [pltpu.with_memory_space_constraint(a, pltpu.HBM) for a in list(srcs) + lands]
    out = pl.pallas_call(
        body, name=name,
        in_specs=[HBM_SPEC] * (2 * n) + [pl.BlockSpec(memory_space=pl.ANY)] * na,
        out_specs=[SEM_SPEC, SEM_SPEC] + [HBM_SPEC] * (2 * n) + [pl.BlockSpec(memory_space=pltpu.VMEM)],
        out_shape=[pltpu.SemaphoreType.DMA((nsem,)), pltpu.SemaphoreType.DMA((nsem,))]
        + [pltpu.HBM(a.shape, a.dtype) for a in bufs] + [jax.ShapeDtypeStruct((8, 128), F32)],
        input_output_aliases={i: 2 + i for i in range(2 * n)},
        compiler_params=pltpu.CompilerParams(has_side_effects=EFFECT),
    )(*bufs, *order_after)
    return out[0], out[1], out[2:2 + n], out[2 + n:2 + 2 * n], out[-1]


def _send_wait(kinds, started, order_after, name):
    send, recv, srcs, lands, _ = started
    n = len(srcs)
    na = len(order_after)

    def body(*refs):
        src_refs, land_refs = refs[:n], refs[n:2 * n]
        send_ref, recv_ref = refs[2 * n], refs[2 * n + 1]
        _, me = _peer(0)
        for w in range(n):
            for k in range(1, NDEV):
                peer, plin = _peer(k)
                s, d = _copy_ends(kinds[w], src_refs[w], land_refs[w], me, plin, k)
                i = w * (NDEV - 1) + k - 1
                cp = pltpu.make_async_remote_copy(src_ref=s, dst_ref=d, send_sem=send_ref.at[i],
                                                  recv_sem=recv_ref.at[i], device_id=peer, device_id_type=MESH_ID)
                cp.wait_send()
                cp.wait_recv()

    bufs = list(srcs) + list(lands)
    out = pl.pallas_call(
        body, name=name,
        in_specs=[HBM_SPEC] * (2 * n) + [SEM_SPEC, SEM_SPEC] + [pl.BlockSpec(memory_space=pl.ANY)] * na,
        out_specs=[HBM_SPEC] * (2 * n),
        out_shape=[pltpu.HBM(a.shape, a.dtype) for a in bufs],
        input_output_aliases={i: i for i in range(2 * n)},
        compiler_params=pltpu.CompilerParams(has_side_effects=EFFECT),
    )(*bufs, send, recv, *order_after)
    return out[:n], out[n:]


def _gsum(own, land, name):
    rows, cols = own.shape
    tr = rows // 2 if rows * cols > 512 * 1024 and rows % 32 == 0 else rows

    def body(own_ref, l_ref, o_ref):
        tot = own_ref[...].astype(F32)
        for s in range(NDEV - 1):
            tot = tot + l_ref[s].astype(F32)
        o_ref[...] = tot

    return pl.pallas_call(
        body, name=name, grid=(rows // tr,),
        in_specs=[pl.BlockSpec((tr, cols), lambda i: (i, 0)),
                  pl.BlockSpec((NDEV - 1, tr, cols), lambda i: (0, i, 0))],
        out_specs=pl.BlockSpec((tr, cols), lambda i: (i, 0)),
        out_shape=jax.ShapeDtypeStruct((rows, cols), F32),
        compiler_params=_cp(("parallel",)),
    )(own, land)


def _adamw_math(w, g, m, v):
    m2 = B1 * m + (1.0 - B1) * g
    v2 = B2 * v + (1.0 - B2) * (g * g)
    m_hat = m2 / (1.0 - B1 ** STEP)
    v_hat = v2 / (1.0 - B2 ** STEP)
    delta = -LR * (m_hat / (jnp.sqrt(v_hat) + AEPS) + WD * w)
    return delta, m2, v2


def _adamw(w, g, m, v, name):
    rows, cols = w.shape
    tr = 256 if rows % 256 == 0 and rows > 256 else rows

    def body(w_ref, g_ref, m_ref, v_ref, d_ref, mo_ref, vo_ref):
        d, m2, v2 = _adamw_math(w_ref[...], g_ref[...], m_ref[...], v_ref[...])
        d_ref[...] = d
        mo_ref[...] = m2
        vo_ref[...] = v2

    blk = pl.BlockSpec((tr, cols), lambda i: (i, 0))
    return pl.pallas_call(
        body, name=name, grid=(rows // tr,), in_specs=[blk] * 4, out_specs=[blk] * 3,
        out_shape=[jax.ShapeDtypeStruct((rows, cols), F32)] * 3,
        compiler_params=_cp(("parallel",)),
    )(w, g, m, v)


UPD_TC = 128


def _update(src, land, w, m, v, name):
    rows, cols = land.shape[1:]
    tc = min(UPD_TC if rows > 512 else 2 * UPD_TC, cols)
    me = jnp.reshape(_my_place(), (1,)).astype(jnp.int32)

    def body(me_ref, own_ref, l_ref, w_ref, m_ref, v_ref, g_ref, d_ref, mo_ref, vo_ref):
        del me_ref
        g = own_ref[...].astype(F32)
        for s in range(NDEV - 1):
            g = g + l_ref[s].astype(F32)
        g_ref[...] = g
        d, m2, v2 = _adamw_math(w_ref[...], g, m_ref[...], v_ref[...])
        d_ref[...] = d
        mo_ref[...] = m2
        vo_ref[...] = v2

    wblk = pl.BlockSpec((rows, tc), lambda j, p: (0, j))
    grid_spec = pltpu.PrefetchScalarGridSpec(
        num_scalar_prefetch=1, grid=(cols // tc,),
        in_specs=[pl.BlockSpec((rows, tc), lambda j, p: (p[0], j)),
                  pl.BlockSpec((NDEV - 1, rows, tc), lambda j, p: (0, 0, j)), wblk, wblk, wblk],
        out_specs=[wblk] * 4)
    return pl.pallas_call(
        body, name=name, grid_spec=grid_spec, out_shape=[jax.ShapeDtypeStruct((rows, cols), F32)] * 4,
        compiler_params=_cp(("parallel",)),
    )(me, src, land, w, m, v)


def _small_update(vland, w8, m8, v8, name):
    def body(l_ref, w_ref, m_ref, v_ref, g_ref, d_ref, mo_ref, vo_ref):
        g = l_ref[0]
        for s in range(1, NDEV):
            g = g + l_ref[s]
        g_ref[...] = g
        d, m2, v2 = _adamw_math(w_ref[...], g, m_ref[...], v_ref[...])
        d_ref[...] = d
        mo_ref[...] = m2
        vo_ref[...] = v2

    return pl.pallas_call(
        body, name=name, out_shape=[jax.ShapeDtypeStruct((8, D), F32)] * 4,
        compiler_params=_cp(None),
    )(vland, w8, m8, v8)


def kernel(x, ffn1_norm, ffn1_w_gate, ffn1_w_up, ffn1_w_down, mix_norm, w_in, conv_dw_kernel, conv_dw_bias, conv_ln_gain, conv_ln_bias, conv_w_out, attn_w_out, w_o, ffn2_norm, ffn2_w_gate, ffn2_w_up, ffn2_w_down, final_norm, loss_target, m_ffn1_norm, m_ffn1_w_gate, m_ffn1_w_up, m_ffn1_w_down, m_mix_norm, m_w_in, m_conv_dw_kernel, m_conv_dw_bias, m_conv_ln_gain, m_conv_ln_bias, m_conv_w_out, m_attn_w_out, m_w_o, m_ffn2_norm, m_ffn2_w_gate, m_ffn2_w_up, m_ffn2_w_down, m_final_norm, v_ffn1_norm, v_ffn1_w_gate, v_ffn1_w_up, v_ffn1_w_down, v_mix_norm, v_w_in, v_conv_dw_kernel, v_conv_dw_bias, v_conv_ln_gain, v_conv_ln_bias, v_conv_w_out, v_attn_w_out, v_w_o, v_ffn2_norm, v_ffn2_w_gate, v_ffn2_w_up, v_ffn2_w_down, v_final_norm):
    names = ["ffn1_norm", "ffn1_w_gate", "ffn1_w_up", "ffn1_w_down", "mix_norm", "w_in", "conv_dw_kernel",
             "conv_dw_bias", "conv_ln_gain", "conv_ln_bias", "conv_w_out", "attn_w_out", "w_o", "ffn2_norm",
             "ffn2_w_gate", "ffn2_w_up", "ffn2_w_down", "final_norm"]
    w = dict(ffn1_norm=ffn1_norm, ffn1_w_gate=ffn1_w_gate, ffn1_w_up=ffn1_w_up, ffn1_w_down=ffn1_w_down, mix_norm=mix_norm, w_in=w_in, conv_dw_kernel=conv_dw_kernel, conv_dw_bias=conv_dw_bias, conv_ln_gain=conv_ln_gain, conv_ln_bias=conv_ln_bias, conv_w_out=conv_w_out, attn_w_out=attn_w_out, w_o=w_o, ffn2_norm=ffn2_norm, ffn2_w_gate=ffn2_w_gate, ffn2_w_up=ffn2_w_up, ffn2_w_down=ffn2_w_down, final_norm=final_norm)
    mo = dict(ffn1_norm=m_ffn1_norm, ffn1_w_gate=m_ffn1_w_gate, ffn1_w_up=m_ffn1_w_up, ffn1_w_down=m_ffn1_w_down, mix_norm=m_mix_norm, w_in=m_w_in, conv_dw_kernel=m_conv_dw_kernel, conv_dw_bias=m_conv_dw_bias, conv_ln_gain=m_conv_ln_gain, conv_ln_bias=m_conv_ln_bias, conv_w_out=m_conv_w_out, attn_w_out=m_attn_w_out, w_o=m_w_o, ffn2_norm=m_ffn2_norm, ffn2_w_gate=m_ffn2_w_gate, ffn2_w_up=m_ffn2_w_up, ffn2_w_down=m_ffn2_w_down, final_norm=m_final_norm)
    vo = dict(ffn1_norm=v_ffn1_norm, ffn1_w_gate=v_ffn1_w_gate, ffn1_w_up=v_ffn1_w_up, ffn1_w_down=v_ffn1_w_down, mix_norm=v_mix_norm, w_in=v_w_in, conv_dw_kernel=v_conv_dw_kernel, conv_dw_bias=v_conv_dw_bias, conv_ln_gain=v_conv_ln_gain, conv_ln_bias=v_conv_ln_bias, conv_w_out=v_conv_w_out, attn_w_out=v_attn_w_out, w_o=v_w_o, ffn2_norm=v_ffn2_norm, ffn2_w_gate=v_ffn2_w_gate, ffn2_w_up=v_ffn2_w_up, ffn2_w_down=v_ffn2_w_down, final_norm=v_final_norm)
    col_sharded = ("ffn1_w_gate", "ffn1_w_up", "w_in", "attn_w_out", "ffn2_w_gate", "ffn2_w_up")
    row_sharded = ("ffn1_w_down", "conv_w_out", "w_o", "ffn2_w_down")
    small = ("ffn1_norm", "mix_norm", "ffn2_norm", "final_norm", "conv_dw_bias", "conv_ln_gain", "conv_ln_bias")

    def landing_view(a, n):
        return jnp.transpose(a[0]) if n in col_sharded else a[0]

    def own_view(a, n):
        return jnp.transpose(a)[None] if n in col_sharded else a[None]

    ag_groups = (("ffn1_w_gate", "ffn1_w_up", "ffn1_w_down"),
                 ("w_in", "attn_w_out", "conv_w_out", "w_o", "conv_dw_kernel"),
                 ("ffn2_w_gate", "ffn2_w_up", "ffn2_w_down"))
    ag, order = [], []
    for gi, grp in enumerate(ag_groups):
        lands = _prep_gather([landing_view(w[n], n) for n in grp], order, f"gather_prep{gi}")
        st = _gather_start(lands, GATHER_DIRECT if gi == 2 else GATHER_A, [], f"gather_a_start{gi}")
        ag.append(st)
        order = [st[3]]

    def chips_in(gi, after):
        lands = _gather_wait(ag[gi], GATHER_A, after, f"gather_a_wait{gi}")
        return _gather_start(lands, GATHER_B, [], f"gather_b_start{gi}")

    def all_in(gi, st, after):
        return _gather_wait(st, GATHER_B, after, f"gather_b_wait{gi}")

    x0 = x[0]
    tgt = loss_target[0]
    gf = final_norm.reshape(1, D)

    wg1, wu1, wd1 = all_in(0, chips_in(0, [ag[2][3]]), [])
    x1, gg1, uu1, h2p = _ffn_fwd(x0, ffn1_norm, wg1, wu1, wd1, "ffn1_fwd", next_gain=mix_norm)
    h2 = h2p[0]
    win_t, wa_t, wc, wo, kern_blocks = all_in(1, chips_in(1, [x1]), [])
    kern = kern_blocks.reshape(NDEV, 32, D // NDEV).transpose(1, 0, 2).reshape(32, D)
    ptm = min(T, 2048)
    ab = _mm(h2, win_t, mode="nt", m=T, n=2 * D, k=D, tm=ptm, tn=512, tk=D, out_dtype=BF16, name="proj_conv")
    z1, z3b, gates = _conv_fwd(ab, kern, conv_dw_bias, conv_ln_gain, conv_ln_bias, "conv_fwd",
                               guest=(h2, win_t, (13, 14, 15, 16, 4, 7, 10), 512))
    qkv, qkv_col0 = [gates], [2 * D]
    for gi in range(1, len(GROUPS)):
        qkv.append(_mm(h2p[gi], win_t, mode="nt", m=T, n=3 * AW, k=D, tm=ptm, tn=AW, tk=D, out_dtype=BF16,
                       b_map=lambda i, j, kk, gi=gi: (4 + gi + 3 * j, 0), name=f"proj_qkv{gi}"))
        qkv_col0.append(0)
    outs, lses = [], []
    for gi, (_, dil) in enumerate(GROUPS):
        o, l = _attn_fwd(qkv[gi], gi, f"attn_fwd{gi}", col0=qkv_col0[gi])
        outs.append(o)
        lses.append(l)
    attnb, lse = _merge(outs, lses, "attn_merge")
    x2, yc, ya, mixedb = _mix_out(z3b, attnb, gates, wc, wa_t, wo, x1, "mix_out_fwd")
    wg2, wu2, wd2 = _gather_wait(ag[2], GATHER_DIRECT, [x2], "gather_a_wait2")
    gg2, uu2, dx3, dgf, loss_part = _ffn_fwd(x2, ffn2_norm, wg2, wu2, wd2, "ffn2_fwd", loss_of=(gf, tgt))

    dx2, dg3, dgb, dub, actb, hb, dob = _ffn_bwd(x2, ffn2_norm, gg2, uu2, dx3, wg2, wu2, wd2, "ffn2_bwd")
    grads = {}
    grads["ffn2_w_down"] = _wgrad(actb, dob, FF, D, "ffn2_dwd")
    rs_groups = [("ffn2_w_gate", "ffn2_w_up", "ffn2_w_down"),
                 ("attn_w_out", "conv_w_out", "w_o", "conv_dw_kernel"),
                 ("w_in",),
                 ("ffn1_w_gate",), ("ffn1_w_up",), ("ffn1_w_down",), ()]
    last = len(rs_groups) - 1
    rs = []

    dgates, dycb, dyab, dx2b, dz3, dh_gates, dattnb, delta = _mix_out_bwd(dx2, gates, yc, ya, attnb, wc, wa_t, wo,
                                                                          win_t, "mix_out_bwd")
    grads["w_o"] = _wgrad(mixedb, dx2b, D, D, "dw_o")
    grads["conv_w_out"] = _wgrad(z3b, dycb, D, D, "dw_conv_out")
    grads["attn_w_out"] = _wgrad(dyab, attnb, D, AW, "dw_attn_out")
    dab, dkern, dvec, grads["ffn2_w_gate"], grads["ffn2_w_up"] = _conv_bwd(
        dz3, z1, ab, kern, conv_ln_gain, conv_ln_bias, "conv_bwd", guest_lhs=(dgb, dub), guest_rhs=hb)
    grads["conv_dw_kernel"] = dkern.reshape(32, NDEV, D // NDEV).transpose(1, 0, 2).reshape(NDEV * 32, D // NDEV)
    rs.append(_send_start(["scatter"] * 3, [grads[n] for n in rs_groups[0]], [], "scatter_start0"))
    rs.append(_send_start(["scatter"] * 4, [grads[n] for n in rs_groups[1]], [rs[0][4]], "scatter_start1"))
    dattnb = [_tie(a, [rs[1][4]], f"tie_after_scatter1_{i}") for i, a in enumerate(dattnb)]

    dqkv, dq3s = [], []
    for gi, (_, dil) in enumerate(GROUPS):
        dq3 = _attn_bwd(qkv[gi], dattnb[gi], lse[gi], delta[gi], gi, f"attn_bwd{gi}", col0=qkv_col0[gi])
        dq3s.append(dq3)
        dqkv.append(dq3.reshape(3 * T, AW))

    wtk = min(T, 2048)
    dwin = _mm(dab, h2, mode="tn", m=2 * D, n=D, k=T, tm=2 * D, tn=D, tk=wtk, out_dtype=BF16, out_rows=IN_W,
               name="dw_in_conv")
    dwin = _mm(dgates, h2, mode="tn", m=2 * D, n=D, k=T, tm=512, tn=D, tk=wtk, out_dtype=BF16, out_rows=IN_W,
               o_map=lambda i, j, kk: (13 + i, 0), passthru=dwin, name="dw_in_gates")
    for gi in range(3):
        dwin = _mm(dqkv[gi], h2p[gi], mode="tn", m=3 * AW, n=D, k=T, tm=AW, tn=D, tk=wtk, out_dtype=BF16,
                   out_rows=IN_W, a_map=lambda i, j, kk: (i * (T // wtk) + kk, 0),
                   o_map=lambda i, j, kk, gi=gi: (4 + gi + 3 * i, 0), passthru=dwin, name=f"dw_in_qkv{gi}")
    grads["w_in"] = dwin
    rs.append(_send_start(["scatter"], [dwin], [rs[1][4]], "scatter_start2"))
    dab = _tie(dab, [rs[2][4]], "tie_after_scatter2")

    dx1, dg2 = _rms_bwd(x1, mix_norm, dh_gates, dab, (0, 1, 2, 3), dq3s, win_t, dx2, "mix_norm_bwd")

    dgb, dub, actb, hb, dob = _ffn_bwd_pre(x0, ffn1_norm, gg1, uu1, dx1, wd1, "ffn1_bwd_pre")
    grads["ffn1_w_gate"] = _wgrad(dgb, hb, FF, D, "ffn1_dwg")
    rs.append(_send_start(["scatter"], [grads["ffn1_w_gate"]], [rs[2][4]], "scatter_start3"))
    hb = _tie(hb, [rs[3][4]], "tie_after_scatter3")
    grads["ffn1_w_up"] = _wgrad(dub, hb, FF, D, "ffn1_dwu")
    rs.append(_send_start(["scatter"], [grads["ffn1_w_up"]], [rs[3][4]], "scatter_start4"))
    dob = _tie(dob, [rs[4][4]], "tie_after_scatter4")
    grads["ffn1_w_down"] = _wgrad(actb, dob, FF, D, "ffn1_dwd")
    rs.append(_send_start(["scatter"], [grads["ffn1_w_down"]], [rs[4][4]], "scatter_start5"))
    dgb = _tie(dgb, [rs[5][4]], "tie_after_scatter5")
    dx0, dg1 = _ffn_bwd_dx(x0, ffn1_norm, dgb, dub, dx1, wg1, wu1, "ffn1_bwd_dx")
    vec = jnp.concatenate([dg1, dg2, dg3, dgf, dvec[0:3], jnp.broadcast_to(loss_part[:, :1], (1, D))], axis=0)
    rs.append(_send_start(["bcast"], [vec], [rs[5][4]], "scatter_start6"))

    g_out, d_out, m_out, v_out = {}, {}, {}, {}
    me = _my_place()
    after, done = [rs[last][4]], []
    for gi, grp in enumerate(rs_groups):
        kinds = ["scatter"] * len(grp) + (["bcast"] if gi == last else [])
        srcs, lands = _send_wait(kinds, rs[gi], after + (done if gi >= last - 1 else []), f"scatter_wait{gi}")
        for n, src, land in zip(grp, srcs, lands):
            if n == "conv_dw_kernel":
                rows = src.shape[0] // NDEV
                own = lax.dynamic_slice(src, (me * rows, 0), (rows, src.shape[1]))
                g = _gsum(own, land, f"gsum_{n}")[:CONV_W]
                d, m2, v2 = _adamw(w[n][0], g, mo[n][0], vo[n][0], f"adamw_{n}")
                after = [d]
                done.append(d)
                g, d, m2, v2 = g[None], d[None], m2[None], v2[None]
            else:
                res = _update(src, land, landing_view(w[n], n), landing_view(mo[n], n), landing_view(vo[n], n),
                              f"update_{n}")
                after = [res[1]]
                done.append(res[1])
                g, d, m2, v2 = (own_view(a, n) for a in res)
            g_out[n], d_out[n], m_out[n], v_out[n] = g, d, m2, v2
    vland = lands[-1]

    def rows8(src):
        return jnp.concatenate([src[n].reshape(1, D) for n in small] + [jnp.ones((1, D), F32)], axis=0)

    g8, d8, m8, v8 = _small_update(vland, rows8(w), rows8(mo), rows8(vo), "small_update")
    for r, n in enumerate(small):
        shp = w[n].shape
        g_out[n], d_out[n], m_out[n], v_out[n] = (a[r].reshape(shp) for a in (g8, d8, m8, v8))
    loss = g8[7, 0]

    return (loss, dx0[None], *[g_out[n] for n in names], *[d_out[n] for n in names],
            *[m_out[n] for n in names], *[v_out[n] for n in names])
```

```python
import numpy as np
import jax
import jax.numpy as jnp
from jax import lax
from jax.experimental import pallas as pl
from jax.experimental.pallas import tpu as pltpu

F32 = jnp.float32
BF16 = jnp.bfloat16

T = 4096
D = 1024
FF = 2816
NDEV = 8
CONV_W = 31
HEAD = 128
BLK = 128
GROUPS = ((128, 1), (512, 4), (2048, 16))
NHG = 4
AW = NHG * HEAD
IN_W = 2 * D + 3 * 3 * AW + 2 * D
EPS = 1e-6
B1, B2, LR, AEPS, WD, STEP = 0.9, 0.999, 0.001, 1e-08, 0.01, 10
NEG = -1e30
VMEM_LIMIT = 56 * 1024 * 1024
MESH_ID = pl.DeviceIdType.MESH

NT = (((1,), (1,)), ((), ()))
NN = (((1,), (0,)), ((), ()))
TN = (((0,), (0,)), ((), ()))
_DIMS = {"nn": NN, "nt": NT, "tn": TN}


def _cp(sem=None):
    return pltpu.CompilerParams(dimension_semantics=sem, vmem_limit_bytes=VMEM_LIMIT)


def _sig(v):
    return 1.0 / (1.0 + jnp.exp(-v))


def _dot(a, b, dims):
    return lax.dot_general(a, b, dims, preferred_element_type=F32)


def _const_spec(shape):
    nd = len(shape)
    return pl.BlockSpec(shape, lambda *_: (0,) * nd)


def _mm(a, b, *, mode, m, n, k, tm, tn, tk, out_dtype, name, a_map=None, b_map=None,
        o_map=None, out_rows=None, init=None, passthru=None):
    gi, gj, gk = m // tm, n // tn, k // tk
    assert gi * tm == m and gj * tn == n and gk * tk == k, (name, m, n, k, tm, tn, tk)
    if mode == "nn":
        a_blk, b_blk = (tm, tk), (tk, tn)
        da, db = (lambda i, j, kk: (i, kk)), (lambda i, j, kk: (kk, j))
    elif mode == "nt":
        a_blk, b_blk = (tm, tk), (tn, tk)
        da, db = (lambda i, j, kk: (i, kk)), (lambda i, j, kk: (j, kk))
    else:
        a_blk, b_blk = (tk, tm), (tk, tn)
        da, db = (lambda i, j, kk: (kk, i)), (lambda i, j, kk: (kk, j))
    a_map = a_map or da
    b_map = b_map or db
    o_map = o_map or (lambda i, j, kk: (i, j))
    dims = _DIMS[mode]
    extra = init if init is not None else passthru
    out_rows = out_rows or m

    def body(*refs):
        if init is not None:
            a_ref, b_ref, i_ref, o_ref = refs[:4]
        elif passthru is not None:
            a_ref, b_ref, _, o_ref = refs[:4]
        else:
            a_ref, b_ref, o_ref = refs[:3]
        if gk == 1:
            prod = _dot(a_ref[...], b_ref[...], dims)
            if init is not None:
                prod = prod + i_ref[...].astype(F32)
            o_ref[...] = prod.astype(out_dtype)
            return
        acc = refs[-1]
        kk = pl.program_id(2)

        @pl.when(kk == 0)
        def _():
            if init is not None:
                acc[...] = i_ref[...].astype(F32)
            else:
                acc[...] = jnp.zeros_like(acc)

        acc[...] += _dot(a_ref[...], b_ref[...], dims)

        @pl.when(kk == gk - 1)
        def _():
            o_ref[...] = acc[...].astype(out_dtype)

    in_specs = [pl.BlockSpec(a_blk, a_map), pl.BlockSpec(b_blk, b_map)]
    args = [a, b]
    aliases = {}
    if init is not None:
        in_specs.append(pl.BlockSpec((tm, tn), o_map))
        args.append(init)
        aliases = {2: 0}
    elif passthru is not None:
        in_specs.append(pl.BlockSpec(memory_space=pl.ANY))
        args.append(passthru)
        aliases = {2: 0}
    out_dt = extra.dtype if extra is not None else out_dtype
    assert out_dt == out_dtype
    return pl.pallas_call(
        body, name=name, grid=(gi, gj, gk),
        in_specs=in_specs, out_specs=pl.BlockSpec((tm, tn), o_map),
        out_shape=jax.ShapeDtypeStruct((out_rows, n), out_dtype),
        scratch_shapes=[pltpu.VMEM((tm, tn), F32)] if gk > 1 else [],
        input_output_aliases=aliases,
        compiler_params=_cp(("parallel", "parallel", "arbitrary")),
    )(*args)


def _ffn_fwd(x, g, wg_t, wu_t, wd, name, next_gain=None, loss_of=None):
    tm, fc = PERM_TM, 256
    nc = FF // fc
    n_in = 5 + (1 if next_gain is not None else 0) + (2 if loss_of is not None else 0)

    def body(*refs):
        x_ref, g_ref, wg_ref, wu_ref, wd_ref = refs[:5]
        extra_in, outs = refs[5:n_in], refs[n_in:]
        act_ref = outs[-1]
        xv = x_ref[...]
        r = lax.rsqrt(jnp.mean(xv * xv, axis=-1, keepdims=True) + EPS)
        h = (xv * r * g_ref[...]).astype(BF16)
        gg_ref, uu_ref = (outs[0], outs[1]) if loss_of is not None else (outs[1], outs[2])
        for c in range(nc):
            sl = pl.ds(c * fc, fc)
            gg = _dot(h, wg_ref[sl, :], NT)
            uu = _dot(h, wu_ref[sl, :], NT)
            gg_ref[:, sl] = gg.astype(BF16)
            uu_ref[:, sl] = uu.astype(BF16)
            act_ref[:, sl] = (gg * _sig(gg) * uu).astype(BF16)
        y = xv + 0.5 * _dot(act_ref[...], wd_ref[...], NN)
        if loss_of is not None:
            _final_math(y, extra_in[0][...], extra_in[1][...], outs[2], outs[3], outs[4], pl.program_id(0))
            return
        outs[0][...] = y
        if next_gain is not None:
            tile = outs[-2]
            r2 = lax.rsqrt(jnp.mean(y * y, axis=-1, keepdims=True) + EPS)
            hv = y * r2 * extra_in[0][...]
            outs[3][...] = hv.astype(BF16)
            _put_tile(tile, hv)
            for dil, p_ref in zip(DILS, outs[4:4 + len(DILS)]):
                _store_perm(p_ref, tile, dil)

    wspec = pl.BlockSpec((FF, D), lambda i: (0, 0), pipeline_mode=pl.Buffered(1))
    row_d = pl.BlockSpec((tm, D), lambda i: (i, 0))
    row_f = pl.BlockSpec((tm, FF), lambda i: (i, 0))
    in_specs = [row_d, _const_spec((1, D)), wspec, wspec, wspec]
    args = [x, g, wg_t, wu_t, wd]
    f_shape = jax.ShapeDtypeStruct((T, FF), BF16)
    scratch = [pltpu.VMEM((tm, FF), BF16)]
    if loss_of is not None:
        in_specs += [_const_spec((1, D)), row_d]
        args += list(loss_of)
        out_specs = [row_f, row_f, row_d, _const_spec((1, D)), _const_spec((1, 128))]
        out_shape = [f_shape, f_shape, jax.ShapeDtypeStruct((T, D), F32), jax.ShapeDtypeStruct((1, D), F32),
                     jax.ShapeDtypeStruct((1, 128), F32)]
    else:
        out_specs = [row_d, row_f, row_f]
        out_shape = [jax.ShapeDtypeStruct((T, D), F32), f_shape, f_shape]
        if next_gain is not None:
            in_specs.append(_const_spec((1, D)))
            args.append(next_gain)
            out_specs += [row_d] + [_perm_spec(d, D) for d in DILS]
            out_shape += [jax.ShapeDtypeStruct((T, D), BF16)] + [_perm_shape(d, D, BF16) for d in DILS]
            scratch = [_tile_scratch(D)] + scratch
    out = pl.pallas_call(
        body, name=name, grid=(T // tm,), in_specs=in_specs, out_specs=out_specs, out_shape=out_shape,
        scratch_shapes=scratch,
        compiler_params=_cp(("arbitrary",) if loss_of is not None else ("parallel",)),
    )(*args)
    if next_gain is not None:
        return out[0], out[1], out[2], [out[3]] + [o.reshape(T, D) for o in out[4:]]
    return tuple(out)


def _ffn_bwd(x, g, gg_all, uu_all, dout, wg_t, wu_t, wd, name):
    tm, fc = 256, 256
    nc = FF // fc

    def body(x_ref, g_ref, gg_ref, uu_ref, do_ref, wg_ref, wu_ref, wd_ref,
             dx_ref, dgam_ref, dg_ref, du_ref, act_ref, h_ref, db_ref):
        i = pl.program_id(0)
        xv = x_ref[...]
        r = lax.rsqrt(jnp.mean(xv * xv, axis=-1, keepdims=True) + EPS)
        xhat = xv * r
        gam = g_ref[...]
        h_ref[...] = (xhat * gam).astype(BF16)
        dov = do_ref[...]
        dbv = (0.5 * dov).astype(BF16)
        db_ref[...] = dbv
        for c in range(nc):
            sl = pl.ds(c * fc, fc)
            da = _dot(dbv, wd_ref[sl, :], NT)
            gg = gg_ref[:, sl].astype(F32)
            uu = uu_ref[:, sl].astype(F32)
            s = _sig(gg)
            si = gg * s
            dgv = (da * uu * (s * (1.0 + gg * (1.0 - s)))).astype(BF16)
            duv = (da * si).astype(BF16)
            dg_ref[:, sl] = dgv
            du_ref[:, sl] = duv
            act_ref[:, sl] = (si * uu).astype(BF16)
        dh = _dot(dg_ref[...], wg_ref[...], NN) + _dot(du_ref[...], wu_ref[...], NN)

        @pl.when(i == 0)
        def _():
            dgam_ref[...] = jnp.zeros_like(dgam_ref)

        dgam_ref[...] += jnp.sum(dh * xhat, axis=0, keepdims=True)
        dxh = dh * gam
        dx_ref[...] = dov + r * (dxh - xhat * jnp.mean(dxh * xhat, axis=-1, keepdims=True))

    wspec = pl.BlockSpec((FF, D), lambda i: (0, 0), pipeline_mode=pl.Buffered(1))
    row_d = pl.BlockSpec((tm, D), lambda i: (i, 0))
    row_f = pl.BlockSpec((tm, FF), lambda i: (i, 0))
    return pl.pallas_call(
        body, name=name, grid=(T // tm,),
        in_specs=[row_d, _const_spec((1, D)), row_f, row_f, row_d, wspec, wspec, wspec],
        out_specs=[row_d, _const_spec((1, D)), row_f, row_f, row_f, row_d, row_d],
        out_shape=[jax.ShapeDtypeStruct((T, D), F32), jax.ShapeDtypeStruct((1, D), F32),
                   jax.ShapeDtypeStruct((T, FF), BF16), jax.ShapeDtypeStruct((T, FF), BF16),
                   jax.ShapeDtypeStruct((T, FF), BF16), jax.ShapeDtypeStruct((T, D), BF16),
                   jax.ShapeDtypeStruct((T, D), BF16)],
        compiler_params=_cp(("arbitrary",)),
    )(x, g, gg_all, uu_all, dout, wg_t, wu_t, wd)


def _ffn_bwd_pre(x, g, gg_all, uu_all, dout, wd, name):
    tm, fc = 512, 256
    nc = FF // fc

    def body(x_ref, g_ref, gg_ref, uu_ref, do_ref, wd_ref, dg_ref, du_ref, act_ref, h_ref, db_ref):
        xv = x_ref[...]
        r = lax.rsqrt(jnp.mean(xv * xv, axis=-1, keepdims=True) + EPS)
        h_ref[...] = (xv * r * g_ref[...]).astype(BF16)
        dbv = (0.5 * do_ref[...]).astype(BF16)
        db_ref[...] = dbv
        for c in range(nc):
            sl = pl.ds(c * fc, fc)
            da = _dot(dbv, wd_ref[sl, :], NT)
            gg = gg_ref[:, sl].astype(F32)
            uu = uu_ref[:, sl].astype(F32)
            s = _sig(gg)
            si = gg * s
            dg_ref[:, sl] = (da * uu * (s * (1.0 + gg * (1.0 - s)))).astype(BF16)
            du_ref[:, sl] = (da * si).astype(BF16)
            act_ref[:, sl] = (si * uu).astype(BF16)

    wspec = pl.BlockSpec((FF, D), lambda i: (0, 0), pipeline_mode=pl.Buffered(1))
    row_d = pl.BlockSpec((tm, D), lambda i: (i, 0))
    row_f = pl.BlockSpec((tm, FF), lambda i: (i, 0))
    return pl.pallas_call(
        body, name=name, grid=(T // tm,),
        in_specs=[row_d, _const_spec((1, D)), row_f, row_f, row_d, wspec],
        out_specs=[row_f, row_f, row_f, row_d, row_d],
        out_shape=[jax.ShapeDtypeStruct((T, FF), BF16), jax.ShapeDtypeStruct((T, FF), BF16),
                   jax.ShapeDtypeStruct((T, FF), BF16), jax.ShapeDtypeStruct((T, D), BF16),
                   jax.ShapeDtypeStruct((T, D), BF16)],
        compiler_params=_cp(("parallel",)),
    )(x, g, gg_all, uu_all, dout, wd)


def _ffn_bwd_dx(x, g, dgb, dub, dout, wg_t, wu_t, name):
    tm = 512

    def body(x_ref, g_ref, dg_ref, du_ref, do_ref, wg_ref, wu_ref, dx_ref, dgam_ref):
        i = pl.program_id(0)
        xv = x_ref[...]
        r = lax.rsqrt(jnp.mean(xv * xv, axis=-1, keepdims=True) + EPS)
        xhat = xv * r
        gam = g_ref[...]
        dh = _dot(dg_ref[...], wg_ref[...], NN) + _dot(du_ref[...], wu_ref[...], NN)

        @pl.when(i == 0)
        def _():
            dgam_ref[...] = jnp.zeros_like(dgam_ref)

        dgam_ref[...] += jnp.sum(dh * xhat, axis=0, keepdims=True)
        dxh = dh * gam
        dx_ref[...] = do_ref[...] + r * (dxh - xhat * jnp.mean(dxh * xhat, axis=-1, keepdims=True))

    wspec = pl.BlockSpec((FF, D), lambda i: (0, 0), pipeline_mode=pl.Buffered(1))
    row_d = pl.BlockSpec((tm, D), lambda i: (i, 0))
    row_f = pl.BlockSpec((tm, FF), lambda i: (i, 0))
    return pl.pallas_call(
        body, name=name, grid=(T // tm,),
        in_specs=[row_d, _const_spec((1, D)), row_f, row_f, row_d, wspec, wspec],
        out_specs=[row_d, _const_spec((1, D))],
        out_shape=[jax.ShapeDtypeStruct((T, D), F32), jax.ShapeDtypeStruct((1, D), F32)],
        compiler_params=_cp(("arbitrary",)),
    )(x, g, dgb, dub, dout, wg_t, wu_t)


def _wgrad(a, b, m, n, name):
    tm = m // 2 if m == FF else m
    return _mm(a, b, mode="tn", m=m, n=n, k=T, tm=tm, tn=n, tk=min(T, 2048), out_dtype=BF16, name=name)


PERM_TM = 512
DILS = tuple(d for _, d in GROUPS if d > 1)


def _perm_spec(dil, cols):
    return pl.BlockSpec((dil, PERM_TM // dil, cols), lambda i: (0, i, 0))


def _perm_shape(dil, cols, dtype):
    return jax.ShapeDtypeStruct((dil, T // dil, cols), dtype)


LANES = 128


def _tile_scratch(cols):
    return pltpu.VMEM((cols // LANES, PERM_TM, LANES), F32)


def _put_tile(tile, value):
    for c in range(tile.shape[0]):
        tile[c] = value[:, c * LANES:(c + 1) * LANES]


def _get_tile(tile):
    return jnp.concatenate([tile[c] for c in range(tile.shape[0])], axis=1)


def _store_perm(out_ref, tile, dil):
    for r in range(dil):
        for c in range(tile.shape[0]):
            out_ref[r, :, pl.ds(c * LANES, LANES)] = tile[c, pl.ds(r, PERM_TM // dil, stride=dil), :].astype(
                out_ref.dtype)


def _load_unperm(in_ref, tile, dil):
    for r in range(dil):
        for c in range(tile.shape[0]):
            tile[c, pl.ds(r, PERM_TM // dil, stride=dil), :] = in_ref[r, :, pl.ds(c * LANES, LANES)].astype(F32)


def _final_math(xv, gam, tgt, dx_ref, dgam_ref, loss_ref, i):
    r = lax.rsqrt(jnp.mean(xv * xv, axis=-1, keepdims=True) + EPS)
    xhat = xv * r
    err = xhat * gam - tgt
    part = 0.5 * jnp.sum(jnp.mean(err * err, axis=-1, keepdims=True), axis=0, keepdims=True)
    dy = err * (1.0 / D)

    @pl.when(i == 0)
    def _():
        dgam_ref[...] = jnp.zeros_like(dgam_ref)
        loss_ref[...] = jnp.zeros_like(loss_ref)

    dgam_ref[...] += jnp.sum(dy * xhat, axis=0, keepdims=True)
    loss_ref[...] += jnp.broadcast_to(part, loss_ref.shape)
    dxh = dy * gam
    dx_ref[...] = r * (dxh - xhat * jnp.mean(dxh * xhat, axis=-1, keepdims=True))


QKV_BLOCK0 = 2 * D // AW


def _rms_bwd(x, g, dh0, dlin, lin_blocks, dqkvs, win_t, dres, name):
    tm = PERM_TM
    dils = [d for _, d in GROUPS]
    ng = len(dils)
    nl = len(lin_blocks)
    assert len(dqkvs) == ng

    def body(*refs):
        x_ref, g_ref, dh0_ref, dl_ref = refs[:4]
        dq_refs = refs[4:4 + 3 * ng]
        w_refs = refs[4 + 3 * ng:4 + 6 * ng]
        wl_refs = refs[4 + 6 * ng:4 + 6 * ng + nl]
        dr_ref, dx_ref, dgam_ref = refs[4 + 6 * ng + nl:7 + 6 * ng + nl]
        tile = refs[7 + 6 * ng + nl]
        stages = refs[8 + 6 * ng + nl:]
        i = pl.program_id(0)
        xv = x_ref[...]
        r = lax.rsqrt(jnp.mean(xv * xv, axis=-1, keepdims=True) + EPS)
        xhat = xv * r
        gam = g_ref[...]
        dh = dh0_ref[...]
        for q in range(nl):
            dh = dh + _dot(dl_ref[:, pl.ds(q * AW, AW)], wl_refs[q][...], NN)
        si = 0
        for gi, dil in enumerate(dils):
            part = None
            for p in range(3):
                blk = dq_refs[3 * gi + p][...]
                term = _dot(blk.reshape(tm, AW), w_refs[3 * gi + p][...], NN)
                part = term if part is None else part + term
            if dil > 1:
                stage = stages[si]
                si += 1
                stage[...] = part.reshape(dil, tm // dil, D)
                _load_unperm(stage, tile, dil)
                part = _get_tile(tile)
            dh = dh + part

        @pl.when(i == 0)
        def _():
            dgam_ref[...] = jnp.zeros_like(dgam_ref)

        dgam_ref[...] += jnp.sum(dh * xhat, axis=0, keepdims=True)
        dxh = dh * gam
        dx_ref[...] = dr_ref[...] + r * (dxh - xhat * jnp.mean(dxh * xhat, axis=-1, keepdims=True))

    row_d = pl.BlockSpec((tm, D), lambda i: (i, 0))
    dq_specs, dq_args, w_specs = [], [], []
    for gi, (d, a) in enumerate(zip(dils, dqkvs)):
        for p in range(3):
            if d == 1:
                dq_specs.append(pl.BlockSpec((None, tm, AW), lambda i, p=p: (p, i, 0)))
                dq_args.append(a)
            else:
                dq_specs.append(pl.BlockSpec((None, d, tm // d, AW), lambda i, p=p: (p, 0, i, 0)))
                dq_args.append(a.reshape(3, d, T // d, AW))
            w_specs.append(pl.BlockSpec((AW, D), lambda i, q=QKV_BLOCK0 + gi + 3 * p: (q, 0),
                                        pipeline_mode=pl.Buffered(1)))
    return pl.pallas_call(
        body, name=name, grid=(T // tm,),
        in_specs=[row_d, _const_spec((1, D)), row_d, pl.BlockSpec((tm, nl * AW), lambda i: (i, 0))] + dq_specs + w_specs
        + [pl.BlockSpec((AW, D), lambda i, q=q: (q, 0), pipeline_mode=pl.Buffered(1)) for q in lin_blocks] + [row_d],
        out_specs=[row_d, _const_spec((1, D))],
        out_shape=[jax.ShapeDtypeStruct((T, D), F32), jax.ShapeDtypeStruct((1, D), F32)],
        scratch_shapes=[_tile_scratch(D)] + [pltpu.VMEM((d, tm // d, D), F32) for d in dils if d > 1],
        compiler_params=_cp(("arbitrary",)),
    )(x, g, dh0, dlin, *dq_args, *([win_t] * (3 * ng + nl)), dres)


CONV_TM = 256
CONV_HALO = 32
CONV_RB = 16


def _glu(ab):
    ab = ab.astype(F32)
    return ab[:, :D] * _sig(ab[:, D:])


def _ln_stats(z1):
    mu = jnp.mean(z1, axis=-1, keepdims=True)
    zc = z1 - mu
    rstd = lax.rsqrt(jnp.mean(zc * zc, axis=-1, keepdims=True) + EPS)
    return zc * rstd, rstd


def _fill_shifts(zs):
    n = zs.shape[1] - 8
    for s in range(1, 8):
        zs[s, pl.ds(0, n), :] = zs[0, pl.ds(s, n), :]


def _shifted(zs, start, rows):
    q, s = divmod(start, 8)
    return zs[s, pl.ds(8 * q, rows), :]


def _conv_fwd(ab, kern, dwb, lng, lnb, name, guest=None):
    tm, hl, rb = CONV_TM, CONV_HALO, CONV_RB
    off = hl - (CONV_W - 1)
    if guest is not None:
        g_a, g_b, g_blocks, g_rows = guest
        g_nblk = len(g_blocks)

    def body(ab_ref, abh_ref, k_ref, dwb_ref, lng_ref, lnb_ref, *rest):
        if guest is not None:
            ga_ref, gb_refs, rest = rest[0], rest[1:1 + g_nblk], rest[1 + g_nblk:]
            z1_ref, z3_ref, go_ref, zs = rest
            for q, gb_ref in enumerate(gb_refs):
                go_ref[:, pl.ds(q * g_rows, g_rows)] = _dot(ga_ref[...], gb_ref[...], NT).astype(BF16)
        else:
            z1_ref, z3_ref, zs = rest
        i = pl.program_id(0)
        zs[0, pl.ds(0, hl), :] = jnp.where(i > 0, _glu(abh_ref[...]), 0.0)
        zs[0, pl.ds(hl, tm), :] = _glu(ab_ref[...])
        _fill_shifts(zs)
        for b in range(tm // rb):
            acc = jnp.zeros((rb, D), F32)
            for j in range(CONV_W):
                acc = acc + _shifted(zs, b * rb + off + j, rb) * k_ref[pl.ds(j, 1), :]
            z1 = acc + dwb_ref[...]
            z1_ref[pl.ds(b * rb, rb), :] = z1
            zn, _ = _ln_stats(z1)
            z2 = zn * lng_ref[...] + lnb_ref[...]
            z3_ref[pl.ds(b * rb, rb), :] = (z2 * _sig(z2)).astype(BF16)

    row = pl.BlockSpec((tm, D), lambda i: (i, 0))
    g_specs, g_args, g_ospecs, g_oshapes = [], [], [], []
    if guest is not None:
        kdim = g_a.shape[1]
        g_specs = [pl.BlockSpec((tm, kdim), lambda i: (i, 0))]
        g_specs += [pl.BlockSpec((g_rows, kdim), lambda i, q=q: (q, 0), pipeline_mode=pl.Buffered(1))
                    for q in g_blocks]
        g_args = [g_a] + [g_b] * g_nblk
        g_ospecs = [pl.BlockSpec((tm, g_nblk * g_rows), lambda i: (i, 0))]
        g_oshapes = [jax.ShapeDtypeStruct((T, g_nblk * g_rows), BF16)]
    return pl.pallas_call(
        body, name=name, grid=(T // tm,),
        in_specs=[pl.BlockSpec((tm, 2 * D), lambda i: (i, 0)),
                  pl.BlockSpec((hl, 2 * D), lambda i: (jnp.maximum(i * (tm // hl) - 1, 0), 0)),
                  _const_spec((32, D)), _const_spec((1, D)), _const_spec((1, D)), _const_spec((1, D))] + g_specs,
        out_specs=[row, row] + g_ospecs,
        out_shape=[jax.ShapeDtypeStruct((T, D), F32), jax.ShapeDtypeStruct((T, D), BF16)] + g_oshapes,
        scratch_shapes=[pltpu.VMEM((8, hl + tm, D), F32)],
        compiler_params=_cp(("parallel",)),
    )(ab, ab, kern, dwb, lng, lnb, *g_args)


GUEST_TM = 256


def _conv_bwd(dz3, z1, ab, kern, lng, lnb, name, guest_lhs=(), guest_rhs=None):
    tm, hl, rb = CONV_TM, CONV_HALO, CONV_RB
    off = hl - (CONV_W - 1)
    nsteps = T // tm
    ng = len(guest_lhs)
    gblocks = [a.shape[1] // GUEST_TM for a in guest_lhs]
    assert all(gb <= nsteps and gb * GUEST_TM == a.shape[1] for gb, a in zip(gblocks, guest_lhs))

    def ln_bwd(dz3v, z1v, lngv, lnbv):
        zn, rstd = _ln_stats(z1v)
        z2 = zn * lngv + lnbv
        s = _sig(z2)
        dz2 = dz3v * (s * (1.0 + z2 * (1.0 - s)))
        dzn = dz2 * lngv
        dz1 = rstd * (dzn - jnp.mean(dzn, axis=-1, keepdims=True)
                      - zn * jnp.mean(dzn * zn, axis=-1, keepdims=True))
        return dz1, dz2, zn

    def body(dz3_ref, dz3h_ref, z1_ref, z1h_ref, ab_ref, abh_ref, k_ref, lng_ref, lnb_ref, *rest):
        g_in, rest = rest[:ng + (1 if ng else 0)], rest[ng + (1 if ng else 0):]
        dab_ref, dk_ref, dvec_ref = rest[:3]
        g_out, (zs, dzs) = rest[3:3 + ng], rest[3 + ng:]
        i = pl.program_id(0)
        lngv, lnbv = lng_ref[...], lnb_ref[...]

        for a_ref, o_ref, gb in zip(g_in[:ng], g_out, gblocks):
            @pl.when(i < gb)
            def _(a_ref=a_ref, o_ref=o_ref):
                o_ref[...] = _dot(a_ref[...], g_in[ng][...], TN).astype(BF16)

        @pl.when(i == 0)
        def _():
            dk_ref[...] = jnp.zeros_like(dk_ref)
            dvec_ref[...] = jnp.zeros_like(dvec_ref)

        dz1, dz2, zn = ln_bwd(dz3_ref[...].astype(F32), z1_ref[...], lngv, lnbv)
        dvec_ref[pl.ds(0, 1), :] += jnp.sum(dz1, axis=0, keepdims=True)
        dvec_ref[pl.ds(1, 1), :] += jnp.sum(dz2 * zn, axis=0, keepdims=True)
        dvec_ref[pl.ds(2, 1), :] += jnp.sum(dz2, axis=0, keepdims=True)
        dzs[0, pl.ds(0, tm), :] = dz1
        dz1h, _, _ = ln_bwd(dz3h_ref[...].astype(F32), z1h_ref[...], lngv, lnbv)
        dzs[0, pl.ds(tm, hl), :] = jnp.where(i < nsteps - 1, dz1h, 0.0)
        _fill_shifts(dzs)
        zs[0, pl.ds(0, hl), :] = jnp.where(i > 0, _glu(abh_ref[...]), 0.0)
        zs[0, pl.ds(hl, tm), :] = _glu(ab_ref[...])
        _fill_shifts(zs)

        for j in range(CONV_W):
            tot = jnp.zeros((rb, D), F32)
            for b in range(tm // rb):
                tot = tot + dzs[0, pl.ds(b * rb, rb), :] * _shifted(zs, b * rb + off + j, rb)
            dk_ref[pl.ds(j, 1), :] += jnp.sum(tot, axis=0, keepdims=True)

        for b in range(tm // rb):
            acc = jnp.zeros((rb, D), F32)
            for j in range(CONV_W):
                acc = acc + _shifted(dzs, b * rb + (CONV_W - 1) - j, rb) * k_ref[pl.ds(j, 1), :]
            av = ab_ref[pl.ds(b * rb, rb), pl.ds(0, D)].astype(F32)
            sb = _sig(ab_ref[pl.ds(b * rb, rb), pl.ds(D, D)].astype(F32))
            dab_ref[pl.ds(b * rb, rb), pl.ds(0, D)] = (acc * sb).astype(BF16)
            dab_ref[pl.ds(b * rb, rb), pl.ds(D, D)] = (acc * av * sb * (1.0 - sb)).astype(BF16)

    row = pl.BlockSpec((tm, D), lambda i: (i, 0))
    nxt = pl.BlockSpec((hl, D), lambda i: (jnp.minimum((i + 1) * (tm // hl), T // hl - 1), 0))
    g_specs, g_args, g_ospecs, g_oshapes = [], [], [], []
    for a, gb in zip(guest_lhs, gblocks):
        g_specs.append(pl.BlockSpec((T, GUEST_TM), lambda i, gb=gb: (0, jnp.minimum(i, gb - 1))))
        g_args.append(a)
        g_ospecs.append(pl.BlockSpec((GUEST_TM, guest_rhs.shape[1]), lambda i, gb=gb: (jnp.minimum(i, gb - 1), 0)))
        g_oshapes.append(jax.ShapeDtypeStruct((a.shape[1], guest_rhs.shape[1]), BF16))
    if ng:
        g_specs.append(pl.BlockSpec(guest_rhs.shape, lambda i: (0, 0), pipeline_mode=pl.Buffered(1)))
        g_args.append(guest_rhs)
    return pl.pallas_call(
        body, name=name, grid=(nsteps,),
        in_specs=[row, nxt, row, nxt,
                  pl.BlockSpec((tm, 2 * D), lambda i: (i, 0)),
                  pl.BlockSpec((hl, 2 * D), lambda i: (jnp.maximum(i * (tm // hl) - 1, 0), 0)),
                  _const_spec((32, D)), _const_spec((1, D)), _const_spec((1, D))] + g_specs,
        out_specs=[pl.BlockSpec((tm, 2 * D), lambda i: (i, 0)), _const_spec((32, D)), _const_spec((8, D))]
        + g_ospecs,
        out_shape=[jax.ShapeDtypeStruct((T, 2 * D), BF16), jax.ShapeDtypeStruct((32, D), F32),
                   jax.ShapeDtypeStruct((8, D), F32)] + g_oshapes,
        scratch_shapes=[pltpu.VMEM((8, hl + tm, D), F32), pltpu.VMEM((8, tm + hl, D), F32)],
        compiler_params=_cp(("arbitrary",)),
    )(dz3, dz3, z1, z1, ab, ab, kern, lng, lnb, *g_args)


def _alibi_slopes():
    h = np.arange(1, 3 * NHG + 1, dtype=np.float32)
    return np.power(np.float32(2.0), -8.0 * h / np.float32(3 * NHG)).astype(np.float32)


def _band_bias(gi):
    _, dil = GROUPS[gi]
    slopes = _alibi_slopes()[gi * NHG:(gi + 1) * NHG]
    qi = np.arange(BLK)[:, None]
    ki = np.arange(2 * BLK)[None, :]
    steps = BLK + qi - ki
    band = (steps >= 0) & (steps <= BLK)
    bias = -slopes[:, None, None] * (dil * steps).astype(np.float32)[None]
    return jnp.asarray(np.where(band[None], bias, np.float32(NEG)).astype(np.float32))


QB_FWD = 8
QB_BWD = 32


def _attn_specs(qb, c0):
    prev = lambda n: jnp.maximum(n * qb - 1, 0)
    return [pl.BlockSpec((qb * BLK, HEAD), lambda h, n: (n, c0 + h)),
            pl.BlockSpec((BLK, HEAD), lambda h, n: (prev(n), c0 + NHG + h)),
            pl.BlockSpec((qb * BLK, HEAD), lambda h, n: (n, c0 + NHG + h)),
            pl.BlockSpec((BLK, HEAD), lambda h, n: (prev(n), c0 + 2 * NHG + h)),
            pl.BlockSpec((qb * BLK, HEAD), lambda h, n: (n, c0 + 2 * NHG + h)),
            pl.BlockSpec((None, BLK, 2 * BLK), lambda h, n: (h, 0, 0))]


def _scores(q, kcat, bias, blk, seg):
    s = _dot(q, kcat, NT) * (HEAD ** -0.5) + bias
    col = lax.broadcasted_iota(jnp.int32, s.shape, 1)
    first = (blk % seg) == 0
    return jnp.where(jnp.logical_and(first, col < BLK), NEG, s)


def _attn_fwd(qkv, gi, name, col0=0):
    seg = (T // GROUPS[gi][1]) // BLK

    qb = min(QB_FWD, T // BLK)

    def body(q_ref, kp_ref, kc_ref, vp_ref, vc_ref, bias_ref, o_ref, l_ref):
        n = pl.program_id(0)
        for h in range(NHG):
            cols = pl.ds(h * HEAD, HEAD)
            kwin = jnp.concatenate([kp_ref[:, cols], kc_ref[:, cols]], axis=0)
            vwin = jnp.concatenate([vp_ref[:, cols], vc_ref[:, cols]], axis=0)
            bias = bias_ref[h]
            for b in range(qb):
                rows = pl.ds(b * BLK, BLK)
                s = _scores(q_ref[rows, cols], kwin[b * BLK:(b + 2) * BLK], bias, n * qb + b, seg)
                mx = jnp.max(s, axis=-1, keepdims=True)
                p = jnp.exp(s - mx)
                den = jnp.sum(p, axis=-1, keepdims=True)
                o_ref[rows, cols] = (_dot(p.astype(BF16), vwin[b * BLK:(b + 2) * BLK], NN) / den).astype(BF16)
                l_ref[rows, cols] = jnp.broadcast_to(mx + jnp.log(den), (BLK, HEAD))

    prev = lambda n: jnp.maximum(n * qb - 1, 0)
    c0 = col0 // AW
    cur = lambda part: pl.BlockSpec((qb * BLK, AW), lambda n: (n, part))
    halo = lambda part: pl.BlockSpec((BLK, AW), lambda n: (prev(n), part))
    return pl.pallas_call(
        body, name=name, grid=(T // (qb * BLK),),
        in_specs=[cur(c0), halo(c0 + 1), cur(c0 + 1), halo(c0 + 2), cur(c0 + 2),
                  _const_spec((NHG, BLK, 2 * BLK))],
        out_specs=[cur(0), cur(0)],
        out_shape=[jax.ShapeDtypeStruct((T, AW), BF16), jax.ShapeDtypeStruct((T, AW), F32)],
        compiler_params=_cp(("parallel",)),
    )(qkv, qkv, qkv, qkv, qkv, _band_bias(gi))


def _attn_bwd(qkv, dob, lse, delta, gi, name, col0=0):
    seg = (T // GROUPS[gi][1]) // BLK
    qb = min(QB_BWD, T // BLK)
    nb = T // (qb * BLK)
    scale = HEAD ** -0.5

    def body(q_ref, kp_ref, kc_ref, vp_ref, vc_ref, bias_ref, do_ref, l_ref, dl_ref, out_ref, dk_acc, dv_acc):
        n = pl.program_id(1)
        kwin = jnp.concatenate([kp_ref[...], kc_ref[...]], axis=0)
        vwin = jnp.concatenate([vp_ref[...], vc_ref[...]], axis=0)
        bias = bias_ref[...]
        mine = lax.broadcasted_iota(jnp.int32, (BLK, LANES), 1) // (LANES // NHG) == pl.program_id(0)
        dks, dvs = [], []
        for b in range(qb):
            rows = pl.ds(b * BLK, BLK)
            q = q_ref[rows, :]
            kcat = kwin[b * BLK:(b + 2) * BLK]
            s = _scores(q, kcat, bias, n * qb + b, seg)
            p = jnp.exp(s - jnp.max(jnp.where(mine, l_ref[rows, :], -jnp.inf), axis=-1, keepdims=True))
            dov = do_ref[rows, :]
            dvs.append(_dot(p.astype(BF16), dov, TN))
            dp = _dot(dov, vwin[b * BLK:(b + 2) * BLK], NT)
            dsb = (p * (dp - dl_ref[rows, pl.ds(0, 1)]) * scale).astype(BF16)
            row = pl.ds(pl.multiple_of((n * qb + b) * BLK, BLK), BLK)
            out_ref[0, row, :] = _dot(dsb, kcat, NN).astype(BF16)
            dks.append(_dot(dsb, q, TN))
        for b in range(qb):
            row = pl.ds(pl.multiple_of((n * qb + b) * BLK, BLK), BLK)
            if b + 1 < qb:
                dk_acc[row, :] = dks[b][BLK:] + dks[b + 1][:BLK]
                dv_acc[row, :] = dvs[b][BLK:] + dvs[b + 1][:BLK]
            else:
                dk_acc[row, :] = dks[b][BLK:]
                dv_acc[row, :] = dvs[b][BLK:]

        @pl.when(n > 0)
        def _():
            prow = pl.ds(pl.multiple_of((n * qb - 1) * BLK, BLK), BLK)
            dk_acc[prow, :] += dks[0][:BLK]
            dv_acc[prow, :] += dvs[0][:BLK]

        @pl.when(n == nb - 1)
        def _():
            out_ref[1] = dk_acc[...].astype(BF16)
            out_ref[2] = dv_acc[...].astype(BF16)

    oblk = pl.BlockSpec((qb * BLK, HEAD), lambda h, n: (n, h))
    return pl.pallas_call(
        body, name=name, grid=(NHG, nb),
        in_specs=_attn_specs(qb, col0 // HEAD) + [oblk, pl.BlockSpec((qb * BLK, LANES), lambda h, n: (n, 0)), oblk],
        out_specs=pl.BlockSpec((3, T, HEAD), lambda h, n: (0, 0, h)),
        out_shape=jax.ShapeDtypeStruct((3, T, AW), BF16),
        scratch_shapes=[pltpu.VMEM((T, HEAD), F32), pltpu.VMEM((T, HEAD), F32)],
        compiler_params=_cp(("parallel", "arbitrary")),
    )(qkv, qkv, qkv, qkv, qkv, _band_bias(gi), dob, lse, delta)


def _merge(outs, lses, name):
    tm = PERM_TM
    dils = [d for _, d in GROUPS]
    ng = len(dils)

    def body(*refs):
        in_refs = refs[:2 * ng]
        ab_ref = refs[2 * ng]
        lse_refs = refs[2 * ng + 1:3 * ng + 1]
        tile, small = refs[-2:]

        def token_order(ref, dil):
            if dil == 1:
                return ref[...].astype(F32)
            _load_unperm(ref, tile, dil)
            return _get_tile(tile)

        os = [token_order(in_refs[2 * i], d) for i, d in enumerate(dils)]
        ls = [token_order(in_refs[2 * i + 1], d) for i, d in enumerate(dils)]
        mx = jnp.maximum(jnp.maximum(ls[0], ls[1]), ls[2])
        es = [jnp.exp(v - mx) for v in ls]
        tot = es[0] + es[1] + es[2]
        att = (es[0] / tot) * os[0] + (es[1] / tot) * os[1] + (es[2] / tot) * os[2]
        ab_ref[...] = att.astype(BF16)
        lse = mx + jnp.log(tot)
        lane = lax.broadcasted_iota(jnp.int32, (tm, LANES), 1)
        packed = lse[:, :HEAD]
        for h in range(1, NHG):
            packed = jnp.where(lane >= h * (LANES // NHG), lse[:, h * HEAD:(h + 1) * HEAD], packed)
        small[0] = packed
        for dil, ref in zip(dils, lse_refs):
            if dil == 1:
                ref[...] = packed
            else:
                _store_perm(ref, small, dil)

    row = pl.BlockSpec((tm, AW), lambda i: (i, 0))
    specs = [row if d == 1 else _perm_spec(d, AW) for d in dils]
    lspecs = [pl.BlockSpec((tm, LANES), lambda i: (i, 0)) if d == 1 else _perm_spec(d, LANES) for d in dils]
    args = []
    for d, o, l in zip(dils, outs, lses):
        args += [o, l] if d == 1 else [o.reshape(d, T // d, AW), l.reshape(d, T // d, AW)]
    out = pl.pallas_call(
        body, name=name, grid=(T // tm,),
        in_specs=[sp for sp in specs for _ in range(2)], out_specs=[row] + lspecs,
        out_shape=[jax.ShapeDtypeStruct((T, AW), BF16)]
        + [jax.ShapeDtypeStruct((T, LANES), F32) if d == 1 else _perm_shape(d, LANES, F32) for d in dils],
        scratch_shapes=[_tile_scratch(AW), _tile_scratch(LANES)],
        compiler_params=_cp(("parallel",)),
    )(*args)
    return out[0], [o.reshape(T, LANES) for o in out[1:]]


GATE_BLOCK0 = (IN_W - 2 * D) // (D // 2)


def _mix_out(z3b, attnb, gates, wc, wa_t, wo, x1, name):
    tm = 512

    def body(z_ref, a_ref, g_ref, wc_ref, wa_ref, wo_ref, x_ref, xo_ref, yc_ref, ya_ref, mx_ref):
        yc = _dot(z_ref[...], wc_ref[...], NN)
        ya = _dot(a_ref[...], wa_ref[...], NT)
        yc_ref[...] = yc.astype(BF16)
        ya_ref[...] = ya.astype(BF16)
        gv = g_ref[...].astype(F32)
        mixed = (_sig(gv[:, :D]) * yc + _sig(gv[:, D:]) * ya).astype(BF16)
        mx_ref[...] = mixed
        xo_ref[...] = x_ref[...] + _dot(mixed, wo_ref[...], NN)

    row = pl.BlockSpec((tm, D), lambda i: (i, 0))
    return pl.pallas_call(
        body, name=name, grid=(T // tm,),
        in_specs=[row, pl.BlockSpec((tm, AW), lambda i: (i, 0)), pl.BlockSpec((tm, 2 * D), lambda i: (i, 0)),
                  _const_spec((D, D)), _const_spec((D, AW)), _const_spec((D, D)), row],
        out_specs=[row, row, row, row],
        out_shape=[jax.ShapeDtypeStruct((T, D), F32), jax.ShapeDtypeStruct((T, D), BF16),
                   jax.ShapeDtypeStruct((T, D), BF16), jax.ShapeDtypeStruct((T, D), BF16)],
        compiler_params=_cp(("parallel",)),
    )(z3b, attnb, gates, wc, wa_t, wo, x1)


def _mix_out_bwd(dx2, gates, yc, ya, attn, wc, wa_t, wo, win_t, name):
    tm = PERM_TM
    dils = [d for _, d in GROUPS]
    ng = len(dils)

    def body(dx_ref, g_ref, yc_ref, ya_ref, at_ref, wc_ref, wa_ref, wo_ref, wg0_ref, wg1_ref, wg2_ref, wg3_ref,
             dg_ref, dyc_ref, dya_ref, dxb_ref, dz3_ref, dhg_ref, *rest):
        dat_refs, dl_refs, tile = rest[:ng], rest[ng:2 * ng], rest[-1]
        dxb = dx_ref[...].astype(BF16)
        dxb_ref[...] = dxb
        dmix = _dot(dxb, wo_ref[...], NT)
        gv = g_ref[...].astype(F32)
        sc = _sig(gv[:, :D])
        sa = _sig(gv[:, D:])
        ycv, yav = yc_ref[...].astype(F32), ya_ref[...].astype(F32)
        dgc = (dmix * ycv * sc * (1.0 - sc)).astype(BF16)
        dga = (dmix * yav * sa * (1.0 - sa)).astype(BF16)
        dg_ref[:, pl.ds(0, D)] = dgc
        dg_ref[:, pl.ds(D, D)] = dga
        half = D // 2
        dhg_ref[...] = (_dot(dgc[:, :half], wg0_ref[...], NN) + _dot(dgc[:, half:], wg1_ref[...], NN)
                        + _dot(dga[:, :half], wg2_ref[...], NN) + _dot(dga[:, half:], wg3_ref[...], NN))
        dyc = (dmix * sc).astype(BF16)
        dya = (dmix * sa).astype(BF16)
        dyc_ref[...] = dyc
        dya_ref[...] = dya
        dz3_ref[...] = _dot(dyc, wc_ref[...], NT).astype(BF16)
        dat = _dot(dya, wa_ref[...], NN)
        prod = dat * at_ref[...].astype(F32)
        delta = jnp.concatenate(
            [jnp.broadcast_to(jnp.sum(prod[:, h * HEAD:(h + 1) * HEAD], axis=-1, keepdims=True), (tm, HEAD))
             for h in range(NHG)], axis=1)
        for value, out_refs in ((dat, dat_refs), (delta, dl_refs)):
            _put_tile(tile, value)
            for dil, ref in zip(dils, out_refs):
                if dil == 1:
                    ref[...] = value.astype(ref.dtype)
                else:
                    _store_perm(ref, tile, dil)

    row = pl.BlockSpec((tm, D), lambda i: (i, 0))
    row2 = pl.BlockSpec((tm, 2 * D), lambda i: (i, 0))
    rowa = pl.BlockSpec((tm, AW), lambda i: (i, 0))
    aspecs = [rowa if d == 1 else _perm_spec(d, AW) for d in dils]

    def ashapes(dtype):
        return [jax.ShapeDtypeStruct((T, AW), dtype) if d == 1 else _perm_shape(d, AW, dtype) for d in dils]

    out = pl.pallas_call(
        body, name=name, grid=(T // tm,),
        in_specs=[row, row2, row, row, rowa, _const_spec((D, D)), _const_spec((D, AW)), _const_spec((D, D))]
        + [pl.BlockSpec((D // 2, D), lambda i, q=q: (GATE_BLOCK0 + q, 0), pipeline_mode=pl.Buffered(1))
           for q in range(4)],
        out_specs=[row2, row, row, row, row, row] + aspecs + aspecs,
        out_shape=[jax.ShapeDtypeStruct((T, 2 * D), BF16), jax.ShapeDtypeStruct((T, D), BF16),
                   jax.ShapeDtypeStruct((T, D), BF16), jax.ShapeDtypeStruct((T, D), BF16),
                   jax.ShapeDtypeStruct((T, D), BF16), jax.ShapeDtypeStruct((T, D), F32)]
        + ashapes(BF16) + ashapes(F32),
        scratch_shapes=[_tile_scratch(AW)],
        compiler_params=_cp(("parallel",)),
    )(dx2, gates, yc, ya, attn, wc, wa_t, wo, win_t, win_t, win_t, win_t)
    dats = [o.reshape(T, AW) for o in out[6:6 + ng]]
    deltas = [o.reshape(T, AW) for o in out[6 + ng:6 + 2 * ng]]
    return out[0], out[1], out[2], out[3], out[4], out[5], dats, deltas


def _peer(k):
    x, y, c = lax.axis_index("x"), lax.axis_index("y"), lax.axis_index("c")
    px = 1 - x if k & 4 else x
    py = 1 - y if k & 2 else y
    pc = 1 - c if k & 1 else c
    return (px, py, pc), 4 * px + 2 * py + pc


HBM_SPEC = pl.BlockSpec(memory_space=pltpu.HBM)
SEM_SPEC = pl.BlockSpec(memory_space=pltpu.SEMAPHORE)
EFFECT = pltpu.SideEffectType.DATAFLOW_SIDE_EFFECTING


def _my_place():
    return 4 * lax.axis_index("x") + 2 * lax.axis_index("y") + lax.axis_index("c")


def _tie(a, order_after, name):
    na = len(order_after)

    def body(*refs):
        del refs

    return pl.pallas_call(
        body, name=name, in_specs=[pl.BlockSpec(memory_space=pl.ANY)] * (1 + na),
        out_specs=pl.BlockSpec(memory_space=pl.ANY), out_shape=jax.ShapeDtypeStruct(a.shape, a.dtype),
        input_output_aliases={0: 0},
    )(a, *order_after)


def _prep_gather(ws, order_after, name):
    me = jnp.reshape(_my_place(), (1,)).astype(jnp.int32)
    n = len(ws)
    na = len(order_after)
    shapes = [((32, wv.shape[1]), F32) if wv.shape[0] == CONV_W else (wv.shape, BF16) for wv in ws]

    def body(me_ref, *refs):
        del me_ref
        ins, outs = refs[:n], refs[n + na:]
        for wv, i_ref, o_ref in zip(ws, ins, outs):
            if wv.shape[0] == CONV_W:
                o_ref[pl.ds(0, CONV_W), :] = i_ref[...]
                o_ref[pl.ds(CONV_W, 1), :] = jnp.zeros((1, wv.shape[1]), F32)
            else:
                o_ref[...] = i_ref[...].astype(BF16)

    grid_spec = pltpu.PrefetchScalarGridSpec(
        num_scalar_prefetch=1, grid=(1,),
        in_specs=[pl.BlockSpec(wv.shape, lambda i, m: (0, 0)) for wv in ws]
        + [pl.BlockSpec(memory_space=pl.ANY)] * na,
        out_specs=[pl.BlockSpec(shp, lambda i, m: (m[0], 0)) for shp, _ in shapes])
    return pl.pallas_call(
        body, name=name, grid_spec=grid_spec,
        out_shape=[jax.ShapeDtypeStruct((NDEV * shp[0], shp[1]), dt) for shp, dt in shapes],
        compiler_params=_cp(("arbitrary",)),
    )(me, *ws, *order_after)


GATHER_A = ((1, 0), (2, 0), (4, 0), (6, 0))
GATHER_B = ((1, 2), (1, 4), (1, 6))
GATHER_DIRECT = tuple((k, 0) for k in range(1, NDEV))


def _gather_start(lands, plan, order_after, name):
    n = len(lands)
    na = len(order_after)
    npl = len(plan)

    def body(*refs):
        land_refs = refs[:n]
        send, recv = refs[n + na], refs[n + na + 1]
        token = refs[-1]
        for w in range(n):
            rows = lands[w].shape[0] // NDEV
            for p, (k, j) in enumerate(plan):
                peer, _ = _peer(k)
                _, blk = _peer(j)
                part = land_refs[w].at[pl.ds(blk * rows, rows)]
                i = w * npl + p
                pltpu.make_async_remote_copy(src_ref=part, dst_ref=part, send_sem=send.at[i], recv_sem=recv.at[i],
                                             device_id=peer, device_id_type=MESH_ID).start()
        token[...] = jnp.zeros_like(token)

    nsem = n * npl
    bufs = [pltpu.with_memory_space_constraint(a, pltpu.HBM) for a in lands]
    out = pl.pallas_call(
        body, name=name,
        in_specs=[HBM_SPEC] * n + [pl.BlockSpec(memory_space=pl.ANY)] * na,
        out_specs=[SEM_SPEC, SEM_SPEC] + [HBM_SPEC] * n + [pl.BlockSpec(memory_space=pltpu.VMEM)],
        out_shape=[pltpu.SemaphoreType.DMA((nsem,)), pltpu.SemaphoreType.DMA((nsem,))]
        + [pltpu.HBM(a.shape, a.dtype) for a in bufs] + [jax.ShapeDtypeStruct((8, 128), F32)],
        input_output_aliases={i: 2 + i for i in range(n)},
        compiler_params=pltpu.CompilerParams(has_side_effects=EFFECT),
    )(*bufs, *order_after)
    return out[0], out[1], out[2:2 + n], out[-1]


def _gather_wait(started, plan, order_after, name):
    send, recv, lands, _ = started
    n = len(lands)
    na = len(order_after)
    npl = len(plan)

    def body(*refs):
        land_refs = refs[:n]
        send_ref, recv_ref = refs[n], refs[n + 1]
        for w in range(n):
            rows = lands[w].shape[0] // NDEV
            for p, (k, j) in enumerate(plan):
                peer, _ = _peer(k)
                _, blk = _peer(j)
                part = land_refs[w].at[pl.ds(blk * rows, rows)]
                i = w * npl + p
                cp = pltpu.make_async_remote_copy(src_ref=part, dst_ref=part, send_sem=send_ref.at[i],
                                                  recv_sem=recv_ref.at[i], device_id=peer, device_id_type=MESH_ID)
                cp.wait_send()
                cp.wait_recv()

    out = pl.pallas_call(
        body, name=name,
        in_specs=[HBM_SPEC] * n + [SEM_SPEC, SEM_SPEC] + [pl.BlockSpec(memory_space=pl.ANY)] * na,
        out_specs=[HBM_SPEC] * n,
        out_shape=[pltpu.HBM(a.shape, a.dtype) for a in lands],
        input_output_aliases={i: i for i in range(n)},
        compiler_params=pltpu.CompilerParams(has_side_effects=EFFECT),
    )(*lands, send, recv, *order_after)
    return list(out)


def _copy_ends(kind, src, land, me, plin, k):
    if kind == "scatter":
        rows = src.shape[0] // NDEV
        return src.at[pl.ds(plin * rows, rows)], land.at[k - 1]
    return src, land.at[me]


def _landing(kind, src):
    me = _my_place()
    if kind == "scatter":
        return lax.empty((NDEV - 1, src.shape[0] // NDEV) + src.shape[1:], src.dtype)
    land = lax.empty((NDEV,) + src.shape, src.dtype)
    return lax.dynamic_update_slice(land, src[None], (me,) + (0,) * src.ndim)


def _send_start(kinds, srcs, order_after, name):
    n = len(srcs)
    lands = [_landing(kd, s) for kd, s in zip(kinds, srcs)]
    na = len(order_after)

    def body(*refs):
        src_refs, land_refs = refs[:n], refs[n:2 * n]
        send, recv = refs[2 * n + na], refs[2 * n + na + 1]
        token = refs[-1]
        _, me = _peer(0)
        for w in range(n):
            for k in range(1, NDEV):
                peer, plin = _peer(k)
                s, d = _copy_ends(kinds[w], src_refs[w], land_refs[w], me, plin, k)
                i = w * (NDEV - 1) + k - 1
                pltpu.make_async_remote_copy(src_ref=s, dst_ref=d, send_sem=send.at[i], recv_sem=recv.at[i],
                                             device_id=peer, device_id_type=MESH_ID).start()
        token[...] = jnp.zeros_like(token)

    nsem = n * (NDEV - 1)
    bufs = [pltpu.with_memory_space_constraint(a, pltpu.HBM) for a in list(srcs) + lands]
    out = pl.pallas_call(
        body, name=name,
        in_specs=[HBM_SPEC] * (2 * n) + [pl.BlockSpec(memory_space=pl.ANY)] * na,
        out_specs=[SEM_SPEC, SEM_SPEC] + [HBM_SPEC] * (2 * n) + [pl.BlockSpec(memory_space=pltpu.VMEM)],
        out_shape=[pltpu.SemaphoreType.DMA((nsem,)), pltpu.SemaphoreType.DMA((nsem,))]
        + [pltpu.HBM(a.shape, a.dtype) for a in bufs] + [jax.ShapeDtypeStruct((8, 128), F32)],
        input_output_aliases={i: 2 + i for i in range(2 * n)},
        compiler_params=pltpu.CompilerParams(has_side_effects=EFFECT),
    )(*bufs, *order_after)
    return out[0], out[1], out[2:2 + n], out[2 + n:2 + 2 * n], out[-1]


def _send_wait(kinds, started, order_after, name):
    send, recv, srcs, lands, _ = started
    n = len(srcs)
    na = len(order_after)

    def body(*refs):
        src_refs, land_refs = refs[:n], refs[n:2 * n]
        send_ref, recv_ref = refs[2 * n], refs[2 * n + 1]
        _, me = _peer(0)
        for w in range(n):
            for k in range(1, NDEV):
                peer, plin = _peer(k)
                s, d = _copy_ends(kinds[w], src_refs[w], land_refs[w], me, plin, k)
                i = w * (NDEV - 1) + k - 1
                cp = pltpu.make_async_remote_copy(src_ref=s, dst_ref=d, send_sem=send_ref.at[i],
                                                  recv_sem=recv_ref.at[i], device_id=peer, device_id_type=MESH_ID)
                cp.wait_send()
                cp.wait_recv()

    bufs = list(srcs) + list(lands)
    out = pl.pallas_call(
        body, name=name,
        in_specs=[HBM_SPEC] * (2 * n) + [SEM_SPEC, SEM_SPEC] + [pl.BlockSpec(memory_space=pl.ANY)] * na,
        out_specs=[HBM_SPEC] * (2 * n),
        out_shape=[pltpu.HBM(a.shape, a.dtype) for a in bufs],
        input_output_aliases={i: i for i in range(2 * n)},
        compiler_params=pltpu.CompilerParams(has_side_effects=EFFECT),
    )(*bufs, send, recv, *order_after)
    return out[:n], out[n:]


def _gsum(own, land, name):
    rows, cols = own.shape
    tr = rows // 2 if rows * cols > 512 * 1024 and rows % 32 == 0 else rows

    def body(own_ref, l_ref, o_ref):
        tot = own_ref[...].astype(F32)
        for s in range(NDEV - 1):
            tot = tot + l_ref[s].astype(F32)
        o_ref[...] = tot

    return pl.pallas_call(
        body, name=name, grid=(rows // tr,),
        in_specs=[pl.BlockSpec((tr, cols), lambda i: (i, 0)),
                  pl.BlockSpec((NDEV - 1, tr, cols), lambda i: (0, i, 0))],
        out_specs=pl.BlockSpec((tr, cols), lambda i: (i, 0)),
        out_shape=jax.ShapeDtypeStruct((rows, cols), F32),
        compiler_params=_cp(("parallel",)),
    )(own, land)


def _adamw_math(w, g, m, v):
    m2 = B1 * m + (1.0 - B1) * g
    v2 = B2 * v + (1.0 - B2) * (g * g)
    m_hat = m2 / (1.0 - B1 ** STEP)
    v_hat = v2 / (1.0 - B2 ** STEP)
    delta = -LR * (m_hat / (jnp.sqrt(v_hat) + AEPS) + WD * w)
    return delta, m2, v2


def _adamw(w, g, m, v, name):
    rows, cols = w.shape
    tr = 256 if rows % 256 == 0 and rows > 256 else rows

    def body(w_ref, g_ref, m_ref, v_ref, d_ref, mo_ref, vo_ref):
        d, m2, v2 = _adamw_math(w_ref[...], g_ref[...], m_ref[...], v_ref[...])
        d_ref[...] = d
        mo_ref[...] = m2
        vo_ref[...] = v2

    blk = pl.BlockSpec((tr, cols), lambda i: (i, 0))
    return pl.pallas_call(
        body, name=name, grid=(rows // tr,), in_specs=[blk] * 4, out_specs=[blk] * 3,
        out_shape=[jax.ShapeDtypeStruct((rows, cols), F32)] * 3,
        compiler_params=_cp(("parallel",)),
    )(w, g, m, v)


UPD_TC = 256


def _update(src, land, w, m, v, name):
    rows, cols = land.shape[1:]
    tc = min(UPD_TC if rows > 512 else 2 * UPD_TC, cols)
    me = jnp.reshape(_my_place(), (1,)).astype(jnp.int32)

    def body(me_ref, own_ref, l_ref, w_ref, m_ref, v_ref, g_ref, d_ref, mo_ref, vo_ref):
        del me_ref
        g = own_ref[...].astype(F32)
        for s in range(NDEV - 1):
            g = g + l_ref[s].astype(F32)
        g_ref[...] = g
        d, m2, v2 = _adamw_math(w_ref[...], g, m_ref[...], v_ref[...])
        d_ref[...] = d
        mo_ref[...] = m2
        vo_ref[...] = v2

    wblk = pl.BlockSpec((rows, tc), lambda j, p: (0, j))
    grid_spec = pltpu.PrefetchScalarGridSpec(
        num_scalar_prefetch=1, grid=(cols // tc,),
        in_specs=[pl.BlockSpec((rows, tc), lambda j, p: (p[0], j)),
                  pl.BlockSpec((NDEV - 1, rows, tc), lambda j, p: (0, 0, j)), wblk, wblk, wblk],
        out_specs=[wblk] * 4)
    return pl.pallas_call(
        body, name=name, grid_spec=grid_spec, out_shape=[jax.ShapeDtypeStruct((rows, cols), F32)] * 4,
        compiler_params=_cp(("parallel",)),
    )(me, src, land, w, m, v)


def _small_update(vland, w8, m8, v8, name):
    def body(l_ref, w_ref, m_ref, v_ref, g_ref, d_ref, mo_ref, vo_ref):
        g = l_ref[0]
        for s in range(1, NDEV):
            g = g + l_ref[s]
        g_ref[...] = g
        d, m2, v2 = _adamw_math(w_ref[...], g, m_ref[...], v_ref[...])
        d_ref[...] = d
        mo_ref[...] = m2
        vo_ref[...] = v2

    return pl.pallas_call(
        body, name=name, out_shape=[jax.ShapeDtypeStruct((8, D), F32)] * 4,
        compiler_params=_cp(None),
    )(vland, w8, m8, v8)


def kernel(x, ffn1_norm, ffn1_w_gate, ffn1_w_up, ffn1_w_down, mix_norm, w_in, conv_dw_kernel, conv_dw_bias, conv_ln_gain, conv_ln_bias, conv_w_out, attn_w_out, w_o, ffn2_norm, ffn2_w_gate, ffn2_w_up, ffn2_w_down, final_norm, loss_target, m_ffn1_norm, m_ffn1_w_gate, m_ffn1_w_up, m_ffn1_w_down, m_mix_norm, m_w_in, m_conv_dw_kernel, m_conv_dw_bias, m_conv_ln_gain, m_conv_ln_bias, m_conv_w_out, m_attn_w_out, m_w_o, m_ffn2_norm, m_ffn2_w_gate, m_ffn2_w_up, m_ffn2_w_down, m_final_norm, v_ffn1_norm, v_ffn1_w_gate, v_ffn1_w_up, v_ffn1_w_down, v_mix_norm, v_w_in, v_conv_dw_kernel, v_conv_dw_bias, v_conv_ln_gain, v_conv_ln_bias, v_conv_w_out, v_attn_w_out, v_w_o, v_ffn2_norm, v_ffn2_w_gate, v_ffn2_w_up, v_ffn2_w_down, v_final_norm):
    names = ["ffn1_norm", "ffn1_w_gate", "ffn1_w_up", "ffn1_w_down", "mix_norm", "w_in", "conv_dw_kernel",
             "conv_dw_bias", "conv_ln_gain", "conv_ln_bias", "conv_w_out", "attn_w_out", "w_o", "ffn2_norm",
             "ffn2_w_gate", "ffn2_w_up", "ffn2_w_down", "final_norm"]
    w = dict(ffn1_norm=ffn1_norm, ffn1_w_gate=ffn1_w_gate, ffn1_w_up=ffn1_w_up, ffn1_w_down=ffn1_w_down, mix_norm=mix_norm, w_in=w_in, conv_dw_kernel=conv_dw_kernel, conv_dw_bias=conv_dw_bias, conv_ln_gain=conv_ln_gain, conv_ln_bias=conv_ln_bias, conv_w_out=conv_w_out, attn_w_out=attn_w_out, w_o=w_o, ffn2_norm=ffn2_norm, ffn2_w_gate=ffn2_w_gate, ffn2_w_up=ffn2_w_up, ffn2_w_down=ffn2_w_down, final_norm=final_norm)
    mo = dict(ffn1_norm=m_ffn1_norm, ffn1_w_gate=m_ffn1_w_gate, ffn1_w_up=m_ffn1_w_up, ffn1_w_down=m_ffn1_w_down, mix_norm=m_mix_norm, w_in=m_w_in, conv_dw_kernel=m_conv_dw_kernel, conv_dw_bias=m_conv_dw_bias, conv_ln_gain=m_conv_ln_gain, conv_ln_bias=m_conv_ln_bias, conv_w_out=m_conv_w_out, attn_w_out=m_attn_w_out, w_o=m_w_o, ffn2_norm=m_ffn2_norm, ffn2_w_gate=m_ffn2_w_gate, ffn2_w_up=m_ffn2_w_up, ffn2_w_down=m_ffn2_w_down, final_norm=m_final_norm)
    vo = dict(ffn1_norm=v_ffn1_norm, ffn1_w_gate=v_ffn1_w_gate, ffn1_w_up=v_ffn1_w_up, ffn1_w_down=v_ffn1_w_down, mix_norm=v_mix_norm, w_in=v_w_in, conv_dw_kernel=v_conv_dw_kernel, conv_dw_bias=v_conv_dw_bias, conv_ln_gain=v_conv_ln_gain, conv_ln_bias=v_conv_ln_bias, conv_w_out=v_conv_w_out, attn_w_out=v_attn_w_out, w_o=v_w_o, ffn2_norm=v_ffn2_norm, ffn2_w_gate=v_ffn2_w_gate, ffn2_w_up=v_ffn2_w_up, ffn2_w_down=v_ffn2_w_down, final_norm=v_final_norm)
    col_sharded = ("ffn1_w_gate", "ffn1_w_up", "w_in", "attn_w_out", "ffn2_w_gate", "ffn2_w_up")
    row_sharded = ("ffn1_w_down", "conv_w_out", "w_o", "ffn2_w_down")
    small = ("ffn1_norm", "mix_norm", "ffn2_norm", "final_norm", "conv_dw_bias", "conv_ln_gain", "conv_ln_bias")

    def landing_view(a, n):
        return jnp.transpose(a[0]) if n in col_sharded else a[0]

    def own_view(a, n):
        return jnp.transpose(a)[None] if n in col_sharded else a[None]

    ag_groups = (("ffn1_w_gate", "ffn1_w_up", "ffn1_w_down"),
                 ("w_in", "attn_w_out", "conv_w_out", "w_o", "conv_dw_kernel"),
                 ("ffn2_w_gate", "ffn2_w_up", "ffn2_w_down"))
    ag, order = [], []
    for gi, grp in enumerate(ag_groups):
        lands = _prep_gather([landing_view(w[n], n) for n in grp], order, f"gather_prep{gi}")
        st = _gather_start(lands, GATHER_DIRECT if gi == 2 else GATHER_A, [], f"gather_a_start{gi}")
        ag.append(st)
        order = [st[3]]

    def chips_in(gi, after):
        lands = _gather_wait(ag[gi], GATHER_A, after, f"gather_a_wait{gi}")
        return _gather_start(lands, GATHER_B, [], f"gather_b_start{gi}")

    def all_in(gi, st, after):
        return _gather_wait(st, GATHER_B, after, f"gather_b_wait{gi}")

    x0 = x[0]
    tgt = loss_target[0]
    gf = final_norm.reshape(1, D)

    wg1, wu1, wd1 = all_in(0, chips_in(0, [ag[2][3]]), [])
    x1, gg1, uu1, h2p = _ffn_fwd(x0, ffn1_norm, wg1, wu1, wd1, "ffn1_fwd", next_gain=mix_norm)
    h2 = h2p[0]
    win_t, wa_t, wc, wo, kern_blocks = all_in(1, chips_in(1, [x1]), [])
    kern = kern_blocks.reshape(NDEV, 32, D // NDEV).transpose(1, 0, 2).reshape(32, D)
    ptm = min(T, 2048)
    ab = _mm(h2, win_t, mode="nt", m=T, n=2 * D, k=D, tm=ptm, tn=512, tk=D, out_dtype=BF16, name="proj_conv")
    z1, z3b, gates = _conv_fwd(ab, kern, conv_dw_bias, conv_ln_gain, conv_ln_bias, "conv_fwd",
                               guest=(h2, win_t, (13, 14, 15, 16, 4, 7, 10), 512))
    qkv, qkv_col0 = [gates], [2 * D]
    for gi in range(1, len(GROUPS)):
        qkv.append(_mm(h2p[gi], win_t, mode="nt", m=T, n=3 * AW, k=D, tm=ptm, tn=AW, tk=D, out_dtype=BF16,
                       b_map=lambda i, j, kk, gi=gi: (4 + gi + 3 * j, 0), name=f"proj_qkv{gi}"))
        qkv_col0.append(0)
    outs, lses = [], []
    for gi, (_, dil) in enumerate(GROUPS):
        o, l = _attn_fwd(qkv[gi], gi, f"attn_fwd{gi}", col0=qkv_col0[gi])
        outs.append(o)
        lses.append(l)
    attnb, lse = _merge(outs, lses, "attn_merge")
    x2, yc, ya, mixedb = _mix_out(z3b, attnb, gates, wc, wa_t, wo, x1, "mix_out_fwd")
    wg2, wu2, wd2 = _gather_wait(ag[2], GATHER_DIRECT, [x2], "gather_a_wait2")
    gg2, uu2, dx3, dgf, loss_part = _ffn_fwd(x2, ffn2_norm, wg2, wu2, wd2, "ffn2_fwd", loss_of=(gf, tgt))

    dx2, dg3, dgb, dub, actb, hb, dob = _ffn_bwd(x2, ffn2_norm, gg2, uu2, dx3, wg2, wu2, wd2, "ffn2_bwd")
    grads = {}
    grads["ffn2_w_down"] = _wgrad(actb, dob, FF, D, "ffn2_dwd")
    rs_groups = [("ffn2_w_gate", "ffn2_w_up", "ffn2_w_down"),
                 ("attn_w_out", "conv_w_out", "w_o", "conv_dw_kernel"),
                 ("w_in",),
                 ("ffn1_w_gate",), ("ffn1_w_up",), ("ffn1_w_down",), ()]
    last = len(rs_groups) - 1
    rs = []

    dgates, dycb, dyab, dx2b, dz3, dh_gates, dattnb, delta = _mix_out_bwd(dx2, gates, yc, ya, attnb, wc, wa_t, wo,
                                                                          win_t, "mix_out_bwd")
    grads["w_o"] = _wgrad(mixedb, dx2b, D, D, "dw_o")
    grads["conv_w_out"] = _wgrad(z3b, dycb, D, D, "dw_conv_out")
    grads["attn_w_out"] = _wgrad(dyab, attnb, D, AW, "dw_attn_out")
    dab, dkern, dvec, grads["ffn2_w_gate"], grads["ffn2_w_up"] = _conv_bwd(
        dz3, z1, ab, kern, conv_ln_gain, conv_ln_bias, "conv_bwd", guest_lhs=(dgb, dub), guest_rhs=hb)
    grads["conv_dw_kernel"] = dkern.reshape(32, NDEV, D // NDEV).transpose(1, 0, 2).reshape(NDEV * 32, D // NDEV)
    rs.append(_send_start(["scatter"] * 3, [grads[n] for n in rs_groups[0]], [], "scatter_start0"))
    rs.append(_send_start(["scatter"] * 4, [grads[n] for n in rs_groups[1]], [rs[0][4]], "scatter_start1"))
    dattnb = [_tie(a, [rs[1][4]], f"tie_after_scatter1_{i}") for i, a in enumerate(dattnb)]

    dqkv, dq3s = [], []
    for gi, (_, dil) in enumerate(GROUPS):
        dq3 = _attn_bwd(qkv[gi], dattnb[gi], lse[gi], delta[gi], gi, f"attn_bwd{gi}", col0=qkv_col0[gi])
        dq3s.append(dq3)
        dqkv.append(dq3.reshape(3 * T, AW))

    wtk = min(T, 2048)
    dwin = _mm(dab, h2, mode="tn", m=2 * D, n=D, k=T, tm=2 * D, tn=D, tk=wtk, out_dtype=BF16, out_rows=IN_W,
               name="dw_in_conv")
    dwin = _mm(dgates, h2, mode="tn", m=2 * D, n=D, k=T, tm=512, tn=D, tk=wtk, out_dtype=BF16, out_rows=IN_W,
               o_map=lambda i, j, kk: (13 + i, 0), passthru=dwin, name="dw_in_gates")
    for gi in range(3):
        dwin = _mm(dqkv[gi], h2p[gi], mode="tn", m=3 * AW, n=D, k=T, tm=AW, tn=D, tk=wtk, out_dtype=BF16,
                   out_rows=IN_W, a_map=lambda i, j, kk: (i * (T // wtk) + kk, 0),
                   o_map=lambda i, j, kk, gi=gi: (4 + gi + 3 * i, 0), passthru=dwin, name=f"dw_in_qkv{gi}")
    grads["w_in"] = dwin
    rs.append(_send_start(["scatter"], [dwin], [rs[1][4]], "scatter_start2"))
    dab = _tie(dab, [rs[2][4]], "tie_after_scatter2")

    dx1, dg2 = _rms_bwd(x1, mix_norm, dh_gates, dab, (0, 1, 2, 3), dq3s, win_t, dx2, "mix_norm_bwd")

    dgb, dub, actb, hb, dob = _ffn_bwd_pre(x0, ffn1_norm, gg1, uu1, dx1, wd1, "ffn1_bwd_pre")
    grads["ffn1_w_gate"] = _wgrad(dgb, hb, FF, D, "ffn1_dwg")
    rs.append(_send_start(["scatter"], [grads["ffn1_w_gate"]], [rs[2][4]], "scatter_start3"))
    hb = _tie(hb, [rs[3][4]], "tie_after_scatter3")
    grads["ffn1_w_up"] = _wgrad(dub, hb, FF, D, "ffn1_dwu")
    rs.append(_send_start(["scatter"], [grads["ffn1_w_up"]], [rs[3][4]], "scatter_start4"))
    dob = _tie(dob, [rs[4][4]], "tie_after_scatter4")
    grads["ffn1_w_down"] = _wgrad(actb, dob, FF, D, "ffn1_dwd")
    rs.append(_send_start(["scatter"], [grads["ffn1_w_down"]], [rs[4][4]], "scatter_start5"))
    dgb = _tie(dgb, [rs[5][4]], "tie_after_scatter5")
    dx0, dg1 = _ffn_bwd_dx(x0, ffn1_norm, dgb, dub, dx1, wg1, wu1, "ffn1_bwd_dx")
    vec = jnp.concatenate([dg1, dg2, dg3, dgf, dvec[0:3], jnp.broadcast_to(loss_part[:, :1], (1, D))], axis=0)
    rs.append(_send_start(["bcast"], [vec], [rs[5][4]], "scatter_start6"))

    g_out, d_out, m_out, v_out = {}, {}, {}, {}
    me = _my_place()
    after, done = [rs[last][4]], []
    for gi, grp in enumerate(rs_groups):
        kinds = ["scatter"] * len(grp) + (["bcast"] if gi == last else [])
        srcs, lands = _send_wait(kinds, rs[gi], after + (done if gi >= last - 1 else []), f"scatter_wait{gi}")
        for n, src, land in zip(grp, srcs, lands):
            if n == "conv_dw_kernel":
                rows = src.shape[0] // NDEV
                own = lax.dynamic_slice(src, (me * rows, 0), (rows, src.shape[1]))
                g = _gsum(own, land, f"gsum_{n}")[:CONV_W]
                d, m2, v2 = _adamw(w[n][0], g, mo[n][0], vo[n][0], f"adamw_{n}")
                after = [d]
                done.append(d)
                g, d, m2, v2 = g[None], d[None], m2[None], v2[None]
            else:
                res = _update(src, land, landing_view(w[n], n), landing_view(mo[n], n), landing_view(vo[n], n),
                              f"update_{n}")
                after = [res[1]]
                done.append(res[1])
                g, d, m2, v2 = (own_view(a, n) for a in res)
            g_out[n], d_out[n], m_out[n], v_out[n] = g, d, m2, v2
    vland = lands[-1]

    def rows8(src):
        return jnp.concatenate([src[n].reshape(1, D) for n in small] + [jnp.ones((1, D), F32)], axis=0)

    g8, d8, m8, v8 = _small_update(vland, rows8(w), rows8(mo), rows8(vo), "small_update")
    for r, n in enumerate(small):
        shp = w[n].shape
        g_out[n], d_out[n], m_out[n], v_out[n] = (a[r].reshape(shp) for a in (g8, d8, m8, v8))
    loss = g8[7, 0]

    return (loss, dx0[None], *[g_out[n] for n in names], *[d_out[n] for n in names],
            *[m_out[n] for n in names], *[v_out[n] for n in names])
```

```python
import numpy as np
import jax
import jax.numpy as jnp
from jax import lax
from jax.experimental import pallas as pl
from jax.experimental.pallas import tpu as pltpu

F32 = jnp.float32
BF16 = jnp.bfloat16

T = 4096
D = 1024
FF = 2816
NDEV = 8
CONV_W = 31
HEAD = 128
BLK = 128
GROUPS = ((128, 1), (512, 4), (2048, 16))
NHG = 4
AW = NHG * HEAD
IN_W = 2 * D + 3 * 3 * AW + 2 * D
EPS = 1e-6
B1, B2, LR, AEPS, WD, STEP = 0.9, 0.999, 0.001, 1e-08, 0.01, 10
NEG = -1e30
VMEM_LIMIT = 56 * 1024 * 1024
MESH_ID = pl.DeviceIdType.MESH

NT = (((1,), (1,)), ((), ()))
NN = (((1,), (0,)), ((), ()))
TN = (((0,), (0,)), ((), ()))
_DIMS = {"nn": NN, "nt": NT, "tn": TN}


def _cp(sem=None):
    return pltpu.CompilerParams(dimension_semantics=sem, vmem_limit_bytes=VMEM_LIMIT)


def _sig(v):
    return 1.0 / (1.0 + jnp.exp(-v))


def _dot(a, b, dims):
    return lax.dot_general(a, b, dims, preferred_element_type=F32)


def _const_spec(shape):
    nd = len(shape)
    return pl.BlockSpec(shape, lambda *_: (0,) * nd)


def _mm(a, b, *, mode, m, n, k, tm, tn, tk, out_dtype, name, a_map=None, b_map=None,
        o_map=None, out_rows=None, init=None, passthru=None):
    gi, gj, gk = m // tm, n // tn, k // tk
    assert gi * tm == m and gj * tn == n and gk * tk == k, (name, m, n, k, tm, tn, tk)
    if mode == "nn":
        a_blk, b_blk = (tm, tk), (tk, tn)
        da, db = (lambda i, j, kk: (i, kk)), (lambda i, j, kk: (kk, j))
    elif mode == "nt":
        a_blk, b_blk = (tm, tk), (tn, tk)
        da, db = (lambda i, j, kk: (i, kk)), (lambda i, j, kk: (j, kk))
    else:
        a_blk, b_blk = (tk, tm), (tk, tn)
        da, db = (lambda i, j, kk: (kk, i)), (lambda i, j, kk: (kk, j))
    a_map = a_map or da
    b_map = b_map or db
    o_map = o_map or (lambda i, j, kk: (i, j))
    dims = _DIMS[mode]
    extra = init if init is not None else passthru
    out_rows = out_rows or m

    def body(*refs):
        if init is not None:
            a_ref, b_ref, i_ref, o_ref = refs[:4]
        elif passthru is not None:
            a_ref, b_ref, _, o_ref = refs[:4]
        else:
            a_ref, b_ref, o_ref = refs[:3]
        if gk == 1:
            prod = _dot(a_ref[...], b_ref[...], dims)
            if init is not None:
                prod = prod + i_ref[...].astype(F32)
            o_ref[...] = prod.astype(out_dtype)
            return
        acc = refs[-1]
        kk = pl.program_id(2)

        @pl.when(kk == 0)
        def _():
            if init is not None:
                acc[...] = i_ref[...].astype(F32)
            else:
                acc[...] = jnp.zeros_like(acc)

        acc[...] += _dot(a_ref[...], b_ref[...], dims)

        @pl.when(kk == gk - 1)
        def _():
            o_ref[...] = acc[...].astype(out_dtype)

    in_specs = [pl.BlockSpec(a_blk, a_map), pl.BlockSpec(b_blk, b_map)]
    args = [a, b]
    aliases = {}
    if init is not None:
        in_specs.append(pl.BlockSpec((tm, tn), o_map))
        args.append(init)
        aliases = {2: 0}
    elif passthru is not None:
        in_specs.append(pl.BlockSpec(memory_space=pl.ANY))
        args.append(passthru)
        aliases = {2: 0}
    out_dt = extra.dtype if extra is not None else out_dtype
    assert out_dt == out_dtype
    return pl.pallas_call(
        body, name=name, grid=(gi, gj, gk),
        in_specs=in_specs, out_specs=pl.BlockSpec((tm, tn), o_map),
        out_shape=jax.ShapeDtypeStruct((out_rows, n), out_dtype),
        scratch_shapes=[pltpu.VMEM((tm, tn), F32)] if gk > 1 else [],
        input_output_aliases=aliases,
        compiler_params=_cp(("parallel", "parallel", "arbitrary")),
    )(*args)


def _ffn_fwd(x, g, wg_t, wu_t, wd, name, next_gain=None, loss_of=None):
    tm, fc = PERM_TM, 256
    nc = FF // fc
    n_in = 5 + (1 if next_gain is not None else 0) + (2 if loss_of is not None else 0)

    def body(*refs):
        x_ref, g_ref, wg_ref, wu_ref, wd_ref = refs[:5]
        extra_in, outs = refs[5:n_in], refs[n_in:]
        act_ref = outs[-1]
        xv = x_ref[...]
        r = lax.rsqrt(jnp.mean(xv * xv, axis=-1, keepdims=True) + EPS)
        h = (xv * r * g_ref[...]).astype(BF16)
        gg_ref, uu_ref = (outs[0], outs[1]) if loss_of is not None else (outs[1], outs[2])
        for c in range(nc):
            sl = pl.ds(c * fc, fc)
            gg = _dot(h, wg_ref[sl, :], NT)
            uu = _dot(h, wu_ref[sl, :], NT)
            gg_ref[:, sl] = gg.astype(BF16)
            uu_ref[:, sl] = uu.astype(BF16)
            act_ref[:, sl] = (gg * _sig(gg) * uu).astype(BF16)
        y = xv + 0.5 * _dot(act_ref[...], wd_ref[...], NN)
        if loss_of is not None:
            _final_math(y, extra_in[0][...], extra_in[1][...], outs[2], outs[3], outs[4], pl.program_id(0))
            return
        outs[0][...] = y
        if next_gain is not None:
            tile = outs[-2]
            r2 = lax.rsqrt(jnp.mean(y * y, axis=-1, keepdims=True) + EPS)
            hv = y * r2 * extra_in[0][...]
            outs[3][...] = hv.astype(BF16)
            _put_tile(tile, hv)
            for dil, p_ref in zip(DILS, outs[4:4 + len(DILS)]):
                _store_perm(p_ref, tile, dil)

    wspec = pl.BlockSpec((FF, D), lambda i: (0, 0), pipeline_mode=pl.Buffered(1))
    row_d = pl.BlockSpec((tm, D), lambda i: (i, 0))
    row_f = pl.BlockSpec((tm, FF), lambda i: (i, 0))
    in_specs = [row_d, _const_spec((1, D)), wspec, wspec, wspec]
    args = [x, g, wg_t, wu_t, wd]
    f_shape = jax.ShapeDtypeStruct((T, FF), BF16)
    scratch = [pltpu.VMEM((tm, FF), BF16)]
    if loss_of is not None:
        in_specs += [_const_spec((1, D)), row_d]
        args += list(loss_of)
        out_specs = [row_f, row_f, row_d, _const_spec((1, D)), _const_spec((1, 128))]
        out_shape = [f_shape, f_shape, jax.ShapeDtypeStruct((T, D), F32), jax.ShapeDtypeStruct((1, D), F32),
                     jax.ShapeDtypeStruct((1, 128), F32)]
    else:
        out_specs = [row_d, row_f, row_f]
        out_shape = [jax.ShapeDtypeStruct((T, D), F32), f_shape, f_shape]
        if next_gain is not None:
            in_specs.append(_const_spec((1, D)))
            args.append(next_gain)
            out_specs += [row_d] + [_perm_spec(d, D) for d in DILS]
            out_shape += [jax.ShapeDtypeStruct((T, D), BF16)] + [_perm_shape(d, D, BF16) for d in DILS]
            scratch = [_tile_scratch(D)] + scratch
    out = pl.pallas_call(
        body, name=name, grid=(T // tm,), in_specs=in_specs, out_specs=out_specs, out_shape=out_shape,
        scratch_shapes=scratch,
        compiler_params=_cp(("arbitrary",) if loss_of is not None else ("parallel",)),
    )(*args)
    if next_gain is not None:
        return out[0], out[1], out[2], [out[3]] + [o.reshape(T, D) for o in out[4:]]
    return tuple(out)


def _ffn_bwd(x, g, gg_all, uu_all, dout, wg_t, wu_t, wd, name):
    tm, fc = 256, 256
    nc = FF // fc

    def body(x_ref, g_ref, gg_ref, uu_ref, do_ref, wg_ref, wu_ref, wd_ref,
             dx_ref, dgam_ref, dg_ref, du_ref, act_ref, h_ref, db_ref):
        i = pl.program_id(0)
        xv = x_ref[...]
        r = lax.rsqrt(jnp.mean(xv * xv, axis=-1, keepdims=True) + EPS)
        xhat = xv * r
        gam = g_ref[...]
        h_ref[...] = (xhat * gam).astype(BF16)
        dov = do_ref[...]
        dbv = (0.5 * dov).astype(BF16)
        db_ref[...] = dbv
        for c in range(nc):
            sl = pl.ds(c * fc, fc)
            da = _dot(dbv, wd_ref[sl, :], NT)
            gg = gg_ref[:, sl].astype(F32)
            uu = uu_ref[:, sl].astype(F32)
            s = _sig(gg)
            si = gg * s
            dgv = (da * uu * (s * (1.0 + gg * (1.0 - s)))).astype(BF16)
            duv = (da * si).astype(BF16)
            dg_ref[:, sl] = dgv
            du_ref[:, sl] = duv
            act_ref[:, sl] = (si * uu).astype(BF16)
        dh = _dot(dg_ref[...], wg_ref[...], NN) + _dot(du_ref[...], wu_ref[...], NN)

        @pl.when(i == 0)
        def _():
            dgam_ref[...] = jnp.zeros_like(dgam_ref)

        dgam_ref[...] += jnp.sum(dh * xhat, axis=0, keepdims=True)
        dxh = dh * gam
        dx_ref[...] = dov + r * (dxh - xhat * jnp.mean(dxh * xhat, axis=-1, keepdims=True))

    wspec = pl.BlockSpec((FF, D), lambda i: (0, 0), pipeline_mode=pl.Buffered(1))
    row_d = pl.BlockSpec((tm, D), lambda i: (i, 0))
    row_f = pl.BlockSpec((tm, FF), lambda i: (i, 0))
    return pl.pallas_call(
        body, name=name, grid=(T // tm,),
        in_specs=[row_d, _const_spec((1, D)), row_f, row_f, row_d, wspec, wspec, wspec],
        out_specs=[row_d, _const_spec((1, D)), row_f, row_f, row_f, row_d, row_d],
        out_shape=[jax.ShapeDtypeStruct((T, D), F32), jax.ShapeDtypeStruct((1, D), F32),
                   jax.ShapeDtypeStruct((T, FF), BF16), jax.ShapeDtypeStruct((T, FF), BF16),
                   jax.ShapeDtypeStruct((T, FF), BF16), jax.ShapeDtypeStruct((T, D), BF16),
                   jax.ShapeDtypeStruct((T, D), BF16)],
        compiler_params=_cp(("arbitrary",)),
    )(x, g, gg_all, uu_all, dout, wg_t, wu_t, wd)


def _ffn_bwd_pre(x, g, gg_all, uu_all, dout, wd, name):
    tm, fc = 512, 256
    nc = FF // fc

    def body(x_ref, g_ref, gg_ref, uu_ref, do_ref, wd_ref, dg_ref, du_ref, act_ref, h_ref, db_ref):
        xv = x_ref[...]
        r = lax.rsqrt(jnp.mean(xv * xv, axis=-1, keepdims=True) + EPS)
        h_ref[...] = (xv * r * g_ref[...]).astype(BF16)
        dbv = (0.5 * do_ref[...]).astype(BF16)
        db_ref[...] = dbv
        for c in range(nc):
            sl = pl.ds(c * fc, fc)
            da = _dot(dbv, wd_ref[sl, :], NT)
            gg = gg_ref[:, sl].astype(F32)
            uu = uu_ref[:, sl].astype(F32)
            s = _sig(gg)
            si = gg * s
            dg_ref[:, sl] = (da * uu * (s * (1.0 + gg * (1.0 - s)))).astype(BF16)
            du_ref[:, sl] = (da * si).astype(BF16)
            act_ref[:, sl] = (si * uu).astype(BF16)

    wspec = pl.BlockSpec((FF, D), lambda i: (0, 0), pipeline_mode=pl.Buffered(1))
    row_d = pl.BlockSpec((tm, D), lambda i: (i, 0))
    row_f = pl.BlockSpec((tm, FF), lambda i: (i, 0))
    return pl.pallas_call(
        body, name=name, grid=(T // tm,),
        in_specs=[row_d, _const_spec((1, D)), row_f, row_f, row_d, wspec],
        out_specs=[row_f, row_f, row_f, row_d, row_d],
        out_shape=[jax.ShapeDtypeStruct((T, FF), BF16), jax.ShapeDtypeStruct((T, FF), BF16),
                   jax.ShapeDtypeStruct((T, FF), BF16), jax.ShapeDtypeStruct((T, D), BF16),
                   jax.ShapeDtypeStruct((T, D), BF16)],
        compiler_params=_cp(("parallel",)),
    )(x, g, gg_all, uu_all, dout, wd)


def _ffn_bwd_dx(x, g, dgb, dub, dout, wg_t, wu_t, name):
    tm = 512

    def body(x_ref, g_ref, dg_ref, du_ref, do_ref, wg_ref, wu_ref, dx_ref, dgam_ref):
        i = pl.program_id(0)
        xv = x_ref[...]
        r = lax.rsqrt(jnp.mean(xv * xv, axis=-1, keepdims=True) + EPS)
        xhat = xv * r
        gam = g_ref[...]
        dh = _dot(dg_ref[...], wg_ref[...], NN) + _dot(du_ref[...], wu_ref[...], NN)

        @pl.when(i == 0)
        def _():
            dgam_ref[...] = jnp.zeros_like(dgam_ref)

        dgam_ref[...] += jnp.sum(dh * xhat, axis=0, keepdims=True)
        dxh = dh * gam
        dx_ref[...] = do_ref[...] + r * (dxh - xhat * jnp.mean(dxh * xhat, axis=-1, keepdims=True))

    wspec = pl.BlockSpec((FF, D), lambda i: (0, 0), pipeline_mode=pl.Buffered(1))
    row_d = pl.BlockSpec((tm, D), lambda i: (i, 0))
    row_f = pl.BlockSpec((tm, FF), lambda i: (i, 0))
    return pl.pallas_call(
        body, name=name, grid=(T // tm,),
        in_specs=[row_d, _const_spec((1, D)), row_f, row_f, row_d, wspec, wspec],
        out_specs=[row_d, _const_spec((1, D))],
        out_shape=[jax.ShapeDtypeStruct((T, D), F32), jax.ShapeDtypeStruct((1, D), F32)],
        compiler_params=_cp(("arbitrary",)),
    )(x, g, dgb, dub, dout, wg_t, wu_t)


def _wgrad(a, b, m, n, name):
    tm = m // 2 if m == FF else m
    return _mm(a, b, mode="tn", m=m, n=n, k=T, tm=tm, tn=n, tk=min(T, 2048), out_dtype=BF16, name=name)


PERM_TM = 512
DILS = tuple(d for _, d in GROUPS if d > 1)


def _perm_spec(dil, cols):
    return pl.BlockSpec((dil, PERM_TM // dil, cols), lambda i: (0, i, 0))


def _perm_shape(dil, cols, dtype):
    return jax.ShapeDtypeStruct((dil, T // dil, cols), dtype)


LANES = 128


def _tile_scratch(cols):
    return pltpu.VMEM((cols // LANES, PERM_TM, LANES), F32)


def _put_tile(tile, value):
    for c in range(tile.shape[0]):
        tile[c] = value[:, c * LANES:(c + 1) * LANES]


def _get_tile(tile):
    return jnp.concatenate([tile[c] for c in range(tile.shape[0])], axis=1)


def _store_perm(out_ref, tile, dil):
    for r in range(dil):
        for c in range(tile.shape[0]):
            out_ref[r, :, pl.ds(c * LANES, LANES)] = tile[c, pl.ds(r, PERM_TM // dil, stride=dil), :].astype(
                out_ref.dtype)


def _load_unperm(in_ref, tile, dil):
    for r in range(dil):
        for c in range(tile.shape[0]):
            tile[c, pl.ds(r, PERM_TM // dil, stride=dil), :] = in_ref[r, :, pl.ds(c * LANES, LANES)].astype(F32)


def _final_math(xv, gam, tgt, dx_ref, dgam_ref, loss_ref, i):
    r = lax.rsqrt(jnp.mean(xv * xv, axis=-1, keepdims=True) + EPS)
    xhat = xv * r
    err = xhat * gam - tgt
    part = 0.5 * jnp.sum(jnp.mean(err * err, axis=-1, keepdims=True), axis=0, keepdims=True)
    dy = err * (1.0 / D)

    @pl.when(i == 0)
    def _():
        dgam_ref[...] = jnp.zeros_like(dgam_ref)
        loss_ref[...] = jnp.zeros_like(loss_ref)

    dgam_ref[...] += jnp.sum(dy * xhat, axis=0, keepdims=True)
    loss_ref[...] += jnp.broadcast_to(part, loss_ref.shape)
    dxh = dy * gam
    dx_ref[...] = r * (dxh - xhat * jnp.mean(dxh * xhat, axis=-1, keepdims=True))


QKV_BLOCK0 = 2 * D // AW


def _rms_bwd(x, g, dh0, dlin, lin_blocks, dqkvs, win_t, dres, name):
    tm = PERM_TM
    dils = [d for _, d in GROUPS]
    ng = len(dils)
    nl = len(lin_blocks)
    assert len(dqkvs) == ng

    def body(*refs):
        x_ref, g_ref, dh0_ref, dl_ref = refs[:4]
        dq_refs = refs[4:4 + 3 * ng]
        w_refs = refs[4 + 3 * ng:4 + 6 * ng]
        wl_refs = refs[4 + 6 * ng:4 + 6 * ng + nl]
        dr_ref, dx_ref, dgam_ref = refs[4 + 6 * ng + nl:7 + 6 * ng + nl]
        tile = refs[7 + 6 * ng + nl]
        stages = refs[8 + 6 * ng + nl:]
        i = pl.program_id(0)
        xv = x_ref[...]
        r = lax.rsqrt(jnp.mean(xv * xv, axis=-1, keepdims=True) + EPS)
        xhat = xv * r
        gam = g_ref[...]
        dh = dh0_ref[...]
        for q in range(nl):
            dh = dh + _dot(dl_ref[:, pl.ds(q * AW, AW)], wl_refs[q][...], NN)
        si = 0
        for gi, dil in enumerate(dils):
            part = None
            for p in range(3):
                blk = dq_refs[3 * gi + p][...]
                term = _dot(blk.reshape(tm, AW), w_refs[3 * gi + p][...], NN)
                part = term if part is None else part + term
            if dil > 1:
                stage = stages[si]
                si += 1
                stage[...] = part.reshape(dil, tm // dil, D)
                _load_unperm(stage, tile, dil)
                part = _get_tile(tile)
            dh = dh + part

        @pl.when(i == 0)
        def _():
            dgam_ref[...] = jnp.zeros_like(dgam_ref)

        dgam_ref[...] += jnp.sum(dh * xhat, axis=0, keepdims=True)
        dxh = dh * gam
        dx_ref[...] = dr_ref[...] + r * (dxh - xhat * jnp.mean(dxh * xhat, axis=-1, keepdims=True))

    row_d = pl.BlockSpec((tm, D), lambda i: (i, 0))
    dq_specs, dq_args, w_specs = [], [], []
    for gi, (d, a) in enumerate(zip(dils, dqkvs)):
        for p in range(3):
            if d == 1:
                dq_specs.append(pl.BlockSpec((None, tm, AW), lambda i, p=p: (p, i, 0)))
                dq_args.append(a)
            else:
                dq_specs.append(pl.BlockSpec((None, d, tm // d, AW), lambda i, p=p: (p, 0, i, 0)))
                dq_args.append(a.reshape(3, d, T // d, AW))
            w_specs.append(pl.BlockSpec((AW, D), lambda i, q=QKV_BLOCK0 + gi + 3 * p: (q, 0),
                                        pipeline_mode=pl.Buffered(1)))
    return pl.pallas_call(
        body, name=name, grid=(T // tm,),
        in_specs=[row_d, _const_spec((1, D)), row_d, pl.BlockSpec((tm, nl * AW), lambda i: (i, 0))] + dq_specs + w_specs
        + [pl.BlockSpec((AW, D), lambda i, q=q: (q, 0), pipeline_mode=pl.Buffered(1)) for q in lin_blocks] + [row_d],
        out_specs=[row_d, _const_spec((1, D))],
        out_shape=[jax.ShapeDtypeStruct((T, D), F32), jax.ShapeDtypeStruct((1, D), F32)],
        scratch_shapes=[_tile_scratch(D)] + [pltpu.VMEM((d, tm // d, D), F32) for d in dils if d > 1],
        compiler_params=_cp(("arbitrary",)),
    )(x, g, dh0, dlin, *dq_args, *([win_t] * (3 * ng + nl)), dres)


CONV_TM = 256
CONV_HALO = 32
CONV_RB = 16


def _glu(ab):
    ab = ab.astype(F32)
    return ab[:, :D] * _sig(ab[:, D:])


def _ln_stats(z1):
    mu = jnp.mean(z1, axis=-1, keepdims=True)
    zc = z1 - mu
    rstd = lax.rsqrt(jnp.mean(zc * zc, axis=-1, keepdims=True) + EPS)
    return zc * rstd, rstd


def _fill_shifts(zs):
    n = zs.shape[1] - 8
    for s in range(1, 8):
        zs[s, pl.ds(0, n), :] = zs[0, pl.ds(s, n), :]


def _shifted(zs, start, rows):
    q, s = divmod(start, 8)
    return zs[s, pl.ds(8 * q, rows), :]


def _conv_fwd(ab, kern, dwb, lng, lnb, name, guest=None):
    tm, hl, rb = CONV_TM, CONV_HALO, CONV_RB
    off = hl - (CONV_W - 1)
    if guest is not None:
        g_a, g_b, g_blocks, g_rows = guest
        g_nblk = len(g_blocks)

    def body(ab_ref, abh_ref, k_ref, dwb_ref, lng_ref, lnb_ref, *rest):
        if guest is not None:
            ga_ref, gb_refs, rest = rest[0], rest[1:1 + g_nblk], rest[1 + g_nblk:]
            z1_ref, z3_ref, go_ref, zs = rest
            for q, gb_ref in enumerate(gb_refs):
                go_ref[:, pl.ds(q * g_rows, g_rows)] = _dot(ga_ref[...], gb_ref[...], NT).astype(BF16)
        else:
            z1_ref, z3_ref, zs = rest
        i = pl.program_id(0)
        zs[0, pl.ds(0, hl), :] = jnp.where(i > 0, _glu(abh_ref[...]), 0.0)
        zs[0, pl.ds(hl, tm), :] = _glu(ab_ref[...])
        _fill_shifts(zs)
        for b in range(tm // rb):
            acc = jnp.zeros((rb, D), F32)
            for j in range(CONV_W):
                acc = acc + _shifted(zs, b * rb + off + j, rb) * k_ref[pl.ds(j, 1), :]
            z1 = acc + dwb_ref[...]
            z1_ref[pl.ds(b * rb, rb), :] = z1
            zn, _ = _ln_stats(z1)
            z2 = zn * lng_ref[...] + lnb_ref[...]
            z3_ref[pl.ds(b * rb, rb), :] = (z2 * _sig(z2)).astype(BF16)

    row = pl.BlockSpec((tm, D), lambda i: (i, 0))
    g_specs, g_args, g_ospecs, g_oshapes = [], [], [], []
    if guest is not None:
        kdim = g_a.shape[1]
        g_specs = [pl.BlockSpec((tm, kdim), lambda i: (i, 0))]
        g_specs += [pl.BlockSpec((g_rows, kdim), lambda i, q=q: (q, 0), pipeline_mode=pl.Buffered(1))
                    for q in g_blocks]
        g_args = [g_a] + [g_b] * g_nblk
        g_ospecs = [pl.BlockSpec((tm, g_nblk * g_rows), lambda i: (i, 0))]
        g_oshapes = [jax.ShapeDtypeStruct((T, g_nblk * g_rows), BF16)]
    return pl.pallas_call(
        body, name=name, grid=(T // tm,),
        in_specs=[pl.BlockSpec((tm, 2 * D), lambda i: (i, 0)),
                  pl.BlockSpec((hl, 2 * D), lambda i: (jnp.maximum(i * (tm // hl) - 1, 0), 0)),
                  _const_spec((32, D)), _const_spec((1, D)), _const_spec((1, D)), _const_spec((1, D))] + g_specs,
        out_specs=[row, row] + g_ospecs,
        out_shape=[jax.ShapeDtypeStruct((T, D), F32), jax.ShapeDtypeStruct((T, D), BF16)] + g_oshapes,
        scratch_shapes=[pltpu.VMEM((8, hl + tm, D), F32)],
        compiler_params=_cp(("parallel",)),
    )(ab, ab, kern, dwb, lng, lnb, *g_args)


GUEST_TM = 256


def _conv_bwd(dz3, z1, ab, kern, lng, lnb, name, guest_lhs=(), guest_rhs=None):
    tm, hl, rb = CONV_TM, CONV_HALO, CONV_RB
    off = hl - (CONV_W - 1)
    nsteps = T // tm
    ng = len(guest_lhs)
    gblocks = [a.shape[1] // GUEST_TM for a in guest_lhs]
    assert all(gb <= nsteps and gb * GUEST_TM == a.shape[1] for gb, a in zip(gblocks, guest_lhs))

    def ln_bwd(dz3v, z1v, lngv, lnbv):
        zn, rstd = _ln_stats(z1v)
        z2 = zn * lngv + lnbv
        s = _sig(z2)
        dz2 = dz3v * (s * (1.0 + z2 * (1.0 - s)))
        dzn = dz2 * lngv
        dz1 = rstd * (dzn - jnp.mean(dzn, axis=-1, keepdims=True)
                      - zn * jnp.mean(dzn * zn, axis=-1, keepdims=True))
        return dz1, dz2, zn

    def body(dz3_ref, dz3h_ref, z1_ref, z1h_ref, ab_ref, abh_ref, k_ref, lng_ref, lnb_ref, *rest):
        g_in, rest = rest[:ng + (1 if ng else 0)], rest[ng + (1 if ng else 0):]
        dab_ref, dk_ref, dvec_ref = rest[:3]
        g_out, (zs, dzs) = rest[3:3 + ng], rest[3 + ng:]
        i = pl.program_id(0)
        lngv, lnbv = lng_ref[...], lnb_ref[...]

        for a_ref, o_ref, gb in zip(g_in[:ng], g_out, gblocks):
            @pl.when(i < gb)
            def _(a_ref=a_ref, o_ref=o_ref):
                o_ref[...] = _dot(a_ref[...], g_in[ng][...], TN).astype(BF16)

        @pl.when(i == 0)
        def _():
            dk_ref[...] = jnp.zeros_like(dk_ref)
            dvec_ref[...] = jnp.zeros_like(dvec_ref)

        dz1, dz2, zn = ln_bwd(dz3_ref[...].astype(F32), z1_ref[...], lngv, lnbv)
        dvec_ref[pl.ds(0, 1), :] += jnp.sum(dz1, axis=0, keepdims=True)
        dvec_ref[pl.ds(1, 1), :] += jnp.sum(dz2 * zn, axis=0, keepdims=True)
        dvec_ref[pl.ds(2, 1), :] += jnp.sum(dz2, axis=0, keepdims=True)
        dzs[0, pl.ds(0, tm), :] = dz1
        dz1h, _, _ = ln_bwd(dz3h_ref[...].astype(F32), z1h_ref[...], lngv, lnbv)
        dzs[0, pl.ds(tm, hl), :] = jnp.where(i < nsteps - 1, dz1h, 0.0)
        _fill_shifts(dzs)
        zs[0, pl.ds(0, hl), :] = jnp.where(i > 0, _glu(abh_ref[...]), 0.0)
        zs[0, pl.ds(hl, tm), :] = _glu(ab_ref[...])
        _fill_shifts(zs)

        for j in range(CONV_W):
            tot = jnp.zeros((rb, D), F32)
            for b in range(tm // rb):
                tot = tot + dzs[0, pl.ds(b * rb, rb), :] * _shifted(zs, b * rb + off + j, rb)
            dk_ref[pl.ds(j, 1), :] += jnp.sum(tot, axis=0, keepdims=True)

        for b in range(tm // rb):
            acc = jnp.zeros((rb, D), F32)
            for j in range(CONV_W):
                acc = acc + _shifted(dzs, b * rb + (CONV_W - 1) - j, rb) * k_ref[pl.ds(j, 1), :]
            av = ab_ref[pl.ds(b * rb, rb), pl.ds(0, D)].astype(F32)
            sb = _sig(ab_ref[pl.ds(b * rb, rb), pl.ds(D, D)].astype(F32))
            dab_ref[pl.ds(b * rb, rb), pl.ds(0, D)] = (acc * sb).astype(BF16)
            dab_ref[pl.ds(b * rb, rb), pl.ds(D, D)] = (acc * av * sb * (1.0 - sb)).astype(BF16)

    row = pl.BlockSpec((tm, D), lambda i: (i, 0))
    nxt = pl.BlockSpec((hl, D), lambda i: (jnp.minimum((i + 1) * (tm // hl), T // hl - 1), 0))
    g_specs, g_args, g_ospecs, g_oshapes = [], [], [], []
    for a, gb in zip(guest_lhs, gblocks):
        g_specs.append(pl.BlockSpec((T, GUEST_TM), lambda i, gb=gb: (0, jnp.minimum(i, gb - 1))))
        g_args.append(a)
        g_ospecs.append(pl.BlockSpec((GUEST_TM, guest_rhs.shape[1]), lambda i, gb=gb: (jnp.minimum(i, gb - 1), 0)))
        g_oshapes.append(jax.ShapeDtypeStruct((a.shape[1], guest_rhs.shape[1]), BF16))
    if ng:
        g_specs.append(pl.BlockSpec(guest_rhs.shape, lambda i: (0, 0), pipeline_mode=pl.Buffered(1)))
        g_args.append(guest_rhs)
    return pl.pallas_call(
        body, name=name, grid=(nsteps,),
        in_specs=[row, nxt, row, nxt,
                  pl.BlockSpec((tm, 2 * D), lambda i: (i, 0)),
                  pl.BlockSpec((hl, 2 * D), lambda i: (jnp.maximum(i * (tm // hl) - 1, 0), 0)),
                  _const_spec((32, D)), _const_spec((1, D)), _const_spec((1, D))] + g_specs,
        out_specs=[pl.BlockSpec((tm, 2 * D), lambda i: (i, 0)), _const_spec((32, D)), _const_spec((8, D))]
        + g_ospecs,
        out_shape=[jax.ShapeDtypeStruct((T, 2 * D), BF16), jax.ShapeDtypeStruct((32, D), F32),
                   jax.ShapeDtypeStruct((8, D), F32)] + g_oshapes,
        scratch_shapes=[pltpu.VMEM((8, hl + tm, D), F32), pltpu.VMEM((8, tm + hl, D), F32)],
        compiler_params=_cp(("arbitrary",)),
    )(dz3, dz3, z1, z1, ab, ab, kern, lng, lnb, *g_args)


def _alibi_slopes():
    h = np.arange(1, 3 * NHG + 1, dtype=np.float32)
    return np.power(np.float32(2.0), -8.0 * h / np.float32(3 * NHG)).astype(np.float32)


def _band_bias(gi):
    _, dil = GROUPS[gi]
    slopes = _alibi_slopes()[gi * NHG:(gi + 1) * NHG]
    qi = np.arange(BLK)[:, None]
    ki = np.arange(2 * BLK)[None, :]
    steps = BLK + qi - ki
    band = (steps >= 0) & (steps <= BLK)
    bias = -slopes[:, None, None] * (dil * steps).astype(np.float32)[None]
    return jnp.asarray(np.where(band[None], bias, np.float32(NEG)).astype(np.float32))


QB_FWD = 8
QB_BWD = 32


def _attn_specs(qb, c0):
    prev = lambda n: jnp.maximum(n * qb - 1, 0)
    return [pl.BlockSpec((qb * BLK, HEAD), lambda h, n: (n, c0 + h)),
            pl.BlockSpec((BLK, HEAD), lambda h, n: (prev(n), c0 + NHG + h)),
            pl.BlockSpec((qb * BLK, HEAD), lambda h, n: (n, c0 + NHG + h)),
            pl.BlockSpec((BLK, HEAD), lambda h, n: (prev(n), c0 + 2 * NHG + h)),
            pl.BlockSpec((qb * BLK, HEAD), lambda h, n: (n, c0 + 2 * NHG + h)),
            pl.BlockSpec((None, BLK, 2 * BLK), lambda h, n: (h, 0, 0))]


def _scores(q, kcat, bias, blk, seg):
    s = _dot(q, kcat, NT) * (HEAD ** -0.5) + bias
    col = lax.broadcasted_iota(jnp.int32, s.shape, 1)
    first = (blk % seg) == 0
    return jnp.where(jnp.logical_and(first, col < BLK), NEG, s)


def _attn_fwd(qkv, gi, name, col0=0):
    seg = (T // GROUPS[gi][1]) // BLK

    qb = min(QB_FWD, T // BLK)

    def body(q_ref, kp_ref, kc_ref, vp_ref, vc_ref, bias_ref, o_ref, l_ref):
        n = pl.program_id(0)
        for h in range(NHG):
            cols = pl.ds(h * HEAD, HEAD)
            kwin = jnp.concatenate([kp_ref[:, cols], kc_ref[:, cols]], axis=0)
            vwin = jnp.concatenate([vp_ref[:, cols], vc_ref[:, cols]], axis=0)
            bias = bias_ref[h]
            for b in range(qb):
                rows = pl.ds(b * BLK, BLK)
                s = _scores(q_ref[rows, cols], kwin[b * BLK:(b + 2) * BLK], bias, n * qb + b, seg)
                mx = jnp.max(s, axis=-1, keepdims=True)
                p = jnp.exp(s - mx)
                den = jnp.sum(p, axis=-1, keepdims=True)
                o_ref[rows, cols] = (_dot(p.astype(BF16), vwin[b * BLK:(b + 2) * BLK], NN) / den).astype(BF16)
                l_ref[rows, cols] = jnp.broadcast_to(mx + jnp.log(den), (BLK, HEAD))

    prev = lambda n: jnp.maximum(n * qb - 1, 0)
    c0 = col0 // AW
    cur = lambda part: pl.BlockSpec((qb * BLK, AW), lambda n: (n, part))
    halo = lambda part: pl.BlockSpec((BLK, AW), lambda n: (prev(n), part))
    return pl.pallas_call(
        body, name=name, grid=(T // (qb * BLK),),
        in_specs=[cur(c0), halo(c0 + 1), cur(c0 + 1), halo(c0 + 2), cur(c0 + 2),
                  _const_spec((NHG, BLK, 2 * BLK))],
        out_specs=[cur(0), cur(0)],
        out_shape=[jax.ShapeDtypeStruct((T, AW), BF16), jax.ShapeDtypeStruct((T, AW), F32)],
        compiler_params=_cp(("parallel",)),
    )(qkv, qkv, qkv, qkv, qkv, _band_bias(gi))


def _attn_bwd(qkv, dob, lse, delta, gi, name, col0=0):
    seg = (T // GROUPS[gi][1]) // BLK
    qb = min(QB_BWD, T // BLK)
    nb = T // (qb * BLK)
    scale = HEAD ** -0.5

    def body(q_ref, kp_ref, kc_ref, vp_ref, vc_ref, bias_ref, do_ref, l_ref, dl_ref, out_ref, dk_acc, dv_acc):
        n = pl.program_id(1)
        kwin = jnp.concatenate([kp_ref[...], kc_ref[...]], axis=0)
        vwin = jnp.concatenate([vp_ref[...], vc_ref[...]], axis=0)
        bias = bias_ref[...]
        head = pl.program_id(0)

        def my_lse(rows):
            col = l_ref[rows, pl.ds(0, 1)]
            for h in range(1, NHG):
                col = jnp.where(head == h, l_ref[rows, pl.ds(h * (LANES // NHG), 1)], col)
            return col

        dks, dvs = [], []
        for b in range(qb):
            rows = pl.ds(b * BLK, BLK)
            q = q_ref[rows, :]
            kcat = kwin[b * BLK:(b + 2) * BLK]
            s = _scores(q, kcat, bias, n * qb + b, seg)
            p = jnp.exp(s - my_lse(rows))
            dov = do_ref[rows, :]
            dvs.append(_dot(p.astype(BF16), dov, TN))
            dp = _dot(dov, vwin[b * BLK:(b + 2) * BLK], NT)
            dsb = (p * (dp - dl_ref[rows, pl.ds(0, 1)]) * scale).astype(BF16)
            row = pl.ds(pl.multiple_of((n * qb + b) * BLK, BLK), BLK)
            out_ref[0, row, :] = _dot(dsb, kcat, NN).astype(BF16)
            dks.append(_dot(dsb, q, TN))
        for b in range(qb):
            row = pl.ds(pl.multiple_of((n * qb + b) * BLK, BLK), BLK)
            if b + 1 < qb:
                dk_acc[row, :] = dks[b][BLK:] + dks[b + 1][:BLK]
                dv_acc[row, :] = dvs[b][BLK:] + dvs[b + 1][:BLK]
            else:
                dk_acc[row, :] = dks[b][BLK:]
                dv_acc[row, :] = dvs[b][BLK:]

        @pl.when(n > 0)
        def _():
            prow = pl.ds(pl.multiple_of((n * qb - 1) * BLK, BLK), BLK)
            dk_acc[prow, :] += dks[0][:BLK]
            dv_acc[prow, :] += dvs[0][:BLK]

        @pl.when(n == nb - 1)
        def _():
            out_ref[1] = dk_acc[...].astype(BF16)
            out_ref[2] = dv_acc[...].astype(BF16)

    oblk = pl.BlockSpec((qb * BLK, HEAD), lambda h, n: (n, h))
    return pl.pallas_call(
        body, name=name, grid=(NHG, nb),
        in_specs=_attn_specs(qb, col0 // HEAD) + [oblk, pl.BlockSpec((qb * BLK, LANES), lambda h, n: (n, 0)), oblk],
        out_specs=pl.BlockSpec((3, T, HEAD), lambda h, n: (0, 0, h)),
        out_shape=jax.ShapeDtypeStruct((3, T, AW), BF16),
        scratch_shapes=[pltpu.VMEM((T, HEAD), F32), pltpu.VMEM((T, HEAD), F32)],
        compiler_params=_cp(("parallel", "arbitrary")),
    )(qkv, qkv, qkv, qkv, qkv, _band_bias(gi), dob, lse, delta)


def _merge(outs, lses, name):
    tm = PERM_TM
    dils = [d for _, d in GROUPS]
    ng = len(dils)

    def body(*refs):
        in_refs = refs[:2 * ng]
        ab_ref = refs[2 * ng]
        lse_refs = refs[2 * ng + 1:3 * ng + 1]
        tile, small = refs[-2:]

        def token_order(ref, dil):
            if dil == 1:
                return ref[...].astype(F32)
            _load_unperm(ref, tile, dil)
            return _get_tile(tile)

        os = [token_order(in_refs[2 * i], d) for i, d in enumerate(dils)]
        ls = [token_order(in_refs[2 * i + 1], d) for i, d in enumerate(dils)]
        mx = jnp.maximum(jnp.maximum(ls[0], ls[1]), ls[2])
        es = [jnp.exp(v - mx) for v in ls]
        tot = es[0] + es[1] + es[2]
        att = (es[0] / tot) * os[0] + (es[1] / tot) * os[1] + (es[2] / tot) * os[2]
        ab_ref[...] = att.astype(BF16)
        lse = mx + jnp.log(tot)
        lane = lax.broadcasted_iota(jnp.int32, (tm, LANES), 1)
        packed = lse[:, :HEAD]
        for h in range(1, NHG):
            packed = jnp.where(lane >= h * (LANES // NHG), lse[:, h * HEAD:(h + 1) * HEAD], packed)
        small[0] = packed
        for dil, ref in zip(dils, lse_refs):
            if dil == 1:
                ref[...] = packed
            else:
                _store_perm(ref, small, dil)

    row = pl.BlockSpec((tm, AW), lambda i: (i, 0))
    specs = [row if d == 1 else _perm_spec(d, AW) for d in dils]
    lspecs = [pl.BlockSpec((tm, LANES), lambda i: (i, 0)) if d == 1 else _perm_spec(d, LANES) for d in dils]
    args = []
    for d, o, l in zip(dils, outs, lses):
        args += [o, l] if d == 1 else [o.reshape(d, T // d, AW), l.reshape(d, T // d, AW)]
    out = pl.pallas_call(
        body, name=name, grid=(T // tm,),
        in_specs=[sp for sp in specs for _ in range(2)], out_specs=[row] + lspecs,
        out_shape=[jax.ShapeDtypeStruct((T, AW), BF16)]
        + [jax.ShapeDtypeStruct((T, LANES), F32) if d == 1 else _perm_shape(d, LANES, F32) for d in dils],
        scratch_shapes=[_tile_scratch(AW), _tile_scratch(LANES)],
        compiler_params=_cp(("parallel",)),
    )(*args)
    return out[0], [o.reshape(T, LANES) for o in out[1:]]


GATE_BLOCK0 = (IN_W - 2 * D) // (D // 2)


def _mix_out(z3b, attnb, gates, wc, wa_t, wo, x1, name):
    tm = 512

    def body(z_ref, a_ref, g_ref, wc_ref, wa_ref, wo_ref, x_ref, xo_ref, yc_ref, ya_ref, mx_ref):
        yc = _dot(z_ref[...], wc_ref[...], NN)
        ya = _dot(a_ref[...], wa_ref[...], NT)
        yc_ref[...] = yc.astype(BF16)
        ya_ref[...] = ya.astype(BF16)
        gv = g_ref[...].astype(F32)
        mixed = (_sig(gv[:, :D]) * yc + _sig(gv[:, D:]) * ya).astype(BF16)
        mx_ref[...] = mixed
        xo_ref[...] = x_ref[...] + _dot(mixed, wo_ref[...], NN)

    row = pl.BlockSpec((tm, D), lambda i: (i, 0))
    return pl.pallas_call(
        body, name=name, grid=(T // tm,),
        in_specs=[row, pl.BlockSpec((tm, AW), lambda i: (i, 0)), pl.BlockSpec((tm, 2 * D), lambda i: (i, 0)),
                  _const_spec((D, D)), _const_spec((D, AW)), _const_spec((D, D)), row],
        out_specs=[row, row, row, row],
        out_shape=[jax.ShapeDtypeStruct((T, D), F32), jax.ShapeDtypeStruct((T, D), BF16),
                   jax.ShapeDtypeStruct((T, D), BF16), jax.ShapeDtypeStruct((T, D), BF16)],
        compiler_params=_cp(("parallel",)),
    )(z3b, attnb, gates, wc, wa_t, wo, x1)


def _mix_out_bwd(dx2, gates, yc, ya, attn, wc, wa_t, wo, win_t, name):
    tm = PERM_TM
    dils = [d for _, d in GROUPS]
    ng = len(dils)

    def body(dx_ref, g_ref, yc_ref, ya_ref, at_ref, wc_ref, wa_ref, wo_ref, wg0_ref, wg1_ref, wg2_ref, wg3_ref,
             dg_ref, dyc_ref, dya_ref, dxb_ref, dz3_ref, dhg_ref, *rest):
        dat_refs, dl_refs, tile = rest[:ng], rest[ng:2 * ng], rest[-1]
        dxb = dx_ref[...].astype(BF16)
        dxb_ref[...] = dxb
        dmix = _dot(dxb, wo_ref[...], NT)
        gv = g_ref[...].astype(F32)
        sc = _sig(gv[:, :D])
        sa = _sig(gv[:, D:])
        ycv, yav = yc_ref[...].astype(F32), ya_ref[...].astype(F32)
        dgc = (dmix * ycv * sc * (1.0 - sc)).astype(BF16)
        dga = (dmix * yav * sa * (1.0 - sa)).astype(BF16)
        dg_ref[:, pl.ds(0, D)] = dgc
        dg_ref[:, pl.ds(D, D)] = dga
        half = D // 2
        dhg_ref[...] = (_dot(dgc[:, :half], wg0_ref[...], NN) + _dot(dgc[:, half:], wg1_ref[...], NN)
                        + _dot(dga[:, :half], wg2_ref[...], NN) + _dot(dga[:, half:], wg3_ref[...], NN))
        dyc = (dmix * sc).astype(BF16)
        dya = (dmix * sa).astype(BF16)
        dyc_ref[...] = dyc
        dya_ref[...] = dya
        dz3_ref[...] = _dot(dyc, wc_ref[...], NT).astype(BF16)
        dat = _dot(dya, wa_ref[...], NN)
        prod = dat * at_ref[...].astype(F32)
        delta = jnp.concatenate(
            [jnp.broadcast_to(jnp.sum(prod[:, h * HEAD:(h + 1) * HEAD], axis=-1, keepdims=True), (tm, HEAD))
             for h in range(NHG)], axis=1)
        for value, out_refs in ((dat, dat_refs), (delta, dl_refs)):
            _put_tile(tile, value)
            for dil, ref in zip(dils, out_refs):
                if dil == 1:
                    ref[...] = value.astype(ref.dtype)
                else:
                    _store_perm(ref, tile, dil)

    row = pl.BlockSpec((tm, D), lambda i: (i, 0))
    row2 = pl.BlockSpec((tm, 2 * D), lambda i: (i, 0))
    rowa = pl.BlockSpec((tm, AW), lambda i: (i, 0))
    aspecs = [rowa if d == 1 else _perm_spec(d, AW) for d in dils]

    def ashapes(dtype):
        return [jax.ShapeDtypeStruct((T, AW), dtype) if d == 1 else _perm_shape(d, AW, dtype) for d in dils]

    out = pl.pallas_call(
        body, name=name, grid=(T // tm,),
        in_specs=[row, row2, row, row, rowa, _const_spec((D, D)), _const_spec((D, AW)), _const_spec((D, D))]
        + [pl.BlockSpec((D // 2, D), lambda i, q=q: (GATE_BLOCK0 + q, 0), pipeline_mode=pl.Buffered(1))
           for q in range(4)],
        out_specs=[row2, row, row, row, row, row] + aspecs + aspecs,
        out_shape=[jax.ShapeDtypeStruct((T, 2 * D), BF16), jax.ShapeDtypeStruct((T, D), BF16),
                   jax.ShapeDtypeStruct((T, D), BF16), jax.ShapeDtypeStruct((T, D), BF16),
                   jax.ShapeDtypeStruct((T, D), BF16), jax.ShapeDtypeStruct((T, D), F32)]
        + ashapes(BF16) + ashapes(F32),
        scratch_shapes=[_tile_scratch(AW)],
        compiler_params=_cp(("parallel",)),
    )(dx2, gates, yc, ya, attn, wc, wa_t, wo, win_t, win_t, win_t, win_t)
    dats = [o.reshape(T, AW) for o in out[6:6 + ng]]
    deltas = [o.reshape(T, AW) for o in out[6 + ng:6 + 2 * ng]]
    return out[0], out[1], out[2], out[3], out[4], out[5], dats, deltas


def _peer(k):
    x, y, c = lax.axis_index("x"), lax.axis_index("y"), lax.axis_index("c")
    px = 1 - x if k & 4 else x
    py = 1 - y if k & 2 else y
    pc = 1 - c if k & 1 else c
    return (px, py, pc), 4 * px + 2 * py + pc


HBM_SPEC = pl.BlockSpec(memory_space=pltpu.HBM)
SEM_SPEC = pl.BlockSpec(memory_space=pltpu.SEMAPHORE)
EFFECT = pltpu.SideEffectType.DATAFLOW_SIDE_EFFECTING


def _my_place():
    return 4 * lax.axis_index("x") + 2 * lax.axis_index("y") + lax.axis_index("c")


def _tie(a, order_after, name):
    na = len(order_after)

    def body(*refs):
        del refs

    return pl.pallas_call(
        body, name=name, in_specs=[pl.BlockSpec(memory_space=pl.ANY)] * (1 + na),
        out_specs=pl.BlockSpec(memory_space=pl.ANY), out_shape=jax.ShapeDtypeStruct(a.shape, a.dtype),
        input_output_aliases={0: 0},
    )(a, *order_after)


def _prep_gather(ws, order_after, name):
    me = jnp.reshape(_my_place(), (1,)).astype(jnp.int32)
    n = len(ws)
    na = len(order_after)
    shapes = [((32, wv.shape[1]), F32) if wv.shape[0] == CONV_W else (wv.shape, BF16) for wv in ws]

    def body(me_ref, *refs):
        del me_ref
        ins, outs = refs[:n], refs[n + na:]
        for wv, i_ref, o_ref in zip(ws, ins, outs):
            if wv.shape[0] == CONV_W:
                o_ref[pl.ds(0, CONV_W), :] = i_ref[...]
                o_ref[pl.ds(CONV_W, 1), :] = jnp.zeros((1, wv.shape[1]), F32)
            else:
                o_ref[...] = i_ref[...].astype(BF16)

    grid_spec = pltpu.PrefetchScalarGridSpec(
        num_scalar_prefetch=1, grid=(1,),
        in_specs=[pl.BlockSpec(wv.shape, lambda i, m: (0, 0)) for wv in ws]
        + [pl.BlockSpec(memory_space=pl.ANY)] * na,
        out_specs=[pl.BlockSpec(shp, lambda i, m: (m[0], 0)) for shp, _ in shapes])
    return pl.pallas_call(
        body, name=name, grid_spec=grid_spec,
        out_shape=[jax.ShapeDtypeStruct((NDEV * shp[0], shp[1]), dt) for shp, dt in shapes],
        compiler_params=_cp(("arbitrary",)),
    )(me, *ws, *order_after)


GATHER_A = ((1, 0), (2, 0), (4, 0), (6, 0))
GATHER_B = ((1, 2), (1, 4), (1, 6))
GATHER_DIRECT = tuple((k, 0) for k in range(1, NDEV))


def _gather_start(lands, plan, order_after, name):
    n = len(lands)
    na = len(order_after)
    npl = len(plan)

    def body(*refs):
        land_refs = refs[:n]
        send, recv = refs[n + na], refs[n + na + 1]
        token = refs[-1]
        for w in range(n):
            rows = lands[w].shape[0] // NDEV
            for p, (k, j) in enumerate(plan):
                peer, _ = _peer(k)
                _, blk = _peer(j)
                part = land_refs[w].at[pl.ds(blk * rows, rows)]
                i = w * npl + p
                pltpu.make_async_remote_copy(src_ref=part, dst_ref=part, send_sem=send.at[i], recv_sem=recv.at[i],
                                             device_id=peer, device_id_type=MESH_ID).start()
        token[...] = jnp.zeros_like(token)

    nsem = n * npl
    bufs = [pltpu.with_memory_space_constraint(a, pltpu.HBM) for a in lands]
    out = pl.pallas_call(
        body, name=name,
        in_specs=[HBM_SPEC] * n + [pl.BlockSpec(memory_space=pl.ANY)] * na,
        out_specs=[SEM_SPEC, SEM_SPEC] + [HBM_SPEC] * n + [pl.BlockSpec(memory_space=pltpu.VMEM)],
        out_shape=[pltpu.SemaphoreType.DMA((nsem,)), pltpu.SemaphoreType.DMA((nsem,))]
        + [pltpu.HBM(a.shape, a.dtype) for a in bufs] + [jax.ShapeDtypeStruct((8, 128), F32)],
        input_output_aliases={i: 2 + i for i in range(n)},
        compiler_params=pltpu.CompilerParams(has_side_effects=EFFECT),
    )(*bufs, *order_after)
    return out[0], out[1], out[2:2 + n], out[-1]


def _gather_wait(started, plan, order_after, name):
    send, recv, lands, _ = started
    n = len(lands)
    na = len(order_after)
    npl = len(plan)

    def body(*refs):
        land_refs = refs[:n]
        send_ref, recv_ref = refs[n], refs[n + 1]
        for w in range(n):
            rows = lands[w].shape[0] // NDEV
            for p, (k, j) in enumerate(plan):
                peer, _ = _peer(k)
                _, blk = _peer(j)
                part = land_refs[w].at[pl.ds(blk * rows, rows)]
                i = w * npl + p
                cp = pltpu.make_async_remote_copy(src_ref=part, dst_ref=part, send_sem=send_ref.at[i],
                                                  recv_sem=recv_ref.at[i], device_id=peer, device_id_type=MESH_ID)
                cp.wait_send()
                cp.wait_recv()

    out = pl.pallas_call(
        body, name=name,
        in_specs=[HBM_SPEC] * n + [SEM_SPEC, SEM_SPEC] + [pl.BlockSpec(memory_space=pl.ANY)] * na,
        out_specs=[HBM_SPEC] * n,
        out_shape=[pltpu.HBM(a.shape, a.dtype) for a in lands],
        input_output_aliases={i: i for i in range(n)},
        compiler_params=pltpu.CompilerParams(has_side_effects=EFFECT),
    )(*lands, send, recv, *order_after)
    return list(out)


def _copy_ends(kind, src, land, me, plin, k):
    if kind == "scatter":
        rows = src.shape[0] // NDEV
        return src.at[pl.ds(plin * rows, rows)], land.at[k - 1]
    return src, land.at[me]


def _landing(kind, src):
    me = _my_place()
    if kind == "scatter":
        return lax.empty((NDEV - 1, src.shape[0] // NDEV) + src.shape[1:], src.dtype)
    land = lax.empty((NDEV,) + src.shape, src.dtype)
    return lax.dynamic_update_slice(land, src[None], (me,) + (0,) * src.ndim)


def _send_start(kinds, srcs, order_after, name):
    n = len(srcs)
    lands = [_landing(kd, s) for kd, s in zip(kinds, srcs)]
    na = len(order_after)

    def body(*refs):
        src_refs, land_refs = refs[:n], refs[n:2 * n]
        send, recv = refs[2 * n + na], refs[2 * n + na + 1]
        token = refs[-1]
        _, me = _peer(0)
        for w in range(n):
            for k in range(1, NDEV):
                peer, plin = _peer(k)
                s, d = _copy_ends(kinds[w], src_refs[w], land_refs[w], me, plin, k)
                i = w * (NDEV - 1) + k - 1
                pltpu.make_async_remote_copy(src_ref=s, dst_ref=d, send_sem=send.at[i], recv_sem=recv.at[i],
                                             device_id=peer, device_id_type=MESH_ID).start()
        token[...] = jnp.zeros_like(token)

    nsem = n * (NDEV - 1)
    bufs = [pltpu.with_memory_space_constraint(a, pltpu.HBM) for a in list(srcs) + lands]
    out = pl.pallas_call(
        body, name=name,
        in_specs=[HBM_SPEC] * (2 * n) + [pl.BlockSpec(memory_space=pl.ANY)] * na,
        out_specs=[SEM_SPEC, SEM_SPEC] + [HBM_SPEC] * (2 * n) + [pl.BlockSpec(memory_space=pltpu.VMEM)],
        out_shape=[pltpu.SemaphoreType.DMA((nsem,)), pltpu.SemaphoreType.DMA((nsem,))]
        + [pltpu.HBM(a.shape, a.dtype) for a in bufs] + [jax.ShapeDtypeStruct((8, 128), F32)],
        input_output_aliases={i: 2 + i for i in range(2 * n)},
        compiler_params=pltpu.CompilerParams(has_side_effects=EFFECT),
    )(*bufs, *order_after)
    return out[0], out[1], out[2:2 + n], out[2 + n:2 + 2 * n], out[-1]


def _send_wait(kinds, started, order_after, name):
    send, recv, srcs, lands, _ = started
    n = len(srcs)
    na = len(order_after)

    def body(*refs):
        src_refs, land_refs = refs[:n], refs[n:2 * n]
        send_ref, recv_ref = refs[2 * n], refs[2 * n + 1]
        _, me = _peer(0)
        for w in range(n):
            for k in range(1, NDEV):
                peer, plin = _peer(k)
                s, d = _copy_ends(kinds[w], src_refs[w], land_refs[w], me, plin, k)
                i = w * (NDEV - 1) + k - 1
                cp = pltpu.make_async_remote_copy(src_ref=s, dst_ref=d, send_sem=send_ref.at[i],
                                                  recv_sem=recv_ref.at[i], device_id=peer, device_id_type=MESH_ID)
                cp.wait_send()
                cp.wait_recv()

    bufs = list(srcs) + list(lands)
    out = pl.pallas_call(
        body, name=name,
        in_specs=[HBM_SPEC] * (2 * n) + [SEM_SPEC, SEM_SPEC] + [pl.BlockSpec(memory_space=pl.ANY)] * na,
        out_specs=[HBM_SPEC] * (2 * n),
        out_shape=[pltpu.HBM(a.shape, a.dtype) for a in bufs],
        input_output_aliases={i: i for i in range(2 * n)},
        compiler_params=pltpu.CompilerParams(has_side_effects=EFFECT),
    )(*bufs, send, recv, *order_after)
    return out[:n], out[n:]


def _gsum(own, land, name):
    rows, cols = own.shape
    tr = rows // 2 if rows * cols > 512 * 1024 and rows % 32 == 0 else rows

    def body(own_ref, l_ref, o_ref):
        tot = own_ref[...].astype(F32)
        for s in range(NDEV - 1):
            tot = tot + l_ref[s].astype(F32)
        o_ref[...] = tot

    return pl.pallas_call(
        body, name=name, grid=(rows // tr,),
        in_specs=[pl.BlockSpec((tr, cols), lambda i: (i, 0)),
                  pl.BlockSpec((NDEV - 1, tr, cols), lambda i: (0, i, 0))],
        out_specs=pl.BlockSpec((tr, cols), lambda i: (i, 0)),
        out_shape=jax.ShapeDtypeStruct((rows, cols), F32),
        compiler_params=_cp(("parallel",)),
    )(own, land)


def _adamw_math(w, g, m, v):
    m2 = B1 * m + (1.0 - B1) * g
    v2 = B2 * v + (1.0 - B2) * (g * g)
    m_hat = m2 / (1.0 - B1 ** STEP)
    v_hat = v2 / (1.0 - B2 ** STEP)
    delta = -LR * (m_hat / (jnp.sqrt(v_hat) + AEPS) + WD * w)
    return delta, m2, v2


def _adamw(w, g, m, v, name):
    rows, cols = w.shape
    tr = 256 if rows % 256 == 0 and rows > 256 else rows

    def body(w_ref, g_ref, m_ref, v_ref, d_ref, mo_ref, vo_ref):
        d, m2, v2 = _adamw_math(w_ref[...], g_ref[...], m_ref[...], v_ref[...])
        d_ref[...] = d
        mo_ref[...] = m2
        vo_ref[...] = v2

    blk = pl.BlockSpec((tr, cols), lambda i: (i, 0))
    return pl.pallas_call(
        body, name=name, grid=(rows // tr,), in_specs=[blk] * 4, out_specs=[blk] * 3,
        out_shape=[jax.ShapeDtypeStruct((rows, cols), F32)] * 3,
        compiler_params=_cp(("parallel",)),
    )(w, g, m, v)


UPD_TC = 256


def _update(src, land, w, m, v, name):
    rows, cols = land.shape[1:]
    tc = min(UPD_TC if rows > 512 else 2 * UPD_TC, cols)
    me = jnp.reshape(_my_place(), (1,)).astype(jnp.int32)

    def body(me_ref, own_ref, l_ref, w_ref, m_ref, v_ref, g_ref, d_ref, mo_ref, vo_ref):
        del me_ref
        g = own_ref[...].astype(F32)
        for s in range(NDEV - 1):
            g = g + l_ref[s].astype(F32)
        g_ref[...] = g
        d, m2, v2 = _adamw_math(w_ref[...], g, m_ref[...], v_ref[...])
        d_ref[...] = d
        mo_ref[...] = m2
        vo_ref[...] = v2

    wblk = pl.BlockSpec((rows, tc), lambda j, p: (0, j))
    grid_spec = pltpu.PrefetchScalarGridSpec(
        num_scalar_prefetch=1, grid=(cols // tc,),
        in_specs=[pl.BlockSpec((rows, tc), lambda j, p: (p[0], j)),
                  pl.BlockSpec((NDEV - 1, rows, tc), lambda j, p: (0, 0, j)), wblk, wblk, wblk],
        out_specs=[wblk] * 4)
    return pl.pallas_call(
        body, name=name, grid_spec=grid_spec, out_shape=[jax.ShapeDtypeStruct((rows, cols), F32)] * 4,
        compiler_params=_cp(("parallel",)),
    )(me, src, land, w, m, v)


def _small_update(vland, w8, m8, v8, name):
    def body(l_ref, w_ref, m_ref, v_ref, g_ref, d_ref, mo_ref, vo_ref):
        g = l_ref[0]
        for s in range(1, NDEV):
            g = g + l_ref[s]
        g_ref[...] = g
        d, m2, v2 = _adamw_math(w_ref[...], g, m_ref[...], v_ref[...])
        d_ref[...] = d
        mo_ref[...] = m2
        vo_ref[...] = v2

    return pl.pallas_call(
        body, name=name, out_shape=[jax.ShapeDtypeStruct((8, D), F32)] * 4,
        compiler_params=_cp(None),
    )(vland, w8, m8, v8)


def kernel(x, ffn1_norm, ffn1_w_gate, ffn1_w_up, ffn1_w_down, mix_norm, w_in, conv_dw_kernel, conv_dw_bias, conv_ln_gain, conv_ln_bias, conv_w_out, attn_w_out, w_o, ffn2_norm, ffn2_w_gate, ffn2_w_up, ffn2_w_down, final_norm, loss_target, m_ffn1_norm, m_ffn1_w_gate, m_ffn1_w_up, m_ffn1_w_down, m_mix_norm, m_w_in, m_conv_dw_kernel, m_conv_dw_bias, m_conv_ln_gain, m_conv_ln_bias, m_conv_w_out, m_attn_w_out, m_w_o, m_ffn2_norm, m_ffn2_w_gate, m_ffn2_w_up, m_ffn2_w_down, m_final_norm, v_ffn1_norm, v_ffn1_w_gate, v_ffn1_w_up, v_ffn1_w_down, v_mix_norm, v_w_in, v_conv_dw_kernel, v_conv_dw_bias, v_conv_ln_gain, v_conv_ln_bias, v_conv_w_out, v_attn_w_out, v_w_o, v_ffn2_norm, v_ffn2_w_gate, v_ffn2_w_up, v_ffn2_w_down, v_final_norm):
    names = ["ffn1_norm", "ffn1_w_gate", "ffn1_w_up", "ffn1_w_down", "mix_norm", "w_in", "conv_dw_kernel",
             "conv_dw_bias", "conv_ln_gain", "conv_ln_bias", "conv_w_out", "attn_w_out", "w_o", "ffn2_norm",
             "ffn2_w_gate", "ffn2_w_up", "ffn2_w_down", "final_norm"]
    w = dict(ffn1_norm=ffn1_norm, ffn1_w_gate=ffn1_w_gate, ffn1_w_up=ffn1_w_up, ffn1_w_down=ffn1_w_down, mix_norm=mix_norm, w_in=w_in, conv_dw_kernel=conv_dw_kernel, conv_dw_bias=conv_dw_bias, conv_ln_gain=conv_ln_gain, conv_ln_bias=conv_ln_bias, conv_w_out=conv_w_out, attn_w_out=attn_w_out, w_o=w_o, ffn2_norm=ffn2_norm, ffn2_w_gate=ffn2_w_gate, ffn2_w_up=ffn2_w_up, ffn2_w_down=ffn2_w_down, final_norm=final_norm)
    mo = dict(ffn1_norm=m_ffn1_norm, ffn1_w_gate=m_ffn1_w_gate, ffn1_w_up=m_ffn1_w_up, ffn1_w_down=m_ffn1_w_down, mix_norm=m_mix_norm, w_in=m_w_in, conv_dw_kernel=m_conv_dw_kernel, conv_dw_bias=m_conv_dw_bias, conv_ln_gain=m_conv_ln_gain, conv_ln_bias=m_conv_ln_bias, conv_w_out=m_conv_w_out, attn_w_out=m_attn_w_out, w_o=m_w_o, ffn2_norm=m_ffn2_norm, ffn2_w_gate=m_ffn2_w_gate, ffn2_w_up=m_ffn2_w_up, ffn2_w_down=m_ffn2_w_down, final_norm=m_final_norm)
    vo = dict(ffn1_norm=v_ffn1_norm, ffn1_w_gate=v_ffn1_w_gate, ffn1_w_up=v_ffn1_w_up, ffn1_w_down=v_ffn1_w_down, mix_norm=v_mix_norm, w_in=v_w_in, conv_dw_kernel=v_conv_dw_kernel, conv_dw_bias=v_conv_dw_bias, conv_ln_gain=v_conv_ln_gain, conv_ln_bias=v_conv_ln_bias, conv_w_out=v_conv_w_out, attn_w_out=v_attn_w_out, w_o=v_w_o, ffn2_norm=v_ffn2_norm, ffn2_w_gate=v_ffn2_w_gate, ffn2_w_up=v_ffn2_w_up, ffn2_w_down=v_ffn2_w_down, final_norm=v_final_norm)
    col_sharded = ("ffn1_w_gate", "ffn1_w_up", "w_in", "attn_w_out", "ffn2_w_gate", "ffn2_w_up")
    row_sharded = ("ffn1_w_down", "conv_w_out", "w_o", "ffn2_w_down")
    small = ("ffn1_norm", "mix_norm", "ffn2_norm", "final_norm", "conv_dw_bias", "conv_ln_gain", "conv_ln_bias")

    def landing_view(a, n):
        return jnp.transpose(a[0]) if n in col_sharded else a[0]

    def own_view(a, n):
        return jnp.transpose(a)[None] if n in col_sharded else a[None]

    ag_groups = (("ffn1_w_gate", "ffn1_w_up", "ffn1_w_down"),
                 ("w_in", "attn_w_out", "conv_w_out", "w_o", "conv_dw_kernel"),
                 ("ffn2_w_gate", "ffn2_w_up", "ffn2_w_down"))
    ag, order = [], []
    for gi, grp in enumerate(ag_groups):
        lands = _prep_gather([landing_view(w[n], n) for n in grp], order, f"gather_prep{gi}")
        st = _gather_start(lands, GATHER_DIRECT if gi == 2 else GATHER_A, [], f"gather_a_start{gi}")
        ag.append(st)
        order = [st[3]]

    def chips_in(gi, after):
        lands = _gather_wait(ag[gi], GATHER_A, after, f"gather_a_wait{gi}")
        return _gather_start(lands, GATHER_B, [], f"gather_b_start{gi}")

    def all_in(gi, st, after):
        return _gather_wait(st, GATHER_B, after, f"gather_b_wait{gi}")

    x0 = x[0]
    tgt = loss_target[0]
    gf = final_norm.reshape(1, D)

    wg1, wu1, wd1 = all_in(0, chips_in(0, [ag[2][3]]), [])
    x1, gg1, uu1, h2p = _ffn_fwd(x0, ffn1_norm, wg1, wu1, wd1, "ffn1_fwd", next_gain=mix_norm)
    h2 = h2p[0]
    win_t, wa_t, wc, wo, kern_blocks = all_in(1, chips_in(1, [x1]), [])
    kern = kern_blocks.reshape(NDEV, 32, D // NDEV).transpose(1, 0, 2).reshape(32, D)
    ptm = min(T, 2048)
    ab = _mm(h2, win_t, mode="nt", m=T, n=2 * D, k=D, tm=ptm, tn=512, tk=D, out_dtype=BF16, name="proj_conv")
    z1, z3b, gates = _conv_fwd(ab, kern, conv_dw_bias, conv_ln_gain, conv_ln_bias, "conv_fwd",
                               guest=(h2, win_t, (13, 14, 15, 16, 4, 7, 10), 512))
    qkv, qkv_col0 = [gates], [2 * D]
    for gi in range(1, len(GROUPS)):
        qkv.append(_mm(h2p[gi], win_t, mode="nt", m=T, n=3 * AW, k=D, tm=ptm, tn=AW, tk=D, out_dtype=BF16,
                       b_map=lambda i, j, kk, gi=gi: (4 + gi + 3 * j, 0), name=f"proj_qkv{gi}"))
        qkv_col0.append(0)
    outs, lses = [], []
    for gi, (_, dil) in enumerate(GROUPS):
        o, l = _attn_fwd(qkv[gi], gi, f"attn_fwd{gi}", col0=qkv_col0[gi])
        outs.append(o)
        lses.append(l)
    attnb, lse = _merge(outs, lses, "attn_merge")
    x2, yc, ya, mixedb = _mix_out(z3b, attnb, gates, wc, wa_t, wo, x1, "mix_out_fwd")
    wg2, wu2, wd2 = _gather_wait(ag[2], GATHER_DIRECT, [x2], "gather_a_wait2")
    gg2, uu2, dx3, dgf, loss_part = _ffn_fwd(x2, ffn2_norm, wg2, wu2, wd2, "ffn2_fwd", loss_of=(gf, tgt))

    dx2, dg3, dgb, dub, actb, hb, dob = _ffn_bwd(x2, ffn2_norm, gg2, uu2, dx3, wg2, wu2, wd2, "ffn2_bwd")
    grads = {}
    grads["ffn2_w_down"] = _wgrad(actb, dob, FF, D, "ffn2_dwd")
    rs_groups = [("ffn2_w_gate", "ffn2_w_up", "ffn2_w_down"),
                 ("attn_w_out", "conv_w_out", "w_o", "conv_dw_kernel"),
                 ("w_in",),
                 ("ffn1_w_gate",), ("ffn1_w_up",), ("ffn1_w_down",), ()]
    last = len(rs_groups) - 1
    rs = []

    dgates, dycb, dyab, dx2b, dz3, dh_gates, dattnb, delta = _mix_out_bwd(dx2, gates, yc, ya, attnb, wc, wa_t, wo,
                                                                          win_t, "mix_out_bwd")
    grads["w_o"] = _wgrad(mixedb, dx2b, D, D, "dw_o")
    grads["conv_w_out"] = _wgrad(z3b, dycb, D, D, "dw_conv_out")
    grads["attn_w_out"] = _wgrad(dyab, attnb, D, AW, "dw_attn_out")
    dab, dkern, dvec, grads["ffn2_w_gate"], grads["ffn2_w_up"] = _conv_bwd(
        dz3, z1, ab, kern, conv_ln_gain, conv_ln_bias, "conv_bwd", guest_lhs=(dgb, dub), guest_rhs=hb)
    grads["conv_dw_kernel"] = dkern.reshape(32, NDEV, D // NDEV).transpose(1, 0, 2).reshape(NDEV * 32, D // NDEV)
    rs.append(_send_start(["scatter"] * 3, [grads[n] for n in rs_groups[0]], [], "scatter_start0"))
    rs.append(_send_start(["scatter"] * 4, [grads[n] for n in rs_groups[1]], [rs[0][4]], "scatter_start1"))
    dattnb = [_tie(a, [rs[1][4]], f"tie_after_scatter1_{i}") for i, a in enumerate(dattnb)]

    dqkv, dq3s = [], []
    for gi, (_, dil) in enumerate(GROUPS):
        dq3 = _attn_bwd(qkv[gi], dattnb[gi], lse[gi], delta[gi], gi, f"attn_bwd{gi}", col0=qkv_col0[gi])
        dq3s.append(dq3)
        dqkv.append(dq3.reshape(3 * T, AW))

    wtk = min(T, 2048)
    dwin = _mm(dab, h2, mode="tn", m=2 * D, n=D, k=T, tm=2 * D, tn=D, tk=wtk, out_dtype=BF16, out_rows=IN_W,
               name="dw_in_conv")
    dwin = _mm(dgates, h2, mode="tn", m=2 * D, n=D, k=T, tm=512, tn=D, tk=wtk, out_dtype=BF16, out_rows=IN_W,
               o_map=lambda i, j, kk: (13 + i, 0), passthru=dwin, name="dw_in_gates")
    for gi in range(3):
        dwin = _mm(dqkv[gi], h2p[gi], mode="tn", m=3 * AW, n=D, k=T, tm=AW, tn=D, tk=wtk, out_dtype=BF16,
                   out_rows=IN_W, a_map=lambda i, j, kk: (i * (T // wtk) + kk, 0),
                   o_map=lambda i, j, kk, gi=gi: (4 + gi + 3 * i, 0), passthru=dwin, name=f"dw_in_qkv{gi}")
    grads["w_in"] = dwin
    rs.append(_send_start(["scatter"], [dwin], [rs[1][4]], "scatter_start2"))
    dab = _tie(dab, [rs[2][4]], "tie_after_scatter2")

    dx1, dg2 = _rms_bwd(x1, mix_norm, dh_gates, dab, (0, 1, 2, 3), dq3s, win_t, dx2, "mix_norm_bwd")

    dgb, dub, actb, hb, dob = _ffn_bwd_pre(x0, ffn1_norm, gg1, uu1, dx1, wd1, "ffn1_bwd_pre")
    grads["ffn1_w_gate"] = _wgrad(dgb, hb, FF, D, "ffn1_dwg")
    rs.append(_send_start(["scatter"], [grads["ffn1_w_gate"]], [rs[2][4]], "scatter_start3"))
    hb = _tie(hb, [rs[3][4]], "tie_after_scatter3")
    grads["ffn1_w_up"] = _wgrad(dub, hb, FF, D, "ffn1_dwu")
    rs.append(_send_start(["scatter"], [grads["ffn1_w_up"]], [rs[3][4]], "scatter_start4"))
    dob = _tie(dob, [rs[4][4]], "tie_after_scatter4")
    grads["ffn1_w_down"] = _wgrad(actb, dob, FF, D, "ffn1_dwd")
    rs.append(_send_start(["scatter"], [grads["ffn1_w_down"]], [rs[4][4]], "scatter_start5"))
    dgb = _tie(dgb, [rs[5][4]], "tie_after_scatter5")
    dx0, dg1 = _ffn_bwd_dx(x0, ffn1_norm, dgb, dub, dx1, wg1, wu1, "ffn1_bwd_dx")
    vec = jnp.concatenate([dg1, dg2, dg3, dgf, dvec[0:3], jnp.broadcast_to(loss_part[:, :1], (1, D))], axis=0)
    rs.append(_send_start(["bcast"], [vec], [rs[5][4]], "scatter_start6"))

    g_out, d_out, m_out, v_out = {}, {}, {}, {}
    me = _my_place()
    after, done = [rs[last][4]], []
    for gi, grp in enumerate(rs_groups):
        kinds = ["scatter"] * len(grp) + (["bcast"] if gi == last else [])
        srcs, lands = _send_wait(kinds, rs[gi], after + (done if gi >= last - 1 else []), f"scatter_wait{gi}")
        for n, src, land in zip(grp, srcs, lands):
            if n == "conv_dw_kernel":
                rows = src.shape[0] // NDEV
                own = lax.dynamic_slice(src, (me * rows, 0), (rows, src.shape[1]))
                g = _gsum(own, land, f"gsum_{n}")[:CONV_W]
                d, m2, v2 = _adamw(w[n][0], g, mo[n][0], vo[n][0], f"adamw_{n}")
                after = [d]
                done.append(d)
                g, d, m2, v2 = g[None], d[None], m2[None], v2[None]
            else:
                res = _update(src, land, landing_view(w[n], n), landing_view(mo[n], n), landing_view(vo[n], n),
                              f"update_{n}")
                after = [res[1]]
                done.append(res[1])
                g, d, m2, v2 = (own_view(a, n) for a in res)
            g_out[n], d_out[n], m_out[n], v_out[n] = g, d, m2, v2
    vland = lands[-1]

    def rows8(src):
        return jnp.concatenate([src[n].reshape(1, D) for n in small] + [jnp.ones((1, D), F32)], axis=0)

    g8, d8, m8, v8 = _small_update(vland, rows8(w), rows8(mo), rows8(vo), "small_update")
    for r, n in enumerate(small):
        shp = w[n].shape
        g_out[n], d_out[n], m_out[n], v_out[n] = (a[r].reshape(shp) for a in (g8, d8, m8, v8))
    loss = g8[7, 0]

    return (loss, dx0[None], *[g_out[n] for n in names], *[d_out[n] for n in names],
            *[m_out[n] for n in names], *[v_out[n] for n in names])
```
